```python
import jax, jax.numpy as jnp
from jax import lax
import numpy as np

D_MODEL = 1024
BATCH = 32
SEQ = 2048
DEPTH = 2

N_MIXERS = 2
N_ATTN_LAYERS = (DEPTH + 1) // 2
N_REC_LAYERS = DEPTH // 2

HEAD_DIM = 64
N_HEADS = D_MODEL // HEAD_DIM
N_KV_HEADS = 2
GROUP = N_HEADS // N_KV_HEADS
ATTN_WIDTH = N_HEADS * HEAD_DIM
KV_WIDTH = N_KV_HEADS * HEAD_DIM
ATTN_IN = 2 * ATTN_WIDTH + 2 * KV_WIDTH
WINDOW = 128
ATTN_BLOCK = 128
ROPE_THETA = 500000.0
ROPE_DIM = HEAD_DIM // 4

REC_HEADS = 8
REC_KEY_DIM = 128
REC_VALUE_DIM = D_MODEL // REC_HEADS
FORGET_DIM = REC_HEADS * REC_KEY_DIM
REC_WIDTH = REC_HEADS * REC_VALUE_DIM
REC_IN = 2 * FORGET_DIM + 2 * REC_WIDTH
REC_CHUNK = 32

NORM_EPS = 1e-6

kernel_name = "hybrid_swa_sink_hgrn2_interleaved"


def rmsnorm(x, w):
    xf = x.astype(jnp.float32)
    y = xf * lax.rsqrt(jnp.mean(xf * xf, axis=-1, keepdims=True) + NORM_EPS)
    return (y * w.astype(jnp.float32)).astype(x.dtype)


def partial_rope(x, positions):
    half = ROPE_DIM // 2
    inv_freq = ROPE_THETA ** (-(jnp.arange(half, dtype=jnp.float32) * 2.0 / ROPE_DIM))
    ang = positions.astype(jnp.float32)[..., None] * inv_freq
    cos = jnp.cos(ang)[:, :, None, :]
    sin = jnp.sin(ang)[:, :, None, :]
    x1 = x[..., :half].astype(jnp.float32)
    x2 = x[..., half:ROPE_DIM].astype(jnp.float32)
    r1 = x1 * cos - x2 * sin
    r2 = x2 * cos + x1 * sin
    return jnp.concatenate([r1.astype(x.dtype), r2.astype(x.dtype), x[..., ROPE_DIM:]], axis=-1)


def sliding_window_gqa(q, k, v, sinks):
    B, T = q.shape[0], q.shape[1]
    nb = T // ATTN_BLOCK
    qb = q.reshape(B, nb, ATTN_BLOCK, N_KV_HEADS, GROUP, HEAD_DIM).transpose(1, 0, 2, 3, 4, 5)

    def span(t):
        tb = t.reshape(B, nb, ATTN_BLOCK, N_KV_HEADS, HEAD_DIM)
        prev = jnp.pad(tb, ((0, 0), (1, 0), (0, 0), (0, 0), (0, 0)))[:, :-1]
        return jnp.concatenate([prev, tb], axis=2).transpose(1, 0, 2, 3, 4)

    kk, vv = span(k), span(v)
    q_rel = jnp.arange(ATTN_BLOCK)[:, None] + ATTN_BLOCK
    k_rel = jnp.arange(2 * ATTN_BLOCK)[None, :]
    band = (k_rel <= q_rel) & (q_rel - k_rel < WINDOW)
    sink = sinks.astype(jnp.float32).reshape(N_KV_HEADS, GROUP)[None, :, :, None]
    scale = HEAD_DIM ** -0.5

    def block(args):
        qi, ki, vi, idx = args
        s = jnp.einsum('bqhgd,bkhd->bhgqk', qi, ki, preferred_element_type=jnp.float32) * scale
        valid = band & ((idx > 0) | (k_rel >= ATTN_BLOCK))
        s = jnp.where(valid, s, -jnp.inf)
        m = jnp.maximum(jnp.max(s, axis=-1), sink)
        p = jnp.exp(s - m[..., None])
        denom = jnp.sum(p, axis=-1) + jnp.exp(sink - m)
        p = (p / denom[..., None]).astype(vi.dtype)
        return jnp.einsum('bhgqk,bkhd->bqhgd', p, vi)

    out = lax.map(block, (qb, kk, vv, jnp.arange(nb)))
    return out.transpose(1, 0, 2, 3, 4, 5).reshape(B, T, ATTN_WIDTH)


def attention_mixer(h, positions, w_in, b_in, sinks, w_out, b_out):
    B, T = h.shape[0], h.shape[1]
    proj = h @ w_in + b_in
    q, k, v, z = jnp.split(proj, [ATTN_WIDTH, ATTN_WIDTH + KV_WIDTH, ATTN_WIDTH + 2 * KV_WIDTH], axis=-1)
    q = partial_rope(q.reshape(B, T, N_HEADS, HEAD_DIM), positions)
    k = partial_rope(k.reshape(B, T, N_KV_HEADS, HEAD_DIM), positions)
    v = v.reshape(B, T, N_KV_HEADS, HEAD_DIM)
    o = sliding_window_gqa(q, k, v, sinks)
    return (o * jax.nn.silu(z)) @ w_out + b_out


def chunked_gated_recurrence(q, k, v, log_f):
    B, T, H, K = q.shape
    V = v.shape[-1]
    nc = T // REC_CHUNK

    def to_chunks(t):
        return t.reshape(B, nc, REC_CHUNK, H, t.shape[-1]).transpose(1, 0, 3, 2, 4)

    qc, kc, vc, gc = to_chunks(q), to_chunks(k), to_chunks(v), to_chunks(log_f)
    causal = jnp.tril(jnp.ones((REC_CHUNK, REC_CHUNK), dtype=bool))[:, :, None]

    def step(S, inp):
        qi, ki, vi, gi = inp
        b = jnp.cumsum(gi, axis=2)
        b_last = b[:, :, -1:, :]
        o_inter = jnp.einsum('bhck,bhkv->bhcv', qi * jnp.exp(b), S)
        diff = b[:, :, :, None, :] - b[:, :, None, :, :]
        decay = jnp.exp(jnp.where(causal, diff, -jnp.inf))
        scores = jnp.einsum('bhtk,bhtsk->bhts', qi, decay * ki[:, :, None, :, :])
        o_intra = jnp.einsum('bhts,bhsv->bhtv', scores, vi)
        S_new = S * jnp.exp(b_last)[:, :, 0, :, None] + jnp.einsum(
            'bhck,bhcv->bhkv', ki * jnp.exp(b_last - b), vi)
        return S_new, o_inter + o_intra

    S0 = jnp.zeros((B, H, K, V), jnp.float32)
    _, o = lax.scan(step, S0, (qc, kc, vc, gc))
    return o.transpose(1, 0, 3, 2, 4).reshape(B, T, H, V)


def hgrn2_mixer(h, lower_bound, w_in, gnorm_w, w_out):
    B, T = h.shape[0], h.shape[1]
    proj = h @ w_in
    q, f, i, z = jnp.split(proj, [FORGET_DIM, 2 * FORGET_DIM, 2 * FORGET_DIM + REC_WIDTH], axis=-1)
    q = jax.nn.silu(q.astype(jnp.float32)).reshape(B, T, REC_HEADS, REC_KEY_DIM)
    lb = lower_bound.astype(jnp.float32)
    log_f = jnp.logaddexp(jnp.log(lb), jnp.log1p(-lb) + jax.nn.log_sigmoid(f.astype(jnp.float32)))
    k = -jnp.expm1(log_f)
    log_f = log_f.reshape(B, T, REC_HEADS, REC_KEY_DIM)
    k = k.reshape(B, T, REC_HEADS, REC_KEY_DIM)
    v = i.astype(jnp.float32).reshape(B, T, REC_HEADS, REC_VALUE_DIM)
    o = chunked_gated_recurrence(q, k, v, log_f)
    o = o * lax.rsqrt(jnp.mean(o * o, axis=-1, keepdims=True) + NORM_EPS) * gnorm_w.astype(jnp.float32)
    o = o.reshape(B, T, REC_WIDTH) * jax.nn.silu(z.astype(jnp.float32))
    return o.astype(h.dtype) @ w_out


def _fwd_setup_inputs(seed: int = 0) -> dict:
    key = jax.random.key(seed)
    ks = jax.random.split(key, 16)
    f32 = jnp.float32
    x = jax.random.normal(ks[0], (BATCH, SEQ, D_MODEL), f32)
    offsets = jax.random.randint(ks[1], (BATCH, 1), 0, 4096, dtype=jnp.int32)
    positions = offsets + jnp.arange(SEQ, dtype=jnp.int32)[None, :]
    pre_norm_w = 1.0 + 0.02 * jax.random.normal(ks[2], (DEPTH, D_MODEL), f32)
    post_norm_w = 1.0 + 0.02 * jax.random.normal(ks[3], (DEPTH, D_MODEL), f32)
    attn_w_in = jax.random.normal(ks[4], (N_ATTN_LAYERS, D_MODEL, ATTN_IN), f32) * D_MODEL ** -0.5
    attn_b_in = 0.02 * jax.random.normal(ks[5], (N_ATTN_LAYERS, ATTN_IN), f32)
    attn_sinks = 0.5 * jax.random.normal(ks[6], (N_ATTN_LAYERS, N_HEADS), f32)
    attn_w_out = jax.random.normal(ks[7], (N_ATTN_LAYERS, ATTN_WIDTH, D_MODEL), f32) * ATTN_WIDTH ** -0.5
    attn_b_out = 0.02 * jax.random.normal(ks[8], (N_ATTN_LAYERS, D_MODEL), f32)
    rec_w_in = jax.random.normal(ks[9], (N_REC_LAYERS, D_MODEL, REC_IN), f32) * D_MODEL ** -0.5
    rec_lb_logits = 0.5 * jax.random.normal(ks[10], (DEPTH, FORGET_DIM), f32)
    rec_gnorm_w = 1.0 + 0.02 * jax.random.normal(ks[11], (N_REC_LAYERS, REC_VALUE_DIM), f32)
    rec_w_out = jax.random.normal(ks[12], (N_REC_LAYERS, REC_WIDTH, D_MODEL), f32) * REC_WIDTH ** -0.5
    return {"x": x, "positions": positions, "pre_norm_w": pre_norm_w, "post_norm_w": post_norm_w,
            "attn_w_in": attn_w_in, "attn_b_in": attn_b_in, "attn_sinks": attn_sinks,
            "attn_w_out": attn_w_out, "attn_b_out": attn_b_out, "rec_w_in": rec_w_in,
            "rec_lb_logits": rec_lb_logits, "rec_gnorm_w": rec_gnorm_w, "rec_w_out": rec_w_out}


def _fwd_reference(x, positions, pre_norm_w, post_norm_w, attn_w_in, attn_b_in, attn_sinks,
              attn_w_out, attn_b_out, rec_w_in, rec_lb_logits, rec_gnorm_w, rec_w_out):
    probs = jax.nn.softmax(rec_lb_logits.astype(jnp.float32), axis=0)
    cum = jnp.cumsum(probs, axis=0)
    lower_bounds = cum - cum[0:1]
    for layer in range(DEPTH):
        h = rmsnorm(x, pre_norm_w[layer])
        j = layer // N_MIXERS
        if layer % N_MIXERS == 0:
            y = attention_mixer(h, positions, attn_w_in[j], attn_b_in[j], attn_sinks[j],
                                attn_w_out[j], attn_b_out[j])
        else:
            y = hgrn2_mixer(h, lower_bounds[layer], rec_w_in[j], rec_gnorm_w[j], rec_w_out[j])
        x = x + rmsnorm(y, post_norm_w[layer])
    return x


import jax as _jax
import jax.numpy as _jnp

TWIN_FORMAT = 'train_step'
FWD_PARAMS = ['x', 'positions', 'pre_norm_w', 'post_norm_w', 'attn_w_in', 'attn_b_in', 'attn_sinks', 'attn_w_out', 'attn_b_out', 'rec_w_in', 'rec_lb_logits', 'rec_gnorm_w', 'rec_w_out']
TWIN_WEIGHTS = ['pre_norm_w', 'post_norm_w', 'attn_w_in', 'attn_b_in', 'attn_sinks', 'attn_w_out', 'attn_b_out', 'rec_w_in', 'rec_lb_logits', 'rec_gnorm_w', 'rec_w_out']
TWIN_DIFF_INPUT = 'x'
TWIN_INPUTS = ['x', 'positions', 'pre_norm_w', 'post_norm_w', 'attn_w_in', 'attn_b_in', 'attn_sinks', 'attn_w_out', 'attn_b_out', 'rec_w_in', 'rec_lb_logits', 'rec_gnorm_w', 'rec_w_out', 'loss_target', 'm_pre_norm_w', 'm_post_norm_w', 'm_attn_w_in', 'm_attn_b_in', 'm_attn_sinks', 'm_attn_w_out', 'm_attn_b_out', 'm_rec_w_in', 'm_rec_lb_logits', 'm_rec_gnorm_w', 'm_rec_w_out', 'v_pre_norm_w', 'v_post_norm_w', 'v_attn_w_in', 'v_attn_b_in', 'v_attn_sinks', 'v_attn_w_out', 'v_attn_b_out', 'v_rec_w_in', 'v_rec_lb_logits', 'v_rec_gnorm_w', 'v_rec_w_out']
TWIN_OUTPUTS = ['loss', 'grad_x', 'grad_pre_norm_w', 'grad_post_norm_w', 'grad_attn_w_in', 'grad_attn_b_in', 'grad_attn_sinks', 'grad_attn_w_out', 'grad_attn_b_out', 'grad_rec_w_in', 'grad_rec_lb_logits', 'grad_rec_gnorm_w', 'grad_rec_w_out', 'delta_pre_norm_w', 'delta_post_norm_w', 'delta_attn_w_in', 'delta_attn_b_in', 'delta_attn_sinks', 'delta_attn_w_out', 'delta_attn_b_out', 'delta_rec_w_in', 'delta_rec_lb_logits', 'delta_rec_gnorm_w', 'delta_rec_w_out', 'new_m_pre_norm_w', 'new_m_post_norm_w', 'new_m_attn_w_in', 'new_m_attn_b_in', 'new_m_attn_sinks', 'new_m_attn_w_out', 'new_m_attn_b_out', 'new_m_rec_w_in', 'new_m_rec_lb_logits', 'new_m_rec_gnorm_w', 'new_m_rec_w_out', 'new_v_pre_norm_w', 'new_v_post_norm_w', 'new_v_attn_w_in', 'new_v_attn_b_in', 'new_v_attn_sinks', 'new_v_attn_w_out', 'new_v_attn_b_out', 'new_v_rec_w_in', 'new_v_rec_lb_logits', 'new_v_rec_gnorm_w', 'new_v_rec_w_out']
TWIN_LEAF_KINDS = {'loss': 'loss', 'grad_x': 'grad_x', 'grad_pre_norm_w': 'grad_w', 'grad_post_norm_w': 'grad_w', 'grad_attn_w_in': 'grad_w', 'grad_attn_b_in': 'grad_w', 'grad_attn_sinks': 'grad_w', 'grad_attn_w_out': 'grad_w', 'grad_attn_b_out': 'grad_w', 'grad_rec_w_in': 'grad_w', 'grad_rec_lb_logits': 'grad_w', 'grad_rec_gnorm_w': 'grad_w', 'grad_rec_w_out': 'grad_w', 'delta_pre_norm_w': 'delta_w', 'delta_post_norm_w': 'delta_w', 'delta_attn_w_in': 'delta_w', 'delta_attn_b_in': 'delta_w', 'delta_attn_sinks': 'delta_w', 'delta_attn_w_out': 'delta_w', 'delta_attn_b_out': 'delta_w', 'delta_rec_w_in': 'delta_w', 'delta_rec_lb_logits': 'delta_w', 'delta_rec_gnorm_w': 'delta_w', 'delta_rec_w_out': 'delta_w', 'new_m_pre_norm_w': 'new_m', 'new_m_post_norm_w': 'new_m', 'new_m_attn_w_in': 'new_m', 'new_m_attn_b_in': 'new_m', 'new_m_attn_sinks': 'new_m', 'new_m_attn_w_out': 'new_m', 'new_m_attn_b_out': 'new_m', 'new_m_rec_w_in': 'new_m', 'new_m_rec_lb_logits': 'new_m', 'new_m_rec_gnorm_w': 'new_m', 'new_m_rec_w_out': 'new_m', 'new_v_pre_norm_w': 'new_v', 'new_v_post_norm_w': 'new_v', 'new_v_attn_w_in': 'new_v', 'new_v_attn_b_in': 'new_v', 'new_v_attn_sinks': 'new_v', 'new_v_attn_w_out': 'new_v', 'new_v_attn_b_out': 'new_v', 'new_v_rec_w_in': 'new_v', 'new_v_rec_lb_logits': 'new_v', 'new_v_rec_gnorm_w': 'new_v', 'new_v_rec_w_out': 'new_v'}


def _forward(args):
    return _fwd_reference(*[args[k] for k in FWD_PARAMS])


def _output_shape():
    out = _jax.eval_shape(lambda: _forward(_fwd_setup_inputs(0)))
    return out.shape, out.dtype

N_MICROBATCH = 1
ADAM_LR = 0.001
ADAM_B1 = 0.9
ADAM_B2 = 0.999
ADAM_EPS = 1e-08
ADAM_WD = 0.01
ADAM_STEP = 10
PER_EXAMPLE_BATCH_AXIS = {'x': 0, 'positions': 0, 'loss_target': 0}
SHARED_INPUTS = []
_WEIGHT_DTYPES = {'pre_norm_w': _jnp.float32, 'post_norm_w': _jnp.float32, 'attn_w_in': _jnp.float32, 'attn_b_in': _jnp.float32, 'attn_sinks': _jnp.float32, 'attn_w_out': _jnp.float32, 'attn_b_out': _jnp.float32, 'rec_w_in': _jnp.float32, 'rec_lb_logits': _jnp.float32, 'rec_gnorm_w': _jnp.float32, 'rec_w_out': _jnp.float32}
MOMENT_SCALE = {'pre_norm_w': 1.282936e+00, 'post_norm_w': 6.412407e+01, 'attn_w_in': 7.978748e-01, 'attn_b_in': 2.194413e+01, 'attn_sinks': 2.509689e-01, 'attn_w_out': 7.878793e-01, 'attn_b_out': 1.491837e+02, 'rec_w_in': 7.442759e-01, 'rec_lb_logits': 3.947437e-02, 'rec_gnorm_w': 4.470379e+00, 'rec_w_out': 1.356754e+00}


def _to_microbatches(a, axis):
    t = _jnp.moveaxis(a, axis, 0)
    t = t.reshape((N_MICROBATCH, t.shape[0] // N_MICROBATCH) + t.shape[1:])
    return _jnp.moveaxis(t, 1, axis + 1)


def setup_inputs(seed: int = 0) -> dict:
    inp = _fwd_setup_inputs(seed)
    key = _jax.random.fold_in(_jax.random.key(seed), 7919)
    shape, _ = _output_shape()
    out = dict(inp)
    out["loss_target"] = _jax.random.normal(_jax.random.fold_in(key, 0), shape, _jnp.float32)
    for i, name in enumerate(TWIN_WEIGHTS):
        w = inp[name].astype(_jnp.float32)
        if MOMENT_SCALE is None:
            s = _jnp.sqrt(_jnp.mean(_jnp.square(w)) + 1e-30)
        else:
            s = MOMENT_SCALE[name]
        km, kv = _jax.random.split(_jax.random.fold_in(key, i + 1))
        out[name] = w
        out["m_" + name] = s * _jax.random.normal(km, w.shape, _jnp.float32)
        out["v_" + name] = (s * s) * _jax.random.uniform(kv, w.shape, _jnp.float32, 0.5, 1.5)
    if N_MICROBATCH > 1:
        for name, axis in PER_EXAMPLE_BATCH_AXIS.items():
            out[name] = _to_microbatches(out[name], axis)
    return {'x': out['x'], 'positions': out['positions'], 'pre_norm_w': out['pre_norm_w'], 'post_norm_w': out['post_norm_w'], 'attn_w_in': out['attn_w_in'], 'attn_b_in': out['attn_b_in'], 'attn_sinks': out['attn_sinks'], 'attn_w_out': out['attn_w_out'], 'attn_b_out': out['attn_b_out'], 'rec_w_in': out['rec_w_in'], 'rec_lb_logits': out['rec_lb_logits'], 'rec_gnorm_w': out['rec_gnorm_w'], 'rec_w_out': out['rec_w_out'], 'loss_target': out['loss_target'], 'm_pre_norm_w': out['m_pre_norm_w'], 'm_post_norm_w': out['m_post_norm_w'], 'm_attn_w_in': out['m_attn_w_in'], 'm_attn_b_in': out['m_attn_b_in'], 'm_attn_sinks': out['m_attn_sinks'], 'm_attn_w_out': out['m_attn_w_out'], 'm_attn_b_out': out['m_attn_b_out'], 'm_rec_w_in': out['m_rec_w_in'], 'm_rec_lb_logits': out['m_rec_lb_logits'], 'm_rec_gnorm_w': out['m_rec_gnorm_w'], 'm_rec_w_out': out['m_rec_w_out'], 'v_pre_norm_w': out['v_pre_norm_w'], 'v_post_norm_w': out['v_post_norm_w'], 'v_attn_w_in': out['v_attn_w_in'], 'v_attn_b_in': out['v_attn_b_in'], 'v_attn_sinks': out['v_attn_sinks'], 'v_attn_w_out': out['v_attn_w_out'], 'v_attn_b_out': out['v_attn_b_out'], 'v_rec_w_in': out['v_rec_w_in'], 'v_rec_lb_logits': out['v_rec_lb_logits'], 'v_rec_gnorm_w': out['v_rec_gnorm_w'], 'v_rec_w_out': out['v_rec_w_out']}


def _loss(weights, diff, rest, loss_target):
    with _jax.named_scope("forward"):
        args = {**rest, TWIN_DIFF_INPUT: diff, **{k: w.astype(_WEIGHT_DTYPES[k]) for k, w in weights.items()}}
        y = _forward(args)
    with _jax.named_scope("loss_head"):
        err = _jnp.square(y.astype(_jnp.float32) - loss_target)
        return 0.5 * _jnp.sum(_jnp.mean(err, axis=-1)) if err.ndim else 0.5 * err


def _adamw(w, g, m, v):
    m = ADAM_B1 * m + (1.0 - ADAM_B1) * g
    v = ADAM_B2 * v + (1.0 - ADAM_B2) * _jnp.square(g)
    m_hat = m / (1.0 - ADAM_B1 ** ADAM_STEP)
    v_hat = v / (1.0 - ADAM_B2 ** ADAM_STEP)
    delta = -ADAM_LR * (m_hat / (_jnp.sqrt(v_hat) + ADAM_EPS) + ADAM_WD * w)
    return delta, m, v


def reference(x, positions, pre_norm_w, post_norm_w, attn_w_in, attn_b_in, attn_sinks, attn_w_out, attn_b_out, rec_w_in, rec_lb_logits, rec_gnorm_w, rec_w_out, loss_target, m_pre_norm_w, m_post_norm_w, m_attn_w_in, m_attn_b_in, m_attn_sinks, m_attn_w_out, m_attn_b_out, m_rec_w_in, m_rec_lb_logits, m_rec_gnorm_w, m_rec_w_out, v_pre_norm_w, v_post_norm_w, v_attn_w_in, v_attn_b_in, v_attn_sinks, v_attn_w_out, v_attn_b_out, v_rec_w_in, v_rec_lb_logits, v_rec_gnorm_w, v_rec_w_out):
    given = dict(x=x, positions=positions, pre_norm_w=pre_norm_w, post_norm_w=post_norm_w, attn_w_in=attn_w_in, attn_b_in=attn_b_in, attn_sinks=attn_sinks, attn_w_out=attn_w_out, attn_b_out=attn_b_out, rec_w_in=rec_w_in, rec_lb_logits=rec_lb_logits, rec_gnorm_w=rec_gnorm_w, rec_w_out=rec_w_out, loss_target=loss_target, m_pre_norm_w=m_pre_norm_w, m_post_norm_w=m_post_norm_w, m_attn_w_in=m_attn_w_in, m_attn_b_in=m_attn_b_in, m_attn_sinks=m_attn_sinks, m_attn_w_out=m_attn_w_out, m_attn_b_out=m_attn_b_out, m_rec_w_in=m_rec_w_in, m_rec_lb_logits=m_rec_lb_logits, m_rec_gnorm_w=m_rec_gnorm_w, m_rec_w_out=m_rec_w_out, v_pre_norm_w=v_pre_norm_w, v_post_norm_w=v_post_norm_w, v_attn_w_in=v_attn_w_in, v_attn_b_in=v_attn_b_in, v_attn_sinks=v_attn_sinks, v_attn_w_out=v_attn_w_out, v_attn_b_out=v_attn_b_out, v_rec_w_in=v_rec_w_in, v_rec_lb_logits=v_rec_lb_logits, v_rec_gnorm_w=v_rec_gnorm_w, v_rec_w_out=v_rec_w_out)
    weights = {n: given[n] for n in TWIN_WEIGHTS}
    shared = {n: given[n] for n in SHARED_INPUTS}
    per_example = {n: given[n] for n in ['x', 'positions']}
    grad_fn = _jax.value_and_grad(_loss, argnums=(0, 1))

    def one_microbatch(ex, loss_target):
        ex = dict(ex)
        diff = ex.pop(TWIN_DIFF_INPUT)
        return grad_fn(weights, diff, {**shared, **ex}, loss_target)

    if N_MICROBATCH == 1:
        loss, (grad_w, grad_x) = one_microbatch(per_example, given["loss_target"])
    else:
        def body(carry, xs):
            loss_sum, grad_sum = carry
            l_k, (gw_k, gx_k) = one_microbatch(xs[0], xs[1])
            with _jax.named_scope("update"):
                return (loss_sum + l_k, _jax.tree.map(_jnp.add, grad_sum, gw_k)), gx_k

        init = (_jnp.zeros((), _jnp.float32), _jax.tree.map(_jnp.zeros_like, weights))
        (loss, grad_w), grad_x = _jax.lax.scan(body, init, (per_example, given["loss_target"]))
    with _jax.named_scope("update"):
        delta_w, new_m, new_v = {}, {}, {}
        for n in TWIN_WEIGHTS:
            delta_w[n], new_m[n], new_v[n] = _adamw(weights[n], grad_w[n], given["m_" + n], given["v_" + n])
    return (loss, grad_x, *[grad_w[n] for n in TWIN_WEIGHTS], *[delta_w[n] for n in TWIN_WEIGHTS],
            *[new_m[n] for n in TWIN_WEIGHTS], *[new_v[n] for n in TWIN_WEIGHTS])
```

```python
import functools

import jax
import jax.numpy as jnp
from jax import lax
from jax.experimental import pallas as pl
from jax.experimental.pallas import tpu as pltpu

F32 = jnp.float32
BF16 = jnp.bfloat16
MESH = pl.DeviceIdType.MESH

D_MODEL = 1024
HEAD_DIM = 64
N_HEADS = 16
N_KV_HEADS = 2
GROUP = N_HEADS // N_KV_HEADS
KV_WIDTH = N_KV_HEADS * HEAD_DIM
ATTN_IN = 2 * D_MODEL + 2 * KV_WIDTH
ATTN_BLOCK = 128
ROPE_THETA = 500000.0
ROPE_DIM = HEAD_DIM // 4
REC_HEADS = 8
REC_DIM = 128
REC_IN = 4 * D_MODEL
REC_BLOCK = 128
DIAG = 8
NORM_EPS = 1e-6
N_CHIPS = 4
N_DEV = 8
LANES = 128

ADAM_LR = 0.001
ADAM_B1 = 0.9
ADAM_B2 = 0.999
ADAM_EPS = 1e-08
ADAM_WD = 0.01
ADAM_STEP = 10

VMEM_LIMIT = 56 * 1024 * 1024


def _cparams(n_axes):
    return pltpu.CompilerParams(dimension_semantics=("arbitrary",) * n_axes, vmem_limit_bytes=VMEM_LIMIT)


def _dot(a, b, contract):
    return lax.dot_general(a.astype(BF16), b.astype(BF16), (contract, ((), ())), preferred_element_type=F32)


_NN = ((1,), (0,))
_NT = ((1,), (1,))
_TN = ((0,), (0,))


@jax.custom_vjp
def mm_nn(a, b):
    return _dot(a, b, _NN)


mm_nn.defvjp(lambda a, b: (_dot(a, b, _NN), (a, b)),
             lambda res, g: (_dot(g, res[1], _NT), _dot(res[0], g, _TN)))


@jax.custom_vjp
def mm_nt(a, b):
    return _dot(a, b, _NT)


mm_nt.defvjp(lambda a, b: (_dot(a, b, _NT), (a, b)),
             lambda res, g: (_dot(g, res[1], _NN), _dot(g, res[0], _TN)))


@jax.custom_vjp
def mm_tn(a, b):
    return _dot(a, b, _TN)


mm_tn.defvjp(lambda a, b: (_dot(a, b, _TN), (a, b)),
             lambda res, g: (_dot(res[1], g, _NT), _dot(res[0], g, _NN)))


def _tri_ones(n, lower):
    r = lax.broadcasted_iota(jnp.int32, (n, n), 0)
    c = lax.broadcasted_iota(jnp.int32, (n, n), 1)
    return ((c <= r) if lower else (c >= r)).astype(F32)


def _dot_exact(a, b):
    return lax.dot_general(a, b, (_NN, ((), ())), precision=lax.Precision.HIGHEST, preferred_element_type=F32)


@jax.custom_vjp
def cumsum_rows(x):
    return _dot_exact(_tri_ones(x.shape[0], True), x)


cumsum_rows.defvjp(lambda x: (cumsum_rows(x), None),
                   lambda _, g: (_dot_exact(_tri_ones(g.shape[0], False), g),))


@functools.partial(jax.custom_vjp, nondiff_argnums=(1,))
def roll_sub(x, d):
    return pltpu.roll(x, d, 1) if d else x


roll_sub.defvjp(lambda x, d: (roll_sub(x, d), None),
                lambda d, _, g: (roll_sub(g, (DIAG - d) % DIAG),))


def sigmoid(x):
    return 0.5 * (jnp.tanh(0.5 * x) + 1.0)


def log_sigmoid_pair(x):
    t = jnp.log(1.0 + jnp.exp(-jnp.abs(x)))
    return jnp.minimum(x, 0.0) - t, jnp.minimum(-x, 0.0) - t


def _rms(x):
    return lax.rsqrt(jnp.mean(x * x, axis=-1, keepdims=True) + NORM_EPS)


def _attn_head(qh, kk, vv, zh, sink, first):
    L = qh.shape[0]
    s = mm_nt(qh, kk)
    r = lax.broadcasted_iota(jnp.int32, (L, 2 * L), 0)
    c = lax.broadcasted_iota(jnp.int32, (L, 2 * L), 1)
    in_prev = (c < L) & (c > r)
    in_cur = (c >= L) & ((c - L) <= r)
    no_prev = jnp.where(first, -jnp.inf, 0.0)
    s = jnp.where(in_cur, s, jnp.where(in_prev, s + no_prev, -jnp.inf))
    m = lax.stop_gradient(jnp.maximum(jnp.max(s, axis=-1, keepdims=True), sink))
    p = jnp.exp(s - m)
    denom = jnp.sum(p, axis=-1, keepdims=True) + jnp.exp(sink - m)
    o = mm_nn(p / denom, vv)
    return o * (zh * sigmoid(zh))


def _rec_head(qr, fr, v, z, S, l0, l1, gw):
    R = qr.shape[0]
    q = qr * sigmoid(qr)
    log_lb, log_1m_lb = log_sigmoid_pair(l1 - l0)
    ls_f, ls_nf = log_sigmoid_pair(fr)
    c = log_1m_lb + ls_f
    lf = jnp.maximum(log_lb, c) + jnp.log(1.0 + jnp.exp(-jnp.abs(log_lb - c)))
    k = jnp.exp(log_1m_lb + ls_nf)
    b = cumsum_rows(lf)
    rows = lax.broadcasted_iota(jnp.int32, (R, REC_DIM), 0)

    o = mm_nt(q * jnp.exp(jnp.minimum(b, 0.0)), S)

    ri = lax.broadcasted_iota(jnp.int32, (R, R), 0)
    ci = lax.broadcasted_iota(jnp.int32, (R, R), 1)
    sc = jnp.zeros((R, R), F32)
    w = R
    while w > DIAG:
        h = w // 2
        b3 = b.reshape(R // w, w, REC_DIM)
        rin = lax.broadcasted_iota(jnp.int32, (R // w, w, REC_DIM), 1)
        mid = jnp.sum(jnp.where(rin == h - 1, b3, 0.0), axis=1, keepdims=True)
        fac = jnp.exp(jnp.minimum(jnp.where(rin >= h, b3 - mid, mid - b3), 0.0)).reshape(R, REC_DIM)
        upper = (rows % w) >= h
        s_w = mm_nt(jnp.where(upper, q * fac, 0.0), jnp.where(upper, 0.0, k * fac))
        sc = sc + jnp.where((ri // w) == (ci // w), s_w, 0.0)
        w = h
    o = o + mm_nn(sc, v)

    g = R // DIAG
    q3, k3, v3, b3 = (t.reshape(g, DIAG, REC_DIM) for t in (q, k, v, b))
    rin = lax.broadcasted_iota(jnp.int32, (g, DIAG, 1), 1)
    od = jnp.zeros((g, DIAG, REC_DIM), F32)
    for d in range(DIAG):
        e = jnp.exp(jnp.minimum(b3 - roll_sub(b3, d), 0.0))
        sd = jnp.sum(q3 * roll_sub(k3, d) * e, axis=-1, keepdims=True)
        od = od + jnp.where(rin >= d, sd, 0.0) * roll_sub(v3, d)
    o = o + od.reshape(R, REC_DIM)

    b_last = jnp.sum(jnp.where(rows == R - 1, b, 0.0), axis=0, keepdims=True)
    S_new = S * jnp.exp(jnp.minimum(b_last, 0.0)) + mm_tn(v, k * jnp.exp(jnp.minimum(b_last - b, 0.0)))

    on = o * _rms(o) * gw
    return on * (z * sigmoid(z)), S_new


def _rope_tables(positions):
    half = ROPE_DIM // 2
    inv_freq = ROPE_THETA ** (-(jnp.arange(half, dtype=F32) * 2.0 / ROPE_DIM))
    ang = positions.astype(F32).reshape(-1, 1) * inv_freq
    cos, sin = jnp.cos(ang), jnp.sin(ang)
    n = ang.shape[0]
    rest = HEAD_DIM - ROPE_DIM
    one, zero, zh = jnp.ones((n, rest), F32), jnp.zeros((n, rest), F32), jnp.zeros((n, half), F32)
    cos_t = jnp.concatenate([cos, cos, one], axis=-1)
    sin_a = jnp.concatenate([zh, sin, zero], axis=-1)
    sin_b = jnp.concatenate([-sin, zh, zero], axis=-1)
    return tuple(jnp.concatenate([t, t], axis=-1) for t in (cos_t, sin_a, sin_b))


def _rope(x, cos_t, sin_a, sin_b):
    half = ROPE_DIM // 2
    return x * cos_t + pltpu.roll(x, half, 1) * sin_a + pltpu.roll(x, LANES - half, 1) * sin_b


def _rope_transposed(g, cos_t, sin_a, sin_b):
    half = ROPE_DIM // 2
    return g * cos_t + pltpu.roll(g * sin_a, LANES - half, 1) + pltpu.roll(g * sin_b, half, 1)


def _row_spec(tm, width):
    return pl.BlockSpec((tm, width), lambda i: (i, 0))


def _full_spec(shape):
    return pl.BlockSpec(shape, lambda *_: (0,) * len(shape))


def attn_in_proj(x, w_pre, w_in, b_in, tables, tm=512):
    n = x.shape[0]

    def body(x_ref, wp_ref, w_ref, b_ref, c_ref, sa_ref, sb_ref, h_ref, q_ref, k_ref, v_ref, z_ref):
        xv = x_ref[...]
        h = (xv * _rms(xv) * wp_ref[...]).astype(BF16)
        h_ref[...] = h
        proj = jnp.dot(h, w_ref[...], preferred_element_type=F32) + b_ref[...]
        tabs = (c_ref[...], sa_ref[...], sb_ref[...])
        for s in range(D_MODEL // LANES):
            sl = slice(s * LANES, (s + 1) * LANES)
            q_ref[:, sl] = _rope(proj[:, sl] * (HEAD_DIM ** -0.5), *tabs).astype(BF16)
        k_ref[...] = _rope(proj[:, D_MODEL:D_MODEL + KV_WIDTH], *tabs).astype(BF16)
        v_ref[...] = proj[:, D_MODEL + KV_WIDTH:D_MODEL + 2 * KV_WIDTH].astype(BF16)
        z_ref[...] = proj[:, D_MODEL + 2 * KV_WIDTH:]

    return pl.pallas_call(
        body, name="attn_in_proj", grid=(n // tm,),
        in_specs=[_row_spec(tm, D_MODEL), _full_spec((1, D_MODEL)), _full_spec((D_MODEL, ATTN_IN)),
                  _full_spec((1, ATTN_IN))] + [_row_spec(tm, LANES)] * 3,
        out_specs=[_row_spec(tm, D_MODEL), _row_spec(tm, D_MODEL), _row_spec(tm, KV_WIDTH),
                   _row_spec(tm, KV_WIDTH), _row_spec(tm, D_MODEL)],
        out_shape=[jax.ShapeDtypeStruct((n, D_MODEL), BF16), jax.ShapeDtypeStruct((n, D_MODEL), BF16),
                   jax.ShapeDtypeStruct((n, KV_WIDTH), BF16), jax.ShapeDtypeStruct((n, KV_WIDTH), BF16),
                   jax.ShapeDtypeStruct((n, D_MODEL), F32)],
        compiler_params=_cparams(1),
    )(x, w_pre, w_in, b_in, *tables)


def rec_in_proj(x, w_pre, w_in, tm=256):
    n = x.shape[0]

    def body(x_ref, wp_ref, w_ref, h_ref, p_ref):
        xv = x_ref[...]
        h = (xv * _rms(xv) * wp_ref[...]).astype(BF16)
        h_ref[...] = h
        p_ref[...] = jnp.dot(h, w_ref[...], preferred_element_type=F32)

    return pl.pallas_call(
        body, name="rec_in_proj", grid=(n // tm,),
        in_specs=[_row_spec(tm, D_MODEL), _full_spec((1, D_MODEL)), _full_spec((D_MODEL, REC_IN))],
        out_specs=[_row_spec(tm, D_MODEL), _row_spec(tm, REC_IN)],
        out_shape=[jax.ShapeDtypeStruct((n, D_MODEL), BF16), jax.ShapeDtypeStruct((n, REC_IN), F32)],
        compiler_params=_cparams(1),
    )(x, w_pre, w_in)


def out_proj(og, w_out, b_out, x_res, w_post, target=None, tm=512):
    n = og.shape[0]
    with_loss = target is not None

    def body(*refs):
        if with_loss:
            og_ref, w_ref, b_ref, x_ref, wp_ref, t_ref, y_ref, dx_ref, l_ref = refs
        else:
            og_ref, w_ref, b_ref, x_ref, wp_ref, y_ref, xo_ref = refs
        y = jnp.dot(og_ref[...], w_ref[...], preferred_element_type=F32) + b_ref[...]
        y_ref[...] = y
        xo = x_ref[...] + y * _rms(y) * wp_ref[...]
        if with_loss:
            err = xo - t_ref[...]
            dx_ref[...] = err * (1.0 / D_MODEL)

            @pl.when(pl.program_id(0) == 0)
            def _():
                l_ref[...] = jnp.zeros_like(l_ref)

            l_ref[...] += jnp.sum(err * err, axis=0, keepdims=True)
        else:
            xo_ref[...] = xo

    in_specs = [_row_spec(tm, D_MODEL), _full_spec((D_MODEL, D_MODEL)), _full_spec((1, D_MODEL)),
                _row_spec(tm, D_MODEL), _full_spec((1, D_MODEL))]
    out_specs = [_row_spec(tm, D_MODEL), _row_spec(tm, D_MODEL)]
    out_shape = [jax.ShapeDtypeStruct((n, D_MODEL), F32), jax.ShapeDtypeStruct((n, D_MODEL), F32)]
    args = [og, w_out, b_out, x_res, w_post]
    if with_loss:
        in_specs.append(_row_spec(tm, D_MODEL))
        out_specs.append(_full_spec((1, D_MODEL)))
        out_shape.append(jax.ShapeDtypeStruct((1, D_MODEL), F32))
        args.append(target)
    return pl.pallas_call(
        body, name="out_proj_loss" if with_loss else "out_proj", grid=(n // tm,),
        in_specs=in_specs, out_specs=out_specs, out_shape=out_shape, compiler_params=_cparams(1),
    )(*args)


def out_proj_bwd(dxo, y, og, w_out, w_post, tm=512):
    n = og.shape[0]

    def body(g_ref, y_ref, og_ref, w_ref, wp_ref, dog_ref, dw_ref, db_ref, dwp_ref):
        @pl.when(pl.program_id(0) == 0)
        def _():
            dw_ref[...] = jnp.zeros_like(dw_ref)
            db_ref[...] = jnp.zeros_like(db_ref)
            dwp_ref[...] = jnp.zeros_like(dwp_ref)

        g, y = g_ref[...], y_ref[...]
        rstd = _rms(y)
        yn = y * rstd
        gw = g * wp_ref[...]
        dwp_ref[...] += jnp.sum(g * yn, axis=0, keepdims=True)
        dy = rstd * (gw - yn * jnp.mean(gw * yn, axis=-1, keepdims=True))
        db_ref[...] += jnp.sum(dy, axis=0, keepdims=True)
        dyb = dy.astype(BF16)
        dog_ref[...] = _dot(dyb, w_ref[...], _NT)
        dw_ref[...] += _dot(og_ref[...], dyb, _TN)

    return pl.pallas_call(
        body, name="out_proj_bwd", grid=(n // tm,),
        in_specs=[_row_spec(tm, D_MODEL), _row_spec(tm, D_MODEL), _row_spec(tm, D_MODEL),
                  _full_spec((D_MODEL, D_MODEL)), _full_spec((1, D_MODEL))],
        out_specs=[_row_spec(tm, D_MODEL), _full_spec((D_MODEL, D_MODEL)), _full_spec((1, D_MODEL)),
                   _full_spec((1, D_MODEL))],
        out_shape=[jax.ShapeDtypeStruct((n, D_MODEL), F32), jax.ShapeDtypeStruct((D_MODEL, D_MODEL), F32),
                   jax.ShapeDtypeStruct((1, D_MODEL), F32), jax.ShapeDtypeStruct((1, D_MODEL), F32)],
        compiler_params=_cparams(1),
    )(dxo, y, og, w_out, w_post)


def in_proj_bwd_x(dproj, w_in, x, w_pre, dxo, tm=512):
    n, p = dproj.shape

    def body(dp_ref, w_ref, x_ref, wp_ref, g_ref, dx_ref, dwp_ref):
        @pl.when(pl.program_id(0) == 0)
        def _():
            dwp_ref[...] = jnp.zeros_like(dwp_ref)

        dh = _dot(dp_ref[...], w_ref[...], _NT)
        xv = x_ref[...]
        rstd = _rms(xv)
        xn = xv * rstd
        gw = dh * wp_ref[...]
        dwp_ref[...] += jnp.sum(dh * xn, axis=0, keepdims=True)
        dx_ref[...] = rstd * (gw - xn * jnp.mean(gw * xn, axis=-1, keepdims=True)) + g_ref[...]

    return pl.pallas_call(
        body, name=f"in_proj_bwd_x_{p}", grid=(n // tm,),
        in_specs=[_row_spec(tm, p), _full_spec((D_MODEL, p)), _row_spec(tm, D_MODEL), _full_spec((1, D_MODEL)),
                  _row_spec(tm, D_MODEL)],
        out_specs=[_row_spec(tm, D_MODEL), _full_spec((1, D_MODEL))],
        out_shape=[jax.ShapeDtypeStruct((n, D_MODEL), F32), jax.ShapeDtypeStruct((1, D_MODEL), F32)],
        compiler_params=_cparams(1),
    )(dproj, w_in, x, w_pre, dxo)


def in_proj_bwd_w(h, dproj, tm=512, rc=256):
    n, p = dproj.shape

    def body(h_ref, dp_ref, dw_ref, db_ref):
        first_tile = pl.program_id(1) == 0

        @pl.when(first_tile)
        def _():
            dw_ref[...] = jnp.zeros_like(dw_ref)

        @pl.when(first_tile & (pl.program_id(0) == 0))
        def _():
            db_ref[...] = jnp.zeros_like(db_ref)

        dp = dp_ref[...]
        dw_ref[...] += _dot(h_ref[...], dp, _TN)

        @pl.when(pl.program_id(0) == 0)
        def _():
            db_ref[...] += jnp.sum(dp.astype(F32), axis=0, keepdims=True)

    return pl.pallas_call(
        body, name=f"in_proj_bwd_w_{p}", grid=(D_MODEL // rc, n // tm),
        in_specs=[pl.BlockSpec((tm, rc), lambda r, i: (i, r)), pl.BlockSpec((tm, p), lambda r, i: (i, 0))],
        out_specs=[pl.BlockSpec((rc, p), lambda r, i: (r, 0)), pl.BlockSpec((1, p), lambda r, i: (0, 0))],
        out_shape=[jax.ShapeDtypeStruct((D_MODEL, p), F32), jax.ShapeDtypeStruct((1, p), F32)],
        compiler_params=_cparams(2),
    )(h, dproj)


def _kv_windows(k_ref, v_ref, i):
    ps = pl.multiple_of(jnp.maximum(i - 1, 0) * ATTN_BLOCK, ATTN_BLOCK)
    cs = pl.multiple_of(i * ATTN_BLOCK, ATTN_BLOCK)
    kw = jnp.concatenate([k_ref[pl.ds(ps, ATTN_BLOCK), :], k_ref[pl.ds(cs, ATTN_BLOCK), :]], axis=0)
    vw = jnp.concatenate([v_ref[pl.ds(ps, ATTN_BLOCK), :], v_ref[pl.ds(cs, ATTN_BLOCK), :]], axis=0)
    return kw, vw, ps, cs


def _head_cols(h):
    return slice(h * HEAD_DIM, (h + 1) * HEAD_DIM)


def attn_fwd(q, k, v, z, sinks, batch, seq):
    nb = seq // ATTN_BLOCK

    def body(q_ref, k_ref, v_ref, z_ref, s_ref, og_ref):
        i = pl.program_id(1)
        kw, vw, _, _ = _kv_windows(k_ref, v_ref, i)
        outs = []
        for h in range(N_HEADS):
            kv = _head_cols(h // GROUP)
            outs.append(_attn_head(q_ref[:, _head_cols(h)], kw[:, kv], vw[:, kv], z_ref[:, _head_cols(h)],
                                   s_ref[0:1, h:h + 1], i == 0))
        og_ref[...] = jnp.concatenate(outs, axis=-1).astype(BF16)

    blk = lambda w: pl.BlockSpec((ATTN_BLOCK, w), lambda b, i: (b * nb + i, 0))
    seq_spec = pl.BlockSpec((seq, KV_WIDTH), lambda b, i: (b, 0))
    return pl.pallas_call(
        body, name="attn_fwd", grid=(batch, nb),
        in_specs=[blk(D_MODEL), seq_spec, seq_spec, blk(D_MODEL), _full_spec((1, N_HEADS))],
        out_specs=blk(D_MODEL),
        out_shape=jax.ShapeDtypeStruct((batch * seq, D_MODEL), BF16),
        compiler_params=_cparams(2),
    )(q, k, v, z, sinks)


def attn_bwd(q, k, v, z, sinks, dog, tables, batch, seq):
    nb = seq // ATTN_BLOCK

    def body(q_ref, k_ref, v_ref, z_ref, s_ref, g_ref, c_ref, sa_ref, sb_ref, dp_ref, dk_ref, dv_ref, ds_ref):
        b, i = pl.program_id(0), pl.program_id(1)

        @pl.when((b == 0) & (i == 0))
        def _():
            ds_ref[...] = jnp.zeros_like(ds_ref)

        @pl.when(i == 0)
        def _():
            dk_ref[...] = jnp.zeros_like(dk_ref)
            dv_ref[...] = jnp.zeros_like(dv_ref)

        kw, vw, ps, cs = _kv_windows(k_ref, v_ref, i)
        kw, vw = kw.astype(F32), vw.astype(F32)
        head = functools.partial(_attn_head, first=i == 0)
        dq, dz, dsink, dkw, dvw = [], [], [], [], []
        for kvh in range(N_KV_HEADS):
            kv = _head_cols(kvh)
            dkk = jnp.zeros((2 * ATTN_BLOCK, HEAD_DIM), F32)
            dvv = jnp.zeros((2 * ATTN_BLOCK, HEAD_DIM), F32)
            for j in range(GROUP):
                cols = _head_cols(kvh * GROUP + j)
                hh = kvh * GROUP + j
                _, vjp = jax.vjp(head, q_ref[:, cols].astype(F32), kw[:, kv], vw[:, kv], z_ref[:, cols],
                                 s_ref[0:1, hh:hh + 1])
                dqh, dkh, dvh, dzh, dsh = vjp(g_ref[:, cols])
                dq.append(dqh)
                dz.append(dzh)
                dsink.append(dsh)
                dkk, dvv = dkk + dkh, dvv + dvh
            dkw.append(dkk)
            dvw.append(dvv)
        dkw, dvw = jnp.concatenate(dkw, axis=-1), jnp.concatenate(dvw, axis=-1)
        dk_ref[pl.ds(ps, ATTN_BLOCK), :] += dkw[:ATTN_BLOCK]
        dk_ref[pl.ds(cs, ATTN_BLOCK), :] += dkw[ATTN_BLOCK:]
        dv_ref[pl.ds(ps, ATTN_BLOCK), :] += dvw[:ATTN_BLOCK]
        dv_ref[pl.ds(cs, ATTN_BLOCK), :] += dvw[ATTN_BLOCK:]
        ds_ref[...] += jnp.concatenate(dsink, axis=-1)

        tabs = (c_ref[...], sa_ref[...], sb_ref[...])
        for s in range(D_MODEL // LANES):
            pair = jnp.concatenate(dq[2 * s:2 * s + 2], axis=-1) * (HEAD_DIM ** -0.5)
            dp_ref[:, s * LANES:(s + 1) * LANES] = _rope_transposed(pair, *tabs).astype(BF16)
        dp_ref[:, D_MODEL:D_MODEL + 2 * KV_WIDTH] = jnp.zeros((ATTN_BLOCK, 2 * KV_WIDTH), BF16)
        dp_ref[:, D_MODEL + 2 * KV_WIDTH:] = jnp.concatenate(dz, axis=-1).astype(BF16)

    blk = lambda w: pl.BlockSpec((ATTN_BLOCK, w), lambda b, i: (b * nb + i, 0))
    seq_spec = pl.BlockSpec((seq, KV_WIDTH), lambda b, i: (b, 0))
    n = batch * seq
    return pl.pallas_call(
        body, name="attn_bwd", grid=(batch, nb),
        in_specs=[blk(D_MODEL), seq_spec, seq_spec, blk(D_MODEL), _full_spec((1, N_HEADS)), blk(D_MODEL)]
        + [blk(LANES)] * 3,
        out_specs=[blk(ATTN_IN), seq_spec, seq_spec, _full_spec((1, N_HEADS))],
        out_shape=[jax.ShapeDtypeStruct((n, ATTN_IN), BF16), jax.ShapeDtypeStruct((n, KV_WIDTH), F32),
                   jax.ShapeDtypeStruct((n, KV_WIDTH), F32), jax.ShapeDtypeStruct((1, N_HEADS), F32)],
        compiler_params=_cparams(2),
    )(q, k, v, z, sinks, dog, *tables)


def attn_bwd_kv(dproj, dk, dv, tables, tm=512):
    n = dproj.shape[0]

    def body(dp_in_ref, dk_ref, dv_ref, c_ref, sa_ref, sb_ref, dp_ref):
        del dp_in_ref
        dp_ref[:, :KV_WIDTH] = _rope_transposed(dk_ref[...], c_ref[...], sa_ref[...], sb_ref[...]).astype(BF16)
        dp_ref[:, KV_WIDTH:] = dv_ref[...].astype(BF16)

    kv_cols = pl.BlockSpec((tm, 2 * KV_WIDTH), lambda i: (i, D_MODEL // (2 * KV_WIDTH)))
    return pl.pallas_call(
        body, name="attn_bwd_kv", grid=(n // tm,),
        in_specs=[kv_cols, _row_spec(tm, KV_WIDTH), _row_spec(tm, KV_WIDTH)] + [_row_spec(tm, LANES)] * 3,
        out_specs=kv_cols, out_shape=jax.ShapeDtypeStruct(dproj.shape, BF16),
        input_output_aliases={0: 0}, compiler_params=_cparams(1),
    )(dproj, dk, dv, *tables)


def _rec_cols(part, h):
    return slice(part * D_MODEL + h * REC_DIM, part * D_MODEL + (h + 1) * REC_DIM)


def _rec_args(p_ref, lb_ref, gw_ref, h, S):
    hs = slice(h * REC_DIM, (h + 1) * REC_DIM)
    return (p_ref[:, _rec_cols(0, h)], p_ref[:, _rec_cols(1, h)], p_ref[:, _rec_cols(2, h)],
            p_ref[:, _rec_cols(3, h)], S, lb_ref[0:1, hs], lb_ref[1:2, hs], gw_ref[...])


def rec_fwd(proj, lb_logits, gnorm_w, batch, seq):
    nblk = seq // REC_BLOCK

    def body(p_ref, lb_ref, gw_ref, og_ref, st_ref, s_scr):
        @pl.when(pl.program_id(1) == 0)
        def _():
            s_scr[...] = jnp.zeros_like(s_scr)

        for h in range(REC_HEADS):
            S = s_scr[h]
            st_ref[0, h] = S
            og, S_new = _rec_head(*_rec_args(p_ref, lb_ref, gw_ref, h, S))
            og_ref[:, h * REC_DIM:(h + 1) * REC_DIM] = og.astype(BF16)
            s_scr[h] = S_new

    blk = lambda w: pl.BlockSpec((REC_BLOCK, w), lambda b, j: (b * nblk + j, 0))
    st_spec = pl.BlockSpec((1, REC_HEADS, REC_DIM, REC_DIM), lambda b, j: (b * nblk + j, 0, 0, 0))
    return pl.pallas_call(
        body, name="rec_fwd", grid=(batch, nblk),
        in_specs=[blk(REC_IN), _full_spec((2, D_MODEL)), _full_spec((1, REC_DIM))],
        out_specs=[blk(D_MODEL), st_spec],
        out_shape=[jax.ShapeDtypeStruct((batch * seq, D_MODEL), BF16),
                   jax.ShapeDtypeStruct((batch * nblk, REC_HEADS, REC_DIM, REC_DIM), F32)],
        scratch_shapes=[pltpu.VMEM((REC_HEADS, REC_DIM, REC_DIM), F32)],
        compiler_params=_cparams(2),
    )(proj, lb_logits, gnorm_w)


def rec_bwd(proj, states, lb_logits, gnorm_w, dog, batch, seq):
    nblk = seq // REC_BLOCK

    def body(p_ref, st_ref, lb_ref, gw_ref, g_ref, dp_ref, dlb_ref, dgw_ref, ds_scr):
        @pl.when((pl.program_id(0) == 0) & (pl.program_id(1) == 0))
        def _():
            dlb_ref[...] = jnp.zeros_like(dlb_ref)
            dgw_ref[...] = jnp.zeros_like(dgw_ref)

        @pl.when(pl.program_id(1) == 0)
        def _():
            ds_scr[...] = jnp.zeros_like(ds_scr)

        for h in range(REC_HEADS):
            hs = slice(h * REC_DIM, (h + 1) * REC_DIM)
            _, vjp = jax.vjp(_rec_head, *_rec_args(p_ref, lb_ref, gw_ref, h, st_ref[0, h]))
            dqr, dfr, dv, dz, dS, dl0, dl1, dgw = vjp((g_ref[:, hs], ds_scr[h]))
            for part, val in enumerate((dqr, dfr, dv, dz)):
                dp_ref[:, _rec_cols(part, h)] = val.astype(BF16)
            ds_scr[h] = dS
            dlb_ref[0:1, hs] += dl0
            dlb_ref[1:2, hs] += dl1
            dgw_ref[...] += dgw

    blk = lambda w: pl.BlockSpec((REC_BLOCK, w), lambda b, j: (b * nblk + nblk - 1 - j, 0))
    st_spec = pl.BlockSpec((1, REC_HEADS, REC_DIM, REC_DIM), lambda b, j: (b * nblk + nblk - 1 - j, 0, 0, 0))
    return pl.pallas_call(
        body, name="rec_bwd", grid=(batch, nblk),
        in_specs=[blk(REC_IN), st_spec, _full_spec((2, D_MODEL)), _full_spec((1, REC_DIM)), blk(D_MODEL)],
        out_specs=[blk(REC_IN), _full_spec((2, D_MODEL)), _full_spec((1, REC_DIM))],
        out_shape=[jax.ShapeDtypeStruct((batch * seq, REC_IN), BF16), jax.ShapeDtypeStruct((2, D_MODEL), F32),
                   jax.ShapeDtypeStruct((1, REC_DIM), F32)],
        scratch_shapes=[pltpu.VMEM((REC_HEADS, REC_DIM, REC_DIM), F32)],
        compiler_params=_cparams(2),
    )(proj, states, lb_logits, gnorm_w, dog)


_ANY = pl.BlockSpec(memory_space=pl.ANY)


def _chip_peers():
    x, y, c = lax.axis_index("x"), lax.axis_index("y"), lax.axis_index("c")
    flips = ((1, 0), (0, 1), (1, 1))
    peers = [(jnp.where(fx, 1 - x, x), jnp.where(fy, 1 - y, y), c) for fx, fy in flips]
    return 2 * x + y, peers


def chip_exchange(arrays, scatter):
    n = len(arrays)

    def body(*refs):
        ins, outs = refs[:n], refs[n:2 * n]
        send_sems, recv_sems, local_sems = refs[2 * n:]
        me, peers = _chip_peers()
        local = [pltpu.make_async_copy(ins[k].at[me] if scatter else ins[k], outs[k].at[me], local_sems.at[k])
                 for k in range(n)]
        for cp in local:
            cp.start()
        sends, recvs = [], []
        for k in range(n):
            for j, (px, py, pc) in enumerate(peers):
                idx = 2 * px + py
                sem = k * len(peers) + j
                sends.append(pltpu.make_async_remote_copy(
                    src_ref=ins[k].at[idx] if scatter else ins[k], dst_ref=outs[k].at[me],
                    send_sem=send_sems.at[sem], recv_sem=recv_sems.at[sem],
                    device_id=(px, py, pc), device_id_type=MESH))
                recvs.append(pltpu.make_async_remote_copy(
                    src_ref=ins[k].at[me] if scatter else ins[k], dst_ref=outs[k].at[idx],
                    send_sem=send_sems.at[sem], recv_sem=recv_sems.at[sem],
                    device_id=(px, py, pc), device_id_type=MESH))
        for cp in sends:
            cp.start()
        for cp in recvs:
            cp.wait_recv()
        for cp in sends:
            cp.wait_send()
        for cp in local:
            cp.wait()

    out_shape = [jax.ShapeDtypeStruct(a.shape if scatter else (N_CHIPS,) + a.shape, a.dtype) for a in arrays]
    n_copies = n * (N_CHIPS - 1)
    return pl.pallas_call(
        body, name="chip_scatter" if scatter else "chip_gather",
        in_specs=[_ANY] * n, out_specs=[_ANY] * n, out_shape=out_shape,
        scratch_shapes=[pltpu.SemaphoreType.DMA((n_copies,)), pltpu.SemaphoreType.DMA((n_copies,)),
                        pltpu.SemaphoreType.DMA((n,))],
    )(*arrays)


def sibling_exchange(arrays):
    n = len(arrays)

    def body(*refs):
        ins, outs = refs[:n], refs[n:2 * n]
        send_sems, recv_sems = refs[2 * n:]
        sibling = (lax.axis_index("x"), lax.axis_index("y"), 1 - lax.axis_index("c"))
        copies = [pltpu.make_async_remote_copy(src_ref=ins[k], dst_ref=outs[k], send_sem=send_sems.at[k],
                                               recv_sem=recv_sems.at[k], device_id=sibling, device_id_type=MESH)
                  for k in range(n)]
        for cp in copies:
            cp.start()
        for cp in copies:
            cp.wait()

    return pl.pallas_call(
        body, name="sibling_exchange", in_specs=[_ANY] * n, out_specs=[_ANY] * n,
        out_shape=[jax.ShapeDtypeStruct(a.shape, a.dtype) for a in arrays],
        scratch_shapes=[pltpu.SemaphoreType.DMA((n,)), pltpu.SemaphoreType.DMA((n,))],
    )(*arrays)


def all_gather_small(vec):
    def body(v_ref, out_ref, send_sems, recv_sems, local_sem):
        x, y, c = lax.axis_index("x"), lax.axis_index("y"), lax.axis_index("c")
        me = 4 * x + 2 * y + c
        local = pltpu.make_async_copy(v_ref, out_ref.at[me], local_sem)
        local.start()
        sends, recvs = [], []
        for j in range(1, N_DEV):
            px = jnp.where(j & 4, 1 - x, x)
            py = jnp.where(j & 2, 1 - y, y)
            pc = jnp.where(j & 1, 1 - c, c)
            common = dict(send_sem=send_sems.at[j - 1], recv_sem=recv_sems.at[j - 1], device_id=(px, py, pc),
                          device_id_type=MESH)
            sends.append(pltpu.make_async_remote_copy(src_ref=v_ref, dst_ref=out_ref.at[me], **common))
            recvs.append(pltpu.make_async_remote_copy(src_ref=v_ref, dst_ref=out_ref.at[4 * px + 2 * py + pc],
                                                      **common))
        for cp in sends:
            cp.start()
        for cp in recvs:
            cp.wait_recv()
        for cp in sends:
            cp.wait_send()
        local.wait()

    return pl.pallas_call(
        body, name="all_gather_small", in_specs=[_ANY], out_specs=_ANY,
        out_shape=jax.ShapeDtypeStruct((N_DEV,) + vec.shape, vec.dtype),
        scratch_shapes=[pltpu.SemaphoreType.DMA((N_DEV - 1,)), pltpu.SemaphoreType.DMA((N_DEV - 1,)),
                        pltpu.SemaphoreType.DMA],
    )(vec)


def sum_slots(stacked, tm=256):
    s, r, c = stacked.shape
    tm = min(tm, r)

    def body(in_ref, out_ref):
        acc = in_ref[0].astype(F32)
        for t in range(1, s):
            acc = acc + in_ref[t].astype(F32)
        out_ref[...] = acc

    return pl.pallas_call(
        body, name=f"sum_slots_{s}_{r}_{c}", grid=(r // tm,),
        in_specs=[pl.BlockSpec((s, tm, c), lambda i: (0, i, 0))], out_specs=_row_spec(tm, c),
        out_shape=jax.ShapeDtypeStruct((r, c), F32), compiler_params=_cparams(1),
    )(stacked)


def adamw(w, m, v, g_a, g_b=None, tm=256):
    r, c = w.shape
    tm = min(tm, r)
    two = g_b is not None

    def body(*refs):
        w_ref, m_ref, v_ref, ga_ref = refs[:4]
        g_ref, d_ref, nm_ref, nv_ref = refs[-4:]
        g = ga_ref[...] + refs[4][...] if two else ga_ref[...]
        nm = ADAM_B1 * m_ref[...] + (1.0 - ADAM_B1) * g
        nv = ADAM_B2 * v_ref[...] + (1.0 - ADAM_B2) * (g * g)
        m_hat = nm / (1.0 - ADAM_B1 ** ADAM_STEP)
        v_hat = nv / (1.0 - ADAM_B2 ** ADAM_STEP)
        g_ref[...] = g
        d_ref[...] = -ADAM_LR * (m_hat / (jnp.sqrt(v_hat) + ADAM_EPS) + ADAM_WD * w_ref[...])
        nm_ref[...] = nm
        nv_ref[...] = nv

    args = [w, m, v, g_a] + ([g_b] if two else [])
    return pl.pallas_call(
        body, name=f"adamw_{r}_{c}", grid=(r // tm,),
        in_specs=[_row_spec(tm, c)] * len(args), out_specs=[_row_spec(tm, c)] * 4,
        out_shape=[jax.ShapeDtypeStruct((r, c), F32)] * 4, compiler_params=_cparams(1),
    )(*args)


_SMALL = (("pre_norm_w", (2, D_MODEL)), ("post_norm_w", (2, D_MODEL)), ("attn_b_in", (1, ATTN_IN)),
          ("attn_sinks", (1, N_HEADS)), ("attn_b_out", (1, D_MODEL)), ("rec_lb_logits", (2, D_MODEL)),
          ("rec_gnorm_w", (1, REC_DIM)))
_SMALL_ROWS = 16


def _pack_small(parts):
    rows = []
    for (name, shape) in _SMALL:
        flat = parts[name].reshape(-1)
        pad = -flat.shape[0] % D_MODEL
        rows.append(jnp.pad(flat, (0, pad)).reshape(-1, D_MODEL))
    packed = jnp.concatenate(rows, axis=0)
    return jnp.pad(packed, ((0, _SMALL_ROWS - packed.shape[0]), (0, 0)))


def _unpack_small(packed):
    out, row = {}, 0
    for (name, shape) in _SMALL:
        size = shape[0] * shape[1]
        nrows = -(-size // D_MODEL)
        out[name] = packed[row:row + nrows].reshape(-1)[:size].reshape(shape)
        row += nrows
    return out


def local_step(x, positions, pre_norm_w, post_norm_w, attn_w_in, attn_b_in, attn_sinks, attn_w_out, attn_b_out,
               rec_w_in, rec_lb_logits, rec_gnorm_w, rec_w_out, loss_target):
    batch, seq, _ = x.shape
    n = batch * seq
    x0 = x.reshape(n, D_MODEL)
    tables = _rope_tables(positions)
    pre0, pre1 = pre_norm_w[0:1], pre_norm_w[1:2]
    post0, post1 = post_norm_w[0:1], post_norm_w[1:2]
    no_bias = jnp.zeros((1, D_MODEL), F32)

    h0, q, k, v, z = attn_in_proj(x0, pre0, attn_w_in, attn_b_in, tables)
    og0 = attn_fwd(q, k, v, z, attn_sinks, batch, seq)
    y0, x1 = out_proj(og0, attn_w_out, attn_b_out, x0, post0)

    h1, proj1 = rec_in_proj(x1, pre1, rec_w_in)
    og1, states = rec_fwd(proj1, rec_lb_logits, rec_gnorm_w, batch, seq)
    y1, dx2, loss_vec = out_proj(og1, rec_w_out, no_bias, x1, post1, target=loss_target.reshape(n, D_MODEL))
    loss = jnp.sum(loss_vec) * (0.5 / D_MODEL)

    dog1, d_rec_w_out, _, d_post1 = out_proj_bwd(dx2, y1, og1, rec_w_out, post1)
    dproj1, d_lb, d_gnorm = rec_bwd(proj1, states, rec_lb_logits, rec_gnorm_w, dog1, batch, seq)
    dx1, d_pre1 = in_proj_bwd_x(dproj1, rec_w_in, x1, pre1, dx2)
    d_rec_w_in, _ = in_proj_bwd_w(h1, dproj1)

    dog0, d_attn_w_out, d_attn_b_out, d_post0 = out_proj_bwd(dx1, y0, og0, attn_w_out, post0)
    dproj0, dk, dv, d_sinks = attn_bwd(q, k, v, z, attn_sinks, dog0, tables, batch, seq)
    dproj0 = attn_bwd_kv(dproj0, dk, dv, tables)
    dx0, d_pre0 = in_proj_bwd_x(dproj0, attn_w_in, x0, pre0, dx1)
    d_attn_w_in, d_attn_b_in = in_proj_bwd_w(h0, dproj0)

    grads = dict(
        pre_norm_w=jnp.concatenate([d_pre0, d_pre1], axis=0), post_norm_w=jnp.concatenate([d_post0, d_post1], axis=0),
        attn_w_in=d_attn_w_in, attn_b_in=d_attn_b_in, attn_sinks=d_sinks, attn_w_out=d_attn_w_out,
        attn_b_out=d_attn_b_out, rec_w_in=d_rec_w_in, rec_lb_logits=d_lb, rec_gnorm_w=d_gnorm,
        rec_w_out=d_rec_w_out)
    return loss, dx0.reshape(batch, seq, D_MODEL), grads


_BIG = ("attn_w_in", "attn_w_out", "rec_w_in", "rec_w_out")
_COLUMN_SHARDED = ("attn_w_in", "rec_w_in")
_ORDER = ("pre_norm_w", "post_norm_w", "attn_w_in", "attn_b_in", "attn_sinks", "attn_w_out", "attn_b_out",
          "rec_w_in", "rec_lb_logits", "rec_gnorm_w", "rec_w_out")


def _whole_from_shards(name, stacked):
    if name in _COLUMN_SHARDED:
        return jnp.transpose(stacked, (1, 0, 2)).reshape(stacked.shape[1], -1)
    return stacked.reshape(-1, stacked.shape[2])


def _shards_from_whole(name, whole):
    if name in _COLUMN_SHARDED:
        return jnp.transpose(whole.reshape(whole.shape[0], N_CHIPS, -1), (1, 0, 2))
    return whole.reshape(N_CHIPS, -1, whole.shape[1])


def kernel(x, positions, pre_norm_w, post_norm_w, attn_w_in, attn_b_in, attn_sinks, attn_w_out, attn_b_out, rec_w_in, rec_lb_logits, rec_gnorm_w, rec_w_out, loss_target, m_pre_norm_w, m_post_norm_w, m_attn_w_in, m_attn_b_in, m_attn_sinks, m_attn_w_out, m_attn_b_out, m_rec_w_in, m_rec_lb_logits, m_rec_gnorm_w, m_rec_w_out, v_pre_norm_w, v_post_norm_w, v_attn_w_in, v_attn_b_in, v_attn_sinks, v_attn_w_out, v_attn_b_out, v_rec_w_in, v_rec_lb_logits, v_rec_gnorm_w, v_rec_w_out):
    w = dict(pre_norm_w=pre_norm_w, post_norm_w=post_norm_w, attn_w_in=attn_w_in, attn_b_in=attn_b_in,
             attn_sinks=attn_sinks, attn_w_out=attn_w_out, attn_b_out=attn_b_out, rec_w_in=rec_w_in,
             rec_lb_logits=rec_lb_logits, rec_gnorm_w=rec_gnorm_w, rec_w_out=rec_w_out)
    m = dict(pre_norm_w=m_pre_norm_w, post_norm_w=m_post_norm_w, attn_w_in=m_attn_w_in, attn_b_in=m_attn_b_in,
             attn_sinks=m_attn_sinks, attn_w_out=m_attn_w_out, attn_b_out=m_attn_b_out, rec_w_in=m_rec_w_in,
             rec_lb_logits=m_rec_lb_logits, rec_gnorm_w=m_rec_gnorm_w, rec_w_out=m_rec_w_out)
    v = dict(pre_norm_w=v_pre_norm_w, post_norm_w=v_post_norm_w, attn_w_in=v_attn_w_in, attn_b_in=v_attn_b_in,
             attn_sinks=v_attn_sinks, attn_w_out=v_attn_w_out, attn_b_out=v_attn_b_out, rec_w_in=v_rec_w_in,
             rec_lb_logits=v_rec_lb_logits, rec_gnorm_w=v_rec_gnorm_w, rec_w_out=v_rec_w_out)

    shards = {name: w[name][0] for name in _BIG}
    gathered = chip_exchange([shards[name].astype(BF16) for name in _BIG], scatter=False)
    whole = {name: _whole_from_shards(name, g) for name, g in zip(_BIG, gathered)}

    loss, grad_x, grads = local_step(
        x, positions, pre_norm_w, post_norm_w, whole["attn_w_in"], attn_b_in, attn_sinks, whole["attn_w_out"],
        attn_b_out, whole["rec_w_in"], rec_lb_logits, rec_gnorm_w, whole["rec_w_out"], loss_target)
    loss = lax.psum(loss, ("x", "y", "c"))

    parts = chip_exchange([_shards_from_whole(name, grads[name]).astype(BF16) for name in _BIG], scatter=True)
    plane_sums = [sum_slots(p) for p in parts]
    other_sums = sibling_exchange(plane_sums)
    out_g, out_d, out_m, out_v = {}, {}, {}, {}
    for name, mine, other in zip(_BIG, plane_sums, other_sums):
        g, d, nm, nv = adamw(shards[name], m[name][0], v[name][0], mine, other)
        out_g[name], out_d[name], out_m[name], out_v[name] = g[None], d[None], nm[None], nv[None]

    small_sum = sum_slots(all_gather_small(_pack_small(grads)))
    packed = adamw(_pack_small(w), _pack_small(m), _pack_small(v), small_sum)
    for dst, val in zip((out_g, out_d, out_m, out_v), packed):
        dst.update(_unpack_small(val))

    return (loss, grad_x, *[out_g[n] for n in _ORDER], *[out_d[n] for n in _ORDER],
            *[out_m[n] for n in _ORDER], *[out_v[n] for n in _ORDER])
```

```python
import functools

import jax
import jax.numpy as jnp
from jax import lax
from jax.experimental import pallas as pl
from jax.experimental.pallas import tpu as pltpu

F32 = jnp.float32
BF16 = jnp.bfloat16
MESH = pl.DeviceIdType.MESH

D_MODEL = 1024
HEAD_DIM = 64
N_HEADS = 16
N_KV_HEADS = 2
GROUP = N_HEADS // N_KV_HEADS
KV_WIDTH = N_KV_HEADS * HEAD_DIM
ATTN_IN = 2 * D_MODEL + 2 * KV_WIDTH
ATTN_BLOCK = 128
ROPE_THETA = 500000.0
ROPE_DIM = HEAD_DIM // 4
REC_HEADS = 8
REC_DIM = 128
REC_IN = 4 * D_MODEL
REC_BLOCK = 128
DIAG = 8
NORM_EPS = 1e-6
N_CHIPS = 4
N_DEV = 8
LANES = 128

ADAM_LR = 0.001
ADAM_B1 = 0.9
ADAM_B2 = 0.999
ADAM_EPS = 1e-08
ADAM_WD = 0.01
ADAM_STEP = 10

VMEM_LIMIT = 56 * 1024 * 1024


def _cparams(n_axes):
    return pltpu.CompilerParams(dimension_semantics=("arbitrary",) * n_axes, vmem_limit_bytes=VMEM_LIMIT)


def _dot(a, b, contract):
    return lax.dot_general(a.astype(BF16), b.astype(BF16), (contract, ((), ())), preferred_element_type=F32)


_NN = ((1,), (0,))
_NT = ((1,), (1,))
_TN = ((0,), (0,))


@jax.custom_vjp
def mm_nn(a, b):
    return _dot(a, b, _NN)


mm_nn.defvjp(lambda a, b: (_dot(a, b, _NN), (a, b)),
             lambda res, g: (_dot(g, res[1], _NT), _dot(res[0], g, _TN)))


@jax.custom_vjp
def mm_nt(a, b):
    return _dot(a, b, _NT)


mm_nt.defvjp(lambda a, b: (_dot(a, b, _NT), (a, b)),
             lambda res, g: (_dot(g, res[1], _NN), _dot(g, res[0], _TN)))


@jax.custom_vjp
def mm_tn(a, b):
    return _dot(a, b, _TN)


mm_tn.defvjp(lambda a, b: (_dot(a, b, _TN), (a, b)),
             lambda res, g: (_dot(res[1], g, _NT), _dot(res[0], g, _NN)))


def _tri_ones(n, lower):
    r = lax.broadcasted_iota(jnp.int32, (n, n), 0)
    c = lax.broadcasted_iota(jnp.int32, (n, n), 1)
    return ((c <= r) if lower else (c >= r)).astype(F32)


def _dot_exact(a, b):
    return lax.dot_general(a, b, (_NN, ((), ())), precision=lax.Precision.HIGHEST, preferred_element_type=F32)


@jax.custom_vjp
def cumsum_rows(x):
    return _dot_exact(_tri_ones(x.shape[0], True), x)


cumsum_rows.defvjp(lambda x: (cumsum_rows(x), None),
                   lambda _, g: (_dot_exact(_tri_ones(g.shape[0], False), g),))


@functools.partial(jax.custom_vjp, nondiff_argnums=(1,))
def roll_sub(x, d):
    return pltpu.roll(x, d, 1) if d else x


roll_sub.defvjp(lambda x, d: (roll_sub(x, d), None),
                lambda d, _, g: (roll_sub(g, (DIAG - d) % DIAG),))


def sigmoid(x):
    return 0.5 * (jnp.tanh(0.5 * x) + 1.0)


def log_sigmoid_pair(x):
    t = jnp.log(1.0 + jnp.exp(-jnp.abs(x)))
    return jnp.minimum(x, 0.0) - t, jnp.minimum(-x, 0.0) - t


def _rms(x):
    return lax.rsqrt(jnp.mean(x * x, axis=-1, keepdims=True) + NORM_EPS)


def _attn_group(qs, k_a, v_a, k_b, v_b, zs, sink_a, sink_b, bias):
    def half(kh, vh, sink):
        s = mm_nt(qs, kh) + bias
        m = lax.stop_gradient(jnp.maximum(jnp.max(s, axis=-1, keepdims=True), jnp.max(sink, axis=-1, keepdims=True)))
        p = jnp.exp(s - m)
        denom = jnp.sum(p, axis=-1, keepdims=True) + jnp.sum(jnp.exp(sink - m), axis=-1, keepdims=True) * (1.0 / LANES)
        return mm_nn(p * (1.0 / denom), vh)

    return (half(k_a, v_a, sink_a) + half(k_b, v_b, sink_b)) * (zs * sigmoid(zs))


def _rec_head(qr, fr, v, z, S, l0, l1, gw):
    R = qr.shape[0]
    q = qr * sigmoid(qr)
    log_lb, log_1m_lb = log_sigmoid_pair(l1 - l0)
    ls_f, ls_nf = log_sigmoid_pair(fr)
    c = log_1m_lb + ls_f
    lf = jnp.maximum(log_lb, c) + jnp.log(1.0 + jnp.exp(-jnp.abs(log_lb - c)))
    k = jnp.exp(log_1m_lb + ls_nf)
    b = cumsum_rows(lf)
    rows = lax.broadcasted_iota(jnp.int32, (R, REC_DIM), 0)

    o = mm_nt(q * jnp.exp(jnp.minimum(b, 0.0)), S)

    ri = lax.broadcasted_iota(jnp.int32, (R, R), 0)
    ci = lax.broadcasted_iota(jnp.int32, (R, R), 1)
    sc = jnp.zeros((R, R), F32)
    w = R
    while w > DIAG:
        h = w // 2
        b3 = b.reshape(R // w, w, REC_DIM)
        rin = lax.broadcasted_iota(jnp.int32, (R // w, w, REC_DIM), 1)
        mid = jnp.sum(jnp.where(rin == h - 1, b3, 0.0), axis=1, keepdims=True)
        fac = jnp.exp(jnp.minimum(jnp.where(rin >= h, b3 - mid, mid - b3), 0.0)).reshape(R, REC_DIM)
        upper = (rows % w) >= h
        s_w = mm_nt(jnp.where(upper, q * fac, 0.0), jnp.where(upper, 0.0, k * fac))
        sc = sc + jnp.where((ri // w) == (ci // w), s_w, 0.0)
        w = h
    o = o + mm_nn(sc, v)

    g = R // DIAG
    q3, k3, v3, b3 = (t.reshape(g, DIAG, REC_DIM) for t in (q, k, v, b))
    rin = lax.broadcasted_iota(jnp.int32, (g, DIAG, 1), 1)
    od = jnp.zeros((g, DIAG, REC_DIM), F32)
    for d in range(DIAG):
        e = jnp.exp(jnp.minimum(b3 - roll_sub(b3, d), 0.0))
        sd = jnp.sum(q3 * roll_sub(k3, d) * e, axis=-1, keepdims=True)
        od = od + jnp.where(rin >= d, sd, 0.0) * roll_sub(v3, d)
    o = o + od.reshape(R, REC_DIM)

    b_last = jnp.sum(jnp.where(rows == R - 1, b, 0.0), axis=0, keepdims=True)
    S_new = S * jnp.exp(jnp.minimum(b_last, 0.0)) + mm_tn(v, k * jnp.exp(jnp.minimum(b_last - b, 0.0)))

    on = o * _rms(o) * gw
    return on * (z * sigmoid(z)), S_new


def _rope_tables(positions):
    half = ROPE_DIM // 2
    inv_freq = ROPE_THETA ** (-(jnp.arange(half, dtype=F32) * 2.0 / ROPE_DIM))
    ang = positions.astype(F32).reshape(-1, 1) * inv_freq
    cos, sin = jnp.cos(ang), jnp.sin(ang)
    n = ang.shape[0]
    rest = HEAD_DIM - ROPE_DIM
    one, zero, zh = jnp.ones((n, rest), F32), jnp.zeros((n, rest), F32), jnp.zeros((n, half), F32)
    cos_t = jnp.concatenate([cos, cos, one], axis=-1)
    sin_a = jnp.concatenate([zh, sin, zero], axis=-1)
    sin_b = jnp.concatenate([-sin, zh, zero], axis=-1)
    return tuple(jnp.concatenate([t, t], axis=-1) for t in (cos_t, sin_a, sin_b))


def _rope(x, cos_t, sin_a, sin_b):
    half = ROPE_DIM // 2
    return x * cos_t + pltpu.roll(x, half, 1) * sin_a + pltpu.roll(x, LANES - half, 1) * sin_b


def _rope_transposed(g, cos_t, sin_a, sin_b):
    half = ROPE_DIM // 2
    return g * cos_t + pltpu.roll(g * sin_a, LANES - half, 1) + pltpu.roll(g * sin_b, half, 1)


def _row_spec(tm, width):
    return pl.BlockSpec((tm, width), lambda i: (i, 0))


def _full_spec(shape):
    return pl.BlockSpec(shape, lambda *_: (0,) * len(shape))


def attn_in_proj(x, w_pre, w_in, b_in, tables, tm=512):
    n = x.shape[0]

    def body(x_ref, wp_ref, w_ref, b_ref, c_ref, sa_ref, sb_ref, h_ref, q_ref, k_ref, v_ref, z_ref):
        xv = x_ref[...]
        h = (xv * _rms(xv) * wp_ref[...]).astype(BF16)
        h_ref[...] = h
        proj = jnp.dot(h, w_ref[...], preferred_element_type=F32) + b_ref[...]
        tabs = (c_ref[...], sa_ref[...], sb_ref[...])
        for s in range(D_MODEL // LANES):
            sl = slice(s * LANES, (s + 1) * LANES)
            q_ref[:, sl] = _rope(proj[:, sl] * (HEAD_DIM ** -0.5), *tabs).astype(BF16)
        k_ref[...] = _rope(proj[:, D_MODEL:D_MODEL + KV_WIDTH], *tabs).astype(BF16)
        v_ref[...] = proj[:, D_MODEL + KV_WIDTH:D_MODEL + 2 * KV_WIDTH].astype(BF16)
        z_ref[...] = proj[:, D_MODEL + 2 * KV_WIDTH:]

    return pl.pallas_call(
        body, name="attn_in_proj", grid=(n // tm,),
        in_specs=[_row_spec(tm, D_MODEL), _full_spec((1, D_MODEL)), _full_spec((D_MODEL, ATTN_IN)),
                  _full_spec((1, ATTN_IN))] + [_row_spec(tm, LANES)] * 3,
        out_specs=[_row_spec(tm, D_MODEL), _row_spec(tm, D_MODEL), _row_spec(tm, KV_WIDTH),
                   _row_spec(tm, KV_WIDTH), _row_spec(tm, D_MODEL)],
        out_shape=[jax.ShapeDtypeStruct((n, D_MODEL), BF16), jax.ShapeDtypeStruct((n, D_MODEL), BF16),
                   jax.ShapeDtypeStruct((n, KV_WIDTH), BF16), jax.ShapeDtypeStruct((n, KV_WIDTH), BF16),
                   jax.ShapeDtypeStruct((n, D_MODEL), F32)],
        compiler_params=_cparams(1),
    )(x, w_pre, w_in, b_in, *tables)


def rec_in_proj(x, w_pre, w_in, tm=256):
    n = x.shape[0]

    def body(x_ref, wp_ref, w_ref, h_ref, p_ref):
        xv = x_ref[...]
        h = (xv * _rms(xv) * wp_ref[...]).astype(BF16)
        h_ref[...] = h
        p_ref[...] = jnp.dot(h, w_ref[...], preferred_element_type=F32)

    return pl.pallas_call(
        body, name="rec_in_proj", grid=(n // tm,),
        in_specs=[_row_spec(tm, D_MODEL), _full_spec((1, D_MODEL)), _full_spec((D_MODEL, REC_IN))],
        out_specs=[_row_spec(tm, D_MODEL), _row_spec(tm, REC_IN)],
        out_shape=[jax.ShapeDtypeStruct((n, D_MODEL), BF16), jax.ShapeDtypeStruct((n, REC_IN), F32)],
        compiler_params=_cparams(1),
    )(x, w_pre, w_in)


def out_proj(og, w_out, b_out, x_res, w_post, target=None, tm=512):
    n = og.shape[0]
    with_loss = target is not None

    def body(*refs):
        if with_loss:
            og_ref, w_ref, b_ref, x_ref, wp_ref, t_ref, y_ref, dx_ref, l_ref = refs
        else:
            og_ref, w_ref, b_ref, x_ref, wp_ref, y_ref, xo_ref = refs
        y = jnp.dot(og_ref[...], w_ref[...], preferred_element_type=F32) + b_ref[...]
        y_ref[...] = y
        xo = x_ref[...] + y * _rms(y) * wp_ref[...]
        if with_loss:
            err = xo - t_ref[...]
            dx_ref[...] = err * (1.0 / D_MODEL)

            @pl.when(pl.program_id(0) == 0)
            def _():
                l_ref[...] = jnp.zeros_like(l_ref)

            l_ref[...] += jnp.sum(err * err, axis=0, keepdims=True)
        else:
            xo_ref[...] = xo

    in_specs = [_row_spec(tm, D_MODEL), _full_spec((D_MODEL, D_MODEL)), _full_spec((1, D_MODEL)),
                _row_spec(tm, D_MODEL), _full_spec((1, D_MODEL))]
    out_specs = [_row_spec(tm, D_MODEL), _row_spec(tm, D_MODEL)]
    out_shape = [jax.ShapeDtypeStruct((n, D_MODEL), F32), jax.ShapeDtypeStruct((n, D_MODEL), F32)]
    args = [og, w_out, b_out, x_res, w_post]
    if with_loss:
        in_specs.append(_row_spec(tm, D_MODEL))
        out_specs.append(_full_spec((1, D_MODEL)))
        out_shape.append(jax.ShapeDtypeStruct((1, D_MODEL), F32))
        args.append(target)
    return pl.pallas_call(
        body, name="out_proj_loss" if with_loss else "out_proj", grid=(n // tm,),
        in_specs=in_specs, out_specs=out_specs, out_shape=out_shape, compiler_params=_cparams(1),
    )(*args)


def out_proj_bwd(dxo, y, og, w_out, w_post, tm=512):
    n = og.shape[0]

    def body(g_ref, y_ref, og_ref, w_ref, wp_ref, dog_ref, dw_ref, db_ref, dwp_ref):
        @pl.when(pl.program_id(0) == 0)
        def _():
            dw_ref[...] = jnp.zeros_like(dw_ref)
            db_ref[...] = jnp.zeros_like(db_ref)
            dwp_ref[...] = jnp.zeros_like(dwp_ref)

        g, y = g_ref[...], y_ref[...]
        rstd = _rms(y)
        yn = y * rstd
        gw = g * wp_ref[...]
        dwp_ref[...] += jnp.sum(g * yn, axis=0, keepdims=True)
        dy = rstd * (gw - yn * jnp.mean(gw * yn, axis=-1, keepdims=True))
        db_ref[...] += jnp.sum(dy, axis=0, keepdims=True)
        dyb = dy.astype(BF16)
        dog_ref[...] = _dot(dyb, w_ref[...], _NT)
        dw_ref[...] += _dot(og_ref[...], dyb, _TN)

    return pl.pallas_call(
        body, name="out_proj_bwd", grid=(n // tm,),
        in_specs=[_row_spec(tm, D_MODEL), _row_spec(tm, D_MODEL), _row_spec(tm, D_MODEL),
                  _full_spec((D_MODEL, D_MODEL)), _full_spec((1, D_MODEL))],
        out_specs=[_row_spec(tm, D_MODEL), _full_spec((D_MODEL, D_MODEL)), _full_spec((1, D_MODEL)),
                   _full_spec((1, D_MODEL))],
        out_shape=[jax.ShapeDtypeStruct((n, D_MODEL), F32), jax.ShapeDtypeStruct((D_MODEL, D_MODEL), F32),
                   jax.ShapeDtypeStruct((1, D_MODEL), F32), jax.ShapeDtypeStruct((1, D_MODEL), F32)],
        compiler_params=_cparams(1),
    )(dxo, y, og, w_out, w_post)


def in_proj_bwd_x(dproj, w_in, x, w_pre, dxo, tm=512):
    n, p = dproj.shape

    def body(dp_ref, w_ref, x_ref, wp_ref, g_ref, dx_ref, dwp_ref):
        @pl.when(pl.program_id(0) == 0)
        def _():
            dwp_ref[...] = jnp.zeros_like(dwp_ref)

        dh = _dot(dp_ref[...], w_ref[...], _NT)
        xv = x_ref[...]
        rstd = _rms(xv)
        xn = xv * rstd
        gw = dh * wp_ref[...]
        dwp_ref[...] += jnp.sum(dh * xn, axis=0, keepdims=True)
        dx_ref[...] = rstd * (gw - xn * jnp.mean(gw * xn, axis=-1, keepdims=True)) + g_ref[...]

    return pl.pallas_call(
        body, name=f"in_proj_bwd_x_{p}", grid=(n // tm,),
        in_specs=[_row_spec(tm, p), _full_spec((D_MODEL, p)), _row_spec(tm, D_MODEL), _full_spec((1, D_MODEL)),
                  _row_spec(tm, D_MODEL)],
        out_specs=[_row_spec(tm, D_MODEL), _full_spec((1, D_MODEL))],
        out_shape=[jax.ShapeDtypeStruct((n, D_MODEL), F32), jax.ShapeDtypeStruct((1, D_MODEL), F32)],
        compiler_params=_cparams(1),
    )(dproj, w_in, x, w_pre, dxo)


def in_proj_bwd_w(h, dproj, tm=512, rc=256):
    n, p = dproj.shape

    def body(h_ref, dp_ref, dw_ref, db_ref):
        first_tile = pl.program_id(1) == 0

        @pl.when(first_tile)
        def _():
            dw_ref[...] = jnp.zeros_like(dw_ref)

        @pl.when(first_tile & (pl.program_id(0) == 0))
        def _():
            db_ref[...] = jnp.zeros_like(db_ref)

        dp = dp_ref[...]
        dw_ref[...] += _dot(h_ref[...], dp, _TN)

        @pl.when(pl.program_id(0) == 0)
        def _():
            db_ref[...] += jnp.sum(dp.astype(F32), axis=0, keepdims=True)

    return pl.pallas_call(
        body, name=f"in_proj_bwd_w_{p}", grid=(D_MODEL // rc, n // tm),
        in_specs=[pl.BlockSpec((tm, rc), lambda r, i: (i, r)), pl.BlockSpec((tm, p), lambda r, i: (i, 0))],
        out_specs=[pl.BlockSpec((rc, p), lambda r, i: (r, 0)), pl.BlockSpec((1, p), lambda r, i: (0, 0))],
        out_shape=[jax.ShapeDtypeStruct((D_MODEL, p), F32), jax.ShapeDtypeStruct((1, p), F32)],
        compiler_params=_cparams(2),
    )(h, dproj)


PAIRS = GROUP // 2
GROUP_ROWS = PAIRS * ATTN_BLOCK
MASKED = -1e30


def _kv_windows(k_ref, v_ref, i):
    ps = pl.multiple_of(jnp.maximum(i - 1, 0) * ATTN_BLOCK, ATTN_BLOCK)
    cs = pl.multiple_of(i * ATTN_BLOCK, ATTN_BLOCK)
    kw = jnp.concatenate([k_ref[pl.ds(ps, ATTN_BLOCK), :], k_ref[pl.ds(cs, ATTN_BLOCK), :]], axis=0)
    vw = jnp.concatenate([v_ref[pl.ds(ps, ATTN_BLOCK), :], v_ref[pl.ds(cs, ATTN_BLOCK), :]], axis=0)
    return kw.astype(F32), vw.astype(F32), ps, cs


def _low_lanes(shape):
    return lax.broadcasted_iota(jnp.int32, shape, 1) < HEAD_DIM


def _spread(w, kvh):
    low = _low_lanes(w.shape)
    swapped = pltpu.roll(w, HEAD_DIM, 1)
    if kvh == 0:
        return jnp.where(low, w, 0.0), jnp.where(low, 0.0, swapped)
    return jnp.where(low, swapped, 0.0), jnp.where(low, 0.0, w)


def _unspread(d_a, d_b, kvh):
    low = _low_lanes(d_a.shape)
    if kvh == 0:
        return jnp.where(low, d_a + pltpu.roll(d_b, HEAD_DIM, 1), 0.0)
    return jnp.where(low, 0.0, pltpu.roll(d_a, HEAD_DIM, 1) + d_b)


def _stack_pairs(ref, kvh):
    return jnp.concatenate([ref[:, (kvh * PAIRS + j) * LANES:(kvh * PAIRS + j + 1) * LANES] for j in range(PAIRS)],
                           axis=0)


def _fill_bias(bias_scr):
    shape = (GROUP_ROWS, 2 * ATTN_BLOCK)
    r = lax.broadcasted_iota(jnp.int32, shape, 0) % ATTN_BLOCK
    c = lax.broadcasted_iota(jnp.int32, shape, 1)
    in_cur = (c >= ATTN_BLOCK) & ((c - ATTN_BLOCK) <= r)
    in_prev = (c < ATTN_BLOCK) & (c > r)
    bias_scr[0] = jnp.where(in_cur, 0.0, MASKED)
    bias_scr[1] = jnp.where(in_cur | in_prev, 0.0, MASKED)


def _sink_table(sinks):
    t = jnp.transpose(sinks.reshape(N_KV_HEADS, PAIRS, 2), (0, 2, 1))
    return jnp.broadcast_to(t[:, :, :, None, None], (N_KV_HEADS, 2, PAIRS, ATTN_BLOCK, LANES)).reshape(
        N_KV_HEADS, 2, GROUP_ROWS, LANES)


def attn_fwd(q, k, v, z, sink_tab, batch, seq):
    nb = seq // ATTN_BLOCK

    def body(q_ref, k_ref, v_ref, z_ref, s_ref, og_ref, bias_scr):
        b, i = pl.program_id(0), pl.program_id(1)

        @pl.when((b == 0) & (i == 0))
        def _():
            _fill_bias(bias_scr)

        kw, vw, _, _ = _kv_windows(k_ref, v_ref, i)
        bias = bias_scr[jnp.minimum(i, 1)]
        for kvh in range(N_KV_HEADS):
            k_a, k_b = _spread(kw, kvh)
            v_a, v_b = _spread(vw, kvh)
            og = _attn_group(_stack_pairs(q_ref, kvh), k_a, v_a, k_b, v_b, _stack_pairs(z_ref, kvh),
                             s_ref[kvh, 0], s_ref[kvh, 1], bias)
            for j in range(PAIRS):
                og_ref[:, (kvh * PAIRS + j) * LANES:(kvh * PAIRS + j + 1) * LANES] = (
                    og[j * ATTN_BLOCK:(j + 1) * ATTN_BLOCK].astype(BF16))

    blk = lambda w: pl.BlockSpec((ATTN_BLOCK, w), lambda b, i: (b * nb + i, 0))
    seq_spec = pl.BlockSpec((seq, KV_WIDTH), lambda b, i: (b, 0))
    return pl.pallas_call(
        body, name="attn_fwd", grid=(batch, nb),
        in_specs=[blk(D_MODEL), seq_spec, seq_spec, blk(D_MODEL), _full_spec(sink_tab.shape)],
        out_specs=blk(D_MODEL),
        out_shape=jax.ShapeDtypeStruct((batch * seq, D_MODEL), BF16),
        scratch_shapes=[pltpu.VMEM((2, GROUP_ROWS, 2 * ATTN_BLOCK), F32)],
        compiler_params=_cparams(2),
    )(q, k, v, z, sink_tab)


def attn_bwd(q, k, v, z, sink_tab, dog, tables, batch, seq):
    nb = seq // ATTN_BLOCK

    def body(q_ref, k_ref, v_ref, z_ref, s_ref, g_ref, c_ref, sa_ref, sb_ref, dp_ref, dk_ref, dv_ref, ds_ref,
             bias_scr):
        b, i = pl.program_id(0), pl.program_id(1)

        @pl.when((b == 0) & (i == 0))
        def _():
            _fill_bias(bias_scr)
            ds_ref[...] = jnp.zeros_like(ds_ref)

        @pl.when(i == 0)
        def _():
            dk_ref[...] = jnp.zeros_like(dk_ref)
            dv_ref[...] = jnp.zeros_like(dv_ref)

        kw, vw, ps, cs = _kv_windows(k_ref, v_ref, i)
        bias = bias_scr[jnp.minimum(i, 1)]
        tabs = (c_ref[...], sa_ref[...], sb_ref[...])
        dkw = jnp.zeros_like(kw)
        dvw = jnp.zeros_like(vw)
        for kvh in range(N_KV_HEADS):
            k_a, k_b = _spread(kw, kvh)
            v_a, v_b = _spread(vw, kvh)
            _, vjp = jax.vjp(functools.partial(_attn_group, bias=bias), _stack_pairs(q_ref, kvh).astype(F32),
                             k_a, v_a, k_b, v_b, _stack_pairs(z_ref, kvh), s_ref[kvh, 0], s_ref[kvh, 1])
            dqs, dk_a, dv_a, dk_b, dv_b, dzs, ds_a, ds_b = vjp(_stack_pairs(g_ref, kvh))
            dkw = dkw + _unspread(dk_a, dk_b, kvh)
            dvw = dvw + _unspread(dv_a, dv_b, kvh)
            ds_ref[kvh, 0] += jnp.sum(ds_a.reshape(PAIRS, ATTN_BLOCK, LANES), axis=1)
            ds_ref[kvh, 1] += jnp.sum(ds_b.reshape(PAIRS, ATTN_BLOCK, LANES), axis=1)
            for j in range(PAIRS):
                rows = slice(j * ATTN_BLOCK, (j + 1) * ATTN_BLOCK)
                col = (kvh * PAIRS + j) * LANES
                dp_ref[:, col:col + LANES] = _rope_transposed(dqs[rows] * (HEAD_DIM ** -0.5), *tabs).astype(BF16)
                zc = D_MODEL + 2 * KV_WIDTH + col
                dp_ref[:, zc:zc + LANES] = dzs[rows].astype(BF16)
        dp_ref[:, D_MODEL:D_MODEL + 2 * KV_WIDTH] = jnp.zeros((ATTN_BLOCK, 2 * KV_WIDTH), BF16)
        dk_ref[pl.ds(ps, ATTN_BLOCK), :] += dkw[:ATTN_BLOCK]
        dk_ref[pl.ds(cs, ATTN_BLOCK), :] += dkw[ATTN_BLOCK:]
        dv_ref[pl.ds(ps, ATTN_BLOCK), :] += dvw[:ATTN_BLOCK]
        dv_ref[pl.ds(cs, ATTN_BLOCK), :] += dvw[ATTN_BLOCK:]

    blk = lambda w: pl.BlockSpec((ATTN_BLOCK, w), lambda b, i: (b * nb + i, 0))
    seq_spec = pl.BlockSpec((seq, KV_WIDTH), lambda b, i: (b, 0))
    n = batch * seq
    ds_shape = (N_KV_HEADS, 2, PAIRS, LANES)
    return pl.pallas_call(
        body, name="attn_bwd", grid=(batch, nb),
        in_specs=[blk(D_MODEL), seq_spec, seq_spec, blk(D_MODEL), _full_spec(sink_tab.shape), blk(D_MODEL)]
        + [blk(LANES)] * 3,
        out_specs=[blk(ATTN_IN), seq_spec, seq_spec, _full_spec(ds_shape)],
        out_shape=[jax.ShapeDtypeStruct((n, ATTN_IN), BF16), jax.ShapeDtypeStruct((n, KV_WIDTH), F32),
                   jax.ShapeDtypeStruct((n, KV_WIDTH), F32), jax.ShapeDtypeStruct(ds_shape, F32)],
        scratch_shapes=[pltpu.VMEM((2, GROUP_ROWS, 2 * ATTN_BLOCK), F32)],
        compiler_params=_cparams(2),
    )(q, k, v, z, sink_tab, dog, *tables)


def attn_bwd_kv(dproj, dk, dv, tables, tm=512):
    n = dproj.shape[0]

    def body(dp_in_ref, dk_ref, dv_ref, c_ref, sa_ref, sb_ref, dp_ref):
        del dp_in_ref
        dp_ref[:, :KV_WIDTH] = _rope_transposed(dk_ref[...], c_ref[...], sa_ref[...], sb_ref[...]).astype(BF16)
        dp_ref[:, KV_WIDTH:] = dv_ref[...].astype(BF16)

    kv_cols = pl.BlockSpec((tm, 2 * KV_WIDTH), lambda i: (i, D_MODEL // (2 * KV_WIDTH)))
    return pl.pallas_call(
        body, name="attn_bwd_kv", grid=(n // tm,),
        in_specs=[kv_cols, _row_spec(tm, KV_WIDTH), _row_spec(tm, KV_WIDTH)] + [_row_spec(tm, LANES)] * 3,
        out_specs=kv_cols, out_shape=jax.ShapeDtypeStruct(dproj.shape, BF16),
        input_output_aliases={0: 0}, compiler_params=_cparams(1),
    )(dproj, dk, dv, *tables)


def _rec_cols(part, h):
    return slice(part * D_MODEL + h * REC_DIM, part * D_MODEL + (h + 1) * REC_DIM)


def _rec_args(p_ref, lb_ref, gw_ref, h, S):
    hs = slice(h * REC_DIM, (h + 1) * REC_DIM)
    return (p_ref[:, _rec_cols(0, h)], p_ref[:, _rec_cols(1, h)], p_ref[:, _rec_cols(2, h)],
            p_ref[:, _rec_cols(3, h)], S, lb_ref[0:1, hs], lb_ref[1:2, hs], gw_ref[...])


def rec_fwd(proj, lb_logits, gnorm_w, batch, seq):
    nblk = seq // REC_BLOCK

    def body(p_ref, lb_ref, gw_ref, og_ref, st_ref, s_scr):
        @pl.when(pl.program_id(1) == 0)
        def _():
            s_scr[...] = jnp.zeros_like(s_scr)

        for h in range(REC_HEADS):
            S = s_scr[h]
            st_ref[0, h] = S
            og, S_new = _rec_head(*_rec_args(p_ref, lb_ref, gw_ref, h, S))
            og_ref[:, h * REC_DIM:(h + 1) * REC_DIM] = og.astype(BF16)
            s_scr[h] = S_new

    blk = lambda w: pl.BlockSpec((REC_BLOCK, w), lambda b, j: (b * nblk + j, 0))
    st_spec = pl.BlockSpec((1, REC_HEADS, REC_DIM, REC_DIM), lambda b, j: (b * nblk + j, 0, 0, 0))
    return pl.pallas_call(
        body, name="rec_fwd", grid=(batch, nblk),
        in_specs=[blk(REC_IN), _full_spec((2, D_MODEL)), _full_spec((1, REC_DIM))],
        out_specs=[blk(D_MODEL), st_spec],
        out_shape=[jax.ShapeDtypeStruct((batch * seq, D_MODEL), BF16),
                   jax.ShapeDtypeStruct((batch * nblk, REC_HEADS, REC_DIM, REC_DIM), F32)],
        scratch_shapes=[pltpu.VMEM((REC_HEADS, REC_DIM, REC_DIM), F32)],
        compiler_params=_cparams(2),
    )(proj, lb_logits, gnorm_w)


def rec_bwd(proj, states, lb_logits, gnorm_w, dog, batch, seq):
    nblk = seq // REC_BLOCK

    def body(p_ref, st_ref, lb_ref, gw_ref, g_ref, dp_ref, dlb_ref, dgw_ref, ds_scr):
        @pl.when((pl.program_id(0) == 0) & (pl.program_id(1) == 0))
        def _():
            dlb_ref[...] = jnp.zeros_like(dlb_ref)
            dgw_ref[...] = jnp.zeros_like(dgw_ref)

        @pl.when(pl.program_id(1) == 0)
        def _():
            ds_scr[...] = jnp.zeros_like(ds_scr)

        for h in range(REC_HEADS):
            hs = slice(h * REC_DIM, (h + 1) * REC_DIM)
            _, vjp = jax.vjp(_rec_head, *_rec_args(p_ref, lb_ref, gw_ref, h, st_ref[0, h]))
            dqr, dfr, dv, dz, dS, dl0, dl1, dgw = vjp((g_ref[:, hs], ds_scr[h]))
            for part, val in enumerate((dqr, dfr, dv, dz)):
                dp_ref[:, _rec_cols(part, h)] = val.astype(BF16)
            ds_scr[h] = dS
            dlb_ref[0:1, hs] += dl0
            dlb_ref[1:2, hs] += dl1
            dgw_ref[...] += dgw

    blk = lambda w: pl.BlockSpec((REC_BLOCK, w), lambda b, j: (b * nblk + nblk - 1 - j, 0))
    st_spec = pl.BlockSpec((1, REC_HEADS, REC_DIM, REC_DIM), lambda b, j: (b * nblk + nblk - 1 - j, 0, 0, 0))
    return pl.pallas_call(
        body, name="rec_bwd", grid=(batch, nblk),
        in_specs=[blk(REC_IN), st_spec, _full_spec((2, D_MODEL)), _full_spec((1, REC_DIM)), blk(D_MODEL)],
        out_specs=[blk(REC_IN), _full_spec((2, D_MODEL)), _full_spec((1, REC_DIM))],
        out_shape=[jax.ShapeDtypeStruct((batch * seq, REC_IN), BF16), jax.ShapeDtypeStruct((2, D_MODEL), F32),
                   jax.ShapeDtypeStruct((1, REC_DIM), F32)],
        scratch_shapes=[pltpu.VMEM((REC_HEADS, REC_DIM, REC_DIM), F32)],
        compiler_params=_cparams(2),
    )(proj, states, lb_logits, gnorm_w, dog)


_ANY = pl.BlockSpec(memory_space=pl.ANY)


def _chip_peers():
    x, y, c = lax.axis_index("x"), lax.axis_index("y"), lax.axis_index("c")
    flips = ((1, 0), (0, 1), (1, 1))
    peers = [(jnp.where(fx, 1 - x, x), jnp.where(fy, 1 - y, y), c) for fx, fy in flips]
    return 2 * x + y, peers


def chip_exchange(arrays, scatter):
    n = len(arrays)

    def body(*refs):
        ins, outs = refs[:n], refs[n:2 * n]
        send_sems, recv_sems, local_sems = refs[2 * n:]
        me, peers = _chip_peers()
        local = [pltpu.make_async_copy(ins[k].at[me] if scatter else ins[k], outs[k].at[me], local_sems.at[k])
                 for k in range(n)]
        for cp in local:
            cp.start()
        sends, recvs = [], []
        for k in range(n):
            for j, (px, py, pc) in enumerate(peers):
                idx = 2 * px + py
                sem = k * len(peers) + j
                sends.append(pltpu.make_async_remote_copy(
                    src_ref=ins[k].at[idx] if scatter else ins[k], dst_ref=outs[k].at[me],
                    send_sem=send_sems.at[sem], recv_sem=recv_sems.at[sem],
                    device_id=(px, py, pc), device_id_type=MESH))
                recvs.append(pltpu.make_async_remote_copy(
                    src_ref=ins[k].at[me] if scatter else ins[k], dst_ref=outs[k].at[idx],
                    send_sem=send_sems.at[sem], recv_sem=recv_sems.at[sem],
                    device_id=(px, py, pc), device_id_type=MESH))
        for cp in sends:
            cp.start()
        for cp in recvs:
            cp.wait_recv()
        for cp in sends:
            cp.wait_send()
        for cp in local:
            cp.wait()

    out_shape = [jax.ShapeDtypeStruct(a.shape if scatter else (N_CHIPS,) + a.shape, a.dtype) for a in arrays]
    n_copies = n * (N_CHIPS - 1)
    return pl.pallas_call(
        body, name="chip_scatter" if scatter else "chip_gather",
        in_specs=[_ANY] * n, out_specs=[_ANY] * n, out_shape=out_shape,
        scratch_shapes=[pltpu.SemaphoreType.DMA((n_copies,)), pltpu.SemaphoreType.DMA((n_copies,)),
                        pltpu.SemaphoreType.DMA((n,))],
    )(*arrays)


def sibling_exchange(arrays):
    n = len(arrays)

    def body(*refs):
        ins, outs = refs[:n], refs[n:2 * n]
        send_sems, recv_sems = refs[2 * n:]
        sibling = (lax.axis_index("x"), lax.axis_index("y"), 1 - lax.axis_index("c"))
        copies = [pltpu.make_async_remote_copy(src_ref=ins[k], dst_ref=outs[k], send_sem=send_sems.at[k],
                                               recv_sem=recv_sems.at[k], device_id=sibling, device_id_type=MESH)
                  for k in range(n)]
        for cp in copies:
            cp.start()
        for cp in copies:
            cp.wait()

    return pl.pallas_call(
        body, name="sibling_exchange", in_specs=[_ANY] * n, out_specs=[_ANY] * n,
        out_shape=[jax.ShapeDtypeStruct(a.shape, a.dtype) for a in arrays],
        scratch_shapes=[pltpu.SemaphoreType.DMA((n,)), pltpu.SemaphoreType.DMA((n,))],
    )(*arrays)


def all_gather_small(vec):
    def body(v_ref, out_ref, send_sems, recv_sems, local_sem):
        x, y, c = lax.axis_index("x"), lax.axis_index("y"), lax.axis_index("c")
        me = 4 * x + 2 * y + c
        local = pltpu.make_async_copy(v_ref, out_ref.at[me], local_sem)
        local.start()
        sends, recvs = [], []
        for j in range(1, N_DEV):
            px = jnp.where(j & 4, 1 - x, x)
            py = jnp.where(j & 2, 1 - y, y)
            pc = jnp.where(j & 1, 1 - c, c)
            common = dict(send_sem=send_sems.at[j - 1], recv_sem=recv_sems.at[j - 1], device_id=(px, py, pc),
                          device_id_type=MESH)
            sends.append(pltpu.make_async_remote_copy(src_ref=v_ref, dst_ref=out_ref.at[me], **common))
            recvs.append(pltpu.make_async_remote_copy(src_ref=v_ref, dst_ref=out_ref.at[4 * px + 2 * py + pc],
                                                      **common))
        for cp in sends:
            cp.start()
        for cp in recvs:
            cp.wait_recv()
        for cp in sends:
            cp.wait_send()
        local.wait()

    return pl.pallas_call(
        body, name="all_gather_small", in_specs=[_ANY], out_specs=_ANY,
        out_shape=jax.ShapeDtypeStruct((N_DEV,) + vec.shape, vec.dtype),
        scratch_shapes=[pltpu.SemaphoreType.DMA((N_DEV - 1,)), pltpu.SemaphoreType.DMA((N_DEV - 1,)),
                        pltpu.SemaphoreType.DMA],
    )(vec)


def sum_slots(stacked, tm=256):
    s, r, c = stacked.shape
    tm = min(tm, r)

    def body(in_ref, out_ref):
        acc = in_ref[0].astype(F32)
        for t in range(1, s):
            acc = acc + in_ref[t].astype(F32)
        out_ref[...] = acc

    return pl.pallas_call(
        body, name=f"sum_slots_{s}_{r}_{c}", grid=(r // tm,),
        in_specs=[pl.BlockSpec((s, tm, c), lambda i: (0, i, 0))], out_specs=_row_spec(tm, c),
        out_shape=jax.ShapeDtypeStruct((r, c), F32), compiler_params=_cparams(1),
    )(stacked)


def adamw(w, m, v, g_a, g_b=None, tm=256):
    r, c = w.shape
    tm = min(tm, r)
    two = g_b is not None

    def body(*refs):
        w_ref, m_ref, v_ref, ga_ref = refs[:4]
        g_ref, d_ref, nm_ref, nv_ref = refs[-4:]
        g = ga_ref[...] + refs[4][...] if two else ga_ref[...]
        nm = ADAM_B1 * m_ref[...] + (1.0 - ADAM_B1) * g
        nv = ADAM_B2 * v_ref[...] + (1.0 - ADAM_B2) * (g * g)
        m_hat = nm / (1.0 - ADAM_B1 ** ADAM_STEP)
        v_hat = nv / (1.0 - ADAM_B2 ** ADAM_STEP)
        g_ref[...] = g
        d_ref[...] = -ADAM_LR * (m_hat / (jnp.sqrt(v_hat) + ADAM_EPS) + ADAM_WD * w_ref[...])
        nm_ref[...] = nm
        nv_ref[...] = nv

    args = [w, m, v, g_a] + ([g_b] if two else [])
    return pl.pallas_call(
        body, name=f"adamw_{r}_{c}", grid=(r // tm,),
        in_specs=[_row_spec(tm, c)] * len(args), out_specs=[_row_spec(tm, c)] * 4,
        out_shape=[jax.ShapeDtypeStruct((r, c), F32)] * 4, compiler_params=_cparams(1),
    )(*args)


_SMALL = (("pre_norm_w", (2, D_MODEL)), ("post_norm_w", (2, D_MODEL)), ("attn_b_in", (1, ATTN_IN)),
          ("attn_sinks", (1, N_HEADS)), ("attn_b_out", (1, D_MODEL)), ("rec_lb_logits", (2, D_MODEL)),
          ("rec_gnorm_w", (1, REC_DIM)))
_SMALL_ROWS = 16


def _pack_small(parts):
    rows = []
    for (name, shape) in _SMALL:
        flat = parts[name].reshape(-1)
        pad = -flat.shape[0] % D_MODEL
        rows.append(jnp.pad(flat, (0, pad)).reshape(-1, D_MODEL))
    packed = jnp.concatenate(rows, axis=0)
    return jnp.pad(packed, ((0, _SMALL_ROWS - packed.shape[0]), (0, 0)))


def _unpack_small(packed):
    out, row = {}, 0
    for (name, shape) in _SMALL:
        size = shape[0] * shape[1]
        nrows = -(-size // D_MODEL)
        out[name] = packed[row:row + nrows].reshape(-1)[:size].reshape(shape)
        row += nrows
    return out


def local_step(x, positions, pre_norm_w, post_norm_w, attn_w_in, attn_b_in, attn_sinks, attn_w_out, attn_b_out,
               rec_w_in, rec_lb_logits, rec_gnorm_w, rec_w_out, loss_target):
    batch, seq, _ = x.shape
    n = batch * seq
    x0 = x.reshape(n, D_MODEL)
    tables = _rope_tables(positions)
    pre0, pre1 = pre_norm_w[0:1], pre_norm_w[1:2]
    post0, post1 = post_norm_w[0:1], post_norm_w[1:2]
    no_bias = jnp.zeros((1, D_MODEL), F32)

    h0, q, k, v, z = attn_in_proj(x0, pre0, attn_w_in, attn_b_in, tables)
    sink_tab = _sink_table(attn_sinks)
    og0 = attn_fwd(q, k, v, z, sink_tab, batch, seq)
    y0, x1 = out_proj(og0, attn_w_out, attn_b_out, x0, post0)

    h1, proj1 = rec_in_proj(x1, pre1, rec_w_in)
    og1, states = rec_fwd(proj1, rec_lb_logits, rec_gnorm_w, batch, seq)
    y1, dx2, loss_vec = out_proj(og1, rec_w_out, no_bias, x1, post1, target=loss_target.reshape(n, D_MODEL))
    loss = jnp.sum(loss_vec) * (0.5 / D_MODEL)

    dog1, d_rec_w_out, _, d_post1 = out_proj_bwd(dx2, y1, og1, rec_w_out, post1)
    dproj1, d_lb, d_gnorm = rec_bwd(proj1, states, rec_lb_logits, rec_gnorm_w, dog1, batch, seq)
    dx1, d_pre1 = in_proj_bwd_x(dproj1, rec_w_in, x1, pre1, dx2)
    d_rec_w_in, _ = in_proj_bwd_w(h1, dproj1)

    dog0, d_attn_w_out, d_attn_b_out, d_post0 = out_proj_bwd(dx1, y0, og0, attn_w_out, post0)
    dproj0, dk, dv, d_sink_tab = attn_bwd(q, k, v, z, sink_tab, dog0, tables, batch, seq)
    d_sinks = jnp.transpose(jnp.sum(d_sink_tab, axis=-1), (0, 2, 1)).reshape(1, N_HEADS)
    dproj0 = attn_bwd_kv(dproj0, dk, dv, tables)
    dx0, d_pre0 = in_proj_bwd_x(dproj0, attn_w_in, x0, pre0, dx1)
    d_attn_w_in, d_attn_b_in = in_proj_bwd_w(h0, dproj0)

    grads = dict(
        pre_norm_w=jnp.concatenate([d_pre0, d_pre1], axis=0), post_norm_w=jnp.concatenate([d_post0, d_post1], axis=0),
        attn_w_in=d_attn_w_in, attn_b_in=d_attn_b_in, attn_sinks=d_sinks, attn_w_out=d_attn_w_out,
        attn_b_out=d_attn_b_out, rec_w_in=d_rec_w_in, rec_lb_logits=d_lb, rec_gnorm_w=d_gnorm,
        rec_w_out=d_rec_w_out)
    return loss, dx0.reshape(batch, seq, D_MODEL), grads


_BIG = ("attn_w_in", "attn_w_out", "rec_w_in", "rec_w_out")
_COLUMN_SHARDED = ("attn_w_in", "rec_w_in")
_ORDER = ("pre_norm_w", "post_norm_w", "attn_w_in", "attn_b_in", "attn_sinks", "attn_w_out", "attn_b_out",
          "rec_w_in", "rec_lb_logits", "rec_gnorm_w", "rec_w_out")


def _whole_from_shards(name, stacked):
    if name in _COLUMN_SHARDED:
        return jnp.transpose(stacked, (1, 0, 2)).reshape(stacked.shape[1], -1)
    return stacked.reshape(-1, stacked.shape[2])


def _shards_from_whole(name, whole):
    if name in _COLUMN_SHARDED:
        return jnp.transpose(whole.reshape(whole.shape[0], N_CHIPS, -1), (1, 0, 2))
    return whole.reshape(N_CHIPS, -1, whole.shape[1])


def kernel(x, positions, pre_norm_w, post_norm_w, attn_w_in, attn_b_in, attn_sinks, attn_w_out, attn_b_out, rec_w_in, rec_lb_logits, rec_gnorm_w, rec_w_out, loss_target, m_pre_norm_w, m_post_norm_w, m_attn_w_in, m_attn_b_in, m_attn_sinks, m_attn_w_out, m_attn_b_out, m_rec_w_in, m_rec_lb_logits, m_rec_gnorm_w, m_rec_w_out, v_pre_norm_w, v_post_norm_w, v_attn_w_in, v_attn_b_in, v_attn_sinks, v_attn_w_out, v_attn_b_out, v_rec_w_in, v_rec_lb_logits, v_rec_gnorm_w, v_rec_w_out):
    w = dict(pre_norm_w=pre_norm_w, post_norm_w=post_norm_w, attn_w_in=attn_w_in, attn_b_in=attn_b_in,
             attn_sinks=attn_sinks, attn_w_out=attn_w_out, attn_b_out=attn_b_out, rec_w_in=rec_w_in,
             rec_lb_logits=rec_lb_logits, rec_gnorm_w=rec_gnorm_w, rec_w_out=rec_w_out)
    m = dict(pre_norm_w=m_pre_norm_w, post_norm_w=m_post_norm_w, attn_w_in=m_attn_w_in, attn_b_in=m_attn_b_in,
             attn_sinks=m_attn_sinks, attn_w_out=m_attn_w_out, attn_b_out=m_attn_b_out, rec_w_in=m_rec_w_in,
             rec_lb_logits=m_rec_lb_logits, rec_gnorm_w=m_rec_gnorm_w, rec_w_out=m_rec_w_out)
    v = dict(pre_norm_w=v_pre_norm_w, post_norm_w=v_post_norm_w, attn_w_in=v_attn_w_in, attn_b_in=v_attn_b_in,
             attn_sinks=v_attn_sinks, attn_w_out=v_attn_w_out, attn_b_out=v_attn_b_out, rec_w_in=v_rec_w_in,
             rec_lb_logits=v_rec_lb_logits, rec_gnorm_w=v_rec_gnorm_w, rec_w_out=v_rec_w_out)

    shards = {name: w[name][0] for name in _BIG}
    gathered = chip_exchange([shards[name].astype(BF16) for name in _BIG], scatter=False)
    whole = {name: _whole_from_shards(name, g) for name, g in zip(_BIG, gathered)}

    loss, grad_x, grads = local_step(
        x, positions, pre_norm_w, post_norm_w, whole["attn_w_in"], attn_b_in, attn_sinks, whole["attn_w_out"],
        attn_b_out, whole["rec_w_in"], rec_lb_logits, rec_gnorm_w, whole["rec_w_out"], loss_target)
    loss = lax.psum(loss, ("x", "y", "c"))

    parts = chip_exchange([_shards_from_whole(name, grads[name]).astype(BF16) for name in _BIG], scatter=True)
    plane_sums = [sum_slots(p) for p in parts]
    other_sums = sibling_exchange(plane_sums)
    out_g, out_d, out_m, out_v = {}, {}, {}, {}
    for name, mine, other in zip(_BIG, plane_sums, other_sums):
        g, d, nm, nv = adamw(shards[name], m[name][0], v[name][0], mine, other)
        out_g[name], out_d[name], out_m[name], out_v[name] = g[None], d[None], nm[None], nv[None]

    small_sum = sum_slots(all_gather_small(_pack_small(grads)))
    packed = adamw(_pack_small(w), _pack_small(m), _pack_small(v), small_sum)
    for dst, val in zip((out_g, out_d, out_m, out_v), packed):
        dst.update(_unpack_small(val))

    return (loss, grad_x, *[out_g[n] for n in _ORDER], *[out_d[n] for n in _ORDER],
            *[out_m[n] for n in _ORDER], *[out_v[n] for n in _ORDER])
```

```python
import functools

import jax
import jax.numpy as jnp
from jax import lax
from jax.experimental import pallas as pl
from jax.experimental.pallas import tpu as pltpu

F32 = jnp.float32
BF16 = jnp.bfloat16
MESH = pl.DeviceIdType.MESH

D_MODEL = 1024
HEAD_DIM = 64
N_HEADS = 16
N_KV_HEADS = 2
GROUP = N_HEADS // N_KV_HEADS
KV_WIDTH = N_KV_HEADS * HEAD_DIM
ATTN_IN = 2 * D_MODEL + 2 * KV_WIDTH
ATTN_BLOCK = 128
ROPE_THETA = 500000.0
ROPE_DIM = HEAD_DIM // 4
REC_HEADS = 8
REC_DIM = 128
REC_IN = 4 * D_MODEL
REC_BLOCK = 128
DIAG = 8
NORM_EPS = 1e-6
N_CHIPS = 4
N_DEV = 8
LANES = 128

ADAM_LR = 0.001
ADAM_B1 = 0.9
ADAM_B2 = 0.999
ADAM_EPS = 1e-08
ADAM_WD = 0.01
ADAM_STEP = 10

VMEM_LIMIT = 56 * 1024 * 1024


def _cparams(n_axes):
    return pltpu.CompilerParams(dimension_semantics=("arbitrary",) * n_axes, vmem_limit_bytes=VMEM_LIMIT)


def _dot(a, b, contract):
    return lax.dot_general(a.astype(BF16), b.astype(BF16), (contract, ((), ())), preferred_element_type=F32)


_NN = ((1,), (0,))
_NT = ((1,), (1,))
_TN = ((0,), (0,))


@jax.custom_vjp
def mm_nn(a, b):
    return _dot(a, b, _NN)


mm_nn.defvjp(lambda a, b: (_dot(a, b, _NN), (a, b)),
             lambda res, g: (_dot(g, res[1], _NT), _dot(res[0], g, _TN)))


@jax.custom_vjp
def mm_nt(a, b):
    return _dot(a, b, _NT)


mm_nt.defvjp(lambda a, b: (_dot(a, b, _NT), (a, b)),
             lambda res, g: (_dot(g, res[1], _NN), _dot(g, res[0], _TN)))


@jax.custom_vjp
def mm_tn(a, b):
    return _dot(a, b, _TN)


mm_tn.defvjp(lambda a, b: (_dot(a, b, _TN), (a, b)),
             lambda res, g: (_dot(res[1], g, _NT), _dot(res[0], g, _NN)))


def _tri_dot(x, lower):
    n = x.shape[0]
    r = lax.broadcasted_iota(jnp.int32, (n, n), 0)
    c = lax.broadcasted_iota(jnp.int32, (n, n), 1)
    tri = ((c <= r) if lower else (c >= r)).astype(BF16)
    hi = x.astype(BF16)
    rest = x - hi.astype(F32)
    mid = rest.astype(BF16)
    lo = (rest - mid.astype(F32)).astype(BF16)
    dot = lambda p: lax.dot_general(tri, p, (_NN, ((), ())), preferred_element_type=F32)
    return (dot(lo) + dot(mid)) + dot(hi)


@jax.custom_vjp
def cumsum_rows(x):
    return _tri_dot(x, True)


cumsum_rows.defvjp(lambda x: (cumsum_rows(x), None), lambda _, g: (_tri_dot(g, False),))


@functools.partial(jax.custom_vjp, nondiff_argnums=(1,))
def roll_sub(x, d):
    return pltpu.roll(x, d, 1) if d else x


roll_sub.defvjp(lambda x, d: (roll_sub(x, d), None),
                lambda d, _, g: (roll_sub(g, (DIAG - d) % DIAG),))


def sigmoid(x):
    return 0.5 * (jnp.tanh(0.5 * x) + 1.0)


def log_sigmoid_pair(x):
    t = jnp.log(1.0 + jnp.exp(-jnp.abs(x)))
    return jnp.minimum(x, 0.0) - t, jnp.minimum(-x, 0.0) - t


def _rms(x):
    return lax.rsqrt(jnp.mean(x * x, axis=-1, keepdims=True) + NORM_EPS)


def _attn_group(qs, k_a, v_a, k_b, v_b, zs, sink_a, sink_b, bias):
    def half(kh, vh, sink):
        s = mm_nt(qs, kh) + bias
        m = lax.stop_gradient(jnp.maximum(jnp.max(s, axis=-1, keepdims=True), jnp.max(sink, axis=-1, keepdims=True)))
        p = jnp.exp(s - m)
        denom = jnp.sum(p, axis=-1, keepdims=True) + jnp.sum(jnp.exp(sink - m), axis=-1, keepdims=True) * (1.0 / LANES)
        return mm_nn(p * (1.0 / denom), vh)

    return (half(k_a, v_a, sink_a) + half(k_b, v_b, sink_b)) * (zs * sigmoid(zs))


SAFE_RANGE = 80.0


def _row(x, r):
    rows = lax.broadcasted_iota(jnp.int32, x.shape, 0)
    return jnp.sum(jnp.where(rows == r, x, 0.0), axis=0, keepdims=True)


def _rec_front(qr, fr, l0, l1):
    q = qr * sigmoid(qr)
    log_lb, log_1m_lb = log_sigmoid_pair(l1 - l0)
    ls_f, ls_nf = log_sigmoid_pair(fr)
    c = log_1m_lb + ls_f
    lf = jnp.maximum(log_lb, c) + jnp.log(1.0 + jnp.exp(-jnp.abs(log_lb - c)))
    return q, jnp.exp(log_1m_lb + ls_nf), lf


def _rec_tail(o, z, gw):
    return o * _rms(o) * gw * (z * sigmoid(z))


def _rec_is_safe(b):
    R = b.shape[0]
    mid, last = _row(b, R // 2 - 1), _row(b, R - 1)
    return jnp.min(jnp.minimum(mid, last - mid)) >= -SAFE_RANGE


def _rec_core_fast(q, k, v, b, S):
    R = q.shape[0]
    ri = lax.broadcasted_iota(jnp.int32, (R, R), 0)
    ci = lax.broadcasted_iota(jnp.int32, (R, R), 1)
    d = b - _row(b, R // 2 - 1)
    sc = jnp.where(ci <= ri, mm_nt(q * jnp.exp(d), k * jnp.exp(-d)), 0.0)
    o = mm_nt(q * jnp.exp(b), S) + mm_nn(sc, v)
    b_last = _row(b, R - 1)
    return o, S * jnp.exp(b_last) + mm_tn(v, k * jnp.exp(b_last - b))


def _rec_core_slow(q, k, v, b, S):
    R = q.shape[0]
    rows = lax.broadcasted_iota(jnp.int32, (R, REC_DIM), 0)

    o = mm_nt(q * jnp.exp(jnp.minimum(b, 0.0)), S)

    ri = lax.broadcasted_iota(jnp.int32, (R, R), 0)
    ci = lax.broadcasted_iota(jnp.int32, (R, R), 1)
    sc = jnp.zeros((R, R), F32)
    w = R
    while w > DIAG:
        h = w // 2
        b3 = b.reshape(R // w, w, REC_DIM)
        rin = lax.broadcasted_iota(jnp.int32, (R // w, w, REC_DIM), 1)
        mid = jnp.sum(jnp.where(rin == h - 1, b3, 0.0), axis=1, keepdims=True)
        fac = jnp.exp(jnp.minimum(jnp.where(rin >= h, b3 - mid, mid - b3), 0.0)).reshape(R, REC_DIM)
        upper = (rows % w) >= h
        s_w = mm_nt(jnp.where(upper, q * fac, 0.0), jnp.where(upper, 0.0, k * fac))
        sc = sc + jnp.where((ri // w) == (ci // w), s_w, 0.0)
        w = h
    o = o + mm_nn(sc, v)

    g = R // DIAG
    q3, k3, v3, b3 = (t.reshape(g, DIAG, REC_DIM) for t in (q, k, v, b))
    rin = lax.broadcasted_iota(jnp.int32, (g, DIAG, 1), 1)
    od = jnp.zeros((g, DIAG, REC_DIM), F32)
    for d in range(DIAG):
        e = jnp.exp(jnp.minimum(b3 - roll_sub(b3, d), 0.0))
        sd = jnp.sum(q3 * roll_sub(k3, d) * e, axis=-1, keepdims=True)
        od = od + jnp.where(rin >= d, sd, 0.0) * roll_sub(v3, d)
    o = o + od.reshape(R, REC_DIM)

    b_last = _row(b, R - 1)
    return o, S * jnp.exp(jnp.minimum(b_last, 0.0)) + mm_tn(v, k * jnp.exp(jnp.minimum(b_last - b, 0.0)))


def _rec_head(core, qr, fr, v, z, S, l0, l1, gw):
    q, k, lf = _rec_front(qr, fr, l0, l1)
    o, S_new = core(q, k, v, cumsum_rows(lf), S)
    return _rec_tail(o, z, gw), S_new


def _rope_tables(positions):
    half = ROPE_DIM // 2
    inv_freq = ROPE_THETA ** (-(jnp.arange(half, dtype=F32) * 2.0 / ROPE_DIM))
    ang = positions.astype(F32).reshape(-1, 1) * inv_freq
    cos, sin = jnp.cos(ang), jnp.sin(ang)
    n = ang.shape[0]
    rest = HEAD_DIM - ROPE_DIM
    one, zero, zh = jnp.ones((n, rest), F32), jnp.zeros((n, rest), F32), jnp.zeros((n, half), F32)
    cos_t = jnp.concatenate([cos, cos, one], axis=-1)
    sin_a = jnp.concatenate([zh, sin, zero], axis=-1)
    sin_b = jnp.concatenate([-sin, zh, zero], axis=-1)
    return tuple(jnp.concatenate([t, t], axis=-1) for t in (cos_t, sin_a, sin_b))


def _rope(x, cos_t, sin_a, sin_b):
    half = ROPE_DIM // 2
    return x * cos_t + pltpu.roll(x, half, 1) * sin_a + pltpu.roll(x, LANES - half, 1) * sin_b


def _rope_transposed(g, cos_t, sin_a, sin_b):
    half = ROPE_DIM // 2
    return g * cos_t + pltpu.roll(g * sin_a, LANES - half, 1) + pltpu.roll(g * sin_b, half, 1)


def _row_spec(tm, width):
    return pl.BlockSpec((tm, width), lambda i: (i, 0))


def _full_spec(shape):
    return pl.BlockSpec(shape, lambda *_: (0,) * len(shape))


def attn_in_proj(x, w_pre, w_in, b_in, tables, tm=512):
    n = x.shape[0]

    def body(x_ref, wp_ref, w_ref, b_ref, c_ref, sa_ref, sb_ref, h_ref, q_ref, k_ref, v_ref, z_ref):
        xv = x_ref[...]
        h = (xv * _rms(xv) * wp_ref[...]).astype(BF16)
        h_ref[...] = h
        proj = jnp.dot(h, w_ref[...], preferred_element_type=F32) + b_ref[...]
        tabs = (c_ref[...], sa_ref[...], sb_ref[...])
        for s in range(D_MODEL // LANES):
            sl = slice(s * LANES, (s + 1) * LANES)
            q_ref[:, sl] = _rope(proj[:, sl] * (HEAD_DIM ** -0.5), *tabs).astype(BF16)
        k_ref[...] = _rope(proj[:, D_MODEL:D_MODEL + KV_WIDTH], *tabs).astype(BF16)
        v_ref[...] = proj[:, D_MODEL + KV_WIDTH:D_MODEL + 2 * KV_WIDTH].astype(BF16)
        z_ref[...] = proj[:, D_MODEL + 2 * KV_WIDTH:]

    return pl.pallas_call(
        body, name="attn_in_proj", grid=(n // tm,),
        in_specs=[_row_spec(tm, D_MODEL), _full_spec((1, D_MODEL)), _full_spec((D_MODEL, ATTN_IN)),
                  _full_spec((1, ATTN_IN))] + [_row_spec(tm, LANES)] * 3,
        out_specs=[_row_spec(tm, D_MODEL), _row_spec(tm, D_MODEL), _row_spec(tm, KV_WIDTH),
                   _row_spec(tm, KV_WIDTH), _row_spec(tm, D_MODEL)],
        out_shape=[jax.ShapeDtypeStruct((n, D_MODEL), BF16), jax.ShapeDtypeStruct((n, D_MODEL), BF16),
                   jax.ShapeDtypeStruct((n, KV_WIDTH), BF16), jax.ShapeDtypeStruct((n, KV_WIDTH), BF16),
                   jax.ShapeDtypeStruct((n, D_MODEL), F32)],
        compiler_params=_cparams(1),
    )(x, w_pre, w_in, b_in, *tables)


def rec_in_proj(x, w_pre, w_in, tm=256):
    n = x.shape[0]

    def body(x_ref, wp_ref, w_ref, h_ref, p_ref):
        xv = x_ref[...]
        h = (xv * _rms(xv) * wp_ref[...]).astype(BF16)
        h_ref[...] = h
        p_ref[...] = jnp.dot(h, w_ref[...], preferred_element_type=F32)

    return pl.pallas_call(
        body, name="rec_in_proj", grid=(n // tm,),
        in_specs=[_row_spec(tm, D_MODEL), _full_spec((1, D_MODEL)), _full_spec((D_MODEL, REC_IN))],
        out_specs=[_row_spec(tm, D_MODEL), _row_spec(tm, REC_IN)],
        out_shape=[jax.ShapeDtypeStruct((n, D_MODEL), BF16), jax.ShapeDtypeStruct((n, REC_IN), F32)],
        compiler_params=_cparams(1),
    )(x, w_pre, w_in)


def out_proj(og, w_out, b_out, x_res, w_post, target=None, tm=512):
    n = og.shape[0]
    with_loss = target is not None

    def body(*refs):
        if with_loss:
            og_ref, w_ref, b_ref, x_ref, wp_ref, t_ref, y_ref, dx_ref, l_ref = refs
        else:
            og_ref, w_ref, b_ref, x_ref, wp_ref, y_ref, xo_ref = refs
        y = jnp.dot(og_ref[...], w_ref[...], preferred_element_type=F32) + b_ref[...]
        y_ref[...] = y
        xo = x_ref[...] + y * _rms(y) * wp_ref[...]
        if with_loss:
            err = xo - t_ref[...]
            dx_ref[...] = err * (1.0 / D_MODEL)

            @pl.when(pl.program_id(0) == 0)
            def _():
                l_ref[...] = jnp.zeros_like(l_ref)

            l_ref[...] += jnp.sum(err * err, axis=0, keepdims=True)
        else:
            xo_ref[...] = xo

    in_specs = [_row_spec(tm, D_MODEL), _full_spec((D_MODEL, D_MODEL)), _full_spec((1, D_MODEL)),
                _row_spec(tm, D_MODEL), _full_spec((1, D_MODEL))]
    out_specs = [_row_spec(tm, D_MODEL), _row_spec(tm, D_MODEL)]
    out_shape = [jax.ShapeDtypeStruct((n, D_MODEL), F32), jax.ShapeDtypeStruct((n, D_MODEL), F32)]
    args = [og, w_out, b_out, x_res, w_post]
    if with_loss:
        in_specs.append(_row_spec(tm, D_MODEL))
        out_specs.append(_full_spec((1, D_MODEL)))
        out_shape.append(jax.ShapeDtypeStruct((1, D_MODEL), F32))
        args.append(target)
    return pl.pallas_call(
        body, name="out_proj_loss" if with_loss else "out_proj", grid=(n // tm,),
        in_specs=in_specs, out_specs=out_specs, out_shape=out_shape, compiler_params=_cparams(1),
    )(*args)


def out_proj_bwd(dxo, y, og, w_out, w_post, tm=512):
    n = og.shape[0]

    def body(g_ref, y_ref, og_ref, w_ref, wp_ref, dog_ref, dw_ref, db_ref, dwp_ref):
        @pl.when(pl.program_id(0) == 0)
        def _():
            dw_ref[...] = jnp.zeros_like(dw_ref)
            db_ref[...] = jnp.zeros_like(db_ref)
            dwp_ref[...] = jnp.zeros_like(dwp_ref)

        g, y = g_ref[...], y_ref[...]
        rstd = _rms(y)
        yn = y * rstd
        gw = g * wp_ref[...]
        dwp_ref[...] += jnp.sum(g * yn, axis=0, keepdims=True)
        dy = rstd * (gw - yn * jnp.mean(gw * yn, axis=-1, keepdims=True))
        db_ref[...] += jnp.sum(dy, axis=0, keepdims=True)
        dyb = dy.astype(BF16)
        dog_ref[...] = _dot(dyb, w_ref[...], _NT)
        dw_ref[...] += _dot(og_ref[...], dyb, _TN)

    return pl.pallas_call(
        body, name="out_proj_bwd", grid=(n // tm,),
        in_specs=[_row_spec(tm, D_MODEL), _row_spec(tm, D_MODEL), _row_spec(tm, D_MODEL),
                  _full_spec((D_MODEL, D_MODEL)), _full_spec((1, D_MODEL))],
        out_specs=[_row_spec(tm, D_MODEL), _full_spec((D_MODEL, D_MODEL)), _full_spec((1, D_MODEL)),
                   _full_spec((1, D_MODEL))],
        out_shape=[jax.ShapeDtypeStruct((n, D_MODEL), F32), jax.ShapeDtypeStruct((D_MODEL, D_MODEL), F32),
                   jax.ShapeDtypeStruct((1, D_MODEL), F32), jax.ShapeDtypeStruct((1, D_MODEL), F32)],
        compiler_params=_cparams(1),
    )(dxo, y, og, w_out, w_post)


def in_proj_bwd_x(dproj, w_in, x, w_pre, dxo, tm=512):
    n, p = dproj.shape

    def body(dp_ref, w_ref, x_ref, wp_ref, g_ref, dx_ref, dwp_ref):
        @pl.when(pl.program_id(0) == 0)
        def _():
            dwp_ref[...] = jnp.zeros_like(dwp_ref)

        dh = _dot(dp_ref[...], w_ref[...], _NT)
        xv = x_ref[...]
        rstd = _rms(xv)
        xn = xv * rstd
        gw = dh * wp_ref[...]
        dwp_ref[...] += jnp.sum(dh * xn, axis=0, keepdims=True)
        dx_ref[...] = rstd * (gw - xn * jnp.mean(gw * xn, axis=-1, keepdims=True)) + g_ref[...]

    return pl.pallas_call(
        body, name=f"in_proj_bwd_x_{p}", grid=(n // tm,),
        in_specs=[_row_spec(tm, p), _full_spec((D_MODEL, p)), _row_spec(tm, D_MODEL), _full_spec((1, D_MODEL)),
                  _row_spec(tm, D_MODEL)],
        out_specs=[_row_spec(tm, D_MODEL), _full_spec((1, D_MODEL))],
        out_shape=[jax.ShapeDtypeStruct((n, D_MODEL), F32), jax.ShapeDtypeStruct((1, D_MODEL), F32)],
        compiler_params=_cparams(1),
    )(dproj, w_in, x, w_pre, dxo)


def in_proj_bwd_w(h, dproj, tm=512, rc=256):
    n, p = dproj.shape

    def body(h_ref, dp_ref, dw_ref, db_ref):
        first_tile = pl.program_id(1) == 0

        @pl.when(first_tile)
        def _():
            dw_ref[...] = jnp.zeros_like(dw_ref)

        @pl.when(first_tile & (pl.program_id(0) == 0))
        def _():
            db_ref[...] = jnp.zeros_like(db_ref)

        dp = dp_ref[...]
        dw_ref[...] += _dot(h_ref[...], dp, _TN)

        @pl.when(pl.program_id(0) == 0)
        def _():
            db_ref[...] += jnp.sum(dp.astype(F32), axis=0, keepdims=True)

    return pl.pallas_call(
        body, name=f"in_proj_bwd_w_{p}", grid=(D_MODEL // rc, n // tm),
        in_specs=[pl.BlockSpec((tm, rc), lambda r, i: (i, r)), pl.BlockSpec((tm, p), lambda r, i: (i, 0))],
        out_specs=[pl.BlockSpec((rc, p), lambda r, i: (r, 0)), pl.BlockSpec((1, p), lambda r, i: (0, 0))],
        out_shape=[jax.ShapeDtypeStruct((D_MODEL, p), F32), jax.ShapeDtypeStruct((1, p), F32)],
        compiler_params=_cparams(2),
    )(h, dproj)


PAIRS = GROUP // 2
GROUP_ROWS = PAIRS * ATTN_BLOCK
MASKED = -1e30


def _kv_windows(k_ref, v_ref, i):
    ps = pl.multiple_of(jnp.maximum(i - 1, 0) * ATTN_BLOCK, ATTN_BLOCK)
    cs = pl.multiple_of(i * ATTN_BLOCK, ATTN_BLOCK)
    kw = jnp.concatenate([k_ref[pl.ds(ps, ATTN_BLOCK), :], k_ref[pl.ds(cs, ATTN_BLOCK), :]], axis=0)
    vw = jnp.concatenate([v_ref[pl.ds(ps, ATTN_BLOCK), :], v_ref[pl.ds(cs, ATTN_BLOCK), :]], axis=0)
    return kw.astype(F32), vw.astype(F32), ps, cs


def _low_lanes(shape):
    return lax.broadcasted_iota(jnp.int32, shape, 1) < HEAD_DIM


def _spread(w, kvh):
    low = _low_lanes(w.shape)
    swapped = pltpu.roll(w, HEAD_DIM, 1)
    if kvh == 0:
        return jnp.where(low, w, 0.0), jnp.where(low, 0.0, swapped)
    return jnp.where(low, swapped, 0.0), jnp.where(low, 0.0, w)


def _unspread(d_a, d_b, kvh):
    low = _low_lanes(d_a.shape)
    if kvh == 0:
        return jnp.where(low, d_a + pltpu.roll(d_b, HEAD_DIM, 1), 0.0)
    return jnp.where(low, 0.0, pltpu.roll(d_a, HEAD_DIM, 1) + d_b)


def _stack_pairs(ref, kvh):
    return jnp.concatenate([ref[:, (kvh * PAIRS + j) * LANES:(kvh * PAIRS + j + 1) * LANES] for j in range(PAIRS)],
                           axis=0)


def _fill_bias(bias_scr):
    shape = (GROUP_ROWS, 2 * ATTN_BLOCK)
    r = lax.broadcasted_iota(jnp.int32, shape, 0) % ATTN_BLOCK
    c = lax.broadcasted_iota(jnp.int32, shape, 1)
    in_cur = (c >= ATTN_BLOCK) & ((c - ATTN_BLOCK) <= r)
    in_prev = (c < ATTN_BLOCK) & (c > r)
    bias_scr[0] = jnp.where(in_cur, 0.0, MASKED)
    bias_scr[1] = jnp.where(in_cur | in_prev, 0.0, MASKED)


def _sink_table(sinks):
    t = jnp.transpose(sinks.reshape(N_KV_HEADS, PAIRS, 2), (0, 2, 1))
    return jnp.broadcast_to(t[:, :, :, None, None], (N_KV_HEADS, 2, PAIRS, ATTN_BLOCK, LANES)).reshape(
        N_KV_HEADS, 2, GROUP_ROWS, LANES)


def attn_fwd(q, k, v, z, sink_tab, batch, seq):
    nb = seq // ATTN_BLOCK

    def body(q_ref, k_ref, v_ref, z_ref, s_ref, og_ref, bias_scr):
        b, i = pl.program_id(0), pl.program_id(1)

        @pl.when((b == 0) & (i == 0))
        def _():
            _fill_bias(bias_scr)

        kw, vw, _, _ = _kv_windows(k_ref, v_ref, i)
        bias = bias_scr[jnp.minimum(i, 1)]
        for kvh in range(N_KV_HEADS):
            k_a, k_b = _spread(kw, kvh)
            v_a, v_b = _spread(vw, kvh)
            og = _attn_group(_stack_pairs(q_ref, kvh), k_a, v_a, k_b, v_b, _stack_pairs(z_ref, kvh),
                             s_ref[kvh, 0], s_ref[kvh, 1], bias)
            for j in range(PAIRS):
                og_ref[:, (kvh * PAIRS + j) * LANES:(kvh * PAIRS + j + 1) * LANES] = (
                    og[j * ATTN_BLOCK:(j + 1) * ATTN_BLOCK].astype(BF16))

    blk = lambda w: pl.BlockSpec((ATTN_BLOCK, w), lambda b, i: (b * nb + i, 0))
    seq_spec = pl.BlockSpec((seq, KV_WIDTH), lambda b, i: (b, 0))
    return pl.pallas_call(
        body, name="attn_fwd", grid=(batch, nb),
        in_specs=[blk(D_MODEL), seq_spec, seq_spec, blk(D_MODEL), _full_spec(sink_tab.shape)],
        out_specs=blk(D_MODEL),
        out_shape=jax.ShapeDtypeStruct((batch * seq, D_MODEL), BF16),
        scratch_shapes=[pltpu.VMEM((2, GROUP_ROWS, 2 * ATTN_BLOCK), F32)],
        compiler_params=_cparams(2),
    )(q, k, v, z, sink_tab)


def attn_bwd(q, k, v, z, sink_tab, dog, tables, batch, seq):
    nb = seq // ATTN_BLOCK

    def body(q_ref, k_ref, v_ref, z_ref, s_ref, g_ref, c_ref, sa_ref, sb_ref, dp_ref, dk_ref, dv_ref, ds_ref,
             bias_scr):
        b, i = pl.program_id(0), pl.program_id(1)

        @pl.when((b == 0) & (i == 0))
        def _():
            _fill_bias(bias_scr)
            ds_ref[...] = jnp.zeros_like(ds_ref)

        @pl.when(i == 0)
        def _():
            dk_ref[...] = jnp.zeros_like(dk_ref)
            dv_ref[...] = jnp.zeros_like(dv_ref)

        kw, vw, ps, cs = _kv_windows(k_ref, v_ref, i)
        bias = bias_scr[jnp.minimum(i, 1)]
        tabs = (c_ref[...], sa_ref[...], sb_ref[...])
        dkw = jnp.zeros_like(kw)
        dvw = jnp.zeros_like(vw)
        for kvh in range(N_KV_HEADS):
            k_a, k_b = _spread(kw, kvh)
            v_a, v_b = _spread(vw, kvh)
            _, vjp = jax.vjp(functools.partial(_attn_group, bias=bias), _stack_pairs(q_ref, kvh).astype(F32),
                             k_a, v_a, k_b, v_b, _stack_pairs(z_ref, kvh), s_ref[kvh, 0], s_ref[kvh, 1])
            dqs, dk_a, dv_a, dk_b, dv_b, dzs, ds_a, ds_b = vjp(_stack_pairs(g_ref, kvh))
            dkw = dkw + _unspread(dk_a, dk_b, kvh)
            dvw = dvw + _unspread(dv_a, dv_b, kvh)
            ds_ref[kvh, 0] += jnp.sum(ds_a.reshape(PAIRS, ATTN_BLOCK, LANES), axis=1)
            ds_ref[kvh, 1] += jnp.sum(ds_b.reshape(PAIRS, ATTN_BLOCK, LANES), axis=1)
            for j in range(PAIRS):
                rows = slice(j * ATTN_BLOCK, (j + 1) * ATTN_BLOCK)
                col = (kvh * PAIRS + j) * LANES
                dp_ref[:, col:col + LANES] = _rope_transposed(dqs[rows] * (HEAD_DIM ** -0.5), *tabs).astype(BF16)
                zc = D_MODEL + 2 * KV_WIDTH + col
                dp_ref[:, zc:zc + LANES] = dzs[rows].astype(BF16)
        dp_ref[:, D_MODEL:D_MODEL + 2 * KV_WIDTH] = jnp.zeros((ATTN_BLOCK, 2 * KV_WIDTH), BF16)
        dk_ref[pl.ds(ps, ATTN_BLOCK), :] += dkw[:ATTN_BLOCK]
        dk_ref[pl.ds(cs, ATTN_BLOCK), :] += dkw[ATTN_BLOCK:]
        dv_ref[pl.ds(ps, ATTN_BLOCK), :] += dvw[:ATTN_BLOCK]
        dv_ref[pl.ds(cs, ATTN_BLOCK), :] += dvw[ATTN_BLOCK:]

    blk = lambda w: pl.BlockSpec((ATTN_BLOCK, w), lambda b, i: (b * nb + i, 0))
    seq_spec = pl.BlockSpec((seq, KV_WIDTH), lambda b, i: (b, 0))
    n = batch * seq
    ds_shape = (N_KV_HEADS, 2, PAIRS, LANES)
    return pl.pallas_call(
        body, name="attn_bwd", grid=(batch, nb),
        in_specs=[blk(D_MODEL), seq_spec, seq_spec, blk(D_MODEL), _full_spec(sink_tab.shape), blk(D_MODEL)]
        + [blk(LANES)] * 3,
        out_specs=[blk(ATTN_IN), seq_spec, seq_spec, _full_spec(ds_shape)],
        out_shape=[jax.ShapeDtypeStruct((n, ATTN_IN), BF16), jax.ShapeDtypeStruct((n, KV_WIDTH), F32),
                   jax.ShapeDtypeStruct((n, KV_WIDTH), F32), jax.ShapeDtypeStruct(ds_shape, F32)],
        scratch_shapes=[pltpu.VMEM((2, GROUP_ROWS, 2 * ATTN_BLOCK), F32)],
        compiler_params=_cparams(2),
    )(q, k, v, z, sink_tab, dog, *tables)


def attn_bwd_kv(dproj, dk, dv, tables, tm=512):
    n = dproj.shape[0]

    def body(dp_in_ref, dk_ref, dv_ref, c_ref, sa_ref, sb_ref, dp_ref):
        del dp_in_ref
        dp_ref[:, :KV_WIDTH] = _rope_transposed(dk_ref[...], c_ref[...], sa_ref[...], sb_ref[...]).astype(BF16)
        dp_ref[:, KV_WIDTH:] = dv_ref[...].astype(BF16)

    kv_cols = pl.BlockSpec((tm, 2 * KV_WIDTH), lambda i: (i, D_MODEL // (2 * KV_WIDTH)))
    return pl.pallas_call(
        body, name="attn_bwd_kv", grid=(n // tm,),
        in_specs=[kv_cols, _row_spec(tm, KV_WIDTH), _row_spec(tm, KV_WIDTH)] + [_row_spec(tm, LANES)] * 3,
        out_specs=kv_cols, out_shape=jax.ShapeDtypeStruct(dproj.shape, BF16),
        input_output_aliases={0: 0}, compiler_params=_cparams(1),
    )(dproj, dk, dv, *tables)


def _rec_cols(part, h):
    return slice(part * D_MODEL + h * REC_DIM, part * D_MODEL + (h + 1) * REC_DIM)


def _rec_args(p_ref, lb_ref, gw_ref, h, S):
    hs = slice(h * REC_DIM, (h + 1) * REC_DIM)
    return (p_ref[:, _rec_cols(0, h)], p_ref[:, _rec_cols(1, h)], p_ref[:, _rec_cols(2, h)],
            p_ref[:, _rec_cols(3, h)], S, lb_ref[0:1, hs], lb_ref[1:2, hs], gw_ref[...])


def rec_fwd(proj, lb_logits, gnorm_w, batch, seq):
    nblk = seq // REC_BLOCK

    def body(p_ref, lb_ref, gw_ref, og_ref, st_ref, safe_ref, s_scr):
        @pl.when(pl.program_id(1) == 0)
        def _():
            s_scr[...] = jnp.zeros_like(s_scr)

        for h in range(REC_HEADS):
            S = s_scr[h]
            st_ref[0, h] = S
            qr, fr, v, z, _, l0, l1, gw = _rec_args(p_ref, lb_ref, gw_ref, h, S)
            q, k, lf = _rec_front(qr, fr, l0, l1)
            b = cumsum_rows(lf)
            safe = _rec_is_safe(b)
            o, S_new = lax.cond(safe, _rec_core_fast, _rec_core_slow, q, k, v, b, S)
            og_ref[:, h * REC_DIM:(h + 1) * REC_DIM] = _rec_tail(o, z, gw).astype(BF16)
            s_scr[h] = S_new
            safe_ref[0, h:h + 1, :] = jnp.full((1, LANES), safe.astype(F32))

    blk = lambda w: pl.BlockSpec((REC_BLOCK, w), lambda b, j: (b * nblk + j, 0))
    st_spec = pl.BlockSpec((1, REC_HEADS, REC_DIM, REC_DIM), lambda b, j: (b * nblk + j, 0, 0, 0))
    safe_spec = pl.BlockSpec((1, REC_HEADS, LANES), lambda b, j: (b * nblk + j, 0, 0))
    return pl.pallas_call(
        body, name="rec_fwd", grid=(batch, nblk),
        in_specs=[blk(REC_IN), _full_spec((2, D_MODEL)), _full_spec((1, REC_DIM))],
        out_specs=[blk(D_MODEL), st_spec, safe_spec],
        out_shape=[jax.ShapeDtypeStruct((batch * seq, D_MODEL), BF16),
                   jax.ShapeDtypeStruct((batch * nblk, REC_HEADS, REC_DIM, REC_DIM), F32),
                   jax.ShapeDtypeStruct((batch * nblk, REC_HEADS, LANES), F32)],
        scratch_shapes=[pltpu.VMEM((REC_HEADS, REC_DIM, REC_DIM), F32)],
        compiler_params=_cparams(2),
    )(proj, lb_logits, gnorm_w)


def rec_bwd(proj, states, safe, lb_logits, gnorm_w, dog, batch, seq):
    nblk = seq // REC_BLOCK

    def body(p_ref, st_ref, safe_ref, lb_ref, gw_ref, g_ref, dp_ref, dlb_ref, dgw_ref, ds_scr):
        @pl.when((pl.program_id(0) == 0) & (pl.program_id(1) == 0))
        def _():
            dlb_ref[...] = jnp.zeros_like(dlb_ref)
            dgw_ref[...] = jnp.zeros_like(dgw_ref)

        @pl.when(pl.program_id(1) == 0)
        def _():
            ds_scr[...] = jnp.zeros_like(ds_scr)

        for h in range(REC_HEADS):
            hs = slice(h * REC_DIM, (h + 1) * REC_DIM)
            primals = _rec_args(p_ref, lb_ref, gw_ref, h, st_ref[0, h])
            cotangents = (g_ref[:, hs], ds_scr[h])

            def pull_back(core, primals=primals, cotangents=cotangents):
                return jax.vjp(functools.partial(_rec_head, core), *primals)[1](cotangents)

            dqr, dfr, dv, dz, dS, dl0, dl1, dgw = lax.cond(
                jnp.max(safe_ref[0, h:h + 1, :]) > 0.5,
                functools.partial(pull_back, _rec_core_fast), functools.partial(pull_back, _rec_core_slow))
            for part, val in enumerate((dqr, dfr, dv, dz)):
                dp_ref[:, _rec_cols(part, h)] = val.astype(BF16)
            ds_scr[h] = dS
            dlb_ref[0:1, hs] += dl0
            dlb_ref[1:2, hs] += dl1
            dgw_ref[...] += dgw

    blk = lambda w: pl.BlockSpec((REC_BLOCK, w), lambda b, j: (b * nblk + nblk - 1 - j, 0))
    st_spec = pl.BlockSpec((1, REC_HEADS, REC_DIM, REC_DIM), lambda b, j: (b * nblk + nblk - 1 - j, 0, 0, 0))
    safe_spec = pl.BlockSpec((1, REC_HEADS, LANES), lambda b, j: (b * nblk + nblk - 1 - j, 0, 0))
    return pl.pallas_call(
        body, name="rec_bwd", grid=(batch, nblk),
        in_specs=[blk(REC_IN), st_spec, safe_spec, _full_spec((2, D_MODEL)), _full_spec((1, REC_DIM)),
                  blk(D_MODEL)],
        out_specs=[blk(REC_IN), _full_spec((2, D_MODEL)), _full_spec((1, REC_DIM))],
        out_shape=[jax.ShapeDtypeStruct((batch * seq, REC_IN), BF16), jax.ShapeDtypeStruct((2, D_MODEL), F32),
                   jax.ShapeDtypeStruct((1, REC_DIM), F32)],
        scratch_shapes=[pltpu.VMEM((REC_HEADS, REC_DIM, REC_DIM), F32)],
        compiler_params=_cparams(2),
    )(proj, states, safe, lb_logits, gnorm_w, dog)


_ANY = pl.BlockSpec(memory_space=pl.ANY)


def _chip_peers():
    x, y, c = lax.axis_index("x"), lax.axis_index("y"), lax.axis_index("c")
    flips = ((1, 0), (0, 1), (1, 1))
    peers = [(jnp.where(fx, 1 - x, x), jnp.where(fy, 1 - y, y), c) for fx, fy in flips]
    return 2 * x + y, peers


def chip_exchange(arrays, scatter):
    n = len(arrays)

    def body(*refs):
        ins, outs = refs[:n], refs[n:2 * n]
        send_sems, recv_sems, local_sems = refs[2 * n:]
        me, peers = _chip_peers()
        local = [pltpu.make_async_copy(ins[k].at[me] if scatter else ins[k], outs[k].at[me], local_sems.at[k])
                 for k in range(n)]
        for cp in local:
            cp.start()
        sends, recvs = [], []
        for k in range(n):
            for j, (px, py, pc) in enumerate(peers):
                idx = 2 * px + py
                sem = k * len(peers) + j
                sends.append(pltpu.make_async_remote_copy(
                    src_ref=ins[k].at[idx] if scatter else ins[k], dst_ref=outs[k].at[me],
                    send_sem=send_sems.at[sem], recv_sem=recv_sems.at[sem],
                    device_id=(px, py, pc), device_id_type=MESH))
                recvs.append(pltpu.make_async_remote_copy(
                    src_ref=ins[k].at[me] if scatter else ins[k], dst_ref=outs[k].at[idx],
                    send_sem=send_sems.at[sem], recv_sem=recv_sems.at[sem],
                    device_id=(px, py, pc), device_id_type=MESH))
        for cp in sends:
            cp.start()
        for cp in recvs:
            cp.wait_recv()
        for cp in sends:
            cp.wait_send()
        for cp in local:
            cp.wait()

    out_shape = [jax.ShapeDtypeStruct(a.shape if scatter else (N_CHIPS,) + a.shape, a.dtype) for a in arrays]
    n_copies = n * (N_CHIPS - 1)
    return pl.pallas_call(
        body, name="chip_scatter" if scatter else "chip_gather",
        in_specs=[_ANY] * n, out_specs=[_ANY] * n, out_shape=out_shape,
        scratch_shapes=[pltpu.SemaphoreType.DMA((n_copies,)), pltpu.SemaphoreType.DMA((n_copies,)),
                        pltpu.SemaphoreType.DMA((n,))],
    )(*arrays)


def sibling_exchange(arrays):
    n = len(arrays)

    def body(*refs):
        ins, outs = refs[:n], refs[n:2 * n]
        send_sems, recv_sems = refs[2 * n:]
        sibling = (lax.axis_index("x"), lax.axis_index("y"), 1 - lax.axis_index("c"))
        copies = [pltpu.make_async_remote_copy(src_ref=ins[k], dst_ref=outs[k], send_sem=send_sems.at[k],
                                               recv_sem=recv_sems.at[k], device_id=sibling, device_id_type=MESH)
                  for k in range(n)]
        for cp in copies:
            cp.start()
        for cp in copies:
            cp.wait()

    return pl.pallas_call(
        body, name="sibling_exchange", in_specs=[_ANY] * n, out_specs=[_ANY] * n,
        out_shape=[jax.ShapeDtypeStruct(a.shape, a.dtype) for a in arrays],
        scratch_shapes=[pltpu.SemaphoreType.DMA((n,)), pltpu.SemaphoreType.DMA((n,))],
    )(*arrays)


def all_gather_small(vec):
    def body(v_ref, out_ref, send_sems, recv_sems, local_sem):
        x, y, c = lax.axis_index("x"), lax.axis_index("y"), lax.axis_index("c")
        me = 4 * x + 2 * y + c
        local = pltpu.make_async_copy(v_ref, out_ref.at[me], local_sem)
        local.start()
        sends, recvs = [], []
        for j in range(1, N_DEV):
            px = jnp.where(j & 4, 1 - x, x)
            py = jnp.where(j & 2, 1 - y, y)
            pc = jnp.where(j & 1, 1 - c, c)
            common = dict(send_sem=send_sems.at[j - 1], recv_sem=recv_sems.at[j - 1], device_id=(px, py, pc),
                          device_id_type=MESH)
            sends.append(pltpu.make_async_remote_copy(src_ref=v_ref, dst_ref=out_ref.at[me], **common))
            recvs.append(pltpu.make_async_remote_copy(src_ref=v_ref, dst_ref=out_ref.at[4 * px + 2 * py + pc],
                                                      **common))
        for cp in sends:
            cp.start()
        for cp in recvs:
            cp.wait_recv()
        for cp in sends:
            cp.wait_send()
        local.wait()

    return pl.pallas_call(
        body, name="all_gather_small", in_specs=[_ANY], out_specs=_ANY,
        out_shape=jax.ShapeDtypeStruct((N_DEV,) + vec.shape, vec.dtype),
        scratch_shapes=[pltpu.SemaphoreType.DMA((N_DEV - 1,)), pltpu.SemaphoreType.DMA((N_DEV - 1,)),
                        pltpu.SemaphoreType.DMA],
    )(vec)


def sum_slots(stacked, tm=256):
    s, r, c = stacked.shape
    tm = min(tm, r)

    def body(in_ref, out_ref):
        acc = in_ref[0].astype(F32)
        for t in range(1, s):
            acc = acc + in_ref[t].astype(F32)
        out_ref[...] = acc

    return pl.pallas_call(
        body, name=f"sum_slots_{s}_{r}_{c}", grid=(r // tm,),
        in_specs=[pl.BlockSpec((s, tm, c), lambda i: (0, i, 0))], out_specs=_row_spec(tm, c),
        out_shape=jax.ShapeDtypeStruct((r, c), F32), compiler_params=_cparams(1),
    )(stacked)


def adamw(w, m, v, g_a, g_b=None, tm=256):
    r, c = w.shape
    tm = min(tm, r)
    two = g_b is not None

    def body(*refs):
        w_ref, m_ref, v_ref, ga_ref = refs[:4]
        g_ref, d_ref, nm_ref, nv_ref = refs[-4:]
        g = ga_ref[...] + refs[4][...] if two else ga_ref[...]
        nm = ADAM_B1 * m_ref[...] + (1.0 - ADAM_B1) * g
        nv = ADAM_B2 * v_ref[...] + (1.0 - ADAM_B2) * (g * g)
        m_hat = nm / (1.0 - ADAM_B1 ** ADAM_STEP)
        v_hat = nv / (1.0 - ADAM_B2 ** ADAM_STEP)
        g_ref[...] = g
        d_ref[...] = -ADAM_LR * (m_hat / (jnp.sqrt(v_hat) + ADAM_EPS) + ADAM_WD * w_ref[...])
        nm_ref[...] = nm
        nv_ref[...] = nv

    args = [w, m, v, g_a] + ([g_b] if two else [])
    return pl.pallas_call(
        body, name=f"adamw_{r}_{c}", grid=(r // tm,),
        in_specs=[_row_spec(tm, c)] * len(args), out_specs=[_row_spec(tm, c)] * 4,
        out_shape=[jax.ShapeDtypeStruct((r, c), F32)] * 4, compiler_params=_cparams(1),
    )(*args)


_SMALL = (("pre_norm_w", (2, D_MODEL)), ("post_norm_w", (2, D_MODEL)), ("attn_b_in", (1, ATTN_IN)),
          ("attn_sinks", (1, N_HEADS)), ("attn_b_out", (1, D_MODEL)), ("rec_lb_logits", (2, D_MODEL)),
          ("rec_gnorm_w", (1, REC_DIM)))
_SMALL_ROWS = 16


def _pack_small(parts):
    rows = []
    for (name, shape) in _SMALL:
        flat = parts[name].reshape(-1)
        pad = -flat.shape[0] % D_MODEL
        rows.append(jnp.pad(flat, (0, pad)).reshape(-1, D_MODEL))
    packed = jnp.concatenate(rows, axis=0)
    return jnp.pad(packed, ((0, _SMALL_ROWS - packed.shape[0]), (0, 0)))


def _unpack_small(packed):
    out, row = {}, 0
    for (name, shape) in _SMALL:
        size = shape[0] * shape[1]
        nrows = -(-size // D_MODEL)
        out[name] = packed[row:row + nrows].reshape(-1)[:size].reshape(shape)
        row += nrows
    return out


def local_step(x, positions, pre_norm_w, post_norm_w, attn_w_in, attn_b_in, attn_sinks, attn_w_out, attn_b_out,
               rec_w_in, rec_lb_logits, rec_gnorm_w, rec_w_out, loss_target):
    batch, seq, _ = x.shape
    n = batch * seq
    x0 = x.reshape(n, D_MODEL)
    tables = _rope_tables(positions)
    pre0, pre1 = pre_norm_w[0:1], pre_norm_w[1:2]
    post0, post1 = post_norm_w[0:1], post_norm_w[1:2]
    no_bias = jnp.zeros((1, D_MODEL), F32)

    h0, q, k, v, z = attn_in_proj(x0, pre0, attn_w_in, attn_b_in, tables)
    sink_tab = _sink_table(attn_sinks)
    og0 = attn_fwd(q, k, v, z, sink_tab, batch, seq)
    y0, x1 = out_proj(og0, attn_w_out, attn_b_out, x0, post0)

    h1, proj1 = rec_in_proj(x1, pre1, rec_w_in)
    og1, states, safe = rec_fwd(proj1, rec_lb_logits, rec_gnorm_w, batch, seq)
    y1, dx2, loss_vec = out_proj(og1, rec_w_out, no_bias, x1, post1, target=loss_target.reshape(n, D_MODEL))
    loss = jnp.sum(loss_vec) * (0.5 / D_MODEL)

    dog1, d_rec_w_out, _, d_post1 = out_proj_bwd(dx2, y1, og1, rec_w_out, post1)
    dproj1, d_lb, d_gnorm = rec_bwd(proj1, states, safe, rec_lb_logits, rec_gnorm_w, dog1, batch, seq)
    dx1, d_pre1 = in_proj_bwd_x(dproj1, rec_w_in, x1, pre1, dx2)
    d_rec_w_in, _ = in_proj_bwd_w(h1, dproj1)

    dog0, d_attn_w_out, d_attn_b_out, d_post0 = out_proj_bwd(dx1, y0, og0, attn_w_out, post0)
    dproj0, dk, dv, d_sink_tab = attn_bwd(q, k, v, z, sink_tab, dog0, tables, batch, seq)
    d_sinks = jnp.transpose(jnp.sum(d_sink_tab, axis=-1), (0, 2, 1)).reshape(1, N_HEADS)
    dproj0 = attn_bwd_kv(dproj0, dk, dv, tables)
    dx0, d_pre0 = in_proj_bwd_x(dproj0, attn_w_in, x0, pre0, dx1)
    d_attn_w_in, d_attn_b_in = in_proj_bwd_w(h0, dproj0)

    grads = dict(
        pre_norm_w=jnp.concatenate([d_pre0, d_pre1], axis=0), post_norm_w=jnp.concatenate([d_post0, d_post1], axis=0),
        attn_w_in=d_attn_w_in, attn_b_in=d_attn_b_in, attn_sinks=d_sinks, attn_w_out=d_attn_w_out,
        attn_b_out=d_attn_b_out, rec_w_in=d_rec_w_in, rec_lb_logits=d_lb, rec_gnorm_w=d_gnorm,
        rec_w_out=d_rec_w_out)
    return loss, dx0.reshape(batch, seq, D_MODEL), grads


_BIG = ("attn_w_in", "attn_w_out", "rec_w_in", "rec_w_out")
_COLUMN_SHARDED = ("attn_w_in", "rec_w_in")
_ORDER = ("pre_norm_w", "post_norm_w", "attn_w_in", "attn_b_in", "attn_sinks", "attn_w_out", "attn_b_out",
          "rec_w_in", "rec_lb_logits", "rec_gnorm_w", "rec_w_out")


def _whole_from_shards(name, stacked):
    if name in _COLUMN_SHARDED:
        return jnp.transpose(stacked, (1, 0, 2)).reshape(stacked.shape[1], -1)
    return stacked.reshape(-1, stacked.shape[2])


def _shards_from_whole(name, whole):
    if name in _COLUMN_SHARDED:
        return jnp.transpose(whole.reshape(whole.shape[0], N_CHIPS, -1), (1, 0, 2))
    return whole.reshape(N_CHIPS, -1, whole.shape[1])


def kernel(x, positions, pre_norm_w, post_norm_w, attn_w_in, attn_b_in, attn_sinks, attn_w_out, attn_b_out, rec_w_in, rec_lb_logits, rec_gnorm_w, rec_w_out, loss_target, m_pre_norm_w, m_post_norm_w, m_attn_w_in, m_attn_b_in, m_attn_sinks, m_attn_w_out, m_attn_b_out, m_rec_w_in, m_rec_lb_logits, m_rec_gnorm_w, m_rec_w_out, v_pre_norm_w, v_post_norm_w, v_attn_w_in, v_attn_b_in, v_attn_sinks, v_attn_w_out, v_attn_b_out, v_rec_w_in, v_rec_lb_logits, v_rec_gnorm_w, v_rec_w_out):
    w = dict(pre_norm_w=pre_norm_w, post_norm_w=post_norm_w, attn_w_in=attn_w_in, attn_b_in=attn_b_in,
             attn_sinks=attn_sinks, attn_w_out=attn_w_out, attn_b_out=attn_b_out, rec_w_in=rec_w_in,
             rec_lb_logits=rec_lb_logits, rec_gnorm_w=rec_gnorm_w, rec_w_out=rec_w_out)
    m = dict(pre_norm_w=m_pre_norm_w, post_norm_w=m_post_norm_w, attn_w_in=m_attn_w_in, attn_b_in=m_attn_b_in,
             attn_sinks=m_attn_sinks, attn_w_out=m_attn_w_out, attn_b_out=m_attn_b_out, rec_w_in=m_rec_w_in,
             rec_lb_logits=m_rec_lb_logits, rec_gnorm_w=m_rec_gnorm_w, rec_w_out=m_rec_w_out)
    v = dict(pre_norm_w=v_pre_norm_w, post_norm_w=v_post_norm_w, attn_w_in=v_attn_w_in, attn_b_in=v_attn_b_in,
             attn_sinks=v_attn_sinks, attn_w_out=v_attn_w_out, attn_b_out=v_attn_b_out, rec_w_in=v_rec_w_in,
             rec_lb_logits=v_rec_lb_logits, rec_gnorm_w=v_rec_gnorm_w, rec_w_out=v_rec_w_out)

    shards = {name: w[name][0] for name in _BIG}
    gathered = chip_exchange([shards[name].astype(BF16) for name in _BIG], scatter=False)
    whole = {name: _whole_from_shards(name, g) for name, g in zip(_BIG, gathered)}

    loss, grad_x, grads = local_step(
        x, positions, pre_norm_w, post_norm_w, whole["attn_w_in"], attn_b_in, attn_sinks, whole["attn_w_out"],
        attn_b_out, whole["rec_w_in"], rec_lb_logits, rec_gnorm_w, whole["rec_w_out"], loss_target)
    loss = lax.psum(loss, ("x", "y", "c"))

    parts = chip_exchange([_shards_from_whole(name, grads[name]).astype(BF16) for name in _BIG], scatter=True)
    plane_sums = [sum_slots(p) for p in parts]
    other_sums = sibling_exchange(plane_sums)
    out_g, out_d, out_m, out_v = {}, {}, {}, {}
    for name, mine, other in zip(_BIG, plane_sums, other_sums):
        g, d, nm, nv = adamw(shards[name], m[name][0], v[name][0], mine, other)
        out_g[name], out_d[name], out_m[name], out_v[name] = g[None], d[None], nm[None], nv[None]

    small_sum = sum_slots(all_gather_small(_pack_small(grads)))
    packed = adamw(_pack_small(w), _pack_small(m), _pack_small(v), small_sum)
    for dst, val in zip((out_g, out_d, out_m, out_v), packed):
        dst.update(_unpack_small(val))

    return (loss, grad_x, *[out_g[n] for n in _ORDER], *[out_d[n] for n in _ORDER],
            *[out_m[n] for n in _ORDER], *[out_v[n] for n in _ORDER])
```

```python
import functools

import jax
import jax.numpy as jnp
from jax import lax
from jax.experimental import pallas as pl
from jax.experimental.pallas import tpu as pltpu

F32 = jnp.float32
BF16 = jnp.bfloat16
MESH = pl.DeviceIdType.MESH

D_MODEL = 1024
HEAD_DIM = 64
N_HEADS = 16
N_KV_HEADS = 2
GROUP = N_HEADS // N_KV_HEADS
KV_WIDTH = N_KV_HEADS * HEAD_DIM
ATTN_IN = 2 * D_MODEL + 2 * KV_WIDTH
ATTN_BLOCK = 128
ROPE_THETA = 500000.0
ROPE_DIM = HEAD_DIM // 4
REC_HEADS = 8
REC_DIM = 128
REC_IN = 4 * D_MODEL
REC_BLOCK = 128
DIAG = 8
NORM_EPS = 1e-6
N_CHIPS = 4
N_DEV = 8
LANES = 128

ADAM_LR = 0.001
ADAM_B1 = 0.9
ADAM_B2 = 0.999
ADAM_EPS = 1e-08
ADAM_WD = 0.01
ADAM_STEP = 10

VMEM_LIMIT = 56 * 1024 * 1024


def _cparams(n_axes):
    return pltpu.CompilerParams(dimension_semantics=("arbitrary",) * n_axes, vmem_limit_bytes=VMEM_LIMIT)


def _dot(a, b, contract):
    return lax.dot_general(a.astype(BF16), b.astype(BF16), (contract, ((), ())), preferred_element_type=F32)


_NN = ((1,), (0,))
_NT = ((1,), (1,))
_TN = ((0,), (0,))


@jax.custom_vjp
def mm_nn(a, b):
    return _dot(a, b, _NN)


mm_nn.defvjp(lambda a, b: (_dot(a, b, _NN), (a, b)),
             lambda res, g: (_dot(g, res[1], _NT), _dot(res[0], g, _TN)))


@jax.custom_vjp
def mm_nt(a, b):
    return _dot(a, b, _NT)


mm_nt.defvjp(lambda a, b: (_dot(a, b, _NT), (a, b)),
             lambda res, g: (_dot(g, res[1], _NN), _dot(g, res[0], _TN)))


@jax.custom_vjp
def mm_tn(a, b):
    return _dot(a, b, _TN)


mm_tn.defvjp(lambda a, b: (_dot(a, b, _TN), (a, b)),
             lambda res, g: (_dot(res[1], g, _NT), _dot(res[0], g, _NN)))


def _tri_dot(x, lower):
    n = x.shape[0]
    r = lax.broadcasted_iota(jnp.int32, (n, n), 0)
    c = lax.broadcasted_iota(jnp.int32, (n, n), 1)
    tri = ((c <= r) if lower else (c >= r)).astype(BF16)
    hi = x.astype(BF16)
    rest = x - hi.astype(F32)
    mid = rest.astype(BF16)
    lo = (rest - mid.astype(F32)).astype(BF16)
    dot = lambda p: lax.dot_general(tri, p, (_NN, ((), ())), preferred_element_type=F32)
    return (dot(lo) + dot(mid)) + dot(hi)


@jax.custom_vjp
def cumsum_rows(x):
    return _tri_dot(x, True)


cumsum_rows.defvjp(lambda x: (cumsum_rows(x), None), lambda _, g: (_tri_dot(g, False),))


@functools.partial(jax.custom_vjp, nondiff_argnums=(1,))
def roll_sub(x, d):
    return pltpu.roll(x, d, 1) if d else x


roll_sub.defvjp(lambda x, d: (roll_sub(x, d), None),
                lambda d, _, g: (roll_sub(g, (DIAG - d) % DIAG),))


def sigmoid(x):
    return 0.5 * (jnp.tanh(0.5 * x) + 1.0)


def log_sigmoid_pair(x):
    t = jnp.log(1.0 + jnp.exp(-jnp.abs(x)))
    return jnp.minimum(x, 0.0) - t, jnp.minimum(-x, 0.0) - t


def _rms(x):
    return lax.rsqrt(jnp.mean(x * x, axis=-1, keepdims=True) + NORM_EPS)


def _attn_group(qs, k_a, v_a, k_b, v_b, zs, sink_a, sink_b, bias):
    def half(kh, vh, sink):
        s = mm_nt(qs, kh) + bias
        m = lax.stop_gradient(jnp.maximum(jnp.max(s, axis=-1, keepdims=True), jnp.max(sink, axis=-1, keepdims=True)))
        p = jnp.exp(s - m)
        denom = jnp.sum(p, axis=-1, keepdims=True) + jnp.sum(jnp.exp(sink - m), axis=-1, keepdims=True) * (1.0 / LANES)
        return mm_nn(p * (1.0 / denom), vh)

    return (half(k_a, v_a, sink_a) + half(k_b, v_b, sink_b)) * (zs * sigmoid(zs))


SAFE_RANGE = 80.0


def _row(x, r):
    rows = lax.broadcasted_iota(jnp.int32, x.shape, 0)
    return jnp.sum(jnp.where(rows == r, x, 0.0), axis=0, keepdims=True)


def _rec_front(qr, fr, l0, l1):
    q = qr * sigmoid(qr)
    log_lb, log_1m_lb = log_sigmoid_pair(l1 - l0)
    ls_f, ls_nf = log_sigmoid_pair(fr)
    c = log_1m_lb + ls_f
    lf = jnp.maximum(log_lb, c) + jnp.log(1.0 + jnp.exp(-jnp.abs(log_lb - c)))
    return q, jnp.exp(log_1m_lb + ls_nf), lf


def _rec_tail(o, z, gw):
    return o * _rms(o) * gw * (z * sigmoid(z))


def _rec_margin(b):
    R = b.shape[0]
    mid, last = _row(b, R // 2 - 1), _row(b, R - 1)
    return jnp.minimum(mid, last - mid)


def _rec_core_fast(q, k, v, b, S):
    R = q.shape[0]
    ri = lax.broadcasted_iota(jnp.int32, (R, R), 0)
    ci = lax.broadcasted_iota(jnp.int32, (R, R), 1)
    d = b - _row(b, R // 2 - 1)
    sc = jnp.where(ci <= ri, mm_nt(q * jnp.exp(d), k * jnp.exp(-d)), 0.0)
    o = mm_nt(q * jnp.exp(b), S) + mm_nn(sc, v)
    b_last = _row(b, R - 1)
    return o, S * jnp.exp(b_last) + mm_tn(v, k * jnp.exp(b_last - b))


def _rec_core_slow(q, k, v, b, S):
    R = q.shape[0]
    rows = lax.broadcasted_iota(jnp.int32, (R, REC_DIM), 0)

    o = mm_nt(q * jnp.exp(jnp.minimum(b, 0.0)), S)

    ri = lax.broadcasted_iota(jnp.int32, (R, R), 0)
    ci = lax.broadcasted_iota(jnp.int32, (R, R), 1)
    sc = jnp.zeros((R, R), F32)
    w = R
    while w > DIAG:
        h = w // 2
        b3 = b.reshape(R // w, w, REC_DIM)
        rin = lax.broadcasted_iota(jnp.int32, (R // w, w, REC_DIM), 1)
        mid = jnp.sum(jnp.where(rin == h - 1, b3, 0.0), axis=1, keepdims=True)
        fac = jnp.exp(jnp.minimum(jnp.where(rin >= h, b3 - mid, mid - b3), 0.0)).reshape(R, REC_DIM)
        upper = (rows % w) >= h
        s_w = mm_nt(jnp.where(upper, q * fac, 0.0), jnp.where(upper, 0.0, k * fac))
        sc = sc + jnp.where((ri // w) == (ci // w), s_w, 0.0)
        w = h
    o = o + mm_nn(sc, v)

    g = R // DIAG
    q3, k3, v3, b3 = (t.reshape(g, DIAG, REC_DIM) for t in (q, k, v, b))
    rin = lax.broadcasted_iota(jnp.int32, (g, DIAG, 1), 1)
    od = jnp.zeros((g, DIAG, REC_DIM), F32)
    for d in range(DIAG):
        e = jnp.exp(jnp.minimum(b3 - roll_sub(b3, d), 0.0))
        sd = jnp.sum(q3 * roll_sub(k3, d) * e, axis=-1, keepdims=True)
        od = od + jnp.where(rin >= d, sd, 0.0) * roll_sub(v3, d)
    o = o + od.reshape(R, REC_DIM)

    b_last = _row(b, R - 1)
    return o, S * jnp.exp(jnp.minimum(b_last, 0.0)) + mm_tn(v, k * jnp.exp(jnp.minimum(b_last - b, 0.0)))


def _rec_head(core, qr, fr, v, z, S, l0, l1, gw):
    q, k, lf = _rec_front(qr, fr, l0, l1)
    o, S_new = core(q, k, v, cumsum_rows(lf), S)
    return _rec_tail(o, z, gw), S_new


def _rope_tables(positions):
    half = ROPE_DIM // 2
    inv_freq = ROPE_THETA ** (-(jnp.arange(half, dtype=F32) * 2.0 / ROPE_DIM))
    ang = positions.astype(F32).reshape(-1, 1) * inv_freq
    cos, sin = jnp.cos(ang), jnp.sin(ang)
    n = ang.shape[0]
    rest = HEAD_DIM - ROPE_DIM
    one, zero, zh = jnp.ones((n, rest), F32), jnp.zeros((n, rest), F32), jnp.zeros((n, half), F32)
    cos_t = jnp.concatenate([cos, cos, one], axis=-1)
    sin_a = jnp.concatenate([zh, sin, zero], axis=-1)
    sin_b = jnp.concatenate([-sin, zh, zero], axis=-1)
    return tuple(jnp.concatenate([t, t], axis=-1) for t in (cos_t, sin_a, sin_b))


def _rope(x, cos_t, sin_a, sin_b):
    half = ROPE_DIM // 2
    return x * cos_t + pltpu.roll(x, half, 1) * sin_a + pltpu.roll(x, LANES - half, 1) * sin_b


def _rope_transposed(g, cos_t, sin_a, sin_b):
    half = ROPE_DIM // 2
    return g * cos_t + pltpu.roll(g * sin_a, LANES - half, 1) + pltpu.roll(g * sin_b, half, 1)


def _row_spec(tm, width):
    return pl.BlockSpec((tm, width), lambda i: (i, 0))


def _full_spec(shape):
    return pl.BlockSpec(shape, lambda *_: (0,) * len(shape))


def attn_in_proj(x, w_pre, w_in, b_in, tables, tm=512):
    n = x.shape[0]

    def body(x_ref, wp_ref, w_ref, b_ref, c_ref, sa_ref, sb_ref, h_ref, q_ref, k_ref, v_ref, z_ref):
        xv = x_ref[...]
        h = (xv * _rms(xv) * wp_ref[...]).astype(BF16)
        h_ref[...] = h
        proj = jnp.dot(h, w_ref[...], preferred_element_type=F32) + b_ref[...]
        tabs = (c_ref[...], sa_ref[...], sb_ref[...])
        for s in range(D_MODEL // LANES):
            sl = slice(s * LANES, (s + 1) * LANES)
            q_ref[:, sl] = _rope(proj[:, sl] * (HEAD_DIM ** -0.5), *tabs).astype(BF16)
        k_ref[...] = _rope(proj[:, D_MODEL:D_MODEL + KV_WIDTH], *tabs).astype(BF16)
        v_ref[...] = proj[:, D_MODEL + KV_WIDTH:D_MODEL + 2 * KV_WIDTH].astype(BF16)
        z_ref[...] = proj[:, D_MODEL + 2 * KV_WIDTH:]

    return pl.pallas_call(
        body, name="attn_in_proj", grid=(n // tm,),
        in_specs=[_row_spec(tm, D_MODEL), _full_spec((1, D_MODEL)), _full_spec((D_MODEL, ATTN_IN)),
                  _full_spec((1, ATTN_IN))] + [_row_spec(tm, LANES)] * 3,
        out_specs=[_row_spec(tm, D_MODEL), _row_spec(tm, D_MODEL), _row_spec(tm, KV_WIDTH),
                   _row_spec(tm, KV_WIDTH), _row_spec(tm, D_MODEL)],
        out_shape=[jax.ShapeDtypeStruct((n, D_MODEL), BF16), jax.ShapeDtypeStruct((n, D_MODEL), BF16),
                   jax.ShapeDtypeStruct((n, KV_WIDTH), BF16), jax.ShapeDtypeStruct((n, KV_WIDTH), BF16),
                   jax.ShapeDtypeStruct((n, D_MODEL), F32)],
        compiler_params=_cparams(1),
    )(x, w_pre, w_in, b_in, *tables)


def rec_in_proj(x, w_pre, w_in, tm=256):
    n = x.shape[0]

    def body(x_ref, wp_ref, w_ref, h_ref, p_ref):
        xv = x_ref[...]
        h = (xv * _rms(xv) * wp_ref[...]).astype(BF16)
        h_ref[...] = h
        p_ref[...] = jnp.dot(h, w_ref[...], preferred_element_type=F32)

    return pl.pallas_call(
        body, name="rec_in_proj", grid=(n // tm,),
        in_specs=[_row_spec(tm, D_MODEL), _full_spec((1, D_MODEL)), _full_spec((D_MODEL, REC_IN))],
        out_specs=[_row_spec(tm, D_MODEL), _row_spec(tm, REC_IN)],
        out_shape=[jax.ShapeDtypeStruct((n, D_MODEL), BF16), jax.ShapeDtypeStruct((n, REC_IN), F32)],
        compiler_params=_cparams(1),
    )(x, w_pre, w_in)


def out_proj(og, w_out, b_out, x_res, w_post, target=None, tm=512):
    n = og.shape[0]
    with_loss = target is not None

    def body(*refs):
        if with_loss:
            og_ref, w_ref, b_ref, x_ref, wp_ref, t_ref, y_ref, dx_ref, l_ref = refs
        else:
            og_ref, w_ref, b_ref, x_ref, wp_ref, y_ref, xo_ref = refs
        y = jnp.dot(og_ref[...], w_ref[...], preferred_element_type=F32) + b_ref[...]
        y_ref[...] = y
        xo = x_ref[...] + y * _rms(y) * wp_ref[...]
        if with_loss:
            err = xo - t_ref[...]
            dx_ref[...] = err * (1.0 / D_MODEL)

            @pl.when(pl.program_id(0) == 0)
            def _():
                l_ref[...] = jnp.zeros_like(l_ref)

            l_ref[...] += jnp.sum(err * err, axis=0, keepdims=True)
        else:
            xo_ref[...] = xo

    in_specs = [_row_spec(tm, D_MODEL), _full_spec((D_MODEL, D_MODEL)), _full_spec((1, D_MODEL)),
                _row_spec(tm, D_MODEL), _full_spec((1, D_MODEL))]
    out_specs = [_row_spec(tm, D_MODEL), _row_spec(tm, D_MODEL)]
    out_shape = [jax.ShapeDtypeStruct((n, D_MODEL), F32), jax.ShapeDtypeStruct((n, D_MODEL), F32)]
    args = [og, w_out, b_out, x_res, w_post]
    if with_loss:
        in_specs.append(_row_spec(tm, D_MODEL))
        out_specs.append(_full_spec((1, D_MODEL)))
        out_shape.append(jax.ShapeDtypeStruct((1, D_MODEL), F32))
        args.append(target)
    return pl.pallas_call(
        body, name="out_proj_loss" if with_loss else "out_proj", grid=(n // tm,),
        in_specs=in_specs, out_specs=out_specs, out_shape=out_shape, compiler_params=_cparams(1),
    )(*args)


def out_proj_bwd(dxo, y, og, w_out, w_post, tm=512):
    n = og.shape[0]

    def body(g_ref, y_ref, og_ref, w_ref, wp_ref, dog_ref, dw_ref, db_ref, dwp_ref):
        @pl.when(pl.program_id(0) == 0)
        def _():
            dw_ref[...] = jnp.zeros_like(dw_ref)
            db_ref[...] = jnp.zeros_like(db_ref)
            dwp_ref[...] = jnp.zeros_like(dwp_ref)

        g, y = g_ref[...], y_ref[...]
        rstd = _rms(y)
        yn = y * rstd
        gw = g * wp_ref[...]
        dwp_ref[...] += jnp.sum(g * yn, axis=0, keepdims=True)
        dy = rstd * (gw - yn * jnp.mean(gw * yn, axis=-1, keepdims=True))
        db_ref[...] += jnp.sum(dy, axis=0, keepdims=True)
        dyb = dy.astype(BF16)
        dog_ref[...] = _dot(dyb, w_ref[...], _NT)
        dw_ref[...] += _dot(og_ref[...], dyb, _TN)

    return pl.pallas_call(
        body, name="out_proj_bwd", grid=(n // tm,),
        in_specs=[_row_spec(tm, D_MODEL), _row_spec(tm, D_MODEL), _row_spec(tm, D_MODEL),
                  _full_spec((D_MODEL, D_MODEL)), _full_spec((1, D_MODEL))],
        out_specs=[_row_spec(tm, D_MODEL), _full_spec((D_MODEL, D_MODEL)), _full_spec((1, D_MODEL)),
                   _full_spec((1, D_MODEL))],
        out_shape=[jax.ShapeDtypeStruct((n, D_MODEL), F32), jax.ShapeDtypeStruct((D_MODEL, D_MODEL), F32),
                   jax.ShapeDtypeStruct((1, D_MODEL), F32), jax.ShapeDtypeStruct((1, D_MODEL), F32)],
        compiler_params=_cparams(1),
    )(dxo, y, og, w_out, w_post)


def in_proj_bwd_x(dproj, w_in, x, w_pre, dxo, tm=512):
    n, p = dproj.shape

    def body(dp_ref, w_ref, x_ref, wp_ref, g_ref, dx_ref, dwp_ref):
        @pl.when(pl.program_id(0) == 0)
        def _():
            dwp_ref[...] = jnp.zeros_like(dwp_ref)

        dh = _dot(dp_ref[...], w_ref[...], _NT)
        xv = x_ref[...]
        rstd = _rms(xv)
        xn = xv * rstd
        gw = dh * wp_ref[...]
        dwp_ref[...] += jnp.sum(dh * xn, axis=0, keepdims=True)
        dx_ref[...] = rstd * (gw - xn * jnp.mean(gw * xn, axis=-1, keepdims=True)) + g_ref[...]

    return pl.pallas_call(
        body, name=f"in_proj_bwd_x_{p}", grid=(n // tm,),
        in_specs=[_row_spec(tm, p), _full_spec((D_MODEL, p)), _row_spec(tm, D_MODEL), _full_spec((1, D_MODEL)),
                  _row_spec(tm, D_MODEL)],
        out_specs=[_row_spec(tm, D_MODEL), _full_spec((1, D_MODEL))],
        out_shape=[jax.ShapeDtypeStruct((n, D_MODEL), F32), jax.ShapeDtypeStruct((1, D_MODEL), F32)],
        compiler_params=_cparams(1),
    )(dproj, w_in, x, w_pre, dxo)


def in_proj_bwd_w(h, dproj, tm=512, rc=256):
    n, p = dproj.shape

    def body(h_ref, dp_ref, dw_ref, db_ref):
        first_tile = pl.program_id(1) == 0

        @pl.when(first_tile)
        def _():
            dw_ref[...] = jnp.zeros_like(dw_ref)

        @pl.when(first_tile & (pl.program_id(0) == 0))
        def _():
            db_ref[...] = jnp.zeros_like(db_ref)

        dp = dp_ref[...]
        dw_ref[...] += _dot(h_ref[...], dp, _TN)

        @pl.when(pl.program_id(0) == 0)
        def _():
            db_ref[...] += jnp.sum(dp.astype(F32), axis=0, keepdims=True)

    return pl.pallas_call(
        body, name=f"in_proj_bwd_w_{p}", grid=(D_MODEL // rc, n // tm),
        in_specs=[pl.BlockSpec((tm, rc), lambda r, i: (i, r)), pl.BlockSpec((tm, p), lambda r, i: (i, 0))],
        out_specs=[pl.BlockSpec((rc, p), lambda r, i: (r, 0)), pl.BlockSpec((1, p), lambda r, i: (0, 0))],
        out_shape=[jax.ShapeDtypeStruct((D_MODEL, p), F32), jax.ShapeDtypeStruct((1, p), F32)],
        compiler_params=_cparams(2),
    )(h, dproj)


PAIRS = GROUP // 2
GROUP_ROWS = PAIRS * ATTN_BLOCK
MASKED = -1e30


def _kv_windows(k_ref, v_ref, i):
    ps = pl.multiple_of(jnp.maximum(i - 1, 0) * ATTN_BLOCK, ATTN_BLOCK)
    cs = pl.multiple_of(i * ATTN_BLOCK, ATTN_BLOCK)
    kw = jnp.concatenate([k_ref[pl.ds(ps, ATTN_BLOCK), :], k_ref[pl.ds(cs, ATTN_BLOCK), :]], axis=0)
    vw = jnp.concatenate([v_ref[pl.ds(ps, ATTN_BLOCK), :], v_ref[pl.ds(cs, ATTN_BLOCK), :]], axis=0)
    return kw.astype(F32), vw.astype(F32), ps, cs


def _low_lanes(shape):
    return lax.broadcasted_iota(jnp.int32, shape, 1) < HEAD_DIM


def _spread(w, kvh):
    low = _low_lanes(w.shape)
    swapped = pltpu.roll(w, HEAD_DIM, 1)
    if kvh == 0:
        return jnp.where(low, w, 0.0), jnp.where(low, 0.0, swapped)
    return jnp.where(low, swapped, 0.0), jnp.where(low, 0.0, w)


def _unspread(d_a, d_b, kvh):
    low = _low_lanes(d_a.shape)
    if kvh == 0:
        return jnp.where(low, d_a + pltpu.roll(d_b, HEAD_DIM, 1), 0.0)
    return jnp.where(low, 0.0, pltpu.roll(d_a, HEAD_DIM, 1) + d_b)


def _stack_pairs(ref, kvh):
    return jnp.concatenate([ref[:, (kvh * PAIRS + j) * LANES:(kvh * PAIRS + j + 1) * LANES] for j in range(PAIRS)],
                           axis=0)


def _fill_bias(bias_scr):
    shape = (GROUP_ROWS, 2 * ATTN_BLOCK)
    r = lax.broadcasted_iota(jnp.int32, shape, 0) % ATTN_BLOCK
    c = lax.broadcasted_iota(jnp.int32, shape, 1)
    in_cur = (c >= ATTN_BLOCK) & ((c - ATTN_BLOCK) <= r)
    in_prev = (c < ATTN_BLOCK) & (c > r)
    bias_scr[0] = jnp.where(in_cur, 0.0, MASKED)
    bias_scr[1] = jnp.where(in_cur | in_prev, 0.0, MASKED)


def _sink_table(sinks):
    t = jnp.transpose(sinks.reshape(N_KV_HEADS, PAIRS, 2), (0, 2, 1))
    return jnp.broadcast_to(t[:, :, :, None, None], (N_KV_HEADS, 2, PAIRS, ATTN_BLOCK, LANES)).reshape(
        N_KV_HEADS, 2, GROUP_ROWS, LANES)


def attn_fwd(q, k, v, z, sink_tab, batch, seq):
    nb = seq // ATTN_BLOCK

    def body(q_ref, k_ref, v_ref, z_ref, s_ref, og_ref, bias_scr):
        b, i = pl.program_id(0), pl.program_id(1)

        @pl.when((b == 0) & (i == 0))
        def _():
            _fill_bias(bias_scr)

        kw, vw, _, _ = _kv_windows(k_ref, v_ref, i)
        bias = bias_scr[jnp.minimum(i, 1)]
        for kvh in range(N_KV_HEADS):
            k_a, k_b = _spread(kw, kvh)
            v_a, v_b = _spread(vw, kvh)
            og = _attn_group(_stack_pairs(q_ref, kvh), k_a, v_a, k_b, v_b, _stack_pairs(z_ref, kvh),
                             s_ref[kvh, 0], s_ref[kvh, 1], bias)
            for j in range(PAIRS):
                og_ref[:, (kvh * PAIRS + j) * LANES:(kvh * PAIRS + j + 1) * LANES] = (
                    og[j * ATTN_BLOCK:(j + 1) * ATTN_BLOCK].astype(BF16))

    blk = lambda w: pl.BlockSpec((ATTN_BLOCK, w), lambda b, i: (b * nb + i, 0))
    seq_spec = pl.BlockSpec((seq, KV_WIDTH), lambda b, i: (b, 0))
    return pl.pallas_call(
        body, name="attn_fwd", grid=(batch, nb),
        in_specs=[blk(D_MODEL), seq_spec, seq_spec, blk(D_MODEL), _full_spec(sink_tab.shape)],
        out_specs=blk(D_MODEL),
        out_shape=jax.ShapeDtypeStruct((batch * seq, D_MODEL), BF16),
        scratch_shapes=[pltpu.VMEM((2, GROUP_ROWS, 2 * ATTN_BLOCK), F32)],
        compiler_params=_cparams(2),
    )(q, k, v, z, sink_tab)


def attn_bwd(q, k, v, z, sink_tab, dog, tables, batch, seq):
    nb = seq // ATTN_BLOCK

    def body(q_ref, k_ref, v_ref, z_ref, s_ref, g_ref, c_ref, sa_ref, sb_ref, dp_ref, dk_ref, dv_ref, ds_ref,
             bias_scr):
        b, i = pl.program_id(0), pl.program_id(1)

        @pl.when((b == 0) & (i == 0))
        def _():
            _fill_bias(bias_scr)
            ds_ref[...] = jnp.zeros_like(ds_ref)

        @pl.when(i == 0)
        def _():
            dk_ref[...] = jnp.zeros_like(dk_ref)
            dv_ref[...] = jnp.zeros_like(dv_ref)

        kw, vw, ps, cs = _kv_windows(k_ref, v_ref, i)
        bias = bias_scr[jnp.minimum(i, 1)]
        tabs = (c_ref[...], sa_ref[...], sb_ref[...])
        dkw = jnp.zeros_like(kw)
        dvw = jnp.zeros_like(vw)
        for kvh in range(N_KV_HEADS):
            k_a, k_b = _spread(kw, kvh)
            v_a, v_b = _spread(vw, kvh)
            _, vjp = jax.vjp(functools.partial(_attn_group, bias=bias), _stack_pairs(q_ref, kvh).astype(F32),
                             k_a, v_a, k_b, v_b, _stack_pairs(z_ref, kvh), s_ref[kvh, 0], s_ref[kvh, 1])
            dqs, dk_a, dv_a, dk_b, dv_b, dzs, ds_a, ds_b = vjp(_stack_pairs(g_ref, kvh))
            dkw = dkw + _unspread(dk_a, dk_b, kvh)
            dvw = dvw + _unspread(dv_a, dv_b, kvh)
            ds_ref[kvh, 0] += jnp.sum(ds_a.reshape(PAIRS, ATTN_BLOCK, LANES), axis=1)
            ds_ref[kvh, 1] += jnp.sum(ds_b.reshape(PAIRS, ATTN_BLOCK, LANES), axis=1)
            for j in range(PAIRS):
                rows = slice(j * ATTN_BLOCK, (j + 1) * ATTN_BLOCK)
                col = (kvh * PAIRS + j) * LANES
                dp_ref[:, col:col + LANES] = _rope_transposed(dqs[rows] * (HEAD_DIM ** -0.5), *tabs).astype(BF16)
                zc = D_MODEL + 2 * KV_WIDTH + col
                dp_ref[:, zc:zc + LANES] = dzs[rows].astype(BF16)
        dp_ref[:, D_MODEL:D_MODEL + 2 * KV_WIDTH] = jnp.zeros((ATTN_BLOCK, 2 * KV_WIDTH), BF16)
        dk_ref[pl.ds(ps, ATTN_BLOCK), :] += dkw[:ATTN_BLOCK]
        dk_ref[pl.ds(cs, ATTN_BLOCK), :] += dkw[ATTN_BLOCK:]
        dv_ref[pl.ds(ps, ATTN_BLOCK), :] += dvw[:ATTN_BLOCK]
        dv_ref[pl.ds(cs, ATTN_BLOCK), :] += dvw[ATTN_BLOCK:]

    blk = lambda w: pl.BlockSpec((ATTN_BLOCK, w), lambda b, i: (b * nb + i, 0))
    seq_spec = pl.BlockSpec((seq, KV_WIDTH), lambda b, i: (b, 0))
    n = batch * seq
    ds_shape = (N_KV_HEADS, 2, PAIRS, LANES)
    return pl.pallas_call(
        body, name="attn_bwd", grid=(batch, nb),
        in_specs=[blk(D_MODEL), seq_spec, seq_spec, blk(D_MODEL), _full_spec(sink_tab.shape), blk(D_MODEL)]
        + [blk(LANES)] * 3,
        out_specs=[blk(ATTN_IN), seq_spec, seq_spec, _full_spec(ds_shape)],
        out_shape=[jax.ShapeDtypeStruct((n, ATTN_IN), BF16), jax.ShapeDtypeStruct((n, KV_WIDTH), F32),
                   jax.ShapeDtypeStruct((n, KV_WIDTH), F32), jax.ShapeDtypeStruct(ds_shape, F32)],
        scratch_shapes=[pltpu.VMEM((2, GROUP_ROWS, 2 * ATTN_BLOCK), F32)],
        compiler_params=_cparams(2),
    )(q, k, v, z, sink_tab, dog, *tables)


def attn_bwd_kv(dproj, dk, dv, tables, tm=512):
    n = dproj.shape[0]

    def body(dp_in_ref, dk_ref, dv_ref, c_ref, sa_ref, sb_ref, dp_ref):
        del dp_in_ref
        dp_ref[:, :KV_WIDTH] = _rope_transposed(dk_ref[...], c_ref[...], sa_ref[...], sb_ref[...]).astype(BF16)
        dp_ref[:, KV_WIDTH:] = dv_ref[...].astype(BF16)

    kv_cols = pl.BlockSpec((tm, 2 * KV_WIDTH), lambda i: (i, D_MODEL // (2 * KV_WIDTH)))
    return pl.pallas_call(
        body, name="attn_bwd_kv", grid=(n // tm,),
        in_specs=[kv_cols, _row_spec(tm, KV_WIDTH), _row_spec(tm, KV_WIDTH)] + [_row_spec(tm, LANES)] * 3,
        out_specs=kv_cols, out_shape=jax.ShapeDtypeStruct(dproj.shape, BF16),
        input_output_aliases={0: 0}, compiler_params=_cparams(1),
    )(dproj, dk, dv, *tables)


def _rec_cols(part, h):
    return slice(part * D_MODEL + h * REC_DIM, part * D_MODEL + (h + 1) * REC_DIM)


def _rec_args(p_ref, lb_ref, gw_ref, h, S):
    hs = slice(h * REC_DIM, (h + 1) * REC_DIM)
    return (p_ref[:, _rec_cols(0, h)], p_ref[:, _rec_cols(1, h)], p_ref[:, _rec_cols(2, h)],
            p_ref[:, _rec_cols(3, h)], S, lb_ref[0:1, hs], lb_ref[1:2, hs], gw_ref[...])


def rec_fwd(proj, lb_logits, gnorm_w, batch, seq):
    nblk = seq // REC_BLOCK

    def body(p_ref, lb_ref, gw_ref, og_ref, st_ref, safe_ref, s_scr):
        @pl.when(pl.program_id(1) == 0)
        def _():
            s_scr[...] = jnp.zeros_like(s_scr)

        heads, margin = [], None
        for h in range(REC_HEADS):
            S = s_scr[h]
            st_ref[0, h] = S
            qr, fr, v, _, _, l0, l1, _ = _rec_args(p_ref, lb_ref, gw_ref, h, S)
            q, k, lf = _rec_front(qr, fr, l0, l1)
            b = cumsum_rows(lf)
            heads.append((q, k, v, b, S))
            m = _rec_margin(b)
            margin = m if margin is None else jnp.minimum(margin, m)
        safe = jnp.min(margin) >= -SAFE_RANGE

        def run(core):
            return [core(*args) for args in heads]

        results = lax.cond(safe, functools.partial(run, _rec_core_fast), functools.partial(run, _rec_core_slow))
        for h, (o, S_new) in enumerate(results):
            hs = slice(h * REC_DIM, (h + 1) * REC_DIM)
            og_ref[:, hs] = _rec_tail(o, p_ref[:, _rec_cols(3, h)], gw_ref[...]).astype(BF16)
            s_scr[h] = S_new
        safe_ref[0] = jnp.full((REC_HEADS, LANES), safe.astype(F32))

    blk = lambda w: pl.BlockSpec((REC_BLOCK, w), lambda b, j: (b * nblk + j, 0))
    st_spec = pl.BlockSpec((1, REC_HEADS, REC_DIM, REC_DIM), lambda b, j: (b * nblk + j, 0, 0, 0))
    safe_spec = pl.BlockSpec((1, REC_HEADS, LANES), lambda b, j: (b * nblk + j, 0, 0))
    return pl.pallas_call(
        body, name="rec_fwd", grid=(batch, nblk),
        in_specs=[blk(REC_IN), _full_spec((2, D_MODEL)), _full_spec((1, REC_DIM))],
        out_specs=[blk(D_MODEL), st_spec, safe_spec],
        out_shape=[jax.ShapeDtypeStruct((batch * seq, D_MODEL), BF16),
                   jax.ShapeDtypeStruct((batch * nblk, REC_HEADS, REC_DIM, REC_DIM), F32),
                   jax.ShapeDtypeStruct((batch * nblk, REC_HEADS, LANES), F32)],
        scratch_shapes=[pltpu.VMEM((REC_HEADS, REC_DIM, REC_DIM), F32)],
        compiler_params=_cparams(2),
    )(proj, lb_logits, gnorm_w)


def rec_bwd(proj, states, safe, lb_logits, gnorm_w, dog, batch, seq):
    nblk = seq // REC_BLOCK

    def body(p_ref, st_ref, safe_ref, lb_ref, gw_ref, g_ref, dp_ref, dlb_ref, dgw_ref, ds_scr):
        @pl.when((pl.program_id(0) == 0) & (pl.program_id(1) == 0))
        def _():
            dlb_ref[...] = jnp.zeros_like(dlb_ref)
            dgw_ref[...] = jnp.zeros_like(dgw_ref)

        @pl.when(pl.program_id(1) == 0)
        def _():
            ds_scr[...] = jnp.zeros_like(ds_scr)

        def pull_back(core):
            out = []
            for h in range(REC_HEADS):
                hs = slice(h * REC_DIM, (h + 1) * REC_DIM)
                primals = _rec_args(p_ref, lb_ref, gw_ref, h, st_ref[0, h])
                out.append(jax.vjp(functools.partial(_rec_head, core), *primals)[1]((g_ref[:, hs], ds_scr[h])))
            return out

        results = lax.cond(jnp.max(safe_ref[0]) > 0.5, functools.partial(pull_back, _rec_core_fast),
                           functools.partial(pull_back, _rec_core_slow))
        for h, (dqr, dfr, dv, dz, dS, dl0, dl1, dgw) in enumerate(results):
            hs = slice(h * REC_DIM, (h + 1) * REC_DIM)
            for part, val in enumerate((dqr, dfr, dv, dz)):
                dp_ref[:, _rec_cols(part, h)] = val.astype(BF16)
            ds_scr[h] = dS
            dlb_ref[0:1, hs] += dl0
            dlb_ref[1:2, hs] += dl1
            dgw_ref[...] += dgw

    blk = lambda w: pl.BlockSpec((REC_BLOCK, w), lambda b, j: (b * nblk + nblk - 1 - j, 0))
    st_spec = pl.BlockSpec((1, REC_HEADS, REC_DIM, REC_DIM), lambda b, j: (b * nblk + nblk - 1 - j, 0, 0, 0))
    safe_spec = pl.BlockSpec((1, REC_HEADS, LANES), lambda b, j: (b * nblk + nblk - 1 - j, 0, 0))
    return pl.pallas_call(
        body, name="rec_bwd", grid=(batch, nblk),
        in_specs=[blk(REC_IN), st_spec, safe_spec, _full_spec((2, D_MODEL)), _full_spec((1, REC_DIM)),
                  blk(D_MODEL)],
        out_specs=[blk(REC_IN), _full_spec((2, D_MODEL)), _full_spec((1, REC_DIM))],
        out_shape=[jax.ShapeDtypeStruct((batch * seq, REC_IN), BF16), jax.ShapeDtypeStruct((2, D_MODEL), F32),
                   jax.ShapeDtypeStruct((1, REC_DIM), F32)],
        scratch_shapes=[pltpu.VMEM((REC_HEADS, REC_DIM, REC_DIM), F32)],
        compiler_params=_cparams(2),
    )(proj, states, safe, lb_logits, gnorm_w, dog)


_ANY = pl.BlockSpec(memory_space=pl.ANY)


def _chip_peers():
    x, y, c = lax.axis_index("x"), lax.axis_index("y"), lax.axis_index("c")
    flips = ((1, 0), (0, 1), (1, 1))
    peers = [(jnp.where(fx, 1 - x, x), jnp.where(fy, 1 - y, y), c) for fx, fy in flips]
    return 2 * x + y, peers


def chip_exchange(arrays, scatter):
    n = len(arrays)

    def body(*refs):
        ins, outs = refs[:n], refs[n:2 * n]
        send_sems, recv_sems, local_sems = refs[2 * n:]
        me, peers = _chip_peers()
        local = [pltpu.make_async_copy(ins[k].at[me] if scatter else ins[k], outs[k].at[me], local_sems.at[k])
                 for k in range(n)]
        for cp in local:
            cp.start()
        sends, recvs = [], []
        for k in range(n):
            for j, (px, py, pc) in enumerate(peers):
                idx = 2 * px + py
                sem = k * len(peers) + j
                sends.append(pltpu.make_async_remote_copy(
                    src_ref=ins[k].at[idx] if scatter else ins[k], dst_ref=outs[k].at[me],
                    send_sem=send_sems.at[sem], recv_sem=recv_sems.at[sem],
                    device_id=(px, py, pc), device_id_type=MESH))
                recvs.append(pltpu.make_async_remote_copy(
                    src_ref=ins[k].at[me] if scatter else ins[k], dst_ref=outs[k].at[idx],
                    send_sem=send_sems.at[sem], recv_sem=recv_sems.at[sem],
                    device_id=(px, py, pc), device_id_type=MESH))
        for cp in sends:
            cp.start()
        for cp in recvs:
            cp.wait_recv()
        for cp in sends:
            cp.wait_send()
        for cp in local:
            cp.wait()

    out_shape = [jax.ShapeDtypeStruct(a.shape if scatter else (N_CHIPS,) + a.shape, a.dtype) for a in arrays]
    n_copies = n * (N_CHIPS - 1)
    return pl.pallas_call(
        body, name="chip_scatter" if scatter else "chip_gather",
        in_specs=[_ANY] * n, out_specs=[_ANY] * n, out_shape=out_shape,
        scratch_shapes=[pltpu.SemaphoreType.DMA((n_copies,)), pltpu.SemaphoreType.DMA((n_copies,)),
                        pltpu.SemaphoreType.DMA((n,))],
    )(*arrays)


def sibling_exchange(arrays):
    n = len(arrays)

    def body(*refs):
        ins, outs = refs[:n], refs[n:2 * n]
        send_sems, recv_sems = refs[2 * n:]
        sibling = (lax.axis_index("x"), lax.axis_index("y"), 1 - lax.axis_index("c"))
        copies = [pltpu.make_async_remote_copy(src_ref=ins[k], dst_ref=outs[k], send_sem=send_sems.at[k],
                                               recv_sem=recv_sems.at[k], device_id=sibling, device_id_type=MESH)
                  for k in range(n)]
        for cp in copies:
            cp.start()
        for cp in copies:
            cp.wait()

    return pl.pallas_call(
        body, name="sibling_exchange", in_specs=[_ANY] * n, out_specs=[_ANY] * n,
        out_shape=[jax.ShapeDtypeStruct(a.shape, a.dtype) for a in arrays],
        scratch_shapes=[pltpu.SemaphoreType.DMA((n,)), pltpu.SemaphoreType.DMA((n,))],
    )(*arrays)


def all_gather_small(vec):
    def body(v_ref, out_ref, send_sems, recv_sems, local_sem):
        x, y, c = lax.axis_index("x"), lax.axis_index("y"), lax.axis_index("c")
        me = 4 * x + 2 * y + c
        local = pltpu.make_async_copy(v_ref, out_ref.at[me], local_sem)
        local.start()
        sends, recvs = [], []
        for j in range(1, N_DEV):
            px = jnp.where(j & 4, 1 - x, x)
            py = jnp.where(j & 2, 1 - y, y)
            pc = jnp.where(j & 1, 1 - c, c)
            common = dict(send_sem=send_sems.at[j - 1], recv_sem=recv_sems.at[j - 1], device_id=(px, py, pc),
                          device_id_type=MESH)
            sends.append(pltpu.make_async_remote_copy(src_ref=v_ref, dst_ref=out_ref.at[me], **common))
            recvs.append(pltpu.make_async_remote_copy(src_ref=v_ref, dst_ref=out_ref.at[4 * px + 2 * py + pc],
                                                      **common))
        for cp in sends:
            cp.start()
        for cp in recvs:
            cp.wait_recv()
        for cp in sends:
            cp.wait_send()
        local.wait()

    return pl.pallas_call(
        body, name="all_gather_small", in_specs=[_ANY], out_specs=_ANY,
        out_shape=jax.ShapeDtypeStruct((N_DEV,) + vec.shape, vec.dtype),
        scratch_shapes=[pltpu.SemaphoreType.DMA((N_DEV - 1,)), pltpu.SemaphoreType.DMA((N_DEV - 1,)),
                        pltpu.SemaphoreType.DMA],
    )(vec)


def sum_slots(stacked, tm=256):
    s, r, c = stacked.shape
    tm = min(tm, r)

    def body(in_ref, out_ref):
        acc = in_ref[0].astype(F32)
        for t in range(1, s):
            acc = acc + in_ref[t].astype(F32)
        out_ref[...] = acc

    return pl.pallas_call(
        body, name=f"sum_slots_{s}_{r}_{c}", grid=(r // tm,),
        in_specs=[pl.BlockSpec((s, tm, c), lambda i: (0, i, 0))], out_specs=_row_spec(tm, c),
        out_shape=jax.ShapeDtypeStruct((r, c), F32), compiler_params=_cparams(1),
    )(stacked)


def adamw(w, m, v, g_a, g_b=None, tm=256):
    r, c = w.shape
    tm = min(tm, r)
    two = g_b is not None

    def body(*refs):
        w_ref, m_ref, v_ref, ga_ref = refs[:4]
        g_ref, d_ref, nm_ref, nv_ref = refs[-4:]
        g = ga_ref[...] + refs[4][...] if two else ga_ref[...]
        nm = ADAM_B1 * m_ref[...] + (1.0 - ADAM_B1) * g
        nv = ADAM_B2 * v_ref[...] + (1.0 - ADAM_B2) * (g * g)
        m_hat = nm / (1.0 - ADAM_B1 ** ADAM_STEP)
        v_hat = nv / (1.0 - ADAM_B2 ** ADAM_STEP)
        g_ref[...] = g
        d_ref[...] = -ADAM_LR * (m_hat / (jnp.sqrt(v_hat) + ADAM_EPS) + ADAM_WD * w_ref[...])
        nm_ref[...] = nm
        nv_ref[...] = nv

    args = [w, m, v, g_a] + ([g_b] if two else [])
    return pl.pallas_call(
        body, name=f"adamw_{r}_{c}", grid=(r // tm,),
        in_specs=[_row_spec(tm, c)] * len(args), out_specs=[_row_spec(tm, c)] * 4,
        out_shape=[jax.ShapeDtypeStruct((r, c), F32)] * 4, compiler_params=_cparams(1),
    )(*args)


_SMALL = (("pre_norm_w", (2, D_MODEL)), ("post_norm_w", (2, D_MODEL)), ("attn_b_in", (1, ATTN_IN)),
          ("attn_sinks", (1, N_HEADS)), ("attn_b_out", (1, D_MODEL)), ("rec_lb_logits", (2, D_MODEL)),
          ("rec_gnorm_w", (1, REC_DIM)))
_SMALL_ROWS = 16


def _pack_small(parts):
    rows = []
    for (name, shape) in _SMALL:
        flat = parts[name].reshape(-1)
        pad = -flat.shape[0] % D_MODEL
        rows.append(jnp.pad(flat, (0, pad)).reshape(-1, D_MODEL))
    packed = jnp.concatenate(rows, axis=0)
    return jnp.pad(packed, ((0, _SMALL_ROWS - packed.shape[0]), (0, 0)))


def _unpack_small(packed):
    out, row = {}, 0
    for (name, shape) in _SMALL:
        size = shape[0] * shape[1]
        nrows = -(-size // D_MODEL)
        out[name] = packed[row:row + nrows].reshape(-1)[:size].reshape(shape)
        row += nrows
    return out


def local_step(x, positions, pre_norm_w, post_norm_w, attn_w_in, attn_b_in, attn_sinks, attn_w_out, attn_b_out,
               rec_w_in, rec_lb_logits, rec_gnorm_w, rec_w_out, loss_target):
    batch, seq, _ = x.shape
    n = batch * seq
    x0 = x.reshape(n, D_MODEL)
    tables = _rope_tables(positions)
    pre0, pre1 = pre_norm_w[0:1], pre_norm_w[1:2]
    post0, post1 = post_norm_w[0:1], post_norm_w[1:2]
    no_bias = jnp.zeros((1, D_MODEL), F32)

    h0, q, k, v, z = attn_in_proj(x0, pre0, attn_w_in, attn_b_in, tables)
    sink_tab = _sink_table(attn_sinks)
    og0 = attn_fwd(q, k, v, z, sink_tab, batch, seq)
    y0, x1 = out_proj(og0, attn_w_out, attn_b_out, x0, post0)

    h1, proj1 = rec_in_proj(x1, pre1, rec_w_in)
    og1, states, safe = rec_fwd(proj1, rec_lb_logits, rec_gnorm_w, batch, seq)
    y1, dx2, loss_vec = out_proj(og1, rec_w_out, no_bias, x1, post1, target=loss_target.reshape(n, D_MODEL))
    loss = jnp.sum(loss_vec) * (0.5 / D_MODEL)

    dog1, d_rec_w_out, _, d_post1 = out_proj_bwd(dx2, y1, og1, rec_w_out, post1)
    dproj1, d_lb, d_gnorm = rec_bwd(proj1, states, safe, rec_lb_logits, rec_gnorm_w, dog1, batch, seq)
    dx1, d_pre1 = in_proj_bwd_x(dproj1, rec_w_in, x1, pre1, dx2)
    d_rec_w_in, _ = in_proj_bwd_w(h1, dproj1)

    dog0, d_attn_w_out, d_attn_b_out, d_post0 = out_proj_bwd(dx1, y0, og0, attn_w_out, post0)
    dproj0, dk, dv, d_sink_tab = attn_bwd(q, k, v, z, sink_tab, dog0, tables, batch, seq)
    d_sinks = jnp.transpose(jnp.sum(d_sink_tab, axis=-1), (0, 2, 1)).reshape(1, N_HEADS)
    dproj0 = attn_bwd_kv(dproj0, dk, dv, tables)
    dx0, d_pre0 = in_proj_bwd_x(dproj0, attn_w_in, x0, pre0, dx1)
    d_attn_w_in, d_attn_b_in = in_proj_bwd_w(h0, dproj0)

    grads = dict(
        pre_norm_w=jnp.concatenate([d_pre0, d_pre1], axis=0), post_norm_w=jnp.concatenate([d_post0, d_post1], axis=0),
        attn_w_in=d_attn_w_in, attn_b_in=d_attn_b_in, attn_sinks=d_sinks, attn_w_out=d_attn_w_out,
        attn_b_out=d_attn_b_out, rec_w_in=d_rec_w_in, rec_lb_logits=d_lb, rec_gnorm_w=d_gnorm,
        rec_w_out=d_rec_w_out)
    return loss, dx0.reshape(batch, seq, D_MODEL), grads


_BIG = ("attn_w_in", "attn_w_out", "rec_w_in", "rec_w_out")
_COLUMN_SHARDED = ("attn_w_in", "rec_w_in")
_ORDER = ("pre_norm_w", "post_norm_w", "attn_w_in", "attn_b_in", "attn_sinks", "attn_w_out", "attn_b_out",
          "rec_w_in", "rec_lb_logits", "rec_gnorm_w", "rec_w_out")


def _whole_from_shards(name, stacked):
    if name in _COLUMN_SHARDED:
        return jnp.transpose(stacked, (1, 0, 2)).reshape(stacked.shape[1], -1)
    return stacked.reshape(-1, stacked.shape[2])


def _shards_from_whole(name, whole):
    if name in _COLUMN_SHARDED:
        return jnp.transpose(whole.reshape(whole.shape[0], N_CHIPS, -1), (1, 0, 2))
    return whole.reshape(N_CHIPS, -1, whole.shape[1])


def kernel(x, positions, pre_norm_w, post_norm_w, attn_w_in, attn_b_in, attn_sinks, attn_w_out, attn_b_out, rec_w_in, rec_lb_logits, rec_gnorm_w, rec_w_out, loss_target, m_pre_norm_w, m_post_norm_w, m_attn_w_in, m_attn_b_in, m_attn_sinks, m_attn_w_out, m_attn_b_out, m_rec_w_in, m_rec_lb_logits, m_rec_gnorm_w, m_rec_w_out, v_pre_norm_w, v_post_norm_w, v_attn_w_in, v_attn_b_in, v_attn_sinks, v_attn_w_out, v_attn_b_out, v_rec_w_in, v_rec_lb_logits, v_rec_gnorm_w, v_rec_w_out):
    w = dict(pre_norm_w=pre_norm_w, post_norm_w=post_norm_w, attn_w_in=attn_w_in, attn_b_in=attn_b_in,
             attn_sinks=attn_sinks, attn_w_out=attn_w_out, attn_b_out=attn_b_out, rec_w_in=rec_w_in,
             rec_lb_logits=rec_lb_logits, rec_gnorm_w=rec_gnorm_w, rec_w_out=rec_w_out)
    m = dict(pre_norm_w=m_pre_norm_w, post_norm_w=m_post_norm_w, attn_w_in=m_attn_w_in, attn_b_in=m_attn_b_in,
             attn_sinks=m_attn_sinks, attn_w_out=m_attn_w_out, attn_b_out=m_attn_b_out, rec_w_in=m_rec_w_in,
             rec_lb_logits=m_rec_lb_logits, rec_gnorm_w=m_rec_gnorm_w, rec_w_out=m_rec_w_out)
    v = dict(pre_norm_w=v_pre_norm_w, post_norm_w=v_post_norm_w, attn_w_in=v_attn_w_in, attn_b_in=v_attn_b_in,
             attn_sinks=v_attn_sinks, attn_w_out=v_attn_w_out, attn_b_out=v_attn_b_out, rec_w_in=v_rec_w_in,
             rec_lb_logits=v_rec_lb_logits, rec_gnorm_w=v_rec_gnorm_w, rec_w_out=v_rec_w_out)

    shards = {name: w[name][0] for name in _BIG}
    gathered = chip_exchange([shards[name].astype(BF16) for name in _BIG], scatter=False)
    whole = {name: _whole_from_shards(name, g) for name, g in zip(_BIG, gathered)}

    loss, grad_x, grads = local_step(
        x, positions, pre_norm_w, post_norm_w, whole["attn_w_in"], attn_b_in, attn_sinks, whole["attn_w_out"],
        attn_b_out, whole["rec_w_in"], rec_lb_logits, rec_gnorm_w, whole["rec_w_out"], loss_target)
    loss = lax.psum(loss, ("x", "y", "c"))

    parts = chip_exchange([_shards_from_whole(name, grads[name]).astype(BF16) for name in _BIG], scatter=True)
    plane_sums = [sum_slots(p) for p in parts]
    other_sums = sibling_exchange(plane_sums)
    out_g, out_d, out_m, out_v = {}, {}, {}, {}
    for name, mine, other in zip(_BIG, plane_sums, other_sums):
        g, d, nm, nv = adamw(shards[name], m[name][0], v[name][0], mine, other)
        out_g[name], out_d[name], out_m[name], out_v[name] = g[None], d[None], nm[None], nv[None]

    small_sum = sum_slots(all_gather_small(_pack_small(grads)))
    packed = adamw(_pack_small(w), _pack_small(m), _pack_small(v), small_sum)
    for dst, val in zip((out_g, out_d, out_m, out_v), packed):
        dst.update(_unpack_small(val))

    return (loss, grad_x, *[out_g[n] for n in _ORDER], *[out_d[n] for n in _ORDER],
            *[out_m[n] for n in _ORDER], *[out_v[n] for n in _ORDER])
```

```python
import functools

import jax
import jax.numpy as jnp
from jax import lax
from jax.experimental import pallas as pl
from jax.experimental.pallas import tpu as pltpu

F32 = jnp.float32
BF16 = jnp.bfloat16
MESH = pl.DeviceIdType.MESH

D_MODEL = 1024
HEAD_DIM = 64
N_HEADS = 16
N_KV_HEADS = 2
GROUP = N_HEADS // N_KV_HEADS
KV_WIDTH = N_KV_HEADS * HEAD_DIM
ATTN_IN = 2 * D_MODEL + 2 * KV_WIDTH
ATTN_BLOCK = 128
ROPE_THETA = 500000.0
ROPE_DIM = HEAD_DIM // 4
REC_HEADS = 8
REC_DIM = 128
REC_IN = 4 * D_MODEL
REC_BLOCK = 128
DIAG = 8
NORM_EPS = 1e-6
N_CHIPS = 4
N_DEV = 8
LANES = 128

ADAM_LR = 0.001
ADAM_B1 = 0.9
ADAM_B2 = 0.999
ADAM_EPS = 1e-08
ADAM_WD = 0.01
ADAM_STEP = 10

VMEM_LIMIT = 56 * 1024 * 1024


def _cparams(n_axes):
    return pltpu.CompilerParams(dimension_semantics=("arbitrary",) * n_axes, vmem_limit_bytes=VMEM_LIMIT)


def _dot(a, b, contract):
    return lax.dot_general(a.astype(BF16), b.astype(BF16), (contract, ((), ())), preferred_element_type=F32)


_NN = ((1,), (0,))
_NT = ((1,), (1,))
_TN = ((0,), (0,))


@jax.custom_vjp
def mm_nn(a, b):
    return _dot(a, b, _NN)


mm_nn.defvjp(lambda a, b: (_dot(a, b, _NN), (a, b)),
             lambda res, g: (_dot(g, res[1], _NT), _dot(res[0], g, _TN)))


@jax.custom_vjp
def mm_nt(a, b):
    return _dot(a, b, _NT)


mm_nt.defvjp(lambda a, b: (_dot(a, b, _NT), (a, b)),
             lambda res, g: (_dot(g, res[1], _NN), _dot(g, res[0], _TN)))


@jax.custom_vjp
def mm_tn(a, b):
    return _dot(a, b, _TN)


mm_tn.defvjp(lambda a, b: (_dot(a, b, _TN), (a, b)),
             lambda res, g: (_dot(res[1], g, _NT), _dot(res[0], g, _NN)))


def _tri_dot(x, lower):
    n = x.shape[0]
    r = lax.broadcasted_iota(jnp.int32, (n, n), 0)
    c = lax.broadcasted_iota(jnp.int32, (n, n), 1)
    tri = ((c <= r) if lower else (c >= r)).astype(BF16)
    hi = x.astype(BF16)
    rest = x - hi.astype(F32)
    mid = rest.astype(BF16)
    lo = (rest - mid.astype(F32)).astype(BF16)
    dot = lambda p: lax.dot_general(tri, p, (_NN, ((), ())), preferred_element_type=F32)
    return (dot(lo) + dot(mid)) + dot(hi)


@jax.custom_vjp
def cumsum_rows(x):
    return _tri_dot(x, True)


cumsum_rows.defvjp(lambda x: (cumsum_rows(x), None), lambda _, g: (_tri_dot(g, False),))


@functools.partial(jax.custom_vjp, nondiff_argnums=(1,))
def roll_sub(x, d):
    return pltpu.roll(x, d, 1) if d else x


roll_sub.defvjp(lambda x, d: (roll_sub(x, d), None),
                lambda d, _, g: (roll_sub(g, (DIAG - d) % DIAG),))


def sigmoid(x):
    return 0.5 * (jnp.tanh(0.5 * x) + 1.0)


@jax.custom_vjp
def silu(x):
    return x * sigmoid(x)


def _silu_fwd(x):
    s = sigmoid(x)
    return x * s, (x, s)


silu.defvjp(_silu_fwd, lambda res, g: (g * (res[1] * (1.0 + res[0] * (1.0 - res[1]))),))


def log_sigmoid_pair(x):
    t = jnp.log(1.0 + jnp.exp(-jnp.abs(x)))
    return jnp.minimum(x, 0.0) - t, jnp.minimum(-x, 0.0) - t


def _forget_fwd(x, a):
    log_lb, log_1m_lb = log_sigmoid_pair(a)
    ls_f, ls_nf = log_sigmoid_pair(x)
    c = log_1m_lb + ls_f
    lf = jnp.maximum(log_lb, c) + jnp.log(1.0 + jnp.exp(-jnp.abs(log_lb - c)))
    k = jnp.exp(log_1m_lb + ls_nf)
    return (lf, k), (log_lb, log_1m_lb, ls_f, ls_nf, c, lf, k)


def _forget_bwd(res, g):
    log_lb, log_1m_lb, ls_f, ls_nf, c, lf, k = res
    g_lf, g_k = g
    wc = jnp.exp(jnp.minimum(c - lf, 0.0))
    gk = g_k * k
    dx = g_lf * wc * jnp.exp(ls_nf) - gk * jnp.exp(ls_f)
    lb = jnp.exp(log_lb)
    da = jnp.sum(g_lf * ((1.0 - wc) * jnp.exp(log_1m_lb) - wc * lb) - gk * lb, axis=0, keepdims=True)
    return dx, da


@jax.custom_vjp
def forget_gate(x, a):
    return _forget_fwd(x, a)[0]


forget_gate.defvjp(_forget_fwd, _forget_bwd)


@jax.custom_vjp
def decayed(x, e):
    return (x * jnp.exp(e)).astype(BF16).astype(F32)


def _decayed_fwd(x, e):
    y = decayed(x, e)
    return y, (y, e)


decayed.defvjp(_decayed_fwd, lambda res, g: (g * jnp.exp(res[1]), g * res[0]))


def _row(x, r):
    shape = x.shape

    @jax.custom_vjp
    def take(x):
        return x[r:r + 1, :]

    take.defvjp(lambda x: (x[r:r + 1, :], None),
                lambda _, g: (jnp.where(lax.broadcasted_iota(jnp.int32, shape, 0) == r, g, 0.0),))
    return take(x)


def _rms(x):
    return lax.rsqrt(jnp.mean(x * x, axis=-1, keepdims=True) + NORM_EPS)


def _attn_group(qs, k_a, v_a, k_b, v_b, zs, sink_a, sink_b, bias):
    def half(kh, vh, sink):
        s = mm_nt(qs, kh) + bias
        m = lax.stop_gradient(jnp.maximum(jnp.max(s, axis=-1, keepdims=True), jnp.max(sink, axis=-1, keepdims=True)))
        p = jnp.exp(s - m)
        denom = jnp.sum(p, axis=-1, keepdims=True) + jnp.sum(jnp.exp(sink - m), axis=-1, keepdims=True) * (1.0 / LANES)
        return mm_nn(p * (1.0 / denom), vh)

    return (half(k_a, v_a, sink_a) + half(k_b, v_b, sink_b)) * silu(zs)


SAFE_RANGE = 80.0


def _rec_front(qr, fr, l0, l1):
    lf, k = forget_gate(fr, l1 - l0)
    return silu(qr), k, lf


def _rec_tail(o, z, gw):
    return o * _rms(o) * gw * silu(z)


def _rec_margin(b):
    R = b.shape[0]
    mid, last = _row(b, R // 2 - 1), _row(b, R - 1)
    return jnp.minimum(mid, last - mid)


def _rec_core_fast(q, k, v, b, S):
    R = q.shape[0]
    ri = lax.broadcasted_iota(jnp.int32, (R, R), 0)
    ci = lax.broadcasted_iota(jnp.int32, (R, R), 1)
    d = b - _row(b, R // 2 - 1)
    sc = jnp.where(ci < ri, mm_nt(decayed(q, d), decayed(k, -d)), 0.0)
    o = mm_nt(q * jnp.exp(b), S) + mm_nn(sc, v) + jnp.sum(q * k, axis=-1, keepdims=True) * v
    b_last = _row(b, R - 1)
    return o, S * jnp.exp(b_last) + mm_tn(v, k * jnp.exp(b_last - b))


def _rec_core_slow(q, k, v, b, S):
    R = q.shape[0]
    rows = lax.broadcasted_iota(jnp.int32, (R, REC_DIM), 0)

    o = mm_nt(q * jnp.exp(jnp.minimum(b, 0.0)), S)

    ri = lax.broadcasted_iota(jnp.int32, (R, R), 0)
    ci = lax.broadcasted_iota(jnp.int32, (R, R), 1)
    sc = jnp.zeros((R, R), F32)
    w = R
    while w > DIAG:
        h = w // 2
        b3 = b.reshape(R // w, w, REC_DIM)
        rin = lax.broadcasted_iota(jnp.int32, (R // w, w, REC_DIM), 1)
        mid = jnp.sum(jnp.where(rin == h - 1, b3, 0.0), axis=1, keepdims=True)
        fac = jnp.exp(jnp.minimum(jnp.where(rin >= h, b3 - mid, mid - b3), 0.0)).reshape(R, REC_DIM)
        upper = (rows % w) >= h
        s_w = mm_nt(jnp.where(upper, q * fac, 0.0), jnp.where(upper, 0.0, k * fac))
        sc = sc + jnp.where((ri // w) == (ci // w), s_w, 0.0)
        w = h
    o = o + mm_nn(sc, v)

    g = R // DIAG
    q3, k3, v3, b3 = (t.reshape(g, DIAG, REC_DIM) for t in (q, k, v, b))
    rin = lax.broadcasted_iota(jnp.int32, (g, DIAG, 1), 1)
    od = jnp.zeros((g, DIAG, REC_DIM), F32)
    for d in range(DIAG):
        e = jnp.exp(jnp.minimum(b3 - roll_sub(b3, d), 0.0))
        sd = jnp.sum(q3 * roll_sub(k3, d) * e, axis=-1, keepdims=True)
        od = od + jnp.where(rin >= d, sd, 0.0) * roll_sub(v3, d)
    o = o + od.reshape(R, REC_DIM)

    b_last = _row(b, R - 1)
    return o, S * jnp.exp(jnp.minimum(b_last, 0.0)) + mm_tn(v, k * jnp.exp(jnp.minimum(b_last - b, 0.0)))


def _rec_head(core, qr, fr, v, z, S, l0, l1, gw):
    q, k, lf = _rec_front(qr, fr, l0, l1)
    o, S_new = core(q, k, v, cumsum_rows(lf), S)
    return _rec_tail(o, z, gw), S_new


def _rope_tables(positions):
    half = ROPE_DIM // 2
    inv_freq = ROPE_THETA ** (-(jnp.arange(half, dtype=F32) * 2.0 / ROPE_DIM))
    ang = positions.astype(F32).reshape(-1, 1) * inv_freq
    cos, sin = jnp.cos(ang), jnp.sin(ang)
    n = ang.shape[0]
    rest = HEAD_DIM - ROPE_DIM
    one, zero, zh = jnp.ones((n, rest), F32), jnp.zeros((n, rest), F32), jnp.zeros((n, half), F32)
    cos_t = jnp.concatenate([cos, cos, one], axis=-1)
    sin_a = jnp.concatenate([zh, sin, zero], axis=-1)
    sin_b = jnp.concatenate([-sin, zh, zero], axis=-1)
    return tuple(jnp.concatenate([t, t], axis=-1) for t in (cos_t, sin_a, sin_b))


def _rope(x, cos_t, sin_a, sin_b):
    half = ROPE_DIM // 2
    return x * cos_t + pltpu.roll(x, half, 1) * sin_a + pltpu.roll(x, LANES - half, 1) * sin_b


def _rope_transposed(g, cos_t, sin_a, sin_b):
    half = ROPE_DIM // 2
    return g * cos_t + pltpu.roll(g * sin_a, LANES - half, 1) + pltpu.roll(g * sin_b, half, 1)


def _row_spec(tm, width):
    return pl.BlockSpec((tm, width), lambda i: (i, 0))


def _full_spec(shape):
    return pl.BlockSpec(shape, lambda *_: (0,) * len(shape))


def attn_in_proj(x, w_pre, w_in, b_in, tables, tm=512):
    n = x.shape[0]

    def body(x_ref, wp_ref, w_ref, b_ref, c_ref, sa_ref, sb_ref, h_ref, q_ref, k_ref, v_ref, z_ref):
        xv = x_ref[...]
        h = (xv * _rms(xv) * wp_ref[...]).astype(BF16)
        h_ref[...] = h
        proj = jnp.dot(h, w_ref[...], preferred_element_type=F32) + b_ref[...]
        tabs = (c_ref[...], sa_ref[...], sb_ref[...])
        for s in range(D_MODEL // LANES):
            sl = slice(s * LANES, (s + 1) * LANES)
            q_ref[:, sl] = _rope(proj[:, sl] * (HEAD_DIM ** -0.5), *tabs).astype(BF16)
        k_ref[...] = _rope(proj[:, D_MODEL:D_MODEL + KV_WIDTH], *tabs).astype(BF16)
        v_ref[...] = proj[:, D_MODEL + KV_WIDTH:D_MODEL + 2 * KV_WIDTH].astype(BF16)
        z_ref[...] = proj[:, D_MODEL + 2 * KV_WIDTH:]

    return pl.pallas_call(
        body, name="attn_in_proj", grid=(n // tm,),
        in_specs=[_row_spec(tm, D_MODEL), _full_spec((1, D_MODEL)), _full_spec((D_MODEL, ATTN_IN)),
                  _full_spec((1, ATTN_IN))] + [_row_spec(tm, LANES)] * 3,
        out_specs=[_row_spec(tm, D_MODEL), _row_spec(tm, D_MODEL), _row_spec(tm, KV_WIDTH),
                   _row_spec(tm, KV_WIDTH), _row_spec(tm, D_MODEL)],
        out_shape=[jax.ShapeDtypeStruct((n, D_MODEL), BF16), jax.ShapeDtypeStruct((n, D_MODEL), BF16),
                   jax.ShapeDtypeStruct((n, KV_WIDTH), BF16), jax.ShapeDtypeStruct((n, KV_WIDTH), BF16),
                   jax.ShapeDtypeStruct((n, D_MODEL), F32)],
        compiler_params=_cparams(1),
    )(x, w_pre, w_in, b_in, *tables)


def rec_in_proj(x, w_pre, w_in, tm=256):
    n = x.shape[0]

    def body(x_ref, wp_ref, w_ref, h_ref, p_ref):
        xv = x_ref[...]
        h = (xv * _rms(xv) * wp_ref[...]).astype(BF16)
        h_ref[...] = h
        p_ref[...] = jnp.dot(h, w_ref[...], preferred_element_type=F32)

    return pl.pallas_call(
        body, name="rec_in_proj", grid=(n // tm,),
        in_specs=[_row_spec(tm, D_MODEL), _full_spec((1, D_MODEL)), _full_spec((D_MODEL, REC_IN))],
        out_specs=[_row_spec(tm, D_MODEL), _row_spec(tm, REC_IN)],
        out_shape=[jax.ShapeDtypeStruct((n, D_MODEL), BF16), jax.ShapeDtypeStruct((n, REC_IN), F32)],
        compiler_params=_cparams(1),
    )(x, w_pre, w_in)


def out_proj(og, w_out, b_out, x_res, w_post, target=None, tm=512):
    n = og.shape[0]
    with_loss = target is not None

    def body(*refs):
        if with_loss:
            og_ref, w_ref, b_ref, x_ref, wp_ref, t_ref, y_ref, dx_ref, l_ref = refs
        else:
            og_ref, w_ref, b_ref, x_ref, wp_ref, y_ref, xo_ref = refs
        y = jnp.dot(og_ref[...], w_ref[...], preferred_element_type=F32) + b_ref[...]
        y_ref[...] = y
        xo = x_ref[...] + y * _rms(y) * wp_ref[...]
        if with_loss:
            err = xo - t_ref[...]
            dx_ref[...] = err * (1.0 / D_MODEL)

            @pl.when(pl.program_id(0) == 0)
            def _():
                l_ref[...] = jnp.zeros_like(l_ref)

            l_ref[...] += jnp.sum(err * err, axis=0, keepdims=True)
        else:
            xo_ref[...] = xo

    in_specs = [_row_spec(tm, D_MODEL), _full_spec((D_MODEL, D_MODEL)), _full_spec((1, D_MODEL)),
                _row_spec(tm, D_MODEL), _full_spec((1, D_MODEL))]
    out_specs = [_row_spec(tm, D_MODEL), _row_spec(tm, D_MODEL)]
    out_shape = [jax.ShapeDtypeStruct((n, D_MODEL), F32), jax.ShapeDtypeStruct((n, D_MODEL), F32)]
    args = [og, w_out, b_out, x_res, w_post]
    if with_loss:
        in_specs.append(_row_spec(tm, D_MODEL))
        out_specs.append(_full_spec((1, D_MODEL)))
        out_shape.append(jax.ShapeDtypeStruct((1, D_MODEL), F32))
        args.append(target)
    return pl.pallas_call(
        body, name="out_proj_loss" if with_loss else "out_proj", grid=(n // tm,),
        in_specs=in_specs, out_specs=out_specs, out_shape=out_shape, compiler_params=_cparams(1),
    )(*args)


def out_proj_bwd(dxo, y, og, w_out, w_post, tm=512):
    n = og.shape[0]

    def body(g_ref, y_ref, og_ref, w_ref, wp_ref, dog_ref, dw_ref, db_ref, dwp_ref):
        @pl.when(pl.program_id(0) == 0)
        def _():
            dw_ref[...] = jnp.zeros_like(dw_ref)
            db_ref[...] = jnp.zeros_like(db_ref)
            dwp_ref[...] = jnp.zeros_like(dwp_ref)

        g, y = g_ref[...], y_ref[...]
        rstd = _rms(y)
        yn = y * rstd
        gw = g * wp_ref[...]
        dwp_ref[...] += jnp.sum(g * yn, axis=0, keepdims=True)
        dy = rstd * (gw - yn * jnp.mean(gw * yn, axis=-1, keepdims=True))
        db_ref[...] += jnp.sum(dy, axis=0, keepdims=True)
        dyb = dy.astype(BF16)
        dog_ref[...] = _dot(dyb, w_ref[...], _NT)
        dw_ref[...] += _dot(og_ref[...], dyb, _TN)

    return pl.pallas_call(
        body, name="out_proj_bwd", grid=(n // tm,),
        in_specs=[_row_spec(tm, D_MODEL), _row_spec(tm, D_MODEL), _row_spec(tm, D_MODEL),
                  _full_spec((D_MODEL, D_MODEL)), _full_spec((1, D_MODEL))],
        out_specs=[_row_spec(tm, D_MODEL), _full_spec((D_MODEL, D_MODEL)), _full_spec((1, D_MODEL)),
                   _full_spec((1, D_MODEL))],
        out_shape=[jax.ShapeDtypeStruct((n, D_MODEL), F32), jax.ShapeDtypeStruct((D_MODEL, D_MODEL), F32),
                   jax.ShapeDtypeStruct((1, D_MODEL), F32), jax.ShapeDtypeStruct((1, D_MODEL), F32)],
        compiler_params=_cparams(1),
    )(dxo, y, og, w_out, w_post)


def in_proj_bwd_x(dproj, w_in, x, w_pre, dxo, tm=512):
    n, p = dproj.shape

    def body(dp_ref, w_ref, x_ref, wp_ref, g_ref, dx_ref, dwp_ref):
        @pl.when(pl.program_id(0) == 0)
        def _():
            dwp_ref[...] = jnp.zeros_like(dwp_ref)

        dh = _dot(dp_ref[...], w_ref[...], _NT)
        xv = x_ref[...]
        rstd = _rms(xv)
        xn = xv * rstd
        gw = dh * wp_ref[...]
        dwp_ref[...] += jnp.sum(dh * xn, axis=0, keepdims=True)
        dx_ref[...] = rstd * (gw - xn * jnp.mean(gw * xn, axis=-1, keepdims=True)) + g_ref[...]

    return pl.pallas_call(
        body, name=f"in_proj_bwd_x_{p}", grid=(n // tm,),
        in_specs=[_row_spec(tm, p), _full_spec((D_MODEL, p)), _row_spec(tm, D_MODEL), _full_spec((1, D_MODEL)),
                  _row_spec(tm, D_MODEL)],
        out_specs=[_row_spec(tm, D_MODEL), _full_spec((1, D_MODEL))],
        out_shape=[jax.ShapeDtypeStruct((n, D_MODEL), F32), jax.ShapeDtypeStruct((1, D_MODEL), F32)],
        compiler_params=_cparams(1),
    )(dproj, w_in, x, w_pre, dxo)


def in_proj_bwd_w(h, dproj, tm=512, rc=256):
    n, p = dproj.shape

    def body(h_ref, dp_ref, dw_ref, db_ref):
        first_tile = pl.program_id(1) == 0

        @pl.when(first_tile)
        def _():
            dw_ref[...] = jnp.zeros_like(dw_ref)

        @pl.when(first_tile & (pl.program_id(0) == 0))
        def _():
            db_ref[...] = jnp.zeros_like(db_ref)

        dp = dp_ref[...]
        dw_ref[...] += _dot(h_ref[...], dp, _TN)

        @pl.when(pl.program_id(0) == 0)
        def _():
            db_ref[...] += jnp.sum(dp.astype(F32), axis=0, keepdims=True)

    return pl.pallas_call(
        body, name=f"in_proj_bwd_w_{p}", grid=(D_MODEL // rc, n // tm),
        in_specs=[pl.BlockSpec((tm, rc), lambda r, i: (i, r)), pl.BlockSpec((tm, p), lambda r, i: (i, 0))],
        out_specs=[pl.BlockSpec((rc, p), lambda r, i: (r, 0)), pl.BlockSpec((1, p), lambda r, i: (0, 0))],
        out_shape=[jax.ShapeDtypeStruct((D_MODEL, p), F32), jax.ShapeDtypeStruct((1, p), F32)],
        compiler_params=_cparams(2),
    )(h, dproj)


PAIRS = GROUP // 2
GROUP_ROWS = PAIRS * ATTN_BLOCK
MASKED = -1e30


def _kv_windows(k_ref, v_ref, i):
    ps = pl.multiple_of(jnp.maximum(i - 1, 0) * ATTN_BLOCK, ATTN_BLOCK)
    cs = pl.multiple_of(i * ATTN_BLOCK, ATTN_BLOCK)
    kw = jnp.concatenate([k_ref[pl.ds(ps, ATTN_BLOCK), :], k_ref[pl.ds(cs, ATTN_BLOCK), :]], axis=0)
    vw = jnp.concatenate([v_ref[pl.ds(ps, ATTN_BLOCK), :], v_ref[pl.ds(cs, ATTN_BLOCK), :]], axis=0)
    return kw.astype(F32), vw.astype(F32), ps, cs


def _low_lanes(shape):
    return lax.broadcasted_iota(jnp.int32, shape, 1) < HEAD_DIM


def _spread(w, kvh):
    low = _low_lanes(w.shape)
    swapped = pltpu.roll(w, HEAD_DIM, 1)
    if kvh == 0:
        return jnp.where(low, w, 0.0), jnp.where(low, 0.0, swapped)
    return jnp.where(low, swapped, 0.0), jnp.where(low, 0.0, w)


def _unspread(d_a, d_b, kvh):
    low = _low_lanes(d_a.shape)
    if kvh == 0:
        return jnp.where(low, d_a + pltpu.roll(d_b, HEAD_DIM, 1), 0.0)
    return jnp.where(low, 0.0, pltpu.roll(d_a, HEAD_DIM, 1) + d_b)


def _stack_pairs(ref, kvh):
    return jnp.concatenate([ref[:, (kvh * PAIRS + j) * LANES:(kvh * PAIRS + j + 1) * LANES] for j in range(PAIRS)],
                           axis=0)


def _fill_bias(bias_scr):
    shape = (GROUP_ROWS, 2 * ATTN_BLOCK)
    r = lax.broadcasted_iota(jnp.int32, shape, 0) % ATTN_BLOCK
    c = lax.broadcasted_iota(jnp.int32, shape, 1)
    in_cur = (c >= ATTN_BLOCK) & ((c - ATTN_BLOCK) <= r)
    in_prev = (c < ATTN_BLOCK) & (c > r)
    bias_scr[0] = jnp.where(in_cur, 0.0, MASKED)
    bias_scr[1] = jnp.where(in_cur | in_prev, 0.0, MASKED)


def _sink_table(sinks):
    t = jnp.transpose(sinks.reshape(N_KV_HEADS, PAIRS, 2), (0, 2, 1))
    return jnp.broadcast_to(t[:, :, :, None, None], (N_KV_HEADS, 2, PAIRS, ATTN_BLOCK, LANES)).reshape(
        N_KV_HEADS, 2, GROUP_ROWS, LANES)


def attn_fwd(q, k, v, z, sink_tab, batch, seq):
    nb = seq // ATTN_BLOCK

    def body(q_ref, k_ref, v_ref, z_ref, s_ref, og_ref, bias_scr):
        b, i = pl.program_id(0), pl.program_id(1)

        @pl.when((b == 0) & (i == 0))
        def _():
            _fill_bias(bias_scr)

        kw, vw, _, _ = _kv_windows(k_ref, v_ref, i)
        bias = bias_scr[jnp.minimum(i, 1)]
        for kvh in range(N_KV_HEADS):
            k_a, k_b = _spread(kw, kvh)
            v_a, v_b = _spread(vw, kvh)
            og = _attn_group(_stack_pairs(q_ref, kvh), k_a, v_a, k_b, v_b, _stack_pairs(z_ref, kvh),
                             s_ref[kvh, 0], s_ref[kvh, 1], bias)
            for j in range(PAIRS):
                og_ref[:, (kvh * PAIRS + j) * LANES:(kvh * PAIRS + j + 1) * LANES] = (
                    og[j * ATTN_BLOCK:(j + 1) * ATTN_BLOCK].astype(BF16))

    blk = lambda w: pl.BlockSpec((ATTN_BLOCK, w), lambda b, i: (b * nb + i, 0))
    seq_spec = pl.BlockSpec((seq, KV_WIDTH), lambda b, i: (b, 0))
    return pl.pallas_call(
        body, name="attn_fwd", grid=(batch, nb),
        in_specs=[blk(D_MODEL), seq_spec, seq_spec, blk(D_MODEL), _full_spec(sink_tab.shape)],
        out_specs=blk(D_MODEL),
        out_shape=jax.ShapeDtypeStruct((batch * seq, D_MODEL), BF16),
        scratch_shapes=[pltpu.VMEM((2, GROUP_ROWS, 2 * ATTN_BLOCK), F32)],
        compiler_params=_cparams(2),
    )(q, k, v, z, sink_tab)


def attn_bwd(q, k, v, z, sink_tab, dog, tables, batch, seq):
    nb = seq // ATTN_BLOCK

    def body(q_ref, k_ref, v_ref, z_ref, s_ref, g_ref, c_ref, sa_ref, sb_ref, dp_ref, dk_ref, dv_ref, ds_ref,
             bias_scr):
        b, i = pl.program_id(0), pl.program_id(1)

        @pl.when((b == 0) & (i == 0))
        def _():
            _fill_bias(bias_scr)
            ds_ref[...] = jnp.zeros_like(ds_ref)

        @pl.when(i == 0)
        def _():
            dk_ref[...] = jnp.zeros_like(dk_ref)
            dv_ref[...] = jnp.zeros_like(dv_ref)

        kw, vw, ps, cs = _kv_windows(k_ref, v_ref, i)
        bias = bias_scr[jnp.minimum(i, 1)]
        tabs = (c_ref[...], sa_ref[...], sb_ref[...])
        dkw = jnp.zeros_like(kw)
        dvw = jnp.zeros_like(vw)
        for kvh in range(N_KV_HEADS):
            k_a, k_b = _spread(kw, kvh)
            v_a, v_b = _spread(vw, kvh)
            _, vjp = jax.vjp(functools.partial(_attn_group, bias=bias), _stack_pairs(q_ref, kvh).astype(F32),
                             k_a, v_a, k_b, v_b, _stack_pairs(z_ref, kvh), s_ref[kvh, 0], s_ref[kvh, 1])
            dqs, dk_a, dv_a, dk_b, dv_b, dzs, ds_a, ds_b = vjp(_stack_pairs(g_ref, kvh))
            dkw = dkw + _unspread(dk_a, dk_b, kvh)
            dvw = dvw + _unspread(dv_a, dv_b, kvh)
            ds_ref[kvh, 0] += jnp.sum(ds_a.reshape(PAIRS, ATTN_BLOCK, LANES), axis=1)
            ds_ref[kvh, 1] += jnp.sum(ds_b.reshape(PAIRS, ATTN_BLOCK, LANES), axis=1)
            for j in range(PAIRS):
                rows = slice(j * ATTN_BLOCK, (j + 1) * ATTN_BLOCK)
                col = (kvh * PAIRS + j) * LANES
                dp_ref[:, col:col + LANES] = _rope_transposed(dqs[rows] * (HEAD_DIM ** -0.5), *tabs).astype(BF16)
                zc = D_MODEL + 2 * KV_WIDTH + col
                dp_ref[:, zc:zc + LANES] = dzs[rows].astype(BF16)
        dp_ref[:, D_MODEL:D_MODEL + 2 * KV_WIDTH] = jnp.zeros((ATTN_BLOCK, 2 * KV_WIDTH), BF16)
        dk_ref[pl.ds(ps, ATTN_BLOCK), :] += dkw[:ATTN_BLOCK]
        dk_ref[pl.ds(cs, ATTN_BLOCK), :] += dkw[ATTN_BLOCK:]
        dv_ref[pl.ds(ps, ATTN_BLOCK), :] += dvw[:ATTN_BLOCK]
        dv_ref[pl.ds(cs, ATTN_BLOCK), :] += dvw[ATTN_BLOCK:]

    blk = lambda w: pl.BlockSpec((ATTN_BLOCK, w), lambda b, i: (b * nb + i, 0))
    seq_spec = pl.BlockSpec((seq, KV_WIDTH), lambda b, i: (b, 0))
    n = batch * seq
    ds_shape = (N_KV_HEADS, 2, PAIRS, LANES)
    return pl.pallas_call(
        body, name="attn_bwd", grid=(batch, nb),
        in_specs=[blk(D_MODEL), seq_spec, seq_spec, blk(D_MODEL), _full_spec(sink_tab.shape), blk(D_MODEL)]
        + [blk(LANES)] * 3,
        out_specs=[blk(ATTN_IN), seq_spec, seq_spec, _full_spec(ds_shape)],
        out_shape=[jax.ShapeDtypeStruct((n, ATTN_IN), BF16), jax.ShapeDtypeStruct((n, KV_WIDTH), F32),
                   jax.ShapeDtypeStruct((n, KV_WIDTH), F32), jax.ShapeDtypeStruct(ds_shape, F32)],
        scratch_shapes=[pltpu.VMEM((2, GROUP_ROWS, 2 * ATTN_BLOCK), F32)],
        compiler_params=_cparams(2),
    )(q, k, v, z, sink_tab, dog, *tables)


def attn_bwd_kv(dproj, dk, dv, tables, tm=512):
    n = dproj.shape[0]

    def body(dp_in_ref, dk_ref, dv_ref, c_ref, sa_ref, sb_ref, dp_ref):
        del dp_in_ref
        dp_ref[:, :KV_WIDTH] = _rope_transposed(dk_ref[...], c_ref[...], sa_ref[...], sb_ref[...]).astype(BF16)
        dp_ref[:, KV_WIDTH:] = dv_ref[...].astype(BF16)

    kv_cols = pl.BlockSpec((tm, 2 * KV_WIDTH), lambda i: (i, D_MODEL // (2 * KV_WIDTH)))
    return pl.pallas_call(
        body, name="attn_bwd_kv", grid=(n // tm,),
        in_specs=[kv_cols, _row_spec(tm, KV_WIDTH), _row_spec(tm, KV_WIDTH)] + [_row_spec(tm, LANES)] * 3,
        out_specs=kv_cols, out_shape=jax.ShapeDtypeStruct(dproj.shape, BF16),
        input_output_aliases={0: 0}, compiler_params=_cparams(1),
    )(dproj, dk, dv, *tables)


def _rec_cols(part, h):
    return slice(part * D_MODEL + h * REC_DIM, part * D_MODEL + (h + 1) * REC_DIM)


def _rec_args(p_ref, lb_ref, gw_ref, h, S):
    hs = slice(h * REC_DIM, (h + 1) * REC_DIM)
    return (p_ref[:, _rec_cols(0, h)], p_ref[:, _rec_cols(1, h)], p_ref[:, _rec_cols(2, h)],
            p_ref[:, _rec_cols(3, h)], S, lb_ref[0:1, hs], lb_ref[1:2, hs], gw_ref[...])


def rec_fwd(proj, lb_logits, gnorm_w, batch, seq):
    nblk = seq // REC_BLOCK

    def body(p_ref, lb_ref, gw_ref, og_ref, st_ref, safe_ref, s_scr):
        @pl.when(pl.program_id(1) == 0)
        def _():
            s_scr[...] = jnp.zeros_like(s_scr)

        heads, margin = [], None
        for h in range(REC_HEADS):
            S = s_scr[h]
            st_ref[0, h] = S
            qr, fr, v, _, _, l0, l1, _ = _rec_args(p_ref, lb_ref, gw_ref, h, S)
            q, k, lf = _rec_front(qr, fr, l0, l1)
            b = cumsum_rows(lf)
            heads.append((q, k, v, b, S))
            m = _rec_margin(b)
            margin = m if margin is None else jnp.minimum(margin, m)
        safe = jnp.min(margin) >= -SAFE_RANGE

        def run(core):
            return [core(*args) for args in heads]

        results = lax.cond(safe, functools.partial(run, _rec_core_fast), functools.partial(run, _rec_core_slow))
        for h, (o, S_new) in enumerate(results):
            hs = slice(h * REC_DIM, (h + 1) * REC_DIM)
            og_ref[:, hs] = _rec_tail(o, p_ref[:, _rec_cols(3, h)], gw_ref[...]).astype(BF16)
            s_scr[h] = S_new
        safe_ref[0] = jnp.full((REC_HEADS, LANES), safe.astype(F32))

    blk = lambda w: pl.BlockSpec((REC_BLOCK, w), lambda b, j: (b * nblk + j, 0))
    st_spec = pl.BlockSpec((1, REC_HEADS, REC_DIM, REC_DIM), lambda b, j: (b * nblk + j, 0, 0, 0))
    safe_spec = pl.BlockSpec((1, REC_HEADS, LANES), lambda b, j: (b * nblk + j, 0, 0))
    return pl.pallas_call(
        body, name="rec_fwd", grid=(batch, nblk),
        in_specs=[blk(REC_IN), _full_spec((2, D_MODEL)), _full_spec((1, REC_DIM))],
        out_specs=[blk(D_MODEL), st_spec, safe_spec],
        out_shape=[jax.ShapeDtypeStruct((batch * seq, D_MODEL), BF16),
                   jax.ShapeDtypeStruct((batch * nblk, REC_HEADS, REC_DIM, REC_DIM), F32),
                   jax.ShapeDtypeStruct((batch * nblk, REC_HEADS, LANES), F32)],
        scratch_shapes=[pltpu.VMEM((REC_HEADS, REC_DIM, REC_DIM), F32)],
        compiler_params=_cparams(2),
    )(proj, lb_logits, gnorm_w)


def rec_bwd(proj, states, safe, lb_logits, gnorm_w, dog, batch, seq):
    nblk = seq // REC_BLOCK

    def body(p_ref, st_ref, safe_ref, lb_ref, gw_ref, g_ref, dp_ref, dlb_ref, dgw_ref, ds_scr):
        @pl.when((pl.program_id(0) == 0) & (pl.program_id(1) == 0))
        def _():
            dlb_ref[...] = jnp.zeros_like(dlb_ref)
            dgw_ref[...] = jnp.zeros_like(dgw_ref)

        @pl.when(pl.program_id(1) == 0)
        def _():
            ds_scr[...] = jnp.zeros_like(ds_scr)

        def pull_back(core):
            out = []
            for h in range(REC_HEADS):
                hs = slice(h * REC_DIM, (h + 1) * REC_DIM)
                primals = _rec_args(p_ref, lb_ref, gw_ref, h, st_ref[0, h])
                out.append(jax.vjp(functools.partial(_rec_head, core), *primals)[1]((g_ref[:, hs], ds_scr[h])))
            return out

        results = lax.cond(jnp.max(safe_ref[0]) > 0.5, functools.partial(pull_back, _rec_core_fast),
                           functools.partial(pull_back, _rec_core_slow))
        for h, (dqr, dfr, dv, dz, dS, dl0, dl1, dgw) in enumerate(results):
            hs = slice(h * REC_DIM, (h + 1) * REC_DIM)
            for part, val in enumerate((dqr, dfr, dv, dz)):
                dp_ref[:, _rec_cols(part, h)] = val.astype(BF16)
            ds_scr[h] = dS
            dlb_ref[0:1, hs] += dl0
            dlb_ref[1:2, hs] += dl1
            dgw_ref[...] += dgw

    blk = lambda w: pl.BlockSpec((REC_BLOCK, w), lambda b, j: (b * nblk + nblk - 1 - j, 0))
    st_spec = pl.BlockSpec((1, REC_HEADS, REC_DIM, REC_DIM), lambda b, j: (b * nblk + nblk - 1 - j, 0, 0, 0))
    safe_spec = pl.BlockSpec((1, REC_HEADS, LANES), lambda b, j: (b * nblk + nblk - 1 - j, 0, 0))
    return pl.pallas_call(
        body, name="rec_bwd", grid=(batch, nblk),
        in_specs=[blk(REC_IN), st_spec, safe_spec, _full_spec((2, D_MODEL)), _full_spec((1, REC_DIM)),
                  blk(D_MODEL)],
        out_specs=[blk(REC_IN), _full_spec((2, D_MODEL)), _full_spec((1, REC_DIM))],
        out_shape=[jax.ShapeDtypeStruct((batch * seq, REC_IN), BF16), jax.ShapeDtypeStruct((2, D_MODEL), F32),
                   jax.ShapeDtypeStruct((1, REC_DIM), F32)],
        scratch_shapes=[pltpu.VMEM((REC_HEADS, REC_DIM, REC_DIM), F32)],
        compiler_params=_cparams(2),
    )(proj, states, safe, lb_logits, gnorm_w, dog)


_ANY = pl.BlockSpec(memory_space=pl.ANY)


def _chip_peers():
    x, y, c = lax.axis_index("x"), lax.axis_index("y"), lax.axis_index("c")
    flips = ((1, 0), (0, 1), (1, 1))
    peers = [(jnp.where(fx, 1 - x, x), jnp.where(fy, 1 - y, y), c) for fx, fy in flips]
    return 2 * x + y, peers


def chip_exchange(arrays, scatter):
    n = len(arrays)

    def body(*refs):
        ins, outs = refs[:n], refs[n:2 * n]
        send_sems, recv_sems, local_sems = refs[2 * n:]
        me, peers = _chip_peers()
        local = [pltpu.make_async_copy(ins[k].at[me] if scatter else ins[k], outs[k].at[me], local_sems.at[k])
                 for k in range(n)]
        for cp in local:
            cp.start()
        sends, recvs = [], []
        for k in range(n):
            for j, (px, py, pc) in enumerate(peers):
                idx = 2 * px + py
                sem = k * len(peers) + j
                sends.append(pltpu.make_async_remote_copy(
                    src_ref=ins[k].at[idx] if scatter else ins[k], dst_ref=outs[k].at[me],
                    send_sem=send_sems.at[sem], recv_sem=recv_sems.at[sem],
                    device_id=(px, py, pc), device_id_type=MESH))
                recvs.append(pltpu.make_async_remote_copy(
                    src_ref=ins[k].at[me] if scatter else ins[k], dst_ref=outs[k].at[idx],
                    send_sem=send_sems.at[sem], recv_sem=recv_sems.at[sem],
                    device_id=(px, py, pc), device_id_type=MESH))
        for cp in sends:
            cp.start()
        for cp in recvs:
            cp.wait_recv()
        for cp in sends:
            cp.wait_send()
        for cp in local:
            cp.wait()

    out_shape = [jax.ShapeDtypeStruct(a.shape if scatter else (N_CHIPS,) + a.shape, a.dtype) for a in arrays]
    n_copies = n * (N_CHIPS - 1)
    return pl.pallas_call(
        body, name="chip_scatter" if scatter else "chip_gather",
        in_specs=[_ANY] * n, out_specs=[_ANY] * n, out_shape=out_shape,
        scratch_shapes=[pltpu.SemaphoreType.DMA((n_copies,)), pltpu.SemaphoreType.DMA((n_copies,)),
                        pltpu.SemaphoreType.DMA((n,))],
    )(*arrays)


def sibling_exchange(arrays):
    n = len(arrays)

    def body(*refs):
        ins, outs = refs[:n], refs[n:2 * n]
        send_sems, recv_sems = refs[2 * n:]
        sibling = (lax.axis_index("x"), lax.axis_index("y"), 1 - lax.axis_index("c"))
        copies = [pltpu.make_async_remote_copy(src_ref=ins[k], dst_ref=outs[k], send_sem=send_sems.at[k],
                                               recv_sem=recv_sems.at[k], device_id=sibling, device_id_type=MESH)
                  for k in range(n)]
        for cp in copies:
            cp.start()
        for cp in copies:
            cp.wait()

    return pl.pallas_call(
        body, name="sibling_exchange", in_specs=[_ANY] * n, out_specs=[_ANY] * n,
        out_shape=[jax.ShapeDtypeStruct(a.shape, a.dtype) for a in arrays],
        scratch_shapes=[pltpu.SemaphoreType.DMA((n,)), pltpu.SemaphoreType.DMA((n,))],
    )(*arrays)


def all_gather_small(vec):
    def body(v_ref, out_ref, send_sems, recv_sems, local_sem):
        x, y, c = lax.axis_index("x"), lax.axis_index("y"), lax.axis_index("c")
        me = 4 * x + 2 * y + c
        local = pltpu.make_async_copy(v_ref, out_ref.at[me], local_sem)
        local.start()
        sends, recvs = [], []
        for j in range(1, N_DEV):
            px = jnp.where(j & 4, 1 - x, x)
            py = jnp.where(j & 2, 1 - y, y)
            pc = jnp.where(j & 1, 1 - c, c)
            common = dict(send_sem=send_sems.at[j - 1], recv_sem=recv_sems.at[j - 1], device_id=(px, py, pc),
                          device_id_type=MESH)
            sends.append(pltpu.make_async_remote_copy(src_ref=v_ref, dst_ref=out_ref.at[me], **common))
            recvs.append(pltpu.make_async_remote_copy(src_ref=v_ref, dst_ref=out_ref.at[4 * px + 2 * py + pc],
                                                      **common))
        for cp in sends:
            cp.start()
        for cp in recvs:
            cp.wait_recv()
        for cp in sends:
            cp.wait_send()
        local.wait()

    return pl.pallas_call(
        body, name="all_gather_small", in_specs=[_ANY], out_specs=_ANY,
        out_shape=jax.ShapeDtypeStruct((N_DEV,) + vec.shape, vec.dtype),
        scratch_shapes=[pltpu.SemaphoreType.DMA((N_DEV - 1,)), pltpu.SemaphoreType.DMA((N_DEV - 1,)),
                        pltpu.SemaphoreType.DMA],
    )(vec)


def sum_slots(stacked, tm=256):
    s, r, c = stacked.shape
    tm = min(tm, r)

    def body(in_ref, out_ref):
        acc = in_ref[0].astype(F32)
        for t in range(1, s):
            acc = acc + in_ref[t].astype(F32)
        out_ref[...] = acc

    return pl.pallas_call(
        body, name=f"sum_slots_{s}_{r}_{c}", grid=(r // tm,),
        in_specs=[pl.BlockSpec((s, tm, c), lambda i: (0, i, 0))], out_specs=_row_spec(tm, c),
        out_shape=jax.ShapeDtypeStruct((r, c), F32), compiler_params=_cparams(1),
    )(stacked)


def adamw(w, m, v, g_a, g_b=None, tm=256):
    r, c = w.shape
    tm = min(tm, r)
    two = g_b is not None

    def body(*refs):
        w_ref, m_ref, v_ref, ga_ref = refs[:4]
        g_ref, d_ref, nm_ref, nv_ref = refs[-4:]
        g = ga_ref[...] + refs[4][...] if two else ga_ref[...]
        nm = ADAM_B1 * m_ref[...] + (1.0 - ADAM_B1) * g
        nv = ADAM_B2 * v_ref[...] + (1.0 - ADAM_B2) * (g * g)
        m_hat = nm / (1.0 - ADAM_B1 ** ADAM_STEP)
        v_hat = nv / (1.0 - ADAM_B2 ** ADAM_STEP)
        g_ref[...] = g
        d_ref[...] = -ADAM_LR * (m_hat / (jnp.sqrt(v_hat) + ADAM_EPS) + ADAM_WD * w_ref[...])
        nm_ref[...] = nm
        nv_ref[...] = nv

    args = [w, m, v, g_a] + ([g_b] if two else [])
    return pl.pallas_call(
        body, name=f"adamw_{r}_{c}", grid=(r // tm,),
        in_specs=[_row_spec(tm, c)] * len(args), out_specs=[_row_spec(tm, c)] * 4,
        out_shape=[jax.ShapeDtypeStruct((r, c), F32)] * 4, compiler_params=_cparams(1),
    )(*args)


_SMALL = (("pre_norm_w", (2, D_MODEL)), ("post_norm_w", (2, D_MODEL)), ("attn_b_in", (1, ATTN_IN)),
          ("attn_sinks", (1, N_HEADS)), ("attn_b_out", (1, D_MODEL)), ("rec_lb_logits", (2, D_MODEL)),
          ("rec_gnorm_w", (1, REC_DIM)))
_SMALL_ROWS = 16


def _pack_small(parts):
    rows = []
    for (name, shape) in _SMALL:
        flat = parts[name].reshape(-1)
        pad = -flat.shape[0] % D_MODEL
        rows.append(jnp.pad(flat, (0, pad)).reshape(-1, D_MODEL))
    packed = jnp.concatenate(rows, axis=0)
    return jnp.pad(packed, ((0, _SMALL_ROWS - packed.shape[0]), (0, 0)))


def _unpack_small(packed):
    out, row = {}, 0
    for (name, shape) in _SMALL:
        size = shape[0] * shape[1]
        nrows = -(-size // D_MODEL)
        out[name] = packed[row:row + nrows].reshape(-1)[:size].reshape(shape)
        row += nrows
    return out


def local_step(x, positions, pre_norm_w, post_norm_w, attn_w_in, attn_b_in, attn_sinks, attn_w_out, attn_b_out,
               rec_w_in, rec_lb_logits, rec_gnorm_w, rec_w_out, loss_target):
    batch, seq, _ = x.shape
    n = batch * seq
    x0 = x.reshape(n, D_MODEL)
    tables = _rope_tables(positions)
    pre0, pre1 = pre_norm_w[0:1], pre_norm_w[1:2]
    post0, post1 = post_norm_w[0:1], post_norm_w[1:2]
    no_bias = jnp.zeros((1, D_MODEL), F32)

    h0, q, k, v, z = attn_in_proj(x0, pre0, attn_w_in, attn_b_in, tables)
    sink_tab = _sink_table(attn_sinks)
    og0 = attn_fwd(q, k, v, z, sink_tab, batch, seq)
    y0, x1 = out_proj(og0, attn_w_out, attn_b_out, x0, post0)

    h1, proj1 = rec_in_proj(x1, pre1, rec_w_in)
    og1, states, safe = rec_fwd(proj1, rec_lb_logits, rec_gnorm_w, batch, seq)
    y1, dx2, loss_vec = out_proj(og1, rec_w_out, no_bias, x1, post1, target=loss_target.reshape(n, D_MODEL))
    loss = jnp.sum(loss_vec) * (0.5 / D_MODEL)

    dog1, d_rec_w_out, _, d_post1 = out_proj_bwd(dx2, y1, og1, rec_w_out, post1)
    dproj1, d_lb, d_gnorm = rec_bwd(proj1, states, safe, rec_lb_logits, rec_gnorm_w, dog1, batch, seq)
    dx1, d_pre1 = in_proj_bwd_x(dproj1, rec_w_in, x1, pre1, dx2)
    d_rec_w_in, _ = in_proj_bwd_w(h1, dproj1)

    dog0, d_attn_w_out, d_attn_b_out, d_post0 = out_proj_bwd(dx1, y0, og0, attn_w_out, post0)
    dproj0, dk, dv, d_sink_tab = attn_bwd(q, k, v, z, sink_tab, dog0, tables, batch, seq)
    d_sinks = jnp.transpose(jnp.sum(d_sink_tab, axis=-1), (0, 2, 1)).reshape(1, N_HEADS)
    dproj0 = attn_bwd_kv(dproj0, dk, dv, tables)
    dx0, d_pre0 = in_proj_bwd_x(dproj0, attn_w_in, x0, pre0, dx1)
    d_attn_w_in, d_attn_b_in = in_proj_bwd_w(h0, dproj0)

    grads = dict(
        pre_norm_w=jnp.concatenate([d_pre0, d_pre1], axis=0), post_norm_w=jnp.concatenate([d_post0, d_post1], axis=0),
        attn_w_in=d_attn_w_in, attn_b_in=d_attn_b_in, attn_sinks=d_sinks, attn_w_out=d_attn_w_out,
        attn_b_out=d_attn_b_out, rec_w_in=d_rec_w_in, rec_lb_logits=d_lb, rec_gnorm_w=d_gnorm,
        rec_w_out=d_rec_w_out)
    return loss, dx0.reshape(batch, seq, D_MODEL), grads


_BIG = ("attn_w_in", "attn_w_out", "rec_w_in", "rec_w_out")
_COLUMN_SHARDED = ("attn_w_in", "rec_w_in")
_ORDER = ("pre_norm_w", "post_norm_w", "attn_w_in", "attn_b_in", "attn_sinks", "attn_w_out", "attn_b_out",
          "rec_w_in", "rec_lb_logits", "rec_gnorm_w", "rec_w_out")


def _whole_from_shards(name, stacked):
    if name in _COLUMN_SHARDED:
        return jnp.transpose(stacked, (1, 0, 2)).reshape(stacked.shape[1], -1)
    return stacked.reshape(-1, stacked.shape[2])


def _shards_from_whole(name, whole):
    if name in _COLUMN_SHARDED:
        return jnp.transpose(whole.reshape(whole.shape[0], N_CHIPS, -1), (1, 0, 2))
    return whole.reshape(N_CHIPS, -1, whole.shape[1])


def kernel(x, positions, pre_norm_w, post_norm_w, attn_w_in, attn_b_in, attn_sinks, attn_w_out, attn_b_out, rec_w_in, rec_lb_logits, rec_gnorm_w, rec_w_out, loss_target, m_pre_norm_w, m_post_norm_w, m_attn_w_in, m_attn_b_in, m_attn_sinks, m_attn_w_out, m_attn_b_out, m_rec_w_in, m_rec_lb_logits, m_rec_gnorm_w, m_rec_w_out, v_pre_norm_w, v_post_norm_w, v_attn_w_in, v_attn_b_in, v_attn_sinks, v_attn_w_out, v_attn_b_out, v_rec_w_in, v_rec_lb_logits, v_rec_gnorm_w, v_rec_w_out):
    w = dict(pre_norm_w=pre_norm_w, post_norm_w=post_norm_w, attn_w_in=attn_w_in, attn_b_in=attn_b_in,
             attn_sinks=attn_sinks, attn_w_out=attn_w_out, attn_b_out=attn_b_out, rec_w_in=rec_w_in,
             rec_lb_logits=rec_lb_logits, rec_gnorm_w=rec_gnorm_w, rec_w_out=rec_w_out)
    m = dict(pre_norm_w=m_pre_norm_w, post_norm_w=m_post_norm_w, attn_w_in=m_attn_w_in, attn_b_in=m_attn_b_in,
             attn_sinks=m_attn_sinks, attn_w_out=m_attn_w_out, attn_b_out=m_attn_b_out, rec_w_in=m_rec_w_in,
             rec_lb_logits=m_rec_lb_logits, rec_gnorm_w=m_rec_gnorm_w, rec_w_out=m_rec_w_out)
    v = dict(pre_norm_w=v_pre_norm_w, post_norm_w=v_post_norm_w, attn_w_in=v_attn_w_in, attn_b_in=v_attn_b_in,
             attn_sinks=v_attn_sinks, attn_w_out=v_attn_w_out, attn_b_out=v_attn_b_out, rec_w_in=v_rec_w_in,
             rec_lb_logits=v_rec_lb_logits, rec_gnorm_w=v_rec_gnorm_w, rec_w_out=v_rec_w_out)

    shards = {name: w[name][0] for name in _BIG}
    gathered = chip_exchange([shards[name].astype(BF16) for name in _BIG], scatter=False)
    whole = {name: _whole_from_shards(name, g) for name, g in zip(_BIG, gathered)}

    loss, grad_x, grads = local_step(
        x, positions, pre_norm_w, post_norm_w, whole["attn_w_in"], attn_b_in, attn_sinks, whole["attn_w_out"],
        attn_b_out, whole["rec_w_in"], rec_lb_logits, rec_gnorm_w, whole["rec_w_out"], loss_target)
    loss = lax.psum(loss, ("x", "y", "c"))

    parts = chip_exchange([_shards_from_whole(name, grads[name]).astype(BF16) for name in _BIG], scatter=True)
    plane_sums = [sum_slots(p) for p in parts]
    other_sums = sibling_exchange(plane_sums)
    out_g, out_d, out_m, out_v = {}, {}, {}, {}
    for name, mine, other in zip(_BIG, plane_sums, other_sums):
        g, d, nm, nv = adamw(shards[name], m[name][0], v[name][0], mine, other)
        out_g[name], out_d[name], out_m[name], out_v[name] = g[None], d[None], nm[None], nv[None]

    small_sum = sum_slots(all_gather_small(_pack_small(grads)))
    packed = adamw(_pack_small(w), _pack_small(m), _pack_small(v), small_sum)
    for dst, val in zip((out_g, out_d, out_m, out_v), packed):
        dst.update(_unpack_small(val))

    return (loss, grad_x, *[out_g[n] for n in _ORDER], *[out_d[n] for n in _ORDER],
            *[out_m[n] for n in _ORDER], *[out_v[n] for n in _ORDER])
```

```python
import functools

import jax
import jax.numpy as jnp
from jax import lax
from jax.experimental import pallas as pl
from jax.experimental.pallas import tpu as pltpu

F32 = jnp.float32
BF16 = jnp.bfloat16
MESH = pl.DeviceIdType.MESH

D_MODEL = 1024
HEAD_DIM = 64
N_HEADS = 16
N_KV_HEADS = 2
GROUP = N_HEADS // N_KV_HEADS
KV_WIDTH = N_KV_HEADS * HEAD_DIM
ATTN_IN = 2 * D_MODEL + 2 * KV_WIDTH
ATTN_BLOCK = 128
ROPE_THETA = 500000.0
ROPE_DIM = HEAD_DIM // 4
REC_HEADS = 8
REC_DIM = 128
REC_IN = 4 * D_MODEL
REC_BLOCK = 128
DIAG = 8
NORM_EPS = 1e-6
N_CHIPS = 4
N_DEV = 8
LANES = 128

ADAM_LR = 0.001
ADAM_B1 = 0.9
ADAM_B2 = 0.999
ADAM_EPS = 1e-08
ADAM_WD = 0.01
ADAM_STEP = 10

VMEM_LIMIT = 56 * 1024 * 1024


def _cparams(n_axes):
    return pltpu.CompilerParams(dimension_semantics=("arbitrary",) * n_axes, vmem_limit_bytes=VMEM_LIMIT)


def _dot(a, b, contract):
    return lax.dot_general(a.astype(BF16), b.astype(BF16), (contract, ((), ())), preferred_element_type=F32)


_NN = ((1,), (0,))
_NT = ((1,), (1,))
_TN = ((0,), (0,))


@jax.custom_vjp
def mm_nn(a, b):
    return _dot(a, b, _NN)


mm_nn.defvjp(lambda a, b: (_dot(a, b, _NN), (a, b)),
             lambda res, g: (_dot(g, res[1], _NT), _dot(res[0], g, _TN)))


@jax.custom_vjp
def mm_nt(a, b):
    return _dot(a, b, _NT)


mm_nt.defvjp(lambda a, b: (_dot(a, b, _NT), (a, b)),
             lambda res, g: (_dot(g, res[1], _NN), _dot(g, res[0], _TN)))


@jax.custom_vjp
def mm_tn(a, b):
    return _dot(a, b, _TN)


mm_tn.defvjp(lambda a, b: (_dot(a, b, _TN), (a, b)),
             lambda res, g: (_dot(res[1], g, _NT), _dot(res[0], g, _NN)))


def _tri_dot(x, lower):
    n = x.shape[0]
    r = lax.broadcasted_iota(jnp.int32, (n, n), 0)
    c = lax.broadcasted_iota(jnp.int32, (n, n), 1)
    tri = ((c <= r) if lower else (c >= r)).astype(BF16)
    hi = x.astype(BF16)
    rest = x - hi.astype(F32)
    mid = rest.astype(BF16)
    lo = (rest - mid.astype(F32)).astype(BF16)
    dot = lambda p: lax.dot_general(tri, p, (_NN, ((), ())), preferred_element_type=F32)
    return (dot(lo) + dot(mid)) + dot(hi)


@jax.custom_vjp
def cumsum_rows(x):
    return _tri_dot(x, True)


cumsum_rows.defvjp(lambda x: (cumsum_rows(x), None), lambda _, g: (_tri_dot(g, False),))


@functools.partial(jax.custom_vjp, nondiff_argnums=(1,))
def roll_sub(x, d):
    return pltpu.roll(x, d, 1) if d else x


roll_sub.defvjp(lambda x, d: (roll_sub(x, d), None),
                lambda d, _, g: (roll_sub(g, (DIAG - d) % DIAG),))


def sigmoid(x):
    return 0.5 * (jnp.tanh(0.5 * x) + 1.0)


@jax.custom_vjp
def silu(x):
    return x * sigmoid(x)


def _silu_fwd(x):
    s = sigmoid(x)
    return x * s, (x, s)


silu.defvjp(_silu_fwd, lambda res, g: (g * (res[1] * (1.0 + res[0] * (1.0 - res[1]))),))


def log_sigmoid_pair(x):
    t = jnp.log(1.0 + jnp.exp(-jnp.abs(x)))
    return jnp.minimum(x, 0.0) - t, jnp.minimum(-x, 0.0) - t


def _forget_fwd(x, a):
    log_lb, log_1m_lb = log_sigmoid_pair(a)
    ls_f, ls_nf = log_sigmoid_pair(x)
    c = log_1m_lb + ls_f
    lf = jnp.maximum(log_lb, c) + jnp.log(1.0 + jnp.exp(-jnp.abs(log_lb - c)))
    k = jnp.exp(log_1m_lb + ls_nf)
    return (lf, k), (log_lb, log_1m_lb, ls_f, ls_nf, c, lf, k)


def _forget_bwd(res, g):
    log_lb, log_1m_lb, ls_f, ls_nf, c, lf, k = res
    g_lf, g_k = g
    wc = jnp.exp(jnp.minimum(c - lf, 0.0))
    gk = g_k * k
    dx = g_lf * wc * jnp.exp(ls_nf) - gk * jnp.exp(ls_f)
    lb = jnp.exp(log_lb)
    da = jnp.sum(g_lf * ((1.0 - wc) * jnp.exp(log_1m_lb) - wc * lb) - gk * lb, axis=0, keepdims=True)
    return dx, da


@jax.custom_vjp
def forget_gate(x, a):
    return _forget_fwd(x, a)[0]


forget_gate.defvjp(_forget_fwd, _forget_bwd)


@jax.custom_vjp
def decayed(x, e):
    return (x * jnp.exp(e)).astype(BF16).astype(F32)


def _decayed_fwd(x, e):
    y = decayed(x, e)
    return y, (y, e)


decayed.defvjp(_decayed_fwd, lambda res, g: (g * jnp.exp(res[1]), g * res[0]))


def _row(x, r):
    shape = x.shape

    @jax.custom_vjp
    def take(x):
        return x[r:r + 1, :]

    take.defvjp(lambda x: (x[r:r + 1, :], None),
                lambda _, g: (jnp.where(lax.broadcasted_iota(jnp.int32, shape, 0) == r, g, 0.0),))
    return take(x)


def _rms(x):
    return lax.rsqrt(jnp.mean(x * x, axis=-1, keepdims=True) + NORM_EPS)


def _attn_group(qs, k_a, v_a, k_b, v_b, zs, sink_a, sink_b, bias):
    def half(kh, vh, sink):
        s = mm_nt(qs, kh) + bias
        m = lax.stop_gradient(jnp.maximum(jnp.max(s, axis=-1, keepdims=True), jnp.max(sink, axis=-1, keepdims=True)))
        p = jnp.exp(s - m)
        denom = jnp.sum(p, axis=-1, keepdims=True) + jnp.sum(jnp.exp(sink - m), axis=-1, keepdims=True) * (1.0 / LANES)
        return mm_nn(p * (1.0 / denom), vh)

    return (half(k_a, v_a, sink_a) + half(k_b, v_b, sink_b)) * silu(zs)


SAFE_RANGE = 80.0


def _rec_front(qr, fr, l0, l1):
    lf, k = forget_gate(fr, l1 - l0)
    return silu(qr), k, lf


def _rec_tail(o, z, gw):
    return o * _rms(o) * gw * silu(z)


def _rec_margin(b):
    R = b.shape[0]
    mid, last = _row(b, R // 2 - 1), _row(b, R - 1)
    return jnp.minimum(mid, last - mid)


def _rec_core_fast(q, k, v, b, S):
    R = q.shape[0]
    ri = lax.broadcasted_iota(jnp.int32, (R, R), 0)
    ci = lax.broadcasted_iota(jnp.int32, (R, R), 1)
    d = b - _row(b, R // 2 - 1)
    sc = jnp.where(ci < ri, mm_nt(decayed(q, d), decayed(k, -d)), 0.0)
    o = mm_nt(q * jnp.exp(b), S) + mm_nn(sc, v) + jnp.sum(q * k, axis=-1, keepdims=True) * v
    b_last = _row(b, R - 1)
    return o, S * jnp.exp(b_last) + mm_tn(v, k * jnp.exp(b_last - b))


def _rec_core_slow(q, k, v, b, S):
    R = q.shape[0]
    rows = lax.broadcasted_iota(jnp.int32, (R, REC_DIM), 0)

    o = mm_nt(q * jnp.exp(jnp.minimum(b, 0.0)), S)

    ri = lax.broadcasted_iota(jnp.int32, (R, R), 0)
    ci = lax.broadcasted_iota(jnp.int32, (R, R), 1)
    sc = jnp.zeros((R, R), F32)
    w = R
    while w > DIAG:
        h = w // 2
        b3 = b.reshape(R // w, w, REC_DIM)
        rin = lax.broadcasted_iota(jnp.int32, (R // w, w, REC_DIM), 1)
        mid = jnp.sum(jnp.where(rin == h - 1, b3, 0.0), axis=1, keepdims=True)
        fac = jnp.exp(jnp.minimum(jnp.where(rin >= h, b3 - mid, mid - b3), 0.0)).reshape(R, REC_DIM)
        upper = (rows % w) >= h
        s_w = mm_nt(jnp.where(upper, q * fac, 0.0), jnp.where(upper, 0.0, k * fac))
        sc = sc + jnp.where((ri // w) == (ci // w), s_w, 0.0)
        w = h
    o = o + mm_nn(sc, v)

    g = R // DIAG
    q3, k3, v3, b3 = (t.reshape(g, DIAG, REC_DIM) for t in (q, k, v, b))
    rin = lax.broadcasted_iota(jnp.int32, (g, DIAG, 1), 1)
    od = jnp.zeros((g, DIAG, REC_DIM), F32)
    for d in range(DIAG):
        e = jnp.exp(jnp.minimum(b3 - roll_sub(b3, d), 0.0))
        sd = jnp.sum(q3 * roll_sub(k3, d) * e, axis=-1, keepdims=True)
        od = od + jnp.where(rin >= d, sd, 0.0) * roll_sub(v3, d)
    o = o + od.reshape(R, REC_DIM)

    b_last = _row(b, R - 1)
    return o, S * jnp.exp(jnp.minimum(b_last, 0.0)) + mm_tn(v, k * jnp.exp(jnp.minimum(b_last - b, 0.0)))


def _rec_head(core, qr, fr, v, z, S, l0, l1, gw):
    q, k, lf = _rec_front(qr, fr, l0, l1)
    o, S_new = core(q, k, v, cumsum_rows(lf), S)
    return _rec_tail(o, z, gw), S_new


def _rope_tables(positions):
    half = ROPE_DIM // 2
    inv_freq = ROPE_THETA ** (-(jnp.arange(half, dtype=F32) * 2.0 / ROPE_DIM))
    rest = jnp.zeros((HEAD_DIM - ROPE_DIM,), F32)
    ones, zeros = jnp.ones((half,), F32), jnp.zeros((half,), F32)
    per_lane = lambda first, second: jnp.tile(jnp.concatenate([first, second, rest]), LANES // HEAD_DIM)[None, :]
    ang = positions.astype(F32).reshape(-1, 1) * per_lane(inv_freq, inv_freq)
    sin = jnp.sin(ang)
    return jnp.cos(ang), sin * per_lane(zeros, ones), sin * per_lane(-ones, zeros)


def _rope(x, cos_t, sin_a, sin_b):
    half = ROPE_DIM // 2
    return x * cos_t + pltpu.roll(x, half, 1) * sin_a + pltpu.roll(x, LANES - half, 1) * sin_b


def _rope_transposed(g, cos_t, sin_a, sin_b):
    half = ROPE_DIM // 2
    return g * cos_t + pltpu.roll(g * sin_a, LANES - half, 1) + pltpu.roll(g * sin_b, half, 1)


def _row_spec(tm, width):
    return pl.BlockSpec((tm, width), lambda i: (i, 0))


def _full_spec(shape):
    return pl.BlockSpec(shape, lambda *_: (0,) * len(shape))


def attn_in_proj(x, w_pre, w_in, b_in, tables, tm=512):
    n = x.shape[0]

    def body(x_ref, wp_ref, w_ref, b_ref, c_ref, sa_ref, sb_ref, h_ref, q_ref, k_ref, v_ref, z_ref):
        xv = x_ref[...]
        h = (xv * _rms(xv) * wp_ref[...]).astype(BF16)
        h_ref[...] = h
        proj = jnp.dot(h, w_ref[...], preferred_element_type=F32) + b_ref[...]
        tabs = (c_ref[...], sa_ref[...], sb_ref[...])
        for s in range(D_MODEL // LANES):
            sl = slice(s * LANES, (s + 1) * LANES)
            q_ref[:, sl] = _rope(proj[:, sl] * (HEAD_DIM ** -0.5), *tabs).astype(BF16)
        k_ref[...] = _rope(proj[:, D_MODEL:D_MODEL + KV_WIDTH], *tabs).astype(BF16)
        v_ref[...] = proj[:, D_MODEL + KV_WIDTH:D_MODEL + 2 * KV_WIDTH].astype(BF16)
        z_ref[...] = proj[:, D_MODEL + 2 * KV_WIDTH:]

    return pl.pallas_call(
        body, name="attn_in_proj", grid=(n // tm,),
        in_specs=[_row_spec(tm, D_MODEL), _full_spec((1, D_MODEL)), _full_spec((D_MODEL, ATTN_IN)),
                  _full_spec((1, ATTN_IN))] + [_row_spec(tm, LANES)] * 3,
        out_specs=[_row_spec(tm, D_MODEL), _row_spec(tm, D_MODEL), _row_spec(tm, KV_WIDTH),
                   _row_spec(tm, KV_WIDTH), _row_spec(tm, D_MODEL)],
        out_shape=[jax.ShapeDtypeStruct((n, D_MODEL), BF16), jax.ShapeDtypeStruct((n, D_MODEL), BF16),
                   jax.ShapeDtypeStruct((n, KV_WIDTH), BF16), jax.ShapeDtypeStruct((n, KV_WIDTH), BF16),
                   jax.ShapeDtypeStruct((n, D_MODEL), F32)],
        compiler_params=_cparams(1),
    )(x, w_pre, w_in, b_in, *tables)


def rec_in_proj(x, w_pre, w_in, tm=256):
    n = x.shape[0]

    def body(x_ref, wp_ref, w_ref, h_ref, p_ref):
        xv = x_ref[...]
        h = (xv * _rms(xv) * wp_ref[...]).astype(BF16)
        h_ref[...] = h
        p_ref[...] = jnp.dot(h, w_ref[...], preferred_element_type=F32)

    return pl.pallas_call(
        body, name="rec_in_proj", grid=(n // tm,),
        in_specs=[_row_spec(tm, D_MODEL), _full_spec((1, D_MODEL)), _full_spec((D_MODEL, REC_IN))],
        out_specs=[_row_spec(tm, D_MODEL), _row_spec(tm, REC_IN)],
        out_shape=[jax.ShapeDtypeStruct((n, D_MODEL), BF16), jax.ShapeDtypeStruct((n, REC_IN), F32)],
        compiler_params=_cparams(1),
    )(x, w_pre, w_in)


def out_proj(og, w_out, b_out, x_res, w_post, target=None, tm=512):
    n = og.shape[0]
    with_loss = target is not None

    def body(*refs):
        if with_loss:
            og_ref, w_ref, b_ref, x_ref, wp_ref, t_ref, y_ref, dx_ref, l_ref = refs
        else:
            og_ref, w_ref, b_ref, x_ref, wp_ref, y_ref, xo_ref = refs
        y = jnp.dot(og_ref[...], w_ref[...], preferred_element_type=F32) + b_ref[...]
        y_ref[...] = y
        xo = x_ref[...] + y * _rms(y) * wp_ref[...]
        if with_loss:
            err = xo - t_ref[...]
            dx_ref[...] = err * (1.0 / D_MODEL)

            @pl.when(pl.program_id(0) == 0)
            def _():
                l_ref[...] = jnp.zeros_like(l_ref)

            l_ref[...] += jnp.sum(err * err, axis=0, keepdims=True)
        else:
            xo_ref[...] = xo

    in_specs = [_row_spec(tm, D_MODEL), _full_spec((D_MODEL, D_MODEL)), _full_spec((1, D_MODEL)),
                _row_spec(tm, D_MODEL), _full_spec((1, D_MODEL))]
    out_specs = [_row_spec(tm, D_MODEL), _row_spec(tm, D_MODEL)]
    out_shape = [jax.ShapeDtypeStruct((n, D_MODEL), F32), jax.ShapeDtypeStruct((n, D_MODEL), F32)]
    args = [og, w_out, b_out, x_res, w_post]
    if with_loss:
        in_specs.append(_row_spec(tm, D_MODEL))
        out_specs.append(_full_spec((1, D_MODEL)))
        out_shape.append(jax.ShapeDtypeStruct((1, D_MODEL), F32))
        args.append(target)
    return pl.pallas_call(
        body, name="out_proj_loss" if with_loss else "out_proj", grid=(n // tm,),
        in_specs=in_specs, out_specs=out_specs, out_shape=out_shape, compiler_params=_cparams(1),
    )(*args)


def out_proj_bwd(dxo, y, og, w_out, w_post, tm=512):
    n = og.shape[0]

    def body(g_ref, y_ref, og_ref, w_ref, wp_ref, dog_ref, dw_ref, db_ref, dwp_ref):
        @pl.when(pl.program_id(0) == 0)
        def _():
            dw_ref[...] = jnp.zeros_like(dw_ref)
            db_ref[...] = jnp.zeros_like(db_ref)
            dwp_ref[...] = jnp.zeros_like(dwp_ref)

        g, y = g_ref[...], y_ref[...]
        rstd = _rms(y)
        yn = y * rstd
        gw = g * wp_ref[...]
        dwp_ref[...] += jnp.sum(g * yn, axis=0, keepdims=True)
        dy = rstd * (gw - yn * jnp.mean(gw * yn, axis=-1, keepdims=True))
        db_ref[...] += jnp.sum(dy, axis=0, keepdims=True)
        dyb = dy.astype(BF16)
        dog_ref[...] = _dot(dyb, w_ref[...], _NT)
        dw_ref[...] += _dot(og_ref[...], dyb, _TN)

    return pl.pallas_call(
        body, name="out_proj_bwd", grid=(n // tm,),
        in_specs=[_row_spec(tm, D_MODEL), _row_spec(tm, D_MODEL), _row_spec(tm, D_MODEL),
                  _full_spec((D_MODEL, D_MODEL)), _full_spec((1, D_MODEL))],
        out_specs=[_row_spec(tm, D_MODEL), _full_spec((D_MODEL, D_MODEL)), _full_spec((1, D_MODEL)),
                   _full_spec((1, D_MODEL))],
        out_shape=[jax.ShapeDtypeStruct((n, D_MODEL), F32), jax.ShapeDtypeStruct((D_MODEL, D_MODEL), F32),
                   jax.ShapeDtypeStruct((1, D_MODEL), F32), jax.ShapeDtypeStruct((1, D_MODEL), F32)],
        compiler_params=_cparams(1),
    )(dxo, y, og, w_out, w_post)


def in_proj_bwd_x(dproj, w_in, x, w_pre, dxo, tm=512):
    n, p = dproj.shape

    def body(dp_ref, w_ref, x_ref, wp_ref, g_ref, dx_ref, dwp_ref):
        @pl.when(pl.program_id(0) == 0)
        def _():
            dwp_ref[...] = jnp.zeros_like(dwp_ref)

        dh = _dot(dp_ref[...], w_ref[...], _NT)
        xv = x_ref[...]
        rstd = _rms(xv)
        xn = xv * rstd
        gw = dh * wp_ref[...]
        dwp_ref[...] += jnp.sum(dh * xn, axis=0, keepdims=True)
        dx_ref[...] = rstd * (gw - xn * jnp.mean(gw * xn, axis=-1, keepdims=True)) + g_ref[...]

    return pl.pallas_call(
        body, name=f"in_proj_bwd_x_{p}", grid=(n // tm,),
        in_specs=[_row_spec(tm, p), _full_spec((D_MODEL, p)), _row_spec(tm, D_MODEL), _full_spec((1, D_MODEL)),
                  _row_spec(tm, D_MODEL)],
        out_specs=[_row_spec(tm, D_MODEL), _full_spec((1, D_MODEL))],
        out_shape=[jax.ShapeDtypeStruct((n, D_MODEL), F32), jax.ShapeDtypeStruct((1, D_MODEL), F32)],
        compiler_params=_cparams(1),
    )(dproj, w_in, x, w_pre, dxo)


def in_proj_bwd_w(h, dproj, tm=512):
    n, p = dproj.shape
    chunk = p // (4 if p % 4096 == 0 else 3)
    steps = n // tm

    def body(h_ref, dp_ref, dw_ref, db_ref, acc_scr, sem):
        i = pl.program_id(0)

        @pl.when(i == 0)
        def _():
            acc_scr[...] = jnp.zeros_like(acc_scr)
            db_ref[...] = jnp.zeros_like(db_ref)

        ht = h_ref[...].T
        for c0 in range(0, p, chunk):
            dp = dp_ref[:, c0:c0 + chunk]
            acc_scr[:, c0:c0 + chunk] += jnp.dot(ht, dp, preferred_element_type=F32)
            db_ref[:, c0:c0 + chunk] += jnp.sum(dp.astype(F32), axis=0, keepdims=True)

        @pl.when(i == steps - 1)
        def _():
            out = pltpu.make_async_copy(acc_scr, dw_ref, sem)
            out.start()
            out.wait()

    return pl.pallas_call(
        body, name=f"in_proj_bwd_w_{p}", grid=(steps,),
        in_specs=[_row_spec(tm, D_MODEL), _row_spec(tm, p)],
        out_specs=[_ANY, _full_spec((1, p))],
        out_shape=[jax.ShapeDtypeStruct((D_MODEL, p), F32), jax.ShapeDtypeStruct((1, p), F32)],
        scratch_shapes=[pltpu.VMEM((D_MODEL, p), F32), pltpu.SemaphoreType.DMA],
        compiler_params=_cparams(1),
    )(h, dproj)


PAIRS = GROUP // 2
GROUP_ROWS = PAIRS * ATTN_BLOCK
MASKED = -1e30


def _kv_windows(k_ref, v_ref, i):
    ps = pl.multiple_of(jnp.maximum(i - 1, 0) * ATTN_BLOCK, ATTN_BLOCK)
    cs = pl.multiple_of(i * ATTN_BLOCK, ATTN_BLOCK)
    kw = jnp.concatenate([k_ref[pl.ds(ps, ATTN_BLOCK), :], k_ref[pl.ds(cs, ATTN_BLOCK), :]], axis=0)
    vw = jnp.concatenate([v_ref[pl.ds(ps, ATTN_BLOCK), :], v_ref[pl.ds(cs, ATTN_BLOCK), :]], axis=0)
    return kw.astype(F32), vw.astype(F32), ps, cs


def _low_lanes(shape):
    return lax.broadcasted_iota(jnp.int32, shape, 1) < HEAD_DIM


def _spread(w, kvh):
    low = _low_lanes(w.shape)
    swapped = pltpu.roll(w, HEAD_DIM, 1)
    if kvh == 0:
        return jnp.where(low, w, 0.0), jnp.where(low, 0.0, swapped)
    return jnp.where(low, swapped, 0.0), jnp.where(low, 0.0, w)


def _unspread(d_a, d_b, kvh):
    low = _low_lanes(d_a.shape)
    if kvh == 0:
        return jnp.where(low, d_a + pltpu.roll(d_b, HEAD_DIM, 1), 0.0)
    return jnp.where(low, 0.0, pltpu.roll(d_a, HEAD_DIM, 1) + d_b)


def _stack_pairs(ref, kvh):
    return jnp.concatenate([ref[:, (kvh * PAIRS + j) * LANES:(kvh * PAIRS + j + 1) * LANES] for j in range(PAIRS)],
                           axis=0)


def _fill_bias(bias_scr):
    shape = (GROUP_ROWS, 2 * ATTN_BLOCK)
    r = lax.broadcasted_iota(jnp.int32, shape, 0) % ATTN_BLOCK
    c = lax.broadcasted_iota(jnp.int32, shape, 1)
    in_cur = (c >= ATTN_BLOCK) & ((c - ATTN_BLOCK) <= r)
    in_prev = (c < ATTN_BLOCK) & (c > r)
    bias_scr[0] = jnp.where(in_cur, 0.0, MASKED)
    bias_scr[1] = jnp.where(in_cur | in_prev, 0.0, MASKED)


def _sink_table(sinks):
    t = jnp.transpose(sinks.reshape(N_KV_HEADS, PAIRS, 2), (0, 2, 1))
    return jnp.broadcast_to(t[:, :, :, None, None], (N_KV_HEADS, 2, PAIRS, ATTN_BLOCK, LANES)).reshape(
        N_KV_HEADS, 2, GROUP_ROWS, LANES)


def attn_fwd(q, k, v, z, sink_tab, batch, seq, gather=()):
    nb = seq // ATTN_BLOCK
    ng = len(gather)

    def body(*refs):
        q_ref, k_ref, v_ref, z_ref, s_ref = refs[:5]
        og_ref, bias_scr = refs[5 + ng], refs[6 + 2 * ng]
        exchange = (refs[5:5 + ng], refs[6 + ng:6 + 2 * ng]) + tuple(refs[7 + 2 * ng:])
        b, i = pl.program_id(0), pl.program_id(1)

        @pl.when((b == 0) & (i == 0))
        def _():
            _fill_bias(bias_scr)
            if ng:
                _gather_start(*exchange)

        kw, vw, _, _ = _kv_windows(k_ref, v_ref, i)
        bias = bias_scr[jnp.minimum(i, 1)]
        for kvh in range(N_KV_HEADS):
            k_a, k_b = _spread(kw, kvh)
            v_a, v_b = _spread(vw, kvh)
            og = _attn_group(_stack_pairs(q_ref, kvh), k_a, v_a, k_b, v_b, _stack_pairs(z_ref, kvh),
                             s_ref[kvh, 0], s_ref[kvh, 1], bias)
            for j in range(PAIRS):
                og_ref[:, (kvh * PAIRS + j) * LANES:(kvh * PAIRS + j + 1) * LANES] = (
                    og[j * ATTN_BLOCK:(j + 1) * ATTN_BLOCK].astype(BF16))

        if ng:
            @pl.when((b == batch - 1) & (i == nb - 1))
            def _():
                _gather_finish(*exchange)

    blk = lambda w: pl.BlockSpec((ATTN_BLOCK, w), lambda b, i: (b * nb + i, 0))
    seq_spec = pl.BlockSpec((seq, KV_WIDTH), lambda b, i: (b, 0))
    out = pl.pallas_call(
        body, name="attn_fwd", grid=(batch, nb),
        in_specs=[blk(D_MODEL), seq_spec, seq_spec, blk(D_MODEL), _full_spec(sink_tab.shape)] + [_ANY] * ng,
        out_specs=[blk(D_MODEL)] + [_ANY] * ng,
        out_shape=[jax.ShapeDtypeStruct((batch * seq, D_MODEL), BF16)]
        + [jax.ShapeDtypeStruct((N_CHIPS,) + a.shape, a.dtype) for a in gather],
        scratch_shapes=[pltpu.VMEM((2, GROUP_ROWS, 2 * ATTN_BLOCK), F32)] + (_gather_sems(ng) if ng else []),
        compiler_params=_cparams(2),
    )(q, k, v, z, sink_tab, *gather)
    return out[0], out[1:]


def attn_bwd(q, k, v, z, sink_tab, dog, tables, batch, seq, scatter=()):
    nb = seq // ATTN_BLOCK
    ns = len(scatter)

    def body(*refs):
        q_ref, k_ref, v_ref, z_ref, s_ref, g_ref, c_ref, sa_ref, sb_ref = refs[:9]
        dp_ref, dk_ref, dv_ref, ds_ref = refs[9 + ns:13 + ns]
        bias_scr = refs[13 + 2 * ns]
        exchange = (refs[9:9 + ns], refs[13 + ns:13 + 2 * ns]) + tuple(refs[14 + 2 * ns:])
        b, i = pl.program_id(0), pl.program_id(1)

        @pl.when((b == 0) & (i == 0))
        def _():
            _fill_bias(bias_scr)
            ds_ref[...] = jnp.zeros_like(ds_ref)
            if ns:
                _scatter_start(*exchange)

        @pl.when(i == 0)
        def _():
            dk_ref[...] = jnp.zeros_like(dk_ref)
            dv_ref[...] = jnp.zeros_like(dv_ref)

        kw, vw, ps, cs = _kv_windows(k_ref, v_ref, i)
        bias = bias_scr[jnp.minimum(i, 1)]
        tabs = (c_ref[...], sa_ref[...], sb_ref[...])
        dkw = jnp.zeros_like(kw)
        dvw = jnp.zeros_like(vw)
        for kvh in range(N_KV_HEADS):
            k_a, k_b = _spread(kw, kvh)
            v_a, v_b = _spread(vw, kvh)
            _, vjp = jax.vjp(functools.partial(_attn_group, bias=bias), _stack_pairs(q_ref, kvh).astype(F32),
                             k_a, v_a, k_b, v_b, _stack_pairs(z_ref, kvh), s_ref[kvh, 0], s_ref[kvh, 1])
            dqs, dk_a, dv_a, dk_b, dv_b, dzs, ds_a, ds_b = vjp(_stack_pairs(g_ref, kvh))
            dkw = dkw + _unspread(dk_a, dk_b, kvh)
            dvw = dvw + _unspread(dv_a, dv_b, kvh)
            ds_ref[kvh, 0] += jnp.sum(ds_a.reshape(PAIRS, ATTN_BLOCK, LANES), axis=1)
            ds_ref[kvh, 1] += jnp.sum(ds_b.reshape(PAIRS, ATTN_BLOCK, LANES), axis=1)
            for j in range(PAIRS):
                rows = slice(j * ATTN_BLOCK, (j + 1) * ATTN_BLOCK)
                col = (kvh * PAIRS + j) * LANES
                dp_ref[:, col:col + LANES] = _rope_transposed(dqs[rows] * (HEAD_DIM ** -0.5), *tabs).astype(BF16)
                zc = D_MODEL + 2 * KV_WIDTH + col
                dp_ref[:, zc:zc + LANES] = dzs[rows].astype(BF16)
        dp_ref[:, D_MODEL:D_MODEL + 2 * KV_WIDTH] = jnp.zeros((ATTN_BLOCK, 2 * KV_WIDTH), BF16)
        dk_ref[pl.ds(ps, ATTN_BLOCK), :] += dkw[:ATTN_BLOCK]
        dk_ref[pl.ds(cs, ATTN_BLOCK), :] += dkw[ATTN_BLOCK:]
        dv_ref[pl.ds(ps, ATTN_BLOCK), :] += dvw[:ATTN_BLOCK]
        dv_ref[pl.ds(cs, ATTN_BLOCK), :] += dvw[ATTN_BLOCK:]

        if ns:
            @pl.when((b == batch - 1) & (i == nb - 1))
            def _():
                _scatter_finish(*exchange)

    blk = lambda w: pl.BlockSpec((ATTN_BLOCK, w), lambda b, i: (b * nb + i, 0))
    seq_spec = pl.BlockSpec((seq, KV_WIDTH), lambda b, i: (b, 0))
    n = batch * seq
    ds_shape = (N_KV_HEADS, 2, PAIRS, LANES)
    out = pl.pallas_call(
        body, name="attn_bwd", grid=(batch, nb),
        in_specs=[blk(D_MODEL), seq_spec, seq_spec, blk(D_MODEL), _full_spec(sink_tab.shape), blk(D_MODEL)]
        + [blk(LANES)] * 3 + [_ANY] * ns,
        out_specs=[blk(ATTN_IN), seq_spec, seq_spec, _full_spec(ds_shape)] + [_ANY] * ns,
        out_shape=[jax.ShapeDtypeStruct((n, ATTN_IN), BF16), jax.ShapeDtypeStruct((n, KV_WIDTH), F32),
                   jax.ShapeDtypeStruct((n, KV_WIDTH), F32), jax.ShapeDtypeStruct(ds_shape, F32)]
        + [jax.ShapeDtypeStruct(a.shape, a.dtype) for a in scatter],
        scratch_shapes=[pltpu.VMEM((2, GROUP_ROWS, 2 * ATTN_BLOCK), F32)] + (_scatter_sems(ns) if ns else []),
        compiler_params=_cparams(2),
    )(q, k, v, z, sink_tab, dog, *tables, *scatter)
    return out[0], out[1], out[2], out[3], out[4:]


def attn_bwd_kv(dproj, dk, dv, tables, tm=512):
    n = dproj.shape[0]

    def body(dp_in_ref, dk_ref, dv_ref, c_ref, sa_ref, sb_ref, dp_ref):
        del dp_in_ref
        dp_ref[:, :KV_WIDTH] = _rope_transposed(dk_ref[...], c_ref[...], sa_ref[...], sb_ref[...]).astype(BF16)
        dp_ref[:, KV_WIDTH:] = dv_ref[...].astype(BF16)

    kv_cols = pl.BlockSpec((tm, 2 * KV_WIDTH), lambda i: (i, D_MODEL // (2 * KV_WIDTH)))
    return pl.pallas_call(
        body, name="attn_bwd_kv", grid=(n // tm,),
        in_specs=[kv_cols, _row_spec(tm, KV_WIDTH), _row_spec(tm, KV_WIDTH)] + [_row_spec(tm, LANES)] * 3,
        out_specs=kv_cols, out_shape=jax.ShapeDtypeStruct(dproj.shape, BF16),
        input_output_aliases={0: 0}, compiler_params=_cparams(1),
    )(dproj, dk, dv, *tables)


def _rec_cols(part, h):
    return slice(part * D_MODEL + h * REC_DIM, part * D_MODEL + (h + 1) * REC_DIM)


def _rec_args(p_ref, lb_ref, gw_ref, h, S):
    hs = slice(h * REC_DIM, (h + 1) * REC_DIM)
    return (p_ref[:, _rec_cols(0, h)], p_ref[:, _rec_cols(1, h)], p_ref[:, _rec_cols(2, h)],
            p_ref[:, _rec_cols(3, h)], S, lb_ref[0:1, hs], lb_ref[1:2, hs], gw_ref[...])


def rec_fwd(proj, lb_logits, gnorm_w, batch, seq):
    nblk = seq // REC_BLOCK

    def body(p_ref, lb_ref, gw_ref, og_ref, st_ref, safe_ref, s_scr):
        @pl.when(pl.program_id(1) == 0)
        def _():
            s_scr[...] = jnp.zeros_like(s_scr)

        heads, margin = [], None
        for h in range(REC_HEADS):
            S = s_scr[h]
            st_ref[0, h] = S
            qr, fr, v, _, _, l0, l1, _ = _rec_args(p_ref, lb_ref, gw_ref, h, S)
            q, k, lf = _rec_front(qr, fr, l0, l1)
            b = cumsum_rows(lf)
            heads.append((q, k, v, b, S))
            m = _rec_margin(b)
            margin = m if margin is None else jnp.minimum(margin, m)
        safe = jnp.min(margin) >= -SAFE_RANGE

        def run(core):
            return [core(*args) for args in heads]

        results = lax.cond(safe, functools.partial(run, _rec_core_fast), functools.partial(run, _rec_core_slow))
        for h, (o, S_new) in enumerate(results):
            hs = slice(h * REC_DIM, (h + 1) * REC_DIM)
            og_ref[:, hs] = _rec_tail(o, p_ref[:, _rec_cols(3, h)], gw_ref[...]).astype(BF16)
            s_scr[h] = S_new
        safe_ref[0] = jnp.full((REC_HEADS, LANES), safe.astype(F32))

    blk = lambda w: pl.BlockSpec((REC_BLOCK, w), lambda b, j: (b * nblk + j, 0))
    st_spec = pl.BlockSpec((1, REC_HEADS, REC_DIM, REC_DIM), lambda b, j: (b * nblk + j, 0, 0, 0))
    safe_spec = pl.BlockSpec((1, REC_HEADS, LANES), lambda b, j: (b * nblk + j, 0, 0))
    return pl.pallas_call(
        body, name="rec_fwd", grid=(batch, nblk),
        in_specs=[blk(REC_IN), _full_spec((2, D_MODEL)), _full_spec((1, REC_DIM))],
        out_specs=[blk(D_MODEL), st_spec, safe_spec],
        out_shape=[jax.ShapeDtypeStruct((batch * seq, D_MODEL), BF16),
                   jax.ShapeDtypeStruct((batch * nblk, REC_HEADS, REC_DIM, REC_DIM), F32),
                   jax.ShapeDtypeStruct((batch * nblk, REC_HEADS, LANES), F32)],
        scratch_shapes=[pltpu.VMEM((REC_HEADS, REC_DIM, REC_DIM), F32)],
        compiler_params=_cparams(2),
    )(proj, lb_logits, gnorm_w)


def rec_bwd(proj, states, safe, lb_logits, gnorm_w, dog, batch, seq):
    nblk = seq // REC_BLOCK

    def body(p_ref, st_ref, safe_ref, lb_ref, gw_ref, g_ref, dp_ref, dlb_ref, dgw_ref, ds_scr):
        @pl.when((pl.program_id(0) == 0) & (pl.program_id(1) == 0))
        def _():
            dlb_ref[...] = jnp.zeros_like(dlb_ref)
            dgw_ref[...] = jnp.zeros_like(dgw_ref)

        @pl.when(pl.program_id(1) == 0)
        def _():
            ds_scr[...] = jnp.zeros_like(ds_scr)

        def pull_back(core):
            out = []
            for h in range(REC_HEADS):
                hs = slice(h * REC_DIM, (h + 1) * REC_DIM)
                primals = _rec_args(p_ref, lb_ref, gw_ref, h, st_ref[0, h])
                out.append(jax.vjp(functools.partial(_rec_head, core), *primals)[1]((g_ref[:, hs], ds_scr[h])))
            return out

        results = lax.cond(jnp.max(safe_ref[0]) > 0.5, functools.partial(pull_back, _rec_core_fast),
                           functools.partial(pull_back, _rec_core_slow))
        for h, (dqr, dfr, dv, dz, dS, dl0, dl1, dgw) in enumerate(results):
            hs = slice(h * REC_DIM, (h + 1) * REC_DIM)
            for part, val in enumerate((dqr, dfr, dv, dz)):
                dp_ref[:, _rec_cols(part, h)] = val.astype(BF16)
            ds_scr[h] = dS
            dlb_ref[0:1, hs] += dl0
            dlb_ref[1:2, hs] += dl1
            dgw_ref[...] += dgw

    blk = lambda w: pl.BlockSpec((REC_BLOCK, w), lambda b, j: (b * nblk + nblk - 1 - j, 0))
    st_spec = pl.BlockSpec((1, REC_HEADS, REC_DIM, REC_DIM), lambda b, j: (b * nblk + nblk - 1 - j, 0, 0, 0))
    safe_spec = pl.BlockSpec((1, REC_HEADS, LANES), lambda b, j: (b * nblk + nblk - 1 - j, 0, 0))
    return pl.pallas_call(
        body, name="rec_bwd", grid=(batch, nblk),
        in_specs=[blk(REC_IN), st_spec, safe_spec, _full_spec((2, D_MODEL)), _full_spec((1, REC_DIM)),
                  blk(D_MODEL)],
        out_specs=[blk(REC_IN), _full_spec((2, D_MODEL)), _full_spec((1, REC_DIM))],
        out_shape=[jax.ShapeDtypeStruct((batch * seq, REC_IN), BF16), jax.ShapeDtypeStruct((2, D_MODEL), F32),
                   jax.ShapeDtypeStruct((1, REC_DIM), F32)],
        scratch_shapes=[pltpu.VMEM((REC_HEADS, REC_DIM, REC_DIM), F32)],
        compiler_params=_cparams(2),
    )(proj, states, safe, lb_logits, gnorm_w, dog)


_ANY = pl.BlockSpec(memory_space=pl.ANY)


def _chip_peers():
    x, y, c = lax.axis_index("x"), lax.axis_index("y"), lax.axis_index("c")
    peers = []
    for fx, fy in ((1, 0), (0, 1), (1, 1)):
        px, py = (1 - x if fx else x), (1 - y if fy else y)
        peers.append(((px, py, c), 2 * px + py))
    return 2 * x + y, peers


def _remote(src, dst, send_sem, recv_sem, device):
    return pltpu.make_async_remote_copy(src_ref=src, dst_ref=dst, send_sem=send_sem, recv_sem=recv_sem,
                                        device_id=device, device_id_type=MESH)


N_FLIPS = N_CHIPS - 1


def _scatter_sems(n):
    return [pltpu.SemaphoreType.DMA((n * N_FLIPS,)), pltpu.SemaphoreType.DMA((n * N_FLIPS,)),
            pltpu.SemaphoreType.DMA((n,))]


def _scatter_copies(ins, outs, send_sems, recv_sems, local_sems, starting):
    me, peers = _chip_peers()
    local = [pltpu.make_async_copy(ins[k].at[me], outs[k].at[me], local_sems.at[k]) for k in range(len(ins))]
    sends, arrivals = [], []
    for k in range(len(ins)):
        for j, (device, idx) in enumerate(peers):
            sems = (send_sems.at[k * N_FLIPS + j], recv_sems.at[k * N_FLIPS + j], device)
            sends.append(_remote(ins[k].at[idx], outs[k].at[me], *sems))
            if not starting:
                arrivals.append(_remote(ins[k].at[me], outs[k].at[idx], *sems))
    return local, sends, arrivals


def _scatter_start(*refs):
    local, sends, _ = _scatter_copies(*refs, starting=True)
    for cp in local + sends:
        cp.start()


def _scatter_finish(*refs):
    local, sends, arrivals = _scatter_copies(*refs, starting=False)
    for cp in arrivals:
        cp.wait_recv()
    for cp in sends:
        cp.wait_send()
    for cp in local:
        cp.wait()


def chip_scatter(arrays):
    n = len(arrays)

    def body(*refs):
        _scatter_start(refs[:n], refs[n:2 * n], *refs[2 * n:])
        _scatter_finish(refs[:n], refs[n:2 * n], *refs[2 * n:])

    return pl.pallas_call(
        body, name="chip_scatter", in_specs=[_ANY] * n, out_specs=[_ANY] * n,
        out_shape=[jax.ShapeDtypeStruct(a.shape, a.dtype) for a in arrays], scratch_shapes=_scatter_sems(n),
    )(*arrays)


def _gather_sems(n):
    return [pltpu.SemaphoreType.DMA((n * N_FLIPS,)) for _ in range(4)] + [pltpu.SemaphoreType.DMA((n,))]


def _gather_copies(ins, outs, send_sems, recv_sems, pass_send_sems, pass_recv_sems, local_sems, starting):
    me, peers = _chip_peers()
    c = lax.axis_index("c")
    sibling = (lax.axis_index("x"), lax.axis_index("y"), 1 - c)
    local = [pltpu.make_async_copy(ins[k], outs[k].at[me], local_sems.at[k]) for k in range(len(ins))]
    sends, arrivals, passes, pass_arrivals = [], [], [], []
    for k in range(len(ins)):
        half = ins[k].shape[0] // 2
        mine, other = pl.ds(c * half, half), pl.ds((1 - c) * half, half)
        for j, (device, idx) in enumerate(peers):
            s = k * N_FLIPS + j
            sends.append(_remote(ins[k].at[mine], outs[k].at[me].at[mine], send_sems.at[s], recv_sems.at[s], device))
            if starting:
                continue
            arrived = outs[k].at[idx].at[mine]
            arrivals.append(_remote(ins[k].at[mine], arrived, send_sems.at[s], recv_sems.at[s], device))
            passes.append(_remote(arrived, arrived, pass_send_sems.at[s], pass_recv_sems.at[s], sibling))
            passed = outs[k].at[idx].at[other]
            pass_arrivals.append(_remote(passed, passed, pass_send_sems.at[s], pass_recv_sems.at[s], sibling))
    return local, sends, arrivals, passes, pass_arrivals


def _gather_start(*refs):
    local, sends, _, _, _ = _gather_copies(*refs, starting=True)
    for cp in local + sends:
        cp.start()


def _gather_finish(*refs):
    local, sends, arrivals, passes, pass_arrivals = _gather_copies(*refs, starting=False)
    for arrival, onward in zip(arrivals, passes):
        arrival.wait_recv()
        onward.start()
    for cp in pass_arrivals:
        cp.wait_recv()
    for cp in sends + passes:
        cp.wait_send()
    for cp in local:
        cp.wait()


def chip_gather(arrays):
    n = len(arrays)

    def body(*refs):
        _gather_start(refs[:n], refs[n:2 * n], *refs[2 * n:])
        _gather_finish(refs[:n], refs[n:2 * n], *refs[2 * n:])

    return pl.pallas_call(
        body, name="chip_gather", in_specs=[_ANY] * n, out_specs=[_ANY] * n,
        out_shape=[jax.ShapeDtypeStruct((N_CHIPS,) + a.shape, a.dtype) for a in arrays],
        scratch_shapes=_gather_sems(n),
    )(*arrays)


def sibling_exchange(arrays):
    n = len(arrays)

    def body(*refs):
        ins, outs = refs[:n], refs[n:2 * n]
        send_sems, recv_sems = refs[2 * n:]
        sibling = (lax.axis_index("x"), lax.axis_index("y"), 1 - lax.axis_index("c"))
        copies = [pltpu.make_async_remote_copy(src_ref=ins[k], dst_ref=outs[k], send_sem=send_sems.at[k],
                                               recv_sem=recv_sems.at[k], device_id=sibling, device_id_type=MESH)
                  for k in range(n)]
        for cp in copies:
            cp.start()
        for cp in copies:
            cp.wait()

    return pl.pallas_call(
        body, name="sibling_exchange", in_specs=[_ANY] * n, out_specs=[_ANY] * n,
        out_shape=[jax.ShapeDtypeStruct(a.shape, a.dtype) for a in arrays],
        scratch_shapes=[pltpu.SemaphoreType.DMA((n,)), pltpu.SemaphoreType.DMA((n,))],
    )(*arrays)


def all_gather_small(vec):
    def body(v_ref, out_ref, send_sems, recv_sems, local_sem):
        x, y, c = lax.axis_index("x"), lax.axis_index("y"), lax.axis_index("c")
        me = 4 * x + 2 * y + c
        local = pltpu.make_async_copy(v_ref, out_ref.at[me], local_sem)
        local.start()
        sends, recvs = [], []
        for j in range(1, N_DEV):
            px = jnp.where(j & 4, 1 - x, x)
            py = jnp.where(j & 2, 1 - y, y)
            pc = jnp.where(j & 1, 1 - c, c)
            common = dict(send_sem=send_sems.at[j - 1], recv_sem=recv_sems.at[j - 1], device_id=(px, py, pc),
                          device_id_type=MESH)
            sends.append(pltpu.make_async_remote_copy(src_ref=v_ref, dst_ref=out_ref.at[me], **common))
            recvs.append(pltpu.make_async_remote_copy(src_ref=v_ref, dst_ref=out_ref.at[4 * px + 2 * py + pc],
                                                      **common))
        for cp in sends:
            cp.start()
        for cp in recvs:
            cp.wait_recv()
        for cp in sends:
            cp.wait_send()
        local.wait()

    return pl.pallas_call(
        body, name="all_gather_small", in_specs=[_ANY], out_specs=_ANY,
        out_shape=jax.ShapeDtypeStruct((N_DEV,) + vec.shape, vec.dtype),
        scratch_shapes=[pltpu.SemaphoreType.DMA((N_DEV - 1,)), pltpu.SemaphoreType.DMA((N_DEV - 1,)),
                        pltpu.SemaphoreType.DMA],
    )(vec)


def sum_slots(stacked, tm=256):
    s, r, c = stacked.shape
    tm = min(tm, r)

    def body(in_ref, out_ref):
        acc = in_ref[0].astype(F32)
        for t in range(1, s):
            acc = acc + in_ref[t].astype(F32)
        out_ref[...] = acc

    return pl.pallas_call(
        body, name=f"sum_slots_{s}_{r}_{c}", grid=(r // tm,),
        in_specs=[pl.BlockSpec((s, tm, c), lambda i: (0, i, 0))], out_specs=_row_spec(tm, c),
        out_shape=jax.ShapeDtypeStruct((r, c), F32), compiler_params=_cparams(1),
    )(stacked)


def adamw(w, m, v, g_a, g_b=None, tm=256):
    r, c = w.shape
    tm = min(tm, r)
    two = g_b is not None

    def body(*refs):
        w_ref, m_ref, v_ref, ga_ref = refs[:4]
        g_ref, d_ref, nm_ref, nv_ref = refs[-4:]
        g = ga_ref[...] + refs[4][...] if two else ga_ref[...]
        nm = ADAM_B1 * m_ref[...] + (1.0 - ADAM_B1) * g
        nv = ADAM_B2 * v_ref[...] + (1.0 - ADAM_B2) * (g * g)
        m_hat = nm / (1.0 - ADAM_B1 ** ADAM_STEP)
        v_hat = nv / (1.0 - ADAM_B2 ** ADAM_STEP)
        g_ref[...] = g
        d_ref[...] = -ADAM_LR * (m_hat / (jnp.sqrt(v_hat) + ADAM_EPS) + ADAM_WD * w_ref[...])
        nm_ref[...] = nm
        nv_ref[...] = nv

    args = [w, m, v, g_a] + ([g_b] if two else [])
    return pl.pallas_call(
        body, name=f"adamw_{r}_{c}", grid=(r // tm,),
        in_specs=[_row_spec(tm, c)] * len(args), out_specs=[_row_spec(tm, c)] * 4,
        out_shape=[jax.ShapeDtypeStruct((r, c), F32)] * 4, compiler_params=_cparams(1),
    )(*args)


_SMALL = (("pre_norm_w", (2, D_MODEL)), ("post_norm_w", (2, D_MODEL)), ("attn_b_in", (1, ATTN_IN)),
          ("attn_sinks", (1, N_HEADS)), ("attn_b_out", (1, D_MODEL)), ("rec_lb_logits", (2, D_MODEL)),
          ("rec_gnorm_w", (1, REC_DIM)))
_SMALL_ROWS = 16


def _pack_small(parts):
    rows = []
    for (name, shape) in _SMALL:
        flat = parts[name].reshape(-1)
        pad = -flat.shape[0] % D_MODEL
        rows.append(jnp.pad(flat, (0, pad)).reshape(-1, D_MODEL))
    packed = jnp.concatenate(rows, axis=0)
    return jnp.pad(packed, ((0, _SMALL_ROWS - packed.shape[0]), (0, 0)))


def _unpack_small(packed):
    out, row = {}, 0
    for (name, shape) in _SMALL:
        size = shape[0] * shape[1]
        nrows = -(-size // D_MODEL)
        out[name] = packed[row:row + nrows].reshape(-1)[:size].reshape(shape)
        row += nrows
    return out


_CARRIED = ("rec_w_in", "rec_w_out", "attn_w_out")


def local_step(x, positions, pre_norm_w, post_norm_w, attn_w_in, attn_b_in, attn_sinks, attn_w_out, attn_b_out,
               rec_w_in, rec_lb_logits, rec_gnorm_w, rec_w_out, loss_target, distributed=False):
    batch, seq, _ = x.shape
    n = batch * seq
    x0 = x.reshape(n, D_MODEL)
    tables = _rope_tables(positions)
    pre0, pre1 = pre_norm_w[0:1], pre_norm_w[1:2]
    post0, post1 = post_norm_w[0:1], post_norm_w[1:2]
    no_bias = jnp.zeros((1, D_MODEL), F32)

    h0, q, k, v, z = attn_in_proj(x0, pre0, attn_w_in, attn_b_in, tables)
    sink_tab = _sink_table(attn_sinks)
    og0, gathered = attn_fwd(q, k, v, z, sink_tab, batch, seq, gather=(rec_w_in, rec_w_out) if distributed else ())
    if distributed:
        rec_w_in, rec_w_out = (_whole_from_shards(name, g) for name, g in zip(("rec_w_in", "rec_w_out"), gathered))
    y0, x1 = out_proj(og0, attn_w_out, attn_b_out, x0, post0)

    h1, proj1 = rec_in_proj(x1, pre1, rec_w_in)
    og1, states, safe = rec_fwd(proj1, rec_lb_logits, rec_gnorm_w, batch, seq)
    y1, dx2, loss_vec = out_proj(og1, rec_w_out, no_bias, x1, post1, target=loss_target.reshape(n, D_MODEL))
    loss = jnp.sum(loss_vec) * (0.5 / D_MODEL)

    dog1, d_rec_w_out, _, d_post1 = out_proj_bwd(dx2, y1, og1, rec_w_out, post1)
    dproj1, d_lb, d_gnorm = rec_bwd(proj1, states, safe, rec_lb_logits, rec_gnorm_w, dog1, batch, seq)
    dx1, d_pre1 = in_proj_bwd_x(dproj1, rec_w_in, x1, pre1, dx2)
    d_rec_w_in, _ = in_proj_bwd_w(h1, dproj1)

    dog0, d_attn_w_out, d_attn_b_out, d_post0 = out_proj_bwd(dx1, y0, og0, attn_w_out, post0)
    ready = dict(rec_w_in=d_rec_w_in, rec_w_out=d_rec_w_out, attn_w_out=d_attn_w_out)
    outgoing = [_shards_from_whole(name, ready[name]).astype(BF16) for name in _CARRIED] if distributed else []
    dproj0, dk, dv, d_sink_tab, arrived = attn_bwd(q, k, v, z, sink_tab, dog0, tables, batch, seq, scatter=outgoing)
    d_sinks = jnp.transpose(jnp.sum(d_sink_tab, axis=-1), (0, 2, 1)).reshape(1, N_HEADS)
    dproj0 = attn_bwd_kv(dproj0, dk, dv, tables)
    dx0, d_pre0 = in_proj_bwd_x(dproj0, attn_w_in, x0, pre0, dx1)
    d_attn_w_in, d_attn_b_in = in_proj_bwd_w(h0, dproj0)

    grads = dict(
        pre_norm_w=jnp.concatenate([d_pre0, d_pre1], axis=0), post_norm_w=jnp.concatenate([d_post0, d_post1], axis=0),
        attn_w_in=d_attn_w_in, attn_b_in=d_attn_b_in, attn_sinks=d_sinks, attn_w_out=d_attn_w_out,
        attn_b_out=d_attn_b_out, rec_w_in=d_rec_w_in, rec_lb_logits=d_lb, rec_gnorm_w=d_gnorm,
        rec_w_out=d_rec_w_out)
    return loss, dx0.reshape(batch, seq, D_MODEL), grads, dict(zip(_CARRIED, arrived))


_BIG = ("attn_w_in", "attn_w_out", "rec_w_in", "rec_w_out")
_COLUMN_SHARDED = ("attn_w_in", "rec_w_in")
_ORDER = ("pre_norm_w", "post_norm_w", "attn_w_in", "attn_b_in", "attn_sinks", "attn_w_out", "attn_b_out",
          "rec_w_in", "rec_lb_logits", "rec_gnorm_w", "rec_w_out")


def _whole_from_shards(name, stacked):
    if name in _COLUMN_SHARDED:
        return jnp.transpose(stacked, (1, 0, 2)).reshape(stacked.shape[1], -1)
    return stacked.reshape(-1, stacked.shape[2])


def _shards_from_whole(name, whole):
    if name in _COLUMN_SHARDED:
        return jnp.transpose(whole.reshape(whole.shape[0], N_CHIPS, -1), (1, 0, 2))
    return whole.reshape(N_CHIPS, -1, whole.shape[1])


def kernel(x, positions, pre_norm_w, post_norm_w, attn_w_in, attn_b_in, attn_sinks, attn_w_out, attn_b_out, rec_w_in, rec_lb_logits, rec_gnorm_w, rec_w_out, loss_target, m_pre_norm_w, m_post_norm_w, m_attn_w_in, m_attn_b_in, m_attn_sinks, m_attn_w_out, m_attn_b_out, m_rec_w_in, m_rec_lb_logits, m_rec_gnorm_w, m_rec_w_out, v_pre_norm_w, v_post_norm_w, v_attn_w_in, v_attn_b_in, v_attn_sinks, v_attn_w_out, v_attn_b_out, v_rec_w_in, v_rec_lb_logits, v_rec_gnorm_w, v_rec_w_out):
    w = dict(pre_norm_w=pre_norm_w, post_norm_w=post_norm_w, attn_w_in=attn_w_in, attn_b_in=attn_b_in,
             attn_sinks=attn_sinks, attn_w_out=attn_w_out, attn_b_out=attn_b_out, rec_w_in=rec_w_in,
             rec_lb_logits=rec_lb_logits, rec_gnorm_w=rec_gnorm_w, rec_w_out=rec_w_out)
    m = dict(pre_norm_w=m_pre_norm_w, post_norm_w=m_post_norm_w, attn_w_in=m_attn_w_in, attn_b_in=m_attn_b_in,
             attn_sinks=m_attn_sinks, attn_w_out=m_attn_w_out, attn_b_out=m_attn_b_out, rec_w_in=m_rec_w_in,
             rec_lb_logits=m_rec_lb_logits, rec_gnorm_w=m_rec_gnorm_w, rec_w_out=m_rec_w_out)
    v = dict(pre_norm_w=v_pre_norm_w, post_norm_w=v_post_norm_w, attn_w_in=v_attn_w_in, attn_b_in=v_attn_b_in,
             attn_sinks=v_attn_sinks, attn_w_out=v_attn_w_out, attn_b_out=v_attn_b_out, rec_w_in=v_rec_w_in,
             rec_lb_logits=v_rec_lb_logits, rec_gnorm_w=v_rec_gnorm_w, rec_w_out=v_rec_w_out)

    shards = {name: w[name][0] for name in _BIG}
    sent = {name: shards[name].astype(BF16) for name in _BIG}
    first = ("attn_w_in", "attn_w_out")
    whole = {name: _whole_from_shards(name, g) for name, g in zip(first, chip_gather([sent[name] for name in first]))}

    loss, grad_x, grads, parts = local_step(
        x, positions, pre_norm_w, post_norm_w, whole["attn_w_in"], attn_b_in, attn_sinks, whole["attn_w_out"],
        attn_b_out, sent["rec_w_in"], rec_lb_logits, rec_gnorm_w, sent["rec_w_out"], loss_target, distributed=True)
    loss = lax.psum(loss, ("x", "y", "c"))

    parts["attn_w_in"], = chip_scatter([_shards_from_whole("attn_w_in", grads["attn_w_in"]).astype(BF16)])
    plane_sums = [sum_slots(parts[name]) for name in _BIG]
    other_sums = sibling_exchange(plane_sums)
    out_g, out_d, out_m, out_v = {}, {}, {}, {}
    for name, mine, other in zip(_BIG, plane_sums, other_sums):
        g, d, nm, nv = adamw(shards[name], m[name][0], v[name][0], mine, other)
        out_g[name], out_d[name], out_m[name], out_v[name] = g[None], d[None], nm[None], nv[None]

    small_sum = sum_slots(all_gather_small(_pack_small(grads)))
    packed = adamw(_pack_small(w), _pack_small(m), _pack_small(v), small_sum)
    for dst, val in zip((out_g, out_d, out_m, out_v), packed):
        dst.update(_unpack_small(val))

    return (loss, grad_x, *[out_g[n] for n in _ORDER], *[out_d[n] for n in _ORDER],
            *[out_m[n] for n in _ORDER], *[out_v[n] for n in _ORDER])
```

```python
import functools

import jax
import jax.numpy as jnp
from jax import lax
from jax.experimental import pallas as pl
from jax.experimental.pallas import tpu as pltpu

F32 = jnp.float32
BF16 = jnp.bfloat16
MESH = pl.DeviceIdType.MESH

D_MODEL = 1024
HEAD_DIM = 64
N_HEADS = 16
N_KV_HEADS = 2
GROUP = N_HEADS // N_KV_HEADS
KV_WIDTH = N_KV_HEADS * HEAD_DIM
ATTN_IN = 2 * D_MODEL + 2 * KV_WIDTH
ATTN_BLOCK = 128
ROPE_THETA = 500000.0
ROPE_DIM = HEAD_DIM // 4
REC_HEADS = 8
REC_DIM = 128
REC_IN = 4 * D_MODEL
REC_BLOCK = 128
DIAG = 8
NORM_EPS = 1e-6
N_CHIPS = 4
N_DEV = 8
LANES = 128

ADAM_LR = 0.001
ADAM_B1 = 0.9
ADAM_B2 = 0.999
ADAM_EPS = 1e-08
ADAM_WD = 0.01
ADAM_STEP = 10

VMEM_LIMIT = 56 * 1024 * 1024


def _cparams(n_axes):
    return pltpu.CompilerParams(dimension_semantics=("arbitrary",) * n_axes, vmem_limit_bytes=VMEM_LIMIT)


def _dot(a, b, contract):
    return lax.dot_general(a.astype(BF16), b.astype(BF16), (contract, ((), ())), preferred_element_type=F32)


_NN = ((1,), (0,))
_NT = ((1,), (1,))
_TN = ((0,), (0,))


@jax.custom_vjp
def mm_nn(a, b):
    return _dot(a, b, _NN)


mm_nn.defvjp(lambda a, b: (_dot(a, b, _NN), (a, b)),
             lambda res, g: (_dot(g, res[1], _NT), _dot(res[0], g, _TN)))


@jax.custom_vjp
def mm_nt(a, b):
    return _dot(a, b, _NT)


mm_nt.defvjp(lambda a, b: (_dot(a, b, _NT), (a, b)),
             lambda res, g: (_dot(g, res[1], _NN), _dot(g, res[0], _TN)))


@jax.custom_vjp
def mm_tn(a, b):
    return _dot(a, b, _TN)


mm_tn.defvjp(lambda a, b: (_dot(a, b, _TN), (a, b)),
             lambda res, g: (_dot(res[1], g, _NT), _dot(res[0], g, _NN)))


def _tri_dot(x, lower):
    n = x.shape[0]
    r = lax.broadcasted_iota(jnp.int32, (n, n), 0)
    c = lax.broadcasted_iota(jnp.int32, (n, n), 1)
    tri = ((c <= r) if lower else (c >= r)).astype(BF16)
    hi = x.astype(BF16)
    rest = x - hi.astype(F32)
    mid = rest.astype(BF16)
    lo = (rest - mid.astype(F32)).astype(BF16)
    dot = lambda p: lax.dot_general(tri, p, (_NN, ((), ())), preferred_element_type=F32)
    return (dot(lo) + dot(mid)) + dot(hi)


@jax.custom_vjp
def cumsum_rows(x):
    return _tri_dot(x, True)


cumsum_rows.defvjp(lambda x: (cumsum_rows(x), None), lambda _, g: (_tri_dot(g, False),))


@functools.partial(jax.custom_vjp, nondiff_argnums=(1,))
def roll_sub(x, d):
    return pltpu.roll(x, d, 1) if d else x


roll_sub.defvjp(lambda x, d: (roll_sub(x, d), None),
                lambda d, _, g: (roll_sub(g, (DIAG - d) % DIAG),))


def sigmoid(x):
    return 1.0 / (1.0 + jnp.exp(-x))


@jax.custom_vjp
def silu(x):
    return x * sigmoid(x)


def _silu_fwd(x):
    s = sigmoid(x)
    return x * s, (x, s)


silu.defvjp(_silu_fwd, lambda res, g: (g * (res[1] * (1.0 + res[0] * (1.0 - res[1]))),))


def log_sigmoid_pair(x):
    t = jnp.log(1.0 + jnp.exp(-jnp.abs(x)))
    return jnp.minimum(x, 0.0) - t, jnp.minimum(-x, 0.0) - t


def _forget_fwd(x, a):
    log_lb, log_1m_lb = log_sigmoid_pair(a)
    ls_f, ls_nf = log_sigmoid_pair(x)
    c = log_1m_lb + ls_f
    lf = jnp.maximum(log_lb, c) + jnp.log(1.0 + jnp.exp(-jnp.abs(log_lb - c)))
    k = jnp.exp(log_1m_lb + ls_nf)
    return (lf, k), (log_lb, log_1m_lb, ls_f, ls_nf, c, lf, k)


def _forget_bwd(res, g):
    log_lb, log_1m_lb, ls_f, ls_nf, c, lf, k = res
    g_lf, g_k = g
    wc = jnp.exp(jnp.minimum(c - lf, 0.0))
    gk = g_k * k
    dx = g_lf * wc * jnp.exp(ls_nf) - gk * jnp.exp(ls_f)
    lb = jnp.exp(log_lb)
    da = jnp.sum(g_lf * ((1.0 - wc) * jnp.exp(log_1m_lb) - wc * lb) - gk * lb, axis=0, keepdims=True)
    return dx, da


@jax.custom_vjp
def forget_gate(x, a):
    return _forget_fwd(x, a)[0]


forget_gate.defvjp(_forget_fwd, _forget_bwd)


@jax.custom_vjp
def decayed(x, e):
    return (x * jnp.exp(e)).astype(BF16).astype(F32)


def _decayed_fwd(x, e):
    y = decayed(x, e)
    return y, (y, e)


decayed.defvjp(_decayed_fwd, lambda res, g: (g * jnp.exp(res[1]), g * res[0]))


def _row(x, r):
    shape = x.shape

    @jax.custom_vjp
    def take(x):
        return x[r:r + 1, :]

    take.defvjp(lambda x: (x[r:r + 1, :], None),
                lambda _, g: (jnp.where(lax.broadcasted_iota(jnp.int32, shape, 0) == r, g, 0.0),))
    return take(x)


def _rms(x):
    return lax.rsqrt(jnp.mean(x * x, axis=-1, keepdims=True) + NORM_EPS)


def _attn_group(qs, k_a, v_a, k_b, v_b, zs, sink_a, sink_b, bias):
    def half(kh, vh, sink):
        s = mm_nt(qs, kh) + bias
        m = lax.stop_gradient(jnp.maximum(jnp.max(s, axis=-1, keepdims=True), jnp.max(sink, axis=-1, keepdims=True)))
        p = jnp.exp(s - m)
        denom = jnp.sum(p, axis=-1, keepdims=True) + jnp.sum(jnp.exp(sink - m), axis=-1, keepdims=True) * (1.0 / LANES)
        return mm_nn(p * (1.0 / denom), vh)

    return (half(k_a, v_a, sink_a) + half(k_b, v_b, sink_b)) * silu(zs)


SAFE_RANGE = 80.0


def _rec_front(qr, fr, l0, l1):
    lf, k = forget_gate(fr, l1 - l0)
    return silu(qr), k, lf


def _rec_tail(o, z, gw):
    return o * _rms(o) * gw * silu(z)


def _rec_margin(b):
    R = b.shape[0]
    mid, last = _row(b, R // 2 - 1), _row(b, R - 1)
    return jnp.minimum(mid, last - mid)


def _heads(x):
    w = x.shape[1] // REC_HEADS
    return [x[:, h * w:(h + 1) * w] for h in range(REC_HEADS)]


def _hdot(a, b, contract):
    return jnp.concatenate([_dot(ah, bh, contract) for ah, bh in zip(_heads(a), _heads(b))], axis=1)


@jax.custom_vjp
def hmm_nn(a, b):
    return _hdot(a, b, _NN)


hmm_nn.defvjp(lambda a, b: (_hdot(a, b, _NN), (a, b)),
              lambda res, g: (_hdot(g, res[1], _NT), _hdot(res[0], g, _TN)))


@jax.custom_vjp
def hmm_nt(a, b):
    return _hdot(a, b, _NT)


hmm_nt.defvjp(lambda a, b: (_hdot(a, b, _NT), (a, b)),
              lambda res, g: (_hdot(g, res[1], _NN), _hdot(g, res[0], _TN)))


@jax.custom_vjp
def hmm_tn(a, b):
    return _hdot(a, b, _TN)


hmm_tn.defvjp(lambda a, b: (_hdot(a, b, _TN), (a, b)),
              lambda res, g: (_hdot(res[1], g, _NT), _hdot(res[0], g, _NN)))


def _head_sums(x):
    return jnp.concatenate([jnp.broadcast_to(jnp.sum(xh, axis=-1, keepdims=True), xh.shape) for xh in _heads(x)],
                           axis=1)


@jax.custom_vjp
def head_sum(x):
    return _head_sums(x)


head_sum.defvjp(lambda x: (_head_sums(x), None), lambda _, g: (_head_sums(g),))


def _rec_cores_fast(q, k, v, b, S):
    R = q.shape[0]
    ri = lax.broadcasted_iota(jnp.int32, (R, REC_HEADS * R), 0)
    ci = lax.broadcasted_iota(jnp.int32, (R, REC_HEADS * R), 1) % R
    d = b - _row(b, R // 2 - 1)
    sc = jnp.where(ci < ri, hmm_nt(decayed(q, d), decayed(k, -d)), 0.0)
    o = hmm_nt(q * jnp.exp(b), S) + hmm_nn(sc, v) + head_sum(q * k) * v
    b_last = _row(b, R - 1)
    return o, S * jnp.exp(b_last) + hmm_tn(v, k * jnp.exp(b_last - b))


def _rec_tails(o, z, gw):
    return o * lax.rsqrt(head_sum(o * o) * (1.0 / REC_DIM) + NORM_EPS) * gw * silu(z)


def _rec_block_fast(qr, fr, v, z, S, l0, l1, gw):
    lf, k = forget_gate(fr, l1 - l0)
    o, S_new = _rec_cores_fast(silu(qr), k, v, cumsum_rows(lf), S)
    return _rec_tails(o, z, gw), S_new


def _rec_core_slow(q, k, v, b, S):
    R = q.shape[0]
    rows = lax.broadcasted_iota(jnp.int32, (R, REC_DIM), 0)

    o = mm_nt(q * jnp.exp(jnp.minimum(b, 0.0)), S)

    ri = lax.broadcasted_iota(jnp.int32, (R, R), 0)
    ci = lax.broadcasted_iota(jnp.int32, (R, R), 1)
    sc = jnp.zeros((R, R), F32)
    w = R
    while w > DIAG:
        h = w // 2
        b3 = b.reshape(R // w, w, REC_DIM)
        rin = lax.broadcasted_iota(jnp.int32, (R // w, w, REC_DIM), 1)
        mid = jnp.sum(jnp.where(rin == h - 1, b3, 0.0), axis=1, keepdims=True)
        fac = jnp.exp(jnp.minimum(jnp.where(rin >= h, b3 - mid, mid - b3), 0.0)).reshape(R, REC_DIM)
        upper = (rows % w) >= h
        s_w = mm_nt(jnp.where(upper, q * fac, 0.0), jnp.where(upper, 0.0, k * fac))
        sc = sc + jnp.where((ri // w) == (ci // w), s_w, 0.0)
        w = h
    o = o + mm_nn(sc, v)

    g = R // DIAG
    q3, k3, v3, b3 = (t.reshape(g, DIAG, REC_DIM) for t in (q, k, v, b))
    rin = lax.broadcasted_iota(jnp.int32, (g, DIAG, 1), 1)
    od = jnp.zeros((g, DIAG, REC_DIM), F32)
    for d in range(DIAG):
        e = jnp.exp(jnp.minimum(b3 - roll_sub(b3, d), 0.0))
        sd = jnp.sum(q3 * roll_sub(k3, d) * e, axis=-1, keepdims=True)
        od = od + jnp.where(rin >= d, sd, 0.0) * roll_sub(v3, d)
    o = o + od.reshape(R, REC_DIM)

    b_last = _row(b, R - 1)
    return o, S * jnp.exp(jnp.minimum(b_last, 0.0)) + mm_tn(v, k * jnp.exp(jnp.minimum(b_last - b, 0.0)))


def _rec_head(core, qr, fr, v, z, S, l0, l1, gw):
    q, k, lf = _rec_front(qr, fr, l0, l1)
    o, S_new = core(q, k, v, cumsum_rows(lf), S)
    return _rec_tail(o, z, gw), S_new


def _rope_tables(positions):
    half = ROPE_DIM // 2
    inv_freq = ROPE_THETA ** (-(jnp.arange(half, dtype=F32) * 2.0 / ROPE_DIM))
    rest = jnp.zeros((HEAD_DIM - ROPE_DIM,), F32)
    ones, zeros = jnp.ones((half,), F32), jnp.zeros((half,), F32)
    per_lane = lambda first, second: jnp.tile(jnp.concatenate([first, second, rest]), LANES // HEAD_DIM)[None, :]
    ang = positions.astype(F32).reshape(-1, 1) * per_lane(inv_freq, inv_freq)
    sin = jnp.sin(ang)
    return jnp.cos(ang), sin * per_lane(zeros, ones), sin * per_lane(-ones, zeros)


def _rope(x, cos_t, sin_a, sin_b):
    half = ROPE_DIM // 2
    return x * cos_t + pltpu.roll(x, half, 1) * sin_a + pltpu.roll(x, LANES - half, 1) * sin_b


def _rope_transposed(g, cos_t, sin_a, sin_b):
    half = ROPE_DIM // 2
    return g * cos_t + pltpu.roll(g * sin_a, LANES - half, 1) + pltpu.roll(g * sin_b, half, 1)


def _row_spec(tm, width):
    return pl.BlockSpec((tm, width), lambda i: (i, 0))


def _full_spec(shape):
    return pl.BlockSpec(shape, lambda *_: (0,) * len(shape))


def attn_in_proj(x, w_pre, w_in, b_in, tables, tm=512):
    n = x.shape[0]

    def body(x_ref, wp_ref, w_ref, b_ref, c_ref, sa_ref, sb_ref, h_ref, q_ref, k_ref, v_ref, z_ref):
        xv = x_ref[...]
        h = (xv * _rms(xv) * wp_ref[...]).astype(BF16)
        h_ref[...] = h
        proj = jnp.dot(h, w_ref[...], preferred_element_type=F32) + b_ref[...]
        tabs = (c_ref[...], sa_ref[...], sb_ref[...])
        for s in range(D_MODEL // LANES):
            sl = slice(s * LANES, (s + 1) * LANES)
            q_ref[:, sl] = _rope(proj[:, sl] * (HEAD_DIM ** -0.5), *tabs).astype(BF16)
        k_ref[...] = _rope(proj[:, D_MODEL:D_MODEL + KV_WIDTH], *tabs).astype(BF16)
        v_ref[...] = proj[:, D_MODEL + KV_WIDTH:D_MODEL + 2 * KV_WIDTH].astype(BF16)
        z_ref[...] = proj[:, D_MODEL + 2 * KV_WIDTH:]

    return pl.pallas_call(
        body, name="attn_in_proj", grid=(n // tm,),
        in_specs=[_row_spec(tm, D_MODEL), _full_spec((1, D_MODEL)), _full_spec((D_MODEL, ATTN_IN)),
                  _full_spec((1, ATTN_IN))] + [_row_spec(tm, LANES)] * 3,
        out_specs=[_row_spec(tm, D_MODEL), _row_spec(tm, D_MODEL), _row_spec(tm, KV_WIDTH),
                   _row_spec(tm, KV_WIDTH), _row_spec(tm, D_MODEL)],
        out_shape=[jax.ShapeDtypeStruct((n, D_MODEL), BF16), jax.ShapeDtypeStruct((n, D_MODEL), BF16),
                   jax.ShapeDtypeStruct((n, KV_WIDTH), BF16), jax.ShapeDtypeStruct((n, KV_WIDTH), BF16),
                   jax.ShapeDtypeStruct((n, D_MODEL), F32)],
        compiler_params=_cparams(1),
    )(x, w_pre, w_in, b_in, *tables)


def rec_in_proj(x, w_pre, w_in, tm=256):
    n = x.shape[0]

    def body(x_ref, wp_ref, w_ref, h_ref, p_ref):
        xv = x_ref[...]
        h = (xv * _rms(xv) * wp_ref[...]).astype(BF16)
        h_ref[...] = h
        p_ref[...] = jnp.dot(h, w_ref[...], preferred_element_type=F32)

    return pl.pallas_call(
        body, name="rec_in_proj", grid=(n // tm,),
        in_specs=[_row_spec(tm, D_MODEL), _full_spec((1, D_MODEL)), _full_spec((D_MODEL, REC_IN))],
        out_specs=[_row_spec(tm, D_MODEL), _row_spec(tm, REC_IN)],
        out_shape=[jax.ShapeDtypeStruct((n, D_MODEL), BF16), jax.ShapeDtypeStruct((n, REC_IN), F32)],
        compiler_params=_cparams(1),
    )(x, w_pre, w_in)


def out_proj(og, w_out, b_out, x_res, w_post, target=None, tm=512):
    n = og.shape[0]
    with_loss = target is not None

    def body(*refs):
        if with_loss:
            og_ref, w_ref, b_ref, x_ref, wp_ref, t_ref, y_ref, dx_ref, l_ref = refs
        else:
            og_ref, w_ref, b_ref, x_ref, wp_ref, y_ref, xo_ref = refs
        y = jnp.dot(og_ref[...], w_ref[...], preferred_element_type=F32) + b_ref[...]
        y_ref[...] = y
        xo = x_ref[...] + y * _rms(y) * wp_ref[...]
        if with_loss:
            err = xo - t_ref[...]
            dx_ref[...] = err * (1.0 / D_MODEL)

            @pl.when(pl.program_id(0) == 0)
            def _():
                l_ref[...] = jnp.zeros_like(l_ref)

            l_ref[...] += jnp.sum(err * err, axis=0, keepdims=True)
        else:
            xo_ref[...] = xo

    in_specs = [_row_spec(tm, D_MODEL), _full_spec((D_MODEL, D_MODEL)), _full_spec((1, D_MODEL)),
                _row_spec(tm, D_MODEL), _full_spec((1, D_MODEL))]
    out_specs = [_row_spec(tm, D_MODEL), _row_spec(tm, D_MODEL)]
    out_shape = [jax.ShapeDtypeStruct((n, D_MODEL), F32), jax.ShapeDtypeStruct((n, D_MODEL), F32)]
    args = [og, w_out, b_out, x_res, w_post]
    if with_loss:
        in_specs.append(_row_spec(tm, D_MODEL))
        out_specs.append(_full_spec((1, D_MODEL)))
        out_shape.append(jax.ShapeDtypeStruct((1, D_MODEL), F32))
        args.append(target)
    return pl.pallas_call(
        body, name="out_proj_loss" if with_loss else "out_proj", grid=(n // tm,),
        in_specs=in_specs, out_specs=out_specs, out_shape=out_shape, compiler_params=_cparams(1),
    )(*args)


def out_proj_bwd(dxo, y, og, w_out, w_post, tm=512):
    n = og.shape[0]

    def body(g_ref, y_ref, og_ref, w_ref, wp_ref, dog_ref, dw_ref, db_ref, dwp_ref):
        @pl.when(pl.program_id(0) == 0)
        def _():
            dw_ref[...] = jnp.zeros_like(dw_ref)
            db_ref[...] = jnp.zeros_like(db_ref)
            dwp_ref[...] = jnp.zeros_like(dwp_ref)

        g, y = g_ref[...], y_ref[...]
        rstd = _rms(y)
        yn = y * rstd
        gw = g * wp_ref[...]
        dwp_ref[...] += jnp.sum(g * yn, axis=0, keepdims=True)
        dy = rstd * (gw - yn * jnp.mean(gw * yn, axis=-1, keepdims=True))
        db_ref[...] += jnp.sum(dy, axis=0, keepdims=True)
        dyb = dy.astype(BF16)
        dog_ref[...] = _dot(dyb, w_ref[...], _NT)
        dw_ref[...] += _dot(og_ref[...], dyb, _TN)

    return pl.pallas_call(
        body, name="out_proj_bwd", grid=(n // tm,),
        in_specs=[_row_spec(tm, D_MODEL), _row_spec(tm, D_MODEL), _row_spec(tm, D_MODEL),
                  _full_spec((D_MODEL, D_MODEL)), _full_spec((1, D_MODEL))],
        out_specs=[_row_spec(tm, D_MODEL), _full_spec((D_MODEL, D_MODEL)), _full_spec((1, D_MODEL)),
                   _full_spec((1, D_MODEL))],
        out_shape=[jax.ShapeDtypeStruct((n, D_MODEL), F32), jax.ShapeDtypeStruct((D_MODEL, D_MODEL), F32),
                   jax.ShapeDtypeStruct((1, D_MODEL), F32), jax.ShapeDtypeStruct((1, D_MODEL), F32)],
        compiler_params=_cparams(1),
    )(dxo, y, og, w_out, w_post)


def in_proj_bwd_x(dproj, w_in, x, w_pre, dxo, tm=512):
    n, p = dproj.shape

    def body(dp_ref, w_ref, x_ref, wp_ref, g_ref, dx_ref, dwp_ref):
        @pl.when(pl.program_id(0) == 0)
        def _():
            dwp_ref[...] = jnp.zeros_like(dwp_ref)

        dh = _dot(dp_ref[...], w_ref[...], _NT)
        xv = x_ref[...]
        rstd = _rms(xv)
        xn = xv * rstd
        gw = dh * wp_ref[...]
        dwp_ref[...] += jnp.sum(dh * xn, axis=0, keepdims=True)
        dx_ref[...] = rstd * (gw - xn * jnp.mean(gw * xn, axis=-1, keepdims=True)) + g_ref[...]

    return pl.pallas_call(
        body, name=f"in_proj_bwd_x_{p}", grid=(n // tm,),
        in_specs=[_row_spec(tm, p), _full_spec((D_MODEL, p)), _row_spec(tm, D_MODEL), _full_spec((1, D_MODEL)),
                  _row_spec(tm, D_MODEL)],
        out_specs=[_row_spec(tm, D_MODEL), _full_spec((1, D_MODEL))],
        out_shape=[jax.ShapeDtypeStruct((n, D_MODEL), F32), jax.ShapeDtypeStruct((1, D_MODEL), F32)],
        compiler_params=_cparams(1),
    )(dproj, w_in, x, w_pre, dxo)


def in_proj_bwd_w(h, dproj, tm=512):
    n, p = dproj.shape
    chunk = p // (4 if p % 4096 == 0 else 3)
    steps = n // tm

    def body(h_ref, dp_ref, dw_ref, db_ref, acc_scr, sem):
        i = pl.program_id(0)

        @pl.when(i == 0)
        def _():
            acc_scr[...] = jnp.zeros_like(acc_scr)
            db_ref[...] = jnp.zeros_like(db_ref)

        ht = h_ref[...].T
        for c0 in range(0, p, chunk):
            dp = dp_ref[:, c0:c0 + chunk]
            acc_scr[:, c0:c0 + chunk] += jnp.dot(ht, dp, preferred_element_type=F32)
            db_ref[:, c0:c0 + chunk] += jnp.sum(dp.astype(F32), axis=0, keepdims=True)

        @pl.when(i == steps - 1)
        def _():
            out = pltpu.make_async_copy(acc_scr, dw_ref, sem)
            out.start()
            out.wait()

    return pl.pallas_call(
        body, name=f"in_proj_bwd_w_{p}", grid=(steps,),
        in_specs=[_row_spec(tm, D_MODEL), _row_spec(tm, p)],
        out_specs=[_ANY, _full_spec((1, p))],
        out_shape=[jax.ShapeDtypeStruct((D_MODEL, p), F32), jax.ShapeDtypeStruct((1, p), F32)],
        scratch_shapes=[pltpu.VMEM((D_MODEL, p), F32), pltpu.SemaphoreType.DMA],
        compiler_params=_cparams(1),
    )(h, dproj)


PAIRS = GROUP // 2
GROUP_ROWS = PAIRS * ATTN_BLOCK
MASKED = -1e30


def _kv_windows(k_ref, v_ref, i):
    ps = pl.multiple_of(jnp.maximum(i - 1, 0) * ATTN_BLOCK, ATTN_BLOCK)
    cs = pl.multiple_of(i * ATTN_BLOCK, ATTN_BLOCK)
    kw = jnp.concatenate([k_ref[pl.ds(ps, ATTN_BLOCK), :], k_ref[pl.ds(cs, ATTN_BLOCK), :]], axis=0)
    vw = jnp.concatenate([v_ref[pl.ds(ps, ATTN_BLOCK), :], v_ref[pl.ds(cs, ATTN_BLOCK), :]], axis=0)
    return kw.astype(F32), vw.astype(F32), ps, cs


def _low_lanes(shape):
    return lax.broadcasted_iota(jnp.int32, shape, 1) < HEAD_DIM


def _spread(w, kvh):
    low = _low_lanes(w.shape)
    swapped = pltpu.roll(w, HEAD_DIM, 1)
    if kvh == 0:
        return jnp.where(low, w, 0.0), jnp.where(low, 0.0, swapped)
    return jnp.where(low, swapped, 0.0), jnp.where(low, 0.0, w)


def _unspread(d_a, d_b, kvh):
    low = _low_lanes(d_a.shape)
    if kvh == 0:
        return jnp.where(low, d_a + pltpu.roll(d_b, HEAD_DIM, 1), 0.0)
    return jnp.where(low, 0.0, pltpu.roll(d_a, HEAD_DIM, 1) + d_b)


def _stack_pairs(ref, kvh):
    return jnp.concatenate([ref[:, (kvh * PAIRS + j) * LANES:(kvh * PAIRS + j + 1) * LANES] for j in range(PAIRS)],
                           axis=0)


def _fill_bias(bias_scr):
    shape = (GROUP_ROWS, 2 * ATTN_BLOCK)
    r = lax.broadcasted_iota(jnp.int32, shape, 0) % ATTN_BLOCK
    c = lax.broadcasted_iota(jnp.int32, shape, 1)
    in_cur = (c >= ATTN_BLOCK) & ((c - ATTN_BLOCK) <= r)
    in_prev = (c < ATTN_BLOCK) & (c > r)
    bias_scr[0] = jnp.where(in_cur, 0.0, MASKED)
    bias_scr[1] = jnp.where(in_cur | in_prev, 0.0, MASKED)


def _sink_table(sinks):
    t = jnp.transpose(sinks.reshape(N_KV_HEADS, PAIRS, 2), (0, 2, 1))
    return jnp.broadcast_to(t[:, :, :, None, None], (N_KV_HEADS, 2, PAIRS, ATTN_BLOCK, LANES)).reshape(
        N_KV_HEADS, 2, GROUP_ROWS, LANES)


def attn_fwd(q, k, v, z, sink_tab, batch, seq, gather=()):
    nb = seq // ATTN_BLOCK
    ng = len(gather)

    def body(*refs):
        q_ref, k_ref, v_ref, z_ref, s_ref = refs[:5]
        og_ref, bias_scr = refs[5 + ng], refs[6 + 2 * ng]
        exchange = (refs[5:5 + ng], refs[6 + ng:6 + 2 * ng]) + tuple(refs[7 + 2 * ng:])
        b, i = pl.program_id(0), pl.program_id(1)

        @pl.when((b == 0) & (i == 0))
        def _():
            _fill_bias(bias_scr)
            if ng:
                _gather_start(*exchange)

        kw, vw, _, _ = _kv_windows(k_ref, v_ref, i)
        bias = bias_scr[jnp.minimum(i, 1)]
        for kvh in range(N_KV_HEADS):
            k_a, k_b = _spread(kw, kvh)
            v_a, v_b = _spread(vw, kvh)
            og = _attn_group(_stack_pairs(q_ref, kvh), k_a, v_a, k_b, v_b, _stack_pairs(z_ref, kvh),
                             s_ref[kvh, 0], s_ref[kvh, 1], bias)
            for j in range(PAIRS):
                og_ref[:, (kvh * PAIRS + j) * LANES:(kvh * PAIRS + j + 1) * LANES] = (
                    og[j * ATTN_BLOCK:(j + 1) * ATTN_BLOCK].astype(BF16))

        if ng:
            @pl.when((b == batch - 1) & (i == nb - 1))
            def _():
                _gather_finish(*exchange)

    blk = lambda w: pl.BlockSpec((ATTN_BLOCK, w), lambda b, i: (b * nb + i, 0))
    seq_spec = pl.BlockSpec((seq, KV_WIDTH), lambda b, i: (b, 0))
    out = pl.pallas_call(
        body, name="attn_fwd", grid=(batch, nb),
        in_specs=[blk(D_MODEL), seq_spec, seq_spec, blk(D_MODEL), _full_spec(sink_tab.shape)] + [_ANY] * ng,
        out_specs=[blk(D_MODEL)] + [_ANY] * ng,
        out_shape=[jax.ShapeDtypeStruct((batch * seq, D_MODEL), BF16)]
        + [jax.ShapeDtypeStruct((N_CHIPS,) + a.shape, a.dtype) for a in gather],
        scratch_shapes=[pltpu.VMEM((2, GROUP_ROWS, 2 * ATTN_BLOCK), F32)] + (_gather_sems(ng) if ng else []),
        compiler_params=_cparams(2),
    )(q, k, v, z, sink_tab, *gather)
    return out[0], out[1:]


def attn_bwd(q, k, v, z, sink_tab, dog, tables, batch, seq, scatter=()):
    nb = seq // ATTN_BLOCK
    ns = len(scatter)

    def body(*refs):
        q_ref, k_ref, v_ref, z_ref, s_ref, g_ref, c_ref, sa_ref, sb_ref = refs[:9]
        dp_ref, dk_ref, dv_ref, ds_ref = refs[9 + ns:13 + ns]
        bias_scr = refs[13 + 2 * ns]
        exchange = (refs[9:9 + ns], refs[13 + ns:13 + 2 * ns]) + tuple(refs[14 + 2 * ns:])
        b, i = pl.program_id(0), pl.program_id(1)

        @pl.when((b == 0) & (i == 0))
        def _():
            _fill_bias(bias_scr)
            ds_ref[...] = jnp.zeros_like(ds_ref)
            if ns:
                _scatter_start(*exchange)

        @pl.when(i == 0)
        def _():
            dk_ref[...] = jnp.zeros_like(dk_ref)
            dv_ref[...] = jnp.zeros_like(dv_ref)

        kw, vw, ps, cs = _kv_windows(k_ref, v_ref, i)
        bias = bias_scr[jnp.minimum(i, 1)]
        tabs = (c_ref[...], sa_ref[...], sb_ref[...])
        dkw = jnp.zeros_like(kw)
        dvw = jnp.zeros_like(vw)
        for kvh in range(N_KV_HEADS):
            k_a, k_b = _spread(kw, kvh)
            v_a, v_b = _spread(vw, kvh)
            _, vjp = jax.vjp(functools.partial(_attn_group, bias=bias), _stack_pairs(q_ref, kvh).astype(F32),
                             k_a, v_a, k_b, v_b, _stack_pairs(z_ref, kvh), s_ref[kvh, 0], s_ref[kvh, 1])
            dqs, dk_a, dv_a, dk_b, dv_b, dzs, ds_a, ds_b = vjp(_stack_pairs(g_ref, kvh))
            dkw = dkw + _unspread(dk_a, dk_b, kvh)
            dvw = dvw + _unspread(dv_a, dv_b, kvh)
            ds_ref[kvh, 0] += jnp.sum(ds_a.reshape(PAIRS, ATTN_BLOCK, LANES), axis=1)
            ds_ref[kvh, 1] += jnp.sum(ds_b.reshape(PAIRS, ATTN_BLOCK, LANES), axis=1)
            for j in range(PAIRS):
                rows = slice(j * ATTN_BLOCK, (j + 1) * ATTN_BLOCK)
                col = (kvh * PAIRS + j) * LANES
                dp_ref[:, col:col + LANES] = _rope_transposed(dqs[rows] * (HEAD_DIM ** -0.5), *tabs).astype(BF16)
                zc = D_MODEL + 2 * KV_WIDTH + col
                dp_ref[:, zc:zc + LANES] = dzs[rows].astype(BF16)
        dp_ref[:, D_MODEL:D_MODEL + 2 * KV_WIDTH] = jnp.zeros((ATTN_BLOCK, 2 * KV_WIDTH), BF16)
        dk_ref[pl.ds(ps, ATTN_BLOCK), :] += dkw[:ATTN_BLOCK]
        dk_ref[pl.ds(cs, ATTN_BLOCK), :] += dkw[ATTN_BLOCK:]
        dv_ref[pl.ds(ps, ATTN_BLOCK), :] += dvw[:ATTN_BLOCK]
        dv_ref[pl.ds(cs, ATTN_BLOCK), :] += dvw[ATTN_BLOCK:]

        if ns:
            @pl.when((b == batch - 1) & (i == nb - 1))
            def _():
                _scatter_finish(*exchange)

    blk = lambda w: pl.BlockSpec((ATTN_BLOCK, w), lambda b, i: (b * nb + i, 0))
    seq_spec = pl.BlockSpec((seq, KV_WIDTH), lambda b, i: (b, 0))
    n = batch * seq
    ds_shape = (N_KV_HEADS, 2, PAIRS, LANES)
    out = pl.pallas_call(
        body, name="attn_bwd", grid=(batch, nb),
        in_specs=[blk(D_MODEL), seq_spec, seq_spec, blk(D_MODEL), _full_spec(sink_tab.shape), blk(D_MODEL)]
        + [blk(LANES)] * 3 + [_ANY] * ns,
        out_specs=[blk(ATTN_IN), seq_spec, seq_spec, _full_spec(ds_shape)] + [_ANY] * ns,
        out_shape=[jax.ShapeDtypeStruct((n, ATTN_IN), BF16), jax.ShapeDtypeStruct((n, KV_WIDTH), F32),
                   jax.ShapeDtypeStruct((n, KV_WIDTH), F32), jax.ShapeDtypeStruct(ds_shape, F32)]
        + [jax.ShapeDtypeStruct(a.shape, a.dtype) for a in scatter],
        scratch_shapes=[pltpu.VMEM((2, GROUP_ROWS, 2 * ATTN_BLOCK), F32)] + (_scatter_sems(ns) if ns else []),
        compiler_params=_cparams(2),
    )(q, k, v, z, sink_tab, dog, *tables, *scatter)
    return out[0], out[1], out[2], out[3], out[4:]


def attn_bwd_kv(dproj, dk, dv, tables, tm=512):
    n = dproj.shape[0]

    def body(dp_in_ref, dk_ref, dv_ref, c_ref, sa_ref, sb_ref, dp_ref):
        del dp_in_ref
        dp_ref[:, :KV_WIDTH] = _rope_transposed(dk_ref[...], c_ref[...], sa_ref[...], sb_ref[...]).astype(BF16)
        dp_ref[:, KV_WIDTH:] = dv_ref[...].astype(BF16)

    kv_cols = pl.BlockSpec((tm, 2 * KV_WIDTH), lambda i: (i, D_MODEL // (2 * KV_WIDTH)))
    return pl.pallas_call(
        body, name="attn_bwd_kv", grid=(n // tm,),
        in_specs=[kv_cols, _row_spec(tm, KV_WIDTH), _row_spec(tm, KV_WIDTH)] + [_row_spec(tm, LANES)] * 3,
        out_specs=kv_cols, out_shape=jax.ShapeDtypeStruct(dproj.shape, BF16),
        input_output_aliases={0: 0}, compiler_params=_cparams(1),
    )(dproj, dk, dv, *tables)


def _rec_cols(part, h):
    return slice(part * D_MODEL + h * REC_DIM, part * D_MODEL + (h + 1) * REC_DIM)


def _rec_args(p_ref, lb_ref, gw_ref, h, S):
    hs = slice(h * REC_DIM, (h + 1) * REC_DIM)
    return (p_ref[:, _rec_cols(0, h)], p_ref[:, _rec_cols(1, h)], p_ref[:, _rec_cols(2, h)],
            p_ref[:, _rec_cols(3, h)], S, lb_ref[0:1, hs], lb_ref[1:2, hs], gw_ref[...])


def rec_fwd(proj, lb_logits, gnorm_w, batch, seq):
    nblk = seq // REC_BLOCK

    def body(p_ref, lb_ref, gw_ref, og_ref, st_ref, safe_ref, s_scr):
        @pl.when(pl.program_id(1) == 0)
        def _():
            s_scr[...] = jnp.zeros_like(s_scr)

        S = s_scr[...]
        st_ref[0] = S
        qr, fr, v, z = (p_ref[:, part * D_MODEL:(part + 1) * D_MODEL] for part in range(4))
        lf, k = forget_gate(fr, lb_ref[1:2, :] - lb_ref[0:1, :])
        q, b = silu(qr), cumsum_rows(lf)
        safe = jnp.min(_rec_margin(b)) >= -SAFE_RANGE

        def head_by_head():
            outs = [_rec_core_slow(*args) for args in zip(*(_heads(t) for t in (q, k, v, b, S)))]
            return tuple(jnp.concatenate(parts, axis=1) for parts in zip(*outs))

        o, S_new = lax.cond(safe, lambda: _rec_cores_fast(q, k, v, b, S), head_by_head)
        og_ref[...] = _rec_tails(o, z, gw_ref[...]).astype(BF16)
        s_scr[...] = S_new
        safe_ref[0] = jnp.full((REC_HEADS, LANES), safe.astype(F32))

    blk = lambda w: pl.BlockSpec((REC_BLOCK, w), lambda b, j: (b * nblk + j, 0))
    st_spec = pl.BlockSpec((1, REC_DIM, D_MODEL), lambda b, j: (b * nblk + j, 0, 0))
    safe_spec = pl.BlockSpec((1, REC_HEADS, LANES), lambda b, j: (b * nblk + j, 0, 0))
    return pl.pallas_call(
        body, name="rec_fwd", grid=(batch, nblk),
        in_specs=[blk(REC_IN), _full_spec((2, D_MODEL)), _full_spec((1, D_MODEL))],
        out_specs=[blk(D_MODEL), st_spec, safe_spec],
        out_shape=[jax.ShapeDtypeStruct((batch * seq, D_MODEL), BF16),
                   jax.ShapeDtypeStruct((batch * nblk, REC_DIM, D_MODEL), F32),
                   jax.ShapeDtypeStruct((batch * nblk, REC_HEADS, LANES), F32)],
        scratch_shapes=[pltpu.VMEM((REC_DIM, D_MODEL), F32)],
        compiler_params=_cparams(2),
    )(proj, lb_logits, jnp.tile(gnorm_w, (1, REC_HEADS)))


def rec_bwd(proj, states, safe, lb_logits, gnorm_w, dog, batch, seq):
    nblk = seq // REC_BLOCK

    def body(p_ref, st_ref, safe_ref, lb_ref, gw_ref, g_ref, dp_ref, dlb_ref, dgw_ref, ds_scr):
        @pl.when((pl.program_id(0) == 0) & (pl.program_id(1) == 0))
        def _():
            dlb_ref[...] = jnp.zeros_like(dlb_ref)
            dgw_ref[...] = jnp.zeros_like(dgw_ref)

        @pl.when(pl.program_id(1) == 0)
        def _():
            ds_scr[...] = jnp.zeros_like(ds_scr)

        primals = tuple(p_ref[:, part * D_MODEL:(part + 1) * D_MODEL] for part in range(4)) + (
            st_ref[0], lb_ref[0:1, :], lb_ref[1:2, :], gw_ref[...])
        cotangents = (g_ref[...], ds_scr[...])

        def all_heads():
            return jax.vjp(_rec_block_fast, *primals)[1](cotangents)

        def head_by_head():
            outs = [jax.vjp(functools.partial(_rec_head, _rec_core_slow), *args)[1](cts)
                    for args, cts in zip(zip(*(_heads(t) for t in primals)), zip(*(_heads(t) for t in cotangents)))]
            return tuple(jnp.concatenate(parts, axis=1) for parts in zip(*outs))

        dqr, dfr, dv, dz, dS, dl0, dl1, dgw = lax.cond(jnp.max(safe_ref[0]) > 0.5, all_heads, head_by_head)
        for part, val in enumerate((dqr, dfr, dv, dz)):
            dp_ref[:, part * D_MODEL:(part + 1) * D_MODEL] = val.astype(BF16)
        ds_scr[...] = dS
        dlb_ref[0:1, :] += dl0
        dlb_ref[1:2, :] += dl1
        dgw_ref[...] += functools.reduce(jnp.add, _heads(dgw))

    blk = lambda w: pl.BlockSpec((REC_BLOCK, w), lambda b, j: (b * nblk + nblk - 1 - j, 0))
    st_spec = pl.BlockSpec((1, REC_DIM, D_MODEL), lambda b, j: (b * nblk + nblk - 1 - j, 0, 0))
    safe_spec = pl.BlockSpec((1, REC_HEADS, LANES), lambda b, j: (b * nblk + nblk - 1 - j, 0, 0))
    return pl.pallas_call(
        body, name="rec_bwd", grid=(batch, nblk),
        in_specs=[blk(REC_IN), st_spec, safe_spec, _full_spec((2, D_MODEL)), _full_spec((1, D_MODEL)),
                  blk(D_MODEL)],
        out_specs=[blk(REC_IN), _full_spec((2, D_MODEL)), _full_spec((1, REC_DIM))],
        out_shape=[jax.ShapeDtypeStruct((batch * seq, REC_IN), BF16), jax.ShapeDtypeStruct((2, D_MODEL), F32),
                   jax.ShapeDtypeStruct((1, REC_DIM), F32)],
        scratch_shapes=[pltpu.VMEM((REC_DIM, D_MODEL), F32)],
        compiler_params=_cparams(2),
    )(proj, states, safe, lb_logits, jnp.tile(gnorm_w, (1, REC_HEADS)), dog)


_ANY = pl.BlockSpec(memory_space=pl.ANY)


def _chip_peers():
    x, y, c = lax.axis_index("x"), lax.axis_index("y"), lax.axis_index("c")
    peers = []
    for fx, fy in ((1, 0), (0, 1), (1, 1)):
        px, py = (1 - x if fx else x), (1 - y if fy else y)
        peers.append(((px, py, c), 2 * px + py))
    return 2 * x + y, peers


def _remote(src, dst, send_sem, recv_sem, device):
    return pltpu.make_async_remote_copy(src_ref=src, dst_ref=dst, send_sem=send_sem, recv_sem=recv_sem,
                                        device_id=device, device_id_type=MESH)


N_FLIPS = N_CHIPS - 1


def _scatter_sems(n):
    return [pltpu.SemaphoreType.DMA((n * N_FLIPS,)), pltpu.SemaphoreType.DMA((n * N_FLIPS,)),
            pltpu.SemaphoreType.DMA((n,))]


def _scatter_copies(ins, outs, send_sems, recv_sems, local_sems, starting):
    me, peers = _chip_peers()
    local = [pltpu.make_async_copy(ins[k].at[me], outs[k].at[me], local_sems.at[k]) for k in range(len(ins))]
    sends, arrivals = [], []
    for k in range(len(ins)):
        for j, (device, idx) in enumerate(peers):
            sems = (send_sems.at[k * N_FLIPS + j], recv_sems.at[k * N_FLIPS + j], device)
            sends.append(_remote(ins[k].at[idx], outs[k].at[me], *sems))
            if not starting:
                arrivals.append(_remote(ins[k].at[me], outs[k].at[idx], *sems))
    return local, sends, arrivals


def _scatter_start(*refs):
    local, sends, _ = _scatter_copies(*refs, starting=True)
    for cp in local + sends:
        cp.start()


def _scatter_finish(*refs):
    local, sends, arrivals = _scatter_copies(*refs, starting=False)
    for cp in arrivals:
        cp.wait_recv()
    for cp in sends:
        cp.wait_send()
    for cp in local:
        cp.wait()


def chip_scatter(arrays):
    n = len(arrays)

    def body(*refs):
        _scatter_start(refs[:n], refs[n:2 * n], *refs[2 * n:])
        _scatter_finish(refs[:n], refs[n:2 * n], *refs[2 * n:])

    return pl.pallas_call(
        body, name="chip_scatter", in_specs=[_ANY] * n, out_specs=[_ANY] * n,
        out_shape=[jax.ShapeDtypeStruct(a.shape, a.dtype) for a in arrays], scratch_shapes=_scatter_sems(n),
    )(*arrays)


def _gather_sems(n):
    return [pltpu.SemaphoreType.DMA((n * N_FLIPS,)) for _ in range(4)] + [pltpu.SemaphoreType.DMA((n,))]


def _gather_copies(ins, outs, send_sems, recv_sems, pass_send_sems, pass_recv_sems, local_sems, starting):
    me, peers = _chip_peers()
    c = lax.axis_index("c")
    sibling = (lax.axis_index("x"), lax.axis_index("y"), 1 - c)
    local = [pltpu.make_async_copy(ins[k], outs[k].at[me], local_sems.at[k]) for k in range(len(ins))]
    sends, arrivals, passes, pass_arrivals = [], [], [], []
    for k in range(len(ins)):
        half = ins[k].shape[0] // 2
        mine, other = pl.ds(c * half, half), pl.ds((1 - c) * half, half)
        for j, (device, idx) in enumerate(peers):
            s = k * N_FLIPS + j
            sends.append(_remote(ins[k].at[mine], outs[k].at[me].at[mine], send_sems.at[s], recv_sems.at[s], device))
            if starting:
                continue
            arrived = outs[k].at[idx].at[mine]
            arrivals.append(_remote(ins[k].at[mine], arrived, send_sems.at[s], recv_sems.at[s], device))
            passes.append(_remote(arrived, arrived, pass_send_sems.at[s], pass_recv_sems.at[s], sibling))
            passed = outs[k].at[idx].at[other]
            pass_arrivals.append(_remote(passed, passed, pass_send_sems.at[s], pass_recv_sems.at[s], sibling))
    return local, sends, arrivals, passes, pass_arrivals


def _gather_start(*refs):
    local, sends, _, _, _ = _gather_copies(*refs, starting=True)
    for cp in local + sends:
        cp.start()


def _gather_finish(*refs):
    local, sends, arrivals, passes, pass_arrivals = _gather_copies(*refs, starting=False)
    for arrival, onward in zip(arrivals, passes):
        arrival.wait_recv()
        onward.start()
    for cp in pass_arrivals:
        cp.wait_recv()
    for cp in sends + passes:
        cp.wait_send()
    for cp in local:
        cp.wait()


def chip_gather(arrays):
    n = len(arrays)

    def body(*refs):
        _gather_start(refs[:n], refs[n:2 * n], *refs[2 * n:])
        _gather_finish(refs[:n], refs[n:2 * n], *refs[2 * n:])

    return pl.pallas_call(
        body, name="chip_gather", in_specs=[_ANY] * n, out_specs=[_ANY] * n,
        out_shape=[jax.ShapeDtypeStruct((N_CHIPS,) + a.shape, a.dtype) for a in arrays],
        scratch_shapes=_gather_sems(n),
    )(*arrays)


def sibling_exchange(arrays):
    n = len(arrays)

    def body(*refs):
        ins, outs = refs[:n], refs[n:2 * n]
        send_sems, recv_sems = refs[2 * n:]
        sibling = (lax.axis_index("x"), lax.axis_index("y"), 1 - lax.axis_index("c"))
        copies = [pltpu.make_async_remote_copy(src_ref=ins[k], dst_ref=outs[k], send_sem=send_sems.at[k],
                                               recv_sem=recv_sems.at[k], device_id=sibling, device_id_type=MESH)
                  for k in range(n)]
        for cp in copies:
            cp.start()
        for cp in copies:
            cp.wait()

    return pl.pallas_call(
        body, name="sibling_exchange", in_specs=[_ANY] * n, out_specs=[_ANY] * n,
        out_shape=[jax.ShapeDtypeStruct(a.shape, a.dtype) for a in arrays],
        scratch_shapes=[pltpu.SemaphoreType.DMA((n,)), pltpu.SemaphoreType.DMA((n,))],
    )(*arrays)


def all_gather_small(vec):
    def body(v_ref, out_ref, send_sems, recv_sems, local_sem):
        x, y, c = lax.axis_index("x"), lax.axis_index("y"), lax.axis_index("c")
        me = 4 * x + 2 * y + c
        local = pltpu.make_async_copy(v_ref, out_ref.at[me], local_sem)
        local.start()
        sends, recvs = [], []
        for j in range(1, N_DEV):
            px = jnp.where(j & 4, 1 - x, x)
            py = jnp.where(j & 2, 1 - y, y)
            pc = jnp.where(j & 1, 1 - c, c)
            common = dict(send_sem=send_sems.at[j - 1], recv_sem=recv_sems.at[j - 1], device_id=(px, py, pc),
                          device_id_type=MESH)
            sends.append(pltpu.make_async_remote_copy(src_ref=v_ref, dst_ref=out_ref.at[me], **common))
            recvs.append(pltpu.make_async_remote_copy(src_ref=v_ref, dst_ref=out_ref.at[4 * px + 2 * py + pc],
                                                      **common))
        for cp in sends:
            cp.start()
        for cp in recvs:
            cp.wait_recv()
        for cp in sends:
            cp.wait_send()
        local.wait()

    return pl.pallas_call(
        body, name="all_gather_small", in_specs=[_ANY], out_specs=_ANY,
        out_shape=jax.ShapeDtypeStruct((N_DEV,) + vec.shape, vec.dtype),
        scratch_shapes=[pltpu.SemaphoreType.DMA((N_DEV - 1,)), pltpu.SemaphoreType.DMA((N_DEV - 1,)),
                        pltpu.SemaphoreType.DMA],
    )(vec)


def sum_slots(stacked, tm=256):
    s, r, c = stacked.shape
    tm = min(tm, r)

    def body(in_ref, out_ref):
        acc = in_ref[0].astype(F32)
        for t in range(1, s):
            acc = acc + in_ref[t].astype(F32)
        out_ref[...] = acc

    return pl.pallas_call(
        body, name=f"sum_slots_{s}_{r}_{c}", grid=(r // tm,),
        in_specs=[pl.BlockSpec((s, tm, c), lambda i: (0, i, 0))], out_specs=_row_spec(tm, c),
        out_shape=jax.ShapeDtypeStruct((r, c), F32), compiler_params=_cparams(1),
    )(stacked)


def adamw(w, m, v, g_a, g_b=None, tm=256):
    r, c = w.shape
    tm = min(tm, r)
    two = g_b is not None

    def body(*refs):
        w_ref, m_ref, v_ref, ga_ref = refs[:4]
        g_ref, d_ref, nm_ref, nv_ref = refs[-4:]
        g = ga_ref[...] + refs[4][...] if two else ga_ref[...]
        nm = ADAM_B1 * m_ref[...] + (1.0 - ADAM_B1) * g
        nv = ADAM_B2 * v_ref[...] + (1.0 - ADAM_B2) * (g * g)
        m_hat = nm / (1.0 - ADAM_B1 ** ADAM_STEP)
        v_hat = nv / (1.0 - ADAM_B2 ** ADAM_STEP)
        g_ref[...] = g
        d_ref[...] = -ADAM_LR * (m_hat / (jnp.sqrt(v_hat) + ADAM_EPS) + ADAM_WD * w_ref[...])
        nm_ref[...] = nm
        nv_ref[...] = nv

    args = [w, m, v, g_a] + ([g_b] if two else [])
    return pl.pallas_call(
        body, name=f"adamw_{r}_{c}", grid=(r // tm,),
        in_specs=[_row_spec(tm, c)] * len(args), out_specs=[_row_spec(tm, c)] * 4,
        out_shape=[jax.ShapeDtypeStruct((r, c), F32)] * 4, compiler_params=_cparams(1),
    )(*args)


_SMALL = (("pre_norm_w", (2, D_MODEL)), ("post_norm_w", (2, D_MODEL)), ("attn_b_in", (1, ATTN_IN)),
          ("attn_sinks", (1, N_HEADS)), ("attn_b_out", (1, D_MODEL)), ("rec_lb_logits", (2, D_MODEL)),
          ("rec_gnorm_w", (1, REC_DIM)))
_SMALL_ROWS = 16


def _pack_small(parts):
    rows = []
    for (name, shape) in _SMALL:
        flat = parts[name].reshape(-1)
        pad = -flat.shape[0] % D_MODEL
        rows.append(jnp.pad(flat, (0, pad)).reshape(-1, D_MODEL))
    packed = jnp.concatenate(rows, axis=0)
    return jnp.pad(packed, ((0, _SMALL_ROWS - packed.shape[0]), (0, 0)))


def _unpack_small(packed):
    out, row = {}, 0
    for (name, shape) in _SMALL:
        size = shape[0] * shape[1]
        nrows = -(-size // D_MODEL)
        out[name] = packed[row:row + nrows].reshape(-1)[:size].reshape(shape)
        row += nrows
    return out


_CARRIED = ("rec_w_in", "rec_w_out", "attn_w_out")


def local_step(x, positions, pre_norm_w, post_norm_w, attn_w_in, attn_b_in, attn_sinks, attn_w_out, attn_b_out,
               rec_w_in, rec_lb_logits, rec_gnorm_w, rec_w_out, loss_target, distributed=False):
    batch, seq, _ = x.shape
    n = batch * seq
    x0 = x.reshape(n, D_MODEL)
    tables = _rope_tables(positions)
    pre0, pre1 = pre_norm_w[0:1], pre_norm_w[1:2]
    post0, post1 = post_norm_w[0:1], post_norm_w[1:2]
    no_bias = jnp.zeros((1, D_MODEL), F32)

    h0, q, k, v, z = attn_in_proj(x0, pre0, attn_w_in, attn_b_in, tables)
    sink_tab = _sink_table(attn_sinks)
    og0, gathered = attn_fwd(q, k, v, z, sink_tab, batch, seq, gather=(rec_w_in, rec_w_out) if distributed else ())
    if distributed:
        rec_w_in, rec_w_out = (_whole_from_shards(name, g) for name, g in zip(("rec_w_in", "rec_w_out"), gathered))
    y0, x1 = out_proj(og0, attn_w_out, attn_b_out, x0, post0)

    h1, proj1 = rec_in_proj(x1, pre1, rec_w_in)
    og1, states, safe = rec_fwd(proj1, rec_lb_logits, rec_gnorm_w, batch, seq)
    y1, dx2, loss_vec = out_proj(og1, rec_w_out, no_bias, x1, post1, target=loss_target.reshape(n, D_MODEL))
    loss = jnp.sum(loss_vec) * (0.5 / D_MODEL)

    dog1, d_rec_w_out, _, d_post1 = out_proj_bwd(dx2, y1, og1, rec_w_out, post1)
    dproj1, d_lb, d_gnorm = rec_bwd(proj1, states, safe, rec_lb_logits, rec_gnorm_w, dog1, batch, seq)
    dx1, d_pre1 = in_proj_bwd_x(dproj1, rec_w_in, x1, pre1, dx2)
    d_rec_w_in, _ = in_proj_bwd_w(h1, dproj1)

    dog0, d_attn_w_out, d_attn_b_out, d_post0 = out_proj_bwd(dx1, y0, og0, attn_w_out, post0)
    ready = dict(rec_w_in=d_rec_w_in, rec_w_out=d_rec_w_out, attn_w_out=d_attn_w_out)
    outgoing = [_shards_from_whole(name, ready[name]).astype(BF16) for name in _CARRIED] if distributed else []
    dproj0, dk, dv, d_sink_tab, arrived = attn_bwd(q, k, v, z, sink_tab, dog0, tables, batch, seq, scatter=outgoing)
    d_sinks = jnp.transpose(jnp.sum(d_sink_tab, axis=-1), (0, 2, 1)).reshape(1, N_HEADS)
    dproj0 = attn_bwd_kv(dproj0, dk, dv, tables)
    dx0, d_pre0 = in_proj_bwd_x(dproj0, attn_w_in, x0, pre0, dx1)
    d_attn_w_in, d_attn_b_in = in_proj_bwd_w(h0, dproj0)

    grads = dict(
        pre_norm_w=jnp.concatenate([d_pre0, d_pre1], axis=0), post_norm_w=jnp.concatenate([d_post0, d_post1], axis=0),
        attn_w_in=d_attn_w_in, attn_b_in=d_attn_b_in, attn_sinks=d_sinks, attn_w_out=d_attn_w_out,
        attn_b_out=d_attn_b_out, rec_w_in=d_rec_w_in, rec_lb_logits=d_lb, rec_gnorm_w=d_gnorm,
        rec_w_out=d_rec_w_out)
    return loss, dx0.reshape(batch, seq, D_MODEL), grads, dict(zip(_CARRIED, arrived))


_BIG = ("attn_w_in", "attn_w_out", "rec_w_in", "rec_w_out")
_COLUMN_SHARDED = ("attn_w_in", "rec_w_in")
_ORDER = ("pre_norm_w", "post_norm_w", "attn_w_in", "attn_b_in", "attn_sinks", "attn_w_out", "attn_b_out",
          "rec_w_in", "rec_lb_logits", "rec_gnorm_w", "rec_w_out")


def _whole_from_shards(name, stacked):
    if name in _COLUMN_SHARDED:
        return jnp.transpose(stacked, (1, 0, 2)).reshape(stacked.shape[1], -1)
    return stacked.reshape(-1, stacked.shape[2])


def _shards_from_whole(name, whole):
    if name in _COLUMN_SHARDED:
        return jnp.transpose(whole.reshape(whole.shape[0], N_CHIPS, -1), (1, 0, 2))
    return whole.reshape(N_CHIPS, -1, whole.shape[1])


def kernel(x, positions, pre_norm_w, post_norm_w, attn_w_in, attn_b_in, attn_sinks, attn_w_out, attn_b_out, rec_w_in, rec_lb_logits, rec_gnorm_w, rec_w_out, loss_target, m_pre_norm_w, m_post_norm_w, m_attn_w_in, m_attn_b_in, m_attn_sinks, m_attn_w_out, m_attn_b_out, m_rec_w_in, m_rec_lb_logits, m_rec_gnorm_w, m_rec_w_out, v_pre_norm_w, v_post_norm_w, v_attn_w_in, v_attn_b_in, v_attn_sinks, v_attn_w_out, v_attn_b_out, v_rec_w_in, v_rec_lb_logits, v_rec_gnorm_w, v_rec_w_out):
    w = dict(pre_norm_w=pre_norm_w, post_norm_w=post_norm_w, attn_w_in=attn_w_in, attn_b_in=attn_b_in,
             attn_sinks=attn_sinks, attn_w_out=attn_w_out, attn_b_out=attn_b_out, rec_w_in=rec_w_in,
             rec_lb_logits=rec_lb_logits, rec_gnorm_w=rec_gnorm_w, rec_w_out=rec_w_out)
    m = dict(pre_norm_w=m_pre_norm_w, post_norm_w=m_post_norm_w, attn_w_in=m_attn_w_in, attn_b_in=m_attn_b_in,
             attn_sinks=m_attn_sinks, attn_w_out=m_attn_w_out, attn_b_out=m_attn_b_out, rec_w_in=m_rec_w_in,
             rec_lb_logits=m_rec_lb_logits, rec_gnorm_w=m_rec_gnorm_w, rec_w_out=m_rec_w_out)
    v = dict(pre_norm_w=v_pre_norm_w, post_norm_w=v_post_norm_w, attn_w_in=v_attn_w_in, attn_b_in=v_attn_b_in,
             attn_sinks=v_attn_sinks, attn_w_out=v_attn_w_out, attn_b_out=v_attn_b_out, rec_w_in=v_rec_w_in,
             rec_lb_logits=v_rec_lb_logits, rec_gnorm_w=v_rec_gnorm_w, rec_w_out=v_rec_w_out)

    shards = {name: w[name][0] for name in _BIG}
    sent = {name: shards[name].astype(BF16) for name in _BIG}
    first = ("attn_w_in", "attn_w_out")
    whole = {name: _whole_from_shards(name, g) for name, g in zip(first, chip_gather([sent[name] for name in first]))}

    loss, grad_x, grads, parts = local_step(
        x, positions, pre_norm_w, post_norm_w, whole["attn_w_in"], attn_b_in, attn_sinks, whole["attn_w_out"],
        attn_b_out, sent["rec_w_in"], rec_lb_logits, rec_gnorm_w, sent["rec_w_out"], loss_target, distributed=True)
    loss = lax.psum(loss, ("x", "y", "c"))

    parts["attn_w_in"], = chip_scatter([_shards_from_whole("attn_w_in", grads["attn_w_in"]).astype(BF16)])
    plane_sums = [sum_slots(parts[name]) for name in _BIG]
    other_sums = sibling_exchange(plane_sums)
    out_g, out_d, out_m, out_v = {}, {}, {}, {}
    for name, mine, other in zip(_BIG, plane_sums, other_sums):
        g, d, nm, nv = adamw(shards[name], m[name][0], v[name][0], mine, other)
        out_g[name], out_d[name], out_m[name], out_v[name] = g[None], d[None], nm[None], nv[None]

    small_sum = sum_slots(all_gather_small(_pack_small(grads)))
    packed = adamw(_pack_small(w), _pack_small(m), _pack_small(v), small_sum)
    for dst, val in zip((out_g, out_d, out_m, out_v), packed):
        dst.update(_unpack_small(val))

    return (loss, grad_x, *[out_g[n] for n in _ORDER], *[out_d[n] for n in _ORDER],
            *[out_m[n] for n in _ORDER], *[out_v[n] for n in _ORDER])
```

```python
import functools

import jax
import jax.numpy as jnp
from jax import lax
from jax.experimental import pallas as pl
from jax.experimental.pallas import tpu as pltpu

F32 = jnp.float32
BF16 = jnp.bfloat16
MESH = pl.DeviceIdType.MESH

D_MODEL = 1024
HEAD_DIM = 64
N_HEADS = 16
N_KV_HEADS = 2
GROUP = N_HEADS // N_KV_HEADS
KV_WIDTH = N_KV_HEADS * HEAD_DIM
ATTN_IN = 2 * D_MODEL + 2 * KV_WIDTH
ATTN_BLOCK = 128
ROPE_THETA = 500000.0
ROPE_DIM = HEAD_DIM // 4
REC_HEADS = 8
REC_DIM = 128
REC_IN = 4 * D_MODEL
REC_BLOCK = 128
DIAG = 8
NORM_EPS = 1e-6
N_CHIPS = 4
N_DEV = 8
LANES = 128

ADAM_LR = 0.001
ADAM_B1 = 0.9
ADAM_B2 = 0.999
ADAM_EPS = 1e-08
ADAM_WD = 0.01
ADAM_STEP = 10

VMEM_LIMIT = 56 * 1024 * 1024


def _cparams(n_axes):
    return pltpu.CompilerParams(dimension_semantics=("arbitrary",) * n_axes, vmem_limit_bytes=VMEM_LIMIT)


def _dot(a, b, contract):
    return lax.dot_general(a.astype(BF16), b.astype(BF16), (contract, ((), ())), preferred_element_type=F32)


_NN = ((1,), (0,))
_NT = ((1,), (1,))
_TN = ((0,), (0,))


@jax.custom_vjp
def mm_nn(a, b):
    return _dot(a, b, _NN)


mm_nn.defvjp(lambda a, b: (_dot(a, b, _NN), (a, b)),
             lambda res, g: (_dot(g, res[1], _NT), _dot(res[0], g, _TN)))


@jax.custom_vjp
def mm_nt(a, b):
    return _dot(a, b, _NT)


mm_nt.defvjp(lambda a, b: (_dot(a, b, _NT), (a, b)),
             lambda res, g: (_dot(g, res[1], _NN), _dot(g, res[0], _TN)))


@jax.custom_vjp
def mm_tn(a, b):
    return _dot(a, b, _TN)


mm_tn.defvjp(lambda a, b: (_dot(a, b, _TN), (a, b)),
             lambda res, g: (_dot(res[1], g, _NT), _dot(res[0], g, _NN)))


def _tri_dot(x, lower):
    n = x.shape[0]
    r = lax.broadcasted_iota(jnp.int32, (n, n), 0)
    c = lax.broadcasted_iota(jnp.int32, (n, n), 1)
    tri = ((c <= r) if lower else (c >= r)).astype(BF16)
    hi = x.astype(BF16)
    rest = x - hi.astype(F32)
    mid = rest.astype(BF16)
    lo = (rest - mid.astype(F32)).astype(BF16)
    dot = lambda p: lax.dot_general(tri, p, (_NN, ((), ())), preferred_element_type=F32)
    return (dot(lo) + dot(mid)) + dot(hi)


@jax.custom_vjp
def cumsum_rows(x):
    return _tri_dot(x, True)


cumsum_rows.defvjp(lambda x: (cumsum_rows(x), None), lambda _, g: (_tri_dot(g, False),))


@functools.partial(jax.custom_vjp, nondiff_argnums=(1,))
def roll_sub(x, d):
    return pltpu.roll(x, d, 1) if d else x


roll_sub.defvjp(lambda x, d: (roll_sub(x, d), None),
                lambda d, _, g: (roll_sub(g, (DIAG - d) % DIAG),))


def sigmoid(x):
    return 1.0 / (1.0 + jnp.exp(-x))


@jax.custom_vjp
def silu(x):
    return x * sigmoid(x)


def _silu_fwd(x):
    s = sigmoid(x)
    return x * s, (x, s)


silu.defvjp(_silu_fwd, lambda res, g: (g * (res[1] * (1.0 + res[0] * (1.0 - res[1]))),))


def log_sigmoid_pair(x):
    t = jnp.log(1.0 + jnp.exp(-jnp.abs(x)))
    return jnp.minimum(x, 0.0) - t, jnp.minimum(-x, 0.0) - t


def _forget_fwd(x, a):
    log_lb, log_1m_lb = log_sigmoid_pair(a)
    ls_f, ls_nf = log_sigmoid_pair(x)
    c = log_1m_lb + ls_f
    lf = jnp.maximum(log_lb, c) + jnp.log(1.0 + jnp.exp(-jnp.abs(log_lb - c)))
    k = jnp.exp(log_1m_lb + ls_nf)
    return (lf, k), (log_lb, log_1m_lb, ls_f, ls_nf, c, lf, k)


def _forget_bwd(res, g):
    log_lb, log_1m_lb, ls_f, ls_nf, c, lf, k = res
    g_lf, g_k = g
    wc = jnp.exp(jnp.minimum(c - lf, 0.0))
    gk = g_k * k
    dx = g_lf * wc * jnp.exp(ls_nf) - gk * jnp.exp(ls_f)
    lb = jnp.exp(log_lb)
    da = jnp.sum(g_lf * ((1.0 - wc) * jnp.exp(log_1m_lb) - wc * lb) - gk * lb, axis=0, keepdims=True)
    return dx, da


@jax.custom_vjp
def forget_gate(x, a):
    return _forget_fwd(x, a)[0]


forget_gate.defvjp(_forget_fwd, _forget_bwd)


@jax.custom_vjp
def decayed(x, e):
    return (x * jnp.exp(e)).astype(BF16).astype(F32)


def _decayed_fwd(x, e):
    y = decayed(x, e)
    return y, (y, e)


decayed.defvjp(_decayed_fwd, lambda res, g: (g * jnp.exp(res[1]), g * res[0]))


def _row(x, r):
    shape = x.shape

    @jax.custom_vjp
    def take(x):
        return x[r:r + 1, :]

    take.defvjp(lambda x: (x[r:r + 1, :], None),
                lambda _, g: (jnp.where(lax.broadcasted_iota(jnp.int32, shape, 0) == r, g, 0.0),))
    return take(x)


def _rms(x):
    return lax.rsqrt(jnp.mean(x * x, axis=-1, keepdims=True) + NORM_EPS)


def _attn_group(qs, k_a, v_a, k_b, v_b, zs, sink_a, sink_b, bias):
    def half(kh, vh, sink):
        s = mm_nt(qs, kh) + bias
        m = lax.stop_gradient(jnp.maximum(jnp.max(s, axis=-1, keepdims=True), jnp.max(sink, axis=-1, keepdims=True)))
        p = jnp.exp(s - m)
        denom = jnp.sum(p, axis=-1, keepdims=True) + jnp.sum(jnp.exp(sink - m), axis=-1, keepdims=True) * (1.0 / LANES)
        return mm_nn(p * (1.0 / denom), vh)

    return (half(k_a, v_a, sink_a) + half(k_b, v_b, sink_b)) * silu(zs)


SAFE_RANGE = 80.0


def _rec_front(qr, fr, l0, l1):
    lf, k = forget_gate(fr, l1 - l0)
    return silu(qr), k, lf


def _rec_tail(o, z, gw):
    return o * _rms(o) * gw * silu(z)


def _rec_margin(b):
    R = b.shape[0]
    mid, last = _row(b, R // 2 - 1), _row(b, R - 1)
    return jnp.minimum(mid, last - mid)


def _heads(x):
    w = x.shape[1] // REC_HEADS
    return [x[:, h * w:(h + 1) * w] for h in range(REC_HEADS)]


def _hdot(a, b, contract):
    return jnp.concatenate([_dot(ah, bh, contract) for ah, bh in zip(_heads(a), _heads(b))], axis=1)


@jax.custom_vjp
def hmm_nn(a, b):
    return _hdot(a, b, _NN)


hmm_nn.defvjp(lambda a, b: (_hdot(a, b, _NN), (a, b)),
              lambda res, g: (_hdot(g, res[1], _NT), _hdot(res[0], g, _TN)))


@jax.custom_vjp
def hmm_nt(a, b):
    return _hdot(a, b, _NT)


hmm_nt.defvjp(lambda a, b: (_hdot(a, b, _NT), (a, b)),
              lambda res, g: (_hdot(g, res[1], _NN), _hdot(g, res[0], _TN)))


@jax.custom_vjp
def hmm_tn(a, b):
    return _hdot(a, b, _TN)


hmm_tn.defvjp(lambda a, b: (_hdot(a, b, _TN), (a, b)),
              lambda res, g: (_hdot(res[1], g, _NT), _hdot(res[0], g, _NN)))


def _head_sums(x):
    return jnp.concatenate([jnp.broadcast_to(jnp.sum(xh, axis=-1, keepdims=True), xh.shape) for xh in _heads(x)],
                           axis=1)


@jax.custom_vjp
def head_sum(x):
    return _head_sums(x)


head_sum.defvjp(lambda x: (_head_sums(x), None), lambda _, g: (_head_sums(g),))


def _rec_cores_fast(q, k, v, b, S):
    R = q.shape[0]
    ri = lax.broadcasted_iota(jnp.int32, (R, REC_HEADS * R), 0)
    ci = lax.broadcasted_iota(jnp.int32, (R, REC_HEADS * R), 1) % R
    d = b - _row(b, R // 2 - 1)
    sc = jnp.where(ci < ri, hmm_nt(decayed(q, d), decayed(k, -d)), 0.0)
    o = hmm_nt(q * jnp.exp(b), S) + hmm_nn(sc, v) + head_sum(q * k) * v
    b_last = _row(b, R - 1)
    return o, S * jnp.exp(b_last) + hmm_tn(v, k * jnp.exp(b_last - b))


def _rec_tails(o, z, gw):
    return o * lax.rsqrt(head_sum(o * o) * (1.0 / REC_DIM) + NORM_EPS) * gw * silu(z)


def _rec_block_fast(qr, fr, v, z, S, l0, l1, gw):
    lf, k = forget_gate(fr, l1 - l0)
    o, S_new = _rec_cores_fast(silu(qr), k, v, cumsum_rows(lf), S)
    return _rec_tails(o, z, gw), S_new


def _rec_core_slow(q, k, v, b, S):
    R = q.shape[0]
    rows = lax.broadcasted_iota(jnp.int32, (R, REC_DIM), 0)

    o = mm_nt(q * jnp.exp(jnp.minimum(b, 0.0)), S)

    ri = lax.broadcasted_iota(jnp.int32, (R, R), 0)
    ci = lax.broadcasted_iota(jnp.int32, (R, R), 1)
    sc = jnp.zeros((R, R), F32)
    w = R
    while w > DIAG:
        h = w // 2
        b3 = b.reshape(R // w, w, REC_DIM)
        rin = lax.broadcasted_iota(jnp.int32, (R // w, w, REC_DIM), 1)
        mid = jnp.sum(jnp.where(rin == h - 1, b3, 0.0), axis=1, keepdims=True)
        fac = jnp.exp(jnp.minimum(jnp.where(rin >= h, b3 - mid, mid - b3), 0.0)).reshape(R, REC_DIM)
        upper = (rows % w) >= h
        s_w = mm_nt(jnp.where(upper, q * fac, 0.0), jnp.where(upper, 0.0, k * fac))
        sc = sc + jnp.where((ri // w) == (ci // w), s_w, 0.0)
        w = h
    o = o + mm_nn(sc, v)

    g = R // DIAG
    q3, k3, v3, b3 = (t.reshape(g, DIAG, REC_DIM) for t in (q, k, v, b))
    rin = lax.broadcasted_iota(jnp.int32, (g, DIAG, 1), 1)
    od = jnp.zeros((g, DIAG, REC_DIM), F32)
    for d in range(DIAG):
        e = jnp.exp(jnp.minimum(b3 - roll_sub(b3, d), 0.0))
        sd = jnp.sum(q3 * roll_sub(k3, d) * e, axis=-1, keepdims=True)
        od = od + jnp.where(rin >= d, sd, 0.0) * roll_sub(v3, d)
    o = o + od.reshape(R, REC_DIM)

    b_last = _row(b, R - 1)
    return o, S * jnp.exp(jnp.minimum(b_last, 0.0)) + mm_tn(v, k * jnp.exp(jnp.minimum(b_last - b, 0.0)))


def _rec_head(core, qr, fr, v, z, S, l0, l1, gw):
    q, k, lf = _rec_front(qr, fr, l0, l1)
    o, S_new = core(q, k, v, cumsum_rows(lf), S)
    return _rec_tail(o, z, gw), S_new


def _rope_tables(positions):
    half = ROPE_DIM // 2
    inv_freq = ROPE_THETA ** (-(jnp.arange(half, dtype=F32) * 2.0 / ROPE_DIM))
    rest = jnp.zeros((HEAD_DIM - ROPE_DIM,), F32)
    ones, zeros = jnp.ones((half,), F32), jnp.zeros((half,), F32)
    per_lane = lambda first, second: jnp.tile(jnp.concatenate([first, second, rest]), LANES // HEAD_DIM)[None, :]
    ang = positions.astype(F32).reshape(-1, 1) * per_lane(inv_freq, inv_freq)
    sin = jnp.sin(ang)
    return jnp.cos(ang), sin * per_lane(zeros, ones), sin * per_lane(-ones, zeros)


def _rope(x, cos_t, sin_a, sin_b):
    half = ROPE_DIM // 2
    return x * cos_t + pltpu.roll(x, half, 1) * sin_a + pltpu.roll(x, LANES - half, 1) * sin_b


def _rope_transposed(g, cos_t, sin_a, sin_b):
    half = ROPE_DIM // 2
    return g * cos_t + pltpu.roll(g * sin_a, LANES - half, 1) + pltpu.roll(g * sin_b, half, 1)


def _row_spec(tm, width):
    return pl.BlockSpec((tm, width), lambda i: (i, 0))


def _full_spec(shape):
    return pl.BlockSpec(shape, lambda *_: (0,) * len(shape))


def attn_in_proj(x, w_pre, w_in, b_in, tables, tm=1024):
    n = x.shape[0]
    tm = min(tm, n)

    def body(x_ref, wp_ref, w_ref, b_ref, c_ref, sa_ref, sb_ref, h_ref, q_ref, k_ref, v_ref, z_ref):
        xv = x_ref[...]
        h = (xv * _rms(xv) * wp_ref[...]).astype(BF16)
        h_ref[...] = h
        proj = jnp.dot(h, w_ref[...], preferred_element_type=F32) + b_ref[...]
        tabs = (c_ref[...], sa_ref[...], sb_ref[...])
        for s in range(D_MODEL // LANES):
            sl = slice(s * LANES, (s + 1) * LANES)
            q_ref[:, sl] = _rope(proj[:, sl] * (HEAD_DIM ** -0.5), *tabs).astype(BF16)
        k_ref[...] = _rope(proj[:, D_MODEL:D_MODEL + KV_WIDTH], *tabs).astype(BF16)
        v_ref[...] = proj[:, D_MODEL + KV_WIDTH:D_MODEL + 2 * KV_WIDTH].astype(BF16)
        z_ref[...] = proj[:, D_MODEL + 2 * KV_WIDTH:]

    return pl.pallas_call(
        body, name="attn_in_proj", grid=(n // tm,),
        in_specs=[_row_spec(tm, D_MODEL), _full_spec((1, D_MODEL)), _full_spec((D_MODEL, ATTN_IN)),
                  _full_spec((1, ATTN_IN))] + [_row_spec(tm, LANES)] * 3,
        out_specs=[_row_spec(tm, D_MODEL), _row_spec(tm, D_MODEL), _row_spec(tm, KV_WIDTH),
                   _row_spec(tm, KV_WIDTH), _row_spec(tm, D_MODEL)],
        out_shape=[jax.ShapeDtypeStruct((n, D_MODEL), BF16), jax.ShapeDtypeStruct((n, D_MODEL), BF16),
                   jax.ShapeDtypeStruct((n, KV_WIDTH), BF16), jax.ShapeDtypeStruct((n, KV_WIDTH), BF16),
                   jax.ShapeDtypeStruct((n, D_MODEL), F32)],
        compiler_params=_cparams(1),
    )(x, w_pre, w_in, b_in, *tables)


def rec_in_proj(x, w_pre, w_in, tm=512):
    n = x.shape[0]
    tm = min(tm, n)

    def body(x_ref, wp_ref, w_ref, h_ref, p_ref):
        xv = x_ref[...]
        h = (xv * _rms(xv) * wp_ref[...]).astype(BF16)
        h_ref[...] = h
        p_ref[...] = jnp.dot(h, w_ref[...], preferred_element_type=F32)

    return pl.pallas_call(
        body, name="rec_in_proj", grid=(n // tm,),
        in_specs=[_row_spec(tm, D_MODEL), _full_spec((1, D_MODEL)), _full_spec((D_MODEL, REC_IN))],
        out_specs=[_row_spec(tm, D_MODEL), _row_spec(tm, REC_IN)],
        out_shape=[jax.ShapeDtypeStruct((n, D_MODEL), BF16), jax.ShapeDtypeStruct((n, REC_IN), F32)],
        compiler_params=_cparams(1),
    )(x, w_pre, w_in)


def out_proj(og, w_out, b_out, x_res, w_post, target=None, tm=1024):
    n = og.shape[0]
    tm = min(tm, n)
    with_loss = target is not None

    def body(*refs):
        if with_loss:
            og_ref, w_ref, b_ref, x_ref, wp_ref, t_ref, y_ref, dx_ref, l_ref = refs
        else:
            og_ref, w_ref, b_ref, x_ref, wp_ref, y_ref, xo_ref = refs
        y = jnp.dot(og_ref[...], w_ref[...], preferred_element_type=F32) + b_ref[...]
        y_ref[...] = y
        xo = x_ref[...] + y * _rms(y) * wp_ref[...]
        if with_loss:
            err = xo - t_ref[...]
            dx_ref[...] = err * (1.0 / D_MODEL)

            @pl.when(pl.program_id(0) == 0)
            def _():
                l_ref[...] = jnp.zeros_like(l_ref)

            l_ref[...] += jnp.sum(err * err, axis=0, keepdims=True)
        else:
            xo_ref[...] = xo

    in_specs = [_row_spec(tm, D_MODEL), _full_spec((D_MODEL, D_MODEL)), _full_spec((1, D_MODEL)),
                _row_spec(tm, D_MODEL), _full_spec((1, D_MODEL))]
    out_specs = [_row_spec(tm, D_MODEL), _row_spec(tm, D_MODEL)]
    out_shape = [jax.ShapeDtypeStruct((n, D_MODEL), F32), jax.ShapeDtypeStruct((n, D_MODEL), F32)]
    args = [og, w_out, b_out, x_res, w_post]
    if with_loss:
        in_specs.append(_row_spec(tm, D_MODEL))
        out_specs.append(_full_spec((1, D_MODEL)))
        out_shape.append(jax.ShapeDtypeStruct((1, D_MODEL), F32))
        args.append(target)
    return pl.pallas_call(
        body, name="out_proj_loss" if with_loss else "out_proj", grid=(n // tm,),
        in_specs=in_specs, out_specs=out_specs, out_shape=out_shape, compiler_params=_cparams(1),
    )(*args)


def out_proj_bwd(dxo, y, og, w_out, w_post, tm=1024):
    n = og.shape[0]
    tm = min(tm, n)

    def body(g_ref, y_ref, og_ref, w_ref, wp_ref, dog_ref, dw_ref, db_ref, dwp_ref):
        @pl.when(pl.program_id(0) == 0)
        def _():
            dw_ref[...] = jnp.zeros_like(dw_ref)
            db_ref[...] = jnp.zeros_like(db_ref)
            dwp_ref[...] = jnp.zeros_like(dwp_ref)

        g, y = g_ref[...], y_ref[...]
        rstd = _rms(y)
        yn = y * rstd
        gw = g * wp_ref[...]
        dwp_ref[...] += jnp.sum(g * yn, axis=0, keepdims=True)
        dy = rstd * (gw - yn * jnp.mean(gw * yn, axis=-1, keepdims=True))
        db_ref[...] += jnp.sum(dy, axis=0, keepdims=True)
        dyb = dy.astype(BF16)
        dog_ref[...] = _dot(dyb, w_ref[...], _NT)
        dw_ref[...] += _dot(og_ref[...], dyb, _TN)

    return pl.pallas_call(
        body, name="out_proj_bwd", grid=(n // tm,),
        in_specs=[_row_spec(tm, D_MODEL), _row_spec(tm, D_MODEL), _row_spec(tm, D_MODEL),
                  _full_spec((D_MODEL, D_MODEL)), _full_spec((1, D_MODEL))],
        out_specs=[_row_spec(tm, D_MODEL), _full_spec((D_MODEL, D_MODEL)), _full_spec((1, D_MODEL)),
                   _full_spec((1, D_MODEL))],
        out_shape=[jax.ShapeDtypeStruct((n, D_MODEL), F32), jax.ShapeDtypeStruct((D_MODEL, D_MODEL), F32),
                   jax.ShapeDtypeStruct((1, D_MODEL), F32), jax.ShapeDtypeStruct((1, D_MODEL), F32)],
        compiler_params=_cparams(1),
    )(dxo, y, og, w_out, w_post)


def in_proj_bwd_x(dproj, w_in, x, w_pre, dxo, tm=512):
    n, p = dproj.shape
    tm = min(tm if p > ATTN_IN else 2 * tm, n)

    def body(dp_ref, w_ref, x_ref, wp_ref, g_ref, dx_ref, dwp_ref):
        @pl.when(pl.program_id(0) == 0)
        def _():
            dwp_ref[...] = jnp.zeros_like(dwp_ref)

        dh = _dot(dp_ref[...], w_ref[...], _NT)
        xv = x_ref[...]
        rstd = _rms(xv)
        xn = xv * rstd
        gw = dh * wp_ref[...]
        dwp_ref[...] += jnp.sum(dh * xn, axis=0, keepdims=True)
        dx_ref[...] = rstd * (gw - xn * jnp.mean(gw * xn, axis=-1, keepdims=True)) + g_ref[...]

    return pl.pallas_call(
        body, name=f"in_proj_bwd_x_{p}", grid=(n // tm,),
        in_specs=[_row_spec(tm, p), _full_spec((D_MODEL, p)), _row_spec(tm, D_MODEL), _full_spec((1, D_MODEL)),
                  _row_spec(tm, D_MODEL)],
        out_specs=[_row_spec(tm, D_MODEL), _full_spec((1, D_MODEL))],
        out_shape=[jax.ShapeDtypeStruct((n, D_MODEL), F32), jax.ShapeDtypeStruct((1, D_MODEL), F32)],
        compiler_params=_cparams(1),
    )(dproj, w_in, x, w_pre, dxo)


def in_proj_bwd_w(h, dproj, tm=1024):
    n, p = dproj.shape
    chunk = p // (4 if p % 4096 == 0 else 3)
    tm = min(tm, n)
    steps = n // tm

    def body(h_ref, dp_ref, dw_ref, db_ref, acc_scr, sem):
        i = pl.program_id(0)

        @pl.when(i == 0)
        def _():
            acc_scr[...] = jnp.zeros_like(acc_scr)
            db_ref[...] = jnp.zeros_like(db_ref)

        ht = h_ref[...].T
        for c0 in range(0, p, chunk):
            dp = dp_ref[:, c0:c0 + chunk]
            acc_scr[:, c0:c0 + chunk] += jnp.dot(ht, dp, preferred_element_type=F32)
            db_ref[:, c0:c0 + chunk] += jnp.sum(dp.astype(F32), axis=0, keepdims=True)

        @pl.when(i == steps - 1)
        def _():
            out = pltpu.make_async_copy(acc_scr, dw_ref, sem)
            out.start()
            out.wait()

    return pl.pallas_call(
        body, name=f"in_proj_bwd_w_{p}", grid=(steps,),
        in_specs=[_row_spec(tm, D_MODEL), _row_spec(tm, p)],
        out_specs=[_ANY, _full_spec((1, p))],
        out_shape=[jax.ShapeDtypeStruct((D_MODEL, p), F32), jax.ShapeDtypeStruct((1, p), F32)],
        scratch_shapes=[pltpu.VMEM((D_MODEL, p), F32), pltpu.SemaphoreType.DMA],
        compiler_params=_cparams(1),
    )(h, dproj)


PAIRS = GROUP // 2
GROUP_ROWS = PAIRS * ATTN_BLOCK
MASKED = -1e30


def _kv_windows(k_ref, v_ref, i):
    ps = pl.multiple_of(jnp.maximum(i - 1, 0) * ATTN_BLOCK, ATTN_BLOCK)
    cs = pl.multiple_of(i * ATTN_BLOCK, ATTN_BLOCK)
    kw = jnp.concatenate([k_ref[pl.ds(ps, ATTN_BLOCK), :], k_ref[pl.ds(cs, ATTN_BLOCK), :]], axis=0)
    vw = jnp.concatenate([v_ref[pl.ds(ps, ATTN_BLOCK), :], v_ref[pl.ds(cs, ATTN_BLOCK), :]], axis=0)
    return kw.astype(F32), vw.astype(F32), ps, cs


def _low_lanes(shape):
    return lax.broadcasted_iota(jnp.int32, shape, 1) < HEAD_DIM


def _spread(w, kvh):
    low = _low_lanes(w.shape)
    swapped = pltpu.roll(w, HEAD_DIM, 1)
    if kvh == 0:
        return jnp.where(low, w, 0.0), jnp.where(low, 0.0, swapped)
    return jnp.where(low, swapped, 0.0), jnp.where(low, 0.0, w)


def _unspread(d_a, d_b, kvh):
    low = _low_lanes(d_a.shape)
    if kvh == 0:
        return jnp.where(low, d_a + pltpu.roll(d_b, HEAD_DIM, 1), 0.0)
    return jnp.where(low, 0.0, pltpu.roll(d_a, HEAD_DIM, 1) + d_b)


def _stack_pairs(ref, kvh):
    return jnp.concatenate([ref[:, (kvh * PAIRS + j) * LANES:(kvh * PAIRS + j + 1) * LANES] for j in range(PAIRS)],
                           axis=0)


def _fill_bias(bias_scr):
    shape = (GROUP_ROWS, 2 * ATTN_BLOCK)
    r = lax.broadcasted_iota(jnp.int32, shape, 0) % ATTN_BLOCK
    c = lax.broadcasted_iota(jnp.int32, shape, 1)
    in_cur = (c >= ATTN_BLOCK) & ((c - ATTN_BLOCK) <= r)
    in_prev = (c < ATTN_BLOCK) & (c > r)
    bias_scr[0] = jnp.where(in_cur, 0.0, MASKED)
    bias_scr[1] = jnp.where(in_cur | in_prev, 0.0, MASKED)


def _sink_table(sinks):
    t = jnp.transpose(sinks.reshape(N_KV_HEADS, PAIRS, 2), (0, 2, 1))
    return jnp.broadcast_to(t[:, :, :, None, None], (N_KV_HEADS, 2, PAIRS, ATTN_BLOCK, LANES)).reshape(
        N_KV_HEADS, 2, GROUP_ROWS, LANES)


def attn_fwd(q, k, v, z, sink_tab, batch, seq, gather=()):
    nb = seq // ATTN_BLOCK
    ng = len(gather)

    def body(*refs):
        q_ref, k_ref, v_ref, z_ref, s_ref = refs[:5]
        og_ref, bias_scr = refs[5 + ng], refs[6 + 2 * ng]
        exchange = (refs[5:5 + ng], refs[6 + ng:6 + 2 * ng]) + tuple(refs[7 + 2 * ng:])
        b, i = pl.program_id(0), pl.program_id(1)

        @pl.when((b == 0) & (i == 0))
        def _():
            _fill_bias(bias_scr)
            if ng:
                _gather_start(*exchange)

        kw, vw, _, _ = _kv_windows(k_ref, v_ref, i)
        bias = bias_scr[jnp.minimum(i, 1)]
        for kvh in range(N_KV_HEADS):
            k_a, k_b = _spread(kw, kvh)
            v_a, v_b = _spread(vw, kvh)
            og = _attn_group(_stack_pairs(q_ref, kvh), k_a, v_a, k_b, v_b, _stack_pairs(z_ref, kvh),
                             s_ref[kvh, 0], s_ref[kvh, 1], bias)
            for j in range(PAIRS):
                og_ref[:, (kvh * PAIRS + j) * LANES:(kvh * PAIRS + j + 1) * LANES] = (
                    og[j * ATTN_BLOCK:(j + 1) * ATTN_BLOCK].astype(BF16))

        if ng:
            @pl.when((b == batch - 1) & (i == nb - 1))
            def _():
                _gather_finish(*exchange)

    blk = lambda w: pl.BlockSpec((ATTN_BLOCK, w), lambda b, i: (b * nb + i, 0))
    seq_spec = pl.BlockSpec((seq, KV_WIDTH), lambda b, i: (b, 0))
    out = pl.pallas_call(
        body, name="attn_fwd", grid=(batch, nb),
        in_specs=[blk(D_MODEL), seq_spec, seq_spec, blk(D_MODEL), _full_spec(sink_tab.shape)] + [_ANY] * ng,
        out_specs=[blk(D_MODEL)] + [_ANY] * ng,
        out_shape=[jax.ShapeDtypeStruct((batch * seq, D_MODEL), BF16)]
        + [jax.ShapeDtypeStruct((N_CHIPS,) + a.shape, a.dtype) for a in gather],
        scratch_shapes=[pltpu.VMEM((2, GROUP_ROWS, 2 * ATTN_BLOCK), F32)] + (_gather_sems(ng) if ng else []),
        compiler_params=_cparams(2),
    )(q, k, v, z, sink_tab, *gather)
    return out[0], out[1:]


def attn_bwd(q, k, v, z, sink_tab, dog, tables, batch, seq, scatter=()):
    nb = seq // ATTN_BLOCK
    ns = len(scatter)

    def body(*refs):
        q_ref, k_ref, v_ref, z_ref, s_ref, g_ref, c_ref, sa_ref, sb_ref = refs[:9]
        dp_ref, dk_ref, dv_ref, ds_ref = refs[9 + ns:13 + ns]
        bias_scr = refs[13 + 2 * ns]
        exchange = (refs[9:9 + ns], refs[13 + ns:13 + 2 * ns]) + tuple(refs[14 + 2 * ns:])
        b, i = pl.program_id(0), pl.program_id(1)

        @pl.when((b == 0) & (i == 0))
        def _():
            _fill_bias(bias_scr)
            ds_ref[...] = jnp.zeros_like(ds_ref)
            if ns:
                _scatter_start(*exchange)

        @pl.when(i == 0)
        def _():
            dk_ref[...] = jnp.zeros_like(dk_ref)
            dv_ref[...] = jnp.zeros_like(dv_ref)

        kw, vw, ps, cs = _kv_windows(k_ref, v_ref, i)
        bias = bias_scr[jnp.minimum(i, 1)]
        tabs = (c_ref[...], sa_ref[...], sb_ref[...])
        dkw = jnp.zeros_like(kw)
        dvw = jnp.zeros_like(vw)
        for kvh in range(N_KV_HEADS):
            k_a, k_b = _spread(kw, kvh)
            v_a, v_b = _spread(vw, kvh)
            _, vjp = jax.vjp(functools.partial(_attn_group, bias=bias), _stack_pairs(q_ref, kvh).astype(F32),
                             k_a, v_a, k_b, v_b, _stack_pairs(z_ref, kvh), s_ref[kvh, 0], s_ref[kvh, 1])
            dqs, dk_a, dv_a, dk_b, dv_b, dzs, ds_a, ds_b = vjp(_stack_pairs(g_ref, kvh))
            dkw = dkw + _unspread(dk_a, dk_b, kvh)
            dvw = dvw + _unspread(dv_a, dv_b, kvh)
            ds_ref[kvh, 0] += jnp.sum(ds_a.reshape(PAIRS, ATTN_BLOCK, LANES), axis=1)
            ds_ref[kvh, 1] += jnp.sum(ds_b.reshape(PAIRS, ATTN_BLOCK, LANES), axis=1)
            for j in range(PAIRS):
                rows = slice(j * ATTN_BLOCK, (j + 1) * ATTN_BLOCK)
                col = (kvh * PAIRS + j) * LANES
                dp_ref[:, col:col + LANES] = _rope_transposed(dqs[rows] * (HEAD_DIM ** -0.5), *tabs).astype(BF16)
                zc = D_MODEL + 2 * KV_WIDTH + col
                dp_ref[:, zc:zc + LANES] = dzs[rows].astype(BF16)
        dp_ref[:, D_MODEL:D_MODEL + 2 * KV_WIDTH] = jnp.zeros((ATTN_BLOCK, 2 * KV_WIDTH), BF16)
        dk_ref[pl.ds(ps, ATTN_BLOCK), :] += dkw[:ATTN_BLOCK]
        dk_ref[pl.ds(cs, ATTN_BLOCK), :] += dkw[ATTN_BLOCK:]
        dv_ref[pl.ds(ps, ATTN_BLOCK), :] += dvw[:ATTN_BLOCK]
        dv_ref[pl.ds(cs, ATTN_BLOCK), :] += dvw[ATTN_BLOCK:]

        if ns:
            @pl.when((b == batch - 1) & (i == nb - 1))
            def _():
                _scatter_finish(*exchange)

    blk = lambda w: pl.BlockSpec((ATTN_BLOCK, w), lambda b, i: (b * nb + i, 0))
    seq_spec = pl.BlockSpec((seq, KV_WIDTH), lambda b, i: (b, 0))
    n = batch * seq
    ds_shape = (N_KV_HEADS, 2, PAIRS, LANES)
    out = pl.pallas_call(
        body, name="attn_bwd", grid=(batch, nb),
        in_specs=[blk(D_MODEL), seq_spec, seq_spec, blk(D_MODEL), _full_spec(sink_tab.shape), blk(D_MODEL)]
        + [blk(LANES)] * 3 + [_ANY] * ns,
        out_specs=[blk(ATTN_IN), seq_spec, seq_spec, _full_spec(ds_shape)] + [_ANY] * ns,
        out_shape=[jax.ShapeDtypeStruct((n, ATTN_IN), BF16), jax.ShapeDtypeStruct((n, KV_WIDTH), F32),
                   jax.ShapeDtypeStruct((n, KV_WIDTH), F32), jax.ShapeDtypeStruct(ds_shape, F32)]
        + [jax.ShapeDtypeStruct(a.shape, a.dtype) for a in scatter],
        scratch_shapes=[pltpu.VMEM((2, GROUP_ROWS, 2 * ATTN_BLOCK), F32)] + (_scatter_sems(ns) if ns else []),
        compiler_params=_cparams(2),
    )(q, k, v, z, sink_tab, dog, *tables, *scatter)
    return out[0], out[1], out[2], out[3], out[4:]


def attn_bwd_kv(dproj, dk, dv, tables, tm=512):
    n = dproj.shape[0]

    def body(dp_in_ref, dk_ref, dv_ref, c_ref, sa_ref, sb_ref, dp_ref):
        del dp_in_ref
        dp_ref[:, :KV_WIDTH] = _rope_transposed(dk_ref[...], c_ref[...], sa_ref[...], sb_ref[...]).astype(BF16)
        dp_ref[:, KV_WIDTH:] = dv_ref[...].astype(BF16)

    kv_cols = pl.BlockSpec((tm, 2 * KV_WIDTH), lambda i: (i, D_MODEL // (2 * KV_WIDTH)))
    return pl.pallas_call(
        body, name="attn_bwd_kv", grid=(n // tm,),
        in_specs=[kv_cols, _row_spec(tm, KV_WIDTH), _row_spec(tm, KV_WIDTH)] + [_row_spec(tm, LANES)] * 3,
        out_specs=kv_cols, out_shape=jax.ShapeDtypeStruct(dproj.shape, BF16),
        input_output_aliases={0: 0}, compiler_params=_cparams(1),
    )(dproj, dk, dv, *tables)


def _rec_cols(part, h):
    return slice(part * D_MODEL + h * REC_DIM, part * D_MODEL + (h + 1) * REC_DIM)


def _rec_args(p_ref, lb_ref, gw_ref, h, S):
    hs = slice(h * REC_DIM, (h + 1) * REC_DIM)
    return (p_ref[:, _rec_cols(0, h)], p_ref[:, _rec_cols(1, h)], p_ref[:, _rec_cols(2, h)],
            p_ref[:, _rec_cols(3, h)], S, lb_ref[0:1, hs], lb_ref[1:2, hs], gw_ref[...])


def rec_fwd(proj, lb_logits, gnorm_w, batch, seq):
    nblk = seq // REC_BLOCK

    def body(p_ref, lb_ref, gw_ref, og_ref, st_ref, safe_ref, s_scr):
        @pl.when(pl.program_id(1) == 0)
        def _():
            s_scr[...] = jnp.zeros_like(s_scr)

        S = s_scr[...]
        st_ref[0] = S
        qr, fr, v, z = (p_ref[:, part * D_MODEL:(part + 1) * D_MODEL] for part in range(4))
        lf, k = forget_gate(fr, lb_ref[1:2, :] - lb_ref[0:1, :])
        q, b = silu(qr), cumsum_rows(lf)
        safe = jnp.min(_rec_margin(b)) >= -SAFE_RANGE

        def head_by_head():
            outs = [_rec_core_slow(*args) for args in zip(*(_heads(t) for t in (q, k, v, b, S)))]
            return tuple(jnp.concatenate(parts, axis=1) for parts in zip(*outs))

        o, S_new = lax.cond(safe, lambda: _rec_cores_fast(q, k, v, b, S), head_by_head)
        og_ref[...] = _rec_tails(o, z, gw_ref[...]).astype(BF16)
        s_scr[...] = S_new
        safe_ref[0] = jnp.full((REC_HEADS, LANES), safe.astype(F32))

    blk = lambda w: pl.BlockSpec((REC_BLOCK, w), lambda b, j: (b * nblk + j, 0))
    st_spec = pl.BlockSpec((1, REC_DIM, D_MODEL), lambda b, j: (b * nblk + j, 0, 0))
    safe_spec = pl.BlockSpec((1, REC_HEADS, LANES), lambda b, j: (b * nblk + j, 0, 0))
    return pl.pallas_call(
        body, name="rec_fwd", grid=(batch, nblk),
        in_specs=[blk(REC_IN), _full_spec((2, D_MODEL)), _full_spec((1, D_MODEL))],
        out_specs=[blk(D_MODEL), st_spec, safe_spec],
        out_shape=[jax.ShapeDtypeStruct((batch * seq, D_MODEL), BF16),
                   jax.ShapeDtypeStruct((batch * nblk, REC_DIM, D_MODEL), F32),
                   jax.ShapeDtypeStruct((batch * nblk, REC_HEADS, LANES), F32)],
        scratch_shapes=[pltpu.VMEM((REC_DIM, D_MODEL), F32)],
        compiler_params=_cparams(2),
    )(proj, lb_logits, jnp.tile(gnorm_w, (1, REC_HEADS)))


def rec_bwd(proj, states, safe, lb_logits, gnorm_w, dog, batch, seq):
    nblk = seq // REC_BLOCK

    def body(p_ref, st_ref, safe_ref, lb_ref, gw_ref, g_ref, dp_ref, dlb_ref, dgw_ref, ds_scr):
        @pl.when((pl.program_id(0) == 0) & (pl.program_id(1) == 0))
        def _():
            dlb_ref[...] = jnp.zeros_like(dlb_ref)
            dgw_ref[...] = jnp.zeros_like(dgw_ref)

        @pl.when(pl.program_id(1) == 0)
        def _():
            ds_scr[...] = jnp.zeros_like(ds_scr)

        primals = tuple(p_ref[:, part * D_MODEL:(part + 1) * D_MODEL] for part in range(4)) + (
            st_ref[0], lb_ref[0:1, :], lb_ref[1:2, :], gw_ref[...])
        cotangents = (g_ref[...], ds_scr[...])

        def all_heads():
            return jax.vjp(_rec_block_fast, *primals)[1](cotangents)

        def head_by_head():
            outs = [jax.vjp(functools.partial(_rec_head, _rec_core_slow), *args)[1](cts)
                    for args, cts in zip(zip(*(_heads(t) for t in primals)), zip(*(_heads(t) for t in cotangents)))]
            return tuple(jnp.concatenate(parts, axis=1) for parts in zip(*outs))

        dqr, dfr, dv, dz, dS, dl0, dl1, dgw = lax.cond(jnp.max(safe_ref[0]) > 0.5, all_heads, head_by_head)
        for part, val in enumerate((dqr, dfr, dv, dz)):
            dp_ref[:, part * D_MODEL:(part + 1) * D_MODEL] = val.astype(BF16)
        ds_scr[...] = dS
        dlb_ref[0:1, :] += dl0
        dlb_ref[1:2, :] += dl1
        dgw_ref[...] += functools.reduce(jnp.add, _heads(dgw))

    blk = lambda w: pl.BlockSpec((REC_BLOCK, w), lambda b, j: (b * nblk + nblk - 1 - j, 0))
    st_spec = pl.BlockSpec((1, REC_DIM, D_MODEL), lambda b, j: (b * nblk + nblk - 1 - j, 0, 0))
    safe_spec = pl.BlockSpec((1, REC_HEADS, LANES), lambda b, j: (b * nblk + nblk - 1 - j, 0, 0))
    return pl.pallas_call(
        body, name="rec_bwd", grid=(batch, nblk),
        in_specs=[blk(REC_IN), st_spec, safe_spec, _full_spec((2, D_MODEL)), _full_spec((1, D_MODEL)),
                  blk(D_MODEL)],
        out_specs=[blk(REC_IN), _full_spec((2, D_MODEL)), _full_spec((1, REC_DIM))],
        out_shape=[jax.ShapeDtypeStruct((batch * seq, REC_IN), BF16), jax.ShapeDtypeStruct((2, D_MODEL), F32),
                   jax.ShapeDtypeStruct((1, REC_DIM), F32)],
        scratch_shapes=[pltpu.VMEM((REC_DIM, D_MODEL), F32)],
        compiler_params=_cparams(2),
    )(proj, states, safe, lb_logits, jnp.tile(gnorm_w, (1, REC_HEADS)), dog)


_ANY = pl.BlockSpec(memory_space=pl.ANY)


def _chip_peers():
    x, y, c = lax.axis_index("x"), lax.axis_index("y"), lax.axis_index("c")
    peers = []
    for fx, fy in ((1, 0), (0, 1), (1, 1)):
        px, py = (1 - x if fx else x), (1 - y if fy else y)
        peers.append(((px, py, c), 2 * px + py))
    return 2 * x + y, peers


def _remote(src, dst, send_sem, recv_sem, device):
    return pltpu.make_async_remote_copy(src_ref=src, dst_ref=dst, send_sem=send_sem, recv_sem=recv_sem,
                                        device_id=device, device_id_type=MESH)


N_FLIPS = N_CHIPS - 1


def _scatter_sems(n):
    return [pltpu.SemaphoreType.DMA((n * N_FLIPS,)), pltpu.SemaphoreType.DMA((n * N_FLIPS,)),
            pltpu.SemaphoreType.DMA((n,))]


def _scatter_copies(ins, outs, send_sems, recv_sems, local_sems, starting):
    me, peers = _chip_peers()
    local = [pltpu.make_async_copy(ins[k].at[me], outs[k].at[me], local_sems.at[k]) for k in range(len(ins))]
    sends, arrivals = [], []
    for k in range(len(ins)):
        for j, (device, idx) in enumerate(peers):
            sems = (send_sems.at[k * N_FLIPS + j], recv_sems.at[k * N_FLIPS + j], device)
            sends.append(_remote(ins[k].at[idx], outs[k].at[me], *sems))
            if not starting:
                arrivals.append(_remote(ins[k].at[me], outs[k].at[idx], *sems))
    return local, sends, arrivals


def _scatter_start(*refs):
    local, sends, _ = _scatter_copies(*refs, starting=True)
    for cp in local + sends:
        cp.start()


def _scatter_finish(*refs):
    local, sends, arrivals = _scatter_copies(*refs, starting=False)
    for cp in arrivals:
        cp.wait_recv()
    for cp in sends:
        cp.wait_send()
    for cp in local:
        cp.wait()


def chip_scatter(arrays):
    n = len(arrays)

    def body(*refs):
        _scatter_start(refs[:n], refs[n:2 * n], *refs[2 * n:])
        _scatter_finish(refs[:n], refs[n:2 * n], *refs[2 * n:])

    return pl.pallas_call(
        body, name="chip_scatter", in_specs=[_ANY] * n, out_specs=[_ANY] * n,
        out_shape=[jax.ShapeDtypeStruct(a.shape, a.dtype) for a in arrays], scratch_shapes=_scatter_sems(n),
    )(*arrays)


def _gather_sems(n):
    return [pltpu.SemaphoreType.DMA((n * N_FLIPS,)) for _ in range(4)] + [pltpu.SemaphoreType.DMA((n,))]


def _gather_copies(ins, outs, send_sems, recv_sems, pass_send_sems, pass_recv_sems, local_sems, starting):
    me, peers = _chip_peers()
    c = lax.axis_index("c")
    sibling = (lax.axis_index("x"), lax.axis_index("y"), 1 - c)
    local = [pltpu.make_async_copy(ins[k], outs[k].at[me], local_sems.at[k]) for k in range(len(ins))]
    sends, arrivals, passes, pass_arrivals = [], [], [], []
    for k in range(len(ins)):
        half = ins[k].shape[0] // 2
        mine, other = pl.ds(c * half, half), pl.ds((1 - c) * half, half)
        for j, (device, idx) in enumerate(peers):
            s = k * N_FLIPS + j
            sends.append(_remote(ins[k].at[mine], outs[k].at[me].at[mine], send_sems.at[s], recv_sems.at[s], device))
            if starting:
                continue
            arrived = outs[k].at[idx].at[mine]
            arrivals.append(_remote(ins[k].at[mine], arrived, send_sems.at[s], recv_sems.at[s], device))
            passes.append(_remote(arrived, arrived, pass_send_sems.at[s], pass_recv_sems.at[s], sibling))
            passed = outs[k].at[idx].at[other]
            pass_arrivals.append(_remote(passed, passed, pass_send_sems.at[s], pass_recv_sems.at[s], sibling))
    return local, sends, arrivals, passes, pass_arrivals


def _gather_start(*refs):
    local, sends, _, _, _ = _gather_copies(*refs, starting=True)
    for cp in local + sends:
        cp.start()


def _gather_finish(*refs):
    local, sends, arrivals, passes, pass_arrivals = _gather_copies(*refs, starting=False)
    for arrival, onward in zip(arrivals, passes):
        arrival.wait_recv()
        onward.start()
    for cp in pass_arrivals:
        cp.wait_recv()
    for cp in sends + passes:
        cp.wait_send()
    for cp in local:
        cp.wait()


def chip_gather(arrays):
    n = len(arrays)

    def body(*refs):
        _gather_start(refs[:n], refs[n:2 * n], *refs[2 * n:])
        _gather_finish(refs[:n], refs[n:2 * n], *refs[2 * n:])

    return pl.pallas_call(
        body, name="chip_gather", in_specs=[_ANY] * n, out_specs=[_ANY] * n,
        out_shape=[jax.ShapeDtypeStruct((N_CHIPS,) + a.shape, a.dtype) for a in arrays],
        scratch_shapes=_gather_sems(n),
    )(*arrays)


def sibling_exchange(arrays):
    n = len(arrays)

    def body(*refs):
        ins, outs = refs[:n], refs[n:2 * n]
        send_sems, recv_sems = refs[2 * n:]
        sibling = (lax.axis_index("x"), lax.axis_index("y"), 1 - lax.axis_index("c"))
        copies = [pltpu.make_async_remote_copy(src_ref=ins[k], dst_ref=outs[k], send_sem=send_sems.at[k],
                                               recv_sem=recv_sems.at[k], device_id=sibling, device_id_type=MESH)
                  for k in range(n)]
        for cp in copies:
            cp.start()
        for cp in copies:
            cp.wait()

    return pl.pallas_call(
        body, name="sibling_exchange", in_specs=[_ANY] * n, out_specs=[_ANY] * n,
        out_shape=[jax.ShapeDtypeStruct(a.shape, a.dtype) for a in arrays],
        scratch_shapes=[pltpu.SemaphoreType.DMA((n,)), pltpu.SemaphoreType.DMA((n,))],
    )(*arrays)


def all_gather_small(vec):
    def body(v_ref, out_ref, send_sems, recv_sems, local_sem):
        x, y, c = lax.axis_index("x"), lax.axis_index("y"), lax.axis_index("c")
        me = 4 * x + 2 * y + c
        local = pltpu.make_async_copy(v_ref, out_ref.at[me], local_sem)
        local.start()
        sends, recvs = [], []
        for j in range(1, N_DEV):
            px = jnp.where(j & 4, 1 - x, x)
            py = jnp.where(j & 2, 1 - y, y)
            pc = jnp.where(j & 1, 1 - c, c)
            common = dict(send_sem=send_sems.at[j - 1], recv_sem=recv_sems.at[j - 1], device_id=(px, py, pc),
                          device_id_type=MESH)
            sends.append(pltpu.make_async_remote_copy(src_ref=v_ref, dst_ref=out_ref.at[me], **common))
            recvs.append(pltpu.make_async_remote_copy(src_ref=v_ref, dst_ref=out_ref.at[4 * px + 2 * py + pc],
                                                      **common))
        for cp in sends:
            cp.start()
        for cp in recvs:
            cp.wait_recv()
        for cp in sends:
            cp.wait_send()
        local.wait()

    return pl.pallas_call(
        body, name="all_gather_small", in_specs=[_ANY], out_specs=_ANY,
        out_shape=jax.ShapeDtypeStruct((N_DEV,) + vec.shape, vec.dtype),
        scratch_shapes=[pltpu.SemaphoreType.DMA((N_DEV - 1,)), pltpu.SemaphoreType.DMA((N_DEV - 1,)),
                        pltpu.SemaphoreType.DMA],
    )(vec)


def sum_slots(stacked, tm=256):
    s, r, c = stacked.shape
    tm = min(tm, r)

    def body(in_ref, out_ref):
        acc = in_ref[0].astype(F32)
        for t in range(1, s):
            acc = acc + in_ref[t].astype(F32)
        out_ref[...] = acc

    return pl.pallas_call(
        body, name=f"sum_slots_{s}_{r}_{c}", grid=(r // tm,),
        in_specs=[pl.BlockSpec((s, tm, c), lambda i: (0, i, 0))], out_specs=_row_spec(tm, c),
        out_shape=jax.ShapeDtypeStruct((r, c), F32), compiler_params=_cparams(1),
    )(stacked)


def adamw(w, m, v, g_a, g_b=None, tm=256):
    r, c = w.shape
    tm = min(tm, r)
    two = g_b is not None

    def body(*refs):
        w_ref, m_ref, v_ref, ga_ref = refs[:4]
        g_ref, d_ref, nm_ref, nv_ref = refs[-4:]
        g = ga_ref[...] + refs[4][...] if two else ga_ref[...]
        nm = ADAM_B1 * m_ref[...] + (1.0 - ADAM_B1) * g
        nv = ADAM_B2 * v_ref[...] + (1.0 - ADAM_B2) * (g * g)
        m_hat = nm / (1.0 - ADAM_B1 ** ADAM_STEP)
        v_hat = nv / (1.0 - ADAM_B2 ** ADAM_STEP)
        g_ref[...] = g
        d_ref[...] = -ADAM_LR * (m_hat / (jnp.sqrt(v_hat) + ADAM_EPS) + ADAM_WD * w_ref[...])
        nm_ref[...] = nm
        nv_ref[...] = nv

    args = [w, m, v, g_a] + ([g_b] if two else [])
    return pl.pallas_call(
        body, name=f"adamw_{r}_{c}", grid=(r // tm,),
        in_specs=[_row_spec(tm, c)] * len(args), out_specs=[_row_spec(tm, c)] * 4,
        out_shape=[jax.ShapeDtypeStruct((r, c), F32)] * 4, compiler_params=_cparams(1),
    )(*args)


_SMALL = (("pre_norm_w", (2, D_MODEL)), ("post_norm_w", (2, D_MODEL)), ("attn_b_in", (1, ATTN_IN)),
          ("attn_sinks", (1, N_HEADS)), ("attn_b_out", (1, D_MODEL)), ("rec_lb_logits", (2, D_MODEL)),
          ("rec_gnorm_w", (1, REC_DIM)))
_SMALL_ROWS = 16


def _pack_small(parts, last_row=None):
    rows = []
    for (name, shape) in _SMALL:
        flat = parts[name].reshape(-1)
        pad = -flat.shape[0] % D_MODEL
        rows.append(jnp.pad(flat, (0, pad)).reshape(-1, D_MODEL))
    used = sum(r.shape[0] for r in rows)
    rows.append(jnp.zeros((_SMALL_ROWS - 1 - used, D_MODEL), F32))
    rows.append(jnp.zeros((1, D_MODEL), F32) if last_row is None else last_row)
    return jnp.concatenate(rows, axis=0)


def _unpack_small(packed):
    out, row = {}, 0
    for (name, shape) in _SMALL:
        size = shape[0] * shape[1]
        nrows = -(-size // D_MODEL)
        out[name] = packed[row:row + nrows].reshape(-1)[:size].reshape(shape)
        row += nrows
    return out


_CARRIED = ("rec_w_in", "rec_w_out", "attn_w_out")


_LATE = ("attn_w_out", "rec_w_in", "rec_w_out")


def local_step(x, positions, pre_norm_w, post_norm_w, attn_w_in, attn_b_in, attn_sinks, attn_w_out, attn_b_out,
               rec_w_in, rec_lb_logits, rec_gnorm_w, rec_w_out, loss_target, distributed=False):
    batch, seq, _ = x.shape
    n = batch * seq
    x0 = x.reshape(n, D_MODEL)
    tables = _rope_tables(positions)
    pre0, pre1 = pre_norm_w[0:1], pre_norm_w[1:2]
    post0, post1 = post_norm_w[0:1], post_norm_w[1:2]
    no_bias = jnp.zeros((1, D_MODEL), F32)

    h0, q, k, v, z = attn_in_proj(x0, pre0, attn_w_in, attn_b_in, tables)
    sink_tab = _sink_table(attn_sinks)
    late = (attn_w_out, rec_w_in, rec_w_out)
    og0, gathered = attn_fwd(q, k, v, z, sink_tab, batch, seq, gather=late if distributed else ())
    if distributed:
        attn_w_out, rec_w_in, rec_w_out = (_whole_from_shards(name, g) for name, g in zip(_LATE, gathered))
    y0, x1 = out_proj(og0, attn_w_out, attn_b_out, x0, post0)

    h1, proj1 = rec_in_proj(x1, pre1, rec_w_in)
    og1, states, safe = rec_fwd(proj1, rec_lb_logits, rec_gnorm_w, batch, seq)
    y1, dx2, loss_vec = out_proj(og1, rec_w_out, no_bias, x1, post1, target=loss_target.reshape(n, D_MODEL))

    dog1, d_rec_w_out, _, d_post1 = out_proj_bwd(dx2, y1, og1, rec_w_out, post1)
    dproj1, d_lb, d_gnorm = rec_bwd(proj1, states, safe, rec_lb_logits, rec_gnorm_w, dog1, batch, seq)
    dx1, d_pre1 = in_proj_bwd_x(dproj1, rec_w_in, x1, pre1, dx2)
    d_rec_w_in, _ = in_proj_bwd_w(h1, dproj1)

    dog0, d_attn_w_out, d_attn_b_out, d_post0 = out_proj_bwd(dx1, y0, og0, attn_w_out, post0)
    ready = dict(rec_w_in=d_rec_w_in, rec_w_out=d_rec_w_out, attn_w_out=d_attn_w_out)
    outgoing = [_shards_from_whole(name, ready[name]).astype(BF16) for name in _CARRIED] if distributed else []
    dproj0, dk, dv, d_sink_tab, arrived = attn_bwd(q, k, v, z, sink_tab, dog0, tables, batch, seq, scatter=outgoing)
    d_sinks = jnp.transpose(jnp.sum(d_sink_tab, axis=-1), (0, 2, 1)).reshape(1, N_HEADS)
    dproj0 = attn_bwd_kv(dproj0, dk, dv, tables)
    dx0, d_pre0 = in_proj_bwd_x(dproj0, attn_w_in, x0, pre0, dx1)
    d_attn_w_in, d_attn_b_in = in_proj_bwd_w(h0, dproj0)

    grads = dict(
        pre_norm_w=jnp.concatenate([d_pre0, d_pre1], axis=0), post_norm_w=jnp.concatenate([d_post0, d_post1], axis=0),
        attn_w_in=d_attn_w_in, attn_b_in=d_attn_b_in, attn_sinks=d_sinks, attn_w_out=d_attn_w_out,
        attn_b_out=d_attn_b_out, rec_w_in=d_rec_w_in, rec_lb_logits=d_lb, rec_gnorm_w=d_gnorm,
        rec_w_out=d_rec_w_out)
    return loss_vec, dx0.reshape(batch, seq, D_MODEL), grads, dict(zip(_CARRIED, arrived))


_BIG = ("attn_w_in", "attn_w_out", "rec_w_in", "rec_w_out")
_COLUMN_SHARDED = ("attn_w_in", "rec_w_in")
_ORDER = ("pre_norm_w", "post_norm_w", "attn_w_in", "attn_b_in", "attn_sinks", "attn_w_out", "attn_b_out",
          "rec_w_in", "rec_lb_logits", "rec_gnorm_w", "rec_w_out")


def _whole_from_shards(name, stacked):
    if name in _COLUMN_SHARDED:
        return jnp.transpose(stacked, (1, 0, 2)).reshape(stacked.shape[1], -1)
    return stacked.reshape(-1, stacked.shape[2])


def _shards_from_whole(name, whole):
    if name in _COLUMN_SHARDED:
        return jnp.transpose(whole.reshape(whole.shape[0], N_CHIPS, -1), (1, 0, 2))
    return whole.reshape(N_CHIPS, -1, whole.shape[1])


def kernel(x, positions, pre_norm_w, post_norm_w, attn_w_in, attn_b_in, attn_sinks, attn_w_out, attn_b_out, rec_w_in, rec_lb_logits, rec_gnorm_w, rec_w_out, loss_target, m_pre_norm_w, m_post_norm_w, m_attn_w_in, m_attn_b_in, m_attn_sinks, m_attn_w_out, m_attn_b_out, m_rec_w_in, m_rec_lb_logits, m_rec_gnorm_w, m_rec_w_out, v_pre_norm_w, v_post_norm_w, v_attn_w_in, v_attn_b_in, v_attn_sinks, v_attn_w_out, v_attn_b_out, v_rec_w_in, v_rec_lb_logits, v_rec_gnorm_w, v_rec_w_out):
    w = dict(pre_norm_w=pre_norm_w, post_norm_w=post_norm_w, attn_w_in=attn_w_in, attn_b_in=attn_b_in,
             attn_sinks=attn_sinks, attn_w_out=attn_w_out, attn_b_out=attn_b_out, rec_w_in=rec_w_in,
             rec_lb_logits=rec_lb_logits, rec_gnorm_w=rec_gnorm_w, rec_w_out=rec_w_out)
    m = dict(pre_norm_w=m_pre_norm_w, post_norm_w=m_post_norm_w, attn_w_in=m_attn_w_in, attn_b_in=m_attn_b_in,
             attn_sinks=m_attn_sinks, attn_w_out=m_attn_w_out, attn_b_out=m_attn_b_out, rec_w_in=m_rec_w_in,
             rec_lb_logits=m_rec_lb_logits, rec_gnorm_w=m_rec_gnorm_w, rec_w_out=m_rec_w_out)
    v = dict(pre_norm_w=v_pre_norm_w, post_norm_w=v_post_norm_w, attn_w_in=v_attn_w_in, attn_b_in=v_attn_b_in,
             attn_sinks=v_attn_sinks, attn_w_out=v_attn_w_out, attn_b_out=v_attn_b_out, rec_w_in=v_rec_w_in,
             rec_lb_logits=v_rec_lb_logits, rec_gnorm_w=v_rec_gnorm_w, rec_w_out=v_rec_w_out)

    shards = {name: w[name][0] for name in _BIG}
    sent = {name: shards[name].astype(BF16) for name in _BIG}
    attn_w_in_whole = _whole_from_shards("attn_w_in", chip_gather([sent["attn_w_in"]])[0])

    loss_vec, grad_x, grads, parts = local_step(
        x, positions, pre_norm_w, post_norm_w, attn_w_in_whole, attn_b_in, attn_sinks, sent["attn_w_out"],
        attn_b_out, sent["rec_w_in"], rec_lb_logits, rec_gnorm_w, sent["rec_w_out"], loss_target, distributed=True)

    parts["attn_w_in"], = chip_scatter([_shards_from_whole("attn_w_in", grads["attn_w_in"]).astype(BF16)])
    plane_sums = [sum_slots(parts[name]) for name in _BIG]
    other_sums = sibling_exchange(plane_sums)
    out_g, out_d, out_m, out_v = {}, {}, {}, {}
    for name, mine, other in zip(_BIG, plane_sums, other_sums):
        g, d, nm, nv = adamw(shards[name], m[name][0], v[name][0], mine, other)
        out_g[name], out_d[name], out_m[name], out_v[name] = g[None], d[None], nm[None], nv[None]

    small_sum = sum_slots(all_gather_small(_pack_small(grads, last_row=loss_vec)))
    loss = jnp.sum(small_sum[_SMALL_ROWS - 1]) * (0.5 / D_MODEL)
    packed = adamw(_pack_small(w), _pack_small(m), _pack_small(v), small_sum)
    for dst, val in zip((out_g, out_d, out_m, out_v), packed):
        dst.update(_unpack_small(val))

    return (loss, grad_x, *[out_g[n] for n in _ORDER], *[out_d[n] for n in _ORDER],
            *[out_m[n] for n in _ORDER], *[out_v[n] for n in _ORDER])
```

```python
import functools

import jax
import jax.numpy as jnp
from jax import lax
from jax.experimental import pallas as pl
from jax.experimental.pallas import tpu as pltpu

F32 = jnp.float32
BF16 = jnp.bfloat16
MESH = pl.DeviceIdType.MESH

D_MODEL = 1024
HEAD_DIM = 64
N_HEADS = 16
N_KV_HEADS = 2
GROUP = N_HEADS // N_KV_HEADS
KV_WIDTH = N_KV_HEADS * HEAD_DIM
ATTN_IN = 2 * D_MODEL + 2 * KV_WIDTH
ATTN_BLOCK = 128
ROPE_THETA = 500000.0
ROPE_DIM = HEAD_DIM // 4
REC_HEADS = 8
REC_DIM = 128
REC_IN = 4 * D_MODEL
REC_BLOCK = 128
DIAG = 8
NORM_EPS = 1e-6
N_CHIPS = 4
N_DEV = 8
LANES = 128

ADAM_LR = 0.001
ADAM_B1 = 0.9
ADAM_B2 = 0.999
ADAM_EPS = 1e-08
ADAM_WD = 0.01
ADAM_STEP = 10

VMEM_LIMIT = 56 * 1024 * 1024


def _cparams(n_axes):
    return pltpu.CompilerParams(dimension_semantics=("arbitrary",) * n_axes, vmem_limit_bytes=VMEM_LIMIT)


def _dot(a, b, contract):
    return lax.dot_general(a.astype(BF16), b.astype(BF16), (contract, ((), ())), preferred_element_type=F32)


_NN = ((1,), (0,))
_NT = ((1,), (1,))
_TN = ((0,), (0,))


@jax.custom_vjp
def mm_nn(a, b):
    return _dot(a, b, _NN)


mm_nn.defvjp(lambda a, b: (_dot(a, b, _NN), (a, b)),
             lambda res, g: (_dot(g, res[1], _NT), _dot(res[0], g, _TN)))


@jax.custom_vjp
def mm_nt(a, b):
    return _dot(a, b, _NT)


mm_nt.defvjp(lambda a, b: (_dot(a, b, _NT), (a, b)),
             lambda res, g: (_dot(g, res[1], _NN), _dot(g, res[0], _TN)))


@jax.custom_vjp
def mm_tn(a, b):
    return _dot(a, b, _TN)


mm_tn.defvjp(lambda a, b: (_dot(a, b, _TN), (a, b)),
             lambda res, g: (_dot(res[1], g, _NT), _dot(res[0], g, _NN)))


def _tri_dot(x, lower):
    n = x.shape[0]
    r = lax.broadcasted_iota(jnp.int32, (n, n), 0)
    c = lax.broadcasted_iota(jnp.int32, (n, n), 1)
    tri = ((c <= r) if lower else (c >= r)).astype(BF16)
    hi = x.astype(BF16)
    rest = x - hi.astype(F32)
    mid = rest.astype(BF16)
    lo = (rest - mid.astype(F32)).astype(BF16)
    dot = lambda p: lax.dot_general(tri, p, (_NN, ((), ())), preferred_element_type=F32)
    return (dot(lo) + dot(mid)) + dot(hi)


@jax.custom_vjp
def cumsum_rows(x):
    return _tri_dot(x, True)


cumsum_rows.defvjp(lambda x: (cumsum_rows(x), None), lambda _, g: (_tri_dot(g, False),))


@functools.partial(jax.custom_vjp, nondiff_argnums=(1,))
def roll_sub(x, d):
    return pltpu.roll(x, d, 1) if d else x


roll_sub.defvjp(lambda x, d: (roll_sub(x, d), None),
                lambda d, _, g: (roll_sub(g, (DIAG - d) % DIAG),))


def sigmoid(x):
    return 1.0 / (1.0 + jnp.exp(-x))


@jax.custom_vjp
def silu(x):
    return x * sigmoid(x)


def _silu_fwd(x):
    s = sigmoid(x)
    return x * s, (x, s)


silu.defvjp(_silu_fwd, lambda res, g: (g * (res[1] * (1.0 + res[0] * (1.0 - res[1]))),))


def log_sigmoid_pair(x):
    t = jnp.log(1.0 + jnp.exp(-jnp.abs(x)))
    return jnp.minimum(x, 0.0) - t, jnp.minimum(-x, 0.0) - t


def _forget_fwd(x, a):
    log_lb, log_1m_lb = log_sigmoid_pair(a)
    ls_f, ls_nf = log_sigmoid_pair(x)
    c = log_1m_lb + ls_f
    lf = jnp.maximum(log_lb, c) + jnp.log(1.0 + jnp.exp(-jnp.abs(log_lb - c)))
    k = jnp.exp(log_1m_lb + ls_nf)
    return (lf, k), (log_lb, log_1m_lb, ls_f, ls_nf, c, lf, k)


def _forget_bwd(res, g):
    log_lb, log_1m_lb, ls_f, ls_nf, c, lf, k = res
    g_lf, g_k = g
    wc = jnp.exp(jnp.minimum(c - lf, 0.0))
    gk = g_k * k
    dx = g_lf * wc * jnp.exp(ls_nf) - gk * jnp.exp(ls_f)
    lb = jnp.exp(log_lb)
    da = jnp.sum(g_lf * ((1.0 - wc) * jnp.exp(log_1m_lb) - wc * lb) - gk * lb, axis=0, keepdims=True)
    return dx, da


@jax.custom_vjp
def forget_gate(x, a):
    return _forget_fwd(x, a)[0]


forget_gate.defvjp(_forget_fwd, _forget_bwd)


@jax.custom_vjp
def decayed(x, e):
    return (x * jnp.exp(e)).astype(BF16).astype(F32)


def _decayed_fwd(x, e):
    y = decayed(x, e)
    return y, (y, e)


decayed.defvjp(_decayed_fwd, lambda res, g: (g * jnp.exp(res[1]), g * res[0]))


def _row(x, r):
    shape = x.shape

    @jax.custom_vjp
    def take(x):
        return x[r:r + 1, :]

    take.defvjp(lambda x: (x[r:r + 1, :], None),
                lambda _, g: (jnp.where(lax.broadcasted_iota(jnp.int32, shape, 0) == r, g, 0.0),))
    return take(x)


def _rms(x):
    return lax.rsqrt(jnp.mean(x * x, axis=-1, keepdims=True) + NORM_EPS)


def _attn_group(qs, k_a, v_a, k_b, v_b, zs, sink_a, sink_b, bias):
    def half(kh, vh, sink):
        s = mm_nt(qs, kh) + bias
        m = lax.stop_gradient(jnp.maximum(jnp.max(s, axis=-1, keepdims=True), jnp.max(sink, axis=-1, keepdims=True)))
        p = jnp.exp(s - m)
        denom = jnp.sum(p, axis=-1, keepdims=True) + jnp.sum(jnp.exp(sink - m), axis=-1, keepdims=True) * (1.0 / LANES)
        return mm_nn(p * (1.0 / denom), vh)

    return (half(k_a, v_a, sink_a) + half(k_b, v_b, sink_b)) * silu(zs)


SAFE_RANGE = 80.0


def _rec_front(qr, fr, l0, l1):
    lf, k = forget_gate(fr, l1 - l0)
    return silu(qr), k, lf


def _rec_tail(o, z, gw):
    return o * _rms(o) * gw * silu(z)


def _rec_margin(b):
    R = b.shape[0]
    mid, last = _row(b, R // 2 - 1), _row(b, R - 1)
    return jnp.minimum(mid, last - mid)


def _heads(x):
    w = x.shape[1] // REC_HEADS
    return [x[:, h * w:(h + 1) * w] for h in range(REC_HEADS)]


def _hdot(a, b, contract):
    return jnp.concatenate([_dot(ah, bh, contract) for ah, bh in zip(_heads(a), _heads(b))], axis=1)


@jax.custom_vjp
def hmm_nn(a, b):
    return _hdot(a, b, _NN)


hmm_nn.defvjp(lambda a, b: (_hdot(a, b, _NN), (a, b)),
              lambda res, g: (_hdot(g, res[1], _NT), _hdot(res[0], g, _TN)))


@jax.custom_vjp
def hmm_nt(a, b):
    return _hdot(a, b, _NT)


hmm_nt.defvjp(lambda a, b: (_hdot(a, b, _NT), (a, b)),
              lambda res, g: (_hdot(g, res[1], _NN), _hdot(g, res[0], _TN)))


@jax.custom_vjp
def hmm_tn(a, b):
    return _hdot(a, b, _TN)


hmm_tn.defvjp(lambda a, b: (_hdot(a, b, _TN), (a, b)),
              lambda res, g: (_hdot(res[1], g, _NT), _hdot(res[0], g, _NN)))


def _head_sums(x):
    return jnp.concatenate([jnp.broadcast_to(jnp.sum(xh, axis=-1, keepdims=True), xh.shape) for xh in _heads(x)],
                           axis=1)


@jax.custom_vjp
def head_sum(x):
    return _head_sums(x)


head_sum.defvjp(lambda x: (_head_sums(x), None), lambda _, g: (_head_sums(g),))


def _rec_cores_fast(q, k, v, b, S):
    R = q.shape[0]
    ri = lax.broadcasted_iota(jnp.int32, (R, REC_HEADS * R), 0)
    ci = lax.broadcasted_iota(jnp.int32, (R, REC_HEADS * R), 1) % R
    d = b - _row(b, R // 2 - 1)
    sc = jnp.where(ci < ri, hmm_nt(decayed(q, d), decayed(k, -d)), 0.0)
    o = hmm_nt(q * jnp.exp(b), S) + hmm_nn(sc, v) + head_sum(q * k) * v
    b_last = _row(b, R - 1)
    return o, S * jnp.exp(b_last) + hmm_tn(v, k * jnp.exp(b_last - b))


def _rec_tails(o, z, gw):
    return o * lax.rsqrt(head_sum(o * o) * (1.0 / REC_DIM) + NORM_EPS) * gw * silu(z)


def _rec_block_fast(qr, fr, v, z, S, l0, l1, gw):
    lf, k = forget_gate(fr, l1 - l0)
    o, S_new = _rec_cores_fast(silu(qr), k, v, cumsum_rows(lf), S)
    return _rec_tails(o, z, gw), S_new


def _rec_core_slow(q, k, v, b, S):
    R = q.shape[0]
    rows = lax.broadcasted_iota(jnp.int32, (R, REC_DIM), 0)

    o = mm_nt(q * jnp.exp(jnp.minimum(b, 0.0)), S)

    ri = lax.broadcasted_iota(jnp.int32, (R, R), 0)
    ci = lax.broadcasted_iota(jnp.int32, (R, R), 1)
    sc = jnp.zeros((R, R), F32)
    w = R
    while w > DIAG:
        h = w // 2
        b3 = b.reshape(R // w, w, REC_DIM)
        rin = lax.broadcasted_iota(jnp.int32, (R // w, w, REC_DIM), 1)
        mid = jnp.sum(jnp.where(rin == h - 1, b3, 0.0), axis=1, keepdims=True)
        fac = jnp.exp(jnp.minimum(jnp.where(rin >= h, b3 - mid, mid - b3), 0.0)).reshape(R, REC_DIM)
        upper = (rows % w) >= h
        s_w = mm_nt(jnp.where(upper, q * fac, 0.0), jnp.where(upper, 0.0, k * fac))
        sc = sc + jnp.where((ri // w) == (ci // w), s_w, 0.0)
        w = h
    o = o + mm_nn(sc, v)

    g = R // DIAG
    q3, k3, v3, b3 = (t.reshape(g, DIAG, REC_DIM) for t in (q, k, v, b))
    rin = lax.broadcasted_iota(jnp.int32, (g, DIAG, 1), 1)
    od = jnp.zeros((g, DIAG, REC_DIM), F32)
    for d in range(DIAG):
        e = jnp.exp(jnp.minimum(b3 - roll_sub(b3, d), 0.0))
        sd = jnp.sum(q3 * roll_sub(k3, d) * e, axis=-1, keepdims=True)
        od = od + jnp.where(rin >= d, sd, 0.0) * roll_sub(v3, d)
    o = o + od.reshape(R, REC_DIM)

    b_last = _row(b, R - 1)
    return o, S * jnp.exp(jnp.minimum(b_last, 0.0)) + mm_tn(v, k * jnp.exp(jnp.minimum(b_last - b, 0.0)))


def _rec_head(core, qr, fr, v, z, S, l0, l1, gw):
    q, k, lf = _rec_front(qr, fr, l0, l1)
    o, S_new = core(q, k, v, cumsum_rows(lf), S)
    return _rec_tail(o, z, gw), S_new


def _rope_tables(positions):
    half = ROPE_DIM // 2
    inv_freq = ROPE_THETA ** (-(jnp.arange(half, dtype=F32) * 2.0 / ROPE_DIM))
    rest = jnp.zeros((HEAD_DIM - ROPE_DIM,), F32)
    ones, zeros = jnp.ones((half,), F32), jnp.zeros((half,), F32)
    per_lane = lambda first, second: jnp.tile(jnp.concatenate([first, second, rest]), LANES // HEAD_DIM)[None, :]
    ang = positions.astype(F32).reshape(-1, 1) * per_lane(inv_freq, inv_freq)
    sin = jnp.sin(ang)
    return jnp.cos(ang), sin * per_lane(zeros, ones), sin * per_lane(-ones, zeros)


def _rope(x, cos_t, sin_a, sin_b):
    half = ROPE_DIM // 2
    return x * cos_t + pltpu.roll(x, half, 1) * sin_a + pltpu.roll(x, LANES - half, 1) * sin_b


def _rope_transposed(g, cos_t, sin_a, sin_b):
    half = ROPE_DIM // 2
    return g * cos_t + pltpu.roll(g * sin_a, LANES - half, 1) + pltpu.roll(g * sin_b, half, 1)


def _row_spec(tm, width):
    return pl.BlockSpec((tm, width), lambda i: (i, 0))


def _full_spec(shape):
    return pl.BlockSpec(shape, lambda *_: (0,) * len(shape))


def attn_in_proj(x, w_pre, w_in, b_in, tables, tm=1024):
    n = x.shape[0]
    tm = min(tm, n)

    def body(x_ref, wp_ref, w_ref, b_ref, c_ref, sa_ref, sb_ref, h_ref, q_ref, k_ref, v_ref, z_ref):
        xv = x_ref[...]
        h = (xv * _rms(xv) * wp_ref[...]).astype(BF16)
        h_ref[...] = h
        proj = jnp.dot(h, w_ref[...], preferred_element_type=F32) + b_ref[...]
        tabs = (c_ref[...], sa_ref[...], sb_ref[...])
        for s in range(D_MODEL // LANES):
            sl = slice(s * LANES, (s + 1) * LANES)
            q_ref[:, sl] = _rope(proj[:, sl] * (HEAD_DIM ** -0.5), *tabs).astype(BF16)
        k_ref[...] = _rope(proj[:, D_MODEL:D_MODEL + KV_WIDTH], *tabs).astype(BF16)
        v_ref[...] = proj[:, D_MODEL + KV_WIDTH:D_MODEL + 2 * KV_WIDTH].astype(BF16)
        z_ref[...] = proj[:, D_MODEL + 2 * KV_WIDTH:]

    return pl.pallas_call(
        body, name="attn_in_proj", grid=(n // tm,),
        in_specs=[_row_spec(tm, D_MODEL), _full_spec((1, D_MODEL)), _full_spec((D_MODEL, ATTN_IN)),
                  _full_spec((1, ATTN_IN))] + [_row_spec(tm, LANES)] * 3,
        out_specs=[_row_spec(tm, D_MODEL), _row_spec(tm, D_MODEL), _row_spec(tm, KV_WIDTH),
                   _row_spec(tm, KV_WIDTH), _row_spec(tm, D_MODEL)],
        out_shape=[jax.ShapeDtypeStruct((n, D_MODEL), BF16), jax.ShapeDtypeStruct((n, D_MODEL), BF16),
                   jax.ShapeDtypeStruct((n, KV_WIDTH), BF16), jax.ShapeDtypeStruct((n, KV_WIDTH), BF16),
                   jax.ShapeDtypeStruct((n, D_MODEL), F32)],
        compiler_params=_cparams(1),
    )(x, w_pre, w_in, b_in, *tables)


def rec_in_proj(x, w_pre, w_in, tm=512):
    n = x.shape[0]
    tm = min(tm, n)

    def body(x_ref, wp_ref, w_ref, h_ref, p_ref):
        xv = x_ref[...]
        h = (xv * _rms(xv) * wp_ref[...]).astype(BF16)
        h_ref[...] = h
        p_ref[...] = jnp.dot(h, w_ref[...], preferred_element_type=F32)

    return pl.pallas_call(
        body, name="rec_in_proj", grid=(n // tm,),
        in_specs=[_row_spec(tm, D_MODEL), _full_spec((1, D_MODEL)), _full_spec((D_MODEL, REC_IN))],
        out_specs=[_row_spec(tm, D_MODEL), _row_spec(tm, REC_IN)],
        out_shape=[jax.ShapeDtypeStruct((n, D_MODEL), BF16), jax.ShapeDtypeStruct((n, REC_IN), F32)],
        compiler_params=_cparams(1),
    )(x, w_pre, w_in)


def out_proj(og, w_out, b_out, x_res, w_post, target=None, tm=1024):
    n = og.shape[0]
    tm = min(tm, n)
    with_loss = target is not None

    def body(*refs):
        if with_loss:
            og_ref, w_ref, b_ref, x_ref, wp_ref, t_ref, y_ref, dx_ref, l_ref = refs
        else:
            og_ref, w_ref, b_ref, x_ref, wp_ref, y_ref, xo_ref = refs
        y = jnp.dot(og_ref[...], w_ref[...], preferred_element_type=F32) + b_ref[...]
        y_ref[...] = y.astype(BF16)
        xo = x_ref[...] + y * _rms(y) * wp_ref[...]
        if with_loss:
            err = xo - t_ref[...]
            dx_ref[...] = err * (1.0 / D_MODEL)

            @pl.when(pl.program_id(0) == 0)
            def _():
                l_ref[...] = jnp.zeros_like(l_ref)

            l_ref[...] += jnp.sum(err * err, axis=0, keepdims=True)
        else:
            xo_ref[...] = xo

    in_specs = [_row_spec(tm, D_MODEL), _full_spec((D_MODEL, D_MODEL)), _full_spec((1, D_MODEL)),
                _row_spec(tm, D_MODEL), _full_spec((1, D_MODEL))]
    out_specs = [_row_spec(tm, D_MODEL), _row_spec(tm, D_MODEL)]
    out_shape = [jax.ShapeDtypeStruct((n, D_MODEL), BF16), jax.ShapeDtypeStruct((n, D_MODEL), F32)]
    args = [og, w_out, b_out, x_res, w_post]
    if with_loss:
        in_specs.append(_row_spec(tm, D_MODEL))
        out_specs.append(_full_spec((1, D_MODEL)))
        out_shape.append(jax.ShapeDtypeStruct((1, D_MODEL), F32))
        args.append(target)
    return pl.pallas_call(
        body, name="out_proj_loss" if with_loss else "out_proj", grid=(n // tm,),
        in_specs=in_specs, out_specs=out_specs, out_shape=out_shape, compiler_params=_cparams(1),
    )(*args)


def out_proj_bwd(dxo, y, og, w_out, w_post, tm=1024):
    n = og.shape[0]
    tm = min(tm, n)

    def body(g_ref, y_ref, og_ref, w_ref, wp_ref, dog_ref, dw_ref, db_ref, dwp_ref):
        @pl.when(pl.program_id(0) == 0)
        def _():
            dw_ref[...] = jnp.zeros_like(dw_ref)
            db_ref[...] = jnp.zeros_like(db_ref)
            dwp_ref[...] = jnp.zeros_like(dwp_ref)

        g, y = g_ref[...], y_ref[...].astype(F32)
        rstd = _rms(y)
        yn = y * rstd
        gw = g * wp_ref[...]
        dwp_ref[...] += jnp.sum(g * yn, axis=0, keepdims=True)
        dy = rstd * (gw - yn * jnp.mean(gw * yn, axis=-1, keepdims=True))
        db_ref[...] += jnp.sum(dy, axis=0, keepdims=True)
        dyb = dy.astype(BF16)
        dog_ref[...] = _dot(dyb, w_ref[...], _NT).astype(BF16)
        dw_ref[...] += _dot(og_ref[...], dyb, _TN)

    return pl.pallas_call(
        body, name="out_proj_bwd", grid=(n // tm,),
        in_specs=[_row_spec(tm, D_MODEL), _row_spec(tm, D_MODEL), _row_spec(tm, D_MODEL),
                  _full_spec((D_MODEL, D_MODEL)), _full_spec((1, D_MODEL))],
        out_specs=[_row_spec(tm, D_MODEL), _full_spec((D_MODEL, D_MODEL)), _full_spec((1, D_MODEL)),
                   _full_spec((1, D_MODEL))],
        out_shape=[jax.ShapeDtypeStruct((n, D_MODEL), BF16), jax.ShapeDtypeStruct((D_MODEL, D_MODEL), F32),
                   jax.ShapeDtypeStruct((1, D_MODEL), F32), jax.ShapeDtypeStruct((1, D_MODEL), F32)],
        compiler_params=_cparams(1),
    )(dxo, y, og, w_out, w_post)


def in_proj_bwd_x(dproj, w_in, x, w_pre, dxo, tm=512, scatter=()):
    n, p = dproj.shape
    tm = min(tm if p > ATTN_IN else 2 * tm, n)
    steps = n // tm
    ns = len(scatter)

    def body(*refs):
        dp_ref, w_ref, x_ref, wp_ref, g_ref = refs[:5]
        dx_ref, dwp_ref = refs[5 + ns:7 + ns]
        exchange = (refs[5:5 + ns], refs[7 + ns:7 + 2 * ns]) + tuple(refs[7 + 2 * ns:])

        @pl.when(pl.program_id(0) == 0)
        def _():
            dwp_ref[...] = jnp.zeros_like(dwp_ref)
            if ns:
                _scatter_start(*exchange)

        dh = _dot(dp_ref[...], w_ref[...], _NT)
        xv = x_ref[...]
        rstd = _rms(xv)
        xn = xv * rstd
        gw = dh * wp_ref[...]
        dwp_ref[...] += jnp.sum(dh * xn, axis=0, keepdims=True)
        dx_ref[...] = rstd * (gw - xn * jnp.mean(gw * xn, axis=-1, keepdims=True)) + g_ref[...]

        if ns:
            @pl.when(pl.program_id(0) == steps - 1)
            def _():
                _scatter_finish(*exchange)

    out = pl.pallas_call(
        body, name=f"in_proj_bwd_x_{p}", grid=(steps,),
        in_specs=[_row_spec(tm, p), _full_spec((D_MODEL, p)), _row_spec(tm, D_MODEL), _full_spec((1, D_MODEL)),
                  _row_spec(tm, D_MODEL)] + [_ANY] * ns,
        out_specs=[_row_spec(tm, D_MODEL), _full_spec((1, D_MODEL))] + [_ANY] * ns,
        out_shape=[jax.ShapeDtypeStruct((n, D_MODEL), F32), jax.ShapeDtypeStruct((1, D_MODEL), F32)]
        + [jax.ShapeDtypeStruct(a.shape, a.dtype) for a in scatter],
        scratch_shapes=_scatter_sems(ns) if ns else [],
        compiler_params=_cparams(1),
    )(dproj, w_in, x, w_pre, dxo, *scatter)
    return out[0], out[1], out[2:]


def in_proj_bwd_w(h, dproj, tm=1024):
    n, p = dproj.shape
    chunk = p // (4 if p % 4096 == 0 else 3)
    tm = min(tm, n)
    steps = n // tm

    def body(h_ref, dp_ref, dw_ref, db_ref, acc_scr, sem):
        i = pl.program_id(0)

        @pl.when(i == 0)
        def _():
            acc_scr[...] = jnp.zeros_like(acc_scr)
            db_ref[...] = jnp.zeros_like(db_ref)

        ht = h_ref[...].T
        for c0 in range(0, p, chunk):
            dp = dp_ref[:, c0:c0 + chunk]
            acc_scr[:, c0:c0 + chunk] += jnp.dot(ht, dp, preferred_element_type=F32)
            db_ref[:, c0:c0 + chunk] += jnp.sum(dp.astype(F32), axis=0, keepdims=True)

        @pl.when(i == steps - 1)
        def _():
            out = pltpu.make_async_copy(acc_scr, dw_ref, sem)
            out.start()
            out.wait()

    return pl.pallas_call(
        body, name=f"in_proj_bwd_w_{p}", grid=(steps,),
        in_specs=[_row_spec(tm, D_MODEL), _row_spec(tm, p)],
        out_specs=[_ANY, _full_spec((1, p))],
        out_shape=[jax.ShapeDtypeStruct((D_MODEL, p), F32), jax.ShapeDtypeStruct((1, p), F32)],
        scratch_shapes=[pltpu.VMEM((D_MODEL, p), F32), pltpu.SemaphoreType.DMA],
        compiler_params=_cparams(1),
    )(h, dproj)


PAIRS = GROUP // 2
GROUP_ROWS = PAIRS * ATTN_BLOCK
MASKED = -1e30


def _kv_windows(k_ref, v_ref, i):
    ps = pl.multiple_of(jnp.maximum(i - 1, 0) * ATTN_BLOCK, ATTN_BLOCK)
    cs = pl.multiple_of(i * ATTN_BLOCK, ATTN_BLOCK)
    kw = jnp.concatenate([k_ref[pl.ds(ps, ATTN_BLOCK), :], k_ref[pl.ds(cs, ATTN_BLOCK), :]], axis=0)
    vw = jnp.concatenate([v_ref[pl.ds(ps, ATTN_BLOCK), :], v_ref[pl.ds(cs, ATTN_BLOCK), :]], axis=0)
    return kw.astype(F32), vw.astype(F32), ps, cs


def _low_lanes(shape):
    return lax.broadcasted_iota(jnp.int32, shape, 1) < HEAD_DIM


def _spread(w, kvh):
    low = _low_lanes(w.shape)
    swapped = pltpu.roll(w, HEAD_DIM, 1)
    if kvh == 0:
        return jnp.where(low, w, 0.0), jnp.where(low, 0.0, swapped)
    return jnp.where(low, swapped, 0.0), jnp.where(low, 0.0, w)


def _unspread(d_a, d_b, kvh):
    low = _low_lanes(d_a.shape)
    if kvh == 0:
        return jnp.where(low, d_a + pltpu.roll(d_b, HEAD_DIM, 1), 0.0)
    return jnp.where(low, 0.0, pltpu.roll(d_a, HEAD_DIM, 1) + d_b)


def _stack_pairs(ref, kvh):
    return jnp.concatenate([ref[:, (kvh * PAIRS + j) * LANES:(kvh * PAIRS + j + 1) * LANES] for j in range(PAIRS)],
                           axis=0)


def _fill_bias(bias_scr):
    shape = (GROUP_ROWS, 2 * ATTN_BLOCK)
    r = lax.broadcasted_iota(jnp.int32, shape, 0) % ATTN_BLOCK
    c = lax.broadcasted_iota(jnp.int32, shape, 1)
    in_cur = (c >= ATTN_BLOCK) & ((c - ATTN_BLOCK) <= r)
    in_prev = (c < ATTN_BLOCK) & (c > r)
    bias_scr[0] = jnp.where(in_cur, 0.0, MASKED)
    bias_scr[1] = jnp.where(in_cur | in_prev, 0.0, MASKED)


def _sink_table(sinks):
    t = jnp.transpose(sinks.reshape(N_KV_HEADS, PAIRS, 2), (0, 2, 1))
    return jnp.broadcast_to(t[:, :, :, None, None], (N_KV_HEADS, 2, PAIRS, ATTN_BLOCK, LANES)).reshape(
        N_KV_HEADS, 2, GROUP_ROWS, LANES)


def attn_fwd(q, k, v, z, sink_tab, batch, seq, gather=()):
    nb = seq // ATTN_BLOCK
    ng = len(gather)

    def body(*refs):
        q_ref, k_ref, v_ref, z_ref, s_ref = refs[:5]
        og_ref, bias_scr = refs[5 + ng], refs[6 + 2 * ng]
        exchange = (refs[5:5 + ng], refs[6 + ng:6 + 2 * ng]) + tuple(refs[7 + 2 * ng:])
        b, i = pl.program_id(0), pl.program_id(1)

        @pl.when((b == 0) & (i == 0))
        def _():
            _fill_bias(bias_scr)
            if ng:
                _gather_start(*exchange)

        kw, vw, _, _ = _kv_windows(k_ref, v_ref, i)
        bias = bias_scr[jnp.minimum(i, 1)]
        for kvh in range(N_KV_HEADS):
            k_a, k_b = _spread(kw, kvh)
            v_a, v_b = _spread(vw, kvh)
            og = _attn_group(_stack_pairs(q_ref, kvh), k_a, v_a, k_b, v_b, _stack_pairs(z_ref, kvh),
                             s_ref[kvh, 0], s_ref[kvh, 1], bias)
            for j in range(PAIRS):
                og_ref[:, (kvh * PAIRS + j) * LANES:(kvh * PAIRS + j + 1) * LANES] = (
                    og[j * ATTN_BLOCK:(j + 1) * ATTN_BLOCK].astype(BF16))

        if ng:
            @pl.when((b == batch - 1) & (i == nb - 1))
            def _():
                _gather_finish(*exchange)

    blk = lambda w: pl.BlockSpec((ATTN_BLOCK, w), lambda b, i: (b * nb + i, 0))
    seq_spec = pl.BlockSpec((seq, KV_WIDTH), lambda b, i: (b, 0))
    out = pl.pallas_call(
        body, name="attn_fwd", grid=(batch, nb),
        in_specs=[blk(D_MODEL), seq_spec, seq_spec, blk(D_MODEL), _full_spec(sink_tab.shape)] + [_ANY] * ng,
        out_specs=[blk(D_MODEL)] + [_ANY] * ng,
        out_shape=[jax.ShapeDtypeStruct((batch * seq, D_MODEL), BF16)]
        + [jax.ShapeDtypeStruct((N_CHIPS,) + a.shape, a.dtype) for a in gather],
        scratch_shapes=[pltpu.VMEM((2, GROUP_ROWS, 2 * ATTN_BLOCK), F32)] + (_gather_sems(ng) if ng else []),
        compiler_params=_cparams(2),
    )(q, k, v, z, sink_tab, *gather)
    return out[0], out[1:]


def attn_bwd(q, k, v, z, sink_tab, dog, tables, batch, seq, scatter=()):
    nb = seq // ATTN_BLOCK
    ns = len(scatter)

    def body(*refs):
        q_ref, k_ref, v_ref, z_ref, s_ref, g_ref, c_ref, sa_ref, sb_ref = refs[:9]
        dp_ref, dk_ref, dv_ref, ds_ref = refs[9 + ns:13 + ns]
        bias_scr = refs[13 + 2 * ns]
        exchange = (refs[9:9 + ns], refs[13 + ns:13 + 2 * ns]) + tuple(refs[14 + 2 * ns:])
        b, i = pl.program_id(0), pl.program_id(1)

        @pl.when((b == 0) & (i == 0))
        def _():
            _fill_bias(bias_scr)
            ds_ref[...] = jnp.zeros_like(ds_ref)
            if ns:
                _scatter_start(*exchange)

        @pl.when(i == 0)
        def _():
            dk_ref[...] = jnp.zeros_like(dk_ref)
            dv_ref[...] = jnp.zeros_like(dv_ref)

        kw, vw, ps, cs = _kv_windows(k_ref, v_ref, i)
        bias = bias_scr[jnp.minimum(i, 1)]
        tabs = (c_ref[...], sa_ref[...], sb_ref[...])
        dkw = jnp.zeros_like(kw)
        dvw = jnp.zeros_like(vw)
        for kvh in range(N_KV_HEADS):
            k_a, k_b = _spread(kw, kvh)
            v_a, v_b = _spread(vw, kvh)
            _, vjp = jax.vjp(functools.partial(_attn_group, bias=bias), _stack_pairs(q_ref, kvh).astype(F32),
                             k_a, v_a, k_b, v_b, _stack_pairs(z_ref, kvh), s_ref[kvh, 0], s_ref[kvh, 1])
            dqs, dk_a, dv_a, dk_b, dv_b, dzs, ds_a, ds_b = vjp(_stack_pairs(g_ref, kvh).astype(F32))
            dkw = dkw + _unspread(dk_a, dk_b, kvh)
            dvw = dvw + _unspread(dv_a, dv_b, kvh)
            ds_ref[kvh, 0] += jnp.sum(ds_a.reshape(PAIRS, ATTN_BLOCK, LANES), axis=1)
            ds_ref[kvh, 1] += jnp.sum(ds_b.reshape(PAIRS, ATTN_BLOCK, LANES), axis=1)
            for j in range(PAIRS):
                rows = slice(j * ATTN_BLOCK, (j + 1) * ATTN_BLOCK)
                col = (kvh * PAIRS + j) * LANES
                dp_ref[:, col:col + LANES] = _rope_transposed(dqs[rows] * (HEAD_DIM ** -0.5), *tabs).astype(BF16)
                zc = D_MODEL + 2 * KV_WIDTH + col
                dp_ref[:, zc:zc + LANES] = dzs[rows].astype(BF16)
        dp_ref[:, D_MODEL:D_MODEL + 2 * KV_WIDTH] = jnp.zeros((ATTN_BLOCK, 2 * KV_WIDTH), BF16)
        dk_ref[pl.ds(ps, ATTN_BLOCK), :] += dkw[:ATTN_BLOCK]
        dk_ref[pl.ds(cs, ATTN_BLOCK), :] += dkw[ATTN_BLOCK:]
        dv_ref[pl.ds(ps, ATTN_BLOCK), :] += dvw[:ATTN_BLOCK]
        dv_ref[pl.ds(cs, ATTN_BLOCK), :] += dvw[ATTN_BLOCK:]

        if ns:
            @pl.when((b == batch - 1) & (i == nb - 1))
            def _():
                _scatter_finish(*exchange)

    blk = lambda w: pl.BlockSpec((ATTN_BLOCK, w), lambda b, i: (b * nb + i, 0))
    seq_spec = pl.BlockSpec((seq, KV_WIDTH), lambda b, i: (b, 0))
    n = batch * seq
    ds_shape = (N_KV_HEADS, 2, PAIRS, LANES)
    out = pl.pallas_call(
        body, name="attn_bwd", grid=(batch, nb),
        in_specs=[blk(D_MODEL), seq_spec, seq_spec, blk(D_MODEL), _full_spec(sink_tab.shape), blk(D_MODEL)]
        + [blk(LANES)] * 3 + [_ANY] * ns,
        out_specs=[blk(ATTN_IN), seq_spec, seq_spec, _full_spec(ds_shape)] + [_ANY] * ns,
        out_shape=[jax.ShapeDtypeStruct((n, ATTN_IN), BF16), jax.ShapeDtypeStruct((n, KV_WIDTH), F32),
                   jax.ShapeDtypeStruct((n, KV_WIDTH), F32), jax.ShapeDtypeStruct(ds_shape, F32)]
        + [jax.ShapeDtypeStruct(a.shape, a.dtype) for a in scatter],
        scratch_shapes=[pltpu.VMEM((2, GROUP_ROWS, 2 * ATTN_BLOCK), F32)] + (_scatter_sems(ns) if ns else []),
        compiler_params=_cparams(2),
    )(q, k, v, z, sink_tab, dog, *tables, *scatter)
    return out[0], out[1], out[2], out[3], out[4:]


def attn_bwd_kv(dproj, dk, dv, tables, tm=512):
    n = dproj.shape[0]

    def body(dp_in_ref, dk_ref, dv_ref, c_ref, sa_ref, sb_ref, dp_ref):
        del dp_in_ref
        dp_ref[:, :KV_WIDTH] = _rope_transposed(dk_ref[...], c_ref[...], sa_ref[...], sb_ref[...]).astype(BF16)
        dp_ref[:, KV_WIDTH:] = dv_ref[...].astype(BF16)

    kv_cols = pl.BlockSpec((tm, 2 * KV_WIDTH), lambda i: (i, D_MODEL // (2 * KV_WIDTH)))
    return pl.pallas_call(
        body, name="attn_bwd_kv", grid=(n // tm,),
        in_specs=[kv_cols, _row_spec(tm, KV_WIDTH), _row_spec(tm, KV_WIDTH)] + [_row_spec(tm, LANES)] * 3,
        out_specs=kv_cols, out_shape=jax.ShapeDtypeStruct(dproj.shape, BF16),
        input_output_aliases={0: 0}, compiler_params=_cparams(1),
    )(dproj, dk, dv, *tables)


def rec_fwd(proj, lb_logits, gnorm_w, batch, seq):
    nblk = seq // REC_BLOCK

    def body(p_ref, lb_ref, gw_ref, og_ref, st_ref, safe_ref, s_scr):
        @pl.when(pl.program_id(1) == 0)
        def _():
            s_scr[...] = jnp.zeros_like(s_scr)

        S = s_scr[...]
        st_ref[0] = S
        qr, fr, v, z = (p_ref[:, part * D_MODEL:(part + 1) * D_MODEL] for part in range(4))
        lf, k = forget_gate(fr, lb_ref[1:2, :] - lb_ref[0:1, :])
        q, b = silu(qr), cumsum_rows(lf)
        safe = jnp.min(_rec_margin(b)) >= -SAFE_RANGE

        def head_by_head():
            outs = [_rec_core_slow(*args) for args in zip(*(_heads(t) for t in (q, k, v, b, S)))]
            return tuple(jnp.concatenate(parts, axis=1) for parts in zip(*outs))

        o, S_new = lax.cond(safe, lambda: _rec_cores_fast(q, k, v, b, S), head_by_head)
        og_ref[...] = _rec_tails(o, z, gw_ref[...]).astype(BF16)
        s_scr[...] = S_new
        safe_ref[0] = jnp.full((REC_HEADS, LANES), safe.astype(F32))

    blk = lambda w: pl.BlockSpec((REC_BLOCK, w), lambda b, j: (b * nblk + j, 0))
    st_spec = pl.BlockSpec((1, REC_DIM, D_MODEL), lambda b, j: (b * nblk + j, 0, 0))
    safe_spec = pl.BlockSpec((1, REC_HEADS, LANES), lambda b, j: (b * nblk + j, 0, 0))
    return pl.pallas_call(
        body, name="rec_fwd", grid=(batch, nblk),
        in_specs=[blk(REC_IN), _full_spec((2, D_MODEL)), _full_spec((1, D_MODEL))],
        out_specs=[blk(D_MODEL), st_spec, safe_spec],
        out_shape=[jax.ShapeDtypeStruct((batch * seq, D_MODEL), BF16),
                   jax.ShapeDtypeStruct((batch * nblk, REC_DIM, D_MODEL), F32),
                   jax.ShapeDtypeStruct((batch * nblk, REC_HEADS, LANES), F32)],
        scratch_shapes=[pltpu.VMEM((REC_DIM, D_MODEL), F32)],
        compiler_params=_cparams(2),
    )(proj, lb_logits, jnp.tile(gnorm_w, (1, REC_HEADS)))


def rec_bwd(proj, states, safe, lb_logits, gnorm_w, dog, batch, seq):
    nblk = seq // REC_BLOCK

    def body(p_ref, st_ref, safe_ref, lb_ref, gw_ref, g_ref, dp_ref, dlb_ref, dgw_ref, ds_scr):
        @pl.when((pl.program_id(0) == 0) & (pl.program_id(1) == 0))
        def _():
            dlb_ref[...] = jnp.zeros_like(dlb_ref)
            dgw_ref[...] = jnp.zeros_like(dgw_ref)

        @pl.when(pl.program_id(1) == 0)
        def _():
            ds_scr[...] = jnp.zeros_like(ds_scr)

        primals = tuple(p_ref[:, part * D_MODEL:(part + 1) * D_MODEL] for part in range(4)) + (
            st_ref[0], lb_ref[0:1, :], lb_ref[1:2, :], gw_ref[...])
        cotangents = (g_ref[...].astype(F32), ds_scr[...])

        def all_heads():
            return jax.vjp(_rec_block_fast, *primals)[1](cotangents)

        def head_by_head():
            outs = [jax.vjp(functools.partial(_rec_head, _rec_core_slow), *args)[1](cts)
                    for args, cts in zip(zip(*(_heads(t) for t in primals)), zip(*(_heads(t) for t in cotangents)))]
            return tuple(jnp.concatenate(parts, axis=1) for parts in zip(*outs))

        dqr, dfr, dv, dz, dS, dl0, dl1, dgw = lax.cond(jnp.max(safe_ref[0]) > 0.5, all_heads, head_by_head)
        for part, val in enumerate((dqr, dfr, dv, dz)):
            dp_ref[:, part * D_MODEL:(part + 1) * D_MODEL] = val.astype(BF16)
        ds_scr[...] = dS
        dlb_ref[0:1, :] += dl0
        dlb_ref[1:2, :] += dl1
        dgw_ref[...] += functools.reduce(jnp.add, _heads(dgw))

    blk = lambda w: pl.BlockSpec((REC_BLOCK, w), lambda b, j: (b * nblk + nblk - 1 - j, 0))
    st_spec = pl.BlockSpec((1, REC_DIM, D_MODEL), lambda b, j: (b * nblk + nblk - 1 - j, 0, 0))
    safe_spec = pl.BlockSpec((1, REC_HEADS, LANES), lambda b, j: (b * nblk + nblk - 1 - j, 0, 0))
    return pl.pallas_call(
        body, name="rec_bwd", grid=(batch, nblk),
        in_specs=[blk(REC_IN), st_spec, safe_spec, _full_spec((2, D_MODEL)), _full_spec((1, D_MODEL)),
                  blk(D_MODEL)],
        out_specs=[blk(REC_IN), _full_spec((2, D_MODEL)), _full_spec((1, REC_DIM))],
        out_shape=[jax.ShapeDtypeStruct((batch * seq, REC_IN), BF16), jax.ShapeDtypeStruct((2, D_MODEL), F32),
                   jax.ShapeDtypeStruct((1, REC_DIM), F32)],
        scratch_shapes=[pltpu.VMEM((REC_DIM, D_MODEL), F32)],
        compiler_params=_cparams(2),
    )(proj, states, safe, lb_logits, jnp.tile(gnorm_w, (1, REC_HEADS)), dog)


_ANY = pl.BlockSpec(memory_space=pl.ANY)


def _chip_peers():
    x, y, c = lax.axis_index("x"), lax.axis_index("y"), lax.axis_index("c")
    peers = []
    for fx, fy in ((1, 0), (0, 1), (1, 1)):
        px, py = (1 - x if fx else x), (1 - y if fy else y)
        peers.append(((px, py, c), 2 * px + py))
    return 2 * x + y, peers


def _remote(src, dst, send_sem, recv_sem, device):
    return pltpu.make_async_remote_copy(src_ref=src, dst_ref=dst, send_sem=send_sem, recv_sem=recv_sem,
                                        device_id=device, device_id_type=MESH)


N_FLIPS = N_CHIPS - 1


def _scatter_sems(n):
    return [pltpu.SemaphoreType.DMA((n * N_FLIPS,)), pltpu.SemaphoreType.DMA((n * N_FLIPS,)),
            pltpu.SemaphoreType.DMA((n,))]


def _scatter_copies(ins, outs, send_sems, recv_sems, local_sems, starting):
    me, peers = _chip_peers()
    local = [pltpu.make_async_copy(ins[k].at[me], outs[k].at[me], local_sems.at[k]) for k in range(len(ins))]
    sends, arrivals = [], []
    for k in range(len(ins)):
        for j, (device, idx) in enumerate(peers):
            sems = (send_sems.at[k * N_FLIPS + j], recv_sems.at[k * N_FLIPS + j], device)
            sends.append(_remote(ins[k].at[idx], outs[k].at[me], *sems))
            if not starting:
                arrivals.append(_remote(ins[k].at[me], outs[k].at[idx], *sems))
    return local, sends, arrivals


def _scatter_start(*refs):
    local, sends, _ = _scatter_copies(*refs, starting=True)
    for cp in local + sends:
        cp.start()


def _scatter_finish(*refs):
    local, sends, arrivals = _scatter_copies(*refs, starting=False)
    for cp in arrivals:
        cp.wait_recv()
    for cp in sends:
        cp.wait_send()
    for cp in local:
        cp.wait()


def _gather_sems(n):
    return [pltpu.SemaphoreType.DMA((n * N_FLIPS,)) for _ in range(4)] + [pltpu.SemaphoreType.DMA((n,))]


def _gather_copies(ins, outs, send_sems, recv_sems, pass_send_sems, pass_recv_sems, local_sems, starting):
    me, peers = _chip_peers()
    c = lax.axis_index("c")
    sibling = (lax.axis_index("x"), lax.axis_index("y"), 1 - c)
    local = [pltpu.make_async_copy(ins[k], outs[k].at[me], local_sems.at[k]) for k in range(len(ins))]
    sends, arrivals, passes, pass_arrivals = [], [], [], []
    for k in range(len(ins)):
        half = ins[k].shape[0] // 2
        mine, other = pl.ds(c * half, half), pl.ds((1 - c) * half, half)
        for j, (device, idx) in enumerate(peers):
            s = k * N_FLIPS + j
            sends.append(_remote(ins[k].at[mine], outs[k].at[me].at[mine], send_sems.at[s], recv_sems.at[s], device))
            if starting:
                continue
            arrived = outs[k].at[idx].at[mine]
            arrivals.append(_remote(ins[k].at[mine], arrived, send_sems.at[s], recv_sems.at[s], device))
            passes.append(_remote(arrived, arrived, pass_send_sems.at[s], pass_recv_sems.at[s], sibling))
            passed = outs[k].at[idx].at[other]
            pass_arrivals.append(_remote(passed, passed, pass_send_sems.at[s], pass_recv_sems.at[s], sibling))
    return local, sends, arrivals, passes, pass_arrivals


def _gather_start(*refs):
    local, sends, _, _, _ = _gather_copies(*refs, starting=True)
    for cp in local + sends:
        cp.start()


def _gather_finish(*refs):
    local, sends, arrivals, passes, pass_arrivals = _gather_copies(*refs, starting=False)
    for arrival, onward in zip(arrivals, passes):
        arrival.wait_recv()
        onward.start()
    for cp in pass_arrivals:
        cp.wait_recv()
    for cp in sends + passes:
        cp.wait_send()
    for cp in local:
        cp.wait()


def chip_gather(arrays):
    n = len(arrays)

    def body(*refs):
        _gather_start(refs[:n], refs[n:2 * n], *refs[2 * n:])
        _gather_finish(refs[:n], refs[n:2 * n], *refs[2 * n:])

    return pl.pallas_call(
        body, name="chip_gather", in_specs=[_ANY] * n, out_specs=[_ANY] * n,
        out_shape=[jax.ShapeDtypeStruct((N_CHIPS,) + a.shape, a.dtype) for a in arrays],
        scratch_shapes=_gather_sems(n),
    )(*arrays)


def sibling_exchange(arrays):
    n = len(arrays)

    def body(*refs):
        ins, outs = refs[:n], refs[n:2 * n]
        send_sems, recv_sems = refs[2 * n:]
        sibling = (lax.axis_index("x"), lax.axis_index("y"), 1 - lax.axis_index("c"))
        copies = [pltpu.make_async_remote_copy(src_ref=ins[k], dst_ref=outs[k], send_sem=send_sems.at[k],
                                               recv_sem=recv_sems.at[k], device_id=sibling, device_id_type=MESH)
                  for k in range(n)]
        for cp in copies:
            cp.start()
        for cp in copies:
            cp.wait()

    return pl.pallas_call(
        body, name="sibling_exchange", in_specs=[_ANY] * n, out_specs=[_ANY] * n,
        out_shape=[jax.ShapeDtypeStruct(a.shape, a.dtype) for a in arrays],
        scratch_shapes=[pltpu.SemaphoreType.DMA((n,)), pltpu.SemaphoreType.DMA((n,))],
    )(*arrays)


def all_gather_small(vec):
    def body(v_ref, out_ref, send_sems, recv_sems, local_sem):
        x, y, c = lax.axis_index("x"), lax.axis_index("y"), lax.axis_index("c")
        me = 4 * x + 2 * y + c
        local = pltpu.make_async_copy(v_ref, out_ref.at[me], local_sem)
        local.start()
        sends, recvs = [], []
        for j in range(1, N_DEV):
            px = jnp.where(j & 4, 1 - x, x)
            py = jnp.where(j & 2, 1 - y, y)
            pc = jnp.where(j & 1, 1 - c, c)
            common = dict(send_sem=send_sems.at[j - 1], recv_sem=recv_sems.at[j - 1], device_id=(px, py, pc),
                          device_id_type=MESH)
            sends.append(pltpu.make_async_remote_copy(src_ref=v_ref, dst_ref=out_ref.at[me], **common))
            recvs.append(pltpu.make_async_remote_copy(src_ref=v_ref, dst_ref=out_ref.at[4 * px + 2 * py + pc],
                                                      **common))
        for cp in sends:
            cp.start()
        for cp in recvs:
            cp.wait_recv()
        for cp in sends:
            cp.wait_send()
        local.wait()

    return pl.pallas_call(
        body, name="all_gather_small", in_specs=[_ANY], out_specs=_ANY,
        out_shape=jax.ShapeDtypeStruct((N_DEV,) + vec.shape, vec.dtype),
        scratch_shapes=[pltpu.SemaphoreType.DMA((N_DEV - 1,)), pltpu.SemaphoreType.DMA((N_DEV - 1,)),
                        pltpu.SemaphoreType.DMA],
    )(vec)


def sum_slots(stacked, tm=256):
    s, r, c = stacked.shape
    tm = min(tm, r)

    def body(in_ref, out_ref):
        acc = in_ref[0].astype(F32)
        for t in range(1, s):
            acc = acc + in_ref[t].astype(F32)
        out_ref[...] = acc

    return pl.pallas_call(
        body, name=f"sum_slots_{s}_{r}_{c}", grid=(r // tm,),
        in_specs=[pl.BlockSpec((s, tm, c), lambda i: (0, i, 0))], out_specs=_row_spec(tm, c),
        out_shape=jax.ShapeDtypeStruct((r, c), F32), compiler_params=_cparams(1),
    )(stacked)


def adamw(w, m, v, g_a, g_b=None, tm=256):
    r, c = w.shape
    tm = min(tm, r)
    two = g_b is not None

    def body(*refs):
        w_ref, m_ref, v_ref, ga_ref = refs[:4]
        g_ref, d_ref, nm_ref, nv_ref = refs[-4:]
        g = ga_ref[...] + refs[4][...] if two else ga_ref[...]
        nm = ADAM_B1 * m_ref[...] + (1.0 - ADAM_B1) * g
        nv = ADAM_B2 * v_ref[...] + (1.0 - ADAM_B2) * (g * g)
        m_hat = nm / (1.0 - ADAM_B1 ** ADAM_STEP)
        v_hat = nv / (1.0 - ADAM_B2 ** ADAM_STEP)
        g_ref[...] = g
        d_ref[...] = -ADAM_LR * (m_hat / (jnp.sqrt(v_hat) + ADAM_EPS) + ADAM_WD * w_ref[...])
        nm_ref[...] = nm
        nv_ref[...] = nv

    args = [w, m, v, g_a] + ([g_b] if two else [])
    return pl.pallas_call(
        body, name=f"adamw_{r}_{c}", grid=(r // tm,),
        in_specs=[_row_spec(tm, c)] * len(args), out_specs=[_row_spec(tm, c)] * 4,
        out_shape=[jax.ShapeDtypeStruct((r, c), F32)] * 4, compiler_params=_cparams(1),
    )(*args)


_SMALL = (("pre_norm_w", (2, D_MODEL)), ("post_norm_w", (2, D_MODEL)), ("attn_b_in", (1, ATTN_IN)),
          ("attn_sinks", (1, N_HEADS)), ("attn_b_out", (1, D_MODEL)), ("rec_lb_logits", (2, D_MODEL)),
          ("rec_gnorm_w", (1, REC_DIM)))
_SMALL_ROWS = 16


def _pack_small(parts, last_row=None):
    rows = []
    for (name, shape) in _SMALL:
        flat = parts[name].reshape(-1)
        pad = -flat.shape[0] % D_MODEL
        rows.append(jnp.pad(flat, (0, pad)).reshape(-1, D_MODEL))
    used = sum(r.shape[0] for r in rows)
    rows.append(jnp.zeros((_SMALL_ROWS - 1 - used, D_MODEL), F32))
    rows.append(jnp.zeros((1, D_MODEL), F32) if last_row is None else last_row)
    return jnp.concatenate(rows, axis=0)


def _unpack_small(packed):
    out, row = {}, 0
    for (name, shape) in _SMALL:
        size = shape[0] * shape[1]
        nrows = -(-size // D_MODEL)
        out[name] = packed[row:row + nrows].reshape(-1)[:size].reshape(shape)
        row += nrows
    return out


_CARRIED = ("rec_w_in", "rec_w_out", "attn_w_out")


_LATE = ("attn_w_out", "rec_w_in", "rec_w_out")


def local_step(x, positions, pre_norm_w, post_norm_w, attn_w_in, attn_b_in, attn_sinks, attn_w_out, attn_b_out,
               rec_w_in, rec_lb_logits, rec_gnorm_w, rec_w_out, loss_target, distributed=False):
    batch, seq, _ = x.shape
    n = batch * seq
    x0 = x.reshape(n, D_MODEL)
    tables = _rope_tables(positions)
    pre0, pre1 = pre_norm_w[0:1], pre_norm_w[1:2]
    post0, post1 = post_norm_w[0:1], post_norm_w[1:2]
    no_bias = jnp.zeros((1, D_MODEL), F32)

    h0, q, k, v, z = attn_in_proj(x0, pre0, attn_w_in, attn_b_in, tables)
    sink_tab = _sink_table(attn_sinks)
    late = (attn_w_out, rec_w_in, rec_w_out)
    og0, gathered = attn_fwd(q, k, v, z, sink_tab, batch, seq, gather=late if distributed else ())
    if distributed:
        attn_w_out, rec_w_in, rec_w_out = (_whole_from_shards(name, g) for name, g in zip(_LATE, gathered))
    y0, x1 = out_proj(og0, attn_w_out, attn_b_out, x0, post0)

    h1, proj1 = rec_in_proj(x1, pre1, rec_w_in)
    og1, states, safe = rec_fwd(proj1, rec_lb_logits, rec_gnorm_w, batch, seq)
    y1, dx2, loss_vec = out_proj(og1, rec_w_out, no_bias, x1, post1, target=loss_target.reshape(n, D_MODEL))

    dog1, d_rec_w_out, _, d_post1 = out_proj_bwd(dx2, y1, og1, rec_w_out, post1)
    dproj1, d_lb, d_gnorm = rec_bwd(proj1, states, safe, rec_lb_logits, rec_gnorm_w, dog1, batch, seq)
    dx1, d_pre1, _ = in_proj_bwd_x(dproj1, rec_w_in, x1, pre1, dx2)
    d_rec_w_in, _ = in_proj_bwd_w(h1, dproj1)

    dog0, d_attn_w_out, d_attn_b_out, d_post0 = out_proj_bwd(dx1, y0, og0, attn_w_out, post0)
    ready = dict(rec_w_in=d_rec_w_in, rec_w_out=d_rec_w_out, attn_w_out=d_attn_w_out)
    outgoing = [_shards_from_whole(name, ready[name]).astype(BF16) for name in _CARRIED] if distributed else []
    dproj0, dk, dv, d_sink_tab, arrived = attn_bwd(q, k, v, z, sink_tab, dog0, tables, batch, seq, scatter=outgoing)
    d_sinks = jnp.transpose(jnp.sum(d_sink_tab, axis=-1), (0, 2, 1)).reshape(1, N_HEADS)
    dproj0 = attn_bwd_kv(dproj0, dk, dv, tables)
    d_attn_w_in, d_attn_b_in = in_proj_bwd_w(h0, dproj0)
    last = [_shards_from_whole("attn_w_in", d_attn_w_in).astype(BF16)] if distributed else []
    dx0, d_pre0, arrived_last = in_proj_bwd_x(dproj0, attn_w_in, x0, pre0, dx1, scatter=last)

    grads = dict(
        pre_norm_w=jnp.concatenate([d_pre0, d_pre1], axis=0), post_norm_w=jnp.concatenate([d_post0, d_post1], axis=0),
        attn_w_in=d_attn_w_in, attn_b_in=d_attn_b_in, attn_sinks=d_sinks, attn_w_out=d_attn_w_out,
        attn_b_out=d_attn_b_out, rec_w_in=d_rec_w_in, rec_lb_logits=d_lb, rec_gnorm_w=d_gnorm,
        rec_w_out=d_rec_w_out)
    parts = dict(zip(_CARRIED + ("attn_w_in",), tuple(arrived) + tuple(arrived_last)))
    return loss_vec, dx0.reshape(batch, seq, D_MODEL), grads, parts


_BIG = ("attn_w_in", "attn_w_out", "rec_w_in", "rec_w_out")
_COLUMN_SHARDED = ("attn_w_in", "rec_w_in")
_ORDER = ("pre_norm_w", "post_norm_w", "attn_w_in", "attn_b_in", "attn_sinks", "attn_w_out", "attn_b_out",
          "rec_w_in", "rec_lb_logits", "rec_gnorm_w", "rec_w_out")


def _whole_from_shards(name, stacked):
    if name in _COLUMN_SHARDED:
        return jnp.transpose(stacked, (1, 0, 2)).reshape(stacked.shape[1], -1)
    return stacked.reshape(-1, stacked.shape[2])


def _shards_from_whole(name, whole):
    if name in _COLUMN_SHARDED:
        return jnp.transpose(whole.reshape(whole.shape[0], N_CHIPS, -1), (1, 0, 2))
    return whole.reshape(N_CHIPS, -1, whole.shape[1])


def kernel(x, positions, pre_norm_w, post_norm_w, attn_w_in, attn_b_in, attn_sinks, attn_w_out, attn_b_out, rec_w_in, rec_lb_logits, rec_gnorm_w, rec_w_out, loss_target, m_pre_norm_w, m_post_norm_w, m_attn_w_in, m_attn_b_in, m_attn_sinks, m_attn_w_out, m_attn_b_out, m_rec_w_in, m_rec_lb_logits, m_rec_gnorm_w, m_rec_w_out, v_pre_norm_w, v_post_norm_w, v_attn_w_in, v_attn_b_in, v_attn_sinks, v_attn_w_out, v_attn_b_out, v_rec_w_in, v_rec_lb_logits, v_rec_gnorm_w, v_rec_w_out):
    w = dict(pre_norm_w=pre_norm_w, post_norm_w=post_norm_w, attn_w_in=attn_w_in, attn_b_in=attn_b_in,
             attn_sinks=attn_sinks, attn_w_out=attn_w_out, attn_b_out=attn_b_out, rec_w_in=rec_w_in,
             rec_lb_logits=rec_lb_logits, rec_gnorm_w=rec_gnorm_w, rec_w_out=rec_w_out)
    m = dict(pre_norm_w=m_pre_norm_w, post_norm_w=m_post_norm_w, attn_w_in=m_attn_w_in, attn_b_in=m_attn_b_in,
             attn_sinks=m_attn_sinks, attn_w_out=m_attn_w_out, attn_b_out=m_attn_b_out, rec_w_in=m_rec_w_in,
             rec_lb_logits=m_rec_lb_logits, rec_gnorm_w=m_rec_gnorm_w, rec_w_out=m_rec_w_out)
    v = dict(pre_norm_w=v_pre_norm_w, post_norm_w=v_post_norm_w, attn_w_in=v_attn_w_in, attn_b_in=v_attn_b_in,
             attn_sinks=v_attn_sinks, attn_w_out=v_attn_w_out, attn_b_out=v_attn_b_out, rec_w_in=v_rec_w_in,
             rec_lb_logits=v_rec_lb_logits, rec_gnorm_w=v_rec_gnorm_w, rec_w_out=v_rec_w_out)

    shards = {name: w[name][0] for name in _BIG}
    sent = {name: shards[name].astype(BF16) for name in _BIG}
    attn_w_in_whole = _whole_from_shards("attn_w_in", chip_gather([sent["attn_w_in"]])[0])

    loss_vec, grad_x, grads, parts = local_step(
        x, positions, pre_norm_w, post_norm_w, attn_w_in_whole, attn_b_in, attn_sinks, sent["attn_w_out"],
        attn_b_out, sent["rec_w_in"], rec_lb_logits, rec_gnorm_w, sent["rec_w_out"], loss_target, distributed=True)

    plane_sums = [sum_slots(parts[name]) for name in _BIG]
    other_sums = sibling_exchange(plane_sums)
    out_g, out_d, out_m, out_v = {}, {}, {}, {}
    for name, mine, other in zip(_BIG, plane_sums, other_sums):
        g, d, nm, nv = adamw(shards[name], m[name][0], v[name][0], mine, other)
        out_g[name], out_d[name], out_m[name], out_v[name] = g[None], d[None], nm[None], nv[None]

    small_sum = sum_slots(all_gather_small(_pack_small(grads, last_row=loss_vec)))
    loss = jnp.sum(small_sum[_SMALL_ROWS - 1]) * (0.5 / D_MODEL)
    packed = adamw(_pack_small(w), _pack_small(m), _pack_small(v), small_sum)
    for dst, val in zip((out_g, out_d, out_m, out_v), packed):
        dst.update(_unpack_small(val))

    return (loss, grad_x, *[out_g[n] for n in _ORDER], *[out_d[n] for n in _ORDER],
            *[out_m[n] for n in _ORDER], *[out_v[n] for n in _ORDER])
```

```python
import functools

import jax
import jax.numpy as jnp
from jax import lax
from jax.experimental import pallas as pl
from jax.experimental.pallas import tpu as pltpu

F32 = jnp.float32
BF16 = jnp.bfloat16
MESH = pl.DeviceIdType.MESH

D_MODEL = 1024
HEAD_DIM = 64
N_HEADS = 16
N_KV_HEADS = 2
GROUP = N_HEADS // N_KV_HEADS
KV_WIDTH = N_KV_HEADS * HEAD_DIM
ATTN_IN = 2 * D_MODEL + 2 * KV_WIDTH
ATTN_BLOCK = 128
ROPE_THETA = 500000.0
ROPE_DIM = HEAD_DIM // 4
REC_HEADS = 8
REC_DIM = 128
REC_IN = 4 * D_MODEL
REC_BLOCK = 128
DIAG = 8
NORM_EPS = 1e-6
N_CHIPS = 4
N_DEV = 8
LANES = 128

ADAM_LR = 0.001
ADAM_B1 = 0.9
ADAM_B2 = 0.999
ADAM_EPS = 1e-08
ADAM_WD = 0.01
ADAM_STEP = 10

VMEM_LIMIT = 56 * 1024 * 1024


def _cparams(n_axes):
    return pltpu.CompilerParams(dimension_semantics=("arbitrary",) * n_axes, vmem_limit_bytes=VMEM_LIMIT)


def _dot(a, b, contract):
    return lax.dot_general(a.astype(BF16), b.astype(BF16), (contract, ((), ())), preferred_element_type=F32)


_NN = ((1,), (0,))
_NT = ((1,), (1,))
_TN = ((0,), (0,))


@jax.custom_vjp
def mm_nn(a, b):
    return _dot(a, b, _NN)


mm_nn.defvjp(lambda a, b: (_dot(a, b, _NN), (a, b)),
             lambda res, g: (_dot(g, res[1], _NT), _dot(res[0], g, _TN)))


@jax.custom_vjp
def mm_nt(a, b):
    return _dot(a, b, _NT)


mm_nt.defvjp(lambda a, b: (_dot(a, b, _NT), (a, b)),
             lambda res, g: (_dot(g, res[1], _NN), _dot(g, res[0], _TN)))


@jax.custom_vjp
def mm_tn(a, b):
    return _dot(a, b, _TN)


mm_tn.defvjp(lambda a, b: (_dot(a, b, _TN), (a, b)),
             lambda res, g: (_dot(res[1], g, _NT), _dot(res[0], g, _NN)))


def _tri_dot(x, lower):
    n = x.shape[0]
    r = lax.broadcasted_iota(jnp.int32, (n, n), 0)
    c = lax.broadcasted_iota(jnp.int32, (n, n), 1)
    tri = ((c <= r) if lower else (c >= r)).astype(BF16)
    hi = x.astype(BF16)
    rest = x - hi.astype(F32)
    mid = rest.astype(BF16)
    lo = (rest - mid.astype(F32)).astype(BF16)
    dot = lambda p: lax.dot_general(tri, p, (_NN, ((), ())), preferred_element_type=F32)
    return (dot(lo) + dot(mid)) + dot(hi)


@jax.custom_vjp
def cumsum_rows(x):
    return _tri_dot(x, True)


cumsum_rows.defvjp(lambda x: (cumsum_rows(x), None), lambda _, g: (_tri_dot(g, False),))


@functools.partial(jax.custom_vjp, nondiff_argnums=(1,))
def roll_sub(x, d):
    return pltpu.roll(x, d, 1) if d else x


roll_sub.defvjp(lambda x, d: (roll_sub(x, d), None),
                lambda d, _, g: (roll_sub(g, (DIAG - d) % DIAG),))


def sigmoid(x):
    return 1.0 / (1.0 + jnp.exp(-x))


@jax.custom_vjp
def silu(x):
    return x * sigmoid(x)


def _silu_fwd(x):
    s = sigmoid(x)
    return x * s, (x, s)


silu.defvjp(_silu_fwd, lambda res, g: (g * (res[1] * (1.0 + res[0] * (1.0 - res[1]))),))


F32_TINY = 1.17549435e-38


def sigmoid_pair(x):
    e = jnp.exp(-jnp.abs(x))
    r = 1.0 / (1.0 + e)
    er = e * r
    pos = x >= 0.0
    return jnp.where(pos, r, er), jnp.where(pos, er, r)


def _forget_fwd(x, a):
    lb, one_m_lb = sigmoid_pair(a)
    sp, sn = sigmoid_pair(x)
    f = lb + one_m_lb * sp
    k = one_m_lb * sn
    return (jnp.log(jnp.maximum(f, F32_TINY)), k), (sp, sn, f, k, lb, one_m_lb)


def _forget_bwd(res, g):
    sp, sn, f, k, lb, one_m_lb = res
    g_lf, g_k = g
    t = jnp.where(f >= F32_TINY, g_lf / jnp.maximum(f, F32_TINY), 0.0) - g_k
    return (k * sp) * t, jnp.sum(sn * t, axis=0, keepdims=True) * (lb * one_m_lb)


@jax.custom_vjp
def forget_gate(x, a):
    return _forget_fwd(x, a)[0]


forget_gate.defvjp(_forget_fwd, _forget_bwd)


@jax.custom_vjp
def decayed(x, e):
    return (x * jnp.exp(e)).astype(BF16).astype(F32)


def _decayed_fwd(x, e):
    y = decayed(x, e)
    return y, (y, e)


decayed.defvjp(_decayed_fwd, lambda res, g: (g * jnp.exp(res[1]), g * res[0]))


def _row(x, r):
    shape = x.shape

    @jax.custom_vjp
    def take(x):
        return x[r:r + 1, :]

    take.defvjp(lambda x: (x[r:r + 1, :], None),
                lambda _, g: (jnp.where(lax.broadcasted_iota(jnp.int32, shape, 0) == r, g, 0.0),))
    return take(x)


def _rms(x):
    return lax.rsqrt(jnp.mean(x * x, axis=-1, keepdims=True) + NORM_EPS)


def _attn_group(qs, k_a, v_a, k_b, v_b, zs, sink_a, sink_b, bias):
    def half(kh, vh, sink):
        s = mm_nn(qs, kh) + bias
        m = lax.stop_gradient(jnp.maximum(jnp.max(s, axis=-1, keepdims=True), jnp.max(sink, axis=-1, keepdims=True)))
        p = jnp.exp(s - m)
        denom = jnp.sum(p, axis=-1, keepdims=True) + jnp.sum(jnp.exp(sink - m), axis=-1, keepdims=True) * (1.0 / LANES)
        return mm_nt(p * (1.0 / denom), vh)

    return (half(k_a, v_a, sink_a) + half(k_b, v_b, sink_b)) * silu(zs)


SAFE_RANGE = 80.0


def _rec_front(qr, fr, l0, l1):
    lf, k = forget_gate(fr, l1 - l0)
    return silu(qr), k, lf


def _rec_tail(o, z, gw):
    return o * _rms(o) * gw * silu(z)


def _rec_margin(b):
    R = b.shape[0]
    mid, last = _row(b, R // 2 - 1), _row(b, R - 1)
    return jnp.minimum(mid, last - mid)


def _heads(x):
    w = x.shape[1] // REC_HEADS
    return [x[:, h * w:(h + 1) * w] for h in range(REC_HEADS)]


def _hdot(a, b, contract):
    return jnp.concatenate([_dot(ah, bh, contract) for ah, bh in zip(_heads(a), _heads(b))], axis=1)


@jax.custom_vjp
def hmm_nn(a, b):
    return _hdot(a, b, _NN)


hmm_nn.defvjp(lambda a, b: (_hdot(a, b, _NN), (a, b)),
              lambda res, g: (_hdot(g, res[1], _NT), _hdot(res[0], g, _TN)))


@jax.custom_vjp
def hmm_nt(a, b):
    return _hdot(a, b, _NT)


hmm_nt.defvjp(lambda a, b: (_hdot(a, b, _NT), (a, b)),
              lambda res, g: (_hdot(g, res[1], _NN), _hdot(g, res[0], _TN)))


@jax.custom_vjp
def hmm_tn(a, b):
    return _hdot(a, b, _TN)


hmm_tn.defvjp(lambda a, b: (_hdot(a, b, _TN), (a, b)),
              lambda res, g: (_hdot(res[1], g, _NT), _hdot(res[0], g, _NN)))


def _head_sums(x):
    return jnp.concatenate([jnp.broadcast_to(jnp.sum(xh, axis=-1, keepdims=True), xh.shape) for xh in _heads(x)],
                           axis=1)


@jax.custom_vjp
def head_sum(x):
    return _head_sums(x)


head_sum.defvjp(lambda x: (_head_sums(x), None), lambda _, g: (_head_sums(g),))


def _rec_cores_fast(q, k, v, b, S):
    R = q.shape[0]
    ri = lax.broadcasted_iota(jnp.int32, (R, REC_HEADS * R), 0)
    ci = lax.broadcasted_iota(jnp.int32, (R, REC_HEADS * R), 1) % R
    d = b - _row(b, R // 2 - 1)
    sc = jnp.where(ci < ri, hmm_nt(decayed(q, d), decayed(k, -d)), 0.0)
    o = hmm_nt(q * jnp.exp(b), S) + hmm_nn(sc, v) + head_sum(q * k) * v
    b_last = _row(b, R - 1)
    return o, S * jnp.exp(b_last) + hmm_tn(v, k * jnp.exp(b_last - b))


def _rec_tails(o, z, gw):
    return o * lax.rsqrt(head_sum(o * o) * (1.0 / REC_DIM) + NORM_EPS) * gw * silu(z)


def _rec_block_fast(qr, fr, v, z, S, l0, l1, gw):
    lf, k = forget_gate(fr, l1 - l0)
    o, S_new = _rec_cores_fast(silu(qr), k, v, cumsum_rows(lf), S)
    return _rec_tails(o, z, gw), S_new


def _rec_core_slow(q, k, v, b, S):
    R = q.shape[0]
    rows = lax.broadcasted_iota(jnp.int32, (R, REC_DIM), 0)

    o = mm_nt(q * jnp.exp(jnp.minimum(b, 0.0)), S)

    ri = lax.broadcasted_iota(jnp.int32, (R, R), 0)
    ci = lax.broadcasted_iota(jnp.int32, (R, R), 1)
    sc = jnp.zeros((R, R), F32)
    w = R
    while w > DIAG:
        h = w // 2
        b3 = b.reshape(R // w, w, REC_DIM)
        rin = lax.broadcasted_iota(jnp.int32, (R // w, w, REC_DIM), 1)
        mid = jnp.sum(jnp.where(rin == h - 1, b3, 0.0), axis=1, keepdims=True)
        fac = jnp.exp(jnp.minimum(jnp.where(rin >= h, b3 - mid, mid - b3), 0.0)).reshape(R, REC_DIM)
        upper = (rows % w) >= h
        s_w = mm_nt(jnp.where(upper, q * fac, 0.0), jnp.where(upper, 0.0, k * fac))
        sc = sc + jnp.where((ri // w) == (ci // w), s_w, 0.0)
        w = h
    o = o + mm_nn(sc, v)

    g = R // DIAG
    q3, k3, v3, b3 = (t.reshape(g, DIAG, REC_DIM) for t in (q, k, v, b))
    rin = lax.broadcasted_iota(jnp.int32, (g, DIAG, 1), 1)
    od = jnp.zeros((g, DIAG, REC_DIM), F32)
    for d in range(DIAG):
        e = jnp.exp(jnp.minimum(b3 - roll_sub(b3, d), 0.0))
        sd = jnp.sum(q3 * roll_sub(k3, d) * e, axis=-1, keepdims=True)
        od = od + jnp.where(rin >= d, sd, 0.0) * roll_sub(v3, d)
    o = o + od.reshape(R, REC_DIM)

    b_last = _row(b, R - 1)
    return o, S * jnp.exp(jnp.minimum(b_last, 0.0)) + mm_tn(v, k * jnp.exp(jnp.minimum(b_last - b, 0.0)))


def _rec_head(core, qr, fr, v, z, S, l0, l1, gw):
    q, k, lf = _rec_front(qr, fr, l0, l1)
    o, S_new = core(q, k, v, cumsum_rows(lf), S)
    return _rec_tail(o, z, gw), S_new


def _rope_tables(positions):
    half = ROPE_DIM // 2
    inv_freq = ROPE_THETA ** (-(jnp.arange(half, dtype=F32) * 2.0 / ROPE_DIM))
    rest = jnp.zeros((HEAD_DIM - ROPE_DIM,), F32)
    ones, zeros = jnp.ones((half,), F32), jnp.zeros((half,), F32)
    per_lane = lambda first, second: jnp.tile(jnp.concatenate([first, second, rest]), LANES // HEAD_DIM)[None, :]
    ang = positions.astype(F32).reshape(-1, 1) * per_lane(inv_freq, inv_freq)
    sin = jnp.sin(ang)
    return jnp.cos(ang), sin * per_lane(zeros, ones), sin * per_lane(-ones, zeros)


def _rope(x, cos_t, sin_a, sin_b):
    half = ROPE_DIM // 2
    return x * cos_t + pltpu.roll(x, half, 1) * sin_a + pltpu.roll(x, LANES - half, 1) * sin_b


def _rope_transposed(g, cos_t, sin_a, sin_b):
    half = ROPE_DIM // 2
    return g * cos_t + pltpu.roll(g * sin_a, LANES - half, 1) + pltpu.roll(g * sin_b, half, 1)


def _row_spec(tm, width):
    return pl.BlockSpec((tm, width), lambda i: (i, 0))


def _full_spec(shape):
    return pl.BlockSpec(shape, lambda *_: (0,) * len(shape))


def attn_in_proj(x, w_pre, w_in, b_in, tables, tm=1024):
    n = x.shape[0]
    tm = min(tm, n)

    def body(x_ref, wp_ref, w_ref, b_ref, c_ref, sa_ref, sb_ref, h_ref, q_ref, k_ref, v_ref, z_ref):
        xv = x_ref[...]
        h = (xv * _rms(xv) * wp_ref[...]).astype(BF16)
        h_ref[...] = h
        proj = jnp.dot(h, w_ref[...], preferred_element_type=F32) + b_ref[...]
        tabs = (c_ref[...], sa_ref[...], sb_ref[...])
        for s in range(D_MODEL // LANES):
            sl = slice(s * LANES, (s + 1) * LANES)
            q_ref[:, sl] = _rope(proj[:, sl] * (HEAD_DIM ** -0.5), *tabs).astype(BF16)
        k_ref[...] = _rope(proj[:, D_MODEL:D_MODEL + KV_WIDTH], *tabs).astype(BF16)
        v_ref[...] = proj[:, D_MODEL + KV_WIDTH:D_MODEL + 2 * KV_WIDTH].astype(BF16)
        z_ref[...] = proj[:, D_MODEL + 2 * KV_WIDTH:]

    return pl.pallas_call(
        body, name="attn_in_proj", grid=(n // tm,),
        in_specs=[_row_spec(tm, D_MODEL), _full_spec((1, D_MODEL)), _full_spec((D_MODEL, ATTN_IN)),
                  _full_spec((1, ATTN_IN))] + [_row_spec(tm, LANES)] * 3,
        out_specs=[_row_spec(tm, D_MODEL), _row_spec(tm, D_MODEL), _row_spec(tm, KV_WIDTH),
                   _row_spec(tm, KV_WIDTH), _row_spec(tm, D_MODEL)],
        out_shape=[jax.ShapeDtypeStruct((n, D_MODEL), BF16), jax.ShapeDtypeStruct((n, D_MODEL), BF16),
                   jax.ShapeDtypeStruct((n, KV_WIDTH), BF16), jax.ShapeDtypeStruct((n, KV_WIDTH), BF16),
                   jax.ShapeDtypeStruct((n, D_MODEL), F32)],
        compiler_params=_cparams(1),
    )(x, w_pre, w_in, b_in, *tables)


def rec_in_proj(x, w_pre, w_in, tm=512):
    n = x.shape[0]
    tm = min(tm, n)

    def body(x_ref, wp_ref, w_ref, h_ref, p_ref):
        xv = x_ref[...]
        h = (xv * _rms(xv) * wp_ref[...]).astype(BF16)
        h_ref[...] = h
        p_ref[...] = jnp.dot(h, w_ref[...], preferred_element_type=F32)

    return pl.pallas_call(
        body, name="rec_in_proj", grid=(n // tm,),
        in_specs=[_row_spec(tm, D_MODEL), _full_spec((1, D_MODEL)), _full_spec((D_MODEL, REC_IN))],
        out_specs=[_row_spec(tm, D_MODEL), _row_spec(tm, REC_IN)],
        out_shape=[jax.ShapeDtypeStruct((n, D_MODEL), BF16), jax.ShapeDtypeStruct((n, REC_IN), F32)],
        compiler_params=_cparams(1),
    )(x, w_pre, w_in)


def out_proj(og, w_out, b_out, x_res, w_post, target=None, tm=1024):
    n = og.shape[0]
    tm = min(tm, n)
    with_loss = target is not None

    def body(*refs):
        if with_loss:
            og_ref, w_ref, b_ref, x_ref, wp_ref, t_ref, y_ref, dx_ref, l_ref = refs
        else:
            og_ref, w_ref, b_ref, x_ref, wp_ref, y_ref, xo_ref = refs
        y = jnp.dot(og_ref[...], w_ref[...], preferred_element_type=F32) + b_ref[...]
        y_ref[...] = y.astype(BF16)
        xo = x_ref[...] + y * _rms(y) * wp_ref[...]
        if with_loss:
            err = xo - t_ref[...]
            dx_ref[...] = err * (1.0 / D_MODEL)

            @pl.when(pl.program_id(0) == 0)
            def _():
                l_ref[...] = jnp.zeros_like(l_ref)

            l_ref[...] += jnp.sum(err * err, axis=0, keepdims=True)
        else:
            xo_ref[...] = xo

    in_specs = [_row_spec(tm, D_MODEL), _full_spec((D_MODEL, D_MODEL)), _full_spec((1, D_MODEL)),
                _row_spec(tm, D_MODEL), _full_spec((1, D_MODEL))]
    out_specs = [_row_spec(tm, D_MODEL), _row_spec(tm, D_MODEL)]
    out_shape = [jax.ShapeDtypeStruct((n, D_MODEL), BF16), jax.ShapeDtypeStruct((n, D_MODEL), F32)]
    args = [og, w_out, b_out, x_res, w_post]
    if with_loss:
        in_specs.append(_row_spec(tm, D_MODEL))
        out_specs.append(_full_spec((1, D_MODEL)))
        out_shape.append(jax.ShapeDtypeStruct((1, D_MODEL), F32))
        args.append(target)
    return pl.pallas_call(
        body, name="out_proj_loss" if with_loss else "out_proj", grid=(n // tm,),
        in_specs=in_specs, out_specs=out_specs, out_shape=out_shape, compiler_params=_cparams(1),
    )(*args)


def out_proj_bwd(dxo, y, og, w_out, w_post, tm=1024):
    n = og.shape[0]
    tm = min(tm, n)

    def body(g_ref, y_ref, og_ref, w_ref, wp_ref, dog_ref, dw_ref, db_ref, dwp_ref):
        @pl.when(pl.program_id(0) == 0)
        def _():
            dw_ref[...] = jnp.zeros_like(dw_ref)
            db_ref[...] = jnp.zeros_like(db_ref)
            dwp_ref[...] = jnp.zeros_like(dwp_ref)

        g, y = g_ref[...], y_ref[...].astype(F32)
        rstd = _rms(y)
        yn = y * rstd
        gw = g * wp_ref[...]
        dwp_ref[...] += jnp.sum(g * yn, axis=0, keepdims=True)
        dy = rstd * (gw - yn * jnp.mean(gw * yn, axis=-1, keepdims=True))
        db_ref[...] += jnp.sum(dy, axis=0, keepdims=True)
        dyb = dy.astype(BF16)
        dog_ref[...] = _dot(dyb, w_ref[...], _NT).astype(BF16)
        dw_ref[...] += _dot(og_ref[...], dyb, _TN)

    return pl.pallas_call(
        body, name="out_proj_bwd", grid=(n // tm,),
        in_specs=[_row_spec(tm, D_MODEL), _row_spec(tm, D_MODEL), _row_spec(tm, D_MODEL),
                  _full_spec((D_MODEL, D_MODEL)), _full_spec((1, D_MODEL))],
        out_specs=[_row_spec(tm, D_MODEL), _full_spec((D_MODEL, D_MODEL)), _full_spec((1, D_MODEL)),
                   _full_spec((1, D_MODEL))],
        out_shape=[jax.ShapeDtypeStruct((n, D_MODEL), BF16), jax.ShapeDtypeStruct((D_MODEL, D_MODEL), F32),
                   jax.ShapeDtypeStruct((1, D_MODEL), F32), jax.ShapeDtypeStruct((1, D_MODEL), F32)],
        compiler_params=_cparams(1),
    )(dxo, y, og, w_out, w_post)


def in_proj_bwd_x(dproj, w_in, x, w_pre, dxo, tm=512, scatter=()):
    n, p = dproj.shape
    tm = min(tm if p > ATTN_IN else 2 * tm, n)
    steps = n // tm
    ns = len(scatter)

    def body(*refs):
        dp_ref, w_ref, x_ref, wp_ref, g_ref = refs[:5]
        dx_ref, dwp_ref = refs[5 + ns:7 + ns]
        exchange = (refs[5:5 + ns], refs[7 + ns:7 + 2 * ns]) + tuple(refs[7 + 2 * ns:])

        @pl.when(pl.program_id(0) == 0)
        def _():
            dwp_ref[...] = jnp.zeros_like(dwp_ref)
            if ns:
                _scatter_start(*exchange)

        dh = _dot(dp_ref[...], w_ref[...], _NT)
        xv = x_ref[...]
        rstd = _rms(xv)
        xn = xv * rstd
        gw = dh * wp_ref[...]
        dwp_ref[...] += jnp.sum(dh * xn, axis=0, keepdims=True)
        dx_ref[...] = rstd * (gw - xn * jnp.mean(gw * xn, axis=-1, keepdims=True)) + g_ref[...]

        if ns:
            @pl.when(pl.program_id(0) == steps - 1)
            def _():
                _scatter_finish(*exchange)

    out = pl.pallas_call(
        body, name=f"in_proj_bwd_x_{p}", grid=(steps,),
        in_specs=[_row_spec(tm, p), _full_spec((D_MODEL, p)), _row_spec(tm, D_MODEL), _full_spec((1, D_MODEL)),
                  _row_spec(tm, D_MODEL)] + [_ANY] * ns,
        out_specs=[_row_spec(tm, D_MODEL), _full_spec((1, D_MODEL))] + [_ANY] * ns,
        out_shape=[jax.ShapeDtypeStruct((n, D_MODEL), F32), jax.ShapeDtypeStruct((1, D_MODEL), F32)]
        + [jax.ShapeDtypeStruct(a.shape, a.dtype) for a in scatter],
        scratch_shapes=_scatter_sems(ns) if ns else [],
        compiler_params=_cparams(1),
    )(dproj, w_in, x, w_pre, dxo, *scatter)
    return out[0], out[1], out[2:]


def in_proj_bwd_w(h, dproj, tm=1024):
    n, p = dproj.shape
    chunk = p // (4 if p % 4096 == 0 else 3)
    tm = min(tm, n)
    steps = n // tm

    def body(h_ref, dp_ref, dw_ref, db_ref, acc_scr, sem):
        i = pl.program_id(0)

        @pl.when(i == 0)
        def _():
            acc_scr[...] = jnp.zeros_like(acc_scr)
            db_ref[...] = jnp.zeros_like(db_ref)

        ht = h_ref[...].T
        for c0 in range(0, p, chunk):
            dp = dp_ref[:, c0:c0 + chunk]
            acc_scr[:, c0:c0 + chunk] += jnp.dot(ht, dp, preferred_element_type=F32)
            db_ref[:, c0:c0 + chunk] += jnp.sum(dp.astype(F32), axis=0, keepdims=True)

        @pl.when(i == steps - 1)
        def _():
            out = pltpu.make_async_copy(acc_scr, dw_ref, sem)
            out.start()
            out.wait()

    return pl.pallas_call(
        body, name=f"in_proj_bwd_w_{p}", grid=(steps,),
        in_specs=[_row_spec(tm, D_MODEL), _row_spec(tm, p)],
        out_specs=[_ANY, _full_spec((1, p))],
        out_shape=[jax.ShapeDtypeStruct((D_MODEL, p), F32), jax.ShapeDtypeStruct((1, p), F32)],
        scratch_shapes=[pltpu.VMEM((D_MODEL, p), F32), pltpu.SemaphoreType.DMA],
        compiler_params=_cparams(1),
    )(h, dproj)


PAIRS = GROUP // 2
GROUP_ROWS = PAIRS * ATTN_BLOCK
MASKED = -1e30


def _kv_windows(k_ref, v_ref, i):
    ps = pl.multiple_of(jnp.maximum(i - 1, 0) * ATTN_BLOCK, ATTN_BLOCK)
    cs = pl.multiple_of(i * ATTN_BLOCK, ATTN_BLOCK)
    kw = jnp.concatenate([k_ref[pl.ds(ps, ATTN_BLOCK), :], k_ref[pl.ds(cs, ATTN_BLOCK), :]], axis=0)
    vw = jnp.concatenate([v_ref[pl.ds(ps, ATTN_BLOCK), :], v_ref[pl.ds(cs, ATTN_BLOCK), :]], axis=0)
    return kw.astype(F32).T, vw.astype(F32).T, ps, cs


def _low_rows(shape):
    return lax.broadcasted_iota(jnp.int32, shape, 0) < HEAD_DIM


def _spread(w, kvh):
    low = _low_rows(w.shape)
    swapped = pltpu.roll(w, HEAD_DIM, 0)
    if kvh == 0:
        return jnp.where(low, w, 0.0), jnp.where(low, 0.0, swapped)
    return jnp.where(low, swapped, 0.0), jnp.where(low, 0.0, w)


def _unspread(d_a, d_b, kvh):
    low = _low_rows(d_a.shape)
    if kvh == 0:
        return jnp.where(low, d_a + pltpu.roll(d_b, HEAD_DIM, 0), 0.0)
    return jnp.where(low, 0.0, pltpu.roll(d_a, HEAD_DIM, 0) + d_b)


def _stack_pairs(ref, kvh):
    return jnp.concatenate([ref[:, (kvh * PAIRS + j) * LANES:(kvh * PAIRS + j + 1) * LANES] for j in range(PAIRS)],
                           axis=0)


def _fill_bias(bias_scr):
    shape = (GROUP_ROWS, 2 * ATTN_BLOCK)
    r = lax.broadcasted_iota(jnp.int32, shape, 0) % ATTN_BLOCK
    c = lax.broadcasted_iota(jnp.int32, shape, 1)
    in_cur = (c >= ATTN_BLOCK) & ((c - ATTN_BLOCK) <= r)
    in_prev = (c < ATTN_BLOCK) & (c > r)
    bias_scr[0] = jnp.where(in_cur, 0.0, MASKED)
    bias_scr[1] = jnp.where(in_cur | in_prev, 0.0, MASKED)


def _sink_table(sinks):
    t = jnp.transpose(sinks.reshape(N_KV_HEADS, PAIRS, 2), (0, 2, 1))
    return jnp.broadcast_to(t[:, :, :, None, None], (N_KV_HEADS, 2, PAIRS, ATTN_BLOCK, LANES)).reshape(
        N_KV_HEADS, 2, GROUP_ROWS, LANES)


def attn_fwd(q, k, v, z, sink_tab, batch, seq, gather=()):
    nb = seq // ATTN_BLOCK
    ng = len(gather)

    def body(*refs):
        q_ref, k_ref, v_ref, z_ref, s_ref = refs[:5]
        og_ref, bias_scr = refs[5 + ng], refs[6 + 2 * ng]
        exchange = (refs[5:5 + ng], refs[6 + ng:6 + 2 * ng]) + tuple(refs[7 + 2 * ng:])
        b, i = pl.program_id(0), pl.program_id(1)

        @pl.when((b == 0) & (i == 0))
        def _():
            _fill_bias(bias_scr)
            if ng:
                _gather_start(*exchange)

        kw, vw, _, _ = _kv_windows(k_ref, v_ref, i)
        bias = bias_scr[jnp.minimum(i, 1)]
        for kvh in range(N_KV_HEADS):
            k_a, k_b = _spread(kw, kvh)
            v_a, v_b = _spread(vw, kvh)
            og = _attn_group(_stack_pairs(q_ref, kvh), k_a, v_a, k_b, v_b, _stack_pairs(z_ref, kvh),
                             s_ref[kvh, 0], s_ref[kvh, 1], bias)
            for j in range(PAIRS):
                og_ref[:, (kvh * PAIRS + j) * LANES:(kvh * PAIRS + j + 1) * LANES] = (
                    og[j * ATTN_BLOCK:(j + 1) * ATTN_BLOCK].astype(BF16))

        if ng:
            @pl.when((b == batch - 1) & (i == nb - 1))
            def _():
                _gather_finish(*exchange)

    blk = lambda w: pl.BlockSpec((ATTN_BLOCK, w), lambda b, i: (b * nb + i, 0))
    seq_spec = pl.BlockSpec((seq, KV_WIDTH), lambda b, i: (b, 0))
    out = pl.pallas_call(
        body, name="attn_fwd", grid=(batch, nb),
        in_specs=[blk(D_MODEL), seq_spec, seq_spec, blk(D_MODEL), _full_spec(sink_tab.shape)] + [_ANY] * ng,
        out_specs=[blk(D_MODEL)] + [_ANY] * ng,
        out_shape=[jax.ShapeDtypeStruct((batch * seq, D_MODEL), BF16)]
        + [jax.ShapeDtypeStruct((N_CHIPS,) + a.shape, a.dtype) for a in gather],
        scratch_shapes=[pltpu.VMEM((2, GROUP_ROWS, 2 * ATTN_BLOCK), F32)] + (_gather_sems(ng) if ng else []),
        compiler_params=_cparams(2),
    )(q, k, v, z, sink_tab, *gather)
    return out[0], out[1:]


def attn_bwd(q, k, v, z, sink_tab, dog, tables, batch, seq, scatter=()):
    nb = seq // ATTN_BLOCK
    ns = len(scatter)

    def body(*refs):
        q_ref, k_ref, v_ref, z_ref, s_ref, g_ref, c_ref, sa_ref, sb_ref = refs[:9]
        dp_ref, dk_ref, dv_ref, ds_ref = refs[9 + ns:13 + ns]
        bias_scr = refs[13 + 2 * ns]
        exchange = (refs[9:9 + ns], refs[13 + ns:13 + 2 * ns]) + tuple(refs[14 + 2 * ns:])
        b, i = pl.program_id(0), pl.program_id(1)

        @pl.when((b == 0) & (i == 0))
        def _():
            _fill_bias(bias_scr)
            ds_ref[...] = jnp.zeros_like(ds_ref)
            if ns:
                _scatter_start(*exchange)

        @pl.when(i == 0)
        def _():
            dk_ref[...] = jnp.zeros_like(dk_ref)
            dv_ref[...] = jnp.zeros_like(dv_ref)

        kw, vw, ps, cs = _kv_windows(k_ref, v_ref, i)
        bias = bias_scr[jnp.minimum(i, 1)]
        tabs = (c_ref[...], sa_ref[...], sb_ref[...])
        dkw = jnp.zeros_like(kw)
        dvw = jnp.zeros_like(vw)
        for kvh in range(N_KV_HEADS):
            k_a, k_b = _spread(kw, kvh)
            v_a, v_b = _spread(vw, kvh)
            _, vjp = jax.vjp(functools.partial(_attn_group, bias=bias), _stack_pairs(q_ref, kvh).astype(F32),
                             k_a, v_a, k_b, v_b, _stack_pairs(z_ref, kvh), s_ref[kvh, 0], s_ref[kvh, 1])
            dqs, dk_a, dv_a, dk_b, dv_b, dzs, ds_a, ds_b = vjp(_stack_pairs(g_ref, kvh).astype(F32))
            dkw = dkw + _unspread(dk_a, dk_b, kvh)
            dvw = dvw + _unspread(dv_a, dv_b, kvh)
            ds_ref[kvh, 0] += jnp.sum(ds_a.reshape(PAIRS, ATTN_BLOCK, LANES), axis=1)
            ds_ref[kvh, 1] += jnp.sum(ds_b.reshape(PAIRS, ATTN_BLOCK, LANES), axis=1)
            for j in range(PAIRS):
                rows = slice(j * ATTN_BLOCK, (j + 1) * ATTN_BLOCK)
                col = (kvh * PAIRS + j) * LANES
                dp_ref[:, col:col + LANES] = _rope_transposed(dqs[rows] * (HEAD_DIM ** -0.5), *tabs).astype(BF16)
                zc = D_MODEL + 2 * KV_WIDTH + col
                dp_ref[:, zc:zc + LANES] = dzs[rows].astype(BF16)
        dp_ref[:, D_MODEL:D_MODEL + 2 * KV_WIDTH] = jnp.zeros((ATTN_BLOCK, 2 * KV_WIDTH), BF16)
        dk_ref[:, pl.ds(ps, ATTN_BLOCK)] += dkw[:, :ATTN_BLOCK]
        dk_ref[:, pl.ds(cs, ATTN_BLOCK)] += dkw[:, ATTN_BLOCK:]
        dv_ref[:, pl.ds(ps, ATTN_BLOCK)] += dvw[:, :ATTN_BLOCK]
        dv_ref[:, pl.ds(cs, ATTN_BLOCK)] += dvw[:, ATTN_BLOCK:]

        if ns:
            @pl.when((b == batch - 1) & (i == nb - 1))
            def _():
                _scatter_finish(*exchange)

    blk = lambda w: pl.BlockSpec((ATTN_BLOCK, w), lambda b, i: (b * nb + i, 0))
    seq_spec = pl.BlockSpec((seq, KV_WIDTH), lambda b, i: (b, 0))
    seq_spec_t = pl.BlockSpec((KV_WIDTH, seq), lambda b, i: (0, b))
    n = batch * seq
    ds_shape = (N_KV_HEADS, 2, PAIRS, LANES)
    out = pl.pallas_call(
        body, name="attn_bwd", grid=(batch, nb),
        in_specs=[blk(D_MODEL), seq_spec, seq_spec, blk(D_MODEL), _full_spec(sink_tab.shape), blk(D_MODEL)]
        + [blk(LANES)] * 3 + [_ANY] * ns,
        out_specs=[blk(ATTN_IN), seq_spec_t, seq_spec_t, _full_spec(ds_shape)] + [_ANY] * ns,
        out_shape=[jax.ShapeDtypeStruct((n, ATTN_IN), BF16), jax.ShapeDtypeStruct((KV_WIDTH, n), F32),
                   jax.ShapeDtypeStruct((KV_WIDTH, n), F32), jax.ShapeDtypeStruct(ds_shape, F32)]
        + [jax.ShapeDtypeStruct(a.shape, a.dtype) for a in scatter],
        scratch_shapes=[pltpu.VMEM((2, GROUP_ROWS, 2 * ATTN_BLOCK), F32)] + (_scatter_sems(ns) if ns else []),
        compiler_params=_cparams(2),
    )(q, k, v, z, sink_tab, dog, *tables, *scatter)
    return out[0], out[1], out[2], out[3], out[4:]


def attn_bwd_kv(dproj, dk_t, dv_t, tables, tm=512):
    n = dproj.shape[0]

    def body(dp_in_ref, dk_ref, dv_ref, c_ref, sa_ref, sb_ref, dp_ref):
        del dp_in_ref
        dp_ref[:, :KV_WIDTH] = _rope_transposed(dk_ref[...].T, c_ref[...], sa_ref[...], sb_ref[...]).astype(BF16)
        dp_ref[:, KV_WIDTH:] = dv_ref[...].T.astype(BF16)

    kv_cols = pl.BlockSpec((tm, 2 * KV_WIDTH), lambda i: (i, D_MODEL // (2 * KV_WIDTH)))
    col_spec = pl.BlockSpec((KV_WIDTH, tm), lambda i: (0, i))
    return pl.pallas_call(
        body, name="attn_bwd_kv", grid=(n // tm,),
        in_specs=[kv_cols, col_spec, col_spec] + [_row_spec(tm, LANES)] * 3,
        out_specs=kv_cols, out_shape=jax.ShapeDtypeStruct(dproj.shape, BF16),
        input_output_aliases={0: 0}, compiler_params=_cparams(1),
    )(dproj, dk_t, dv_t, *tables)


def rec_fwd(proj, lb_logits, gnorm_w, batch, seq):
    nblk = seq // REC_BLOCK

    def body(p_ref, lb_ref, gw_ref, og_ref, st_ref, safe_ref, s_scr):
        @pl.when(pl.program_id(1) == 0)
        def _():
            s_scr[...] = jnp.zeros_like(s_scr)

        S = s_scr[...]
        st_ref[0] = S
        qr, fr, v, z = (p_ref[:, part * D_MODEL:(part + 1) * D_MODEL] for part in range(4))
        lf, k = forget_gate(fr, lb_ref[1:2, :] - lb_ref[0:1, :])
        q, b = silu(qr), cumsum_rows(lf)
        safe = jnp.min(_rec_margin(b)) >= -SAFE_RANGE

        def head_by_head():
            outs = [_rec_core_slow(*args) for args in zip(*(_heads(t) for t in (q, k, v, b, S)))]
            return tuple(jnp.concatenate(parts, axis=1) for parts in zip(*outs))

        o, S_new = lax.cond(safe, lambda: _rec_cores_fast(q, k, v, b, S), head_by_head)
        og_ref[...] = _rec_tails(o, z, gw_ref[...]).astype(BF16)
        s_scr[...] = S_new
        safe_ref[0] = jnp.full((REC_HEADS, LANES), safe.astype(F32))

    blk = lambda w: pl.BlockSpec((REC_BLOCK, w), lambda b, j: (b * nblk + j, 0))
    st_spec = pl.BlockSpec((1, REC_DIM, D_MODEL), lambda b, j: (b * nblk + j, 0, 0))
    safe_spec = pl.BlockSpec((1, REC_HEADS, LANES), lambda b, j: (b * nblk + j, 0, 0))
    return pl.pallas_call(
        body, name="rec_fwd", grid=(batch, nblk),
        in_specs=[blk(REC_IN), _full_spec((2, D_MODEL)), _full_spec((1, D_MODEL))],
        out_specs=[blk(D_MODEL), st_spec, safe_spec],
        out_shape=[jax.ShapeDtypeStruct((batch * seq, D_MODEL), BF16),
                   jax.ShapeDtypeStruct((batch * nblk, REC_DIM, D_MODEL), F32),
                   jax.ShapeDtypeStruct((batch * nblk, REC_HEADS, LANES), F32)],
        scratch_shapes=[pltpu.VMEM((REC_DIM, D_MODEL), F32)],
        compiler_params=_cparams(2),
    )(proj, lb_logits, jnp.tile(gnorm_w, (1, REC_HEADS)))


def rec_bwd(proj, states, safe, lb_logits, gnorm_w, dog, batch, seq):
    nblk = seq // REC_BLOCK

    def body(p_ref, st_ref, safe_ref, lb_ref, gw_ref, g_ref, dp_ref, dlb_ref, dgw_ref, ds_scr):
        @pl.when((pl.program_id(0) == 0) & (pl.program_id(1) == 0))
        def _():
            dlb_ref[...] = jnp.zeros_like(dlb_ref)
            dgw_ref[...] = jnp.zeros_like(dgw_ref)

        @pl.when(pl.program_id(1) == 0)
        def _():
            ds_scr[...] = jnp.zeros_like(ds_scr)

        primals = tuple(p_ref[:, part * D_MODEL:(part + 1) * D_MODEL] for part in range(4)) + (
            st_ref[0], lb_ref[0:1, :], lb_ref[1:2, :], gw_ref[...])
        cotangents = (g_ref[...].astype(F32), ds_scr[...])

        def all_heads():
            return jax.vjp(_rec_block_fast, *primals)[1](cotangents)

        def head_by_head():
            outs = [jax.vjp(functools.partial(_rec_head, _rec_core_slow), *args)[1](cts)
                    for args, cts in zip(zip(*(_heads(t) for t in primals)), zip(*(_heads(t) for t in cotangents)))]
            return tuple(jnp.concatenate(parts, axis=1) for parts in zip(*outs))

        dqr, dfr, dv, dz, dS, dl0, dl1, dgw = lax.cond(jnp.max(safe_ref[0]) > 0.5, all_heads, head_by_head)
        for part, val in enumerate((dqr, dfr, dv, dz)):
            dp_ref[:, part * D_MODEL:(part + 1) * D_MODEL] = val.astype(BF16)
        ds_scr[...] = dS
        dlb_ref[0:1, :] += dl0
        dlb_ref[1:2, :] += dl1
        dgw_ref[...] += functools.reduce(jnp.add, _heads(dgw))

    blk = lambda w: pl.BlockSpec((REC_BLOCK, w), lambda b, j: (b * nblk + nblk - 1 - j, 0))
    st_spec = pl.BlockSpec((1, REC_DIM, D_MODEL), lambda b, j: (b * nblk + nblk - 1 - j, 0, 0))
    safe_spec = pl.BlockSpec((1, REC_HEADS, LANES), lambda b, j: (b * nblk + nblk - 1 - j, 0, 0))
    return pl.pallas_call(
        body, name="rec_bwd", grid=(batch, nblk),
        in_specs=[blk(REC_IN), st_spec, safe_spec, _full_spec((2, D_MODEL)), _full_spec((1, D_MODEL)),
                  blk(D_MODEL)],
        out_specs=[blk(REC_IN), _full_spec((2, D_MODEL)), _full_spec((1, REC_DIM))],
        out_shape=[jax.ShapeDtypeStruct((batch * seq, REC_IN), BF16), jax.ShapeDtypeStruct((2, D_MODEL), F32),
                   jax.ShapeDtypeStruct((1, REC_DIM), F32)],
        scratch_shapes=[pltpu.VMEM((REC_DIM, D_MODEL), F32)],
        compiler_params=_cparams(2),
    )(proj, states, safe, lb_logits, jnp.tile(gnorm_w, (1, REC_HEADS)), dog)


_ANY = pl.BlockSpec(memory_space=pl.ANY)


def _chip_peers():
    x, y, c = lax.axis_index("x"), lax.axis_index("y"), lax.axis_index("c")
    peers = []
    for fx, fy in ((1, 0), (0, 1), (1, 1)):
        px, py = (1 - x if fx else x), (1 - y if fy else y)
        peers.append(((px, py, c), 2 * px + py))
    return 2 * x + y, peers


def _remote(src, dst, send_sem, recv_sem, device):
    return pltpu.make_async_remote_copy(src_ref=src, dst_ref=dst, send_sem=send_sem, recv_sem=recv_sem,
                                        device_id=device, device_id_type=MESH)


N_FLIPS = N_CHIPS - 1


def _scatter_sems(n):
    return [pltpu.SemaphoreType.DMA((n * N_FLIPS,)), pltpu.SemaphoreType.DMA((n * N_FLIPS,)),
            pltpu.SemaphoreType.DMA((n,))]


def _scatter_copies(ins, outs, send_sems, recv_sems, local_sems, starting):
    me, peers = _chip_peers()
    local = [pltpu.make_async_copy(ins[k].at[me], outs[k].at[me], local_sems.at[k]) for k in range(len(ins))]
    sends, arrivals = [], []
    for k in range(len(ins)):
        for j, (device, idx) in enumerate(peers):
            sems = (send_sems.at[k * N_FLIPS + j], recv_sems.at[k * N_FLIPS + j], device)
            sends.append(_remote(ins[k].at[idx], outs[k].at[me], *sems))
            if not starting:
                arrivals.append(_remote(ins[k].at[me], outs[k].at[idx], *sems))
    return local, sends, arrivals


def _scatter_start(*refs):
    local, sends, _ = _scatter_copies(*refs, starting=True)
    for cp in local + sends:
        cp.start()


def _scatter_finish(*refs):
    local, sends, arrivals = _scatter_copies(*refs, starting=False)
    for cp in arrivals:
        cp.wait_recv()
    for cp in sends:
        cp.wait_send()
    for cp in local:
        cp.wait()


def _gather_sems(n):
    return [pltpu.SemaphoreType.DMA((n * N_FLIPS,)) for _ in range(4)] + [pltpu.SemaphoreType.DMA((n,))]


def _gather_copies(ins, outs, send_sems, recv_sems, pass_send_sems, pass_recv_sems, local_sems, starting):
    me, peers = _chip_peers()
    c = lax.axis_index("c")
    sibling = (lax.axis_index("x"), lax.axis_index("y"), 1 - c)
    local = [pltpu.make_async_copy(ins[k], outs[k].at[me], local_sems.at[k]) for k in range(len(ins))]
    sends, arrivals, passes, pass_arrivals = [], [], [], []
    for k in range(len(ins)):
        half = ins[k].shape[0] // 2
        mine, other = pl.ds(c * half, half), pl.ds((1 - c) * half, half)
        for j, (device, idx) in enumerate(peers):
            s = k * N_FLIPS + j
            sends.append(_remote(ins[k].at[mine], outs[k].at[me].at[mine], send_sems.at[s], recv_sems.at[s], device))
            if starting:
                continue
            arrived = outs[k].at[idx].at[mine]
            arrivals.append(_remote(ins[k].at[mine], arrived, send_sems.at[s], recv_sems.at[s], device))
            passes.append(_remote(arrived, arrived, pass_send_sems.at[s], pass_recv_sems.at[s], sibling))
            passed = outs[k].at[idx].at[other]
            pass_arrivals.append(_remote(passed, passed, pass_send_sems.at[s], pass_recv_sems.at[s], sibling))
    return local, sends, arrivals, passes, pass_arrivals


def _gather_start(*refs):
    local, sends, _, _, _ = _gather_copies(*refs, starting=True)
    for cp in local + sends:
        cp.start()


def _gather_finish(*refs):
    local, sends, arrivals, passes, pass_arrivals = _gather_copies(*refs, starting=False)
    for arrival, onward in zip(arrivals, passes):
        arrival.wait_recv()
        onward.start()
    for cp in pass_arrivals:
        cp.wait_recv()
    for cp in sends + passes:
        cp.wait_send()
    for cp in local:
        cp.wait()


def chip_gather(arrays):
    n = len(arrays)

    def body(*refs):
        _gather_start(refs[:n], refs[n:2 * n], *refs[2 * n:])
        _gather_finish(refs[:n], refs[n:2 * n], *refs[2 * n:])

    return pl.pallas_call(
        body, name="chip_gather", in_specs=[_ANY] * n, out_specs=[_ANY] * n,
        out_shape=[jax.ShapeDtypeStruct((N_CHIPS,) + a.shape, a.dtype) for a in arrays],
        scratch_shapes=_gather_sems(n),
    )(*arrays)


def sibling_exchange(arrays):
    n = len(arrays)

    def body(*refs):
        ins, outs = refs[:n], refs[n:2 * n]
        send_sems, recv_sems = refs[2 * n:]
        sibling = (lax.axis_index("x"), lax.axis_index("y"), 1 - lax.axis_index("c"))
        copies = [pltpu.make_async_remote_copy(src_ref=ins[k], dst_ref=outs[k], send_sem=send_sems.at[k],
                                               recv_sem=recv_sems.at[k], device_id=sibling, device_id_type=MESH)
                  for k in range(n)]
        for cp in copies:
            cp.start()
        for cp in copies:
            cp.wait()

    return pl.pallas_call(
        body, name="sibling_exchange", in_specs=[_ANY] * n, out_specs=[_ANY] * n,
        out_shape=[jax.ShapeDtypeStruct(a.shape, a.dtype) for a in arrays],
        scratch_shapes=[pltpu.SemaphoreType.DMA((n,)), pltpu.SemaphoreType.DMA((n,))],
    )(*arrays)


def all_gather_small(vec):
    def body(v_ref, out_ref, send_sems, recv_sems, local_sem):
        x, y, c = lax.axis_index("x"), lax.axis_index("y"), lax.axis_index("c")
        me = 4 * x + 2 * y + c
        local = pltpu.make_async_copy(v_ref, out_ref.at[me], local_sem)
        local.start()
        sends, recvs = [], []
        for j in range(1, N_DEV):
            px = jnp.where(j & 4, 1 - x, x)
            py = jnp.where(j & 2, 1 - y, y)
            pc = jnp.where(j & 1, 1 - c, c)
            common = dict(send_sem=send_sems.at[j - 1], recv_sem=recv_sems.at[j - 1], device_id=(px, py, pc),
                          device_id_type=MESH)
            sends.append(pltpu.make_async_remote_copy(src_ref=v_ref, dst_ref=out_ref.at[me], **common))
            recvs.append(pltpu.make_async_remote_copy(src_ref=v_ref, dst_ref=out_ref.at[4 * px + 2 * py + pc],
                                                      **common))
        for cp in sends:
            cp.start()
        for cp in recvs:
            cp.wait_recv()
        for cp in sends:
            cp.wait_send()
        local.wait()

    return pl.pallas_call(
        body, name="all_gather_small", in_specs=[_ANY], out_specs=_ANY,
        out_shape=jax.ShapeDtypeStruct((N_DEV,) + vec.shape, vec.dtype),
        scratch_shapes=[pltpu.SemaphoreType.DMA((N_DEV - 1,)), pltpu.SemaphoreType.DMA((N_DEV - 1,)),
                        pltpu.SemaphoreType.DMA],
    )(vec)


def sum_slots(stacked, tm=256):
    s, r, c = stacked.shape
    tm = min(tm, r)

    def body(in_ref, out_ref):
        acc = in_ref[0].astype(F32)
        for t in range(1, s):
            acc = acc + in_ref[t].astype(F32)
        out_ref[...] = acc

    return pl.pallas_call(
        body, name=f"sum_slots_{s}_{r}_{c}", grid=(r // tm,),
        in_specs=[pl.BlockSpec((s, tm, c), lambda i: (0, i, 0))], out_specs=_row_spec(tm, c),
        out_shape=jax.ShapeDtypeStruct((r, c), F32), compiler_params=_cparams(1),
    )(stacked)


def adamw(w, m, v, g_a, g_b=None, tm=256):
    r, c = w.shape
    tm = min(tm, r)
    two = g_b is not None

    def body(*refs):
        w_ref, m_ref, v_ref, ga_ref = refs[:4]
        g_ref, d_ref, nm_ref, nv_ref = refs[-4:]
        g = ga_ref[...] + refs[4][...] if two else ga_ref[...]
        nm = ADAM_B1 * m_ref[...] + (1.0 - ADAM_B1) * g
        nv = ADAM_B2 * v_ref[...] + (1.0 - ADAM_B2) * (g * g)
        m_hat = nm / (1.0 - ADAM_B1 ** ADAM_STEP)
        v_hat = nv / (1.0 - ADAM_B2 ** ADAM_STEP)
        g_ref[...] = g
        d_ref[...] = -ADAM_LR * (m_hat / (jnp.sqrt(v_hat) + ADAM_EPS) + ADAM_WD * w_ref[...])
        nm_ref[...] = nm
        nv_ref[...] = nv

    args = [w, m, v, g_a] + ([g_b] if two else [])
    return pl.pallas_call(
        body, name=f"adamw_{r}_{c}", grid=(r // tm,),
        in_specs=[_row_spec(tm, c)] * len(args), out_specs=[_row_spec(tm, c)] * 4,
        out_shape=[jax.ShapeDtypeStruct((r, c), F32)] * 4, compiler_params=_cparams(1),
    )(*args)


_SMALL = (("pre_norm_w", (2, D_MODEL)), ("post_norm_w", (2, D_MODEL)), ("attn_b_in", (1, ATTN_IN)),
          ("attn_sinks", (1, N_HEADS)), ("attn_b_out", (1, D_MODEL)), ("rec_lb_logits", (2, D_MODEL)),
          ("rec_gnorm_w", (1, REC_DIM)))
_SMALL_ROWS = 16


def _pack_small(parts, last_row=None):
    rows = []
    for (name, shape) in _SMALL:
        flat = parts[name].reshape(-1)
        pad = -flat.shape[0] % D_MODEL
        rows.append(jnp.pad(flat, (0, pad)).reshape(-1, D_MODEL))
    used = sum(r.shape[0] for r in rows)
    rows.append(jnp.zeros((_SMALL_ROWS - 1 - used, D_MODEL), F32))
    rows.append(jnp.zeros((1, D_MODEL), F32) if last_row is None else last_row)
    return jnp.concatenate(rows, axis=0)


def _unpack_small(packed):
    out, row = {}, 0
    for (name, shape) in _SMALL:
        size = shape[0] * shape[1]
        nrows = -(-size // D_MODEL)
        out[name] = packed[row:row + nrows].reshape(-1)[:size].reshape(shape)
        row += nrows
    return out


_CARRIED = ("rec_w_in", "rec_w_out", "attn_w_out")


_LATE = ("attn_w_out", "rec_w_in", "rec_w_out")


def local_step(x, positions, pre_norm_w, post_norm_w, attn_w_in, attn_b_in, attn_sinks, attn_w_out, attn_b_out,
               rec_w_in, rec_lb_logits, rec_gnorm_w, rec_w_out, loss_target, distributed=False):
    batch, seq, _ = x.shape
    n = batch * seq
    x0 = x.reshape(n, D_MODEL)
    tables = _rope_tables(positions)
    pre0, pre1 = pre_norm_w[0:1], pre_norm_w[1:2]
    post0, post1 = post_norm_w[0:1], post_norm_w[1:2]
    no_bias = jnp.zeros((1, D_MODEL), F32)

    h0, q, k, v, z = attn_in_proj(x0, pre0, attn_w_in, attn_b_in, tables)
    sink_tab = _sink_table(attn_sinks)
    late = (attn_w_out, rec_w_in, rec_w_out)
    og0, gathered = attn_fwd(q, k, v, z, sink_tab, batch, seq, gather=late if distributed else ())
    if distributed:
        attn_w_out, rec_w_in, rec_w_out = (_whole_from_shards(name, g) for name, g in zip(_LATE, gathered))
    y0, x1 = out_proj(og0, attn_w_out, attn_b_out, x0, post0)

    h1, proj1 = rec_in_proj(x1, pre1, rec_w_in)
    og1, states, safe = rec_fwd(proj1, rec_lb_logits, rec_gnorm_w, batch, seq)
    y1, dx2, loss_vec = out_proj(og1, rec_w_out, no_bias, x1, post1, target=loss_target.reshape(n, D_MODEL))

    dog1, d_rec_w_out, _, d_post1 = out_proj_bwd(dx2, y1, og1, rec_w_out, post1)
    dproj1, d_lb, d_gnorm = rec_bwd(proj1, states, safe, rec_lb_logits, rec_gnorm_w, dog1, batch, seq)
    dx1, d_pre1, _ = in_proj_bwd_x(dproj1, rec_w_in, x1, pre1, dx2)
    d_rec_w_in, _ = in_proj_bwd_w(h1, dproj1)

    dog0, d_attn_w_out, d_attn_b_out, d_post0 = out_proj_bwd(dx1, y0, og0, attn_w_out, post0)
    ready = dict(rec_w_in=d_rec_w_in, rec_w_out=d_rec_w_out, attn_w_out=d_attn_w_out)
    outgoing = [_shards_from_whole(name, ready[name]).astype(BF16) for name in _CARRIED] if distributed else []
    dproj0, dk, dv, d_sink_tab, arrived = attn_bwd(q, k, v, z, sink_tab, dog0, tables, batch, seq, scatter=outgoing)
    d_sinks = jnp.transpose(jnp.sum(d_sink_tab, axis=-1), (0, 2, 1)).reshape(1, N_HEADS)
    dproj0 = attn_bwd_kv(dproj0, dk, dv, tables)
    d_attn_w_in, d_attn_b_in = in_proj_bwd_w(h0, dproj0)
    last = [_shards_from_whole("attn_w_in", d_attn_w_in).astype(BF16)] if distributed else []
    dx0, d_pre0, arrived_last = in_proj_bwd_x(dproj0, attn_w_in, x0, pre0, dx1, scatter=last)

    grads = dict(
        pre_norm_w=jnp.concatenate([d_pre0, d_pre1], axis=0), post_norm_w=jnp.concatenate([d_post0, d_post1], axis=0),
        attn_w_in=d_attn_w_in, attn_b_in=d_attn_b_in, attn_sinks=d_sinks, attn_w_out=d_attn_w_out,
        attn_b_out=d_attn_b_out, rec_w_in=d_rec_w_in, rec_lb_logits=d_lb, rec_gnorm_w=d_gnorm,
        rec_w_out=d_rec_w_out)
    parts = dict(zip(_CARRIED + ("attn_w_in",), tuple(arrived) + tuple(arrived_last)))
    return loss_vec, dx0.reshape(batch, seq, D_MODEL), grads, parts


_BIG = ("attn_w_in", "attn_w_out", "rec_w_in", "rec_w_out")
_COLUMN_SHARDED = ("attn_w_in", "rec_w_in")
_ORDER = ("pre_norm_w", "post_norm_w", "attn_w_in", "attn_b_in", "attn_sinks", "attn_w_out", "attn_b_out",
          "rec_w_in", "rec_lb_logits", "rec_gnorm_w", "rec_w_out")


def _whole_from_shards(name, stacked):
    if name in _COLUMN_SHARDED:
        return jnp.transpose(stacked, (1, 0, 2)).reshape(stacked.shape[1], -1)
    return stacked.reshape(-1, stacked.shape[2])


def _shards_from_whole(name, whole):
    if name in _COLUMN_SHARDED:
        return jnp.transpose(whole.reshape(whole.shape[0], N_CHIPS, -1), (1, 0, 2))
    return whole.reshape(N_CHIPS, -1, whole.shape[1])


def kernel(x, positions, pre_norm_w, post_norm_w, attn_w_in, attn_b_in, attn_sinks, attn_w_out, attn_b_out, rec_w_in, rec_lb_logits, rec_gnorm_w, rec_w_out, loss_target, m_pre_norm_w, m_post_norm_w, m_attn_w_in, m_attn_b_in, m_attn_sinks, m_attn_w_out, m_attn_b_out, m_rec_w_in, m_rec_lb_logits, m_rec_gnorm_w, m_rec_w_out, v_pre_norm_w, v_post_norm_w, v_attn_w_in, v_attn_b_in, v_attn_sinks, v_attn_w_out, v_attn_b_out, v_rec_w_in, v_rec_lb_logits, v_rec_gnorm_w, v_rec_w_out):
    w = dict(pre_norm_w=pre_norm_w, post_norm_w=post_norm_w, attn_w_in=attn_w_in, attn_b_in=attn_b_in,
             attn_sinks=attn_sinks, attn_w_out=attn_w_out, attn_b_out=attn_b_out, rec_w_in=rec_w_in,
             rec_lb_logits=rec_lb_logits, rec_gnorm_w=rec_gnorm_w, rec_w_out=rec_w_out)
    m = dict(pre_norm_w=m_pre_norm_w, post_norm_w=m_post_norm_w, attn_w_in=m_attn_w_in, attn_b_in=m_attn_b_in,
             attn_sinks=m_attn_sinks, attn_w_out=m_attn_w_out, attn_b_out=m_attn_b_out, rec_w_in=m_rec_w_in,
             rec_lb_logits=m_rec_lb_logits, rec_gnorm_w=m_rec_gnorm_w, rec_w_out=m_rec_w_out)
    v = dict(pre_norm_w=v_pre_norm_w, post_norm_w=v_post_norm_w, attn_w_in=v_attn_w_in, attn_b_in=v_attn_b_in,
             attn_sinks=v_attn_sinks, attn_w_out=v_attn_w_out, attn_b_out=v_attn_b_out, rec_w_in=v_rec_w_in,
             rec_lb_logits=v_rec_lb_logits, rec_gnorm_w=v_rec_gnorm_w, rec_w_out=v_rec_w_out)

    shards = {name: w[name][0] for name in _BIG}
    sent = {name: shards[name].astype(BF16) for name in _BIG}
    attn_w_in_whole = _whole_from_shards("attn_w_in", chip_gather([sent["attn_w_in"]])[0])

    loss_vec, grad_x, grads, parts = local_step(
        x, positions, pre_norm_w, post_norm_w, attn_w_in_whole, attn_b_in, attn_sinks, sent["attn_w_out"],
        attn_b_out, sent["rec_w_in"], rec_lb_logits, rec_gnorm_w, sent["rec_w_out"], loss_target, distributed=True)

    plane_sums = [sum_slots(parts[name]) for name in _BIG]
    other_sums = sibling_exchange(plane_sums)
    out_g, out_d, out_m, out_v = {}, {}, {}, {}
    for name, mine, other in zip(_BIG, plane_sums, other_sums):
        g, d, nm, nv = adamw(shards[name], m[name][0], v[name][0], mine, other)
        out_g[name], out_d[name], out_m[name], out_v[name] = g[None], d[None], nm[None], nv[None]

    small_sum = sum_slots(all_gather_small(_pack_small(grads, last_row=loss_vec)))
    loss = jnp.sum(small_sum[_SMALL_ROWS - 1]) * (0.5 / D_MODEL)
    packed = adamw(_pack_small(w), _pack_small(m), _pack_small(v), small_sum)
    for dst, val in zip((out_g, out_d, out_m, out_v), packed):
        dst.update(_unpack_small(val))

    return (loss, grad_x, *[out_g[n] for n in _ORDER], *[out_d[n] for n in _ORDER],
            *[out_m[n] for n in _ORDER], *[out_v[n] for n in _ORDER])
```

```python
import functools

import jax
import jax.numpy as jnp
from jax import lax
from jax.experimental import pallas as pl
from jax.experimental.pallas import tpu as pltpu

F32 = jnp.float32
BF16 = jnp.bfloat16
MESH = pl.DeviceIdType.MESH

D_MODEL = 1024
HEAD_DIM = 64
N_HEADS = 16
N_KV_HEADS = 2
GROUP = N_HEADS // N_KV_HEADS
KV_WIDTH = N_KV_HEADS * HEAD_DIM
ATTN_IN = 2 * D_MODEL + 2 * KV_WIDTH
ATTN_BLOCK = 128
ROPE_THETA = 500000.0
ROPE_DIM = HEAD_DIM // 4
REC_HEADS = 8
REC_DIM = 128
REC_IN = 4 * D_MODEL
REC_BLOCK = 128
DIAG = 8
NORM_EPS = 1e-6
N_CHIPS = 4
N_DEV = 8
LANES = 128

ADAM_LR = 0.001
ADAM_B1 = 0.9
ADAM_B2 = 0.999
ADAM_EPS = 1e-08
ADAM_WD = 0.01
ADAM_STEP = 10

VMEM_LIMIT = 56 * 1024 * 1024


def _cparams(n_axes):
    return pltpu.CompilerParams(dimension_semantics=("arbitrary",) * n_axes, vmem_limit_bytes=VMEM_LIMIT)


def _dot(a, b, contract):
    return lax.dot_general(a.astype(BF16), b.astype(BF16), (contract, ((), ())), preferred_element_type=F32)


_NN = ((1,), (0,))
_NT = ((1,), (1,))
_TN = ((0,), (0,))


@jax.custom_vjp
def mm_nn(a, b):
    return _dot(a, b, _NN)


mm_nn.defvjp(lambda a, b: (_dot(a, b, _NN), (a, b)),
             lambda res, g: (_dot(g, res[1], _NT), _dot(res[0], g, _TN)))


@jax.custom_vjp
def mm_nt(a, b):
    return _dot(a, b, _NT)


mm_nt.defvjp(lambda a, b: (_dot(a, b, _NT), (a, b)),
             lambda res, g: (_dot(g, res[1], _NN), _dot(g, res[0], _TN)))


@jax.custom_vjp
def mm_tn(a, b):
    return _dot(a, b, _TN)


mm_tn.defvjp(lambda a, b: (_dot(a, b, _TN), (a, b)),
             lambda res, g: (_dot(res[1], g, _NT), _dot(res[0], g, _NN)))


def _tri_dot(x, lower):
    n = x.shape[0]
    r = lax.broadcasted_iota(jnp.int32, (n, n), 0)
    c = lax.broadcasted_iota(jnp.int32, (n, n), 1)
    tri = ((c <= r) if lower else (c >= r)).astype(BF16)
    hi = x.astype(BF16)
    rest = x - hi.astype(F32)
    mid = rest.astype(BF16)
    lo = (rest - mid.astype(F32)).astype(BF16)
    dot = lambda p: lax.dot_general(tri, p, (_NN, ((), ())), preferred_element_type=F32)
    return (dot(lo) + dot(mid)) + dot(hi)


@jax.custom_vjp
def cumsum_rows(x):
    return _tri_dot(x, True)


cumsum_rows.defvjp(lambda x: (cumsum_rows(x), None), lambda _, g: (_tri_dot(g, False),))


@functools.partial(jax.custom_vjp, nondiff_argnums=(1,))
def roll_sub(x, d):
    return pltpu.roll(x, d, 1) if d else x


roll_sub.defvjp(lambda x, d: (roll_sub(x, d), None),
                lambda d, _, g: (roll_sub(g, (DIAG - d) % DIAG),))


def sigmoid(x):
    return 1.0 / (1.0 + jnp.exp(-x))


@jax.custom_vjp
def silu(x):
    return x * sigmoid(x)


def _silu_fwd(x):
    s = sigmoid(x)
    return x * s, (x, s)


silu.defvjp(_silu_fwd, lambda res, g: (g * (res[1] * (1.0 + res[0] * (1.0 - res[1]))),))


F32_TINY = 1.17549435e-38


def sigmoid_pair(x):
    e = jnp.exp(-jnp.abs(x))
    r = 1.0 / (1.0 + e)
    er = e * r
    pos = x >= 0.0
    return jnp.where(pos, r, er), jnp.where(pos, er, r)


def _forget_fwd(x, a):
    lb, one_m_lb = sigmoid_pair(a)
    sp, sn = sigmoid_pair(x)
    f = lb + one_m_lb * sp
    k = one_m_lb * sn
    return (jnp.log(jnp.maximum(f, F32_TINY)), k), (sp, sn, f, k, lb, one_m_lb)


def _forget_bwd(res, g):
    sp, sn, f, k, lb, one_m_lb = res
    g_lf, g_k = g
    t = jnp.where(f >= F32_TINY, g_lf / jnp.maximum(f, F32_TINY), 0.0) - g_k
    return (k * sp) * t, jnp.sum(sn * t, axis=0, keepdims=True) * (lb * one_m_lb)


@jax.custom_vjp
def forget_gate(x, a):
    return _forget_fwd(x, a)[0]


forget_gate.defvjp(_forget_fwd, _forget_bwd)


@jax.custom_vjp
def decayed(x, e):
    return (x * jnp.exp(e)).astype(BF16).astype(F32)


def _decayed_fwd(x, e):
    y = decayed(x, e)
    return y, (y, e)


decayed.defvjp(_decayed_fwd, lambda res, g: (g * jnp.exp(res[1]), g * res[0]))


def _row(x, r):
    shape = x.shape

    @jax.custom_vjp
    def take(x):
        return x[r:r + 1, :]

    take.defvjp(lambda x: (x[r:r + 1, :], None),
                lambda _, g: (jnp.where(lax.broadcasted_iota(jnp.int32, shape, 0) == r, g, 0.0),))
    return take(x)


def _rms(x):
    return lax.rsqrt(jnp.mean(x * x, axis=-1, keepdims=True) + NORM_EPS)


def _attn_group(qs, k_a, v_a, k_b, v_b, zs, sink_a, sink_b, bias, at_sink):
    def half(kh, vh, sink):
        s = jnp.where(at_sink, jnp.concatenate([sink, sink], axis=1), mm_nn(qs, kh) + bias)
        p = jnp.exp(s - lax.stop_gradient(jnp.max(s, axis=-1, keepdims=True)))
        pn = p * (1.0 / jnp.sum(p, axis=-1, keepdims=True))
        return mm_nt(jnp.where(at_sink, 0.0, pn), vh)

    return (half(k_a, v_a, sink_a) + half(k_b, v_b, sink_b)) * silu(zs)


SAFE_RANGE = 80.0


def _rec_front(qr, fr, l0, l1):
    lf, k = forget_gate(fr, l1 - l0)
    return silu(qr), k, lf


def _rec_tail(o, z, gw):
    return o * _rms(o) * gw * silu(z)


def _rec_margin(b):
    R = b.shape[0]
    mid, last = _row(b, R // 2 - 1), _row(b, R - 1)
    return jnp.minimum(mid, last - mid)


def _heads(x):
    w = x.shape[1] // REC_HEADS
    return [x[:, h * w:(h + 1) * w] for h in range(REC_HEADS)]


def _hdot(a, b, contract):
    return jnp.concatenate([_dot(ah, bh, contract) for ah, bh in zip(_heads(a), _heads(b))], axis=1)


@jax.custom_vjp
def hmm_nn(a, b):
    return _hdot(a, b, _NN)


hmm_nn.defvjp(lambda a, b: (_hdot(a, b, _NN), (a, b)),
              lambda res, g: (_hdot(g, res[1], _NT), _hdot(res[0], g, _TN)))


@jax.custom_vjp
def hmm_nt(a, b):
    return _hdot(a, b, _NT)


hmm_nt.defvjp(lambda a, b: (_hdot(a, b, _NT), (a, b)),
              lambda res, g: (_hdot(g, res[1], _NN), _hdot(g, res[0], _TN)))


@jax.custom_vjp
def hmm_tn(a, b):
    return _hdot(a, b, _TN)


hmm_tn.defvjp(lambda a, b: (_hdot(a, b, _TN), (a, b)),
              lambda res, g: (_hdot(res[1], g, _NT), _hdot(res[0], g, _NN)))


def _head_sums(x):
    return jnp.concatenate([jnp.broadcast_to(jnp.sum(xh, axis=-1, keepdims=True), xh.shape) for xh in _heads(x)],
                           axis=1)


@jax.custom_vjp
def head_sum(x):
    return _head_sums(x)


head_sum.defvjp(lambda x: (_head_sums(x), None), lambda _, g: (_head_sums(g),))


def _rec_cores_fast(q, k, v, b, S):
    R = q.shape[0]
    ri = lax.broadcasted_iota(jnp.int32, (R, REC_HEADS * R), 0)
    ci = lax.broadcasted_iota(jnp.int32, (R, REC_HEADS * R), 1) % R
    d = b - _row(b, R // 2 - 1)
    sc = jnp.where(ci < ri, hmm_nt(decayed(q, d), decayed(k, -d)), 0.0)
    o = hmm_nt(q * jnp.exp(b), S) + hmm_nn(sc, v) + head_sum(q * k) * v
    b_last = _row(b, R - 1)
    return o, S * jnp.exp(b_last) + hmm_tn(v, k * jnp.exp(b_last - b))


def _rec_tails(o, z, gw):
    return o * lax.rsqrt(head_sum(o * o) * (1.0 / REC_DIM) + NORM_EPS) * gw * silu(z)


def _rec_block_fast(qr, fr, v, z, S, l0, l1, gw):
    lf, k = forget_gate(fr, l1 - l0)
    o, S_new = _rec_cores_fast(silu(qr), k, v, cumsum_rows(lf), S)
    return _rec_tails(o, z, gw), S_new


def _rec_core_slow(q, k, v, b, S):
    R = q.shape[0]
    rows = lax.broadcasted_iota(jnp.int32, (R, REC_DIM), 0)

    o = mm_nt(q * jnp.exp(jnp.minimum(b, 0.0)), S)

    ri = lax.broadcasted_iota(jnp.int32, (R, R), 0)
    ci = lax.broadcasted_iota(jnp.int32, (R, R), 1)
    sc = jnp.zeros((R, R), F32)
    w = R
    while w > DIAG:
        h = w // 2
        b3 = b.reshape(R // w, w, REC_DIM)
        rin = lax.broadcasted_iota(jnp.int32, (R // w, w, REC_DIM), 1)
        mid = jnp.sum(jnp.where(rin == h - 1, b3, 0.0), axis=1, keepdims=True)
        fac = jnp.exp(jnp.minimum(jnp.where(rin >= h, b3 - mid, mid - b3), 0.0)).reshape(R, REC_DIM)
        upper = (rows % w) >= h
        s_w = mm_nt(jnp.where(upper, q * fac, 0.0), jnp.where(upper, 0.0, k * fac))
        sc = sc + jnp.where((ri // w) == (ci // w), s_w, 0.0)
        w = h
    o = o + mm_nn(sc, v)

    g = R // DIAG
    q3, k3, v3, b3 = (t.reshape(g, DIAG, REC_DIM) for t in (q, k, v, b))
    rin = lax.broadcasted_iota(jnp.int32, (g, DIAG, 1), 1)
    od = jnp.zeros((g, DIAG, REC_DIM), F32)
    for d in range(DIAG):
        e = jnp.exp(jnp.minimum(b3 - roll_sub(b3, d), 0.0))
        sd = jnp.sum(q3 * roll_sub(k3, d) * e, axis=-1, keepdims=True)
        od = od + jnp.where(rin >= d, sd, 0.0) * roll_sub(v3, d)
    o = o + od.reshape(R, REC_DIM)

    b_last = _row(b, R - 1)
    return o, S * jnp.exp(jnp.minimum(b_last, 0.0)) + mm_tn(v, k * jnp.exp(jnp.minimum(b_last - b, 0.0)))


def _rec_head(core, qr, fr, v, z, S, l0, l1, gw):
    q, k, lf = _rec_front(qr, fr, l0, l1)
    o, S_new = core(q, k, v, cumsum_rows(lf), S)
    return _rec_tail(o, z, gw), S_new


def _rope_tables(positions):
    half = ROPE_DIM // 2
    inv_freq = ROPE_THETA ** (-(jnp.arange(half, dtype=F32) * 2.0 / ROPE_DIM))
    rest = jnp.zeros((HEAD_DIM - ROPE_DIM,), F32)
    ones, zeros = jnp.ones((half,), F32), jnp.zeros((half,), F32)
    per_lane = lambda first, second: jnp.tile(jnp.concatenate([first, second, rest]), LANES // HEAD_DIM)[None, :]
    ang = positions.astype(F32).reshape(-1, 1) * per_lane(inv_freq, inv_freq)
    sin = jnp.sin(ang)
    return jnp.cos(ang), sin * per_lane(zeros, ones), sin * per_lane(-ones, zeros)


def _rope(x, cos_t, sin_a, sin_b):
    half = ROPE_DIM // 2
    return x * cos_t + pltpu.roll(x, half, 1) * sin_a + pltpu.roll(x, LANES - half, 1) * sin_b


def _rope_transposed(g, cos_t, sin_a, sin_b):
    half = ROPE_DIM // 2
    return g * cos_t + pltpu.roll(g * sin_a, LANES - half, 1) + pltpu.roll(g * sin_b, half, 1)


def _row_spec(tm, width):
    return pl.BlockSpec((tm, width), lambda i: (i, 0))


def _full_spec(shape):
    return pl.BlockSpec(shape, lambda *_: (0,) * len(shape))


def attn_in_proj(x, w_pre, w_in, b_in, tables, tm=1024):
    n = x.shape[0]
    tm = min(tm, n)

    def body(x_ref, wp_ref, w_ref, b_ref, c_ref, sa_ref, sb_ref, h_ref, q_ref, k_ref, v_ref, z_ref):
        xv = x_ref[...]
        h = (xv * _rms(xv) * wp_ref[...]).astype(BF16)
        h_ref[...] = h
        proj = jnp.dot(h, w_ref[...], preferred_element_type=F32) + b_ref[...]
        tabs = (c_ref[...], sa_ref[...], sb_ref[...])
        for s in range(D_MODEL // LANES):
            sl = slice(s * LANES, (s + 1) * LANES)
            q_ref[:, sl] = _rope(proj[:, sl] * (HEAD_DIM ** -0.5), *tabs).astype(BF16)
        k_ref[...] = _rope(proj[:, D_MODEL:D_MODEL + KV_WIDTH], *tabs).astype(BF16)
        v_ref[...] = proj[:, D_MODEL + KV_WIDTH:D_MODEL + 2 * KV_WIDTH].astype(BF16)
        z_ref[...] = proj[:, D_MODEL + 2 * KV_WIDTH:]

    return pl.pallas_call(
        body, name="attn_in_proj", grid=(n // tm,),
        in_specs=[_row_spec(tm, D_MODEL), _full_spec((1, D_MODEL)), _full_spec((D_MODEL, ATTN_IN)),
                  _full_spec((1, ATTN_IN))] + [_row_spec(tm, LANES)] * 3,
        out_specs=[_row_spec(tm, D_MODEL), _row_spec(tm, D_MODEL), _row_spec(tm, KV_WIDTH),
                   _row_spec(tm, KV_WIDTH), _row_spec(tm, D_MODEL)],
        out_shape=[jax.ShapeDtypeStruct((n, D_MODEL), BF16), jax.ShapeDtypeStruct((n, D_MODEL), BF16),
                   jax.ShapeDtypeStruct((n, KV_WIDTH), BF16), jax.ShapeDtypeStruct((n, KV_WIDTH), BF16),
                   jax.ShapeDtypeStruct((n, D_MODEL), F32)],
        compiler_params=_cparams(1),
    )(x, w_pre, w_in, b_in, *tables)


def _column_blocks(w):
    if len(w.shape) == 2:
        return [slice(0, w.shape[1])], lambda ref, s: ref[...]
    width = w.shape[2]
    return [slice(s * width, (s + 1) * width) for s in range(w.shape[0])], lambda ref, s: ref[s]


def rec_in_proj(x, w_pre, w_in, tm=512):
    n = x.shape[0]
    tm = min(tm, n)
    columns, block = _column_blocks(w_in)

    def body(x_ref, wp_ref, w_ref, h_ref, p_ref):
        xv = x_ref[...]
        h = (xv * _rms(xv) * wp_ref[...]).astype(BF16)
        h_ref[...] = h
        for s, cols in enumerate(columns):
            p_ref[:, cols] = jnp.dot(h, block(w_ref, s), preferred_element_type=F32)

    return pl.pallas_call(
        body, name="rec_in_proj", grid=(n // tm,),
        in_specs=[_row_spec(tm, D_MODEL), _full_spec((1, D_MODEL)), _full_spec(w_in.shape)],
        out_specs=[_row_spec(tm, D_MODEL), _row_spec(tm, REC_IN)],
        out_shape=[jax.ShapeDtypeStruct((n, D_MODEL), BF16), jax.ShapeDtypeStruct((n, REC_IN), F32)],
        compiler_params=_cparams(1),
    )(x, w_pre, w_in)


def out_proj(og, w_out, b_out, x_res, w_post, target=None, tm=1024):
    n = og.shape[0]
    tm = min(tm, n)
    with_loss = target is not None

    def body(*refs):
        if with_loss:
            og_ref, w_ref, b_ref, x_ref, wp_ref, t_ref, y_ref, dx_ref, l_ref = refs
        else:
            og_ref, w_ref, b_ref, x_ref, wp_ref, y_ref, xo_ref = refs
        y = jnp.dot(og_ref[...], w_ref[...], preferred_element_type=F32) + b_ref[...]
        y_ref[...] = y.astype(BF16)
        xo = x_ref[...] + y * _rms(y) * wp_ref[...]
        if with_loss:
            err = xo - t_ref[...]
            dx_ref[...] = err * (1.0 / D_MODEL)

            @pl.when(pl.program_id(0) == 0)
            def _():
                l_ref[...] = jnp.zeros_like(l_ref)

            l_ref[...] += jnp.sum(err * err, axis=0, keepdims=True)
        else:
            xo_ref[...] = xo

    in_specs = [_row_spec(tm, D_MODEL), _full_spec((D_MODEL, D_MODEL)), _full_spec((1, D_MODEL)),
                _row_spec(tm, D_MODEL), _full_spec((1, D_MODEL))]
    out_specs = [_row_spec(tm, D_MODEL), _row_spec(tm, D_MODEL)]
    out_shape = [jax.ShapeDtypeStruct((n, D_MODEL), BF16), jax.ShapeDtypeStruct((n, D_MODEL), F32)]
    args = [og, w_out, b_out, x_res, w_post]
    if with_loss:
        in_specs.append(_row_spec(tm, D_MODEL))
        out_specs.append(_full_spec((1, D_MODEL)))
        out_shape.append(jax.ShapeDtypeStruct((1, D_MODEL), F32))
        args.append(target)
    return pl.pallas_call(
        body, name="out_proj_loss" if with_loss else "out_proj", grid=(n // tm,),
        in_specs=in_specs, out_specs=out_specs, out_shape=out_shape, compiler_params=_cparams(1),
    )(*args)


def out_proj_bwd(dxo, y, og, w_out, w_post, tm=1024):
    n = og.shape[0]
    tm = min(tm, n)

    def body(g_ref, y_ref, og_ref, w_ref, wp_ref, dog_ref, dw_ref, db_ref, dwp_ref):
        @pl.when(pl.program_id(0) == 0)
        def _():
            dw_ref[...] = jnp.zeros_like(dw_ref)
            db_ref[...] = jnp.zeros_like(db_ref)
            dwp_ref[...] = jnp.zeros_like(dwp_ref)

        g, y = g_ref[...], y_ref[...].astype(F32)
        rstd = _rms(y)
        yn = y * rstd
        gw = g * wp_ref[...]
        dwp_ref[...] += jnp.sum(g * yn, axis=0, keepdims=True)
        dy = rstd * (gw - yn * jnp.mean(gw * yn, axis=-1, keepdims=True))
        db_ref[...] += jnp.sum(dy, axis=0, keepdims=True)
        dyb = dy.astype(BF16)
        dog_ref[...] = _dot(dyb, w_ref[...], _NT).astype(BF16)
        dw_ref[...] += _dot(og_ref[...], dyb, _TN)

    return pl.pallas_call(
        body, name="out_proj_bwd", grid=(n // tm,),
        in_specs=[_row_spec(tm, D_MODEL), _row_spec(tm, D_MODEL), _row_spec(tm, D_MODEL),
                  _full_spec((D_MODEL, D_MODEL)), _full_spec((1, D_MODEL))],
        out_specs=[_row_spec(tm, D_MODEL), _full_spec((D_MODEL, D_MODEL)), _full_spec((1, D_MODEL)),
                   _full_spec((1, D_MODEL))],
        out_shape=[jax.ShapeDtypeStruct((n, D_MODEL), BF16), jax.ShapeDtypeStruct((D_MODEL, D_MODEL), F32),
                   jax.ShapeDtypeStruct((1, D_MODEL), F32), jax.ShapeDtypeStruct((1, D_MODEL), F32)],
        compiler_params=_cparams(1),
    )(dxo, y, og, w_out, w_post)


def in_proj_bwd_x(dproj, w_in, x, w_pre, dxo, tm=512, scatter=()):
    n, p = dproj.shape
    tm = min(tm if p > ATTN_IN else 2 * tm, n)
    steps = n // tm
    ns = len(scatter)
    columns, block = _column_blocks(w_in)

    def body(*refs):
        dp_ref, w_ref, x_ref, wp_ref, g_ref = refs[:5]
        dx_ref, dwp_ref = refs[5 + ns:7 + ns]
        exchange = (refs[5:5 + ns], refs[7 + ns:7 + 2 * ns]) + tuple(refs[7 + 2 * ns:])

        @pl.when(pl.program_id(0) == 0)
        def _():
            dwp_ref[...] = jnp.zeros_like(dwp_ref)
            if ns:
                _scatter_start(*exchange)

        dh = functools.reduce(jnp.add, [_dot(dp_ref[:, cols], block(w_ref, s), _NT)
                                        for s, cols in enumerate(columns)])
        xv = x_ref[...]
        rstd = _rms(xv)
        xn = xv * rstd
        gw = dh * wp_ref[...]
        dwp_ref[...] += jnp.sum(dh * xn, axis=0, keepdims=True)
        dx_ref[...] = rstd * (gw - xn * jnp.mean(gw * xn, axis=-1, keepdims=True)) + g_ref[...]

        if ns:
            @pl.when(pl.program_id(0) == steps - 1)
            def _():
                _scatter_finish(*exchange)

    out = pl.pallas_call(
        body, name=f"in_proj_bwd_x_{p}", grid=(steps,),
        in_specs=[_row_spec(tm, p), _full_spec(w_in.shape), _row_spec(tm, D_MODEL), _full_spec((1, D_MODEL)),
                  _row_spec(tm, D_MODEL)] + [_ANY] * ns,
        out_specs=[_row_spec(tm, D_MODEL), _full_spec((1, D_MODEL))] + [_ANY] * ns,
        out_shape=[jax.ShapeDtypeStruct((n, D_MODEL), F32), jax.ShapeDtypeStruct((1, D_MODEL), F32)]
        + [jax.ShapeDtypeStruct(a.shape, a.dtype) for a in scatter],
        scratch_shapes=_scatter_sems(ns) if ns else [],
        compiler_params=_cparams(1),
    )(dproj, w_in, x, w_pre, dxo, *scatter)
    return out[0], out[1], out[2:]


def in_proj_bwd_w(h, dproj, tm=1024, as_shards=False):
    n, p = dproj.shape
    chunk = p // (4 if p % 4096 == 0 else 3)
    tm = min(tm, n)
    steps = n // tm
    shard = p // N_CHIPS

    def body(h_ref, dp_ref, dw_ref, db_ref, acc_scr, sem, *staging):
        i = pl.program_id(0)

        @pl.when(i == 0)
        def _():
            acc_scr[...] = jnp.zeros_like(acc_scr)
            db_ref[...] = jnp.zeros_like(db_ref)

        ht = h_ref[...].T
        for c0 in range(0, p, chunk):
            dp = dp_ref[:, c0:c0 + chunk]
            acc_scr[:, c0:c0 + chunk] += jnp.dot(ht, dp, preferred_element_type=F32)
            db_ref[:, c0:c0 + chunk] += jnp.sum(dp.astype(F32), axis=0, keepdims=True)

        @pl.when(i == steps - 1)
        def _():
            if as_shards:
                for s in range(N_CHIPS):
                    staging[0][...] = acc_scr[:, s * shard:(s + 1) * shard].astype(BF16)
                    out = pltpu.make_async_copy(staging[0], dw_ref.at[s], sem)
                    out.start()
                    out.wait()
            else:
                out = pltpu.make_async_copy(acc_scr, dw_ref, sem)
                out.start()
                out.wait()

    dw_shape = jax.ShapeDtypeStruct((N_CHIPS, D_MODEL, shard), BF16) if as_shards else (
        jax.ShapeDtypeStruct((D_MODEL, p), F32))
    return pl.pallas_call(
        body, name=f"in_proj_bwd_w_{p}", grid=(steps,),
        in_specs=[_row_spec(tm, D_MODEL), _row_spec(tm, p)],
        out_specs=[_ANY, _full_spec((1, p))],
        out_shape=[dw_shape, jax.ShapeDtypeStruct((1, p), F32)],
        scratch_shapes=[pltpu.VMEM((D_MODEL, p), F32), pltpu.SemaphoreType.DMA]
        + ([pltpu.VMEM((D_MODEL, shard), BF16)] if as_shards else []),
        compiler_params=_cparams(1),
    )(h, dproj)


PAIRS = GROUP // 2
GROUP_ROWS = PAIRS * ATTN_BLOCK
MASKED = -1e30


def _kv_windows(k_ref, v_ref, i):
    ps = pl.multiple_of(jnp.maximum(i - 1, 0) * ATTN_BLOCK, ATTN_BLOCK)
    cs = pl.multiple_of(i * ATTN_BLOCK, ATTN_BLOCK)
    kw = jnp.concatenate([k_ref[pl.ds(ps, ATTN_BLOCK), :], k_ref[pl.ds(cs, ATTN_BLOCK), :]], axis=0)
    vw = jnp.concatenate([v_ref[pl.ds(ps, ATTN_BLOCK), :], v_ref[pl.ds(cs, ATTN_BLOCK), :]], axis=0)
    return kw.astype(F32).T, vw.astype(F32).T, ps, cs


def _low_rows(shape):
    return lax.broadcasted_iota(jnp.int32, shape, 0) < HEAD_DIM


def _spread(w, kvh):
    low = _low_rows(w.shape)
    swapped = pltpu.roll(w, HEAD_DIM, 0)
    if kvh == 0:
        return jnp.where(low, w, 0.0), jnp.where(low, 0.0, swapped)
    return jnp.where(low, swapped, 0.0), jnp.where(low, 0.0, w)


def _unspread(d_a, d_b, kvh):
    low = _low_rows(d_a.shape)
    if kvh == 0:
        return jnp.where(low, d_a + pltpu.roll(d_b, HEAD_DIM, 0), 0.0)
    return jnp.where(low, 0.0, pltpu.roll(d_a, HEAD_DIM, 0) + d_b)


def _stack_pairs(ref, kvh):
    return jnp.concatenate([ref[:, (kvh * PAIRS + j) * LANES:(kvh * PAIRS + j + 1) * LANES] for j in range(PAIRS)],
                           axis=0)


def _fill_bias(bias_scr):
    shape = (GROUP_ROWS, 2 * ATTN_BLOCK)
    r = lax.broadcasted_iota(jnp.int32, shape, 0) % ATTN_BLOCK
    c = lax.broadcasted_iota(jnp.int32, shape, 1)
    in_cur = (c >= ATTN_BLOCK) & ((c - ATTN_BLOCK) <= r)
    in_prev = (c < ATTN_BLOCK) & (c > r)
    bias_scr[0] = jnp.where(in_cur, 0.0, MASKED)
    bias_scr[1] = jnp.where(in_cur | in_prev, 0.0, MASKED)
    bias_scr[2] = jnp.where(c == r, 1.0, 0.0)


N_BIAS_TABLES = 3


def _sink_table(sinks):
    t = jnp.transpose(sinks.reshape(N_KV_HEADS, PAIRS, 2), (0, 2, 1))
    return jnp.broadcast_to(t[:, :, :, None, None], (N_KV_HEADS, 2, PAIRS, ATTN_BLOCK, LANES)).reshape(
        N_KV_HEADS, 2, GROUP_ROWS, LANES)


def attn_fwd(q, k, v, z, sink_tab, batch, seq, gather=()):
    nb = seq // ATTN_BLOCK
    ng = len(gather)

    def body(*refs):
        q_ref, k_ref, v_ref, z_ref, s_ref = refs[:5]
        og_ref, bias_scr = refs[5 + ng], refs[6 + 2 * ng]
        exchange = (refs[5:5 + ng], refs[6 + ng:6 + 2 * ng]) + tuple(refs[7 + 2 * ng:])
        b, i = pl.program_id(0), pl.program_id(1)

        @pl.when((b == 0) & (i == 0))
        def _():
            _fill_bias(bias_scr)
            if ng:
                _gather_start(*exchange)

        kw, vw, _, _ = _kv_windows(k_ref, v_ref, i)
        bias, at_sink = bias_scr[jnp.minimum(i, 1)], bias_scr[2] > 0.5
        for kvh in range(N_KV_HEADS):
            k_a, k_b = _spread(kw, kvh)
            v_a, v_b = _spread(vw, kvh)
            og = _attn_group(_stack_pairs(q_ref, kvh), k_a, v_a, k_b, v_b, _stack_pairs(z_ref, kvh),
                             s_ref[kvh, 0], s_ref[kvh, 1], bias, at_sink)
            for j in range(PAIRS):
                og_ref[:, (kvh * PAIRS + j) * LANES:(kvh * PAIRS + j + 1) * LANES] = (
                    og[j * ATTN_BLOCK:(j + 1) * ATTN_BLOCK].astype(BF16))

        if ng:
            @pl.when((b == batch - 1) & (i == nb - 1))
            def _():
                _gather_finish(*exchange)

    blk = lambda w: pl.BlockSpec((ATTN_BLOCK, w), lambda b, i: (b * nb + i, 0))
    seq_spec = pl.BlockSpec((seq, KV_WIDTH), lambda b, i: (b, 0))
    out = pl.pallas_call(
        body, name="attn_fwd", grid=(batch, nb),
        in_specs=[blk(D_MODEL), seq_spec, seq_spec, blk(D_MODEL), _full_spec(sink_tab.shape)] + [_ANY] * ng,
        out_specs=[blk(D_MODEL)] + [_ANY] * ng,
        out_shape=[jax.ShapeDtypeStruct((batch * seq, D_MODEL), BF16)]
        + [jax.ShapeDtypeStruct((N_CHIPS,) + a.shape, a.dtype) for a in gather],
        scratch_shapes=[pltpu.VMEM((N_BIAS_TABLES, GROUP_ROWS, 2 * ATTN_BLOCK), F32)] + (_gather_sems(ng) if ng else []),
        compiler_params=_cparams(2),
    )(q, k, v, z, sink_tab, *gather)
    return out[0], out[1:]


def attn_bwd(q, k, v, z, sink_tab, dog, tables, batch, seq, scatter=()):
    nb = seq // ATTN_BLOCK
    ns = len(scatter)

    def body(*refs):
        q_ref, k_ref, v_ref, z_ref, s_ref, g_ref, c_ref, sa_ref, sb_ref = refs[:9]
        dp_ref, dk_ref, dv_ref, ds_ref = refs[9 + ns:13 + ns]
        bias_scr = refs[13 + 2 * ns]
        exchange = (refs[9:9 + ns], refs[13 + ns:13 + 2 * ns]) + tuple(refs[14 + 2 * ns:])
        b, i = pl.program_id(0), pl.program_id(1)

        @pl.when((b == 0) & (i == 0))
        def _():
            _fill_bias(bias_scr)
            ds_ref[...] = jnp.zeros_like(ds_ref)
            if ns:
                _scatter_start(*exchange)

        @pl.when(i == 0)
        def _():
            dk_ref[...] = jnp.zeros_like(dk_ref)
            dv_ref[...] = jnp.zeros_like(dv_ref)

        kw, vw, ps, cs = _kv_windows(k_ref, v_ref, i)
        bias, at_sink = bias_scr[jnp.minimum(i, 1)], bias_scr[2] > 0.5
        tabs = (c_ref[...], sa_ref[...], sb_ref[...])
        dkw = jnp.zeros_like(kw)
        dvw = jnp.zeros_like(vw)
        for kvh in range(N_KV_HEADS):
            k_a, k_b = _spread(kw, kvh)
            v_a, v_b = _spread(vw, kvh)
            _, vjp = jax.vjp(functools.partial(_attn_group, bias=bias, at_sink=at_sink), _stack_pairs(q_ref, kvh).astype(F32),
                             k_a, v_a, k_b, v_b, _stack_pairs(z_ref, kvh), s_ref[kvh, 0], s_ref[kvh, 1])
            dqs, dk_a, dv_a, dk_b, dv_b, dzs, ds_a, ds_b = vjp(_stack_pairs(g_ref, kvh).astype(F32))
            dkw = dkw + _unspread(dk_a, dk_b, kvh)
            dvw = dvw + _unspread(dv_a, dv_b, kvh)
            ds_ref[kvh, 0] += jnp.sum(ds_a.reshape(PAIRS, ATTN_BLOCK, LANES), axis=1)
            ds_ref[kvh, 1] += jnp.sum(ds_b.reshape(PAIRS, ATTN_BLOCK, LANES), axis=1)
            for j in range(PAIRS):
                rows = slice(j * ATTN_BLOCK, (j + 1) * ATTN_BLOCK)
                col = (kvh * PAIRS + j) * LANES
                dp_ref[:, col:col + LANES] = _rope_transposed(dqs[rows] * (HEAD_DIM ** -0.5), *tabs).astype(BF16)
                zc = D_MODEL + 2 * KV_WIDTH + col
                dp_ref[:, zc:zc + LANES] = dzs[rows].astype(BF16)
        dp_ref[:, D_MODEL:D_MODEL + 2 * KV_WIDTH] = jnp.zeros((ATTN_BLOCK, 2 * KV_WIDTH), BF16)
        dk_ref[:, pl.ds(ps, ATTN_BLOCK)] += dkw[:, :ATTN_BLOCK]
        dk_ref[:, pl.ds(cs, ATTN_BLOCK)] += dkw[:, ATTN_BLOCK:]
        dv_ref[:, pl.ds(ps, ATTN_BLOCK)] += dvw[:, :ATTN_BLOCK]
        dv_ref[:, pl.ds(cs, ATTN_BLOCK)] += dvw[:, ATTN_BLOCK:]

        if ns:
            @pl.when((b == batch - 1) & (i == nb - 1))
            def _():
                _scatter_finish(*exchange)

    blk = lambda w: pl.BlockSpec((ATTN_BLOCK, w), lambda b, i: (b * nb + i, 0))
    seq_spec = pl.BlockSpec((seq, KV_WIDTH), lambda b, i: (b, 0))
    seq_spec_t = pl.BlockSpec((KV_WIDTH, seq), lambda b, i: (0, b))
    n = batch * seq
    ds_shape = (N_KV_HEADS, 2, PAIRS, LANES)
    out = pl.pallas_call(
        body, name="attn_bwd", grid=(batch, nb),
        in_specs=[blk(D_MODEL), seq_spec, seq_spec, blk(D_MODEL), _full_spec(sink_tab.shape), blk(D_MODEL)]
        + [blk(LANES)] * 3 + [_ANY] * ns,
        out_specs=[blk(ATTN_IN), seq_spec_t, seq_spec_t, _full_spec(ds_shape)] + [_ANY] * ns,
        out_shape=[jax.ShapeDtypeStruct((n, ATTN_IN), BF16), jax.ShapeDtypeStruct((KV_WIDTH, n), F32),
                   jax.ShapeDtypeStruct((KV_WIDTH, n), F32), jax.ShapeDtypeStruct(ds_shape, F32)]
        + [jax.ShapeDtypeStruct(a.shape, a.dtype) for a in scatter],
        scratch_shapes=[pltpu.VMEM((N_BIAS_TABLES, GROUP_ROWS, 2 * ATTN_BLOCK), F32)] + (_scatter_sems(ns) if ns else []),
        compiler_params=_cparams(2),
    )(q, k, v, z, sink_tab, dog, *tables, *scatter)
    return out[0], out[1], out[2], out[3], out[4:]


def attn_bwd_kv(dproj, dk_t, dv_t, tables, tm=512):
    n = dproj.shape[0]

    def body(dp_in_ref, dk_ref, dv_ref, c_ref, sa_ref, sb_ref, dp_ref):
        del dp_in_ref
        dp_ref[:, :KV_WIDTH] = _rope_transposed(dk_ref[...].T, c_ref[...], sa_ref[...], sb_ref[...]).astype(BF16)
        dp_ref[:, KV_WIDTH:] = dv_ref[...].T.astype(BF16)

    kv_cols = pl.BlockSpec((tm, 2 * KV_WIDTH), lambda i: (i, D_MODEL // (2 * KV_WIDTH)))
    col_spec = pl.BlockSpec((KV_WIDTH, tm), lambda i: (0, i))
    return pl.pallas_call(
        body, name="attn_bwd_kv", grid=(n // tm,),
        in_specs=[kv_cols, col_spec, col_spec] + [_row_spec(tm, LANES)] * 3,
        out_specs=kv_cols, out_shape=jax.ShapeDtypeStruct(dproj.shape, BF16),
        input_output_aliases={0: 0}, compiler_params=_cparams(1),
    )(dproj, dk_t, dv_t, *tables)


def rec_fwd(proj, lb_logits, gnorm_w, batch, seq):
    nblk = seq // REC_BLOCK

    def body(p_ref, lb_ref, gw_ref, og_ref, st_ref, safe_ref, s_scr):
        @pl.when(pl.program_id(1) == 0)
        def _():
            s_scr[...] = jnp.zeros_like(s_scr)

        S = s_scr[...]
        st_ref[0] = S
        qr, fr, v, z = (p_ref[:, part * D_MODEL:(part + 1) * D_MODEL] for part in range(4))
        lf, k = forget_gate(fr, lb_ref[1:2, :] - lb_ref[0:1, :])
        q, b = silu(qr), cumsum_rows(lf)
        safe = jnp.min(_rec_margin(b)) >= -SAFE_RANGE

        def head_by_head():
            outs = [_rec_core_slow(*args) for args in zip(*(_heads(t) for t in (q, k, v, b, S)))]
            return tuple(jnp.concatenate(parts, axis=1) for parts in zip(*outs))

        o, S_new = lax.cond(safe, lambda: _rec_cores_fast(q, k, v, b, S), head_by_head)
        og_ref[...] = _rec_tails(o, z, gw_ref[...]).astype(BF16)
        s_scr[...] = S_new
        safe_ref[0] = jnp.full((REC_HEADS, LANES), safe.astype(F32))

    blk = lambda w: pl.BlockSpec((REC_BLOCK, w), lambda b, j: (b * nblk + j, 0))
    st_spec = pl.BlockSpec((1, REC_DIM, D_MODEL), lambda b, j: (b * nblk + j, 0, 0))
    safe_spec = pl.BlockSpec((1, REC_HEADS, LANES), lambda b, j: (b * nblk + j, 0, 0))
    return pl.pallas_call(
        body, name="rec_fwd", grid=(batch, nblk),
        in_specs=[blk(REC_IN), _full_spec((2, D_MODEL)), _full_spec((1, D_MODEL))],
        out_specs=[blk(D_MODEL), st_spec, safe_spec],
        out_shape=[jax.ShapeDtypeStruct((batch * seq, D_MODEL), BF16),
                   jax.ShapeDtypeStruct((batch * nblk, REC_DIM, D_MODEL), F32),
                   jax.ShapeDtypeStruct((batch * nblk, REC_HEADS, LANES), F32)],
        scratch_shapes=[pltpu.VMEM((REC_DIM, D_MODEL), F32)],
        compiler_params=_cparams(2),
    )(proj, lb_logits, jnp.tile(gnorm_w, (1, REC_HEADS)))


def rec_bwd(proj, states, safe, lb_logits, gnorm_w, dog, batch, seq):
    nblk = seq // REC_BLOCK

    def body(p_ref, st_ref, safe_ref, lb_ref, gw_ref, g_ref, dp_ref, dlb_ref, dgw_ref, ds_scr):
        @pl.when((pl.program_id(0) == 0) & (pl.program_id(1) == 0))
        def _():
            dlb_ref[...] = jnp.zeros_like(dlb_ref)
            dgw_ref[...] = jnp.zeros_like(dgw_ref)

        @pl.when(pl.program_id(1) == 0)
        def _():
            ds_scr[...] = jnp.zeros_like(ds_scr)

        primals = tuple(p_ref[:, part * D_MODEL:(part + 1) * D_MODEL] for part in range(4)) + (
            st_ref[0], lb_ref[0:1, :], lb_ref[1:2, :], gw_ref[...])
        cotangents = (g_ref[...].astype(F32), ds_scr[...])

        def all_heads():
            return jax.vjp(_rec_block_fast, *primals)[1](cotangents)

        def head_by_head():
            outs = [jax.vjp(functools.partial(_rec_head, _rec_core_slow), *args)[1](cts)
                    for args, cts in zip(zip(*(_heads(t) for t in primals)), zip(*(_heads(t) for t in cotangents)))]
            return tuple(jnp.concatenate(parts, axis=1) for parts in zip(*outs))

        dqr, dfr, dv, dz, dS, dl0, dl1, dgw = lax.cond(jnp.max(safe_ref[0]) > 0.5, all_heads, head_by_head)
        for part, val in enumerate((dqr, dfr, dv, dz)):
            dp_ref[:, part * D_MODEL:(part + 1) * D_MODEL] = val.astype(BF16)
        ds_scr[...] = dS
        dlb_ref[0:1, :] += dl0
        dlb_ref[1:2, :] += dl1
        dgw_ref[...] += functools.reduce(jnp.add, _heads(dgw))

    blk = lambda w: pl.BlockSpec((REC_BLOCK, w), lambda b, j: (b * nblk + nblk - 1 - j, 0))
    st_spec = pl.BlockSpec((1, REC_DIM, D_MODEL), lambda b, j: (b * nblk + nblk - 1 - j, 0, 0))
    safe_spec = pl.BlockSpec((1, REC_HEADS, LANES), lambda b, j: (b * nblk + nblk - 1 - j, 0, 0))
    return pl.pallas_call(
        body, name="rec_bwd", grid=(batch, nblk),
        in_specs=[blk(REC_IN), st_spec, safe_spec, _full_spec((2, D_MODEL)), _full_spec((1, D_MODEL)),
                  blk(D_MODEL)],
        out_specs=[blk(REC_IN), _full_spec((2, D_MODEL)), _full_spec((1, REC_DIM))],
        out_shape=[jax.ShapeDtypeStruct((batch * seq, REC_IN), BF16), jax.ShapeDtypeStruct((2, D_MODEL), F32),
                   jax.ShapeDtypeStruct((1, REC_DIM), F32)],
        scratch_shapes=[pltpu.VMEM((REC_DIM, D_MODEL), F32)],
        compiler_params=_cparams(2),
    )(proj, states, safe, lb_logits, jnp.tile(gnorm_w, (1, REC_HEADS)), dog)


_ANY = pl.BlockSpec(memory_space=pl.ANY)


def _chip_peers():
    x, y, c = lax.axis_index("x"), lax.axis_index("y"), lax.axis_index("c")
    peers = []
    for fx, fy in ((1, 0), (0, 1), (1, 1)):
        px, py = (1 - x if fx else x), (1 - y if fy else y)
        peers.append(((px, py, c), 2 * px + py))
    return 2 * x + y, peers


def _remote(src, dst, send_sem, recv_sem, device):
    return pltpu.make_async_remote_copy(src_ref=src, dst_ref=dst, send_sem=send_sem, recv_sem=recv_sem,
                                        device_id=device, device_id_type=MESH)


N_FLIPS = N_CHIPS - 1


def _scatter_sems(n):
    return [pltpu.SemaphoreType.DMA((n * N_FLIPS,)), pltpu.SemaphoreType.DMA((n * N_FLIPS,)),
            pltpu.SemaphoreType.DMA((n,))]


def _scatter_copies(ins, outs, send_sems, recv_sems, local_sems, starting):
    me, peers = _chip_peers()
    local = [pltpu.make_async_copy(ins[k].at[me], outs[k].at[me], local_sems.at[k]) for k in range(len(ins))]
    sends, arrivals = [], []
    for k in range(len(ins)):
        for j, (device, idx) in enumerate(peers):
            sems = (send_sems.at[k * N_FLIPS + j], recv_sems.at[k * N_FLIPS + j], device)
            sends.append(_remote(ins[k].at[idx], outs[k].at[me], *sems))
            if not starting:
                arrivals.append(_remote(ins[k].at[me], outs[k].at[idx], *sems))
    return local, sends, arrivals


def _scatter_start(*refs):
    local, sends, _ = _scatter_copies(*refs, starting=True)
    for cp in local + sends:
        cp.start()


def _scatter_finish(*refs):
    local, sends, arrivals = _scatter_copies(*refs, starting=False)
    for cp in arrivals:
        cp.wait_recv()
    for cp in sends:
        cp.wait_send()
    for cp in local:
        cp.wait()


def _gather_sems(n):
    return [pltpu.SemaphoreType.DMA((n * N_FLIPS,)) for _ in range(4)] + [pltpu.SemaphoreType.DMA((n,))]


def _gather_copies(ins, outs, send_sems, recv_sems, pass_send_sems, pass_recv_sems, local_sems, starting):
    me, peers = _chip_peers()
    c = lax.axis_index("c")
    sibling = (lax.axis_index("x"), lax.axis_index("y"), 1 - c)
    local = [pltpu.make_async_copy(ins[k], outs[k].at[me], local_sems.at[k]) for k in range(len(ins))]
    sends, arrivals, passes, pass_arrivals = [], [], [], []
    for k in range(len(ins)):
        half = ins[k].shape[0] // 2
        mine, other = pl.ds(c * half, half), pl.ds((1 - c) * half, half)
        for j, (device, idx) in enumerate(peers):
            s = k * N_FLIPS + j
            sends.append(_remote(ins[k].at[mine], outs[k].at[me].at[mine], send_sems.at[s], recv_sems.at[s], device))
            if starting:
                continue
            arrived = outs[k].at[idx].at[mine]
            arrivals.append(_remote(ins[k].at[mine], arrived, send_sems.at[s], recv_sems.at[s], device))
            passes.append(_remote(arrived, arrived, pass_send_sems.at[s], pass_recv_sems.at[s], sibling))
            passed = outs[k].at[idx].at[other]
            pass_arrivals.append(_remote(passed, passed, pass_send_sems.at[s], pass_recv_sems.at[s], sibling))
    return local, sends, arrivals, passes, pass_arrivals


def _gather_start(*refs):
    local, sends, _, _, _ = _gather_copies(*refs, starting=True)
    for cp in local + sends:
        cp.start()


def _gather_finish(*refs):
    local, sends, arrivals, passes, pass_arrivals = _gather_copies(*refs, starting=False)
    for arrival, onward in zip(arrivals, passes):
        arrival.wait_recv()
        onward.start()
    for cp in pass_arrivals:
        cp.wait_recv()
    for cp in sends + passes:
        cp.wait_send()
    for cp in local:
        cp.wait()


def chip_gather(arrays):
    n = len(arrays)

    def body(*refs):
        _gather_start(refs[:n], refs[n:2 * n], *refs[2 * n:])
        _gather_finish(refs[:n], refs[n:2 * n], *refs[2 * n:])

    return pl.pallas_call(
        body, name="chip_gather", in_specs=[_ANY] * n, out_specs=[_ANY] * n,
        out_shape=[jax.ShapeDtypeStruct((N_CHIPS,) + a.shape, a.dtype) for a in arrays],
        scratch_shapes=_gather_sems(n),
    )(*arrays)


def sibling_exchange(arrays):
    n = len(arrays)

    def body(*refs):
        ins, outs = refs[:n], refs[n:2 * n]
        send_sems, recv_sems = refs[2 * n:]
        sibling = (lax.axis_index("x"), lax.axis_index("y"), 1 - lax.axis_index("c"))
        copies = [pltpu.make_async_remote_copy(src_ref=ins[k], dst_ref=outs[k], send_sem=send_sems.at[k],
                                               recv_sem=recv_sems.at[k], device_id=sibling, device_id_type=MESH)
                  for k in range(n)]
        for cp in copies:
            cp.start()
        for cp in copies:
            cp.wait()

    return pl.pallas_call(
        body, name="sibling_exchange", in_specs=[_ANY] * n, out_specs=[_ANY] * n,
        out_shape=[jax.ShapeDtypeStruct(a.shape, a.dtype) for a in arrays],
        scratch_shapes=[pltpu.SemaphoreType.DMA((n,)), pltpu.SemaphoreType.DMA((n,))],
    )(*arrays)


def all_gather_small(vec):
    def body(v_ref, out_ref, send_sems, recv_sems, local_sem):
        x, y, c = lax.axis_index("x"), lax.axis_index("y"), lax.axis_index("c")
        me = 4 * x + 2 * y + c
        local = pltpu.make_async_copy(v_ref, out_ref.at[me], local_sem)
        local.start()
        sends, recvs = [], []
        for j in range(1, N_DEV):
            px = jnp.where(j & 4, 1 - x, x)
            py = jnp.where(j & 2, 1 - y, y)
            pc = jnp.where(j & 1, 1 - c, c)
            common = dict(send_sem=send_sems.at[j - 1], recv_sem=recv_sems.at[j - 1], device_id=(px, py, pc),
                          device_id_type=MESH)
            sends.append(pltpu.make_async_remote_copy(src_ref=v_ref, dst_ref=out_ref.at[me], **common))
            recvs.append(pltpu.make_async_remote_copy(src_ref=v_ref, dst_ref=out_ref.at[4 * px + 2 * py + pc],
                                                      **common))
        for cp in sends:
            cp.start()
        for cp in recvs:
            cp.wait_recv()
        for cp in sends:
            cp.wait_send()
        local.wait()

    return pl.pallas_call(
        body, name="all_gather_small", in_specs=[_ANY], out_specs=_ANY,
        out_shape=jax.ShapeDtypeStruct((N_DEV,) + vec.shape, vec.dtype),
        scratch_shapes=[pltpu.SemaphoreType.DMA((N_DEV - 1,)), pltpu.SemaphoreType.DMA((N_DEV - 1,)),
                        pltpu.SemaphoreType.DMA],
    )(vec)


def sum_slots(stacked, tm=256):
    s, r, c = stacked.shape
    tm = min(tm, r)

    def body(in_ref, out_ref):
        acc = in_ref[0].astype(F32)
        for t in range(1, s):
            acc = acc + in_ref[t].astype(F32)
        out_ref[...] = acc

    return pl.pallas_call(
        body, name=f"sum_slots_{s}_{r}_{c}", grid=(r // tm,),
        in_specs=[pl.BlockSpec((s, tm, c), lambda i: (0, i, 0))], out_specs=_row_spec(tm, c),
        out_shape=jax.ShapeDtypeStruct((r, c), F32), compiler_params=_cparams(1),
    )(stacked)


def adamw(w, m, v, g_a, g_b=None, tm=256):
    r, c = w.shape
    tm = min(tm, r)
    two = g_b is not None

    def body(*refs):
        w_ref, m_ref, v_ref, ga_ref = refs[:4]
        g_ref, d_ref, nm_ref, nv_ref = refs[-4:]
        g = ga_ref[...] + refs[4][...] if two else ga_ref[...]
        nm = ADAM_B1 * m_ref[...] + (1.0 - ADAM_B1) * g
        nv = ADAM_B2 * v_ref[...] + (1.0 - ADAM_B2) * (g * g)
        m_hat = nm / (1.0 - ADAM_B1 ** ADAM_STEP)
        v_hat = nv / (1.0 - ADAM_B2 ** ADAM_STEP)
        g_ref[...] = g
        d_ref[...] = -ADAM_LR * (m_hat / (jnp.sqrt(v_hat) + ADAM_EPS) + ADAM_WD * w_ref[...])
        nm_ref[...] = nm
        nv_ref[...] = nv

    args = [w, m, v, g_a] + ([g_b] if two else [])
    return pl.pallas_call(
        body, name=f"adamw_{r}_{c}", grid=(r // tm,),
        in_specs=[_row_spec(tm, c)] * len(args), out_specs=[_row_spec(tm, c)] * 4,
        out_shape=[jax.ShapeDtypeStruct((r, c), F32)] * 4, compiler_params=_cparams(1),
    )(*args)


_SMALL = (("pre_norm_w", (2, D_MODEL)), ("post_norm_w", (2, D_MODEL)), ("attn_b_in", (1, ATTN_IN)),
          ("attn_sinks", (1, N_HEADS)), ("attn_b_out", (1, D_MODEL)), ("rec_lb_logits", (2, D_MODEL)),
          ("rec_gnorm_w", (1, REC_DIM)))
_SMALL_ROWS = 16


def _pack_small(parts, last_row=None):
    rows = []
    for (name, shape) in _SMALL:
        flat = parts[name].reshape(-1)
        pad = -flat.shape[0] % D_MODEL
        rows.append(jnp.pad(flat, (0, pad)).reshape(-1, D_MODEL))
    used = sum(r.shape[0] for r in rows)
    rows.append(jnp.zeros((_SMALL_ROWS - 1 - used, D_MODEL), F32))
    rows.append(jnp.zeros((1, D_MODEL), F32) if last_row is None else last_row)
    return jnp.concatenate(rows, axis=0)


def _unpack_small(packed):
    out, row = {}, 0
    for (name, shape) in _SMALL:
        size = shape[0] * shape[1]
        nrows = -(-size // D_MODEL)
        out[name] = packed[row:row + nrows].reshape(-1)[:size].reshape(shape)
        row += nrows
    return out


_CARRIED = ("rec_w_in", "rec_w_out", "attn_w_out")


_LATE = ("attn_w_out", "rec_w_in", "rec_w_out")


def local_step(x, positions, pre_norm_w, post_norm_w, attn_w_in, attn_b_in, attn_sinks, attn_w_out, attn_b_out,
               rec_w_in, rec_lb_logits, rec_gnorm_w, rec_w_out, loss_target, distributed=False):
    batch, seq, _ = x.shape
    n = batch * seq
    x0 = x.reshape(n, D_MODEL)
    tables = _rope_tables(positions)
    pre0, pre1 = pre_norm_w[0:1], pre_norm_w[1:2]
    post0, post1 = post_norm_w[0:1], post_norm_w[1:2]
    no_bias = jnp.zeros((1, D_MODEL), F32)

    h0, q, k, v, z = attn_in_proj(x0, pre0, attn_w_in, attn_b_in, tables)
    sink_tab = _sink_table(attn_sinks)
    late = (attn_w_out, rec_w_in, rec_w_out)
    og0, gathered = attn_fwd(q, k, v, z, sink_tab, batch, seq, gather=late if distributed else ())
    if distributed:
        attn_w_out, rec_w_in, rec_w_out = (g if name == "rec_w_in" else _whole_from_shards(name, g)
                                           for name, g in zip(_LATE, gathered))
    y0, x1 = out_proj(og0, attn_w_out, attn_b_out, x0, post0)

    h1, proj1 = rec_in_proj(x1, pre1, rec_w_in)
    og1, states, safe = rec_fwd(proj1, rec_lb_logits, rec_gnorm_w, batch, seq)
    y1, dx2, loss_vec = out_proj(og1, rec_w_out, no_bias, x1, post1, target=loss_target.reshape(n, D_MODEL))

    dog1, d_rec_w_out, _, d_post1 = out_proj_bwd(dx2, y1, og1, rec_w_out, post1)
    dproj1, d_lb, d_gnorm = rec_bwd(proj1, states, safe, rec_lb_logits, rec_gnorm_w, dog1, batch, seq)
    dx1, d_pre1, _ = in_proj_bwd_x(dproj1, rec_w_in, x1, pre1, dx2)
    d_rec_w_in, _ = in_proj_bwd_w(h1, dproj1, as_shards=distributed)

    dog0, d_attn_w_out, d_attn_b_out, d_post0 = out_proj_bwd(dx1, y0, og0, attn_w_out, post0)
    ready = dict(rec_w_out=d_rec_w_out, attn_w_out=d_attn_w_out)
    outgoing = [d_rec_w_in if name == "rec_w_in" else _shards_from_whole(name, ready[name]).astype(BF16)
                for name in _CARRIED] if distributed else []
    dproj0, dk, dv, d_sink_tab, arrived = attn_bwd(q, k, v, z, sink_tab, dog0, tables, batch, seq, scatter=outgoing)
    d_sinks = jnp.transpose(jnp.sum(d_sink_tab, axis=-1), (0, 2, 1)).reshape(1, N_HEADS)
    dproj0 = attn_bwd_kv(dproj0, dk, dv, tables)
    d_attn_w_in, d_attn_b_in = in_proj_bwd_w(h0, dproj0)
    last = [_shards_from_whole("attn_w_in", d_attn_w_in).astype(BF16)] if distributed else []
    dx0, d_pre0, arrived_last = in_proj_bwd_x(dproj0, attn_w_in, x0, pre0, dx1, scatter=last)

    grads = dict(
        pre_norm_w=jnp.concatenate([d_pre0, d_pre1], axis=0), post_norm_w=jnp.concatenate([d_post0, d_post1], axis=0),
        attn_w_in=d_attn_w_in, attn_b_in=d_attn_b_in, attn_sinks=d_sinks, attn_w_out=d_attn_w_out,
        attn_b_out=d_attn_b_out, rec_w_in=d_rec_w_in, rec_lb_logits=d_lb, rec_gnorm_w=d_gnorm,
        rec_w_out=d_rec_w_out)
    parts = dict(zip(_CARRIED + ("attn_w_in",), tuple(arrived) + tuple(arrived_last)))
    return loss_vec, dx0.reshape(batch, seq, D_MODEL), grads, parts


_BIG = ("attn_w_in", "attn_w_out", "rec_w_in", "rec_w_out")
_COLUMN_SHARDED = ("attn_w_in", "rec_w_in")
_ORDER = ("pre_norm_w", "post_norm_w", "attn_w_in", "attn_b_in", "attn_sinks", "attn_w_out", "attn_b_out",
          "rec_w_in", "rec_lb_logits", "rec_gnorm_w", "rec_w_out")


def _whole_from_shards(name, stacked):
    if name in _COLUMN_SHARDED:
        return jnp.transpose(stacked, (1, 0, 2)).reshape(stacked.shape[1], -1)
    return stacked.reshape(-1, stacked.shape[2])


def _shards_from_whole(name, whole):
    if name in _COLUMN_SHARDED:
        return jnp.transpose(whole.reshape(whole.shape[0], N_CHIPS, -1), (1, 0, 2))
    return whole.reshape(N_CHIPS, -1, whole.shape[1])


def kernel(x, positions, pre_norm_w, post_norm_w, attn_w_in, attn_b_in, attn_sinks, attn_w_out, attn_b_out, rec_w_in, rec_lb_logits, rec_gnorm_w, rec_w_out, loss_target, m_pre_norm_w, m_post_norm_w, m_attn_w_in, m_attn_b_in, m_attn_sinks, m_attn_w_out, m_attn_b_out, m_rec_w_in, m_rec_lb_logits, m_rec_gnorm_w, m_rec_w_out, v_pre_norm_w, v_post_norm_w, v_attn_w_in, v_attn_b_in, v_attn_sinks, v_attn_w_out, v_attn_b_out, v_rec_w_in, v_rec_lb_logits, v_rec_gnorm_w, v_rec_w_out):
    w = dict(pre_norm_w=pre_norm_w, post_norm_w=post_norm_w, attn_w_in=attn_w_in, attn_b_in=attn_b_in,
             attn_sinks=attn_sinks, attn_w_out=attn_w_out, attn_b_out=attn_b_out, rec_w_in=rec_w_in,
             rec_lb_logits=rec_lb_logits, rec_gnorm_w=rec_gnorm_w, rec_w_out=rec_w_out)
    m = dict(pre_norm_w=m_pre_norm_w, post_norm_w=m_post_norm_w, attn_w_in=m_attn_w_in, attn_b_in=m_attn_b_in,
             attn_sinks=m_attn_sinks, attn_w_out=m_attn_w_out, attn_b_out=m_attn_b_out, rec_w_in=m_rec_w_in,
             rec_lb_logits=m_rec_lb_logits, rec_gnorm_w=m_rec_gnorm_w, rec_w_out=m_rec_w_out)
    v = dict(pre_norm_w=v_pre_norm_w, post_norm_w=v_post_norm_w, attn_w_in=v_attn_w_in, attn_b_in=v_attn_b_in,
             attn_sinks=v_attn_sinks, attn_w_out=v_attn_w_out, attn_b_out=v_attn_b_out, rec_w_in=v_rec_w_in,
             rec_lb_logits=v_rec_lb_logits, rec_gnorm_w=v_rec_gnorm_w, rec_w_out=v_rec_w_out)

    shards = {name: w[name][0] for name in _BIG}
    sent = {name: shards[name].astype(BF16) for name in _BIG}
    attn_w_in_whole = _whole_from_shards("attn_w_in", chip_gather([sent["attn_w_in"]])[0])

    loss_vec, grad_x, grads, parts = local_step(
        x, positions, pre_norm_w, post_norm_w, attn_w_in_whole, attn_b_in, attn_sinks, sent["attn_w_out"],
        attn_b_out, sent["rec_w_in"], rec_lb_logits, rec_gnorm_w, sent["rec_w_out"], loss_target, distributed=True)

    plane_sums = [sum_slots(parts[name]) for name in _BIG]
    other_sums = sibling_exchange(plane_sums)
    out_g, out_d, out_m, out_v = {}, {}, {}, {}
    for name, mine, other in zip(_BIG, plane_sums, other_sums):
        g, d, nm, nv = adamw(shards[name], m[name][0], v[name][0], mine, other)
        out_g[name], out_d[name], out_m[name], out_v[name] = g[None], d[None], nm[None], nv[None]

    small_sum = sum_slots(all_gather_small(_pack_small(grads, last_row=loss_vec)))
    loss = jnp.sum(small_sum[_SMALL_ROWS - 1]) * (0.5 / D_MODEL)
    packed = adamw(_pack_small(w), _pack_small(m), _pack_small(v), small_sum)
    for dst, val in zip((out_g, out_d, out_m, out_v), packed):
        dst.update(_unpack_small(val))

    return (loss, grad_x, *[out_g[n] for n in _ORDER], *[out_d[n] for n in _ORDER],
            *[out_m[n] for n in _ORDER], *[out_v[n] for n in _ORDER])
```

```python
import functools

import jax
import jax.numpy as jnp
from jax import lax
from jax.experimental import pallas as pl
from jax.experimental.pallas import tpu as pltpu

F32 = jnp.float32
BF16 = jnp.bfloat16
MESH = pl.DeviceIdType.MESH

D_MODEL = 1024
HEAD_DIM = 64
N_HEADS = 16
N_KV_HEADS = 2
GROUP = N_HEADS // N_KV_HEADS
KV_WIDTH = N_KV_HEADS * HEAD_DIM
ATTN_IN = 2 * D_MODEL + 2 * KV_WIDTH
ATTN_BLOCK = 128
ROPE_THETA = 500000.0
ROPE_DIM = HEAD_DIM // 4
REC_HEADS = 8
REC_DIM = 128
REC_IN = 4 * D_MODEL
REC_BLOCK = 128
DIAG = 8
NORM_EPS = 1e-6
N_CHIPS = 4
N_DEV = 8
LANES = 128

ADAM_LR = 0.001
ADAM_B1 = 0.9
ADAM_B2 = 0.999
ADAM_EPS = 1e-08
ADAM_WD = 0.01
ADAM_STEP = 10

VMEM_LIMIT = 56 * 1024 * 1024


def _cparams(n_axes):
    return pltpu.CompilerParams(dimension_semantics=("arbitrary",) * n_axes, vmem_limit_bytes=VMEM_LIMIT)


def _dot(a, b, contract):
    return lax.dot_general(a.astype(BF16), b.astype(BF16), (contract, ((), ())), preferred_element_type=F32)


_NN = ((1,), (0,))
_NT = ((1,), (1,))
_TN = ((0,), (0,))


@jax.custom_vjp
def mm_nn(a, b):
    return _dot(a, b, _NN)


mm_nn.defvjp(lambda a, b: (_dot(a, b, _NN), (a, b)),
             lambda res, g: (_dot(g, res[1], _NT), _dot(res[0], g, _TN)))


@jax.custom_vjp
def mm_nt(a, b):
    return _dot(a, b, _NT)


mm_nt.defvjp(lambda a, b: (_dot(a, b, _NT), (a, b)),
             lambda res, g: (_dot(g, res[1], _NN), _dot(g, res[0], _TN)))


@jax.custom_vjp
def mm_tn(a, b):
    return _dot(a, b, _TN)


mm_tn.defvjp(lambda a, b: (_dot(a, b, _TN), (a, b)),
             lambda res, g: (_dot(res[1], g, _NT), _dot(res[0], g, _NN)))


def _tri_dot(x, lower):
    n = x.shape[0]
    r = lax.broadcasted_iota(jnp.int32, (n, n), 0)
    c = lax.broadcasted_iota(jnp.int32, (n, n), 1)
    tri = ((c <= r) if lower else (c >= r)).astype(BF16)
    hi = x.astype(BF16)
    rest = x - hi.astype(F32)
    mid = rest.astype(BF16)
    lo = (rest - mid.astype(F32)).astype(BF16)
    dot = lambda p: lax.dot_general(tri, p, (_NN, ((), ())), preferred_element_type=F32)
    return (dot(lo) + dot(mid)) + dot(hi)


@jax.custom_vjp
def cumsum_rows(x):
    return _tri_dot(x, True)


cumsum_rows.defvjp(lambda x: (cumsum_rows(x), None), lambda _, g: (_tri_dot(g, False),))


@functools.partial(jax.custom_vjp, nondiff_argnums=(1,))
def roll_sub(x, d):
    return pltpu.roll(x, d, 1) if d else x


roll_sub.defvjp(lambda x, d: (roll_sub(x, d), None),
                lambda d, _, g: (roll_sub(g, (DIAG - d) % DIAG),))


def sigmoid(x):
    return 1.0 / (1.0 + jnp.exp(-x))


@jax.custom_vjp
def silu(x):
    return x * sigmoid(x)


def _silu_fwd(x):
    s = sigmoid(x)
    return x * s, (x, s)


silu.defvjp(_silu_fwd, lambda res, g: (g * (res[1] * (1.0 + res[0] * (1.0 - res[1]))),))


F32_TINY = 1.17549435e-38


def sigmoid_pair(x):
    e = jnp.exp(-jnp.abs(x))
    r = 1.0 / (1.0 + e)
    er = e * r
    pos = x >= 0.0
    return jnp.where(pos, r, er), jnp.where(pos, er, r)


def _forget_fwd(x, a):
    lb, one_m_lb = sigmoid_pair(a)
    sp, sn = sigmoid_pair(x)
    f = lb + one_m_lb * sp
    k = one_m_lb * sn
    return (jnp.log(jnp.maximum(f, F32_TINY)), k), (sp, sn, f, k, lb, one_m_lb)


def _forget_bwd(res, g):
    sp, sn, f, k, lb, one_m_lb = res
    g_lf, g_k = g
    t = jnp.where(f >= F32_TINY, g_lf / jnp.maximum(f, F32_TINY), 0.0) - g_k
    return (k * sp) * t, jnp.sum(sn * t, axis=0, keepdims=True) * (lb * one_m_lb)


@jax.custom_vjp
def forget_gate(x, a):
    return _forget_fwd(x, a)[0]


forget_gate.defvjp(_forget_fwd, _forget_bwd)


@jax.custom_vjp
def decayed(x, e):
    return (x * jnp.exp(e)).astype(BF16).astype(F32)


def _decayed_fwd(x, e):
    y = decayed(x, e)
    return y, (y, e)


decayed.defvjp(_decayed_fwd, lambda res, g: (g * jnp.exp(res[1]), g * res[0]))


def _row(x, r):
    shape = x.shape

    @jax.custom_vjp
    def take(x):
        return x[r:r + 1, :]

    take.defvjp(lambda x: (x[r:r + 1, :], None),
                lambda _, g: (jnp.where(lax.broadcasted_iota(jnp.int32, shape, 0) == r, g, 0.0),))
    return take(x)


def _rms(x):
    return lax.rsqrt(jnp.mean(x * x, axis=-1, keepdims=True) + NORM_EPS)


def _attn_group(qs, k_a, v_a, k_b, v_b, zs, sink_a, sink_b, bias, at_sink=None):
    def half(kh, vh, sink):
        s = mm_nn(qs, kh) + bias
        if at_sink is None:
            m = jnp.maximum(jnp.max(s, axis=-1, keepdims=True), jnp.max(sink, axis=-1, keepdims=True))
            p = jnp.exp(s - lax.stop_gradient(m))
            own = jnp.sum(jnp.exp(sink - lax.stop_gradient(m)), axis=-1, keepdims=True) * (1.0 / LANES)
            return mm_nt(p * (1.0 / (jnp.sum(p, axis=-1, keepdims=True) + own)), vh)
        s = jnp.where(at_sink, jnp.concatenate([sink, sink], axis=1), s)
        p = jnp.exp(s - jnp.max(s, axis=-1, keepdims=True))
        return mm_nt(jnp.where(at_sink, 0.0, p * (1.0 / jnp.sum(p, axis=-1, keepdims=True))), vh)

    return (half(k_a, v_a, sink_a) + half(k_b, v_b, sink_b)) * silu(zs)


SAFE_RANGE = 80.0


def _rec_front(qr, fr, l0, l1):
    lf, k = forget_gate(fr, l1 - l0)
    return silu(qr), k, lf


def _rec_tail(o, z, gw):
    return o * _rms(o) * gw * silu(z)


def _rec_margin(b):
    R = b.shape[0]
    mid, last = _row(b, R // 2 - 1), _row(b, R - 1)
    return jnp.minimum(mid, last - mid)


def _heads(x):
    w = x.shape[1] // REC_HEADS
    return [x[:, h * w:(h + 1) * w] for h in range(REC_HEADS)]


def _hdot(a, b, contract):
    return jnp.concatenate([_dot(ah, bh, contract) for ah, bh in zip(_heads(a), _heads(b))], axis=1)


@jax.custom_vjp
def hmm_nn(a, b):
    return _hdot(a, b, _NN)


hmm_nn.defvjp(lambda a, b: (_hdot(a, b, _NN), (a, b)),
              lambda res, g: (_hdot(g, res[1], _NT), _hdot(res[0], g, _TN)))


@jax.custom_vjp
def hmm_nt(a, b):
    return _hdot(a, b, _NT)


hmm_nt.defvjp(lambda a, b: (_hdot(a, b, _NT), (a, b)),
              lambda res, g: (_hdot(g, res[1], _NN), _hdot(g, res[0], _TN)))


@jax.custom_vjp
def hmm_tn(a, b):
    return _hdot(a, b, _TN)


hmm_tn.defvjp(lambda a, b: (_hdot(a, b, _TN), (a, b)),
              lambda res, g: (_hdot(res[1], g, _NT), _hdot(res[0], g, _NN)))


def _head_sums(x):
    return jnp.concatenate([jnp.broadcast_to(jnp.sum(xh, axis=-1, keepdims=True), xh.shape) for xh in _heads(x)],
                           axis=1)


@jax.custom_vjp
def head_sum(x):
    return _head_sums(x)


head_sum.defvjp(lambda x: (_head_sums(x), None), lambda _, g: (_head_sums(g),))


def _rec_cores_fast(q, k, v, b, S):
    R = q.shape[0]
    ri = lax.broadcasted_iota(jnp.int32, (R, REC_HEADS * R), 0)
    ci = lax.broadcasted_iota(jnp.int32, (R, REC_HEADS * R), 1) % R
    d = b - _row(b, R // 2 - 1)
    sc = jnp.where(ci < ri, hmm_nt(decayed(q, d), decayed(k, -d)), 0.0)
    o = hmm_nt(q * jnp.exp(b), S) + hmm_nn(sc, v) + head_sum(q * k) * v
    b_last = _row(b, R - 1)
    return o, S * jnp.exp(b_last) + hmm_tn(v, k * jnp.exp(b_last - b))


def _rec_tails(o, z, gw):
    return o * lax.rsqrt(head_sum(o * o) * (1.0 / REC_DIM) + NORM_EPS) * gw * silu(z)


def _rec_block_fast(qr, fr, v, z, S, l0, l1, gw):
    lf, k = forget_gate(fr, l1 - l0)
    o, S_new = _rec_cores_fast(silu(qr), k, v, cumsum_rows(lf), S)
    return _rec_tails(o, z, gw), S_new


def _rec_core_slow(q, k, v, b, S):
    R = q.shape[0]
    rows = lax.broadcasted_iota(jnp.int32, (R, REC_DIM), 0)

    o = mm_nt(q * jnp.exp(jnp.minimum(b, 0.0)), S)

    ri = lax.broadcasted_iota(jnp.int32, (R, R), 0)
    ci = lax.broadcasted_iota(jnp.int32, (R, R), 1)
    sc = jnp.zeros((R, R), F32)
    w = R
    while w > DIAG:
        h = w // 2
        b3 = b.reshape(R // w, w, REC_DIM)
        rin = lax.broadcasted_iota(jnp.int32, (R // w, w, REC_DIM), 1)
        mid = jnp.sum(jnp.where(rin == h - 1, b3, 0.0), axis=1, keepdims=True)
        fac = jnp.exp(jnp.minimum(jnp.where(rin >= h, b3 - mid, mid - b3), 0.0)).reshape(R, REC_DIM)
        upper = (rows % w) >= h
        s_w = mm_nt(jnp.where(upper, q * fac, 0.0), jnp.where(upper, 0.0, k * fac))
        sc = sc + jnp.where((ri // w) == (ci // w), s_w, 0.0)
        w = h
    o = o + mm_nn(sc, v)

    g = R // DIAG
    q3, k3, v3, b3 = (t.reshape(g, DIAG, REC_DIM) for t in (q, k, v, b))
    rin = lax.broadcasted_iota(jnp.int32, (g, DIAG, 1), 1)
    od = jnp.zeros((g, DIAG, REC_DIM), F32)
    for d in range(DIAG):
        e = jnp.exp(jnp.minimum(b3 - roll_sub(b3, d), 0.0))
        sd = jnp.sum(q3 * roll_sub(k3, d) * e, axis=-1, keepdims=True)
        od = od + jnp.where(rin >= d, sd, 0.0) * roll_sub(v3, d)
    o = o + od.reshape(R, REC_DIM)

    b_last = _row(b, R - 1)
    return o, S * jnp.exp(jnp.minimum(b_last, 0.0)) + mm_tn(v, k * jnp.exp(jnp.minimum(b_last - b, 0.0)))


def _rec_head(core, qr, fr, v, z, S, l0, l1, gw):
    q, k, lf = _rec_front(qr, fr, l0, l1)
    o, S_new = core(q, k, v, cumsum_rows(lf), S)
    return _rec_tail(o, z, gw), S_new


def _rope_tables(positions):
    half = ROPE_DIM // 2
    inv_freq = ROPE_THETA ** (-(jnp.arange(half, dtype=F32) * 2.0 / ROPE_DIM))
    rest = jnp.zeros((HEAD_DIM - ROPE_DIM,), F32)
    ones, zeros = jnp.ones((half,), F32), jnp.zeros((half,), F32)
    per_lane = lambda first, second: jnp.tile(jnp.concatenate([first, second, rest]), LANES // HEAD_DIM)[None, :]
    ang = positions.astype(F32).reshape(-1, 1) * per_lane(inv_freq, inv_freq)
    sin = jnp.sin(ang)
    return jnp.cos(ang), sin * per_lane(zeros, ones), sin * per_lane(-ones, zeros)


def _rope(x, cos_t, sin_a, sin_b):
    half = ROPE_DIM // 2
    return x * cos_t + pltpu.roll(x, half, 1) * sin_a + pltpu.roll(x, LANES - half, 1) * sin_b


def _rope_transposed(g, cos_t, sin_a, sin_b):
    half = ROPE_DIM // 2
    return g * cos_t + pltpu.roll(g * sin_a, LANES - half, 1) + pltpu.roll(g * sin_b, half, 1)


def _row_spec(tm, width):
    return pl.BlockSpec((tm, width), lambda i: (i, 0))


def _full_spec(shape):
    return pl.BlockSpec(shape, lambda *_: (0,) * len(shape))


def attn_in_proj(x, w_pre, w_in, b_in, tables, tm=1024):
    n = x.shape[0]
    tm = min(tm, n)

    def body(x_ref, wp_ref, w_ref, b_ref, c_ref, sa_ref, sb_ref, h_ref, q_ref, k_ref, v_ref, z_ref):
        xv = x_ref[...]
        h = (xv * _rms(xv) * wp_ref[...]).astype(BF16)
        h_ref[...] = h
        proj = jnp.dot(h, w_ref[...], preferred_element_type=F32) + b_ref[...]
        tabs = (c_ref[...], sa_ref[...], sb_ref[...])
        for s in range(D_MODEL // LANES):
            sl = slice(s * LANES, (s + 1) * LANES)
            q_ref[:, sl] = _rope(proj[:, sl] * (HEAD_DIM ** -0.5), *tabs).astype(BF16)
        k_ref[...] = _rope(proj[:, D_MODEL:D_MODEL + KV_WIDTH], *tabs).astype(BF16)
        v_ref[...] = proj[:, D_MODEL + KV_WIDTH:D_MODEL + 2 * KV_WIDTH].astype(BF16)
        z_ref[...] = proj[:, D_MODEL + 2 * KV_WIDTH:]

    return pl.pallas_call(
        body, name="attn_in_proj", grid=(n // tm,),
        in_specs=[_row_spec(tm, D_MODEL), _full_spec((1, D_MODEL)), _full_spec((D_MODEL, ATTN_IN)),
                  _full_spec((1, ATTN_IN))] + [_row_spec(tm, LANES)] * 3,
        out_specs=[_row_spec(tm, D_MODEL), _row_spec(tm, D_MODEL), _row_spec(tm, KV_WIDTH),
                   _row_spec(tm, KV_WIDTH), _row_spec(tm, D_MODEL)],
        out_shape=[jax.ShapeDtypeStruct((n, D_MODEL), BF16), jax.ShapeDtypeStruct((n, D_MODEL), BF16),
                   jax.ShapeDtypeStruct((n, KV_WIDTH), BF16), jax.ShapeDtypeStruct((n, KV_WIDTH), BF16),
                   jax.ShapeDtypeStruct((n, D_MODEL), F32)],
        compiler_params=_cparams(1),
    )(x, w_pre, w_in, b_in, *tables)


def _column_blocks(w):
    if len(w.shape) == 2:
        return [slice(0, w.shape[1])], lambda ref, s: ref[...]
    width = w.shape[2]
    return [slice(s * width, (s + 1) * width) for s in range(w.shape[0])], lambda ref, s: ref[s]


def rec_in_proj(x, w_pre, w_in, tm=512):
    n = x.shape[0]
    tm = min(tm, n)
    columns, block = _column_blocks(w_in)

    def body(x_ref, wp_ref, w_ref, h_ref, p_ref):
        xv = x_ref[...]
        h = (xv * _rms(xv) * wp_ref[...]).astype(BF16)
        h_ref[...] = h
        for s, cols in enumerate(columns):
            p_ref[:, cols] = jnp.dot(h, block(w_ref, s), preferred_element_type=F32)

    return pl.pallas_call(
        body, name="rec_in_proj", grid=(n // tm,),
        in_specs=[_row_spec(tm, D_MODEL), _full_spec((1, D_MODEL)), _full_spec(w_in.shape)],
        out_specs=[_row_spec(tm, D_MODEL), _row_spec(tm, REC_IN)],
        out_shape=[jax.ShapeDtypeStruct((n, D_MODEL), BF16), jax.ShapeDtypeStruct((n, REC_IN), F32)],
        compiler_params=_cparams(1),
    )(x, w_pre, w_in)


def out_proj(og, w_out, b_out, x_res, w_post, target=None, tm=1024):
    n = og.shape[0]
    tm = min(tm, n)
    with_loss = target is not None

    def body(*refs):
        if with_loss:
            og_ref, w_ref, b_ref, x_ref, wp_ref, t_ref, y_ref, dx_ref, l_ref = refs
        else:
            og_ref, w_ref, b_ref, x_ref, wp_ref, y_ref, xo_ref = refs
        y = jnp.dot(og_ref[...], w_ref[...], preferred_element_type=F32) + b_ref[...]
        y_ref[...] = y.astype(BF16)
        xo = x_ref[...] + y * _rms(y) * wp_ref[...]
        if with_loss:
            err = xo - t_ref[...]
            dx_ref[...] = err * (1.0 / D_MODEL)

            @pl.when(pl.program_id(0) == 0)
            def _():
                l_ref[...] = jnp.zeros_like(l_ref)

            l_ref[...] += jnp.sum(err * err, axis=0, keepdims=True)
        else:
            xo_ref[...] = xo

    in_specs = [_row_spec(tm, D_MODEL), _full_spec((D_MODEL, D_MODEL)), _full_spec((1, D_MODEL)),
                _row_spec(tm, D_MODEL), _full_spec((1, D_MODEL))]
    out_specs = [_row_spec(tm, D_MODEL), _row_spec(tm, D_MODEL)]
    out_shape = [jax.ShapeDtypeStruct((n, D_MODEL), BF16), jax.ShapeDtypeStruct((n, D_MODEL), F32)]
    args = [og, w_out, b_out, x_res, w_post]
    if with_loss:
        in_specs.append(_row_spec(tm, D_MODEL))
        out_specs.append(_full_spec((1, D_MODEL)))
        out_shape.append(jax.ShapeDtypeStruct((1, D_MODEL), F32))
        args.append(target)
    return pl.pallas_call(
        body, name="out_proj_loss" if with_loss else "out_proj", grid=(n // tm,),
        in_specs=in_specs, out_specs=out_specs, out_shape=out_shape, compiler_params=_cparams(1),
    )(*args)


def out_proj_bwd(dxo, y, og, w_out, w_post, tm=1024):
    n = og.shape[0]
    tm = min(tm, n)

    def body(g_ref, y_ref, og_ref, w_ref, wp_ref, dog_ref, dw_ref, db_ref, dwp_ref):
        @pl.when(pl.program_id(0) == 0)
        def _():
            dw_ref[...] = jnp.zeros_like(dw_ref)
            db_ref[...] = jnp.zeros_like(db_ref)
            dwp_ref[...] = jnp.zeros_like(dwp_ref)

        g, y = g_ref[...], y_ref[...].astype(F32)
        rstd = _rms(y)
        yn = y * rstd
        gw = g * wp_ref[...]
        dwp_ref[...] += jnp.sum(g * yn, axis=0, keepdims=True)
        dy = rstd * (gw - yn * jnp.mean(gw * yn, axis=-1, keepdims=True))
        db_ref[...] += jnp.sum(dy, axis=0, keepdims=True)
        dyb = dy.astype(BF16)
        dog_ref[...] = _dot(dyb, w_ref[...], _NT).astype(BF16)
        dw_ref[...] += _dot(og_ref[...], dyb, _TN)

    return pl.pallas_call(
        body, name="out_proj_bwd", grid=(n // tm,),
        in_specs=[_row_spec(tm, D_MODEL), _row_spec(tm, D_MODEL), _row_spec(tm, D_MODEL),
                  _full_spec((D_MODEL, D_MODEL)), _full_spec((1, D_MODEL))],
        out_specs=[_row_spec(tm, D_MODEL), _full_spec((D_MODEL, D_MODEL)), _full_spec((1, D_MODEL)),
                   _full_spec((1, D_MODEL))],
        out_shape=[jax.ShapeDtypeStruct((n, D_MODEL), BF16), jax.ShapeDtypeStruct((D_MODEL, D_MODEL), F32),
                   jax.ShapeDtypeStruct((1, D_MODEL), F32), jax.ShapeDtypeStruct((1, D_MODEL), F32)],
        compiler_params=_cparams(1),
    )(dxo, y, og, w_out, w_post)


def in_proj_bwd_x(dproj, w_in, x, w_pre, dxo, tm=512, scatter=()):
    n, p = dproj.shape
    tm = min(tm if p > ATTN_IN else 2 * tm, n)
    steps = n // tm
    ns = len(scatter)
    columns, block = _column_blocks(w_in)

    def body(*refs):
        dp_ref, w_ref, x_ref, wp_ref, g_ref = refs[:5]
        dx_ref, dwp_ref = refs[5 + ns:7 + ns]
        exchange = (refs[5:5 + ns], refs[7 + ns:7 + 2 * ns]) + tuple(refs[7 + 2 * ns:])

        @pl.when(pl.program_id(0) == 0)
        def _():
            dwp_ref[...] = jnp.zeros_like(dwp_ref)
            if ns:
                _scatter_start(*exchange)

        dh = functools.reduce(jnp.add, [_dot(dp_ref[:, cols], block(w_ref, s), _NT)
                                        for s, cols in enumerate(columns)])
        xv = x_ref[...]
        rstd = _rms(xv)
        xn = xv * rstd
        gw = dh * wp_ref[...]
        dwp_ref[...] += jnp.sum(dh * xn, axis=0, keepdims=True)
        dx_ref[...] = rstd * (gw - xn * jnp.mean(gw * xn, axis=-1, keepdims=True)) + g_ref[...]

        if ns:
            @pl.when(pl.program_id(0) == steps - 1)
            def _():
                _scatter_finish(*exchange)

    out = pl.pallas_call(
        body, name=f"in_proj_bwd_x_{p}", grid=(steps,),
        in_specs=[_row_spec(tm, p), _full_spec(w_in.shape), _row_spec(tm, D_MODEL), _full_spec((1, D_MODEL)),
                  _row_spec(tm, D_MODEL)] + [_ANY] * ns,
        out_specs=[_row_spec(tm, D_MODEL), _full_spec((1, D_MODEL))] + [_ANY] * ns,
        out_shape=[jax.ShapeDtypeStruct((n, D_MODEL), F32), jax.ShapeDtypeStruct((1, D_MODEL), F32)]
        + [jax.ShapeDtypeStruct(a.shape, a.dtype) for a in scatter],
        scratch_shapes=_scatter_sems(ns) if ns else [],
        compiler_params=_cparams(1),
    )(dproj, w_in, x, w_pre, dxo, *scatter)
    return out[0], out[1], out[2:]


def in_proj_bwd_w(h, dproj, tm=1024, as_shards=False):
    n, p = dproj.shape
    chunk = p // (4 if p % 4096 == 0 else 3)
    tm = min(tm, n)
    steps = n // tm
    shard = p // N_CHIPS

    def body(h_ref, dp_ref, dw_ref, db_ref, acc_scr, sem, *staging):
        i = pl.program_id(0)

        @pl.when(i == 0)
        def _():
            acc_scr[...] = jnp.zeros_like(acc_scr)
            db_ref[...] = jnp.zeros_like(db_ref)

        ht = h_ref[...].T
        for c0 in range(0, p, chunk):
            dp = dp_ref[:, c0:c0 + chunk]
            acc_scr[:, c0:c0 + chunk] += jnp.dot(ht, dp, preferred_element_type=F32)
            db_ref[:, c0:c0 + chunk] += jnp.sum(dp.astype(F32), axis=0, keepdims=True)

        @pl.when(i == steps - 1)
        def _():
            if as_shards:
                for s in range(N_CHIPS):
                    staging[0][...] = acc_scr[:, s * shard:(s + 1) * shard].astype(BF16)
                    out = pltpu.make_async_copy(staging[0], dw_ref.at[s], sem)
                    out.start()
                    out.wait()
            else:
                out = pltpu.make_async_copy(acc_scr, dw_ref, sem)
                out.start()
                out.wait()

    dw_shape = jax.ShapeDtypeStruct((N_CHIPS, D_MODEL, shard), BF16) if as_shards else (
        jax.ShapeDtypeStruct((D_MODEL, p), F32))
    return pl.pallas_call(
        body, name=f"in_proj_bwd_w_{p}", grid=(steps,),
        in_specs=[_row_spec(tm, D_MODEL), _row_spec(tm, p)],
        out_specs=[_ANY, _full_spec((1, p))],
        out_shape=[dw_shape, jax.ShapeDtypeStruct((1, p), F32)],
        scratch_shapes=[pltpu.VMEM((D_MODEL, p), F32), pltpu.SemaphoreType.DMA]
        + ([pltpu.VMEM((D_MODEL, shard), BF16)] if as_shards else []),
        compiler_params=_cparams(1),
    )(h, dproj)


PAIRS = GROUP // 2
GROUP_ROWS = PAIRS * ATTN_BLOCK
MASKED = -1e30


def _kv_windows(k_ref, v_ref, i):
    ps = pl.multiple_of(jnp.maximum(i - 1, 0) * ATTN_BLOCK, ATTN_BLOCK)
    cs = pl.multiple_of(i * ATTN_BLOCK, ATTN_BLOCK)
    kw = jnp.concatenate([k_ref[pl.ds(ps, ATTN_BLOCK), :], k_ref[pl.ds(cs, ATTN_BLOCK), :]], axis=0)
    vw = jnp.concatenate([v_ref[pl.ds(ps, ATTN_BLOCK), :], v_ref[pl.ds(cs, ATTN_BLOCK), :]], axis=0)
    return kw.astype(F32).T, vw.astype(F32).T, ps, cs


def _low_rows(shape):
    return lax.broadcasted_iota(jnp.int32, shape, 0) < HEAD_DIM


def _spread(w, kvh):
    low = _low_rows(w.shape)
    swapped = pltpu.roll(w, HEAD_DIM, 0)
    if kvh == 0:
        return jnp.where(low, w, 0.0), jnp.where(low, 0.0, swapped)
    return jnp.where(low, swapped, 0.0), jnp.where(low, 0.0, w)


def _unspread(d_a, d_b, kvh):
    low = _low_rows(d_a.shape)
    if kvh == 0:
        return jnp.where(low, d_a + pltpu.roll(d_b, HEAD_DIM, 0), 0.0)
    return jnp.where(low, 0.0, pltpu.roll(d_a, HEAD_DIM, 0) + d_b)


def _stack_pairs(ref, kvh):
    return jnp.concatenate([ref[:, (kvh * PAIRS + j) * LANES:(kvh * PAIRS + j + 1) * LANES] for j in range(PAIRS)],
                           axis=0)


def _fill_bias(bias_scr):
    shape = (GROUP_ROWS, 2 * ATTN_BLOCK)
    r = lax.broadcasted_iota(jnp.int32, shape, 0) % ATTN_BLOCK
    c = lax.broadcasted_iota(jnp.int32, shape, 1)
    in_cur = (c >= ATTN_BLOCK) & ((c - ATTN_BLOCK) <= r)
    in_prev = (c < ATTN_BLOCK) & (c > r)
    bias_scr[0] = jnp.where(in_cur, 0.0, MASKED)
    bias_scr[1] = jnp.where(in_cur | in_prev, 0.0, MASKED)
    bias_scr[2] = jnp.where(c == r, 1.0, 0.0)


N_BIAS_TABLES = 3


def _sink_table(sinks):
    t = jnp.transpose(sinks.reshape(N_KV_HEADS, PAIRS, 2), (0, 2, 1))
    return jnp.broadcast_to(t[:, :, :, None, None], (N_KV_HEADS, 2, PAIRS, ATTN_BLOCK, LANES)).reshape(
        N_KV_HEADS, 2, GROUP_ROWS, LANES)


def attn_fwd(q, k, v, z, sink_tab, batch, seq, gather=()):
    nb = seq // ATTN_BLOCK
    ng = len(gather)

    def body(*refs):
        q_ref, k_ref, v_ref, z_ref, s_ref = refs[:5]
        og_ref, bias_scr = refs[5 + ng], refs[6 + 2 * ng]
        exchange = (refs[5:5 + ng], refs[6 + ng:6 + 2 * ng]) + tuple(refs[7 + 2 * ng:])
        b, i = pl.program_id(0), pl.program_id(1)

        @pl.when((b == 0) & (i == 0))
        def _():
            _fill_bias(bias_scr)
            if ng:
                _gather_start(*exchange)

        kw, vw, _, _ = _kv_windows(k_ref, v_ref, i)
        bias, at_sink = bias_scr[jnp.minimum(i, 1)], bias_scr[2] > 0.5
        for kvh in range(N_KV_HEADS):
            k_a, k_b = _spread(kw, kvh)
            v_a, v_b = _spread(vw, kvh)
            og = _attn_group(_stack_pairs(q_ref, kvh), k_a, v_a, k_b, v_b, _stack_pairs(z_ref, kvh),
                             s_ref[kvh, 0], s_ref[kvh, 1], bias, at_sink)
            for j in range(PAIRS):
                og_ref[:, (kvh * PAIRS + j) * LANES:(kvh * PAIRS + j + 1) * LANES] = (
                    og[j * ATTN_BLOCK:(j + 1) * ATTN_BLOCK].astype(BF16))

        if ng:
            @pl.when((b == batch - 1) & (i == nb - 1))
            def _():
                _gather_finish(*exchange)

    blk = lambda w: pl.BlockSpec((ATTN_BLOCK, w), lambda b, i: (b * nb + i, 0))
    seq_spec = pl.BlockSpec((seq, KV_WIDTH), lambda b, i: (b, 0))
    out = pl.pallas_call(
        body, name="attn_fwd", grid=(batch, nb),
        in_specs=[blk(D_MODEL), seq_spec, seq_spec, blk(D_MODEL), _full_spec(sink_tab.shape)] + [_ANY] * ng,
        out_specs=[blk(D_MODEL)] + [_ANY] * ng,
        out_shape=[jax.ShapeDtypeStruct((batch * seq, D_MODEL), BF16)]
        + [jax.ShapeDtypeStruct((N_CHIPS,) + a.shape, a.dtype) for a in gather],
        scratch_shapes=[pltpu.VMEM((N_BIAS_TABLES, GROUP_ROWS, 2 * ATTN_BLOCK), F32)] + (_gather_sems(ng) if ng else []),
        compiler_params=_cparams(2),
    )(q, k, v, z, sink_tab, *gather)
    return out[0], out[1:]


def attn_bwd(q, k, v, z, sink_tab, dog, tables, batch, seq, scatter=()):
    nb = seq // ATTN_BLOCK
    ns = len(scatter)

    def body(*refs):
        q_ref, k_ref, v_ref, z_ref, s_ref, g_ref, c_ref, sa_ref, sb_ref = refs[:9]
        dp_ref, dk_ref, dv_ref, ds_ref = refs[9 + ns:13 + ns]
        bias_scr = refs[13 + 2 * ns]
        exchange = (refs[9:9 + ns], refs[13 + ns:13 + 2 * ns]) + tuple(refs[14 + 2 * ns:])
        b, i = pl.program_id(0), pl.program_id(1)

        @pl.when((b == 0) & (i == 0))
        def _():
            _fill_bias(bias_scr)
            ds_ref[...] = jnp.zeros_like(ds_ref)
            if ns:
                _scatter_start(*exchange)

        @pl.when(i == 0)
        def _():
            dk_ref[...] = jnp.zeros_like(dk_ref)
            dv_ref[...] = jnp.zeros_like(dv_ref)

        kw, vw, ps, cs = _kv_windows(k_ref, v_ref, i)
        bias = bias_scr[jnp.minimum(i, 1)]
        tabs = (c_ref[...], sa_ref[...], sb_ref[...])
        dkw = jnp.zeros_like(kw)
        dvw = jnp.zeros_like(vw)
        for kvh in range(N_KV_HEADS):
            k_a, k_b = _spread(kw, kvh)
            v_a, v_b = _spread(vw, kvh)
            _, vjp = jax.vjp(functools.partial(_attn_group, bias=bias), _stack_pairs(q_ref, kvh).astype(F32),
                             k_a, v_a, k_b, v_b, _stack_pairs(z_ref, kvh), s_ref[kvh, 0], s_ref[kvh, 1])
            dqs, dk_a, dv_a, dk_b, dv_b, dzs, ds_a, ds_b = vjp(_stack_pairs(g_ref, kvh).astype(F32))
            dkw = dkw + _unspread(dk_a, dk_b, kvh)
            dvw = dvw + _unspread(dv_a, dv_b, kvh)
            ds_ref[kvh, 0] += jnp.sum(ds_a.reshape(PAIRS, ATTN_BLOCK, LANES), axis=1)
            ds_ref[kvh, 1] += jnp.sum(ds_b.reshape(PAIRS, ATTN_BLOCK, LANES), axis=1)
            for j in range(PAIRS):
                rows = slice(j * ATTN_BLOCK, (j + 1) * ATTN_BLOCK)
                col = (kvh * PAIRS + j) * LANES
                dp_ref[:, col:col + LANES] = _rope_transposed(dqs[rows] * (HEAD_DIM ** -0.5), *tabs).astype(BF16)
                zc = D_MODEL + 2 * KV_WIDTH + col
                dp_ref[:, zc:zc + LANES] = dzs[rows].astype(BF16)
        dp_ref[:, D_MODEL:D_MODEL + 2 * KV_WIDTH] = jnp.zeros((ATTN_BLOCK, 2 * KV_WIDTH), BF16)
        dk_ref[:, pl.ds(ps, ATTN_BLOCK)] += dkw[:, :ATTN_BLOCK]
        dk_ref[:, pl.ds(cs, ATTN_BLOCK)] += dkw[:, ATTN_BLOCK:]
        dv_ref[:, pl.ds(ps, ATTN_BLOCK)] += dvw[:, :ATTN_BLOCK]
        dv_ref[:, pl.ds(cs, ATTN_BLOCK)] += dvw[:, ATTN_BLOCK:]

        if ns:
            @pl.when((b == batch - 1) & (i == nb - 1))
            def _():
                _scatter_finish(*exchange)

    blk = lambda w: pl.BlockSpec((ATTN_BLOCK, w), lambda b, i: (b * nb + i, 0))
    seq_spec = pl.BlockSpec((seq, KV_WIDTH), lambda b, i: (b, 0))
    seq_spec_t = pl.BlockSpec((KV_WIDTH, seq), lambda b, i: (0, b))
    n = batch * seq
    ds_shape = (N_KV_HEADS, 2, PAIRS, LANES)
    out = pl.pallas_call(
        body, name="attn_bwd", grid=(batch, nb),
        in_specs=[blk(D_MODEL), seq_spec, seq_spec, blk(D_MODEL), _full_spec(sink_tab.shape), blk(D_MODEL)]
        + [blk(LANES)] * 3 + [_ANY] * ns,
        out_specs=[blk(ATTN_IN), seq_spec_t, seq_spec_t, _full_spec(ds_shape)] + [_ANY] * ns,
        out_shape=[jax.ShapeDtypeStruct((n, ATTN_IN), BF16), jax.ShapeDtypeStruct((KV_WIDTH, n), F32),
                   jax.ShapeDtypeStruct((KV_WIDTH, n), F32), jax.ShapeDtypeStruct(ds_shape, F32)]
        + [jax.ShapeDtypeStruct(a.shape, a.dtype) for a in scatter],
        scratch_shapes=[pltpu.VMEM((N_BIAS_TABLES, GROUP_ROWS, 2 * ATTN_BLOCK), F32)] + (_scatter_sems(ns) if ns else []),
        compiler_params=_cparams(2),
    )(q, k, v, z, sink_tab, dog, *tables, *scatter)
    return out[0], out[1], out[2], out[3], out[4:]


def attn_bwd_kv(dproj, dk_t, dv_t, tables, tm=512):
    n = dproj.shape[0]

    def body(dp_in_ref, dk_ref, dv_ref, c_ref, sa_ref, sb_ref, dp_ref):
        del dp_in_ref
        dp_ref[:, :KV_WIDTH] = _rope_transposed(dk_ref[...].T, c_ref[...], sa_ref[...], sb_ref[...]).astype(BF16)
        dp_ref[:, KV_WIDTH:] = dv_ref[...].T.astype(BF16)

    kv_cols = pl.BlockSpec((tm, 2 * KV_WIDTH), lambda i: (i, D_MODEL // (2 * KV_WIDTH)))
    col_spec = pl.BlockSpec((KV_WIDTH, tm), lambda i: (0, i))
    return pl.pallas_call(
        body, name="attn_bwd_kv", grid=(n // tm,),
        in_specs=[kv_cols, col_spec, col_spec] + [_row_spec(tm, LANES)] * 3,
        out_specs=kv_cols, out_shape=jax.ShapeDtypeStruct(dproj.shape, BF16),
        input_output_aliases={0: 0}, compiler_params=_cparams(1),
    )(dproj, dk_t, dv_t, *tables)


def rec_fwd(proj, lb_logits, gnorm_w, batch, seq):
    nblk = seq // REC_BLOCK

    def body(p_ref, lb_ref, gw_ref, og_ref, st_ref, safe_ref, s_scr):
        @pl.when(pl.program_id(1) == 0)
        def _():
            s_scr[...] = jnp.zeros_like(s_scr)

        S = s_scr[...]
        st_ref[0] = S
        qr, fr, v, z = (p_ref[:, part * D_MODEL:(part + 1) * D_MODEL] for part in range(4))
        lf, k = forget_gate(fr, lb_ref[1:2, :] - lb_ref[0:1, :])
        q, b = silu(qr), cumsum_rows(lf)
        safe = jnp.min(_rec_margin(b)) >= -SAFE_RANGE

        gate = gw_ref[...] * silu(z)
        safe_ref[0] = jnp.full((REC_HEADS, LANES), safe.astype(F32))

        def store(o, S_new):
            og_ref[...] = (o * lax.rsqrt(head_sum(o * o) * (1.0 / REC_DIM) + NORM_EPS) * gate).astype(BF16)
            s_scr[...] = S_new

        @pl.when(safe)
        def _():
            store(*_rec_cores_fast(q, k, v, b, S))

        @pl.when(jnp.logical_not(safe))
        def _():
            outs = [_rec_core_slow(*args) for args in zip(*(_heads(t) for t in (q, k, v, b, S)))]
            store(*(jnp.concatenate(parts, axis=1) for parts in zip(*outs)))

    blk = lambda w: pl.BlockSpec((REC_BLOCK, w), lambda b, j: (b * nblk + j, 0))
    st_spec = pl.BlockSpec((1, REC_DIM, D_MODEL), lambda b, j: (b * nblk + j, 0, 0))
    safe_spec = pl.BlockSpec((1, REC_HEADS, LANES), lambda b, j: (b * nblk + j, 0, 0))
    return pl.pallas_call(
        body, name="rec_fwd", grid=(batch, nblk),
        in_specs=[blk(REC_IN), _full_spec((2, D_MODEL)), _full_spec((1, D_MODEL))],
        out_specs=[blk(D_MODEL), st_spec, safe_spec],
        out_shape=[jax.ShapeDtypeStruct((batch * seq, D_MODEL), BF16),
                   jax.ShapeDtypeStruct((batch * nblk, REC_DIM, D_MODEL), F32),
                   jax.ShapeDtypeStruct((batch * nblk, REC_HEADS, LANES), F32)],
        scratch_shapes=[pltpu.VMEM((REC_DIM, D_MODEL), F32)],
        compiler_params=_cparams(2),
    )(proj, lb_logits, jnp.tile(gnorm_w, (1, REC_HEADS)))


def rec_bwd(proj, states, safe, lb_logits, gnorm_w, dog, batch, seq):
    nblk = seq // REC_BLOCK

    def body(p_ref, st_ref, safe_ref, lb_ref, gw_ref, g_ref, dp_ref, dlb_ref, dgw_ref, ds_scr):
        @pl.when((pl.program_id(0) == 0) & (pl.program_id(1) == 0))
        def _():
            dlb_ref[...] = jnp.zeros_like(dlb_ref)
            dgw_ref[...] = jnp.zeros_like(dgw_ref)

        @pl.when(pl.program_id(1) == 0)
        def _():
            ds_scr[...] = jnp.zeros_like(ds_scr)

        def load():
            primals = tuple(p_ref[:, part * D_MODEL:(part + 1) * D_MODEL] for part in range(4)) + (
                st_ref[0], lb_ref[0:1, :], lb_ref[1:2, :], gw_ref[...])
            return primals, (g_ref[...].astype(F32), ds_scr[...])

        def store(dqr, dfr, dv, dz, dS, dl0, dl1, dgw):
            for part, val in enumerate((dqr, dfr, dv, dz)):
                dp_ref[:, part * D_MODEL:(part + 1) * D_MODEL] = val.astype(BF16)
            ds_scr[...] = dS
            dlb_ref[0:1, :] += dl0
            dlb_ref[1:2, :] += dl1
            dgw_ref[...] += functools.reduce(jnp.add, _heads(dgw))

        fast = jnp.max(safe_ref[0]) > 0.5

        @pl.when(fast)
        def _():
            primals, cotangents = load()
            store(*jax.vjp(_rec_block_fast, *primals)[1](cotangents))

        @pl.when(jnp.logical_not(fast))
        def _():
            primals, cotangents = load()
            outs = [jax.vjp(functools.partial(_rec_head, _rec_core_slow), *args)[1](cts)
                    for args, cts in zip(zip(*(_heads(t) for t in primals)), zip(*(_heads(t) for t in cotangents)))]
            store(*(jnp.concatenate(parts, axis=1) for parts in zip(*outs)))

    blk = lambda w: pl.BlockSpec((REC_BLOCK, w), lambda b, j: (b * nblk + nblk - 1 - j, 0))
    st_spec = pl.BlockSpec((1, REC_DIM, D_MODEL), lambda b, j: (b * nblk + nblk - 1 - j, 0, 0))
    safe_spec = pl.BlockSpec((1, REC_HEADS, LANES), lambda b, j: (b * nblk + nblk - 1 - j, 0, 0))
    return pl.pallas_call(
        body, name="rec_bwd", grid=(batch, nblk),
        in_specs=[blk(REC_IN), st_spec, safe_spec, _full_spec((2, D_MODEL)), _full_spec((1, D_MODEL)),
                  blk(D_MODEL)],
        out_specs=[blk(REC_IN), _full_spec((2, D_MODEL)), _full_spec((1, REC_DIM))],
        out_shape=[jax.ShapeDtypeStruct((batch * seq, REC_IN), BF16), jax.ShapeDtypeStruct((2, D_MODEL), F32),
                   jax.ShapeDtypeStruct((1, REC_DIM), F32)],
        scratch_shapes=[pltpu.VMEM((REC_DIM, D_MODEL), F32)],
        compiler_params=_cparams(2),
    )(proj, states, safe, lb_logits, jnp.tile(gnorm_w, (1, REC_HEADS)), dog)


_ANY = pl.BlockSpec(memory_space=pl.ANY)


def _chip_peers():
    x, y, c = lax.axis_index("x"), lax.axis_index("y"), lax.axis_index("c")
    peers = []
    for fx, fy in ((1, 0), (0, 1), (1, 1)):
        px, py = (1 - x if fx else x), (1 - y if fy else y)
        peers.append(((px, py, c), 2 * px + py))
    return 2 * x + y, peers


def _remote(src, dst, send_sem, recv_sem, device):
    return pltpu.make_async_remote_copy(src_ref=src, dst_ref=dst, send_sem=send_sem, recv_sem=recv_sem,
                                        device_id=device, device_id_type=MESH)


N_FLIPS = N_CHIPS - 1


def _scatter_sems(n):
    return [pltpu.SemaphoreType.DMA((n * N_FLIPS,)), pltpu.SemaphoreType.DMA((n * N_FLIPS,)),
            pltpu.SemaphoreType.DMA((n,))]


def _scatter_copies(ins, outs, send_sems, recv_sems, local_sems, starting):
    me, peers = _chip_peers()
    local = [pltpu.make_async_copy(ins[k].at[me], outs[k].at[me], local_sems.at[k]) for k in range(len(ins))]
    sends, arrivals = [], []
    for k in range(len(ins)):
        for j, (device, idx) in enumerate(peers):
            sems = (send_sems.at[k * N_FLIPS + j], recv_sems.at[k * N_FLIPS + j], device)
            sends.append(_remote(ins[k].at[idx], outs[k].at[me], *sems))
            if not starting:
                arrivals.append(_remote(ins[k].at[me], outs[k].at[idx], *sems))
    return local, sends, arrivals


def _scatter_start(*refs):
    local, sends, _ = _scatter_copies(*refs, starting=True)
    for cp in local + sends:
        cp.start()


def _scatter_finish(*refs):
    local, sends, arrivals = _scatter_copies(*refs, starting=False)
    for cp in arrivals:
        cp.wait_recv()
    for cp in sends:
        cp.wait_send()
    for cp in local:
        cp.wait()


def _gather_sems(n):
    return [pltpu.SemaphoreType.DMA((n * N_FLIPS,)) for _ in range(4)] + [pltpu.SemaphoreType.DMA((n,))]


def _gather_copies(ins, outs, send_sems, recv_sems, pass_send_sems, pass_recv_sems, local_sems, starting):
    me, peers = _chip_peers()
    c = lax.axis_index("c")
    sibling = (lax.axis_index("x"), lax.axis_index("y"), 1 - c)
    local = [pltpu.make_async_copy(ins[k], outs[k].at[me], local_sems.at[k]) for k in range(len(ins))]
    sends, arrivals, passes, pass_arrivals = [], [], [], []
    for k in range(len(ins)):
        half = ins[k].shape[0] // 2
        mine, other = pl.ds(c * half, half), pl.ds((1 - c) * half, half)
        for j, (device, idx) in enumerate(peers):
            s = k * N_FLIPS + j
            sends.append(_remote(ins[k].at[mine], outs[k].at[me].at[mine], send_sems.at[s], recv_sems.at[s], device))
            if starting:
                continue
            arrived = outs[k].at[idx].at[mine]
            arrivals.append(_remote(ins[k].at[mine], arrived, send_sems.at[s], recv_sems.at[s], device))
            passes.append(_remote(arrived, arrived, pass_send_sems.at[s], pass_recv_sems.at[s], sibling))
            passed = outs[k].at[idx].at[other]
            pass_arrivals.append(_remote(passed, passed, pass_send_sems.at[s], pass_recv_sems.at[s], sibling))
    return local, sends, arrivals, passes, pass_arrivals


def _gather_start(*refs):
    local, sends, _, _, _ = _gather_copies(*refs, starting=True)
    for cp in local + sends:
        cp.start()


def _gather_finish(*refs):
    local, sends, arrivals, passes, pass_arrivals = _gather_copies(*refs, starting=False)
    for arrival, onward in zip(arrivals, passes):
        arrival.wait_recv()
        onward.start()
    for cp in pass_arrivals:
        cp.wait_recv()
    for cp in sends + passes:
        cp.wait_send()
    for cp in local:
        cp.wait()


def chip_gather(arrays):
    n = len(arrays)

    def body(*refs):
        _gather_start(refs[:n], refs[n:2 * n], *refs[2 * n:])
        _gather_finish(refs[:n], refs[n:2 * n], *refs[2 * n:])

    return pl.pallas_call(
        body, name="chip_gather", in_specs=[_ANY] * n, out_specs=[_ANY] * n,
        out_shape=[jax.ShapeDtypeStruct((N_CHIPS,) + a.shape, a.dtype) for a in arrays],
        scratch_shapes=_gather_sems(n),
    )(*arrays)


def sibling_exchange(arrays):
    n = len(arrays)

    def body(*refs):
        ins, outs = refs[:n], refs[n:2 * n]
        send_sems, recv_sems = refs[2 * n:]
        sibling = (lax.axis_index("x"), lax.axis_index("y"), 1 - lax.axis_index("c"))
        copies = [pltpu.make_async_remote_copy(src_ref=ins[k], dst_ref=outs[k], send_sem=send_sems.at[k],
                                               recv_sem=recv_sems.at[k], device_id=sibling, device_id_type=MESH)
                  for k in range(n)]
        for cp in copies:
            cp.start()
        for cp in copies:
            cp.wait()

    return pl.pallas_call(
        body, name="sibling_exchange", in_specs=[_ANY] * n, out_specs=[_ANY] * n,
        out_shape=[jax.ShapeDtypeStruct(a.shape, a.dtype) for a in arrays],
        scratch_shapes=[pltpu.SemaphoreType.DMA((n,)), pltpu.SemaphoreType.DMA((n,))],
    )(*arrays)


def all_gather_small(vec):
    def body(v_ref, out_ref, send_sems, recv_sems, local_sem):
        x, y, c = lax.axis_index("x"), lax.axis_index("y"), lax.axis_index("c")
        me = 4 * x + 2 * y + c
        local = pltpu.make_async_copy(v_ref, out_ref.at[me], local_sem)
        local.start()
        sends, recvs = [], []
        for j in range(1, N_DEV):
            px = jnp.where(j & 4, 1 - x, x)
            py = jnp.where(j & 2, 1 - y, y)
            pc = jnp.where(j & 1, 1 - c, c)
            common = dict(send_sem=send_sems.at[j - 1], recv_sem=recv_sems.at[j - 1], device_id=(px, py, pc),
                          device_id_type=MESH)
            sends.append(pltpu.make_async_remote_copy(src_ref=v_ref, dst_ref=out_ref.at[me], **common))
            recvs.append(pltpu.make_async_remote_copy(src_ref=v_ref, dst_ref=out_ref.at[4 * px + 2 * py + pc],
                                                      **common))
        for cp in sends:
            cp.start()
        for cp in recvs:
            cp.wait_recv()
        for cp in sends:
            cp.wait_send()
        local.wait()

    return pl.pallas_call(
        body, name="all_gather_small", in_specs=[_ANY], out_specs=_ANY,
        out_shape=jax.ShapeDtypeStruct((N_DEV,) + vec.shape, vec.dtype),
        scratch_shapes=[pltpu.SemaphoreType.DMA((N_DEV - 1,)), pltpu.SemaphoreType.DMA((N_DEV - 1,)),
                        pltpu.SemaphoreType.DMA],
    )(vec)


def sum_slots(stacked, tm=256):
    s, r, c = stacked.shape
    tm = min(tm, r)

    def body(in_ref, out_ref):
        acc = in_ref[0].astype(F32)
        for t in range(1, s):
            acc = acc + in_ref[t].astype(F32)
        out_ref[...] = acc

    return pl.pallas_call(
        body, name=f"sum_slots_{s}_{r}_{c}", grid=(r // tm,),
        in_specs=[pl.BlockSpec((s, tm, c), lambda i: (0, i, 0))], out_specs=_row_spec(tm, c),
        out_shape=jax.ShapeDtypeStruct((r, c), F32), compiler_params=_cparams(1),
    )(stacked)


def adamw(w, m, v, g_a, g_b=None, tm=256):
    r, c = w.shape
    tm = min(tm, r)
    two = g_b is not None

    def body(*refs):
        w_ref, m_ref, v_ref, ga_ref = refs[:4]
        g_ref, d_ref, nm_ref, nv_ref = refs[-4:]
        g = ga_ref[...] + refs[4][...] if two else ga_ref[...]
        nm = ADAM_B1 * m_ref[...] + (1.0 - ADAM_B1) * g
        nv = ADAM_B2 * v_ref[...] + (1.0 - ADAM_B2) * (g * g)
        m_hat = nm / (1.0 - ADAM_B1 ** ADAM_STEP)
        v_hat = nv / (1.0 - ADAM_B2 ** ADAM_STEP)
        g_ref[...] = g
        d_ref[...] = -ADAM_LR * (m_hat / (jnp.sqrt(v_hat) + ADAM_EPS) + ADAM_WD * w_ref[...])
        nm_ref[...] = nm
        nv_ref[...] = nv

    args = [w, m, v, g_a] + ([g_b] if two else [])
    return pl.pallas_call(
        body, name=f"adamw_{r}_{c}", grid=(r // tm,),
        in_specs=[_row_spec(tm, c)] * len(args), out_specs=[_row_spec(tm, c)] * 4,
        out_shape=[jax.ShapeDtypeStruct((r, c), F32)] * 4, compiler_params=_cparams(1),
    )(*args)


_SMALL = (("pre_norm_w", (2, D_MODEL)), ("post_norm_w", (2, D_MODEL)), ("attn_b_in", (1, ATTN_IN)),
          ("attn_sinks", (1, N_HEADS)), ("attn_b_out", (1, D_MODEL)), ("rec_lb_logits", (2, D_MODEL)),
          ("rec_gnorm_w", (1, REC_DIM)))
_SMALL_ROWS = 16


def _pack_small(parts, last_row=None):
    rows = []
    for (name, shape) in _SMALL:
        flat = parts[name].reshape(-1)
        pad = -flat.shape[0] % D_MODEL
        rows.append(jnp.pad(flat, (0, pad)).reshape(-1, D_MODEL))
    used = sum(r.shape[0] for r in rows)
    rows.append(jnp.zeros((_SMALL_ROWS - 1 - used, D_MODEL), F32))
    rows.append(jnp.zeros((1, D_MODEL), F32) if last_row is None else last_row)
    return jnp.concatenate(rows, axis=0)


def _unpack_small(packed):
    out, row = {}, 0
    for (name, shape) in _SMALL:
        size = shape[0] * shape[1]
        nrows = -(-size // D_MODEL)
        out[name] = packed[row:row + nrows].reshape(-1)[:size].reshape(shape)
        row += nrows
    return out


_CARRIED = ("rec_w_in", "rec_w_out", "attn_w_out")


_LATE = ("attn_w_out", "rec_w_in", "rec_w_out")


def local_step(x, positions, pre_norm_w, post_norm_w, attn_w_in, attn_b_in, attn_sinks, attn_w_out, attn_b_out,
               rec_w_in, rec_lb_logits, rec_gnorm_w, rec_w_out, loss_target, distributed=False):
    batch, seq, _ = x.shape
    n = batch * seq
    x0 = x.reshape(n, D_MODEL)
    tables = _rope_tables(positions)
    pre0, pre1 = pre_norm_w[0:1], pre_norm_w[1:2]
    post0, post1 = post_norm_w[0:1], post_norm_w[1:2]
    no_bias = jnp.zeros((1, D_MODEL), F32)

    h0, q, k, v, z = attn_in_proj(x0, pre0, attn_w_in, attn_b_in, tables)
    sink_tab = _sink_table(attn_sinks)
    late = (attn_w_out, rec_w_in, rec_w_out)
    og0, gathered = attn_fwd(q, k, v, z, sink_tab, batch, seq, gather=late if distributed else ())
    if distributed:
        attn_w_out, rec_w_in, rec_w_out = (g if name == "rec_w_in" else _whole_from_shards(name, g)
                                           for name, g in zip(_LATE, gathered))
    y0, x1 = out_proj(og0, attn_w_out, attn_b_out, x0, post0)

    h1, proj1 = rec_in_proj(x1, pre1, rec_w_in)
    og1, states, safe = rec_fwd(proj1, rec_lb_logits, rec_gnorm_w, batch, seq)
    y1, dx2, loss_vec = out_proj(og1, rec_w_out, no_bias, x1, post1, target=loss_target.reshape(n, D_MODEL))

    dog1, d_rec_w_out, _, d_post1 = out_proj_bwd(dx2, y1, og1, rec_w_out, post1)
    dproj1, d_lb, d_gnorm = rec_bwd(proj1, states, safe, rec_lb_logits, rec_gnorm_w, dog1, batch, seq)
    dx1, d_pre1, _ = in_proj_bwd_x(dproj1, rec_w_in, x1, pre1, dx2)
    d_rec_w_in, _ = in_proj_bwd_w(h1, dproj1, as_shards=distributed)

    dog0, d_attn_w_out, d_attn_b_out, d_post0 = out_proj_bwd(dx1, y0, og0, attn_w_out, post0)
    ready = dict(rec_w_out=d_rec_w_out, attn_w_out=d_attn_w_out)
    outgoing = [d_rec_w_in if name == "rec_w_in" else _shards_from_whole(name, ready[name]).astype(BF16)
                for name in _CARRIED] if distributed else []
    dproj0, dk, dv, d_sink_tab, arrived = attn_bwd(q, k, v, z, sink_tab, dog0, tables, batch, seq, scatter=outgoing)
    d_sinks = jnp.transpose(jnp.sum(d_sink_tab, axis=-1), (0, 2, 1)).reshape(1, N_HEADS)
    dproj0 = attn_bwd_kv(dproj0, dk, dv, tables)
    d_attn_w_in, d_attn_b_in = in_proj_bwd_w(h0, dproj0)
    last = [_shards_from_whole("attn_w_in", d_attn_w_in).astype(BF16)] if distributed else []
    dx0, d_pre0, arrived_last = in_proj_bwd_x(dproj0, attn_w_in, x0, pre0, dx1, scatter=last)

    grads = dict(
        pre_norm_w=jnp.concatenate([d_pre0, d_pre1], axis=0), post_norm_w=jnp.concatenate([d_post0, d_post1], axis=0),
        attn_w_in=d_attn_w_in, attn_b_in=d_attn_b_in, attn_sinks=d_sinks, attn_w_out=d_attn_w_out,
        attn_b_out=d_attn_b_out, rec_w_in=d_rec_w_in, rec_lb_logits=d_lb, rec_gnorm_w=d_gnorm,
        rec_w_out=d_rec_w_out)
    parts = dict(zip(_CARRIED + ("attn_w_in",), tuple(arrived) + tuple(arrived_last)))
    return loss_vec, dx0.reshape(batch, seq, D_MODEL), grads, parts


_BIG = ("attn_w_in", "attn_w_out", "rec_w_in", "rec_w_out")
_COLUMN_SHARDED = ("attn_w_in", "rec_w_in")
_ORDER = ("pre_norm_w", "post_norm_w", "attn_w_in", "attn_b_in", "attn_sinks", "attn_w_out", "attn_b_out",
          "rec_w_in", "rec_lb_logits", "rec_gnorm_w", "rec_w_out")


def _whole_from_shards(name, stacked):
    if name in _COLUMN_SHARDED:
        return jnp.transpose(stacked, (1, 0, 2)).reshape(stacked.shape[1], -1)
    return stacked.reshape(-1, stacked.shape[2])


def _shards_from_whole(name, whole):
    if name in _COLUMN_SHARDED:
        return jnp.transpose(whole.reshape(whole.shape[0], N_CHIPS, -1), (1, 0, 2))
    return whole.reshape(N_CHIPS, -1, whole.shape[1])


def kernel(x, positions, pre_norm_w, post_norm_w, attn_w_in, attn_b_in, attn_sinks, attn_w_out, attn_b_out, rec_w_in, rec_lb_logits, rec_gnorm_w, rec_w_out, loss_target, m_pre_norm_w, m_post_norm_w, m_attn_w_in, m_attn_b_in, m_attn_sinks, m_attn_w_out, m_attn_b_out, m_rec_w_in, m_rec_lb_logits, m_rec_gnorm_w, m_rec_w_out, v_pre_norm_w, v_post_norm_w, v_attn_w_in, v_attn_b_in, v_attn_sinks, v_attn_w_out, v_attn_b_out, v_rec_w_in, v_rec_lb_logits, v_rec_gnorm_w, v_rec_w_out):
    w = dict(pre_norm_w=pre_norm_w, post_norm_w=post_norm_w, attn_w_in=attn_w_in, attn_b_in=attn_b_in,
             attn_sinks=attn_sinks, attn_w_out=attn_w_out, attn_b_out=attn_b_out, rec_w_in=rec_w_in,
             rec_lb_logits=rec_lb_logits, rec_gnorm_w=rec_gnorm_w, rec_w_out=rec_w_out)
    m = dict(pre_norm_w=m_pre_norm_w, post_norm_w=m_post_norm_w, attn_w_in=m_attn_w_in, attn_b_in=m_attn_b_in,
             attn_sinks=m_attn_sinks, attn_w_out=m_attn_w_out, attn_b_out=m_attn_b_out, rec_w_in=m_rec_w_in,
             rec_lb_logits=m_rec_lb_logits, rec_gnorm_w=m_rec_gnorm_w, rec_w_out=m_rec_w_out)
    v = dict(pre_norm_w=v_pre_norm_w, post_norm_w=v_post_norm_w, attn_w_in=v_attn_w_in, attn_b_in=v_attn_b_in,
             attn_sinks=v_attn_sinks, attn_w_out=v_attn_w_out, attn_b_out=v_attn_b_out, rec_w_in=v_rec_w_in,
             rec_lb_logits=v_rec_lb_logits, rec_gnorm_w=v_rec_gnorm_w, rec_w_out=v_rec_w_out)

    shards = {name: w[name][0] for name in _BIG}
    sent = {name: shards[name].astype(BF16) for name in _BIG}
    attn_w_in_whole = _whole_from_shards("attn_w_in", chip_gather([sent["attn_w_in"]])[0])

    loss_vec, grad_x, grads, parts = local_step(
        x, positions, pre_norm_w, post_norm_w, attn_w_in_whole, attn_b_in, attn_sinks, sent["attn_w_out"],
        attn_b_out, sent["rec_w_in"], rec_lb_logits, rec_gnorm_w, sent["rec_w_out"], loss_target, distributed=True)

    plane_sums = [sum_slots(parts[name]) for name in _BIG]
    other_sums = sibling_exchange(plane_sums)
    out_g, out_d, out_m, out_v = {}, {}, {}, {}
    for name, mine, other in zip(_BIG, plane_sums, other_sums):
        g, d, nm, nv = adamw(shards[name], m[name][0], v[name][0], mine, other)
        out_g[name], out_d[name], out_m[name], out_v[name] = g[None], d[None], nm[None], nv[None]

    small_sum = sum_slots(all_gather_small(_pack_small(grads, last_row=loss_vec)))
    loss = jnp.sum(small_sum[_SMALL_ROWS - 1]) * (0.5 / D_MODEL)
    packed = adamw(_pack_small(w), _pack_small(m), _pack_small(v), small_sum)
    for dst, val in zip((out_g, out_d, out_m, out_v), packed):
        dst.update(_unpack_small(val))

    return (loss, grad_x, *[out_g[n] for n in _ORDER], *[out_d[n] for n in _ORDER],
            *[out_m[n] for n in _ORDER], *[out_v[n] for n in _ORDER])
```

```python
import functools

import jax
import jax.numpy as jnp
from jax import lax
from jax.experimental import pallas as pl
from jax.experimental.pallas import tpu as pltpu

F32 = jnp.float32
BF16 = jnp.bfloat16
MESH = pl.DeviceIdType.MESH

D_MODEL = 1024
HEAD_DIM = 64
N_HEADS = 16
N_KV_HEADS = 2
GROUP = N_HEADS // N_KV_HEADS
KV_WIDTH = N_KV_HEADS * HEAD_DIM
ATTN_IN = 2 * D_MODEL + 2 * KV_WIDTH
ATTN_BLOCK = 128
ROPE_THETA = 500000.0
ROPE_DIM = HEAD_DIM // 4
REC_HEADS = 8
REC_DIM = 128
REC_IN = 4 * D_MODEL
REC_BLOCK = 128
DIAG = 8
NORM_EPS = 1e-6
N_CHIPS = 4
N_DEV = 8
LANES = 128

ADAM_LR = 0.001
ADAM_B1 = 0.9
ADAM_B2 = 0.999
ADAM_EPS = 1e-08
ADAM_WD = 0.01
ADAM_STEP = 10

VMEM_LIMIT = 56 * 1024 * 1024


def _cparams(n_axes):
    return pltpu.CompilerParams(dimension_semantics=("arbitrary",) * n_axes, vmem_limit_bytes=VMEM_LIMIT)


def _dot(a, b, contract):
    return lax.dot_general(a.astype(BF16), b.astype(BF16), (contract, ((), ())), preferred_element_type=F32)


_NN = ((1,), (0,))
_NT = ((1,), (1,))
_TN = ((0,), (0,))


@jax.custom_vjp
def mm_nn(a, b):
    return _dot(a, b, _NN)


mm_nn.defvjp(lambda a, b: (_dot(a, b, _NN), (a, b)),
             lambda res, g: (_dot(g, res[1], _NT), _dot(res[0], g, _TN)))


@jax.custom_vjp
def mm_nt(a, b):
    return _dot(a, b, _NT)


mm_nt.defvjp(lambda a, b: (_dot(a, b, _NT), (a, b)),
             lambda res, g: (_dot(g, res[1], _NN), _dot(g, res[0], _TN)))


@jax.custom_vjp
def mm_tn(a, b):
    return _dot(a, b, _TN)


mm_tn.defvjp(lambda a, b: (_dot(a, b, _TN), (a, b)),
             lambda res, g: (_dot(res[1], g, _NT), _dot(res[0], g, _NN)))


def _tri_dot(x, lower):
    n = x.shape[0]
    r = lax.broadcasted_iota(jnp.int32, (n, n), 0)
    c = lax.broadcasted_iota(jnp.int32, (n, n), 1)
    tri = ((c <= r) if lower else (c >= r)).astype(BF16)
    hi = x.astype(BF16)
    rest = x - hi.astype(F32)
    mid = rest.astype(BF16)
    lo = (rest - mid.astype(F32)).astype(BF16)
    dot = lambda p: lax.dot_general(tri, p, (_NN, ((), ())), preferred_element_type=F32)
    return (dot(lo) + dot(mid)) + dot(hi)


@jax.custom_vjp
def cumsum_rows(x):
    return _tri_dot(x, True)


cumsum_rows.defvjp(lambda x: (cumsum_rows(x), None), lambda _, g: (_tri_dot(g, False),))


@functools.partial(jax.custom_vjp, nondiff_argnums=(1,))
def roll_sub(x, d):
    return pltpu.roll(x, d, 1) if d else x


roll_sub.defvjp(lambda x, d: (roll_sub(x, d), None),
                lambda d, _, g: (roll_sub(g, (DIAG - d) % DIAG),))


def sigmoid(x):
    return 1.0 / (1.0 + jnp.exp(-x))


@jax.custom_vjp
def silu(x):
    return x * sigmoid(x)


def _silu_fwd(x):
    s = sigmoid(x)
    return x * s, (x, s)


silu.defvjp(_silu_fwd, lambda res, g: (g * (res[1] * (1.0 + res[0] * (1.0 - res[1]))),))


F32_TINY = 1.17549435e-38


def sigmoid_pair(x):
    e = jnp.exp(-jnp.abs(x))
    r = 1.0 / (1.0 + e)
    er = e * r
    pos = x >= 0.0
    return jnp.where(pos, r, er), jnp.where(pos, er, r)


def _forget_fwd(x, a):
    lb, one_m_lb = sigmoid_pair(a)
    sp, sn = sigmoid_pair(x)
    f = lb + one_m_lb * sp
    k = one_m_lb * sn
    return (jnp.log(jnp.maximum(f, F32_TINY)), k), (sp, sn, f, k, lb, one_m_lb)


def _forget_bwd(res, g):
    sp, sn, f, k, lb, one_m_lb = res
    g_lf, g_k = g
    t = jnp.where(f >= F32_TINY, g_lf / jnp.maximum(f, F32_TINY), 0.0) - g_k
    return (k * sp) * t, jnp.sum(sn * t, axis=0, keepdims=True) * (lb * one_m_lb)


@jax.custom_vjp
def forget_gate(x, a):
    return _forget_fwd(x, a)[0]


forget_gate.defvjp(_forget_fwd, _forget_bwd)


@jax.custom_vjp
def decayed(x, e):
    return (x * jnp.exp(e)).astype(BF16).astype(F32)


def _decayed_fwd(x, e):
    y = decayed(x, e)
    return y, (y, e)


decayed.defvjp(_decayed_fwd, lambda res, g: (g * jnp.exp(res[1]), g * res[0]))


def _row(x, r):
    shape = x.shape

    @jax.custom_vjp
    def take(x):
        return x[r:r + 1, :]

    take.defvjp(lambda x: (x[r:r + 1, :], None),
                lambda _, g: (jnp.where(lax.broadcasted_iota(jnp.int32, shape, 0) == r, g, 0.0),))
    return take(x)


def _rms(x):
    return lax.rsqrt(jnp.mean(x * x, axis=-1, keepdims=True) + NORM_EPS)


def _attn_group(qs, k_a, v_a, k_b, v_b, zs, sink_a, sink_b, bias, at_sink=None):
    def half(kh, vh, sink):
        s = mm_nn(qs, kh) + bias
        if at_sink is None:
            m = jnp.maximum(jnp.max(s, axis=-1, keepdims=True), jnp.max(sink, axis=-1, keepdims=True))
            p = jnp.exp(s - lax.stop_gradient(m))
            own = jnp.sum(jnp.exp(sink - lax.stop_gradient(m)), axis=-1, keepdims=True) * (1.0 / LANES)
            return mm_nt(p * (1.0 / (jnp.sum(p, axis=-1, keepdims=True) + own)), vh)
        s = jnp.where(at_sink, jnp.concatenate([sink, sink], axis=1), s)
        p = jnp.exp(s - jnp.max(s, axis=-1, keepdims=True))
        return mm_nt(jnp.where(at_sink, 0.0, p * (1.0 / jnp.sum(p, axis=-1, keepdims=True))), vh)

    return (half(k_a, v_a, sink_a) + half(k_b, v_b, sink_b)) * silu(zs)


SAFE_RANGE = 80.0


def _rec_front(qr, fr, l0, l1):
    lf, k = forget_gate(fr, l1 - l0)
    return silu(qr), k, lf


def _rec_tail(o, z, gw):
    return o * _rms(o) * gw * silu(z)


def _rec_margin(b):
    R = b.shape[0]
    mid, last = _row(b, R // 2 - 1), _row(b, R - 1)
    return jnp.minimum(mid, last - mid)


def _heads(x):
    w = x.shape[1] // REC_HEADS
    return [x[:, h * w:(h + 1) * w] for h in range(REC_HEADS)]


def _hdot(a, b, contract):
    return jnp.concatenate([_dot(ah, bh, contract) for ah, bh in zip(_heads(a), _heads(b))], axis=1)


@jax.custom_vjp
def hmm_nn(a, b):
    return _hdot(a, b, _NN)


hmm_nn.defvjp(lambda a, b: (_hdot(a, b, _NN), (a, b)),
              lambda res, g: (_hdot(g, res[1], _NT), _hdot(res[0], g, _TN)))


@jax.custom_vjp
def hmm_nt(a, b):
    return _hdot(a, b, _NT)


hmm_nt.defvjp(lambda a, b: (_hdot(a, b, _NT), (a, b)),
              lambda res, g: (_hdot(g, res[1], _NN), _hdot(g, res[0], _TN)))


@jax.custom_vjp
def hmm_tn(a, b):
    return _hdot(a, b, _TN)


hmm_tn.defvjp(lambda a, b: (_hdot(a, b, _TN), (a, b)),
              lambda res, g: (_hdot(res[1], g, _NT), _hdot(res[0], g, _NN)))


def _head_sums(x):
    return jnp.concatenate([jnp.broadcast_to(jnp.sum(xh, axis=-1, keepdims=True), xh.shape) for xh in _heads(x)],
                           axis=1)


@jax.custom_vjp
def head_sum(x):
    return _head_sums(x)


head_sum.defvjp(lambda x: (_head_sums(x), None), lambda _, g: (_head_sums(g),))


def _rec_cores_fast(q, k, v, b, S):
    R = q.shape[0]
    ri = lax.broadcasted_iota(jnp.int32, (R, REC_HEADS * R), 0)
    ci = lax.broadcasted_iota(jnp.int32, (R, REC_HEADS * R), 1) % R
    d = b - _row(b, R // 2 - 1)
    sc = jnp.where(ci < ri, hmm_nt(decayed(q, d), decayed(k, -d)), 0.0)
    o = hmm_nt(q * jnp.exp(b), S) + hmm_nn(sc, v) + head_sum(q * k) * v
    b_last = _row(b, R - 1)
    return o, S * jnp.exp(b_last) + hmm_tn(v, k * jnp.exp(b_last - b))


def _rec_tails(o, z, gw):
    return o * lax.rsqrt(head_sum(o * o) * (1.0 / REC_DIM) + NORM_EPS) * gw * silu(z)


def _rec_block_fast(qr, fr, v, z, S, l0, l1, gw):
    lf, k = forget_gate(fr, l1 - l0)
    o, S_new = _rec_cores_fast(silu(qr), k, v, cumsum_rows(lf), S)
    return _rec_tails(o, z, gw), S_new


def _rec_core_slow(q, k, v, b, S):
    R = q.shape[0]
    rows = lax.broadcasted_iota(jnp.int32, (R, REC_DIM), 0)

    o = mm_nt(q * jnp.exp(jnp.minimum(b, 0.0)), S)

    ri = lax.broadcasted_iota(jnp.int32, (R, R), 0)
    ci = lax.broadcasted_iota(jnp.int32, (R, R), 1)
    sc = jnp.zeros((R, R), F32)
    w = R
    while w > DIAG:
        h = w // 2
        b3 = b.reshape(R // w, w, REC_DIM)
        rin = lax.broadcasted_iota(jnp.int32, (R // w, w, REC_DIM), 1)
        mid = jnp.sum(jnp.where(rin == h - 1, b3, 0.0), axis=1, keepdims=True)
        fac = jnp.exp(jnp.minimum(jnp.where(rin >= h, b3 - mid, mid - b3), 0.0)).reshape(R, REC_DIM)
        upper = (rows % w) >= h
        s_w = mm_nt(jnp.where(upper, q * fac, 0.0), jnp.where(upper, 0.0, k * fac))
        sc = sc + jnp.where((ri // w) == (ci // w), s_w, 0.0)
        w = h
    o = o + mm_nn(sc, v)

    g = R // DIAG
    q3, k3, v3, b3 = (t.reshape(g, DIAG, REC_DIM) for t in (q, k, v, b))
    rin = lax.broadcasted_iota(jnp.int32, (g, DIAG, 1), 1)
    od = jnp.zeros((g, DIAG, REC_DIM), F32)
    for d in range(DIAG):
        e = jnp.exp(jnp.minimum(b3 - roll_sub(b3, d), 0.0))
        sd = jnp.sum(q3 * roll_sub(k3, d) * e, axis=-1, keepdims=True)
        od = od + jnp.where(rin >= d, sd, 0.0) * roll_sub(v3, d)
    o = o + od.reshape(R, REC_DIM)

    b_last = _row(b, R - 1)
    return o, S * jnp.exp(jnp.minimum(b_last, 0.0)) + mm_tn(v, k * jnp.exp(jnp.minimum(b_last - b, 0.0)))


def _rec_head(core, qr, fr, v, z, S, l0, l1, gw):
    q, k, lf = _rec_front(qr, fr, l0, l1)
    o, S_new = core(q, k, v, cumsum_rows(lf), S)
    return _rec_tail(o, z, gw), S_new


def _rope_angles(positions):
    half = ROPE_DIM // 2
    inv_freq = ROPE_THETA ** (-(jnp.arange(half, dtype=F32) * 2.0 / ROPE_DIM))
    ang = positions.astype(F32).reshape(-1, 1) * inv_freq
    return jnp.concatenate([jnp.cos(ang), jnp.sin(ang)], axis=-1)


def _rope_tables(cs):
    half = ROPE_DIM // 2
    r = lax.broadcasted_iota(jnp.int32, (ROPE_DIM, 3 * LANES), 0)
    c = lax.broadcasted_iota(jnp.int32, (ROPE_DIM, 3 * LANES), 1)
    table, j = c // LANES, c % HEAD_DIM
    angle, low = j % half, j < half
    plus = ((table == 0) & (j < ROPE_DIM) & (r == angle)) | ((table == 1) & (j >= half) & (j < ROPE_DIM)
                                                                & (r == half + angle))
    minus = (table == 2) & low & (r == half + angle)
    pick = jnp.where(plus, 1.0, jnp.where(minus, -1.0, 0.0)).astype(BF16)
    hi = cs.astype(BF16)
    rest = cs - hi.astype(F32)
    mid = rest.astype(BF16)
    lo = (rest - mid.astype(F32)).astype(BF16)
    dot = lambda piece: jnp.dot(piece, pick, preferred_element_type=F32)
    out = (dot(lo) + dot(mid)) + dot(hi)
    lane = lax.broadcasted_iota(jnp.int32, (1, LANES), 1) % HEAD_DIM
    return out[:, :LANES] + jnp.where(lane < ROPE_DIM, 0.0, 1.0), out[:, LANES:2 * LANES], out[:, 2 * LANES:]


def _rope(x, cos_t, sin_a, sin_b):
    half = ROPE_DIM // 2
    return x * cos_t + pltpu.roll(x, half, 1) * sin_a + pltpu.roll(x, LANES - half, 1) * sin_b


def _rope_transposed(g, cos_t, sin_a, sin_b):
    half = ROPE_DIM // 2
    return g * cos_t + pltpu.roll(g * sin_a, LANES - half, 1) + pltpu.roll(g * sin_b, half, 1)


def _row_spec(tm, width):
    return pl.BlockSpec((tm, width), lambda i: (i, 0))


def _full_spec(shape):
    return pl.BlockSpec(shape, lambda *_: (0,) * len(shape))


def attn_in_proj(x, w_pre, w_in, b_in, angles, tm=1024):
    n = x.shape[0]
    tm = min(tm, n)

    def body(x_ref, wp_ref, w_ref, b_ref, cs_ref, h_ref, q_ref, k_ref, v_ref, z_ref):
        xv = x_ref[...]
        h = (xv * _rms(xv) * wp_ref[...]).astype(BF16)
        h_ref[...] = h
        proj = jnp.dot(h, w_ref[...], preferred_element_type=F32) + b_ref[...]
        tabs = _rope_tables(cs_ref[...])
        for s in range(D_MODEL // LANES):
            sl = slice(s * LANES, (s + 1) * LANES)
            q_ref[:, sl] = _rope(proj[:, sl] * (HEAD_DIM ** -0.5), *tabs).astype(BF16)
        k_ref[...] = _rope(proj[:, D_MODEL:D_MODEL + KV_WIDTH], *tabs).astype(BF16)
        v_ref[...] = proj[:, D_MODEL + KV_WIDTH:D_MODEL + 2 * KV_WIDTH].astype(BF16)
        z_ref[...] = proj[:, D_MODEL + 2 * KV_WIDTH:]

    return pl.pallas_call(
        body, name="attn_in_proj", grid=(n // tm,),
        in_specs=[_row_spec(tm, D_MODEL), _full_spec((1, D_MODEL)), _full_spec((D_MODEL, ATTN_IN)),
                  _full_spec((1, ATTN_IN)), _row_spec(tm, ROPE_DIM)],
        out_specs=[_row_spec(tm, D_MODEL), _row_spec(tm, D_MODEL), _row_spec(tm, KV_WIDTH),
                   _row_spec(tm, KV_WIDTH), _row_spec(tm, D_MODEL)],
        out_shape=[jax.ShapeDtypeStruct((n, D_MODEL), BF16), jax.ShapeDtypeStruct((n, D_MODEL), BF16),
                   jax.ShapeDtypeStruct((n, KV_WIDTH), BF16), jax.ShapeDtypeStruct((n, KV_WIDTH), BF16),
                   jax.ShapeDtypeStruct((n, D_MODEL), F32)],
        compiler_params=_cparams(1),
    )(x, w_pre, w_in, b_in, angles)


def _column_blocks(w):
    if len(w.shape) == 2:
        return [slice(0, w.shape[1])], lambda ref, s: ref[...]
    width = w.shape[2]
    return [slice(s * width, (s + 1) * width) for s in range(w.shape[0])], lambda ref, s: ref[s]


def rec_in_proj(x, w_pre, w_in, tm=512):
    n = x.shape[0]
    tm = min(tm, n)
    columns, block = _column_blocks(w_in)

    def body(x_ref, wp_ref, w_ref, h_ref, p_ref):
        xv = x_ref[...]
        h = (xv * _rms(xv) * wp_ref[...]).astype(BF16)
        h_ref[...] = h
        for s, cols in enumerate(columns):
            p_ref[:, cols] = jnp.dot(h, block(w_ref, s), preferred_element_type=F32)

    return pl.pallas_call(
        body, name="rec_in_proj", grid=(n // tm,),
        in_specs=[_row_spec(tm, D_MODEL), _full_spec((1, D_MODEL)), _full_spec(w_in.shape)],
        out_specs=[_row_spec(tm, D_MODEL), _row_spec(tm, REC_IN)],
        out_shape=[jax.ShapeDtypeStruct((n, D_MODEL), BF16), jax.ShapeDtypeStruct((n, REC_IN), F32)],
        compiler_params=_cparams(1),
    )(x, w_pre, w_in)


def out_proj(og, w_out, b_out, x_res, w_post, target=None, tm=1024):
    n = og.shape[0]
    tm = min(tm, n)
    with_loss = target is not None

    def body(*refs):
        if with_loss:
            og_ref, w_ref, b_ref, x_ref, wp_ref, t_ref, y_ref, dx_ref, l_ref = refs
        else:
            og_ref, w_ref, b_ref, x_ref, wp_ref, y_ref, xo_ref = refs
        y = jnp.dot(og_ref[...], w_ref[...], preferred_element_type=F32) + b_ref[...]
        y_ref[...] = y.astype(BF16)
        xo = x_ref[...] + y * _rms(y) * wp_ref[...]
        if with_loss:
            err = xo - t_ref[...]
            dx_ref[...] = err * (1.0 / D_MODEL)

            @pl.when(pl.program_id(0) == 0)
            def _():
                l_ref[...] = jnp.zeros_like(l_ref)

            l_ref[...] += jnp.sum(err * err, axis=0, keepdims=True)
        else:
            xo_ref[...] = xo

    in_specs = [_row_spec(tm, D_MODEL), _full_spec((D_MODEL, D_MODEL)), _full_spec((1, D_MODEL)),
                _row_spec(tm, D_MODEL), _full_spec((1, D_MODEL))]
    out_specs = [_row_spec(tm, D_MODEL), _row_spec(tm, D_MODEL)]
    out_shape = [jax.ShapeDtypeStruct((n, D_MODEL), BF16), jax.ShapeDtypeStruct((n, D_MODEL), F32)]
    args = [og, w_out, b_out, x_res, w_post]
    if with_loss:
        in_specs.append(_row_spec(tm, D_MODEL))
        out_specs.append(_full_spec((1, D_MODEL)))
        out_shape.append(jax.ShapeDtypeStruct((1, D_MODEL), F32))
        args.append(target)
    return pl.pallas_call(
        body, name="out_proj_loss" if with_loss else "out_proj", grid=(n // tm,),
        in_specs=in_specs, out_specs=out_specs, out_shape=out_shape, compiler_params=_cparams(1),
    )(*args)


def out_proj_bwd(dxo, y, og, w_out, w_post, tm=1024):
    n = og.shape[0]
    tm = min(tm, n)

    def body(g_ref, y_ref, og_ref, w_ref, wp_ref, dog_ref, dw_ref, db_ref, dwp_ref):
        @pl.when(pl.program_id(0) == 0)
        def _():
            dw_ref[...] = jnp.zeros_like(dw_ref)
            db_ref[...] = jnp.zeros_like(db_ref)
            dwp_ref[...] = jnp.zeros_like(dwp_ref)

        g, y = g_ref[...], y_ref[...].astype(F32)
        rstd = _rms(y)
        yn = y * rstd
        gw = g * wp_ref[...]
        dwp_ref[...] += jnp.sum(g * yn, axis=0, keepdims=True)
        dy = rstd * (gw - yn * jnp.mean(gw * yn, axis=-1, keepdims=True))
        db_ref[...] += jnp.sum(dy, axis=0, keepdims=True)
        dyb = dy.astype(BF16)
        dog_ref[...] = _dot(dyb, w_ref[...], _NT).astype(BF16)
        dw_ref[...] += _dot(og_ref[...], dyb, _TN)

    return pl.pallas_call(
        body, name="out_proj_bwd", grid=(n // tm,),
        in_specs=[_row_spec(tm, D_MODEL), _row_spec(tm, D_MODEL), _row_spec(tm, D_MODEL),
                  _full_spec((D_MODEL, D_MODEL)), _full_spec((1, D_MODEL))],
        out_specs=[_row_spec(tm, D_MODEL), _full_spec((D_MODEL, D_MODEL)), _full_spec((1, D_MODEL)),
                   _full_spec((1, D_MODEL))],
        out_shape=[jax.ShapeDtypeStruct((n, D_MODEL), BF16), jax.ShapeDtypeStruct((D_MODEL, D_MODEL), F32),
                   jax.ShapeDtypeStruct((1, D_MODEL), F32), jax.ShapeDtypeStruct((1, D_MODEL), F32)],
        compiler_params=_cparams(1),
    )(dxo, y, og, w_out, w_post)


def in_proj_bwd_x(dproj, w_in, x, w_pre, dxo, tm=512, scatter=()):
    n, p = dproj.shape
    tm = min(tm if p > ATTN_IN else 2 * tm, n)
    steps = n // tm
    ns = len(scatter)
    columns, block = _column_blocks(w_in)

    def body(*refs):
        dp_ref, w_ref, x_ref, wp_ref, g_ref = refs[:5]
        dx_ref, dwp_ref = refs[5 + ns:7 + ns]
        exchange = (refs[5:5 + ns], refs[7 + ns:7 + 2 * ns]) + tuple(refs[7 + 2 * ns:])

        @pl.when(pl.program_id(0) == 0)
        def _():
            dwp_ref[...] = jnp.zeros_like(dwp_ref)
            if ns:
                _scatter_start(*exchange)

        dh = functools.reduce(jnp.add, [_dot(dp_ref[:, cols], block(w_ref, s), _NT)
                                        for s, cols in enumerate(columns)])
        xv = x_ref[...]
        rstd = _rms(xv)
        xn = xv * rstd
        gw = dh * wp_ref[...]
        dwp_ref[...] += jnp.sum(dh * xn, axis=0, keepdims=True)
        dx_ref[...] = rstd * (gw - xn * jnp.mean(gw * xn, axis=-1, keepdims=True)) + g_ref[...]

        if ns:
            @pl.when(pl.program_id(0) == steps - 1)
            def _():
                _scatter_finish(*exchange)

    out = pl.pallas_call(
        body, name=f"in_proj_bwd_x_{p}", grid=(steps,),
        in_specs=[_row_spec(tm, p), _full_spec(w_in.shape), _row_spec(tm, D_MODEL), _full_spec((1, D_MODEL)),
                  _row_spec(tm, D_MODEL)] + [_ANY] * ns,
        out_specs=[_row_spec(tm, D_MODEL), _full_spec((1, D_MODEL))] + [_ANY] * ns,
        out_shape=[jax.ShapeDtypeStruct((n, D_MODEL), F32), jax.ShapeDtypeStruct((1, D_MODEL), F32)]
        + [jax.ShapeDtypeStruct(a.shape, a.dtype) for a in scatter],
        scratch_shapes=_scatter_sems(ns) if ns else [],
        compiler_params=_cparams(1),
    )(dproj, w_in, x, w_pre, dxo, *scatter)
    return out[0], out[1], out[2:]


def in_proj_bwd_w(h, dproj, tm=1024, as_shards=False):
    n, p = dproj.shape
    chunk = p // (4 if p % 4096 == 0 else 3)
    tm = min(tm, n)
    steps = n // tm
    shard = p // N_CHIPS

    def body(h_ref, dp_ref, dw_ref, db_ref, acc_scr, sem, *staging):
        i = pl.program_id(0)

        @pl.when(i == 0)
        def _():
            acc_scr[...] = jnp.zeros_like(acc_scr)
            db_ref[...] = jnp.zeros_like(db_ref)

        ht = h_ref[...].T
        for c0 in range(0, p, chunk):
            dp = dp_ref[:, c0:c0 + chunk]
            acc_scr[:, c0:c0 + chunk] += jnp.dot(ht, dp, preferred_element_type=F32)
            db_ref[:, c0:c0 + chunk] += jnp.sum(dp.astype(F32), axis=0, keepdims=True)

        @pl.when(i == steps - 1)
        def _():
            if as_shards:
                for s in range(N_CHIPS):
                    staging[0][...] = acc_scr[:, s * shard:(s + 1) * shard].astype(BF16)
                    out = pltpu.make_async_copy(staging[0], dw_ref.at[s], sem)
                    out.start()
                    out.wait()
            else:
                out = pltpu.make_async_copy(acc_scr, dw_ref, sem)
                out.start()
                out.wait()

    dw_shape = jax.ShapeDtypeStruct((N_CHIPS, D_MODEL, shard), BF16) if as_shards else (
        jax.ShapeDtypeStruct((D_MODEL, p), F32))
    return pl.pallas_call(
        body, name=f"in_proj_bwd_w_{p}", grid=(steps,),
        in_specs=[_row_spec(tm, D_MODEL), _row_spec(tm, p)],
        out_specs=[_ANY, _full_spec((1, p))],
        out_shape=[dw_shape, jax.ShapeDtypeStruct((1, p), F32)],
        scratch_shapes=[pltpu.VMEM((D_MODEL, p), F32), pltpu.SemaphoreType.DMA]
        + ([pltpu.VMEM((D_MODEL, shard), BF16)] if as_shards else []),
        compiler_params=_cparams(1),
    )(h, dproj)


PAIRS = GROUP // 2
GROUP_ROWS = PAIRS * ATTN_BLOCK
MASKED = -1e30


def _kv_windows(k_ref, v_ref, i):
    ps = pl.multiple_of(jnp.maximum(i - 1, 0) * ATTN_BLOCK, ATTN_BLOCK)
    cs = pl.multiple_of(i * ATTN_BLOCK, ATTN_BLOCK)
    kw = jnp.concatenate([k_ref[pl.ds(ps, ATTN_BLOCK), :], k_ref[pl.ds(cs, ATTN_BLOCK), :]], axis=0)
    vw = jnp.concatenate([v_ref[pl.ds(ps, ATTN_BLOCK), :], v_ref[pl.ds(cs, ATTN_BLOCK), :]], axis=0)
    return kw.astype(F32).T, vw.astype(F32).T, ps, cs


def _low_rows(shape):
    return lax.broadcasted_iota(jnp.int32, shape, 0) < HEAD_DIM


def _spread(w, kvh):
    low = _low_rows(w.shape)
    swapped = pltpu.roll(w, HEAD_DIM, 0)
    if kvh == 0:
        return jnp.where(low, w, 0.0), jnp.where(low, 0.0, swapped)
    return jnp.where(low, swapped, 0.0), jnp.where(low, 0.0, w)


def _unspread(d_a, d_b, kvh):
    low = _low_rows(d_a.shape)
    if kvh == 0:
        return jnp.where(low, d_a + pltpu.roll(d_b, HEAD_DIM, 0), 0.0)
    return jnp.where(low, 0.0, pltpu.roll(d_a, HEAD_DIM, 0) + d_b)


def _stack_pairs(ref, kvh):
    return jnp.concatenate([ref[:, (kvh * PAIRS + j) * LANES:(kvh * PAIRS + j + 1) * LANES] for j in range(PAIRS)],
                           axis=0)


def _fill_bias(bias_scr):
    shape = (GROUP_ROWS, 2 * ATTN_BLOCK)
    r = lax.broadcasted_iota(jnp.int32, shape, 0) % ATTN_BLOCK
    c = lax.broadcasted_iota(jnp.int32, shape, 1)
    in_cur = (c >= ATTN_BLOCK) & ((c - ATTN_BLOCK) <= r)
    in_prev = (c < ATTN_BLOCK) & (c > r)
    bias_scr[0] = jnp.where(in_cur, 0.0, MASKED)
    bias_scr[1] = jnp.where(in_cur | in_prev, 0.0, MASKED)
    bias_scr[2] = jnp.where(c == r, 1.0, 0.0)


N_BIAS_TABLES = 3


def _sink_table(sinks):
    t = jnp.transpose(sinks.reshape(N_KV_HEADS, PAIRS, 2), (0, 2, 1))
    return jnp.broadcast_to(t[:, :, :, None, None], (N_KV_HEADS, 2, PAIRS, ATTN_BLOCK, LANES)).reshape(
        N_KV_HEADS, 2, GROUP_ROWS, LANES)


def attn_fwd(q, k, v, z, sink_tab, batch, seq, gather=()):
    nb = seq // ATTN_BLOCK
    ng = len(gather)

    def body(*refs):
        q_ref, k_ref, v_ref, z_ref, s_ref = refs[:5]
        og_ref, bias_scr = refs[5 + ng], refs[6 + 2 * ng]
        exchange = (refs[5:5 + ng], refs[6 + ng:6 + 2 * ng]) + tuple(refs[7 + 2 * ng:])
        b, i = pl.program_id(0), pl.program_id(1)

        @pl.when((b == 0) & (i == 0))
        def _():
            _fill_bias(bias_scr)
            if ng:
                _gather_start(*exchange)

        kw, vw, _, _ = _kv_windows(k_ref, v_ref, i)
        bias, at_sink = bias_scr[jnp.minimum(i, 1)], bias_scr[2] > 0.5
        for kvh in range(N_KV_HEADS):
            k_a, k_b = _spread(kw, kvh)
            v_a, v_b = _spread(vw, kvh)
            og = _attn_group(_stack_pairs(q_ref, kvh), k_a, v_a, k_b, v_b, _stack_pairs(z_ref, kvh),
                             s_ref[kvh, 0], s_ref[kvh, 1], bias, at_sink)
            for j in range(PAIRS):
                og_ref[:, (kvh * PAIRS + j) * LANES:(kvh * PAIRS + j + 1) * LANES] = (
                    og[j * ATTN_BLOCK:(j + 1) * ATTN_BLOCK].astype(BF16))

        if ng:
            @pl.when((b == batch - 1) & (i == nb - 1))
            def _():
                _gather_finish(*exchange)

    blk = lambda w: pl.BlockSpec((ATTN_BLOCK, w), lambda b, i: (b * nb + i, 0))
    seq_spec = pl.BlockSpec((seq, KV_WIDTH), lambda b, i: (b, 0))
    out = pl.pallas_call(
        body, name="attn_fwd", grid=(batch, nb),
        in_specs=[blk(D_MODEL), seq_spec, seq_spec, blk(D_MODEL), _full_spec(sink_tab.shape)] + [_ANY] * ng,
        out_specs=[blk(D_MODEL)] + [_ANY] * ng,
        out_shape=[jax.ShapeDtypeStruct((batch * seq, D_MODEL), BF16)]
        + [jax.ShapeDtypeStruct((N_CHIPS,) + a.shape, a.dtype) for a in gather],
        scratch_shapes=[pltpu.VMEM((N_BIAS_TABLES, GROUP_ROWS, 2 * ATTN_BLOCK), F32)] + (_gather_sems(ng) if ng else []),
        compiler_params=_cparams(2),
    )(q, k, v, z, sink_tab, *gather)
    return out[0], out[1:]


def attn_bwd(q, k, v, z, sink_tab, dog, angles, batch, seq, scatter=()):
    nb = seq // ATTN_BLOCK
    ns = len(scatter)

    def body(*refs):
        q_ref, k_ref, v_ref, z_ref, s_ref, g_ref, cs_ref = refs[:7]
        dp_ref, dk_ref, dv_ref, ds_ref = refs[7 + ns:11 + ns]
        bias_scr = refs[11 + 2 * ns]
        exchange = (refs[7:7 + ns], refs[11 + ns:11 + 2 * ns]) + tuple(refs[12 + 2 * ns:])
        b, i = pl.program_id(0), pl.program_id(1)

        @pl.when((b == 0) & (i == 0))
        def _():
            _fill_bias(bias_scr)
            ds_ref[...] = jnp.zeros_like(ds_ref)
            if ns:
                _scatter_start(*exchange)

        @pl.when(i == 0)
        def _():
            dk_ref[...] = jnp.zeros_like(dk_ref)
            dv_ref[...] = jnp.zeros_like(dv_ref)

        kw, vw, ps, cs = _kv_windows(k_ref, v_ref, i)
        bias = bias_scr[jnp.minimum(i, 1)]
        tabs = _rope_tables(cs_ref[...])
        dkw = jnp.zeros_like(kw)
        dvw = jnp.zeros_like(vw)
        for kvh in range(N_KV_HEADS):
            k_a, k_b = _spread(kw, kvh)
            v_a, v_b = _spread(vw, kvh)
            _, vjp = jax.vjp(functools.partial(_attn_group, bias=bias), _stack_pairs(q_ref, kvh).astype(F32),
                             k_a, v_a, k_b, v_b, _stack_pairs(z_ref, kvh), s_ref[kvh, 0], s_ref[kvh, 1])
            dqs, dk_a, dv_a, dk_b, dv_b, dzs, ds_a, ds_b = vjp(_stack_pairs(g_ref, kvh).astype(F32))
            dkw = dkw + _unspread(dk_a, dk_b, kvh)
            dvw = dvw + _unspread(dv_a, dv_b, kvh)
            ds_ref[kvh, 0] += jnp.sum(ds_a.reshape(PAIRS, ATTN_BLOCK, LANES), axis=1)
            ds_ref[kvh, 1] += jnp.sum(ds_b.reshape(PAIRS, ATTN_BLOCK, LANES), axis=1)
            for j in range(PAIRS):
                rows = slice(j * ATTN_BLOCK, (j + 1) * ATTN_BLOCK)
                col = (kvh * PAIRS + j) * LANES
                dp_ref[:, col:col + LANES] = _rope_transposed(dqs[rows] * (HEAD_DIM ** -0.5), *tabs).astype(BF16)
                zc = D_MODEL + 2 * KV_WIDTH + col
                dp_ref[:, zc:zc + LANES] = dzs[rows].astype(BF16)
        dp_ref[:, D_MODEL:D_MODEL + 2 * KV_WIDTH] = jnp.zeros((ATTN_BLOCK, 2 * KV_WIDTH), BF16)
        dk_ref[:, pl.ds(ps, ATTN_BLOCK)] += dkw[:, :ATTN_BLOCK]
        dk_ref[:, pl.ds(cs, ATTN_BLOCK)] += dkw[:, ATTN_BLOCK:]
        dv_ref[:, pl.ds(ps, ATTN_BLOCK)] += dvw[:, :ATTN_BLOCK]
        dv_ref[:, pl.ds(cs, ATTN_BLOCK)] += dvw[:, ATTN_BLOCK:]

        if ns:
            @pl.when((b == batch - 1) & (i == nb - 1))
            def _():
                _scatter_finish(*exchange)

    blk = lambda w: pl.BlockSpec((ATTN_BLOCK, w), lambda b, i: (b * nb + i, 0))
    seq_spec = pl.BlockSpec((seq, KV_WIDTH), lambda b, i: (b, 0))
    seq_spec_t = pl.BlockSpec((KV_WIDTH, seq), lambda b, i: (0, b))
    n = batch * seq
    ds_shape = (N_KV_HEADS, 2, PAIRS, LANES)
    out = pl.pallas_call(
        body, name="attn_bwd", grid=(batch, nb),
        in_specs=[blk(D_MODEL), seq_spec, seq_spec, blk(D_MODEL), _full_spec(sink_tab.shape), blk(D_MODEL)]
        + [blk(ROPE_DIM)] + [_ANY] * ns,
        out_specs=[blk(ATTN_IN), seq_spec_t, seq_spec_t, _full_spec(ds_shape)] + [_ANY] * ns,
        out_shape=[jax.ShapeDtypeStruct((n, ATTN_IN), BF16), jax.ShapeDtypeStruct((KV_WIDTH, n), F32),
                   jax.ShapeDtypeStruct((KV_WIDTH, n), F32), jax.ShapeDtypeStruct(ds_shape, F32)]
        + [jax.ShapeDtypeStruct(a.shape, a.dtype) for a in scatter],
        scratch_shapes=[pltpu.VMEM((N_BIAS_TABLES, GROUP_ROWS, 2 * ATTN_BLOCK), F32)] + (_scatter_sems(ns) if ns else []),
        compiler_params=_cparams(2),
    )(q, k, v, z, sink_tab, dog, angles, *scatter)
    return out[0], out[1], out[2], out[3], out[4:]


def attn_bwd_kv(dproj, dk_t, dv_t, angles, tm=512):
    n = dproj.shape[0]

    def body(dp_in_ref, dk_ref, dv_ref, cs_ref, dp_ref):
        del dp_in_ref
        dp_ref[:, :KV_WIDTH] = _rope_transposed(dk_ref[...].T, *_rope_tables(cs_ref[...])).astype(BF16)
        dp_ref[:, KV_WIDTH:] = dv_ref[...].T.astype(BF16)

    kv_cols = pl.BlockSpec((tm, 2 * KV_WIDTH), lambda i: (i, D_MODEL // (2 * KV_WIDTH)))
    col_spec = pl.BlockSpec((KV_WIDTH, tm), lambda i: (0, i))
    return pl.pallas_call(
        body, name="attn_bwd_kv", grid=(n // tm,),
        in_specs=[kv_cols, col_spec, col_spec, _row_spec(tm, ROPE_DIM)],
        out_specs=kv_cols, out_shape=jax.ShapeDtypeStruct(dproj.shape, BF16),
        input_output_aliases={0: 0}, compiler_params=_cparams(1),
    )(dproj, dk_t, dv_t, angles)


def rec_fwd(proj, lb_logits, gnorm_w, batch, seq):
    nblk = seq // REC_BLOCK

    def body(p_ref, lb_ref, gw_ref, og_ref, st_ref, safe_ref, s_scr):
        @pl.when(pl.program_id(1) == 0)
        def _():
            s_scr[...] = jnp.zeros_like(s_scr)

        S = s_scr[...]
        st_ref[0] = S
        qr, fr, v, z = (p_ref[:, part * D_MODEL:(part + 1) * D_MODEL] for part in range(4))
        lf, k = forget_gate(fr, lb_ref[1:2, :] - lb_ref[0:1, :])
        q, b = silu(qr), cumsum_rows(lf)
        safe = jnp.min(_rec_margin(b)) >= -SAFE_RANGE

        gate = gw_ref[...] * silu(z)
        safe_ref[0] = jnp.full((REC_HEADS, LANES), safe.astype(F32))

        def store(o, S_new):
            og_ref[...] = (o * lax.rsqrt(head_sum(o * o) * (1.0 / REC_DIM) + NORM_EPS) * gate).astype(BF16)
            s_scr[...] = S_new

        @pl.when(safe)
        def _():
            store(*_rec_cores_fast(q, k, v, b, S))

        @pl.when(jnp.logical_not(safe))
        def _():
            outs = [_rec_core_slow(*args) for args in zip(*(_heads(t) for t in (q, k, v, b, S)))]
            store(*(jnp.concatenate(parts, axis=1) for parts in zip(*outs)))

    blk = lambda w: pl.BlockSpec((REC_BLOCK, w), lambda b, j: (b * nblk + j, 0))
    st_spec = pl.BlockSpec((1, REC_DIM, D_MODEL), lambda b, j: (b * nblk + j, 0, 0))
    safe_spec = pl.BlockSpec((1, REC_HEADS, LANES), lambda b, j: (b * nblk + j, 0, 0))
    return pl.pallas_call(
        body, name="rec_fwd", grid=(batch, nblk),
        in_specs=[blk(REC_IN), _full_spec((2, D_MODEL)), _full_spec((1, D_MODEL))],
        out_specs=[blk(D_MODEL), st_spec, safe_spec],
        out_shape=[jax.ShapeDtypeStruct((batch * seq, D_MODEL), BF16),
                   jax.ShapeDtypeStruct((batch * nblk, REC_DIM, D_MODEL), F32),
                   jax.ShapeDtypeStruct((batch * nblk, REC_HEADS, LANES), F32)],
        scratch_shapes=[pltpu.VMEM((REC_DIM, D_MODEL), F32)],
        compiler_params=_cparams(2),
    )(proj, lb_logits, jnp.tile(gnorm_w, (1, REC_HEADS)))


def rec_bwd(proj, states, safe, lb_logits, gnorm_w, dog, batch, seq):
    nblk = seq // REC_BLOCK

    def body(p_ref, st_ref, safe_ref, lb_ref, gw_ref, g_ref, dp_ref, dlb_ref, dgw_ref, ds_scr):
        @pl.when((pl.program_id(0) == 0) & (pl.program_id(1) == 0))
        def _():
            dlb_ref[...] = jnp.zeros_like(dlb_ref)
            dgw_ref[...] = jnp.zeros_like(dgw_ref)

        @pl.when(pl.program_id(1) == 0)
        def _():
            ds_scr[...] = jnp.zeros_like(ds_scr)

        def load():
            primals = tuple(p_ref[:, part * D_MODEL:(part + 1) * D_MODEL] for part in range(4)) + (
                st_ref[0], lb_ref[0:1, :], lb_ref[1:2, :], gw_ref[...])
            return primals, (g_ref[...].astype(F32), ds_scr[...])

        def store(dqr, dfr, dv, dz, dS, dl0, dl1, dgw):
            for part, val in enumerate((dqr, dfr, dv, dz)):
                dp_ref[:, part * D_MODEL:(part + 1) * D_MODEL] = val.astype(BF16)
            ds_scr[...] = dS
            dlb_ref[0:1, :] += dl0
            dlb_ref[1:2, :] += dl1
            dgw_ref[...] += functools.reduce(jnp.add, _heads(dgw))

        fast = jnp.max(safe_ref[0]) > 0.5

        @pl.when(fast)
        def _():
            primals, cotangents = load()
            store(*jax.vjp(_rec_block_fast, *primals)[1](cotangents))

        @pl.when(jnp.logical_not(fast))
        def _():
            primals, cotangents = load()
            outs = [jax.vjp(functools.partial(_rec_head, _rec_core_slow), *args)[1](cts)
                    for args, cts in zip(zip(*(_heads(t) for t in primals)), zip(*(_heads(t) for t in cotangents)))]
            store(*(jnp.concatenate(parts, axis=1) for parts in zip(*outs)))

    blk = lambda w: pl.BlockSpec((REC_BLOCK, w), lambda b, j: (b * nblk + nblk - 1 - j, 0))
    st_spec = pl.BlockSpec((1, REC_DIM, D_MODEL), lambda b, j: (b * nblk + nblk - 1 - j, 0, 0))
    safe_spec = pl.BlockSpec((1, REC_HEADS, LANES), lambda b, j: (b * nblk + nblk - 1 - j, 0, 0))
    return pl.pallas_call(
        body, name="rec_bwd", grid=(batch, nblk),
        in_specs=[blk(REC_IN), st_spec, safe_spec, _full_spec((2, D_MODEL)), _full_spec((1, D_MODEL)),
                  blk(D_MODEL)],
        out_specs=[blk(REC_IN), _full_spec((2, D_MODEL)), _full_spec((1, REC_DIM))],
        out_shape=[jax.ShapeDtypeStruct((batch * seq, REC_IN), BF16), jax.ShapeDtypeStruct((2, D_MODEL), F32),
                   jax.ShapeDtypeStruct((1, REC_DIM), F32)],
        scratch_shapes=[pltpu.VMEM((REC_DIM, D_MODEL), F32)],
        compiler_params=_cparams(2),
    )(proj, states, safe, lb_logits, jnp.tile(gnorm_w, (1, REC_HEADS)), dog)


_ANY = pl.BlockSpec(memory_space=pl.ANY)


def _chip_peers():
    x, y, c = lax.axis_index("x"), lax.axis_index("y"), lax.axis_index("c")
    peers = []
    for fx, fy in ((1, 0), (0, 1), (1, 1)):
        px, py = (1 - x if fx else x), (1 - y if fy else y)
        peers.append(((px, py, c), 2 * px + py))
    return 2 * x + y, peers


def _remote(src, dst, send_sem, recv_sem, device):
    return pltpu.make_async_remote_copy(src_ref=src, dst_ref=dst, send_sem=send_sem, recv_sem=recv_sem,
                                        device_id=device, device_id_type=MESH)


N_FLIPS = N_CHIPS - 1


def _scatter_sems(n):
    return [pltpu.SemaphoreType.DMA((n * N_FLIPS,)), pltpu.SemaphoreType.DMA((n * N_FLIPS,)),
            pltpu.SemaphoreType.DMA((n,))]


def _scatter_copies(ins, outs, send_sems, recv_sems, local_sems, starting):
    me, peers = _chip_peers()
    local = [pltpu.make_async_copy(ins[k].at[me], outs[k].at[me], local_sems.at[k]) for k in range(len(ins))]
    sends, arrivals = [], []
    for k in range(len(ins)):
        for j, (device, idx) in enumerate(peers):
            sems = (send_sems.at[k * N_FLIPS + j], recv_sems.at[k * N_FLIPS + j], device)
            sends.append(_remote(ins[k].at[idx], outs[k].at[me], *sems))
            if not starting:
                arrivals.append(_remote(ins[k].at[me], outs[k].at[idx], *sems))
    return local, sends, arrivals


def _scatter_start(*refs):
    local, sends, _ = _scatter_copies(*refs, starting=True)
    for cp in local + sends:
        cp.start()


def _scatter_finish(*refs):
    local, sends, arrivals = _scatter_copies(*refs, starting=False)
    for cp in arrivals:
        cp.wait_recv()
    for cp in sends:
        cp.wait_send()
    for cp in local:
        cp.wait()


def _gather_sems(n):
    return [pltpu.SemaphoreType.DMA((n * N_FLIPS,)) for _ in range(4)] + [pltpu.SemaphoreType.DMA((n,))]


def _gather_copies(ins, outs, send_sems, recv_sems, pass_send_sems, pass_recv_sems, local_sems, starting):
    me, peers = _chip_peers()
    c = lax.axis_index("c")
    sibling = (lax.axis_index("x"), lax.axis_index("y"), 1 - c)
    local = [pltpu.make_async_copy(ins[k], outs[k].at[me], local_sems.at[k]) for k in range(len(ins))]
    sends, arrivals, passes, pass_arrivals = [], [], [], []
    for k in range(len(ins)):
        half = ins[k].shape[0] // 2
        mine, other = pl.ds(c * half, half), pl.ds((1 - c) * half, half)
        for j, (device, idx) in enumerate(peers):
            s = k * N_FLIPS + j
            sends.append(_remote(ins[k].at[mine], outs[k].at[me].at[mine], send_sems.at[s], recv_sems.at[s], device))
            if starting:
                continue
            arrived = outs[k].at[idx].at[mine]
            arrivals.append(_remote(ins[k].at[mine], arrived, send_sems.at[s], recv_sems.at[s], device))
            passes.append(_remote(arrived, arrived, pass_send_sems.at[s], pass_recv_sems.at[s], sibling))
            passed = outs[k].at[idx].at[other]
            pass_arrivals.append(_remote(passed, passed, pass_send_sems.at[s], pass_recv_sems.at[s], sibling))
    return local, sends, arrivals, passes, pass_arrivals


def _gather_start(*refs):
    local, sends, _, _, _ = _gather_copies(*refs, starting=True)
    for cp in local + sends:
        cp.start()


def _gather_finish(*refs):
    local, sends, arrivals, passes, pass_arrivals = _gather_copies(*refs, starting=False)
    for arrival, onward in zip(arrivals, passes):
        arrival.wait_recv()
        onward.start()
    for cp in pass_arrivals:
        cp.wait_recv()
    for cp in sends + passes:
        cp.wait_send()
    for cp in local:
        cp.wait()


def chip_gather(arrays):
    n = len(arrays)

    def body(*refs):
        _gather_start(refs[:n], refs[n:2 * n], *refs[2 * n:])
        _gather_finish(refs[:n], refs[n:2 * n], *refs[2 * n:])

    return pl.pallas_call(
        body, name="chip_gather", in_specs=[_ANY] * n, out_specs=[_ANY] * n,
        out_shape=[jax.ShapeDtypeStruct((N_CHIPS,) + a.shape, a.dtype) for a in arrays],
        scratch_shapes=_gather_sems(n),
    )(*arrays)


def sibling_exchange(arrays):
    n = len(arrays)

    def body(*refs):
        ins, outs = refs[:n], refs[n:2 * n]
        send_sems, recv_sems = refs[2 * n:]
        sibling = (lax.axis_index("x"), lax.axis_index("y"), 1 - lax.axis_index("c"))
        copies = [pltpu.make_async_remote_copy(src_ref=ins[k], dst_ref=outs[k], send_sem=send_sems.at[k],
                                               recv_sem=recv_sems.at[k], device_id=sibling, device_id_type=MESH)
                  for k in range(n)]
        for cp in copies:
            cp.start()
        for cp in copies:
            cp.wait()

    return pl.pallas_call(
        body, name="sibling_exchange", in_specs=[_ANY] * n, out_specs=[_ANY] * n,
        out_shape=[jax.ShapeDtypeStruct(a.shape, a.dtype) for a in arrays],
        scratch_shapes=[pltpu.SemaphoreType.DMA((n,)), pltpu.SemaphoreType.DMA((n,))],
    )(*arrays)


def all_gather_small(vec):
    def body(v_ref, out_ref, send_sems, recv_sems, local_sem):
        x, y, c = lax.axis_index("x"), lax.axis_index("y"), lax.axis_index("c")
        me = 4 * x + 2 * y + c
        local = pltpu.make_async_copy(v_ref, out_ref.at[me], local_sem)
        local.start()
        sends, recvs = [], []
        for j in range(1, N_DEV):
            px = jnp.where(j & 4, 1 - x, x)
            py = jnp.where(j & 2, 1 - y, y)
            pc = jnp.where(j & 1, 1 - c, c)
            common = dict(send_sem=send_sems.at[j - 1], recv_sem=recv_sems.at[j - 1], device_id=(px, py, pc),
                          device_id_type=MESH)
            sends.append(pltpu.make_async_remote_copy(src_ref=v_ref, dst_ref=out_ref.at[me], **common))
            recvs.append(pltpu.make_async_remote_copy(src_ref=v_ref, dst_ref=out_ref.at[4 * px + 2 * py + pc],
                                                      **common))
        for cp in sends:
            cp.start()
        for cp in recvs:
            cp.wait_recv()
        for cp in sends:
            cp.wait_send()
        local.wait()

    return pl.pallas_call(
        body, name="all_gather_small", in_specs=[_ANY], out_specs=_ANY,
        out_shape=jax.ShapeDtypeStruct((N_DEV,) + vec.shape, vec.dtype),
        scratch_shapes=[pltpu.SemaphoreType.DMA((N_DEV - 1,)), pltpu.SemaphoreType.DMA((N_DEV - 1,)),
                        pltpu.SemaphoreType.DMA],
    )(vec)


def sum_slots(stacked, tm=256):
    s, r, c = stacked.shape
    tm = min(tm, r)

    def body(in_ref, out_ref):
        acc = in_ref[0].astype(F32)
        for t in range(1, s):
            acc = acc + in_ref[t].astype(F32)
        out_ref[...] = acc

    return pl.pallas_call(
        body, name=f"sum_slots_{s}_{r}_{c}", grid=(r // tm,),
        in_specs=[pl.BlockSpec((s, tm, c), lambda i: (0, i, 0))], out_specs=_row_spec(tm, c),
        out_shape=jax.ShapeDtypeStruct((r, c), F32), compiler_params=_cparams(1),
    )(stacked)


def adamw(w, m, v, g_a, g_b=None, tm=256):
    r, c = w.shape
    tm = min(tm, r)
    two = g_b is not None

    def body(*refs):
        w_ref, m_ref, v_ref, ga_ref = refs[:4]
        g_ref, d_ref, nm_ref, nv_ref = refs[-4:]
        g = ga_ref[...] + refs[4][...] if two else ga_ref[...]
        nm = ADAM_B1 * m_ref[...] + (1.0 - ADAM_B1) * g
        nv = ADAM_B2 * v_ref[...] + (1.0 - ADAM_B2) * (g * g)
        m_hat = nm / (1.0 - ADAM_B1 ** ADAM_STEP)
        v_hat = nv / (1.0 - ADAM_B2 ** ADAM_STEP)
        g_ref[...] = g
        d_ref[...] = -ADAM_LR * (m_hat / (jnp.sqrt(v_hat) + ADAM_EPS) + ADAM_WD * w_ref[...])
        nm_ref[...] = nm
        nv_ref[...] = nv

    args = [w, m, v, g_a] + ([g_b] if two else [])
    return pl.pallas_call(
        body, name=f"adamw_{r}_{c}", grid=(r // tm,),
        in_specs=[_row_spec(tm, c)] * len(args), out_specs=[_row_spec(tm, c)] * 4,
        out_shape=[jax.ShapeDtypeStruct((r, c), F32)] * 4, compiler_params=_cparams(1),
    )(*args)


_SMALL = (("pre_norm_w", (2, D_MODEL)), ("post_norm_w", (2, D_MODEL)), ("attn_b_in", (1, ATTN_IN)),
          ("attn_sinks", (1, N_HEADS)), ("attn_b_out", (1, D_MODEL)), ("rec_lb_logits", (2, D_MODEL)),
          ("rec_gnorm_w", (1, REC_DIM)))
_SMALL_ROWS = 16


def _pack_small(parts, last_row=None):
    rows = []
    for (name, shape) in _SMALL:
        flat = parts[name].reshape(-1)
        pad = -flat.shape[0] % D_MODEL
        rows.append(jnp.pad(flat, (0, pad)).reshape(-1, D_MODEL))
    used = sum(r.shape[0] for r in rows)
    rows.append(jnp.zeros((_SMALL_ROWS - 1 - used, D_MODEL), F32))
    rows.append(jnp.zeros((1, D_MODEL), F32) if last_row is None else last_row)
    return jnp.concatenate(rows, axis=0)


def _unpack_small(packed):
    out, row = {}, 0
    for (name, shape) in _SMALL:
        size = shape[0] * shape[1]
        nrows = -(-size // D_MODEL)
        out[name] = packed[row:row + nrows].reshape(-1)[:size].reshape(shape)
        row += nrows
    return out


_CARRIED = ("rec_w_in", "rec_w_out", "attn_w_out")


_LATE = ("attn_w_out", "rec_w_in", "rec_w_out")


def local_step(x, positions, pre_norm_w, post_norm_w, attn_w_in, attn_b_in, attn_sinks, attn_w_out, attn_b_out,
               rec_w_in, rec_lb_logits, rec_gnorm_w, rec_w_out, loss_target, distributed=False):
    batch, seq, _ = x.shape
    n = batch * seq
    x0 = x.reshape(n, D_MODEL)
    angles = _rope_angles(positions)
    pre0, pre1 = pre_norm_w[0:1], pre_norm_w[1:2]
    post0, post1 = post_norm_w[0:1], post_norm_w[1:2]
    no_bias = jnp.zeros((1, D_MODEL), F32)

    h0, q, k, v, z = attn_in_proj(x0, pre0, attn_w_in, attn_b_in, angles)
    sink_tab = _sink_table(attn_sinks)
    late = (attn_w_out, rec_w_in, rec_w_out)
    og0, gathered = attn_fwd(q, k, v, z, sink_tab, batch, seq, gather=late if distributed else ())
    if distributed:
        attn_w_out, rec_w_in, rec_w_out = (g if name == "rec_w_in" else _whole_from_shards(name, g)
                                           for name, g in zip(_LATE, gathered))
    y0, x1 = out_proj(og0, attn_w_out, attn_b_out, x0, post0)

    h1, proj1 = rec_in_proj(x1, pre1, rec_w_in)
    og1, states, safe = rec_fwd(proj1, rec_lb_logits, rec_gnorm_w, batch, seq)
    y1, dx2, loss_vec = out_proj(og1, rec_w_out, no_bias, x1, post1, target=loss_target.reshape(n, D_MODEL))

    dog1, d_rec_w_out, _, d_post1 = out_proj_bwd(dx2, y1, og1, rec_w_out, post1)
    dproj1, d_lb, d_gnorm = rec_bwd(proj1, states, safe, rec_lb_logits, rec_gnorm_w, dog1, batch, seq)
    dx1, d_pre1, _ = in_proj_bwd_x(dproj1, rec_w_in, x1, pre1, dx2)
    d_rec_w_in, _ = in_proj_bwd_w(h1, dproj1, as_shards=distributed)

    dog0, d_attn_w_out, d_attn_b_out, d_post0 = out_proj_bwd(dx1, y0, og0, attn_w_out, post0)
    ready = dict(rec_w_out=d_rec_w_out, attn_w_out=d_attn_w_out)
    outgoing = [d_rec_w_in if name == "rec_w_in" else _shards_from_whole(name, ready[name]).astype(BF16)
                for name in _CARRIED] if distributed else []
    dproj0, dk, dv, d_sink_tab, arrived = attn_bwd(q, k, v, z, sink_tab, dog0, angles, batch, seq, scatter=outgoing)
    d_sinks = jnp.transpose(jnp.sum(d_sink_tab, axis=-1), (0, 2, 1)).reshape(1, N_HEADS)
    dproj0 = attn_bwd_kv(dproj0, dk, dv, angles)
    d_attn_w_in, d_attn_b_in = in_proj_bwd_w(h0, dproj0)
    last = [_shards_from_whole("attn_w_in", d_attn_w_in).astype(BF16)] if distributed else []
    dx0, d_pre0, arrived_last = in_proj_bwd_x(dproj0, attn_w_in, x0, pre0, dx1, scatter=last)

    grads = dict(
        pre_norm_w=jnp.concatenate([d_pre0, d_pre1], axis=0), post_norm_w=jnp.concatenate([d_post0, d_post1], axis=0),
        attn_w_in=d_attn_w_in, attn_b_in=d_attn_b_in, attn_sinks=d_sinks, attn_w_out=d_attn_w_out,
        attn_b_out=d_attn_b_out, rec_w_in=d_rec_w_in, rec_lb_logits=d_lb, rec_gnorm_w=d_gnorm,
        rec_w_out=d_rec_w_out)
    parts = dict(zip(_CARRIED + ("attn_w_in",), tuple(arrived) + tuple(arrived_last)))
    return loss_vec, dx0.reshape(batch, seq, D_MODEL), grads, parts


_BIG = ("attn_w_in", "attn_w_out", "rec_w_in", "rec_w_out")
_COLUMN_SHARDED = ("attn_w_in", "rec_w_in")
_ORDER = ("pre_norm_w", "post_norm_w", "attn_w_in", "attn_b_in", "attn_sinks", "attn_w_out", "attn_b_out",
          "rec_w_in", "rec_lb_logits", "rec_gnorm_w", "rec_w_out")


def _whole_from_shards(name, stacked):
    if name in _COLUMN_SHARDED:
        return jnp.transpose(stacked, (1, 0, 2)).reshape(stacked.shape[1], -1)
    return stacked.reshape(-1, stacked.shape[2])


def _shards_from_whole(name, whole):
    if name in _COLUMN_SHARDED:
        return jnp.transpose(whole.reshape(whole.shape[0], N_CHIPS, -1), (1, 0, 2))
    return whole.reshape(N_CHIPS, -1, whole.shape[1])


def kernel(x, positions, pre_norm_w, post_norm_w, attn_w_in, attn_b_in, attn_sinks, attn_w_out, attn_b_out, rec_w_in, rec_lb_logits, rec_gnorm_w, rec_w_out, loss_target, m_pre_norm_w, m_post_norm_w, m_attn_w_in, m_attn_b_in, m_attn_sinks, m_attn_w_out, m_attn_b_out, m_rec_w_in, m_rec_lb_logits, m_rec_gnorm_w, m_rec_w_out, v_pre_norm_w, v_post_norm_w, v_attn_w_in, v_attn_b_in, v_attn_sinks, v_attn_w_out, v_attn_b_out, v_rec_w_in, v_rec_lb_logits, v_rec_gnorm_w, v_rec_w_out):
    w = dict(pre_norm_w=pre_norm_w, post_norm_w=post_norm_w, attn_w_in=attn_w_in, attn_b_in=attn_b_in,
             attn_sinks=attn_sinks, attn_w_out=attn_w_out, attn_b_out=attn_b_out, rec_w_in=rec_w_in,
             rec_lb_logits=rec_lb_logits, rec_gnorm_w=rec_gnorm_w, rec_w_out=rec_w_out)
    m = dict(pre_norm_w=m_pre_norm_w, post_norm_w=m_post_norm_w, attn_w_in=m_attn_w_in, attn_b_in=m_attn_b_in,
             attn_sinks=m_attn_sinks, attn_w_out=m_attn_w_out, attn_b_out=m_attn_b_out, rec_w_in=m_rec_w_in,
             rec_lb_logits=m_rec_lb_logits, rec_gnorm_w=m_rec_gnorm_w, rec_w_out=m_rec_w_out)
    v = dict(pre_norm_w=v_pre_norm_w, post_norm_w=v_post_norm_w, attn_w_in=v_attn_w_in, attn_b_in=v_attn_b_in,
             attn_sinks=v_attn_sinks, attn_w_out=v_attn_w_out, attn_b_out=v_attn_b_out, rec_w_in=v_rec_w_in,
             rec_lb_logits=v_rec_lb_logits, rec_gnorm_w=v_rec_gnorm_w, rec_w_out=v_rec_w_out)

    shards = {name: w[name][0] for name in _BIG}
    sent = {name: shards[name].astype(BF16) for name in _BIG}
    attn_w_in_whole = _whole_from_shards("attn_w_in", chip_gather([sent["attn_w_in"]])[0])

    loss_vec, grad_x, grads, parts = local_step(
        x, positions, pre_norm_w, post_norm_w, attn_w_in_whole, attn_b_in, attn_sinks, sent["attn_w_out"],
        attn_b_out, sent["rec_w_in"], rec_lb_logits, rec_gnorm_w, sent["rec_w_out"], loss_target, distributed=True)

    plane_sums = [sum_slots(parts[name]) for name in _BIG]
    other_sums = sibling_exchange(plane_sums)
    out_g, out_d, out_m, out_v = {}, {}, {}, {}
    for name, mine, other in zip(_BIG, plane_sums, other_sums):
        g, d, nm, nv = adamw(shards[name], m[name][0], v[name][0], mine, other)
        out_g[name], out_d[name], out_m[name], out_v[name] = g[None], d[None], nm[None], nv[None]

    small_sum = sum_slots(all_gather_small(_pack_small(grads, last_row=loss_vec)))
    loss = jnp.sum(small_sum[_SMALL_ROWS - 1]) * (0.5 / D_MODEL)
    packed = adamw(_pack_small(w), _pack_small(m), _pack_small(v), small_sum)
    for dst, val in zip((out_g, out_d, out_m, out_v), packed):
        dst.update(_unpack_small(val))

    return (loss, grad_x, *[out_g[n] for n in _ORDER], *[out_d[n] for n in _ORDER],
            *[out_m[n] for n in _ORDER], *[out_v[n] for n in _ORDER])
```

```python
import functools

import jax
import jax.numpy as jnp
from jax import lax
from jax.experimental import pallas as pl
from jax.experimental.pallas import tpu as pltpu

F32 = jnp.float32
BF16 = jnp.bfloat16
MESH = pl.DeviceIdType.MESH

D_MODEL = 1024
HEAD_DIM = 64
N_HEADS = 16
N_KV_HEADS = 2
GROUP = N_HEADS // N_KV_HEADS
KV_WIDTH = N_KV_HEADS * HEAD_DIM
ATTN_IN = 2 * D_MODEL + 2 * KV_WIDTH
ATTN_BLOCK = 128
ROPE_THETA = 500000.0
ROPE_DIM = HEAD_DIM // 4
REC_HEADS = 8
REC_DIM = 128
REC_IN = 4 * D_MODEL
REC_BLOCK = 128
DIAG = 8
NORM_EPS = 1e-6
N_CHIPS = 4
N_DEV = 8
LANES = 128

ADAM_LR = 0.001
ADAM_B1 = 0.9
ADAM_B2 = 0.999
ADAM_EPS = 1e-08
ADAM_WD = 0.01
ADAM_STEP = 10

VMEM_LIMIT = 56 * 1024 * 1024


def _cparams(n_axes):
    return pltpu.CompilerParams(dimension_semantics=("arbitrary",) * n_axes, vmem_limit_bytes=VMEM_LIMIT)


def _dot(a, b, contract):
    return lax.dot_general(a.astype(BF16), b.astype(BF16), (contract, ((), ())), preferred_element_type=F32)


_NN = ((1,), (0,))
_NT = ((1,), (1,))
_TN = ((0,), (0,))


@jax.custom_vjp
def mm_nn(a, b):
    return _dot(a, b, _NN)


mm_nn.defvjp(lambda a, b: (_dot(a, b, _NN), (a, b)),
             lambda res, g: (_dot(g, res[1], _NT), _dot(res[0], g, _TN)))


@jax.custom_vjp
def mm_nt(a, b):
    return _dot(a, b, _NT)


mm_nt.defvjp(lambda a, b: (_dot(a, b, _NT), (a, b)),
             lambda res, g: (_dot(g, res[1], _NN), _dot(g, res[0], _TN)))


@jax.custom_vjp
def mm_tn(a, b):
    return _dot(a, b, _TN)


mm_tn.defvjp(lambda a, b: (_dot(a, b, _TN), (a, b)),
             lambda res, g: (_dot(res[1], g, _NT), _dot(res[0], g, _NN)))


def _tri_dot(x, lower):
    n = x.shape[0]
    r = lax.broadcasted_iota(jnp.int32, (n, n), 0)
    c = lax.broadcasted_iota(jnp.int32, (n, n), 1)
    tri = ((c <= r) if lower else (c >= r)).astype(BF16)
    hi = x.astype(BF16)
    rest = x - hi.astype(F32)
    mid = rest.astype(BF16)
    lo = (rest - mid.astype(F32)).astype(BF16)
    dot = lambda p: lax.dot_general(tri, p, (_NN, ((), ())), preferred_element_type=F32)
    return (dot(lo) + dot(mid)) + dot(hi)


@jax.custom_vjp
def cumsum_rows(x):
    return _tri_dot(x, True)


cumsum_rows.defvjp(lambda x: (cumsum_rows(x), None), lambda _, g: (_tri_dot(g, False),))


@functools.partial(jax.custom_vjp, nondiff_argnums=(1,))
def roll_sub(x, d):
    return pltpu.roll(x, d, 1) if d else x


roll_sub.defvjp(lambda x, d: (roll_sub(x, d), None),
                lambda d, _, g: (roll_sub(g, (DIAG - d) % DIAG),))


def sigmoid(x):
    return 1.0 / (1.0 + jnp.exp(-x))


@jax.custom_vjp
def silu(x):
    return x * sigmoid(x)


def _silu_fwd(x):
    s = sigmoid(x)
    return x * s, (x, s)


silu.defvjp(_silu_fwd, lambda res, g: (g * (res[1] * (1.0 + res[0] * (1.0 - res[1]))),))


F32_TINY = 1.17549435e-38


def sigmoid_pair(x):
    e = jnp.exp(-jnp.abs(x))
    r = 1.0 / (1.0 + e)
    er = e * r
    pos = x >= 0.0
    return jnp.where(pos, r, er), jnp.where(pos, er, r)


def _forget_fwd(x, a):
    lb, one_m_lb = sigmoid_pair(a)
    sp, sn = sigmoid_pair(x)
    f = lb + one_m_lb * sp
    k = one_m_lb * sn
    return (jnp.log(jnp.maximum(f, F32_TINY)), k), (sp, sn, f, k, lb, one_m_lb)


def _forget_bwd(res, g):
    sp, sn, f, k, lb, one_m_lb = res
    g_lf, g_k = g
    t = jnp.where(f >= F32_TINY, g_lf / jnp.maximum(f, F32_TINY), 0.0) - g_k
    return (k * sp) * t, jnp.sum(sn * t, axis=0, keepdims=True) * (lb * one_m_lb)


@jax.custom_vjp
def forget_gate(x, a):
    return _forget_fwd(x, a)[0]


forget_gate.defvjp(_forget_fwd, _forget_bwd)


@jax.custom_vjp
def decayed(x, e):
    return (x * jnp.exp(e)).astype(BF16).astype(F32)


def _decayed_fwd(x, e):
    y = decayed(x, e)
    return y, (y, e)


decayed.defvjp(_decayed_fwd, lambda res, g: (g * jnp.exp(res[1]), g * res[0]))


def _row(x, r):
    shape = x.shape

    @jax.custom_vjp
    def take(x):
        return x[r:r + 1, :]

    take.defvjp(lambda x: (x[r:r + 1, :], None),
                lambda _, g: (jnp.where(lax.broadcasted_iota(jnp.int32, shape, 0) == r, g, 0.0),))
    return take(x)


def _rms(x):
    return lax.rsqrt(jnp.mean(x * x, axis=-1, keepdims=True) + NORM_EPS)


def _attn_group(qs, k_a, v_a, k_b, v_b, zs, sink_a, sink_b, bias, at_sink=None):
    def half(kh, vh, sink):
        s = mm_nn(qs, kh) + bias
        if at_sink is None:
            m = jnp.maximum(jnp.max(s, axis=-1, keepdims=True), jnp.max(sink, axis=-1, keepdims=True))
            p = jnp.exp(s - lax.stop_gradient(m))
            own = jnp.sum(jnp.exp(sink - lax.stop_gradient(m)), axis=-1, keepdims=True) * (1.0 / LANES)
            return mm_nt(p * (1.0 / (jnp.sum(p, axis=-1, keepdims=True) + own)), vh)
        s = jnp.where(at_sink, jnp.concatenate([sink, sink], axis=1), s)
        p = jnp.exp(s - jnp.max(s, axis=-1, keepdims=True))
        return mm_nt(jnp.where(at_sink, 0.0, p * (1.0 / jnp.sum(p, axis=-1, keepdims=True))), vh)

    return (half(k_a, v_a, sink_a) + half(k_b, v_b, sink_b)) * silu(zs)


SAFE_RANGE = 80.0


def _rec_front(qr, fr, l0, l1):
    lf, k = forget_gate(fr, l1 - l0)
    return silu(qr), k, lf


def _rec_tail(o, z, gw):
    return o * _rms(o) * gw * silu(z)


def _rec_margin(b):
    R = b.shape[0]
    mid, last = _row(b, R // 2 - 1), _row(b, R - 1)
    return jnp.minimum(mid, last - mid)


def _heads(x):
    w = x.shape[1] // REC_HEADS
    return [x[:, h * w:(h + 1) * w] for h in range(REC_HEADS)]


def _hdot(a, b, contract):
    return jnp.concatenate([_dot(ah, bh, contract) for ah, bh in zip(_heads(a), _heads(b))], axis=1)


@jax.custom_vjp
def hmm_nn(a, b):
    return _hdot(a, b, _NN)


hmm_nn.defvjp(lambda a, b: (_hdot(a, b, _NN), (a, b)),
              lambda res, g: (_hdot(g, res[1], _NT), _hdot(res[0], g, _TN)))


@jax.custom_vjp
def hmm_nt(a, b):
    return _hdot(a, b, _NT)


hmm_nt.defvjp(lambda a, b: (_hdot(a, b, _NT), (a, b)),
              lambda res, g: (_hdot(g, res[1], _NN), _hdot(g, res[0], _TN)))


@jax.custom_vjp
def hmm_tn(a, b):
    return _hdot(a, b, _TN)


hmm_tn.defvjp(lambda a, b: (_hdot(a, b, _TN), (a, b)),
              lambda res, g: (_hdot(res[1], g, _NT), _hdot(res[0], g, _NN)))


def _head_sums(x):
    return jnp.concatenate([jnp.broadcast_to(jnp.sum(xh, axis=-1, keepdims=True), xh.shape) for xh in _heads(x)],
                           axis=1)


@jax.custom_vjp
def head_sum(x):
    return _head_sums(x)


head_sum.defvjp(lambda x: (_head_sums(x), None), lambda _, g: (_head_sums(g),))


def _rec_cores_fast(q, k, v, b, S):
    R = q.shape[0]
    ri = lax.broadcasted_iota(jnp.int32, (R, REC_HEADS * R), 0)
    ci = lax.broadcasted_iota(jnp.int32, (R, REC_HEADS * R), 1) % R
    d = b - _row(b, R // 2 - 1)
    sc = jnp.where(ci < ri, hmm_nt(decayed(q, d), decayed(k, -d)), 0.0)
    o = hmm_nt(q * jnp.exp(b), S) + hmm_nn(sc, v) + head_sum(q * k) * v
    b_last = _row(b, R - 1)
    return o, S * jnp.exp(b_last) + hmm_tn(v, k * jnp.exp(b_last - b))


def _rec_tails(o, z, gw):
    return o * lax.rsqrt(head_sum(o * o) * (1.0 / REC_DIM) + NORM_EPS) * gw * silu(z)


def _rec_block_fast(qr, fr, v, z, S, l0, l1, gw):
    lf, k = forget_gate(fr, l1 - l0)
    o, S_new = _rec_cores_fast(silu(qr), k, v, cumsum_rows(lf), S)
    return _rec_tails(o, z, gw), S_new


def _rec_core_slow(q, k, v, b, S):
    R = q.shape[0]
    rows = lax.broadcasted_iota(jnp.int32, (R, REC_DIM), 0)

    o = mm_nt(q * jnp.exp(jnp.minimum(b, 0.0)), S)

    ri = lax.broadcasted_iota(jnp.int32, (R, R), 0)
    ci = lax.broadcasted_iota(jnp.int32, (R, R), 1)
    sc = jnp.zeros((R, R), F32)
    w = R
    while w > DIAG:
        h = w // 2
        b3 = b.reshape(R // w, w, REC_DIM)
        rin = lax.broadcasted_iota(jnp.int32, (R // w, w, REC_DIM), 1)
        mid = jnp.sum(jnp.where(rin == h - 1, b3, 0.0), axis=1, keepdims=True)
        fac = jnp.exp(jnp.minimum(jnp.where(rin >= h, b3 - mid, mid - b3), 0.0)).reshape(R, REC_DIM)
        upper = (rows % w) >= h
        s_w = mm_nt(jnp.where(upper, q * fac, 0.0), jnp.where(upper, 0.0, k * fac))
        sc = sc + jnp.where((ri // w) == (ci // w), s_w, 0.0)
        w = h
    o = o + mm_nn(sc, v)

    g = R // DIAG
    q3, k3, v3, b3 = (t.reshape(g, DIAG, REC_DIM) for t in (q, k, v, b))
    rin = lax.broadcasted_iota(jnp.int32, (g, DIAG, 1), 1)
    od = jnp.zeros((g, DIAG, REC_DIM), F32)
    for d in range(DIAG):
        e = jnp.exp(jnp.minimum(b3 - roll_sub(b3, d), 0.0))
        sd = jnp.sum(q3 * roll_sub(k3, d) * e, axis=-1, keepdims=True)
        od = od + jnp.where(rin >= d, sd, 0.0) * roll_sub(v3, d)
    o = o + od.reshape(R, REC_DIM)

    b_last = _row(b, R - 1)
    return o, S * jnp.exp(jnp.minimum(b_last, 0.0)) + mm_tn(v, k * jnp.exp(jnp.minimum(b_last - b, 0.0)))


def _rec_head(core, qr, fr, v, z, S, l0, l1, gw):
    q, k, lf = _rec_front(qr, fr, l0, l1)
    o, S_new = core(q, k, v, cumsum_rows(lf), S)
    return _rec_tail(o, z, gw), S_new


ANGLE_COLS = 3 * ROPE_DIM


def _rope_angles(positions):
    half = ROPE_DIM // 2
    inv_freq = ROPE_THETA ** (-(jnp.arange(half, dtype=F32) * 2.0 / ROPE_DIM))
    ang = positions.astype(F32).reshape(-1, 1) * inv_freq
    cs = jnp.concatenate([jnp.cos(ang), jnp.sin(ang)], axis=-1)
    hi = cs.astype(BF16)
    rest = cs - hi.astype(F32)
    mid = rest.astype(BF16)
    return jnp.concatenate([hi, mid, (rest - mid.astype(F32)).astype(BF16)], axis=-1)


def _rope_tables(pieces):
    half = ROPE_DIM // 2
    r = lax.broadcasted_iota(jnp.int32, (ANGLE_COLS, 3 * LANES), 0) % ROPE_DIM
    c = lax.broadcasted_iota(jnp.int32, (ANGLE_COLS, 3 * LANES), 1)
    table, j = c // LANES, c % HEAD_DIM
    angle, low = j % half, j < half
    plus = ((table == 0) & (j < ROPE_DIM) & (r == angle)) | ((table == 1) & (j >= half) & (j < ROPE_DIM)
                                                                & (r == half + angle))
    minus = (table == 2) & low & (r == half + angle)
    pick = jnp.where(plus, 1.0, jnp.where(minus, -1.0, 0.0)).astype(BF16)
    out = jnp.dot(pieces, pick, preferred_element_type=F32)
    lane = lax.broadcasted_iota(jnp.int32, (1, LANES), 1) % HEAD_DIM
    return out[:, :LANES] + jnp.where(lane < ROPE_DIM, 0.0, 1.0), out[:, LANES:2 * LANES], out[:, 2 * LANES:]


def _rope(x, cos_t, sin_a, sin_b):
    half = ROPE_DIM // 2
    return x * cos_t + pltpu.roll(x, half, 1) * sin_a + pltpu.roll(x, LANES - half, 1) * sin_b


def _rope_transposed(g, cos_t, sin_a, sin_b):
    half = ROPE_DIM // 2
    return g * cos_t + pltpu.roll(g * sin_a, LANES - half, 1) + pltpu.roll(g * sin_b, half, 1)


def _row_spec(tm, width):
    return pl.BlockSpec((tm, width), lambda i: (i, 0))


def _weight_spec(shape):
    return pl.BlockSpec(shape, lambda *_: (0,) * len(shape), pipeline_mode=pl.Buffered(1))


def _full_spec(shape):
    return pl.BlockSpec(shape, lambda *_: (0,) * len(shape))


def attn_in_proj(x, w_pre, w_in, b_in, angles, tm=1024):
    n = x.shape[0]
    tm = min(tm, n)

    def body(x_ref, wp_ref, w_ref, b_ref, cs_ref, h_ref, q_ref, k_ref, v_ref, z_ref):
        xv = x_ref[...]
        h = (xv * _rms(xv) * wp_ref[...]).astype(BF16)
        h_ref[...] = h
        proj = jnp.dot(h, w_ref[...], preferred_element_type=F32) + b_ref[...]
        tabs = _rope_tables(cs_ref[...])
        for s in range(D_MODEL // LANES):
            sl = slice(s * LANES, (s + 1) * LANES)
            q_ref[:, sl] = _rope(proj[:, sl] * (HEAD_DIM ** -0.5), *tabs).astype(BF16)
        k_ref[...] = _rope(proj[:, D_MODEL:D_MODEL + KV_WIDTH], *tabs).astype(BF16)
        v_ref[...] = proj[:, D_MODEL + KV_WIDTH:D_MODEL + 2 * KV_WIDTH].astype(BF16)
        z_ref[...] = proj[:, D_MODEL + 2 * KV_WIDTH:]

    return pl.pallas_call(
        body, name="attn_in_proj", grid=(n // tm,),
        in_specs=[_row_spec(tm, D_MODEL), _full_spec((1, D_MODEL)), _weight_spec((D_MODEL, ATTN_IN)),
                  _full_spec((1, ATTN_IN)), _row_spec(tm, ANGLE_COLS)],
        out_specs=[_row_spec(tm, D_MODEL), _row_spec(tm, D_MODEL), _row_spec(tm, KV_WIDTH),
                   _row_spec(tm, KV_WIDTH), _row_spec(tm, D_MODEL)],
        out_shape=[jax.ShapeDtypeStruct((n, D_MODEL), BF16), jax.ShapeDtypeStruct((n, D_MODEL), BF16),
                   jax.ShapeDtypeStruct((n, KV_WIDTH), BF16), jax.ShapeDtypeStruct((n, KV_WIDTH), BF16),
                   jax.ShapeDtypeStruct((n, D_MODEL), F32)],
        compiler_params=_cparams(1),
    )(x, w_pre, w_in, b_in, angles)


def _column_blocks(w):
    if len(w.shape) == 2:
        return [slice(0, w.shape[1])], lambda ref, s: ref[...]
    width = w.shape[2]
    return [slice(s * width, (s + 1) * width) for s in range(w.shape[0])], lambda ref, s: ref[s]


def rec_in_proj(x, w_pre, w_in, tm=1024):
    n = x.shape[0]
    tm = min(tm, n)
    columns, block = _column_blocks(w_in)

    def body(x_ref, wp_ref, w_ref, h_ref, p_ref):
        xv = x_ref[...]
        h = (xv * _rms(xv) * wp_ref[...]).astype(BF16)
        h_ref[...] = h
        for s, cols in enumerate(columns):
            p_ref[:, cols] = jnp.dot(h, block(w_ref, s), preferred_element_type=F32)

    return pl.pallas_call(
        body, name="rec_in_proj", grid=(n // tm,),
        in_specs=[_row_spec(tm, D_MODEL), _full_spec((1, D_MODEL)), _weight_spec(w_in.shape)],
        out_specs=[_row_spec(tm, D_MODEL), _row_spec(tm, REC_IN)],
        out_shape=[jax.ShapeDtypeStruct((n, D_MODEL), BF16), jax.ShapeDtypeStruct((n, REC_IN), F32)],
        compiler_params=_cparams(1),
    )(x, w_pre, w_in)


def out_proj(og, w_out, b_out, x_res, w_post, target=None, tm=1024):
    n = og.shape[0]
    tm = min(tm, n)
    with_loss = target is not None

    def body(*refs):
        if with_loss:
            og_ref, w_ref, b_ref, x_ref, wp_ref, t_ref, y_ref, dx_ref, l_ref = refs
        else:
            og_ref, w_ref, b_ref, x_ref, wp_ref, y_ref, xo_ref = refs
        y = jnp.dot(og_ref[...], w_ref[...], preferred_element_type=F32) + b_ref[...]
        y_ref[...] = y.astype(BF16)
        xo = x_ref[...] + y * _rms(y) * wp_ref[...]
        if with_loss:
            err = xo - t_ref[...]
            dx_ref[...] = err * (1.0 / D_MODEL)

            @pl.when(pl.program_id(0) == 0)
            def _():
                l_ref[...] = jnp.zeros_like(l_ref)

            l_ref[...] += jnp.sum(err * err, axis=0, keepdims=True)
        else:
            xo_ref[...] = xo

    in_specs = [_row_spec(tm, D_MODEL), _weight_spec((D_MODEL, D_MODEL)), _full_spec((1, D_MODEL)),
                _row_spec(tm, D_MODEL), _full_spec((1, D_MODEL))]
    out_specs = [_row_spec(tm, D_MODEL), _row_spec(tm, D_MODEL)]
    out_shape = [jax.ShapeDtypeStruct((n, D_MODEL), BF16), jax.ShapeDtypeStruct((n, D_MODEL), F32)]
    args = [og, w_out, b_out, x_res, w_post]
    if with_loss:
        in_specs.append(_row_spec(tm, D_MODEL))
        out_specs.append(_full_spec((1, D_MODEL)))
        out_shape.append(jax.ShapeDtypeStruct((1, D_MODEL), F32))
        args.append(target)
    return pl.pallas_call(
        body, name="out_proj_loss" if with_loss else "out_proj", grid=(n // tm,),
        in_specs=in_specs, out_specs=out_specs, out_shape=out_shape, compiler_params=_cparams(1),
    )(*args)


def out_proj_bwd(dxo, y, og, w_out, w_post, tm=1024):
    n = og.shape[0]
    tm = min(tm, n)

    def body(g_ref, y_ref, og_ref, w_ref, wp_ref, dog_ref, dw_ref, db_ref, dwp_ref):
        @pl.when(pl.program_id(0) == 0)
        def _():
            dw_ref[...] = jnp.zeros_like(dw_ref)
            db_ref[...] = jnp.zeros_like(db_ref)
            dwp_ref[...] = jnp.zeros_like(dwp_ref)

        g, y = g_ref[...], y_ref[...].astype(F32)
        rstd = _rms(y)
        yn = y * rstd
        gw = g * wp_ref[...]
        dwp_ref[...] += jnp.sum(g * yn, axis=0, keepdims=True)
        dy = rstd * (gw - yn * jnp.mean(gw * yn, axis=-1, keepdims=True))
        db_ref[...] += jnp.sum(dy, axis=0, keepdims=True)
        dyb = dy.astype(BF16)
        dog_ref[...] = _dot(dyb, w_ref[...], _NT).astype(BF16)
        dw_ref[...] += _dot(og_ref[...], dyb, _TN)

    return pl.pallas_call(
        body, name="out_proj_bwd", grid=(n // tm,),
        in_specs=[_row_spec(tm, D_MODEL), _row_spec(tm, D_MODEL), _row_spec(tm, D_MODEL),
                  _weight_spec((D_MODEL, D_MODEL)), _full_spec((1, D_MODEL))],
        out_specs=[_row_spec(tm, D_MODEL), _full_spec((D_MODEL, D_MODEL)), _full_spec((1, D_MODEL)),
                   _full_spec((1, D_MODEL))],
        out_shape=[jax.ShapeDtypeStruct((n, D_MODEL), BF16), jax.ShapeDtypeStruct((D_MODEL, D_MODEL), F32),
                   jax.ShapeDtypeStruct((1, D_MODEL), F32), jax.ShapeDtypeStruct((1, D_MODEL), F32)],
        compiler_params=_cparams(1),
    )(dxo, y, og, w_out, w_post)


def in_proj_bwd_x(dproj, w_in, x, w_pre, dxo, tm=1024, scatter=()):
    n, p = dproj.shape
    tm = min(tm, n)
    steps = n // tm
    ns = len(scatter)
    columns, block = _column_blocks(w_in)

    def body(*refs):
        dp_ref, w_ref, x_ref, wp_ref, g_ref = refs[:5]
        dx_ref, dwp_ref = refs[5 + ns:7 + ns]
        exchange = (refs[5:5 + ns], refs[7 + ns:7 + 2 * ns]) + tuple(refs[7 + 2 * ns:])

        @pl.when(pl.program_id(0) == 0)
        def _():
            dwp_ref[...] = jnp.zeros_like(dwp_ref)
            if ns:
                _scatter_start(*exchange)

        dh = functools.reduce(jnp.add, [_dot(dp_ref[:, cols], block(w_ref, s), _NT)
                                        for s, cols in enumerate(columns)])
        xv = x_ref[...]
        rstd = _rms(xv)
        xn = xv * rstd
        gw = dh * wp_ref[...]
        dwp_ref[...] += jnp.sum(dh * xn, axis=0, keepdims=True)
        dx_ref[...] = rstd * (gw - xn * jnp.mean(gw * xn, axis=-1, keepdims=True)) + g_ref[...]

        if ns:
            @pl.when(pl.program_id(0) == steps - 1)
            def _():
                _scatter_finish(*exchange)

    out = pl.pallas_call(
        body, name=f"in_proj_bwd_x_{p}", grid=(steps,),
        in_specs=[_row_spec(tm, p), _weight_spec(w_in.shape), _row_spec(tm, D_MODEL), _full_spec((1, D_MODEL)),
                  _row_spec(tm, D_MODEL)] + [_ANY] * ns,
        out_specs=[_row_spec(tm, D_MODEL), _full_spec((1, D_MODEL))] + [_ANY] * ns,
        out_shape=[jax.ShapeDtypeStruct((n, D_MODEL), F32), jax.ShapeDtypeStruct((1, D_MODEL), F32)]
        + [jax.ShapeDtypeStruct(a.shape, a.dtype) for a in scatter],
        scratch_shapes=_scatter_sems(ns) if ns else [],
        compiler_params=_cparams(1),
    )(dproj, w_in, x, w_pre, dxo, *scatter)
    return out[0], out[1], out[2:]


def in_proj_bwd_w(h, dproj, tm=1024, as_shards=False):
    n, p = dproj.shape
    chunk = p // (4 if p % 4096 == 0 else 3)
    tm = min(tm, n)
    steps = n // tm
    shard = p // N_CHIPS

    def body(h_ref, dp_ref, dw_ref, db_ref, acc_scr, sem, *staging):
        i = pl.program_id(0)

        @pl.when(i == 0)
        def _():
            acc_scr[...] = jnp.zeros_like(acc_scr)
            db_ref[...] = jnp.zeros_like(db_ref)

        ht = h_ref[...].T
        for c0 in range(0, p, chunk):
            dp = dp_ref[:, c0:c0 + chunk]
            acc_scr[:, c0:c0 + chunk] += jnp.dot(ht, dp, preferred_element_type=F32)
            db_ref[:, c0:c0 + chunk] += jnp.sum(dp.astype(F32), axis=0, keepdims=True)

        @pl.when(i == steps - 1)
        def _():
            if as_shards:
                for s in range(N_CHIPS):
                    staging[0][...] = acc_scr[:, s * shard:(s + 1) * shard].astype(BF16)
                    out = pltpu.make_async_copy(staging[0], dw_ref.at[s], sem)
                    out.start()
                    out.wait()
            else:
                out = pltpu.make_async_copy(acc_scr, dw_ref, sem)
                out.start()
                out.wait()

    dw_shape = jax.ShapeDtypeStruct((N_CHIPS, D_MODEL, shard), BF16) if as_shards else (
        jax.ShapeDtypeStruct((D_MODEL, p), F32))
    return pl.pallas_call(
        body, name=f"in_proj_bwd_w_{p}", grid=(steps,),
        in_specs=[_row_spec(tm, D_MODEL), _row_spec(tm, p)],
        out_specs=[_ANY, _full_spec((1, p))],
        out_shape=[dw_shape, jax.ShapeDtypeStruct((1, p), F32)],
        scratch_shapes=[pltpu.VMEM((D_MODEL, p), F32), pltpu.SemaphoreType.DMA]
        + ([pltpu.VMEM((D_MODEL, shard), BF16)] if as_shards else []),
        compiler_params=_cparams(1),
    )(h, dproj)


PAIRS = GROUP // 2
GROUP_ROWS = PAIRS * ATTN_BLOCK
MASKED = -1e30


def _kv_windows(k_ref, v_ref, i):
    ps = pl.multiple_of(jnp.maximum(i - 1, 0) * ATTN_BLOCK, ATTN_BLOCK)
    cs = pl.multiple_of(i * ATTN_BLOCK, ATTN_BLOCK)
    kw = jnp.concatenate([k_ref[pl.ds(ps, ATTN_BLOCK), :], k_ref[pl.ds(cs, ATTN_BLOCK), :]], axis=0)
    vw = jnp.concatenate([v_ref[pl.ds(ps, ATTN_BLOCK), :], v_ref[pl.ds(cs, ATTN_BLOCK), :]], axis=0)
    return kw.astype(F32).T, vw.astype(F32).T, ps, cs


def _low_rows(shape):
    return lax.broadcasted_iota(jnp.int32, shape, 0) < HEAD_DIM


def _spread(w, kvh):
    low = _low_rows(w.shape)
    swapped = pltpu.roll(w, HEAD_DIM, 0)
    if kvh == 0:
        return jnp.where(low, w, 0.0), jnp.where(low, 0.0, swapped)
    return jnp.where(low, swapped, 0.0), jnp.where(low, 0.0, w)


def _unspread(d_a, d_b, kvh):
    low = _low_rows(d_a.shape)
    if kvh == 0:
        return jnp.where(low, d_a + pltpu.roll(d_b, HEAD_DIM, 0), 0.0)
    return jnp.where(low, 0.0, pltpu.roll(d_a, HEAD_DIM, 0) + d_b)


def _stack_pairs(ref, kvh):
    return jnp.concatenate([ref[:, (kvh * PAIRS + j) * LANES:(kvh * PAIRS + j + 1) * LANES] for j in range(PAIRS)],
                           axis=0)


def _fill_bias(bias_scr):
    shape = (GROUP_ROWS, 2 * ATTN_BLOCK)
    r = lax.broadcasted_iota(jnp.int32, shape, 0) % ATTN_BLOCK
    c = lax.broadcasted_iota(jnp.int32, shape, 1)
    in_cur = (c >= ATTN_BLOCK) & ((c - ATTN_BLOCK) <= r)
    in_prev = (c < ATTN_BLOCK) & (c > r)
    bias_scr[0] = jnp.where(in_cur, 0.0, MASKED)
    bias_scr[1] = jnp.where(in_cur | in_prev, 0.0, MASKED)
    bias_scr[2] = jnp.where(c == r, 1.0, 0.0)


N_BIAS_TABLES = 3


def _sink_table(sinks):
    t = jnp.transpose(sinks.reshape(N_KV_HEADS, PAIRS, 2), (0, 2, 1))
    return jnp.broadcast_to(t[:, :, :, None, None], (N_KV_HEADS, 2, PAIRS, ATTN_BLOCK, LANES)).reshape(
        N_KV_HEADS, 2, GROUP_ROWS, LANES)


def attn_fwd(q, k, v, z, sink_tab, batch, seq, gather=()):
    nb = seq // ATTN_BLOCK
    ng = len(gather)

    def body(*refs):
        q_ref, k_ref, v_ref, z_ref, s_ref = refs[:5]
        og_ref, bias_scr = refs[5 + ng], refs[6 + 2 * ng]
        exchange = (refs[5:5 + ng], refs[6 + ng:6 + 2 * ng]) + tuple(refs[7 + 2 * ng:])
        b, i = pl.program_id(0), pl.program_id(1)

        @pl.when((b == 0) & (i == 0))
        def _():
            _fill_bias(bias_scr)
            if ng:
                _gather_start(*exchange)

        kw, vw, _, _ = _kv_windows(k_ref, v_ref, i)
        bias, at_sink = bias_scr[jnp.minimum(i, 1)], bias_scr[2] > 0.5
        for kvh in range(N_KV_HEADS):
            k_a, k_b = _spread(kw, kvh)
            v_a, v_b = _spread(vw, kvh)
            og = _attn_group(_stack_pairs(q_ref, kvh), k_a, v_a, k_b, v_b, _stack_pairs(z_ref, kvh),
                             s_ref[kvh, 0], s_ref[kvh, 1], bias, at_sink)
            for j in range(PAIRS):
                og_ref[:, (kvh * PAIRS + j) * LANES:(kvh * PAIRS + j + 1) * LANES] = (
                    og[j * ATTN_BLOCK:(j + 1) * ATTN_BLOCK].astype(BF16))

        if ng:
            @pl.when((b == batch - 1) & (i == nb - 1))
            def _():
                _gather_finish(*exchange)

    blk = lambda w: pl.BlockSpec((ATTN_BLOCK, w), lambda b, i: (b * nb + i, 0))
    seq_spec = pl.BlockSpec((seq, KV_WIDTH), lambda b, i: (b, 0))
    out = pl.pallas_call(
        body, name="attn_fwd", grid=(batch, nb),
        in_specs=[blk(D_MODEL), seq_spec, seq_spec, blk(D_MODEL), _full_spec(sink_tab.shape)] + [_ANY] * ng,
        out_specs=[blk(D_MODEL)] + [_ANY] * ng,
        out_shape=[jax.ShapeDtypeStruct((batch * seq, D_MODEL), BF16)]
        + [jax.ShapeDtypeStruct((N_CHIPS,) + a.shape, a.dtype) for a in gather],
        scratch_shapes=[pltpu.VMEM((N_BIAS_TABLES, GROUP_ROWS, 2 * ATTN_BLOCK), F32)] + (_gather_sems(ng) if ng else []),
        compiler_params=_cparams(2),
    )(q, k, v, z, sink_tab, *gather)
    return out[0], out[1:]


def attn_bwd(q, k, v, z, sink_tab, dog, angles, batch, seq, scatter=()):
    nb = seq // ATTN_BLOCK
    ns = len(scatter)

    def body(*refs):
        q_ref, k_ref, v_ref, z_ref, s_ref, g_ref, cs_ref = refs[:7]
        dp_ref, dk_ref, dv_ref, ds_ref = refs[7 + ns:11 + ns]
        bias_scr = refs[11 + 2 * ns]
        exchange = (refs[7:7 + ns], refs[11 + ns:11 + 2 * ns]) + tuple(refs[12 + 2 * ns:])
        b, i = pl.program_id(0), pl.program_id(1)

        @pl.when((b == 0) & (i == 0))
        def _():
            _fill_bias(bias_scr)
            ds_ref[...] = jnp.zeros_like(ds_ref)
            if ns:
                _scatter_start(*exchange)

        @pl.when(i == 0)
        def _():
            dk_ref[...] = jnp.zeros_like(dk_ref)
            dv_ref[...] = jnp.zeros_like(dv_ref)

        kw, vw, ps, cs = _kv_windows(k_ref, v_ref, i)
        bias = bias_scr[jnp.minimum(i, 1)]
        tabs = _rope_tables(cs_ref[...])
        dkw = jnp.zeros_like(kw)
        dvw = jnp.zeros_like(vw)
        for kvh in range(N_KV_HEADS):
            k_a, k_b = _spread(kw, kvh)
            v_a, v_b = _spread(vw, kvh)
            _, vjp = jax.vjp(functools.partial(_attn_group, bias=bias), _stack_pairs(q_ref, kvh).astype(F32),
                             k_a, v_a, k_b, v_b, _stack_pairs(z_ref, kvh), s_ref[kvh, 0], s_ref[kvh, 1])
            dqs, dk_a, dv_a, dk_b, dv_b, dzs, ds_a, ds_b = vjp(_stack_pairs(g_ref, kvh).astype(F32))
            dkw = dkw + _unspread(dk_a, dk_b, kvh)
            dvw = dvw + _unspread(dv_a, dv_b, kvh)
            ds_ref[kvh, 0] += jnp.sum(ds_a.reshape(PAIRS, ATTN_BLOCK, LANES), axis=1)
            ds_ref[kvh, 1] += jnp.sum(ds_b.reshape(PAIRS, ATTN_BLOCK, LANES), axis=1)
            for j in range(PAIRS):
                rows = slice(j * ATTN_BLOCK, (j + 1) * ATTN_BLOCK)
                col = (kvh * PAIRS + j) * LANES
                dp_ref[:, col:col + LANES] = _rope_transposed(dqs[rows] * (HEAD_DIM ** -0.5), *tabs).astype(BF16)
                zc = D_MODEL + 2 * KV_WIDTH + col
                dp_ref[:, zc:zc + LANES] = dzs[rows].astype(BF16)
        dp_ref[:, D_MODEL:D_MODEL + 2 * KV_WIDTH] = jnp.zeros((ATTN_BLOCK, 2 * KV_WIDTH), BF16)
        dk_ref[:, pl.ds(ps, ATTN_BLOCK)] += dkw[:, :ATTN_BLOCK]
        dk_ref[:, pl.ds(cs, ATTN_BLOCK)] += dkw[:, ATTN_BLOCK:]
        dv_ref[:, pl.ds(ps, ATTN_BLOCK)] += dvw[:, :ATTN_BLOCK]
        dv_ref[:, pl.ds(cs, ATTN_BLOCK)] += dvw[:, ATTN_BLOCK:]

        if ns:
            @pl.when((b == batch - 1) & (i == nb - 1))
            def _():
                _scatter_finish(*exchange)

    blk = lambda w: pl.BlockSpec((ATTN_BLOCK, w), lambda b, i: (b * nb + i, 0))
    seq_spec = pl.BlockSpec((seq, KV_WIDTH), lambda b, i: (b, 0))
    seq_spec_t = pl.BlockSpec((KV_WIDTH, seq), lambda b, i: (0, b))
    n = batch * seq
    ds_shape = (N_KV_HEADS, 2, PAIRS, LANES)
    out = pl.pallas_call(
        body, name="attn_bwd", grid=(batch, nb),
        in_specs=[blk(D_MODEL), seq_spec, seq_spec, blk(D_MODEL), _full_spec(sink_tab.shape), blk(D_MODEL)]
        + [blk(ANGLE_COLS)] + [_ANY] * ns,
        out_specs=[blk(ATTN_IN), seq_spec_t, seq_spec_t, _full_spec(ds_shape)] + [_ANY] * ns,
        out_shape=[jax.ShapeDtypeStruct((n, ATTN_IN), BF16), jax.ShapeDtypeStruct((KV_WIDTH, n), F32),
                   jax.ShapeDtypeStruct((KV_WIDTH, n), F32), jax.ShapeDtypeStruct(ds_shape, F32)]
        + [jax.ShapeDtypeStruct(a.shape, a.dtype) for a in scatter],
        scratch_shapes=[pltpu.VMEM((N_BIAS_TABLES, GROUP_ROWS, 2 * ATTN_BLOCK), F32)] + (_scatter_sems(ns) if ns else []),
        compiler_params=_cparams(2),
    )(q, k, v, z, sink_tab, dog, angles, *scatter)
    return out[0], out[1], out[2], out[3], out[4:]


def attn_bwd_kv(dproj, dk_t, dv_t, angles, tm=512):
    n = dproj.shape[0]

    def body(dp_in_ref, dk_ref, dv_ref, cs_ref, dp_ref):
        del dp_in_ref
        dp_ref[:, :KV_WIDTH] = _rope_transposed(dk_ref[...].T, *_rope_tables(cs_ref[...])).astype(BF16)
        dp_ref[:, KV_WIDTH:] = dv_ref[...].T.astype(BF16)

    kv_cols = pl.BlockSpec((tm, 2 * KV_WIDTH), lambda i: (i, D_MODEL // (2 * KV_WIDTH)))
    col_spec = pl.BlockSpec((KV_WIDTH, tm), lambda i: (0, i))
    return pl.pallas_call(
        body, name="attn_bwd_kv", grid=(n // tm,),
        in_specs=[kv_cols, col_spec, col_spec, _row_spec(tm, ANGLE_COLS)],
        out_specs=kv_cols, out_shape=jax.ShapeDtypeStruct(dproj.shape, BF16),
        input_output_aliases={0: 0}, compiler_params=_cparams(1),
    )(dproj, dk_t, dv_t, angles)


def rec_fwd(proj, lb_logits, gnorm_w, batch, seq):
    nblk = seq // REC_BLOCK

    def body(p_ref, lb_ref, gw_ref, og_ref, st_ref, safe_ref, s_scr):
        @pl.when(pl.program_id(1) == 0)
        def _():
            s_scr[...] = jnp.zeros_like(s_scr)

        S = s_scr[...]
        st_ref[0] = S
        qr, fr, v, z = (p_ref[:, part * D_MODEL:(part + 1) * D_MODEL] for part in range(4))
        lf, k = forget_gate(fr, lb_ref[1:2, :] - lb_ref[0:1, :])
        q, b = silu(qr), cumsum_rows(lf)
        safe = jnp.min(_rec_margin(b)) >= -SAFE_RANGE

        gate = gw_ref[...] * silu(z)
        safe_ref[0] = jnp.full((REC_HEADS, LANES), safe.astype(F32))

        def store(o, S_new):
            og_ref[...] = (o * lax.rsqrt(head_sum(o * o) * (1.0 / REC_DIM) + NORM_EPS) * gate).astype(BF16)
            s_scr[...] = S_new

        @pl.when(safe)
        def _():
            store(*_rec_cores_fast(q, k, v, b, S))

        @pl.when(jnp.logical_not(safe))
        def _():
            outs = [_rec_core_slow(*args) for args in zip(*(_heads(t) for t in (q, k, v, b, S)))]
            store(*(jnp.concatenate(parts, axis=1) for parts in zip(*outs)))

    blk = lambda w: pl.BlockSpec((REC_BLOCK, w), lambda b, j: (b * nblk + j, 0))
    st_spec = pl.BlockSpec((1, REC_DIM, D_MODEL), lambda b, j: (b * nblk + j, 0, 0))
    safe_spec = pl.BlockSpec((1, REC_HEADS, LANES), lambda b, j: (b * nblk + j, 0, 0))
    return pl.pallas_call(
        body, name="rec_fwd", grid=(batch, nblk),
        in_specs=[blk(REC_IN), _full_spec((2, D_MODEL)), _full_spec((1, D_MODEL))],
        out_specs=[blk(D_MODEL), st_spec, safe_spec],
        out_shape=[jax.ShapeDtypeStruct((batch * seq, D_MODEL), BF16),
                   jax.ShapeDtypeStruct((batch * nblk, REC_DIM, D_MODEL), F32),
                   jax.ShapeDtypeStruct((batch * nblk, REC_HEADS, LANES), F32)],
        scratch_shapes=[pltpu.VMEM((REC_DIM, D_MODEL), F32)],
        compiler_params=_cparams(2),
    )(proj, lb_logits, jnp.tile(gnorm_w, (1, REC_HEADS)))


def rec_bwd(proj, states, safe, lb_logits, gnorm_w, dog, batch, seq):
    nblk = seq // REC_BLOCK

    def body(p_ref, st_ref, safe_ref, lb_ref, gw_ref, g_ref, dp_ref, dlb_ref, dgw_ref, ds_scr):
        @pl.when((pl.program_id(0) == 0) & (pl.program_id(1) == 0))
        def _():
            dlb_ref[...] = jnp.zeros_like(dlb_ref)
            dgw_ref[...] = jnp.zeros_like(dgw_ref)

        @pl.when(pl.program_id(1) == 0)
        def _():
            ds_scr[...] = jnp.zeros_like(ds_scr)

        def load():
            primals = tuple(p_ref[:, part * D_MODEL:(part + 1) * D_MODEL] for part in range(4)) + (
                st_ref[0], lb_ref[0:1, :], lb_ref[1:2, :], gw_ref[...])
            return primals, (g_ref[...].astype(F32), ds_scr[...])

        def store(dqr, dfr, dv, dz, dS, dl0, dl1, dgw):
            for part, val in enumerate((dqr, dfr, dv, dz)):
                dp_ref[:, part * D_MODEL:(part + 1) * D_MODEL] = val.astype(BF16)
            ds_scr[...] = dS
            dlb_ref[0:1, :] += dl0
            dlb_ref[1:2, :] += dl1
            dgw_ref[...] += functools.reduce(jnp.add, _heads(dgw))

        fast = jnp.max(safe_ref[0]) > 0.5

        @pl.when(fast)
        def _():
            primals, cotangents = load()
            store(*jax.vjp(_rec_block_fast, *primals)[1](cotangents))

        @pl.when(jnp.logical_not(fast))
        def _():
            primals, cotangents = load()
            outs = [jax.vjp(functools.partial(_rec_head, _rec_core_slow), *args)[1](cts)
                    for args, cts in zip(zip(*(_heads(t) for t in primals)), zip(*(_heads(t) for t in cotangents)))]
            store(*(jnp.concatenate(parts, axis=1) for parts in zip(*outs)))

    blk = lambda w: pl.BlockSpec((REC_BLOCK, w), lambda b, j: (b * nblk + nblk - 1 - j, 0))
    st_spec = pl.BlockSpec((1, REC_DIM, D_MODEL), lambda b, j: (b * nblk + nblk - 1 - j, 0, 0))
    safe_spec = pl.BlockSpec((1, REC_HEADS, LANES), lambda b, j: (b * nblk + nblk - 1 - j, 0, 0))
    return pl.pallas_call(
        body, name="rec_bwd", grid=(batch, nblk),
        in_specs=[blk(REC_IN), st_spec, safe_spec, _full_spec((2, D_MODEL)), _full_spec((1, D_MODEL)),
                  blk(D_MODEL)],
        out_specs=[blk(REC_IN), _full_spec((2, D_MODEL)), _full_spec((1, REC_DIM))],
        out_shape=[jax.ShapeDtypeStruct((batch * seq, REC_IN), BF16), jax.ShapeDtypeStruct((2, D_MODEL), F32),
                   jax.ShapeDtypeStruct((1, REC_DIM), F32)],
        scratch_shapes=[pltpu.VMEM((REC_DIM, D_MODEL), F32)],
        compiler_params=_cparams(2),
    )(proj, states, safe, lb_logits, jnp.tile(gnorm_w, (1, REC_HEADS)), dog)


_ANY = pl.BlockSpec(memory_space=pl.ANY)


def _chip_peers():
    x, y, c = lax.axis_index("x"), lax.axis_index("y"), lax.axis_index("c")
    peers = []
    for fx, fy in ((1, 0), (0, 1), (1, 1)):
        px, py = (1 - x if fx else x), (1 - y if fy else y)
        peers.append(((px, py, c), 2 * px + py))
    return 2 * x + y, peers


def _remote(src, dst, send_sem, recv_sem, device):
    return pltpu.make_async_remote_copy(src_ref=src, dst_ref=dst, send_sem=send_sem, recv_sem=recv_sem,
                                        device_id=device, device_id_type=MESH)


N_FLIPS = N_CHIPS - 1


def _scatter_sems(n):
    return [pltpu.SemaphoreType.DMA((n * N_FLIPS,)), pltpu.SemaphoreType.DMA((n * N_FLIPS,)),
            pltpu.SemaphoreType.DMA((n,))]


def _scatter_copies(ins, outs, send_sems, recv_sems, local_sems, starting):
    me, peers = _chip_peers()
    local = [pltpu.make_async_copy(ins[k].at[me], outs[k].at[me], local_sems.at[k]) for k in range(len(ins))]
    sends, arrivals = [], []
    for k in range(len(ins)):
        for j, (device, idx) in enumerate(peers):
            sems = (send_sems.at[k * N_FLIPS + j], recv_sems.at[k * N_FLIPS + j], device)
            sends.append(_remote(ins[k].at[idx], outs[k].at[me], *sems))
            if not starting:
                arrivals.append(_remote(ins[k].at[me], outs[k].at[idx], *sems))
    return local, sends, arrivals


def _scatter_start(*refs):
    local, sends, _ = _scatter_copies(*refs, starting=True)
    for cp in local + sends:
        cp.start()


def _scatter_finish(*refs):
    local, sends, arrivals = _scatter_copies(*refs, starting=False)
    for cp in arrivals:
        cp.wait_recv()
    for cp in sends:
        cp.wait_send()
    for cp in local:
        cp.wait()


def _gather_sems(n):
    return [pltpu.SemaphoreType.DMA((n * N_FLIPS,)) for _ in range(4)] + [pltpu.SemaphoreType.DMA((n,))]


def _gather_copies(ins, outs, send_sems, recv_sems, pass_send_sems, pass_recv_sems, local_sems, starting):
    me, peers = _chip_peers()
    c = lax.axis_index("c")
    sibling = (lax.axis_index("x"), lax.axis_index("y"), 1 - c)
    local = [pltpu.make_async_copy(ins[k], outs[k].at[me], local_sems.at[k]) for k in range(len(ins))]
    sends, arrivals, passes, pass_arrivals = [], [], [], []
    for k in range(len(ins)):
        half = ins[k].shape[0] // 2
        mine, other = pl.ds(c * half, half), pl.ds((1 - c) * half, half)
        for j, (device, idx) in enumerate(peers):
            s = k * N_FLIPS + j
            sends.append(_remote(ins[k].at[mine], outs[k].at[me].at[mine], send_sems.at[s], recv_sems.at[s], device))
            if starting:
                continue
            arrived = outs[k].at[idx].at[mine]
            arrivals.append(_remote(ins[k].at[mine], arrived, send_sems.at[s], recv_sems.at[s], device))
            passes.append(_remote(arrived, arrived, pass_send_sems.at[s], pass_recv_sems.at[s], sibling))
            passed = outs[k].at[idx].at[other]
            pass_arrivals.append(_remote(passed, passed, pass_send_sems.at[s], pass_recv_sems.at[s], sibling))
    return local, sends, arrivals, passes, pass_arrivals


def _gather_start(*refs):
    local, sends, _, _, _ = _gather_copies(*refs, starting=True)
    for cp in local + sends:
        cp.start()


def _gather_finish(*refs):
    local, sends, arrivals, passes, pass_arrivals = _gather_copies(*refs, starting=False)
    for arrival, onward in zip(arrivals, passes):
        arrival.wait_recv()
        onward.start()
    for cp in pass_arrivals:
        cp.wait_recv()
    for cp in sends + passes:
        cp.wait_send()
    for cp in local:
        cp.wait()


def chip_gather(arrays):
    n = len(arrays)

    def body(*refs):
        _gather_start(refs[:n], refs[n:2 * n], *refs[2 * n:])
        _gather_finish(refs[:n], refs[n:2 * n], *refs[2 * n:])

    return pl.pallas_call(
        body, name="chip_gather", in_specs=[_ANY] * n, out_specs=[_ANY] * n,
        out_shape=[jax.ShapeDtypeStruct((N_CHIPS,) + a.shape, a.dtype) for a in arrays],
        scratch_shapes=_gather_sems(n),
    )(*arrays)


def sibling_exchange(arrays):
    n = len(arrays)

    def body(*refs):
        ins, outs = refs[:n], refs[n:2 * n]
        send_sems, recv_sems = refs[2 * n:]
        sibling = (lax.axis_index("x"), lax.axis_index("y"), 1 - lax.axis_index("c"))
        copies = [pltpu.make_async_remote_copy(src_ref=ins[k], dst_ref=outs[k], send_sem=send_sems.at[k],
                                               recv_sem=recv_sems.at[k], device_id=sibling, device_id_type=MESH)
                  for k in range(n)]
        for cp in copies:
            cp.start()
        for cp in copies:
            cp.wait()

    return pl.pallas_call(
        body, name="sibling_exchange", in_specs=[_ANY] * n, out_specs=[_ANY] * n,
        out_shape=[jax.ShapeDtypeStruct(a.shape, a.dtype) for a in arrays],
        scratch_shapes=[pltpu.SemaphoreType.DMA((n,)), pltpu.SemaphoreType.DMA((n,))],
    )(*arrays)


def all_gather_small(vec):
    def body(v_ref, out_ref, send_sems, recv_sems, local_sem):
        x, y, c = lax.axis_index("x"), lax.axis_index("y"), lax.axis_index("c")
        me = 4 * x + 2 * y + c
        local = pltpu.make_async_copy(v_ref, out_ref.at[me], local_sem)
        local.start()
        sends, recvs = [], []
        for j in range(1, N_DEV):
            px = jnp.where(j & 4, 1 - x, x)
            py = jnp.where(j & 2, 1 - y, y)
            pc = jnp.where(j & 1, 1 - c, c)
            common = dict(send_sem=send_sems.at[j - 1], recv_sem=recv_sems.at[j - 1], device_id=(px, py, pc),
                          device_id_type=MESH)
            sends.append(pltpu.make_async_remote_copy(src_ref=v_ref, dst_ref=out_ref.at[me], **common))
            recvs.append(pltpu.make_async_remote_copy(src_ref=v_ref, dst_ref=out_ref.at[4 * px + 2 * py + pc],
                                                      **common))
        for cp in sends:
            cp.start()
        for cp in recvs:
            cp.wait_recv()
        for cp in sends:
            cp.wait_send()
        local.wait()

    return pl.pallas_call(
        body, name="all_gather_small", in_specs=[_ANY], out_specs=_ANY,
        out_shape=jax.ShapeDtypeStruct((N_DEV,) + vec.shape, vec.dtype),
        scratch_shapes=[pltpu.SemaphoreType.DMA((N_DEV - 1,)), pltpu.SemaphoreType.DMA((N_DEV - 1,)),
                        pltpu.SemaphoreType.DMA],
    )(vec)


def sum_slots(stacked, tm=256):
    s, r, c = stacked.shape
    tm = min(tm, r)

    def body(in_ref, out_ref):
        acc = in_ref[0].astype(F32)
        for t in range(1, s):
            acc = acc + in_ref[t].astype(F32)
        out_ref[...] = acc

    return pl.pallas_call(
        body, name=f"sum_slots_{s}_{r}_{c}", grid=(r // tm,),
        in_specs=[pl.BlockSpec((s, tm, c), lambda i: (0, i, 0))], out_specs=_row_spec(tm, c),
        out_shape=jax.ShapeDtypeStruct((r, c), F32), compiler_params=_cparams(1),
    )(stacked)


def adamw(w, m, v, g_a, g_b=None, tm=256):
    r, c = w.shape
    tm = min(tm, r)
    two = g_b is not None

    def body(*refs):
        w_ref, m_ref, v_ref, ga_ref = refs[:4]
        g_ref, d_ref, nm_ref, nv_ref = refs[-4:]
        g = ga_ref[...] + refs[4][...] if two else ga_ref[...]
        nm = ADAM_B1 * m_ref[...] + (1.0 - ADAM_B1) * g
        nv = ADAM_B2 * v_ref[...] + (1.0 - ADAM_B2) * (g * g)
        m_hat = nm / (1.0 - ADAM_B1 ** ADAM_STEP)
        v_hat = nv / (1.0 - ADAM_B2 ** ADAM_STEP)
        g_ref[...] = g
        d_ref[...] = -ADAM_LR * (m_hat / (jnp.sqrt(v_hat) + ADAM_EPS) + ADAM_WD * w_ref[...])
        nm_ref[...] = nm
        nv_ref[...] = nv

    args = [w, m, v, g_a] + ([g_b] if two else [])
    return pl.pallas_call(
        body, name=f"adamw_{r}_{c}", grid=(r // tm,),
        in_specs=[_row_spec(tm, c)] * len(args), out_specs=[_row_spec(tm, c)] * 4,
        out_shape=[jax.ShapeDtypeStruct((r, c), F32)] * 4, compiler_params=_cparams(1),
    )(*args)


_SMALL = (("pre_norm_w", (2, D_MODEL)), ("post_norm_w", (2, D_MODEL)), ("attn_b_in", (1, ATTN_IN)),
          ("attn_sinks", (1, N_HEADS)), ("attn_b_out", (1, D_MODEL)), ("rec_lb_logits", (2, D_MODEL)),
          ("rec_gnorm_w", (1, REC_DIM)))
_SMALL_ROWS = 16


def _pack_small(parts, last_row=None):
    rows = []
    for (name, shape) in _SMALL:
        flat = parts[name].reshape(-1)
        pad = -flat.shape[0] % D_MODEL
        rows.append(jnp.pad(flat, (0, pad)).reshape(-1, D_MODEL))
    used = sum(r.shape[0] for r in rows)
    rows.append(jnp.zeros((_SMALL_ROWS - 1 - used, D_MODEL), F32))
    rows.append(jnp.zeros((1, D_MODEL), F32) if last_row is None else last_row)
    return jnp.concatenate(rows, axis=0)


def _unpack_small(packed):
    out, row = {}, 0
    for (name, shape) in _SMALL:
        size = shape[0] * shape[1]
        nrows = -(-size // D_MODEL)
        out[name] = packed[row:row + nrows].reshape(-1)[:size].reshape(shape)
        row += nrows
    return out


_CARRIED = ("rec_w_in", "rec_w_out", "attn_w_out")


_LATE = ("attn_w_out", "rec_w_in", "rec_w_out")


def local_step(x, positions, pre_norm_w, post_norm_w, attn_w_in, attn_b_in, attn_sinks, attn_w_out, attn_b_out,
               rec_w_in, rec_lb_logits, rec_gnorm_w, rec_w_out, loss_target, distributed=False):
    batch, seq, _ = x.shape
    n = batch * seq
    x0 = x.reshape(n, D_MODEL)
    angles = _rope_angles(positions)
    pre0, pre1 = pre_norm_w[0:1], pre_norm_w[1:2]
    post0, post1 = post_norm_w[0:1], post_norm_w[1:2]
    no_bias = jnp.zeros((1, D_MODEL), F32)

    h0, q, k, v, z = attn_in_proj(x0, pre0, attn_w_in, attn_b_in, angles)
    sink_tab = _sink_table(attn_sinks)
    late = (attn_w_out, rec_w_in, rec_w_out)
    og0, gathered = attn_fwd(q, k, v, z, sink_tab, batch, seq, gather=late if distributed else ())
    if distributed:
        attn_w_out, rec_w_in, rec_w_out = (g if name == "rec_w_in" else _whole_from_shards(name, g)
                                           for name, g in zip(_LATE, gathered))
    y0, x1 = out_proj(og0, attn_w_out, attn_b_out, x0, post0)

    h1, proj1 = rec_in_proj(x1, pre1, rec_w_in)
    og1, states, safe = rec_fwd(proj1, rec_lb_logits, rec_gnorm_w, batch, seq)
    y1, dx2, loss_vec = out_proj(og1, rec_w_out, no_bias, x1, post1, target=loss_target.reshape(n, D_MODEL))

    dog1, d_rec_w_out, _, d_post1 = out_proj_bwd(dx2, y1, og1, rec_w_out, post1)
    dproj1, d_lb, d_gnorm = rec_bwd(proj1, states, safe, rec_lb_logits, rec_gnorm_w, dog1, batch, seq)
    dx1, d_pre1, _ = in_proj_bwd_x(dproj1, rec_w_in, x1, pre1, dx2)
    d_rec_w_in, _ = in_proj_bwd_w(h1, dproj1, as_shards=distributed)

    dog0, d_attn_w_out, d_attn_b_out, d_post0 = out_proj_bwd(dx1, y0, og0, attn_w_out, post0)
    ready = dict(rec_w_out=d_rec_w_out, attn_w_out=d_attn_w_out)
    outgoing = [d_rec_w_in if name == "rec_w_in" else _shards_from_whole(name, ready[name]).astype(BF16)
                for name in _CARRIED] if distributed else []
    dproj0, dk, dv, d_sink_tab, arrived = attn_bwd(q, k, v, z, sink_tab, dog0, angles, batch, seq, scatter=outgoing)
    d_sinks = jnp.transpose(jnp.sum(d_sink_tab, axis=-1), (0, 2, 1)).reshape(1, N_HEADS)
    dproj0 = attn_bwd_kv(dproj0, dk, dv, angles)
    d_attn_w_in, d_attn_b_in = in_proj_bwd_w(h0, dproj0)
    last = [_shards_from_whole("attn_w_in", d_attn_w_in).astype(BF16)] if distributed else []
    dx0, d_pre0, arrived_last = in_proj_bwd_x(dproj0, attn_w_in, x0, pre0, dx1, scatter=last)

    grads = dict(
        pre_norm_w=jnp.concatenate([d_pre0, d_pre1], axis=0), post_norm_w=jnp.concatenate([d_post0, d_post1], axis=0),
        attn_w_in=d_attn_w_in, attn_b_in=d_attn_b_in, attn_sinks=d_sinks, attn_w_out=d_attn_w_out,
        attn_b_out=d_attn_b_out, rec_w_in=d_rec_w_in, rec_lb_logits=d_lb, rec_gnorm_w=d_gnorm,
        rec_w_out=d_rec_w_out)
    parts = dict(zip(_CARRIED + ("attn_w_in",), tuple(arrived) + tuple(arrived_last)))
    return loss_vec, dx0.reshape(batch, seq, D_MODEL), grads, parts


_BIG = ("attn_w_in", "attn_w_out", "rec_w_in", "rec_w_out")
_COLUMN_SHARDED = ("attn_w_in", "rec_w_in")
_ORDER = ("pre_norm_w", "post_norm_w", "attn_w_in", "attn_b_in", "attn_sinks", "attn_w_out", "attn_b_out",
          "rec_w_in", "rec_lb_logits", "rec_gnorm_w", "rec_w_out")


def _whole_from_shards(name, stacked):
    if name in _COLUMN_SHARDED:
        return jnp.transpose(stacked, (1, 0, 2)).reshape(stacked.shape[1], -1)
    return stacked.reshape(-1, stacked.shape[2])


def _shards_from_whole(name, whole):
    if name in _COLUMN_SHARDED:
        return jnp.transpose(whole.reshape(whole.shape[0], N_CHIPS, -1), (1, 0, 2))
    return whole.reshape(N_CHIPS, -1, whole.shape[1])


def kernel(x, positions, pre_norm_w, post_norm_w, attn_w_in, attn_b_in, attn_sinks, attn_w_out, attn_b_out, rec_w_in, rec_lb_logits, rec_gnorm_w, rec_w_out, loss_target, m_pre_norm_w, m_post_norm_w, m_attn_w_in, m_attn_b_in, m_attn_sinks, m_attn_w_out, m_attn_b_out, m_rec_w_in, m_rec_lb_logits, m_rec_gnorm_w, m_rec_w_out, v_pre_norm_w, v_post_norm_w, v_attn_w_in, v_attn_b_in, v_attn_sinks, v_attn_w_out, v_attn_b_out, v_rec_w_in, v_rec_lb_logits, v_rec_gnorm_w, v_rec_w_out):
    w = dict(pre_norm_w=pre_norm_w, post_norm_w=post_norm_w, attn_w_in=attn_w_in, attn_b_in=attn_b_in,
             attn_sinks=attn_sinks, attn_w_out=attn_w_out, attn_b_out=attn_b_out, rec_w_in=rec_w_in,
             rec_lb_logits=rec_lb_logits, rec_gnorm_w=rec_gnorm_w, rec_w_out=rec_w_out)
    m = dict(pre_norm_w=m_pre_norm_w, post_norm_w=m_post_norm_w, attn_w_in=m_attn_w_in, attn_b_in=m_attn_b_in,
             attn_sinks=m_attn_sinks, attn_w_out=m_attn_w_out, attn_b_out=m_attn_b_out, rec_w_in=m_rec_w_in,
             rec_lb_logits=m_rec_lb_logits, rec_gnorm_w=m_rec_gnorm_w, rec_w_out=m_rec_w_out)
    v = dict(pre_norm_w=v_pre_norm_w, post_norm_w=v_post_norm_w, attn_w_in=v_attn_w_in, attn_b_in=v_attn_b_in,
             attn_sinks=v_attn_sinks, attn_w_out=v_attn_w_out, attn_b_out=v_attn_b_out, rec_w_in=v_rec_w_in,
             rec_lb_logits=v_rec_lb_logits, rec_gnorm_w=v_rec_gnorm_w, rec_w_out=v_rec_w_out)

    shards = {name: w[name][0] for name in _BIG}
    sent = {name: shards[name].astype(BF16) for name in _BIG}
    attn_w_in_whole = _whole_from_shards("attn_w_in", chip_gather([sent["attn_w_in"]])[0])

    loss_vec, grad_x, grads, parts = local_step(
        x, positions, pre_norm_w, post_norm_w, attn_w_in_whole, attn_b_in, attn_sinks, sent["attn_w_out"],
        attn_b_out, sent["rec_w_in"], rec_lb_logits, rec_gnorm_w, sent["rec_w_out"], loss_target, distributed=True)

    plane_sums = [sum_slots(parts[name]) for name in _BIG]
    other_sums = sibling_exchange(plane_sums)
    out_g, out_d, out_m, out_v = {}, {}, {}, {}
    for name, mine, other in zip(_BIG, plane_sums, other_sums):
        g, d, nm, nv = adamw(shards[name], m[name][0], v[name][0], mine, other)
        out_g[name], out_d[name], out_m[name], out_v[name] = g[None], d[None], nm[None], nv[None]

    small_sum = sum_slots(all_gather_small(_pack_small(grads, last_row=loss_vec)))
    loss = jnp.sum(small_sum[_SMALL_ROWS - 1]) * (0.5 / D_MODEL)
    packed = adamw(_pack_small(w), _pack_small(m), _pack_small(v), small_sum)
    for dst, val in zip((out_g, out_d, out_m, out_v), packed):
        dst.update(_unpack_small(val))

    return (loss, grad_x, *[out_g[n] for n in _ORDER], *[out_d[n] for n in _ORDER],
            *[out_m[n] for n in _ORDER], *[out_v[n] for n in _ORDER])
```

```python
import functools

import jax
import jax.numpy as jnp
from jax import lax
from jax.experimental import pallas as pl
from jax.experimental.pallas import tpu as pltpu

F32 = jnp.float32
BF16 = jnp.bfloat16
MESH = pl.DeviceIdType.MESH

D_MODEL = 1024
HEAD_DIM = 64
N_HEADS = 16
N_KV_HEADS = 2
GROUP = N_HEADS // N_KV_HEADS
KV_WIDTH = N_KV_HEADS * HEAD_DIM
ATTN_IN = 2 * D_MODEL + 2 * KV_WIDTH
ATTN_BLOCK = 128
ROPE_THETA = 500000.0
ROPE_DIM = HEAD_DIM // 4
REC_HEADS = 8
REC_DIM = 128
REC_IN = 4 * D_MODEL
REC_BLOCK = 128
DIAG = 8
NORM_EPS = 1e-6
N_CHIPS = 4
N_DEV = 8
LANES = 128

ADAM_LR = 0.001
ADAM_B1 = 0.9
ADAM_B2 = 0.999
ADAM_EPS = 1e-08
ADAM_WD = 0.01
ADAM_STEP = 10

VMEM_LIMIT = 56 * 1024 * 1024


def _cparams(n_axes):
    return pltpu.CompilerParams(dimension_semantics=("arbitrary",) * n_axes, vmem_limit_bytes=VMEM_LIMIT)


def _dot(a, b, contract):
    return lax.dot_general(a.astype(BF16), b.astype(BF16), (contract, ((), ())), preferred_element_type=F32)


_NN = ((1,), (0,))
_NT = ((1,), (1,))
_TN = ((0,), (0,))


@jax.custom_vjp
def mm_nn(a, b):
    return _dot(a, b, _NN)


mm_nn.defvjp(lambda a, b: (_dot(a, b, _NN), (a, b)),
             lambda res, g: (_dot(g, res[1], _NT), _dot(res[0], g, _TN)))


@jax.custom_vjp
def mm_nt(a, b):
    return _dot(a, b, _NT)


mm_nt.defvjp(lambda a, b: (_dot(a, b, _NT), (a, b)),
             lambda res, g: (_dot(g, res[1], _NN), _dot(g, res[0], _TN)))


@jax.custom_vjp
def mm_tn(a, b):
    return _dot(a, b, _TN)


mm_tn.defvjp(lambda a, b: (_dot(a, b, _TN), (a, b)),
             lambda res, g: (_dot(res[1], g, _NT), _dot(res[0], g, _NN)))


def _tri_dot(x, lower):
    n = x.shape[0]
    r = lax.broadcasted_iota(jnp.int32, (n, n), 0)
    c = lax.broadcasted_iota(jnp.int32, (n, n), 1)
    tri = ((c <= r) if lower else (c >= r)).astype(BF16)
    hi = x.astype(BF16)
    rest = x - hi.astype(F32)
    mid = rest.astype(BF16)
    lo = (rest - mid.astype(F32)).astype(BF16)
    dot = lambda p: lax.dot_general(tri, p, (_NN, ((), ())), preferred_element_type=F32)
    return (dot(lo) + dot(mid)) + dot(hi)


@jax.custom_vjp
def cumsum_rows(x):
    return _tri_dot(x, True)


cumsum_rows.defvjp(lambda x: (cumsum_rows(x), None), lambda _, g: (_tri_dot(g, False),))


@functools.partial(jax.custom_vjp, nondiff_argnums=(1,))
def roll_sub(x, d):
    return pltpu.roll(x, d, 1) if d else x


roll_sub.defvjp(lambda x, d: (roll_sub(x, d), None),
                lambda d, _, g: (roll_sub(g, (DIAG - d) % DIAG),))


def sigmoid(x):
    return 1.0 / (1.0 + jnp.exp(-x))


@jax.custom_vjp
def silu(x):
    return x * sigmoid(x)


def _silu_fwd(x):
    s = sigmoid(x)
    return x * s, (x, s)


silu.defvjp(_silu_fwd, lambda res, g: (g * (res[1] * (1.0 + res[0] * (1.0 - res[1]))),))


F32_TINY = 1.17549435e-38


def sigmoid_pair(x):
    e = jnp.exp(-jnp.abs(x))
    r = 1.0 / (1.0 + e)
    er = e * r
    pos = x >= 0.0
    return jnp.where(pos, r, er), jnp.where(pos, er, r)


def _forget_fwd(x, a):
    lb, one_m_lb = sigmoid_pair(a)
    sp, sn = sigmoid_pair(x)
    f = lb + one_m_lb * sp
    k = one_m_lb * sn
    return (jnp.log(jnp.maximum(f, F32_TINY)), k), (sp, sn, f, k, lb, one_m_lb)


def _forget_bwd(res, g):
    sp, sn, f, k, lb, one_m_lb = res
    g_lf, g_k = g
    t = jnp.where(f >= F32_TINY, g_lf / jnp.maximum(f, F32_TINY), 0.0) - g_k
    return (k * sp) * t, jnp.sum(sn * t, axis=0, keepdims=True) * (lb * one_m_lb)


@jax.custom_vjp
def forget_gate(x, a):
    return _forget_fwd(x, a)[0]


forget_gate.defvjp(_forget_fwd, _forget_bwd)


@jax.custom_vjp
def decayed(x, e):
    return (x * jnp.exp(e)).astype(BF16).astype(F32)


def _decayed_fwd(x, e):
    y = decayed(x, e)
    return y, (y, e)


decayed.defvjp(_decayed_fwd, lambda res, g: (g * jnp.exp(res[1]), g * res[0]))


def _row(x, r):
    shape = x.shape

    @jax.custom_vjp
    def take(x):
        return x[r:r + 1, :]

    take.defvjp(lambda x: (x[r:r + 1, :], None),
                lambda _, g: (jnp.where(lax.broadcasted_iota(jnp.int32, shape, 0) == r, g, 0.0),))
    return take(x)


def _rms(x):
    return lax.rsqrt(jnp.mean(x * x, axis=-1, keepdims=True) + NORM_EPS)


def _attn_group(qs, k_a, v_a, k_b, v_b, zs, sink_a, sink_b, bias, at_sink=None):
    def half(kh, vh, sink):
        s = mm_nn(qs, kh) + bias
        if at_sink is None:
            m = jnp.maximum(jnp.max(s, axis=-1, keepdims=True), jnp.max(sink, axis=-1, keepdims=True))
            p = jnp.exp(s - lax.stop_gradient(m))
            own = jnp.sum(jnp.exp(sink - lax.stop_gradient(m)), axis=-1, keepdims=True) * (1.0 / LANES)
            return mm_nt(p * (1.0 / (jnp.sum(p, axis=-1, keepdims=True) + own)), vh)
        s = jnp.where(at_sink, jnp.concatenate([sink, sink], axis=1), s)
        p = jnp.exp(s - jnp.max(s, axis=-1, keepdims=True))
        return mm_nt(jnp.where(at_sink, 0.0, p), vh) * (1.0 / jnp.sum(p, axis=-1, keepdims=True))

    return (half(k_a, v_a, sink_a) + half(k_b, v_b, sink_b)) * silu(zs)


SAFE_RANGE = 80.0


def _rec_front(qr, fr, l0, l1):
    lf, k = forget_gate(fr, l1 - l0)
    return silu(qr), k, lf


def _rec_tail(o, z, gw):
    return o * _rms(o) * gw * silu(z)


def _rec_margin(b):
    R = b.shape[0]
    mid, last = _row(b, R // 2 - 1), _row(b, R - 1)
    return jnp.minimum(mid, last - mid)


def _heads(x):
    w = x.shape[1] // REC_HEADS
    return [x[:, h * w:(h + 1) * w] for h in range(REC_HEADS)]


def _hdot(a, b, contract):
    return jnp.concatenate([_dot(ah, bh, contract) for ah, bh in zip(_heads(a), _heads(b))], axis=1)


@jax.custom_vjp
def hmm_nn(a, b):
    return _hdot(a, b, _NN)


hmm_nn.defvjp(lambda a, b: (_hdot(a, b, _NN), (a, b)),
              lambda res, g: (_hdot(g, res[1], _NT), _hdot(res[0], g, _TN)))


@jax.custom_vjp
def hmm_nt(a, b):
    return _hdot(a, b, _NT)


hmm_nt.defvjp(lambda a, b: (_hdot(a, b, _NT), (a, b)),
              lambda res, g: (_hdot(g, res[1], _NN), _hdot(g, res[0], _TN)))


@jax.custom_vjp
def hmm_tn(a, b):
    return _hdot(a, b, _TN)


hmm_tn.defvjp(lambda a, b: (_hdot(a, b, _TN), (a, b)),
              lambda res, g: (_hdot(res[1], g, _NT), _hdot(res[0], g, _NN)))


def _head_sums(x):
    return jnp.concatenate([jnp.broadcast_to(jnp.sum(xh, axis=-1, keepdims=True), xh.shape) for xh in _heads(x)],
                           axis=1)


@jax.custom_vjp
def head_sum(x):
    return _head_sums(x)


head_sum.defvjp(lambda x: (_head_sums(x), None), lambda _, g: (_head_sums(g),))


def _rec_cores_fast(q, k, v, b, S):
    R = q.shape[0]
    ri = lax.broadcasted_iota(jnp.int32, (R, REC_HEADS * R), 0)
    ci = lax.broadcasted_iota(jnp.int32, (R, REC_HEADS * R), 1) % R
    d = b - _row(b, R // 2 - 1)
    sc = jnp.where(ci < ri, hmm_nt(decayed(q, d), decayed(k, -d)), 0.0)
    o = hmm_nt(q * jnp.exp(b), S) + hmm_nn(sc, v) + head_sum(q * k) * v
    b_last = _row(b, R - 1)
    return o, S * jnp.exp(b_last) + hmm_tn(v, k * jnp.exp(b_last - b))


def _rec_tails(o, z, gw):
    return o * lax.rsqrt(head_sum(o * o) * (1.0 / REC_DIM) + NORM_EPS) * gw * silu(z)


def _rec_block_fast(qr, fr, v, z, S, l0, l1, gw):
    lf, k = forget_gate(fr, l1 - l0)
    o, S_new = _rec_cores_fast(silu(qr), k, v, cumsum_rows(lf), S)
    return _rec_tails(o, z, gw), S_new


def _rec_core_slow(q, k, v, b, S):
    R = q.shape[0]
    rows = lax.broadcasted_iota(jnp.int32, (R, REC_DIM), 0)

    o = mm_nt(q * jnp.exp(jnp.minimum(b, 0.0)), S)

    ri = lax.broadcasted_iota(jnp.int32, (R, R), 0)
    ci = lax.broadcasted_iota(jnp.int32, (R, R), 1)
    sc = jnp.zeros((R, R), F32)
    w = R
    while w > DIAG:
        h = w // 2
        b3 = b.reshape(R // w, w, REC_DIM)
        rin = lax.broadcasted_iota(jnp.int32, (R // w, w, REC_DIM), 1)
        mid = jnp.sum(jnp.where(rin == h - 1, b3, 0.0), axis=1, keepdims=True)
        fac = jnp.exp(jnp.minimum(jnp.where(rin >= h, b3 - mid, mid - b3), 0.0)).reshape(R, REC_DIM)
        upper = (rows % w) >= h
        s_w = mm_nt(jnp.where(upper, q * fac, 0.0), jnp.where(upper, 0.0, k * fac))
        sc = sc + jnp.where((ri // w) == (ci // w), s_w, 0.0)
        w = h
    o = o + mm_nn(sc, v)

    g = R // DIAG
    q3, k3, v3, b3 = (t.reshape(g, DIAG, REC_DIM) for t in (q, k, v, b))
    rin = lax.broadcasted_iota(jnp.int32, (g, DIAG, 1), 1)
    od = jnp.zeros((g, DIAG, REC_DIM), F32)
    for d in range(DIAG):
        e = jnp.exp(jnp.minimum(b3 - roll_sub(b3, d), 0.0))
        sd = jnp.sum(q3 * roll_sub(k3, d) * e, axis=-1, keepdims=True)
        od = od + jnp.where(rin >= d, sd, 0.0) * roll_sub(v3, d)
    o = o + od.reshape(R, REC_DIM)

    b_last = _row(b, R - 1)
    return o, S * jnp.exp(jnp.minimum(b_last, 0.0)) + mm_tn(v, k * jnp.exp(jnp.minimum(b_last - b, 0.0)))


def _rec_head(core, qr, fr, v, z, S, l0, l1, gw):
    q, k, lf = _rec_front(qr, fr, l0, l1)
    o, S_new = core(q, k, v, cumsum_rows(lf), S)
    return _rec_tail(o, z, gw), S_new


ANGLE_COLS = 3 * ROPE_DIM


def _rope_angles(positions):
    half = ROPE_DIM // 2
    inv_freq = ROPE_THETA ** (-(jnp.arange(half, dtype=F32) * 2.0 / ROPE_DIM))
    ang = positions.astype(F32).reshape(-1, 1) * inv_freq
    cs = jnp.concatenate([jnp.cos(ang), jnp.sin(ang)], axis=-1)
    hi = cs.astype(BF16)
    rest = cs - hi.astype(F32)
    mid = rest.astype(BF16)
    return jnp.concatenate([hi, mid, (rest - mid.astype(F32)).astype(BF16)], axis=-1)


def _rope_tables(pieces):
    half = ROPE_DIM // 2
    r = lax.broadcasted_iota(jnp.int32, (ANGLE_COLS, 3 * LANES), 0) % ROPE_DIM
    c = lax.broadcasted_iota(jnp.int32, (ANGLE_COLS, 3 * LANES), 1)
    table, j = c // LANES, c % HEAD_DIM
    angle, low = j % half, j < half
    plus = ((table == 0) & (j < ROPE_DIM) & (r == angle)) | ((table == 1) & (j >= half) & (j < ROPE_DIM)
                                                                & (r == half + angle))
    minus = (table == 2) & low & (r == half + angle)
    pick = jnp.where(plus, 1.0, jnp.where(minus, -1.0, 0.0)).astype(BF16)
    out = jnp.dot(pieces, pick, preferred_element_type=F32)
    lane = lax.broadcasted_iota(jnp.int32, (1, LANES), 1) % HEAD_DIM
    return out[:, :LANES] + jnp.where(lane < ROPE_DIM, 0.0, 1.0), out[:, LANES:2 * LANES], out[:, 2 * LANES:]


def _rope(x, cos_t, sin_a, sin_b):
    half = ROPE_DIM // 2
    return x * cos_t + pltpu.roll(x, half, 1) * sin_a + pltpu.roll(x, LANES - half, 1) * sin_b


def _rope_transposed(g, cos_t, sin_a, sin_b):
    half = ROPE_DIM // 2
    return g * cos_t + pltpu.roll(g * sin_a, LANES - half, 1) + pltpu.roll(g * sin_b, half, 1)


def _row_spec(tm, width):
    return pl.BlockSpec((tm, width), lambda i: (i, 0))


def _weight_spec(shape):
    return pl.BlockSpec(shape, lambda *_: (0,) * len(shape), pipeline_mode=pl.Buffered(1))


def _full_spec(shape):
    return pl.BlockSpec(shape, lambda *_: (0,) * len(shape))


def attn_in_proj(x, w_pre, w_in, b_in, angles, tm=1024):
    n = x.shape[0]
    tm = min(tm, n)

    def body(x_ref, wp_ref, w_ref, b_ref, cs_ref, h_ref, q_ref, k_ref, v_ref, z_ref):
        xv = x_ref[...]
        h = (xv * _rms(xv) * wp_ref[...]).astype(BF16)
        h_ref[...] = h
        proj = jnp.dot(h, w_ref[...], preferred_element_type=F32) + b_ref[...]
        tabs = _rope_tables(cs_ref[...])
        for s in range(D_MODEL // LANES):
            sl = slice(s * LANES, (s + 1) * LANES)
            q_ref[:, sl] = _rope(proj[:, sl] * (HEAD_DIM ** -0.5), *tabs).astype(BF16)
        k_ref[...] = _rope(proj[:, D_MODEL:D_MODEL + KV_WIDTH], *tabs).astype(BF16)
        v_ref[...] = proj[:, D_MODEL + KV_WIDTH:D_MODEL + 2 * KV_WIDTH].astype(BF16)
        z_ref[...] = proj[:, D_MODEL + 2 * KV_WIDTH:]

    return pl.pallas_call(
        body, name="attn_in_proj", grid=(n // tm,),
        in_specs=[_row_spec(tm, D_MODEL), _full_spec((1, D_MODEL)), _weight_spec((D_MODEL, ATTN_IN)),
                  _full_spec((1, ATTN_IN)), _row_spec(tm, ANGLE_COLS)],
        out_specs=[_row_spec(tm, D_MODEL), _row_spec(tm, D_MODEL), _row_spec(tm, KV_WIDTH),
                   _row_spec(tm, KV_WIDTH), _row_spec(tm, D_MODEL)],
        out_shape=[jax.ShapeDtypeStruct((n, D_MODEL), BF16), jax.ShapeDtypeStruct((n, D_MODEL), BF16),
                   jax.ShapeDtypeStruct((n, KV_WIDTH), BF16), jax.ShapeDtypeStruct((n, KV_WIDTH), BF16),
                   jax.ShapeDtypeStruct((n, D_MODEL), F32)],
        compiler_params=_cparams(1),
    )(x, w_pre, w_in, b_in, angles)


def _column_blocks(w):
    if len(w.shape) == 2:
        return [slice(0, w.shape[1])], lambda ref, s: ref[...]
    width = w.shape[2]
    return [slice(s * width, (s + 1) * width) for s in range(w.shape[0])], lambda ref, s: ref[s]


def rec_in_proj(x, w_pre, w_in, tm=1024):
    n = x.shape[0]
    tm = min(tm, n)
    columns, block = _column_blocks(w_in)

    def body(x_ref, wp_ref, w_ref, h_ref, p_ref):
        xv = x_ref[...]
        h = (xv * _rms(xv) * wp_ref[...]).astype(BF16)
        h_ref[...] = h
        for s, cols in enumerate(columns):
            p_ref[:, cols] = jnp.dot(h, block(w_ref, s), preferred_element_type=F32)

    return pl.pallas_call(
        body, name="rec_in_proj", grid=(n // tm,),
        in_specs=[_row_spec(tm, D_MODEL), _full_spec((1, D_MODEL)), _weight_spec(w_in.shape)],
        out_specs=[_row_spec(tm, D_MODEL), _row_spec(tm, REC_IN)],
        out_shape=[jax.ShapeDtypeStruct((n, D_MODEL), BF16), jax.ShapeDtypeStruct((n, REC_IN), F32)],
        compiler_params=_cparams(1),
    )(x, w_pre, w_in)


def out_proj(og, w_out, b_out, x_res, w_post, tm=1024):
    n = og.shape[0]
    tm = min(tm, n)

    def body(og_ref, w_ref, b_ref, x_ref, wp_ref, y_ref, xo_ref):
        y = jnp.dot(og_ref[...], w_ref[...], preferred_element_type=F32) + b_ref[...]
        y_ref[...] = y.astype(BF16)
        xo_ref[...] = x_ref[...] + y * _rms(y) * wp_ref[...]

    return pl.pallas_call(
        body, name="out_proj", grid=(n // tm,),
        in_specs=[_row_spec(tm, D_MODEL), _weight_spec((D_MODEL, D_MODEL)), _full_spec((1, D_MODEL)),
                  _row_spec(tm, D_MODEL), _full_spec((1, D_MODEL))],
        out_specs=[_row_spec(tm, D_MODEL), _row_spec(tm, D_MODEL)],
        out_shape=[jax.ShapeDtypeStruct((n, D_MODEL), BF16), jax.ShapeDtypeStruct((n, D_MODEL), F32)],
        compiler_params=_cparams(1),
    )(og, w_out, b_out, x_res, w_post)


def _post_norm_bwd(g, y, w_post):
    rstd = _rms(y)
    yn = y * rstd
    gw = g * w_post
    return rstd * (gw - yn * jnp.mean(gw * yn, axis=-1, keepdims=True)), jnp.sum(g * yn, axis=0, keepdims=True)


def out_proj_loss_bwd(og, w_out, x_res, w_post, target, tm=1024):
    n = og.shape[0]
    tm = min(tm, n)

    def body(og_ref, w_ref, x_ref, wp_ref, t_ref, dx_ref, l_ref, dog_ref, dw_ref, dwp_ref):
        @pl.when(pl.program_id(0) == 0)
        def _():
            l_ref[...] = jnp.zeros_like(l_ref)
            dw_ref[...] = jnp.zeros_like(dw_ref)
            dwp_ref[...] = jnp.zeros_like(dwp_ref)

        og_tile = og_ref[...]
        y = jnp.dot(og_tile, w_ref[...], preferred_element_type=F32)
        err = x_ref[...] + y * _rms(y) * wp_ref[...] - t_ref[...]
        g = err * (1.0 / D_MODEL)
        dx_ref[...] = g
        l_ref[...] += jnp.sum(err * err, axis=0, keepdims=True)
        dy, dwp = _post_norm_bwd(g, y, wp_ref[...])
        dwp_ref[...] += dwp
        dyb = dy.astype(BF16)
        dog_ref[...] = _dot(dyb, w_ref[...], _NT).astype(BF16)
        dw_ref[...] += _dot(og_tile, dyb, _TN)

    return pl.pallas_call(
        body, name="out_proj_loss_bwd", grid=(n // tm,),
        in_specs=[_row_spec(tm, D_MODEL), _weight_spec((D_MODEL, D_MODEL)), _row_spec(tm, D_MODEL),
                  _full_spec((1, D_MODEL)), _row_spec(tm, D_MODEL)],
        out_specs=[_row_spec(tm, D_MODEL), _full_spec((1, D_MODEL)), _row_spec(tm, D_MODEL),
                   _full_spec((D_MODEL, D_MODEL)), _full_spec((1, D_MODEL))],
        out_shape=[jax.ShapeDtypeStruct((n, D_MODEL), F32), jax.ShapeDtypeStruct((1, D_MODEL), F32),
                   jax.ShapeDtypeStruct((n, D_MODEL), BF16), jax.ShapeDtypeStruct((D_MODEL, D_MODEL), F32),
                   jax.ShapeDtypeStruct((1, D_MODEL), F32)],
        compiler_params=_cparams(1),
    )(og, w_out, x_res, w_post, target)


def out_proj_bwd(dxo, y, og, w_out, w_post, tm=1024):
    n = og.shape[0]
    tm = min(tm, n)

    def body(g_ref, y_ref, og_ref, w_ref, wp_ref, dog_ref, dw_ref, db_ref, dwp_ref):
        @pl.when(pl.program_id(0) == 0)
        def _():
            dw_ref[...] = jnp.zeros_like(dw_ref)
            db_ref[...] = jnp.zeros_like(db_ref)
            dwp_ref[...] = jnp.zeros_like(dwp_ref)

        dy, dwp = _post_norm_bwd(g_ref[...], y_ref[...].astype(F32), wp_ref[...])
        dwp_ref[...] += dwp
        db_ref[...] += jnp.sum(dy, axis=0, keepdims=True)
        dyb = dy.astype(BF16)
        dog_ref[...] = _dot(dyb, w_ref[...], _NT).astype(BF16)
        dw_ref[...] += _dot(og_ref[...], dyb, _TN)

    return pl.pallas_call(
        body, name="out_proj_bwd", grid=(n // tm,),
        in_specs=[_row_spec(tm, D_MODEL), _row_spec(tm, D_MODEL), _row_spec(tm, D_MODEL),
                  _weight_spec((D_MODEL, D_MODEL)), _full_spec((1, D_MODEL))],
        out_specs=[_row_spec(tm, D_MODEL), _full_spec((D_MODEL, D_MODEL)), _full_spec((1, D_MODEL)),
                   _full_spec((1, D_MODEL))],
        out_shape=[jax.ShapeDtypeStruct((n, D_MODEL), BF16), jax.ShapeDtypeStruct((D_MODEL, D_MODEL), F32),
                   jax.ShapeDtypeStruct((1, D_MODEL), F32), jax.ShapeDtypeStruct((1, D_MODEL), F32)],
        compiler_params=_cparams(1),
    )(dxo, y, og, w_out, w_post)


def in_proj_bwd_x(dproj, w_in, x, w_pre, dxo, tm=1024, scatter=()):
    n, p = dproj.shape
    tm = min(tm, n)
    steps = n // tm
    ns = len(scatter)
    columns, block = _column_blocks(w_in)

    def body(*refs):
        dp_ref, w_ref, x_ref, wp_ref, g_ref = refs[:5]
        dx_ref, dwp_ref = refs[5 + ns:7 + ns]
        exchange = (refs[5:5 + ns], refs[7 + ns:7 + 2 * ns]) + tuple(refs[7 + 2 * ns:])

        @pl.when(pl.program_id(0) == 0)
        def _():
            dwp_ref[...] = jnp.zeros_like(dwp_ref)
            if ns:
                _scatter_start(*exchange)

        dh = functools.reduce(jnp.add, [_dot(dp_ref[:, cols], block(w_ref, s), _NT)
                                        for s, cols in enumerate(columns)])
        xv = x_ref[...]
        rstd = _rms(xv)
        xn = xv * rstd
        gw = dh * wp_ref[...]
        dwp_ref[...] += jnp.sum(dh * xn, axis=0, keepdims=True)
        dx_ref[...] = rstd * (gw - xn * jnp.mean(gw * xn, axis=-1, keepdims=True)) + g_ref[...]

        if ns:
            @pl.when(pl.program_id(0) == steps - 1)
            def _():
                _scatter_finish(*exchange)

    out = pl.pallas_call(
        body, name=f"in_proj_bwd_x_{p}", grid=(steps,),
        in_specs=[_row_spec(tm, p), _weight_spec(w_in.shape), _row_spec(tm, D_MODEL), _full_spec((1, D_MODEL)),
                  _row_spec(tm, D_MODEL)] + [_ANY] * ns,
        out_specs=[_row_spec(tm, D_MODEL), _full_spec((1, D_MODEL))] + [_ANY] * ns,
        out_shape=[jax.ShapeDtypeStruct((n, D_MODEL), F32), jax.ShapeDtypeStruct((1, D_MODEL), F32)]
        + [jax.ShapeDtypeStruct(a.shape, a.dtype) for a in scatter],
        scratch_shapes=_scatter_sems(ns) if ns else [],
        compiler_params=_cparams(1),
    )(dproj, w_in, x, w_pre, dxo, *scatter)
    return out[0], out[1], out[2:]


def in_proj_bwd_w(h, dproj, tm=1024, as_shards=False):
    n, p = dproj.shape
    chunk = p // (4 if p % 4096 == 0 else 3)
    tm = min(tm, n)
    steps = n // tm
    shard = p // N_CHIPS

    def body(h_ref, dp_ref, dw_ref, db_ref, acc_scr, sem, *staging):
        i = pl.program_id(0)

        @pl.when(i == 0)
        def _():
            acc_scr[...] = jnp.zeros_like(acc_scr)
            db_ref[...] = jnp.zeros_like(db_ref)

        ht = h_ref[...].T
        for c0 in range(0, p, chunk):
            dp = dp_ref[:, c0:c0 + chunk]
            acc_scr[:, c0:c0 + chunk] += jnp.dot(ht, dp, preferred_element_type=F32)
            db_ref[:, c0:c0 + chunk] += jnp.sum(dp.astype(F32), axis=0, keepdims=True)

        @pl.when(i == steps - 1)
        def _():
            if as_shards:
                for s in range(N_CHIPS):
                    staging[0][...] = acc_scr[:, s * shard:(s + 1) * shard].astype(BF16)
                    out = pltpu.make_async_copy(staging[0], dw_ref.at[s], sem)
                    out.start()
                    out.wait()
            else:
                out = pltpu.make_async_copy(acc_scr, dw_ref, sem)
                out.start()
                out.wait()

    dw_shape = jax.ShapeDtypeStruct((N_CHIPS, D_MODEL, shard), BF16) if as_shards else (
        jax.ShapeDtypeStruct((D_MODEL, p), F32))
    return pl.pallas_call(
        body, name=f"in_proj_bwd_w_{p}", grid=(steps,),
        in_specs=[_row_spec(tm, D_MODEL), _row_spec(tm, p)],
        out_specs=[_ANY, _full_spec((1, p))],
        out_shape=[dw_shape, jax.ShapeDtypeStruct((1, p), F32)],
        scratch_shapes=[pltpu.VMEM((D_MODEL, p), F32), pltpu.SemaphoreType.DMA]
        + ([pltpu.VMEM((D_MODEL, shard), BF16)] if as_shards else []),
        compiler_params=_cparams(1),
    )(h, dproj)


PAIRS = GROUP // 2
GROUP_ROWS = PAIRS * ATTN_BLOCK
MASKED = -1e30


def _kv_windows(k_ref, v_ref, i):
    ps = pl.multiple_of(jnp.maximum(i - 1, 0) * ATTN_BLOCK, ATTN_BLOCK)
    cs = pl.multiple_of(i * ATTN_BLOCK, ATTN_BLOCK)
    kw = jnp.concatenate([k_ref[pl.ds(ps, ATTN_BLOCK), :], k_ref[pl.ds(cs, ATTN_BLOCK), :]], axis=0)
    vw = jnp.concatenate([v_ref[pl.ds(ps, ATTN_BLOCK), :], v_ref[pl.ds(cs, ATTN_BLOCK), :]], axis=0)
    return kw.astype(F32).T, vw.astype(F32).T, ps, cs


def _low_rows(shape):
    return lax.broadcasted_iota(jnp.int32, shape, 0) < HEAD_DIM


def _spread(w, kvh):
    low = _low_rows(w.shape)
    swapped = pltpu.roll(w, HEAD_DIM, 0)
    if kvh == 0:
        return jnp.where(low, w, 0.0), jnp.where(low, 0.0, swapped)
    return jnp.where(low, swapped, 0.0), jnp.where(low, 0.0, w)


def _unspread(d_a, d_b, kvh):
    low = _low_rows(d_a.shape)
    if kvh == 0:
        return jnp.where(low, d_a + pltpu.roll(d_b, HEAD_DIM, 0), 0.0)
    return jnp.where(low, 0.0, pltpu.roll(d_a, HEAD_DIM, 0) + d_b)


def _stack_pairs(ref, kvh):
    return jnp.concatenate([ref[:, (kvh * PAIRS + j) * LANES:(kvh * PAIRS + j + 1) * LANES] for j in range(PAIRS)],
                           axis=0)


def _fill_bias(bias_scr):
    shape = (GROUP_ROWS, 2 * ATTN_BLOCK)
    r = lax.broadcasted_iota(jnp.int32, shape, 0) % ATTN_BLOCK
    c = lax.broadcasted_iota(jnp.int32, shape, 1)
    in_cur = (c >= ATTN_BLOCK) & ((c - ATTN_BLOCK) <= r)
    in_prev = (c < ATTN_BLOCK) & (c > r)
    bias_scr[0] = jnp.where(in_cur, 0.0, MASKED)
    bias_scr[1] = jnp.where(in_cur | in_prev, 0.0, MASKED)
    bias_scr[2] = jnp.where(c == r, 1.0, 0.0)


N_BIAS_TABLES = 3


def _sink_table(sinks):
    t = jnp.transpose(sinks.reshape(N_KV_HEADS, PAIRS, 2), (0, 2, 1))
    return jnp.broadcast_to(t[:, :, :, None, None], (N_KV_HEADS, 2, PAIRS, ATTN_BLOCK, LANES)).reshape(
        N_KV_HEADS, 2, GROUP_ROWS, LANES)


def attn_fwd(q, k, v, z, sink_tab, batch, seq, gather=()):
    nb = seq // ATTN_BLOCK
    ng = len(gather)

    def body(*refs):
        q_ref, k_ref, v_ref, z_ref, s_ref = refs[:5]
        og_ref, bias_scr = refs[5 + ng], refs[6 + 2 * ng]
        exchange = (refs[5:5 + ng], refs[6 + ng:6 + 2 * ng]) + tuple(refs[7 + 2 * ng:])
        b, i = pl.program_id(0), pl.program_id(1)

        @pl.when((b == 0) & (i == 0))
        def _():
            _fill_bias(bias_scr)
            if ng:
                _gather_start(*exchange)

        kw, vw, _, _ = _kv_windows(k_ref, v_ref, i)
        bias, at_sink = bias_scr[jnp.minimum(i, 1)], bias_scr[2] > 0.5
        for kvh in range(N_KV_HEADS):
            k_a, k_b = _spread(kw, kvh)
            v_a, v_b = _spread(vw, kvh)
            og = _attn_group(_stack_pairs(q_ref, kvh), k_a, v_a, k_b, v_b, _stack_pairs(z_ref, kvh),
                             s_ref[kvh, 0], s_ref[kvh, 1], bias, at_sink)
            for j in range(PAIRS):
                og_ref[:, (kvh * PAIRS + j) * LANES:(kvh * PAIRS + j + 1) * LANES] = (
                    og[j * ATTN_BLOCK:(j + 1) * ATTN_BLOCK].astype(BF16))

        if ng:
            @pl.when((b == batch - 1) & (i == nb - 1))
            def _():
                _gather_finish(*exchange)

    blk = lambda w: pl.BlockSpec((ATTN_BLOCK, w), lambda b, i: (b * nb + i, 0))
    seq_spec = pl.BlockSpec((seq, KV_WIDTH), lambda b, i: (b, 0))
    out = pl.pallas_call(
        body, name="attn_fwd", grid=(batch, nb),
        in_specs=[blk(D_MODEL), seq_spec, seq_spec, blk(D_MODEL), _full_spec(sink_tab.shape)] + [_ANY] * ng,
        out_specs=[blk(D_MODEL)] + [_ANY] * ng,
        out_shape=[jax.ShapeDtypeStruct((batch * seq, D_MODEL), BF16)]
        + [jax.ShapeDtypeStruct((N_CHIPS,) + a.shape, a.dtype) for a in gather],
        scratch_shapes=[pltpu.VMEM((N_BIAS_TABLES, GROUP_ROWS, 2 * ATTN_BLOCK), F32)] + (_gather_sems(ng) if ng else []),
        compiler_params=_cparams(2),
    )(q, k, v, z, sink_tab, *gather)
    return out[0], out[1:]


def attn_bwd(q, k, v, z, sink_tab, dog, angles, batch, seq, scatter=()):
    nb = seq // ATTN_BLOCK
    ns = len(scatter)

    def body(*refs):
        q_ref, k_ref, v_ref, z_ref, s_ref, g_ref, cs_ref = refs[:7]
        dp_ref, dk_ref, dv_ref, ds_ref = refs[7 + ns:11 + ns]
        bias_scr = refs[11 + 2 * ns]
        exchange = (refs[7:7 + ns], refs[11 + ns:11 + 2 * ns]) + tuple(refs[12 + 2 * ns:])
        b, i = pl.program_id(0), pl.program_id(1)

        @pl.when((b == 0) & (i == 0))
        def _():
            _fill_bias(bias_scr)
            ds_ref[...] = jnp.zeros_like(ds_ref)
            if ns:
                _scatter_start(*exchange)

        @pl.when(i == 0)
        def _():
            dk_ref[...] = jnp.zeros_like(dk_ref)
            dv_ref[...] = jnp.zeros_like(dv_ref)

        kw, vw, ps, cs = _kv_windows(k_ref, v_ref, i)
        bias = bias_scr[jnp.minimum(i, 1)]
        tabs = _rope_tables(cs_ref[...])
        dkw = jnp.zeros_like(kw)
        dvw = jnp.zeros_like(vw)
        for kvh in range(N_KV_HEADS):
            k_a, k_b = _spread(kw, kvh)
            v_a, v_b = _spread(vw, kvh)
            _, vjp = jax.vjp(functools.partial(_attn_group, bias=bias), _stack_pairs(q_ref, kvh).astype(F32),
                             k_a, v_a, k_b, v_b, _stack_pairs(z_ref, kvh), s_ref[kvh, 0], s_ref[kvh, 1])
            dqs, dk_a, dv_a, dk_b, dv_b, dzs, ds_a, ds_b = vjp(_stack_pairs(g_ref, kvh).astype(F32))
            dkw = dkw + _unspread(dk_a, dk_b, kvh)
            dvw = dvw + _unspread(dv_a, dv_b, kvh)
            ds_ref[kvh, 0] += jnp.sum(ds_a.reshape(PAIRS, ATTN_BLOCK, LANES), axis=1)
            ds_ref[kvh, 1] += jnp.sum(ds_b.reshape(PAIRS, ATTN_BLOCK, LANES), axis=1)
            for j in range(PAIRS):
                rows = slice(j * ATTN_BLOCK, (j + 1) * ATTN_BLOCK)
                col = (kvh * PAIRS + j) * LANES
                dp_ref[:, col:col + LANES] = _rope_transposed(dqs[rows] * (HEAD_DIM ** -0.5), *tabs).astype(BF16)
                zc = D_MODEL + 2 * KV_WIDTH + col
                dp_ref[:, zc:zc + LANES] = dzs[rows].astype(BF16)
        dp_ref[:, D_MODEL:D_MODEL + 2 * KV_WIDTH] = jnp.zeros((ATTN_BLOCK, 2 * KV_WIDTH), BF16)
        dk_ref[:, pl.ds(ps, ATTN_BLOCK)] += dkw[:, :ATTN_BLOCK]
        dk_ref[:, pl.ds(cs, ATTN_BLOCK)] += dkw[:, ATTN_BLOCK:]
        dv_ref[:, pl.ds(ps, ATTN_BLOCK)] += dvw[:, :ATTN_BLOCK]
        dv_ref[:, pl.ds(cs, ATTN_BLOCK)] += dvw[:, ATTN_BLOCK:]

        if ns:
            @pl.when((b == batch - 1) & (i == nb - 1))
            def _():
                _scatter_finish(*exchange)

    blk = lambda w: pl.BlockSpec((ATTN_BLOCK, w), lambda b, i: (b * nb + i, 0))
    seq_spec = pl.BlockSpec((seq, KV_WIDTH), lambda b, i: (b, 0))
    seq_spec_t = pl.BlockSpec((KV_WIDTH, seq), lambda b, i: (0, b))
    n = batch * seq
    ds_shape = (N_KV_HEADS, 2, PAIRS, LANES)
    out = pl.pallas_call(
        body, name="attn_bwd", grid=(batch, nb),
        in_specs=[blk(D_MODEL), seq_spec, seq_spec, blk(D_MODEL), _full_spec(sink_tab.shape), blk(D_MODEL)]
        + [blk(ANGLE_COLS)] + [_ANY] * ns,
        out_specs=[blk(ATTN_IN), seq_spec_t, seq_spec_t, _full_spec(ds_shape)] + [_ANY] * ns,
        out_shape=[jax.ShapeDtypeStruct((n, ATTN_IN), BF16), jax.ShapeDtypeStruct((KV_WIDTH, n), F32),
                   jax.ShapeDtypeStruct((KV_WIDTH, n), F32), jax.ShapeDtypeStruct(ds_shape, F32)]
        + [jax.ShapeDtypeStruct(a.shape, a.dtype) for a in scatter],
        scratch_shapes=[pltpu.VMEM((N_BIAS_TABLES, GROUP_ROWS, 2 * ATTN_BLOCK), F32)] + (_scatter_sems(ns) if ns else []),
        compiler_params=_cparams(2),
    )(q, k, v, z, sink_tab, dog, angles, *scatter)
    return out[0], out[1], out[2], out[3], out[4:]


def attn_bwd_kv(dproj, dk_t, dv_t, angles, tm=512):
    n = dproj.shape[0]

    def body(dp_in_ref, dk_ref, dv_ref, cs_ref, dp_ref):
        del dp_in_ref
        dp_ref[:, :KV_WIDTH] = _rope_transposed(dk_ref[...].T, *_rope_tables(cs_ref[...])).astype(BF16)
        dp_ref[:, KV_WIDTH:] = dv_ref[...].T.astype(BF16)

    kv_cols = pl.BlockSpec((tm, 2 * KV_WIDTH), lambda i: (i, D_MODEL // (2 * KV_WIDTH)))
    col_spec = pl.BlockSpec((KV_WIDTH, tm), lambda i: (0, i))
    return pl.pallas_call(
        body, name="attn_bwd_kv", grid=(n // tm,),
        in_specs=[kv_cols, col_spec, col_spec, _row_spec(tm, ANGLE_COLS)],
        out_specs=kv_cols, out_shape=jax.ShapeDtypeStruct(dproj.shape, BF16),
        input_output_aliases={0: 0}, compiler_params=_cparams(1),
    )(dproj, dk_t, dv_t, angles)


def rec_fwd(proj, lb_logits, gnorm_w, batch, seq):
    nblk = seq // REC_BLOCK

    def body(p_ref, lb_ref, gw_ref, og_ref, st_ref, safe_ref, s_scr):
        @pl.when(pl.program_id(1) == 0)
        def _():
            s_scr[...] = jnp.zeros_like(s_scr)

        S = s_scr[...]
        st_ref[0] = S
        qr, fr, v, z = (p_ref[:, part * D_MODEL:(part + 1) * D_MODEL] for part in range(4))
        lf, k = forget_gate(fr, lb_ref[1:2, :] - lb_ref[0:1, :])
        q, b = silu(qr), cumsum_rows(lf)
        safe = jnp.min(_rec_margin(b)) >= -SAFE_RANGE

        gate = gw_ref[...] * silu(z)
        safe_ref[0] = jnp.full((REC_HEADS, LANES), safe.astype(F32))

        def store(o, S_new):
            og_ref[...] = (o * lax.rsqrt(head_sum(o * o) * (1.0 / REC_DIM) + NORM_EPS) * gate).astype(BF16)
            s_scr[...] = S_new

        @pl.when(safe)
        def _():
            store(*_rec_cores_fast(q, k, v, b, S))

        @pl.when(jnp.logical_not(safe))
        def _():
            outs = [_rec_core_slow(*args) for args in zip(*(_heads(t) for t in (q, k, v, b, S)))]
            store(*(jnp.concatenate(parts, axis=1) for parts in zip(*outs)))

    blk = lambda w: pl.BlockSpec((REC_BLOCK, w), lambda b, j: (b * nblk + j, 0))
    st_spec = pl.BlockSpec((1, REC_DIM, D_MODEL), lambda b, j: (b * nblk + j, 0, 0))
    safe_spec = pl.BlockSpec((1, REC_HEADS, LANES), lambda b, j: (b * nblk + j, 0, 0))
    return pl.pallas_call(
        body, name="rec_fwd", grid=(batch, nblk),
        in_specs=[blk(REC_IN), _full_spec((2, D_MODEL)), _full_spec((1, D_MODEL))],
        out_specs=[blk(D_MODEL), st_spec, safe_spec],
        out_shape=[jax.ShapeDtypeStruct((batch * seq, D_MODEL), BF16),
                   jax.ShapeDtypeStruct((batch * nblk, REC_DIM, D_MODEL), F32),
                   jax.ShapeDtypeStruct((batch * nblk, REC_HEADS, LANES), F32)],
        scratch_shapes=[pltpu.VMEM((REC_DIM, D_MODEL), F32)],
        compiler_params=_cparams(2),
    )(proj, lb_logits, jnp.tile(gnorm_w, (1, REC_HEADS)))


def rec_bwd(proj, states, safe, lb_logits, gnorm_w, dog, batch, seq):
    nblk = seq // REC_BLOCK

    def body(p_ref, st_ref, safe_ref, lb_ref, gw_ref, g_ref, dp_ref, dlb_ref, dgw_ref, ds_scr):
        @pl.when((pl.program_id(0) == 0) & (pl.program_id(1) == 0))
        def _():
            dlb_ref[...] = jnp.zeros_like(dlb_ref)
            dgw_ref[...] = jnp.zeros_like(dgw_ref)

        @pl.when(pl.program_id(1) == 0)
        def _():
            ds_scr[...] = jnp.zeros_like(ds_scr)

        def load():
            primals = tuple(p_ref[:, part * D_MODEL:(part + 1) * D_MODEL] for part in range(4)) + (
                st_ref[0], lb_ref[0:1, :], lb_ref[1:2, :], gw_ref[...])
            return primals, (g_ref[...].astype(F32), ds_scr[...])

        def store(dqr, dfr, dv, dz, dS, dl0, dl1, dgw):
            for part, val in enumerate((dqr, dfr, dv, dz)):
                dp_ref[:, part * D_MODEL:(part + 1) * D_MODEL] = val.astype(BF16)
            ds_scr[...] = dS
            dlb_ref[0:1, :] += dl0
            dlb_ref[1:2, :] += dl1
            dgw_ref[...] += functools.reduce(jnp.add, _heads(dgw))

        fast = jnp.max(safe_ref[0]) > 0.5

        @pl.when(fast)
        def _():
            primals, cotangents = load()
            store(*jax.vjp(_rec_block_fast, *primals)[1](cotangents))

        @pl.when(jnp.logical_not(fast))
        def _():
            primals, cotangents = load()
            outs = [jax.vjp(functools.partial(_rec_head, _rec_core_slow), *args)[1](cts)
                    for args, cts in zip(zip(*(_heads(t) for t in primals)), zip(*(_heads(t) for t in cotangents)))]
            store(*(jnp.concatenate(parts, axis=1) for parts in zip(*outs)))

    blk = lambda w: pl.BlockSpec((REC_BLOCK, w), lambda b, j: (b * nblk + nblk - 1 - j, 0))
    st_spec = pl.BlockSpec((1, REC_DIM, D_MODEL), lambda b, j: (b * nblk + nblk - 1 - j, 0, 0))
    safe_spec = pl.BlockSpec((1, REC_HEADS, LANES), lambda b, j: (b * nblk + nblk - 1 - j, 0, 0))
    return pl.pallas_call(
        body, name="rec_bwd", grid=(batch, nblk),
        in_specs=[blk(REC_IN), st_spec, safe_spec, _full_spec((2, D_MODEL)), _full_spec((1, D_MODEL)),
                  blk(D_MODEL)],
        out_specs=[blk(REC_IN), _full_spec((2, D_MODEL)), _full_spec((1, REC_DIM))],
        out_shape=[jax.ShapeDtypeStruct((batch * seq, REC_IN), BF16), jax.ShapeDtypeStruct((2, D_MODEL), F32),
                   jax.ShapeDtypeStruct((1, REC_DIM), F32)],
        scratch_shapes=[pltpu.VMEM((REC_DIM, D_MODEL), F32)],
        compiler_params=_cparams(2),
    )(proj, states, safe, lb_logits, jnp.tile(gnorm_w, (1, REC_HEADS)), dog)


_ANY = pl.BlockSpec(memory_space=pl.ANY)


def _chip_peers():
    x, y, c = lax.axis_index("x"), lax.axis_index("y"), lax.axis_index("c")
    peers = []
    for fx, fy in ((1, 0), (0, 1), (1, 1)):
        px, py = (1 - x if fx else x), (1 - y if fy else y)
        peers.append(((px, py, c), 2 * px + py))
    return 2 * x + y, peers


def _remote(src, dst, send_sem, recv_sem, device):
    return pltpu.make_async_remote_copy(src_ref=src, dst_ref=dst, send_sem=send_sem, recv_sem=recv_sem,
                                        device_id=device, device_id_type=MESH)


N_FLIPS = N_CHIPS - 1


def _scatter_sems(n):
    return [pltpu.SemaphoreType.DMA((n * N_FLIPS,)), pltpu.SemaphoreType.DMA((n * N_FLIPS,)),
            pltpu.SemaphoreType.DMA((n,))]


def _scatter_copies(ins, outs, send_sems, recv_sems, local_sems, starting):
    me, peers = _chip_peers()
    local = [pltpu.make_async_copy(ins[k].at[me], outs[k].at[me], local_sems.at[k]) for k in range(len(ins))]
    sends, arrivals = [], []
    for k in range(len(ins)):
        for j, (device, idx) in enumerate(peers):
            sems = (send_sems.at[k * N_FLIPS + j], recv_sems.at[k * N_FLIPS + j], device)
            sends.append(_remote(ins[k].at[idx], outs[k].at[me], *sems))
            if not starting:
                arrivals.append(_remote(ins[k].at[me], outs[k].at[idx], *sems))
    return local, sends, arrivals


def _scatter_start(*refs):
    local, sends, _ = _scatter_copies(*refs, starting=True)
    for cp in local + sends:
        cp.start()


def _scatter_finish(*refs):
    local, sends, arrivals = _scatter_copies(*refs, starting=False)
    for cp in arrivals:
        cp.wait_recv()
    for cp in sends:
        cp.wait_send()
    for cp in local:
        cp.wait()


def _gather_sems(n):
    return [pltpu.SemaphoreType.DMA((n * N_FLIPS,)) for _ in range(4)] + [pltpu.SemaphoreType.DMA((n,))]


def _gather_copies(ins, outs, send_sems, recv_sems, pass_send_sems, pass_recv_sems, local_sems, starting):
    me, peers = _chip_peers()
    c = lax.axis_index("c")
    sibling = (lax.axis_index("x"), lax.axis_index("y"), 1 - c)
    local = [pltpu.make_async_copy(ins[k], outs[k].at[me], local_sems.at[k]) for k in range(len(ins))]
    sends, arrivals, passes, pass_arrivals = [], [], [], []
    for k in range(len(ins)):
        half = ins[k].shape[0] // 2
        mine, other = pl.ds(c * half, half), pl.ds((1 - c) * half, half)
        for j, (device, idx) in enumerate(peers):
            s = k * N_FLIPS + j
            sends.append(_remote(ins[k].at[mine], outs[k].at[me].at[mine], send_sems.at[s], recv_sems.at[s], device))
            if starting:
                continue
            arrived = outs[k].at[idx].at[mine]
            arrivals.append(_remote(ins[k].at[mine], arrived, send_sems.at[s], recv_sems.at[s], device))
            passes.append(_remote(arrived, arrived, pass_send_sems.at[s], pass_recv_sems.at[s], sibling))
            passed = outs[k].at[idx].at[other]
            pass_arrivals.append(_remote(passed, passed, pass_send_sems.at[s], pass_recv_sems.at[s], sibling))
    return local, sends, arrivals, passes, pass_arrivals


def _gather_start(*refs):
    local, sends, _, _, _ = _gather_copies(*refs, starting=True)
    for cp in local + sends:
        cp.start()


def _gather_finish(*refs):
    local, sends, arrivals, passes, pass_arrivals = _gather_copies(*refs, starting=False)
    for arrival, onward in zip(arrivals, passes):
        arrival.wait_recv()
        onward.start()
    for cp in pass_arrivals:
        cp.wait_recv()
    for cp in sends + passes:
        cp.wait_send()
    for cp in local:
        cp.wait()


def chip_gather(arrays):
    n = len(arrays)

    def body(*refs):
        _gather_start(refs[:n], refs[n:2 * n], *refs[2 * n:])
        _gather_finish(refs[:n], refs[n:2 * n], *refs[2 * n:])

    return pl.pallas_call(
        body, name="chip_gather", in_specs=[_ANY] * n, out_specs=[_ANY] * n,
        out_shape=[jax.ShapeDtypeStruct((N_CHIPS,) + a.shape, a.dtype) for a in arrays],
        scratch_shapes=_gather_sems(n),
    )(*arrays)


def sibling_exchange(arrays):
    n = len(arrays)

    def body(*refs):
        ins, outs = refs[:n], refs[n:2 * n]
        send_sems, recv_sems = refs[2 * n:]
        sibling = (lax.axis_index("x"), lax.axis_index("y"), 1 - lax.axis_index("c"))
        copies = [pltpu.make_async_remote_copy(src_ref=ins[k], dst_ref=outs[k], send_sem=send_sems.at[k],
                                               recv_sem=recv_sems.at[k], device_id=sibling, device_id_type=MESH)
                  for k in range(n)]
        for cp in copies:
            cp.start()
        for cp in copies:
            cp.wait()

    return pl.pallas_call(
        body, name="sibling_exchange", in_specs=[_ANY] * n, out_specs=[_ANY] * n,
        out_shape=[jax.ShapeDtypeStruct(a.shape, a.dtype) for a in arrays],
        scratch_shapes=[pltpu.SemaphoreType.DMA((n,)), pltpu.SemaphoreType.DMA((n,))],
    )(*arrays)


def all_gather_small(vec):
    def body(v_ref, out_ref, send_sems, recv_sems, local_sem):
        x, y, c = lax.axis_index("x"), lax.axis_index("y"), lax.axis_index("c")
        me = 4 * x + 2 * y + c
        local = pltpu.make_async_copy(v_ref, out_ref.at[me], local_sem)
        local.start()
        sends, recvs = [], []
        for j in range(1, N_DEV):
            px = jnp.where(j & 4, 1 - x, x)
            py = jnp.where(j & 2, 1 - y, y)
            pc = jnp.where(j & 1, 1 - c, c)
            common = dict(send_sem=send_sems.at[j - 1], recv_sem=recv_sems.at[j - 1], device_id=(px, py, pc),
                          device_id_type=MESH)
            sends.append(pltpu.make_async_remote_copy(src_ref=v_ref, dst_ref=out_ref.at[me], **common))
            recvs.append(pltpu.make_async_remote_copy(src_ref=v_ref, dst_ref=out_ref.at[4 * px + 2 * py + pc],
                                                      **common))
        for cp in sends:
            cp.start()
        for cp in recvs:
            cp.wait_recv()
        for cp in sends:
            cp.wait_send()
        local.wait()

    return pl.pallas_call(
        body, name="all_gather_small", in_specs=[_ANY], out_specs=_ANY,
        out_shape=jax.ShapeDtypeStruct((N_DEV,) + vec.shape, vec.dtype),
        scratch_shapes=[pltpu.SemaphoreType.DMA((N_DEV - 1,)), pltpu.SemaphoreType.DMA((N_DEV - 1,)),
                        pltpu.SemaphoreType.DMA],
    )(vec)


def sum_slots(stacked, tm=256):
    s, r, c = stacked.shape
    tm = min(tm, r)

    def body(in_ref, out_ref):
        acc = in_ref[0].astype(F32)
        for t in range(1, s):
            acc = acc + in_ref[t].astype(F32)
        out_ref[...] = acc

    return pl.pallas_call(
        body, name=f"sum_slots_{s}_{r}_{c}", grid=(r // tm,),
        in_specs=[pl.BlockSpec((s, tm, c), lambda i: (0, i, 0))], out_specs=_row_spec(tm, c),
        out_shape=jax.ShapeDtypeStruct((r, c), F32), compiler_params=_cparams(1),
    )(stacked)


def adamw(w, m, v, g_a, g_b=None, tm=256):
    r, c = w.shape
    tm = min(tm, r)
    two = g_b is not None

    def body(*refs):
        w_ref, m_ref, v_ref, ga_ref = refs[:4]
        g_ref, d_ref, nm_ref, nv_ref = refs[-4:]
        g = ga_ref[...] + refs[4][...] if two else ga_ref[...]
        nm = ADAM_B1 * m_ref[...] + (1.0 - ADAM_B1) * g
        nv = ADAM_B2 * v_ref[...] + (1.0 - ADAM_B2) * (g * g)
        m_hat = nm / (1.0 - ADAM_B1 ** ADAM_STEP)
        v_hat = nv / (1.0 - ADAM_B2 ** ADAM_STEP)
        g_ref[...] = g
        d_ref[...] = -ADAM_LR * (m_hat / (jnp.sqrt(v_hat) + ADAM_EPS) + ADAM_WD * w_ref[...])
        nm_ref[...] = nm
        nv_ref[...] = nv

    args = [w, m, v, g_a] + ([g_b] if two else [])
    return pl.pallas_call(
        body, name=f"adamw_{r}_{c}", grid=(r // tm,),
        in_specs=[_row_spec(tm, c)] * len(args), out_specs=[_row_spec(tm, c)] * 4,
        out_shape=[jax.ShapeDtypeStruct((r, c), F32)] * 4, compiler_params=_cparams(1),
    )(*args)


_SMALL = (("pre_norm_w", (2, D_MODEL)), ("post_norm_w", (2, D_MODEL)), ("attn_b_in", (1, ATTN_IN)),
          ("attn_sinks", (1, N_HEADS)), ("attn_b_out", (1, D_MODEL)), ("rec_lb_logits", (2, D_MODEL)),
          ("rec_gnorm_w", (1, REC_DIM)))
_SMALL_ROWS = 16


def _pack_small(parts, last_row=None):
    rows = []
    for (name, shape) in _SMALL:
        flat = parts[name].reshape(-1)
        pad = -flat.shape[0] % D_MODEL
        rows.append(jnp.pad(flat, (0, pad)).reshape(-1, D_MODEL))
    used = sum(r.shape[0] for r in rows)
    rows.append(jnp.zeros((_SMALL_ROWS - 1 - used, D_MODEL), F32))
    rows.append(jnp.zeros((1, D_MODEL), F32) if last_row is None else last_row)
    return jnp.concatenate(rows, axis=0)


def _unpack_small(packed):
    out, row = {}, 0
    for (name, shape) in _SMALL:
        size = shape[0] * shape[1]
        nrows = -(-size // D_MODEL)
        out[name] = packed[row:row + nrows].reshape(-1)[:size].reshape(shape)
        row += nrows
    return out


_CARRIED = ("rec_w_in", "rec_w_out", "attn_w_out")


_LATE = ("attn_w_out", "rec_w_in", "rec_w_out")


def local_step(x, positions, pre_norm_w, post_norm_w, attn_w_in, attn_b_in, attn_sinks, attn_w_out, attn_b_out,
               rec_w_in, rec_lb_logits, rec_gnorm_w, rec_w_out, loss_target, distributed=False):
    batch, seq, _ = x.shape
    n = batch * seq
    x0 = x.reshape(n, D_MODEL)
    angles = _rope_angles(positions)
    pre0, pre1 = pre_norm_w[0:1], pre_norm_w[1:2]
    post0, post1 = post_norm_w[0:1], post_norm_w[1:2]

    h0, q, k, v, z = attn_in_proj(x0, pre0, attn_w_in, attn_b_in, angles)
    sink_tab = _sink_table(attn_sinks)
    late = (attn_w_out, rec_w_in, rec_w_out)
    og0, gathered = attn_fwd(q, k, v, z, sink_tab, batch, seq, gather=late if distributed else ())
    if distributed:
        attn_w_out, rec_w_in, rec_w_out = (g if name == "rec_w_in" else _whole_from_shards(name, g)
                                           for name, g in zip(_LATE, gathered))
    y0, x1 = out_proj(og0, attn_w_out, attn_b_out, x0, post0)

    h1, proj1 = rec_in_proj(x1, pre1, rec_w_in)
    og1, states, safe = rec_fwd(proj1, rec_lb_logits, rec_gnorm_w, batch, seq)
    dx2, loss_vec, dog1, d_rec_w_out, d_post1 = out_proj_loss_bwd(og1, rec_w_out, x1, post1,
                                                                   loss_target.reshape(n, D_MODEL))
    dproj1, d_lb, d_gnorm = rec_bwd(proj1, states, safe, rec_lb_logits, rec_gnorm_w, dog1, batch, seq)
    dx1, d_pre1, _ = in_proj_bwd_x(dproj1, rec_w_in, x1, pre1, dx2)
    d_rec_w_in, _ = in_proj_bwd_w(h1, dproj1, as_shards=distributed)

    dog0, d_attn_w_out, d_attn_b_out, d_post0 = out_proj_bwd(dx1, y0, og0, attn_w_out, post0)
    ready = dict(rec_w_out=d_rec_w_out, attn_w_out=d_attn_w_out)
    outgoing = [d_rec_w_in if name == "rec_w_in" else _shards_from_whole(name, ready[name]).astype(BF16)
                for name in _CARRIED] if distributed else []
    dproj0, dk, dv, d_sink_tab, arrived = attn_bwd(q, k, v, z, sink_tab, dog0, angles, batch, seq, scatter=outgoing)
    d_sinks = jnp.transpose(jnp.sum(d_sink_tab, axis=-1), (0, 2, 1)).reshape(1, N_HEADS)
    dproj0 = attn_bwd_kv(dproj0, dk, dv, angles)
    d_attn_w_in, d_attn_b_in = in_proj_bwd_w(h0, dproj0)
    last = [_shards_from_whole("attn_w_in", d_attn_w_in).astype(BF16)] if distributed else []
    dx0, d_pre0, arrived_last = in_proj_bwd_x(dproj0, attn_w_in, x0, pre0, dx1, scatter=last)

    grads = dict(
        pre_norm_w=jnp.concatenate([d_pre0, d_pre1], axis=0), post_norm_w=jnp.concatenate([d_post0, d_post1], axis=0),
        attn_w_in=d_attn_w_in, attn_b_in=d_attn_b_in, attn_sinks=d_sinks, attn_w_out=d_attn_w_out,
        attn_b_out=d_attn_b_out, rec_w_in=d_rec_w_in, rec_lb_logits=d_lb, rec_gnorm_w=d_gnorm,
        rec_w_out=d_rec_w_out)
    parts = dict(zip(_CARRIED + ("attn_w_in",), tuple(arrived) + tuple(arrived_last)))
    return loss_vec, dx0.reshape(batch, seq, D_MODEL), grads, parts


_BIG = ("attn_w_in", "attn_w_out", "rec_w_in", "rec_w_out")
_COLUMN_SHARDED = ("attn_w_in", "rec_w_in")
_ORDER = ("pre_norm_w", "post_norm_w", "attn_w_in", "attn_b_in", "attn_sinks", "attn_w_out", "attn_b_out",
          "rec_w_in", "rec_lb_logits", "rec_gnorm_w", "rec_w_out")


def _whole_from_shards(name, stacked):
    if name in _COLUMN_SHARDED:
        return jnp.transpose(stacked, (1, 0, 2)).reshape(stacked.shape[1], -1)
    return stacked.reshape(-1, stacked.shape[2])


def _shards_from_whole(name, whole):
    if name in _COLUMN_SHARDED:
        return jnp.transpose(whole.reshape(whole.shape[0], N_CHIPS, -1), (1, 0, 2))
    return whole.reshape(N_CHIPS, -1, whole.shape[1])


def kernel(x, positions, pre_norm_w, post_norm_w, attn_w_in, attn_b_in, attn_sinks, attn_w_out, attn_b_out, rec_w_in, rec_lb_logits, rec_gnorm_w, rec_w_out, loss_target, m_pre_norm_w, m_post_norm_w, m_attn_w_in, m_attn_b_in, m_attn_sinks, m_attn_w_out, m_attn_b_out, m_rec_w_in, m_rec_lb_logits, m_rec_gnorm_w, m_rec_w_out, v_pre_norm_w, v_post_norm_w, v_attn_w_in, v_attn_b_in, v_attn_sinks, v_attn_w_out, v_attn_b_out, v_rec_w_in, v_rec_lb_logits, v_rec_gnorm_w, v_rec_w_out):
    w = dict(pre_norm_w=pre_norm_w, post_norm_w=post_norm_w, attn_w_in=attn_w_in, attn_b_in=attn_b_in,
             attn_sinks=attn_sinks, attn_w_out=attn_w_out, attn_b_out=attn_b_out, rec_w_in=rec_w_in,
             rec_lb_logits=rec_lb_logits, rec_gnorm_w=rec_gnorm_w, rec_w_out=rec_w_out)
    m = dict(pre_norm_w=m_pre_norm_w, post_norm_w=m_post_norm_w, attn_w_in=m_attn_w_in, attn_b_in=m_attn_b_in,
             attn_sinks=m_attn_sinks, attn_w_out=m_attn_w_out, attn_b_out=m_attn_b_out, rec_w_in=m_rec_w_in,
             rec_lb_logits=m_rec_lb_logits, rec_gnorm_w=m_rec_gnorm_w, rec_w_out=m_rec_w_out)
    v = dict(pre_norm_w=v_pre_norm_w, post_norm_w=v_post_norm_w, attn_w_in=v_attn_w_in, attn_b_in=v_attn_b_in,
             attn_sinks=v_attn_sinks, attn_w_out=v_attn_w_out, attn_b_out=v_attn_b_out, rec_w_in=v_rec_w_in,
             rec_lb_logits=v_rec_lb_logits, rec_gnorm_w=v_rec_gnorm_w, rec_w_out=v_rec_w_out)

    shards = {name: w[name][0] for name in _BIG}
    sent = {name: shards[name].astype(BF16) for name in _BIG}
    attn_w_in_whole = _whole_from_shards("attn_w_in", chip_gather([sent["attn_w_in"]])[0])

    loss_vec, grad_x, grads, parts = local_step(
        x, positions, pre_norm_w, post_norm_w, attn_w_in_whole, attn_b_in, attn_sinks, sent["attn_w_out"],
        attn_b_out, sent["rec_w_in"], rec_lb_logits, rec_gnorm_w, sent["rec_w_out"], loss_target, distributed=True)

    plane_sums = [sum_slots(parts[name]) for name in _BIG]
    other_sums = sibling_exchange(plane_sums)
    out_g, out_d, out_m, out_v = {}, {}, {}, {}
    for name, mine, other in zip(_BIG, plane_sums, other_sums):
        g, d, nm, nv = adamw(shards[name], m[name][0], v[name][0], mine, other)
        out_g[name], out_d[name], out_m[name], out_v[name] = g[None], d[None], nm[None], nv[None]

    small_sum = sum_slots(all_gather_small(_pack_small(grads, last_row=loss_vec)))
    loss = jnp.sum(small_sum[_SMALL_ROWS - 1]) * (0.5 / D_MODEL)
    packed = adamw(_pack_small(w), _pack_small(m), _pack_small(v), small_sum)
    for dst, val in zip((out_g, out_d, out_m, out_v), packed):
        dst.update(_unpack_small(val))

    return (loss, grad_x, *[out_g[n] for n in _ORDER], *[out_d[n] for n in _ORDER],
            *[out_m[n] for n in _ORDER], *[out_v[n] for n in _ORDER])
```

```python
import functools

import jax
import jax.numpy as jnp
from jax import lax
from jax.experimental import pallas as pl
from jax.experimental.pallas import tpu as pltpu

F32 = jnp.float32
BF16 = jnp.bfloat16
MESH = pl.DeviceIdType.MESH

D_MODEL = 1024
HEAD_DIM = 64
N_HEADS = 16
N_KV_HEADS = 2
GROUP = N_HEADS // N_KV_HEADS
KV_WIDTH = N_KV_HEADS * HEAD_DIM
ATTN_IN = 2 * D_MODEL + 2 * KV_WIDTH
ATTN_BLOCK = 128
ROPE_THETA = 500000.0
ROPE_DIM = HEAD_DIM // 4
REC_HEADS = 8
REC_DIM = 128
REC_IN = 4 * D_MODEL
REC_BLOCK = 128
DIAG = 8
NORM_EPS = 1e-6
N_CHIPS = 4
N_DEV = 8
LANES = 128

ADAM_LR = 0.001
ADAM_B1 = 0.9
ADAM_B2 = 0.999
ADAM_EPS = 1e-08
ADAM_WD = 0.01
ADAM_STEP = 10

VMEM_LIMIT = 56 * 1024 * 1024


def _cparams(n_axes):
    return pltpu.CompilerParams(dimension_semantics=("arbitrary",) * n_axes, vmem_limit_bytes=VMEM_LIMIT)


def _dot(a, b, contract):
    return lax.dot_general(a.astype(BF16), b.astype(BF16), (contract, ((), ())), preferred_element_type=F32)


_NN = ((1,), (0,))
_NT = ((1,), (1,))
_TN = ((0,), (0,))


@jax.custom_vjp
def mm_nn(a, b):
    return _dot(a, b, _NN)


mm_nn.defvjp(lambda a, b: (_dot(a, b, _NN), (a, b)),
             lambda res, g: (_dot(g, res[1], _NT), _dot(res[0], g, _TN)))


@jax.custom_vjp
def mm_nt(a, b):
    return _dot(a, b, _NT)


mm_nt.defvjp(lambda a, b: (_dot(a, b, _NT), (a, b)),
             lambda res, g: (_dot(g, res[1], _NN), _dot(g, res[0], _TN)))


@jax.custom_vjp
def mm_tn(a, b):
    return _dot(a, b, _TN)


mm_tn.defvjp(lambda a, b: (_dot(a, b, _TN), (a, b)),
             lambda res, g: (_dot(res[1], g, _NT), _dot(res[0], g, _NN)))


def _tri_dot(x, lower):
    n = x.shape[0]
    r = lax.broadcasted_iota(jnp.int32, (n, n), 0)
    c = lax.broadcasted_iota(jnp.int32, (n, n), 1)
    tri = ((c <= r) if lower else (c >= r)).astype(BF16)
    hi = x.astype(BF16)
    rest = x - hi.astype(F32)
    mid = rest.astype(BF16)
    lo = (rest - mid.astype(F32)).astype(BF16)
    dot = lambda p: lax.dot_general(tri, p, (_NN, ((), ())), preferred_element_type=F32)
    return (dot(lo) + dot(mid)) + dot(hi)


@jax.custom_vjp
def cumsum_rows(x):
    return _tri_dot(x, True)


cumsum_rows.defvjp(lambda x: (cumsum_rows(x), None), lambda _, g: (_tri_dot(g, False),))


@functools.partial(jax.custom_vjp, nondiff_argnums=(1,))
def roll_sub(x, d):
    return pltpu.roll(x, d, 1) if d else x


roll_sub.defvjp(lambda x, d: (roll_sub(x, d), None),
                lambda d, _, g: (roll_sub(g, (DIAG - d) % DIAG),))


def sigmoid(x):
    return 1.0 / (1.0 + jnp.exp(-x))


@jax.custom_vjp
def silu(x):
    return x * sigmoid(x)


def _silu_fwd(x):
    s = sigmoid(x)
    return x * s, (x, s)


silu.defvjp(_silu_fwd, lambda res, g: (g * (res[1] * (1.0 + res[0] * (1.0 - res[1]))),))


F32_TINY = 1.17549435e-38


def sigmoid_pair(x):
    e = jnp.exp(-jnp.abs(x))
    r = 1.0 / (1.0 + e)
    er = e * r
    pos = x >= 0.0
    return jnp.where(pos, r, er), jnp.where(pos, er, r)


def _forget_fwd(x, a):
    lb, one_m_lb = sigmoid_pair(a)
    sp, sn = sigmoid_pair(x)
    f = lb + one_m_lb * sp
    k = one_m_lb * sn
    return (jnp.log(jnp.maximum(f, F32_TINY)), k), (sp, sn, f, k, lb, one_m_lb)


def _forget_bwd(res, g):
    sp, sn, f, k, lb, one_m_lb = res
    g_lf, g_k = g
    t = jnp.where(f >= F32_TINY, g_lf / jnp.maximum(f, F32_TINY), 0.0) - g_k
    return (k * sp) * t, jnp.sum(sn * t, axis=0, keepdims=True) * (lb * one_m_lb)


@jax.custom_vjp
def forget_gate(x, a):
    return _forget_fwd(x, a)[0]


forget_gate.defvjp(_forget_fwd, _forget_bwd)


@jax.custom_vjp
def decayed(x, e):
    return (x * jnp.exp(e)).astype(BF16).astype(F32)


def _decayed_fwd(x, e):
    y = decayed(x, e)
    return y, (y, e)


decayed.defvjp(_decayed_fwd, lambda res, g: (g * jnp.exp(res[1]), g * res[0]))


def _row(x, r):
    shape = x.shape

    @jax.custom_vjp
    def take(x):
        return x[r:r + 1, :]

    take.defvjp(lambda x: (x[r:r + 1, :], None),
                lambda _, g: (jnp.where(lax.broadcasted_iota(jnp.int32, shape, 0) == r, g, 0.0),))
    return take(x)


def _rms(x):
    return lax.rsqrt(jnp.mean(x * x, axis=-1, keepdims=True) + NORM_EPS)


def _attn_group(qs, k_a, v_a, k_b, v_b, zs, sink_a, sink_b, bias, at_sink=None):
    def half(kh, vh, sink):
        s = mm_nn(qs, kh) + bias
        if at_sink is None:
            m = jnp.maximum(jnp.max(s, axis=-1, keepdims=True), jnp.max(sink, axis=-1, keepdims=True))
            p = jnp.exp(s - lax.stop_gradient(m))
            own = jnp.sum(jnp.exp(sink - lax.stop_gradient(m)), axis=-1, keepdims=True) * (1.0 / LANES)
            return mm_nt(p * (1.0 / (jnp.sum(p, axis=-1, keepdims=True) + own)), vh)
        s = jnp.where(at_sink, jnp.concatenate([sink, sink], axis=1), s)
        p = jnp.exp(s - jnp.max(s, axis=-1, keepdims=True))
        return mm_nt(jnp.where(at_sink, 0.0, p), vh) * (1.0 / jnp.sum(p, axis=-1, keepdims=True))

    return (half(k_a, v_a, sink_a) + half(k_b, v_b, sink_b)) * silu(zs)


SAFE_RANGE = 80.0


def _rec_front(qr, fr, l0, l1):
    lf, k = forget_gate(fr, l1 - l0)
    return silu(qr), k, lf


def _rec_tail(o, z, gw):
    return o * _rms(o) * gw * silu(z)


def _rec_margin(b):
    R = b.shape[0]
    mid, last = _row(b, R // 2 - 1), _row(b, R - 1)
    return jnp.minimum(mid, last - mid)


def _heads(x):
    w = x.shape[1] // REC_HEADS
    return [x[:, h * w:(h + 1) * w] for h in range(REC_HEADS)]


def _hdot(a, b, contract):
    return jnp.concatenate([_dot(ah, bh, contract) for ah, bh in zip(_heads(a), _heads(b))], axis=1)


@jax.custom_vjp
def hmm_nn(a, b):
    return _hdot(a, b, _NN)


hmm_nn.defvjp(lambda a, b: (_hdot(a, b, _NN), (a, b)),
              lambda res, g: (_hdot(g, res[1], _NT), _hdot(res[0], g, _TN)))


@jax.custom_vjp
def hmm_nt(a, b):
    return _hdot(a, b, _NT)


hmm_nt.defvjp(lambda a, b: (_hdot(a, b, _NT), (a, b)),
              lambda res, g: (_hdot(g, res[1], _NN), _hdot(g, res[0], _TN)))


@jax.custom_vjp
def hmm_tn(a, b):
    return _hdot(a, b, _TN)


hmm_tn.defvjp(lambda a, b: (_hdot(a, b, _TN), (a, b)),
              lambda res, g: (_hdot(res[1], g, _NT), _hdot(res[0], g, _NN)))


def _head_sums(x):
    return jnp.concatenate([jnp.broadcast_to(jnp.sum(xh, axis=-1, keepdims=True), xh.shape) for xh in _heads(x)],
                           axis=1)


@jax.custom_vjp
def head_sum(x):
    return _head_sums(x)


head_sum.defvjp(lambda x: (_head_sums(x), None), lambda _, g: (_head_sums(g),))


def _rec_cores_fast(q, k, v, b, S):
    R = q.shape[0]
    ri = lax.broadcasted_iota(jnp.int32, (R, REC_HEADS * R), 0)
    ci = lax.broadcasted_iota(jnp.int32, (R, REC_HEADS * R), 1) % R
    d = b - _row(b, R // 2 - 1)
    sc = jnp.where(ci < ri, hmm_nt(decayed(q, d), decayed(k, -d)), 0.0)
    o = hmm_nt(q * jnp.exp(b), S) + hmm_nn(sc, v) + head_sum(q * k) * v
    b_last = _row(b, R - 1)
    return o, S * jnp.exp(b_last) + hmm_tn(v, k * jnp.exp(b_last - b))


def _rec_tails(o, z, gw):
    return o * lax.rsqrt(head_sum(o * o) * (1.0 / REC_DIM) + NORM_EPS) * gw * silu(z)


def _rec_block_fast(qr, fr, v, z, S, l0, l1, gw):
    lf, k = forget_gate(fr, l1 - l0)
    o, S_new = _rec_cores_fast(silu(qr), k, v, cumsum_rows(lf), S)
    return _rec_tails(o, z, gw), S_new


def _rec_core_slow(q, k, v, b, S):
    R = q.shape[0]
    rows = lax.broadcasted_iota(jnp.int32, (R, REC_DIM), 0)

    o = mm_nt(q * jnp.exp(jnp.minimum(b, 0.0)), S)

    ri = lax.broadcasted_iota(jnp.int32, (R, R), 0)
    ci = lax.broadcasted_iota(jnp.int32, (R, R), 1)
    sc = jnp.zeros((R, R), F32)
    w = R
    while w > DIAG:
        h = w // 2
        b3 = b.reshape(R // w, w, REC_DIM)
        rin = lax.broadcasted_iota(jnp.int32, (R // w, w, REC_DIM), 1)
        mid = jnp.sum(jnp.where(rin == h - 1, b3, 0.0), axis=1, keepdims=True)
        fac = jnp.exp(jnp.minimum(jnp.where(rin >= h, b3 - mid, mid - b3), 0.0)).reshape(R, REC_DIM)
        upper = (rows % w) >= h
        s_w = mm_nt(jnp.where(upper, q * fac, 0.0), jnp.where(upper, 0.0, k * fac))
        sc = sc + jnp.where((ri // w) == (ci // w), s_w, 0.0)
        w = h
    o = o + mm_nn(sc, v)

    g = R // DIAG
    q3, k3, v3, b3 = (t.reshape(g, DIAG, REC_DIM) for t in (q, k, v, b))
    rin = lax.broadcasted_iota(jnp.int32, (g, DIAG, 1), 1)
    od = jnp.zeros((g, DIAG, REC_DIM), F32)
    for d in range(DIAG):
        e = jnp.exp(jnp.minimum(b3 - roll_sub(b3, d), 0.0))
        sd = jnp.sum(q3 * roll_sub(k3, d) * e, axis=-1, keepdims=True)
        od = od + jnp.where(rin >= d, sd, 0.0) * roll_sub(v3, d)
    o = o + od.reshape(R, REC_DIM)

    b_last = _row(b, R - 1)
    return o, S * jnp.exp(jnp.minimum(b_last, 0.0)) + mm_tn(v, k * jnp.exp(jnp.minimum(b_last - b, 0.0)))


def _rec_head(core, qr, fr, v, z, S, l0, l1, gw):
    q, k, lf = _rec_front(qr, fr, l0, l1)
    o, S_new = core(q, k, v, cumsum_rows(lf), S)
    return _rec_tail(o, z, gw), S_new


ANGLE_COLS = 3 * ROPE_DIM


def _rope_angles(positions):
    half = ROPE_DIM // 2
    inv_freq = ROPE_THETA ** (-(jnp.arange(half, dtype=F32) * 2.0 / ROPE_DIM))
    ang = positions.astype(F32).reshape(-1, 1) * inv_freq
    cs = jnp.concatenate([jnp.cos(ang), jnp.sin(ang)], axis=-1)
    hi = cs.astype(BF16)
    rest = cs - hi.astype(F32)
    mid = rest.astype(BF16)
    return jnp.concatenate([hi, mid, (rest - mid.astype(F32)).astype(BF16)], axis=-1)


def _rope_tables(pieces):
    half = ROPE_DIM // 2
    r = lax.broadcasted_iota(jnp.int32, (ANGLE_COLS, 3 * LANES), 0) % ROPE_DIM
    c = lax.broadcasted_iota(jnp.int32, (ANGLE_COLS, 3 * LANES), 1)
    table, j = c // LANES, c % HEAD_DIM
    angle, low = j % half, j < half
    plus = ((table == 0) & (j < ROPE_DIM) & (r == angle)) | ((table == 1) & (j >= half) & (j < ROPE_DIM)
                                                                & (r == half + angle))
    minus = (table == 2) & low & (r == half + angle)
    pick = jnp.where(plus, 1.0, jnp.where(minus, -1.0, 0.0)).astype(BF16)
    out = jnp.dot(pieces, pick, preferred_element_type=F32)
    lane = lax.broadcasted_iota(jnp.int32, (1, LANES), 1) % HEAD_DIM
    return out[:, :LANES] + jnp.where(lane < ROPE_DIM, 0.0, 1.0), out[:, LANES:2 * LANES], out[:, 2 * LANES:]


def _rope(x, cos_t, sin_a, sin_b):
    half = ROPE_DIM // 2
    return x * cos_t + pltpu.roll(x, half, 1) * sin_a + pltpu.roll(x, LANES - half, 1) * sin_b


def _rope_transposed(g, cos_t, sin_a, sin_b):
    half = ROPE_DIM // 2
    return g * cos_t + pltpu.roll(g * sin_a, LANES - half, 1) + pltpu.roll(g * sin_b, half, 1)


def _row_spec(tm, width):
    return pl.BlockSpec((tm, width), lambda i: (i, 0))


def _weight_spec(shape):
    return pl.BlockSpec(shape, lambda *_: (0,) * len(shape), pipeline_mode=pl.Buffered(1))


def _full_spec(shape):
    return pl.BlockSpec(shape, lambda *_: (0,) * len(shape))


def attn_in_proj(x, w_pre, w_in, b_in, angles, tm=1024):
    n = x.shape[0]
    tm = min(tm, n)

    def body(x_ref, wp_ref, w_ref, b_ref, cs_ref, h_ref, q_ref, k_ref, v_ref, z_ref):
        xv = x_ref[...]
        h = (xv * _rms(xv) * wp_ref[...]).astype(BF16)
        h_ref[...] = h
        proj = jnp.dot(h, w_ref[...], preferred_element_type=F32) + b_ref[...]
        tabs = _rope_tables(cs_ref[...])
        for s in range(D_MODEL // LANES):
            sl = slice(s * LANES, (s + 1) * LANES)
            q_ref[:, sl] = _rope(proj[:, sl] * (HEAD_DIM ** -0.5), *tabs).astype(BF16)
        k_ref[...] = _rope(proj[:, D_MODEL:D_MODEL + KV_WIDTH], *tabs).astype(BF16)
        v_ref[...] = proj[:, D_MODEL + KV_WIDTH:D_MODEL + 2 * KV_WIDTH].astype(BF16)
        z_ref[...] = proj[:, D_MODEL + 2 * KV_WIDTH:]

    return pl.pallas_call(
        body, name="attn_in_proj", grid=(n // tm,),
        in_specs=[_row_spec(tm, D_MODEL), _full_spec((1, D_MODEL)), _weight_spec((D_MODEL, ATTN_IN)),
                  _full_spec((1, ATTN_IN)), _row_spec(tm, ANGLE_COLS)],
        out_specs=[_row_spec(tm, D_MODEL), _row_spec(tm, D_MODEL), _row_spec(tm, KV_WIDTH),
                   _row_spec(tm, KV_WIDTH), _row_spec(tm, D_MODEL)],
        out_shape=[jax.ShapeDtypeStruct((n, D_MODEL), BF16), jax.ShapeDtypeStruct((n, D_MODEL), BF16),
                   jax.ShapeDtypeStruct((n, KV_WIDTH), BF16), jax.ShapeDtypeStruct((n, KV_WIDTH), BF16),
                   jax.ShapeDtypeStruct((n, D_MODEL), F32)],
        compiler_params=_cparams(1),
    )(x, w_pre, w_in, b_in, angles)


def _column_blocks(w):
    if len(w.shape) == 2:
        return [slice(0, w.shape[1])], lambda ref, s: ref[...]
    width = w.shape[2]
    return [slice(s * width, (s + 1) * width) for s in range(w.shape[0])], lambda ref, s: ref[s]


def rec_in_proj(x, w_pre, w_in, tm=1024):
    n = x.shape[0]
    tm = min(tm, n)
    columns, block = _column_blocks(w_in)

    def body(x_ref, wp_ref, w_ref, h_ref, p_ref):
        xv = x_ref[...]
        h = (xv * _rms(xv) * wp_ref[...]).astype(BF16)
        h_ref[...] = h
        for s, cols in enumerate(columns):
            p_ref[:, cols] = jnp.dot(h, block(w_ref, s), preferred_element_type=F32)

    return pl.pallas_call(
        body, name="rec_in_proj", grid=(n // tm,),
        in_specs=[_row_spec(tm, D_MODEL), _full_spec((1, D_MODEL)), _weight_spec(w_in.shape)],
        out_specs=[_row_spec(tm, D_MODEL), _row_spec(tm, REC_IN)],
        out_shape=[jax.ShapeDtypeStruct((n, D_MODEL), BF16), jax.ShapeDtypeStruct((n, REC_IN), F32)],
        compiler_params=_cparams(1),
    )(x, w_pre, w_in)


def out_proj(og, w_out, b_out, x_res, w_post, tm=1024):
    n = og.shape[0]
    tm = min(tm, n)

    def body(og_ref, w_ref, b_ref, x_ref, wp_ref, y_ref, xo_ref):
        y = jnp.dot(og_ref[...], w_ref[...], preferred_element_type=F32) + b_ref[...]
        y_ref[...] = y.astype(BF16)
        xo_ref[...] = x_ref[...] + y * _rms(y) * wp_ref[...]

    return pl.pallas_call(
        body, name="out_proj", grid=(n // tm,),
        in_specs=[_row_spec(tm, D_MODEL), _weight_spec((D_MODEL, D_MODEL)), _full_spec((1, D_MODEL)),
                  _row_spec(tm, D_MODEL), _full_spec((1, D_MODEL))],
        out_specs=[_row_spec(tm, D_MODEL), _row_spec(tm, D_MODEL)],
        out_shape=[jax.ShapeDtypeStruct((n, D_MODEL), BF16), jax.ShapeDtypeStruct((n, D_MODEL), F32)],
        compiler_params=_cparams(1),
    )(og, w_out, b_out, x_res, w_post)


def _post_norm_bwd(g, y, w_post):
    rstd = _rms(y)
    yn = y * rstd
    gw = g * w_post
    return rstd * (gw - yn * jnp.mean(gw * yn, axis=-1, keepdims=True)), jnp.sum(g * yn, axis=0, keepdims=True)


def out_proj_loss_bwd(og, w_out, x_res, w_post, target, tm=1024):
    n = og.shape[0]
    tm = min(tm, n)

    def body(og_ref, w_ref, x_ref, wp_ref, t_ref, dx_ref, l_ref, dog_ref, dw_ref, dwp_ref):
        @pl.when(pl.program_id(0) == 0)
        def _():
            l_ref[...] = jnp.zeros_like(l_ref)
            dw_ref[...] = jnp.zeros_like(dw_ref)
            dwp_ref[...] = jnp.zeros_like(dwp_ref)

        og_tile = og_ref[...]
        y = jnp.dot(og_tile, w_ref[...], preferred_element_type=F32)
        err = x_ref[...] + y * _rms(y) * wp_ref[...] - t_ref[...]
        g = err * (1.0 / D_MODEL)
        dx_ref[...] = g
        l_ref[...] += jnp.sum(err * err, axis=0, keepdims=True)
        dy, dwp = _post_norm_bwd(g, y, wp_ref[...])
        dwp_ref[...] += dwp
        dyb = dy.astype(BF16)
        dog_ref[...] = _dot(dyb, w_ref[...], _NT).astype(BF16)
        dw_ref[...] += _dot(og_tile, dyb, _TN)

    return pl.pallas_call(
        body, name="out_proj_loss_bwd", grid=(n // tm,),
        in_specs=[_row_spec(tm, D_MODEL), _weight_spec((D_MODEL, D_MODEL)), _row_spec(tm, D_MODEL),
                  _full_spec((1, D_MODEL)), _row_spec(tm, D_MODEL)],
        out_specs=[_row_spec(tm, D_MODEL), _full_spec((1, D_MODEL)), _row_spec(tm, D_MODEL),
                   _full_spec((D_MODEL, D_MODEL)), _full_spec((1, D_MODEL))],
        out_shape=[jax.ShapeDtypeStruct((n, D_MODEL), F32), jax.ShapeDtypeStruct((1, D_MODEL), F32),
                   jax.ShapeDtypeStruct((n, D_MODEL), BF16), jax.ShapeDtypeStruct((D_MODEL, D_MODEL), F32),
                   jax.ShapeDtypeStruct((1, D_MODEL), F32)],
        compiler_params=_cparams(1),
    )(og, w_out, x_res, w_post, target)


def out_proj_bwd(dxo, y, og, w_out, w_post, tm=1024):
    n = og.shape[0]
    tm = min(tm, n)

    def body(g_ref, y_ref, og_ref, w_ref, wp_ref, dog_ref, dw_ref, db_ref, dwp_ref):
        @pl.when(pl.program_id(0) == 0)
        def _():
            dw_ref[...] = jnp.zeros_like(dw_ref)
            db_ref[...] = jnp.zeros_like(db_ref)
            dwp_ref[...] = jnp.zeros_like(dwp_ref)

        dy, dwp = _post_norm_bwd(g_ref[...], y_ref[...].astype(F32), wp_ref[...])
        dwp_ref[...] += dwp
        db_ref[...] += jnp.sum(dy, axis=0, keepdims=True)
        dyb = dy.astype(BF16)
        dog_ref[...] = _dot(dyb, w_ref[...], _NT).astype(BF16)
        dw_ref[...] += _dot(og_ref[...], dyb, _TN)

    return pl.pallas_call(
        body, name="out_proj_bwd", grid=(n // tm,),
        in_specs=[_row_spec(tm, D_MODEL), _row_spec(tm, D_MODEL), _row_spec(tm, D_MODEL),
                  _weight_spec((D_MODEL, D_MODEL)), _full_spec((1, D_MODEL))],
        out_specs=[_row_spec(tm, D_MODEL), _full_spec((D_MODEL, D_MODEL)), _full_spec((1, D_MODEL)),
                   _full_spec((1, D_MODEL))],
        out_shape=[jax.ShapeDtypeStruct((n, D_MODEL), BF16), jax.ShapeDtypeStruct((D_MODEL, D_MODEL), F32),
                   jax.ShapeDtypeStruct((1, D_MODEL), F32), jax.ShapeDtypeStruct((1, D_MODEL), F32)],
        compiler_params=_cparams(1),
    )(dxo, y, og, w_out, w_post)


def in_proj_bwd_x(dproj, w_in, x, w_pre, dxo, tm=1024, scatter=(), tiles=None, dx_so_far=None):
    n, p = dproj.shape
    tm = min(tm, n)
    first, steps = tiles if tiles is not None else (0, n // tm)
    ns = len(scatter)
    columns, block = _column_blocks(w_in)
    rows = lambda width: pl.BlockSpec((tm, width), lambda i: (first + i, 0))
    carried = [] if dx_so_far is None else [dx_so_far]

    def body(*refs):
        dp_ref, w_ref, x_ref, wp_ref, g_ref = refs[:5]
        outs = 5 + ns + len(carried)
        dx_ref, dwp_ref = refs[outs:outs + 2]
        exchange = (refs[5:5 + ns], refs[outs + 2:outs + 2 + ns]) + tuple(refs[outs + 2 + ns:])

        @pl.when(pl.program_id(0) == 0)
        def _():
            dwp_ref[...] = jnp.zeros_like(dwp_ref)
            if ns:
                _scatter_start(*exchange)

        dh = functools.reduce(jnp.add, [_dot(dp_ref[:, cols], block(w_ref, s), _NT)
                                        for s, cols in enumerate(columns)])
        xv = x_ref[...]
        rstd = _rms(xv)
        xn = xv * rstd
        gw = dh * wp_ref[...]
        dwp_ref[...] += jnp.sum(dh * xn, axis=0, keepdims=True)
        dx_ref[...] = rstd * (gw - xn * jnp.mean(gw * xn, axis=-1, keepdims=True)) + g_ref[...]

        if ns:
            @pl.when(pl.program_id(0) == steps - 1)
            def _():
                _scatter_finish(*exchange)

    out = pl.pallas_call(
        body, name=f"in_proj_bwd_x_{p}" + (f"_from_{first}" if tiles is not None else ""), grid=(steps,),
        in_specs=[rows(p), _weight_spec(w_in.shape), rows(D_MODEL), _full_spec((1, D_MODEL)), rows(D_MODEL)]
        + [_ANY] * (ns + len(carried)),
        out_specs=[rows(D_MODEL), _full_spec((1, D_MODEL))] + [_ANY] * ns,
        out_shape=[jax.ShapeDtypeStruct((n, D_MODEL), F32), jax.ShapeDtypeStruct((1, D_MODEL), F32)]
        + [jax.ShapeDtypeStruct(a.shape, a.dtype) for a in scatter],
        scratch_shapes=_scatter_sems(ns) if ns else [],
        input_output_aliases={5 + ns: 0} if carried else {},
        compiler_params=_cparams(1),
    )(dproj, w_in, x, w_pre, dxo, *scatter, *carried)
    return out[0], out[1], out[2:]


def in_proj_bwd_w(h, dproj, tm=1024, as_shards=False):
    n, p = dproj.shape
    chunk = p // (4 if p % 4096 == 0 else 3)
    tm = min(tm, n)
    steps = n // tm
    shard = p // N_CHIPS

    def body(h_ref, dp_ref, dw_ref, db_ref, acc_scr, sem, *staging):
        i = pl.program_id(0)

        @pl.when(i == 0)
        def _():
            acc_scr[...] = jnp.zeros_like(acc_scr)
            db_ref[...] = jnp.zeros_like(db_ref)

        ht = h_ref[...].T
        for c0 in range(0, p, chunk):
            dp = dp_ref[:, c0:c0 + chunk]
            acc_scr[:, c0:c0 + chunk] += jnp.dot(ht, dp, preferred_element_type=F32)
            db_ref[:, c0:c0 + chunk] += jnp.sum(dp.astype(F32), axis=0, keepdims=True)

        @pl.when(i == steps - 1)
        def _():
            if as_shards:
                for s in range(N_CHIPS):
                    staging[0][...] = acc_scr[:, s * shard:(s + 1) * shard].astype(BF16)
                    out = pltpu.make_async_copy(staging[0], dw_ref.at[s], sem)
                    out.start()
                    out.wait()
            else:
                out = pltpu.make_async_copy(acc_scr, dw_ref, sem)
                out.start()
                out.wait()

    dw_shape = jax.ShapeDtypeStruct((N_CHIPS, D_MODEL, shard), BF16) if as_shards else (
        jax.ShapeDtypeStruct((D_MODEL, p), F32))
    return pl.pallas_call(
        body, name=f"in_proj_bwd_w_{p}", grid=(steps,),
        in_specs=[_row_spec(tm, D_MODEL), _row_spec(tm, p)],
        out_specs=[_ANY, _full_spec((1, p))],
        out_shape=[dw_shape, jax.ShapeDtypeStruct((1, p), F32)],
        scratch_shapes=[pltpu.VMEM((D_MODEL, p), F32), pltpu.SemaphoreType.DMA]
        + ([pltpu.VMEM((D_MODEL, shard), BF16)] if as_shards else []),
        compiler_params=_cparams(1),
    )(h, dproj)


PAIRS = GROUP // 2
GROUP_ROWS = PAIRS * ATTN_BLOCK
MASKED = -1e30


def _kv_windows(k_ref, v_ref, i):
    ps = pl.multiple_of(jnp.maximum(i - 1, 0) * ATTN_BLOCK, ATTN_BLOCK)
    cs = pl.multiple_of(i * ATTN_BLOCK, ATTN_BLOCK)
    kw = jnp.concatenate([k_ref[pl.ds(ps, ATTN_BLOCK), :], k_ref[pl.ds(cs, ATTN_BLOCK), :]], axis=0)
    vw = jnp.concatenate([v_ref[pl.ds(ps, ATTN_BLOCK), :], v_ref[pl.ds(cs, ATTN_BLOCK), :]], axis=0)
    return kw.astype(F32).T, vw.astype(F32).T, ps, cs


def _low_rows(shape):
    return lax.broadcasted_iota(jnp.int32, shape, 0) < HEAD_DIM


def _spread(w, kvh):
    low = _low_rows(w.shape)
    swapped = pltpu.roll(w, HEAD_DIM, 0)
    if kvh == 0:
        return jnp.where(low, w, 0.0), jnp.where(low, 0.0, swapped)
    return jnp.where(low, swapped, 0.0), jnp.where(low, 0.0, w)


def _unspread(d_a, d_b, kvh):
    low = _low_rows(d_a.shape)
    if kvh == 0:
        return jnp.where(low, d_a + pltpu.roll(d_b, HEAD_DIM, 0), 0.0)
    return jnp.where(low, 0.0, pltpu.roll(d_a, HEAD_DIM, 0) + d_b)


def _stack_pairs(ref, kvh):
    return jnp.concatenate([ref[:, (kvh * PAIRS + j) * LANES:(kvh * PAIRS + j + 1) * LANES] for j in range(PAIRS)],
                           axis=0)


def _fill_bias(bias_scr):
    shape = (GROUP_ROWS, 2 * ATTN_BLOCK)
    r = lax.broadcasted_iota(jnp.int32, shape, 0) % ATTN_BLOCK
    c = lax.broadcasted_iota(jnp.int32, shape, 1)
    in_cur = (c >= ATTN_BLOCK) & ((c - ATTN_BLOCK) <= r)
    in_prev = (c < ATTN_BLOCK) & (c > r)
    bias_scr[0] = jnp.where(in_cur, 0.0, MASKED)
    bias_scr[1] = jnp.where(in_cur | in_prev, 0.0, MASKED)
    bias_scr[2] = jnp.where(c == r, 1.0, 0.0)


N_BIAS_TABLES = 3


def _sink_table(sinks):
    t = jnp.transpose(sinks.reshape(N_KV_HEADS, PAIRS, 2), (0, 2, 1))
    return jnp.broadcast_to(t[:, :, :, None, None], (N_KV_HEADS, 2, PAIRS, ATTN_BLOCK, LANES)).reshape(
        N_KV_HEADS, 2, GROUP_ROWS, LANES)


def attn_fwd(q, k, v, z, sink_tab, batch, seq, gather=()):
    nb = seq // ATTN_BLOCK
    ng = len(gather)

    def body(*refs):
        q_ref, k_ref, v_ref, z_ref, s_ref = refs[:5]
        og_ref, bias_scr = refs[5 + ng], refs[6 + 2 * ng]
        exchange = (refs[5:5 + ng], refs[6 + ng:6 + 2 * ng]) + tuple(refs[7 + 2 * ng:])
        b, i = pl.program_id(0), pl.program_id(1)

        @pl.when((b == 0) & (i == 0))
        def _():
            _fill_bias(bias_scr)
            if ng:
                _gather_start(*exchange)

        kw, vw, _, _ = _kv_windows(k_ref, v_ref, i)
        bias, at_sink = bias_scr[jnp.minimum(i, 1)], bias_scr[2] > 0.5
        for kvh in range(N_KV_HEADS):
            k_a, k_b = _spread(kw, kvh)
            v_a, v_b = _spread(vw, kvh)
            og = _attn_group(_stack_pairs(q_ref, kvh), k_a, v_a, k_b, v_b, _stack_pairs(z_ref, kvh),
                             s_ref[kvh, 0], s_ref[kvh, 1], bias, at_sink)
            for j in range(PAIRS):
                og_ref[:, (kvh * PAIRS + j) * LANES:(kvh * PAIRS + j + 1) * LANES] = (
                    og[j * ATTN_BLOCK:(j + 1) * ATTN_BLOCK].astype(BF16))

        if ng:
            @pl.when((b == batch - 1) & (i == nb - 1))
            def _():
                _gather_finish(*exchange)

    blk = lambda w: pl.BlockSpec((ATTN_BLOCK, w), lambda b, i: (b * nb + i, 0))
    seq_spec = pl.BlockSpec((seq, KV_WIDTH), lambda b, i: (b, 0))
    out = pl.pallas_call(
        body, name="attn_fwd", grid=(batch, nb),
        in_specs=[blk(D_MODEL), seq_spec, seq_spec, blk(D_MODEL), _full_spec(sink_tab.shape)] + [_ANY] * ng,
        out_specs=[blk(D_MODEL)] + [_ANY] * ng,
        out_shape=[jax.ShapeDtypeStruct((batch * seq, D_MODEL), BF16)]
        + [jax.ShapeDtypeStruct((N_CHIPS,) + a.shape, a.dtype) for a in gather],
        scratch_shapes=[pltpu.VMEM((N_BIAS_TABLES, GROUP_ROWS, 2 * ATTN_BLOCK), F32)] + (_gather_sems(ng) if ng else []),
        compiler_params=_cparams(2),
    )(q, k, v, z, sink_tab, *gather)
    return out[0], out[1:]


def attn_bwd(q, k, v, z, sink_tab, dog, angles, batch, seq, scatter=()):
    nb = seq // ATTN_BLOCK
    ns = len(scatter)

    def body(*refs):
        q_ref, k_ref, v_ref, z_ref, s_ref, g_ref, cs_ref = refs[:7]
        dp_ref, dk_ref, dv_ref, ds_ref = refs[7 + ns:11 + ns]
        bias_scr = refs[11 + 2 * ns]
        exchange = (refs[7:7 + ns], refs[11 + ns:11 + 2 * ns]) + tuple(refs[12 + 2 * ns:])
        b, i = pl.program_id(0), pl.program_id(1)

        @pl.when((b == 0) & (i == 0))
        def _():
            _fill_bias(bias_scr)
            ds_ref[...] = jnp.zeros_like(ds_ref)
            if ns:
                _scatter_start(*exchange)

        @pl.when(i == 0)
        def _():
            dk_ref[...] = jnp.zeros_like(dk_ref)
            dv_ref[...] = jnp.zeros_like(dv_ref)

        kw, vw, ps, cs = _kv_windows(k_ref, v_ref, i)
        bias = bias_scr[jnp.minimum(i, 1)]
        tabs = _rope_tables(cs_ref[...])
        dkw = jnp.zeros_like(kw)
        dvw = jnp.zeros_like(vw)
        for kvh in range(N_KV_HEADS):
            k_a, k_b = _spread(kw, kvh)
            v_a, v_b = _spread(vw, kvh)
            _, vjp = jax.vjp(functools.partial(_attn_group, bias=bias), _stack_pairs(q_ref, kvh).astype(F32),
                             k_a, v_a, k_b, v_b, _stack_pairs(z_ref, kvh), s_ref[kvh, 0], s_ref[kvh, 1])
            dqs, dk_a, dv_a, dk_b, dv_b, dzs, ds_a, ds_b = vjp(_stack_pairs(g_ref, kvh).astype(F32))
            dkw = dkw + _unspread(dk_a, dk_b, kvh)
            dvw = dvw + _unspread(dv_a, dv_b, kvh)
            ds_ref[kvh, 0] += jnp.sum(ds_a.reshape(PAIRS, ATTN_BLOCK, LANES), axis=1)
            ds_ref[kvh, 1] += jnp.sum(ds_b.reshape(PAIRS, ATTN_BLOCK, LANES), axis=1)
            for j in range(PAIRS):
                rows = slice(j * ATTN_BLOCK, (j + 1) * ATTN_BLOCK)
                col = (kvh * PAIRS + j) * LANES
                dp_ref[:, col:col + LANES] = _rope_transposed(dqs[rows] * (HEAD_DIM ** -0.5), *tabs).astype(BF16)
                zc = D_MODEL + 2 * KV_WIDTH + col
                dp_ref[:, zc:zc + LANES] = dzs[rows].astype(BF16)
        dp_ref[:, D_MODEL:D_MODEL + 2 * KV_WIDTH] = jnp.zeros((ATTN_BLOCK, 2 * KV_WIDTH), BF16)
        dk_ref[:, pl.ds(ps, ATTN_BLOCK)] += dkw[:, :ATTN_BLOCK]
        dk_ref[:, pl.ds(cs, ATTN_BLOCK)] += dkw[:, ATTN_BLOCK:]
        dv_ref[:, pl.ds(ps, ATTN_BLOCK)] += dvw[:, :ATTN_BLOCK]
        dv_ref[:, pl.ds(cs, ATTN_BLOCK)] += dvw[:, ATTN_BLOCK:]

        if ns:
            @pl.when((b == batch - 1) & (i == nb - 1))
            def _():
                _scatter_finish(*exchange)

    blk = lambda w: pl.BlockSpec((ATTN_BLOCK, w), lambda b, i: (b * nb + i, 0))
    seq_spec = pl.BlockSpec((seq, KV_WIDTH), lambda b, i: (b, 0))
    seq_spec_t = pl.BlockSpec((KV_WIDTH, seq), lambda b, i: (0, b))
    n = batch * seq
    ds_shape = (N_KV_HEADS, 2, PAIRS, LANES)
    out = pl.pallas_call(
        body, name="attn_bwd", grid=(batch, nb),
        in_specs=[blk(D_MODEL), seq_spec, seq_spec, blk(D_MODEL), _full_spec(sink_tab.shape), blk(D_MODEL)]
        + [blk(ANGLE_COLS)] + [_ANY] * ns,
        out_specs=[blk(ATTN_IN), seq_spec_t, seq_spec_t, _full_spec(ds_shape)] + [_ANY] * ns,
        out_shape=[jax.ShapeDtypeStruct((n, ATTN_IN), BF16), jax.ShapeDtypeStruct((KV_WIDTH, n), F32),
                   jax.ShapeDtypeStruct((KV_WIDTH, n), F32), jax.ShapeDtypeStruct(ds_shape, F32)]
        + [jax.ShapeDtypeStruct(a.shape, a.dtype) for a in scatter],
        scratch_shapes=[pltpu.VMEM((N_BIAS_TABLES, GROUP_ROWS, 2 * ATTN_BLOCK), F32)] + (_scatter_sems(ns) if ns else []),
        compiler_params=_cparams(2),
    )(q, k, v, z, sink_tab, dog, angles, *scatter)
    return out[0], out[1], out[2], out[3], out[4:]


def attn_bwd_kv(dproj, dk_t, dv_t, angles, tm=512):
    n = dproj.shape[0]

    def body(dp_in_ref, dk_ref, dv_ref, cs_ref, dp_ref):
        del dp_in_ref
        dp_ref[:, :KV_WIDTH] = _rope_transposed(dk_ref[...].T, *_rope_tables(cs_ref[...])).astype(BF16)
        dp_ref[:, KV_WIDTH:] = dv_ref[...].T.astype(BF16)

    kv_cols = pl.BlockSpec((tm, 2 * KV_WIDTH), lambda i: (i, D_MODEL // (2 * KV_WIDTH)))
    col_spec = pl.BlockSpec((KV_WIDTH, tm), lambda i: (0, i))
    return pl.pallas_call(
        body, name="attn_bwd_kv", grid=(n // tm,),
        in_specs=[kv_cols, col_spec, col_spec, _row_spec(tm, ANGLE_COLS)],
        out_specs=kv_cols, out_shape=jax.ShapeDtypeStruct(dproj.shape, BF16),
        input_output_aliases={0: 0}, compiler_params=_cparams(1),
    )(dproj, dk_t, dv_t, angles)


def rec_fwd(proj, lb_logits, gnorm_w, batch, seq):
    nblk = seq // REC_BLOCK

    def body(p_ref, lb_ref, gw_ref, og_ref, st_ref, safe_ref, s_scr):
        @pl.when(pl.program_id(1) == 0)
        def _():
            s_scr[...] = jnp.zeros_like(s_scr)

        S = s_scr[...]
        st_ref[0] = S
        qr, fr, v, z = (p_ref[:, part * D_MODEL:(part + 1) * D_MODEL] for part in range(4))
        lf, k = forget_gate(fr, lb_ref[1:2, :] - lb_ref[0:1, :])
        q, b = silu(qr), cumsum_rows(lf)
        safe = jnp.min(_rec_margin(b)) >= -SAFE_RANGE

        gate = gw_ref[...] * silu(z)
        safe_ref[0] = jnp.full((REC_HEADS, LANES), safe.astype(F32))

        def store(o, S_new):
            og_ref[...] = (o * lax.rsqrt(head_sum(o * o) * (1.0 / REC_DIM) + NORM_EPS) * gate).astype(BF16)
            s_scr[...] = S_new

        @pl.when(safe)
        def _():
            store(*_rec_cores_fast(q, k, v, b, S))

        @pl.when(jnp.logical_not(safe))
        def _():
            outs = [_rec_core_slow(*args) for args in zip(*(_heads(t) for t in (q, k, v, b, S)))]
            store(*(jnp.concatenate(parts, axis=1) for parts in zip(*outs)))

    blk = lambda w: pl.BlockSpec((REC_BLOCK, w), lambda b, j: (b * nblk + j, 0))
    st_spec = pl.BlockSpec((1, REC_DIM, D_MODEL), lambda b, j: (b * nblk + j, 0, 0))
    safe_spec = pl.BlockSpec((1, REC_HEADS, LANES), lambda b, j: (b * nblk + j, 0, 0))
    return pl.pallas_call(
        body, name="rec_fwd", grid=(batch, nblk),
        in_specs=[blk(REC_IN), _full_spec((2, D_MODEL)), _full_spec((1, D_MODEL))],
        out_specs=[blk(D_MODEL), st_spec, safe_spec],
        out_shape=[jax.ShapeDtypeStruct((batch * seq, D_MODEL), BF16),
                   jax.ShapeDtypeStruct((batch * nblk, REC_DIM, D_MODEL), F32),
                   jax.ShapeDtypeStruct((batch * nblk, REC_HEADS, LANES), F32)],
        scratch_shapes=[pltpu.VMEM((REC_DIM, D_MODEL), F32)],
        compiler_params=_cparams(2),
    )(proj, lb_logits, jnp.tile(gnorm_w, (1, REC_HEADS)))


def rec_bwd(proj, states, safe, lb_logits, gnorm_w, dog, batch, seq):
    nblk = seq // REC_BLOCK

    def body(p_ref, st_ref, safe_ref, lb_ref, gw_ref, g_ref, dp_ref, dlb_ref, dgw_ref, ds_scr):
        @pl.when((pl.program_id(0) == 0) & (pl.program_id(1) == 0))
        def _():
            dlb_ref[...] = jnp.zeros_like(dlb_ref)
            dgw_ref[...] = jnp.zeros_like(dgw_ref)

        @pl.when(pl.program_id(1) == 0)
        def _():
            ds_scr[...] = jnp.zeros_like(ds_scr)

        def load():
            primals = tuple(p_ref[:, part * D_MODEL:(part + 1) * D_MODEL] for part in range(4)) + (
                st_ref[0], lb_ref[0:1, :], lb_ref[1:2, :], gw_ref[...])
            return primals, (g_ref[...].astype(F32), ds_scr[...])

        def store(dqr, dfr, dv, dz, dS, dl0, dl1, dgw):
            for part, val in enumerate((dqr, dfr, dv, dz)):
                dp_ref[:, part * D_MODEL:(part + 1) * D_MODEL] = val.astype(BF16)
            ds_scr[...] = dS
            dlb_ref[0:1, :] += dl0
            dlb_ref[1:2, :] += dl1
            dgw_ref[...] += functools.reduce(jnp.add, _heads(dgw))

        fast = jnp.max(safe_ref[0]) > 0.5

        @pl.when(fast)
        def _():
            primals, cotangents = load()
            store(*jax.vjp(_rec_block_fast, *primals)[1](cotangents))

        @pl.when(jnp.logical_not(fast))
        def _():
            primals, cotangents = load()
            outs = [jax.vjp(functools.partial(_rec_head, _rec_core_slow), *args)[1](cts)
                    for args, cts in zip(zip(*(_heads(t) for t in primals)), zip(*(_heads(t) for t in cotangents)))]
            store(*(jnp.concatenate(parts, axis=1) for parts in zip(*outs)))

    blk = lambda w: pl.BlockSpec((REC_BLOCK, w), lambda b, j: (b * nblk + nblk - 1 - j, 0))
    st_spec = pl.BlockSpec((1, REC_DIM, D_MODEL), lambda b, j: (b * nblk + nblk - 1 - j, 0, 0))
    safe_spec = pl.BlockSpec((1, REC_HEADS, LANES), lambda b, j: (b * nblk + nblk - 1 - j, 0, 0))
    return pl.pallas_call(
        body, name="rec_bwd", grid=(batch, nblk),
        in_specs=[blk(REC_IN), st_spec, safe_spec, _full_spec((2, D_MODEL)), _full_spec((1, D_MODEL)),
                  blk(D_MODEL)],
        out_specs=[blk(REC_IN), _full_spec((2, D_MODEL)), _full_spec((1, REC_DIM))],
        out_shape=[jax.ShapeDtypeStruct((batch * seq, REC_IN), BF16), jax.ShapeDtypeStruct((2, D_MODEL), F32),
                   jax.ShapeDtypeStruct((1, REC_DIM), F32)],
        scratch_shapes=[pltpu.VMEM((REC_DIM, D_MODEL), F32)],
        compiler_params=_cparams(2),
    )(proj, states, safe, lb_logits, jnp.tile(gnorm_w, (1, REC_HEADS)), dog)


_ANY = pl.BlockSpec(memory_space=pl.ANY)


def _chip_peers():
    x, y, c = lax.axis_index("x"), lax.axis_index("y"), lax.axis_index("c")
    peers = []
    for fx, fy in ((1, 0), (0, 1), (1, 1)):
        px, py = (1 - x if fx else x), (1 - y if fy else y)
        peers.append(((px, py, c), 2 * px + py))
    return 2 * x + y, peers


def _remote(src, dst, send_sem, recv_sem, device):
    return pltpu.make_async_remote_copy(src_ref=src, dst_ref=dst, send_sem=send_sem, recv_sem=recv_sem,
                                        device_id=device, device_id_type=MESH)


N_FLIPS = N_CHIPS - 1


def _scatter_sems(n):
    return [pltpu.SemaphoreType.DMA((n * N_FLIPS,)), pltpu.SemaphoreType.DMA((n * N_FLIPS,)),
            pltpu.SemaphoreType.DMA((n,))]


def _scatter_copies(ins, outs, send_sems, recv_sems, local_sems, starting):
    me, peers = _chip_peers()
    local = [pltpu.make_async_copy(ins[k].at[me], outs[k].at[me], local_sems.at[k]) for k in range(len(ins))]
    sends, arrivals = [], []
    for k in range(len(ins)):
        for j, (device, idx) in enumerate(peers):
            sems = (send_sems.at[k * N_FLIPS + j], recv_sems.at[k * N_FLIPS + j], device)
            sends.append(_remote(ins[k].at[idx], outs[k].at[me], *sems))
            if not starting:
                arrivals.append(_remote(ins[k].at[me], outs[k].at[idx], *sems))
    return local, sends, arrivals


def _scatter_start(*refs):
    local, sends, _ = _scatter_copies(*refs, starting=True)
    for cp in local + sends:
        cp.start()


def _scatter_finish(*refs):
    local, sends, arrivals = _scatter_copies(*refs, starting=False)
    for cp in arrivals:
        cp.wait_recv()
    for cp in sends:
        cp.wait_send()
    for cp in local:
        cp.wait()


def _gather_sems(n):
    return [pltpu.SemaphoreType.DMA((n * N_FLIPS,)) for _ in range(4)] + [pltpu.SemaphoreType.DMA((n,))]


def _gather_copies(ins, outs, send_sems, recv_sems, pass_send_sems, pass_recv_sems, local_sems, starting):
    me, peers = _chip_peers()
    c = lax.axis_index("c")
    sibling = (lax.axis_index("x"), lax.axis_index("y"), 1 - c)
    local = [pltpu.make_async_copy(ins[k], outs[k].at[me], local_sems.at[k]) for k in range(len(ins))]
    sends, arrivals, passes, pass_arrivals = [], [], [], []
    for k in range(len(ins)):
        half = ins[k].shape[0] // 2
        mine, other = pl.ds(c * half, half), pl.ds((1 - c) * half, half)
        for j, (device, idx) in enumerate(peers):
            s = k * N_FLIPS + j
            sends.append(_remote(ins[k].at[mine], outs[k].at[me].at[mine], send_sems.at[s], recv_sems.at[s], device))
            if starting:
                continue
            arrived = outs[k].at[idx].at[mine]
            arrivals.append(_remote(ins[k].at[mine], arrived, send_sems.at[s], recv_sems.at[s], device))
            passes.append(_remote(arrived, arrived, pass_send_sems.at[s], pass_recv_sems.at[s], sibling))
            passed = outs[k].at[idx].at[other]
            pass_arrivals.append(_remote(passed, passed, pass_send_sems.at[s], pass_recv_sems.at[s], sibling))
    return local, sends, arrivals, passes, pass_arrivals


def _gather_start(*refs):
    local, sends, _, _, _ = _gather_copies(*refs, starting=True)
    for cp in local + sends:
        cp.start()


def _gather_finish(*refs):
    local, sends, arrivals, passes, pass_arrivals = _gather_copies(*refs, starting=False)
    for arrival, onward in zip(arrivals, passes):
        arrival.wait_recv()
        onward.start()
    for cp in pass_arrivals:
        cp.wait_recv()
    for cp in sends + passes:
        cp.wait_send()
    for cp in local:
        cp.wait()


def chip_gather(arrays):
    n = len(arrays)

    def body(*refs):
        _gather_start(refs[:n], refs[n:2 * n], *refs[2 * n:])
        _gather_finish(refs[:n], refs[n:2 * n], *refs[2 * n:])

    return pl.pallas_call(
        body, name="chip_gather", in_specs=[_ANY] * n, out_specs=[_ANY] * n,
        out_shape=[jax.ShapeDtypeStruct((N_CHIPS,) + a.shape, a.dtype) for a in arrays],
        scratch_shapes=_gather_sems(n),
    )(*arrays)


def sibling_exchange(arrays):
    n = len(arrays)

    def body(*refs):
        ins, outs = refs[:n], refs[n:2 * n]
        send_sems, recv_sems = refs[2 * n:]
        sibling = (lax.axis_index("x"), lax.axis_index("y"), 1 - lax.axis_index("c"))
        copies = [pltpu.make_async_remote_copy(src_ref=ins[k], dst_ref=outs[k], send_sem=send_sems.at[k],
                                               recv_sem=recv_sems.at[k], device_id=sibling, device_id_type=MESH)
                  for k in range(n)]
        for cp in copies:
            cp.start()
        for cp in copies:
            cp.wait()

    return pl.pallas_call(
        body, name="sibling_exchange", in_specs=[_ANY] * n, out_specs=[_ANY] * n,
        out_shape=[jax.ShapeDtypeStruct(a.shape, a.dtype) for a in arrays],
        scratch_shapes=[pltpu.SemaphoreType.DMA((n,)), pltpu.SemaphoreType.DMA((n,))],
    )(*arrays)


def all_gather_small(vec):
    def body(v_ref, out_ref, send_sems, recv_sems, local_sem):
        x, y, c = lax.axis_index("x"), lax.axis_index("y"), lax.axis_index("c")
        me = 4 * x + 2 * y + c
        local = pltpu.make_async_copy(v_ref, out_ref.at[me], local_sem)
        local.start()
        sends, recvs = [], []
        for j in range(1, N_DEV):
            px = jnp.where(j & 4, 1 - x, x)
            py = jnp.where(j & 2, 1 - y, y)
            pc = jnp.where(j & 1, 1 - c, c)
            common = dict(send_sem=send_sems.at[j - 1], recv_sem=recv_sems.at[j - 1], device_id=(px, py, pc),
                          device_id_type=MESH)
            sends.append(pltpu.make_async_remote_copy(src_ref=v_ref, dst_ref=out_ref.at[me], **common))
            recvs.append(pltpu.make_async_remote_copy(src_ref=v_ref, dst_ref=out_ref.at[4 * px + 2 * py + pc],
                                                      **common))
        for cp in sends:
            cp.start()
        for cp in recvs:
            cp.wait_recv()
        for cp in sends:
            cp.wait_send()
        local.wait()

    return pl.pallas_call(
        body, name="all_gather_small", in_specs=[_ANY], out_specs=_ANY,
        out_shape=jax.ShapeDtypeStruct((N_DEV,) + vec.shape, vec.dtype),
        scratch_shapes=[pltpu.SemaphoreType.DMA((N_DEV - 1,)), pltpu.SemaphoreType.DMA((N_DEV - 1,)),
                        pltpu.SemaphoreType.DMA],
    )(vec)


def sum_slots(stacked, tm=256):
    s, r, c = stacked.shape
    tm = min(tm, r)

    def body(in_ref, out_ref):
        acc = in_ref[0].astype(F32)
        for t in range(1, s):
            acc = acc + in_ref[t].astype(F32)
        out_ref[...] = acc

    return pl.pallas_call(
        body, name=f"sum_slots_{s}_{r}_{c}", grid=(r // tm,),
        in_specs=[pl.BlockSpec((s, tm, c), lambda i: (0, i, 0))], out_specs=_row_spec(tm, c),
        out_shape=jax.ShapeDtypeStruct((r, c), F32), compiler_params=_cparams(1),
    )(stacked)


def adamw(w, m, v, g_a, g_b=None, tm=256):
    r, c = w.shape
    tm = min(tm, r)
    two = g_b is not None

    def body(*refs):
        w_ref, m_ref, v_ref, ga_ref = refs[:4]
        g_ref, d_ref, nm_ref, nv_ref = refs[-4:]
        g = ga_ref[...] + refs[4][...] if two else ga_ref[...]
        nm = ADAM_B1 * m_ref[...] + (1.0 - ADAM_B1) * g
        nv = ADAM_B2 * v_ref[...] + (1.0 - ADAM_B2) * (g * g)
        m_hat = nm / (1.0 - ADAM_B1 ** ADAM_STEP)
        v_hat = nv / (1.0 - ADAM_B2 ** ADAM_STEP)
        g_ref[...] = g
        d_ref[...] = -ADAM_LR * (m_hat / (jnp.sqrt(v_hat) + ADAM_EPS) + ADAM_WD * w_ref[...])
        nm_ref[...] = nm
        nv_ref[...] = nv

    args = [w, m, v, g_a] + ([g_b] if two else [])
    return pl.pallas_call(
        body, name=f"adamw_{r}_{c}", grid=(r // tm,),
        in_specs=[_row_spec(tm, c)] * len(args), out_specs=[_row_spec(tm, c)] * 4,
        out_shape=[jax.ShapeDtypeStruct((r, c), F32)] * 4, compiler_params=_cparams(1),
    )(*args)


_SMALL = (("pre_norm_w", (2, D_MODEL)), ("post_norm_w", (2, D_MODEL)), ("attn_b_in", (1, ATTN_IN)),
          ("attn_sinks", (1, N_HEADS)), ("attn_b_out", (1, D_MODEL)), ("rec_lb_logits", (2, D_MODEL)),
          ("rec_gnorm_w", (1, REC_DIM)))
_SMALL_ROWS = 16


def _pack_small(parts, last_row=None):
    rows = []
    for (name, shape) in _SMALL:
        flat = parts[name].reshape(-1)
        pad = -flat.shape[0] % D_MODEL
        rows.append(jnp.pad(flat, (0, pad)).reshape(-1, D_MODEL))
    used = sum(r.shape[0] for r in rows)
    rows.append(jnp.zeros((_SMALL_ROWS - 1 - used, D_MODEL), F32))
    rows.append(jnp.zeros((1, D_MODEL), F32) if last_row is None else last_row)
    return jnp.concatenate(rows, axis=0)


def _unpack_small(packed):
    out, row = {}, 0
    for (name, shape) in _SMALL:
        size = shape[0] * shape[1]
        nrows = -(-size // D_MODEL)
        out[name] = packed[row:row + nrows].reshape(-1)[:size].reshape(shape)
        row += nrows
    return out


_CARRIED = ("rec_w_in", "rec_w_out", "attn_w_out")


_LATE = ("attn_w_out", "rec_w_in", "rec_w_out")


def local_step(x, positions, pre_norm_w, post_norm_w, attn_w_in, attn_b_in, attn_sinks, attn_w_out, attn_b_out,
               rec_w_in, rec_lb_logits, rec_gnorm_w, rec_w_out, loss_target, distributed=False):
    batch, seq, _ = x.shape
    n = batch * seq
    x0 = x.reshape(n, D_MODEL)
    angles = _rope_angles(positions)
    pre0, pre1 = pre_norm_w[0:1], pre_norm_w[1:2]
    post0, post1 = post_norm_w[0:1], post_norm_w[1:2]

    h0, q, k, v, z = attn_in_proj(x0, pre0, attn_w_in, attn_b_in, angles)
    sink_tab = _sink_table(attn_sinks)
    late = (attn_w_out, rec_w_in, rec_w_out)
    og0, gathered = attn_fwd(q, k, v, z, sink_tab, batch, seq, gather=late if distributed else ())
    if distributed:
        attn_w_out, rec_w_in, rec_w_out = (g if name == "rec_w_in" else _whole_from_shards(name, g)
                                           for name, g in zip(_LATE, gathered))
    y0, x1 = out_proj(og0, attn_w_out, attn_b_out, x0, post0)

    h1, proj1 = rec_in_proj(x1, pre1, rec_w_in)
    og1, states, safe = rec_fwd(proj1, rec_lb_logits, rec_gnorm_w, batch, seq)
    dx2, loss_vec, dog1, d_rec_w_out, d_post1 = out_proj_loss_bwd(og1, rec_w_out, x1, post1,
                                                                   loss_target.reshape(n, D_MODEL))
    dproj1, d_lb, d_gnorm = rec_bwd(proj1, states, safe, rec_lb_logits, rec_gnorm_w, dog1, batch, seq)
    dx1, d_pre1, _ = in_proj_bwd_x(dproj1, rec_w_in, x1, pre1, dx2)
    d_rec_w_in, _ = in_proj_bwd_w(h1, dproj1, as_shards=distributed)

    dog0, d_attn_w_out, d_attn_b_out, d_post0 = out_proj_bwd(dx1, y0, og0, attn_w_out, post0)
    ready = dict(rec_w_out=d_rec_w_out, attn_w_out=d_attn_w_out)
    outgoing = [d_rec_w_in if name == "rec_w_in" else _shards_from_whole(name, ready[name]).astype(BF16)
                for name in _CARRIED] if distributed else []
    dproj0, dk, dv, d_sink_tab, arrived = attn_bwd(q, k, v, z, sink_tab, dog0, angles, batch, seq, scatter=outgoing)
    d_sinks = jnp.transpose(jnp.sum(d_sink_tab, axis=-1), (0, 2, 1)).reshape(1, N_HEADS)
    dproj0 = attn_bwd_kv(dproj0, dk, dv, angles)
    d_attn_w_in, d_attn_b_in = in_proj_bwd_w(h0, dproj0)
    last = [_shards_from_whole("attn_w_in", d_attn_w_in).astype(BF16)] if distributed else []
    n_tiles = n // min(1024, n)
    if distributed and n_tiles > 1:
        dx_most, d_pre_most, arrived_last = in_proj_bwd_x(dproj0, attn_w_in, x0, pre0, dx1, tm=1024, scatter=last,
                                                          tiles=(0, n_tiles - 1))
        dx0, d_pre_rest, _ = in_proj_bwd_x(dproj0, attn_w_in, x0, pre0, dx1, tm=1024, tiles=(n_tiles - 1, 1),
                                           dx_so_far=dx_most)
        d_pre0 = d_pre_most + d_pre_rest
    else:
        dx0, d_pre0, arrived_last = in_proj_bwd_x(dproj0, attn_w_in, x0, pre0, dx1, scatter=last)

    grads = dict(
        pre_norm_w=jnp.concatenate([d_pre0, d_pre1], axis=0), post_norm_w=jnp.concatenate([d_post0, d_post1], axis=0),
        attn_w_in=d_attn_w_in, attn_b_in=d_attn_b_in, attn_sinks=d_sinks, attn_w_out=d_attn_w_out,
        attn_b_out=d_attn_b_out, rec_w_in=d_rec_w_in, rec_lb_logits=d_lb, rec_gnorm_w=d_gnorm,
        rec_w_out=d_rec_w_out)
    parts = dict(zip(_CARRIED + ("attn_w_in",), tuple(arrived) + tuple(arrived_last)))
    return loss_vec, dx0.reshape(batch, seq, D_MODEL), grads, parts


_BIG = ("attn_w_in", "attn_w_out", "rec_w_in", "rec_w_out")
_COLUMN_SHARDED = ("attn_w_in", "rec_w_in")
_ORDER = ("pre_norm_w", "post_norm_w", "attn_w_in", "attn_b_in", "attn_sinks", "attn_w_out", "attn_b_out",
          "rec_w_in", "rec_lb_logits", "rec_gnorm_w", "rec_w_out")


def _whole_from_shards(name, stacked):
    if name in _COLUMN_SHARDED:
        return jnp.transpose(stacked, (1, 0, 2)).reshape(stacked.shape[1], -1)
    return stacked.reshape(-1, stacked.shape[2])


def _shards_from_whole(name, whole):
    if name in _COLUMN_SHARDED:
        return jnp.transpose(whole.reshape(whole.shape[0], N_CHIPS, -1), (1, 0, 2))
    return whole.reshape(N_CHIPS, -1, whole.shape[1])


def kernel(x, positions, pre_norm_w, post_norm_w, attn_w_in, attn_b_in, attn_sinks, attn_w_out, attn_b_out, rec_w_in, rec_lb_logits, rec_gnorm_w, rec_w_out, loss_target, m_pre_norm_w, m_post_norm_w, m_attn_w_in, m_attn_b_in, m_attn_sinks, m_attn_w_out, m_attn_b_out, m_rec_w_in, m_rec_lb_logits, m_rec_gnorm_w, m_rec_w_out, v_pre_norm_w, v_post_norm_w, v_attn_w_in, v_attn_b_in, v_attn_sinks, v_attn_w_out, v_attn_b_out, v_rec_w_in, v_rec_lb_logits, v_rec_gnorm_w, v_rec_w_out):
    w = dict(pre_norm_w=pre_norm_w, post_norm_w=post_norm_w, attn_w_in=attn_w_in, attn_b_in=attn_b_in,
             attn_sinks=attn_sinks, attn_w_out=attn_w_out, attn_b_out=attn_b_out, rec_w_in=rec_w_in,
             rec_lb_logits=rec_lb_logits, rec_gnorm_w=rec_gnorm_w, rec_w_out=rec_w_out)
    m = dict(pre_norm_w=m_pre_norm_w, post_norm_w=m_post_norm_w, attn_w_in=m_attn_w_in, attn_b_in=m_attn_b_in,
             attn_sinks=m_attn_sinks, attn_w_out=m_attn_w_out, attn_b_out=m_attn_b_out, rec_w_in=m_rec_w_in,
             rec_lb_logits=m_rec_lb_logits, rec_gnorm_w=m_rec_gnorm_w, rec_w_out=m_rec_w_out)
    v = dict(pre_norm_w=v_pre_norm_w, post_norm_w=v_post_norm_w, attn_w_in=v_attn_w_in, attn_b_in=v_attn_b_in,
             attn_sinks=v_attn_sinks, attn_w_out=v_attn_w_out, attn_b_out=v_attn_b_out, rec_w_in=v_rec_w_in,
             rec_lb_logits=v_rec_lb_logits, rec_gnorm_w=v_rec_gnorm_w, rec_w_out=v_rec_w_out)

    shards = {name: w[name][0] for name in _BIG}
    sent = {name: shards[name].astype(BF16) for name in _BIG}
    attn_w_in_whole = _whole_from_shards("attn_w_in", chip_gather([sent["attn_w_in"]])[0])

    loss_vec, grad_x, grads, parts = local_step(
        x, positions, pre_norm_w, post_norm_w, attn_w_in_whole, attn_b_in, attn_sinks, sent["attn_w_out"],
        attn_b_out, sent["rec_w_in"], rec_lb_logits, rec_gnorm_w, sent["rec_w_out"], loss_target, distributed=True)

    plane_sums = [sum_slots(parts[name]) for name in _BIG]
    other_sums = sibling_exchange(plane_sums)
    out_g, out_d, out_m, out_v = {}, {}, {}, {}
    for name, mine, other in zip(_BIG, plane_sums, other_sums):
        g, d, nm, nv = adamw(shards[name], m[name][0], v[name][0], mine, other)
        out_g[name], out_d[name], out_m[name], out_v[name] = g[None], d[None], nm[None], nv[None]

    small_sum = sum_slots(all_gather_small(_pack_small(grads, last_row=loss_vec)))
    loss = jnp.sum(small_sum[_SMALL_ROWS - 1]) * (0.5 / D_MODEL)
    packed = adamw(_pack_small(w), _pack_small(m), _pack_small(v), small_sum)
    for dst, val in zip((out_g, out_d, out_m, out_v), packed):
        dst.update(_unpack_small(val))

    return (loss, grad_x, *[out_g[n] for n in _ORDER], *[out_d[n] for n in _ORDER],
            *[out_m[n] for n in _ORDER], *[out_v[n] for n in _ORDER])
```

```python
import functools

import jax
import jax.numpy as jnp
from jax import lax
from jax.experimental import pallas as pl
from jax.experimental.pallas import tpu as pltpu

F32 = jnp.float32
BF16 = jnp.bfloat16
MESH = pl.DeviceIdType.MESH

D_MODEL = 1024
HEAD_DIM = 64
N_HEADS = 16
N_KV_HEADS = 2
GROUP = N_HEADS // N_KV_HEADS
KV_WIDTH = N_KV_HEADS * HEAD_DIM
ATTN_IN = 2 * D_MODEL + 2 * KV_WIDTH
ATTN_BLOCK = 128
ROPE_THETA = 500000.0
ROPE_DIM = HEAD_DIM // 4
REC_HEADS = 8
REC_DIM = 128
REC_IN = 4 * D_MODEL
REC_BLOCK = 128
DIAG = 8
NORM_EPS = 1e-6
N_CHIPS = 4
N_DEV = 8
LANES = 128

ADAM_LR = 0.001
ADAM_B1 = 0.9
ADAM_B2 = 0.999
ADAM_EPS = 1e-08
ADAM_WD = 0.01
ADAM_STEP = 10

VMEM_LIMIT = 56 * 1024 * 1024


def _cparams(n_axes):
    return pltpu.CompilerParams(dimension_semantics=("arbitrary",) * n_axes, vmem_limit_bytes=VMEM_LIMIT)


def _dot(a, b, contract):
    return lax.dot_general(a.astype(BF16), b.astype(BF16), (contract, ((), ())), preferred_element_type=F32)


_NN = ((1,), (0,))
_NT = ((1,), (1,))
_TN = ((0,), (0,))


@jax.custom_vjp
def mm_nn(a, b):
    return _dot(a, b, _NN)


mm_nn.defvjp(lambda a, b: (_dot(a, b, _NN), (a, b)),
             lambda res, g: (_dot(g, res[1], _NT), _dot(res[0], g, _TN)))


@jax.custom_vjp
def mm_nt(a, b):
    return _dot(a, b, _NT)


mm_nt.defvjp(lambda a, b: (_dot(a, b, _NT), (a, b)),
             lambda res, g: (_dot(g, res[1], _NN), _dot(g, res[0], _TN)))


@jax.custom_vjp
def mm_tn(a, b):
    return _dot(a, b, _TN)


mm_tn.defvjp(lambda a, b: (_dot(a, b, _TN), (a, b)),
             lambda res, g: (_dot(res[1], g, _NT), _dot(res[0], g, _NN)))


def _tri_dot(x, lower):
    n = x.shape[0]
    r = lax.broadcasted_iota(jnp.int32, (n, n), 0)
    c = lax.broadcasted_iota(jnp.int32, (n, n), 1)
    tri = ((c <= r) if lower else (c >= r)).astype(BF16)
    hi = x.astype(BF16)
    rest = x - hi.astype(F32)
    mid = rest.astype(BF16)
    lo = (rest - mid.astype(F32)).astype(BF16)
    dot = lambda p: lax.dot_general(tri, p, (_NN, ((), ())), preferred_element_type=F32)
    return (dot(lo) + dot(mid)) + dot(hi)


@jax.custom_vjp
def cumsum_rows(x):
    return _tri_dot(x, True)


cumsum_rows.defvjp(lambda x: (cumsum_rows(x), None), lambda _, g: (_tri_dot(g, False),))


@functools.partial(jax.custom_vjp, nondiff_argnums=(1,))
def roll_sub(x, d):
    return pltpu.roll(x, d, 1) if d else x


roll_sub.defvjp(lambda x, d: (roll_sub(x, d), None),
                lambda d, _, g: (roll_sub(g, (DIAG - d) % DIAG),))


def sigmoid(x):
    return 1.0 / (1.0 + jnp.exp(-x))


@jax.custom_vjp
def silu(x):
    return x * sigmoid(x)


def _silu_fwd(x):
    s = sigmoid(x)
    return x * s, (x, s)


silu.defvjp(_silu_fwd, lambda res, g: (g * (res[1] * (1.0 + res[0] * (1.0 - res[1]))),))


F32_TINY = 1.17549435e-38


def sigmoid_pair(x):
    e = jnp.exp(-jnp.abs(x))
    r = 1.0 / (1.0 + e)
    er = e * r
    pos = x >= 0.0
    return jnp.where(pos, r, er), jnp.where(pos, er, r)


def _forget_fwd(x, a):
    lb, one_m_lb = sigmoid_pair(a)
    sp, sn = sigmoid_pair(x)
    f = lb + one_m_lb * sp
    k = one_m_lb * sn
    return (jnp.log(jnp.maximum(f, F32_TINY)), k), (sp, sn, f, k, lb, one_m_lb)


def _forget_bwd(res, g):
    sp, sn, f, k, lb, one_m_lb = res
    g_lf, g_k = g
    t = jnp.where(f >= F32_TINY, g_lf / jnp.maximum(f, F32_TINY), 0.0) - g_k
    return (k * sp) * t, jnp.sum(sn * t, axis=0, keepdims=True) * (lb * one_m_lb)


@jax.custom_vjp
def forget_gate(x, a):
    return _forget_fwd(x, a)[0]


forget_gate.defvjp(_forget_fwd, _forget_bwd)


@jax.custom_vjp
def decayed(x, e):
    return (x * jnp.exp(e)).astype(BF16).astype(F32)


def _decayed_fwd(x, e):
    y = decayed(x, e)
    return y, (y, e)


decayed.defvjp(_decayed_fwd, lambda res, g: (g * jnp.exp(res[1]), g * res[0]))


def _row(x, r):
    shape = x.shape

    @jax.custom_vjp
    def take(x):
        return x[r:r + 1, :]

    take.defvjp(lambda x: (x[r:r + 1, :], None),
                lambda _, g: (jnp.where(lax.broadcasted_iota(jnp.int32, shape, 0) == r, g, 0.0),))
    return take(x)


def _rms(x):
    return lax.rsqrt(jnp.mean(x * x, axis=-1, keepdims=True) + NORM_EPS)


def _attn_group(qs, k_a, v_a, k_b, v_b, zs, sink_a, sink_b, bias, at_sink=None):
    def half(kh, vh, sink):
        s = mm_nn(qs, kh) + bias
        if at_sink is None:
            m = jnp.maximum(jnp.max(s, axis=-1, keepdims=True), jnp.max(sink, axis=-1, keepdims=True))
            p = jnp.exp(s - lax.stop_gradient(m))
            own = jnp.sum(jnp.exp(sink - lax.stop_gradient(m)), axis=-1, keepdims=True) * (1.0 / LANES)
            return mm_nt(p * (1.0 / (jnp.sum(p, axis=-1, keepdims=True) + own)), vh)
        s = jnp.where(at_sink, jnp.concatenate([sink, sink], axis=1), s)
        p = jnp.exp(s - jnp.max(s, axis=-1, keepdims=True))
        return mm_nt(jnp.where(at_sink, 0.0, p), vh) * (1.0 / jnp.sum(p, axis=-1, keepdims=True))

    return (half(k_a, v_a, sink_a) + half(k_b, v_b, sink_b)) * silu(zs)


SAFE_RANGE = 80.0


def _rec_front(qr, fr, l0, l1):
    lf, k = forget_gate(fr, l1 - l0)
    return silu(qr), k, lf


def _rec_tail(o, z, gw):
    return o * _rms(o) * gw * silu(z)


def _rec_margin(b):
    R = b.shape[0]
    mid, last = _row(b, R // 2 - 1), _row(b, R - 1)
    return jnp.minimum(mid, last - mid)


def _heads(x):
    w = x.shape[1] // REC_HEADS
    return [x[:, h * w:(h + 1) * w] for h in range(REC_HEADS)]


def _hdot(a, b, contract):
    return jnp.concatenate([_dot(ah, bh, contract) for ah, bh in zip(_heads(a), _heads(b))], axis=1)


@jax.custom_vjp
def hmm_nn(a, b):
    return _hdot(a, b, _NN)


hmm_nn.defvjp(lambda a, b: (_hdot(a, b, _NN), (a, b)),
              lambda res, g: (_hdot(g, res[1], _NT), _hdot(res[0], g, _TN)))


@jax.custom_vjp
def hmm_nt(a, b):
    return _hdot(a, b, _NT)


hmm_nt.defvjp(lambda a, b: (_hdot(a, b, _NT), (a, b)),
              lambda res, g: (_hdot(g, res[1], _NN), _hdot(g, res[0], _TN)))


@jax.custom_vjp
def hmm_tn(a, b):
    return _hdot(a, b, _TN)


hmm_tn.defvjp(lambda a, b: (_hdot(a, b, _TN), (a, b)),
              lambda res, g: (_hdot(res[1], g, _NT), _hdot(res[0], g, _NN)))


def _head_sums(x):
    return jnp.concatenate([jnp.broadcast_to(jnp.sum(xh, axis=-1, keepdims=True), xh.shape) for xh in _heads(x)],
                           axis=1)


@jax.custom_vjp
def head_sum(x):
    return _head_sums(x)


head_sum.defvjp(lambda x: (_head_sums(x), None), lambda _, g: (_head_sums(g),))


def _rec_cores_fast(q, k, v, b, S):
    R = q.shape[0]
    ri = lax.broadcasted_iota(jnp.int32, (R, REC_HEADS * R), 0)
    ci = lax.broadcasted_iota(jnp.int32, (R, REC_HEADS * R), 1) % R
    d = b - _row(b, R // 2 - 1)
    sc = jnp.where(ci < ri, hmm_nt(decayed(q, d), decayed(k, -d)), 0.0)
    o = hmm_nt(q * jnp.exp(b), S) + hmm_nn(sc, v) + head_sum(q * k) * v
    b_last = _row(b, R - 1)
    return o, S * jnp.exp(b_last) + hmm_tn(v, k * jnp.exp(b_last - b))


def _rec_tails(o, z, gw):
    return o * lax.rsqrt(head_sum(o * o) * (1.0 / REC_DIM) + NORM_EPS) * gw * silu(z)


def _rec_block_fast(qr, fr, v, z, S, l0, l1, gw):
    lf, k = forget_gate(fr, l1 - l0)
    o, S_new = _rec_cores_fast(silu(qr), k, v, cumsum_rows(lf), S)
    return _rec_tails(o, z, gw), S_new


def _rec_core_slow(q, k, v, b, S):
    R = q.shape[0]
    rows = lax.broadcasted_iota(jnp.int32, (R, REC_DIM), 0)

    o = mm_nt(q * jnp.exp(jnp.minimum(b, 0.0)), S)

    ri = lax.broadcasted_iota(jnp.int32, (R, R), 0)
    ci = lax.broadcasted_iota(jnp.int32, (R, R), 1)
    sc = jnp.zeros((R, R), F32)
    w = R
    while w > DIAG:
        h = w // 2
        b3 = b.reshape(R // w, w, REC_DIM)
        rin = lax.broadcasted_iota(jnp.int32, (R // w, w, REC_DIM), 1)
        mid = jnp.sum(jnp.where(rin == h - 1, b3, 0.0), axis=1, keepdims=True)
        fac = jnp.exp(jnp.minimum(jnp.where(rin >= h, b3 - mid, mid - b3), 0.0)).reshape(R, REC_DIM)
        upper = (rows % w) >= h
        s_w = mm_nt(jnp.where(upper, q * fac, 0.0), jnp.where(upper, 0.0, k * fac))
        sc = sc + jnp.where((ri // w) == (ci // w), s_w, 0.0)
        w = h
    o = o + mm_nn(sc, v)

    g = R // DIAG
    q3, k3, v3, b3 = (t.reshape(g, DIAG, REC_DIM) for t in (q, k, v, b))
    rin = lax.broadcasted_iota(jnp.int32, (g, DIAG, 1), 1)
    od = jnp.zeros((g, DIAG, REC_DIM), F32)
    for d in range(DIAG):
        e = jnp.exp(jnp.minimum(b3 - roll_sub(b3, d), 0.0))
        sd = jnp.sum(q3 * roll_sub(k3, d) * e, axis=-1, keepdims=True)
        od = od + jnp.where(rin >= d, sd, 0.0) * roll_sub(v3, d)
    o = o + od.reshape(R, REC_DIM)

    b_last = _row(b, R - 1)
    return o, S * jnp.exp(jnp.minimum(b_last, 0.0)) + mm_tn(v, k * jnp.exp(jnp.minimum(b_last - b, 0.0)))


def _rec_head(core, qr, fr, v, z, S, l0, l1, gw):
    q, k, lf = _rec_front(qr, fr, l0, l1)
    o, S_new = core(q, k, v, cumsum_rows(lf), S)
    return _rec_tail(o, z, gw), S_new


ANGLE_COLS = 3 * ROPE_DIM


def _rope_angles(positions):
    half = ROPE_DIM // 2
    inv_freq = ROPE_THETA ** (-(jnp.arange(half, dtype=F32) * 2.0 / ROPE_DIM))
    ang = positions.astype(F32).reshape(-1, 1) * inv_freq
    cs = jnp.concatenate([jnp.cos(ang), jnp.sin(ang)], axis=-1)
    hi = cs.astype(BF16)
    rest = cs - hi.astype(F32)
    mid = rest.astype(BF16)
    return jnp.concatenate([hi, mid, (rest - mid.astype(F32)).astype(BF16)], axis=-1)


def _rope_tables(pieces):
    half = ROPE_DIM // 2
    r = lax.broadcasted_iota(jnp.int32, (ANGLE_COLS, 3 * LANES), 0) % ROPE_DIM
    c = lax.broadcasted_iota(jnp.int32, (ANGLE_COLS, 3 * LANES), 1)
    table, j = c // LANES, c % HEAD_DIM
    angle, low = j % half, j < half
    plus = ((table == 0) & (j < ROPE_DIM) & (r == angle)) | ((table == 1) & (j >= half) & (j < ROPE_DIM)
                                                                & (r == half + angle))
    minus = (table == 2) & low & (r == half + angle)
    pick = jnp.where(plus, 1.0, jnp.where(minus, -1.0, 0.0)).astype(BF16)
    out = jnp.dot(pieces, pick, preferred_element_type=F32)
    lane = lax.broadcasted_iota(jnp.int32, (1, LANES), 1) % HEAD_DIM
    return out[:, :LANES] + jnp.where(lane < ROPE_DIM, 0.0, 1.0), out[:, LANES:2 * LANES], out[:, 2 * LANES:]


def _rope(x, cos_t, sin_a, sin_b):
    half = ROPE_DIM // 2
    return x * cos_t + pltpu.roll(x, half, 1) * sin_a + pltpu.roll(x, LANES - half, 1) * sin_b


def _rope_transposed(g, cos_t, sin_a, sin_b):
    half = ROPE_DIM // 2
    return g * cos_t + pltpu.roll(g * sin_a, LANES - half, 1) + pltpu.roll(g * sin_b, half, 1)


def _row_spec(tm, width):
    return pl.BlockSpec((tm, width), lambda i: (i, 0))


def _weight_spec(shape):
    return pl.BlockSpec(shape, lambda *_: (0,) * len(shape), pipeline_mode=pl.Buffered(1))


def _full_spec(shape):
    return pl.BlockSpec(shape, lambda *_: (0,) * len(shape))


def attn_in_proj(x, w_pre, w_in, b_in, angles, tm=1024):
    n = x.shape[0]
    tm = min(tm, n)

    def body(x_ref, wp_ref, w_ref, b_ref, cs_ref, h_ref, q_ref, k_ref, v_ref, z_ref):
        xv = x_ref[...]
        h = (xv * _rms(xv) * wp_ref[...]).astype(BF16)
        h_ref[...] = h
        proj = jnp.dot(h, w_ref[...], preferred_element_type=F32) + b_ref[...]
        tabs = _rope_tables(cs_ref[...])
        for s in range(D_MODEL // LANES):
            sl = slice(s * LANES, (s + 1) * LANES)
            q_ref[:, sl] = _rope(proj[:, sl] * (HEAD_DIM ** -0.5), *tabs).astype(BF16)
        k_ref[...] = _rope(proj[:, D_MODEL:D_MODEL + KV_WIDTH], *tabs).astype(BF16)
        v_ref[...] = proj[:, D_MODEL + KV_WIDTH:D_MODEL + 2 * KV_WIDTH].astype(BF16)
        z_ref[...] = proj[:, D_MODEL + 2 * KV_WIDTH:]

    return pl.pallas_call(
        body, name="attn_in_proj", grid=(n // tm,),
        in_specs=[_row_spec(tm, D_MODEL), _full_spec((1, D_MODEL)), _weight_spec((D_MODEL, ATTN_IN)),
                  _full_spec((1, ATTN_IN)), _row_spec(tm, ANGLE_COLS)],
        out_specs=[_row_spec(tm, D_MODEL), _row_spec(tm, D_MODEL), _row_spec(tm, KV_WIDTH),
                   _row_spec(tm, KV_WIDTH), _row_spec(tm, D_MODEL)],
        out_shape=[jax.ShapeDtypeStruct((n, D_MODEL), BF16), jax.ShapeDtypeStruct((n, D_MODEL), BF16),
                   jax.ShapeDtypeStruct((n, KV_WIDTH), BF16), jax.ShapeDtypeStruct((n, KV_WIDTH), BF16),
                   jax.ShapeDtypeStruct((n, D_MODEL), F32)],
        compiler_params=_cparams(1),
    )(x, w_pre, w_in, b_in, angles)


def _column_blocks(w):
    if len(w.shape) == 2:
        return [slice(0, w.shape[1])], lambda ref, s: ref[...]
    width = w.shape[2]
    return [slice(s * width, (s + 1) * width) for s in range(w.shape[0])], lambda ref, s: ref[s]


def rec_in_proj(x, w_pre, w_in, tm=1024):
    n = x.shape[0]
    tm = min(tm, n)
    columns, block = _column_blocks(w_in)

    def body(x_ref, wp_ref, w_ref, h_ref, p_ref):
        xv = x_ref[...]
        h = (xv * _rms(xv) * wp_ref[...]).astype(BF16)
        h_ref[...] = h
        for s, cols in enumerate(columns):
            p_ref[:, cols] = jnp.dot(h, block(w_ref, s), preferred_element_type=F32)

    return pl.pallas_call(
        body, name="rec_in_proj", grid=(n // tm,),
        in_specs=[_row_spec(tm, D_MODEL), _full_spec((1, D_MODEL)), _weight_spec(w_in.shape)],
        out_specs=[_row_spec(tm, D_MODEL), _row_spec(tm, REC_IN)],
        out_shape=[jax.ShapeDtypeStruct((n, D_MODEL), BF16), jax.ShapeDtypeStruct((n, REC_IN), F32)],
        compiler_params=_cparams(1),
    )(x, w_pre, w_in)


def out_proj(og, w_out, b_out, x_res, w_post, tm=1024):
    n = og.shape[0]
    tm = min(tm, n)

    def body(og_ref, w_ref, b_ref, x_ref, wp_ref, y_ref, xo_ref):
        y = jnp.dot(og_ref[...], w_ref[...], preferred_element_type=F32) + b_ref[...]
        y_ref[...] = y.astype(BF16)
        xo_ref[...] = x_ref[...] + y * _rms(y) * wp_ref[...]

    return pl.pallas_call(
        body, name="out_proj", grid=(n // tm,),
        in_specs=[_row_spec(tm, D_MODEL), _weight_spec((D_MODEL, D_MODEL)), _full_spec((1, D_MODEL)),
                  _row_spec(tm, D_MODEL), _full_spec((1, D_MODEL))],
        out_specs=[_row_spec(tm, D_MODEL), _row_spec(tm, D_MODEL)],
        out_shape=[jax.ShapeDtypeStruct((n, D_MODEL), BF16), jax.ShapeDtypeStruct((n, D_MODEL), F32)],
        compiler_params=_cparams(1),
    )(og, w_out, b_out, x_res, w_post)


def _post_norm_bwd(g, y, w_post):
    rstd = _rms(y)
    yn = y * rstd
    gw = g * w_post
    return rstd * (gw - yn * jnp.mean(gw * yn, axis=-1, keepdims=True)), jnp.sum(g * yn, axis=0, keepdims=True)


def out_proj_loss_bwd(og, w_out, x_res, w_post, target, tm=1024):
    n = og.shape[0]
    tm = min(tm, n)

    def body(og_ref, w_ref, x_ref, wp_ref, t_ref, dx_ref, l_ref, dog_ref, dw_ref, dwp_ref):
        @pl.when(pl.program_id(0) == 0)
        def _():
            l_ref[...] = jnp.zeros_like(l_ref)
            dw_ref[...] = jnp.zeros_like(dw_ref)
            dwp_ref[...] = jnp.zeros_like(dwp_ref)

        og_tile = og_ref[...]
        y = jnp.dot(og_tile, w_ref[...], preferred_element_type=F32)
        err = x_ref[...] + y * _rms(y) * wp_ref[...] - t_ref[...]
        g = err * (1.0 / D_MODEL)
        dx_ref[...] = g
        l_ref[...] += jnp.sum(err * err, axis=0, keepdims=True)
        dy, dwp = _post_norm_bwd(g, y, wp_ref[...])
        dwp_ref[...] += dwp
        dyb = dy.astype(BF16)
        dog_ref[...] = _dot(dyb, w_ref[...], _NT).astype(BF16)
        dw_ref[...] += _dot(og_tile, dyb, _TN)

    return pl.pallas_call(
        body, name="out_proj_loss_bwd", grid=(n // tm,),
        in_specs=[_row_spec(tm, D_MODEL), _weight_spec((D_MODEL, D_MODEL)), _row_spec(tm, D_MODEL),
                  _full_spec((1, D_MODEL)), _row_spec(tm, D_MODEL)],
        out_specs=[_row_spec(tm, D_MODEL), _full_spec((1, D_MODEL)), _row_spec(tm, D_MODEL),
                   _full_spec((D_MODEL, D_MODEL)), _full_spec((1, D_MODEL))],
        out_shape=[jax.ShapeDtypeStruct((n, D_MODEL), F32), jax.ShapeDtypeStruct((1, D_MODEL), F32),
                   jax.ShapeDtypeStruct((n, D_MODEL), BF16), jax.ShapeDtypeStruct((D_MODEL, D_MODEL), F32),
                   jax.ShapeDtypeStruct((1, D_MODEL), F32)],
        compiler_params=_cparams(1),
    )(og, w_out, x_res, w_post, target)


def out_proj_bwd(dxo, y, og, w_out, w_post, tm=1024):
    n = og.shape[0]
    tm = min(tm, n)

    def body(g_ref, y_ref, og_ref, w_ref, wp_ref, dog_ref, dw_ref, db_ref, dwp_ref):
        @pl.when(pl.program_id(0) == 0)
        def _():
            dw_ref[...] = jnp.zeros_like(dw_ref)
            db_ref[...] = jnp.zeros_like(db_ref)
            dwp_ref[...] = jnp.zeros_like(dwp_ref)

        dy, dwp = _post_norm_bwd(g_ref[...], y_ref[...].astype(F32), wp_ref[...])
        dwp_ref[...] += dwp
        db_ref[...] += jnp.sum(dy, axis=0, keepdims=True)
        dyb = dy.astype(BF16)
        dog_ref[...] = _dot(dyb, w_ref[...], _NT).astype(BF16)
        dw_ref[...] += _dot(og_ref[...], dyb, _TN)

    return pl.pallas_call(
        body, name="out_proj_bwd", grid=(n // tm,),
        in_specs=[_row_spec(tm, D_MODEL), _row_spec(tm, D_MODEL), _row_spec(tm, D_MODEL),
                  _weight_spec((D_MODEL, D_MODEL)), _full_spec((1, D_MODEL))],
        out_specs=[_row_spec(tm, D_MODEL), _full_spec((D_MODEL, D_MODEL)), _full_spec((1, D_MODEL)),
                   _full_spec((1, D_MODEL))],
        out_shape=[jax.ShapeDtypeStruct((n, D_MODEL), BF16), jax.ShapeDtypeStruct((D_MODEL, D_MODEL), F32),
                   jax.ShapeDtypeStruct((1, D_MODEL), F32), jax.ShapeDtypeStruct((1, D_MODEL), F32)],
        compiler_params=_cparams(1),
    )(dxo, y, og, w_out, w_post)


def in_proj_bwd_x(dproj, w_in, x, w_pre, dxo, tm=1024, scatter=()):
    n, p = dproj.shape
    tm = min(tm, n)
    steps = n // tm
    ns = len(scatter)
    columns, block = _column_blocks(w_in)

    def body(*refs):
        dp_ref, w_ref, x_ref, wp_ref, g_ref = refs[:5]
        dx_ref, dwp_ref = refs[5 + ns:7 + ns]
        exchange = (refs[5:5 + ns], refs[7 + ns:7 + 2 * ns]) + tuple(refs[7 + 2 * ns:])

        @pl.when(pl.program_id(0) == 0)
        def _():
            dwp_ref[...] = jnp.zeros_like(dwp_ref)
            if ns:
                _scatter_start(*exchange)

        dh = functools.reduce(jnp.add, [_dot(dp_ref[:, cols], block(w_ref, s), _NT)
                                        for s, cols in enumerate(columns)])
        xv = x_ref[...]
        rstd = _rms(xv)
        xn = xv * rstd
        gw = dh * wp_ref[...]
        dwp_ref[...] += jnp.sum(dh * xn, axis=0, keepdims=True)
        dx_ref[...] = rstd * (gw - xn * jnp.mean(gw * xn, axis=-1, keepdims=True)) + g_ref[...]

        if ns:
            @pl.when(pl.program_id(0) == steps - 1)
            def _():
                _scatter_finish(*exchange)

    out = pl.pallas_call(
        body, name=f"in_proj_bwd_x_{p}", grid=(steps,),
        in_specs=[_row_spec(tm, p), _weight_spec(w_in.shape), _row_spec(tm, D_MODEL), _full_spec((1, D_MODEL)),
                  _row_spec(tm, D_MODEL)] + [_ANY] * ns,
        out_specs=[_row_spec(tm, D_MODEL), _full_spec((1, D_MODEL))] + [_ANY] * ns,
        out_shape=[jax.ShapeDtypeStruct((n, D_MODEL), F32), jax.ShapeDtypeStruct((1, D_MODEL), F32)]
        + [jax.ShapeDtypeStruct(a.shape, a.dtype) for a in scatter],
        scratch_shapes=_scatter_sems(ns) if ns else [],
        compiler_params=_cparams(1),
    )(dproj, w_in, x, w_pre, dxo, *scatter)
    return out[0], out[1], out[2:]


def in_proj_bwd_w(h, dproj, tm=1024, as_shards=False, kv=None):
    n, p = dproj.shape
    chunk = p // (4 if p % 4096 == 0 else 3)
    tm = min(tm, n)
    steps = n // tm
    shard = p // N_CHIPS
    n_kv = 0 if kv is None else 3
    kv_from, kv_to = D_MODEL, D_MODEL + 2 * KV_WIDTH

    def body(*refs):
        h_ref, dp_ref = refs[:2]
        dw_ref, db_ref = refs[2 + n_kv:4 + n_kv]
        scratch = refs[4 + n_kv + (kv is not None):]
        acc_scr, sem, staging = scratch[0], scratch[1], scratch[2:]
        i = pl.program_id(0)

        @pl.when(i == 0)
        def _():
            acc_scr[...] = jnp.zeros_like(acc_scr)
            db_ref[...] = jnp.zeros_like(db_ref)

        if kv is not None:
            dk_ref, dv_ref, cs_ref, kv_ref = refs[2], refs[3], refs[4], refs[4 + n_kv]
            made = jnp.concatenate([_rope_transposed(dk_ref[...].T, *_rope_tables(cs_ref[...])), dv_ref[...].T],
                                   axis=1).astype(BF16)
            kv_ref[...] = made

        def columns(c0):
            if kv is None or c0 + chunk <= kv_from or c0 >= kv_to:
                return dp_ref[:, c0:c0 + chunk]
            return jnp.concatenate([dp_ref[:, c0:kv_from], made, dp_ref[:, kv_to:c0 + chunk]], axis=1)

        ht = h_ref[...].T
        for c0 in range(0, p, chunk):
            dp = columns(c0)
            acc_scr[:, c0:c0 + chunk] += jnp.dot(ht, dp, preferred_element_type=F32)
            db_ref[:, c0:c0 + chunk] += jnp.sum(dp.astype(F32), axis=0, keepdims=True)

        @pl.when(i == steps - 1)
        def _():
            if as_shards:
                for s in range(N_CHIPS):
                    staging[0][...] = acc_scr[:, s * shard:(s + 1) * shard].astype(BF16)
                    out = pltpu.make_async_copy(staging[0], dw_ref.at[s], sem)
                    out.start()
                    out.wait()
            else:
                out = pltpu.make_async_copy(acc_scr, dw_ref, sem)
                out.start()
                out.wait()

    dw_shape = jax.ShapeDtypeStruct((N_CHIPS, D_MODEL, shard), BF16) if as_shards else (
        jax.ShapeDtypeStruct((D_MODEL, p), F32))
    in_specs = [_row_spec(tm, D_MODEL), _row_spec(tm, p)]
    out_specs = [_ANY, _full_spec((1, p))]
    out_shape = [dw_shape, jax.ShapeDtypeStruct((1, p), F32)]
    if kv is not None:
        columns_t = pl.BlockSpec((KV_WIDTH, tm), lambda i: (0, i))
        in_specs += [columns_t, columns_t, _row_spec(tm, ANGLE_COLS)]
        out_specs.append(pl.BlockSpec((tm, kv_to - kv_from), lambda i: (i, kv_from // (kv_to - kv_from))))
        out_shape.append(jax.ShapeDtypeStruct(dproj.shape, dproj.dtype))
    return pl.pallas_call(
        body, name=f"in_proj_bwd_w_{p}", grid=(steps,),
        in_specs=in_specs, out_specs=out_specs, out_shape=out_shape,
        scratch_shapes=[pltpu.VMEM((D_MODEL, p), F32), pltpu.SemaphoreType.DMA]
        + ([pltpu.VMEM((D_MODEL, shard), BF16)] if as_shards else []),
        input_output_aliases={1: 2} if kv is not None else {},
        compiler_params=_cparams(1),
    )(h, dproj, *(kv or ()))


PAIRS = GROUP // 2
GROUP_ROWS = PAIRS * ATTN_BLOCK
MASKED = -1e30


def _kv_windows(k_ref, v_ref, i):
    ps = pl.multiple_of(jnp.maximum(i - 1, 0) * ATTN_BLOCK, ATTN_BLOCK)
    cs = pl.multiple_of(i * ATTN_BLOCK, ATTN_BLOCK)
    kw = jnp.concatenate([k_ref[pl.ds(ps, ATTN_BLOCK), :], k_ref[pl.ds(cs, ATTN_BLOCK), :]], axis=0)
    vw = jnp.concatenate([v_ref[pl.ds(ps, ATTN_BLOCK), :], v_ref[pl.ds(cs, ATTN_BLOCK), :]], axis=0)
    return kw.astype(F32).T, vw.astype(F32).T, ps, cs


def _low_rows(shape):
    return lax.broadcasted_iota(jnp.int32, shape, 0) < HEAD_DIM


def _spread(w, kvh):
    low = _low_rows(w.shape)
    swapped = pltpu.roll(w, HEAD_DIM, 0)
    if kvh == 0:
        return jnp.where(low, w, 0.0), jnp.where(low, 0.0, swapped)
    return jnp.where(low, swapped, 0.0), jnp.where(low, 0.0, w)


def _unspread(d_a, d_b, kvh):
    low = _low_rows(d_a.shape)
    if kvh == 0:
        return jnp.where(low, d_a + pltpu.roll(d_b, HEAD_DIM, 0), 0.0)
    return jnp.where(low, 0.0, pltpu.roll(d_a, HEAD_DIM, 0) + d_b)


def _stack_pairs(ref, kvh):
    return jnp.concatenate([ref[:, (kvh * PAIRS + j) * LANES:(kvh * PAIRS + j + 1) * LANES] for j in range(PAIRS)],
                           axis=0)


def _fill_bias(bias_scr):
    shape = (GROUP_ROWS, 2 * ATTN_BLOCK)
    r = lax.broadcasted_iota(jnp.int32, shape, 0) % ATTN_BLOCK
    c = lax.broadcasted_iota(jnp.int32, shape, 1)
    in_cur = (c >= ATTN_BLOCK) & ((c - ATTN_BLOCK) <= r)
    in_prev = (c < ATTN_BLOCK) & (c > r)
    bias_scr[0] = jnp.where(in_cur, 0.0, MASKED)
    bias_scr[1] = jnp.where(in_cur | in_prev, 0.0, MASKED)
    bias_scr[2] = jnp.where(c == r, 1.0, 0.0)


N_BIAS_TABLES = 3


def _sink_table(sinks):
    t = jnp.transpose(sinks.reshape(N_KV_HEADS, PAIRS, 2), (0, 2, 1))
    return jnp.broadcast_to(t[:, :, :, None, None], (N_KV_HEADS, 2, PAIRS, ATTN_BLOCK, LANES)).reshape(
        N_KV_HEADS, 2, GROUP_ROWS, LANES)


def attn_fwd(q, k, v, z, sink_tab, batch, seq, gather=()):
    nb = seq // ATTN_BLOCK
    ng = len(gather)

    def body(*refs):
        q_ref, k_ref, v_ref, z_ref, s_ref = refs[:5]
        og_ref, bias_scr = refs[5 + ng], refs[6 + 2 * ng]
        exchange = (refs[5:5 + ng], refs[6 + ng:6 + 2 * ng]) + tuple(refs[7 + 2 * ng:])
        b, i = pl.program_id(0), pl.program_id(1)

        @pl.when((b == 0) & (i == 0))
        def _():
            _fill_bias(bias_scr)
            if ng:
                _gather_start(*exchange)

        kw, vw, _, _ = _kv_windows(k_ref, v_ref, i)
        bias, at_sink = bias_scr[jnp.minimum(i, 1)], bias_scr[2] > 0.5
        for kvh in range(N_KV_HEADS):
            k_a, k_b = _spread(kw, kvh)
            v_a, v_b = _spread(vw, kvh)
            og = _attn_group(_stack_pairs(q_ref, kvh), k_a, v_a, k_b, v_b, _stack_pairs(z_ref, kvh),
                             s_ref[kvh, 0], s_ref[kvh, 1], bias, at_sink)
            for j in range(PAIRS):
                og_ref[:, (kvh * PAIRS + j) * LANES:(kvh * PAIRS + j + 1) * LANES] = (
                    og[j * ATTN_BLOCK:(j + 1) * ATTN_BLOCK].astype(BF16))

        if ng:
            @pl.when((b == batch - 1) & (i == nb - 1))
            def _():
                _gather_finish(*exchange)

    blk = lambda w: pl.BlockSpec((ATTN_BLOCK, w), lambda b, i: (b * nb + i, 0))
    seq_spec = pl.BlockSpec((seq, KV_WIDTH), lambda b, i: (b, 0))
    out = pl.pallas_call(
        body, name="attn_fwd", grid=(batch, nb),
        in_specs=[blk(D_MODEL), seq_spec, seq_spec, blk(D_MODEL), _full_spec(sink_tab.shape)] + [_ANY] * ng,
        out_specs=[blk(D_MODEL)] + [_ANY] * ng,
        out_shape=[jax.ShapeDtypeStruct((batch * seq, D_MODEL), BF16)]
        + [jax.ShapeDtypeStruct((N_CHIPS,) + a.shape, a.dtype) for a in gather],
        scratch_shapes=[pltpu.VMEM((N_BIAS_TABLES, GROUP_ROWS, 2 * ATTN_BLOCK), F32)] + (_gather_sems(ng) if ng else []),
        compiler_params=_cparams(2),
    )(q, k, v, z, sink_tab, *gather)
    return out[0], out[1:]


def attn_bwd(q, k, v, z, sink_tab, dog, angles, batch, seq, scatter=()):
    nb = seq // ATTN_BLOCK
    ns = len(scatter)

    def body(*refs):
        q_ref, k_ref, v_ref, z_ref, s_ref, g_ref, cs_ref = refs[:7]
        dp_ref, dk_ref, dv_ref, ds_ref = refs[7 + ns:11 + ns]
        bias_scr = refs[11 + 2 * ns]
        exchange = (refs[7:7 + ns], refs[11 + ns:11 + 2 * ns]) + tuple(refs[12 + 2 * ns:])
        b, i = pl.program_id(0), pl.program_id(1)

        @pl.when((b == 0) & (i == 0))
        def _():
            _fill_bias(bias_scr)
            ds_ref[...] = jnp.zeros_like(ds_ref)
            if ns:
                _scatter_start(*exchange)

        @pl.when(i == 0)
        def _():
            dk_ref[...] = jnp.zeros_like(dk_ref)
            dv_ref[...] = jnp.zeros_like(dv_ref)

        kw, vw, ps, cs = _kv_windows(k_ref, v_ref, i)
        bias = bias_scr[jnp.minimum(i, 1)]
        tabs = _rope_tables(cs_ref[...])
        dkw = jnp.zeros_like(kw)
        dvw = jnp.zeros_like(vw)
        for kvh in range(N_KV_HEADS):
            k_a, k_b = _spread(kw, kvh)
            v_a, v_b = _spread(vw, kvh)
            _, vjp = jax.vjp(functools.partial(_attn_group, bias=bias), _stack_pairs(q_ref, kvh).astype(F32),
                             k_a, v_a, k_b, v_b, _stack_pairs(z_ref, kvh), s_ref[kvh, 0], s_ref[kvh, 1])
            dqs, dk_a, dv_a, dk_b, dv_b, dzs, ds_a, ds_b = vjp(_stack_pairs(g_ref, kvh).astype(F32))
            dkw = dkw + _unspread(dk_a, dk_b, kvh)
            dvw = dvw + _unspread(dv_a, dv_b, kvh)
            ds_ref[kvh, 0] += jnp.sum(ds_a.reshape(PAIRS, ATTN_BLOCK, LANES), axis=1)
            ds_ref[kvh, 1] += jnp.sum(ds_b.reshape(PAIRS, ATTN_BLOCK, LANES), axis=1)
            for j in range(PAIRS):
                rows = slice(j * ATTN_BLOCK, (j + 1) * ATTN_BLOCK)
                col = (kvh * PAIRS + j) * LANES
                dp_ref[:, col:col + LANES] = _rope_transposed(dqs[rows] * (HEAD_DIM ** -0.5), *tabs).astype(BF16)
                zc = D_MODEL + 2 * KV_WIDTH + col
                dp_ref[:, zc:zc + LANES] = dzs[rows].astype(BF16)
        dp_ref[:, D_MODEL:D_MODEL + 2 * KV_WIDTH] = jnp.zeros((ATTN_BLOCK, 2 * KV_WIDTH), BF16)
        dk_ref[:, pl.ds(ps, ATTN_BLOCK)] += dkw[:, :ATTN_BLOCK]
        dk_ref[:, pl.ds(cs, ATTN_BLOCK)] += dkw[:, ATTN_BLOCK:]
        dv_ref[:, pl.ds(ps, ATTN_BLOCK)] += dvw[:, :ATTN_BLOCK]
        dv_ref[:, pl.ds(cs, ATTN_BLOCK)] += dvw[:, ATTN_BLOCK:]

        if ns:
            @pl.when((b == batch - 1) & (i == nb - 1))
            def _():
                _scatter_finish(*exchange)

    blk = lambda w: pl.BlockSpec((ATTN_BLOCK, w), lambda b, i: (b * nb + i, 0))
    seq_spec = pl.BlockSpec((seq, KV_WIDTH), lambda b, i: (b, 0))
    seq_spec_t = pl.BlockSpec((KV_WIDTH, seq), lambda b, i: (0, b))
    n = batch * seq
    ds_shape = (N_KV_HEADS, 2, PAIRS, LANES)
    out = pl.pallas_call(
        body, name="attn_bwd", grid=(batch, nb),
        in_specs=[blk(D_MODEL), seq_spec, seq_spec, blk(D_MODEL), _full_spec(sink_tab.shape), blk(D_MODEL)]
        + [blk(ANGLE_COLS)] + [_ANY] * ns,
        out_specs=[blk(ATTN_IN), seq_spec_t, seq_spec_t, _full_spec(ds_shape)] + [_ANY] * ns,
        out_shape=[jax.ShapeDtypeStruct((n, ATTN_IN), BF16), jax.ShapeDtypeStruct((KV_WIDTH, n), F32),
                   jax.ShapeDtypeStruct((KV_WIDTH, n), F32), jax.ShapeDtypeStruct(ds_shape, F32)]
        + [jax.ShapeDtypeStruct(a.shape, a.dtype) for a in scatter],
        scratch_shapes=[pltpu.VMEM((N_BIAS_TABLES, GROUP_ROWS, 2 * ATTN_BLOCK), F32)] + (_scatter_sems(ns) if ns else []),
        compiler_params=_cparams(2),
    )(q, k, v, z, sink_tab, dog, angles, *scatter)
    return out[0], out[1], out[2], out[3], out[4:]


def rec_fwd(proj, lb_logits, gnorm_w, batch, seq):
    nblk = seq // REC_BLOCK

    def body(p_ref, lb_ref, gw_ref, og_ref, st_ref, safe_ref, s_scr):
        @pl.when(pl.program_id(1) == 0)
        def _():
            s_scr[...] = jnp.zeros_like(s_scr)

        S = s_scr[...]
        st_ref[0] = S
        qr, fr, v, z = (p_ref[:, part * D_MODEL:(part + 1) * D_MODEL] for part in range(4))
        lf, k = forget_gate(fr, lb_ref[1:2, :] - lb_ref[0:1, :])
        q, b = silu(qr), cumsum_rows(lf)
        safe = jnp.min(_rec_margin(b)) >= -SAFE_RANGE

        gate = gw_ref[...] * silu(z)
        safe_ref[0] = jnp.full((REC_HEADS, LANES), safe.astype(F32))

        def store(o, S_new):
            og_ref[...] = (o * lax.rsqrt(head_sum(o * o) * (1.0 / REC_DIM) + NORM_EPS) * gate).astype(BF16)
            s_scr[...] = S_new

        @pl.when(safe)
        def _():
            store(*_rec_cores_fast(q, k, v, b, S))

        @pl.when(jnp.logical_not(safe))
        def _():
            outs = [_rec_core_slow(*args) for args in zip(*(_heads(t) for t in (q, k, v, b, S)))]
            store(*(jnp.concatenate(parts, axis=1) for parts in zip(*outs)))

    blk = lambda w: pl.BlockSpec((REC_BLOCK, w), lambda b, j: (b * nblk + j, 0))
    st_spec = pl.BlockSpec((1, REC_DIM, D_MODEL), lambda b, j: (b * nblk + j, 0, 0))
    safe_spec = pl.BlockSpec((1, REC_HEADS, LANES), lambda b, j: (b * nblk + j, 0, 0))
    return pl.pallas_call(
        body, name="rec_fwd", grid=(batch, nblk),
        in_specs=[blk(REC_IN), _full_spec((2, D_MODEL)), _full_spec((1, D_MODEL))],
        out_specs=[blk(D_MODEL), st_spec, safe_spec],
        out_shape=[jax.ShapeDtypeStruct((batch * seq, D_MODEL), BF16),
                   jax.ShapeDtypeStruct((batch * nblk, REC_DIM, D_MODEL), F32),
                   jax.ShapeDtypeStruct((batch * nblk, REC_HEADS, LANES), F32)],
        scratch_shapes=[pltpu.VMEM((REC_DIM, D_MODEL), F32)],
        compiler_params=_cparams(2),
    )(proj, lb_logits, jnp.tile(gnorm_w, (1, REC_HEADS)))


def rec_bwd(proj, states, safe, lb_logits, gnorm_w, dog, batch, seq):
    nblk = seq // REC_BLOCK

    def body(p_ref, st_ref, safe_ref, lb_ref, gw_ref, g_ref, dp_ref, dlb_ref, dgw_ref, ds_scr):
        @pl.when((pl.program_id(0) == 0) & (pl.program_id(1) == 0))
        def _():
            dlb_ref[...] = jnp.zeros_like(dlb_ref)
            dgw_ref[...] = jnp.zeros_like(dgw_ref)

        @pl.when(pl.program_id(1) == 0)
        def _():
            ds_scr[...] = jnp.zeros_like(ds_scr)

        def load():
            primals = tuple(p_ref[:, part * D_MODEL:(part + 1) * D_MODEL] for part in range(4)) + (
                st_ref[0], lb_ref[0:1, :], lb_ref[1:2, :], gw_ref[...])
            return primals, (g_ref[...].astype(F32), ds_scr[...])

        def store(dqr, dfr, dv, dz, dS, dl0, dl1, dgw):
            for part, val in enumerate((dqr, dfr, dv, dz)):
                dp_ref[:, part * D_MODEL:(part + 1) * D_MODEL] = val.astype(BF16)
            ds_scr[...] = dS
            dlb_ref[0:1, :] += dl0
            dlb_ref[1:2, :] += dl1
            dgw_ref[...] += functools.reduce(jnp.add, _heads(dgw))

        fast = jnp.max(safe_ref[0]) > 0.5

        @pl.when(fast)
        def _():
            primals, cotangents = load()
            store(*jax.vjp(_rec_block_fast, *primals)[1](cotangents))

        @pl.when(jnp.logical_not(fast))
        def _():
            primals, cotangents = load()
            outs = [jax.vjp(functools.partial(_rec_head, _rec_core_slow), *args)[1](cts)
                    for args, cts in zip(zip(*(_heads(t) for t in primals)), zip(*(_heads(t) for t in cotangents)))]
            store(*(jnp.concatenate(parts, axis=1) for parts in zip(*outs)))

    blk = lambda w: pl.BlockSpec((REC_BLOCK, w), lambda b, j: (b * nblk + nblk - 1 - j, 0))
    st_spec = pl.BlockSpec((1, REC_DIM, D_MODEL), lambda b, j: (b * nblk + nblk - 1 - j, 0, 0))
    safe_spec = pl.BlockSpec((1, REC_HEADS, LANES), lambda b, j: (b * nblk + nblk - 1 - j, 0, 0))
    return pl.pallas_call(
        body, name="rec_bwd", grid=(batch, nblk),
        in_specs=[blk(REC_IN), st_spec, safe_spec, _full_spec((2, D_MODEL)), _full_spec((1, D_MODEL)),
                  blk(D_MODEL)],
        out_specs=[blk(REC_IN), _full_spec((2, D_MODEL)), _full_spec((1, REC_DIM))],
        out_shape=[jax.ShapeDtypeStruct((batch * seq, REC_IN), BF16), jax.ShapeDtypeStruct((2, D_MODEL), F32),
                   jax.ShapeDtypeStruct((1, REC_DIM), F32)],
        scratch_shapes=[pltpu.VMEM((REC_DIM, D_MODEL), F32)],
        compiler_params=_cparams(2),
    )(proj, states, safe, lb_logits, jnp.tile(gnorm_w, (1, REC_HEADS)), dog)


_ANY = pl.BlockSpec(memory_space=pl.ANY)


def _chip_peers():
    x, y, c = lax.axis_index("x"), lax.axis_index("y"), lax.axis_index("c")
    peers = []
    for fx, fy in ((1, 0), (0, 1), (1, 1)):
        px, py = (1 - x if fx else x), (1 - y if fy else y)
        peers.append(((px, py, c), 2 * px + py))
    return 2 * x + y, peers


def _remote(src, dst, send_sem, recv_sem, device):
    return pltpu.make_async_remote_copy(src_ref=src, dst_ref=dst, send_sem=send_sem, recv_sem=recv_sem,
                                        device_id=device, device_id_type=MESH)


N_FLIPS = N_CHIPS - 1


def _scatter_sems(n):
    return [pltpu.SemaphoreType.DMA((n * N_FLIPS,)), pltpu.SemaphoreType.DMA((n * N_FLIPS,)),
            pltpu.SemaphoreType.DMA((n,))]


def _scatter_copies(ins, outs, send_sems, recv_sems, local_sems, starting):
    me, peers = _chip_peers()
    local = [pltpu.make_async_copy(ins[k].at[me], outs[k].at[me], local_sems.at[k]) for k in range(len(ins))]
    sends, arrivals = [], []
    for k in range(len(ins)):
        for j, (device, idx) in enumerate(peers):
            sems = (send_sems.at[k * N_FLIPS + j], recv_sems.at[k * N_FLIPS + j], device)
            sends.append(_remote(ins[k].at[idx], outs[k].at[me], *sems))
            if not starting:
                arrivals.append(_remote(ins[k].at[me], outs[k].at[idx], *sems))
    return local, sends, arrivals


def _scatter_start(*refs):
    local, sends, _ = _scatter_copies(*refs, starting=True)
    for cp in local + sends:
        cp.start()


def _scatter_finish(*refs):
    local, sends, arrivals = _scatter_copies(*refs, starting=False)
    for cp in arrivals:
        cp.wait_recv()
    for cp in sends:
        cp.wait_send()
    for cp in local:
        cp.wait()


def _gather_sems(n):
    return [pltpu.SemaphoreType.DMA((n * N_FLIPS,)) for _ in range(4)] + [pltpu.SemaphoreType.DMA((n,))]


def _gather_copies(ins, outs, send_sems, recv_sems, pass_send_sems, pass_recv_sems, local_sems, starting):
    me, peers = _chip_peers()
    c = lax.axis_index("c")
    sibling = (lax.axis_index("x"), lax.axis_index("y"), 1 - c)
    local = [pltpu.make_async_copy(ins[k], outs[k].at[me], local_sems.at[k]) for k in range(len(ins))]
    sends, arrivals, passes, pass_arrivals = [], [], [], []
    for k in range(len(ins)):
        half = ins[k].shape[0] // 2
        mine, other = pl.ds(c * half, half), pl.ds((1 - c) * half, half)
        for j, (device, idx) in enumerate(peers):
            s = k * N_FLIPS + j
            sends.append(_remote(ins[k].at[mine], outs[k].at[me].at[mine], send_sems.at[s], recv_sems.at[s], device))
            if starting:
                continue
            arrived = outs[k].at[idx].at[mine]
            arrivals.append(_remote(ins[k].at[mine], arrived, send_sems.at[s], recv_sems.at[s], device))
            passes.append(_remote(arrived, arrived, pass_send_sems.at[s], pass_recv_sems.at[s], sibling))
            passed = outs[k].at[idx].at[other]
            pass_arrivals.append(_remote(passed, passed, pass_send_sems.at[s], pass_recv_sems.at[s], sibling))
    return local, sends, arrivals, passes, pass_arrivals


def _gather_start(*refs):
    local, sends, _, _, _ = _gather_copies(*refs, starting=True)
    for cp in local + sends:
        cp.start()


def _gather_finish(*refs):
    local, sends, arrivals, passes, pass_arrivals = _gather_copies(*refs, starting=False)
    for arrival, onward in zip(arrivals, passes):
        arrival.wait_recv()
        onward.start()
    for cp in pass_arrivals:
        cp.wait_recv()
    for cp in sends + passes:
        cp.wait_send()
    for cp in local:
        cp.wait()


def chip_gather(arrays):
    n = len(arrays)

    def body(*refs):
        _gather_start(refs[:n], refs[n:2 * n], *refs[2 * n:])
        _gather_finish(refs[:n], refs[n:2 * n], *refs[2 * n:])

    return pl.pallas_call(
        body, name="chip_gather", in_specs=[_ANY] * n, out_specs=[_ANY] * n,
        out_shape=[jax.ShapeDtypeStruct((N_CHIPS,) + a.shape, a.dtype) for a in arrays],
        scratch_shapes=_gather_sems(n),
    )(*arrays)


def sibling_exchange(arrays):
    n = len(arrays)

    def body(*refs):
        ins, outs = refs[:n], refs[n:2 * n]
        send_sems, recv_sems = refs[2 * n:]
        sibling = (lax.axis_index("x"), lax.axis_index("y"), 1 - lax.axis_index("c"))
        copies = [pltpu.make_async_remote_copy(src_ref=ins[k], dst_ref=outs[k], send_sem=send_sems.at[k],
                                               recv_sem=recv_sems.at[k], device_id=sibling, device_id_type=MESH)
                  for k in range(n)]
        for cp in copies:
            cp.start()
        for cp in copies:
            cp.wait()

    return pl.pallas_call(
        body, name="sibling_exchange", in_specs=[_ANY] * n, out_specs=[_ANY] * n,
        out_shape=[jax.ShapeDtypeStruct(a.shape, a.dtype) for a in arrays],
        scratch_shapes=[pltpu.SemaphoreType.DMA((n,)), pltpu.SemaphoreType.DMA((n,))],
    )(*arrays)


def all_gather_small(vec):
    def body(v_ref, out_ref, send_sems, recv_sems, local_sem):
        x, y, c = lax.axis_index("x"), lax.axis_index("y"), lax.axis_index("c")
        me = 4 * x + 2 * y + c
        local = pltpu.make_async_copy(v_ref, out_ref.at[me], local_sem)
        local.start()
        sends, recvs = [], []
        for j in range(1, N_DEV):
            px = jnp.where(j & 4, 1 - x, x)
            py = jnp.where(j & 2, 1 - y, y)
            pc = jnp.where(j & 1, 1 - c, c)
            common = dict(send_sem=send_sems.at[j - 1], recv_sem=recv_sems.at[j - 1], device_id=(px, py, pc),
                          device_id_type=MESH)
            sends.append(pltpu.make_async_remote_copy(src_ref=v_ref, dst_ref=out_ref.at[me], **common))
            recvs.append(pltpu.make_async_remote_copy(src_ref=v_ref, dst_ref=out_ref.at[4 * px + 2 * py + pc],
                                                      **common))
        for cp in sends:
            cp.start()
        for cp in recvs:
            cp.wait_recv()
        for cp in sends:
            cp.wait_send()
        local.wait()

    return pl.pallas_call(
        body, name="all_gather_small", in_specs=[_ANY], out_specs=_ANY,
        out_shape=jax.ShapeDtypeStruct((N_DEV,) + vec.shape, vec.dtype),
        scratch_shapes=[pltpu.SemaphoreType.DMA((N_DEV - 1,)), pltpu.SemaphoreType.DMA((N_DEV - 1,)),
                        pltpu.SemaphoreType.DMA],
    )(vec)


def sum_slots(stacked, tm=256):
    s, r, c = stacked.shape
    tm = min(tm, r)

    def body(in_ref, out_ref):
        acc = in_ref[0].astype(F32)
        for t in range(1, s):
            acc = acc + in_ref[t].astype(F32)
        out_ref[...] = acc

    return pl.pallas_call(
        body, name=f"sum_slots_{s}_{r}_{c}", grid=(r // tm,),
        in_specs=[pl.BlockSpec((s, tm, c), lambda i: (0, i, 0))], out_specs=_row_spec(tm, c),
        out_shape=jax.ShapeDtypeStruct((r, c), F32), compiler_params=_cparams(1),
    )(stacked)


def adamw(w, m, v, g_a, g_b=None, tm=256):
    r, c = w.shape
    tm = min(tm, r)
    two = g_b is not None

    def body(*refs):
        w_ref, m_ref, v_ref, ga_ref = refs[:4]
        g_ref, d_ref, nm_ref, nv_ref = refs[-4:]
        g = ga_ref[...] + refs[4][...] if two else ga_ref[...]
        nm = ADAM_B1 * m_ref[...] + (1.0 - ADAM_B1) * g
        nv = ADAM_B2 * v_ref[...] + (1.0 - ADAM_B2) * (g * g)
        m_hat = nm / (1.0 - ADAM_B1 ** ADAM_STEP)
        v_hat = nv / (1.0 - ADAM_B2 ** ADAM_STEP)
        g_ref[...] = g
        d_ref[...] = -ADAM_LR * (m_hat / (jnp.sqrt(v_hat) + ADAM_EPS) + ADAM_WD * w_ref[...])
        nm_ref[...] = nm
        nv_ref[...] = nv

    args = [w, m, v, g_a] + ([g_b] if two else [])
    return pl.pallas_call(
        body, name=f"adamw_{r}_{c}", grid=(r // tm,),
        in_specs=[_row_spec(tm, c)] * len(args), out_specs=[_row_spec(tm, c)] * 4,
        out_shape=[jax.ShapeDtypeStruct((r, c), F32)] * 4, compiler_params=_cparams(1),
    )(*args)


_SMALL = (("pre_norm_w", (2, D_MODEL)), ("post_norm_w", (2, D_MODEL)), ("attn_b_in", (1, ATTN_IN)),
          ("attn_sinks", (1, N_HEADS)), ("attn_b_out", (1, D_MODEL)), ("rec_lb_logits", (2, D_MODEL)),
          ("rec_gnorm_w", (1, REC_DIM)))
_SMALL_ROWS = 16


def _pack_small(parts, last_row=None):
    rows = []
    for (name, shape) in _SMALL:
        flat = parts[name].reshape(-1)
        pad = -flat.shape[0] % D_MODEL
        rows.append(jnp.pad(flat, (0, pad)).reshape(-1, D_MODEL))
    used = sum(r.shape[0] for r in rows)
    rows.append(jnp.zeros((_SMALL_ROWS - 1 - used, D_MODEL), F32))
    rows.append(jnp.zeros((1, D_MODEL), F32) if last_row is None else last_row)
    return jnp.concatenate(rows, axis=0)


def _unpack_small(packed):
    out, row = {}, 0
    for (name, shape) in _SMALL:
        size = shape[0] * shape[1]
        nrows = -(-size // D_MODEL)
        out[name] = packed[row:row + nrows].reshape(-1)[:size].reshape(shape)
        row += nrows
    return out


_CARRIED = ("rec_w_in", "rec_w_out", "attn_w_out")


_LATE = ("attn_w_out", "rec_w_in", "rec_w_out")


def local_step(x, positions, pre_norm_w, post_norm_w, attn_w_in, attn_b_in, attn_sinks, attn_w_out, attn_b_out,
               rec_w_in, rec_lb_logits, rec_gnorm_w, rec_w_out, loss_target, distributed=False):
    batch, seq, _ = x.shape
    n = batch * seq
    x0 = x.reshape(n, D_MODEL)
    angles = _rope_angles(positions)
    pre0, pre1 = pre_norm_w[0:1], pre_norm_w[1:2]
    post0, post1 = post_norm_w[0:1], post_norm_w[1:2]

    h0, q, k, v, z = attn_in_proj(x0, pre0, attn_w_in, attn_b_in, angles)
    sink_tab = _sink_table(attn_sinks)
    late = (attn_w_out, rec_w_in, rec_w_out)
    og0, gathered = attn_fwd(q, k, v, z, sink_tab, batch, seq, gather=late if distributed else ())
    if distributed:
        attn_w_out, rec_w_in, rec_w_out = (g if name == "rec_w_in" else _whole_from_shards(name, g)
                                           for name, g in zip(_LATE, gathered))
    y0, x1 = out_proj(og0, attn_w_out, attn_b_out, x0, post0)

    h1, proj1 = rec_in_proj(x1, pre1, rec_w_in)
    og1, states, safe = rec_fwd(proj1, rec_lb_logits, rec_gnorm_w, batch, seq)
    dx2, loss_vec, dog1, d_rec_w_out, d_post1 = out_proj_loss_bwd(og1, rec_w_out, x1, post1,
                                                                   loss_target.reshape(n, D_MODEL))
    dproj1, d_lb, d_gnorm = rec_bwd(proj1, states, safe, rec_lb_logits, rec_gnorm_w, dog1, batch, seq)
    dx1, d_pre1, _ = in_proj_bwd_x(dproj1, rec_w_in, x1, pre1, dx2)
    d_rec_w_in, _ = in_proj_bwd_w(h1, dproj1, as_shards=distributed)

    dog0, d_attn_w_out, d_attn_b_out, d_post0 = out_proj_bwd(dx1, y0, og0, attn_w_out, post0)
    ready = dict(rec_w_out=d_rec_w_out, attn_w_out=d_attn_w_out)
    outgoing = [d_rec_w_in if name == "rec_w_in" else _shards_from_whole(name, ready[name]).astype(BF16)
                for name in _CARRIED] if distributed else []
    dproj0, dk, dv, d_sink_tab, arrived = attn_bwd(q, k, v, z, sink_tab, dog0, angles, batch, seq, scatter=outgoing)
    d_sinks = jnp.transpose(jnp.sum(d_sink_tab, axis=-1), (0, 2, 1)).reshape(1, N_HEADS)
    d_attn_w_in, d_attn_b_in, dproj0 = in_proj_bwd_w(h0, dproj0, kv=(dk, dv, angles))
    last = [_shards_from_whole("attn_w_in", d_attn_w_in).astype(BF16)] if distributed else []
    dx0, d_pre0, arrived_last = in_proj_bwd_x(dproj0, attn_w_in, x0, pre0, dx1, scatter=last)

    grads = dict(
        pre_norm_w=jnp.concatenate([d_pre0, d_pre1], axis=0), post_norm_w=jnp.concatenate([d_post0, d_post1], axis=0),
        attn_w_in=d_attn_w_in, attn_b_in=d_attn_b_in, attn_sinks=d_sinks, attn_w_out=d_attn_w_out,
        attn_b_out=d_attn_b_out, rec_w_in=d_rec_w_in, rec_lb_logits=d_lb, rec_gnorm_w=d_gnorm,
        rec_w_out=d_rec_w_out)
    parts = dict(zip(_CARRIED + ("attn_w_in",), tuple(arrived) + tuple(arrived_last)))
    return loss_vec, dx0.reshape(batch, seq, D_MODEL), grads, parts


_BIG = ("attn_w_in", "attn_w_out", "rec_w_in", "rec_w_out")
_COLUMN_SHARDED = ("attn_w_in", "rec_w_in")
_ORDER = ("pre_norm_w", "post_norm_w", "attn_w_in", "attn_b_in", "attn_sinks", "attn_w_out", "attn_b_out",
          "rec_w_in", "rec_lb_logits", "rec_gnorm_w", "rec_w_out")


def _whole_from_shards(name, stacked):
    if name in _COLUMN_SHARDED:
        return jnp.transpose(stacked, (1, 0, 2)).reshape(stacked.shape[1], -1)
    return stacked.reshape(-1, stacked.shape[2])


def _shards_from_whole(name, whole):
    if name in _COLUMN_SHARDED:
        return jnp.transpose(whole.reshape(whole.shape[0], N_CHIPS, -1), (1, 0, 2))
    return whole.reshape(N_CHIPS, -1, whole.shape[1])


def kernel(x, positions, pre_norm_w, post_norm_w, attn_w_in, attn_b_in, attn_sinks, attn_w_out, attn_b_out, rec_w_in, rec_lb_logits, rec_gnorm_w, rec_w_out, loss_target, m_pre_norm_w, m_post_norm_w, m_attn_w_in, m_attn_b_in, m_attn_sinks, m_attn_w_out, m_attn_b_out, m_rec_w_in, m_rec_lb_logits, m_rec_gnorm_w, m_rec_w_out, v_pre_norm_w, v_post_norm_w, v_attn_w_in, v_attn_b_in, v_attn_sinks, v_attn_w_out, v_attn_b_out, v_rec_w_in, v_rec_lb_logits, v_rec_gnorm_w, v_rec_w_out):
    w = dict(pre_norm_w=pre_norm_w, post_norm_w=post_norm_w, attn_w_in=attn_w_in, attn_b_in=attn_b_in,
             attn_sinks=attn_sinks, attn_w_out=attn_w_out, attn_b_out=attn_b_out, rec_w_in=rec_w_in,
             rec_lb_logits=rec_lb_logits, rec_gnorm_w=rec_gnorm_w, rec_w_out=rec_w_out)
    m = dict(pre_norm_w=m_pre_norm_w, post_norm_w=m_post_norm_w, attn_w_in=m_attn_w_in, attn_b_in=m_attn_b_in,
             attn_sinks=m_attn_sinks, attn_w_out=m_attn_w_out, attn_b_out=m_attn_b_out, rec_w_in=m_rec_w_in,
             rec_lb_logits=m_rec_lb_logits, rec_gnorm_w=m_rec_gnorm_w, rec_w_out=m_rec_w_out)
    v = dict(pre_norm_w=v_pre_norm_w, post_norm_w=v_post_norm_w, attn_w_in=v_attn_w_in, attn_b_in=v_attn_b_in,
             attn_sinks=v_attn_sinks, attn_w_out=v_attn_w_out, attn_b_out=v_attn_b_out, rec_w_in=v_rec_w_in,
             rec_lb_logits=v_rec_lb_logits, rec_gnorm_w=v_rec_gnorm_w, rec_w_out=v_rec_w_out)

    shards = {name: w[name][0] for name in _BIG}
    sent = {name: shards[name].astype(BF16) for name in _BIG}
    attn_w_in_whole = _whole_from_shards("attn_w_in", chip_gather([sent["attn_w_in"]])[0])

    loss_vec, grad_x, grads, parts = local_step(
        x, positions, pre_norm_w, post_norm_w, attn_w_in_whole, attn_b_in, attn_sinks, sent["attn_w_out"],
        attn_b_out, sent["rec_w_in"], rec_lb_logits, rec_gnorm_w, sent["rec_w_out"], loss_target, distributed=True)

    plane_sums = [sum_slots(parts[name]) for name in _BIG]
    other_sums = sibling_exchange(plane_sums)
    out_g, out_d, out_m, out_v = {}, {}, {}, {}
    for name, mine, other in zip(_BIG, plane_sums, other_sums):
        g, d, nm, nv = adamw(shards[name], m[name][0], v[name][0], mine, other)
        out_g[name], out_d[name], out_m[name], out_v[name] = g[None], d[None], nm[None], nv[None]

    small_sum = sum_slots(all_gather_small(_pack_small(grads, last_row=loss_vec)))
    loss = jnp.sum(small_sum[_SMALL_ROWS - 1]) * (0.5 / D_MODEL)
    packed = adamw(_pack_small(w), _pack_small(m), _pack_small(v), small_sum)
    for dst, val in zip((out_g, out_d, out_m, out_v), packed):
        dst.update(_unpack_small(val))

    return (loss, grad_x, *[out_g[n] for n in _ORDER], *[out_d[n] for n in _ORDER],
            *[out_m[n] for n in _ORDER], *[out_v[n] for n in _ORDER])
```

```python
import functools

import jax
import jax.numpy as jnp
from jax import lax
from jax.experimental import pallas as pl
from jax.experimental.pallas import tpu as pltpu

F32 = jnp.float32
BF16 = jnp.bfloat16
MESH = pl.DeviceIdType.MESH

D_MODEL = 1024
HEAD_DIM = 64
N_HEADS = 16
N_KV_HEADS = 2
GROUP = N_HEADS // N_KV_HEADS
KV_WIDTH = N_KV_HEADS * HEAD_DIM
ATTN_IN = 2 * D_MODEL + 2 * KV_WIDTH
ATTN_BLOCK = 128
ROPE_THETA = 500000.0
ROPE_DIM = HEAD_DIM // 4
REC_HEADS = 8
REC_DIM = 128
REC_IN = 4 * D_MODEL
REC_BLOCK = 128
DIAG = 8
NORM_EPS = 1e-6
N_CHIPS = 4
N_DEV = 8
LANES = 128

ADAM_LR = 0.001
ADAM_B1 = 0.9
ADAM_B2 = 0.999
ADAM_EPS = 1e-08
ADAM_WD = 0.01
ADAM_STEP = 10

VMEM_LIMIT = 56 * 1024 * 1024


def _cparams(n_axes):
    return pltpu.CompilerParams(dimension_semantics=("arbitrary",) * n_axes, vmem_limit_bytes=VMEM_LIMIT)


def _dot(a, b, contract):
    return lax.dot_general(a.astype(BF16), b.astype(BF16), (contract, ((), ())), preferred_element_type=F32)


_NN = ((1,), (0,))
_NT = ((1,), (1,))
_TN = ((0,), (0,))


@jax.custom_vjp
def mm_nn(a, b):
    return _dot(a, b, _NN)


mm_nn.defvjp(lambda a, b: (_dot(a, b, _NN), (a, b)),
             lambda res, g: (_dot(g, res[1], _NT), _dot(res[0], g, _TN)))


@jax.custom_vjp
def mm_nt(a, b):
    return _dot(a, b, _NT)


mm_nt.defvjp(lambda a, b: (_dot(a, b, _NT), (a, b)),
             lambda res, g: (_dot(g, res[1], _NN), _dot(g, res[0], _TN)))


@jax.custom_vjp
def mm_tn(a, b):
    return _dot(a, b, _TN)


mm_tn.defvjp(lambda a, b: (_dot(a, b, _TN), (a, b)),
             lambda res, g: (_dot(res[1], g, _NT), _dot(res[0], g, _NN)))


def _tri_dot(x, lower):
    n = x.shape[0]
    r = lax.broadcasted_iota(jnp.int32, (n, n), 0)
    c = lax.broadcasted_iota(jnp.int32, (n, n), 1)
    tri = ((c <= r) if lower else (c >= r)).astype(BF16)
    hi = x.astype(BF16)
    rest = x - hi.astype(F32)
    mid = rest.astype(BF16)
    lo = (rest - mid.astype(F32)).astype(BF16)
    dot = lambda p: lax.dot_general(tri, p, (_NN, ((), ())), preferred_element_type=F32)
    return (dot(lo) + dot(mid)) + dot(hi)


@jax.custom_vjp
def cumsum_rows(x):
    return _tri_dot(x, True)


cumsum_rows.defvjp(lambda x: (cumsum_rows(x), None), lambda _, g: (_tri_dot(g, False),))


@functools.partial(jax.custom_vjp, nondiff_argnums=(1,))
def roll_sub(x, d):
    return pltpu.roll(x, d, 1) if d else x


roll_sub.defvjp(lambda x, d: (roll_sub(x, d), None),
                lambda d, _, g: (roll_sub(g, (DIAG - d) % DIAG),))


def sigmoid(x):
    return 1.0 / (1.0 + jnp.exp(-x))


@jax.custom_vjp
def silu(x):
    return x * sigmoid(x)


def _silu_fwd(x):
    s = sigmoid(x)
    return x * s, (x, s)


silu.defvjp(_silu_fwd, lambda res, g: (g * (res[1] * (1.0 + res[0] * (1.0 - res[1]))),))


F32_TINY = 1.17549435e-38


def sigmoid_pair(x):
    e = jnp.exp(-jnp.abs(x))
    r = 1.0 / (1.0 + e)
    er = e * r
    pos = x >= 0.0
    return jnp.where(pos, r, er), jnp.where(pos, er, r)


def _forget_fwd(x, a):
    lb, one_m_lb = sigmoid_pair(a)
    sp, sn = sigmoid_pair(x)
    f = lb + one_m_lb * sp
    k = one_m_lb * sn
    return (jnp.log(jnp.maximum(f, F32_TINY)), k), (sp, sn, f, k, lb, one_m_lb)


def _forget_bwd(res, g):
    sp, sn, f, k, lb, one_m_lb = res
    g_lf, g_k = g
    t = jnp.where(f >= F32_TINY, g_lf / jnp.maximum(f, F32_TINY), 0.0) - g_k
    return (k * sp) * t, jnp.sum(sn * t, axis=0, keepdims=True) * (lb * one_m_lb)


@jax.custom_vjp
def forget_gate(x, a):
    return _forget_fwd(x, a)[0]


forget_gate.defvjp(_forget_fwd, _forget_bwd)


@jax.custom_vjp
def decayed(x, e):
    return (x * jnp.exp(e)).astype(BF16).astype(F32)


def _decayed_fwd(x, e):
    y = decayed(x, e)
    return y, (y, e)


decayed.defvjp(_decayed_fwd, lambda res, g: (g * jnp.exp(res[1]), g * res[0]))


def _row(x, r):
    shape = x.shape

    @jax.custom_vjp
    def take(x):
        return x[r:r + 1, :]

    take.defvjp(lambda x: (x[r:r + 1, :], None),
                lambda _, g: (jnp.where(lax.broadcasted_iota(jnp.int32, shape, 0) == r, g, 0.0),))
    return take(x)


def _rms(x):
    return lax.rsqrt(jnp.mean(x * x, axis=-1, keepdims=True) + NORM_EPS)


def _attn_group(qs, k_a, v_a, k_b, v_b, zs, sink_a, sink_b, bias, at_sink=None):
    def half(kh, vh, sink):
        s = mm_nn(qs, kh) + bias
        if at_sink is None:
            m = jnp.maximum(jnp.max(s, axis=-1, keepdims=True), jnp.max(sink, axis=-1, keepdims=True))
            p = jnp.exp(s - lax.stop_gradient(m))
            own = jnp.sum(jnp.exp(sink - lax.stop_gradient(m)), axis=-1, keepdims=True) * (1.0 / LANES)
            return mm_nt(p * (1.0 / (jnp.sum(p, axis=-1, keepdims=True) + own)), vh)
        s = jnp.where(at_sink, jnp.concatenate([sink, sink], axis=1), s)
        p = jnp.exp(s - jnp.max(s, axis=-1, keepdims=True))
        return mm_nt(jnp.where(at_sink, 0.0, p), vh) * (1.0 / jnp.sum(p, axis=-1, keepdims=True))

    return (half(k_a, v_a, sink_a) + half(k_b, v_b, sink_b)) * silu(zs)


SAFE_RANGE = 80.0


def _rec_front(qr, fr, l0, l1):
    lf, k = forget_gate(fr, l1 - l0)
    return silu(qr), k, lf


def _rec_tail(o, z, gw):
    return o * _rms(o) * gw * silu(z)


def _rec_margin(b):
    R = b.shape[0]
    mid, last = _row(b, R // 2 - 1), _row(b, R - 1)
    return jnp.minimum(mid, last - mid)


def _heads(x):
    w = x.shape[1] // REC_HEADS
    return [x[:, h * w:(h + 1) * w] for h in range(REC_HEADS)]


def _hdot(a, b, contract):
    return jnp.concatenate([_dot(ah, bh, contract) for ah, bh in zip(_heads(a), _heads(b))], axis=1)


@jax.custom_vjp
def hmm_nn(a, b):
    return _hdot(a, b, _NN)


hmm_nn.defvjp(lambda a, b: (_hdot(a, b, _NN), (a, b)),
              lambda res, g: (_hdot(g, res[1], _NT), _hdot(res[0], g, _TN)))


@jax.custom_vjp
def hmm_nt(a, b):
    return _hdot(a, b, _NT)


hmm_nt.defvjp(lambda a, b: (_hdot(a, b, _NT), (a, b)),
              lambda res, g: (_hdot(g, res[1], _NN), _hdot(g, res[0], _TN)))


@jax.custom_vjp
def hmm_tn(a, b):
    return _hdot(a, b, _TN)


hmm_tn.defvjp(lambda a, b: (_hdot(a, b, _TN), (a, b)),
              lambda res, g: (_hdot(res[1], g, _NT), _hdot(res[0], g, _NN)))


def _head_sums(x):
    return jnp.concatenate([jnp.broadcast_to(jnp.sum(xh, axis=-1, keepdims=True), xh.shape) for xh in _heads(x)],
                           axis=1)


@jax.custom_vjp
def head_sum(x):
    return _head_sums(x)


head_sum.defvjp(lambda x: (_head_sums(x), None), lambda _, g: (_head_sums(g),))


def _rec_cores_fast(q, k, v, b, S):
    R = q.shape[0]
    ri = lax.broadcasted_iota(jnp.int32, (R, REC_HEADS * R), 0)
    ci = lax.broadcasted_iota(jnp.int32, (R, REC_HEADS * R), 1) % R
    d = b - _row(b, R // 2 - 1)
    sc = jnp.where(ci < ri, hmm_nt(decayed(q, d), decayed(k, -d)), 0.0)
    o = hmm_nt(q * jnp.exp(b), S) + hmm_nn(sc, v) + head_sum(q * k) * v
    b_last = _row(b, R - 1)
    return o, S * jnp.exp(b_last) + hmm_tn(v, k * jnp.exp(b_last - b))


def _rec_tails(o, z, gw):
    return o * lax.rsqrt(head_sum(o * o) * (1.0 / REC_DIM) + NORM_EPS) * gw * silu(z)


def _rec_block_fast(qr, fr, v, z, S, l0, l1, gw):
    lf, k = forget_gate(fr, l1 - l0)
    o, S_new = _rec_cores_fast(silu(qr), k, v, cumsum_rows(lf), S)
    return _rec_tails(o, z, gw), S_new


def _rec_core_slow(q, k, v, b, S):
    R = q.shape[0]
    rows = lax.broadcasted_iota(jnp.int32, (R, REC_DIM), 0)

    o = mm_nt(q * jnp.exp(jnp.minimum(b, 0.0)), S)

    ri = lax.broadcasted_iota(jnp.int32, (R, R), 0)
    ci = lax.broadcasted_iota(jnp.int32, (R, R), 1)
    sc = jnp.zeros((R, R), F32)
    w = R
    while w > DIAG:
        h = w // 2
        b3 = b.reshape(R // w, w, REC_DIM)
        rin = lax.broadcasted_iota(jnp.int32, (R // w, w, REC_DIM), 1)
        mid = jnp.sum(jnp.where(rin == h - 1, b3, 0.0), axis=1, keepdims=True)
        fac = jnp.exp(jnp.minimum(jnp.where(rin >= h, b3 - mid, mid - b3), 0.0)).reshape(R, REC_DIM)
        upper = (rows % w) >= h
        s_w = mm_nt(jnp.where(upper, q * fac, 0.0), jnp.where(upper, 0.0, k * fac))
        sc = sc + jnp.where((ri // w) == (ci // w), s_w, 0.0)
        w = h
    o = o + mm_nn(sc, v)

    g = R // DIAG
    q3, k3, v3, b3 = (t.reshape(g, DIAG, REC_DIM) for t in (q, k, v, b))
    rin = lax.broadcasted_iota(jnp.int32, (g, DIAG, 1), 1)
    od = jnp.zeros((g, DIAG, REC_DIM), F32)
    for d in range(DIAG):
        e = jnp.exp(jnp.minimum(b3 - roll_sub(b3, d), 0.0))
        sd = jnp.sum(q3 * roll_sub(k3, d) * e, axis=-1, keepdims=True)
        od = od + jnp.where(rin >= d, sd, 0.0) * roll_sub(v3, d)
    o = o + od.reshape(R, REC_DIM)

    b_last = _row(b, R - 1)
    return o, S * jnp.exp(jnp.minimum(b_last, 0.0)) + mm_tn(v, k * jnp.exp(jnp.minimum(b_last - b, 0.0)))


def _rec_head(core, qr, fr, v, z, S, l0, l1, gw):
    q, k, lf = _rec_front(qr, fr, l0, l1)
    o, S_new = core(q, k, v, cumsum_rows(lf), S)
    return _rec_tail(o, z, gw), S_new


ANGLE_COLS = 3 * ROPE_DIM


def _rope_angles(positions):
    half = ROPE_DIM // 2
    inv_freq = ROPE_THETA ** (-(jnp.arange(half, dtype=F32) * 2.0 / ROPE_DIM))
    ang = positions.astype(F32).reshape(-1, 1) * inv_freq
    cs = jnp.concatenate([jnp.cos(ang), jnp.sin(ang)], axis=-1)
    hi = cs.astype(BF16)
    rest = cs - hi.astype(F32)
    mid = rest.astype(BF16)
    return jnp.concatenate([hi, mid, (rest - mid.astype(F32)).astype(BF16)], axis=-1)


def _rope_tables(pieces):
    half = ROPE_DIM // 2
    r = lax.broadcasted_iota(jnp.int32, (ANGLE_COLS, 3 * LANES), 0) % ROPE_DIM
    c = lax.broadcasted_iota(jnp.int32, (ANGLE_COLS, 3 * LANES), 1)
    table, j = c // LANES, c % HEAD_DIM
    angle, low = j % half, j < half
    plus = ((table == 0) & (j < ROPE_DIM) & (r == angle)) | ((table == 1) & (j >= half) & (j < ROPE_DIM)
                                                                & (r == half + angle))
    minus = (table == 2) & low & (r == half + angle)
    pick = jnp.where(plus, 1.0, jnp.where(minus, -1.0, 0.0)).astype(BF16)
    out = jnp.dot(pieces, pick, preferred_element_type=F32)
    lane = lax.broadcasted_iota(jnp.int32, (1, LANES), 1) % HEAD_DIM
    return out[:, :LANES] + jnp.where(lane < ROPE_DIM, 0.0, 1.0), out[:, LANES:2 * LANES], out[:, 2 * LANES:]


def _rope(x, cos_t, sin_a, sin_b):
    half = ROPE_DIM // 2
    return x * cos_t + pltpu.roll(x, half, 1) * sin_a + pltpu.roll(x, LANES - half, 1) * sin_b


def _rope_transposed(g, cos_t, sin_a, sin_b):
    half = ROPE_DIM // 2
    return g * cos_t + pltpu.roll(g * sin_a, LANES - half, 1) + pltpu.roll(g * sin_b, half, 1)


def _row_spec(tm, width):
    return pl.BlockSpec((tm, width), lambda i: (i, 0))


def _weight_spec(shape):
    return pl.BlockSpec(shape, lambda *_: (0,) * len(shape), pipeline_mode=pl.Buffered(1))


def _full_spec(shape):
    return pl.BlockSpec(shape, lambda *_: (0,) * len(shape))


def attn_in_proj(x, w_pre, w_in, b_in, angles, tm=1024, to_bf16=()):
    n = x.shape[0]
    tm = min(tm, n)
    nc = len(to_bf16)

    def body(*refs):
        x_ref, wp_ref, w_ref, b_ref, cs_ref = refs[:5]
        h_ref, q_ref, k_ref, v_ref, z_ref = refs[5 + nc:10 + nc]

        @pl.when(pl.program_id(0) == 0)
        def _():
            for src, dst in zip(refs[5:5 + nc], refs[10 + nc:]):
                dst[...] = src[...].astype(BF16)

        xv = x_ref[...]
        h = (xv * _rms(xv) * wp_ref[...]).astype(BF16)
        h_ref[...] = h
        proj = jnp.dot(h, w_ref[...], preferred_element_type=F32) + b_ref[...]
        tabs = _rope_tables(cs_ref[...])
        for s in range(D_MODEL // LANES):
            sl = slice(s * LANES, (s + 1) * LANES)
            q_ref[:, sl] = _rope(proj[:, sl] * (HEAD_DIM ** -0.5), *tabs).astype(BF16)
        k_ref[...] = _rope(proj[:, D_MODEL:D_MODEL + KV_WIDTH], *tabs).astype(BF16)
        v_ref[...] = proj[:, D_MODEL + KV_WIDTH:D_MODEL + 2 * KV_WIDTH].astype(BF16)
        z_ref[...] = proj[:, D_MODEL + 2 * KV_WIDTH:]

    out = pl.pallas_call(
        body, name="attn_in_proj", grid=(n // tm,),
        in_specs=[_row_spec(tm, D_MODEL), _full_spec((1, D_MODEL)), _weight_spec((D_MODEL, ATTN_IN)),
                  _full_spec((1, ATTN_IN)), _row_spec(tm, ANGLE_COLS)] + [_weight_spec(a.shape) for a in to_bf16],
        out_specs=[_row_spec(tm, D_MODEL), _row_spec(tm, D_MODEL), _row_spec(tm, KV_WIDTH),
                   _row_spec(tm, KV_WIDTH), _row_spec(tm, D_MODEL)] + [_full_spec(a.shape) for a in to_bf16],
        out_shape=[jax.ShapeDtypeStruct((n, D_MODEL), BF16), jax.ShapeDtypeStruct((n, D_MODEL), BF16),
                   jax.ShapeDtypeStruct((n, KV_WIDTH), BF16), jax.ShapeDtypeStruct((n, KV_WIDTH), BF16),
                   jax.ShapeDtypeStruct((n, D_MODEL), F32)] + [jax.ShapeDtypeStruct(a.shape, BF16) for a in to_bf16],
        compiler_params=_cparams(1),
    )(x, w_pre, w_in, b_in, angles, *to_bf16)
    return out[:5], out[5:]


def _column_blocks(w):
    if len(w.shape) == 2:
        return [slice(0, w.shape[1])], lambda ref, s: ref[...]
    width = w.shape[2]
    return [slice(s * width, (s + 1) * width) for s in range(w.shape[0])], lambda ref, s: ref[s]


def rec_in_proj(x, w_pre, w_in, tm=1024):
    n = x.shape[0]
    tm = min(tm, n)
    columns, block = _column_blocks(w_in)

    def body(x_ref, wp_ref, w_ref, h_ref, p_ref):
        xv = x_ref[...]
        h = (xv * _rms(xv) * wp_ref[...]).astype(BF16)
        h_ref[...] = h
        for s, cols in enumerate(columns):
            p_ref[:, cols] = jnp.dot(h, block(w_ref, s), preferred_element_type=F32)

    return pl.pallas_call(
        body, name="rec_in_proj", grid=(n // tm,),
        in_specs=[_row_spec(tm, D_MODEL), _full_spec((1, D_MODEL)), _weight_spec(w_in.shape)],
        out_specs=[_row_spec(tm, D_MODEL), _row_spec(tm, REC_IN)],
        out_shape=[jax.ShapeDtypeStruct((n, D_MODEL), BF16), jax.ShapeDtypeStruct((n, REC_IN), F32)],
        compiler_params=_cparams(1),
    )(x, w_pre, w_in)


def out_proj(og, w_out, b_out, x_res, w_post, tm=1024):
    n = og.shape[0]
    tm = min(tm, n)

    def body(og_ref, w_ref, b_ref, x_ref, wp_ref, y_ref, xo_ref):
        y = jnp.dot(og_ref[...], w_ref[...], preferred_element_type=F32) + b_ref[...]
        y_ref[...] = y.astype(BF16)
        xo_ref[...] = x_ref[...] + y * _rms(y) * wp_ref[...]

    return pl.pallas_call(
        body, name="out_proj", grid=(n // tm,),
        in_specs=[_row_spec(tm, D_MODEL), _weight_spec((D_MODEL, D_MODEL)), _full_spec((1, D_MODEL)),
                  _row_spec(tm, D_MODEL), _full_spec((1, D_MODEL))],
        out_specs=[_row_spec(tm, D_MODEL), _row_spec(tm, D_MODEL)],
        out_shape=[jax.ShapeDtypeStruct((n, D_MODEL), BF16), jax.ShapeDtypeStruct((n, D_MODEL), F32)],
        compiler_params=_cparams(1),
    )(og, w_out, b_out, x_res, w_post)


def _post_norm_bwd(g, y, w_post):
    rstd = _rms(y)
    yn = y * rstd
    gw = g * w_post
    return rstd * (gw - yn * jnp.mean(gw * yn, axis=-1, keepdims=True)), jnp.sum(g * yn, axis=0, keepdims=True)


def out_proj_loss_bwd(og, w_out, x_res, w_post, target, tm=1024):
    n = og.shape[0]
    tm = min(tm, n)
    steps = n // tm

    def body(og_ref, w_ref, x_ref, wp_ref, t_ref, dx_ref, l_ref, dog_ref, dw_ref, dwp_ref, dwb_ref):
        @pl.when(pl.program_id(0) == 0)
        def _():
            l_ref[...] = jnp.zeros_like(l_ref)
            dw_ref[...] = jnp.zeros_like(dw_ref)
            dwp_ref[...] = jnp.zeros_like(dwp_ref)

        og_tile = og_ref[...]
        y = jnp.dot(og_tile, w_ref[...], preferred_element_type=F32)
        err = x_ref[...] + y * _rms(y) * wp_ref[...] - t_ref[...]
        g = err * (1.0 / D_MODEL)
        dx_ref[...] = g
        l_ref[...] += jnp.sum(err * err, axis=0, keepdims=True)
        dy, dwp = _post_norm_bwd(g, y, wp_ref[...])
        dwp_ref[...] += dwp
        dyb = dy.astype(BF16)
        dog_ref[...] = _dot(dyb, w_ref[...], _NT).astype(BF16)
        dw_ref[...] += _dot(og_tile, dyb, _TN)

        @pl.when(pl.program_id(0) == steps - 1)
        def _():
            dwb_ref[...] = dw_ref[...].astype(BF16)

    return pl.pallas_call(
        body, name="out_proj_loss_bwd", grid=(steps,),
        in_specs=[_row_spec(tm, D_MODEL), _weight_spec((D_MODEL, D_MODEL)), _row_spec(tm, D_MODEL),
                  _full_spec((1, D_MODEL)), _row_spec(tm, D_MODEL)],
        out_specs=[_row_spec(tm, D_MODEL), _full_spec((1, D_MODEL)), _row_spec(tm, D_MODEL),
                   _full_spec((D_MODEL, D_MODEL)), _full_spec((1, D_MODEL)), _full_spec((D_MODEL, D_MODEL))],
        out_shape=[jax.ShapeDtypeStruct((n, D_MODEL), F32), jax.ShapeDtypeStruct((1, D_MODEL), F32),
                   jax.ShapeDtypeStruct((n, D_MODEL), BF16), jax.ShapeDtypeStruct((D_MODEL, D_MODEL), F32),
                   jax.ShapeDtypeStruct((1, D_MODEL), F32), jax.ShapeDtypeStruct((D_MODEL, D_MODEL), BF16)],
        compiler_params=_cparams(1),
    )(og, w_out, x_res, w_post, target)


def out_proj_bwd(dxo, y, og, w_out, w_post, tm=1024):
    n = og.shape[0]
    tm = min(tm, n)
    steps = n // tm

    def body(g_ref, y_ref, og_ref, w_ref, wp_ref, dog_ref, dw_ref, db_ref, dwp_ref, dwb_ref):
        @pl.when(pl.program_id(0) == 0)
        def _():
            dw_ref[...] = jnp.zeros_like(dw_ref)
            db_ref[...] = jnp.zeros_like(db_ref)
            dwp_ref[...] = jnp.zeros_like(dwp_ref)

        dy, dwp = _post_norm_bwd(g_ref[...], y_ref[...].astype(F32), wp_ref[...])
        dwp_ref[...] += dwp
        db_ref[...] += jnp.sum(dy, axis=0, keepdims=True)
        dyb = dy.astype(BF16)
        dog_ref[...] = _dot(dyb, w_ref[...], _NT).astype(BF16)
        dw_ref[...] += _dot(og_ref[...], dyb, _TN)

        @pl.when(pl.program_id(0) == steps - 1)
        def _():
            dwb_ref[...] = dw_ref[...].astype(BF16)

    return pl.pallas_call(
        body, name="out_proj_bwd", grid=(steps,),
        in_specs=[_row_spec(tm, D_MODEL), _row_spec(tm, D_MODEL), _row_spec(tm, D_MODEL),
                  _weight_spec((D_MODEL, D_MODEL)), _full_spec((1, D_MODEL))],
        out_specs=[_row_spec(tm, D_MODEL), _full_spec((D_MODEL, D_MODEL)), _full_spec((1, D_MODEL)),
                   _full_spec((1, D_MODEL)), _full_spec((D_MODEL, D_MODEL))],
        out_shape=[jax.ShapeDtypeStruct((n, D_MODEL), BF16), jax.ShapeDtypeStruct((D_MODEL, D_MODEL), F32),
                   jax.ShapeDtypeStruct((1, D_MODEL), F32), jax.ShapeDtypeStruct((1, D_MODEL), F32),
                   jax.ShapeDtypeStruct((D_MODEL, D_MODEL), BF16)],
        compiler_params=_cparams(1),
    )(dxo, y, og, w_out, w_post)


def in_proj_bwd_x(dproj, w_in, x, w_pre, dxo, tm=1024, scatter=()):
    n, p = dproj.shape
    tm = min(tm, n)
    steps = n // tm
    ns = len(scatter)
    columns, block = _column_blocks(w_in)

    def body(*refs):
        dp_ref, w_ref, x_ref, wp_ref, g_ref = refs[:5]
        dx_ref, dwp_ref = refs[5 + ns:7 + ns]
        exchange = (refs[5:5 + ns], refs[7 + ns:7 + 2 * ns]) + tuple(refs[7 + 2 * ns:])

        @pl.when(pl.program_id(0) == 0)
        def _():
            dwp_ref[...] = jnp.zeros_like(dwp_ref)
            if ns:
                _scatter_start(*exchange)

        dh = functools.reduce(jnp.add, [_dot(dp_ref[:, cols], block(w_ref, s), _NT)
                                        for s, cols in enumerate(columns)])
        xv = x_ref[...]
        rstd = _rms(xv)
        xn = xv * rstd
        gw = dh * wp_ref[...]
        dwp_ref[...] += jnp.sum(dh * xn, axis=0, keepdims=True)
        dx_ref[...] = rstd * (gw - xn * jnp.mean(gw * xn, axis=-1, keepdims=True)) + g_ref[...]

        if ns:
            @pl.when(pl.program_id(0) == steps - 1)
            def _():
                _scatter_finish(*exchange)

    out = pl.pallas_call(
        body, name=f"in_proj_bwd_x_{p}", grid=(steps,),
        in_specs=[_row_spec(tm, p), _weight_spec(w_in.shape), _row_spec(tm, D_MODEL), _full_spec((1, D_MODEL)),
                  _row_spec(tm, D_MODEL)] + [_ANY] * ns,
        out_specs=[_row_spec(tm, D_MODEL), _full_spec((1, D_MODEL))] + [_ANY] * ns,
        out_shape=[jax.ShapeDtypeStruct((n, D_MODEL), F32), jax.ShapeDtypeStruct((1, D_MODEL), F32)]
        + [jax.ShapeDtypeStruct(a.shape, a.dtype) for a in scatter],
        scratch_shapes=_scatter_sems(ns) if ns else [],
        compiler_params=_cparams(1),
    )(dproj, w_in, x, w_pre, dxo, *scatter)
    return out[0], out[1], out[2:]


def in_proj_bwd_w(h, dproj, tm=1024, as_shards=False, kv=None):
    n, p = dproj.shape
    chunk = p // (4 if p % 4096 == 0 else 3)
    tm = min(tm, n)
    steps = n // tm
    shard = p // N_CHIPS
    n_kv = 0 if kv is None else 3
    kv_from, kv_to = D_MODEL, D_MODEL + 2 * KV_WIDTH

    def body(*refs):
        h_ref, dp_ref = refs[:2]
        dw_ref, db_ref = refs[2 + n_kv:4 + n_kv]
        scratch = refs[4 + n_kv + (kv is not None):]
        acc_scr, sem, staging = scratch[0], scratch[1], scratch[2:]
        i = pl.program_id(0)

        @pl.when(i == 0)
        def _():
            acc_scr[...] = jnp.zeros_like(acc_scr)
            db_ref[...] = jnp.zeros_like(db_ref)

        if kv is not None:
            dk_ref, dv_ref, cs_ref, kv_ref = refs[2], refs[3], refs[4], refs[4 + n_kv]
            made = jnp.concatenate([_rope_transposed(dk_ref[...].T, *_rope_tables(cs_ref[...])), dv_ref[...].T],
                                   axis=1).astype(BF16)
            kv_ref[...] = made

        def columns(c0):
            if kv is None or c0 + chunk <= kv_from or c0 >= kv_to:
                return dp_ref[:, c0:c0 + chunk]
            return jnp.concatenate([dp_ref[:, c0:kv_from], made, dp_ref[:, kv_to:c0 + chunk]], axis=1)

        ht = h_ref[...].T
        for c0 in range(0, p, chunk):
            dp = columns(c0)
            acc_scr[:, c0:c0 + chunk] += jnp.dot(ht, dp, preferred_element_type=F32)
            db_ref[:, c0:c0 + chunk] += jnp.sum(dp.astype(F32), axis=0, keepdims=True)

        @pl.when(i == steps - 1)
        def _():
            if as_shards:
                for s in range(N_CHIPS):
                    staging[0][...] = acc_scr[:, s * shard:(s + 1) * shard].astype(BF16)
                    out = pltpu.make_async_copy(staging[0], dw_ref.at[s], sem)
                    out.start()
                    out.wait()
            else:
                out = pltpu.make_async_copy(acc_scr, dw_ref, sem)
                out.start()
                out.wait()

    dw_shape = jax.ShapeDtypeStruct((N_CHIPS, D_MODEL, shard), BF16) if as_shards else (
        jax.ShapeDtypeStruct((D_MODEL, p), F32))
    in_specs = [_row_spec(tm, D_MODEL), _row_spec(tm, p)]
    out_specs = [_ANY, _full_spec((1, p))]
    out_shape = [dw_shape, jax.ShapeDtypeStruct((1, p), F32)]
    if kv is not None:
        columns_t = pl.BlockSpec((KV_WIDTH, tm), lambda i: (0, i))
        in_specs += [columns_t, columns_t, _row_spec(tm, ANGLE_COLS)]
        out_specs.append(pl.BlockSpec((tm, kv_to - kv_from), lambda i: (i, kv_from // (kv_to - kv_from))))
        out_shape.append(jax.ShapeDtypeStruct(dproj.shape, dproj.dtype))
    return pl.pallas_call(
        body, name=f"in_proj_bwd_w_{p}", grid=(steps,),
        in_specs=in_specs, out_specs=out_specs, out_shape=out_shape,
        scratch_shapes=[pltpu.VMEM((D_MODEL, p), F32), pltpu.SemaphoreType.DMA]
        + ([pltpu.VMEM((D_MODEL, shard), BF16)] if as_shards else []),
        input_output_aliases={1: 2} if kv is not None else {},
        compiler_params=_cparams(1),
    )(h, dproj, *(kv or ()))


PAIRS = GROUP // 2
GROUP_ROWS = PAIRS * ATTN_BLOCK
MASKED = -1e30


def _kv_windows(k_ref, v_ref, i):
    ps = pl.multiple_of(jnp.maximum(i - 1, 0) * ATTN_BLOCK, ATTN_BLOCK)
    cs = pl.multiple_of(i * ATTN_BLOCK, ATTN_BLOCK)
    kw = jnp.concatenate([k_ref[pl.ds(ps, ATTN_BLOCK), :], k_ref[pl.ds(cs, ATTN_BLOCK), :]], axis=0)
    vw = jnp.concatenate([v_ref[pl.ds(ps, ATTN_BLOCK), :], v_ref[pl.ds(cs, ATTN_BLOCK), :]], axis=0)
    return kw.astype(F32).T, vw.astype(F32).T, ps, cs


def _low_rows(shape):
    return lax.broadcasted_iota(jnp.int32, shape, 0) < HEAD_DIM


def _spread(w, kvh):
    low = _low_rows(w.shape)
    swapped = pltpu.roll(w, HEAD_DIM, 0)
    if kvh == 0:
        return jnp.where(low, w, 0.0), jnp.where(low, 0.0, swapped)
    return jnp.where(low, swapped, 0.0), jnp.where(low, 0.0, w)


def _unspread(d_a, d_b, kvh):
    low = _low_rows(d_a.shape)
    if kvh == 0:
        return jnp.where(low, d_a + pltpu.roll(d_b, HEAD_DIM, 0), 0.0)
    return jnp.where(low, 0.0, pltpu.roll(d_a, HEAD_DIM, 0) + d_b)


def _stack_pairs(ref, kvh):
    return jnp.concatenate([ref[:, (kvh * PAIRS + j) * LANES:(kvh * PAIRS + j + 1) * LANES] for j in range(PAIRS)],
                           axis=0)


def _fill_bias(bias_scr):
    shape = (GROUP_ROWS, 2 * ATTN_BLOCK)
    r = lax.broadcasted_iota(jnp.int32, shape, 0) % ATTN_BLOCK
    c = lax.broadcasted_iota(jnp.int32, shape, 1)
    in_cur = (c >= ATTN_BLOCK) & ((c - ATTN_BLOCK) <= r)
    in_prev = (c < ATTN_BLOCK) & (c > r)
    bias_scr[0] = jnp.where(in_cur, 0.0, MASKED)
    bias_scr[1] = jnp.where(in_cur | in_prev, 0.0, MASKED)
    bias_scr[2] = jnp.where(c == r, 1.0, 0.0)


N_BIAS_TABLES = 3


def _sink_table(sinks):
    t = jnp.transpose(sinks.reshape(N_KV_HEADS, PAIRS, 2), (0, 2, 1))
    return jnp.broadcast_to(t[:, :, :, None, None], (N_KV_HEADS, 2, PAIRS, ATTN_BLOCK, LANES)).reshape(
        N_KV_HEADS, 2, GROUP_ROWS, LANES)


def attn_fwd(q, k, v, z, sink_tab, batch, seq, gather=()):
    nb = seq // ATTN_BLOCK
    ng = len(gather)

    def body(*refs):
        q_ref, k_ref, v_ref, z_ref, s_ref = refs[:5]
        og_ref, bias_scr = refs[5 + ng], refs[6 + 2 * ng]
        exchange = (refs[5:5 + ng], refs[6 + ng:6 + 2 * ng]) + tuple(refs[7 + 2 * ng:])
        b, i = pl.program_id(0), pl.program_id(1)

        @pl.when((b == 0) & (i == 0))
        def _():
            _fill_bias(bias_scr)
            if ng:
                _gather_start(*exchange)

        kw, vw, _, _ = _kv_windows(k_ref, v_ref, i)
        bias, at_sink = bias_scr[jnp.minimum(i, 1)], bias_scr[2] > 0.5
        for kvh in range(N_KV_HEADS):
            k_a, k_b = _spread(kw, kvh)
            v_a, v_b = _spread(vw, kvh)
            og = _attn_group(_stack_pairs(q_ref, kvh), k_a, v_a, k_b, v_b, _stack_pairs(z_ref, kvh),
                             s_ref[kvh, 0], s_ref[kvh, 1], bias, at_sink)
            for j in range(PAIRS):
                og_ref[:, (kvh * PAIRS + j) * LANES:(kvh * PAIRS + j + 1) * LANES] = (
                    og[j * ATTN_BLOCK:(j + 1) * ATTN_BLOCK].astype(BF16))

        if ng:
            @pl.when((b == batch - 1) & (i == nb - 1))
            def _():
                _gather_finish(*exchange)

    blk = lambda w: pl.BlockSpec((ATTN_BLOCK, w), lambda b, i: (b * nb + i, 0))
    seq_spec = pl.BlockSpec((seq, KV_WIDTH), lambda b, i: (b, 0))
    out = pl.pallas_call(
        body, name="attn_fwd", grid=(batch, nb),
        in_specs=[blk(D_MODEL), seq_spec, seq_spec, blk(D_MODEL), _full_spec(sink_tab.shape)] + [_ANY] * ng,
        out_specs=[blk(D_MODEL)] + [_ANY] * ng,
        out_shape=[jax.ShapeDtypeStruct((batch * seq, D_MODEL), BF16)]
        + [jax.ShapeDtypeStruct((N_CHIPS,) + a.shape, a.dtype) for a in gather],
        scratch_shapes=[pltpu.VMEM((N_BIAS_TABLES, GROUP_ROWS, 2 * ATTN_BLOCK), F32)] + (_gather_sems(ng) if ng else []),
        compiler_params=_cparams(2),
    )(q, k, v, z, sink_tab, *gather)
    return out[0], out[1:]


def attn_bwd(q, k, v, z, sink_tab, dog, angles, batch, seq, scatter=()):
    nb = seq // ATTN_BLOCK
    ns = len(scatter)

    def body(*refs):
        q_ref, k_ref, v_ref, z_ref, s_ref, g_ref, cs_ref = refs[:7]
        dp_ref, dk_ref, dv_ref, ds_ref = refs[7 + ns:11 + ns]
        bias_scr = refs[11 + 2 * ns]
        exchange = (refs[7:7 + ns], refs[11 + ns:11 + 2 * ns]) + tuple(refs[12 + 2 * ns:])
        b, i = pl.program_id(0), pl.program_id(1)

        @pl.when((b == 0) & (i == 0))
        def _():
            _fill_bias(bias_scr)
            ds_ref[...] = jnp.zeros_like(ds_ref)
            if ns:
                _scatter_start(*exchange)

        @pl.when(i == 0)
        def _():
            dk_ref[...] = jnp.zeros_like(dk_ref)
            dv_ref[...] = jnp.zeros_like(dv_ref)

        kw, vw, ps, cs = _kv_windows(k_ref, v_ref, i)
        bias = bias_scr[jnp.minimum(i, 1)]
        tabs = _rope_tables(cs_ref[...])
        dkw = jnp.zeros_like(kw)
        dvw = jnp.zeros_like(vw)
        for kvh in range(N_KV_HEADS):
            k_a, k_b = _spread(kw, kvh)
            v_a, v_b = _spread(vw, kvh)
            _, vjp = jax.vjp(functools.partial(_attn_group, bias=bias), _stack_pairs(q_ref, kvh).astype(F32),
                             k_a, v_a, k_b, v_b, _stack_pairs(z_ref, kvh), s_ref[kvh, 0], s_ref[kvh, 1])
            dqs, dk_a, dv_a, dk_b, dv_b, dzs, ds_a, ds_b = vjp(_stack_pairs(g_ref, kvh).astype(F32))
            dkw = dkw + _unspread(dk_a, dk_b, kvh)
            dvw = dvw + _unspread(dv_a, dv_b, kvh)
            ds_ref[kvh, 0] += jnp.sum(ds_a.reshape(PAIRS, ATTN_BLOCK, LANES), axis=1)
            ds_ref[kvh, 1] += jnp.sum(ds_b.reshape(PAIRS, ATTN_BLOCK, LANES), axis=1)
            for j in range(PAIRS):
                rows = slice(j * ATTN_BLOCK, (j + 1) * ATTN_BLOCK)
                col = (kvh * PAIRS + j) * LANES
                dp_ref[:, col:col + LANES] = _rope_transposed(dqs[rows] * (HEAD_DIM ** -0.5), *tabs).astype(BF16)
                zc = D_MODEL + 2 * KV_WIDTH + col
                dp_ref[:, zc:zc + LANES] = dzs[rows].astype(BF16)
        dp_ref[:, D_MODEL:D_MODEL + 2 * KV_WIDTH] = jnp.zeros((ATTN_BLOCK, 2 * KV_WIDTH), BF16)
        dk_ref[:, pl.ds(ps, ATTN_BLOCK)] += dkw[:, :ATTN_BLOCK]
        dk_ref[:, pl.ds(cs, ATTN_BLOCK)] += dkw[:, ATTN_BLOCK:]
        dv_ref[:, pl.ds(ps, ATTN_BLOCK)] += dvw[:, :ATTN_BLOCK]
        dv_ref[:, pl.ds(cs, ATTN_BLOCK)] += dvw[:, ATTN_BLOCK:]

        if ns:
            @pl.when((b == batch - 1) & (i == nb - 1))
            def _():
                _scatter_finish(*exchange)

    blk = lambda w: pl.BlockSpec((ATTN_BLOCK, w), lambda b, i: (b * nb + i, 0))
    seq_spec = pl.BlockSpec((seq, KV_WIDTH), lambda b, i: (b, 0))
    seq_spec_t = pl.BlockSpec((KV_WIDTH, seq), lambda b, i: (0, b))
    n = batch * seq
    ds_shape = (N_KV_HEADS, 2, PAIRS, LANES)
    out = pl.pallas_call(
        body, name="attn_bwd", grid=(batch, nb),
        in_specs=[blk(D_MODEL), seq_spec, seq_spec, blk(D_MODEL), _full_spec(sink_tab.shape), blk(D_MODEL)]
        + [blk(ANGLE_COLS)] + [_ANY] * ns,
        out_specs=[blk(ATTN_IN), seq_spec_t, seq_spec_t, _full_spec(ds_shape)] + [_ANY] * ns,
        out_shape=[jax.ShapeDtypeStruct((n, ATTN_IN), BF16), jax.ShapeDtypeStruct((KV_WIDTH, n), F32),
                   jax.ShapeDtypeStruct((KV_WIDTH, n), F32), jax.ShapeDtypeStruct(ds_shape, F32)]
        + [jax.ShapeDtypeStruct(a.shape, a.dtype) for a in scatter],
        scratch_shapes=[pltpu.VMEM((N_BIAS_TABLES, GROUP_ROWS, 2 * ATTN_BLOCK), F32)] + (_scatter_sems(ns) if ns else []),
        compiler_params=_cparams(2),
    )(q, k, v, z, sink_tab, dog, angles, *scatter)
    return out[0], out[1], out[2], out[3], out[4:]


def rec_fwd(proj, lb_logits, gnorm_w, batch, seq):
    nblk = seq // REC_BLOCK

    def body(p_ref, lb_ref, gw_ref, og_ref, st_ref, safe_ref, s_scr):
        @pl.when(pl.program_id(1) == 0)
        def _():
            s_scr[...] = jnp.zeros_like(s_scr)

        S = s_scr[...]
        st_ref[0] = S
        qr, fr, v, z = (p_ref[:, part * D_MODEL:(part + 1) * D_MODEL] for part in range(4))
        lf, k = forget_gate(fr, lb_ref[1:2, :] - lb_ref[0:1, :])
        q, b = silu(qr), cumsum_rows(lf)
        safe = jnp.min(_rec_margin(b)) >= -SAFE_RANGE

        gate = gw_ref[...] * silu(z)
        safe_ref[0] = jnp.full((REC_HEADS, LANES), safe.astype(F32))

        def store(o, S_new):
            og_ref[...] = (o * lax.rsqrt(head_sum(o * o) * (1.0 / REC_DIM) + NORM_EPS) * gate).astype(BF16)
            s_scr[...] = S_new

        @pl.when(safe)
        def _():
            store(*_rec_cores_fast(q, k, v, b, S))

        @pl.when(jnp.logical_not(safe))
        def _():
            outs = [_rec_core_slow(*args) for args in zip(*(_heads(t) for t in (q, k, v, b, S)))]
            store(*(jnp.concatenate(parts, axis=1) for parts in zip(*outs)))

    blk = lambda w: pl.BlockSpec((REC_BLOCK, w), lambda b, j: (b * nblk + j, 0))
    st_spec = pl.BlockSpec((1, REC_DIM, D_MODEL), lambda b, j: (b * nblk + j, 0, 0))
    safe_spec = pl.BlockSpec((1, REC_HEADS, LANES), lambda b, j: (b * nblk + j, 0, 0))
    return pl.pallas_call(
        body, name="rec_fwd", grid=(batch, nblk),
        in_specs=[blk(REC_IN), _full_spec((2, D_MODEL)), _full_spec((1, D_MODEL))],
        out_specs=[blk(D_MODEL), st_spec, safe_spec],
        out_shape=[jax.ShapeDtypeStruct((batch * seq, D_MODEL), BF16),
                   jax.ShapeDtypeStruct((batch * nblk, REC_DIM, D_MODEL), F32),
                   jax.ShapeDtypeStruct((batch * nblk, REC_HEADS, LANES), F32)],
        scratch_shapes=[pltpu.VMEM((REC_DIM, D_MODEL), F32)],
        compiler_params=_cparams(2),
    )(proj, lb_logits, jnp.tile(gnorm_w, (1, REC_HEADS)))


def rec_bwd(proj, states, safe, lb_logits, gnorm_w, dog, batch, seq):
    nblk = seq // REC_BLOCK

    def body(p_ref, st_ref, safe_ref, lb_ref, gw_ref, g_ref, dp_ref, dlb_ref, dgw_ref, ds_scr):
        @pl.when((pl.program_id(0) == 0) & (pl.program_id(1) == 0))
        def _():
            dlb_ref[...] = jnp.zeros_like(dlb_ref)
            dgw_ref[...] = jnp.zeros_like(dgw_ref)

        @pl.when(pl.program_id(1) == 0)
        def _():
            ds_scr[...] = jnp.zeros_like(ds_scr)

        def load():
            primals = tuple(p_ref[:, part * D_MODEL:(part + 1) * D_MODEL] for part in range(4)) + (
                st_ref[0], lb_ref[0:1, :], lb_ref[1:2, :], gw_ref[...])
            return primals, (g_ref[...].astype(F32), ds_scr[...])

        def store(dqr, dfr, dv, dz, dS, dl0, dl1, dgw):
            for part, val in enumerate((dqr, dfr, dv, dz)):
                dp_ref[:, part * D_MODEL:(part + 1) * D_MODEL] = val.astype(BF16)
            ds_scr[...] = dS
            dlb_ref[0:1, :] += dl0
            dlb_ref[1:2, :] += dl1
            dgw_ref[...] += functools.reduce(jnp.add, _heads(dgw))

        fast = jnp.max(safe_ref[0]) > 0.5

        @pl.when(fast)
        def _():
            primals, cotangents = load()
            store(*jax.vjp(_rec_block_fast, *primals)[1](cotangents))

        @pl.when(jnp.logical_not(fast))
        def _():
            primals, cotangents = load()
            outs = [jax.vjp(functools.partial(_rec_head, _rec_core_slow), *args)[1](cts)
                    for args, cts in zip(zip(*(_heads(t) for t in primals)), zip(*(_heads(t) for t in cotangents)))]
            store(*(jnp.concatenate(parts, axis=1) for parts in zip(*outs)))

    blk = lambda w: pl.BlockSpec((REC_BLOCK, w), lambda b, j: (b * nblk + nblk - 1 - j, 0))
    st_spec = pl.BlockSpec((1, REC_DIM, D_MODEL), lambda b, j: (b * nblk + nblk - 1 - j, 0, 0))
    safe_spec = pl.BlockSpec((1, REC_HEADS, LANES), lambda b, j: (b * nblk + nblk - 1 - j, 0, 0))
    return pl.pallas_call(
        body, name="rec_bwd", grid=(batch, nblk),
        in_specs=[blk(REC_IN), st_spec, safe_spec, _full_spec((2, D_MODEL)), _full_spec((1, D_MODEL)),
                  blk(D_MODEL)],
        out_specs=[blk(REC_IN), _full_spec((2, D_MODEL)), _full_spec((1, REC_DIM))],
        out_shape=[jax.ShapeDtypeStruct((batch * seq, REC_IN), BF16), jax.ShapeDtypeStruct((2, D_MODEL), F32),
                   jax.ShapeDtypeStruct((1, REC_DIM), F32)],
        scratch_shapes=[pltpu.VMEM((REC_DIM, D_MODEL), F32)],
        compiler_params=_cparams(2),
    )(proj, states, safe, lb_logits, jnp.tile(gnorm_w, (1, REC_HEADS)), dog)


_ANY = pl.BlockSpec(memory_space=pl.ANY)


def _chip_peers():
    x, y, c = lax.axis_index("x"), lax.axis_index("y"), lax.axis_index("c")
    peers = []
    for fx, fy in ((1, 0), (0, 1), (1, 1)):
        px, py = (1 - x if fx else x), (1 - y if fy else y)
        peers.append(((px, py, c), 2 * px + py))
    return 2 * x + y, peers


def _remote(src, dst, send_sem, recv_sem, device):
    return pltpu.make_async_remote_copy(src_ref=src, dst_ref=dst, send_sem=send_sem, recv_sem=recv_sem,
                                        device_id=device, device_id_type=MESH)


N_FLIPS = N_CHIPS - 1


def _scatter_sems(n):
    return [pltpu.SemaphoreType.DMA((n * N_FLIPS,)), pltpu.SemaphoreType.DMA((n * N_FLIPS,)),
            pltpu.SemaphoreType.DMA((n,))]


def _scatter_copies(ins, outs, send_sems, recv_sems, local_sems, starting):
    me, peers = _chip_peers()
    local = [pltpu.make_async_copy(ins[k].at[me], outs[k].at[me], local_sems.at[k]) for k in range(len(ins))]
    sends, arrivals = [], []
    for k in range(len(ins)):
        for j, (device, idx) in enumerate(peers):
            sems = (send_sems.at[k * N_FLIPS + j], recv_sems.at[k * N_FLIPS + j], device)
            sends.append(_remote(ins[k].at[idx], outs[k].at[me], *sems))
            if not starting:
                arrivals.append(_remote(ins[k].at[me], outs[k].at[idx], *sems))
    return local, sends, arrivals


def _scatter_start(*refs):
    local, sends, _ = _scatter_copies(*refs, starting=True)
    for cp in local + sends:
        cp.start()


def _scatter_finish(*refs):
    local, sends, arrivals = _scatter_copies(*refs, starting=False)
    for cp in arrivals:
        cp.wait_recv()
    for cp in sends:
        cp.wait_send()
    for cp in local:
        cp.wait()


def _gather_sems(n):
    return [pltpu.SemaphoreType.DMA((n * N_FLIPS,)) for _ in range(4)] + [pltpu.SemaphoreType.DMA((n,))]


def _gather_copies(ins, outs, send_sems, recv_sems, pass_send_sems, pass_recv_sems, local_sems, starting):
    me, peers = _chip_peers()
    c = lax.axis_index("c")
    sibling = (lax.axis_index("x"), lax.axis_index("y"), 1 - c)
    local = [pltpu.make_async_copy(ins[k], outs[k].at[me], local_sems.at[k]) for k in range(len(ins))]
    sends, arrivals, passes, pass_arrivals = [], [], [], []
    for k in range(len(ins)):
        half = ins[k].shape[0] // 2
        mine, other = pl.ds(c * half, half), pl.ds((1 - c) * half, half)
        for j, (device, idx) in enumerate(peers):
            s = k * N_FLIPS + j
            sends.append(_remote(ins[k].at[mine], outs[k].at[me].at[mine], send_sems.at[s], recv_sems.at[s], device))
            if starting:
                continue
            arrived = outs[k].at[idx].at[mine]
            arrivals.append(_remote(ins[k].at[mine], arrived, send_sems.at[s], recv_sems.at[s], device))
            passes.append(_remote(arrived, arrived, pass_send_sems.at[s], pass_recv_sems.at[s], sibling))
            passed = outs[k].at[idx].at[other]
            pass_arrivals.append(_remote(passed, passed, pass_send_sems.at[s], pass_recv_sems.at[s], sibling))
    return local, sends, arrivals, passes, pass_arrivals


def _gather_start(*refs):
    local, sends, _, _, _ = _gather_copies(*refs, starting=True)
    for cp in local + sends:
        cp.start()


def _gather_finish(*refs):
    local, sends, arrivals, passes, pass_arrivals = _gather_copies(*refs, starting=False)
    for arrival, onward in zip(arrivals, passes):
        arrival.wait_recv()
        onward.start()
    for cp in pass_arrivals:
        cp.wait_recv()
    for cp in sends + passes:
        cp.wait_send()
    for cp in local:
        cp.wait()


def chip_gather(arrays):
    n = len(arrays)

    def body(*refs):
        _gather_start(refs[:n], refs[n:2 * n], *refs[2 * n:])
        _gather_finish(refs[:n], refs[n:2 * n], *refs[2 * n:])

    return pl.pallas_call(
        body, name="chip_gather", in_specs=[_ANY] * n, out_specs=[_ANY] * n,
        out_shape=[jax.ShapeDtypeStruct((N_CHIPS,) + a.shape, a.dtype) for a in arrays],
        scratch_shapes=_gather_sems(n),
    )(*arrays)


def sibling_exchange(arrays):
    n = len(arrays)

    def body(*refs):
        ins, outs = refs[:n], refs[n:2 * n]
        send_sems, recv_sems = refs[2 * n:]
        sibling = (lax.axis_index("x"), lax.axis_index("y"), 1 - lax.axis_index("c"))
        copies = [pltpu.make_async_remote_copy(src_ref=ins[k], dst_ref=outs[k], send_sem=send_sems.at[k],
                                               recv_sem=recv_sems.at[k], device_id=sibling, device_id_type=MESH)
                  for k in range(n)]
        for cp in copies:
            cp.start()
        for cp in copies:
            cp.wait()

    return pl.pallas_call(
        body, name="sibling_exchange", in_specs=[_ANY] * n, out_specs=[_ANY] * n,
        out_shape=[jax.ShapeDtypeStruct(a.shape, a.dtype) for a in arrays],
        scratch_shapes=[pltpu.SemaphoreType.DMA((n,)), pltpu.SemaphoreType.DMA((n,))],
    )(*arrays)


def all_gather_small(vec):
    def body(v_ref, out_ref, send_sems, recv_sems, local_sem):
        x, y, c = lax.axis_index("x"), lax.axis_index("y"), lax.axis_index("c")
        me = 4 * x + 2 * y + c
        local = pltpu.make_async_copy(v_ref, out_ref.at[me], local_sem)
        local.start()
        sends, recvs = [], []
        for j in range(1, N_DEV):
            px = jnp.where(j & 4, 1 - x, x)
            py = jnp.where(j & 2, 1 - y, y)
            pc = jnp.where(j & 1, 1 - c, c)
            common = dict(send_sem=send_sems.at[j - 1], recv_sem=recv_sems.at[j - 1], device_id=(px, py, pc),
                          device_id_type=MESH)
            sends.append(pltpu.make_async_remote_copy(src_ref=v_ref, dst_ref=out_ref.at[me], **common))
            recvs.append(pltpu.make_async_remote_copy(src_ref=v_ref, dst_ref=out_ref.at[4 * px + 2 * py + pc],
                                                      **common))
        for cp in sends:
            cp.start()
        for cp in recvs:
            cp.wait_recv()
        for cp in sends:
            cp.wait_send()
        local.wait()

    return pl.pallas_call(
        body, name="all_gather_small", in_specs=[_ANY], out_specs=_ANY,
        out_shape=jax.ShapeDtypeStruct((N_DEV,) + vec.shape, vec.dtype),
        scratch_shapes=[pltpu.SemaphoreType.DMA((N_DEV - 1,)), pltpu.SemaphoreType.DMA((N_DEV - 1,)),
                        pltpu.SemaphoreType.DMA],
    )(vec)


def sum_slots(stacked, tm=256):
    s, r, c = stacked.shape
    tm = min(tm, r)

    def body(in_ref, out_ref):
        acc = in_ref[0].astype(F32)
        for t in range(1, s):
            acc = acc + in_ref[t].astype(F32)
        out_ref[...] = acc

    return pl.pallas_call(
        body, name=f"sum_slots_{s}_{r}_{c}", grid=(r // tm,),
        in_specs=[pl.BlockSpec((s, tm, c), lambda i: (0, i, 0))], out_specs=_row_spec(tm, c),
        out_shape=jax.ShapeDtypeStruct((r, c), F32), compiler_params=_cparams(1),
    )(stacked)


def adamw(w, m, v, g_a, g_b=None, tm=256):
    r, c = w.shape
    tm = min(tm, r)
    two = g_b is not None

    def body(*refs):
        w_ref, m_ref, v_ref, ga_ref = refs[:4]
        g_ref, d_ref, nm_ref, nv_ref = refs[-4:]
        g = ga_ref[...] + refs[4][...] if two else ga_ref[...]
        nm = ADAM_B1 * m_ref[...] + (1.0 - ADAM_B1) * g
        nv = ADAM_B2 * v_ref[...] + (1.0 - ADAM_B2) * (g * g)
        m_hat = nm / (1.0 - ADAM_B1 ** ADAM_STEP)
        v_hat = nv / (1.0 - ADAM_B2 ** ADAM_STEP)
        g_ref[...] = g
        d_ref[...] = -ADAM_LR * (m_hat / (jnp.sqrt(v_hat) + ADAM_EPS) + ADAM_WD * w_ref[...])
        nm_ref[...] = nm
        nv_ref[...] = nv

    args = [w, m, v, g_a] + ([g_b] if two else [])
    return pl.pallas_call(
        body, name=f"adamw_{r}_{c}", grid=(r // tm,),
        in_specs=[_row_spec(tm, c)] * len(args), out_specs=[_row_spec(tm, c)] * 4,
        out_shape=[jax.ShapeDtypeStruct((r, c), F32)] * 4, compiler_params=_cparams(1),
    )(*args)


_SMALL = (("pre_norm_w", (2, D_MODEL)), ("post_norm_w", (2, D_MODEL)), ("attn_b_in", (1, ATTN_IN)),
          ("attn_sinks", (1, N_HEADS)), ("attn_b_out", (1, D_MODEL)), ("rec_lb_logits", (2, D_MODEL)),
          ("rec_gnorm_w", (1, REC_DIM)))
_SMALL_ROWS = 16


def _pack_small(parts, last_row=None):
    rows = []
    for (name, shape) in _SMALL:
        flat = parts[name].reshape(-1)
        pad = -flat.shape[0] % D_MODEL
        rows.append(jnp.pad(flat, (0, pad)).reshape(-1, D_MODEL))
    used = sum(r.shape[0] for r in rows)
    rows.append(jnp.zeros((_SMALL_ROWS - 1 - used, D_MODEL), F32))
    rows.append(jnp.zeros((1, D_MODEL), F32) if last_row is None else last_row)
    return jnp.concatenate(rows, axis=0)


def _unpack_small(packed):
    out, row = {}, 0
    for (name, shape) in _SMALL:
        size = shape[0] * shape[1]
        nrows = -(-size // D_MODEL)
        out[name] = packed[row:row + nrows].reshape(-1)[:size].reshape(shape)
        row += nrows
    return out


_CARRIED = ("rec_w_in", "rec_w_out", "attn_w_out")


_LATE = ("attn_w_out", "rec_w_in", "rec_w_out")


def local_step(x, positions, pre_norm_w, post_norm_w, attn_w_in, attn_b_in, attn_sinks, attn_w_out, attn_b_out,
               rec_w_in, rec_lb_logits, rec_gnorm_w, rec_w_out, loss_target, distributed=False):
    batch, seq, _ = x.shape
    n = batch * seq
    x0 = x.reshape(n, D_MODEL)
    angles = _rope_angles(positions)
    pre0, pre1 = pre_norm_w[0:1], pre_norm_w[1:2]
    post0, post1 = post_norm_w[0:1], post_norm_w[1:2]

    late = (attn_w_out, rec_w_in, rec_w_out) if distributed else ()
    (h0, q, k, v, z), late = attn_in_proj(x0, pre0, attn_w_in, attn_b_in, angles, to_bf16=late)
    sink_tab = _sink_table(attn_sinks)
    og0, gathered = attn_fwd(q, k, v, z, sink_tab, batch, seq, gather=late)
    if distributed:
        attn_w_out, rec_w_in, rec_w_out = (g if name == "rec_w_in" else _whole_from_shards(name, g)
                                           for name, g in zip(_LATE, gathered))
    y0, x1 = out_proj(og0, attn_w_out, attn_b_out, x0, post0)

    h1, proj1 = rec_in_proj(x1, pre1, rec_w_in)
    og1, states, safe = rec_fwd(proj1, rec_lb_logits, rec_gnorm_w, batch, seq)
    dx2, loss_vec, dog1, d_rec_w_out, d_post1, d_rec_w_out_bf16 = out_proj_loss_bwd(
        og1, rec_w_out, x1, post1, loss_target.reshape(n, D_MODEL))
    dproj1, d_lb, d_gnorm = rec_bwd(proj1, states, safe, rec_lb_logits, rec_gnorm_w, dog1, batch, seq)
    dx1, d_pre1, _ = in_proj_bwd_x(dproj1, rec_w_in, x1, pre1, dx2)
    d_rec_w_in, _ = in_proj_bwd_w(h1, dproj1, as_shards=distributed)

    dog0, d_attn_w_out, d_attn_b_out, d_post0, d_attn_w_out_bf16 = out_proj_bwd(dx1, y0, og0, attn_w_out, post0)
    ready = dict(rec_w_in=d_rec_w_in, rec_w_out=_shards_from_whole("rec_w_out", d_rec_w_out_bf16),
                 attn_w_out=_shards_from_whole("attn_w_out", d_attn_w_out_bf16))
    outgoing = [ready[name] for name in _CARRIED] if distributed else []
    dproj0, dk, dv, d_sink_tab, arrived = attn_bwd(q, k, v, z, sink_tab, dog0, angles, batch, seq, scatter=outgoing)
    d_sinks = jnp.transpose(jnp.sum(d_sink_tab, axis=-1), (0, 2, 1)).reshape(1, N_HEADS)
    d_attn_w_in, d_attn_b_in, dproj0 = in_proj_bwd_w(h0, dproj0, kv=(dk, dv, angles))
    last = [_shards_from_whole("attn_w_in", d_attn_w_in).astype(BF16)] if distributed else []
    dx0, d_pre0, arrived_last = in_proj_bwd_x(dproj0, attn_w_in, x0, pre0, dx1, scatter=last)

    grads = dict(
        pre_norm_w=jnp.concatenate([d_pre0, d_pre1], axis=0), post_norm_w=jnp.concatenate([d_post0, d_post1], axis=0),
        attn_w_in=d_attn_w_in, attn_b_in=d_attn_b_in, attn_sinks=d_sinks, attn_w_out=d_attn_w_out,
        attn_b_out=d_attn_b_out, rec_w_in=d_rec_w_in, rec_lb_logits=d_lb, rec_gnorm_w=d_gnorm,
        rec_w_out=d_rec_w_out)
    parts = dict(zip(_CARRIED + ("attn_w_in",), tuple(arrived) + tuple(arrived_last)))
    return loss_vec, dx0.reshape(batch, seq, D_MODEL), grads, parts


_BIG = ("attn_w_in", "attn_w_out", "rec_w_in", "rec_w_out")
_COLUMN_SHARDED = ("attn_w_in", "rec_w_in")
_ORDER = ("pre_norm_w", "post_norm_w", "attn_w_in", "attn_b_in", "attn_sinks", "attn_w_out", "attn_b_out",
          "rec_w_in", "rec_lb_logits", "rec_gnorm_w", "rec_w_out")


def _whole_from_shards(name, stacked):
    if name in _COLUMN_SHARDED:
        return jnp.transpose(stacked, (1, 0, 2)).reshape(stacked.shape[1], -1)
    return stacked.reshape(-1, stacked.shape[2])


def _shards_from_whole(name, whole):
    if name in _COLUMN_SHARDED:
        return jnp.transpose(whole.reshape(whole.shape[0], N_CHIPS, -1), (1, 0, 2))
    return whole.reshape(N_CHIPS, -1, whole.shape[1])


def kernel(x, positions, pre_norm_w, post_norm_w, attn_w_in, attn_b_in, attn_sinks, attn_w_out, attn_b_out, rec_w_in, rec_lb_logits, rec_gnorm_w, rec_w_out, loss_target, m_pre_norm_w, m_post_norm_w, m_attn_w_in, m_attn_b_in, m_attn_sinks, m_attn_w_out, m_attn_b_out, m_rec_w_in, m_rec_lb_logits, m_rec_gnorm_w, m_rec_w_out, v_pre_norm_w, v_post_norm_w, v_attn_w_in, v_attn_b_in, v_attn_sinks, v_attn_w_out, v_attn_b_out, v_rec_w_in, v_rec_lb_logits, v_rec_gnorm_w, v_rec_w_out):
    w = dict(pre_norm_w=pre_norm_w, post_norm_w=post_norm_w, attn_w_in=attn_w_in, attn_b_in=attn_b_in,
             attn_sinks=attn_sinks, attn_w_out=attn_w_out, attn_b_out=attn_b_out, rec_w_in=rec_w_in,
             rec_lb_logits=rec_lb_logits, rec_gnorm_w=rec_gnorm_w, rec_w_out=rec_w_out)
    m = dict(pre_norm_w=m_pre_norm_w, post_norm_w=m_post_norm_w, attn_w_in=m_attn_w_in, attn_b_in=m_attn_b_in,
             attn_sinks=m_attn_sinks, attn_w_out=m_attn_w_out, attn_b_out=m_attn_b_out, rec_w_in=m_rec_w_in,
             rec_lb_logits=m_rec_lb_logits, rec_gnorm_w=m_rec_gnorm_w, rec_w_out=m_rec_w_out)
    v = dict(pre_norm_w=v_pre_norm_w, post_norm_w=v_post_norm_w, attn_w_in=v_attn_w_in, attn_b_in=v_attn_b_in,
             attn_sinks=v_attn_sinks, attn_w_out=v_attn_w_out, attn_b_out=v_attn_b_out, rec_w_in=v_rec_w_in,
             rec_lb_logits=v_rec_lb_logits, rec_gnorm_w=v_rec_gnorm_w, rec_w_out=v_rec_w_out)

    shards = {name: w[name][0] for name in _BIG}
    attn_w_in_whole = _whole_from_shards("attn_w_in", chip_gather([shards["attn_w_in"].astype(BF16)])[0])

    loss_vec, grad_x, grads, parts = local_step(
        x, positions, pre_norm_w, post_norm_w, attn_w_in_whole, attn_b_in, attn_sinks, shards["attn_w_out"],
        attn_b_out, shards["rec_w_in"], rec_lb_logits, rec_gnorm_w, shards["rec_w_out"], loss_target,
        distributed=True)

    plane_sums = [sum_slots(parts[name]) for name in _BIG]
    other_sums = sibling_exchange(plane_sums)
    out_g, out_d, out_m, out_v = {}, {}, {}, {}
    for name, mine, other in zip(_BIG, plane_sums, other_sums):
        g, d, nm, nv = adamw(shards[name], m[name][0], v[name][0], mine, other)
        out_g[name], out_d[name], out_m[name], out_v[name] = g[None], d[None], nm[None], nv[None]

    small_sum = sum_slots(all_gather_small(_pack_small(grads, last_row=loss_vec)))
    loss = jnp.sum(small_sum[_SMALL_ROWS - 1]) * (0.5 / D_MODEL)
    packed = adamw(_pack_small(w), _pack_small(m), _pack_small(v), small_sum)
    for dst, val in zip((out_g, out_d, out_m, out_v), packed):
        dst.update(_unpack_small(val))

    return (loss, grad_x, *[out_g[n] for n in _ORDER], *[out_d[n] for n in _ORDER],
            *[out_m[n] for n in _ORDER], *[out_v[n] for n in _ORDER])
```

```python
import functools

import jax
import jax.numpy as jnp
from jax import lax
from jax.experimental import pallas as pl
from jax.experimental.pallas import tpu as pltpu

F32 = jnp.float32
BF16 = jnp.bfloat16
MESH = pl.DeviceIdType.MESH

D_MODEL = 1024
HEAD_DIM = 64
N_HEADS = 16
N_KV_HEADS = 2
GROUP = N_HEADS // N_KV_HEADS
KV_WIDTH = N_KV_HEADS * HEAD_DIM
ATTN_IN = 2 * D_MODEL + 2 * KV_WIDTH
ATTN_BLOCK = 128
ROPE_THETA = 500000.0
ROPE_DIM = HEAD_DIM // 4
REC_HEADS = 8
REC_DIM = 128
REC_IN = 4 * D_MODEL
REC_BLOCK = 128
DIAG = 8
NORM_EPS = 1e-6
N_CHIPS = 4
N_DEV = 8
LANES = 128

ADAM_LR = 0.001
ADAM_B1 = 0.9
ADAM_B2 = 0.999
ADAM_EPS = 1e-08
ADAM_WD = 0.01
ADAM_STEP = 10

VMEM_LIMIT = 56 * 1024 * 1024


def _cparams(n_axes):
    return pltpu.CompilerParams(dimension_semantics=("arbitrary",) * n_axes, vmem_limit_bytes=VMEM_LIMIT)


def _dot(a, b, contract):
    return lax.dot_general(a.astype(BF16), b.astype(BF16), (contract, ((), ())), preferred_element_type=F32)


_NN = ((1,), (0,))
_NT = ((1,), (1,))
_TN = ((0,), (0,))


@jax.custom_vjp
def mm_nn(a, b):
    return _dot(a, b, _NN)


mm_nn.defvjp(lambda a, b: (_dot(a, b, _NN), (a, b)),
             lambda res, g: (_dot(g, res[1], _NT), _dot(res[0], g, _TN)))


@jax.custom_vjp
def mm_nt(a, b):
    return _dot(a, b, _NT)


mm_nt.defvjp(lambda a, b: (_dot(a, b, _NT), (a, b)),
             lambda res, g: (_dot(g, res[1], _NN), _dot(g, res[0], _TN)))


@jax.custom_vjp
def mm_tn(a, b):
    return _dot(a, b, _TN)


mm_tn.defvjp(lambda a, b: (_dot(a, b, _TN), (a, b)),
             lambda res, g: (_dot(res[1], g, _NT), _dot(res[0], g, _NN)))


def _tri_dot(x, lower):
    n = x.shape[0]
    r = lax.broadcasted_iota(jnp.int32, (n, n), 0)
    c = lax.broadcasted_iota(jnp.int32, (n, n), 1)
    tri = ((c <= r) if lower else (c >= r)).astype(BF16)
    hi = x.astype(BF16)
    rest = x - hi.astype(F32)
    mid = rest.astype(BF16)
    lo = (rest - mid.astype(F32)).astype(BF16)
    dot = lambda p: lax.dot_general(tri, p, (_NN, ((), ())), preferred_element_type=F32)
    return (dot(lo) + dot(mid)) + dot(hi)


@jax.custom_vjp
def cumsum_rows(x):
    return _tri_dot(x, True)


cumsum_rows.defvjp(lambda x: (cumsum_rows(x), None), lambda _, g: (_tri_dot(g, False),))


@functools.partial(jax.custom_vjp, nondiff_argnums=(1,))
def roll_sub(x, d):
    return pltpu.roll(x, d, 1) if d else x


roll_sub.defvjp(lambda x, d: (roll_sub(x, d), None),
                lambda d, _, g: (roll_sub(g, (DIAG - d) % DIAG),))


def sigmoid(x):
    return 1.0 / (1.0 + jnp.exp(-x))


@jax.custom_vjp
def silu(x):
    return x * sigmoid(x)


def _silu_fwd(x):
    s = sigmoid(x)
    return x * s, (x, s)


silu.defvjp(_silu_fwd, lambda res, g: (g * (res[1] * (1.0 + res[0] * (1.0 - res[1]))),))


F32_TINY = 1.17549435e-38


def sigmoid_pair(x):
    e = jnp.exp(-jnp.abs(x))
    r = 1.0 / (1.0 + e)
    er = e * r
    pos = x >= 0.0
    return jnp.where(pos, r, er), jnp.where(pos, er, r)


def _forget_fwd(x, a):
    lb, one_m_lb = sigmoid_pair(a)
    sp, sn = sigmoid_pair(x)
    f = lb + one_m_lb * sp
    k = one_m_lb * sn
    return (jnp.log(jnp.maximum(f, F32_TINY)), k), (sp, sn, f, k, lb, one_m_lb)


def _forget_bwd(res, g):
    sp, sn, f, k, lb, one_m_lb = res
    g_lf, g_k = g
    t = jnp.where(f >= F32_TINY, g_lf / jnp.maximum(f, F32_TINY), 0.0) - g_k
    return (k * sp) * t, jnp.sum(sn * t, axis=0, keepdims=True) * (lb * one_m_lb)


@jax.custom_vjp
def forget_gate(x, a):
    return _forget_fwd(x, a)[0]


forget_gate.defvjp(_forget_fwd, _forget_bwd)


@jax.custom_vjp
def decayed(x, e):
    return (x * jnp.exp(e)).astype(BF16).astype(F32)


def _decayed_fwd(x, e):
    y = decayed(x, e)
    return y, (y, e)


decayed.defvjp(_decayed_fwd, lambda res, g: (g * jnp.exp(res[1]), g * res[0]))


def _row(x, r):
    shape = x.shape

    @jax.custom_vjp
    def take(x):
        return x[r:r + 1, :]

    take.defvjp(lambda x: (x[r:r + 1, :], None),
                lambda _, g: (jnp.where(lax.broadcasted_iota(jnp.int32, shape, 0) == r, g, 0.0),))
    return take(x)


def _rms(x):
    return lax.rsqrt(jnp.mean(x * x, axis=-1, keepdims=True) + NORM_EPS)


def _attn_group(qs, k_a, v_a, k_b, v_b, zs, sink_a, sink_b, bias, at_sink=None):
    def half(kh, vh, sink):
        s = mm_nn(qs, kh) + bias
        if at_sink is None:
            m = jnp.maximum(jnp.max(s, axis=-1, keepdims=True), jnp.max(sink, axis=-1, keepdims=True))
            p = jnp.exp(s - lax.stop_gradient(m))
            own = jnp.sum(jnp.exp(sink - lax.stop_gradient(m)), axis=-1, keepdims=True) * (1.0 / LANES)
            return mm_nt(p * (1.0 / (jnp.sum(p, axis=-1, keepdims=True) + own)), vh)
        s = jnp.where(at_sink, jnp.concatenate([sink, sink], axis=1), s)
        p = jnp.exp(s - jnp.max(s, axis=-1, keepdims=True))
        return mm_nt(jnp.where(at_sink, 0.0, p), vh) * (1.0 / jnp.sum(p, axis=-1, keepdims=True))

    return (half(k_a, v_a, sink_a) + half(k_b, v_b, sink_b)) * silu(zs)


SAFE_RANGE = 80.0


def _rec_front(qr, fr, l0, l1):
    lf, k = forget_gate(fr, l1 - l0)
    return silu(qr), k, lf


def _rec_tail(o, z, gw):
    return o * _rms(o) * gw * silu(z)


def _rec_margin(b):
    R = b.shape[0]
    mid, last = _row(b, R // 2 - 1), _row(b, R - 1)
    return jnp.minimum(mid, last - mid)


def _heads(x):
    w = x.shape[1] // REC_HEADS
    return [x[:, h * w:(h + 1) * w] for h in range(REC_HEADS)]


def _hdot(a, b, contract):
    return jnp.concatenate([_dot(ah, bh, contract) for ah, bh in zip(_heads(a), _heads(b))], axis=1)


@jax.custom_vjp
def hmm_nn(a, b):
    return _hdot(a, b, _NN)


hmm_nn.defvjp(lambda a, b: (_hdot(a, b, _NN), (a, b)),
              lambda res, g: (_hdot(g, res[1], _NT), _hdot(res[0], g, _TN)))


@jax.custom_vjp
def hmm_nt(a, b):
    return _hdot(a, b, _NT)


hmm_nt.defvjp(lambda a, b: (_hdot(a, b, _NT), (a, b)),
              lambda res, g: (_hdot(g, res[1], _NN), _hdot(g, res[0], _TN)))


@jax.custom_vjp
def hmm_tn(a, b):
    return _hdot(a, b, _TN)


hmm_tn.defvjp(lambda a, b: (_hdot(a, b, _TN), (a, b)),
              lambda res, g: (_hdot(res[1], g, _NT), _hdot(res[0], g, _NN)))


def _head_sums(x):
    return jnp.concatenate([jnp.broadcast_to(jnp.sum(xh, axis=-1, keepdims=True), xh.shape) for xh in _heads(x)],
                           axis=1)


@jax.custom_vjp
def head_sum(x):
    return _head_sums(x)


head_sum.defvjp(lambda x: (_head_sums(x), None), lambda _, g: (_head_sums(g),))


def _rec_cores_fast(q, k, v, b, S):
    R = q.shape[0]
    ri = lax.broadcasted_iota(jnp.int32, (R, REC_HEADS * R), 0)
    ci = lax.broadcasted_iota(jnp.int32, (R, REC_HEADS * R), 1) % R
    d = b - _row(b, R // 2 - 1)
    sc = jnp.where(ci < ri, hmm_nt(decayed(q, d), decayed(k, -d)), 0.0)
    o = hmm_nt(q * jnp.exp(b), S) + hmm_nn(sc, v) + head_sum(q * k) * v
    b_last = _row(b, R - 1)
    return o, S * jnp.exp(b_last) + hmm_tn(v, k * jnp.exp(b_last - b))


def _rec_tails(o, z, gw):
    return o * lax.rsqrt(head_sum(o * o) * (1.0 / REC_DIM) + NORM_EPS) * gw * silu(z)


def _rec_block_fast(qr, fr, v, z, S, l0, l1, gw):
    lf, k = forget_gate(fr, l1 - l0)
    o, S_new = _rec_cores_fast(silu(qr), k, v, cumsum_rows(lf), S)
    return _rec_tails(o, z, gw), S_new


def _rec_core_slow(q, k, v, b, S):
    R = q.shape[0]
    rows = lax.broadcasted_iota(jnp.int32, (R, REC_DIM), 0)

    o = mm_nt(q * jnp.exp(jnp.minimum(b, 0.0)), S)

    ri = lax.broadcasted_iota(jnp.int32, (R, R), 0)
    ci = lax.broadcasted_iota(jnp.int32, (R, R), 1)
    sc = jnp.zeros((R, R), F32)
    w = R
    while w > DIAG:
        h = w // 2
        b3 = b.reshape(R // w, w, REC_DIM)
        rin = lax.broadcasted_iota(jnp.int32, (R // w, w, REC_DIM), 1)
        mid = jnp.sum(jnp.where(rin == h - 1, b3, 0.0), axis=1, keepdims=True)
        fac = jnp.exp(jnp.minimum(jnp.where(rin >= h, b3 - mid, mid - b3), 0.0)).reshape(R, REC_DIM)
        upper = (rows % w) >= h
        s_w = mm_nt(jnp.where(upper, q * fac, 0.0), jnp.where(upper, 0.0, k * fac))
        sc = sc + jnp.where((ri // w) == (ci // w), s_w, 0.0)
        w = h
    o = o + mm_nn(sc, v)

    g = R // DIAG
    q3, k3, v3, b3 = (t.reshape(g, DIAG, REC_DIM) for t in (q, k, v, b))
    rin = lax.broadcasted_iota(jnp.int32, (g, DIAG, 1), 1)
    od = jnp.zeros((g, DIAG, REC_DIM), F32)
    for d in range(DIAG):
        e = jnp.exp(jnp.minimum(b3 - roll_sub(b3, d), 0.0))
        sd = jnp.sum(q3 * roll_sub(k3, d) * e, axis=-1, keepdims=True)
        od = od + jnp.where(rin >= d, sd, 0.0) * roll_sub(v3, d)
    o = o + od.reshape(R, REC_DIM)

    b_last = _row(b, R - 1)
    return o, S * jnp.exp(jnp.minimum(b_last, 0.0)) + mm_tn(v, k * jnp.exp(jnp.minimum(b_last - b, 0.0)))


def _rec_head(core, qr, fr, v, z, S, l0, l1, gw):
    q, k, lf = _rec_front(qr, fr, l0, l1)
    o, S_new = core(q, k, v, cumsum_rows(lf), S)
    return _rec_tail(o, z, gw), S_new


ANGLE_COLS = 3 * ROPE_DIM


def _rope_angles(positions):
    half = ROPE_DIM // 2
    inv_freq = ROPE_THETA ** (-(jnp.arange(half, dtype=F32) * 2.0 / ROPE_DIM))
    ang = positions.astype(F32).reshape(-1, 1) * inv_freq
    cs = jnp.concatenate([jnp.cos(ang), jnp.sin(ang)], axis=-1)
    hi = cs.astype(BF16)
    rest = cs - hi.astype(F32)
    mid = rest.astype(BF16)
    return jnp.concatenate([hi, mid, (rest - mid.astype(F32)).astype(BF16)], axis=-1)


def _rope_tables(pieces):
    half = ROPE_DIM // 2
    r = lax.broadcasted_iota(jnp.int32, (ANGLE_COLS, 3 * LANES), 0) % ROPE_DIM
    c = lax.broadcasted_iota(jnp.int32, (ANGLE_COLS, 3 * LANES), 1)
    table, j = c // LANES, c % HEAD_DIM
    angle, low = j % half, j < half
    plus = ((table == 0) & (j < ROPE_DIM) & (r == angle)) | ((table == 1) & (j >= half) & (j < ROPE_DIM)
                                                                & (r == half + angle))
    minus = (table == 2) & low & (r == half + angle)
    pick = jnp.where(plus, 1.0, jnp.where(minus, -1.0, 0.0)).astype(BF16)
    out = jnp.dot(pieces, pick, preferred_element_type=F32)
    lane = lax.broadcasted_iota(jnp.int32, (1, LANES), 1) % HEAD_DIM
    return out[:, :LANES] + jnp.where(lane < ROPE_DIM, 0.0, 1.0), out[:, LANES:2 * LANES], out[:, 2 * LANES:]


def _rope(x, cos_t, sin_a, sin_b):
    half = ROPE_DIM // 2
    return x * cos_t + pltpu.roll(x, half, 1) * sin_a + pltpu.roll(x, LANES - half, 1) * sin_b


def _rope_transposed(g, cos_t, sin_a, sin_b):
    half = ROPE_DIM // 2
    return g * cos_t + pltpu.roll(g * sin_a, LANES - half, 1) + pltpu.roll(g * sin_b, half, 1)


def _row_spec(tm, width):
    return pl.BlockSpec((tm, width), lambda i: (i, 0))


def _weight_spec(shape):
    return pl.BlockSpec(shape, lambda *_: (0,) * len(shape), pipeline_mode=pl.Buffered(1))


def _full_spec(shape):
    return pl.BlockSpec(shape, lambda *_: (0,) * len(shape))


def attn_in_proj(x, w_pre, w_in, b_in, angles, tm=1024, to_bf16=()):
    n = x.shape[0]
    tm = min(tm, n)
    nc = len(to_bf16)

    def body(*refs):
        x_ref, wp_ref, w_ref, b_ref, cs_ref = refs[:5]
        h_ref, q_ref, k_ref, v_ref, z_ref = refs[5 + nc:10 + nc]

        @pl.when(pl.program_id(0) == 0)
        def _():
            for src, dst in zip(refs[5:5 + nc], refs[10 + nc:]):
                dst[...] = src[...].astype(BF16)

        xv = x_ref[...]
        h = (xv * _rms(xv) * wp_ref[...]).astype(BF16)
        h_ref[...] = h
        proj = jnp.dot(h, w_ref[...], preferred_element_type=F32) + b_ref[...]
        tabs = _rope_tables(cs_ref[...])
        for s in range(D_MODEL // LANES):
            sl = slice(s * LANES, (s + 1) * LANES)
            q_ref[:, sl] = _rope(proj[:, sl] * (HEAD_DIM ** -0.5), *tabs).astype(BF16)
        k_ref[...] = _rope(proj[:, D_MODEL:D_MODEL + KV_WIDTH], *tabs).astype(BF16)
        v_ref[...] = proj[:, D_MODEL + KV_WIDTH:D_MODEL + 2 * KV_WIDTH].astype(BF16)
        z_ref[...] = proj[:, D_MODEL + 2 * KV_WIDTH:]

    out = pl.pallas_call(
        body, name="attn_in_proj", grid=(n // tm,),
        in_specs=[_row_spec(tm, D_MODEL), _full_spec((1, D_MODEL)), _weight_spec((D_MODEL, ATTN_IN)),
                  _full_spec((1, ATTN_IN)), _row_spec(tm, ANGLE_COLS)] + [_weight_spec(a.shape) for a in to_bf16],
        out_specs=[_row_spec(tm, D_MODEL), _row_spec(tm, D_MODEL), _row_spec(tm, KV_WIDTH),
                   _row_spec(tm, KV_WIDTH), _row_spec(tm, D_MODEL)] + [_full_spec(a.shape) for a in to_bf16],
        out_shape=[jax.ShapeDtypeStruct((n, D_MODEL), BF16), jax.ShapeDtypeStruct((n, D_MODEL), BF16),
                   jax.ShapeDtypeStruct((n, KV_WIDTH), BF16), jax.ShapeDtypeStruct((n, KV_WIDTH), BF16),
                   jax.ShapeDtypeStruct((n, D_MODEL), F32)] + [jax.ShapeDtypeStruct(a.shape, BF16) for a in to_bf16],
        compiler_params=_cparams(1),
    )(x, w_pre, w_in, b_in, angles, *to_bf16)
    return out[:5], out[5:]


def _column_blocks(w):
    if len(w.shape) == 2:
        return [slice(0, w.shape[1])], lambda ref, s: ref[...]
    width = w.shape[2]
    return [slice(s * width, (s + 1) * width) for s in range(w.shape[0])], lambda ref, s: ref[s]


def rec_in_proj(x, w_pre, w_in, tm=1024):
    n = x.shape[0]
    tm = min(tm, n)
    columns, block = _column_blocks(w_in)

    def body(x_ref, wp_ref, w_ref, h_ref, p_ref):
        xv = x_ref[...]
        h = (xv * _rms(xv) * wp_ref[...]).astype(BF16)
        h_ref[...] = h
        for s, cols in enumerate(columns):
            p_ref[:, cols] = jnp.dot(h, block(w_ref, s), preferred_element_type=F32)

    return pl.pallas_call(
        body, name="rec_in_proj", grid=(n // tm,),
        in_specs=[_row_spec(tm, D_MODEL), _full_spec((1, D_MODEL)), _weight_spec(w_in.shape)],
        out_specs=[_row_spec(tm, D_MODEL), _row_spec(tm, REC_IN)],
        out_shape=[jax.ShapeDtypeStruct((n, D_MODEL), BF16), jax.ShapeDtypeStruct((n, REC_IN), F32)],
        compiler_params=_cparams(1),
    )(x, w_pre, w_in)


def out_proj(og, w_out, b_out, x_res, w_post, tm=1024):
    n = og.shape[0]
    tm = min(tm, n)

    def body(og_ref, w_ref, b_ref, x_ref, wp_ref, y_ref, xo_ref):
        y = jnp.dot(og_ref[...], w_ref[...], preferred_element_type=F32) + b_ref[...]
        y_ref[...] = y.astype(BF16)
        xo_ref[...] = x_ref[...] + y * _rms(y) * wp_ref[...]

    return pl.pallas_call(
        body, name="out_proj", grid=(n // tm,),
        in_specs=[_row_spec(tm, D_MODEL), _weight_spec((D_MODEL, D_MODEL)), _full_spec((1, D_MODEL)),
                  _row_spec(tm, D_MODEL), _full_spec((1, D_MODEL))],
        out_specs=[_row_spec(tm, D_MODEL), _row_spec(tm, D_MODEL)],
        out_shape=[jax.ShapeDtypeStruct((n, D_MODEL), BF16), jax.ShapeDtypeStruct((n, D_MODEL), F32)],
        compiler_params=_cparams(1),
    )(og, w_out, b_out, x_res, w_post)


def _post_norm_bwd(g, y, w_post):
    rstd = _rms(y)
    yn = y * rstd
    gw = g * w_post
    return rstd * (gw - yn * jnp.mean(gw * yn, axis=-1, keepdims=True)), jnp.sum(g * yn, axis=0, keepdims=True)


def out_proj_loss_bwd(og, w_out, x_res, w_post, target, tm=1024):
    n = og.shape[0]
    tm = min(tm, n)
    steps = n // tm

    def body(og_ref, w_ref, x_ref, wp_ref, t_ref, dx_ref, l_ref, dog_ref, dw_ref, dwp_ref, dwb_ref):
        @pl.when(pl.program_id(0) == 0)
        def _():
            l_ref[...] = jnp.zeros_like(l_ref)
            dw_ref[...] = jnp.zeros_like(dw_ref)
            dwp_ref[...] = jnp.zeros_like(dwp_ref)

        og_tile = og_ref[...]
        y = jnp.dot(og_tile, w_ref[...], preferred_element_type=F32)
        err = x_ref[...] + y * _rms(y) * wp_ref[...] - t_ref[...]
        g = err * (1.0 / D_MODEL)
        dx_ref[...] = g
        l_ref[...] += jnp.sum(err * err, axis=0, keepdims=True)
        dy, dwp = _post_norm_bwd(g, y, wp_ref[...])
        dwp_ref[...] += dwp
        dyb = dy.astype(BF16)
        dog_ref[...] = _dot(dyb, w_ref[...], _NT).astype(BF16)
        dw_ref[...] += _dot(og_tile, dyb, _TN)

        @pl.when(pl.program_id(0) == steps - 1)
        def _():
            dwb_ref[...] = dw_ref[...].astype(BF16)

    return pl.pallas_call(
        body, name="out_proj_loss_bwd", grid=(steps,),
        in_specs=[_row_spec(tm, D_MODEL), _weight_spec((D_MODEL, D_MODEL)), _row_spec(tm, D_MODEL),
                  _full_spec((1, D_MODEL)), _row_spec(tm, D_MODEL)],
        out_specs=[_row_spec(tm, D_MODEL), _full_spec((1, D_MODEL)), _row_spec(tm, D_MODEL),
                   _full_spec((D_MODEL, D_MODEL)), _full_spec((1, D_MODEL)), _full_spec((D_MODEL, D_MODEL))],
        out_shape=[jax.ShapeDtypeStruct((n, D_MODEL), F32), jax.ShapeDtypeStruct((1, D_MODEL), F32),
                   jax.ShapeDtypeStruct((n, D_MODEL), BF16), jax.ShapeDtypeStruct((D_MODEL, D_MODEL), F32),
                   jax.ShapeDtypeStruct((1, D_MODEL), F32), jax.ShapeDtypeStruct((D_MODEL, D_MODEL), BF16)],
        compiler_params=_cparams(1),
    )(og, w_out, x_res, w_post, target)


def out_proj_bwd(dxo, y, og, w_out, w_post, tm=1024):
    n = og.shape[0]
    tm = min(tm, n)
    steps = n // tm

    def body(g_ref, y_ref, og_ref, w_ref, wp_ref, dog_ref, dw_ref, db_ref, dwp_ref, dwb_ref):
        @pl.when(pl.program_id(0) == 0)
        def _():
            dw_ref[...] = jnp.zeros_like(dw_ref)
            db_ref[...] = jnp.zeros_like(db_ref)
            dwp_ref[...] = jnp.zeros_like(dwp_ref)

        dy, dwp = _post_norm_bwd(g_ref[...], y_ref[...].astype(F32), wp_ref[...])
        dwp_ref[...] += dwp
        db_ref[...] += jnp.sum(dy, axis=0, keepdims=True)
        dyb = dy.astype(BF16)
        dog_ref[...] = _dot(dyb, w_ref[...], _NT).astype(BF16)
        dw_ref[...] += _dot(og_ref[...], dyb, _TN)

        @pl.when(pl.program_id(0) == steps - 1)
        def _():
            dwb_ref[...] = dw_ref[...].astype(BF16)

    return pl.pallas_call(
        body, name="out_proj_bwd", grid=(steps,),
        in_specs=[_row_spec(tm, D_MODEL), _row_spec(tm, D_MODEL), _row_spec(tm, D_MODEL),
                  _weight_spec((D_MODEL, D_MODEL)), _full_spec((1, D_MODEL))],
        out_specs=[_row_spec(tm, D_MODEL), _full_spec((D_MODEL, D_MODEL)), _full_spec((1, D_MODEL)),
                   _full_spec((1, D_MODEL)), _full_spec((D_MODEL, D_MODEL))],
        out_shape=[jax.ShapeDtypeStruct((n, D_MODEL), BF16), jax.ShapeDtypeStruct((D_MODEL, D_MODEL), F32),
                   jax.ShapeDtypeStruct((1, D_MODEL), F32), jax.ShapeDtypeStruct((1, D_MODEL), F32),
                   jax.ShapeDtypeStruct((D_MODEL, D_MODEL), BF16)],
        compiler_params=_cparams(1),
    )(dxo, y, og, w_out, w_post)


def in_proj_bwd_x(dproj, w_in, x, w_pre, dxo, tm=1024, scatter=(), slot_sums=()):
    n, p = dproj.shape
    tm = min(tm, n)
    steps = n // tm
    ns, nsum = len(scatter), len(slot_sums)
    columns, block = _column_blocks(w_in)

    def body(*refs):
        dp_ref, w_ref, x_ref, wp_ref, g_ref = refs[:5]
        outs = 5 + ns + nsum
        dx_ref, dwp_ref = refs[outs:outs + 2]
        exchange = (refs[5:5 + ns], refs[outs + 2:outs + 2 + ns]) + tuple(refs[outs + 2 + ns + nsum:])

        for slots_ref, sum_ref in zip(refs[5 + ns:outs], refs[outs + 2 + ns:outs + 2 + ns + nsum]):
            sum_ref[...] = functools.reduce(jnp.add, [slots_ref[t].astype(F32) for t in range(slots_ref.shape[0])])

        @pl.when(pl.program_id(0) == 0)
        def _():
            dwp_ref[...] = jnp.zeros_like(dwp_ref)
            if ns:
                _scatter_start(*exchange)

        dh = functools.reduce(jnp.add, [_dot(dp_ref[:, cols], block(w_ref, s), _NT)
                                        for s, cols in enumerate(columns)])
        xv = x_ref[...]
        rstd = _rms(xv)
        xn = xv * rstd
        gw = dh * wp_ref[...]
        dwp_ref[...] += jnp.sum(dh * xn, axis=0, keepdims=True)
        dx_ref[...] = rstd * (gw - xn * jnp.mean(gw * xn, axis=-1, keepdims=True)) + g_ref[...]

        if ns:
            @pl.when(pl.program_id(0) == steps - 1)
            def _():
                _scatter_finish(*exchange)

    out = pl.pallas_call(
        body, name=f"in_proj_bwd_x_{p}", grid=(steps,),
        in_specs=[_row_spec(tm, p), _weight_spec(w_in.shape), _row_spec(tm, D_MODEL), _full_spec((1, D_MODEL)),
                  _row_spec(tm, D_MODEL)] + [_ANY] * ns
        + [pl.BlockSpec((a.shape[0], a.shape[1] // steps, a.shape[2]), lambda i: (0, i, 0)) for a in slot_sums],
        out_specs=[_row_spec(tm, D_MODEL), _full_spec((1, D_MODEL))] + [_ANY] * ns
        + [_row_spec(a.shape[1] // steps, a.shape[2]) for a in slot_sums],
        out_shape=[jax.ShapeDtypeStruct((n, D_MODEL), F32), jax.ShapeDtypeStruct((1, D_MODEL), F32)]
        + [jax.ShapeDtypeStruct(a.shape, a.dtype) for a in scatter]
        + [jax.ShapeDtypeStruct(a.shape[1:], F32) for a in slot_sums],
        scratch_shapes=_scatter_sems(ns) if ns else [],
        compiler_params=_cparams(1),
    )(dproj, w_in, x, w_pre, dxo, *scatter, *slot_sums)
    return out[0], out[1], out[2:2 + ns], out[2 + ns:]


def in_proj_bwd_w(h, dproj, tm=1024, as_shards=False, kv=None):
    n, p = dproj.shape
    chunk = p // (4 if p % 4096 == 0 else 3)
    tm = min(tm, n)
    steps = n // tm
    shard = p // N_CHIPS
    n_kv = 0 if kv is None else 3
    kv_from, kv_to = D_MODEL, D_MODEL + 2 * KV_WIDTH

    def body(*refs):
        h_ref, dp_ref = refs[:2]
        dw_ref, db_ref = refs[2 + n_kv:4 + n_kv]
        scratch = refs[4 + n_kv + (kv is not None):]
        acc_scr, sem, staging = scratch[0], scratch[1], scratch[2:]
        i = pl.program_id(0)

        @pl.when(i == 0)
        def _():
            acc_scr[...] = jnp.zeros_like(acc_scr)
            db_ref[...] = jnp.zeros_like(db_ref)

        if kv is not None:
            dk_ref, dv_ref, cs_ref, kv_ref = refs[2], refs[3], refs[4], refs[4 + n_kv]
            made = jnp.concatenate([_rope_transposed(dk_ref[...].T, *_rope_tables(cs_ref[...])), dv_ref[...].T],
                                   axis=1).astype(BF16)
            kv_ref[...] = made

        def columns(c0):
            if kv is None or c0 + chunk <= kv_from or c0 >= kv_to:
                return dp_ref[:, c0:c0 + chunk]
            return jnp.concatenate([dp_ref[:, c0:kv_from], made, dp_ref[:, kv_to:c0 + chunk]], axis=1)

        ht = h_ref[...].T
        for c0 in range(0, p, chunk):
            dp = columns(c0)
            acc_scr[:, c0:c0 + chunk] += jnp.dot(ht, dp, preferred_element_type=F32)
            db_ref[:, c0:c0 + chunk] += jnp.sum(dp.astype(F32), axis=0, keepdims=True)

        @pl.when(i == steps - 1)
        def _():
            if as_shards:
                for s in range(N_CHIPS):
                    staging[0][...] = acc_scr[:, s * shard:(s + 1) * shard].astype(BF16)
                    out = pltpu.make_async_copy(staging[0], dw_ref.at[s], sem)
                    out.start()
                    out.wait()
            else:
                out = pltpu.make_async_copy(acc_scr, dw_ref, sem)
                out.start()
                out.wait()

    dw_shape = jax.ShapeDtypeStruct((N_CHIPS, D_MODEL, shard), BF16) if as_shards else (
        jax.ShapeDtypeStruct((D_MODEL, p), F32))
    in_specs = [_row_spec(tm, D_MODEL), _row_spec(tm, p)]
    out_specs = [_ANY, _full_spec((1, p))]
    out_shape = [dw_shape, jax.ShapeDtypeStruct((1, p), F32)]
    if kv is not None:
        columns_t = pl.BlockSpec((KV_WIDTH, tm), lambda i: (0, i))
        in_specs += [columns_t, columns_t, _row_spec(tm, ANGLE_COLS)]
        out_specs.append(pl.BlockSpec((tm, kv_to - kv_from), lambda i: (i, kv_from // (kv_to - kv_from))))
        out_shape.append(jax.ShapeDtypeStruct(dproj.shape, dproj.dtype))
    return pl.pallas_call(
        body, name=f"in_proj_bwd_w_{p}", grid=(steps,),
        in_specs=in_specs, out_specs=out_specs, out_shape=out_shape,
        scratch_shapes=[pltpu.VMEM((D_MODEL, p), F32), pltpu.SemaphoreType.DMA]
        + ([pltpu.VMEM((D_MODEL, shard), BF16)] if as_shards else []),
        input_output_aliases={1: 2} if kv is not None else {},
        compiler_params=_cparams(1),
    )(h, dproj, *(kv or ()))


PAIRS = GROUP // 2
GROUP_ROWS = PAIRS * ATTN_BLOCK
MASKED = -1e30


def _kv_windows(k_ref, v_ref, i):
    ps = pl.multiple_of(jnp.maximum(i - 1, 0) * ATTN_BLOCK, ATTN_BLOCK)
    cs = pl.multiple_of(i * ATTN_BLOCK, ATTN_BLOCK)
    kw = jnp.concatenate([k_ref[pl.ds(ps, ATTN_BLOCK), :], k_ref[pl.ds(cs, ATTN_BLOCK), :]], axis=0)
    vw = jnp.concatenate([v_ref[pl.ds(ps, ATTN_BLOCK), :], v_ref[pl.ds(cs, ATTN_BLOCK), :]], axis=0)
    return kw.astype(F32).T, vw.astype(F32).T, ps, cs


def _low_rows(shape):
    return lax.broadcasted_iota(jnp.int32, shape, 0) < HEAD_DIM


def _spread(w, kvh):
    low = _low_rows(w.shape)
    swapped = pltpu.roll(w, HEAD_DIM, 0)
    if kvh == 0:
        return jnp.where(low, w, 0.0), jnp.where(low, 0.0, swapped)
    return jnp.where(low, swapped, 0.0), jnp.where(low, 0.0, w)


def _unspread(d_a, d_b, kvh):
    low = _low_rows(d_a.shape)
    if kvh == 0:
        return jnp.where(low, d_a + pltpu.roll(d_b, HEAD_DIM, 0), 0.0)
    return jnp.where(low, 0.0, pltpu.roll(d_a, HEAD_DIM, 0) + d_b)


def _stack_pairs(ref, kvh):
    return jnp.concatenate([ref[:, (kvh * PAIRS + j) * LANES:(kvh * PAIRS + j + 1) * LANES] for j in range(PAIRS)],
                           axis=0)


def _fill_bias(bias_scr):
    shape = (GROUP_ROWS, 2 * ATTN_BLOCK)
    r = lax.broadcasted_iota(jnp.int32, shape, 0) % ATTN_BLOCK
    c = lax.broadcasted_iota(jnp.int32, shape, 1)
    in_cur = (c >= ATTN_BLOCK) & ((c - ATTN_BLOCK) <= r)
    in_prev = (c < ATTN_BLOCK) & (c > r)
    bias_scr[0] = jnp.where(in_cur, 0.0, MASKED)
    bias_scr[1] = jnp.where(in_cur | in_prev, 0.0, MASKED)
    bias_scr[2] = jnp.where(c == r, 1.0, 0.0)


N_BIAS_TABLES = 3


def _sink_table(sinks):
    t = jnp.transpose(sinks.reshape(N_KV_HEADS, PAIRS, 2), (0, 2, 1))
    return jnp.broadcast_to(t[:, :, :, None, None], (N_KV_HEADS, 2, PAIRS, ATTN_BLOCK, LANES)).reshape(
        N_KV_HEADS, 2, GROUP_ROWS, LANES)


def attn_fwd(q, k, v, z, sink_tab, batch, seq, gather=()):
    nb = seq // ATTN_BLOCK
    ng = len(gather)

    def body(*refs):
        q_ref, k_ref, v_ref, z_ref, s_ref = refs[:5]
        og_ref, bias_scr = refs[5 + ng], refs[6 + 2 * ng]
        exchange = (refs[5:5 + ng], refs[6 + ng:6 + 2 * ng]) + tuple(refs[7 + 2 * ng:])
        b, i = pl.program_id(0), pl.program_id(1)

        @pl.when((b == 0) & (i == 0))
        def _():
            _fill_bias(bias_scr)
            if ng:
                _gather_start(*exchange)

        kw, vw, _, _ = _kv_windows(k_ref, v_ref, i)
        bias, at_sink = bias_scr[jnp.minimum(i, 1)], bias_scr[2] > 0.5
        for kvh in range(N_KV_HEADS):
            k_a, k_b = _spread(kw, kvh)
            v_a, v_b = _spread(vw, kvh)
            og = _attn_group(_stack_pairs(q_ref, kvh), k_a, v_a, k_b, v_b, _stack_pairs(z_ref, kvh),
                             s_ref[kvh, 0], s_ref[kvh, 1], bias, at_sink)
            for j in range(PAIRS):
                og_ref[:, (kvh * PAIRS + j) * LANES:(kvh * PAIRS + j + 1) * LANES] = (
                    og[j * ATTN_BLOCK:(j + 1) * ATTN_BLOCK].astype(BF16))

        if ng:
            @pl.when((b == batch - 1) & (i == nb - 1))
            def _():
                _gather_finish(*exchange)

    blk = lambda w: pl.BlockSpec((ATTN_BLOCK, w), lambda b, i: (b * nb + i, 0))
    seq_spec = pl.BlockSpec((seq, KV_WIDTH), lambda b, i: (b, 0))
    out = pl.pallas_call(
        body, name="attn_fwd", grid=(batch, nb),
        in_specs=[blk(D_MODEL), seq_spec, seq_spec, blk(D_MODEL), _full_spec(sink_tab.shape)] + [_ANY] * ng,
        out_specs=[blk(D_MODEL)] + [_ANY] * ng,
        out_shape=[jax.ShapeDtypeStruct((batch * seq, D_MODEL), BF16)]
        + [jax.ShapeDtypeStruct((N_CHIPS,) + a.shape, a.dtype) for a in gather],
        scratch_shapes=[pltpu.VMEM((N_BIAS_TABLES, GROUP_ROWS, 2 * ATTN_BLOCK), F32)] + (_gather_sems(ng) if ng else []),
        compiler_params=_cparams(2),
    )(q, k, v, z, sink_tab, *gather)
    return out[0], out[1:]


def attn_bwd(q, k, v, z, sink_tab, dog, angles, batch, seq, scatter=()):
    nb = seq // ATTN_BLOCK
    ns = len(scatter)

    def body(*refs):
        q_ref, k_ref, v_ref, z_ref, s_ref, g_ref, cs_ref = refs[:7]
        dp_ref, dk_ref, dv_ref, ds_ref = refs[7 + ns:11 + ns]
        bias_scr = refs[11 + 2 * ns]
        exchange = (refs[7:7 + ns], refs[11 + ns:11 + 2 * ns]) + tuple(refs[12 + 2 * ns:])
        b, i = pl.program_id(0), pl.program_id(1)

        @pl.when((b == 0) & (i == 0))
        def _():
            _fill_bias(bias_scr)
            ds_ref[...] = jnp.zeros_like(ds_ref)
            if ns:
                _scatter_start(*exchange)

        @pl.when(i == 0)
        def _():
            dk_ref[...] = jnp.zeros_like(dk_ref)
            dv_ref[...] = jnp.zeros_like(dv_ref)

        kw, vw, ps, cs = _kv_windows(k_ref, v_ref, i)
        bias = bias_scr[jnp.minimum(i, 1)]
        tabs = _rope_tables(cs_ref[...])
        dkw = jnp.zeros_like(kw)
        dvw = jnp.zeros_like(vw)
        for kvh in range(N_KV_HEADS):
            k_a, k_b = _spread(kw, kvh)
            v_a, v_b = _spread(vw, kvh)
            _, vjp = jax.vjp(functools.partial(_attn_group, bias=bias), _stack_pairs(q_ref, kvh).astype(F32),
                             k_a, v_a, k_b, v_b, _stack_pairs(z_ref, kvh), s_ref[kvh, 0], s_ref[kvh, 1])
            dqs, dk_a, dv_a, dk_b, dv_b, dzs, ds_a, ds_b = vjp(_stack_pairs(g_ref, kvh).astype(F32))
            dkw = dkw + _unspread(dk_a, dk_b, kvh)
            dvw = dvw + _unspread(dv_a, dv_b, kvh)
            ds_ref[kvh, 0] += jnp.sum(ds_a.reshape(PAIRS, ATTN_BLOCK, LANES), axis=1)
            ds_ref[kvh, 1] += jnp.sum(ds_b.reshape(PAIRS, ATTN_BLOCK, LANES), axis=1)
            for j in range(PAIRS):
                rows = slice(j * ATTN_BLOCK, (j + 1) * ATTN_BLOCK)
                col = (kvh * PAIRS + j) * LANES
                dp_ref[:, col:col + LANES] = _rope_transposed(dqs[rows] * (HEAD_DIM ** -0.5), *tabs).astype(BF16)
                zc = D_MODEL + 2 * KV_WIDTH + col
                dp_ref[:, zc:zc + LANES] = dzs[rows].astype(BF16)
        dp_ref[:, D_MODEL:D_MODEL + 2 * KV_WIDTH] = jnp.zeros((ATTN_BLOCK, 2 * KV_WIDTH), BF16)
        dk_ref[:, pl.ds(ps, ATTN_BLOCK)] += dkw[:, :ATTN_BLOCK]
        dk_ref[:, pl.ds(cs, ATTN_BLOCK)] += dkw[:, ATTN_BLOCK:]
        dv_ref[:, pl.ds(ps, ATTN_BLOCK)] += dvw[:, :ATTN_BLOCK]
        dv_ref[:, pl.ds(cs, ATTN_BLOCK)] += dvw[:, ATTN_BLOCK:]

        if ns:
            @pl.when((b == batch - 1) & (i == nb - 1))
            def _():
                _scatter_finish(*exchange)

    blk = lambda w: pl.BlockSpec((ATTN_BLOCK, w), lambda b, i: (b * nb + i, 0))
    seq_spec = pl.BlockSpec((seq, KV_WIDTH), lambda b, i: (b, 0))
    seq_spec_t = pl.BlockSpec((KV_WIDTH, seq), lambda b, i: (0, b))
    n = batch * seq
    ds_shape = (N_KV_HEADS, 2, PAIRS, LANES)
    out = pl.pallas_call(
        body, name="attn_bwd", grid=(batch, nb),
        in_specs=[blk(D_MODEL), seq_spec, seq_spec, blk(D_MODEL), _full_spec(sink_tab.shape), blk(D_MODEL)]
        + [blk(ANGLE_COLS)] + [_ANY] * ns,
        out_specs=[blk(ATTN_IN), seq_spec_t, seq_spec_t, _full_spec(ds_shape)] + [_ANY] * ns,
        out_shape=[jax.ShapeDtypeStruct((n, ATTN_IN), BF16), jax.ShapeDtypeStruct((KV_WIDTH, n), F32),
                   jax.ShapeDtypeStruct((KV_WIDTH, n), F32), jax.ShapeDtypeStruct(ds_shape, F32)]
        + [jax.ShapeDtypeStruct(a.shape, a.dtype) for a in scatter],
        scratch_shapes=[pltpu.VMEM((N_BIAS_TABLES, GROUP_ROWS, 2 * ATTN_BLOCK), F32)] + (_scatter_sems(ns) if ns else []),
        compiler_params=_cparams(2),
    )(q, k, v, z, sink_tab, dog, angles, *scatter)
    return out[0], out[1], out[2], out[3], out[4:]


def rec_fwd(proj, lb_logits, gnorm_w, batch, seq):
    nblk = seq // REC_BLOCK

    def body(p_ref, lb_ref, gw_ref, og_ref, st_ref, safe_ref, s_scr):
        @pl.when(pl.program_id(1) == 0)
        def _():
            s_scr[...] = jnp.zeros_like(s_scr)

        S = s_scr[...]
        st_ref[0] = S
        qr, fr, v, z = (p_ref[:, part * D_MODEL:(part + 1) * D_MODEL] for part in range(4))
        lf, k = forget_gate(fr, lb_ref[1:2, :] - lb_ref[0:1, :])
        q, b = silu(qr), cumsum_rows(lf)
        safe = jnp.min(_rec_margin(b)) >= -SAFE_RANGE

        gate = gw_ref[...] * silu(z)
        safe_ref[0] = jnp.full((REC_HEADS, LANES), safe.astype(F32))

        def store(o, S_new):
            og_ref[...] = (o * lax.rsqrt(head_sum(o * o) * (1.0 / REC_DIM) + NORM_EPS) * gate).astype(BF16)
            s_scr[...] = S_new

        @pl.when(safe)
        def _():
            store(*_rec_cores_fast(q, k, v, b, S))

        @pl.when(jnp.logical_not(safe))
        def _():
            outs = [_rec_core_slow(*args) for args in zip(*(_heads(t) for t in (q, k, v, b, S)))]
            store(*(jnp.concatenate(parts, axis=1) for parts in zip(*outs)))

    blk = lambda w: pl.BlockSpec((REC_BLOCK, w), lambda b, j: (b * nblk + j, 0))
    st_spec = pl.BlockSpec((1, REC_DIM, D_MODEL), lambda b, j: (b * nblk + j, 0, 0))
    safe_spec = pl.BlockSpec((1, REC_HEADS, LANES), lambda b, j: (b * nblk + j, 0, 0))
    return pl.pallas_call(
        body, name="rec_fwd", grid=(batch, nblk),
        in_specs=[blk(REC_IN), _full_spec((2, D_MODEL)), _full_spec((1, D_MODEL))],
        out_specs=[blk(D_MODEL), st_spec, safe_spec],
        out_shape=[jax.ShapeDtypeStruct((batch * seq, D_MODEL), BF16),
                   jax.ShapeDtypeStruct((batch * nblk, REC_DIM, D_MODEL), F32),
                   jax.ShapeDtypeStruct((batch * nblk, REC_HEADS, LANES), F32)],
        scratch_shapes=[pltpu.VMEM((REC_DIM, D_MODEL), F32)],
        compiler_params=_cparams(2),
    )(proj, lb_logits, jnp.tile(gnorm_w, (1, REC_HEADS)))


def rec_bwd(proj, states, safe, lb_logits, gnorm_w, dog, batch, seq):
    nblk = seq // REC_BLOCK

    def body(p_ref, st_ref, safe_ref, lb_ref, gw_ref, g_ref, dp_ref, dlb_ref, dgw_ref, ds_scr):
        @pl.when((pl.program_id(0) == 0) & (pl.program_id(1) == 0))
        def _():
            dlb_ref[...] = jnp.zeros_like(dlb_ref)
            dgw_ref[...] = jnp.zeros_like(dgw_ref)

        @pl.when(pl.program_id(1) == 0)
        def _():
            ds_scr[...] = jnp.zeros_like(ds_scr)

        def load():
            primals = tuple(p_ref[:, part * D_MODEL:(part + 1) * D_MODEL] for part in range(4)) + (
                st_ref[0], lb_ref[0:1, :], lb_ref[1:2, :], gw_ref[...])
            return primals, (g_ref[...].astype(F32), ds_scr[...])

        def store(dqr, dfr, dv, dz, dS, dl0, dl1, dgw):
            for part, val in enumerate((dqr, dfr, dv, dz)):
                dp_ref[:, part * D_MODEL:(part + 1) * D_MODEL] = val.astype(BF16)
            ds_scr[...] = dS
            dlb_ref[0:1, :] += dl0
            dlb_ref[1:2, :] += dl1
            dgw_ref[...] += functools.reduce(jnp.add, _heads(dgw))

        fast = jnp.max(safe_ref[0]) > 0.5

        @pl.when(fast)
        def _():
            primals, cotangents = load()
            store(*jax.vjp(_rec_block_fast, *primals)[1](cotangents))

        @pl.when(jnp.logical_not(fast))
        def _():
            primals, cotangents = load()
            outs = [jax.vjp(functools.partial(_rec_head, _rec_core_slow), *args)[1](cts)
                    for args, cts in zip(zip(*(_heads(t) for t in primals)), zip(*(_heads(t) for t in cotangents)))]
            store(*(jnp.concatenate(parts, axis=1) for parts in zip(*outs)))

    blk = lambda w: pl.BlockSpec((REC_BLOCK, w), lambda b, j: (b * nblk + nblk - 1 - j, 0))
    st_spec = pl.BlockSpec((1, REC_DIM, D_MODEL), lambda b, j: (b * nblk + nblk - 1 - j, 0, 0))
    safe_spec = pl.BlockSpec((1, REC_HEADS, LANES), lambda b, j: (b * nblk + nblk - 1 - j, 0, 0))
    return pl.pallas_call(
        body, name="rec_bwd", grid=(batch, nblk),
        in_specs=[blk(REC_IN), st_spec, safe_spec, _full_spec((2, D_MODEL)), _full_spec((1, D_MODEL)),
                  blk(D_MODEL)],
        out_specs=[blk(REC_IN), _full_spec((2, D_MODEL)), _full_spec((1, REC_DIM))],
        out_shape=[jax.ShapeDtypeStruct((batch * seq, REC_IN), BF16), jax.ShapeDtypeStruct((2, D_MODEL), F32),
                   jax.ShapeDtypeStruct((1, REC_DIM), F32)],
        scratch_shapes=[pltpu.VMEM((REC_DIM, D_MODEL), F32)],
        compiler_params=_cparams(2),
    )(proj, states, safe, lb_logits, jnp.tile(gnorm_w, (1, REC_HEADS)), dog)


_ANY = pl.BlockSpec(memory_space=pl.ANY)


def _chip_peers():
    x, y, c = lax.axis_index("x"), lax.axis_index("y"), lax.axis_index("c")
    peers = []
    for fx, fy in ((1, 0), (0, 1), (1, 1)):
        px, py = (1 - x if fx else x), (1 - y if fy else y)
        peers.append(((px, py, c), 2 * px + py))
    return 2 * x + y, peers


def _remote(src, dst, send_sem, recv_sem, device):
    return pltpu.make_async_remote_copy(src_ref=src, dst_ref=dst, send_sem=send_sem, recv_sem=recv_sem,
                                        device_id=device, device_id_type=MESH)


N_FLIPS = N_CHIPS - 1


def _scatter_sems(n):
    return [pltpu.SemaphoreType.DMA((n * N_FLIPS,)), pltpu.SemaphoreType.DMA((n * N_FLIPS,)),
            pltpu.SemaphoreType.DMA((n,))]


def _scatter_copies(ins, outs, send_sems, recv_sems, local_sems, starting):
    me, peers = _chip_peers()
    local = [pltpu.make_async_copy(ins[k].at[me], outs[k].at[me], local_sems.at[k]) for k in range(len(ins))]
    sends, arrivals = [], []
    for k in range(len(ins)):
        for j, (device, idx) in enumerate(peers):
            sems = (send_sems.at[k * N_FLIPS + j], recv_sems.at[k * N_FLIPS + j], device)
            sends.append(_remote(ins[k].at[idx], outs[k].at[me], *sems))
            if not starting:
                arrivals.append(_remote(ins[k].at[me], outs[k].at[idx], *sems))
    return local, sends, arrivals


def _scatter_start(*refs):
    local, sends, _ = _scatter_copies(*refs, starting=True)
    for cp in local + sends:
        cp.start()


def _scatter_finish(*refs):
    local, sends, arrivals = _scatter_copies(*refs, starting=False)
    for cp in arrivals:
        cp.wait_recv()
    for cp in sends:
        cp.wait_send()
    for cp in local:
        cp.wait()


def _gather_sems(n):
    return [pltpu.SemaphoreType.DMA((n * N_FLIPS,)) for _ in range(4)] + [pltpu.SemaphoreType.DMA((n,))]


def _gather_copies(ins, outs, send_sems, recv_sems, pass_send_sems, pass_recv_sems, local_sems, starting):
    me, peers = _chip_peers()
    c = lax.axis_index("c")
    sibling = (lax.axis_index("x"), lax.axis_index("y"), 1 - c)
    local = [pltpu.make_async_copy(ins[k], outs[k].at[me], local_sems.at[k]) for k in range(len(ins))]
    sends, arrivals, passes, pass_arrivals = [], [], [], []
    for k in range(len(ins)):
        half = ins[k].shape[0] // 2
        mine, other = pl.ds(c * half, half), pl.ds((1 - c) * half, half)
        for j, (device, idx) in enumerate(peers):
            s = k * N_FLIPS + j
            sends.append(_remote(ins[k].at[mine], outs[k].at[me].at[mine], send_sems.at[s], recv_sems.at[s], device))
            if starting:
                continue
            arrived = outs[k].at[idx].at[mine]
            arrivals.append(_remote(ins[k].at[mine], arrived, send_sems.at[s], recv_sems.at[s], device))
            passes.append(_remote(arrived, arrived, pass_send_sems.at[s], pass_recv_sems.at[s], sibling))
            passed = outs[k].at[idx].at[other]
            pass_arrivals.append(_remote(passed, passed, pass_send_sems.at[s], pass_recv_sems.at[s], sibling))
    return local, sends, arrivals, passes, pass_arrivals


def _gather_start(*refs):
    local, sends, _, _, _ = _gather_copies(*refs, starting=True)
    for cp in local + sends:
        cp.start()


def _gather_finish(*refs):
    local, sends, arrivals, passes, pass_arrivals = _gather_copies(*refs, starting=False)
    for arrival, onward in zip(arrivals, passes):
        arrival.wait_recv()
        onward.start()
    for cp in pass_arrivals:
        cp.wait_recv()
    for cp in sends + passes:
        cp.wait_send()
    for cp in local:
        cp.wait()


def chip_gather(arrays):
    n = len(arrays)

    def body(*refs):
        _gather_start(refs[:n], refs[n:2 * n], *refs[2 * n:])
        _gather_finish(refs[:n], refs[n:2 * n], *refs[2 * n:])

    return pl.pallas_call(
        body, name="chip_gather", in_specs=[_ANY] * n, out_specs=[_ANY] * n,
        out_shape=[jax.ShapeDtypeStruct((N_CHIPS,) + a.shape, a.dtype) for a in arrays],
        scratch_shapes=_gather_sems(n),
    )(*arrays)


def sibling_exchange(arrays):
    n = len(arrays)

    def body(*refs):
        ins, outs = refs[:n], refs[n:2 * n]
        send_sems, recv_sems = refs[2 * n:]
        sibling = (lax.axis_index("x"), lax.axis_index("y"), 1 - lax.axis_index("c"))
        copies = [pltpu.make_async_remote_copy(src_ref=ins[k], dst_ref=outs[k], send_sem=send_sems.at[k],
                                               recv_sem=recv_sems.at[k], device_id=sibling, device_id_type=MESH)
                  for k in range(n)]
        for cp in copies:
            cp.start()
        for cp in copies:
            cp.wait()

    return pl.pallas_call(
        body, name="sibling_exchange", in_specs=[_ANY] * n, out_specs=[_ANY] * n,
        out_shape=[jax.ShapeDtypeStruct(a.shape, a.dtype) for a in arrays],
        scratch_shapes=[pltpu.SemaphoreType.DMA((n,)), pltpu.SemaphoreType.DMA((n,))],
    )(*arrays)


def all_gather_small(vec):
    def body(v_ref, out_ref, send_sems, recv_sems, local_sem):
        x, y, c = lax.axis_index("x"), lax.axis_index("y"), lax.axis_index("c")
        me = 4 * x + 2 * y + c
        local = pltpu.make_async_copy(v_ref, out_ref.at[me], local_sem)
        local.start()
        sends, recvs = [], []
        for j in range(1, N_DEV):
            px = jnp.where(j & 4, 1 - x, x)
            py = jnp.where(j & 2, 1 - y, y)
            pc = jnp.where(j & 1, 1 - c, c)
            common = dict(send_sem=send_sems.at[j - 1], recv_sem=recv_sems.at[j - 1], device_id=(px, py, pc),
                          device_id_type=MESH)
            sends.append(pltpu.make_async_remote_copy(src_ref=v_ref, dst_ref=out_ref.at[me], **common))
            recvs.append(pltpu.make_async_remote_copy(src_ref=v_ref, dst_ref=out_ref.at[4 * px + 2 * py + pc],
                                                      **common))
        for cp in sends:
            cp.start()
        for cp in recvs:
            cp.wait_recv()
        for cp in sends:
            cp.wait_send()
        local.wait()

    return pl.pallas_call(
        body, name="all_gather_small", in_specs=[_ANY], out_specs=_ANY,
        out_shape=jax.ShapeDtypeStruct((N_DEV,) + vec.shape, vec.dtype),
        scratch_shapes=[pltpu.SemaphoreType.DMA((N_DEV - 1,)), pltpu.SemaphoreType.DMA((N_DEV - 1,)),
                        pltpu.SemaphoreType.DMA],
    )(vec)


def sum_slots(stacked, tm=256):
    s, r, c = stacked.shape
    tm = min(tm, r)

    def body(in_ref, out_ref):
        acc = in_ref[0].astype(F32)
        for t in range(1, s):
            acc = acc + in_ref[t].astype(F32)
        out_ref[...] = acc

    return pl.pallas_call(
        body, name=f"sum_slots_{s}_{r}_{c}", grid=(r // tm,),
        in_specs=[pl.BlockSpec((s, tm, c), lambda i: (0, i, 0))], out_specs=_row_spec(tm, c),
        out_shape=jax.ShapeDtypeStruct((r, c), F32), compiler_params=_cparams(1),
    )(stacked)


def adamw(w, m, v, g_a, g_b=None, tm=256):
    r, c = w.shape
    tm = min(tm, r)
    two = g_b is not None

    def body(*refs):
        w_ref, m_ref, v_ref, ga_ref = refs[:4]
        g_ref, d_ref, nm_ref, nv_ref = refs[-4:]
        g = ga_ref[...] + refs[4][...] if two else ga_ref[...]
        nm = ADAM_B1 * m_ref[...] + (1.0 - ADAM_B1) * g
        nv = ADAM_B2 * v_ref[...] + (1.0 - ADAM_B2) * (g * g)
        m_hat = nm / (1.0 - ADAM_B1 ** ADAM_STEP)
        v_hat = nv / (1.0 - ADAM_B2 ** ADAM_STEP)
        g_ref[...] = g
        d_ref[...] = -ADAM_LR * (m_hat / (jnp.sqrt(v_hat) + ADAM_EPS) + ADAM_WD * w_ref[...])
        nm_ref[...] = nm
        nv_ref[...] = nv

    args = [w, m, v, g_a] + ([g_b] if two else [])
    return pl.pallas_call(
        body, name=f"adamw_{r}_{c}", grid=(r // tm,),
        in_specs=[_row_spec(tm, c)] * len(args), out_specs=[_row_spec(tm, c)] * 4,
        out_shape=[jax.ShapeDtypeStruct((r, c), F32)] * 4, compiler_params=_cparams(1),
    )(*args)


_SMALL = (("pre_norm_w", (2, D_MODEL)), ("post_norm_w", (2, D_MODEL)), ("attn_b_in", (1, ATTN_IN)),
          ("attn_sinks", (1, N_HEADS)), ("attn_b_out", (1, D_MODEL)), ("rec_lb_logits", (2, D_MODEL)),
          ("rec_gnorm_w", (1, REC_DIM)))
_SMALL_ROWS = 16


def _pack_small(parts, last_row=None):
    rows = []
    for (name, shape) in _SMALL:
        flat = parts[name].reshape(-1)
        pad = -flat.shape[0] % D_MODEL
        rows.append(jnp.pad(flat, (0, pad)).reshape(-1, D_MODEL))
    used = sum(r.shape[0] for r in rows)
    rows.append(jnp.zeros((_SMALL_ROWS - 1 - used, D_MODEL), F32))
    rows.append(jnp.zeros((1, D_MODEL), F32) if last_row is None else last_row)
    return jnp.concatenate(rows, axis=0)


def _unpack_small(packed):
    out, row = {}, 0
    for (name, shape) in _SMALL:
        size = shape[0] * shape[1]
        nrows = -(-size // D_MODEL)
        out[name] = packed[row:row + nrows].reshape(-1)[:size].reshape(shape)
        row += nrows
    return out


_CARRIED = ("rec_w_in", "rec_w_out", "attn_w_out")


_LATE = ("attn_w_out", "rec_w_in", "rec_w_out")


def local_step(x, positions, pre_norm_w, post_norm_w, attn_w_in, attn_b_in, attn_sinks, attn_w_out, attn_b_out,
               rec_w_in, rec_lb_logits, rec_gnorm_w, rec_w_out, loss_target, distributed=False):
    batch, seq, _ = x.shape
    n = batch * seq
    x0 = x.reshape(n, D_MODEL)
    angles = _rope_angles(positions)
    pre0, pre1 = pre_norm_w[0:1], pre_norm_w[1:2]
    post0, post1 = post_norm_w[0:1], post_norm_w[1:2]

    late = (attn_w_out, rec_w_in, rec_w_out) if distributed else ()
    (h0, q, k, v, z), late = attn_in_proj(x0, pre0, attn_w_in, attn_b_in, angles, to_bf16=late)
    sink_tab = _sink_table(attn_sinks)
    og0, gathered = attn_fwd(q, k, v, z, sink_tab, batch, seq, gather=late)
    if distributed:
        attn_w_out, rec_w_in, rec_w_out = (g if name == "rec_w_in" else _whole_from_shards(name, g)
                                           for name, g in zip(_LATE, gathered))
    y0, x1 = out_proj(og0, attn_w_out, attn_b_out, x0, post0)

    h1, proj1 = rec_in_proj(x1, pre1, rec_w_in)
    og1, states, safe = rec_fwd(proj1, rec_lb_logits, rec_gnorm_w, batch, seq)
    dx2, loss_vec, dog1, d_rec_w_out, d_post1, d_rec_w_out_bf16 = out_proj_loss_bwd(
        og1, rec_w_out, x1, post1, loss_target.reshape(n, D_MODEL))
    dproj1, d_lb, d_gnorm = rec_bwd(proj1, states, safe, rec_lb_logits, rec_gnorm_w, dog1, batch, seq)
    dx1, d_pre1, _, _ = in_proj_bwd_x(dproj1, rec_w_in, x1, pre1, dx2)
    d_rec_w_in, _ = in_proj_bwd_w(h1, dproj1, as_shards=distributed)

    dog0, d_attn_w_out, d_attn_b_out, d_post0, d_attn_w_out_bf16 = out_proj_bwd(dx1, y0, og0, attn_w_out, post0)
    ready = dict(rec_w_in=d_rec_w_in, rec_w_out=_shards_from_whole("rec_w_out", d_rec_w_out_bf16),
                 attn_w_out=_shards_from_whole("attn_w_out", d_attn_w_out_bf16))
    outgoing = [ready[name] for name in _CARRIED] if distributed else []
    dproj0, dk, dv, d_sink_tab, arrived = attn_bwd(q, k, v, z, sink_tab, dog0, angles, batch, seq, scatter=outgoing)
    d_sinks = jnp.transpose(jnp.sum(d_sink_tab, axis=-1), (0, 2, 1)).reshape(1, N_HEADS)
    d_attn_w_in, d_attn_b_in, dproj0 = in_proj_bwd_w(h0, dproj0, kv=(dk, dv, angles))
    last = [_shards_from_whole("attn_w_in", d_attn_w_in).astype(BF16)] if distributed else []
    dx0, d_pre0, arrived_last, summed = in_proj_bwd_x(dproj0, attn_w_in, x0, pre0, dx1, scatter=last,
                                                      slot_sums=list(arrived))

    grads = dict(
        pre_norm_w=jnp.concatenate([d_pre0, d_pre1], axis=0), post_norm_w=jnp.concatenate([d_post0, d_post1], axis=0),
        attn_w_in=d_attn_w_in, attn_b_in=d_attn_b_in, attn_sinks=d_sinks, attn_w_out=d_attn_w_out,
        attn_b_out=d_attn_b_out, rec_w_in=d_rec_w_in, rec_lb_logits=d_lb, rec_gnorm_w=d_gnorm,
        rec_w_out=d_rec_w_out)
    exchanged = dict(zip(_CARRIED, summed))
    exchanged.update(zip(("attn_w_in",), arrived_last))
    return loss_vec, dx0.reshape(batch, seq, D_MODEL), grads, exchanged


_BIG = ("attn_w_in", "attn_w_out", "rec_w_in", "rec_w_out")
_COLUMN_SHARDED = ("attn_w_in", "rec_w_in")
_ORDER = ("pre_norm_w", "post_norm_w", "attn_w_in", "attn_b_in", "attn_sinks", "attn_w_out", "attn_b_out",
          "rec_w_in", "rec_lb_logits", "rec_gnorm_w", "rec_w_out")


def _whole_from_shards(name, stacked):
    if name in _COLUMN_SHARDED:
        return jnp.transpose(stacked, (1, 0, 2)).reshape(stacked.shape[1], -1)
    return stacked.reshape(-1, stacked.shape[2])


def _shards_from_whole(name, whole):
    if name in _COLUMN_SHARDED:
        return jnp.transpose(whole.reshape(whole.shape[0], N_CHIPS, -1), (1, 0, 2))
    return whole.reshape(N_CHIPS, -1, whole.shape[1])


def kernel(x, positions, pre_norm_w, post_norm_w, attn_w_in, attn_b_in, attn_sinks, attn_w_out, attn_b_out, rec_w_in, rec_lb_logits, rec_gnorm_w, rec_w_out, loss_target, m_pre_norm_w, m_post_norm_w, m_attn_w_in, m_attn_b_in, m_attn_sinks, m_attn_w_out, m_attn_b_out, m_rec_w_in, m_rec_lb_logits, m_rec_gnorm_w, m_rec_w_out, v_pre_norm_w, v_post_norm_w, v_attn_w_in, v_attn_b_in, v_attn_sinks, v_attn_w_out, v_attn_b_out, v_rec_w_in, v_rec_lb_logits, v_rec_gnorm_w, v_rec_w_out):
    w = dict(pre_norm_w=pre_norm_w, post_norm_w=post_norm_w, attn_w_in=attn_w_in, attn_b_in=attn_b_in,
             attn_sinks=attn_sinks, attn_w_out=attn_w_out, attn_b_out=attn_b_out, rec_w_in=rec_w_in,
             rec_lb_logits=rec_lb_logits, rec_gnorm_w=rec_gnorm_w, rec_w_out=rec_w_out)
    m = dict(pre_norm_w=m_pre_norm_w, post_norm_w=m_post_norm_w, attn_w_in=m_attn_w_in, attn_b_in=m_attn_b_in,
             attn_sinks=m_attn_sinks, attn_w_out=m_attn_w_out, attn_b_out=m_attn_b_out, rec_w_in=m_rec_w_in,
             rec_lb_logits=m_rec_lb_logits, rec_gnorm_w=m_rec_gnorm_w, rec_w_out=m_rec_w_out)
    v = dict(pre_norm_w=v_pre_norm_w, post_norm_w=v_post_norm_w, attn_w_in=v_attn_w_in, attn_b_in=v_attn_b_in,
             attn_sinks=v_attn_sinks, attn_w_out=v_attn_w_out, attn_b_out=v_attn_b_out, rec_w_in=v_rec_w_in,
             rec_lb_logits=v_rec_lb_logits, rec_gnorm_w=v_rec_gnorm_w, rec_w_out=v_rec_w_out)

    shards = {name: w[name][0] for name in _BIG}
    attn_w_in_whole = _whole_from_shards("attn_w_in", chip_gather([shards["attn_w_in"].astype(BF16)])[0])

    loss_vec, grad_x, grads, parts = local_step(
        x, positions, pre_norm_w, post_norm_w, attn_w_in_whole, attn_b_in, attn_sinks, shards["attn_w_out"],
        attn_b_out, shards["rec_w_in"], rec_lb_logits, rec_gnorm_w, shards["rec_w_out"], loss_target,
        distributed=True)

    plane_sums = [parts[name] if name in _CARRIED else sum_slots(parts[name]) for name in _BIG]
    other_sums = sibling_exchange(plane_sums)
    out_g, out_d, out_m, out_v = {}, {}, {}, {}
    for name, mine, other in zip(_BIG, plane_sums, other_sums):
        g, d, nm, nv = adamw(shards[name], m[name][0], v[name][0], mine, other)
        out_g[name], out_d[name], out_m[name], out_v[name] = g[None], d[None], nm[None], nv[None]

    small_sum = sum_slots(all_gather_small(_pack_small(grads, last_row=loss_vec)))
    loss = jnp.sum(small_sum[_SMALL_ROWS - 1]) * (0.5 / D_MODEL)
    packed = adamw(_pack_small(w), _pack_small(m), _pack_small(v), small_sum)
    for dst, val in zip((out_g, out_d, out_m, out_v), packed):
        dst.update(_unpack_small(val))

    return (loss, grad_x, *[out_g[n] for n in _ORDER], *[out_d[n] for n in _ORDER],
            *[out_m[n] for n in _ORDER], *[out_v[n] for n in _ORDER])
```

```python
import functools

import jax
import jax.numpy as jnp
from jax import lax
from jax.experimental import pallas as pl
from jax.experimental.pallas import tpu as pltpu

F32 = jnp.float32
BF16 = jnp.bfloat16
MESH = pl.DeviceIdType.MESH

D_MODEL = 1024
HEAD_DIM = 64
N_HEADS = 16
N_KV_HEADS = 2
GROUP = N_HEADS // N_KV_HEADS
KV_WIDTH = N_KV_HEADS * HEAD_DIM
ATTN_IN = 2 * D_MODEL + 2 * KV_WIDTH
ATTN_BLOCK = 128
ROPE_THETA = 500000.0
ROPE_DIM = HEAD_DIM // 4
REC_HEADS = 8
REC_DIM = 128
REC_IN = 4 * D_MODEL
REC_BLOCK = 128
DIAG = 8
NORM_EPS = 1e-6
N_CHIPS = 4
N_DEV = 8
LANES = 128

ADAM_LR = 0.001
ADAM_B1 = 0.9
ADAM_B2 = 0.999
ADAM_EPS = 1e-08
ADAM_WD = 0.01
ADAM_STEP = 10

VMEM_LIMIT = 56 * 1024 * 1024


def _cparams(n_axes):
    return pltpu.CompilerParams(dimension_semantics=("arbitrary",) * n_axes, vmem_limit_bytes=VMEM_LIMIT)


def _dot(a, b, contract):
    return lax.dot_general(a.astype(BF16), b.astype(BF16), (contract, ((), ())), preferred_element_type=F32)


_NN = ((1,), (0,))
_NT = ((1,), (1,))
_TN = ((0,), (0,))


@jax.custom_vjp
def mm_nn(a, b):
    return _dot(a, b, _NN)


mm_nn.defvjp(lambda a, b: (_dot(a, b, _NN), (a, b)),
             lambda res, g: (_dot(g, res[1], _NT), _dot(res[0], g, _TN)))


@jax.custom_vjp
def mm_nt(a, b):
    return _dot(a, b, _NT)


mm_nt.defvjp(lambda a, b: (_dot(a, b, _NT), (a, b)),
             lambda res, g: (_dot(g, res[1], _NN), _dot(g, res[0], _TN)))


@jax.custom_vjp
def mm_tn(a, b):
    return _dot(a, b, _TN)


mm_tn.defvjp(lambda a, b: (_dot(a, b, _TN), (a, b)),
             lambda res, g: (_dot(res[1], g, _NT), _dot(res[0], g, _NN)))


def _tri_dot(x, lower):
    n = x.shape[0]
    r = lax.broadcasted_iota(jnp.int32, (n, n), 0)
    c = lax.broadcasted_iota(jnp.int32, (n, n), 1)
    tri = ((c <= r) if lower else (c >= r)).astype(BF16)
    hi = x.astype(BF16)
    rest = x - hi.astype(F32)
    mid = rest.astype(BF16)
    lo = (rest - mid.astype(F32)).astype(BF16)
    dot = lambda p: lax.dot_general(tri, p, (_NN, ((), ())), preferred_element_type=F32)
    return (dot(lo) + dot(mid)) + dot(hi)


@jax.custom_vjp
def cumsum_rows(x):
    return _tri_dot(x, True)


cumsum_rows.defvjp(lambda x: (cumsum_rows(x), None), lambda _, g: (_tri_dot(g, False),))


@functools.partial(jax.custom_vjp, nondiff_argnums=(1,))
def roll_sub(x, d):
    return pltpu.roll(x, d, 1) if d else x


roll_sub.defvjp(lambda x, d: (roll_sub(x, d), None),
                lambda d, _, g: (roll_sub(g, (DIAG - d) % DIAG),))


def sigmoid(x):
    return 1.0 / (1.0 + jnp.exp(-x))


@jax.custom_vjp
def silu(x):
    return x * sigmoid(x)


def _silu_fwd(x):
    s = sigmoid(x)
    return x * s, (x, s)


silu.defvjp(_silu_fwd, lambda res, g: (g * (res[1] * (1.0 + res[0] * (1.0 - res[1]))),))


F32_TINY = 1.17549435e-38


def sigmoid_pair(x):
    e = jnp.exp(-jnp.abs(x))
    r = 1.0 / (1.0 + e)
    er = e * r
    pos = x >= 0.0
    return jnp.where(pos, r, er), jnp.where(pos, er, r)


def _forget_fwd(x, a):
    lb, one_m_lb = sigmoid_pair(a)
    sp, sn = sigmoid_pair(x)
    f = lb + one_m_lb * sp
    k = one_m_lb * sn
    return (jnp.log(jnp.maximum(f, F32_TINY)), k), (sp, sn, f, k, lb, one_m_lb)


def _forget_bwd(res, g):
    sp, sn, f, k, lb, one_m_lb = res
    g_lf, g_k = g
    t = jnp.where(f >= F32_TINY, g_lf / jnp.maximum(f, F32_TINY), 0.0) - g_k
    return (k * sp) * t, jnp.sum(sn * t, axis=0, keepdims=True) * (lb * one_m_lb)


@jax.custom_vjp
def forget_gate(x, a):
    return _forget_fwd(x, a)[0]


forget_gate.defvjp(_forget_fwd, _forget_bwd)


@jax.custom_vjp
def decayed(x, e):
    return (x * jnp.exp(e)).astype(BF16).astype(F32)


def _decayed_fwd(x, e):
    y = decayed(x, e)
    return y, (y, e)


decayed.defvjp(_decayed_fwd, lambda res, g: (g * jnp.exp(res[1]), g * res[0]))


def _row(x, r):
    shape = x.shape

    @jax.custom_vjp
    def take(x):
        return x[r:r + 1, :]

    take.defvjp(lambda x: (x[r:r + 1, :], None),
                lambda _, g: (jnp.where(lax.broadcasted_iota(jnp.int32, shape, 0) == r, g, 0.0),))
    return take(x)


def _rms(x):
    return lax.rsqrt(jnp.mean(x * x, axis=-1, keepdims=True) + NORM_EPS)


def _attn_group(qs, k_a, v_a, k_b, v_b, zs, sink_a, sink_b, bias, at_sink=None):
    def half(kh, vh, sink):
        s = mm_nn(qs, kh) + bias
        if at_sink is None:
            m = jnp.maximum(jnp.max(s, axis=-1, keepdims=True), jnp.max(sink, axis=-1, keepdims=True))
            p = jnp.exp(s - lax.stop_gradient(m))
            own = jnp.sum(jnp.exp(sink - lax.stop_gradient(m)), axis=-1, keepdims=True) * (1.0 / LANES)
            return mm_nt(p * (1.0 / (jnp.sum(p, axis=-1, keepdims=True) + own)), vh)
        s = jnp.where(at_sink, jnp.concatenate([sink, sink], axis=1), s)
        p = jnp.exp(s - jnp.max(s, axis=-1, keepdims=True))
        return mm_nt(jnp.where(at_sink, 0.0, p), vh) * (1.0 / jnp.sum(p, axis=-1, keepdims=True))

    return (half(k_a, v_a, sink_a) + half(k_b, v_b, sink_b)) * silu(zs)


SAFE_RANGE = 80.0


def _rec_front(qr, fr, l0, l1):
    lf, k = forget_gate(fr, l1 - l0)
    return silu(qr), k, lf


def _rec_tail(o, z, gw):
    return o * _rms(o) * gw * silu(z)


def _rec_margin(b):
    R = b.shape[0]
    mid, last = _row(b, R // 2 - 1), _row(b, R - 1)
    return jnp.minimum(mid, last - mid)


def _heads(x):
    w = x.shape[1] // REC_HEADS
    return [x[:, h * w:(h + 1) * w] for h in range(REC_HEADS)]


def _hdot(a, b, contract):
    return jnp.concatenate([_dot(ah, bh, contract) for ah, bh in zip(_heads(a), _heads(b))], axis=1)


@jax.custom_vjp
def hmm_nn(a, b):
    return _hdot(a, b, _NN)


hmm_nn.defvjp(lambda a, b: (_hdot(a, b, _NN), (a, b)),
              lambda res, g: (_hdot(g, res[1], _NT), _hdot(res[0], g, _TN)))


@jax.custom_vjp
def hmm_nt(a, b):
    return _hdot(a, b, _NT)


hmm_nt.defvjp(lambda a, b: (_hdot(a, b, _NT), (a, b)),
              lambda res, g: (_hdot(g, res[1], _NN), _hdot(g, res[0], _TN)))


@jax.custom_vjp
def hmm_tn(a, b):
    return _hdot(a, b, _TN)


hmm_tn.defvjp(lambda a, b: (_hdot(a, b, _TN), (a, b)),
              lambda res, g: (_hdot(res[1], g, _NT), _hdot(res[0], g, _NN)))


def _head_sums(x):
    return jnp.concatenate([jnp.broadcast_to(jnp.sum(xh, axis=-1, keepdims=True), xh.shape) for xh in _heads(x)],
                           axis=1)


@jax.custom_vjp
def head_sum(x):
    return _head_sums(x)


head_sum.defvjp(lambda x: (_head_sums(x), None), lambda _, g: (_head_sums(g),))


def _rec_cores_fast(q, k, v, b, S):
    R = q.shape[0]
    ri = lax.broadcasted_iota(jnp.int32, (R, REC_HEADS * R), 0)
    ci = lax.broadcasted_iota(jnp.int32, (R, REC_HEADS * R), 1) % R
    d = b - _row(b, R // 2 - 1)
    sc = jnp.where(ci < ri, hmm_nt(decayed(q, d), decayed(k, -d)), 0.0)
    o = hmm_nt(q * jnp.exp(b), S) + hmm_nn(sc, v) + head_sum(q * k) * v
    b_last = _row(b, R - 1)
    return o, S * jnp.exp(b_last) + hmm_tn(v, k * jnp.exp(b_last - b))


def _rec_tails(o, z, gw):
    return o * lax.rsqrt(head_sum(o * o) * (1.0 / REC_DIM) + NORM_EPS) * gw * silu(z)


def _rec_block_fast(qr, fr, v, z, S, l0, l1, gw):
    lf, k = forget_gate(fr, l1 - l0)
    o, S_new = _rec_cores_fast(silu(qr), k, v, cumsum_rows(lf), S)
    return _rec_tails(o, z, gw), S_new


def _rec_core_slow(q, k, v, b, S):
    R = q.shape[0]
    rows = lax.broadcasted_iota(jnp.int32, (R, REC_DIM), 0)

    o = mm_nt(q * jnp.exp(jnp.minimum(b, 0.0)), S)

    ri = lax.broadcasted_iota(jnp.int32, (R, R), 0)
    ci = lax.broadcasted_iota(jnp.int32, (R, R), 1)
    sc = jnp.zeros((R, R), F32)
    w = R
    while w > DIAG:
        h = w // 2
        b3 = b.reshape(R // w, w, REC_DIM)
        rin = lax.broadcasted_iota(jnp.int32, (R // w, w, REC_DIM), 1)
        mid = jnp.sum(jnp.where(rin == h - 1, b3, 0.0), axis=1, keepdims=True)
        fac = jnp.exp(jnp.minimum(jnp.where(rin >= h, b3 - mid, mid - b3), 0.0)).reshape(R, REC_DIM)
        upper = (rows % w) >= h
        s_w = mm_nt(jnp.where(upper, q * fac, 0.0), jnp.where(upper, 0.0, k * fac))
        sc = sc + jnp.where((ri // w) == (ci // w), s_w, 0.0)
        w = h
    o = o + mm_nn(sc, v)

    g = R // DIAG
    q3, k3, v3, b3 = (t.reshape(g, DIAG, REC_DIM) for t in (q, k, v, b))
    rin = lax.broadcasted_iota(jnp.int32, (g, DIAG, 1), 1)
    od = jnp.zeros((g, DIAG, REC_DIM), F32)
    for d in range(DIAG):
        e = jnp.exp(jnp.minimum(b3 - roll_sub(b3, d), 0.0))
        sd = jnp.sum(q3 * roll_sub(k3, d) * e, axis=-1, keepdims=True)
        od = od + jnp.where(rin >= d, sd, 0.0) * roll_sub(v3, d)
    o = o + od.reshape(R, REC_DIM)

    b_last = _row(b, R - 1)
    return o, S * jnp.exp(jnp.minimum(b_last, 0.0)) + mm_tn(v, k * jnp.exp(jnp.minimum(b_last - b, 0.0)))


def _rec_head(core, qr, fr, v, z, S, l0, l1, gw):
    q, k, lf = _rec_front(qr, fr, l0, l1)
    o, S_new = core(q, k, v, cumsum_rows(lf), S)
    return _rec_tail(o, z, gw), S_new


ANGLE_COLS = 3 * ROPE_DIM


def _rope_angles(positions):
    half = ROPE_DIM // 2
    inv_freq = ROPE_THETA ** (-(jnp.arange(half, dtype=F32) * 2.0 / ROPE_DIM))
    ang = positions.astype(F32).reshape(-1, 1) * inv_freq
    cs = jnp.concatenate([jnp.cos(ang), jnp.sin(ang)], axis=-1)
    hi = cs.astype(BF16)
    rest = cs - hi.astype(F32)
    mid = rest.astype(BF16)
    return jnp.concatenate([hi, mid, (rest - mid.astype(F32)).astype(BF16)], axis=-1)


def _rope_tables(pieces):
    half = ROPE_DIM // 2
    r = lax.broadcasted_iota(jnp.int32, (ANGLE_COLS, 3 * LANES), 0) % ROPE_DIM
    c = lax.broadcasted_iota(jnp.int32, (ANGLE_COLS, 3 * LANES), 1)
    table, j = c // LANES, c % HEAD_DIM
    angle, low = j % half, j < half
    plus = ((table == 0) & (j < ROPE_DIM) & (r == angle)) | ((table == 1) & (j >= half) & (j < ROPE_DIM)
                                                                & (r == half + angle))
    minus = (table == 2) & low & (r == half + angle)
    pick = jnp.where(plus, 1.0, jnp.where(minus, -1.0, 0.0)).astype(BF16)
    out = jnp.dot(pieces, pick, preferred_element_type=F32)
    lane = lax.broadcasted_iota(jnp.int32, (1, LANES), 1) % HEAD_DIM
    return out[:, :LANES] + jnp.where(lane < ROPE_DIM, 0.0, 1.0), out[:, LANES:2 * LANES], out[:, 2 * LANES:]


def _rope(x, cos_t, sin_a, sin_b):
    half = ROPE_DIM // 2
    return x * cos_t + pltpu.roll(x, half, 1) * sin_a + pltpu.roll(x, LANES - half, 1) * sin_b


def _rope_transposed(g, cos_t, sin_a, sin_b):
    half = ROPE_DIM // 2
    return g * cos_t + pltpu.roll(g * sin_a, LANES - half, 1) + pltpu.roll(g * sin_b, half, 1)


def _row_spec(tm, width):
    return pl.BlockSpec((tm, width), lambda i: (i, 0))


def _weight_spec(shape):
    return pl.BlockSpec(shape, lambda *_: (0,) * len(shape), pipeline_mode=pl.Buffered(1))


def _full_spec(shape):
    return pl.BlockSpec(shape, lambda *_: (0,) * len(shape))


def attn_in_proj(x, w_pre, w_in, b_in, angles, tm=1024, to_bf16=()):
    n = x.shape[0]
    tm = min(tm, n)
    nc = len(to_bf16)

    def body(*refs):
        x_ref, wp_ref, w_ref, b_ref, cs_ref = refs[:5]
        h_ref, q_ref, k_ref, v_ref, z_ref = refs[5 + nc:10 + nc]

        @pl.when(pl.program_id(0) == 0)
        def _():
            for src, dst in zip(refs[5:5 + nc], refs[10 + nc:]):
                dst[...] = src[...].astype(BF16)

        xv = x_ref[...]
        h = (xv * _rms(xv) * wp_ref[...]).astype(BF16)
        h_ref[...] = h
        proj = jnp.dot(h, w_ref[...], preferred_element_type=F32) + b_ref[...]
        tabs = _rope_tables(cs_ref[...])
        for s in range(D_MODEL // LANES):
            sl = slice(s * LANES, (s + 1) * LANES)
            q_ref[:, sl] = _rope(proj[:, sl] * (HEAD_DIM ** -0.5), *tabs).astype(BF16)
        k_ref[...] = _rope(proj[:, D_MODEL:D_MODEL + KV_WIDTH], *tabs).astype(BF16)
        v_ref[...] = proj[:, D_MODEL + KV_WIDTH:D_MODEL + 2 * KV_WIDTH].astype(BF16)
        z_ref[...] = proj[:, D_MODEL + 2 * KV_WIDTH:]

    out = pl.pallas_call(
        body, name="attn_in_proj", grid=(n // tm,),
        in_specs=[_row_spec(tm, D_MODEL), _full_spec((1, D_MODEL)), _weight_spec((D_MODEL, ATTN_IN)),
                  _full_spec((1, ATTN_IN)), _row_spec(tm, ANGLE_COLS)] + [_weight_spec(a.shape) for a in to_bf16],
        out_specs=[_row_spec(tm, D_MODEL), _row_spec(tm, D_MODEL), _row_spec(tm, KV_WIDTH),
                   _row_spec(tm, KV_WIDTH), _row_spec(tm, D_MODEL)] + [_full_spec(a.shape) for a in to_bf16],
        out_shape=[jax.ShapeDtypeStruct((n, D_MODEL), BF16), jax.ShapeDtypeStruct((n, D_MODEL), BF16),
                   jax.ShapeDtypeStruct((n, KV_WIDTH), BF16), jax.ShapeDtypeStruct((n, KV_WIDTH), BF16),
                   jax.ShapeDtypeStruct((n, D_MODEL), F32)] + [jax.ShapeDtypeStruct(a.shape, BF16) for a in to_bf16],
        compiler_params=_cparams(1),
    )(x, w_pre, w_in, b_in, angles, *to_bf16)
    return out[:5], out[5:]


def _column_blocks(w):
    if len(w.shape) == 2:
        return [slice(0, w.shape[1])], lambda ref, s: ref[...]
    width = w.shape[2]
    return [slice(s * width, (s + 1) * width) for s in range(w.shape[0])], lambda ref, s: ref[s]


def rec_in_proj(x, w_pre, w_in, tm=1024):
    n = x.shape[0]
    tm = min(tm, n)
    columns, block = _column_blocks(w_in)

    def body(x_ref, wp_ref, w_ref, h_ref, p_ref):
        xv = x_ref[...]
        h = (xv * _rms(xv) * wp_ref[...]).astype(BF16)
        h_ref[...] = h
        for s, cols in enumerate(columns):
            p_ref[:, cols] = jnp.dot(h, block(w_ref, s), preferred_element_type=F32)

    return pl.pallas_call(
        body, name="rec_in_proj", grid=(n // tm,),
        in_specs=[_row_spec(tm, D_MODEL), _full_spec((1, D_MODEL)), _weight_spec(w_in.shape)],
        out_specs=[_row_spec(tm, D_MODEL), _row_spec(tm, REC_IN)],
        out_shape=[jax.ShapeDtypeStruct((n, D_MODEL), BF16), jax.ShapeDtypeStruct((n, REC_IN), F32)],
        compiler_params=_cparams(1),
    )(x, w_pre, w_in)


def out_proj(og, w_out, b_out, x_res, w_post, tm=1024):
    n = og.shape[0]
    tm = min(tm, n)

    def body(og_ref, w_ref, b_ref, x_ref, wp_ref, y_ref, xo_ref):
        y = jnp.dot(og_ref[...], w_ref[...], preferred_element_type=F32) + b_ref[...]
        y_ref[...] = y.astype(BF16)
        xo_ref[...] = x_ref[...] + y * _rms(y) * wp_ref[...]

    return pl.pallas_call(
        body, name="out_proj", grid=(n // tm,),
        in_specs=[_row_spec(tm, D_MODEL), _weight_spec((D_MODEL, D_MODEL)), _full_spec((1, D_MODEL)),
                  _row_spec(tm, D_MODEL), _full_spec((1, D_MODEL))],
        out_specs=[_row_spec(tm, D_MODEL), _row_spec(tm, D_MODEL)],
        out_shape=[jax.ShapeDtypeStruct((n, D_MODEL), BF16), jax.ShapeDtypeStruct((n, D_MODEL), F32)],
        compiler_params=_cparams(1),
    )(og, w_out, b_out, x_res, w_post)


def _post_norm_bwd(g, y, w_post):
    rstd = _rms(y)
    yn = y * rstd
    gw = g * w_post
    return rstd * (gw - yn * jnp.mean(gw * yn, axis=-1, keepdims=True)), jnp.sum(g * yn, axis=0, keepdims=True)


def out_proj_loss_bwd(og, w_out, x_res, w_post, target, tm=1024):
    n = og.shape[0]
    tm = min(tm, n)
    steps = n // tm

    def body(og_ref, w_ref, x_ref, wp_ref, t_ref, dx_ref, l_ref, dog_ref, dw_ref, dwp_ref, dwb_ref):
        @pl.when(pl.program_id(0) == 0)
        def _():
            l_ref[...] = jnp.zeros_like(l_ref)
            dw_ref[...] = jnp.zeros_like(dw_ref)
            dwp_ref[...] = jnp.zeros_like(dwp_ref)

        og_tile = og_ref[...]
        y = jnp.dot(og_tile, w_ref[...], preferred_element_type=F32)
        err = x_ref[...] + y * _rms(y) * wp_ref[...] - t_ref[...]
        g = err * (1.0 / D_MODEL)
        dx_ref[...] = g
        l_ref[...] += jnp.sum(err * err, axis=0, keepdims=True)
        dy, dwp = _post_norm_bwd(g, y, wp_ref[...])
        dwp_ref[...] += dwp
        dyb = dy.astype(BF16)
        dog_ref[...] = _dot(dyb, w_ref[...], _NT).astype(BF16)
        dw_ref[...] += _dot(og_tile, dyb, _TN)

        @pl.when(pl.program_id(0) == steps - 1)
        def _():
            dwb_ref[...] = dw_ref[...].astype(BF16)

    return pl.pallas_call(
        body, name="out_proj_loss_bwd", grid=(steps,),
        in_specs=[_row_spec(tm, D_MODEL), _weight_spec((D_MODEL, D_MODEL)), _row_spec(tm, D_MODEL),
                  _full_spec((1, D_MODEL)), _row_spec(tm, D_MODEL)],
        out_specs=[_row_spec(tm, D_MODEL), _full_spec((1, D_MODEL)), _row_spec(tm, D_MODEL),
                   _full_spec((D_MODEL, D_MODEL)), _full_spec((1, D_MODEL)), _full_spec((D_MODEL, D_MODEL))],
        out_shape=[jax.ShapeDtypeStruct((n, D_MODEL), F32), jax.ShapeDtypeStruct((1, D_MODEL), F32),
                   jax.ShapeDtypeStruct((n, D_MODEL), BF16), jax.ShapeDtypeStruct((D_MODEL, D_MODEL), F32),
                   jax.ShapeDtypeStruct((1, D_MODEL), F32), jax.ShapeDtypeStruct((D_MODEL, D_MODEL), BF16)],
        compiler_params=_cparams(1),
    )(og, w_out, x_res, w_post, target)


def out_proj_bwd(dxo, y, og, w_out, w_post, tm=1024):
    n = og.shape[0]
    tm = min(tm, n)
    steps = n // tm

    def body(g_ref, y_ref, og_ref, w_ref, wp_ref, dog_ref, dw_ref, db_ref, dwp_ref, dwb_ref):
        @pl.when(pl.program_id(0) == 0)
        def _():
            dw_ref[...] = jnp.zeros_like(dw_ref)
            db_ref[...] = jnp.zeros_like(db_ref)
            dwp_ref[...] = jnp.zeros_like(dwp_ref)

        dy, dwp = _post_norm_bwd(g_ref[...], y_ref[...].astype(F32), wp_ref[...])
        dwp_ref[...] += dwp
        db_ref[...] += jnp.sum(dy, axis=0, keepdims=True)
        dyb = dy.astype(BF16)
        dog_ref[...] = _dot(dyb, w_ref[...], _NT).astype(BF16)
        dw_ref[...] += _dot(og_ref[...], dyb, _TN)

        @pl.when(pl.program_id(0) == steps - 1)
        def _():
            dwb_ref[...] = dw_ref[...].astype(BF16)

    return pl.pallas_call(
        body, name="out_proj_bwd", grid=(steps,),
        in_specs=[_row_spec(tm, D_MODEL), _row_spec(tm, D_MODEL), _row_spec(tm, D_MODEL),
                  _weight_spec((D_MODEL, D_MODEL)), _full_spec((1, D_MODEL))],
        out_specs=[_row_spec(tm, D_MODEL), _full_spec((D_MODEL, D_MODEL)), _full_spec((1, D_MODEL)),
                   _full_spec((1, D_MODEL)), _full_spec((D_MODEL, D_MODEL))],
        out_shape=[jax.ShapeDtypeStruct((n, D_MODEL), BF16), jax.ShapeDtypeStruct((D_MODEL, D_MODEL), F32),
                   jax.ShapeDtypeStruct((1, D_MODEL), F32), jax.ShapeDtypeStruct((1, D_MODEL), F32),
                   jax.ShapeDtypeStruct((D_MODEL, D_MODEL), BF16)],
        compiler_params=_cparams(1),
    )(dxo, y, og, w_out, w_post)


def _slot_sum_specs(slot_sums, steps):
    return ([pl.BlockSpec((a.shape[0], a.shape[1] // steps, a.shape[2]), lambda i: (0, i, 0)) for a in slot_sums],
            [_row_spec(a.shape[1] // steps, a.shape[2]) for a in slot_sums],
            [jax.ShapeDtypeStruct(a.shape[1:], F32) for a in slot_sums])


def _sum_slots_into(slot_refs, sum_refs):
    for slots_ref, sum_ref in zip(slot_refs, sum_refs):
        sum_ref[...] = functools.reduce(jnp.add, [slots_ref[t].astype(F32) for t in range(slots_ref.shape[0])])


def in_proj_bwd_x(dproj, w_in, x, w_pre, dxo, tm=1024, scatter=(), to_sibling=()):
    n, p = dproj.shape
    tm = min(tm, n)
    steps = n // tm
    ns, nsib = len(scatter), len(to_sibling)
    columns, block = _column_blocks(w_in)

    def body(*refs):
        dp_ref, w_ref, x_ref, wp_ref, g_ref = refs[:5]
        outs = 5 + ns + nsib
        dx_ref, dwp_ref = refs[outs:outs + 2]
        sems = refs[outs + 2 + ns + nsib:]
        exchange = (refs[5:5 + ns], refs[outs + 2:outs + 2 + ns]) + tuple(sems[:3])
        sibling = (refs[5 + ns:outs], refs[outs + 2 + ns:outs + 2 + ns + nsib]) + tuple(sems[3:])

        @pl.when(pl.program_id(0) == 0)
        def _():
            dwp_ref[...] = jnp.zeros_like(dwp_ref)
            if ns:
                _scatter_start(*exchange)
            for cp in _sibling_copies(*sibling) if nsib else ():
                cp.start()

        dh = functools.reduce(jnp.add, [_dot(dp_ref[:, cols], block(w_ref, s), _NT)
                                        for s, cols in enumerate(columns)])
        xv = x_ref[...]
        rstd = _rms(xv)
        xn = xv * rstd
        gw = dh * wp_ref[...]
        dwp_ref[...] += jnp.sum(dh * xn, axis=0, keepdims=True)
        dx_ref[...] = rstd * (gw - xn * jnp.mean(gw * xn, axis=-1, keepdims=True)) + g_ref[...]

        if ns or nsib:
            @pl.when(pl.program_id(0) == steps - 1)
            def _():
                if ns:
                    _scatter_finish(*exchange)
                for cp in _sibling_copies(*sibling) if nsib else ():
                    cp.wait()

    out = pl.pallas_call(
        body, name=f"in_proj_bwd_x_{p}", grid=(steps,),
        in_specs=[_row_spec(tm, p), _weight_spec(w_in.shape), _row_spec(tm, D_MODEL), _full_spec((1, D_MODEL)),
                  _row_spec(tm, D_MODEL)] + [_ANY] * (ns + nsib),
        out_specs=[_row_spec(tm, D_MODEL), _full_spec((1, D_MODEL))] + [_ANY] * (ns + nsib),
        out_shape=[jax.ShapeDtypeStruct((n, D_MODEL), F32), jax.ShapeDtypeStruct((1, D_MODEL), F32)]
        + [jax.ShapeDtypeStruct(a.shape, a.dtype) for a in tuple(scatter) + tuple(to_sibling)],
        scratch_shapes=(_scatter_sems(ns) if ns else []) + (_sibling_sems(nsib) if nsib else []),
        compiler_params=_cparams(1),
    )(dproj, w_in, x, w_pre, dxo, *scatter, *to_sibling)
    return out[0], out[1], out[2:2 + ns], out[2 + ns:]


def in_proj_bwd_w(h, dproj, tm=1024, as_shards=False, kv=None, slot_sums=()):
    n, p = dproj.shape
    chunk = p // (4 if p % 4096 == 0 else 3)
    tm = min(tm, n)
    steps = n // tm
    shard = p // N_CHIPS
    n_kv = 0 if kv is None else 3
    n_sum = len(slot_sums)
    kv_from, kv_to = D_MODEL, D_MODEL + 2 * KV_WIDTH

    def body(*refs):
        h_ref, dp_ref = refs[:2]
        outs = 2 + n_kv + n_sum
        dw_ref, db_ref = refs[outs:outs + 2]
        sums_at = outs + 2 + (kv is not None)
        scratch = refs[sums_at + n_sum:]
        acc_scr, sem, staging = scratch[0], scratch[1], scratch[2:]
        i = pl.program_id(0)
        _sum_slots_into(refs[2 + n_kv:outs], refs[sums_at:sums_at + n_sum])

        @pl.when(i == 0)
        def _():
            acc_scr[...] = jnp.zeros_like(acc_scr)
            db_ref[...] = jnp.zeros_like(db_ref)

        if kv is not None:
            dk_ref, dv_ref, cs_ref, kv_ref = refs[2], refs[3], refs[4], refs[outs + 2]
            made = jnp.concatenate([_rope_transposed(dk_ref[...].T, *_rope_tables(cs_ref[...])), dv_ref[...].T],
                                   axis=1).astype(BF16)
            kv_ref[...] = made

        def columns(c0):
            if kv is None or c0 + chunk <= kv_from or c0 >= kv_to:
                return dp_ref[:, c0:c0 + chunk]
            return jnp.concatenate([dp_ref[:, c0:kv_from], made, dp_ref[:, kv_to:c0 + chunk]], axis=1)

        ht = h_ref[...].T
        for c0 in range(0, p, chunk):
            dp = columns(c0)
            acc_scr[:, c0:c0 + chunk] += jnp.dot(ht, dp, preferred_element_type=F32)
            db_ref[:, c0:c0 + chunk] += jnp.sum(dp.astype(F32), axis=0, keepdims=True)

        @pl.when(i == steps - 1)
        def _():
            if as_shards:
                for s in range(N_CHIPS):
                    staging[0][...] = acc_scr[:, s * shard:(s + 1) * shard].astype(BF16)
                    out = pltpu.make_async_copy(staging[0], dw_ref.at[s], sem)
                    out.start()
                    out.wait()
            else:
                out = pltpu.make_async_copy(acc_scr, dw_ref, sem)
                out.start()
                out.wait()

    dw_shape = jax.ShapeDtypeStruct((N_CHIPS, D_MODEL, shard), BF16) if as_shards else (
        jax.ShapeDtypeStruct((D_MODEL, p), F32))
    in_specs = [_row_spec(tm, D_MODEL), _row_spec(tm, p)]
    out_specs = [_ANY, _full_spec((1, p))]
    out_shape = [dw_shape, jax.ShapeDtypeStruct((1, p), F32)]
    if kv is not None:
        columns_t = pl.BlockSpec((KV_WIDTH, tm), lambda i: (0, i))
        in_specs += [columns_t, columns_t, _row_spec(tm, ANGLE_COLS)]
        out_specs.append(pl.BlockSpec((tm, kv_to - kv_from), lambda i: (i, kv_from // (kv_to - kv_from))))
        out_shape.append(jax.ShapeDtypeStruct(dproj.shape, dproj.dtype))
    sum_in, sum_out, sum_shapes = _slot_sum_specs(slot_sums, steps)
    in_specs, out_specs, out_shape = in_specs + sum_in, out_specs + sum_out, out_shape + sum_shapes
    return pl.pallas_call(
        body, name=f"in_proj_bwd_w_{p}", grid=(steps,),
        in_specs=in_specs, out_specs=out_specs, out_shape=out_shape,
        scratch_shapes=[pltpu.VMEM((D_MODEL, p), F32), pltpu.SemaphoreType.DMA]
        + ([pltpu.VMEM((D_MODEL, shard), BF16)] if as_shards else []),
        input_output_aliases={1: 2} if kv is not None else {},
        compiler_params=_cparams(1),
    )(h, dproj, *(kv or ()), *slot_sums)


PAIRS = GROUP // 2
GROUP_ROWS = PAIRS * ATTN_BLOCK
MASKED = -1e30


def _kv_windows(k_ref, v_ref, i):
    ps = pl.multiple_of(jnp.maximum(i - 1, 0) * ATTN_BLOCK, ATTN_BLOCK)
    cs = pl.multiple_of(i * ATTN_BLOCK, ATTN_BLOCK)
    kw = jnp.concatenate([k_ref[pl.ds(ps, ATTN_BLOCK), :], k_ref[pl.ds(cs, ATTN_BLOCK), :]], axis=0)
    vw = jnp.concatenate([v_ref[pl.ds(ps, ATTN_BLOCK), :], v_ref[pl.ds(cs, ATTN_BLOCK), :]], axis=0)
    return kw.astype(F32).T, vw.astype(F32).T, ps, cs


def _low_rows(shape):
    return lax.broadcasted_iota(jnp.int32, shape, 0) < HEAD_DIM


def _spread(w, kvh):
    low = _low_rows(w.shape)
    swapped = pltpu.roll(w, HEAD_DIM, 0)
    if kvh == 0:
        return jnp.where(low, w, 0.0), jnp.where(low, 0.0, swapped)
    return jnp.where(low, swapped, 0.0), jnp.where(low, 0.0, w)


def _unspread(d_a, d_b, kvh):
    low = _low_rows(d_a.shape)
    if kvh == 0:
        return jnp.where(low, d_a + pltpu.roll(d_b, HEAD_DIM, 0), 0.0)
    return jnp.where(low, 0.0, pltpu.roll(d_a, HEAD_DIM, 0) + d_b)


def _stack_pairs(ref, kvh):
    return jnp.concatenate([ref[:, (kvh * PAIRS + j) * LANES:(kvh * PAIRS + j + 1) * LANES] for j in range(PAIRS)],
                           axis=0)


def _fill_bias(bias_scr):
    shape = (GROUP_ROWS, 2 * ATTN_BLOCK)
    r = lax.broadcasted_iota(jnp.int32, shape, 0) % ATTN_BLOCK
    c = lax.broadcasted_iota(jnp.int32, shape, 1)
    in_cur = (c >= ATTN_BLOCK) & ((c - ATTN_BLOCK) <= r)
    in_prev = (c < ATTN_BLOCK) & (c > r)
    bias_scr[0] = jnp.where(in_cur, 0.0, MASKED)
    bias_scr[1] = jnp.where(in_cur | in_prev, 0.0, MASKED)
    bias_scr[2] = jnp.where(c == r, 1.0, 0.0)


N_BIAS_TABLES = 3


def _sink_table(sinks):
    t = jnp.transpose(sinks.reshape(N_KV_HEADS, PAIRS, 2), (0, 2, 1))
    return jnp.broadcast_to(t[:, :, :, None, None], (N_KV_HEADS, 2, PAIRS, ATTN_BLOCK, LANES)).reshape(
        N_KV_HEADS, 2, GROUP_ROWS, LANES)


def attn_fwd(q, k, v, z, sink_tab, batch, seq, gather=()):
    nb = seq // ATTN_BLOCK
    ng = len(gather)

    def body(*refs):
        q_ref, k_ref, v_ref, z_ref, s_ref = refs[:5]
        og_ref, bias_scr = refs[5 + ng], refs[6 + 2 * ng]
        exchange = (refs[5:5 + ng], refs[6 + ng:6 + 2 * ng]) + tuple(refs[7 + 2 * ng:])
        b, i = pl.program_id(0), pl.program_id(1)

        @pl.when((b == 0) & (i == 0))
        def _():
            _fill_bias(bias_scr)
            if ng:
                _gather_start(*exchange)

        kw, vw, _, _ = _kv_windows(k_ref, v_ref, i)
        bias, at_sink = bias_scr[jnp.minimum(i, 1)], bias_scr[2] > 0.5
        for kvh in range(N_KV_HEADS):
            k_a, k_b = _spread(kw, kvh)
            v_a, v_b = _spread(vw, kvh)
            og = _attn_group(_stack_pairs(q_ref, kvh), k_a, v_a, k_b, v_b, _stack_pairs(z_ref, kvh),
                             s_ref[kvh, 0], s_ref[kvh, 1], bias, at_sink)
            for j in range(PAIRS):
                og_ref[:, (kvh * PAIRS + j) * LANES:(kvh * PAIRS + j + 1) * LANES] = (
                    og[j * ATTN_BLOCK:(j + 1) * ATTN_BLOCK].astype(BF16))

        if ng:
            @pl.when((b == batch - 1) & (i == nb - 1))
            def _():
                _gather_finish(*exchange)

    blk = lambda w: pl.BlockSpec((ATTN_BLOCK, w), lambda b, i: (b * nb + i, 0))
    seq_spec = pl.BlockSpec((seq, KV_WIDTH), lambda b, i: (b, 0))
    out = pl.pallas_call(
        body, name="attn_fwd", grid=(batch, nb),
        in_specs=[blk(D_MODEL), seq_spec, seq_spec, blk(D_MODEL), _full_spec(sink_tab.shape)] + [_ANY] * ng,
        out_specs=[blk(D_MODEL)] + [_ANY] * ng,
        out_shape=[jax.ShapeDtypeStruct((batch * seq, D_MODEL), BF16)]
        + [jax.ShapeDtypeStruct((N_CHIPS,) + a.shape, a.dtype) for a in gather],
        scratch_shapes=[pltpu.VMEM((N_BIAS_TABLES, GROUP_ROWS, 2 * ATTN_BLOCK), F32)] + (_gather_sems(ng) if ng else []),
        compiler_params=_cparams(2),
    )(q, k, v, z, sink_tab, *gather)
    return out[0], out[1:]


def attn_bwd(q, k, v, z, sink_tab, dog, angles, batch, seq, scatter=()):
    nb = seq // ATTN_BLOCK
    ns = len(scatter)

    def body(*refs):
        q_ref, k_ref, v_ref, z_ref, s_ref, g_ref, cs_ref = refs[:7]
        dp_ref, dk_ref, dv_ref, ds_ref = refs[7 + ns:11 + ns]
        bias_scr = refs[11 + 2 * ns]
        exchange = (refs[7:7 + ns], refs[11 + ns:11 + 2 * ns]) + tuple(refs[12 + 2 * ns:])
        b, i = pl.program_id(0), pl.program_id(1)

        @pl.when((b == 0) & (i == 0))
        def _():
            _fill_bias(bias_scr)
            ds_ref[...] = jnp.zeros_like(ds_ref)
            if ns:
                _scatter_start(*exchange)

        @pl.when(i == 0)
        def _():
            dk_ref[...] = jnp.zeros_like(dk_ref)
            dv_ref[...] = jnp.zeros_like(dv_ref)

        kw, vw, ps, cs = _kv_windows(k_ref, v_ref, i)
        bias = bias_scr[jnp.minimum(i, 1)]
        tabs = _rope_tables(cs_ref[...])
        dkw = jnp.zeros_like(kw)
        dvw = jnp.zeros_like(vw)
        for kvh in range(N_KV_HEADS):
            k_a, k_b = _spread(kw, kvh)
            v_a, v_b = _spread(vw, kvh)
            _, vjp = jax.vjp(functools.partial(_attn_group, bias=bias), _stack_pairs(q_ref, kvh).astype(F32),
                             k_a, v_a, k_b, v_b, _stack_pairs(z_ref, kvh), s_ref[kvh, 0], s_ref[kvh, 1])
            dqs, dk_a, dv_a, dk_b, dv_b, dzs, ds_a, ds_b = vjp(_stack_pairs(g_ref, kvh).astype(F32))
            dkw = dkw + _unspread(dk_a, dk_b, kvh)
            dvw = dvw + _unspread(dv_a, dv_b, kvh)
            ds_ref[kvh, 0] += jnp.sum(ds_a.reshape(PAIRS, ATTN_BLOCK, LANES), axis=1)
            ds_ref[kvh, 1] += jnp.sum(ds_b.reshape(PAIRS, ATTN_BLOCK, LANES), axis=1)
            for j in range(PAIRS):
                rows = slice(j * ATTN_BLOCK, (j + 1) * ATTN_BLOCK)
                col = (kvh * PAIRS + j) * LANES
                dp_ref[:, col:col + LANES] = _rope_transposed(dqs[rows] * (HEAD_DIM ** -0.5), *tabs).astype(BF16)
                zc = D_MODEL + 2 * KV_WIDTH + col
                dp_ref[:, zc:zc + LANES] = dzs[rows].astype(BF16)
        dp_ref[:, D_MODEL:D_MODEL + 2 * KV_WIDTH] = jnp.zeros((ATTN_BLOCK, 2 * KV_WIDTH), BF16)
        dk_ref[:, pl.ds(ps, ATTN_BLOCK)] += dkw[:, :ATTN_BLOCK]
        dk_ref[:, pl.ds(cs, ATTN_BLOCK)] += dkw[:, ATTN_BLOCK:]
        dv_ref[:, pl.ds(ps, ATTN_BLOCK)] += dvw[:, :ATTN_BLOCK]
        dv_ref[:, pl.ds(cs, ATTN_BLOCK)] += dvw[:, ATTN_BLOCK:]

        if ns:
            @pl.when((b == batch - 1) & (i == nb - 1))
            def _():
                _scatter_finish(*exchange)

    blk = lambda w: pl.BlockSpec((ATTN_BLOCK, w), lambda b, i: (b * nb + i, 0))
    seq_spec = pl.BlockSpec((seq, KV_WIDTH), lambda b, i: (b, 0))
    seq_spec_t = pl.BlockSpec((KV_WIDTH, seq), lambda b, i: (0, b))
    n = batch * seq
    ds_shape = (N_KV_HEADS, 2, PAIRS, LANES)
    out = pl.pallas_call(
        body, name="attn_bwd", grid=(batch, nb),
        in_specs=[blk(D_MODEL), seq_spec, seq_spec, blk(D_MODEL), _full_spec(sink_tab.shape), blk(D_MODEL)]
        + [blk(ANGLE_COLS)] + [_ANY] * ns,
        out_specs=[blk(ATTN_IN), seq_spec_t, seq_spec_t, _full_spec(ds_shape)] + [_ANY] * ns,
        out_shape=[jax.ShapeDtypeStruct((n, ATTN_IN), BF16), jax.ShapeDtypeStruct((KV_WIDTH, n), F32),
                   jax.ShapeDtypeStruct((KV_WIDTH, n), F32), jax.ShapeDtypeStruct(ds_shape, F32)]
        + [jax.ShapeDtypeStruct(a.shape, a.dtype) for a in scatter],
        scratch_shapes=[pltpu.VMEM((N_BIAS_TABLES, GROUP_ROWS, 2 * ATTN_BLOCK), F32)] + (_scatter_sems(ns) if ns else []),
        compiler_params=_cparams(2),
    )(q, k, v, z, sink_tab, dog, angles, *scatter)
    return out[0], out[1], out[2], out[3], out[4:]


def rec_fwd(proj, lb_logits, gnorm_w, batch, seq):
    nblk = seq // REC_BLOCK

    def body(p_ref, lb_ref, gw_ref, og_ref, st_ref, safe_ref, s_scr):
        @pl.when(pl.program_id(1) == 0)
        def _():
            s_scr[...] = jnp.zeros_like(s_scr)

        S = s_scr[...]
        st_ref[0] = S
        qr, fr, v, z = (p_ref[:, part * D_MODEL:(part + 1) * D_MODEL] for part in range(4))
        lf, k = forget_gate(fr, lb_ref[1:2, :] - lb_ref[0:1, :])
        q, b = silu(qr), cumsum_rows(lf)
        safe = jnp.min(_rec_margin(b)) >= -SAFE_RANGE

        gate = gw_ref[...] * silu(z)
        safe_ref[0] = jnp.full((REC_HEADS, LANES), safe.astype(F32))

        def store(o, S_new):
            og_ref[...] = (o * lax.rsqrt(head_sum(o * o) * (1.0 / REC_DIM) + NORM_EPS) * gate).astype(BF16)
            s_scr[...] = S_new

        @pl.when(safe)
        def _():
            store(*_rec_cores_fast(q, k, v, b, S))

        @pl.when(jnp.logical_not(safe))
        def _():
            outs = [_rec_core_slow(*args) for args in zip(*(_heads(t) for t in (q, k, v, b, S)))]
            store(*(jnp.concatenate(parts, axis=1) for parts in zip(*outs)))

    blk = lambda w: pl.BlockSpec((REC_BLOCK, w), lambda b, j: (b * nblk + j, 0))
    st_spec = pl.BlockSpec((1, REC_DIM, D_MODEL), lambda b, j: (b * nblk + j, 0, 0))
    safe_spec = pl.BlockSpec((1, REC_HEADS, LANES), lambda b, j: (b * nblk + j, 0, 0))
    return pl.pallas_call(
        body, name="rec_fwd", grid=(batch, nblk),
        in_specs=[blk(REC_IN), _full_spec((2, D_MODEL)), _full_spec((1, D_MODEL))],
        out_specs=[blk(D_MODEL), st_spec, safe_spec],
        out_shape=[jax.ShapeDtypeStruct((batch * seq, D_MODEL), BF16),
                   jax.ShapeDtypeStruct((batch * nblk, REC_DIM, D_MODEL), F32),
                   jax.ShapeDtypeStruct((batch * nblk, REC_HEADS, LANES), F32)],
        scratch_shapes=[pltpu.VMEM((REC_DIM, D_MODEL), F32)],
        compiler_params=_cparams(2),
    )(proj, lb_logits, jnp.tile(gnorm_w, (1, REC_HEADS)))


def rec_bwd(proj, states, safe, lb_logits, gnorm_w, dog, batch, seq):
    nblk = seq // REC_BLOCK

    def body(p_ref, st_ref, safe_ref, lb_ref, gw_ref, g_ref, dp_ref, dlb_ref, dgw_ref, ds_scr):
        @pl.when((pl.program_id(0) == 0) & (pl.program_id(1) == 0))
        def _():
            dlb_ref[...] = jnp.zeros_like(dlb_ref)
            dgw_ref[...] = jnp.zeros_like(dgw_ref)

        @pl.when(pl.program_id(1) == 0)
        def _():
            ds_scr[...] = jnp.zeros_like(ds_scr)

        def load():
            primals = tuple(p_ref[:, part * D_MODEL:(part + 1) * D_MODEL] for part in range(4)) + (
                st_ref[0], lb_ref[0:1, :], lb_ref[1:2, :], gw_ref[...])
            return primals, (g_ref[...].astype(F32), ds_scr[...])

        def store(dqr, dfr, dv, dz, dS, dl0, dl1, dgw):
            for part, val in enumerate((dqr, dfr, dv, dz)):
                dp_ref[:, part * D_MODEL:(part + 1) * D_MODEL] = val.astype(BF16)
            ds_scr[...] = dS
            dlb_ref[0:1, :] += dl0
            dlb_ref[1:2, :] += dl1
            dgw_ref[...] += functools.reduce(jnp.add, _heads(dgw))

        fast = jnp.max(safe_ref[0]) > 0.5

        @pl.when(fast)
        def _():
            primals, cotangents = load()
            store(*jax.vjp(_rec_block_fast, *primals)[1](cotangents))

        @pl.when(jnp.logical_not(fast))
        def _():
            primals, cotangents = load()
            outs = [jax.vjp(functools.partial(_rec_head, _rec_core_slow), *args)[1](cts)
                    for args, cts in zip(zip(*(_heads(t) for t in primals)), zip(*(_heads(t) for t in cotangents)))]
            store(*(jnp.concatenate(parts, axis=1) for parts in zip(*outs)))

    blk = lambda w: pl.BlockSpec((REC_BLOCK, w), lambda b, j: (b * nblk + nblk - 1 - j, 0))
    st_spec = pl.BlockSpec((1, REC_DIM, D_MODEL), lambda b, j: (b * nblk + nblk - 1 - j, 0, 0))
    safe_spec = pl.BlockSpec((1, REC_HEADS, LANES), lambda b, j: (b * nblk + nblk - 1 - j, 0, 0))
    return pl.pallas_call(
        body, name="rec_bwd", grid=(batch, nblk),
        in_specs=[blk(REC_IN), st_spec, safe_spec, _full_spec((2, D_MODEL)), _full_spec((1, D_MODEL)),
                  blk(D_MODEL)],
        out_specs=[blk(REC_IN), _full_spec((2, D_MODEL)), _full_spec((1, REC_DIM))],
        out_shape=[jax.ShapeDtypeStruct((batch * seq, REC_IN), BF16), jax.ShapeDtypeStruct((2, D_MODEL), F32),
                   jax.ShapeDtypeStruct((1, REC_DIM), F32)],
        scratch_shapes=[pltpu.VMEM((REC_DIM, D_MODEL), F32)],
        compiler_params=_cparams(2),
    )(proj, states, safe, lb_logits, jnp.tile(gnorm_w, (1, REC_HEADS)), dog)


_ANY = pl.BlockSpec(memory_space=pl.ANY)


def _chip_peers():
    x, y, c = lax.axis_index("x"), lax.axis_index("y"), lax.axis_index("c")
    peers = []
    for fx, fy in ((1, 0), (0, 1), (1, 1)):
        px, py = (1 - x if fx else x), (1 - y if fy else y)
        peers.append(((px, py, c), 2 * px + py))
    return 2 * x + y, peers


def _remote(src, dst, send_sem, recv_sem, device):
    return pltpu.make_async_remote_copy(src_ref=src, dst_ref=dst, send_sem=send_sem, recv_sem=recv_sem,
                                        device_id=device, device_id_type=MESH)


N_FLIPS = N_CHIPS - 1


def _scatter_sems(n):
    return [pltpu.SemaphoreType.DMA((n * N_FLIPS,)), pltpu.SemaphoreType.DMA((n * N_FLIPS,)),
            pltpu.SemaphoreType.DMA((n,))]


def _scatter_copies(ins, outs, send_sems, recv_sems, local_sems, starting):
    me, peers = _chip_peers()
    local = [pltpu.make_async_copy(ins[k].at[me], outs[k].at[me], local_sems.at[k]) for k in range(len(ins))]
    sends, arrivals = [], []
    for k in range(len(ins)):
        for j, (device, idx) in enumerate(peers):
            sems = (send_sems.at[k * N_FLIPS + j], recv_sems.at[k * N_FLIPS + j], device)
            sends.append(_remote(ins[k].at[idx], outs[k].at[me], *sems))
            if not starting:
                arrivals.append(_remote(ins[k].at[me], outs[k].at[idx], *sems))
    return local, sends, arrivals


def _scatter_start(*refs):
    local, sends, _ = _scatter_copies(*refs, starting=True)
    for cp in local + sends:
        cp.start()


def _scatter_finish(*refs):
    local, sends, arrivals = _scatter_copies(*refs, starting=False)
    for cp in arrivals:
        cp.wait_recv()
    for cp in sends:
        cp.wait_send()
    for cp in local:
        cp.wait()


def _gather_sems(n):
    return [pltpu.SemaphoreType.DMA((n * N_FLIPS,)) for _ in range(4)] + [pltpu.SemaphoreType.DMA((n,))]


def _gather_copies(ins, outs, send_sems, recv_sems, pass_send_sems, pass_recv_sems, local_sems, starting):
    me, peers = _chip_peers()
    c = lax.axis_index("c")
    sibling = (lax.axis_index("x"), lax.axis_index("y"), 1 - c)
    local = [pltpu.make_async_copy(ins[k], outs[k].at[me], local_sems.at[k]) for k in range(len(ins))]
    sends, arrivals, passes, pass_arrivals = [], [], [], []
    for k in range(len(ins)):
        half = ins[k].shape[0] // 2
        mine, other = pl.ds(c * half, half), pl.ds((1 - c) * half, half)
        for j, (device, idx) in enumerate(peers):
            s = k * N_FLIPS + j
            sends.append(_remote(ins[k].at[mine], outs[k].at[me].at[mine], send_sems.at[s], recv_sems.at[s], device))
            if starting:
                continue
            arrived = outs[k].at[idx].at[mine]
            arrivals.append(_remote(ins[k].at[mine], arrived, send_sems.at[s], recv_sems.at[s], device))
            passes.append(_remote(arrived, arrived, pass_send_sems.at[s], pass_recv_sems.at[s], sibling))
            passed = outs[k].at[idx].at[other]
            pass_arrivals.append(_remote(passed, passed, pass_send_sems.at[s], pass_recv_sems.at[s], sibling))
    return local, sends, arrivals, passes, pass_arrivals


def _gather_start(*refs):
    local, sends, _, _, _ = _gather_copies(*refs, starting=True)
    for cp in local + sends:
        cp.start()


def _gather_finish(*refs):
    local, sends, arrivals, passes, pass_arrivals = _gather_copies(*refs, starting=False)
    for arrival, onward in zip(arrivals, passes):
        arrival.wait_recv()
        onward.start()
    for cp in pass_arrivals:
        cp.wait_recv()
    for cp in sends + passes:
        cp.wait_send()
    for cp in local:
        cp.wait()


def chip_gather(arrays):
    n = len(arrays)

    def body(*refs):
        _gather_start(refs[:n], refs[n:2 * n], *refs[2 * n:])
        _gather_finish(refs[:n], refs[n:2 * n], *refs[2 * n:])

    return pl.pallas_call(
        body, name="chip_gather", in_specs=[_ANY] * n, out_specs=[_ANY] * n,
        out_shape=[jax.ShapeDtypeStruct((N_CHIPS,) + a.shape, a.dtype) for a in arrays],
        scratch_shapes=_gather_sems(n),
    )(*arrays)


def _sibling_sems(n):
    return [pltpu.SemaphoreType.DMA((n,)), pltpu.SemaphoreType.DMA((n,))]


def _sibling_copies(ins, outs, send_sems, recv_sems):
    sibling = (lax.axis_index("x"), lax.axis_index("y"), 1 - lax.axis_index("c"))
    return [_remote(ins[k], outs[k], send_sems.at[k], recv_sems.at[k], sibling) for k in range(len(ins))]


def sibling_exchange(arrays):
    n = len(arrays)

    def body(*refs):
        copies = _sibling_copies(refs[:n], refs[n:2 * n], *refs[2 * n:])
        for cp in copies:
            cp.start()
        for cp in copies:
            cp.wait()

    return pl.pallas_call(
        body, name="sibling_exchange", in_specs=[_ANY] * n, out_specs=[_ANY] * n,
        out_shape=[jax.ShapeDtypeStruct(a.shape, a.dtype) for a in arrays], scratch_shapes=_sibling_sems(n),
    )(*arrays)


def all_gather_small(vec):
    def body(v_ref, out_ref, send_sems, recv_sems, local_sem):
        x, y, c = lax.axis_index("x"), lax.axis_index("y"), lax.axis_index("c")
        me = 4 * x + 2 * y + c
        local = pltpu.make_async_copy(v_ref, out_ref.at[me], local_sem)
        local.start()
        sends, recvs = [], []
        for j in range(1, N_DEV):
            px = jnp.where(j & 4, 1 - x, x)
            py = jnp.where(j & 2, 1 - y, y)
            pc = jnp.where(j & 1, 1 - c, c)
            common = dict(send_sem=send_sems.at[j - 1], recv_sem=recv_sems.at[j - 1], device_id=(px, py, pc),
                          device_id_type=MESH)
            sends.append(pltpu.make_async_remote_copy(src_ref=v_ref, dst_ref=out_ref.at[me], **common))
            recvs.append(pltpu.make_async_remote_copy(src_ref=v_ref, dst_ref=out_ref.at[4 * px + 2 * py + pc],
                                                      **common))
        for cp in sends:
            cp.start()
        for cp in recvs:
            cp.wait_recv()
        for cp in sends:
            cp.wait_send()
        local.wait()

    return pl.pallas_call(
        body, name="all_gather_small", in_specs=[_ANY], out_specs=_ANY,
        out_shape=jax.ShapeDtypeStruct((N_DEV,) + vec.shape, vec.dtype),
        scratch_shapes=[pltpu.SemaphoreType.DMA((N_DEV - 1,)), pltpu.SemaphoreType.DMA((N_DEV - 1,)),
                        pltpu.SemaphoreType.DMA],
    )(vec)


def sum_slots(stacked, tm=256):
    s, r, c = stacked.shape
    tm = min(tm, r)

    def body(in_ref, out_ref):
        acc = in_ref[0].astype(F32)
        for t in range(1, s):
            acc = acc + in_ref[t].astype(F32)
        out_ref[...] = acc

    return pl.pallas_call(
        body, name=f"sum_slots_{s}_{r}_{c}", grid=(r // tm,),
        in_specs=[pl.BlockSpec((s, tm, c), lambda i: (0, i, 0))], out_specs=_row_spec(tm, c),
        out_shape=jax.ShapeDtypeStruct((r, c), F32), compiler_params=_cparams(1),
    )(stacked)


def adamw(w, m, v, g_a, g_b=None, tm=256):
    r, c = w.shape
    tm = min(tm, r)
    two = g_b is not None

    def body(*refs):
        w_ref, m_ref, v_ref, ga_ref = refs[:4]
        g_ref, d_ref, nm_ref, nv_ref = refs[-4:]
        g = ga_ref[...] + refs[4][...] if two else ga_ref[...]
        nm = ADAM_B1 * m_ref[...] + (1.0 - ADAM_B1) * g
        nv = ADAM_B2 * v_ref[...] + (1.0 - ADAM_B2) * (g * g)
        m_hat = nm / (1.0 - ADAM_B1 ** ADAM_STEP)
        v_hat = nv / (1.0 - ADAM_B2 ** ADAM_STEP)
        g_ref[...] = g
        d_ref[...] = -ADAM_LR * (m_hat / (jnp.sqrt(v_hat) + ADAM_EPS) + ADAM_WD * w_ref[...])
        nm_ref[...] = nm
        nv_ref[...] = nv

    args = [w, m, v, g_a] + ([g_b] if two else [])
    return pl.pallas_call(
        body, name=f"adamw_{r}_{c}", grid=(r // tm,),
        in_specs=[_row_spec(tm, c)] * len(args), out_specs=[_row_spec(tm, c)] * 4,
        out_shape=[jax.ShapeDtypeStruct((r, c), F32)] * 4, compiler_params=_cparams(1),
    )(*args)


_SMALL = (("pre_norm_w", (2, D_MODEL)), ("post_norm_w", (2, D_MODEL)), ("attn_b_in", (1, ATTN_IN)),
          ("attn_sinks", (1, N_HEADS)), ("attn_b_out", (1, D_MODEL)), ("rec_lb_logits", (2, D_MODEL)),
          ("rec_gnorm_w", (1, REC_DIM)))
_SMALL_ROWS = 16


def _pack_small(parts, last_row=None):
    rows = []
    for (name, shape) in _SMALL:
        flat = parts[name].reshape(-1)
        pad = -flat.shape[0] % D_MODEL
        rows.append(jnp.pad(flat, (0, pad)).reshape(-1, D_MODEL))
    used = sum(r.shape[0] for r in rows)
    rows.append(jnp.zeros((_SMALL_ROWS - 1 - used, D_MODEL), F32))
    rows.append(jnp.zeros((1, D_MODEL), F32) if last_row is None else last_row)
    return jnp.concatenate(rows, axis=0)


def _unpack_small(packed):
    out, row = {}, 0
    for (name, shape) in _SMALL:
        size = shape[0] * shape[1]
        nrows = -(-size // D_MODEL)
        out[name] = packed[row:row + nrows].reshape(-1)[:size].reshape(shape)
        row += nrows
    return out


_CARRIED = ("rec_w_in", "rec_w_out", "attn_w_out")


_LATE = ("attn_w_out", "rec_w_in", "rec_w_out")


def local_step(x, positions, pre_norm_w, post_norm_w, attn_w_in, attn_b_in, attn_sinks, attn_w_out, attn_b_out,
               rec_w_in, rec_lb_logits, rec_gnorm_w, rec_w_out, loss_target, distributed=False):
    batch, seq, _ = x.shape
    n = batch * seq
    x0 = x.reshape(n, D_MODEL)
    angles = _rope_angles(positions)
    pre0, pre1 = pre_norm_w[0:1], pre_norm_w[1:2]
    post0, post1 = post_norm_w[0:1], post_norm_w[1:2]

    late = (attn_w_out, rec_w_in, rec_w_out) if distributed else ()
    (h0, q, k, v, z), late = attn_in_proj(x0, pre0, attn_w_in, attn_b_in, angles, to_bf16=late)
    sink_tab = _sink_table(attn_sinks)
    og0, gathered = attn_fwd(q, k, v, z, sink_tab, batch, seq, gather=late)
    if distributed:
        attn_w_out, rec_w_in, rec_w_out = (g if name == "rec_w_in" else _whole_from_shards(name, g)
                                           for name, g in zip(_LATE, gathered))
    y0, x1 = out_proj(og0, attn_w_out, attn_b_out, x0, post0)

    h1, proj1 = rec_in_proj(x1, pre1, rec_w_in)
    og1, states, safe = rec_fwd(proj1, rec_lb_logits, rec_gnorm_w, batch, seq)
    dx2, loss_vec, dog1, d_rec_w_out, d_post1, d_rec_w_out_bf16 = out_proj_loss_bwd(
        og1, rec_w_out, x1, post1, loss_target.reshape(n, D_MODEL))
    dproj1, d_lb, d_gnorm = rec_bwd(proj1, states, safe, rec_lb_logits, rec_gnorm_w, dog1, batch, seq)
    dx1, d_pre1, _, _ = in_proj_bwd_x(dproj1, rec_w_in, x1, pre1, dx2)
    d_rec_w_in, _ = in_proj_bwd_w(h1, dproj1, as_shards=distributed)

    dog0, d_attn_w_out, d_attn_b_out, d_post0, d_attn_w_out_bf16 = out_proj_bwd(dx1, y0, og0, attn_w_out, post0)
    ready = dict(rec_w_in=d_rec_w_in, rec_w_out=_shards_from_whole("rec_w_out", d_rec_w_out_bf16),
                 attn_w_out=_shards_from_whole("attn_w_out", d_attn_w_out_bf16))
    outgoing = [ready[name] for name in _CARRIED] if distributed else []
    dproj0, dk, dv, d_sink_tab, arrived = attn_bwd(q, k, v, z, sink_tab, dog0, angles, batch, seq, scatter=outgoing)
    d_sinks = jnp.transpose(jnp.sum(d_sink_tab, axis=-1), (0, 2, 1)).reshape(1, N_HEADS)
    d_attn_w_in, d_attn_b_in, dproj0, *summed = in_proj_bwd_w(h0, dproj0, kv=(dk, dv, angles),
                                                              slot_sums=list(arrived))
    last = [_shards_from_whole("attn_w_in", d_attn_w_in).astype(BF16)] if distributed else []
    dx0, d_pre0, arrived_last, theirs = in_proj_bwd_x(dproj0, attn_w_in, x0, pre0, dx1, scatter=last,
                                                      to_sibling=summed)

    grads = dict(
        pre_norm_w=jnp.concatenate([d_pre0, d_pre1], axis=0), post_norm_w=jnp.concatenate([d_post0, d_post1], axis=0),
        attn_w_in=d_attn_w_in, attn_b_in=d_attn_b_in, attn_sinks=d_sinks, attn_w_out=d_attn_w_out,
        attn_b_out=d_attn_b_out, rec_w_in=d_rec_w_in, rec_lb_logits=d_lb, rec_gnorm_w=d_gnorm,
        rec_w_out=d_rec_w_out)
    exchanged = dict(zip(_CARRIED, zip(summed, theirs)))
    exchanged.update(zip(("attn_w_in",), arrived_last))
    return loss_vec, dx0.reshape(batch, seq, D_MODEL), grads, exchanged


_BIG = ("attn_w_in", "attn_w_out", "rec_w_in", "rec_w_out")
_COLUMN_SHARDED = ("attn_w_in", "rec_w_in")
_ORDER = ("pre_norm_w", "post_norm_w", "attn_w_in", "attn_b_in", "attn_sinks", "attn_w_out", "attn_b_out",
          "rec_w_in", "rec_lb_logits", "rec_gnorm_w", "rec_w_out")


def _whole_from_shards(name, stacked):
    if name in _COLUMN_SHARDED:
        return jnp.transpose(stacked, (1, 0, 2)).reshape(stacked.shape[1], -1)
    return stacked.reshape(-1, stacked.shape[2])


def _shards_from_whole(name, whole):
    if name in _COLUMN_SHARDED:
        return jnp.transpose(whole.reshape(whole.shape[0], N_CHIPS, -1), (1, 0, 2))
    return whole.reshape(N_CHIPS, -1, whole.shape[1])


def kernel(x, positions, pre_norm_w, post_norm_w, attn_w_in, attn_b_in, attn_sinks, attn_w_out, attn_b_out, rec_w_in, rec_lb_logits, rec_gnorm_w, rec_w_out, loss_target, m_pre_norm_w, m_post_norm_w, m_attn_w_in, m_attn_b_in, m_attn_sinks, m_attn_w_out, m_attn_b_out, m_rec_w_in, m_rec_lb_logits, m_rec_gnorm_w, m_rec_w_out, v_pre_norm_w, v_post_norm_w, v_attn_w_in, v_attn_b_in, v_attn_sinks, v_attn_w_out, v_attn_b_out, v_rec_w_in, v_rec_lb_logits, v_rec_gnorm_w, v_rec_w_out):
    w = dict(pre_norm_w=pre_norm_w, post_norm_w=post_norm_w, attn_w_in=attn_w_in, attn_b_in=attn_b_in,
             attn_sinks=attn_sinks, attn_w_out=attn_w_out, attn_b_out=attn_b_out, rec_w_in=rec_w_in,
             rec_lb_logits=rec_lb_logits, rec_gnorm_w=rec_gnorm_w, rec_w_out=rec_w_out)
    m = dict(pre_norm_w=m_pre_norm_w, post_norm_w=m_post_norm_w, attn_w_in=m_attn_w_in, attn_b_in=m_attn_b_in,
             attn_sinks=m_attn_sinks, attn_w_out=m_attn_w_out, attn_b_out=m_attn_b_out, rec_w_in=m_rec_w_in,
             rec_lb_logits=m_rec_lb_logits, rec_gnorm_w=m_rec_gnorm_w, rec_w_out=m_rec_w_out)
    v = dict(pre_norm_w=v_pre_norm_w, post_norm_w=v_post_norm_w, attn_w_in=v_attn_w_in, attn_b_in=v_attn_b_in,
             attn_sinks=v_attn_sinks, attn_w_out=v_attn_w_out, attn_b_out=v_attn_b_out, rec_w_in=v_rec_w_in,
             rec_lb_logits=v_rec_lb_logits, rec_gnorm_w=v_rec_gnorm_w, rec_w_out=v_rec_w_out)

    shards = {name: w[name][0] for name in _BIG}
    attn_w_in_whole = _whole_from_shards("attn_w_in", chip_gather([shards["attn_w_in"].astype(BF16)])[0])

    loss_vec, grad_x, grads, exchanged = local_step(
        x, positions, pre_norm_w, post_norm_w, attn_w_in_whole, attn_b_in, attn_sinks, shards["attn_w_out"],
        attn_b_out, shards["rec_w_in"], rec_lb_logits, rec_gnorm_w, shards["rec_w_out"], loss_target,
        distributed=True)

    mine = sum_slots(exchanged["attn_w_in"])
    exchanged["attn_w_in"] = (mine, sibling_exchange([mine])[0])
    out_g, out_d, out_m, out_v = {}, {}, {}, {}
    for name in _BIG:
        mine, other = exchanged[name]
        g, d, nm, nv = adamw(shards[name], m[name][0], v[name][0], mine, other)
        out_g[name], out_d[name], out_m[name], out_v[name] = g[None], d[None], nm[None], nv[None]

    small_sum = sum_slots(all_gather_small(_pack_small(grads, last_row=loss_vec)))
    loss = jnp.sum(small_sum[_SMALL_ROWS - 1]) * (0.5 / D_MODEL)
    packed = adamw(_pack_small(w), _pack_small(m), _pack_small(v), small_sum)
    for dst, val in zip((out_g, out_d, out_m, out_v), packed):
        dst.update(_unpack_small(val))

    return (loss, grad_x, *[out_g[n] for n in _ORDER], *[out_d[n] for n in _ORDER],
            *[out_m[n] for n in _ORDER], *[out_v[n] for n in _ORDER])
```

```python
import functools

import jax
import jax.numpy as jnp
from jax import lax
from jax.experimental import pallas as pl
from jax.experimental.pallas import tpu as pltpu

F32 = jnp.float32
BF16 = jnp.bfloat16
MESH = pl.DeviceIdType.MESH

D_MODEL = 1024
HEAD_DIM = 64
N_HEADS = 16
N_KV_HEADS = 2
GROUP = N_HEADS // N_KV_HEADS
KV_WIDTH = N_KV_HEADS * HEAD_DIM
ATTN_IN = 2 * D_MODEL + 2 * KV_WIDTH
ATTN_BLOCK = 128
ROPE_THETA = 500000.0
ROPE_DIM = HEAD_DIM // 4
REC_HEADS = 8
REC_DIM = 128
REC_IN = 4 * D_MODEL
REC_BLOCK = 128
DIAG = 8
NORM_EPS = 1e-6
N_CHIPS = 4
N_DEV = 8
LANES = 128

ADAM_LR = 0.001
ADAM_B1 = 0.9
ADAM_B2 = 0.999
ADAM_EPS = 1e-08
ADAM_WD = 0.01
ADAM_STEP = 10

VMEM_LIMIT = 56 * 1024 * 1024


def _cparams(n_axes):
    return pltpu.CompilerParams(dimension_semantics=("arbitrary",) * n_axes, vmem_limit_bytes=VMEM_LIMIT)


def _dot(a, b, contract):
    return lax.dot_general(a.astype(BF16), b.astype(BF16), (contract, ((), ())), preferred_element_type=F32)


_NN = ((1,), (0,))
_NT = ((1,), (1,))
_TN = ((0,), (0,))


@jax.custom_vjp
def mm_nn(a, b):
    return _dot(a, b, _NN)


mm_nn.defvjp(lambda a, b: (_dot(a, b, _NN), (a, b)),
             lambda res, g: (_dot(g, res[1], _NT), _dot(res[0], g, _TN)))


@jax.custom_vjp
def mm_nt(a, b):
    return _dot(a, b, _NT)


mm_nt.defvjp(lambda a, b: (_dot(a, b, _NT), (a, b)),
             lambda res, g: (_dot(g, res[1], _NN), _dot(g, res[0], _TN)))


@jax.custom_vjp
def mm_tn(a, b):
    return _dot(a, b, _TN)


mm_tn.defvjp(lambda a, b: (_dot(a, b, _TN), (a, b)),
             lambda res, g: (_dot(res[1], g, _NT), _dot(res[0], g, _NN)))


def _tri_dot(x, lower):
    n = x.shape[0]
    r = lax.broadcasted_iota(jnp.int32, (n, n), 0)
    c = lax.broadcasted_iota(jnp.int32, (n, n), 1)
    tri = ((c <= r) if lower else (c >= r)).astype(BF16)
    hi = x.astype(BF16)
    rest = x - hi.astype(F32)
    mid = rest.astype(BF16)
    lo = (rest - mid.astype(F32)).astype(BF16)
    dot = lambda p: lax.dot_general(tri, p, (_NN, ((), ())), preferred_element_type=F32)
    return (dot(lo) + dot(mid)) + dot(hi)


@jax.custom_vjp
def cumsum_rows(x):
    return _tri_dot(x, True)


cumsum_rows.defvjp(lambda x: (cumsum_rows(x), None), lambda _, g: (_tri_dot(g, False),))


@functools.partial(jax.custom_vjp, nondiff_argnums=(1,))
def roll_sub(x, d):
    return pltpu.roll(x, d, 1) if d else x


roll_sub.defvjp(lambda x, d: (roll_sub(x, d), None),
                lambda d, _, g: (roll_sub(g, (DIAG - d) % DIAG),))


def sigmoid(x):
    return 1.0 / (1.0 + jnp.exp(-x))


@jax.custom_vjp
def silu(x):
    return x * sigmoid(x)


def _silu_fwd(x):
    s = sigmoid(x)
    return x * s, (x, s)


silu.defvjp(_silu_fwd, lambda res, g: (g * (res[1] * (1.0 + res[0] * (1.0 - res[1]))),))


F32_TINY = 1.17549435e-38


def sigmoid_pair(x):
    e = jnp.exp(-jnp.abs(x))
    r = 1.0 / (1.0 + e)
    er = e * r
    pos = x >= 0.0
    return jnp.where(pos, r, er), jnp.where(pos, er, r)


def _forget_fwd(x, a):
    lb, one_m_lb = sigmoid_pair(a)
    sp, sn = sigmoid_pair(x)
    f = lb + one_m_lb * sp
    k = one_m_lb * sn
    return (jnp.log(jnp.maximum(f, F32_TINY)), k), (sp, sn, f, k, lb, one_m_lb)


def _forget_bwd(res, g):
    sp, sn, f, k, lb, one_m_lb = res
    g_lf, g_k = g
    t = jnp.where(f >= F32_TINY, g_lf / jnp.maximum(f, F32_TINY), 0.0) - g_k
    return (k * sp) * t, jnp.sum(sn * t, axis=0, keepdims=True) * (lb * one_m_lb)


@jax.custom_vjp
def forget_gate(x, a):
    return _forget_fwd(x, a)[0]


forget_gate.defvjp(_forget_fwd, _forget_bwd)


@jax.custom_vjp
def decayed(x, e):
    return (x * jnp.exp(e)).astype(BF16).astype(F32)


def _decayed_fwd(x, e):
    y = decayed(x, e)
    return y, (y, e)


decayed.defvjp(_decayed_fwd, lambda res, g: (g * jnp.exp(res[1]), g * res[0]))


def _row(x, r):
    shape = x.shape

    @jax.custom_vjp
    def take(x):
        return x[r:r + 1, :]

    take.defvjp(lambda x: (x[r:r + 1, :], None),
                lambda _, g: (jnp.where(lax.broadcasted_iota(jnp.int32, shape, 0) == r, g, 0.0),))
    return take(x)


def _rms(x):
    return lax.rsqrt(jnp.mean(x * x, axis=-1, keepdims=True) + NORM_EPS)


def _attn_group(qs, k_a, v_a, k_b, v_b, zs, sink_a, sink_b, bias, at_sink=None):
    def half(kh, vh, sink):
        s = mm_nn(qs, kh) + bias
        if at_sink is None:
            m = jnp.maximum(jnp.max(s, axis=-1, keepdims=True), jnp.max(sink, axis=-1, keepdims=True))
            p = jnp.exp(s - lax.stop_gradient(m))
            own = jnp.sum(jnp.exp(sink - lax.stop_gradient(m)), axis=-1, keepdims=True) * (1.0 / LANES)
            return mm_nt(p * (1.0 / (jnp.sum(p, axis=-1, keepdims=True) + own)), vh)
        s = jnp.where(at_sink, jnp.concatenate([sink, sink], axis=1), s)
        p = jnp.exp(s - jnp.max(s, axis=-1, keepdims=True))
        return mm_nt(jnp.where(at_sink, 0.0, p), vh) * (1.0 / jnp.sum(p, axis=-1, keepdims=True))

    return (half(k_a, v_a, sink_a) + half(k_b, v_b, sink_b)) * silu(zs)


SAFE_RANGE = 80.0


def _rec_front(qr, fr, l0, l1):
    lf, k = forget_gate(fr, l1 - l0)
    return silu(qr), k, lf


def _rec_tail(o, z, gw):
    return o * _rms(o) * gw * silu(z)


def _rec_margin(b):
    R = b.shape[0]
    mid, last = _row(b, R // 2 - 1), _row(b, R - 1)
    return jnp.minimum(mid, last - mid)


def _heads(x):
    w = x.shape[1] // REC_HEADS
    return [x[:, h * w:(h + 1) * w] for h in range(REC_HEADS)]


def _hdot(a, b, contract):
    return jnp.concatenate([_dot(ah, bh, contract) for ah, bh in zip(_heads(a), _heads(b))], axis=1)


@jax.custom_vjp
def hmm_nn(a, b):
    return _hdot(a, b, _NN)


hmm_nn.defvjp(lambda a, b: (_hdot(a, b, _NN), (a, b)),
              lambda res, g: (_hdot(g, res[1], _NT), _hdot(res[0], g, _TN)))


@jax.custom_vjp
def hmm_nt(a, b):
    return _hdot(a, b, _NT)


hmm_nt.defvjp(lambda a, b: (_hdot(a, b, _NT), (a, b)),
              lambda res, g: (_hdot(g, res[1], _NN), _hdot(g, res[0], _TN)))


@jax.custom_vjp
def hmm_tn(a, b):
    return _hdot(a, b, _TN)


hmm_tn.defvjp(lambda a, b: (_hdot(a, b, _TN), (a, b)),
              lambda res, g: (_hdot(res[1], g, _NT), _hdot(res[0], g, _NN)))


def _head_sums(x):
    return jnp.concatenate([jnp.broadcast_to(jnp.sum(xh, axis=-1, keepdims=True), xh.shape) for xh in _heads(x)],
                           axis=1)


@jax.custom_vjp
def head_sum(x):
    return _head_sums(x)


head_sum.defvjp(lambda x: (_head_sums(x), None), lambda _, g: (_head_sums(g),))


def _rec_cores_fast(q, k, v, b, S):
    R = q.shape[0]
    ri = lax.broadcasted_iota(jnp.int32, (R, REC_HEADS * R), 0)
    ci = lax.broadcasted_iota(jnp.int32, (R, REC_HEADS * R), 1) % R
    d = b - _row(b, R // 2 - 1)
    sc = jnp.where(ci < ri, hmm_nt(decayed(q, d), decayed(k, -d)), 0.0)
    o = hmm_nt(q * jnp.exp(b), S) + hmm_nn(sc, v) + head_sum(q * k) * v
    b_last = _row(b, R - 1)
    return o, S * jnp.exp(b_last) + hmm_tn(v, k * jnp.exp(b_last - b))


def _rec_tails(o, z, gw):
    return o * lax.rsqrt(head_sum(o * o) * (1.0 / REC_DIM) + NORM_EPS) * gw * silu(z)


def _rec_block_fast(qr, fr, v, z, S, l0, l1, gw):
    lf, k = forget_gate(fr, l1 - l0)
    o, S_new = _rec_cores_fast(silu(qr), k, v, cumsum_rows(lf), S)
    return _rec_tails(o, z, gw), S_new


def _rec_core_slow(q, k, v, b, S):
    R = q.shape[0]
    rows = lax.broadcasted_iota(jnp.int32, (R, REC_DIM), 0)

    o = mm_nt(q * jnp.exp(jnp.minimum(b, 0.0)), S)

    ri = lax.broadcasted_iota(jnp.int32, (R, R), 0)
    ci = lax.broadcasted_iota(jnp.int32, (R, R), 1)
    sc = jnp.zeros((R, R), F32)
    w = R
    while w > DIAG:
        h = w // 2
        b3 = b.reshape(R // w, w, REC_DIM)
        rin = lax.broadcasted_iota(jnp.int32, (R // w, w, REC_DIM), 1)
        mid = jnp.sum(jnp.where(rin == h - 1, b3, 0.0), axis=1, keepdims=True)
        fac = jnp.exp(jnp.minimum(jnp.where(rin >= h, b3 - mid, mid - b3), 0.0)).reshape(R, REC_DIM)
        upper = (rows % w) >= h
        s_w = mm_nt(jnp.where(upper, q * fac, 0.0), jnp.where(upper, 0.0, k * fac))
        sc = sc + jnp.where((ri // w) == (ci // w), s_w, 0.0)
        w = h
    o = o + mm_nn(sc, v)

    g = R // DIAG
    q3, k3, v3, b3 = (t.reshape(g, DIAG, REC_DIM) for t in (q, k, v, b))
    rin = lax.broadcasted_iota(jnp.int32, (g, DIAG, 1), 1)
    od = jnp.zeros((g, DIAG, REC_DIM), F32)
    for d in range(DIAG):
        e = jnp.exp(jnp.minimum(b3 - roll_sub(b3, d), 0.0))
        sd = jnp.sum(q3 * roll_sub(k3, d) * e, axis=-1, keepdims=True)
        od = od + jnp.where(rin >= d, sd, 0.0) * roll_sub(v3, d)
    o = o + od.reshape(R, REC_DIM)

    b_last = _row(b, R - 1)
    return o, S * jnp.exp(jnp.minimum(b_last, 0.0)) + mm_tn(v, k * jnp.exp(jnp.minimum(b_last - b, 0.0)))


def _rec_head(core, qr, fr, v, z, S, l0, l1, gw):
    q, k, lf = _rec_front(qr, fr, l0, l1)
    o, S_new = core(q, k, v, cumsum_rows(lf), S)
    return _rec_tail(o, z, gw), S_new


ANGLE_COLS = 3 * ROPE_DIM


def _rope_angles(positions):
    half = ROPE_DIM // 2
    inv_freq = ROPE_THETA ** (-(jnp.arange(half, dtype=F32) * 2.0 / ROPE_DIM))
    ang = positions.astype(F32).reshape(-1, 1) * inv_freq
    cs = jnp.concatenate([jnp.cos(ang), jnp.sin(ang)], axis=-1)
    hi = cs.astype(BF16)
    rest = cs - hi.astype(F32)
    mid = rest.astype(BF16)
    return jnp.concatenate([hi, mid, (rest - mid.astype(F32)).astype(BF16)], axis=-1)


def _rope_tables(pieces):
    half = ROPE_DIM // 2
    r = lax.broadcasted_iota(jnp.int32, (ANGLE_COLS, 3 * LANES), 0) % ROPE_DIM
    c = lax.broadcasted_iota(jnp.int32, (ANGLE_COLS, 3 * LANES), 1)
    table, j = c // LANES, c % HEAD_DIM
    angle, low = j % half, j < half
    plus = ((table == 0) & (j < ROPE_DIM) & (r == angle)) | ((table == 1) & (j >= half) & (j < ROPE_DIM)
                                                                & (r == half + angle))
    minus = (table == 2) & low & (r == half + angle)
    pick = jnp.where(plus, 1.0, jnp.where(minus, -1.0, 0.0)).astype(BF16)
    out = jnp.dot(pieces, pick, preferred_element_type=F32)
    lane = lax.broadcasted_iota(jnp.int32, (1, LANES), 1) % HEAD_DIM
    return out[:, :LANES] + jnp.where(lane < ROPE_DIM, 0.0, 1.0), out[:, LANES:2 * LANES], out[:, 2 * LANES:]


def _rope(x, cos_t, sin_a, sin_b):
    half = ROPE_DIM // 2
    return x * cos_t + pltpu.roll(x, half, 1) * sin_a + pltpu.roll(x, LANES - half, 1) * sin_b


def _rope_transposed(g, cos_t, sin_a, sin_b):
    half = ROPE_DIM // 2
    return g * cos_t + pltpu.roll(g * sin_a, LANES - half, 1) + pltpu.roll(g * sin_b, half, 1)


def _row_spec(tm, width):
    return pl.BlockSpec((tm, width), lambda i: (i, 0))


def _weight_spec(shape):
    return pl.BlockSpec(shape, lambda *_: (0,) * len(shape), pipeline_mode=pl.Buffered(1))


def _full_spec(shape):
    return pl.BlockSpec(shape, lambda *_: (0,) * len(shape))


def attn_in_proj(x, w_pre, w_in, b_in, angles, tm=1024, to_bf16=()):
    n = x.shape[0]
    tm = min(tm, n)
    nc = len(to_bf16)

    def body(*refs):
        x_ref, wp_ref, w_ref, b_ref, cs_ref = refs[:5]
        h_ref, q_ref, k_ref, v_ref, z_ref = refs[5 + nc:10 + nc]

        @pl.when(pl.program_id(0) == 0)
        def _():
            for src, dst in zip(refs[5:5 + nc], refs[10 + nc:]):
                dst[...] = src[...].astype(BF16)

        xv = x_ref[...]
        h = (xv * _rms(xv) * wp_ref[...]).astype(BF16)
        h_ref[...] = h
        proj = jnp.dot(h, w_ref[...], preferred_element_type=F32) + b_ref[...]
        tabs = _rope_tables(cs_ref[...])
        for s in range(D_MODEL // LANES):
            sl = slice(s * LANES, (s + 1) * LANES)
            q_ref[:, sl] = _rope(proj[:, sl] * (HEAD_DIM ** -0.5), *tabs).astype(BF16)
        k_ref[...] = _rope(proj[:, D_MODEL:D_MODEL + KV_WIDTH], *tabs).astype(BF16)
        v_ref[...] = proj[:, D_MODEL + KV_WIDTH:D_MODEL + 2 * KV_WIDTH].astype(BF16)
        z_ref[...] = proj[:, D_MODEL + 2 * KV_WIDTH:]

    out = pl.pallas_call(
        body, name="attn_in_proj", grid=(n // tm,),
        in_specs=[_row_spec(tm, D_MODEL), _full_spec((1, D_MODEL)), _weight_spec((D_MODEL, ATTN_IN)),
                  _full_spec((1, ATTN_IN)), _row_spec(tm, ANGLE_COLS)] + [_weight_spec(a.shape) for a in to_bf16],
        out_specs=[_row_spec(tm, D_MODEL), _row_spec(tm, D_MODEL), _row_spec(tm, KV_WIDTH),
                   _row_spec(tm, KV_WIDTH), _row_spec(tm, D_MODEL)] + [_full_spec(a.shape) for a in to_bf16],
        out_shape=[jax.ShapeDtypeStruct((n, D_MODEL), BF16), jax.ShapeDtypeStruct((n, D_MODEL), BF16),
                   jax.ShapeDtypeStruct((n, KV_WIDTH), BF16), jax.ShapeDtypeStruct((n, KV_WIDTH), BF16),
                   jax.ShapeDtypeStruct((n, D_MODEL), F32)] + [jax.ShapeDtypeStruct(a.shape, BF16) for a in to_bf16],
        compiler_params=_cparams(1),
    )(x, w_pre, w_in, b_in, angles, *to_bf16)
    return out[:5], out[5:]


def _column_blocks(w):
    if len(w.shape) == 2:
        return [slice(0, w.shape[1])], lambda ref, s: ref[...]
    width = w.shape[2]
    return [slice(s * width, (s + 1) * width) for s in range(w.shape[0])], lambda ref, s: ref[s]


def rec_in_proj(x, w_pre, w_in, tm=1024):
    n = x.shape[0]
    tm = min(tm, n)
    columns, block = _column_blocks(w_in)

    def body(x_ref, wp_ref, w_ref, h_ref, p_ref):
        xv = x_ref[...]
        h = (xv * _rms(xv) * wp_ref[...]).astype(BF16)
        h_ref[...] = h
        for s, cols in enumerate(columns):
            p_ref[:, cols] = jnp.dot(h, block(w_ref, s), preferred_element_type=F32)

    return pl.pallas_call(
        body, name="rec_in_proj", grid=(n // tm,),
        in_specs=[_row_spec(tm, D_MODEL), _full_spec((1, D_MODEL)), _weight_spec(w_in.shape)],
        out_specs=[_row_spec(tm, D_MODEL), _row_spec(tm, REC_IN)],
        out_shape=[jax.ShapeDtypeStruct((n, D_MODEL), BF16), jax.ShapeDtypeStruct((n, REC_IN), F32)],
        compiler_params=_cparams(1),
    )(x, w_pre, w_in)


def out_proj(og, w_out, b_out, x_res, w_post, tm=1024):
    n = og.shape[0]
    tm = min(tm, n)

    def body(og_ref, w_ref, b_ref, x_ref, wp_ref, y_ref, xo_ref):
        y = jnp.dot(og_ref[...], w_ref[...], preferred_element_type=F32) + b_ref[...]
        y_ref[...] = y.astype(BF16)
        xo_ref[...] = x_ref[...] + y * _rms(y) * wp_ref[...]

    return pl.pallas_call(
        body, name="out_proj", grid=(n // tm,),
        in_specs=[_row_spec(tm, D_MODEL), _weight_spec((D_MODEL, D_MODEL)), _full_spec((1, D_MODEL)),
                  _row_spec(tm, D_MODEL), _full_spec((1, D_MODEL))],
        out_specs=[_row_spec(tm, D_MODEL), _row_spec(tm, D_MODEL)],
        out_shape=[jax.ShapeDtypeStruct((n, D_MODEL), BF16), jax.ShapeDtypeStruct((n, D_MODEL), F32)],
        compiler_params=_cparams(1),
    )(og, w_out, b_out, x_res, w_post)


def _post_norm_bwd(g, y, w_post):
    rstd = _rms(y)
    yn = y * rstd
    gw = g * w_post
    return rstd * (gw - yn * jnp.mean(gw * yn, axis=-1, keepdims=True)), jnp.sum(g * yn, axis=0, keepdims=True)


def out_proj_loss_bwd(og, w_out, x_res, w_post, target, tm=1024):
    n = og.shape[0]
    tm = min(tm, n)
    steps = n // tm

    def body(og_ref, w_ref, x_ref, wp_ref, t_ref, dx_ref, l_ref, dog_ref, dw_ref, dwp_ref, dwb_ref):
        @pl.when(pl.program_id(0) == 0)
        def _():
            l_ref[...] = jnp.zeros_like(l_ref)
            dw_ref[...] = jnp.zeros_like(dw_ref)
            dwp_ref[...] = jnp.zeros_like(dwp_ref)

        og_tile = og_ref[...]
        y = jnp.dot(og_tile, w_ref[...], preferred_element_type=F32)
        err = x_ref[...] + y * _rms(y) * wp_ref[...] - t_ref[...]
        g = err * (1.0 / D_MODEL)
        dx_ref[...] = g
        l_ref[...] += jnp.sum(err * err, axis=0, keepdims=True)
        dy, dwp = _post_norm_bwd(g, y, wp_ref[...])
        dwp_ref[...] += dwp
        dyb = dy.astype(BF16)
        dog_ref[...] = _dot(dyb, w_ref[...], _NT).astype(BF16)
        dw_ref[...] += _dot(og_tile, dyb, _TN)

        @pl.when(pl.program_id(0) == steps - 1)
        def _():
            dwb_ref[...] = dw_ref[...].astype(BF16)

    return pl.pallas_call(
        body, name="out_proj_loss_bwd", grid=(steps,),
        in_specs=[_row_spec(tm, D_MODEL), _weight_spec((D_MODEL, D_MODEL)), _row_spec(tm, D_MODEL),
                  _full_spec((1, D_MODEL)), _row_spec(tm, D_MODEL)],
        out_specs=[_row_spec(tm, D_MODEL), _full_spec((1, D_MODEL)), _row_spec(tm, D_MODEL),
                   _full_spec((D_MODEL, D_MODEL)), _full_spec((1, D_MODEL)), _full_spec((D_MODEL, D_MODEL))],
        out_shape=[jax.ShapeDtypeStruct((n, D_MODEL), F32), jax.ShapeDtypeStruct((1, D_MODEL), F32),
                   jax.ShapeDtypeStruct((n, D_MODEL), BF16), jax.ShapeDtypeStruct((D_MODEL, D_MODEL), F32),
                   jax.ShapeDtypeStruct((1, D_MODEL), F32), jax.ShapeDtypeStruct((D_MODEL, D_MODEL), BF16)],
        compiler_params=_cparams(1),
    )(og, w_out, x_res, w_post, target)


def out_proj_bwd(dxo, y, og, w_out, w_post, tm=1024):
    n = og.shape[0]
    tm = min(tm, n)
    steps = n // tm

    def body(g_ref, y_ref, og_ref, w_ref, wp_ref, dog_ref, dw_ref, db_ref, dwp_ref, dwb_ref):
        @pl.when(pl.program_id(0) == 0)
        def _():
            dw_ref[...] = jnp.zeros_like(dw_ref)
            db_ref[...] = jnp.zeros_like(db_ref)
            dwp_ref[...] = jnp.zeros_like(dwp_ref)

        dy, dwp = _post_norm_bwd(g_ref[...], y_ref[...].astype(F32), wp_ref[...])
        dwp_ref[...] += dwp
        db_ref[...] += jnp.sum(dy, axis=0, keepdims=True)
        dyb = dy.astype(BF16)
        dog_ref[...] = _dot(dyb, w_ref[...], _NT).astype(BF16)
        dw_ref[...] += _dot(og_ref[...], dyb, _TN)

        @pl.when(pl.program_id(0) == steps - 1)
        def _():
            dwb_ref[...] = dw_ref[...].astype(BF16)

    return pl.pallas_call(
        body, name="out_proj_bwd", grid=(steps,),
        in_specs=[_row_spec(tm, D_MODEL), _row_spec(tm, D_MODEL), _row_spec(tm, D_MODEL),
                  _weight_spec((D_MODEL, D_MODEL)), _full_spec((1, D_MODEL))],
        out_specs=[_row_spec(tm, D_MODEL), _full_spec((D_MODEL, D_MODEL)), _full_spec((1, D_MODEL)),
                   _full_spec((1, D_MODEL)), _full_spec((D_MODEL, D_MODEL))],
        out_shape=[jax.ShapeDtypeStruct((n, D_MODEL), BF16), jax.ShapeDtypeStruct((D_MODEL, D_MODEL), F32),
                   jax.ShapeDtypeStruct((1, D_MODEL), F32), jax.ShapeDtypeStruct((1, D_MODEL), F32),
                   jax.ShapeDtypeStruct((D_MODEL, D_MODEL), BF16)],
        compiler_params=_cparams(1),
    )(dxo, y, og, w_out, w_post)


def _slot_sum_specs(slot_sums, steps):
    return ([pl.BlockSpec((a.shape[0], a.shape[1] // steps, a.shape[2]), lambda i: (0, i, 0)) for a in slot_sums],
            [_row_spec(a.shape[1] // steps, a.shape[2]) for a in slot_sums],
            [jax.ShapeDtypeStruct(a.shape[1:], F32) for a in slot_sums])


def _sum_slots_into(slot_refs, sum_refs):
    for slots_ref, sum_ref in zip(slot_refs, sum_refs):
        sum_ref[...] = functools.reduce(jnp.add, [slots_ref[t].astype(F32) for t in range(slots_ref.shape[0])])


def in_proj_bwd_x(dproj, w_in, x, w_pre, dxo, tm=1024, scatter=(), to_sibling=()):
    n, p = dproj.shape
    tm = min(tm, n)
    steps = n // tm
    ns, nsib = len(scatter), len(to_sibling)
    columns, block = _column_blocks(w_in)

    def body(*refs):
        dp_ref, w_ref, x_ref, wp_ref, g_ref = refs[:5]
        outs = 5 + ns + nsib
        dx_ref, dwp_ref = refs[outs:outs + 2]
        sems = refs[outs + 2 + ns + nsib:]
        exchange = (refs[5:5 + ns], refs[outs + 2:outs + 2 + ns]) + tuple(sems[:3])
        sibling = (refs[5 + ns:outs], refs[outs + 2 + ns:outs + 2 + ns + nsib]) + tuple(sems[3:])

        @pl.when(pl.program_id(0) == 0)
        def _():
            dwp_ref[...] = jnp.zeros_like(dwp_ref)
            if ns:
                _scatter_start(*exchange)
            for cp in _sibling_copies(*sibling) if nsib else ():
                cp.start()

        dh = functools.reduce(jnp.add, [_dot(dp_ref[:, cols], block(w_ref, s), _NT)
                                        for s, cols in enumerate(columns)])
        xv = x_ref[...]
        rstd = _rms(xv)
        xn = xv * rstd
        gw = dh * wp_ref[...]
        dwp_ref[...] += jnp.sum(dh * xn, axis=0, keepdims=True)
        dx_ref[...] = rstd * (gw - xn * jnp.mean(gw * xn, axis=-1, keepdims=True)) + g_ref[...]

        if ns or nsib:
            @pl.when(pl.program_id(0) == steps - 1)
            def _():
                if ns:
                    _scatter_finish(*exchange)
                for cp in _sibling_copies(*sibling) if nsib else ():
                    cp.wait()

    out = pl.pallas_call(
        body, name=f"in_proj_bwd_x_{p}", grid=(steps,),
        in_specs=[_row_spec(tm, p), _weight_spec(w_in.shape), _row_spec(tm, D_MODEL), _full_spec((1, D_MODEL)),
                  _row_spec(tm, D_MODEL)] + [_ANY] * (ns + nsib),
        out_specs=[_row_spec(tm, D_MODEL), _full_spec((1, D_MODEL))] + [_ANY] * (ns + nsib),
        out_shape=[jax.ShapeDtypeStruct((n, D_MODEL), F32), jax.ShapeDtypeStruct((1, D_MODEL), F32)]
        + [jax.ShapeDtypeStruct(a.shape, a.dtype) for a in tuple(scatter) + tuple(to_sibling)],
        scratch_shapes=(_scatter_sems(ns) if ns else []) + (_sibling_sems(nsib) if nsib else []),
        compiler_params=_cparams(1),
    )(dproj, w_in, x, w_pre, dxo, *scatter, *to_sibling)
    return out[0], out[1], out[2:2 + ns], out[2 + ns:]


def in_proj_bwd_w(h, dproj, tm=1024, as_shards=False, kv=None, slot_sums=()):
    n, p = dproj.shape
    chunk = p // (4 if p % 4096 == 0 else 3)
    tm = min(tm, n)
    steps = n // tm
    shard = p // N_CHIPS
    n_kv = 0 if kv is None else 3
    n_sum = len(slot_sums)
    kv_from, kv_to = D_MODEL, D_MODEL + 2 * KV_WIDTH

    def body(*refs):
        h_ref, dp_ref = refs[:2]
        outs = 2 + n_kv + n_sum
        dw_ref, db_ref = refs[outs:outs + 2]
        sums_at = outs + 2 + (kv is not None)
        scratch = refs[sums_at + n_sum:]
        acc_scr, sem, staging = scratch[0], scratch[1], scratch[2:]
        i = pl.program_id(0)
        _sum_slots_into(refs[2 + n_kv:outs], refs[sums_at:sums_at + n_sum])

        @pl.when(i == 0)
        def _():
            acc_scr[...] = jnp.zeros_like(acc_scr)
            db_ref[...] = jnp.zeros_like(db_ref)

        if kv is not None:
            dk_ref, dv_ref, cs_ref, kv_ref = refs[2], refs[3], refs[4], refs[outs + 2]
            made = jnp.concatenate([_rope_transposed(dk_ref[...].T, *_rope_tables(cs_ref[...])), dv_ref[...].T],
                                   axis=1).astype(BF16)
            kv_ref[...] = made

        def columns(c0):
            if kv is None or c0 + chunk <= kv_from or c0 >= kv_to:
                return dp_ref[:, c0:c0 + chunk]
            return jnp.concatenate([dp_ref[:, c0:kv_from], made, dp_ref[:, kv_to:c0 + chunk]], axis=1)

        ht = h_ref[...].T
        for c0 in range(0, p, chunk):
            dp = columns(c0)
            acc_scr[:, c0:c0 + chunk] += jnp.dot(ht, dp, preferred_element_type=F32)
            db_ref[:, c0:c0 + chunk] += jnp.sum(dp.astype(F32), axis=0, keepdims=True)

        @pl.when(i == steps - 1)
        def _():
            if as_shards:
                for s in range(N_CHIPS):
                    staging[0][...] = acc_scr[:, s * shard:(s + 1) * shard].astype(BF16)
                    out = pltpu.make_async_copy(staging[0], dw_ref.at[s], sem)
                    out.start()
                    out.wait()
            else:
                out = pltpu.make_async_copy(acc_scr, dw_ref, sem)
                out.start()
                out.wait()

    dw_shape = jax.ShapeDtypeStruct((N_CHIPS, D_MODEL, shard), BF16) if as_shards else (
        jax.ShapeDtypeStruct((D_MODEL, p), F32))
    in_specs = [_row_spec(tm, D_MODEL), _row_spec(tm, p)]
    out_specs = [_ANY, _full_spec((1, p))]
    out_shape = [dw_shape, jax.ShapeDtypeStruct((1, p), F32)]
    if kv is not None:
        columns_t = pl.BlockSpec((KV_WIDTH, tm), lambda i: (0, i))
        in_specs += [columns_t, columns_t, _row_spec(tm, ANGLE_COLS)]
        out_specs.append(pl.BlockSpec((tm, kv_to - kv_from), lambda i: (i, kv_from // (kv_to - kv_from))))
        out_shape.append(jax.ShapeDtypeStruct(dproj.shape, dproj.dtype))
    sum_in, sum_out, sum_shapes = _slot_sum_specs(slot_sums, steps)
    in_specs, out_specs, out_shape = in_specs + sum_in, out_specs + sum_out, out_shape + sum_shapes
    return pl.pallas_call(
        body, name=f"in_proj_bwd_w_{p}", grid=(steps,),
        in_specs=in_specs, out_specs=out_specs, out_shape=out_shape,
        scratch_shapes=[pltpu.VMEM((D_MODEL, p), F32), pltpu.SemaphoreType.DMA]
        + ([pltpu.VMEM((D_MODEL, shard), BF16)] if as_shards else []),
        input_output_aliases={1: 2} if kv is not None else {},
        compiler_params=_cparams(1),
    )(h, dproj, *(kv or ()), *slot_sums)


PAIRS = GROUP // 2
GROUP_ROWS = PAIRS * ATTN_BLOCK
MASKED = -1e30


def _kv_windows(k_ref, v_ref, i):
    ps = pl.multiple_of(jnp.maximum(i - 1, 0) * ATTN_BLOCK, ATTN_BLOCK)
    cs = pl.multiple_of(i * ATTN_BLOCK, ATTN_BLOCK)
    kw = jnp.concatenate([k_ref[pl.ds(ps, ATTN_BLOCK), :], k_ref[pl.ds(cs, ATTN_BLOCK), :]], axis=0)
    vw = jnp.concatenate([v_ref[pl.ds(ps, ATTN_BLOCK), :], v_ref[pl.ds(cs, ATTN_BLOCK), :]], axis=0)
    return kw.astype(F32).T, vw.astype(F32).T, ps, cs


def _low_rows(shape):
    return lax.broadcasted_iota(jnp.int32, shape, 0) < HEAD_DIM


def _spread(w, kvh):
    low = _low_rows(w.shape)
    swapped = pltpu.roll(w, HEAD_DIM, 0)
    if kvh == 0:
        return jnp.where(low, w, 0.0), jnp.where(low, 0.0, swapped)
    return jnp.where(low, swapped, 0.0), jnp.where(low, 0.0, w)


def _unspread(d_a, d_b, kvh):
    low = _low_rows(d_a.shape)
    if kvh == 0:
        return jnp.where(low, d_a + pltpu.roll(d_b, HEAD_DIM, 0), 0.0)
    return jnp.where(low, 0.0, pltpu.roll(d_a, HEAD_DIM, 0) + d_b)


def _stack_pairs(ref, kvh):
    return jnp.concatenate([ref[:, (kvh * PAIRS + j) * LANES:(kvh * PAIRS + j + 1) * LANES] for j in range(PAIRS)],
                           axis=0)


def _fill_bias(bias_scr):
    shape = (GROUP_ROWS, 2 * ATTN_BLOCK)
    r = lax.broadcasted_iota(jnp.int32, shape, 0) % ATTN_BLOCK
    c = lax.broadcasted_iota(jnp.int32, shape, 1)
    in_cur = (c >= ATTN_BLOCK) & ((c - ATTN_BLOCK) <= r)
    in_prev = (c < ATTN_BLOCK) & (c > r)
    bias_scr[0] = jnp.where(in_cur, 0.0, MASKED)
    bias_scr[1] = jnp.where(in_cur | in_prev, 0.0, MASKED)
    bias_scr[2] = jnp.where(c == r, 1.0, 0.0)


N_BIAS_TABLES = 3


def _sink_table(sinks):
    t = jnp.transpose(sinks.reshape(N_KV_HEADS, PAIRS, 2), (0, 2, 1))
    return jnp.broadcast_to(t[:, :, :, None, None], (N_KV_HEADS, 2, PAIRS, ATTN_BLOCK, LANES)).reshape(
        N_KV_HEADS, 2, GROUP_ROWS, LANES)


def attn_fwd(q, k, v, z, sink_tab, batch, seq, gather=()):
    nb = seq // ATTN_BLOCK
    ng = len(gather)

    def body(*refs):
        q_ref, k_ref, v_ref, z_ref, s_ref = refs[:5]
        og_ref, bias_scr = refs[5 + ng], refs[6 + 2 * ng]
        exchange = (refs[5:5 + ng], refs[6 + ng:6 + 2 * ng]) + tuple(refs[7 + 2 * ng:])
        b, i = pl.program_id(0), pl.program_id(1)

        @pl.when((b == 0) & (i == 0))
        def _():
            _fill_bias(bias_scr)
            if ng:
                _gather_start(*exchange)

        kw, vw, _, _ = _kv_windows(k_ref, v_ref, i)
        bias, at_sink = bias_scr[jnp.minimum(i, 1)], bias_scr[2] > 0.5
        for kvh in range(N_KV_HEADS):
            k_a, k_b = _spread(kw, kvh)
            v_a, v_b = _spread(vw, kvh)
            og = _attn_group(_stack_pairs(q_ref, kvh), k_a, v_a, k_b, v_b, _stack_pairs(z_ref, kvh),
                             s_ref[kvh, 0], s_ref[kvh, 1], bias, at_sink)
            for j in range(PAIRS):
                og_ref[:, (kvh * PAIRS + j) * LANES:(kvh * PAIRS + j + 1) * LANES] = (
                    og[j * ATTN_BLOCK:(j + 1) * ATTN_BLOCK].astype(BF16))

        if ng:
            @pl.when((b == batch - 1) & (i == nb - 1))
            def _():
                _gather_finish(*exchange)

    blk = lambda w: pl.BlockSpec((ATTN_BLOCK, w), lambda b, i: (b * nb + i, 0))
    seq_spec = pl.BlockSpec((seq, KV_WIDTH), lambda b, i: (b, 0))
    out = pl.pallas_call(
        body, name="attn_fwd", grid=(batch, nb),
        in_specs=[blk(D_MODEL), seq_spec, seq_spec, blk(D_MODEL), _full_spec(sink_tab.shape)] + [_ANY] * ng,
        out_specs=[blk(D_MODEL)] + [_ANY] * ng,
        out_shape=[jax.ShapeDtypeStruct((batch * seq, D_MODEL), BF16)]
        + [jax.ShapeDtypeStruct((N_CHIPS,) + a.shape, a.dtype) for a in gather],
        scratch_shapes=[pltpu.VMEM((N_BIAS_TABLES, GROUP_ROWS, 2 * ATTN_BLOCK), F32)] + (_gather_sems(ng) if ng else []),
        compiler_params=_cparams(2),
    )(q, k, v, z, sink_tab, *gather)
    return out[0], out[1:]


def attn_bwd(q, k, v, z, sink_tab, dog, angles, batch, seq, scatter=()):
    nb = seq // ATTN_BLOCK
    ns = len(scatter)

    def body(*refs):
        q_ref, k_ref, v_ref, z_ref, s_ref, g_ref, cs_ref = refs[:7]
        dp_ref, dk_ref, dv_ref, ds_ref = refs[7 + ns:11 + ns]
        bias_scr = refs[11 + 2 * ns]
        exchange = (refs[7:7 + ns], refs[11 + ns:11 + 2 * ns]) + tuple(refs[12 + 2 * ns:])
        b, i = pl.program_id(0), pl.program_id(1)

        @pl.when((b == 0) & (i == 0))
        def _():
            _fill_bias(bias_scr)
            ds_ref[...] = jnp.zeros_like(ds_ref)
            if ns:
                _scatter_start(*exchange)

        @pl.when(i == 0)
        def _():
            dk_ref[...] = jnp.zeros_like(dk_ref)
            dv_ref[...] = jnp.zeros_like(dv_ref)

        kw, vw, ps, cs = _kv_windows(k_ref, v_ref, i)
        bias = bias_scr[jnp.minimum(i, 1)]
        tabs = _rope_tables(cs_ref[...])
        dkw = jnp.zeros_like(kw)
        dvw = jnp.zeros_like(vw)
        for kvh in range(N_KV_HEADS):
            k_a, k_b = _spread(kw, kvh)
            v_a, v_b = _spread(vw, kvh)
            _, vjp = jax.vjp(functools.partial(_attn_group, bias=bias), _stack_pairs(q_ref, kvh).astype(F32),
                             k_a, v_a, k_b, v_b, _stack_pairs(z_ref, kvh), s_ref[kvh, 0], s_ref[kvh, 1])
            dqs, dk_a, dv_a, dk_b, dv_b, dzs, ds_a, ds_b = vjp(_stack_pairs(g_ref, kvh).astype(F32))
            dkw = dkw + _unspread(dk_a, dk_b, kvh)
            dvw = dvw + _unspread(dv_a, dv_b, kvh)
            ds_ref[kvh, 0] += jnp.sum(ds_a.reshape(PAIRS, ATTN_BLOCK, LANES), axis=1)
            ds_ref[kvh, 1] += jnp.sum(ds_b.reshape(PAIRS, ATTN_BLOCK, LANES), axis=1)
            for j in range(PAIRS):
                rows = slice(j * ATTN_BLOCK, (j + 1) * ATTN_BLOCK)
                col = (kvh * PAIRS + j) * LANES
                dp_ref[:, col:col + LANES] = _rope_transposed(dqs[rows] * (HEAD_DIM ** -0.5), *tabs).astype(BF16)
                zc = D_MODEL + 2 * KV_WIDTH + col
                dp_ref[:, zc:zc + LANES] = dzs[rows].astype(BF16)
        dp_ref[:, D_MODEL:D_MODEL + 2 * KV_WIDTH] = jnp.zeros((ATTN_BLOCK, 2 * KV_WIDTH), BF16)
        dk_ref[:, pl.ds(ps, ATTN_BLOCK)] += dkw[:, :ATTN_BLOCK]
        dk_ref[:, pl.ds(cs, ATTN_BLOCK)] += dkw[:, ATTN_BLOCK:]
        dv_ref[:, pl.ds(ps, ATTN_BLOCK)] += dvw[:, :ATTN_BLOCK]
        dv_ref[:, pl.ds(cs, ATTN_BLOCK)] += dvw[:, ATTN_BLOCK:]

        if ns:
            @pl.when((b == batch - 1) & (i == nb - 1))
            def _():
                _scatter_finish(*exchange)

    blk = lambda w: pl.BlockSpec((ATTN_BLOCK, w), lambda b, i: (b * nb + i, 0))
    seq_spec = pl.BlockSpec((seq, KV_WIDTH), lambda b, i: (b, 0))
    seq_spec_t = pl.BlockSpec((KV_WIDTH, seq), lambda b, i: (0, b))
    n = batch * seq
    ds_shape = (N_KV_HEADS, 2, PAIRS, LANES)
    out = pl.pallas_call(
        body, name="attn_bwd", grid=(batch, nb),
        in_specs=[blk(D_MODEL), seq_spec, seq_spec, blk(D_MODEL), _full_spec(sink_tab.shape), blk(D_MODEL)]
        + [blk(ANGLE_COLS)] + [_ANY] * ns,
        out_specs=[blk(ATTN_IN), seq_spec_t, seq_spec_t, _full_spec(ds_shape)] + [_ANY] * ns,
        out_shape=[jax.ShapeDtypeStruct((n, ATTN_IN), BF16), jax.ShapeDtypeStruct((KV_WIDTH, n), F32),
                   jax.ShapeDtypeStruct((KV_WIDTH, n), F32), jax.ShapeDtypeStruct(ds_shape, F32)]
        + [jax.ShapeDtypeStruct(a.shape, a.dtype) for a in scatter],
        scratch_shapes=[pltpu.VMEM((N_BIAS_TABLES, GROUP_ROWS, 2 * ATTN_BLOCK), F32)] + (_scatter_sems(ns) if ns else []),
        compiler_params=_cparams(2),
    )(q, k, v, z, sink_tab, dog, angles, *scatter)
    return out[0], out[1], out[2], out[3], out[4:]


def rec_fwd(proj, lb_logits, gnorm_w, batch, seq):
    nblk = seq // REC_BLOCK

    def body(p_ref, lb_ref, gw_ref, og_ref, st_ref, safe_ref, s_scr):
        @pl.when(pl.program_id(1) == 0)
        def _():
            s_scr[...] = jnp.zeros_like(s_scr)

        S = s_scr[...]
        st_ref[0] = S
        qr, fr, v, z = (p_ref[:, part * D_MODEL:(part + 1) * D_MODEL] for part in range(4))
        lf, k = forget_gate(fr, lb_ref[1:2, :] - lb_ref[0:1, :])
        q, b = silu(qr), cumsum_rows(lf)
        safe = jnp.min(_rec_margin(b)) >= -SAFE_RANGE

        gate = gw_ref[...] * silu(z)
        safe_ref[0] = jnp.full((REC_HEADS, LANES), safe.astype(F32))

        def store(o, S_new):
            og_ref[...] = (o * lax.rsqrt(head_sum(o * o) * (1.0 / REC_DIM) + NORM_EPS) * gate).astype(BF16)
            s_scr[...] = S_new

        @pl.when(safe)
        def _():
            store(*_rec_cores_fast(q, k, v, b, S))

        @pl.when(jnp.logical_not(safe))
        def _():
            outs = [_rec_core_slow(*args) for args in zip(*(_heads(t) for t in (q, k, v, b, S)))]
            store(*(jnp.concatenate(parts, axis=1) for parts in zip(*outs)))

    blk = lambda w: pl.BlockSpec((REC_BLOCK, w), lambda b, j: (b * nblk + j, 0))
    st_spec = pl.BlockSpec((1, REC_DIM, D_MODEL), lambda b, j: (b * nblk + j, 0, 0))
    safe_spec = pl.BlockSpec((1, REC_HEADS, LANES), lambda b, j: (b * nblk + j, 0, 0))
    return pl.pallas_call(
        body, name="rec_fwd", grid=(batch, nblk),
        in_specs=[blk(REC_IN), _full_spec((2, D_MODEL)), _full_spec((1, D_MODEL))],
        out_specs=[blk(D_MODEL), st_spec, safe_spec],
        out_shape=[jax.ShapeDtypeStruct((batch * seq, D_MODEL), BF16),
                   jax.ShapeDtypeStruct((batch * nblk, REC_DIM, D_MODEL), F32),
                   jax.ShapeDtypeStruct((batch * nblk, REC_HEADS, LANES), F32)],
        scratch_shapes=[pltpu.VMEM((REC_DIM, D_MODEL), F32)],
        compiler_params=_cparams(2),
    )(proj, lb_logits, jnp.tile(gnorm_w, (1, REC_HEADS)))


def rec_bwd(proj, states, safe, lb_logits, gnorm_w, dog, batch, seq):
    nblk = seq // REC_BLOCK

    def body(p_ref, st_ref, safe_ref, lb_ref, gw_ref, g_ref, dp_ref, dlb_ref, dgw_ref, ds_scr):
        @pl.when((pl.program_id(0) == 0) & (pl.program_id(1) == 0))
        def _():
            dlb_ref[...] = jnp.zeros_like(dlb_ref)
            dgw_ref[...] = jnp.zeros_like(dgw_ref)

        @pl.when(pl.program_id(1) == 0)
        def _():
            ds_scr[...] = jnp.zeros_like(ds_scr)

        def load():
            primals = tuple(p_ref[:, part * D_MODEL:(part + 1) * D_MODEL] for part in range(4)) + (
                st_ref[0], lb_ref[0:1, :], lb_ref[1:2, :], gw_ref[...])
            return primals, (g_ref[...].astype(F32), ds_scr[...])

        def store(dqr, dfr, dv, dz, dS, dl0, dl1, dgw):
            for part, val in enumerate((dqr, dfr, dv, dz)):
                dp_ref[:, part * D_MODEL:(part + 1) * D_MODEL] = val.astype(BF16)
            ds_scr[...] = dS
            dlb_ref[0:1, :] += dl0
            dlb_ref[1:2, :] += dl1
            dgw_ref[...] += functools.reduce(jnp.add, _heads(dgw))

        fast = jnp.max(safe_ref[0]) > 0.5

        @pl.when(fast)
        def _():
            primals, cotangents = load()
            store(*jax.vjp(_rec_block_fast, *primals)[1](cotangents))

        @pl.when(jnp.logical_not(fast))
        def _():
            primals, cotangents = load()
            outs = [jax.vjp(functools.partial(_rec_head, _rec_core_slow), *args)[1](cts)
                    for args, cts in zip(zip(*(_heads(t) for t in primals)), zip(*(_heads(t) for t in cotangents)))]
            store(*(jnp.concatenate(parts, axis=1) for parts in zip(*outs)))

    blk = lambda w: pl.BlockSpec((REC_BLOCK, w), lambda b, j: (b * nblk + nblk - 1 - j, 0))
    st_spec = pl.BlockSpec((1, REC_DIM, D_MODEL), lambda b, j: (b * nblk + nblk - 1 - j, 0, 0))
    safe_spec = pl.BlockSpec((1, REC_HEADS, LANES), lambda b, j: (b * nblk + nblk - 1 - j, 0, 0))
    return pl.pallas_call(
        body, name="rec_bwd", grid=(batch, nblk),
        in_specs=[blk(REC_IN), st_spec, safe_spec, _full_spec((2, D_MODEL)), _full_spec((1, D_MODEL)),
                  blk(D_MODEL)],
        out_specs=[blk(REC_IN), _full_spec((2, D_MODEL)), _full_spec((1, REC_DIM))],
        out_shape=[jax.ShapeDtypeStruct((batch * seq, REC_IN), BF16), jax.ShapeDtypeStruct((2, D_MODEL), F32),
                   jax.ShapeDtypeStruct((1, REC_DIM), F32)],
        scratch_shapes=[pltpu.VMEM((REC_DIM, D_MODEL), F32)],
        compiler_params=_cparams(2),
    )(proj, states, safe, lb_logits, jnp.tile(gnorm_w, (1, REC_HEADS)), dog)


_ANY = pl.BlockSpec(memory_space=pl.ANY)


def _chip_peers():
    x, y, c = lax.axis_index("x"), lax.axis_index("y"), lax.axis_index("c")
    peers = []
    for fx, fy in ((1, 0), (0, 1), (1, 1)):
        px, py = (1 - x if fx else x), (1 - y if fy else y)
        peers.append(((px, py, c), 2 * px + py))
    return 2 * x + y, peers


def _remote(src, dst, send_sem, recv_sem, device):
    return pltpu.make_async_remote_copy(src_ref=src, dst_ref=dst, send_sem=send_sem, recv_sem=recv_sem,
                                        device_id=device, device_id_type=MESH)


N_FLIPS = N_CHIPS - 1


def _scatter_sems(n):
    return [pltpu.SemaphoreType.DMA((n * N_FLIPS,)), pltpu.SemaphoreType.DMA((n * N_FLIPS,)),
            pltpu.SemaphoreType.DMA((n,))]


def _scatter_copies(ins, outs, send_sems, recv_sems, local_sems, starting):
    me, peers = _chip_peers()
    local = [pltpu.make_async_copy(ins[k].at[me], outs[k].at[me], local_sems.at[k]) for k in range(len(ins))]
    sends, arrivals = [], []
    for k in range(len(ins)):
        for j, (device, idx) in enumerate(peers):
            sems = (send_sems.at[k * N_FLIPS + j], recv_sems.at[k * N_FLIPS + j], device)
            sends.append(_remote(ins[k].at[idx], outs[k].at[me], *sems))
            if not starting:
                arrivals.append(_remote(ins[k].at[me], outs[k].at[idx], *sems))
    return local, sends, arrivals


def _scatter_start(*refs):
    local, sends, _ = _scatter_copies(*refs, starting=True)
    for cp in local + sends:
        cp.start()


def _scatter_finish(*refs):
    local, sends, arrivals = _scatter_copies(*refs, starting=False)
    for cp in arrivals:
        cp.wait_recv()
    for cp in sends:
        cp.wait_send()
    for cp in local:
        cp.wait()


def _gather_sems(n):
    return [pltpu.SemaphoreType.DMA((n * N_FLIPS,)) for _ in range(4)] + [pltpu.SemaphoreType.DMA((n,))]


def _gather_copies(ins, outs, send_sems, recv_sems, pass_send_sems, pass_recv_sems, local_sems, starting):
    me, peers = _chip_peers()
    c = lax.axis_index("c")
    sibling = (lax.axis_index("x"), lax.axis_index("y"), 1 - c)
    local = [pltpu.make_async_copy(ins[k], outs[k].at[me], local_sems.at[k]) for k in range(len(ins))]
    sends, arrivals, passes, pass_arrivals = [], [], [], []
    for k in range(len(ins)):
        half = ins[k].shape[0] // 2
        mine, other = pl.ds(c * half, half), pl.ds((1 - c) * half, half)
        for j, (device, idx) in enumerate(peers):
            s = k * N_FLIPS + j
            sends.append(_remote(ins[k].at[mine], outs[k].at[me].at[mine], send_sems.at[s], recv_sems.at[s], device))
            if starting:
                continue
            arrived = outs[k].at[idx].at[mine]
            arrivals.append(_remote(ins[k].at[mine], arrived, send_sems.at[s], recv_sems.at[s], device))
            passes.append(_remote(arrived, arrived, pass_send_sems.at[s], pass_recv_sems.at[s], sibling))
            passed = outs[k].at[idx].at[other]
            pass_arrivals.append(_remote(passed, passed, pass_send_sems.at[s], pass_recv_sems.at[s], sibling))
    return local, sends, arrivals, passes, pass_arrivals


def _gather_start(*refs):
    local, sends, _, _, _ = _gather_copies(*refs, starting=True)
    for cp in local + sends:
        cp.start()


def _gather_finish(*refs):
    local, sends, arrivals, passes, pass_arrivals = _gather_copies(*refs, starting=False)
    for arrival, onward in zip(arrivals, passes):
        arrival.wait_recv()
        onward.start()
    for cp in pass_arrivals:
        cp.wait_recv()
    for cp in sends + passes:
        cp.wait_send()
    for cp in local:
        cp.wait()


def chip_gather(arrays):
    n = len(arrays)

    def body(*refs):
        _gather_start(refs[:n], refs[n:2 * n], *refs[2 * n:])
        _gather_finish(refs[:n], refs[n:2 * n], *refs[2 * n:])

    return pl.pallas_call(
        body, name="chip_gather", in_specs=[_ANY] * n, out_specs=[_ANY] * n,
        out_shape=[jax.ShapeDtypeStruct((N_CHIPS,) + a.shape, a.dtype) for a in arrays],
        scratch_shapes=_gather_sems(n),
    )(*arrays)


def _sibling_sems(n):
    return [pltpu.SemaphoreType.DMA((n,)), pltpu.SemaphoreType.DMA((n,))]


def _sibling_copies(ins, outs, send_sems, recv_sems):
    sibling = (lax.axis_index("x"), lax.axis_index("y"), 1 - lax.axis_index("c"))
    return [_remote(ins[k], outs[k], send_sems.at[k], recv_sems.at[k], sibling) for k in range(len(ins))]


def all_gather_small(vec, to_sibling=()):
    nsib = len(to_sibling)

    def body(*refs):
        v_ref, out_ref = refs[0], refs[1 + nsib]
        send_sems, recv_sems, local_sem = refs[2 + 2 * nsib:5 + 2 * nsib]
        passed = _sibling_copies(refs[1:1 + nsib], refs[2 + nsib:2 + 2 * nsib], *refs[5 + 2 * nsib:]) if nsib else []
        for cp in passed:
            cp.start()
        x, y, c = lax.axis_index("x"), lax.axis_index("y"), lax.axis_index("c")
        me = 4 * x + 2 * y + c
        local = pltpu.make_async_copy(v_ref, out_ref.at[me], local_sem)
        local.start()
        sends, recvs = [], []
        for j in range(1, N_DEV):
            px = jnp.where(j & 4, 1 - x, x)
            py = jnp.where(j & 2, 1 - y, y)
            pc = jnp.where(j & 1, 1 - c, c)
            common = dict(send_sem=send_sems.at[j - 1], recv_sem=recv_sems.at[j - 1], device_id=(px, py, pc),
                          device_id_type=MESH)
            sends.append(pltpu.make_async_remote_copy(src_ref=v_ref, dst_ref=out_ref.at[me], **common))
            recvs.append(pltpu.make_async_remote_copy(src_ref=v_ref, dst_ref=out_ref.at[4 * px + 2 * py + pc],
                                                      **common))
        for cp in sends:
            cp.start()
        for cp in recvs:
            cp.wait_recv()
        for cp in sends:
            cp.wait_send()
        local.wait()
        for cp in passed:
            cp.wait()

    out = pl.pallas_call(
        body, name="all_gather_small", in_specs=[_ANY] * (1 + nsib), out_specs=[_ANY] * (1 + nsib),
        out_shape=[jax.ShapeDtypeStruct((N_DEV,) + vec.shape, vec.dtype)]
        + [jax.ShapeDtypeStruct(a.shape, a.dtype) for a in to_sibling],
        scratch_shapes=[pltpu.SemaphoreType.DMA((N_DEV - 1,)), pltpu.SemaphoreType.DMA((N_DEV - 1,)),
                        pltpu.SemaphoreType.DMA] + (_sibling_sems(nsib) if nsib else []),
    )(vec, *to_sibling)
    return out[0], out[1:]


def sum_slots(stacked, tm=256):
    s, r, c = stacked.shape
    tm = min(tm, r)

    def body(in_ref, out_ref):
        acc = in_ref[0].astype(F32)
        for t in range(1, s):
            acc = acc + in_ref[t].astype(F32)
        out_ref[...] = acc

    return pl.pallas_call(
        body, name=f"sum_slots_{s}_{r}_{c}", grid=(r // tm,),
        in_specs=[pl.BlockSpec((s, tm, c), lambda i: (0, i, 0))], out_specs=_row_spec(tm, c),
        out_shape=jax.ShapeDtypeStruct((r, c), F32), compiler_params=_cparams(1),
    )(stacked)


def adamw(w, m, v, g_a, g_b=None, tm=256):
    r, c = w.shape
    tm = min(tm, r)
    two = g_b is not None

    def body(*refs):
        w_ref, m_ref, v_ref, ga_ref = refs[:4]
        g_ref, d_ref, nm_ref, nv_ref = refs[-4:]
        g = ga_ref[...] + refs[4][...] if two else ga_ref[...]
        nm = ADAM_B1 * m_ref[...] + (1.0 - ADAM_B1) * g
        nv = ADAM_B2 * v_ref[...] + (1.0 - ADAM_B2) * (g * g)
        m_hat = nm / (1.0 - ADAM_B1 ** ADAM_STEP)
        v_hat = nv / (1.0 - ADAM_B2 ** ADAM_STEP)
        g_ref[...] = g
        d_ref[...] = -ADAM_LR * (m_hat / (jnp.sqrt(v_hat) + ADAM_EPS) + ADAM_WD * w_ref[...])
        nm_ref[...] = nm
        nv_ref[...] = nv

    args = [w, m, v, g_a] + ([g_b] if two else [])
    return pl.pallas_call(
        body, name=f"adamw_{r}_{c}", grid=(r // tm,),
        in_specs=[_row_spec(tm, c)] * len(args), out_specs=[_row_spec(tm, c)] * 4,
        out_shape=[jax.ShapeDtypeStruct((r, c), F32)] * 4, compiler_params=_cparams(1),
    )(*args)


_SMALL = (("pre_norm_w", (2, D_MODEL)), ("post_norm_w", (2, D_MODEL)), ("attn_b_in", (1, ATTN_IN)),
          ("attn_sinks", (1, N_HEADS)), ("attn_b_out", (1, D_MODEL)), ("rec_lb_logits", (2, D_MODEL)),
          ("rec_gnorm_w", (1, REC_DIM)))
_SMALL_ROWS = 16


def _pack_small(parts, last_row=None):
    rows = []
    for (name, shape) in _SMALL:
        flat = parts[name].reshape(-1)
        pad = -flat.shape[0] % D_MODEL
        rows.append(jnp.pad(flat, (0, pad)).reshape(-1, D_MODEL))
    used = sum(r.shape[0] for r in rows)
    rows.append(jnp.zeros((_SMALL_ROWS - 1 - used, D_MODEL), F32))
    rows.append(jnp.zeros((1, D_MODEL), F32) if last_row is None else last_row)
    return jnp.concatenate(rows, axis=0)


def _unpack_small(packed):
    out, row = {}, 0
    for (name, shape) in _SMALL:
        size = shape[0] * shape[1]
        nrows = -(-size // D_MODEL)
        out[name] = packed[row:row + nrows].reshape(-1)[:size].reshape(shape)
        row += nrows
    return out


_CARRIED = ("rec_w_in", "rec_w_out", "attn_w_out")


_LATE = ("attn_w_out", "rec_w_in", "rec_w_out")


def local_step(x, positions, pre_norm_w, post_norm_w, attn_w_in, attn_b_in, attn_sinks, attn_w_out, attn_b_out,
               rec_w_in, rec_lb_logits, rec_gnorm_w, rec_w_out, loss_target, distributed=False):
    batch, seq, _ = x.shape
    n = batch * seq
    x0 = x.reshape(n, D_MODEL)
    angles = _rope_angles(positions)
    pre0, pre1 = pre_norm_w[0:1], pre_norm_w[1:2]
    post0, post1 = post_norm_w[0:1], post_norm_w[1:2]

    late = (attn_w_out, rec_w_in, rec_w_out) if distributed else ()
    (h0, q, k, v, z), late = attn_in_proj(x0, pre0, attn_w_in, attn_b_in, angles, to_bf16=late)
    sink_tab = _sink_table(attn_sinks)
    og0, gathered = attn_fwd(q, k, v, z, sink_tab, batch, seq, gather=late)
    if distributed:
        attn_w_out, rec_w_in, rec_w_out = (g if name == "rec_w_in" else _whole_from_shards(name, g)
                                           for name, g in zip(_LATE, gathered))
    y0, x1 = out_proj(og0, attn_w_out, attn_b_out, x0, post0)

    h1, proj1 = rec_in_proj(x1, pre1, rec_w_in)
    og1, states, safe = rec_fwd(proj1, rec_lb_logits, rec_gnorm_w, batch, seq)
    dx2, loss_vec, dog1, d_rec_w_out, d_post1, d_rec_w_out_bf16 = out_proj_loss_bwd(
        og1, rec_w_out, x1, post1, loss_target.reshape(n, D_MODEL))
    dproj1, d_lb, d_gnorm = rec_bwd(proj1, states, safe, rec_lb_logits, rec_gnorm_w, dog1, batch, seq)
    dx1, d_pre1, _, _ = in_proj_bwd_x(dproj1, rec_w_in, x1, pre1, dx2)
    d_rec_w_in, _ = in_proj_bwd_w(h1, dproj1, as_shards=distributed)

    dog0, d_attn_w_out, d_attn_b_out, d_post0, d_attn_w_out_bf16 = out_proj_bwd(dx1, y0, og0, attn_w_out, post0)
    ready = dict(rec_w_in=d_rec_w_in, rec_w_out=_shards_from_whole("rec_w_out", d_rec_w_out_bf16),
                 attn_w_out=_shards_from_whole("attn_w_out", d_attn_w_out_bf16))
    outgoing = [ready[name] for name in _CARRIED] if distributed else []
    dproj0, dk, dv, d_sink_tab, arrived = attn_bwd(q, k, v, z, sink_tab, dog0, angles, batch, seq, scatter=outgoing)
    d_sinks = jnp.transpose(jnp.sum(d_sink_tab, axis=-1), (0, 2, 1)).reshape(1, N_HEADS)
    d_attn_w_in, d_attn_b_in, dproj0, *summed = in_proj_bwd_w(h0, dproj0, kv=(dk, dv, angles),
                                                              slot_sums=list(arrived))
    last = [_shards_from_whole("attn_w_in", d_attn_w_in).astype(BF16)] if distributed else []
    dx0, d_pre0, arrived_last, theirs = in_proj_bwd_x(dproj0, attn_w_in, x0, pre0, dx1, scatter=last,
                                                      to_sibling=summed)

    grads = dict(
        pre_norm_w=jnp.concatenate([d_pre0, d_pre1], axis=0), post_norm_w=jnp.concatenate([d_post0, d_post1], axis=0),
        attn_w_in=d_attn_w_in, attn_b_in=d_attn_b_in, attn_sinks=d_sinks, attn_w_out=d_attn_w_out,
        attn_b_out=d_attn_b_out, rec_w_in=d_rec_w_in, rec_lb_logits=d_lb, rec_gnorm_w=d_gnorm,
        rec_w_out=d_rec_w_out)
    exchanged = dict(zip(_CARRIED, zip(summed, theirs)))
    exchanged.update(zip(("attn_w_in",), arrived_last))
    return loss_vec, dx0.reshape(batch, seq, D_MODEL), grads, exchanged


_BIG = ("attn_w_in", "attn_w_out", "rec_w_in", "rec_w_out")
_COLUMN_SHARDED = ("attn_w_in", "rec_w_in")
_ORDER = ("pre_norm_w", "post_norm_w", "attn_w_in", "attn_b_in", "attn_sinks", "attn_w_out", "attn_b_out",
          "rec_w_in", "rec_lb_logits", "rec_gnorm_w", "rec_w_out")


def _whole_from_shards(name, stacked):
    if name in _COLUMN_SHARDED:
        return jnp.transpose(stacked, (1, 0, 2)).reshape(stacked.shape[1], -1)
    return stacked.reshape(-1, stacked.shape[2])


def _shards_from_whole(name, whole):
    if name in _COLUMN_SHARDED:
        return jnp.transpose(whole.reshape(whole.shape[0], N_CHIPS, -1), (1, 0, 2))
    return whole.reshape(N_CHIPS, -1, whole.shape[1])


def kernel(x, positions, pre_norm_w, post_norm_w, attn_w_in, attn_b_in, attn_sinks, attn_w_out, attn_b_out, rec_w_in, rec_lb_logits, rec_gnorm_w, rec_w_out, loss_target, m_pre_norm_w, m_post_norm_w, m_attn_w_in, m_attn_b_in, m_attn_sinks, m_attn_w_out, m_attn_b_out, m_rec_w_in, m_rec_lb_logits, m_rec_gnorm_w, m_rec_w_out, v_pre_norm_w, v_post_norm_w, v_attn_w_in, v_attn_b_in, v_attn_sinks, v_attn_w_out, v_attn_b_out, v_rec_w_in, v_rec_lb_logits, v_rec_gnorm_w, v_rec_w_out):
    w = dict(pre_norm_w=pre_norm_w, post_norm_w=post_norm_w, attn_w_in=attn_w_in, attn_b_in=attn_b_in,
             attn_sinks=attn_sinks, attn_w_out=attn_w_out, attn_b_out=attn_b_out, rec_w_in=rec_w_in,
             rec_lb_logits=rec_lb_logits, rec_gnorm_w=rec_gnorm_w, rec_w_out=rec_w_out)
    m = dict(pre_norm_w=m_pre_norm_w, post_norm_w=m_post_norm_w, attn_w_in=m_attn_w_in, attn_b_in=m_attn_b_in,
             attn_sinks=m_attn_sinks, attn_w_out=m_attn_w_out, attn_b_out=m_attn_b_out, rec_w_in=m_rec_w_in,
             rec_lb_logits=m_rec_lb_logits, rec_gnorm_w=m_rec_gnorm_w, rec_w_out=m_rec_w_out)
    v = dict(pre_norm_w=v_pre_norm_w, post_norm_w=v_post_norm_w, attn_w_in=v_attn_w_in, attn_b_in=v_attn_b_in,
             attn_sinks=v_attn_sinks, attn_w_out=v_attn_w_out, attn_b_out=v_attn_b_out, rec_w_in=v_rec_w_in,
             rec_lb_logits=v_rec_lb_logits, rec_gnorm_w=v_rec_gnorm_w, rec_w_out=v_rec_w_out)

    shards = {name: w[name][0] for name in _BIG}
    attn_w_in_whole = _whole_from_shards("attn_w_in", chip_gather([shards["attn_w_in"].astype(BF16)])[0])

    loss_vec, grad_x, grads, exchanged = local_step(
        x, positions, pre_norm_w, post_norm_w, attn_w_in_whole, attn_b_in, attn_sinks, shards["attn_w_out"],
        attn_b_out, shards["rec_w_in"], rec_lb_logits, rec_gnorm_w, shards["rec_w_out"], loss_target,
        distributed=True)

    mine = sum_slots(exchanged["attn_w_in"])
    small_parts, (other,) = all_gather_small(_pack_small(grads, last_row=loss_vec), to_sibling=[mine])
    exchanged["attn_w_in"] = (mine, other)
    out_g, out_d, out_m, out_v = {}, {}, {}, {}
    for name in _BIG:
        mine, other = exchanged[name]
        g, d, nm, nv = adamw(shards[name], m[name][0], v[name][0], mine, other)
        out_g[name], out_d[name], out_m[name], out_v[name] = g[None], d[None], nm[None], nv[None]

    small_sum = sum_slots(small_parts)
    loss = jnp.sum(small_sum[_SMALL_ROWS - 1]) * (0.5 / D_MODEL)
    packed = adamw(_pack_small(w), _pack_small(m), _pack_small(v), small_sum)
    for dst, val in zip((out_g, out_d, out_m, out_v), packed):
        dst.update(_unpack_small(val))

    return (loss, grad_x, *[out_g[n] for n in _ORDER], *[out_d[n] for n in _ORDER],
            *[out_m[n] for n in _ORDER], *[out_v[n] for n in _ORDER])
```

```python
import functools

import jax
import jax.numpy as jnp
from jax import lax
from jax.experimental import pallas as pl
from jax.experimental.pallas import tpu as pltpu

F32 = jnp.float32
BF16 = jnp.bfloat16
MESH = pl.DeviceIdType.MESH

D_MODEL = 1024
HEAD_DIM = 64
N_HEADS = 16
N_KV_HEADS = 2
GROUP = N_HEADS // N_KV_HEADS
KV_WIDTH = N_KV_HEADS * HEAD_DIM
ATTN_IN = 2 * D_MODEL + 2 * KV_WIDTH
ATTN_BLOCK = 128
ROPE_THETA = 500000.0
ROPE_DIM = HEAD_DIM // 4
REC_HEADS = 8
REC_DIM = 128
REC_IN = 4 * D_MODEL
REC_BLOCK = 128
DIAG = 8
NORM_EPS = 1e-6
N_CHIPS = 4
N_DEV = 8
LANES = 128

ADAM_LR = 0.001
ADAM_B1 = 0.9
ADAM_B2 = 0.999
ADAM_EPS = 1e-08
ADAM_WD = 0.01
ADAM_STEP = 10

VMEM_LIMIT = 56 * 1024 * 1024


def _cparams(n_axes):
    return pltpu.CompilerParams(dimension_semantics=("arbitrary",) * n_axes, vmem_limit_bytes=VMEM_LIMIT)


def _dot(a, b, contract):
    return lax.dot_general(a.astype(BF16), b.astype(BF16), (contract, ((), ())), preferred_element_type=F32)


_NN = ((1,), (0,))
_NT = ((1,), (1,))
_TN = ((0,), (0,))


@jax.custom_vjp
def mm_nn(a, b):
    return _dot(a, b, _NN)


mm_nn.defvjp(lambda a, b: (_dot(a, b, _NN), (a, b)),
             lambda res, g: (_dot(g, res[1], _NT), _dot(res[0], g, _TN)))


@jax.custom_vjp
def mm_nt(a, b):
    return _dot(a, b, _NT)


mm_nt.defvjp(lambda a, b: (_dot(a, b, _NT), (a, b)),
             lambda res, g: (_dot(g, res[1], _NN), _dot(g, res[0], _TN)))


@jax.custom_vjp
def mm_tn(a, b):
    return _dot(a, b, _TN)


mm_tn.defvjp(lambda a, b: (_dot(a, b, _TN), (a, b)),
             lambda res, g: (_dot(res[1], g, _NT), _dot(res[0], g, _NN)))


def _tri_dot(x, lower):
    n = x.shape[0]
    r = lax.broadcasted_iota(jnp.int32, (n, n), 0)
    c = lax.broadcasted_iota(jnp.int32, (n, n), 1)
    tri = ((c <= r) if lower else (c >= r)).astype(BF16)
    hi = x.astype(BF16)
    rest = x - hi.astype(F32)
    mid = rest.astype(BF16)
    lo = (rest - mid.astype(F32)).astype(BF16)
    dot = lambda p: lax.dot_general(tri, p, (_NN, ((), ())), preferred_element_type=F32)
    return (dot(lo) + dot(mid)) + dot(hi)


@jax.custom_vjp
def cumsum_rows(x):
    return _tri_dot(x, True)


cumsum_rows.defvjp(lambda x: (cumsum_rows(x), None), lambda _, g: (_tri_dot(g, False),))


@functools.partial(jax.custom_vjp, nondiff_argnums=(1,))
def roll_sub(x, d):
    return pltpu.roll(x, d, 1) if d else x


roll_sub.defvjp(lambda x, d: (roll_sub(x, d), None),
                lambda d, _, g: (roll_sub(g, (DIAG - d) % DIAG),))


def sigmoid(x):
    return 1.0 / (1.0 + jnp.exp(-x))


@jax.custom_vjp
def silu(x):
    return x * sigmoid(x)


def _silu_fwd(x):
    s = sigmoid(x)
    return x * s, (x, s)


silu.defvjp(_silu_fwd, lambda res, g: (g * (res[1] * (1.0 + res[0] * (1.0 - res[1]))),))


F32_TINY = 1.17549435e-38


def sigmoid_pair(x):
    e = jnp.exp(-jnp.abs(x))
    r = 1.0 / (1.0 + e)
    er = e * r
    pos = x >= 0.0
    return jnp.where(pos, r, er), jnp.where(pos, er, r)


def _forget_fwd(x, a):
    lb, one_m_lb = sigmoid_pair(a)
    sp, sn = sigmoid_pair(x)
    f = lb + one_m_lb * sp
    k = one_m_lb * sn
    return (jnp.log(jnp.maximum(f, F32_TINY)), k), (sp, sn, f, k, lb, one_m_lb)


def _forget_bwd(res, g):
    sp, sn, f, k, lb, one_m_lb = res
    g_lf, g_k = g
    t = jnp.where(f >= F32_TINY, g_lf / jnp.maximum(f, F32_TINY), 0.0) - g_k
    return (k * sp) * t, jnp.sum(sn * t, axis=0, keepdims=True) * (lb * one_m_lb)


@jax.custom_vjp
def forget_gate(x, a):
    return _forget_fwd(x, a)[0]


forget_gate.defvjp(_forget_fwd, _forget_bwd)


@jax.custom_vjp
def decayed(x, e):
    return (x * jnp.exp(e)).astype(BF16).astype(F32)


def _decayed_fwd(x, e):
    y = decayed(x, e)
    return y, (y, e)


decayed.defvjp(_decayed_fwd, lambda res, g: (g * jnp.exp(res[1]), g * res[0]))


def _row(x, r):
    shape = x.shape

    @jax.custom_vjp
    def take(x):
        return x[r:r + 1, :]

    take.defvjp(lambda x: (x[r:r + 1, :], None),
                lambda _, g: (jnp.where(lax.broadcasted_iota(jnp.int32, shape, 0) == r, g, 0.0),))
    return take(x)


def _rms(x):
    return lax.rsqrt(jnp.mean(x * x, axis=-1, keepdims=True) + NORM_EPS)


def _attn_group(qs, k_a, v_a, k_b, v_b, zs, sink_a, sink_b, bias, at_sink=None):
    def half(kh, vh, sink):
        s = mm_nn(qs, kh) + bias
        if at_sink is None:
            m = jnp.maximum(jnp.max(s, axis=-1, keepdims=True), jnp.max(sink, axis=-1, keepdims=True))
            p = jnp.exp(s - lax.stop_gradient(m))
            own = jnp.sum(jnp.exp(sink - lax.stop_gradient(m)), axis=-1, keepdims=True) * (1.0 / LANES)
            return mm_nt(p * (1.0 / (jnp.sum(p, axis=-1, keepdims=True) + own)), vh)
        s = jnp.where(at_sink, jnp.concatenate([sink, sink], axis=1), s)
        p = jnp.exp(s - jnp.max(s, axis=-1, keepdims=True))
        return mm_nt(jnp.where(at_sink, 0.0, p), vh) * (1.0 / jnp.sum(p, axis=-1, keepdims=True))

    return (half(k_a, v_a, sink_a) + half(k_b, v_b, sink_b)) * silu(zs)


SAFE_RANGE = 80.0


def _rec_front(qr, fr, l0, l1):
    lf, k = forget_gate(fr, l1 - l0)
    return silu(qr), k, lf


def _rec_tail(o, z, gw):
    return o * _rms(o) * gw * silu(z)


def _rec_margin(b):
    R = b.shape[0]
    mid, last = _row(b, R // 2 - 1), _row(b, R - 1)
    return jnp.minimum(mid, last - mid)


def _heads(x):
    w = x.shape[1] // REC_HEADS
    return [x[:, h * w:(h + 1) * w] for h in range(REC_HEADS)]


def _hdot(a, b, contract):
    return jnp.concatenate([_dot(ah, bh, contract) for ah, bh in zip(_heads(a), _heads(b))], axis=1)


@jax.custom_vjp
def hmm_nn(a, b):
    return _hdot(a, b, _NN)


hmm_nn.defvjp(lambda a, b: (_hdot(a, b, _NN), (a, b)),
              lambda res, g: (_hdot(g, res[1], _NT), _hdot(res[0], g, _TN)))


@jax.custom_vjp
def hmm_nt(a, b):
    return _hdot(a, b, _NT)


hmm_nt.defvjp(lambda a, b: (_hdot(a, b, _NT), (a, b)),
              lambda res, g: (_hdot(g, res[1], _NN), _hdot(g, res[0], _TN)))


@jax.custom_vjp
def hmm_tn(a, b):
    return _hdot(a, b, _TN)


hmm_tn.defvjp(lambda a, b: (_hdot(a, b, _TN), (a, b)),
              lambda res, g: (_hdot(res[1], g, _NT), _hdot(res[0], g, _NN)))


def _head_sums(x):
    return jnp.concatenate([jnp.broadcast_to(jnp.sum(xh, axis=-1, keepdims=True), xh.shape) for xh in _heads(x)],
                           axis=1)


@jax.custom_vjp
def head_sum(x):
    return _head_sums(x)


head_sum.defvjp(lambda x: (_head_sums(x), None), lambda _, g: (_head_sums(g),))


def _rec_cores_fast(q, k, v, b, S):
    R = q.shape[0]
    ri = lax.broadcasted_iota(jnp.int32, (R, REC_HEADS * R), 0)
    ci = lax.broadcasted_iota(jnp.int32, (R, REC_HEADS * R), 1) % R
    d = b - _row(b, R // 2 - 1)
    sc = jnp.where(ci < ri, hmm_nt(decayed(q, d), decayed(k, -d)), 0.0)
    o = hmm_nt(q * jnp.exp(b), S) + hmm_nn(sc, v) + head_sum(q * k) * v
    b_last = _row(b, R - 1)
    return o, S * jnp.exp(b_last) + hmm_tn(v, k * jnp.exp(b_last - b))


def _rec_tails(o, z, gw):
    return o * lax.rsqrt(head_sum(o * o) * (1.0 / REC_DIM) + NORM_EPS) * gw * silu(z)


def _rec_block_fast(qr, fr, v, z, S, l0, l1, gw):
    lf, k = forget_gate(fr, l1 - l0)
    o, S_new = _rec_cores_fast(silu(qr), k, v, cumsum_rows(lf), S)
    return _rec_tails(o, z, gw), S_new


def _rec_core_slow(q, k, v, b, S):
    R = q.shape[0]
    rows = lax.broadcasted_iota(jnp.int32, (R, REC_DIM), 0)

    o = mm_nt(q * jnp.exp(jnp.minimum(b, 0.0)), S)

    ri = lax.broadcasted_iota(jnp.int32, (R, R), 0)
    ci = lax.broadcasted_iota(jnp.int32, (R, R), 1)
    sc = jnp.zeros((R, R), F32)
    w = R
    while w > DIAG:
        h = w // 2
        b3 = b.reshape(R // w, w, REC_DIM)
        rin = lax.broadcasted_iota(jnp.int32, (R // w, w, REC_DIM), 1)
        mid = jnp.sum(jnp.where(rin == h - 1, b3, 0.0), axis=1, keepdims=True)
        fac = jnp.exp(jnp.minimum(jnp.where(rin >= h, b3 - mid, mid - b3), 0.0)).reshape(R, REC_DIM)
        upper = (rows % w) >= h
        s_w = mm_nt(jnp.where(upper, q * fac, 0.0), jnp.where(upper, 0.0, k * fac))
        sc = sc + jnp.where((ri // w) == (ci // w), s_w, 0.0)
        w = h
    o = o + mm_nn(sc, v)

    g = R // DIAG
    q3, k3, v3, b3 = (t.reshape(g, DIAG, REC_DIM) for t in (q, k, v, b))
    rin = lax.broadcasted_iota(jnp.int32, (g, DIAG, 1), 1)
    od = jnp.zeros((g, DIAG, REC_DIM), F32)
    for d in range(DIAG):
        e = jnp.exp(jnp.minimum(b3 - roll_sub(b3, d), 0.0))
        sd = jnp.sum(q3 * roll_sub(k3, d) * e, axis=-1, keepdims=True)
        od = od + jnp.where(rin >= d, sd, 0.0) * roll_sub(v3, d)
    o = o + od.reshape(R, REC_DIM)

    b_last = _row(b, R - 1)
    return o, S * jnp.exp(jnp.minimum(b_last, 0.0)) + mm_tn(v, k * jnp.exp(jnp.minimum(b_last - b, 0.0)))


def _rec_head(core, qr, fr, v, z, S, l0, l1, gw):
    q, k, lf = _rec_front(qr, fr, l0, l1)
    o, S_new = core(q, k, v, cumsum_rows(lf), S)
    return _rec_tail(o, z, gw), S_new


ANGLE_COLS = 3 * ROPE_DIM


def _rope_angles(positions):
    half = ROPE_DIM // 2
    inv_freq = ROPE_THETA ** (-(jnp.arange(half, dtype=F32) * 2.0 / ROPE_DIM))
    ang = positions.astype(F32).reshape(-1, 1) * inv_freq
    cs = jnp.concatenate([jnp.cos(ang), jnp.sin(ang)], axis=-1)
    hi = cs.astype(BF16)
    rest = cs - hi.astype(F32)
    mid = rest.astype(BF16)
    return jnp.concatenate([hi, mid, (rest - mid.astype(F32)).astype(BF16)], axis=-1)


def _rope_tables(pieces):
    half = ROPE_DIM // 2
    r = lax.broadcasted_iota(jnp.int32, (ANGLE_COLS, 3 * LANES), 0) % ROPE_DIM
    c = lax.broadcasted_iota(jnp.int32, (ANGLE_COLS, 3 * LANES), 1)
    table, j = c // LANES, c % HEAD_DIM
    angle, low = j % half, j < half
    plus = ((table == 0) & (j < ROPE_DIM) & (r == angle)) | ((table == 1) & (j >= half) & (j < ROPE_DIM)
                                                                & (r == half + angle))
    minus = (table == 2) & low & (r == half + angle)
    pick = jnp.where(plus, 1.0, jnp.where(minus, -1.0, 0.0)).astype(BF16)
    out = jnp.dot(pieces, pick, preferred_element_type=F32)
    lane = lax.broadcasted_iota(jnp.int32, (1, LANES), 1) % HEAD_DIM
    return out[:, :LANES] + jnp.where(lane < ROPE_DIM, 0.0, 1.0), out[:, LANES:2 * LANES], out[:, 2 * LANES:]


def _rope(x, cos_t, sin_a, sin_b):
    half = ROPE_DIM // 2
    return x * cos_t + pltpu.roll(x, half, 1) * sin_a + pltpu.roll(x, LANES - half, 1) * sin_b


def _rope_transposed(g, cos_t, sin_a, sin_b):
    half = ROPE_DIM // 2
    return g * cos_t + pltpu.roll(g * sin_a, LANES - half, 1) + pltpu.roll(g * sin_b, half, 1)


def _row_spec(tm, width):
    return pl.BlockSpec((tm, width), lambda i: (i, 0))


def _weight_spec(shape):
    return pl.BlockSpec(shape, lambda *_: (0,) * len(shape), pipeline_mode=pl.Buffered(1))


def _full_spec(shape):
    return pl.BlockSpec(shape, lambda *_: (0,) * len(shape))


def attn_in_proj(x, w_pre, w_in, b_in, angles, tm=1024, to_bf16=()):
    n = x.shape[0]
    tm = min(tm, n)
    nc = len(to_bf16)

    def body(*refs):
        x_ref, wp_ref, w_ref, b_ref, cs_ref = refs[:5]
        h_ref, q_ref, k_ref, v_ref, z_ref = refs[5 + nc:10 + nc]

        @pl.when(pl.program_id(0) == 0)
        def _():
            for src, dst in zip(refs[5:5 + nc], refs[10 + nc:]):
                dst[...] = src[...].astype(BF16)

        xv = x_ref[...]
        h = (xv * _rms(xv) * wp_ref[...]).astype(BF16)
        h_ref[...] = h
        proj = jnp.dot(h, w_ref[...], preferred_element_type=F32) + b_ref[...]
        tabs = _rope_tables(cs_ref[...])
        for s in range(D_MODEL // LANES):
            sl = slice(s * LANES, (s + 1) * LANES)
            q_ref[:, sl] = _rope(proj[:, sl] * (HEAD_DIM ** -0.5), *tabs).astype(BF16)
        k_ref[...] = _rope(proj[:, D_MODEL:D_MODEL + KV_WIDTH], *tabs).astype(BF16)
        v_ref[...] = proj[:, D_MODEL + KV_WIDTH:D_MODEL + 2 * KV_WIDTH].astype(BF16)
        z_ref[...] = proj[:, D_MODEL + 2 * KV_WIDTH:]

    out = pl.pallas_call(
        body, name="attn_in_proj", grid=(n // tm,),
        in_specs=[_row_spec(tm, D_MODEL), _full_spec((1, D_MODEL)), _weight_spec((D_MODEL, ATTN_IN)),
                  _full_spec((1, ATTN_IN)), _row_spec(tm, ANGLE_COLS)] + [_weight_spec(a.shape) for a in to_bf16],
        out_specs=[_row_spec(tm, D_MODEL), _row_spec(tm, D_MODEL), _row_spec(tm, KV_WIDTH),
                   _row_spec(tm, KV_WIDTH), _row_spec(tm, D_MODEL)] + [_full_spec(a.shape) for a in to_bf16],
        out_shape=[jax.ShapeDtypeStruct((n, D_MODEL), BF16), jax.ShapeDtypeStruct((n, D_MODEL), BF16),
                   jax.ShapeDtypeStruct((n, KV_WIDTH), BF16), jax.ShapeDtypeStruct((n, KV_WIDTH), BF16),
                   jax.ShapeDtypeStruct((n, D_MODEL), F32)] + [jax.ShapeDtypeStruct(a.shape, BF16) for a in to_bf16],
        compiler_params=_cparams(1),
    )(x, w_pre, w_in, b_in, angles, *to_bf16)
    return out[:5], out[5:]


def _column_blocks(w):
    if len(w.shape) == 2:
        return [slice(0, w.shape[1])], lambda ref, s: ref[...]
    width = w.shape[2]
    return [slice(s * width, (s + 1) * width) for s in range(w.shape[0])], lambda ref, s: ref[s]


def rec_in_proj(x, w_pre, w_in, tm=1024):
    n = x.shape[0]
    tm = min(tm, n)
    columns, block = _column_blocks(w_in)

    def body(x_ref, wp_ref, w_ref, h_ref, p_ref):
        xv = x_ref[...]
        h = (xv * _rms(xv) * wp_ref[...]).astype(BF16)
        h_ref[...] = h
        for s, cols in enumerate(columns):
            p_ref[:, cols] = jnp.dot(h, block(w_ref, s), preferred_element_type=F32)

    return pl.pallas_call(
        body, name="rec_in_proj", grid=(n // tm,),
        in_specs=[_row_spec(tm, D_MODEL), _full_spec((1, D_MODEL)), _weight_spec(w_in.shape)],
        out_specs=[_row_spec(tm, D_MODEL), _row_spec(tm, REC_IN)],
        out_shape=[jax.ShapeDtypeStruct((n, D_MODEL), BF16), jax.ShapeDtypeStruct((n, REC_IN), F32)],
        compiler_params=_cparams(1),
    )(x, w_pre, w_in)


def out_proj(og, w_out, b_out, x_res, w_post, tm=1024):
    n = og.shape[0]
    tm = min(tm, n)

    def body(og_ref, w_ref, b_ref, x_ref, wp_ref, y_ref, xo_ref):
        y = jnp.dot(og_ref[...], w_ref[...], preferred_element_type=F32) + b_ref[...]
        y_ref[...] = y.astype(BF16)
        xo_ref[...] = x_ref[...] + y * _rms(y) * wp_ref[...]

    return pl.pallas_call(
        body, name="out_proj", grid=(n // tm,),
        in_specs=[_row_spec(tm, D_MODEL), _weight_spec((D_MODEL, D_MODEL)), _full_spec((1, D_MODEL)),
                  _row_spec(tm, D_MODEL), _full_spec((1, D_MODEL))],
        out_specs=[_row_spec(tm, D_MODEL), _row_spec(tm, D_MODEL)],
        out_shape=[jax.ShapeDtypeStruct((n, D_MODEL), BF16), jax.ShapeDtypeStruct((n, D_MODEL), F32)],
        compiler_params=_cparams(1),
    )(og, w_out, b_out, x_res, w_post)


def _post_norm_bwd(g, y, w_post):
    rstd = _rms(y)
    yn = y * rstd
    gw = g * w_post
    return rstd * (gw - yn * jnp.mean(gw * yn, axis=-1, keepdims=True)), jnp.sum(g * yn, axis=0, keepdims=True)


def out_proj_loss_bwd(og, w_out, x_res, w_post, target, tm=1024):
    n = og.shape[0]
    tm = min(tm, n)
    steps = n // tm

    def body(og_ref, w_ref, x_ref, wp_ref, t_ref, dx_ref, l_ref, dog_ref, dw_ref, dwp_ref, dwb_ref):
        @pl.when(pl.program_id(0) == 0)
        def _():
            l_ref[...] = jnp.zeros_like(l_ref)
            dw_ref[...] = jnp.zeros_like(dw_ref)
            dwp_ref[...] = jnp.zeros_like(dwp_ref)

        og_tile = og_ref[...]
        y = jnp.dot(og_tile, w_ref[...], preferred_element_type=F32)
        err = x_ref[...] + y * _rms(y) * wp_ref[...] - t_ref[...]
        g = err * (1.0 / D_MODEL)
        dx_ref[...] = g
        l_ref[...] += jnp.sum(err * err, axis=0, keepdims=True)
        dy, dwp = _post_norm_bwd(g, y, wp_ref[...])
        dwp_ref[...] += dwp
        dyb = dy.astype(BF16)
        dog_ref[...] = _dot(dyb, w_ref[...], _NT).astype(BF16)
        dw_ref[...] += _dot(og_tile, dyb, _TN)

        @pl.when(pl.program_id(0) == steps - 1)
        def _():
            dwb_ref[...] = dw_ref[...].astype(BF16)

    return pl.pallas_call(
        body, name="out_proj_loss_bwd", grid=(steps,),
        in_specs=[_row_spec(tm, D_MODEL), _weight_spec((D_MODEL, D_MODEL)), _row_spec(tm, D_MODEL),
                  _full_spec((1, D_MODEL)), _row_spec(tm, D_MODEL)],
        out_specs=[_row_spec(tm, D_MODEL), _full_spec((1, D_MODEL)), _row_spec(tm, D_MODEL),
                   _full_spec((D_MODEL, D_MODEL)), _full_spec((1, D_MODEL)), _full_spec((D_MODEL, D_MODEL))],
        out_shape=[jax.ShapeDtypeStruct((n, D_MODEL), F32), jax.ShapeDtypeStruct((1, D_MODEL), F32),
                   jax.ShapeDtypeStruct((n, D_MODEL), BF16), jax.ShapeDtypeStruct((D_MODEL, D_MODEL), F32),
                   jax.ShapeDtypeStruct((1, D_MODEL), F32), jax.ShapeDtypeStruct((D_MODEL, D_MODEL), BF16)],
        compiler_params=_cparams(1),
    )(og, w_out, x_res, w_post, target)


def out_proj_bwd(dxo, y, og, w_out, w_post, tm=1024):
    n = og.shape[0]
    tm = min(tm, n)
    steps = n // tm

    def body(g_ref, y_ref, og_ref, w_ref, wp_ref, dog_ref, dw_ref, db_ref, dwp_ref, dwb_ref):
        @pl.when(pl.program_id(0) == 0)
        def _():
            dw_ref[...] = jnp.zeros_like(dw_ref)
            db_ref[...] = jnp.zeros_like(db_ref)
            dwp_ref[...] = jnp.zeros_like(dwp_ref)

        dy, dwp = _post_norm_bwd(g_ref[...], y_ref[...].astype(F32), wp_ref[...])
        dwp_ref[...] += dwp
        db_ref[...] += jnp.sum(dy, axis=0, keepdims=True)
        dyb = dy.astype(BF16)
        dog_ref[...] = _dot(dyb, w_ref[...], _NT).astype(BF16)
        dw_ref[...] += _dot(og_ref[...], dyb, _TN)

        @pl.when(pl.program_id(0) == steps - 1)
        def _():
            dwb_ref[...] = dw_ref[...].astype(BF16)

    return pl.pallas_call(
        body, name="out_proj_bwd", grid=(steps,),
        in_specs=[_row_spec(tm, D_MODEL), _row_spec(tm, D_MODEL), _row_spec(tm, D_MODEL),
                  _weight_spec((D_MODEL, D_MODEL)), _full_spec((1, D_MODEL))],
        out_specs=[_row_spec(tm, D_MODEL), _full_spec((D_MODEL, D_MODEL)), _full_spec((1, D_MODEL)),
                   _full_spec((1, D_MODEL)), _full_spec((D_MODEL, D_MODEL))],
        out_shape=[jax.ShapeDtypeStruct((n, D_MODEL), BF16), jax.ShapeDtypeStruct((D_MODEL, D_MODEL), F32),
                   jax.ShapeDtypeStruct((1, D_MODEL), F32), jax.ShapeDtypeStruct((1, D_MODEL), F32),
                   jax.ShapeDtypeStruct((D_MODEL, D_MODEL), BF16)],
        compiler_params=_cparams(1),
    )(dxo, y, og, w_out, w_post)


def _slot_sum_specs(slot_sums, steps):
    return ([pl.BlockSpec((a.shape[0], a.shape[1] // steps, a.shape[2]), lambda i: (0, i, 0)) for a in slot_sums],
            [_row_spec(a.shape[1] // steps, a.shape[2]) for a in slot_sums],
            [jax.ShapeDtypeStruct(a.shape[1:], F32) for a in slot_sums])


def _sum_slots_into(slot_refs, sum_refs):
    for slots_ref, sum_ref in zip(slot_refs, sum_refs):
        sum_ref[...] = functools.reduce(jnp.add, [slots_ref[t].astype(F32) for t in range(slots_ref.shape[0])])


def in_proj_bwd_x(dproj, w_in, x, w_pre, dxo, tm=1024, scatter=(), to_sibling=()):
    n, p = dproj.shape
    tm = min(tm, n)
    steps = n // tm
    ns, nsib = len(scatter), len(to_sibling)
    columns, block = _column_blocks(w_in)

    def body(*refs):
        dp_ref, w_ref, x_ref, wp_ref, g_ref = refs[:5]
        outs = 5 + ns + nsib
        dx_ref, dwp_ref = refs[outs:outs + 2]
        sems = refs[outs + 2 + ns + nsib:]
        exchange = (refs[5:5 + ns], refs[outs + 2:outs + 2 + ns]) + tuple(sems[:3])
        sibling = (refs[5 + ns:outs], refs[outs + 2 + ns:outs + 2 + ns + nsib]) + tuple(sems[3:])

        @pl.when(pl.program_id(0) == 0)
        def _():
            dwp_ref[...] = jnp.zeros_like(dwp_ref)
            if ns:
                _scatter_start(*exchange)
            for cp in _sibling_copies(*sibling) if nsib else ():
                cp.start()

        dh = functools.reduce(jnp.add, [_dot(dp_ref[:, cols], block(w_ref, s), _NT)
                                        for s, cols in enumerate(columns)])
        xv = x_ref[...]
        rstd = _rms(xv)
        xn = xv * rstd
        gw = dh * wp_ref[...]
        dwp_ref[...] += jnp.sum(dh * xn, axis=0, keepdims=True)
        dx_ref[...] = rstd * (gw - xn * jnp.mean(gw * xn, axis=-1, keepdims=True)) + g_ref[...]

        if ns or nsib:
            @pl.when(pl.program_id(0) == steps - 1)
            def _():
                if ns:
                    _scatter_finish(*exchange)
                for cp in _sibling_copies(*sibling) if nsib else ():
                    cp.wait()

    out = pl.pallas_call(
        body, name=f"in_proj_bwd_x_{p}", grid=(steps,),
        in_specs=[_row_spec(tm, p), _weight_spec(w_in.shape), _row_spec(tm, D_MODEL), _full_spec((1, D_MODEL)),
                  _row_spec(tm, D_MODEL)] + [_ANY] * (ns + nsib),
        out_specs=[_row_spec(tm, D_MODEL), _full_spec((1, D_MODEL))] + [_ANY] * (ns + nsib),
        out_shape=[jax.ShapeDtypeStruct((n, D_MODEL), F32), jax.ShapeDtypeStruct((1, D_MODEL), F32)]
        + [jax.ShapeDtypeStruct(a.shape, a.dtype) for a in tuple(scatter) + tuple(to_sibling)],
        scratch_shapes=(_scatter_sems(ns) if ns else []) + (_sibling_sems(nsib) if nsib else []),
        compiler_params=_cparams(1),
    )(dproj, w_in, x, w_pre, dxo, *scatter, *to_sibling)
    return out[0], out[1], out[2:2 + ns], out[2 + ns:]


def in_proj_bwd_w(h, dproj, tm=1024, as_shards=False, kv=None, slot_sums=()):
    n, p = dproj.shape
    chunk = p // (4 if p % 4096 == 0 else 3)
    tm = min(tm, n)
    steps = n // tm
    shard = p // N_CHIPS
    n_kv = 0 if kv is None else 3
    n_sum = len(slot_sums)
    kv_from, kv_to = D_MODEL, D_MODEL + 2 * KV_WIDTH

    def body(*refs):
        h_ref, dp_ref = refs[:2]
        outs = 2 + n_kv + n_sum
        dw_ref, db_ref = refs[outs:outs + 2]
        sums_at = outs + 2 + (kv is not None)
        scratch = refs[sums_at + n_sum:]
        acc_scr, sem, staging = scratch[0], scratch[1], scratch[2:]
        i = pl.program_id(0)
        _sum_slots_into(refs[2 + n_kv:outs], refs[sums_at:sums_at + n_sum])

        @pl.when(i == 0)
        def _():
            acc_scr[...] = jnp.zeros_like(acc_scr)
            db_ref[...] = jnp.zeros_like(db_ref)

        if kv is not None:
            dk_ref, dv_ref, cs_ref, kv_ref = refs[2], refs[3], refs[4], refs[outs + 2]
            made = jnp.concatenate([_rope_transposed(dk_ref[...].T, *_rope_tables(cs_ref[...])), dv_ref[...].T],
                                   axis=1).astype(BF16)
            kv_ref[...] = made

        def columns(c0):
            if kv is None or c0 + chunk <= kv_from or c0 >= kv_to:
                return dp_ref[:, c0:c0 + chunk]
            return jnp.concatenate([dp_ref[:, c0:kv_from], made, dp_ref[:, kv_to:c0 + chunk]], axis=1)

        ht = h_ref[...].T
        for c0 in range(0, p, chunk):
            dp = columns(c0)
            acc_scr[:, c0:c0 + chunk] += jnp.dot(ht, dp, preferred_element_type=F32)
            db_ref[:, c0:c0 + chunk] += jnp.sum(dp.astype(F32), axis=0, keepdims=True)

        @pl.when(i == steps - 1)
        def _():
            if as_shards:
                for s in range(N_CHIPS):
                    staging[0][...] = acc_scr[:, s * shard:(s + 1) * shard].astype(BF16)
                    out = pltpu.make_async_copy(staging[0], dw_ref.at[s], sem)
                    out.start()
                    out.wait()
            else:
                out = pltpu.make_async_copy(acc_scr, dw_ref, sem)
                out.start()
                out.wait()

    dw_shape = jax.ShapeDtypeStruct((N_CHIPS, D_MODEL, shard), BF16) if as_shards else (
        jax.ShapeDtypeStruct((D_MODEL, p), F32))
    in_specs = [_row_spec(tm, D_MODEL), _row_spec(tm, p)]
    out_specs = [_ANY, _full_spec((1, p))]
    out_shape = [dw_shape, jax.ShapeDtypeStruct((1, p), F32)]
    if kv is not None:
        columns_t = pl.BlockSpec((KV_WIDTH, tm), lambda i: (0, i))
        in_specs += [columns_t, columns_t, _row_spec(tm, ANGLE_COLS)]
        out_specs.append(pl.BlockSpec((tm, kv_to - kv_from), lambda i: (i, kv_from // (kv_to - kv_from))))
        out_shape.append(jax.ShapeDtypeStruct(dproj.shape, dproj.dtype))
    sum_in, sum_out, sum_shapes = _slot_sum_specs(slot_sums, steps)
    in_specs, out_specs, out_shape = in_specs + sum_in, out_specs + sum_out, out_shape + sum_shapes
    return pl.pallas_call(
        body, name=f"in_proj_bwd_w_{p}", grid=(steps,),
        in_specs=in_specs, out_specs=out_specs, out_shape=out_shape,
        scratch_shapes=[pltpu.VMEM((D_MODEL, p), F32), pltpu.SemaphoreType.DMA]
        + ([pltpu.VMEM((D_MODEL, shard), BF16)] if as_shards else []),
        input_output_aliases={1: 2} if kv is not None else {},
        compiler_params=_cparams(1),
    )(h, dproj, *(kv or ()), *slot_sums)


PAIRS = GROUP // 2
GROUP_ROWS = PAIRS * ATTN_BLOCK
MASKED = -1e30


def _kv_windows(k_ref, v_ref, i):
    ps = pl.multiple_of(jnp.maximum(i - 1, 0) * ATTN_BLOCK, ATTN_BLOCK)
    cs = pl.multiple_of(i * ATTN_BLOCK, ATTN_BLOCK)
    kw = jnp.concatenate([k_ref[pl.ds(ps, ATTN_BLOCK), :], k_ref[pl.ds(cs, ATTN_BLOCK), :]], axis=0)
    vw = jnp.concatenate([v_ref[pl.ds(ps, ATTN_BLOCK), :], v_ref[pl.ds(cs, ATTN_BLOCK), :]], axis=0)
    return kw.astype(F32).T, vw.astype(F32).T, ps, cs


def _low_rows(shape):
    return lax.broadcasted_iota(jnp.int32, shape, 0) < HEAD_DIM


def _spread(w, kvh):
    low = _low_rows(w.shape)
    swapped = pltpu.roll(w, HEAD_DIM, 0)
    if kvh == 0:
        return jnp.where(low, w, 0.0), jnp.where(low, 0.0, swapped)
    return jnp.where(low, swapped, 0.0), jnp.where(low, 0.0, w)


def _unspread(d_a, d_b, kvh):
    low = _low_rows(d_a.shape)
    if kvh == 0:
        return jnp.where(low, d_a + pltpu.roll(d_b, HEAD_DIM, 0), 0.0)
    return jnp.where(low, 0.0, pltpu.roll(d_a, HEAD_DIM, 0) + d_b)


def _stack_pairs(ref, kvh):
    return jnp.concatenate([ref[:, (kvh * PAIRS + j) * LANES:(kvh * PAIRS + j + 1) * LANES] for j in range(PAIRS)],
                           axis=0)


def _fill_bias(bias_scr):
    shape = (GROUP_ROWS, 2 * ATTN_BLOCK)
    r = lax.broadcasted_iota(jnp.int32, shape, 0) % ATTN_BLOCK
    c = lax.broadcasted_iota(jnp.int32, shape, 1)
    in_cur = (c >= ATTN_BLOCK) & ((c - ATTN_BLOCK) <= r)
    in_prev = (c < ATTN_BLOCK) & (c > r)
    bias_scr[0] = jnp.where(in_cur, 0.0, MASKED)
    bias_scr[1] = jnp.where(in_cur | in_prev, 0.0, MASKED)
    bias_scr[2] = jnp.where(c == r, 1.0, 0.0)


N_BIAS_TABLES = 3


def _sink_table(sinks):
    t = jnp.transpose(sinks.reshape(N_KV_HEADS, PAIRS, 2), (0, 2, 1))
    return jnp.broadcast_to(t[:, :, :, None, None], (N_KV_HEADS, 2, PAIRS, ATTN_BLOCK, LANES)).reshape(
        N_KV_HEADS, 2, GROUP_ROWS, LANES)


def attn_fwd(q, k, v, z, sink_tab, batch, seq, gather=()):
    nb = seq // ATTN_BLOCK
    ng = len(gather)

    def body(*refs):
        q_ref, k_ref, v_ref, z_ref, s_ref = refs[:5]
        og_ref, bias_scr = refs[5 + ng], refs[6 + 2 * ng]
        exchange = (refs[5:5 + ng], refs[6 + ng:6 + 2 * ng]) + tuple(refs[7 + 2 * ng:])
        b, i = pl.program_id(0), pl.program_id(1)

        @pl.when((b == 0) & (i == 0))
        def _():
            _fill_bias(bias_scr)
            if ng:
                _gather_start(*exchange)

        kw, vw, _, _ = _kv_windows(k_ref, v_ref, i)
        bias, at_sink = bias_scr[jnp.minimum(i, 1)], bias_scr[2] > 0.5
        for kvh in range(N_KV_HEADS):
            k_a, k_b = _spread(kw, kvh)
            v_a, v_b = _spread(vw, kvh)
            og = _attn_group(_stack_pairs(q_ref, kvh), k_a, v_a, k_b, v_b, _stack_pairs(z_ref, kvh),
                             s_ref[kvh, 0], s_ref[kvh, 1], bias, at_sink)
            for j in range(PAIRS):
                og_ref[:, (kvh * PAIRS + j) * LANES:(kvh * PAIRS + j + 1) * LANES] = (
                    og[j * ATTN_BLOCK:(j + 1) * ATTN_BLOCK].astype(BF16))

        if ng:
            @pl.when((b == batch - 1) & (i == nb - 1))
            def _():
                _gather_finish(*exchange)

    blk = lambda w: pl.BlockSpec((ATTN_BLOCK, w), lambda b, i: (b * nb + i, 0))
    seq_spec = pl.BlockSpec((seq, KV_WIDTH), lambda b, i: (b, 0))
    out = pl.pallas_call(
        body, name="attn_fwd", grid=(batch, nb),
        in_specs=[blk(D_MODEL), seq_spec, seq_spec, blk(D_MODEL), _full_spec(sink_tab.shape)] + [_ANY] * ng,
        out_specs=[blk(D_MODEL)] + [_ANY] * ng,
        out_shape=[jax.ShapeDtypeStruct((batch * seq, D_MODEL), BF16)]
        + [jax.ShapeDtypeStruct((N_CHIPS,) + a.shape, a.dtype) for a in gather],
        scratch_shapes=[pltpu.VMEM((N_BIAS_TABLES, GROUP_ROWS, 2 * ATTN_BLOCK), F32)] + (_gather_sems(ng) if ng else []),
        compiler_params=_cparams(2),
    )(q, k, v, z, sink_tab, *gather)
    return out[0], out[1:]


def attn_bwd(q, k, v, z, sink_tab, dog, angles, batch, seq, scatter=()):
    nb = seq // ATTN_BLOCK
    ns = len(scatter)

    def body(*refs):
        q_ref, k_ref, v_ref, z_ref, s_ref, g_ref, cs_ref = refs[:7]
        dp_ref, dk_ref, dv_ref, ds_ref = refs[7 + ns:11 + ns]
        bias_scr = refs[11 + 2 * ns]
        exchange = (refs[7:7 + ns], refs[11 + ns:11 + 2 * ns]) + tuple(refs[12 + 2 * ns:])
        b, i = pl.program_id(0), pl.program_id(1)

        @pl.when((b == 0) & (i == 0))
        def _():
            _fill_bias(bias_scr)
            ds_ref[...] = jnp.zeros_like(ds_ref)
            if ns:
                _scatter_start(*exchange)

        @pl.when(i == 0)
        def _():
            dk_ref[...] = jnp.zeros_like(dk_ref)
            dv_ref[...] = jnp.zeros_like(dv_ref)

        kw, vw, ps, cs = _kv_windows(k_ref, v_ref, i)
        bias = bias_scr[jnp.minimum(i, 1)]
        tabs = _rope_tables(cs_ref[...])
        dkw = jnp.zeros_like(kw)
        dvw = jnp.zeros_like(vw)
        for kvh in range(N_KV_HEADS):
            k_a, k_b = _spread(kw, kvh)
            v_a, v_b = _spread(vw, kvh)
            _, vjp = jax.vjp(functools.partial(_attn_group, bias=bias), _stack_pairs(q_ref, kvh).astype(F32),
                             k_a, v_a, k_b, v_b, _stack_pairs(z_ref, kvh), s_ref[kvh, 0], s_ref[kvh, 1])
            dqs, dk_a, dv_a, dk_b, dv_b, dzs, ds_a, ds_b = vjp(_stack_pairs(g_ref, kvh).astype(F32))
            dkw = dkw + _unspread(dk_a, dk_b, kvh)
            dvw = dvw + _unspread(dv_a, dv_b, kvh)
            ds_ref[kvh, 0] += jnp.sum(ds_a.reshape(PAIRS, ATTN_BLOCK, LANES), axis=1)
            ds_ref[kvh, 1] += jnp.sum(ds_b.reshape(PAIRS, ATTN_BLOCK, LANES), axis=1)
            for j in range(PAIRS):
                rows = slice(j * ATTN_BLOCK, (j + 1) * ATTN_BLOCK)
                col = (kvh * PAIRS + j) * LANES
                dp_ref[:, col:col + LANES] = _rope_transposed(dqs[rows] * (HEAD_DIM ** -0.5), *tabs).astype(BF16)
                zc = D_MODEL + 2 * KV_WIDTH + col
                dp_ref[:, zc:zc + LANES] = dzs[rows].astype(BF16)
        dp_ref[:, D_MODEL:D_MODEL + 2 * KV_WIDTH] = jnp.zeros((ATTN_BLOCK, 2 * KV_WIDTH), BF16)
        dk_ref[:, pl.ds(ps, ATTN_BLOCK)] += dkw[:, :ATTN_BLOCK]
        dk_ref[:, pl.ds(cs, ATTN_BLOCK)] += dkw[:, ATTN_BLOCK:]
        dv_ref[:, pl.ds(ps, ATTN_BLOCK)] += dvw[:, :ATTN_BLOCK]
        dv_ref[:, pl.ds(cs, ATTN_BLOCK)] += dvw[:, ATTN_BLOCK:]

        if ns:
            @pl.when((b == batch - 1) & (i == nb - 1))
            def _():
                _scatter_finish(*exchange)

    blk = lambda w: pl.BlockSpec((ATTN_BLOCK, w), lambda b, i: (b * nb + i, 0))
    seq_spec = pl.BlockSpec((seq, KV_WIDTH), lambda b, i: (b, 0))
    seq_spec_t = pl.BlockSpec((KV_WIDTH, seq), lambda b, i: (0, b))
    n = batch * seq
    ds_shape = (N_KV_HEADS, 2, PAIRS, LANES)
    out = pl.pallas_call(
        body, name="attn_bwd", grid=(batch, nb),
        in_specs=[blk(D_MODEL), seq_spec, seq_spec, blk(D_MODEL), _full_spec(sink_tab.shape), blk(D_MODEL)]
        + [blk(ANGLE_COLS)] + [_ANY] * ns,
        out_specs=[blk(ATTN_IN), seq_spec_t, seq_spec_t, _full_spec(ds_shape)] + [_ANY] * ns,
        out_shape=[jax.ShapeDtypeStruct((n, ATTN_IN), BF16), jax.ShapeDtypeStruct((KV_WIDTH, n), F32),
                   jax.ShapeDtypeStruct((KV_WIDTH, n), F32), jax.ShapeDtypeStruct(ds_shape, F32)]
        + [jax.ShapeDtypeStruct(a.shape, a.dtype) for a in scatter],
        scratch_shapes=[pltpu.VMEM((N_BIAS_TABLES, GROUP_ROWS, 2 * ATTN_BLOCK), F32)] + (_scatter_sems(ns) if ns else []),
        compiler_params=_cparams(2),
    )(q, k, v, z, sink_tab, dog, angles, *scatter)
    return out[0], out[1], out[2], out[3], out[4:]


def rec_fwd(proj, lb_logits, gnorm_w, batch, seq):
    nblk = seq // REC_BLOCK

    def body(p_ref, lb_ref, gw_ref, og_ref, st_ref, safe_ref, s_scr):
        @pl.when(pl.program_id(1) == 0)
        def _():
            s_scr[...] = jnp.zeros_like(s_scr)

        S = s_scr[...]
        st_ref[0] = S
        qr, fr, v, z = (p_ref[:, part * D_MODEL:(part + 1) * D_MODEL] for part in range(4))
        lf, k = forget_gate(fr, lb_ref[1:2, :] - lb_ref[0:1, :])
        q, b = silu(qr), cumsum_rows(lf)
        safe = jnp.min(_rec_margin(b)) >= -SAFE_RANGE

        gate = gw_ref[...] * silu(z)
        safe_ref[0] = jnp.full((REC_HEADS, LANES), safe.astype(F32))

        def store(o, S_new):
            og_ref[...] = (o * lax.rsqrt(head_sum(o * o) * (1.0 / REC_DIM) + NORM_EPS) * gate).astype(BF16)
            s_scr[...] = S_new

        @pl.when(safe)
        def _():
            store(*_rec_cores_fast(q, k, v, b, S))

        @pl.when(jnp.logical_not(safe))
        def _():
            outs = [_rec_core_slow(*args) for args in zip(*(_heads(t) for t in (q, k, v, b, S)))]
            store(*(jnp.concatenate(parts, axis=1) for parts in zip(*outs)))

    blk = lambda w: pl.BlockSpec((REC_BLOCK, w), lambda b, j: (b * nblk + j, 0))
    st_spec = pl.BlockSpec((1, REC_DIM, D_MODEL), lambda b, j: (b * nblk + j, 0, 0))
    safe_spec = pl.BlockSpec((1, REC_HEADS, LANES), lambda b, j: (b * nblk + j, 0, 0))
    return pl.pallas_call(
        body, name="rec_fwd", grid=(batch, nblk),
        in_specs=[blk(REC_IN), _full_spec((2, D_MODEL)), _full_spec((1, D_MODEL))],
        out_specs=[blk(D_MODEL), st_spec, safe_spec],
        out_shape=[jax.ShapeDtypeStruct((batch * seq, D_MODEL), BF16),
                   jax.ShapeDtypeStruct((batch * nblk, REC_DIM, D_MODEL), F32),
                   jax.ShapeDtypeStruct((batch * nblk, REC_HEADS, LANES), F32)],
        scratch_shapes=[pltpu.VMEM((REC_DIM, D_MODEL), F32)],
        compiler_params=_cparams(2),
    )(proj, lb_logits, jnp.tile(gnorm_w, (1, REC_HEADS)))


def rec_bwd(proj, states, safe, lb_logits, gnorm_w, dog, batch, seq):
    nblk = seq // REC_BLOCK

    def body(p_ref, st_ref, safe_ref, lb_ref, gw_ref, g_ref, dp_ref, dlb_ref, dgw_ref, ds_scr):
        @pl.when((pl.program_id(0) == 0) & (pl.program_id(1) == 0))
        def _():
            dlb_ref[...] = jnp.zeros_like(dlb_ref)
            dgw_ref[...] = jnp.zeros_like(dgw_ref)

        @pl.when(pl.program_id(1) == 0)
        def _():
            ds_scr[...] = jnp.zeros_like(ds_scr)

        def load():
            primals = tuple(p_ref[:, part * D_MODEL:(part + 1) * D_MODEL] for part in range(4)) + (
                st_ref[0], lb_ref[0:1, :], lb_ref[1:2, :], gw_ref[...])
            return primals, (g_ref[...].astype(F32), ds_scr[...])

        def store(dqr, dfr, dv, dz, dS, dl0, dl1, dgw):
            for part, val in enumerate((dqr, dfr, dv, dz)):
                dp_ref[:, part * D_MODEL:(part + 1) * D_MODEL] = val.astype(BF16)
            ds_scr[...] = dS
            dlb_ref[0:1, :] += dl0
            dlb_ref[1:2, :] += dl1
            dgw_ref[...] += functools.reduce(jnp.add, _heads(dgw))

        fast = jnp.max(safe_ref[0]) > 0.5

        @pl.when(fast)
        def _():
            primals, cotangents = load()
            store(*jax.vjp(_rec_block_fast, *primals)[1](cotangents))

        @pl.when(jnp.logical_not(fast))
        def _():
            primals, cotangents = load()
            outs = [jax.vjp(functools.partial(_rec_head, _rec_core_slow), *args)[1](cts)
                    for args, cts in zip(zip(*(_heads(t) for t in primals)), zip(*(_heads(t) for t in cotangents)))]
            store(*(jnp.concatenate(parts, axis=1) for parts in zip(*outs)))

    blk = lambda w: pl.BlockSpec((REC_BLOCK, w), lambda b, j: (b * nblk + nblk - 1 - j, 0))
    st_spec = pl.BlockSpec((1, REC_DIM, D_MODEL), lambda b, j: (b * nblk + nblk - 1 - j, 0, 0))
    safe_spec = pl.BlockSpec((1, REC_HEADS, LANES), lambda b, j: (b * nblk + nblk - 1 - j, 0, 0))
    return pl.pallas_call(
        body, name="rec_bwd", grid=(batch, nblk),
        in_specs=[blk(REC_IN), st_spec, safe_spec, _full_spec((2, D_MODEL)), _full_spec((1, D_MODEL)),
                  blk(D_MODEL)],
        out_specs=[blk(REC_IN), _full_spec((2, D_MODEL)), _full_spec((1, REC_DIM))],
        out_shape=[jax.ShapeDtypeStruct((batch * seq, REC_IN), BF16), jax.ShapeDtypeStruct((2, D_MODEL), F32),
                   jax.ShapeDtypeStruct((1, REC_DIM), F32)],
        scratch_shapes=[pltpu.VMEM((REC_DIM, D_MODEL), F32)],
        compiler_params=_cparams(2),
    )(proj, states, safe, lb_logits, jnp.tile(gnorm_w, (1, REC_HEADS)), dog)


_ANY = pl.BlockSpec(memory_space=pl.ANY)


def _chip_peers():
    x, y, c = lax.axis_index("x"), lax.axis_index("y"), lax.axis_index("c")
    peers = []
    for fx, fy in ((1, 0), (0, 1), (1, 1)):
        px, py = (1 - x if fx else x), (1 - y if fy else y)
        peers.append(((px, py, c), 2 * px + py))
    return 2 * x + y, peers


def _remote(src, dst, send_sem, recv_sem, device):
    return pltpu.make_async_remote_copy(src_ref=src, dst_ref=dst, send_sem=send_sem, recv_sem=recv_sem,
                                        device_id=device, device_id_type=MESH)


N_FLIPS = N_CHIPS - 1


def _scatter_sems(n):
    return [pltpu.SemaphoreType.DMA((n * N_FLIPS,)), pltpu.SemaphoreType.DMA((n * N_FLIPS,)),
            pltpu.SemaphoreType.DMA((n,))]


def _scatter_copies(ins, outs, send_sems, recv_sems, local_sems, starting):
    me, peers = _chip_peers()
    local = [pltpu.make_async_copy(ins[k].at[me], outs[k].at[me], local_sems.at[k]) for k in range(len(ins))]
    sends, arrivals = [], []
    for k in range(len(ins)):
        for j, (device, idx) in enumerate(peers):
            sems = (send_sems.at[k * N_FLIPS + j], recv_sems.at[k * N_FLIPS + j], device)
            sends.append(_remote(ins[k].at[idx], outs[k].at[me], *sems))
            if not starting:
                arrivals.append(_remote(ins[k].at[me], outs[k].at[idx], *sems))
    return local, sends, arrivals


def _scatter_start(*refs):
    local, sends, _ = _scatter_copies(*refs, starting=True)
    for cp in local + sends:
        cp.start()


def _scatter_finish(*refs):
    local, sends, arrivals = _scatter_copies(*refs, starting=False)
    for cp in arrivals:
        cp.wait_recv()
    for cp in sends:
        cp.wait_send()
    for cp in local:
        cp.wait()


def _gather_sems(n):
    return [pltpu.SemaphoreType.DMA((n * N_FLIPS,)) for _ in range(4)] + [pltpu.SemaphoreType.DMA((n,))]


def _gather_copies(ins, outs, send_sems, recv_sems, pass_send_sems, pass_recv_sems, local_sems, starting):
    me, peers = _chip_peers()
    c = lax.axis_index("c")
    sibling = (lax.axis_index("x"), lax.axis_index("y"), 1 - c)
    local = [pltpu.make_async_copy(ins[k], outs[k].at[me], local_sems.at[k]) for k in range(len(ins))]
    sends, arrivals, passes, pass_arrivals = [], [], [], []
    for k in range(len(ins)):
        half = ins[k].shape[0] // 2
        mine, other = pl.ds(c * half, half), pl.ds((1 - c) * half, half)
        for j, (device, idx) in enumerate(peers):
            s = k * N_FLIPS + j
            sends.append(_remote(ins[k].at[mine], outs[k].at[me].at[mine], send_sems.at[s], recv_sems.at[s], device))
            if starting:
                continue
            arrived = outs[k].at[idx].at[mine]
            arrivals.append(_remote(ins[k].at[mine], arrived, send_sems.at[s], recv_sems.at[s], device))
            passes.append(_remote(arrived, arrived, pass_send_sems.at[s], pass_recv_sems.at[s], sibling))
            passed = outs[k].at[idx].at[other]
            pass_arrivals.append(_remote(passed, passed, pass_send_sems.at[s], pass_recv_sems.at[s], sibling))
    return local, sends, arrivals, passes, pass_arrivals


def _gather_start(*refs):
    local, sends, _, _, _ = _gather_copies(*refs, starting=True)
    for cp in local + sends:
        cp.start()


def _gather_finish(*refs):
    local, sends, arrivals, passes, pass_arrivals = _gather_copies(*refs, starting=False)
    for arrival, onward in zip(arrivals, passes):
        arrival.wait_recv()
        onward.start()
    for cp in pass_arrivals:
        cp.wait_recv()
    for cp in sends + passes:
        cp.wait_send()
    for cp in local:
        cp.wait()


def chip_gather(arrays):
    n = len(arrays)

    def body(*refs):
        _gather_start(refs[:n], refs[n:2 * n], *refs[2 * n:])
        _gather_finish(refs[:n], refs[n:2 * n], *refs[2 * n:])

    return pl.pallas_call(
        body, name="chip_gather", in_specs=[_ANY] * n, out_specs=[_ANY] * n,
        out_shape=[jax.ShapeDtypeStruct((N_CHIPS,) + a.shape, a.dtype) for a in arrays],
        scratch_shapes=_gather_sems(n),
    )(*arrays)


def _sibling_sems(n):
    return [pltpu.SemaphoreType.DMA((n,)), pltpu.SemaphoreType.DMA((n,))]


def _sibling_copies(ins, outs, send_sems, recv_sems):
    sibling = (lax.axis_index("x"), lax.axis_index("y"), 1 - lax.axis_index("c"))
    return [_remote(ins[k], outs[k], send_sems.at[k], recv_sems.at[k], sibling) for k in range(len(ins))]


def all_gather_small(vec, to_sibling=()):
    nsib = len(to_sibling)

    def body(*refs):
        v_ref, out_ref = refs[0], refs[1 + nsib]
        send_sems, recv_sems, local_sem = refs[2 + 2 * nsib:5 + 2 * nsib]
        passed = _sibling_copies(refs[1:1 + nsib], refs[2 + nsib:2 + 2 * nsib], *refs[5 + 2 * nsib:]) if nsib else []
        for cp in passed:
            cp.start()
        x, y, c = lax.axis_index("x"), lax.axis_index("y"), lax.axis_index("c")
        me = 4 * x + 2 * y + c
        local = pltpu.make_async_copy(v_ref, out_ref.at[me], local_sem)
        local.start()
        sends, recvs = [], []
        for j in range(1, N_DEV):
            px = jnp.where(j & 4, 1 - x, x)
            py = jnp.where(j & 2, 1 - y, y)
            pc = jnp.where(j & 1, 1 - c, c)
            common = dict(send_sem=send_sems.at[j - 1], recv_sem=recv_sems.at[j - 1], device_id=(px, py, pc),
                          device_id_type=MESH)
            sends.append(pltpu.make_async_remote_copy(src_ref=v_ref, dst_ref=out_ref.at[me], **common))
            recvs.append(pltpu.make_async_remote_copy(src_ref=v_ref, dst_ref=out_ref.at[4 * px + 2 * py + pc],
                                                      **common))
        for cp in sends:
            cp.start()
        for cp in recvs:
            cp.wait_recv()
        for cp in sends:
            cp.wait_send()
        local.wait()
        for cp in passed:
            cp.wait()

    out = pl.pallas_call(
        body, name="all_gather_small", in_specs=[_ANY] * (1 + nsib), out_specs=[_ANY] * (1 + nsib),
        out_shape=[jax.ShapeDtypeStruct((N_DEV,) + vec.shape, vec.dtype)]
        + [jax.ShapeDtypeStruct(a.shape, a.dtype) for a in to_sibling],
        scratch_shapes=[pltpu.SemaphoreType.DMA((N_DEV - 1,)), pltpu.SemaphoreType.DMA((N_DEV - 1,)),
                        pltpu.SemaphoreType.DMA] + (_sibling_sems(nsib) if nsib else []),
    )(vec, *to_sibling)
    return out[0], out[1:]


def sum_slots(stacked, tm=256):
    s, r, c = stacked.shape
    tm = min(tm, r)

    def body(in_ref, out_ref):
        acc = in_ref[0].astype(F32)
        for t in range(1, s):
            acc = acc + in_ref[t].astype(F32)
        out_ref[...] = acc

    return pl.pallas_call(
        body, name=f"sum_slots_{s}_{r}_{c}", grid=(r // tm,),
        in_specs=[pl.BlockSpec((s, tm, c), lambda i: (0, i, 0))], out_specs=_row_spec(tm, c),
        out_shape=jax.ShapeDtypeStruct((r, c), F32), compiler_params=_cparams(1),
    )(stacked)


def adamw(w, m, v, g_a, g_b=None, tm=512):
    r, c = w.shape
    tm = min(tm, r)
    two = g_b is not None
    slots = g_a.shape[0] if g_a.ndim == 3 else 0

    def body(*refs):
        w_ref, m_ref, v_ref, ga_ref = refs[:4]
        g_ref, d_ref, nm_ref, nv_ref = refs[-4:]
        if slots:
            g = ga_ref[0]
            for t in range(1, slots):
                g = g + ga_ref[t]
        else:
            g = ga_ref[...] + refs[4][...] if two else ga_ref[...]
        nm = ADAM_B1 * m_ref[...] + (1.0 - ADAM_B1) * g
        nv = ADAM_B2 * v_ref[...] + (1.0 - ADAM_B2) * (g * g)
        m_hat = nm / (1.0 - ADAM_B1 ** ADAM_STEP)
        v_hat = nv / (1.0 - ADAM_B2 ** ADAM_STEP)
        g_ref[...] = g
        d_ref[...] = -ADAM_LR * (m_hat / (jnp.sqrt(v_hat) + ADAM_EPS) + ADAM_WD * w_ref[...])
        nm_ref[...] = nm
        nv_ref[...] = nv

    args = [w, m, v, g_a] + ([g_b] if two else [])
    in_specs = [_row_spec(tm, c)] * len(args)
    if slots:
        in_specs[3] = pl.BlockSpec((slots, tm, c), lambda i: (0, i, 0))
    return pl.pallas_call(
        body, name=f"adamw_{r}_{c}", grid=(r // tm,),
        in_specs=in_specs, out_specs=[_row_spec(tm, c)] * 4,
        out_shape=[jax.ShapeDtypeStruct((r, c), F32)] * 4, compiler_params=_cparams(1),
    )(*args)


_SMALL = (("pre_norm_w", (2, D_MODEL)), ("post_norm_w", (2, D_MODEL)), ("attn_b_in", (1, ATTN_IN)),
          ("attn_sinks", (1, N_HEADS)), ("attn_b_out", (1, D_MODEL)), ("rec_lb_logits", (2, D_MODEL)),
          ("rec_gnorm_w", (1, REC_DIM)))
_SMALL_ROWS = 16


def _pack_small(parts, last_row=None):
    rows = []
    for (name, shape) in _SMALL:
        flat = parts[name].reshape(-1)
        pad = -flat.shape[0] % D_MODEL
        rows.append(jnp.pad(flat, (0, pad)).reshape(-1, D_MODEL))
    used = sum(r.shape[0] for r in rows)
    rows.append(jnp.zeros((_SMALL_ROWS - 1 - used, D_MODEL), F32))
    rows.append(jnp.zeros((1, D_MODEL), F32) if last_row is None else last_row)
    return jnp.concatenate(rows, axis=0)


def _unpack_small(packed):
    out, row = {}, 0
    for (name, shape) in _SMALL:
        size = shape[0] * shape[1]
        nrows = -(-size // D_MODEL)
        out[name] = packed[row:row + nrows].reshape(-1)[:size].reshape(shape)
        row += nrows
    return out


_CARRIED = ("rec_w_in", "rec_w_out", "attn_w_out")


_LATE = ("attn_w_out", "rec_w_in", "rec_w_out")


def local_step(x, positions, pre_norm_w, post_norm_w, attn_w_in, attn_b_in, attn_sinks, attn_w_out, attn_b_out,
               rec_w_in, rec_lb_logits, rec_gnorm_w, rec_w_out, loss_target, distributed=False):
    batch, seq, _ = x.shape
    n = batch * seq
    x0 = x.reshape(n, D_MODEL)
    angles = _rope_angles(positions)
    pre0, pre1 = pre_norm_w[0:1], pre_norm_w[1:2]
    post0, post1 = post_norm_w[0:1], post_norm_w[1:2]

    late = (attn_w_out, rec_w_in, rec_w_out) if distributed else ()
    (h0, q, k, v, z), late = attn_in_proj(x0, pre0, attn_w_in, attn_b_in, angles, to_bf16=late)
    sink_tab = _sink_table(attn_sinks)
    og0, gathered = attn_fwd(q, k, v, z, sink_tab, batch, seq, gather=late)
    if distributed:
        attn_w_out, rec_w_in, rec_w_out = (g if name == "rec_w_in" else _whole_from_shards(name, g)
                                           for name, g in zip(_LATE, gathered))
    y0, x1 = out_proj(og0, attn_w_out, attn_b_out, x0, post0)

    h1, proj1 = rec_in_proj(x1, pre1, rec_w_in)
    og1, states, safe = rec_fwd(proj1, rec_lb_logits, rec_gnorm_w, batch, seq)
    dx2, loss_vec, dog1, d_rec_w_out, d_post1, d_rec_w_out_bf16 = out_proj_loss_bwd(
        og1, rec_w_out, x1, post1, loss_target.reshape(n, D_MODEL))
    dproj1, d_lb, d_gnorm = rec_bwd(proj1, states, safe, rec_lb_logits, rec_gnorm_w, dog1, batch, seq)
    dx1, d_pre1, _, _ = in_proj_bwd_x(dproj1, rec_w_in, x1, pre1, dx2)
    d_rec_w_in, _ = in_proj_bwd_w(h1, dproj1, as_shards=distributed)

    dog0, d_attn_w_out, d_attn_b_out, d_post0, d_attn_w_out_bf16 = out_proj_bwd(dx1, y0, og0, attn_w_out, post0)
    ready = dict(rec_w_in=d_rec_w_in, rec_w_out=_shards_from_whole("rec_w_out", d_rec_w_out_bf16),
                 attn_w_out=_shards_from_whole("attn_w_out", d_attn_w_out_bf16))
    outgoing = [ready[name] for name in _CARRIED] if distributed else []
    dproj0, dk, dv, d_sink_tab, arrived = attn_bwd(q, k, v, z, sink_tab, dog0, angles, batch, seq, scatter=outgoing)
    d_sinks = jnp.transpose(jnp.sum(d_sink_tab, axis=-1), (0, 2, 1)).reshape(1, N_HEADS)
    d_attn_w_in, d_attn_b_in, dproj0, *summed = in_proj_bwd_w(h0, dproj0, kv=(dk, dv, angles),
                                                              slot_sums=list(arrived))
    last = [_shards_from_whole("attn_w_in", d_attn_w_in).astype(BF16)] if distributed else []
    dx0, d_pre0, arrived_last, theirs = in_proj_bwd_x(dproj0, attn_w_in, x0, pre0, dx1, scatter=last,
                                                      to_sibling=summed)

    grads = dict(
        pre_norm_w=jnp.concatenate([d_pre0, d_pre1], axis=0), post_norm_w=jnp.concatenate([d_post0, d_post1], axis=0),
        attn_w_in=d_attn_w_in, attn_b_in=d_attn_b_in, attn_sinks=d_sinks, attn_w_out=d_attn_w_out,
        attn_b_out=d_attn_b_out, rec_w_in=d_rec_w_in, rec_lb_logits=d_lb, rec_gnorm_w=d_gnorm,
        rec_w_out=d_rec_w_out)
    exchanged = dict(zip(_CARRIED, zip(summed, theirs)))
    exchanged.update(zip(("attn_w_in",), arrived_last))
    return loss_vec, dx0.reshape(batch, seq, D_MODEL), grads, exchanged


_BIG = ("attn_w_in", "attn_w_out", "rec_w_in", "rec_w_out")
_COLUMN_SHARDED = ("attn_w_in", "rec_w_in")
_ORDER = ("pre_norm_w", "post_norm_w", "attn_w_in", "attn_b_in", "attn_sinks", "attn_w_out", "attn_b_out",
          "rec_w_in", "rec_lb_logits", "rec_gnorm_w", "rec_w_out")


def _whole_from_shards(name, stacked):
    if name in _COLUMN_SHARDED:
        return jnp.transpose(stacked, (1, 0, 2)).reshape(stacked.shape[1], -1)
    return stacked.reshape(-1, stacked.shape[2])


def _shards_from_whole(name, whole):
    if name in _COLUMN_SHARDED:
        return jnp.transpose(whole.reshape(whole.shape[0], N_CHIPS, -1), (1, 0, 2))
    return whole.reshape(N_CHIPS, -1, whole.shape[1])


def kernel(x, positions, pre_norm_w, post_norm_w, attn_w_in, attn_b_in, attn_sinks, attn_w_out, attn_b_out, rec_w_in, rec_lb_logits, rec_gnorm_w, rec_w_out, loss_target, m_pre_norm_w, m_post_norm_w, m_attn_w_in, m_attn_b_in, m_attn_sinks, m_attn_w_out, m_attn_b_out, m_rec_w_in, m_rec_lb_logits, m_rec_gnorm_w, m_rec_w_out, v_pre_norm_w, v_post_norm_w, v_attn_w_in, v_attn_b_in, v_attn_sinks, v_attn_w_out, v_attn_b_out, v_rec_w_in, v_rec_lb_logits, v_rec_gnorm_w, v_rec_w_out):
    w = dict(pre_norm_w=pre_norm_w, post_norm_w=post_norm_w, attn_w_in=attn_w_in, attn_b_in=attn_b_in,
             attn_sinks=attn_sinks, attn_w_out=attn_w_out, attn_b_out=attn_b_out, rec_w_in=rec_w_in,
             rec_lb_logits=rec_lb_logits, rec_gnorm_w=rec_gnorm_w, rec_w_out=rec_w_out)
    m = dict(pre_norm_w=m_pre_norm_w, post_norm_w=m_post_norm_w, attn_w_in=m_attn_w_in, attn_b_in=m_attn_b_in,
             attn_sinks=m_attn_sinks, attn_w_out=m_attn_w_out, attn_b_out=m_attn_b_out, rec_w_in=m_rec_w_in,
             rec_lb_logits=m_rec_lb_logits, rec_gnorm_w=m_rec_gnorm_w, rec_w_out=m_rec_w_out)
    v = dict(pre_norm_w=v_pre_norm_w, post_norm_w=v_post_norm_w, attn_w_in=v_attn_w_in, attn_b_in=v_attn_b_in,
             attn_sinks=v_attn_sinks, attn_w_out=v_attn_w_out, attn_b_out=v_attn_b_out, rec_w_in=v_rec_w_in,
             rec_lb_logits=v_rec_lb_logits, rec_gnorm_w=v_rec_gnorm_w, rec_w_out=v_rec_w_out)

    shards = {name: w[name][0] for name in _BIG}
    attn_w_in_whole = _whole_from_shards("attn_w_in", chip_gather([shards["attn_w_in"].astype(BF16)])[0])

    loss_vec, grad_x, grads, exchanged = local_step(
        x, positions, pre_norm_w, post_norm_w, attn_w_in_whole, attn_b_in, attn_sinks, shards["attn_w_out"],
        attn_b_out, shards["rec_w_in"], rec_lb_logits, rec_gnorm_w, shards["rec_w_out"], loss_target,
        distributed=True)

    mine = sum_slots(exchanged["attn_w_in"])
    small_parts, (other,) = all_gather_small(_pack_small(grads, last_row=loss_vec), to_sibling=[mine])
    exchanged["attn_w_in"] = (mine, other)
    out_g, out_d, out_m, out_v = {}, {}, {}, {}
    for name in _BIG:
        mine, other = exchanged[name]
        g, d, nm, nv = adamw(shards[name], m[name][0], v[name][0], mine, other)
        out_g[name], out_d[name], out_m[name], out_v[name] = g[None], d[None], nm[None], nv[None]

    packed = adamw(_pack_small(w), _pack_small(m), _pack_small(v), small_parts)
    loss = jnp.sum(packed[0][_SMALL_ROWS - 1]) * (0.5 / D_MODEL)
    for dst, val in zip((out_g, out_d, out_m, out_v), packed):
        dst.update(_unpack_small(val))

    return (loss, grad_x, *[out_g[n] for n in _ORDER], *[out_d[n] for n in _ORDER],
            *[out_m[n] for n in _ORDER], *[out_v[n] for n in _ORDER])
```

```python
import functools

import jax
import jax.numpy as jnp
from jax import lax
from jax.experimental import pallas as pl
from jax.experimental.pallas import tpu as pltpu

F32 = jnp.float32
BF16 = jnp.bfloat16
MESH = pl.DeviceIdType.MESH

D_MODEL = 1024
HEAD_DIM = 64
N_HEADS = 16
N_KV_HEADS = 2
GROUP = N_HEADS // N_KV_HEADS
KV_WIDTH = N_KV_HEADS * HEAD_DIM
ATTN_IN = 2 * D_MODEL + 2 * KV_WIDTH
ATTN_BLOCK = 128
ROPE_THETA = 500000.0
ROPE_DIM = HEAD_DIM // 4
REC_HEADS = 8
REC_DIM = 128
REC_IN = 4 * D_MODEL
REC_BLOCK = 128
DIAG = 8
NORM_EPS = 1e-6
N_CHIPS = 4
N_DEV = 8
LANES = 128

ADAM_LR = 0.001
ADAM_B1 = 0.9
ADAM_B2 = 0.999
ADAM_EPS = 1e-08
ADAM_WD = 0.01
ADAM_STEP = 10

VMEM_LIMIT = 56 * 1024 * 1024


def _cparams(n_axes):
    return pltpu.CompilerParams(dimension_semantics=("arbitrary",) * n_axes, vmem_limit_bytes=VMEM_LIMIT)


def _dot(a, b, contract):
    return lax.dot_general(a.astype(BF16), b.astype(BF16), (contract, ((), ())), preferred_element_type=F32)


_NN = ((1,), (0,))
_NT = ((1,), (1,))
_TN = ((0,), (0,))


@jax.custom_vjp
def mm_nn(a, b):
    return _dot(a, b, _NN)


mm_nn.defvjp(lambda a, b: (_dot(a, b, _NN), (a, b)),
             lambda res, g: (_dot(g, res[1], _NT), _dot(res[0], g, _TN)))


@jax.custom_vjp
def mm_nt(a, b):
    return _dot(a, b, _NT)


mm_nt.defvjp(lambda a, b: (_dot(a, b, _NT), (a, b)),
             lambda res, g: (_dot(g, res[1], _NN), _dot(g, res[0], _TN)))


@jax.custom_vjp
def mm_tn(a, b):
    return _dot(a, b, _TN)


mm_tn.defvjp(lambda a, b: (_dot(a, b, _TN), (a, b)),
             lambda res, g: (_dot(res[1], g, _NT), _dot(res[0], g, _NN)))


def _tri_dot(x, lower):
    n = x.shape[0]
    r = lax.broadcasted_iota(jnp.int32, (n, n), 0)
    c = lax.broadcasted_iota(jnp.int32, (n, n), 1)
    tri = ((c <= r) if lower else (c >= r)).astype(BF16)
    hi = x.astype(BF16)
    rest = x - hi.astype(F32)
    mid = rest.astype(BF16)
    lo = (rest - mid.astype(F32)).astype(BF16)
    dot = lambda p: lax.dot_general(tri, p, (_NN, ((), ())), preferred_element_type=F32)
    return (dot(lo) + dot(mid)) + dot(hi)


@jax.custom_vjp
def cumsum_rows(x):
    return _tri_dot(x, True)


cumsum_rows.defvjp(lambda x: (cumsum_rows(x), None), lambda _, g: (_tri_dot(g, False),))


@functools.partial(jax.custom_vjp, nondiff_argnums=(1,))
def roll_sub(x, d):
    return pltpu.roll(x, d, 1) if d else x


roll_sub.defvjp(lambda x, d: (roll_sub(x, d), None),
                lambda d, _, g: (roll_sub(g, (DIAG - d) % DIAG),))


def sigmoid(x):
    return 1.0 / (1.0 + jnp.exp(-x))


@jax.custom_vjp
def silu(x):
    return x * sigmoid(x)


def _silu_fwd(x):
    s = sigmoid(x)
    return x * s, (x, s)


silu.defvjp(_silu_fwd, lambda res, g: (g * (res[1] * (1.0 + res[0] * (1.0 - res[1]))),))


F32_TINY = 1.17549435e-38


def sigmoid_pair(x):
    e = jnp.exp(-jnp.abs(x))
    r = 1.0 / (1.0 + e)
    er = e * r
    pos = x >= 0.0
    return jnp.where(pos, r, er), jnp.where(pos, er, r)


def _forget_fwd(x, a):
    lb, one_m_lb = sigmoid_pair(a)
    sp, sn = sigmoid_pair(x)
    f = lb + one_m_lb * sp
    k = one_m_lb * sn
    return (jnp.log(jnp.maximum(f, F32_TINY)), k), (sp, sn, f, k, lb, one_m_lb)


def _forget_bwd(res, g):
    sp, sn, f, k, lb, one_m_lb = res
    g_lf, g_k = g
    t = jnp.where(f >= F32_TINY, g_lf / jnp.maximum(f, F32_TINY), 0.0) - g_k
    return (k * sp) * t, jnp.sum(sn * t, axis=0, keepdims=True) * (lb * one_m_lb)


@jax.custom_vjp
def forget_gate(x, a):
    return _forget_fwd(x, a)[0]


forget_gate.defvjp(_forget_fwd, _forget_bwd)


@jax.custom_vjp
def decayed(x, e):
    return (x * jnp.exp(e)).astype(BF16).astype(F32)


def _decayed_fwd(x, e):
    y = decayed(x, e)
    return y, (y, e)


decayed.defvjp(_decayed_fwd, lambda res, g: (g * jnp.exp(res[1]), g * res[0]))


def _row(x, r):
    shape = x.shape

    @jax.custom_vjp
    def take(x):
        return x[r:r + 1, :]

    take.defvjp(lambda x: (x[r:r + 1, :], None),
                lambda _, g: (jnp.where(lax.broadcasted_iota(jnp.int32, shape, 0) == r, g, 0.0),))
    return take(x)


def _rms(x):
    return lax.rsqrt(jnp.mean(x * x, axis=-1, keepdims=True) + NORM_EPS)


def _attn_group(qs, k_a, v_a, k_b, v_b, zs, sink_a, sink_b, bias, at_sink=None):
    def half(kh, vh, sink):
        s = mm_nn(qs, kh) + bias
        if at_sink is None:
            m = jnp.maximum(jnp.max(s, axis=-1, keepdims=True), jnp.max(sink, axis=-1, keepdims=True))
            p = jnp.exp(s - lax.stop_gradient(m))
            own = jnp.sum(jnp.exp(sink - lax.stop_gradient(m)), axis=-1, keepdims=True) * (1.0 / LANES)
            return mm_nt(p * (1.0 / (jnp.sum(p, axis=-1, keepdims=True) + own)), vh)
        s = jnp.where(at_sink, jnp.concatenate([sink, sink], axis=1), s)
        p = jnp.exp(s - jnp.max(s, axis=-1, keepdims=True))
        return mm_nt(jnp.where(at_sink, 0.0, p), vh) * (1.0 / jnp.sum(p, axis=-1, keepdims=True))

    return (half(k_a, v_a, sink_a) + half(k_b, v_b, sink_b)) * silu(zs)


SAFE_RANGE = 80.0


def _rec_front(qr, fr, l0, l1):
    lf, k = forget_gate(fr, l1 - l0)
    return silu(qr), k, lf


def _rec_tail(o, z, gw):
    return o * _rms(o) * gw * silu(z)


def _rec_margin(b):
    R = b.shape[0]
    mid, last = _row(b, R // 2 - 1), _row(b, R - 1)
    return jnp.minimum(mid, last - mid)


def _heads(x):
    w = x.shape[1] // REC_HEADS
    return [x[:, h * w:(h + 1) * w] for h in range(REC_HEADS)]


def _hdot(a, b, contract):
    return jnp.concatenate([_dot(ah, bh, contract) for ah, bh in zip(_heads(a), _heads(b))], axis=1)


@jax.custom_vjp
def hmm_nn(a, b):
    return _hdot(a, b, _NN)


hmm_nn.defvjp(lambda a, b: (_hdot(a, b, _NN), (a, b)),
              lambda res, g: (_hdot(g, res[1], _NT), _hdot(res[0], g, _TN)))


@jax.custom_vjp
def hmm_nt(a, b):
    return _hdot(a, b, _NT)


hmm_nt.defvjp(lambda a, b: (_hdot(a, b, _NT), (a, b)),
              lambda res, g: (_hdot(g, res[1], _NN), _hdot(g, res[0], _TN)))


@jax.custom_vjp
def hmm_tn(a, b):
    return _hdot(a, b, _TN)


hmm_tn.defvjp(lambda a, b: (_hdot(a, b, _TN), (a, b)),
              lambda res, g: (_hdot(res[1], g, _NT), _hdot(res[0], g, _NN)))


def _head_sums(x):
    return jnp.concatenate([jnp.broadcast_to(jnp.sum(xh, axis=-1, keepdims=True), xh.shape) for xh in _heads(x)],
                           axis=1)


@jax.custom_vjp
def head_sum(x):
    return _head_sums(x)


head_sum.defvjp(lambda x: (_head_sums(x), None), lambda _, g: (_head_sums(g),))


def _rec_cores_fast(q, k, v, b, S):
    R = q.shape[0]
    ri = lax.broadcasted_iota(jnp.int32, (R, REC_HEADS * R), 0)
    ci = lax.broadcasted_iota(jnp.int32, (R, REC_HEADS * R), 1) % R
    d = b - _row(b, R // 2 - 1)
    sc = jnp.where(ci < ri, hmm_nt(decayed(q, d), decayed(k, -d)), 0.0)
    o = hmm_nt(q * jnp.exp(b), S) + hmm_nn(sc, v) + head_sum(q * k) * v
    b_last = _row(b, R - 1)
    return o, S * jnp.exp(b_last) + hmm_tn(v, k * jnp.exp(b_last - b))


def _rec_tails(o, z, gw):
    return o * lax.rsqrt(head_sum(o * o) * (1.0 / REC_DIM) + NORM_EPS) * gw * silu(z)


def _rec_block_fast(qr, fr, v, z, S, l0, l1, gw):
    lf, k = forget_gate(fr, l1 - l0)
    o, S_new = _rec_cores_fast(silu(qr), k, v, cumsum_rows(lf), S)
    return _rec_tails(o, z, gw), S_new


def _rec_core_slow(q, k, v, b, S):
    R = q.shape[0]
    rows = lax.broadcasted_iota(jnp.int32, (R, REC_DIM), 0)

    o = mm_nt(q * jnp.exp(jnp.minimum(b, 0.0)), S)

    ri = lax.broadcasted_iota(jnp.int32, (R, R), 0)
    ci = lax.broadcasted_iota(jnp.int32, (R, R), 1)
    sc = jnp.zeros((R, R), F32)
    w = R
    while w > DIAG:
        h = w // 2
        b3 = b.reshape(R // w, w, REC_DIM)
        rin = lax.broadcasted_iota(jnp.int32, (R // w, w, REC_DIM), 1)
        mid = jnp.sum(jnp.where(rin == h - 1, b3, 0.0), axis=1, keepdims=True)
        fac = jnp.exp(jnp.minimum(jnp.where(rin >= h, b3 - mid, mid - b3), 0.0)).reshape(R, REC_DIM)
        upper = (rows % w) >= h
        s_w = mm_nt(jnp.where(upper, q * fac, 0.0), jnp.where(upper, 0.0, k * fac))
        sc = sc + jnp.where((ri // w) == (ci // w), s_w, 0.0)
        w = h
    o = o + mm_nn(sc, v)

    g = R // DIAG
    q3, k3, v3, b3 = (t.reshape(g, DIAG, REC_DIM) for t in (q, k, v, b))
    rin = lax.broadcasted_iota(jnp.int32, (g, DIAG, 1), 1)
    od = jnp.zeros((g, DIAG, REC_DIM), F32)
    for d in range(DIAG):
        e = jnp.exp(jnp.minimum(b3 - roll_sub(b3, d), 0.0))
        sd = jnp.sum(q3 * roll_sub(k3, d) * e, axis=-1, keepdims=True)
        od = od + jnp.where(rin >= d, sd, 0.0) * roll_sub(v3, d)
    o = o + od.reshape(R, REC_DIM)

    b_last = _row(b, R - 1)
    return o, S * jnp.exp(jnp.minimum(b_last, 0.0)) + mm_tn(v, k * jnp.exp(jnp.minimum(b_last - b, 0.0)))


def _rec_head(core, qr, fr, v, z, S, l0, l1, gw):
    q, k, lf = _rec_front(qr, fr, l0, l1)
    o, S_new = core(q, k, v, cumsum_rows(lf), S)
    return _rec_tail(o, z, gw), S_new


ANGLE_COLS = 3 * ROPE_DIM


def _rope_angles(positions):
    half = ROPE_DIM // 2
    inv_freq = ROPE_THETA ** (-(jnp.arange(half, dtype=F32) * 2.0 / ROPE_DIM))
    ang = positions.astype(F32).reshape(-1, 1) * inv_freq
    cs = jnp.concatenate([jnp.cos(ang), jnp.sin(ang)], axis=-1)
    hi = cs.astype(BF16)
    rest = cs - hi.astype(F32)
    mid = rest.astype(BF16)
    return jnp.concatenate([hi, mid, (rest - mid.astype(F32)).astype(BF16)], axis=-1)


def _rope_tables(pieces):
    half = ROPE_DIM // 2
    r = lax.broadcasted_iota(jnp.int32, (ANGLE_COLS, 3 * LANES), 0) % ROPE_DIM
    c = lax.broadcasted_iota(jnp.int32, (ANGLE_COLS, 3 * LANES), 1)
    table, j = c // LANES, c % HEAD_DIM
    angle, low = j % half, j < half
    plus = ((table == 0) & (j < ROPE_DIM) & (r == angle)) | ((table == 1) & (j >= half) & (j < ROPE_DIM)
                                                                & (r == half + angle))
    minus = (table == 2) & low & (r == half + angle)
    pick = jnp.where(plus, 1.0, jnp.where(minus, -1.0, 0.0)).astype(BF16)
    out = jnp.dot(pieces, pick, preferred_element_type=F32)
    lane = lax.broadcasted_iota(jnp.int32, (1, LANES), 1) % HEAD_DIM
    return out[:, :LANES] + jnp.where(lane < ROPE_DIM, 0.0, 1.0), out[:, LANES:2 * LANES], out[:, 2 * LANES:]


def _rope(x, cos_t, sin_a, sin_b):
    half = ROPE_DIM // 2
    return x * cos_t + pltpu.roll(x, half, 1) * sin_a + pltpu.roll(x, LANES - half, 1) * sin_b


def _rope_transposed(g, cos_t, sin_a, sin_b):
    half = ROPE_DIM // 2
    return g * cos_t + pltpu.roll(g * sin_a, LANES - half, 1) + pltpu.roll(g * sin_b, half, 1)


def _row_spec(tm, width):
    return pl.BlockSpec((tm, width), lambda i: (i, 0))


def _weight_spec(shape):
    return pl.BlockSpec(shape, lambda *_: (0,) * len(shape), pipeline_mode=pl.Buffered(1))


def _full_spec(shape):
    return pl.BlockSpec(shape, lambda *_: (0,) * len(shape))


def attn_in_proj(x, w_pre, w_in, b_in, angles, tm=1024, to_bf16=()):
    n = x.shape[0]
    tm = min(tm, n)
    nc = len(to_bf16)

    def body(*refs):
        x_ref, wp_ref, w_ref, b_ref, cs_ref = refs[:5]
        h_ref, q_ref, k_ref, v_ref, z_ref = refs[5 + nc:10 + nc]

        @pl.when(pl.program_id(0) == 0)
        def _():
            for src, dst in zip(refs[5:5 + nc], refs[10 + nc:]):
                dst[...] = src[...].astype(BF16)

        xv = x_ref[...]
        h = (xv * _rms(xv) * wp_ref[...]).astype(BF16)
        h_ref[...] = h
        proj = jnp.dot(h, w_ref[...], preferred_element_type=F32) + b_ref[...]
        tabs = _rope_tables(cs_ref[...])
        for s in range(D_MODEL // LANES):
            sl = slice(s * LANES, (s + 1) * LANES)
            q_ref[:, sl] = _rope(proj[:, sl] * (HEAD_DIM ** -0.5), *tabs).astype(BF16)
        k_ref[...] = _rope(proj[:, D_MODEL:D_MODEL + KV_WIDTH], *tabs).astype(BF16)
        v_ref[...] = proj[:, D_MODEL + KV_WIDTH:D_MODEL + 2 * KV_WIDTH].astype(BF16)
        z_ref[...] = proj[:, D_MODEL + 2 * KV_WIDTH:]

    out = pl.pallas_call(
        body, name="attn_in_proj", grid=(n // tm,),
        in_specs=[_row_spec(tm, D_MODEL), _full_spec((1, D_MODEL)), _weight_spec((D_MODEL, ATTN_IN)),
                  _full_spec((1, ATTN_IN)), _row_spec(tm, ANGLE_COLS)] + [_weight_spec(a.shape) for a in to_bf16],
        out_specs=[_row_spec(tm, D_MODEL), _row_spec(tm, D_MODEL), _row_spec(tm, KV_WIDTH),
                   _row_spec(tm, KV_WIDTH), _row_spec(tm, D_MODEL)] + [_full_spec(a.shape) for a in to_bf16],
        out_shape=[jax.ShapeDtypeStruct((n, D_MODEL), BF16), jax.ShapeDtypeStruct((n, D_MODEL), BF16),
                   jax.ShapeDtypeStruct((n, KV_WIDTH), BF16), jax.ShapeDtypeStruct((n, KV_WIDTH), BF16),
                   jax.ShapeDtypeStruct((n, D_MODEL), F32)] + [jax.ShapeDtypeStruct(a.shape, BF16) for a in to_bf16],
        compiler_params=_cparams(1),
    )(x, w_pre, w_in, b_in, angles, *to_bf16)
    return out[:5], out[5:]


def _column_blocks(w):
    if len(w.shape) == 2:
        return [slice(0, w.shape[1])], lambda ref, s: ref[...]
    width = w.shape[2]
    return [slice(s * width, (s + 1) * width) for s in range(w.shape[0])], lambda ref, s: ref[s]


def rec_in_proj(x, w_pre, w_in, tm=1024):
    n = x.shape[0]
    tm = min(tm, n)
    columns, block = _column_blocks(w_in)

    def body(x_ref, wp_ref, w_ref, h_ref, p_ref):
        xv = x_ref[...]
        h = (xv * _rms(xv) * wp_ref[...]).astype(BF16)
        h_ref[...] = h
        for s, cols in enumerate(columns):
            p_ref[:, cols] = jnp.dot(h, block(w_ref, s), preferred_element_type=F32)

    return pl.pallas_call(
        body, name="rec_in_proj", grid=(n // tm,),
        in_specs=[_row_spec(tm, D_MODEL), _full_spec((1, D_MODEL)), _weight_spec(w_in.shape)],
        out_specs=[_row_spec(tm, D_MODEL), _row_spec(tm, REC_IN)],
        out_shape=[jax.ShapeDtypeStruct((n, D_MODEL), BF16), jax.ShapeDtypeStruct((n, REC_IN), F32)],
        compiler_params=_cparams(1),
    )(x, w_pre, w_in)


def out_proj(og, w_out, b_out, x_res, w_post, tm=1024):
    n = og.shape[0]
    tm = min(tm, n)

    def body(og_ref, w_ref, b_ref, x_ref, wp_ref, y_ref, xo_ref):
        y = jnp.dot(og_ref[...], w_ref[...], preferred_element_type=F32) + b_ref[...]
        y_ref[...] = y.astype(BF16)
        xo_ref[...] = x_ref[...] + y * _rms(y) * wp_ref[...]

    return pl.pallas_call(
        body, name="out_proj", grid=(n // tm,),
        in_specs=[_row_spec(tm, D_MODEL), _weight_spec((D_MODEL, D_MODEL)), _full_spec((1, D_MODEL)),
                  _row_spec(tm, D_MODEL), _full_spec((1, D_MODEL))],
        out_specs=[_row_spec(tm, D_MODEL), _row_spec(tm, D_MODEL)],
        out_shape=[jax.ShapeDtypeStruct((n, D_MODEL), BF16), jax.ShapeDtypeStruct((n, D_MODEL), F32)],
        compiler_params=_cparams(1),
    )(og, w_out, b_out, x_res, w_post)


def _post_norm_bwd(g, y, w_post):
    rstd = _rms(y)
    yn = y * rstd
    gw = g * w_post
    return rstd * (gw - yn * jnp.mean(gw * yn, axis=-1, keepdims=True)), jnp.sum(g * yn, axis=0, keepdims=True)


def out_proj_loss_bwd(og, w_out, x_res, w_post, target, tm=1024):
    n = og.shape[0]
    tm = min(tm, n)
    steps = n // tm

    def body(og_ref, w_ref, x_ref, wp_ref, t_ref, dx_ref, l_ref, dog_ref, dw_ref, dwp_ref, dwb_ref):
        @pl.when(pl.program_id(0) == 0)
        def _():
            l_ref[...] = jnp.zeros_like(l_ref)
            dw_ref[...] = jnp.zeros_like(dw_ref)
            dwp_ref[...] = jnp.zeros_like(dwp_ref)

        og_tile = og_ref[...]
        y = jnp.dot(og_tile, w_ref[...], preferred_element_type=F32)
        err = x_ref[...] + y * _rms(y) * wp_ref[...] - t_ref[...]
        g = err * (1.0 / D_MODEL)
        dx_ref[...] = g
        l_ref[...] += jnp.sum(err * err, axis=0, keepdims=True)
        dy, dwp = _post_norm_bwd(g, y, wp_ref[...])
        dwp_ref[...] += dwp
        dyb = dy.astype(BF16)
        dog_ref[...] = _dot(dyb, w_ref[...], _NT).astype(BF16)
        dw_ref[...] += _dot(og_tile, dyb, _TN)

        @pl.when(pl.program_id(0) == steps - 1)
        def _():
            dwb_ref[...] = dw_ref[...].astype(BF16)

    return pl.pallas_call(
        body, name="out_proj_loss_bwd", grid=(steps,),
        in_specs=[_row_spec(tm, D_MODEL), _weight_spec((D_MODEL, D_MODEL)), _row_spec(tm, D_MODEL),
                  _full_spec((1, D_MODEL)), _row_spec(tm, D_MODEL)],
        out_specs=[_row_spec(tm, D_MODEL), _full_spec((1, D_MODEL)), _row_spec(tm, D_MODEL),
                   _full_spec((D_MODEL, D_MODEL)), _full_spec((1, D_MODEL)), _full_spec((D_MODEL, D_MODEL))],
        out_shape=[jax.ShapeDtypeStruct((n, D_MODEL), F32), jax.ShapeDtypeStruct((1, D_MODEL), F32),
                   jax.ShapeDtypeStruct((n, D_MODEL), BF16), jax.ShapeDtypeStruct((D_MODEL, D_MODEL), F32),
                   jax.ShapeDtypeStruct((1, D_MODEL), F32), jax.ShapeDtypeStruct((D_MODEL, D_MODEL), BF16)],
        compiler_params=_cparams(1),
    )(og, w_out, x_res, w_post, target)


def out_proj_bwd(dxo, y, og, w_out, w_post, tm=1024):
    n = og.shape[0]
    tm = min(tm, n)
    steps = n // tm

    def body(g_ref, y_ref, og_ref, w_ref, wp_ref, dog_ref, dw_ref, db_ref, dwp_ref, dwb_ref):
        @pl.when(pl.program_id(0) == 0)
        def _():
            dw_ref[...] = jnp.zeros_like(dw_ref)
            db_ref[...] = jnp.zeros_like(db_ref)
            dwp_ref[...] = jnp.zeros_like(dwp_ref)

        dy, dwp = _post_norm_bwd(g_ref[...], y_ref[...].astype(F32), wp_ref[...])
        dwp_ref[...] += dwp
        db_ref[...] += jnp.sum(dy, axis=0, keepdims=True)
        dyb = dy.astype(BF16)
        dog_ref[...] = _dot(dyb, w_ref[...], _NT).astype(BF16)
        dw_ref[...] += _dot(og_ref[...], dyb, _TN)

        @pl.when(pl.program_id(0) == steps - 1)
        def _():
            dwb_ref[...] = dw_ref[...].astype(BF16)

    return pl.pallas_call(
        body, name="out_proj_bwd", grid=(steps,),
        in_specs=[_row_spec(tm, D_MODEL), _row_spec(tm, D_MODEL), _row_spec(tm, D_MODEL),
                  _weight_spec((D_MODEL, D_MODEL)), _full_spec((1, D_MODEL))],
        out_specs=[_row_spec(tm, D_MODEL), _full_spec((D_MODEL, D_MODEL)), _full_spec((1, D_MODEL)),
                   _full_spec((1, D_MODEL)), _full_spec((D_MODEL, D_MODEL))],
        out_shape=[jax.ShapeDtypeStruct((n, D_MODEL), BF16), jax.ShapeDtypeStruct((D_MODEL, D_MODEL), F32),
                   jax.ShapeDtypeStruct((1, D_MODEL), F32), jax.ShapeDtypeStruct((1, D_MODEL), F32),
                   jax.ShapeDtypeStruct((D_MODEL, D_MODEL), BF16)],
        compiler_params=_cparams(1),
    )(dxo, y, og, w_out, w_post)


def _slot_sum_specs(slot_sums, steps):
    return ([pl.BlockSpec((a.shape[0], a.shape[1] // steps, a.shape[2]), lambda i: (0, i, 0)) for a in slot_sums],
            [_row_spec(a.shape[1] // steps, a.shape[2]) for a in slot_sums],
            [jax.ShapeDtypeStruct(a.shape[1:], F32) for a in slot_sums])


def _sum_slots_into(slot_refs, sum_refs):
    for slots_ref, sum_ref in zip(slot_refs, sum_refs):
        sum_ref[...] = functools.reduce(jnp.add, [slots_ref[t].astype(F32) for t in range(slots_ref.shape[0])])


def in_proj_bwd_x(dproj, w_in, x, w_pre, dxo, tm=1024, scatter=(), to_sibling=()):
    n, p = dproj.shape
    tm = min(tm, n)
    steps = n // tm
    ns, nsib = len(scatter), len(to_sibling)
    columns, block = _column_blocks(w_in)

    def body(*refs):
        dp_ref, w_ref, x_ref, wp_ref, g_ref = refs[:5]
        outs = 5 + ns + nsib
        dx_ref, dwp_ref = refs[outs:outs + 2]
        sems = refs[outs + 2 + ns + nsib:]
        exchange = (refs[5:5 + ns], refs[outs + 2:outs + 2 + ns]) + tuple(sems[:3])
        sibling = (refs[5 + ns:outs], refs[outs + 2 + ns:outs + 2 + ns + nsib]) + tuple(sems[3:])

        @pl.when(pl.program_id(0) == 0)
        def _():
            dwp_ref[...] = jnp.zeros_like(dwp_ref)
            if ns:
                _scatter_start(*exchange)
            for cp in _sibling_copies(*sibling) if nsib else ():
                cp.start()

        dh = functools.reduce(jnp.add, [_dot(dp_ref[:, cols], block(w_ref, s), _NT)
                                        for s, cols in enumerate(columns)])
        xv = x_ref[...]
        rstd = _rms(xv)
        xn = xv * rstd
        gw = dh * wp_ref[...]
        dwp_ref[...] += jnp.sum(dh * xn, axis=0, keepdims=True)
        dx_ref[...] = rstd * (gw - xn * jnp.mean(gw * xn, axis=-1, keepdims=True)) + g_ref[...]

        if ns or nsib:
            @pl.when(pl.program_id(0) == steps - 1)
            def _():
                if ns:
                    _scatter_finish(*exchange)
                for cp in _sibling_copies(*sibling) if nsib else ():
                    cp.wait()

    out = pl.pallas_call(
        body, name=f"in_proj_bwd_x_{p}", grid=(steps,),
        in_specs=[_row_spec(tm, p), _weight_spec(w_in.shape), _row_spec(tm, D_MODEL), _full_spec((1, D_MODEL)),
                  _row_spec(tm, D_MODEL)] + [_ANY] * (ns + nsib),
        out_specs=[_row_spec(tm, D_MODEL), _full_spec((1, D_MODEL))] + [_ANY] * (ns + nsib),
        out_shape=[jax.ShapeDtypeStruct((n, D_MODEL), F32), jax.ShapeDtypeStruct((1, D_MODEL), F32)]
        + [jax.ShapeDtypeStruct(a.shape, a.dtype) for a in tuple(scatter) + tuple(to_sibling)],
        scratch_shapes=(_scatter_sems(ns) if ns else []) + (_sibling_sems(nsib) if nsib else []),
        compiler_params=_cparams(1),
    )(dproj, w_in, x, w_pre, dxo, *scatter, *to_sibling)
    return out[0], out[1], out[2:2 + ns], out[2 + ns:]


def in_proj_bwd_w(h, dproj, tm=1024, as_shards=False, kv=None, slot_sums=()):
    n, p = dproj.shape
    chunk = p // (4 if p % 4096 == 0 else 3)
    tm = min(tm, n)
    steps = n // tm
    shard = p // N_CHIPS
    n_kv = 0 if kv is None else 3
    n_sum = len(slot_sums)
    kv_from, kv_to = D_MODEL, D_MODEL + 2 * KV_WIDTH

    def body(*refs):
        h_ref, dp_ref = refs[:2]
        outs = 2 + n_kv + n_sum
        dw_ref, db_ref = refs[outs:outs + 2]
        sums_at = outs + 2 + (kv is not None)
        scratch = refs[sums_at + n_sum:]
        acc_scr, sem, staging = scratch[0], scratch[1], scratch[2:]
        i = pl.program_id(0)
        _sum_slots_into(refs[2 + n_kv:outs], refs[sums_at:sums_at + n_sum])

        @pl.when(i == 0)
        def _():
            acc_scr[...] = jnp.zeros_like(acc_scr)
            db_ref[...] = jnp.zeros_like(db_ref)

        if kv is not None:
            dk_ref, dv_ref, cs_ref, kv_ref = refs[2], refs[3], refs[4], refs[outs + 2]
            made = jnp.concatenate([_rope_transposed(dk_ref[...].T, *_rope_tables(cs_ref[...])), dv_ref[...].T],
                                   axis=1).astype(BF16)
            kv_ref[...] = made

        def columns(c0):
            if kv is None or c0 + chunk <= kv_from or c0 >= kv_to:
                return dp_ref[:, c0:c0 + chunk]
            return jnp.concatenate([dp_ref[:, c0:kv_from], made, dp_ref[:, kv_to:c0 + chunk]], axis=1)

        ht = h_ref[...].T
        for c0 in range(0, p, chunk):
            dp = columns(c0)
            acc_scr[:, c0:c0 + chunk] += jnp.dot(ht, dp, preferred_element_type=F32)
            db_ref[:, c0:c0 + chunk] += jnp.sum(dp.astype(F32), axis=0, keepdims=True)

        @pl.when(i == steps - 1)
        def _():
            if as_shards:
                for s in range(N_CHIPS):
                    staging[0][...] = acc_scr[:, s * shard:(s + 1) * shard].astype(BF16)
                    out = pltpu.make_async_copy(staging[0], dw_ref.at[s], sem)
                    out.start()
                    out.wait()
            else:
                out = pltpu.make_async_copy(acc_scr, dw_ref, sem)
                out.start()
                out.wait()

    dw_shape = jax.ShapeDtypeStruct((N_CHIPS, D_MODEL, shard), BF16) if as_shards else (
        jax.ShapeDtypeStruct((D_MODEL, p), F32))
    in_specs = [_row_spec(tm, D_MODEL), _row_spec(tm, p)]
    out_specs = [_ANY, _full_spec((1, p))]
    out_shape = [dw_shape, jax.ShapeDtypeStruct((1, p), F32)]
    if kv is not None:
        columns_t = pl.BlockSpec((KV_WIDTH, tm), lambda i: (0, i))
        in_specs += [columns_t, columns_t, _row_spec(tm, ANGLE_COLS)]
        out_specs.append(pl.BlockSpec((tm, kv_to - kv_from), lambda i: (i, kv_from // (kv_to - kv_from))))
        out_shape.append(jax.ShapeDtypeStruct(dproj.shape, dproj.dtype))
    sum_in, sum_out, sum_shapes = _slot_sum_specs(slot_sums, steps)
    in_specs, out_specs, out_shape = in_specs + sum_in, out_specs + sum_out, out_shape + sum_shapes
    return pl.pallas_call(
        body, name=f"in_proj_bwd_w_{p}", grid=(steps,),
        in_specs=in_specs, out_specs=out_specs, out_shape=out_shape,
        scratch_shapes=[pltpu.VMEM((D_MODEL, p), F32), pltpu.SemaphoreType.DMA]
        + ([pltpu.VMEM((D_MODEL, shard), BF16)] if as_shards else []),
        input_output_aliases={1: 2} if kv is not None else {},
        compiler_params=_cparams(1),
    )(h, dproj, *(kv or ()), *slot_sums)


PAIRS = GROUP // 2
GROUP_ROWS = PAIRS * ATTN_BLOCK
MASKED = -1e30


def _kv_windows(k_ref, v_ref, i):
    ps = pl.multiple_of(jnp.maximum(i - 1, 0) * ATTN_BLOCK, ATTN_BLOCK)
    cs = pl.multiple_of(i * ATTN_BLOCK, ATTN_BLOCK)
    kw = jnp.concatenate([k_ref[pl.ds(ps, ATTN_BLOCK), :], k_ref[pl.ds(cs, ATTN_BLOCK), :]], axis=0)
    vw = jnp.concatenate([v_ref[pl.ds(ps, ATTN_BLOCK), :], v_ref[pl.ds(cs, ATTN_BLOCK), :]], axis=0)
    return kw.astype(F32).T, vw.astype(F32).T, ps, cs


def _low_rows(shape):
    return lax.broadcasted_iota(jnp.int32, shape, 0) < HEAD_DIM


def _spread(w, kvh):
    low = _low_rows(w.shape)
    swapped = pltpu.roll(w, HEAD_DIM, 0)
    if kvh == 0:
        return jnp.where(low, w, 0.0), jnp.where(low, 0.0, swapped)
    return jnp.where(low, swapped, 0.0), jnp.where(low, 0.0, w)


def _unspread(d_a, d_b, kvh):
    low = _low_rows(d_a.shape)
    if kvh == 0:
        return jnp.where(low, d_a + pltpu.roll(d_b, HEAD_DIM, 0), 0.0)
    return jnp.where(low, 0.0, pltpu.roll(d_a, HEAD_DIM, 0) + d_b)


def _stack_pairs(ref, kvh):
    return jnp.concatenate([ref[:, (kvh * PAIRS + j) * LANES:(kvh * PAIRS + j + 1) * LANES] for j in range(PAIRS)],
                           axis=0)


def _fill_bias(bias_scr):
    shape = (GROUP_ROWS, 2 * ATTN_BLOCK)
    r = lax.broadcasted_iota(jnp.int32, shape, 0) % ATTN_BLOCK
    c = lax.broadcasted_iota(jnp.int32, shape, 1)
    in_cur = (c >= ATTN_BLOCK) & ((c - ATTN_BLOCK) <= r)
    in_prev = (c < ATTN_BLOCK) & (c > r)
    bias_scr[0] = jnp.where(in_cur, 0.0, MASKED)
    bias_scr[1] = jnp.where(in_cur | in_prev, 0.0, MASKED)
    bias_scr[2] = jnp.where(c == r, 1.0, 0.0)


N_BIAS_TABLES = 3


def _sink_table(sinks):
    t = jnp.transpose(sinks.reshape(N_KV_HEADS, PAIRS, 2), (0, 2, 1))
    return jnp.broadcast_to(t[:, :, :, None, None], (N_KV_HEADS, 2, PAIRS, ATTN_BLOCK, LANES)).reshape(
        N_KV_HEADS, 2, GROUP_ROWS, LANES)


def attn_fwd(q, k, v, z, sink_tab, batch, seq, gather=()):
    nb = seq // ATTN_BLOCK
    ng = len(gather)

    def body(*refs):
        q_ref, k_ref, v_ref, z_ref, s_ref = refs[:5]
        og_ref, bias_scr = refs[5 + ng], refs[6 + 2 * ng]
        exchange = (refs[5:5 + ng], refs[6 + ng:6 + 2 * ng]) + tuple(refs[7 + 2 * ng:])
        b, i = pl.program_id(0), pl.program_id(1)

        @pl.when((b == 0) & (i == 0))
        def _():
            _fill_bias(bias_scr)
            if ng:
                _gather_start(*exchange)

        kw, vw, _, _ = _kv_windows(k_ref, v_ref, i)
        bias, at_sink = bias_scr[jnp.minimum(i, 1)], bias_scr[2] > 0.5
        for kvh in range(N_KV_HEADS):
            k_a, k_b = _spread(kw, kvh)
            v_a, v_b = _spread(vw, kvh)
            og = _attn_group(_stack_pairs(q_ref, kvh), k_a, v_a, k_b, v_b, _stack_pairs(z_ref, kvh),
                             s_ref[kvh, 0], s_ref[kvh, 1], bias, at_sink)
            for j in range(PAIRS):
                og_ref[:, (kvh * PAIRS + j) * LANES:(kvh * PAIRS + j + 1) * LANES] = (
                    og[j * ATTN_BLOCK:(j + 1) * ATTN_BLOCK].astype(BF16))

        if ng:
            @pl.when((b == batch - 1) & (i == nb - 1))
            def _():
                _gather_finish(*exchange)

    blk = lambda w: pl.BlockSpec((ATTN_BLOCK, w), lambda b, i: (b * nb + i, 0))
    seq_spec = pl.BlockSpec((seq, KV_WIDTH), lambda b, i: (b, 0))
    out = pl.pallas_call(
        body, name="attn_fwd", grid=(batch, nb),
        in_specs=[blk(D_MODEL), seq_spec, seq_spec, blk(D_MODEL), _full_spec(sink_tab.shape)] + [_ANY] * ng,
        out_specs=[blk(D_MODEL)] + [_ANY] * ng,
        out_shape=[jax.ShapeDtypeStruct((batch * seq, D_MODEL), BF16)]
        + [jax.ShapeDtypeStruct((N_CHIPS,) + a.shape, a.dtype) for a in gather],
        scratch_shapes=[pltpu.VMEM((N_BIAS_TABLES, GROUP_ROWS, 2 * ATTN_BLOCK), F32)] + (_gather_sems(ng) if ng else []),
        compiler_params=_cparams(2),
    )(q, k, v, z, sink_tab, *gather)
    return out[0], out[1:]


def attn_bwd(q, k, v, z, sink_tab, dog, angles, batch, seq, scatter=()):
    nb = seq // ATTN_BLOCK
    ns = len(scatter)

    def body(*refs):
        q_ref, k_ref, v_ref, z_ref, s_ref, g_ref, cs_ref = refs[:7]
        dp_ref, dk_ref, dv_ref, ds_ref = refs[7 + ns:11 + ns]
        bias_scr = refs[11 + 2 * ns]
        exchange = (refs[7:7 + ns], refs[11 + ns:11 + 2 * ns]) + tuple(refs[12 + 2 * ns:])
        b, i = pl.program_id(0), pl.program_id(1)

        @pl.when((b == 0) & (i == 0))
        def _():
            _fill_bias(bias_scr)
            ds_ref[...] = jnp.zeros_like(ds_ref)
            if ns:
                _scatter_start(*exchange)

        @pl.when(i == 0)
        def _():
            dk_ref[...] = jnp.zeros_like(dk_ref)
            dv_ref[...] = jnp.zeros_like(dv_ref)

        kw, vw, ps, cs = _kv_windows(k_ref, v_ref, i)
        bias = bias_scr[jnp.minimum(i, 1)]
        tabs = _rope_tables(cs_ref[...])
        dkw = jnp.zeros_like(kw)
        dvw = jnp.zeros_like(vw)
        for kvh in range(N_KV_HEADS):
            k_a, k_b = _spread(kw, kvh)
            v_a, v_b = _spread(vw, kvh)
            _, vjp = jax.vjp(functools.partial(_attn_group, bias=bias), _stack_pairs(q_ref, kvh).astype(F32),
                             k_a, v_a, k_b, v_b, _stack_pairs(z_ref, kvh), s_ref[kvh, 0], s_ref[kvh, 1])
            dqs, dk_a, dv_a, dk_b, dv_b, dzs, ds_a, ds_b = vjp(_stack_pairs(g_ref, kvh).astype(F32))
            dkw = dkw + _unspread(dk_a, dk_b, kvh)
            dvw = dvw + _unspread(dv_a, dv_b, kvh)
            ds_ref[kvh, 0] += jnp.sum(ds_a.reshape(PAIRS, ATTN_BLOCK, LANES), axis=1)
            ds_ref[kvh, 1] += jnp.sum(ds_b.reshape(PAIRS, ATTN_BLOCK, LANES), axis=1)
            for j in range(PAIRS):
                rows = slice(j * ATTN_BLOCK, (j + 1) * ATTN_BLOCK)
                col = (kvh * PAIRS + j) * LANES
                dp_ref[:, col:col + LANES] = _rope_transposed(dqs[rows] * (HEAD_DIM ** -0.5), *tabs).astype(BF16)
                zc = D_MODEL + 2 * KV_WIDTH + col
                dp_ref[:, zc:zc + LANES] = dzs[rows].astype(BF16)
        dp_ref[:, D_MODEL:D_MODEL + 2 * KV_WIDTH] = jnp.zeros((ATTN_BLOCK, 2 * KV_WIDTH), BF16)
        dk_ref[:, pl.ds(ps, ATTN_BLOCK)] += dkw[:, :ATTN_BLOCK]
        dk_ref[:, pl.ds(cs, ATTN_BLOCK)] += dkw[:, ATTN_BLOCK:]
        dv_ref[:, pl.ds(ps, ATTN_BLOCK)] += dvw[:, :ATTN_BLOCK]
        dv_ref[:, pl.ds(cs, ATTN_BLOCK)] += dvw[:, ATTN_BLOCK:]

        if ns:
            @pl.when((b == batch - 1) & (i == nb - 1))
            def _():
                _scatter_finish(*exchange)

    blk = lambda w: pl.BlockSpec((ATTN_BLOCK, w), lambda b, i: (b * nb + i, 0))
    seq_spec = pl.BlockSpec((seq, KV_WIDTH), lambda b, i: (b, 0))
    seq_spec_t = pl.BlockSpec((KV_WIDTH, seq), lambda b, i: (0, b))
    n = batch * seq
    ds_shape = (N_KV_HEADS, 2, PAIRS, LANES)
    out = pl.pallas_call(
        body, name="attn_bwd", grid=(batch, nb),
        in_specs=[blk(D_MODEL), seq_spec, seq_spec, blk(D_MODEL), _full_spec(sink_tab.shape), blk(D_MODEL)]
        + [blk(ANGLE_COLS)] + [_ANY] * ns,
        out_specs=[blk(ATTN_IN), seq_spec_t, seq_spec_t, _full_spec(ds_shape)] + [_ANY] * ns,
        out_shape=[jax.ShapeDtypeStruct((n, ATTN_IN), BF16), jax.ShapeDtypeStruct((KV_WIDTH, n), F32),
                   jax.ShapeDtypeStruct((KV_WIDTH, n), F32), jax.ShapeDtypeStruct(ds_shape, F32)]
        + [jax.ShapeDtypeStruct(a.shape, a.dtype) for a in scatter],
        scratch_shapes=[pltpu.VMEM((N_BIAS_TABLES, GROUP_ROWS, 2 * ATTN_BLOCK), F32)] + (_scatter_sems(ns) if ns else []),
        compiler_params=_cparams(2),
    )(q, k, v, z, sink_tab, dog, angles, *scatter)
    return out[0], out[1], out[2], out[3], out[4:]


def rec_fwd(proj, lb_logits, gnorm_w, batch, seq):
    nblk = seq // REC_BLOCK

    def body(p_ref, lb_ref, gw_ref, og_ref, st_ref, safe_ref, s_scr):
        @pl.when(pl.program_id(1) == 0)
        def _():
            s_scr[...] = jnp.zeros_like(s_scr)

        S = s_scr[...]
        st_ref[0] = S
        qr, fr, v, z = (p_ref[:, part * D_MODEL:(part + 1) * D_MODEL] for part in range(4))
        lf, k = forget_gate(fr, lb_ref[1:2, :] - lb_ref[0:1, :])
        q, b = silu(qr), cumsum_rows(lf)
        safe = jnp.min(_rec_margin(b)) >= -SAFE_RANGE

        gate = gw_ref[...] * silu(z)
        safe_ref[0] = jnp.full((REC_HEADS, LANES), safe.astype(F32))

        def store(o, S_new):
            og_ref[...] = (o * lax.rsqrt(head_sum(o * o) * (1.0 / REC_DIM) + NORM_EPS) * gate).astype(BF16)
            s_scr[...] = S_new

        @pl.when(safe)
        def _():
            store(*_rec_cores_fast(q, k, v, b, S))

        @pl.when(jnp.logical_not(safe))
        def _():
            outs = [_rec_core_slow(*args) for args in zip(*(_heads(t) for t in (q, k, v, b, S)))]
            store(*(jnp.concatenate(parts, axis=1) for parts in zip(*outs)))

    blk = lambda w: pl.BlockSpec((REC_BLOCK, w), lambda b, j: (b * nblk + j, 0))
    st_spec = pl.BlockSpec((1, REC_DIM, D_MODEL), lambda b, j: (b * nblk + j, 0, 0))
    safe_spec = pl.BlockSpec((1, REC_HEADS, LANES), lambda b, j: (b * nblk + j, 0, 0))
    return pl.pallas_call(
        body, name="rec_fwd", grid=(batch, nblk),
        in_specs=[blk(REC_IN), _full_spec((2, D_MODEL)), _full_spec((1, D_MODEL))],
        out_specs=[blk(D_MODEL), st_spec, safe_spec],
        out_shape=[jax.ShapeDtypeStruct((batch * seq, D_MODEL), BF16),
                   jax.ShapeDtypeStruct((batch * nblk, REC_DIM, D_MODEL), F32),
                   jax.ShapeDtypeStruct((batch * nblk, REC_HEADS, LANES), F32)],
        scratch_shapes=[pltpu.VMEM((REC_DIM, D_MODEL), F32)],
        compiler_params=_cparams(2),
    )(proj, lb_logits, jnp.tile(gnorm_w, (1, REC_HEADS)))


def rec_bwd(proj, states, safe, lb_logits, gnorm_w, dog, batch, seq):
    nblk = seq // REC_BLOCK

    def body(p_ref, st_ref, safe_ref, lb_ref, gw_ref, g_ref, dp_ref, dlb_ref, dgw_ref, ds_scr):
        @pl.when((pl.program_id(0) == 0) & (pl.program_id(1) == 0))
        def _():
            dlb_ref[...] = jnp.zeros_like(dlb_ref)
            dgw_ref[...] = jnp.zeros_like(dgw_ref)

        @pl.when(pl.program_id(1) == 0)
        def _():
            ds_scr[...] = jnp.zeros_like(ds_scr)

        def load():
            primals = tuple(p_ref[:, part * D_MODEL:(part + 1) * D_MODEL] for part in range(4)) + (
                st_ref[0], lb_ref[0:1, :], lb_ref[1:2, :], gw_ref[...])
            return primals, (g_ref[...].astype(F32), ds_scr[...])

        def store(dqr, dfr, dv, dz, dS, dl0, dl1, dgw):
            for part, val in enumerate((dqr, dfr, dv, dz)):
                dp_ref[:, part * D_MODEL:(part + 1) * D_MODEL] = val.astype(BF16)
            ds_scr[...] = dS
            dlb_ref[0:1, :] += dl0
            dlb_ref[1:2, :] += dl1
            dgw_ref[...] += functools.reduce(jnp.add, _heads(dgw))

        fast = jnp.max(safe_ref[0]) > 0.5

        @pl.when(fast)
        def _():
            primals, cotangents = load()
            store(*jax.vjp(_rec_block_fast, *primals)[1](cotangents))

        @pl.when(jnp.logical_not(fast))
        def _():
            primals, cotangents = load()
            outs = [jax.vjp(functools.partial(_rec_head, _rec_core_slow), *args)[1](cts)
                    for args, cts in zip(zip(*(_heads(t) for t in primals)), zip(*(_heads(t) for t in cotangents)))]
            store(*(jnp.concatenate(parts, axis=1) for parts in zip(*outs)))

    blk = lambda w: pl.BlockSpec((REC_BLOCK, w), lambda b, j: (b * nblk + nblk - 1 - j, 0))
    st_spec = pl.BlockSpec((1, REC_DIM, D_MODEL), lambda b, j: (b * nblk + nblk - 1 - j, 0, 0))
    safe_spec = pl.BlockSpec((1, REC_HEADS, LANES), lambda b, j: (b * nblk + nblk - 1 - j, 0, 0))
    return pl.pallas_call(
        body, name="rec_bwd", grid=(batch, nblk),
        in_specs=[blk(REC_IN), st_spec, safe_spec, _full_spec((2, D_MODEL)), _full_spec((1, D_MODEL)),
                  blk(D_MODEL)],
        out_specs=[blk(REC_IN), _full_spec((2, D_MODEL)), _full_spec((1, REC_DIM))],
        out_shape=[jax.ShapeDtypeStruct((batch * seq, REC_IN), BF16), jax.ShapeDtypeStruct((2, D_MODEL), F32),
                   jax.ShapeDtypeStruct((1, REC_DIM), F32)],
        scratch_shapes=[pltpu.VMEM((REC_DIM, D_MODEL), F32)],
        compiler_params=_cparams(2),
    )(proj, states, safe, lb_logits, jnp.tile(gnorm_w, (1, REC_HEADS)), dog)


_ANY = pl.BlockSpec(memory_space=pl.ANY)


def _chip_peers():
    x, y, c = lax.axis_index("x"), lax.axis_index("y"), lax.axis_index("c")
    peers = []
    for fx, fy in ((1, 0), (0, 1), (1, 1)):
        px, py = (1 - x if fx else x), (1 - y if fy else y)
        peers.append(((px, py, c), 2 * px + py))
    return 2 * x + y, peers


def _remote(src, dst, send_sem, recv_sem, device):
    return pltpu.make_async_remote_copy(src_ref=src, dst_ref=dst, send_sem=send_sem, recv_sem=recv_sem,
                                        device_id=device, device_id_type=MESH)


N_FLIPS = N_CHIPS - 1


def _scatter_sems(n):
    return [pltpu.SemaphoreType.DMA((n * N_FLIPS,)), pltpu.SemaphoreType.DMA((n * N_FLIPS,)),
            pltpu.SemaphoreType.DMA((n,))]


def _scatter_copies(ins, outs, send_sems, recv_sems, local_sems, starting):
    me, peers = _chip_peers()
    local = [pltpu.make_async_copy(ins[k].at[me], outs[k].at[me], local_sems.at[k]) for k in range(len(ins))]
    sends, arrivals = [], []
    for k in range(len(ins)):
        for j, (device, idx) in enumerate(peers):
            sems = (send_sems.at[k * N_FLIPS + j], recv_sems.at[k * N_FLIPS + j], device)
            sends.append(_remote(ins[k].at[idx], outs[k].at[me], *sems))
            if not starting:
                arrivals.append(_remote(ins[k].at[me], outs[k].at[idx], *sems))
    return local, sends, arrivals


def _scatter_start(*refs):
    local, sends, _ = _scatter_copies(*refs, starting=True)
    for cp in local + sends:
        cp.start()


def _scatter_finish(*refs):
    local, sends, arrivals = _scatter_copies(*refs, starting=False)
    for cp in arrivals:
        cp.wait_recv()
    for cp in sends:
        cp.wait_send()
    for cp in local:
        cp.wait()


def _gather_sems(n):
    return [pltpu.SemaphoreType.DMA((n * N_FLIPS,)) for _ in range(4)] + [pltpu.SemaphoreType.DMA((n,))]


def _gather_copies(ins, outs, send_sems, recv_sems, pass_send_sems, pass_recv_sems, local_sems, starting):
    me, peers = _chip_peers()
    c = lax.axis_index("c")
    sibling = (lax.axis_index("x"), lax.axis_index("y"), 1 - c)
    local = [pltpu.make_async_copy(ins[k], outs[k].at[me], local_sems.at[k]) for k in range(len(ins))]
    sends, arrivals, passes, pass_arrivals = [], [], [], []
    for k in range(len(ins)):
        half = ins[k].shape[0] // 2
        mine, other = pl.ds(c * half, half), pl.ds((1 - c) * half, half)
        for j, (device, idx) in enumerate(peers):
            s = k * N_FLIPS + j
            sends.append(_remote(ins[k].at[mine], outs[k].at[me].at[mine], send_sems.at[s], recv_sems.at[s], device))
            if starting:
                continue
            arrived = outs[k].at[idx].at[mine]
            arrivals.append(_remote(ins[k].at[mine], arrived, send_sems.at[s], recv_sems.at[s], device))
            passes.append(_remote(arrived, arrived, pass_send_sems.at[s], pass_recv_sems.at[s], sibling))
            passed = outs[k].at[idx].at[other]
            pass_arrivals.append(_remote(passed, passed, pass_send_sems.at[s], pass_recv_sems.at[s], sibling))
    return local, sends, arrivals, passes, pass_arrivals


def _gather_start(*refs):
    local, sends, _, _, _ = _gather_copies(*refs, starting=True)
    for cp in local + sends:
        cp.start()


def _gather_finish(*refs):
    local, sends, arrivals, passes, pass_arrivals = _gather_copies(*refs, starting=False)
    for arrival, onward in zip(arrivals, passes):
        arrival.wait_recv()
        onward.start()
    for cp in pass_arrivals:
        cp.wait_recv()
    for cp in sends + passes:
        cp.wait_send()
    for cp in local:
        cp.wait()


def chip_gather(arrays):
    n = len(arrays)

    def body(*refs):
        _gather_start(refs[:n], refs[n:2 * n], *refs[2 * n:])
        _gather_finish(refs[:n], refs[n:2 * n], *refs[2 * n:])

    return pl.pallas_call(
        body, name="chip_gather", in_specs=[_ANY] * n, out_specs=[_ANY] * n,
        out_shape=[jax.ShapeDtypeStruct((N_CHIPS,) + a.shape, a.dtype) for a in arrays],
        scratch_shapes=_gather_sems(n),
    )(*arrays)


def _sibling_sems(n):
    return [pltpu.SemaphoreType.DMA((n,)), pltpu.SemaphoreType.DMA((n,))]


def _sibling_copies(ins, outs, send_sems, recv_sems):
    sibling = (lax.axis_index("x"), lax.axis_index("y"), 1 - lax.axis_index("c"))
    return [_remote(ins[k], outs[k], send_sems.at[k], recv_sems.at[k], sibling) for k in range(len(ins))]


def all_gather_small(vec, parts):
    s, r, c_dim = parts.shape
    rows = min(128, r)

    def body(v_ref, parts_ref, out_ref, mine_ref, other_ref, send_sems, recv_sems, local_sem, sib_send, sib_recv):
        x, y, c = lax.axis_index("x"), lax.axis_index("y"), lax.axis_index("c")
        me = 4 * x + 2 * y + c
        local = pltpu.make_async_copy(v_ref, out_ref.at[me], local_sem)
        local.start()
        sends, recvs = [], []
        for j in range(1, N_DEV):
            px = jnp.where(j & 4, 1 - x, x)
            py = jnp.where(j & 2, 1 - y, y)
            pc = jnp.where(j & 1, 1 - c, c)
            common = dict(send_sem=send_sems.at[j - 1], recv_sem=recv_sems.at[j - 1], device_id=(px, py, pc),
                          device_id_type=MESH)
            sends.append(pltpu.make_async_remote_copy(src_ref=v_ref, dst_ref=out_ref.at[me], **common))
            recvs.append(pltpu.make_async_remote_copy(src_ref=v_ref, dst_ref=out_ref.at[4 * px + 2 * py + pc],
                                                      **common))
        for cp in sends:
            cp.start()

        for i in range(r // rows):
            at = pl.ds(i * rows, rows)
            acc = parts_ref[0, at, :].astype(F32)
            for t in range(1, s):
                acc = acc + parts_ref[t, at, :].astype(F32)
            mine_ref[at, :] = acc
        passed = _sibling_copies([mine_ref], [other_ref], sib_send, sib_recv)[0]
        passed.start()

        for cp in recvs:
            cp.wait_recv()
        for cp in sends:
            cp.wait_send()
        local.wait()
        passed.wait()

    vmem = pl.BlockSpec(memory_space=pltpu.VMEM)
    return pl.pallas_call(
        body, name="all_gather_small", in_specs=[_ANY, vmem], out_specs=[_ANY, vmem, _ANY],
        out_shape=[jax.ShapeDtypeStruct((N_DEV,) + vec.shape, vec.dtype), jax.ShapeDtypeStruct((r, c_dim), F32),
                   jax.ShapeDtypeStruct((r, c_dim), F32)],
        scratch_shapes=[pltpu.SemaphoreType.DMA((N_DEV - 1,)), pltpu.SemaphoreType.DMA((N_DEV - 1,)),
                        pltpu.SemaphoreType.DMA] + _sibling_sems(1),
        compiler_params=_cparams(0),
    )(vec, parts)


def sum_slots(stacked, tm=256):
    s, r, c = stacked.shape
    tm = min(tm, r)

    def body(in_ref, out_ref):
        acc = in_ref[0].astype(F32)
        for t in range(1, s):
            acc = acc + in_ref[t].astype(F32)
        out_ref[...] = acc

    return pl.pallas_call(
        body, name=f"sum_slots_{s}_{r}_{c}", grid=(r // tm,),
        in_specs=[pl.BlockSpec((s, tm, c), lambda i: (0, i, 0))], out_specs=_row_spec(tm, c),
        out_shape=jax.ShapeDtypeStruct((r, c), F32), compiler_params=_cparams(1),
    )(stacked)


def adamw(w, m, v, g_a, g_b=None, tm=512):
    r, c = w.shape
    tm = min(tm, r)
    two = g_b is not None
    slots = g_a.shape[0] if g_a.ndim == 3 else 0

    def body(*refs):
        w_ref, m_ref, v_ref, ga_ref = refs[:4]
        g_ref, d_ref, nm_ref, nv_ref = refs[-4:]
        if slots:
            g = ga_ref[0]
            for t in range(1, slots):
                g = g + ga_ref[t]
        else:
            g = ga_ref[...] + refs[4][...] if two else ga_ref[...]
        nm = ADAM_B1 * m_ref[...] + (1.0 - ADAM_B1) * g
        nv = ADAM_B2 * v_ref[...] + (1.0 - ADAM_B2) * (g * g)
        m_hat = nm / (1.0 - ADAM_B1 ** ADAM_STEP)
        v_hat = nv / (1.0 - ADAM_B2 ** ADAM_STEP)
        g_ref[...] = g
        d_ref[...] = -ADAM_LR * (m_hat / (jnp.sqrt(v_hat) + ADAM_EPS) + ADAM_WD * w_ref[...])
        nm_ref[...] = nm
        nv_ref[...] = nv

    args = [w, m, v, g_a] + ([g_b] if two else [])
    in_specs = [_row_spec(tm, c)] * len(args)
    if slots:
        in_specs[3] = pl.BlockSpec((slots, tm, c), lambda i: (0, i, 0))
    return pl.pallas_call(
        body, name=f"adamw_{r}_{c}", grid=(r // tm,),
        in_specs=in_specs, out_specs=[_row_spec(tm, c)] * 4,
        out_shape=[jax.ShapeDtypeStruct((r, c), F32)] * 4, compiler_params=_cparams(1),
    )(*args)


_SMALL = (("pre_norm_w", (2, D_MODEL)), ("post_norm_w", (2, D_MODEL)), ("attn_b_in", (1, ATTN_IN)),
          ("attn_sinks", (1, N_HEADS)), ("attn_b_out", (1, D_MODEL)), ("rec_lb_logits", (2, D_MODEL)),
          ("rec_gnorm_w", (1, REC_DIM)))
_SMALL_ROWS = 16


def _pack_small(parts, last_row=None):
    rows = []
    for (name, shape) in _SMALL:
        flat = parts[name].reshape(-1)
        pad = -flat.shape[0] % D_MODEL
        rows.append(jnp.pad(flat, (0, pad)).reshape(-1, D_MODEL))
    used = sum(r.shape[0] for r in rows)
    rows.append(jnp.zeros((_SMALL_ROWS - 1 - used, D_MODEL), F32))
    rows.append(jnp.zeros((1, D_MODEL), F32) if last_row is None else last_row)
    return jnp.concatenate(rows, axis=0)


def _unpack_small(packed):
    out, row = {}, 0
    for (name, shape) in _SMALL:
        size = shape[0] * shape[1]
        nrows = -(-size // D_MODEL)
        out[name] = packed[row:row + nrows].reshape(-1)[:size].reshape(shape)
        row += nrows
    return out


_CARRIED = ("rec_w_in", "rec_w_out", "attn_w_out")


_LATE = ("attn_w_out", "rec_w_in", "rec_w_out")


def local_step(x, positions, pre_norm_w, post_norm_w, attn_w_in, attn_b_in, attn_sinks, attn_w_out, attn_b_out,
               rec_w_in, rec_lb_logits, rec_gnorm_w, rec_w_out, loss_target, distributed=False):
    batch, seq, _ = x.shape
    n = batch * seq
    x0 = x.reshape(n, D_MODEL)
    angles = _rope_angles(positions)
    pre0, pre1 = pre_norm_w[0:1], pre_norm_w[1:2]
    post0, post1 = post_norm_w[0:1], post_norm_w[1:2]

    late = (attn_w_out, rec_w_in, rec_w_out) if distributed else ()
    (h0, q, k, v, z), late = attn_in_proj(x0, pre0, attn_w_in, attn_b_in, angles, to_bf16=late)
    sink_tab = _sink_table(attn_sinks)
    og0, gathered = attn_fwd(q, k, v, z, sink_tab, batch, seq, gather=late)
    if distributed:
        attn_w_out, rec_w_in, rec_w_out = (g if name == "rec_w_in" else _whole_from_shards(name, g)
                                           for name, g in zip(_LATE, gathered))
    y0, x1 = out_proj(og0, attn_w_out, attn_b_out, x0, post0)

    h1, proj1 = rec_in_proj(x1, pre1, rec_w_in)
    og1, states, safe = rec_fwd(proj1, rec_lb_logits, rec_gnorm_w, batch, seq)
    dx2, loss_vec, dog1, d_rec_w_out, d_post1, d_rec_w_out_bf16 = out_proj_loss_bwd(
        og1, rec_w_out, x1, post1, loss_target.reshape(n, D_MODEL))
    dproj1, d_lb, d_gnorm = rec_bwd(proj1, states, safe, rec_lb_logits, rec_gnorm_w, dog1, batch, seq)
    dx1, d_pre1, _, _ = in_proj_bwd_x(dproj1, rec_w_in, x1, pre1, dx2)
    d_rec_w_in, _ = in_proj_bwd_w(h1, dproj1, as_shards=distributed)

    dog0, d_attn_w_out, d_attn_b_out, d_post0, d_attn_w_out_bf16 = out_proj_bwd(dx1, y0, og0, attn_w_out, post0)
    ready = dict(rec_w_in=d_rec_w_in, rec_w_out=_shards_from_whole("rec_w_out", d_rec_w_out_bf16),
                 attn_w_out=_shards_from_whole("attn_w_out", d_attn_w_out_bf16))
    outgoing = [ready[name] for name in _CARRIED] if distributed else []
    dproj0, dk, dv, d_sink_tab, arrived = attn_bwd(q, k, v, z, sink_tab, dog0, angles, batch, seq, scatter=outgoing)
    d_sinks = jnp.transpose(jnp.sum(d_sink_tab, axis=-1), (0, 2, 1)).reshape(1, N_HEADS)
    d_attn_w_in, d_attn_b_in, dproj0, *summed = in_proj_bwd_w(h0, dproj0, kv=(dk, dv, angles),
                                                              slot_sums=list(arrived))
    last = [_shards_from_whole("attn_w_in", d_attn_w_in).astype(BF16)] if distributed else []
    dx0, d_pre0, arrived_last, theirs = in_proj_bwd_x(dproj0, attn_w_in, x0, pre0, dx1, scatter=last,
                                                      to_sibling=summed)

    grads = dict(
        pre_norm_w=jnp.concatenate([d_pre0, d_pre1], axis=0), post_norm_w=jnp.concatenate([d_post0, d_post1], axis=0),
        attn_w_in=d_attn_w_in, attn_b_in=d_attn_b_in, attn_sinks=d_sinks, attn_w_out=d_attn_w_out,
        attn_b_out=d_attn_b_out, rec_w_in=d_rec_w_in, rec_lb_logits=d_lb, rec_gnorm_w=d_gnorm,
        rec_w_out=d_rec_w_out)
    exchanged = dict(zip(_CARRIED, zip(summed, theirs)))
    exchanged.update(zip(("attn_w_in",), arrived_last))
    return loss_vec, dx0.reshape(batch, seq, D_MODEL), grads, exchanged


_BIG = ("attn_w_in", "attn_w_out", "rec_w_in", "rec_w_out")
_COLUMN_SHARDED = ("attn_w_in", "rec_w_in")
_ORDER = ("pre_norm_w", "post_norm_w", "attn_w_in", "attn_b_in", "attn_sinks", "attn_w_out", "attn_b_out",
          "rec_w_in", "rec_lb_logits", "rec_gnorm_w", "rec_w_out")


def _whole_from_shards(name, stacked):
    if name in _COLUMN_SHARDED:
        return jnp.transpose(stacked, (1, 0, 2)).reshape(stacked.shape[1], -1)
    return stacked.reshape(-1, stacked.shape[2])


def _shards_from_whole(name, whole):
    if name in _COLUMN_SHARDED:
        return jnp.transpose(whole.reshape(whole.shape[0], N_CHIPS, -1), (1, 0, 2))
    return whole.reshape(N_CHIPS, -1, whole.shape[1])


def kernel(x, positions, pre_norm_w, post_norm_w, attn_w_in, attn_b_in, attn_sinks, attn_w_out, attn_b_out, rec_w_in, rec_lb_logits, rec_gnorm_w, rec_w_out, loss_target, m_pre_norm_w, m_post_norm_w, m_attn_w_in, m_attn_b_in, m_attn_sinks, m_attn_w_out, m_attn_b_out, m_rec_w_in, m_rec_lb_logits, m_rec_gnorm_w, m_rec_w_out, v_pre_norm_w, v_post_norm_w, v_attn_w_in, v_attn_b_in, v_attn_sinks, v_attn_w_out, v_attn_b_out, v_rec_w_in, v_rec_lb_logits, v_rec_gnorm_w, v_rec_w_out):
    w = dict(pre_norm_w=pre_norm_w, post_norm_w=post_norm_w, attn_w_in=attn_w_in, attn_b_in=attn_b_in,
             attn_sinks=attn_sinks, attn_w_out=attn_w_out, attn_b_out=attn_b_out, rec_w_in=rec_w_in,
             rec_lb_logits=rec_lb_logits, rec_gnorm_w=rec_gnorm_w, rec_w_out=rec_w_out)
    m = dict(pre_norm_w=m_pre_norm_w, post_norm_w=m_post_norm_w, attn_w_in=m_attn_w_in, attn_b_in=m_attn_b_in,
             attn_sinks=m_attn_sinks, attn_w_out=m_attn_w_out, attn_b_out=m_attn_b_out, rec_w_in=m_rec_w_in,
             rec_lb_logits=m_rec_lb_logits, rec_gnorm_w=m_rec_gnorm_w, rec_w_out=m_rec_w_out)
    v = dict(pre_norm_w=v_pre_norm_w, post_norm_w=v_post_norm_w, attn_w_in=v_attn_w_in, attn_b_in=v_attn_b_in,
             attn_sinks=v_attn_sinks, attn_w_out=v_attn_w_out, attn_b_out=v_attn_b_out, rec_w_in=v_rec_w_in,
             rec_lb_logits=v_rec_lb_logits, rec_gnorm_w=v_rec_gnorm_w, rec_w_out=v_rec_w_out)

    shards = {name: w[name][0] for name in _BIG}
    attn_w_in_whole = _whole_from_shards("attn_w_in", chip_gather([shards["attn_w_in"].astype(BF16)])[0])

    loss_vec, grad_x, grads, exchanged = local_step(
        x, positions, pre_norm_w, post_norm_w, attn_w_in_whole, attn_b_in, attn_sinks, shards["attn_w_out"],
        attn_b_out, shards["rec_w_in"], rec_lb_logits, rec_gnorm_w, shards["rec_w_out"], loss_target,
        distributed=True)

    small_parts, mine, other = all_gather_small(_pack_small(grads, last_row=loss_vec), exchanged["attn_w_in"])
    exchanged["attn_w_in"] = (mine, other)
    out_g, out_d, out_m, out_v = {}, {}, {}, {}
    for name in _BIG:
        mine, other = exchanged[name]
        g, d, nm, nv = adamw(shards[name], m[name][0], v[name][0], mine, other)
        out_g[name], out_d[name], out_m[name], out_v[name] = g[None], d[None], nm[None], nv[None]

    packed = adamw(_pack_small(w), _pack_small(m), _pack_small(v), small_parts)
    loss = jnp.sum(packed[0][_SMALL_ROWS - 1]) * (0.5 / D_MODEL)
    for dst, val in zip((out_g, out_d, out_m, out_v), packed):
        dst.update(_unpack_small(val))

    return (loss, grad_x, *[out_g[n] for n in _ORDER], *[out_d[n] for n in _ORDER],
            *[out_m[n] for n in _ORDER], *[out_v[n] for n in _ORDER])
```

```python
import functools

import jax
import jax.numpy as jnp
from jax import lax
from jax.experimental import pallas as pl
from jax.experimental.pallas import tpu as pltpu

F32 = jnp.float32
BF16 = jnp.bfloat16
MESH = pl.DeviceIdType.MESH

D_MODEL = 1024
HEAD_DIM = 64
N_HEADS = 16
N_KV_HEADS = 2
GROUP = N_HEADS // N_KV_HEADS
KV_WIDTH = N_KV_HEADS * HEAD_DIM
ATTN_IN = 2 * D_MODEL + 2 * KV_WIDTH
ATTN_BLOCK = 128
ROPE_THETA = 500000.0
ROPE_DIM = HEAD_DIM // 4
REC_HEADS = 8
REC_DIM = 128
REC_IN = 4 * D_MODEL
REC_BLOCK = 128
DIAG = 8
NORM_EPS = 1e-6
N_CHIPS = 4
N_DEV = 8
LANES = 128

ADAM_LR = 0.001
ADAM_B1 = 0.9
ADAM_B2 = 0.999
ADAM_EPS = 1e-08
ADAM_WD = 0.01
ADAM_STEP = 10

VMEM_LIMIT = 56 * 1024 * 1024


def _cparams(n_axes):
    return pltpu.CompilerParams(dimension_semantics=("arbitrary",) * n_axes, vmem_limit_bytes=VMEM_LIMIT)


def _dot(a, b, contract):
    return lax.dot_general(a.astype(BF16), b.astype(BF16), (contract, ((), ())), preferred_element_type=F32)


_NN = ((1,), (0,))
_NT = ((1,), (1,))
_TN = ((0,), (0,))


@jax.custom_vjp
def mm_nn(a, b):
    return _dot(a, b, _NN)


mm_nn.defvjp(lambda a, b: (_dot(a, b, _NN), (a, b)),
             lambda res, g: (_dot(g, res[1], _NT), _dot(res[0], g, _TN)))


@jax.custom_vjp
def mm_nt(a, b):
    return _dot(a, b, _NT)


mm_nt.defvjp(lambda a, b: (_dot(a, b, _NT), (a, b)),
             lambda res, g: (_dot(g, res[1], _NN), _dot(g, res[0], _TN)))


@jax.custom_vjp
def mm_tn(a, b):
    return _dot(a, b, _TN)


mm_tn.defvjp(lambda a, b: (_dot(a, b, _TN), (a, b)),
             lambda res, g: (_dot(res[1], g, _NT), _dot(res[0], g, _NN)))


def _tri_dot(x, lower):
    n = x.shape[0]
    r = lax.broadcasted_iota(jnp.int32, (n, n), 0)
    c = lax.broadcasted_iota(jnp.int32, (n, n), 1)
    tri = ((c <= r) if lower else (c >= r)).astype(BF16)
    hi = x.astype(BF16)
    rest = x - hi.astype(F32)
    mid = rest.astype(BF16)
    lo = (rest - mid.astype(F32)).astype(BF16)
    dot = lambda p: lax.dot_general(tri, p, (_NN, ((), ())), preferred_element_type=F32)
    return (dot(lo) + dot(mid)) + dot(hi)


@jax.custom_vjp
def cumsum_rows(x):
    return _tri_dot(x, True)


cumsum_rows.defvjp(lambda x: (cumsum_rows(x), None), lambda _, g: (_tri_dot(g, False),))


@functools.partial(jax.custom_vjp, nondiff_argnums=(1,))
def roll_sub(x, d):
    return pltpu.roll(x, d, 1) if d else x


roll_sub.defvjp(lambda x, d: (roll_sub(x, d), None),
                lambda d, _, g: (roll_sub(g, (DIAG - d) % DIAG),))


def sigmoid(x):
    return 1.0 / (1.0 + jnp.exp(-x))


@jax.custom_vjp
def silu(x):
    return x * sigmoid(x)


def _silu_fwd(x):
    s = sigmoid(x)
    return x * s, (x, s)


silu.defvjp(_silu_fwd, lambda res, g: (g * (res[1] * (1.0 + res[0] * (1.0 - res[1]))),))


F32_TINY = 1.17549435e-38


def sigmoid_pair(x):
    e = jnp.exp(-jnp.abs(x))
    r = 1.0 / (1.0 + e)
    er = e * r
    pos = x >= 0.0
    return jnp.where(pos, r, er), jnp.where(pos, er, r)


def _forget_fwd(x, a):
    lb, one_m_lb = sigmoid_pair(a)
    sp, sn = sigmoid_pair(x)
    f = lb + one_m_lb * sp
    k = one_m_lb * sn
    return (jnp.log(jnp.maximum(f, F32_TINY)), k), (sp, sn, f, k, lb, one_m_lb)


def _forget_bwd(res, g):
    sp, sn, f, k, lb, one_m_lb = res
    g_lf, g_k = g
    t = jnp.where(f >= F32_TINY, g_lf / jnp.maximum(f, F32_TINY), 0.0) - g_k
    return (k * sp) * t, jnp.sum(sn * t, axis=0, keepdims=True) * (lb * one_m_lb)


@jax.custom_vjp
def forget_gate(x, a):
    return _forget_fwd(x, a)[0]


forget_gate.defvjp(_forget_fwd, _forget_bwd)


@jax.custom_vjp
def decayed(x, e):
    return (x * jnp.exp(e)).astype(BF16).astype(F32)


def _decayed_fwd(x, e):
    y = decayed(x, e)
    return y, (y, e)


decayed.defvjp(_decayed_fwd, lambda res, g: (g * jnp.exp(res[1]), g * res[0]))


def _row(x, r):
    shape = x.shape

    @jax.custom_vjp
    def take(x):
        return x[r:r + 1, :]

    take.defvjp(lambda x: (x[r:r + 1, :], None),
                lambda _, g: (jnp.where(lax.broadcasted_iota(jnp.int32, shape, 0) == r, g, 0.0),))
    return take(x)


def _rms(x):
    return lax.rsqrt(jnp.mean(x * x, axis=-1, keepdims=True) + NORM_EPS)


def _attn_group(qs, k_a, v_a, k_b, v_b, zs, sink_a, sink_b, bias, at_sink=None):
    def half(kh, vh, sink):
        s = mm_nn(qs, kh) + bias
        if at_sink is None:
            m = jnp.maximum(jnp.max(s, axis=-1, keepdims=True), jnp.max(sink, axis=-1, keepdims=True))
            p = jnp.exp(s - lax.stop_gradient(m))
            own = jnp.sum(jnp.exp(sink - lax.stop_gradient(m)), axis=-1, keepdims=True) * (1.0 / LANES)
            return mm_nt(p * (1.0 / (jnp.sum(p, axis=-1, keepdims=True) + own)), vh)
        s = jnp.where(at_sink, jnp.concatenate([sink, sink], axis=1), s)
        p = jnp.exp(s - jnp.max(s, axis=-1, keepdims=True))
        return mm_nt(jnp.where(at_sink, 0.0, p), vh) * (1.0 / jnp.sum(p, axis=-1, keepdims=True))

    return (half(k_a, v_a, sink_a) + half(k_b, v_b, sink_b)) * silu(zs)


SAFE_RANGE = 80.0


def _rec_front(qr, fr, l0, l1):
    lf, k = forget_gate(fr, l1 - l0)
    return silu(qr), k, lf


def _rec_tail(o, z, gw):
    return o * _rms(o) * gw * silu(z)


def _rec_margin(b):
    R = b.shape[0]
    mid, last = _row(b, R // 2 - 1), _row(b, R - 1)
    return jnp.minimum(mid, last - mid)


def _heads(x):
    w = x.shape[1] // REC_HEADS
    return [x[:, h * w:(h + 1) * w] for h in range(REC_HEADS)]


def _hdot(a, b, contract):
    return jnp.concatenate([_dot(ah, bh, contract) for ah, bh in zip(_heads(a), _heads(b))], axis=1)


@jax.custom_vjp
def hmm_nn(a, b):
    return _hdot(a, b, _NN)


hmm_nn.defvjp(lambda a, b: (_hdot(a, b, _NN), (a, b)),
              lambda res, g: (_hdot(g, res[1], _NT), _hdot(res[0], g, _TN)))


@jax.custom_vjp
def hmm_nt(a, b):
    return _hdot(a, b, _NT)


hmm_nt.defvjp(lambda a, b: (_hdot(a, b, _NT), (a, b)),
              lambda res, g: (_hdot(g, res[1], _NN), _hdot(g, res[0], _TN)))


@jax.custom_vjp
def hmm_tn(a, b):
    return _hdot(a, b, _TN)


hmm_tn.defvjp(lambda a, b: (_hdot(a, b, _TN), (a, b)),
              lambda res, g: (_hdot(res[1], g, _NT), _hdot(res[0], g, _NN)))


def _head_sums(x):
    return jnp.concatenate([jnp.broadcast_to(jnp.sum(xh, axis=-1, keepdims=True), xh.shape) for xh in _heads(x)],
                           axis=1)


@jax.custom_vjp
def head_sum(x):
    return _head_sums(x)


head_sum.defvjp(lambda x: (_head_sums(x), None), lambda _, g: (_head_sums(g),))


def _rec_cores_fast(q, k, v, b, S):
    R = q.shape[0]
    ri = lax.broadcasted_iota(jnp.int32, (R, REC_HEADS * R), 0)
    ci = lax.broadcasted_iota(jnp.int32, (R, REC_HEADS * R), 1) % R
    d = b - _row(b, R // 2 - 1)
    sc = jnp.where(ci < ri, hmm_nt(decayed(q, d), decayed(k, -d)), 0.0)
    o = hmm_nt(q * jnp.exp(b), S) + hmm_nn(sc, v) + head_sum(q * k) * v
    b_last = _row(b, R - 1)
    return o, S * jnp.exp(b_last) + hmm_tn(v, k * jnp.exp(b_last - b))


def _rec_tails(o, z, gw):
    return o * lax.rsqrt(head_sum(o * o) * (1.0 / REC_DIM) + NORM_EPS) * gw * silu(z)


def _rec_block_fast(qr, fr, v, z, S, l0, l1, gw):
    lf, k = forget_gate(fr, l1 - l0)
    o, S_new = _rec_cores_fast(silu(qr), k, v, cumsum_rows(lf), S)
    return _rec_tails(o, z, gw), S_new


def _rec_core_slow(q, k, v, b, S):
    R = q.shape[0]
    rows = lax.broadcasted_iota(jnp.int32, (R, REC_DIM), 0)

    o = mm_nt(q * jnp.exp(jnp.minimum(b, 0.0)), S)

    ri = lax.broadcasted_iota(jnp.int32, (R, R), 0)
    ci = lax.broadcasted_iota(jnp.int32, (R, R), 1)
    sc = jnp.zeros((R, R), F32)
    w = R
    while w > DIAG:
        h = w // 2
        b3 = b.reshape(R // w, w, REC_DIM)
        rin = lax.broadcasted_iota(jnp.int32, (R // w, w, REC_DIM), 1)
        mid = jnp.sum(jnp.where(rin == h - 1, b3, 0.0), axis=1, keepdims=True)
        fac = jnp.exp(jnp.minimum(jnp.where(rin >= h, b3 - mid, mid - b3), 0.0)).reshape(R, REC_DIM)
        upper = (rows % w) >= h
        s_w = mm_nt(jnp.where(upper, q * fac, 0.0), jnp.where(upper, 0.0, k * fac))
        sc = sc + jnp.where((ri // w) == (ci // w), s_w, 0.0)
        w = h
    o = o + mm_nn(sc, v)

    g = R // DIAG
    q3, k3, v3, b3 = (t.reshape(g, DIAG, REC_DIM) for t in (q, k, v, b))
    rin = lax.broadcasted_iota(jnp.int32, (g, DIAG, 1), 1)
    od = jnp.zeros((g, DIAG, REC_DIM), F32)
    for d in range(DIAG):
        e = jnp.exp(jnp.minimum(b3 - roll_sub(b3, d), 0.0))
        sd = jnp.sum(q3 * roll_sub(k3, d) * e, axis=-1, keepdims=True)
        od = od + jnp.where(rin >= d, sd, 0.0) * roll_sub(v3, d)
    o = o + od.reshape(R, REC_DIM)

    b_last = _row(b, R - 1)
    return o, S * jnp.exp(jnp.minimum(b_last, 0.0)) + mm_tn(v, k * jnp.exp(jnp.minimum(b_last - b, 0.0)))


def _rec_head(core, qr, fr, v, z, S, l0, l1, gw):
    q, k, lf = _rec_front(qr, fr, l0, l1)
    o, S_new = core(q, k, v, cumsum_rows(lf), S)
    return _rec_tail(o, z, gw), S_new


ANGLE_COLS = 3 * ROPE_DIM


def _rope_angles(positions):
    half = ROPE_DIM // 2
    inv_freq = ROPE_THETA ** (-(jnp.arange(half, dtype=F32) * 2.0 / ROPE_DIM))
    ang = positions.astype(F32).reshape(-1, 1) * inv_freq
    cs = jnp.concatenate([jnp.cos(ang), jnp.sin(ang)], axis=-1)
    hi = cs.astype(BF16)
    rest = cs - hi.astype(F32)
    mid = rest.astype(BF16)
    return jnp.concatenate([hi, mid, (rest - mid.astype(F32)).astype(BF16)], axis=-1)


def _rope_tables(pieces):
    half = ROPE_DIM // 2
    r = lax.broadcasted_iota(jnp.int32, (ANGLE_COLS, 3 * LANES), 0) % ROPE_DIM
    c = lax.broadcasted_iota(jnp.int32, (ANGLE_COLS, 3 * LANES), 1)
    table, j = c // LANES, c % HEAD_DIM
    angle, low = j % half, j < half
    plus = ((table == 0) & (j < ROPE_DIM) & (r == angle)) | ((table == 1) & (j >= half) & (j < ROPE_DIM)
                                                                & (r == half + angle))
    minus = (table == 2) & low & (r == half + angle)
    pick = jnp.where(plus, 1.0, jnp.where(minus, -1.0, 0.0)).astype(BF16)
    out = jnp.dot(pieces, pick, preferred_element_type=F32)
    lane = lax.broadcasted_iota(jnp.int32, (1, LANES), 1) % HEAD_DIM
    return out[:, :LANES] + jnp.where(lane < ROPE_DIM, 0.0, 1.0), out[:, LANES:2 * LANES], out[:, 2 * LANES:]


def _rope(x, cos_t, sin_a, sin_b):
    half = ROPE_DIM // 2
    return x * cos_t + pltpu.roll(x, half, 1) * sin_a + pltpu.roll(x, LANES - half, 1) * sin_b


def _rope_transposed(g, cos_t, sin_a, sin_b):
    half = ROPE_DIM // 2
    return g * cos_t + pltpu.roll(g * sin_a, LANES - half, 1) + pltpu.roll(g * sin_b, half, 1)


def _row_spec(tm, width):
    return pl.BlockSpec((tm, width), lambda i: (i, 0))


def _weight_spec(shape):
    return pl.BlockSpec(shape, lambda *_: (0,) * len(shape), pipeline_mode=pl.Buffered(1))


def _full_spec(shape):
    return pl.BlockSpec(shape, lambda *_: (0,) * len(shape))


def attn_in_proj(x, w_pre, w_in, b_in, angles, tm=1024, to_bf16=()):
    n = x.shape[0]
    tm = min(tm, n)
    nc = len(to_bf16)

    def body(*refs):
        x_ref, wp_ref, w_ref, b_ref, cs_ref = refs[:5]
        h_ref, q_ref, k_ref, v_ref, z_ref = refs[5 + nc:10 + nc]

        @pl.when(pl.program_id(0) == 0)
        def _():
            for src, dst in zip(refs[5:5 + nc], refs[10 + nc:]):
                dst[...] = src[...].astype(BF16)

        xv = x_ref[...]
        h = (xv * _rms(xv) * wp_ref[...]).astype(BF16)
        h_ref[...] = h
        proj = jnp.dot(h, w_ref[...], preferred_element_type=F32) + b_ref[...]
        tabs = _rope_tables(cs_ref[...])
        for s in range(D_MODEL // LANES):
            sl = slice(s * LANES, (s + 1) * LANES)
            q_ref[:, sl] = _rope(proj[:, sl] * (HEAD_DIM ** -0.5), *tabs).astype(BF16)
        k_ref[...] = _rope(proj[:, D_MODEL:D_MODEL + KV_WIDTH], *tabs).astype(BF16)
        v_ref[...] = proj[:, D_MODEL + KV_WIDTH:D_MODEL + 2 * KV_WIDTH].astype(BF16)
        z_ref[...] = proj[:, D_MODEL + 2 * KV_WIDTH:]

    out = pl.pallas_call(
        body, name="attn_in_proj", grid=(n // tm,),
        in_specs=[_row_spec(tm, D_MODEL), _full_spec((1, D_MODEL)), _weight_spec((D_MODEL, ATTN_IN)),
                  _full_spec((1, ATTN_IN)), _row_spec(tm, ANGLE_COLS)] + [_weight_spec(a.shape) for a in to_bf16],
        out_specs=[_row_spec(tm, D_MODEL), _row_spec(tm, D_MODEL), _row_spec(tm, KV_WIDTH),
                   _row_spec(tm, KV_WIDTH), _row_spec(tm, D_MODEL)] + [_full_spec(a.shape) for a in to_bf16],
        out_shape=[jax.ShapeDtypeStruct((n, D_MODEL), BF16), jax.ShapeDtypeStruct((n, D_MODEL), BF16),
                   jax.ShapeDtypeStruct((n, KV_WIDTH), BF16), jax.ShapeDtypeStruct((n, KV_WIDTH), BF16),
                   jax.ShapeDtypeStruct((n, D_MODEL), F32)] + [jax.ShapeDtypeStruct(a.shape, BF16) for a in to_bf16],
        compiler_params=_cparams(1),
    )(x, w_pre, w_in, b_in, angles, *to_bf16)
    return out[:5], out[5:]


def _column_blocks(w):
    if len(w.shape) == 2:
        return [slice(0, w.shape[1])], lambda ref, s: ref[...]
    width = w.shape[2]
    return [slice(s * width, (s + 1) * width) for s in range(w.shape[0])], lambda ref, s: ref[s]


def rec_in_proj(x, w_pre, w_in, tm=1024):
    n = x.shape[0]
    tm = min(tm, n)
    columns, block = _column_blocks(w_in)

    def body(x_ref, wp_ref, w_ref, h_ref, p_ref):
        xv = x_ref[...]
        h = (xv * _rms(xv) * wp_ref[...]).astype(BF16)
        h_ref[...] = h
        for s, cols in enumerate(columns):
            p_ref[:, cols] = jnp.dot(h, block(w_ref, s), preferred_element_type=F32)

    return pl.pallas_call(
        body, name="rec_in_proj", grid=(n // tm,),
        in_specs=[_row_spec(tm, D_MODEL), _full_spec((1, D_MODEL)), _weight_spec(w_in.shape)],
        out_specs=[_row_spec(tm, D_MODEL), _row_spec(tm, REC_IN)],
        out_shape=[jax.ShapeDtypeStruct((n, D_MODEL), BF16), jax.ShapeDtypeStruct((n, REC_IN), F32)],
        compiler_params=_cparams(1),
    )(x, w_pre, w_in)


def out_proj(og, w_out, b_out, x_res, w_post, tm=1024):
    n = og.shape[0]
    tm = min(tm, n)

    def body(og_ref, w_ref, b_ref, x_ref, wp_ref, y_ref, xo_ref):
        y = jnp.dot(og_ref[...], w_ref[...], preferred_element_type=F32) + b_ref[...]
        y_ref[...] = y.astype(BF16)
        xo_ref[...] = x_ref[...] + y * _rms(y) * wp_ref[...]

    return pl.pallas_call(
        body, name="out_proj", grid=(n // tm,),
        in_specs=[_row_spec(tm, D_MODEL), _weight_spec((D_MODEL, D_MODEL)), _full_spec((1, D_MODEL)),
                  _row_spec(tm, D_MODEL), _full_spec((1, D_MODEL))],
        out_specs=[_row_spec(tm, D_MODEL), _row_spec(tm, D_MODEL)],
        out_shape=[jax.ShapeDtypeStruct((n, D_MODEL), BF16), jax.ShapeDtypeStruct((n, D_MODEL), F32)],
        compiler_params=_cparams(1),
    )(og, w_out, b_out, x_res, w_post)


def _post_norm_bwd(g, y, w_post):
    rstd = _rms(y)
    yn = y * rstd
    gw = g * w_post
    return rstd * (gw - yn * jnp.mean(gw * yn, axis=-1, keepdims=True)), jnp.sum(g * yn, axis=0, keepdims=True)


def out_proj_loss_bwd(og, w_out, x_res, w_post, target, tm=1024):
    n = og.shape[0]
    tm = min(tm, n)
    steps = n // tm

    def body(og_ref, w_ref, x_ref, wp_ref, t_ref, dx_ref, l_ref, dog_ref, dw_ref, dwp_ref, dwb_ref):
        @pl.when(pl.program_id(0) == 0)
        def _():
            l_ref[...] = jnp.zeros_like(l_ref)
            dw_ref[...] = jnp.zeros_like(dw_ref)
            dwp_ref[...] = jnp.zeros_like(dwp_ref)

        og_tile = og_ref[...]
        y = jnp.dot(og_tile, w_ref[...], preferred_element_type=F32)
        err = x_ref[...] + y * _rms(y) * wp_ref[...] - t_ref[...]
        g = err * (1.0 / D_MODEL)
        dx_ref[...] = g
        l_ref[...] += jnp.sum(err * err, axis=0, keepdims=True)
        dy, dwp = _post_norm_bwd(g, y, wp_ref[...])
        dwp_ref[...] += dwp
        dyb = dy.astype(BF16)
        dog_ref[...] = _dot(dyb, w_ref[...], _NT).astype(BF16)
        dw_ref[...] += _dot(og_tile, dyb, _TN)

        @pl.when(pl.program_id(0) == steps - 1)
        def _():
            dwb_ref[...] = dw_ref[...].astype(BF16)

    return pl.pallas_call(
        body, name="out_proj_loss_bwd", grid=(steps,),
        in_specs=[_row_spec(tm, D_MODEL), _weight_spec((D_MODEL, D_MODEL)), _row_spec(tm, D_MODEL),
                  _full_spec((1, D_MODEL)), _row_spec(tm, D_MODEL)],
        out_specs=[_row_spec(tm, D_MODEL), _full_spec((1, D_MODEL)), _row_spec(tm, D_MODEL),
                   _full_spec((D_MODEL, D_MODEL)), _full_spec((1, D_MODEL)), _full_spec((D_MODEL, D_MODEL))],
        out_shape=[jax.ShapeDtypeStruct((n, D_MODEL), F32), jax.ShapeDtypeStruct((1, D_MODEL), F32),
                   jax.ShapeDtypeStruct((n, D_MODEL), BF16), jax.ShapeDtypeStruct((D_MODEL, D_MODEL), F32),
                   jax.ShapeDtypeStruct((1, D_MODEL), F32), jax.ShapeDtypeStruct((D_MODEL, D_MODEL), BF16)],
        compiler_params=_cparams(1),
    )(og, w_out, x_res, w_post, target)


def out_proj_bwd(dxo, y, og, w_out, w_post, tm=1024):
    n = og.shape[0]
    tm = min(tm, n)
    steps = n // tm

    def body(g_ref, y_ref, og_ref, w_ref, wp_ref, dog_ref, dw_ref, db_ref, dwp_ref, dwb_ref):
        @pl.when(pl.program_id(0) == 0)
        def _():
            dw_ref[...] = jnp.zeros_like(dw_ref)
            db_ref[...] = jnp.zeros_like(db_ref)
            dwp_ref[...] = jnp.zeros_like(dwp_ref)

        dy, dwp = _post_norm_bwd(g_ref[...], y_ref[...].astype(F32), wp_ref[...])
        dwp_ref[...] += dwp
        db_ref[...] += jnp.sum(dy, axis=0, keepdims=True)
        dyb = dy.astype(BF16)
        dog_ref[...] = _dot(dyb, w_ref[...], _NT).astype(BF16)
        dw_ref[...] += _dot(og_ref[...], dyb, _TN)

        @pl.when(pl.program_id(0) == steps - 1)
        def _():
            dwb_ref[...] = dw_ref[...].astype(BF16)

    return pl.pallas_call(
        body, name="out_proj_bwd", grid=(steps,),
        in_specs=[_row_spec(tm, D_MODEL), _row_spec(tm, D_MODEL), _row_spec(tm, D_MODEL),
                  _weight_spec((D_MODEL, D_MODEL)), _full_spec((1, D_MODEL))],
        out_specs=[_row_spec(tm, D_MODEL), _full_spec((D_MODEL, D_MODEL)), _full_spec((1, D_MODEL)),
                   _full_spec((1, D_MODEL)), _full_spec((D_MODEL, D_MODEL))],
        out_shape=[jax.ShapeDtypeStruct((n, D_MODEL), BF16), jax.ShapeDtypeStruct((D_MODEL, D_MODEL), F32),
                   jax.ShapeDtypeStruct((1, D_MODEL), F32), jax.ShapeDtypeStruct((1, D_MODEL), F32),
                   jax.ShapeDtypeStruct((D_MODEL, D_MODEL), BF16)],
        compiler_params=_cparams(1),
    )(dxo, y, og, w_out, w_post)


def _slot_sum_specs(slot_sums, steps):
    return ([pl.BlockSpec((a.shape[0], a.shape[1] // steps, a.shape[2]), lambda i: (0, i, 0)) for a in slot_sums],
            [_row_spec(a.shape[1] // steps, a.shape[2]) for a in slot_sums],
            [jax.ShapeDtypeStruct(a.shape[1:], F32) for a in slot_sums])


def _sum_slots_into(slot_refs, sum_refs):
    for slots_ref, sum_ref in zip(slot_refs, sum_refs):
        sum_ref[...] = functools.reduce(jnp.add, [slots_ref[t].astype(F32) for t in range(slots_ref.shape[0])])


def in_proj_bwd_x(dproj, w_in, x, w_pre, dxo, tm=1024, scatter=(), to_sibling=()):
    n, p = dproj.shape
    tm = min(tm, n)
    steps = n // tm
    ns, nsib = len(scatter), len(to_sibling)
    columns, block = _column_blocks(w_in)

    def body(*refs):
        dp_ref, w_ref, x_ref, wp_ref, g_ref = refs[:5]
        outs = 5 + ns + nsib
        dx_ref, dwp_ref = refs[outs:outs + 2]
        sems = refs[outs + 2 + ns + nsib:]
        exchange = (refs[5:5 + ns], refs[outs + 2:outs + 2 + ns]) + tuple(sems[:3])
        sibling = (refs[5 + ns:outs], refs[outs + 2 + ns:outs + 2 + ns + nsib]) + tuple(sems[3:])

        @pl.when(pl.program_id(0) == 0)
        def _():
            dwp_ref[...] = jnp.zeros_like(dwp_ref)
            if ns:
                _scatter_start(*exchange)
            for cp in _sibling_copies(*sibling) if nsib else ():
                cp.start()

        dh = functools.reduce(jnp.add, [_dot(dp_ref[:, cols], block(w_ref, s), _NT)
                                        for s, cols in enumerate(columns)])
        xv = x_ref[...]
        rstd = _rms(xv)
        xn = xv * rstd
        gw = dh * wp_ref[...]
        dwp_ref[...] += jnp.sum(dh * xn, axis=0, keepdims=True)
        dx_ref[...] = rstd * (gw - xn * jnp.mean(gw * xn, axis=-1, keepdims=True)) + g_ref[...]

        if ns or nsib:
            @pl.when(pl.program_id(0) == steps - 1)
            def _():
                if ns:
                    _scatter_finish(*exchange)
                for cp in _sibling_copies(*sibling) if nsib else ():
                    cp.wait()

    out = pl.pallas_call(
        body, name=f"in_proj_bwd_x_{p}", grid=(steps,),
        in_specs=[_row_spec(tm, p), _weight_spec(w_in.shape), _row_spec(tm, D_MODEL), _full_spec((1, D_MODEL)),
                  _row_spec(tm, D_MODEL)] + [_ANY] * (ns + nsib),
        out_specs=[_row_spec(tm, D_MODEL), _full_spec((1, D_MODEL))] + [_ANY] * (ns + nsib),
        out_shape=[jax.ShapeDtypeStruct((n, D_MODEL), F32), jax.ShapeDtypeStruct((1, D_MODEL), F32)]
        + [jax.ShapeDtypeStruct(a.shape, a.dtype) for a in tuple(scatter) + tuple(to_sibling)],
        scratch_shapes=(_scatter_sems(ns) if ns else []) + (_sibling_sems(nsib) if nsib else []),
        compiler_params=_cparams(1),
    )(dproj, w_in, x, w_pre, dxo, *scatter, *to_sibling)
    return out[0], out[1], out[2:2 + ns], out[2 + ns:]


def in_proj_bwd_w(h, dproj, tm=1024, as_shards=False, kv=None, slot_sums=()):
    n, p = dproj.shape
    chunk = p // (4 if p % 4096 == 0 else 3)
    tm = min(tm, n)
    steps = n // tm
    shard = p // N_CHIPS
    n_kv = 0 if kv is None else 3
    n_sum = len(slot_sums)
    kv_from, kv_to = D_MODEL, D_MODEL + 2 * KV_WIDTH

    def body(*refs):
        h_ref, dp_ref = refs[:2]
        outs = 2 + n_kv + n_sum
        dw_ref, db_ref = refs[outs:outs + 2]
        sums_at = outs + 2 + (kv is not None)
        scratch = refs[sums_at + n_sum:]
        acc_scr, sem, staging = scratch[0], scratch[1], scratch[2:]
        i = pl.program_id(0)
        _sum_slots_into(refs[2 + n_kv:outs], refs[sums_at:sums_at + n_sum])

        @pl.when(i == 0)
        def _():
            acc_scr[...] = jnp.zeros_like(acc_scr)
            db_ref[...] = jnp.zeros_like(db_ref)

        if kv is not None:
            dk_ref, dv_ref, cs_ref, kv_ref = refs[2], refs[3], refs[4], refs[outs + 2]
            made = jnp.concatenate([_rope_transposed(dk_ref[...].T, *_rope_tables(cs_ref[...])), dv_ref[...].T],
                                   axis=1).astype(BF16)
            kv_ref[...] = made

        def columns(c0):
            if kv is None or c0 + chunk <= kv_from or c0 >= kv_to:
                return dp_ref[:, c0:c0 + chunk]
            return jnp.concatenate([dp_ref[:, c0:kv_from], made, dp_ref[:, kv_to:c0 + chunk]], axis=1)

        ht = h_ref[...].T
        for c0 in range(0, p, chunk):
            dp = columns(c0)
            acc_scr[:, c0:c0 + chunk] += jnp.dot(ht, dp, preferred_element_type=F32)
            db_ref[:, c0:c0 + chunk] += jnp.sum(dp.astype(F32), axis=0, keepdims=True)

        @pl.when(i == steps - 1)
        def _():
            if as_shards:
                for s in range(N_CHIPS):
                    staging[0][...] = acc_scr[:, s * shard:(s + 1) * shard].astype(BF16)
                    out = pltpu.make_async_copy(staging[0], dw_ref.at[s], sem)
                    out.start()
                    out.wait()
            else:
                out = pltpu.make_async_copy(acc_scr, dw_ref, sem)
                out.start()
                out.wait()

    dw_shape = jax.ShapeDtypeStruct((N_CHIPS, D_MODEL, shard), BF16) if as_shards else (
        jax.ShapeDtypeStruct((D_MODEL, p), F32))
    in_specs = [_row_spec(tm, D_MODEL), _row_spec(tm, p)]
    out_specs = [_ANY, _full_spec((1, p))]
    out_shape = [dw_shape, jax.ShapeDtypeStruct((1, p), F32)]
    if kv is not None:
        columns_t = pl.BlockSpec((KV_WIDTH, tm), lambda i: (0, i))
        in_specs += [columns_t, columns_t, _row_spec(tm, ANGLE_COLS)]
        out_specs.append(pl.BlockSpec((tm, kv_to - kv_from), lambda i: (i, kv_from // (kv_to - kv_from))))
        out_shape.append(jax.ShapeDtypeStruct(dproj.shape, dproj.dtype))
    sum_in, sum_out, sum_shapes = _slot_sum_specs(slot_sums, steps)
    in_specs, out_specs, out_shape = in_specs + sum_in, out_specs + sum_out, out_shape + sum_shapes
    return pl.pallas_call(
        body, name=f"in_proj_bwd_w_{p}", grid=(steps,),
        in_specs=in_specs, out_specs=out_specs, out_shape=out_shape,
        scratch_shapes=[pltpu.VMEM((D_MODEL, p), F32), pltpu.SemaphoreType.DMA]
        + ([pltpu.VMEM((D_MODEL, shard), BF16)] if as_shards else []),
        input_output_aliases={1: 2} if kv is not None else {},
        compiler_params=_cparams(1),
    )(h, dproj, *(kv or ()), *slot_sums)


PAIRS = GROUP // 2
GROUP_ROWS = PAIRS * ATTN_BLOCK
MASKED = -1e30


def _kv_windows(k_ref, v_ref, i):
    ps = pl.multiple_of(jnp.maximum(i - 1, 0) * ATTN_BLOCK, ATTN_BLOCK)
    cs = pl.multiple_of(i * ATTN_BLOCK, ATTN_BLOCK)
    kw = jnp.concatenate([k_ref[pl.ds(ps, ATTN_BLOCK), :], k_ref[pl.ds(cs, ATTN_BLOCK), :]], axis=0)
    vw = jnp.concatenate([v_ref[pl.ds(ps, ATTN_BLOCK), :], v_ref[pl.ds(cs, ATTN_BLOCK), :]], axis=0)
    return kw.astype(F32).T, vw.astype(F32).T, ps, cs


def _low_rows(shape):
    return lax.broadcasted_iota(jnp.int32, shape, 0) < HEAD_DIM


def _spread(w, kvh):
    low = _low_rows(w.shape)
    swapped = pltpu.roll(w, HEAD_DIM, 0)
    if kvh == 0:
        return jnp.where(low, w, 0.0), jnp.where(low, 0.0, swapped)
    return jnp.where(low, swapped, 0.0), jnp.where(low, 0.0, w)


def _unspread(d_a, d_b, kvh):
    low = _low_rows(d_a.shape)
    if kvh == 0:
        return jnp.where(low, d_a + pltpu.roll(d_b, HEAD_DIM, 0), 0.0)
    return jnp.where(low, 0.0, pltpu.roll(d_a, HEAD_DIM, 0) + d_b)


def _stack_pairs(ref, kvh):
    return jnp.concatenate([ref[:, (kvh * PAIRS + j) * LANES:(kvh * PAIRS + j + 1) * LANES] for j in range(PAIRS)],
                           axis=0)


def _fill_bias(bias_scr):
    shape = (GROUP_ROWS, 2 * ATTN_BLOCK)
    r = lax.broadcasted_iota(jnp.int32, shape, 0) % ATTN_BLOCK
    c = lax.broadcasted_iota(jnp.int32, shape, 1)
    in_cur = (c >= ATTN_BLOCK) & ((c - ATTN_BLOCK) <= r)
    in_prev = (c < ATTN_BLOCK) & (c > r)
    bias_scr[0] = jnp.where(in_cur, 0.0, MASKED)
    bias_scr[1] = jnp.where(in_cur | in_prev, 0.0, MASKED)
    bias_scr[2] = jnp.where(c == r, 1.0, 0.0)


N_BIAS_TABLES = 3


def _sink_table(sinks):
    t = jnp.transpose(sinks.reshape(N_KV_HEADS, PAIRS, 2), (0, 2, 1))
    return jnp.broadcast_to(t[:, :, :, None, None], (N_KV_HEADS, 2, PAIRS, ATTN_BLOCK, LANES)).reshape(
        N_KV_HEADS, 2, GROUP_ROWS, LANES)


def attn_fwd(q, k, v, z, sink_tab, batch, seq, gather=()):
    nb = seq // ATTN_BLOCK
    ng = len(gather)

    def body(*refs):
        q_ref, k_ref, v_ref, z_ref, s_ref = refs[:5]
        og_ref, bias_scr = refs[5 + ng], refs[6 + 2 * ng]
        exchange = (refs[5:5 + ng], refs[6 + ng:6 + 2 * ng]) + tuple(refs[7 + 2 * ng:])
        b, i = pl.program_id(0), pl.program_id(1)

        @pl.when((b == 0) & (i == 0))
        def _():
            _fill_bias(bias_scr)
            if ng:
                _gather_start(*exchange)

        kw, vw, _, _ = _kv_windows(k_ref, v_ref, i)
        bias, at_sink = bias_scr[jnp.minimum(i, 1)], bias_scr[2] > 0.5
        for kvh in range(N_KV_HEADS):
            k_a, k_b = _spread(kw, kvh)
            v_a, v_b = _spread(vw, kvh)
            og = _attn_group(_stack_pairs(q_ref, kvh), k_a, v_a, k_b, v_b, _stack_pairs(z_ref, kvh),
                             s_ref[kvh, 0], s_ref[kvh, 1], bias, at_sink)
            for j in range(PAIRS):
                og_ref[:, (kvh * PAIRS + j) * LANES:(kvh * PAIRS + j + 1) * LANES] = (
                    og[j * ATTN_BLOCK:(j + 1) * ATTN_BLOCK].astype(BF16))

        if ng:
            @pl.when((b == batch - 1) & (i == nb - 1))
            def _():
                _gather_finish(*exchange)

    blk = lambda w: pl.BlockSpec((ATTN_BLOCK, w), lambda b, i: (b * nb + i, 0))
    seq_spec = pl.BlockSpec((seq, KV_WIDTH), lambda b, i: (b, 0))
    out = pl.pallas_call(
        body, name="attn_fwd", grid=(batch, nb),
        in_specs=[blk(D_MODEL), seq_spec, seq_spec, blk(D_MODEL), _full_spec(sink_tab.shape)] + [_ANY] * ng,
        out_specs=[blk(D_MODEL)] + [_ANY] * ng,
        out_shape=[jax.ShapeDtypeStruct((batch * seq, D_MODEL), BF16)]
        + [jax.ShapeDtypeStruct((N_CHIPS,) + a.shape, a.dtype) for a in gather],
        scratch_shapes=[pltpu.VMEM((N_BIAS_TABLES, GROUP_ROWS, 2 * ATTN_BLOCK), F32)] + (_gather_sems(ng) if ng else []),
        compiler_params=_cparams(2),
    )(q, k, v, z, sink_tab, *gather)
    return out[0], out[1:]


def attn_bwd(q, k, v, z, sink_tab, dog, angles, batch, seq, scatter=()):
    nb = seq // ATTN_BLOCK
    ns = len(scatter)

    def body(*refs):
        q_ref, k_ref, v_ref, z_ref, s_ref, g_ref, cs_ref = refs[:7]
        dp_ref, dk_ref, dv_ref, ds_ref = refs[7 + ns:11 + ns]
        bias_scr = refs[11 + 2 * ns]
        exchange = (refs[7:7 + ns], refs[11 + ns:11 + 2 * ns]) + tuple(refs[12 + 2 * ns:])
        b, i = pl.program_id(0), pl.program_id(1)

        @pl.when((b == 0) & (i == 0))
        def _():
            _fill_bias(bias_scr)
            ds_ref[...] = jnp.zeros_like(ds_ref)
            if ns:
                _scatter_start(*exchange)

        @pl.when(i == 0)
        def _():
            dk_ref[...] = jnp.zeros_like(dk_ref)
            dv_ref[...] = jnp.zeros_like(dv_ref)

        kw, vw, ps, cs = _kv_windows(k_ref, v_ref, i)
        bias = bias_scr[jnp.minimum(i, 1)]
        tabs = _rope_tables(cs_ref[...])
        dkw = jnp.zeros_like(kw)
        dvw = jnp.zeros_like(vw)
        for kvh in range(N_KV_HEADS):
            k_a, k_b = _spread(kw, kvh)
            v_a, v_b = _spread(vw, kvh)
            _, vjp = jax.vjp(functools.partial(_attn_group, bias=bias), _stack_pairs(q_ref, kvh).astype(F32),
                             k_a, v_a, k_b, v_b, _stack_pairs(z_ref, kvh), s_ref[kvh, 0], s_ref[kvh, 1])
            dqs, dk_a, dv_a, dk_b, dv_b, dzs, ds_a, ds_b = vjp(_stack_pairs(g_ref, kvh).astype(F32))
            dkw = dkw + _unspread(dk_a, dk_b, kvh)
            dvw = dvw + _unspread(dv_a, dv_b, kvh)
            ds_ref[kvh, 0] += jnp.sum(ds_a.reshape(PAIRS, ATTN_BLOCK, LANES), axis=1)
            ds_ref[kvh, 1] += jnp.sum(ds_b.reshape(PAIRS, ATTN_BLOCK, LANES), axis=1)
            for j in range(PAIRS):
                rows = slice(j * ATTN_BLOCK, (j + 1) * ATTN_BLOCK)
                col = (kvh * PAIRS + j) * LANES
                dp_ref[:, col:col + LANES] = _rope_transposed(dqs[rows] * (HEAD_DIM ** -0.5), *tabs).astype(BF16)
                zc = D_MODEL + 2 * KV_WIDTH + col
                dp_ref[:, zc:zc + LANES] = dzs[rows].astype(BF16)
        dp_ref[:, D_MODEL:D_MODEL + 2 * KV_WIDTH] = jnp.zeros((ATTN_BLOCK, 2 * KV_WIDTH), BF16)
        dk_ref[:, pl.ds(ps, ATTN_BLOCK)] += dkw[:, :ATTN_BLOCK]
        dk_ref[:, pl.ds(cs, ATTN_BLOCK)] += dkw[:, ATTN_BLOCK:]
        dv_ref[:, pl.ds(ps, ATTN_BLOCK)] += dvw[:, :ATTN_BLOCK]
        dv_ref[:, pl.ds(cs, ATTN_BLOCK)] += dvw[:, ATTN_BLOCK:]

        if ns:
            @pl.when((b == batch - 1) & (i == nb - 1))
            def _():
                _scatter_finish(*exchange)

    blk = lambda w: pl.BlockSpec((ATTN_BLOCK, w), lambda b, i: (b * nb + i, 0))
    seq_spec = pl.BlockSpec((seq, KV_WIDTH), lambda b, i: (b, 0))
    seq_spec_t = pl.BlockSpec((KV_WIDTH, seq), lambda b, i: (0, b))
    n = batch * seq
    ds_shape = (N_KV_HEADS, 2, PAIRS, LANES)
    out = pl.pallas_call(
        body, name="attn_bwd", grid=(batch, nb),
        in_specs=[blk(D_MODEL), seq_spec, seq_spec, blk(D_MODEL), _full_spec(sink_tab.shape), blk(D_MODEL)]
        + [blk(ANGLE_COLS)] + [_ANY] * ns,
        out_specs=[blk(ATTN_IN), seq_spec_t, seq_spec_t, _full_spec(ds_shape)] + [_ANY] * ns,
        out_shape=[jax.ShapeDtypeStruct((n, ATTN_IN), BF16), jax.ShapeDtypeStruct((KV_WIDTH, n), F32),
                   jax.ShapeDtypeStruct((KV_WIDTH, n), F32), jax.ShapeDtypeStruct(ds_shape, F32)]
        + [jax.ShapeDtypeStruct(a.shape, a.dtype) for a in scatter],
        scratch_shapes=[pltpu.VMEM((N_BIAS_TABLES, GROUP_ROWS, 2 * ATTN_BLOCK), F32)] + (_scatter_sems(ns) if ns else []),
        compiler_params=_cparams(2),
    )(q, k, v, z, sink_tab, dog, angles, *scatter)
    return out[0], out[1], out[2], out[3], out[4:]


def rec_fwd(proj, lb_logits, gnorm_w, batch, seq):
    nblk = seq // REC_BLOCK

    def body(p_ref, lb_ref, gw_ref, og_ref, st_ref, safe_ref, s_scr):
        @pl.when(pl.program_id(1) == 0)
        def _():
            s_scr[...] = jnp.zeros_like(s_scr)

        S = s_scr[...]
        st_ref[0] = S
        qr, fr, v, z = (p_ref[:, part * D_MODEL:(part + 1) * D_MODEL] for part in range(4))
        lf, k = forget_gate(fr, lb_ref[1:2, :] - lb_ref[0:1, :])
        q, b = silu(qr), cumsum_rows(lf)
        safe = jnp.min(_rec_margin(b)) >= -SAFE_RANGE

        gate = gw_ref[...] * silu(z)
        safe_ref[0] = jnp.full((REC_HEADS, LANES), safe.astype(F32))

        def store(o, S_new):
            og_ref[...] = (o * lax.rsqrt(head_sum(o * o) * (1.0 / REC_DIM) + NORM_EPS) * gate).astype(BF16)
            s_scr[...] = S_new

        @pl.when(safe)
        def _():
            store(*_rec_cores_fast(q, k, v, b, S))

        @pl.when(jnp.logical_not(safe))
        def _():
            outs = [_rec_core_slow(*args) for args in zip(*(_heads(t) for t in (q, k, v, b, S)))]
            store(*(jnp.concatenate(parts, axis=1) for parts in zip(*outs)))

    blk = lambda w: pl.BlockSpec((REC_BLOCK, w), lambda b, j: (b * nblk + j, 0))
    st_spec = pl.BlockSpec((1, REC_DIM, D_MODEL), lambda b, j: (b * nblk + j, 0, 0))
    safe_spec = pl.BlockSpec((1, REC_HEADS, LANES), lambda b, j: (b * nblk + j, 0, 0))
    return pl.pallas_call(
        body, name="rec_fwd", grid=(batch, nblk),
        in_specs=[blk(REC_IN), _full_spec((2, D_MODEL)), _full_spec((1, D_MODEL))],
        out_specs=[blk(D_MODEL), st_spec, safe_spec],
        out_shape=[jax.ShapeDtypeStruct((batch * seq, D_MODEL), BF16),
                   jax.ShapeDtypeStruct((batch * nblk, REC_DIM, D_MODEL), F32),
                   jax.ShapeDtypeStruct((batch * nblk, REC_HEADS, LANES), F32)],
        scratch_shapes=[pltpu.VMEM((REC_DIM, D_MODEL), F32)],
        compiler_params=_cparams(2),
    )(proj, lb_logits, jnp.tile(gnorm_w, (1, REC_HEADS)))


def rec_bwd(proj, states, safe, lb_logits, gnorm_w, dog, batch, seq):
    nblk = seq // REC_BLOCK

    def body(p_ref, st_ref, safe_ref, lb_ref, gw_ref, g_ref, dp_ref, dlb_ref, dgw_ref, ds_scr):
        @pl.when((pl.program_id(0) == 0) & (pl.program_id(1) == 0))
        def _():
            dlb_ref[...] = jnp.zeros_like(dlb_ref)
            dgw_ref[...] = jnp.zeros_like(dgw_ref)

        @pl.when(pl.program_id(1) == 0)
        def _():
            ds_scr[...] = jnp.zeros_like(ds_scr)

        def load():
            primals = tuple(p_ref[:, part * D_MODEL:(part + 1) * D_MODEL] for part in range(4)) + (
                st_ref[0], lb_ref[0:1, :], lb_ref[1:2, :], gw_ref[...])
            return primals, (g_ref[...].astype(F32), ds_scr[...])

        def store(dqr, dfr, dv, dz, dS, dl0, dl1, dgw):
            for part, val in enumerate((dqr, dfr, dv, dz)):
                dp_ref[:, part * D_MODEL:(part + 1) * D_MODEL] = val.astype(BF16)
            ds_scr[...] = dS
            dlb_ref[0:1, :] += dl0
            dlb_ref[1:2, :] += dl1
            dgw_ref[...] += functools.reduce(jnp.add, _heads(dgw))

        fast = jnp.max(safe_ref[0]) > 0.5

        @pl.when(fast)
        def _():
            primals, cotangents = load()
            store(*jax.vjp(_rec_block_fast, *primals)[1](cotangents))

        @pl.when(jnp.logical_not(fast))
        def _():
            primals, cotangents = load()
            outs = [jax.vjp(functools.partial(_rec_head, _rec_core_slow), *args)[1](cts)
                    for args, cts in zip(zip(*(_heads(t) for t in primals)), zip(*(_heads(t) for t in cotangents)))]
            store(*(jnp.concatenate(parts, axis=1) for parts in zip(*outs)))

    blk = lambda w: pl.BlockSpec((REC_BLOCK, w), lambda b, j: (b * nblk + nblk - 1 - j, 0))
    st_spec = pl.BlockSpec((1, REC_DIM, D_MODEL), lambda b, j: (b * nblk + nblk - 1 - j, 0, 0))
    safe_spec = pl.BlockSpec((1, REC_HEADS, LANES), lambda b, j: (b * nblk + nblk - 1 - j, 0, 0))
    return pl.pallas_call(
        body, name="rec_bwd", grid=(batch, nblk),
        in_specs=[blk(REC_IN), st_spec, safe_spec, _full_spec((2, D_MODEL)), _full_spec((1, D_MODEL)),
                  blk(D_MODEL)],
        out_specs=[blk(REC_IN), _full_spec((2, D_MODEL)), _full_spec((1, REC_DIM))],
        out_shape=[jax.ShapeDtypeStruct((batch * seq, REC_IN), BF16), jax.ShapeDtypeStruct((2, D_MODEL), F32),
                   jax.ShapeDtypeStruct((1, REC_DIM), F32)],
        scratch_shapes=[pltpu.VMEM((REC_DIM, D_MODEL), F32)],
        compiler_params=_cparams(2),
    )(proj, states, safe, lb_logits, jnp.tile(gnorm_w, (1, REC_HEADS)), dog)


_ANY = pl.BlockSpec(memory_space=pl.ANY)


def _chip_peers():
    x, y, c = lax.axis_index("x"), lax.axis_index("y"), lax.axis_index("c")
    peers = []
    for fx, fy in ((1, 0), (0, 1), (1, 1)):
        px, py = (1 - x if fx else x), (1 - y if fy else y)
        peers.append(((px, py, c), 2 * px + py))
    return 2 * x + y, peers


def _remote(src, dst, send_sem, recv_sem, device):
    return pltpu.make_async_remote_copy(src_ref=src, dst_ref=dst, send_sem=send_sem, recv_sem=recv_sem,
                                        device_id=device, device_id_type=MESH)


N_FLIPS = N_CHIPS - 1


def _scatter_sems(n):
    return [pltpu.SemaphoreType.DMA((n * N_FLIPS,)), pltpu.SemaphoreType.DMA((n * N_FLIPS,)),
            pltpu.SemaphoreType.DMA((n,))]


def _scatter_copies(ins, outs, send_sems, recv_sems, local_sems, starting):
    me, peers = _chip_peers()
    local = [pltpu.make_async_copy(ins[k].at[me], outs[k].at[me], local_sems.at[k]) for k in range(len(ins))]
    sends, arrivals = [], []
    for k in range(len(ins)):
        for j, (device, idx) in enumerate(peers):
            sems = (send_sems.at[k * N_FLIPS + j], recv_sems.at[k * N_FLIPS + j], device)
            sends.append(_remote(ins[k].at[idx], outs[k].at[me], *sems))
            if not starting:
                arrivals.append(_remote(ins[k].at[me], outs[k].at[idx], *sems))
    return local, sends, arrivals


def _scatter_start(*refs):
    local, sends, _ = _scatter_copies(*refs, starting=True)
    for cp in local + sends:
        cp.start()


def _scatter_finish(*refs):
    local, sends, arrivals = _scatter_copies(*refs, starting=False)
    for cp in arrivals:
        cp.wait_recv()
    for cp in sends:
        cp.wait_send()
    for cp in local:
        cp.wait()


def _gather_sems(n):
    return [pltpu.SemaphoreType.DMA((n * N_FLIPS,)) for _ in range(4)] + [pltpu.SemaphoreType.DMA((n,))]


def _gather_copies(ins, outs, send_sems, recv_sems, pass_send_sems, pass_recv_sems, local_sems, starting):
    me, peers = _chip_peers()
    c = lax.axis_index("c")
    sibling = (lax.axis_index("x"), lax.axis_index("y"), 1 - c)
    local = [pltpu.make_async_copy(ins[k], outs[k].at[me], local_sems.at[k]) for k in range(len(ins))]
    sends, arrivals, passes, pass_arrivals = [], [], [], []
    for k in range(len(ins)):
        half = ins[k].shape[0] // 2
        mine, other = pl.ds(c * half, half), pl.ds((1 - c) * half, half)
        for j, (device, idx) in enumerate(peers):
            s = k * N_FLIPS + j
            sends.append(_remote(ins[k].at[mine], outs[k].at[me].at[mine], send_sems.at[s], recv_sems.at[s], device))
            if starting:
                continue
            arrived = outs[k].at[idx].at[mine]
            arrivals.append(_remote(ins[k].at[mine], arrived, send_sems.at[s], recv_sems.at[s], device))
            passes.append(_remote(arrived, arrived, pass_send_sems.at[s], pass_recv_sems.at[s], sibling))
            passed = outs[k].at[idx].at[other]
            pass_arrivals.append(_remote(passed, passed, pass_send_sems.at[s], pass_recv_sems.at[s], sibling))
    return local, sends, arrivals, passes, pass_arrivals


def _gather_start(*refs):
    local, sends, _, _, _ = _gather_copies(*refs, starting=True)
    for cp in local + sends:
        cp.start()


def _gather_finish(*refs):
    local, sends, arrivals, passes, pass_arrivals = _gather_copies(*refs, starting=False)
    for arrival, onward in zip(arrivals, passes):
        arrival.wait_recv()
        onward.start()
    for cp in pass_arrivals:
        cp.wait_recv()
    for cp in sends + passes:
        cp.wait_send()
    for cp in local:
        cp.wait()


def chip_gather(arrays):
    n = len(arrays)

    def body(*refs):
        _gather_start(refs[:n], refs[n:2 * n], *refs[2 * n:])
        _gather_finish(refs[:n], refs[n:2 * n], *refs[2 * n:])

    return pl.pallas_call(
        body, name="chip_gather", in_specs=[_ANY] * n, out_specs=[_ANY] * n,
        out_shape=[jax.ShapeDtypeStruct((N_CHIPS,) + a.shape, a.dtype) for a in arrays],
        scratch_shapes=_gather_sems(n),
    )(*arrays)


def _sibling_sems(n):
    return [pltpu.SemaphoreType.DMA((n,)), pltpu.SemaphoreType.DMA((n,))]


def _sibling_copies(ins, outs, send_sems, recv_sems):
    sibling = (lax.axis_index("x"), lax.axis_index("y"), 1 - lax.axis_index("c"))
    return [_remote(ins[k], outs[k], send_sems.at[k], recv_sems.at[k], sibling) for k in range(len(ins))]


def all_gather_small(vec, parts):
    s, r, c_dim = parts.shape
    rows = min(128, r)

    def body(v_ref, parts_ref, out_ref, mine_ref, other_ref, send_sems, recv_sems, local_sem, sib_send, sib_recv):
        x, y, c = lax.axis_index("x"), lax.axis_index("y"), lax.axis_index("c")
        me = 4 * x + 2 * y + c
        local = pltpu.make_async_copy(v_ref, out_ref.at[me], local_sem)
        local.start()
        sends, recvs = [], []
        for j in range(1, N_DEV):
            px = jnp.where(j & 4, 1 - x, x)
            py = jnp.where(j & 2, 1 - y, y)
            pc = jnp.where(j & 1, 1 - c, c)
            common = dict(send_sem=send_sems.at[j - 1], recv_sem=recv_sems.at[j - 1], device_id=(px, py, pc),
                          device_id_type=MESH)
            sends.append(pltpu.make_async_remote_copy(src_ref=v_ref, dst_ref=out_ref.at[me], **common))
            recvs.append(pltpu.make_async_remote_copy(src_ref=v_ref, dst_ref=out_ref.at[4 * px + 2 * py + pc],
                                                      **common))
        for cp in sends:
            cp.start()

        for i in range(r // rows):
            at = pl.ds(i * rows, rows)
            acc = parts_ref[0, at, :].astype(F32)
            for t in range(1, s):
                acc = acc + parts_ref[t, at, :].astype(F32)
            mine_ref[at, :] = acc
        passed = _sibling_copies([mine_ref], [other_ref], sib_send, sib_recv)[0]
        passed.start()

        for cp in recvs:
            cp.wait_recv()
        for cp in sends:
            cp.wait_send()
        local.wait()
        passed.wait()

    vmem = pl.BlockSpec(memory_space=pltpu.VMEM)
    return pl.pallas_call(
        body, name="all_gather_small", in_specs=[_ANY, vmem], out_specs=[_ANY, vmem, _ANY],
        out_shape=[jax.ShapeDtypeStruct((N_DEV,) + vec.shape, vec.dtype), jax.ShapeDtypeStruct((r, c_dim), F32),
                   jax.ShapeDtypeStruct((r, c_dim), F32)],
        scratch_shapes=[pltpu.SemaphoreType.DMA((N_DEV - 1,)), pltpu.SemaphoreType.DMA((N_DEV - 1,)),
                        pltpu.SemaphoreType.DMA] + _sibling_sems(1),
        compiler_params=_cparams(0),
    )(vec, parts)


def sum_slots(stacked, tm=256):
    s, r, c = stacked.shape
    tm = min(tm, r)

    def body(in_ref, out_ref):
        acc = in_ref[0].astype(F32)
        for t in range(1, s):
            acc = acc + in_ref[t].astype(F32)
        out_ref[...] = acc

    return pl.pallas_call(
        body, name=f"sum_slots_{s}_{r}_{c}", grid=(r // tm,),
        in_specs=[pl.BlockSpec((s, tm, c), lambda i: (0, i, 0))], out_specs=_row_spec(tm, c),
        out_shape=jax.ShapeDtypeStruct((r, c), F32), compiler_params=_cparams(1),
    )(stacked)


def adamw(w, m, v, g_a, g_b=None, tm=512):
    r, c = w.shape
    tm = min(tm, r)
    two = g_b is not None
    slots = g_a.shape[0] if g_a.ndim == 3 else 0

    def body(*refs):
        w_ref, m_ref, v_ref, ga_ref = refs[:4]
        g_ref, d_ref, nm_ref, nv_ref = refs[-4:]
        if slots:
            g = ga_ref[0]
            for t in range(1, slots):
                g = g + ga_ref[t]
        else:
            g = ga_ref[...] + refs[4][...] if two else ga_ref[...]
        nm = ADAM_B1 * m_ref[...] + (1.0 - ADAM_B1) * g
        nv = ADAM_B2 * v_ref[...] + (1.0 - ADAM_B2) * (g * g)
        m_hat = nm / (1.0 - ADAM_B1 ** ADAM_STEP)
        v_hat = nv / (1.0 - ADAM_B2 ** ADAM_STEP)
        g_ref[...] = g
        d_ref[...] = -ADAM_LR * (m_hat / (jnp.sqrt(v_hat) + ADAM_EPS) + ADAM_WD * w_ref[...])
        nm_ref[...] = nm
        nv_ref[...] = nv

    args = [w, m, v, g_a] + ([g_b] if two else [])
    in_specs = [_row_spec(tm, c)] * len(args)
    if slots:
        in_specs[3] = pl.BlockSpec((slots, tm, c), lambda i: (0, i, 0))
    return pl.pallas_call(
        body, name=f"adamw_{r}_{c}", grid=(r // tm,),
        in_specs=in_specs, out_specs=[_row_spec(tm, c)] * 4,
        out_shape=[jax.ShapeDtypeStruct((r, c), F32)] * 4, compiler_params=_cparams(1),
    )(*args)


def adamw_shards(items, steps=8):
    n = len(items)

    def body(*refs):
        for k in range(n):
            w_ref, m_ref, v_ref, ga_ref, gb_ref = refs[5 * k:5 * k + 5]
            g_ref, d_ref, nm_ref, nv_ref = refs[5 * n + 4 * k:5 * n + 4 * k + 4]
            g = ga_ref[...] + gb_ref[...]
            nm = ADAM_B1 * m_ref[...] + (1.0 - ADAM_B1) * g
            nv = ADAM_B2 * v_ref[...] + (1.0 - ADAM_B2) * (g * g)
            m_hat = nm / (1.0 - ADAM_B1 ** ADAM_STEP)
            v_hat = nv / (1.0 - ADAM_B2 ** ADAM_STEP)
            g_ref[...] = g
            d_ref[...] = -ADAM_LR * (m_hat / (jnp.sqrt(v_hat) + ADAM_EPS) + ADAM_WD * w_ref[...])
            nm_ref[...] = nm
            nv_ref[...] = nv

    in_specs, out_specs, out_shape = [], [], []
    for item in items:
        r, c = item[0].shape
        in_specs += [_row_spec(r // steps, c)] * 5
        out_specs += [_row_spec(r // steps, c)] * 4
        out_shape += [jax.ShapeDtypeStruct((r, c), F32)] * 4
    out = pl.pallas_call(
        body, name="adamw_shards", grid=(steps,), in_specs=in_specs, out_specs=out_specs, out_shape=out_shape,
        compiler_params=_cparams(1),
    )(*[a for item in items for a in item])
    return [out[4 * k:4 * k + 4] for k in range(n)]


_SMALL = (("pre_norm_w", (2, D_MODEL)), ("post_norm_w", (2, D_MODEL)), ("attn_b_in", (1, ATTN_IN)),
          ("attn_sinks", (1, N_HEADS)), ("attn_b_out", (1, D_MODEL)), ("rec_lb_logits", (2, D_MODEL)),
          ("rec_gnorm_w", (1, REC_DIM)))
_SMALL_ROWS = 16


def _pack_small(parts, last_row=None):
    rows = []
    for (name, shape) in _SMALL:
        flat = parts[name].reshape(-1)
        pad = -flat.shape[0] % D_MODEL
        rows.append(jnp.pad(flat, (0, pad)).reshape(-1, D_MODEL))
    used = sum(r.shape[0] for r in rows)
    rows.append(jnp.zeros((_SMALL_ROWS - 1 - used, D_MODEL), F32))
    rows.append(jnp.zeros((1, D_MODEL), F32) if last_row is None else last_row)
    return jnp.concatenate(rows, axis=0)


def _unpack_small(packed):
    out, row = {}, 0
    for (name, shape) in _SMALL:
        size = shape[0] * shape[1]
        nrows = -(-size // D_MODEL)
        out[name] = packed[row:row + nrows].reshape(-1)[:size].reshape(shape)
        row += nrows
    return out


_CARRIED = ("rec_w_in", "rec_w_out", "attn_w_out")


_LATE = ("attn_w_out", "rec_w_in", "rec_w_out")


def local_step(x, positions, pre_norm_w, post_norm_w, attn_w_in, attn_b_in, attn_sinks, attn_w_out, attn_b_out,
               rec_w_in, rec_lb_logits, rec_gnorm_w, rec_w_out, loss_target, distributed=False):
    batch, seq, _ = x.shape
    n = batch * seq
    x0 = x.reshape(n, D_MODEL)
    angles = _rope_angles(positions)
    pre0, pre1 = pre_norm_w[0:1], pre_norm_w[1:2]
    post0, post1 = post_norm_w[0:1], post_norm_w[1:2]

    late = (attn_w_out, rec_w_in, rec_w_out) if distributed else ()
    (h0, q, k, v, z), late = attn_in_proj(x0, pre0, attn_w_in, attn_b_in, angles, to_bf16=late)
    sink_tab = _sink_table(attn_sinks)
    og0, gathered = attn_fwd(q, k, v, z, sink_tab, batch, seq, gather=late)
    if distributed:
        attn_w_out, rec_w_in, rec_w_out = (g if name == "rec_w_in" else _whole_from_shards(name, g)
                                           for name, g in zip(_LATE, gathered))
    y0, x1 = out_proj(og0, attn_w_out, attn_b_out, x0, post0)

    h1, proj1 = rec_in_proj(x1, pre1, rec_w_in)
    og1, states, safe = rec_fwd(proj1, rec_lb_logits, rec_gnorm_w, batch, seq)
    dx2, loss_vec, dog1, d_rec_w_out, d_post1, d_rec_w_out_bf16 = out_proj_loss_bwd(
        og1, rec_w_out, x1, post1, loss_target.reshape(n, D_MODEL))
    dproj1, d_lb, d_gnorm = rec_bwd(proj1, states, safe, rec_lb_logits, rec_gnorm_w, dog1, batch, seq)
    dx1, d_pre1, _, _ = in_proj_bwd_x(dproj1, rec_w_in, x1, pre1, dx2)
    d_rec_w_in, _ = in_proj_bwd_w(h1, dproj1, as_shards=distributed)

    dog0, d_attn_w_out, d_attn_b_out, d_post0, d_attn_w_out_bf16 = out_proj_bwd(dx1, y0, og0, attn_w_out, post0)
    ready = dict(rec_w_in=d_rec_w_in, rec_w_out=_shards_from_whole("rec_w_out", d_rec_w_out_bf16),
                 attn_w_out=_shards_from_whole("attn_w_out", d_attn_w_out_bf16))
    outgoing = [ready[name] for name in _CARRIED] if distributed else []
    dproj0, dk, dv, d_sink_tab, arrived = attn_bwd(q, k, v, z, sink_tab, dog0, angles, batch, seq, scatter=outgoing)
    d_sinks = jnp.transpose(jnp.sum(d_sink_tab, axis=-1), (0, 2, 1)).reshape(1, N_HEADS)
    d_attn_w_in, d_attn_b_in, dproj0, *summed = in_proj_bwd_w(h0, dproj0, kv=(dk, dv, angles),
                                                              slot_sums=list(arrived))
    last = [_shards_from_whole("attn_w_in", d_attn_w_in).astype(BF16)] if distributed else []
    dx0, d_pre0, arrived_last, theirs = in_proj_bwd_x(dproj0, attn_w_in, x0, pre0, dx1, scatter=last,
                                                      to_sibling=summed)

    grads = dict(
        pre_norm_w=jnp.concatenate([d_pre0, d_pre1], axis=0), post_norm_w=jnp.concatenate([d_post0, d_post1], axis=0),
        attn_w_in=d_attn_w_in, attn_b_in=d_attn_b_in, attn_sinks=d_sinks, attn_w_out=d_attn_w_out,
        attn_b_out=d_attn_b_out, rec_w_in=d_rec_w_in, rec_lb_logits=d_lb, rec_gnorm_w=d_gnorm,
        rec_w_out=d_rec_w_out)
    exchanged = dict(zip(_CARRIED, zip(summed, theirs)))
    exchanged.update(zip(("attn_w_in",), arrived_last))
    return loss_vec, dx0.reshape(batch, seq, D_MODEL), grads, exchanged


_BIG = ("attn_w_in", "attn_w_out", "rec_w_in", "rec_w_out")
_COLUMN_SHARDED = ("attn_w_in", "rec_w_in")
_ORDER = ("pre_norm_w", "post_norm_w", "attn_w_in", "attn_b_in", "attn_sinks", "attn_w_out", "attn_b_out",
          "rec_w_in", "rec_lb_logits", "rec_gnorm_w", "rec_w_out")


def _whole_from_shards(name, stacked):
    if name in _COLUMN_SHARDED:
        return jnp.transpose(stacked, (1, 0, 2)).reshape(stacked.shape[1], -1)
    return stacked.reshape(-1, stacked.shape[2])


def _shards_from_whole(name, whole):
    if name in _COLUMN_SHARDED:
        return jnp.transpose(whole.reshape(whole.shape[0], N_CHIPS, -1), (1, 0, 2))
    return whole.reshape(N_CHIPS, -1, whole.shape[1])


def kernel(x, positions, pre_norm_w, post_norm_w, attn_w_in, attn_b_in, attn_sinks, attn_w_out, attn_b_out, rec_w_in, rec_lb_logits, rec_gnorm_w, rec_w_out, loss_target, m_pre_norm_w, m_post_norm_w, m_attn_w_in, m_attn_b_in, m_attn_sinks, m_attn_w_out, m_attn_b_out, m_rec_w_in, m_rec_lb_logits, m_rec_gnorm_w, m_rec_w_out, v_pre_norm_w, v_post_norm_w, v_attn_w_in, v_attn_b_in, v_attn_sinks, v_attn_w_out, v_attn_b_out, v_rec_w_in, v_rec_lb_logits, v_rec_gnorm_w, v_rec_w_out):
    w = dict(pre_norm_w=pre_norm_w, post_norm_w=post_norm_w, attn_w_in=attn_w_in, attn_b_in=attn_b_in,
             attn_sinks=attn_sinks, attn_w_out=attn_w_out, attn_b_out=attn_b_out, rec_w_in=rec_w_in,
             rec_lb_logits=rec_lb_logits, rec_gnorm_w=rec_gnorm_w, rec_w_out=rec_w_out)
    m = dict(pre_norm_w=m_pre_norm_w, post_norm_w=m_post_norm_w, attn_w_in=m_attn_w_in, attn_b_in=m_attn_b_in,
             attn_sinks=m_attn_sinks, attn_w_out=m_attn_w_out, attn_b_out=m_attn_b_out, rec_w_in=m_rec_w_in,
             rec_lb_logits=m_rec_lb_logits, rec_gnorm_w=m_rec_gnorm_w, rec_w_out=m_rec_w_out)
    v = dict(pre_norm_w=v_pre_norm_w, post_norm_w=v_post_norm_w, attn_w_in=v_attn_w_in, attn_b_in=v_attn_b_in,
             attn_sinks=v_attn_sinks, attn_w_out=v_attn_w_out, attn_b_out=v_attn_b_out, rec_w_in=v_rec_w_in,
             rec_lb_logits=v_rec_lb_logits, rec_gnorm_w=v_rec_gnorm_w, rec_w_out=v_rec_w_out)

    shards = {name: w[name][0] for name in _BIG}
    attn_w_in_whole = _whole_from_shards("attn_w_in", chip_gather([shards["attn_w_in"].astype(BF16)])[0])

    loss_vec, grad_x, grads, exchanged = local_step(
        x, positions, pre_norm_w, post_norm_w, attn_w_in_whole, attn_b_in, attn_sinks, shards["attn_w_out"],
        attn_b_out, shards["rec_w_in"], rec_lb_logits, rec_gnorm_w, shards["rec_w_out"], loss_target,
        distributed=True)

    small_parts, mine, other = all_gather_small(_pack_small(grads, last_row=loss_vec), exchanged["attn_w_in"])
    exchanged["attn_w_in"] = (mine, other)
    out_g, out_d, out_m, out_v = {}, {}, {}, {}
    updated = adamw_shards([(shards[name], m[name][0], v[name][0], *exchanged[name]) for name in _BIG])
    for name, (g, d, nm, nv) in zip(_BIG, updated):
        out_g[name], out_d[name], out_m[name], out_v[name] = g[None], d[None], nm[None], nv[None]

    packed = adamw(_pack_small(w), _pack_small(m), _pack_small(v), small_parts)
    loss = jnp.sum(packed[0][_SMALL_ROWS - 1]) * (0.5 / D_MODEL)
    for dst, val in zip((out_g, out_d, out_m, out_v), packed):
        dst.update(_unpack_small(val))

    return (loss, grad_x, *[out_g[n] for n in _ORDER], *[out_d[n] for n in _ORDER],
            *[out_m[n] for n in _ORDER], *[out_v[n] for n in _ORDER])
```

```python
import functools

import jax
import jax.numpy as jnp
from jax import lax
from jax.experimental import pallas as pl
from jax.experimental.pallas import tpu as pltpu

F32 = jnp.float32
BF16 = jnp.bfloat16
MESH = pl.DeviceIdType.MESH

D_MODEL = 1024
HEAD_DIM = 64
N_HEADS = 16
N_KV_HEADS = 2
GROUP = N_HEADS // N_KV_HEADS
KV_WIDTH = N_KV_HEADS * HEAD_DIM
ATTN_IN = 2 * D_MODEL + 2 * KV_WIDTH
ATTN_BLOCK = 128
ROPE_THETA = 500000.0
ROPE_DIM = HEAD_DIM // 4
REC_HEADS = 8
REC_DIM = 128
REC_IN = 4 * D_MODEL
REC_BLOCK = 128
DIAG = 8
NORM_EPS = 1e-6
N_CHIPS = 4
N_DEV = 8
LANES = 128

ADAM_LR = 0.001
ADAM_B1 = 0.9
ADAM_B2 = 0.999
ADAM_EPS = 1e-08
ADAM_WD = 0.01
ADAM_STEP = 10

VMEM_LIMIT = 56 * 1024 * 1024


def _cparams(n_axes):
    return pltpu.CompilerParams(dimension_semantics=("arbitrary",) * n_axes, vmem_limit_bytes=VMEM_LIMIT)


def _dot(a, b, contract):
    return lax.dot_general(a.astype(BF16), b.astype(BF16), (contract, ((), ())), preferred_element_type=F32)


_NN = ((1,), (0,))
_NT = ((1,), (1,))
_TN = ((0,), (0,))


@jax.custom_vjp
def mm_nn(a, b):
    return _dot(a, b, _NN)


mm_nn.defvjp(lambda a, b: (_dot(a, b, _NN), (a, b)),
             lambda res, g: (_dot(g, res[1], _NT), _dot(res[0], g, _TN)))


@jax.custom_vjp
def mm_nt(a, b):
    return _dot(a, b, _NT)


mm_nt.defvjp(lambda a, b: (_dot(a, b, _NT), (a, b)),
             lambda res, g: (_dot(g, res[1], _NN), _dot(g, res[0], _TN)))


@jax.custom_vjp
def mm_tn(a, b):
    return _dot(a, b, _TN)


mm_tn.defvjp(lambda a, b: (_dot(a, b, _TN), (a, b)),
             lambda res, g: (_dot(res[1], g, _NT), _dot(res[0], g, _NN)))


def _tri_dot(x, lower):
    n = x.shape[0]
    r = lax.broadcasted_iota(jnp.int32, (n, n), 0)
    c = lax.broadcasted_iota(jnp.int32, (n, n), 1)
    tri = ((c <= r) if lower else (c >= r)).astype(BF16)
    hi = x.astype(BF16)
    rest = x - hi.astype(F32)
    mid = rest.astype(BF16)
    lo = (rest - mid.astype(F32)).astype(BF16)
    dot = lambda p: lax.dot_general(tri, p, (_NN, ((), ())), preferred_element_type=F32)
    return (dot(lo) + dot(mid)) + dot(hi)


@jax.custom_vjp
def cumsum_rows(x):
    return _tri_dot(x, True)


cumsum_rows.defvjp(lambda x: (cumsum_rows(x), None), lambda _, g: (_tri_dot(g, False),))


@functools.partial(jax.custom_vjp, nondiff_argnums=(1,))
def roll_sub(x, d):
    return pltpu.roll(x, d, 1) if d else x


roll_sub.defvjp(lambda x, d: (roll_sub(x, d), None),
                lambda d, _, g: (roll_sub(g, (DIAG - d) % DIAG),))


def sigmoid(x):
    return 1.0 / (1.0 + jnp.exp(-x))


@jax.custom_vjp
def silu(x):
    return x * sigmoid(x)


def _silu_fwd(x):
    s = sigmoid(x)
    return x * s, (x, s)


silu.defvjp(_silu_fwd, lambda res, g: (g * (res[1] * (1.0 + res[0] * (1.0 - res[1]))),))


F32_TINY = 1.17549435e-38


def sigmoid_pair(x):
    e = jnp.exp(-jnp.abs(x))
    r = 1.0 / (1.0 + e)
    er = e * r
    pos = x >= 0.0
    return jnp.where(pos, r, er), jnp.where(pos, er, r)


def _forget_fwd(x, a):
    lb, one_m_lb = sigmoid_pair(a)
    sp, sn = sigmoid_pair(x)
    f = lb + one_m_lb * sp
    k = one_m_lb * sn
    return (jnp.log(jnp.maximum(f, F32_TINY)), k), (sp, sn, f, k, lb, one_m_lb)


def _forget_bwd(res, g):
    sp, sn, f, k, lb, one_m_lb = res
    g_lf, g_k = g
    t = jnp.where(f >= F32_TINY, g_lf / jnp.maximum(f, F32_TINY), 0.0) - g_k
    return (k * sp) * t, jnp.sum(sn * t, axis=0, keepdims=True) * (lb * one_m_lb)


@jax.custom_vjp
def forget_gate(x, a):
    return _forget_fwd(x, a)[0]


forget_gate.defvjp(_forget_fwd, _forget_bwd)


@jax.custom_vjp
def decayed(x, e):
    return (x * jnp.exp(e)).astype(BF16).astype(F32)


def _decayed_fwd(x, e):
    y = decayed(x, e)
    return y, (y, e)


decayed.defvjp(_decayed_fwd, lambda res, g: (g * jnp.exp(res[1]), g * res[0]))


def _row(x, r):
    shape = x.shape

    @jax.custom_vjp
    def take(x):
        return x[r:r + 1, :]

    take.defvjp(lambda x: (x[r:r + 1, :], None),
                lambda _, g: (jnp.where(lax.broadcasted_iota(jnp.int32, shape, 0) == r, g, 0.0),))
    return take(x)


def _rms(x):
    return lax.rsqrt(jnp.mean(x * x, axis=-1, keepdims=True) + NORM_EPS)


def _attn_group(qs, k_a, v_a, k_b, v_b, zs, sink_a, sink_b, bias, at_sink=None):
    def half(kh, vh, sink):
        s = mm_nn(qs, kh) + bias
        if at_sink is None:
            m = jnp.maximum(jnp.max(s, axis=-1, keepdims=True), jnp.max(sink, axis=-1, keepdims=True))
            p = jnp.exp(s - lax.stop_gradient(m))
            own = jnp.sum(jnp.exp(sink - lax.stop_gradient(m)), axis=-1, keepdims=True) * (1.0 / LANES)
            return mm_nt(p * (1.0 / (jnp.sum(p, axis=-1, keepdims=True) + own)), vh)
        s = jnp.where(at_sink, jnp.concatenate([sink, sink], axis=1), s)
        p = jnp.exp(s - jnp.max(s, axis=-1, keepdims=True))
        return mm_nt(jnp.where(at_sink, 0.0, p), vh) * (1.0 / jnp.sum(p, axis=-1, keepdims=True))

    return (half(k_a, v_a, sink_a) + half(k_b, v_b, sink_b)) * silu(zs)


SAFE_RANGE = 80.0


def _rec_front(qr, fr, l0, l1):
    lf, k = forget_gate(fr, l1 - l0)
    return silu(qr), k, lf


def _rec_tail(o, z, gw):
    return o * _rms(o) * gw * silu(z)


def _rec_margin(b):
    R = b.shape[0]
    mid, last = _row(b, R // 2 - 1), _row(b, R - 1)
    return jnp.minimum(mid, last - mid)


def _heads(x):
    w = x.shape[1] // REC_HEADS
    return [x[:, h * w:(h + 1) * w] for h in range(REC_HEADS)]


def _hdot(a, b, contract):
    return jnp.concatenate([_dot(ah, bh, contract) for ah, bh in zip(_heads(a), _heads(b))], axis=1)


@jax.custom_vjp
def hmm_nn(a, b):
    return _hdot(a, b, _NN)


hmm_nn.defvjp(lambda a, b: (_hdot(a, b, _NN), (a, b)),
              lambda res, g: (_hdot(g, res[1], _NT), _hdot(res[0], g, _TN)))


@jax.custom_vjp
def hmm_nt(a, b):
    return _hdot(a, b, _NT)


hmm_nt.defvjp(lambda a, b: (_hdot(a, b, _NT), (a, b)),
              lambda res, g: (_hdot(g, res[1], _NN), _hdot(g, res[0], _TN)))


@jax.custom_vjp
def hmm_tn(a, b):
    return _hdot(a, b, _TN)


hmm_tn.defvjp(lambda a, b: (_hdot(a, b, _TN), (a, b)),
              lambda res, g: (_hdot(res[1], g, _NT), _hdot(res[0], g, _NN)))


def _head_sums(x):
    return jnp.concatenate([jnp.broadcast_to(jnp.sum(xh, axis=-1, keepdims=True), xh.shape) for xh in _heads(x)],
                           axis=1)


@jax.custom_vjp
def head_sum(x):
    return _head_sums(x)


head_sum.defvjp(lambda x: (_head_sums(x), None), lambda _, g: (_head_sums(g),))


def _rec_cores_fast(q, k, v, b, S):
    R = q.shape[0]
    ri = lax.broadcasted_iota(jnp.int32, (R, REC_HEADS * R), 0)
    ci = lax.broadcasted_iota(jnp.int32, (R, REC_HEADS * R), 1) % R
    d = b - _row(b, R // 2 - 1)
    sc = jnp.where(ci < ri, hmm_nt(decayed(q, d), decayed(k, -d)), 0.0)
    o = hmm_nt(q * jnp.exp(b), S) + hmm_nn(sc, v) + head_sum(q * k) * v
    b_last = _row(b, R - 1)
    return o, S * jnp.exp(b_last) + hmm_tn(v, k * jnp.exp(b_last - b))


def _rec_tails(o, z, gw):
    return o * lax.rsqrt(head_sum(o * o) * (1.0 / REC_DIM) + NORM_EPS) * gw * silu(z)


def _rec_block_fast(qr, fr, v, z, S, l0, l1, gw):
    lf, k = forget_gate(fr, l1 - l0)
    o, S_new = _rec_cores_fast(silu(qr), k, v, cumsum_rows(lf), S)
    return _rec_tails(o, z, gw), S_new


def _rec_core_slow(q, k, v, b, S):
    R = q.shape[0]
    rows = lax.broadcasted_iota(jnp.int32, (R, REC_DIM), 0)

    o = mm_nt(q * jnp.exp(jnp.minimum(b, 0.0)), S)

    ri = lax.broadcasted_iota(jnp.int32, (R, R), 0)
    ci = lax.broadcasted_iota(jnp.int32, (R, R), 1)
    sc = jnp.zeros((R, R), F32)
    w = R
    while w > DIAG:
        h = w // 2
        b3 = b.reshape(R // w, w, REC_DIM)
        rin = lax.broadcasted_iota(jnp.int32, (R // w, w, REC_DIM), 1)
        mid = jnp.sum(jnp.where(rin == h - 1, b3, 0.0), axis=1, keepdims=True)
        fac = jnp.exp(jnp.minimum(jnp.where(rin >= h, b3 - mid, mid - b3), 0.0)).reshape(R, REC_DIM)
        upper = (rows % w) >= h
        s_w = mm_nt(jnp.where(upper, q * fac, 0.0), jnp.where(upper, 0.0, k * fac))
        sc = sc + jnp.where((ri // w) == (ci // w), s_w, 0.0)
        w = h
    o = o + mm_nn(sc, v)

    g = R // DIAG
    q3, k3, v3, b3 = (t.reshape(g, DIAG, REC_DIM) for t in (q, k, v, b))
    rin = lax.broadcasted_iota(jnp.int32, (g, DIAG, 1), 1)
    od = jnp.zeros((g, DIAG, REC_DIM), F32)
    for d in range(DIAG):
        e = jnp.exp(jnp.minimum(b3 - roll_sub(b3, d), 0.0))
        sd = jnp.sum(q3 * roll_sub(k3, d) * e, axis=-1, keepdims=True)
        od = od + jnp.where(rin >= d, sd, 0.0) * roll_sub(v3, d)
    o = o + od.reshape(R, REC_DIM)

    b_last = _row(b, R - 1)
    return o, S * jnp.exp(jnp.minimum(b_last, 0.0)) + mm_tn(v, k * jnp.exp(jnp.minimum(b_last - b, 0.0)))


def _rec_head(core, qr, fr, v, z, S, l0, l1, gw):
    q, k, lf = _rec_front(qr, fr, l0, l1)
    o, S_new = core(q, k, v, cumsum_rows(lf), S)
    return _rec_tail(o, z, gw), S_new


ANGLE_COLS = 3 * ROPE_DIM


def _rope_angles(positions):
    half = ROPE_DIM // 2
    inv_freq = ROPE_THETA ** (-(jnp.arange(half, dtype=F32) * 2.0 / ROPE_DIM))
    ang = positions.astype(F32).reshape(-1, 1) * inv_freq
    cs = jnp.concatenate([jnp.cos(ang), jnp.sin(ang)], axis=-1)
    hi = cs.astype(BF16)
    rest = cs - hi.astype(F32)
    mid = rest.astype(BF16)
    return jnp.concatenate([hi, mid, (rest - mid.astype(F32)).astype(BF16)], axis=-1)


def _rope_tables(pieces):
    half = ROPE_DIM // 2
    r = lax.broadcasted_iota(jnp.int32, (ANGLE_COLS, 3 * LANES), 0) % ROPE_DIM
    c = lax.broadcasted_iota(jnp.int32, (ANGLE_COLS, 3 * LANES), 1)
    table, j = c // LANES, c % HEAD_DIM
    angle, low = j % half, j < half
    plus = ((table == 0) & (j < ROPE_DIM) & (r == angle)) | ((table == 1) & (j >= half) & (j < ROPE_DIM)
                                                                & (r == half + angle))
    minus = (table == 2) & low & (r == half + angle)
    pick = jnp.where(plus, 1.0, jnp.where(minus, -1.0, 0.0)).astype(BF16)
    out = jnp.dot(pieces, pick, preferred_element_type=F32)
    lane = lax.broadcasted_iota(jnp.int32, (1, LANES), 1) % HEAD_DIM
    return out[:, :LANES] + jnp.where(lane < ROPE_DIM, 0.0, 1.0), out[:, LANES:2 * LANES], out[:, 2 * LANES:]


def _rope(x, cos_t, sin_a, sin_b):
    half = ROPE_DIM // 2
    return x * cos_t + pltpu.roll(x, half, 1) * sin_a + pltpu.roll(x, LANES - half, 1) * sin_b


def _rope_transposed(g, cos_t, sin_a, sin_b):
    half = ROPE_DIM // 2
    return g * cos_t + pltpu.roll(g * sin_a, LANES - half, 1) + pltpu.roll(g * sin_b, half, 1)


def _row_spec(tm, width):
    return pl.BlockSpec((tm, width), lambda i: (i, 0))


def _weight_spec(shape):
    return pl.BlockSpec(shape, lambda *_: (0,) * len(shape), pipeline_mode=pl.Buffered(1))


def _full_spec(shape):
    return pl.BlockSpec(shape, lambda *_: (0,) * len(shape))


def attn_in_proj(x, w_pre, w_in, b_in, angles, tm=1024, to_bf16=()):
    n = x.shape[0]
    tm = min(tm, n)
    nc = len(to_bf16)

    def body(*refs):
        x_ref, wp_ref, w_ref, b_ref, cs_ref = refs[:5]
        h_ref, q_ref, k_ref, v_ref, z_ref = refs[5 + nc:10 + nc]

        @pl.when(pl.program_id(0) == 0)
        def _():
            for src, dst in zip(refs[5:5 + nc], refs[10 + nc:]):
                dst[...] = src[...].astype(BF16)

        xv = x_ref[...]
        h = (xv * _rms(xv) * wp_ref[...]).astype(BF16)
        h_ref[...] = h
        proj = jnp.dot(h, w_ref[...], preferred_element_type=F32) + b_ref[...]
        tabs = _rope_tables(cs_ref[...])
        for s in range(D_MODEL // LANES):
            sl = slice(s * LANES, (s + 1) * LANES)
            q_ref[:, sl] = _rope(proj[:, sl] * (HEAD_DIM ** -0.5), *tabs).astype(BF16)
        k_ref[...] = _rope(proj[:, D_MODEL:D_MODEL + KV_WIDTH], *tabs).astype(BF16)
        v_ref[...] = proj[:, D_MODEL + KV_WIDTH:D_MODEL + 2 * KV_WIDTH].astype(BF16)
        z_ref[...] = proj[:, D_MODEL + 2 * KV_WIDTH:]

    out = pl.pallas_call(
        body, name="attn_in_proj", grid=(n // tm,),
        in_specs=[_row_spec(tm, D_MODEL), _full_spec((1, D_MODEL)), _weight_spec((D_MODEL, ATTN_IN)),
                  _full_spec((1, ATTN_IN)), _row_spec(tm, ANGLE_COLS)] + [_weight_spec(a.shape) for a in to_bf16],
        out_specs=[_row_spec(tm, D_MODEL), _row_spec(tm, D_MODEL), _row_spec(tm, KV_WIDTH),
                   _row_spec(tm, KV_WIDTH), _row_spec(tm, D_MODEL)] + [_full_spec(a.shape) for a in to_bf16],
        out_shape=[jax.ShapeDtypeStruct((n, D_MODEL), BF16), jax.ShapeDtypeStruct((n, D_MODEL), BF16),
                   jax.ShapeDtypeStruct((n, KV_WIDTH), BF16), jax.ShapeDtypeStruct((n, KV_WIDTH), BF16),
                   jax.ShapeDtypeStruct((n, D_MODEL), F32)] + [jax.ShapeDtypeStruct(a.shape, BF16) for a in to_bf16],
        compiler_params=_cparams(1),
    )(x, w_pre, w_in, b_in, angles, *to_bf16)
    return out[:5], out[5:]


def _column_blocks(w):
    if len(w.shape) == 2:
        return [slice(0, w.shape[1])], lambda ref, s: ref[...]
    width = w.shape[2]
    return [slice(s * width, (s + 1) * width) for s in range(w.shape[0])], lambda ref, s: ref[s]


def rec_in_proj(x, w_pre, w_in, tm=1024):
    n = x.shape[0]
    tm = min(tm, n)
    columns, block = _column_blocks(w_in)

    def body(x_ref, wp_ref, w_ref, h_ref, p_ref):
        xv = x_ref[...]
        h = (xv * _rms(xv) * wp_ref[...]).astype(BF16)
        h_ref[...] = h
        for s, cols in enumerate(columns):
            p_ref[:, cols] = jnp.dot(h, block(w_ref, s), preferred_element_type=F32)

    return pl.pallas_call(
        body, name="rec_in_proj", grid=(n // tm,),
        in_specs=[_row_spec(tm, D_MODEL), _full_spec((1, D_MODEL)), _weight_spec(w_in.shape)],
        out_specs=[_row_spec(tm, D_MODEL), _row_spec(tm, REC_IN)],
        out_shape=[jax.ShapeDtypeStruct((n, D_MODEL), BF16), jax.ShapeDtypeStruct((n, REC_IN), F32)],
        compiler_params=_cparams(1),
    )(x, w_pre, w_in)


def out_proj(og, w_out, b_out, x_res, w_post, tm=1024):
    n = og.shape[0]
    tm = min(tm, n)

    def body(og_ref, w_ref, b_ref, x_ref, wp_ref, y_ref, xo_ref):
        y = jnp.dot(og_ref[...], w_ref[...], preferred_element_type=F32) + b_ref[...]
        y_ref[...] = y.astype(BF16)
        xo_ref[...] = x_ref[...] + y * _rms(y) * wp_ref[...]

    return pl.pallas_call(
        body, name="out_proj", grid=(n // tm,),
        in_specs=[_row_spec(tm, D_MODEL), _weight_spec((D_MODEL, D_MODEL)), _full_spec((1, D_MODEL)),
                  _row_spec(tm, D_MODEL), _full_spec((1, D_MODEL))],
        out_specs=[_row_spec(tm, D_MODEL), _row_spec(tm, D_MODEL)],
        out_shape=[jax.ShapeDtypeStruct((n, D_MODEL), BF16), jax.ShapeDtypeStruct((n, D_MODEL), F32)],
        compiler_params=_cparams(1),
    )(og, w_out, b_out, x_res, w_post)


def _post_norm_bwd(g, y, w_post):
    rstd = _rms(y)
    yn = y * rstd
    gw = g * w_post
    return rstd * (gw - yn * jnp.mean(gw * yn, axis=-1, keepdims=True)), jnp.sum(g * yn, axis=0, keepdims=True)


def out_proj_loss_bwd(og, w_out, x_res, w_post, target, tm=1024):
    n = og.shape[0]
    tm = min(tm, n)
    steps = n // tm

    def body(og_ref, w_ref, x_ref, wp_ref, t_ref, dx_ref, l_ref, dog_ref, dw_ref, dwp_ref, dwb_ref):
        @pl.when(pl.program_id(0) == 0)
        def _():
            l_ref[...] = jnp.zeros_like(l_ref)
            dw_ref[...] = jnp.zeros_like(dw_ref)
            dwp_ref[...] = jnp.zeros_like(dwp_ref)

        og_tile = og_ref[...]
        y = jnp.dot(og_tile, w_ref[...], preferred_element_type=F32)
        err = x_ref[...] + y * _rms(y) * wp_ref[...] - t_ref[...]
        g = err * (1.0 / D_MODEL)
        dx_ref[...] = g
        l_ref[...] += jnp.sum(err * err, axis=0, keepdims=True)
        dy, dwp = _post_norm_bwd(g, y, wp_ref[...])
        dwp_ref[...] += dwp
        dyb = dy.astype(BF16)
        dog_ref[...] = _dot(dyb, w_ref[...], _NT).astype(BF16)
        dw_ref[...] += _dot(og_tile, dyb, _TN)

        @pl.when(pl.program_id(0) == steps - 1)
        def _():
            dwb_ref[...] = dw_ref[...].astype(BF16)

    return pl.pallas_call(
        body, name="out_proj_loss_bwd", grid=(steps,),
        in_specs=[_row_spec(tm, D_MODEL), _weight_spec((D_MODEL, D_MODEL)), _row_spec(tm, D_MODEL),
                  _full_spec((1, D_MODEL)), _row_spec(tm, D_MODEL)],
        out_specs=[_row_spec(tm, D_MODEL), _full_spec((1, D_MODEL)), _row_spec(tm, D_MODEL),
                   _full_spec((D_MODEL, D_MODEL)), _full_spec((1, D_MODEL)), _full_spec((D_MODEL, D_MODEL))],
        out_shape=[jax.ShapeDtypeStruct((n, D_MODEL), F32), jax.ShapeDtypeStruct((1, D_MODEL), F32),
                   jax.ShapeDtypeStruct((n, D_MODEL), BF16), jax.ShapeDtypeStruct((D_MODEL, D_MODEL), F32),
                   jax.ShapeDtypeStruct((1, D_MODEL), F32), jax.ShapeDtypeStruct((D_MODEL, D_MODEL), BF16)],
        compiler_params=_cparams(1),
    )(og, w_out, x_res, w_post, target)


def out_proj_bwd(dxo, y, og, w_out, w_post, tm=1024):
    n = og.shape[0]
    tm = min(tm, n)
    steps = n // tm

    def body(g_ref, y_ref, og_ref, w_ref, wp_ref, dog_ref, dw_ref, db_ref, dwp_ref, dwb_ref):
        @pl.when(pl.program_id(0) == 0)
        def _():
            dw_ref[...] = jnp.zeros_like(dw_ref)
            db_ref[...] = jnp.zeros_like(db_ref)
            dwp_ref[...] = jnp.zeros_like(dwp_ref)

        dy, dwp = _post_norm_bwd(g_ref[...], y_ref[...].astype(F32), wp_ref[...])
        dwp_ref[...] += dwp
        db_ref[...] += jnp.sum(dy, axis=0, keepdims=True)
        dyb = dy.astype(BF16)
        dog_ref[...] = _dot(dyb, w_ref[...], _NT).astype(BF16)
        dw_ref[...] += _dot(og_ref[...], dyb, _TN)

        @pl.when(pl.program_id(0) == steps - 1)
        def _():
            dwb_ref[...] = dw_ref[...].astype(BF16)

    return pl.pallas_call(
        body, name="out_proj_bwd", grid=(steps,),
        in_specs=[_row_spec(tm, D_MODEL), _row_spec(tm, D_MODEL), _row_spec(tm, D_MODEL),
                  _weight_spec((D_MODEL, D_MODEL)), _full_spec((1, D_MODEL))],
        out_specs=[_row_spec(tm, D_MODEL), _full_spec((D_MODEL, D_MODEL)), _full_spec((1, D_MODEL)),
                   _full_spec((1, D_MODEL)), _full_spec((D_MODEL, D_MODEL))],
        out_shape=[jax.ShapeDtypeStruct((n, D_MODEL), BF16), jax.ShapeDtypeStruct((D_MODEL, D_MODEL), F32),
                   jax.ShapeDtypeStruct((1, D_MODEL), F32), jax.ShapeDtypeStruct((1, D_MODEL), F32),
                   jax.ShapeDtypeStruct((D_MODEL, D_MODEL), BF16)],
        compiler_params=_cparams(1),
    )(dxo, y, og, w_out, w_post)


def _slot_sum_specs(slot_sums, steps):
    return ([pl.BlockSpec((a.shape[0], a.shape[1] // steps, a.shape[2]), lambda i: (0, i, 0)) for a in slot_sums],
            [_row_spec(a.shape[1] // steps, a.shape[2]) for a in slot_sums],
            [jax.ShapeDtypeStruct(a.shape[1:], F32) for a in slot_sums])


def _sum_slots_into(slot_refs, sum_refs):
    for slots_ref, sum_ref in zip(slot_refs, sum_refs):
        sum_ref[...] = functools.reduce(jnp.add, [slots_ref[t].astype(F32) for t in range(slots_ref.shape[0])])


def in_proj_bwd_x(dproj, w_in, x, w_pre, dxo, tm=1024, scatter=(), to_sibling=()):
    n, p = dproj.shape
    tm = min(tm, n)
    steps = n // tm
    ns, nsib = len(scatter), len(to_sibling)
    columns, block = _column_blocks(w_in)

    def body(*refs):
        dp_ref, w_ref, x_ref, wp_ref, g_ref = refs[:5]
        outs = 5 + ns + nsib
        dx_ref, dwp_ref = refs[outs:outs + 2]
        sems = refs[outs + 2 + ns + nsib:]
        exchange = (refs[5:5 + ns], refs[outs + 2:outs + 2 + ns]) + tuple(sems[:3])
        sibling = (refs[5 + ns:outs], refs[outs + 2 + ns:outs + 2 + ns + nsib]) + tuple(sems[3:])

        @pl.when(pl.program_id(0) == 0)
        def _():
            dwp_ref[...] = jnp.zeros_like(dwp_ref)
            if ns:
                _scatter_start(*exchange)
            for cp in _sibling_copies(*sibling) if nsib else ():
                cp.start()

        dh = functools.reduce(jnp.add, [_dot(dp_ref[:, cols], block(w_ref, s), _NT)
                                        for s, cols in enumerate(columns)])
        xv = x_ref[...]
        rstd = _rms(xv)
        xn = xv * rstd
        gw = dh * wp_ref[...]
        dwp_ref[...] += jnp.sum(dh * xn, axis=0, keepdims=True)
        dx_ref[...] = rstd * (gw - xn * jnp.mean(gw * xn, axis=-1, keepdims=True)) + g_ref[...]

        if ns or nsib:
            @pl.when(pl.program_id(0) == steps - 1)
            def _():
                if ns:
                    _scatter_finish(*exchange)
                for cp in _sibling_copies(*sibling) if nsib else ():
                    cp.wait()

    out = pl.pallas_call(
        body, name=f"in_proj_bwd_x_{p}", grid=(steps,),
        in_specs=[_row_spec(tm, p), _weight_spec(w_in.shape), _row_spec(tm, D_MODEL), _full_spec((1, D_MODEL)),
                  _row_spec(tm, D_MODEL)] + [_ANY] * (ns + nsib),
        out_specs=[_row_spec(tm, D_MODEL), _full_spec((1, D_MODEL))] + [_ANY] * (ns + nsib),
        out_shape=[jax.ShapeDtypeStruct((n, D_MODEL), F32), jax.ShapeDtypeStruct((1, D_MODEL), F32)]
        + [jax.ShapeDtypeStruct(a.shape, a.dtype) for a in tuple(scatter) + tuple(to_sibling)],
        scratch_shapes=(_scatter_sems(ns) if ns else []) + (_sibling_sems(nsib) if nsib else []),
        compiler_params=_cparams(1),
    )(dproj, w_in, x, w_pre, dxo, *scatter, *to_sibling)
    return out[0], out[1], out[2:2 + ns], out[2 + ns:]


def in_proj_bwd_w(h, dproj, tm=1024, as_shards=False, kv=None, slot_sums=()):
    n, p = dproj.shape
    chunk = p // (4 if p % 4096 == 0 else 3)
    tm = min(tm, n)
    steps = n // tm
    shard = p // N_CHIPS
    n_kv = 0 if kv is None else 3
    n_sum = len(slot_sums)
    kv_from, kv_to = D_MODEL, D_MODEL + 2 * KV_WIDTH

    def body(*refs):
        h_ref, dp_ref = refs[:2]
        outs = 2 + n_kv + n_sum
        dw_ref, db_ref = refs[outs:outs + 2]
        sums_at = outs + 2 + (kv is not None)
        scratch = refs[sums_at + n_sum:]
        acc_scr, sem, staging = scratch[0], scratch[1], scratch[2:]
        i = pl.program_id(0)
        _sum_slots_into(refs[2 + n_kv:outs], refs[sums_at:sums_at + n_sum])

        @pl.when(i == 0)
        def _():
            acc_scr[...] = jnp.zeros_like(acc_scr)
            db_ref[...] = jnp.zeros_like(db_ref)

        if kv is not None:
            dk_ref, dv_ref, cs_ref, kv_ref = refs[2], refs[3], refs[4], refs[outs + 2]
            made = jnp.concatenate([_rope_transposed(dk_ref[...].T, *_rope_tables(cs_ref[...])), dv_ref[...].T],
                                   axis=1).astype(BF16)
            kv_ref[...] = made

        def columns(c0):
            if kv is None or c0 + chunk <= kv_from or c0 >= kv_to:
                return dp_ref[:, c0:c0 + chunk]
            return jnp.concatenate([dp_ref[:, c0:kv_from], made, dp_ref[:, kv_to:c0 + chunk]], axis=1)

        ht = h_ref[...].T
        for c0 in range(0, p, chunk):
            dp = columns(c0)
            acc_scr[:, c0:c0 + chunk] += jnp.dot(ht, dp, preferred_element_type=F32)
            db_ref[:, c0:c0 + chunk] += jnp.sum(dp.astype(F32), axis=0, keepdims=True)

        @pl.when(i == steps - 1)
        def _():
            if as_shards:
                for s in range(N_CHIPS):
                    staging[0][...] = acc_scr[:, s * shard:(s + 1) * shard].astype(BF16)
                    out = pltpu.make_async_copy(staging[0], dw_ref.at[s], sem)
                    out.start()
                    out.wait()
            else:
                out = pltpu.make_async_copy(acc_scr, dw_ref, sem)
                out.start()
                out.wait()

    dw_shape = jax.ShapeDtypeStruct((N_CHIPS, D_MODEL, shard), BF16) if as_shards else (
        jax.ShapeDtypeStruct((D_MODEL, p), F32))
    in_specs = [_row_spec(tm, D_MODEL), _row_spec(tm, p)]
    out_specs = [_ANY, _full_spec((1, p))]
    out_shape = [dw_shape, jax.ShapeDtypeStruct((1, p), F32)]
    if kv is not None:
        columns_t = pl.BlockSpec((KV_WIDTH, tm), lambda i: (0, i))
        in_specs += [columns_t, columns_t, _row_spec(tm, ANGLE_COLS)]
        out_specs.append(pl.BlockSpec((tm, kv_to - kv_from), lambda i: (i, kv_from // (kv_to - kv_from))))
        out_shape.append(jax.ShapeDtypeStruct(dproj.shape, dproj.dtype))
    sum_in, sum_out, sum_shapes = _slot_sum_specs(slot_sums, steps)
    in_specs, out_specs, out_shape = in_specs + sum_in, out_specs + sum_out, out_shape + sum_shapes
    return pl.pallas_call(
        body, name=f"in_proj_bwd_w_{p}", grid=(steps,),
        in_specs=in_specs, out_specs=out_specs, out_shape=out_shape,
        scratch_shapes=[pltpu.VMEM((D_MODEL, p), F32), pltpu.SemaphoreType.DMA]
        + ([pltpu.VMEM((D_MODEL, shard), BF16)] if as_shards else []),
        input_output_aliases={1: 2} if kv is not None else {},
        compiler_params=_cparams(1),
    )(h, dproj, *(kv or ()), *slot_sums)


PAIRS = GROUP // 2
GROUP_ROWS = PAIRS * ATTN_BLOCK
MASKED = -1e30


def _kv_windows(k_ref, v_ref, i):
    ps = pl.multiple_of(jnp.maximum(i - 1, 0) * ATTN_BLOCK, ATTN_BLOCK)
    cs = pl.multiple_of(i * ATTN_BLOCK, ATTN_BLOCK)
    kw = jnp.concatenate([k_ref[pl.ds(ps, ATTN_BLOCK), :], k_ref[pl.ds(cs, ATTN_BLOCK), :]], axis=0)
    vw = jnp.concatenate([v_ref[pl.ds(ps, ATTN_BLOCK), :], v_ref[pl.ds(cs, ATTN_BLOCK), :]], axis=0)
    return kw.astype(F32).T, vw.astype(F32).T, ps, cs


def _low_rows(shape):
    return lax.broadcasted_iota(jnp.int32, shape, 0) < HEAD_DIM


def _spread(w, kvh):
    low = _low_rows(w.shape)
    swapped = pltpu.roll(w, HEAD_DIM, 0)
    if kvh == 0:
        return jnp.where(low, w, 0.0), jnp.where(low, 0.0, swapped)
    return jnp.where(low, swapped, 0.0), jnp.where(low, 0.0, w)


def _unspread(d_a, d_b, kvh):
    low = _low_rows(d_a.shape)
    if kvh == 0:
        return jnp.where(low, d_a + pltpu.roll(d_b, HEAD_DIM, 0), 0.0)
    return jnp.where(low, 0.0, pltpu.roll(d_a, HEAD_DIM, 0) + d_b)


def _stack_pairs(ref, kvh):
    return jnp.concatenate([ref[:, (kvh * PAIRS + j) * LANES:(kvh * PAIRS + j + 1) * LANES] for j in range(PAIRS)],
                           axis=0)


def _fill_bias(bias_scr):
    shape = (GROUP_ROWS, 2 * ATTN_BLOCK)
    r = lax.broadcasted_iota(jnp.int32, shape, 0) % ATTN_BLOCK
    c = lax.broadcasted_iota(jnp.int32, shape, 1)
    in_cur = (c >= ATTN_BLOCK) & ((c - ATTN_BLOCK) <= r)
    in_prev = (c < ATTN_BLOCK) & (c > r)
    bias_scr[0] = jnp.where(in_cur, 0.0, MASKED)
    bias_scr[1] = jnp.where(in_cur | in_prev, 0.0, MASKED)
    bias_scr[2] = jnp.where(c == r, 1.0, 0.0)


N_BIAS_TABLES = 3


def _sink_table(sinks):
    t = jnp.transpose(sinks.reshape(N_KV_HEADS, PAIRS, 2), (0, 2, 1))
    return jnp.broadcast_to(t[:, :, :, None, None], (N_KV_HEADS, 2, PAIRS, ATTN_BLOCK, LANES)).reshape(
        N_KV_HEADS, 2, GROUP_ROWS, LANES)


def attn_fwd(q, k, v, z, sink_tab, batch, seq, gather=()):
    nb = seq // ATTN_BLOCK
    ng = len(gather)

    def body(*refs):
        q_ref, k_ref, v_ref, z_ref, s_ref = refs[:5]
        og_ref, bias_scr = refs[5 + ng], refs[6 + 2 * ng]
        exchange = (refs[5:5 + ng], refs[6 + ng:6 + 2 * ng]) + tuple(refs[7 + 2 * ng:])
        b, i = pl.program_id(0), pl.program_id(1)

        @pl.when((b == 0) & (i == 0))
        def _():
            _fill_bias(bias_scr)
            if ng:
                _gather_start(*exchange)

        kw, vw, _, _ = _kv_windows(k_ref, v_ref, i)
        bias, at_sink = bias_scr[jnp.minimum(i, 1)], bias_scr[2] > 0.5
        for kvh in range(N_KV_HEADS):
            k_a, k_b = _spread(kw, kvh)
            v_a, v_b = _spread(vw, kvh)
            og = _attn_group(_stack_pairs(q_ref, kvh), k_a, v_a, k_b, v_b, _stack_pairs(z_ref, kvh),
                             s_ref[kvh, 0], s_ref[kvh, 1], bias, at_sink)
            for j in range(PAIRS):
                og_ref[:, (kvh * PAIRS + j) * LANES:(kvh * PAIRS + j + 1) * LANES] = (
                    og[j * ATTN_BLOCK:(j + 1) * ATTN_BLOCK].astype(BF16))

        if ng:
            @pl.when((b == batch - 1) & (i == nb - 1))
            def _():
                _gather_finish(*exchange)

    blk = lambda w: pl.BlockSpec((ATTN_BLOCK, w), lambda b, i: (b * nb + i, 0))
    seq_spec = pl.BlockSpec((seq, KV_WIDTH), lambda b, i: (b, 0))
    out = pl.pallas_call(
        body, name="attn_fwd", grid=(batch, nb),
        in_specs=[blk(D_MODEL), seq_spec, seq_spec, blk(D_MODEL), _full_spec(sink_tab.shape)] + [_ANY] * ng,
        out_specs=[blk(D_MODEL)] + [_ANY] * ng,
        out_shape=[jax.ShapeDtypeStruct((batch * seq, D_MODEL), BF16)]
        + [jax.ShapeDtypeStruct((N_CHIPS,) + a.shape, a.dtype) for a in gather],
        scratch_shapes=[pltpu.VMEM((N_BIAS_TABLES, GROUP_ROWS, 2 * ATTN_BLOCK), F32)] + (_gather_sems(ng) if ng else []),
        compiler_params=_cparams(2),
    )(q, k, v, z, sink_tab, *gather)
    return out[0], out[1:]


def attn_bwd(q, k, v, z, sink_tab, dog, angles, batch, seq, scatter=()):
    nb = seq // ATTN_BLOCK
    ns = len(scatter)

    def body(*refs):
        q_ref, k_ref, v_ref, z_ref, s_ref, g_ref, cs_ref = refs[:7]
        dp_ref, dk_ref, dv_ref, ds_ref = refs[7 + ns:11 + ns]
        bias_scr = refs[11 + 2 * ns]
        exchange = (refs[7:7 + ns], refs[11 + ns:11 + 2 * ns]) + tuple(refs[12 + 2 * ns:])
        b, i = pl.program_id(0), pl.program_id(1)

        @pl.when((b == 0) & (i == 0))
        def _():
            _fill_bias(bias_scr)
            ds_ref[...] = jnp.zeros_like(ds_ref)
            if ns:
                _scatter_start(*exchange)

        @pl.when(i == 0)
        def _():
            dk_ref[...] = jnp.zeros_like(dk_ref)
            dv_ref[...] = jnp.zeros_like(dv_ref)

        kw, vw, ps, cs = _kv_windows(k_ref, v_ref, i)
        bias = bias_scr[jnp.minimum(i, 1)]
        tabs = _rope_tables(cs_ref[...])
        dkw = jnp.zeros_like(kw)
        dvw = jnp.zeros_like(vw)
        for kvh in range(N_KV_HEADS):
            k_a, k_b = _spread(kw, kvh)
            v_a, v_b = _spread(vw, kvh)
            _, vjp = jax.vjp(functools.partial(_attn_group, bias=bias), _stack_pairs(q_ref, kvh).astype(F32),
                             k_a, v_a, k_b, v_b, _stack_pairs(z_ref, kvh), s_ref[kvh, 0], s_ref[kvh, 1])
            dqs, dk_a, dv_a, dk_b, dv_b, dzs, ds_a, ds_b = vjp(_stack_pairs(g_ref, kvh).astype(F32))
            dkw = dkw + _unspread(dk_a, dk_b, kvh)
            dvw = dvw + _unspread(dv_a, dv_b, kvh)
            ds_ref[kvh, 0] += jnp.sum(ds_a.reshape(PAIRS, ATTN_BLOCK, LANES), axis=1)
            ds_ref[kvh, 1] += jnp.sum(ds_b.reshape(PAIRS, ATTN_BLOCK, LANES), axis=1)
            for j in range(PAIRS):
                rows = slice(j * ATTN_BLOCK, (j + 1) * ATTN_BLOCK)
                col = (kvh * PAIRS + j) * LANES
                dp_ref[:, col:col + LANES] = _rope_transposed(dqs[rows] * (HEAD_DIM ** -0.5), *tabs).astype(BF16)
                zc = D_MODEL + 2 * KV_WIDTH + col
                dp_ref[:, zc:zc + LANES] = dzs[rows].astype(BF16)
        dp_ref[:, D_MODEL:D_MODEL + 2 * KV_WIDTH] = jnp.zeros((ATTN_BLOCK, 2 * KV_WIDTH), BF16)
        dk_ref[:, pl.ds(ps, ATTN_BLOCK)] += dkw[:, :ATTN_BLOCK]
        dk_ref[:, pl.ds(cs, ATTN_BLOCK)] += dkw[:, ATTN_BLOCK:]
        dv_ref[:, pl.ds(ps, ATTN_BLOCK)] += dvw[:, :ATTN_BLOCK]
        dv_ref[:, pl.ds(cs, ATTN_BLOCK)] += dvw[:, ATTN_BLOCK:]

        if ns:
            @pl.when((b == batch - 1) & (i == nb - 1))
            def _():
                _scatter_finish(*exchange)

    blk = lambda w: pl.BlockSpec((ATTN_BLOCK, w), lambda b, i: (b * nb + i, 0))
    seq_spec = pl.BlockSpec((seq, KV_WIDTH), lambda b, i: (b, 0))
    seq_spec_t = pl.BlockSpec((KV_WIDTH, seq), lambda b, i: (0, b))
    n = batch * seq
    ds_shape = (N_KV_HEADS, 2, PAIRS, LANES)
    out = pl.pallas_call(
        body, name="attn_bwd", grid=(batch, nb),
        in_specs=[blk(D_MODEL), seq_spec, seq_spec, blk(D_MODEL), _full_spec(sink_tab.shape), blk(D_MODEL)]
        + [blk(ANGLE_COLS)] + [_ANY] * ns,
        out_specs=[blk(ATTN_IN), seq_spec_t, seq_spec_t, _full_spec(ds_shape)] + [_ANY] * ns,
        out_shape=[jax.ShapeDtypeStruct((n, ATTN_IN), BF16), jax.ShapeDtypeStruct((KV_WIDTH, n), F32),
                   jax.ShapeDtypeStruct((KV_WIDTH, n), F32), jax.ShapeDtypeStruct(ds_shape, F32)]
        + [jax.ShapeDtypeStruct(a.shape, a.dtype) for a in scatter],
        scratch_shapes=[pltpu.VMEM((N_BIAS_TABLES, GROUP_ROWS, 2 * ATTN_BLOCK), F32)] + (_scatter_sems(ns) if ns else []),
        compiler_params=_cparams(2),
    )(q, k, v, z, sink_tab, dog, angles, *scatter)
    return out[0], out[1], out[2], out[3], out[4:]


def rec_fwd(proj, lb_logits, gnorm_w, batch, seq):
    nblk = seq // REC_BLOCK

    def body(p_ref, lb_ref, gw_ref, og_ref, st_ref, safe_ref, s_scr):
        @pl.when(pl.program_id(1) == 0)
        def _():
            s_scr[...] = jnp.zeros_like(s_scr)

        S = s_scr[...]
        st_ref[0] = S
        qr, fr, v, z = (p_ref[:, part * D_MODEL:(part + 1) * D_MODEL] for part in range(4))
        lf, k = forget_gate(fr, lb_ref[1:2, :] - lb_ref[0:1, :])
        q, b = silu(qr), cumsum_rows(lf)
        safe = jnp.min(_rec_margin(b)) >= -SAFE_RANGE

        gate = gw_ref[...] * silu(z)
        safe_ref[0] = jnp.full((REC_HEADS, LANES), safe.astype(F32))

        def store(o, S_new):
            og_ref[...] = (o * lax.rsqrt(head_sum(o * o) * (1.0 / REC_DIM) + NORM_EPS) * gate).astype(BF16)
            s_scr[...] = S_new

        @pl.when(safe)
        def _():
            store(*_rec_cores_fast(q, k, v, b, S))

        @pl.when(jnp.logical_not(safe))
        def _():
            outs = [_rec_core_slow(*args) for args in zip(*(_heads(t) for t in (q, k, v, b, S)))]
            store(*(jnp.concatenate(parts, axis=1) for parts in zip(*outs)))

    blk = lambda w: pl.BlockSpec((REC_BLOCK, w), lambda b, j: (b * nblk + j, 0))
    st_spec = pl.BlockSpec((1, REC_DIM, D_MODEL), lambda b, j: (b * nblk + j, 0, 0))
    safe_spec = pl.BlockSpec((1, REC_HEADS, LANES), lambda b, j: (b * nblk + j, 0, 0))
    return pl.pallas_call(
        body, name="rec_fwd", grid=(batch, nblk),
        in_specs=[blk(REC_IN), _full_spec((2, D_MODEL)), _full_spec((1, D_MODEL))],
        out_specs=[blk(D_MODEL), st_spec, safe_spec],
        out_shape=[jax.ShapeDtypeStruct((batch * seq, D_MODEL), BF16),
                   jax.ShapeDtypeStruct((batch * nblk, REC_DIM, D_MODEL), F32),
                   jax.ShapeDtypeStruct((batch * nblk, REC_HEADS, LANES), F32)],
        scratch_shapes=[pltpu.VMEM((REC_DIM, D_MODEL), F32)],
        compiler_params=_cparams(2),
    )(proj, lb_logits, jnp.tile(gnorm_w, (1, REC_HEADS)))


def rec_bwd(proj, states, safe, lb_logits, gnorm_w, dog, batch, seq):
    nblk = seq // REC_BLOCK

    def body(p_ref, st_ref, safe_ref, lb_ref, gw_ref, g_ref, dp_ref, dlb_ref, dgw_ref, ds_scr):
        @pl.when((pl.program_id(0) == 0) & (pl.program_id(1) == 0))
        def _():
            dlb_ref[...] = jnp.zeros_like(dlb_ref)
            dgw_ref[...] = jnp.zeros_like(dgw_ref)

        @pl.when(pl.program_id(1) == 0)
        def _():
            ds_scr[...] = jnp.zeros_like(ds_scr)

        def load():
            primals = tuple(p_ref[:, part * D_MODEL:(part + 1) * D_MODEL] for part in range(4)) + (
                st_ref[0], lb_ref[0:1, :], lb_ref[1:2, :], gw_ref[...])
            return primals, (g_ref[...].astype(F32), ds_scr[...])

        def store(dqr, dfr, dv, dz, dS, dl0, dl1, dgw):
            for part, val in enumerate((dqr, dfr, dv, dz)):
                dp_ref[:, part * D_MODEL:(part + 1) * D_MODEL] = val.astype(BF16)
            ds_scr[...] = dS
            dlb_ref[0:1, :] += dl0
            dlb_ref[1:2, :] += dl1
            dgw_ref[...] += functools.reduce(jnp.add, _heads(dgw))

        fast = jnp.max(safe_ref[0]) > 0.5

        @pl.when(fast)
        def _():
            primals, cotangents = load()
            store(*jax.vjp(_rec_block_fast, *primals)[1](cotangents))

        @pl.when(jnp.logical_not(fast))
        def _():
            primals, cotangents = load()
            outs = [jax.vjp(functools.partial(_rec_head, _rec_core_slow), *args)[1](cts)
                    for args, cts in zip(zip(*(_heads(t) for t in primals)), zip(*(_heads(t) for t in cotangents)))]
            store(*(jnp.concatenate(parts, axis=1) for parts in zip(*outs)))

    blk = lambda w: pl.BlockSpec((REC_BLOCK, w), lambda b, j: (b * nblk + nblk - 1 - j, 0))
    st_spec = pl.BlockSpec((1, REC_DIM, D_MODEL), lambda b, j: (b * nblk + nblk - 1 - j, 0, 0))
    safe_spec = pl.BlockSpec((1, REC_HEADS, LANES), lambda b, j: (b * nblk + nblk - 1 - j, 0, 0))
    return pl.pallas_call(
        body, name="rec_bwd", grid=(batch, nblk),
        in_specs=[blk(REC_IN), st_spec, safe_spec, _full_spec((2, D_MODEL)), _full_spec((1, D_MODEL)),
                  blk(D_MODEL)],
        out_specs=[blk(REC_IN), _full_spec((2, D_MODEL)), _full_spec((1, REC_DIM))],
        out_shape=[jax.ShapeDtypeStruct((batch * seq, REC_IN), BF16), jax.ShapeDtypeStruct((2, D_MODEL), F32),
                   jax.ShapeDtypeStruct((1, REC_DIM), F32)],
        scratch_shapes=[pltpu.VMEM((REC_DIM, D_MODEL), F32)],
        compiler_params=_cparams(2),
    )(proj, states, safe, lb_logits, jnp.tile(gnorm_w, (1, REC_HEADS)), dog)


_ANY = pl.BlockSpec(memory_space=pl.ANY)


def _chip_peers():
    x, y, c = lax.axis_index("x"), lax.axis_index("y"), lax.axis_index("c")
    peers = []
    for fx, fy in ((1, 0), (0, 1), (1, 1)):
        px, py = (1 - x if fx else x), (1 - y if fy else y)
        peers.append(((px, py, c), 2 * px + py))
    return 2 * x + y, peers


def _remote(src, dst, send_sem, recv_sem, device):
    return pltpu.make_async_remote_copy(src_ref=src, dst_ref=dst, send_sem=send_sem, recv_sem=recv_sem,
                                        device_id=device, device_id_type=MESH)


N_FLIPS = N_CHIPS - 1


def _scatter_sems(n):
    return [pltpu.SemaphoreType.DMA((n * N_FLIPS,)), pltpu.SemaphoreType.DMA((n * N_FLIPS,)),
            pltpu.SemaphoreType.DMA((n,))]


def _scatter_copies(ins, outs, send_sems, recv_sems, local_sems, starting):
    me, peers = _chip_peers()
    local = [pltpu.make_async_copy(ins[k].at[me], outs[k].at[me], local_sems.at[k]) for k in range(len(ins))]
    sends, arrivals = [], []
    for k in range(len(ins)):
        for j, (device, idx) in enumerate(peers):
            sems = (send_sems.at[k * N_FLIPS + j], recv_sems.at[k * N_FLIPS + j], device)
            sends.append(_remote(ins[k].at[idx], outs[k].at[me], *sems))
            if not starting:
                arrivals.append(_remote(ins[k].at[me], outs[k].at[idx], *sems))
    return local, sends, arrivals


def _scatter_start(*refs):
    local, sends, _ = _scatter_copies(*refs, starting=True)
    for cp in local + sends:
        cp.start()


def _scatter_finish(*refs):
    local, sends, arrivals = _scatter_copies(*refs, starting=False)
    for cp in arrivals:
        cp.wait_recv()
    for cp in sends:
        cp.wait_send()
    for cp in local:
        cp.wait()


def _gather_sems(n):
    return [pltpu.SemaphoreType.DMA((n * N_FLIPS,)) for _ in range(4)] + [pltpu.SemaphoreType.DMA((n,))]


def _gather_copies(ins, outs, send_sems, recv_sems, pass_send_sems, pass_recv_sems, local_sems, starting):
    me, peers = _chip_peers()
    c = lax.axis_index("c")
    sibling = (lax.axis_index("x"), lax.axis_index("y"), 1 - c)
    local = [pltpu.make_async_copy(ins[k], outs[k].at[me], local_sems.at[k]) for k in range(len(ins))]
    sends, arrivals, passes, pass_arrivals = [], [], [], []
    for k in range(len(ins)):
        half = ins[k].shape[0] // 2
        mine, other = pl.ds(c * half, half), pl.ds((1 - c) * half, half)
        for j, (device, idx) in enumerate(peers):
            s = k * N_FLIPS + j
            sends.append(_remote(ins[k].at[mine], outs[k].at[me].at[mine], send_sems.at[s], recv_sems.at[s], device))
            if starting:
                continue
            arrived = outs[k].at[idx].at[mine]
            arrivals.append(_remote(ins[k].at[mine], arrived, send_sems.at[s], recv_sems.at[s], device))
            passes.append(_remote(arrived, arrived, pass_send_sems.at[s], pass_recv_sems.at[s], sibling))
            passed = outs[k].at[idx].at[other]
            pass_arrivals.append(_remote(passed, passed, pass_send_sems.at[s], pass_recv_sems.at[s], sibling))
    return local, sends, arrivals, passes, pass_arrivals


def _gather_start(*refs):
    local, sends, _, _, _ = _gather_copies(*refs, starting=True)
    for cp in local + sends:
        cp.start()


def _gather_finish(*refs):
    local, sends, arrivals, passes, pass_arrivals = _gather_copies(*refs, starting=False)
    for arrival, onward in zip(arrivals, passes):
        arrival.wait_recv()
        onward.start()
    for cp in pass_arrivals:
        cp.wait_recv()
    for cp in sends + passes:
        cp.wait_send()
    for cp in local:
        cp.wait()


def chip_gather(arrays):
    n = len(arrays)

    def body(*refs):
        _gather_start(refs[:n], refs[n:2 * n], *refs[2 * n:])
        _gather_finish(refs[:n], refs[n:2 * n], *refs[2 * n:])

    return pl.pallas_call(
        body, name="chip_gather", in_specs=[_ANY] * n, out_specs=[_ANY] * n,
        out_shape=[jax.ShapeDtypeStruct((N_CHIPS,) + a.shape, a.dtype) for a in arrays],
        scratch_shapes=_gather_sems(n),
    )(*arrays)


def _sibling_sems(n):
    return [pltpu.SemaphoreType.DMA((n,)), pltpu.SemaphoreType.DMA((n,))]


def _sibling_copies(ins, outs, send_sems, recv_sems):
    sibling = (lax.axis_index("x"), lax.axis_index("y"), 1 - lax.axis_index("c"))
    return [_remote(ins[k], outs[k], send_sems.at[k], recv_sems.at[k], sibling) for k in range(len(ins))]


def all_gather_small(vec, parts):
    s, r, c_dim = parts.shape
    rows = min(128, r)

    def body(v_ref, parts_ref, out_ref, mine_ref, other_ref, send_sems, recv_sems, local_sem, sib_send, sib_recv):
        x, y, c = lax.axis_index("x"), lax.axis_index("y"), lax.axis_index("c")
        me = 4 * x + 2 * y + c
        local = pltpu.make_async_copy(v_ref, out_ref.at[me], local_sem)
        local.start()
        sends, recvs = [], []
        for j in range(1, N_DEV):
            px = jnp.where(j & 4, 1 - x, x)
            py = jnp.where(j & 2, 1 - y, y)
            pc = jnp.where(j & 1, 1 - c, c)
            common = dict(send_sem=send_sems.at[j - 1], recv_sem=recv_sems.at[j - 1], device_id=(px, py, pc),
                          device_id_type=MESH)
            sends.append(pltpu.make_async_remote_copy(src_ref=v_ref, dst_ref=out_ref.at[me], **common))
            recvs.append(pltpu.make_async_remote_copy(src_ref=v_ref, dst_ref=out_ref.at[4 * px + 2 * py + pc],
                                                      **common))
        for cp in sends:
            cp.start()

        for i in range(r // rows):
            at = pl.ds(i * rows, rows)
            acc = parts_ref[0, at, :].astype(F32)
            for t in range(1, s):
                acc = acc + parts_ref[t, at, :].astype(F32)
            mine_ref[at, :] = acc
        passed = _sibling_copies([mine_ref], [other_ref], sib_send, sib_recv)[0]
        passed.start()

        for cp in recvs:
            cp.wait_recv()
        for cp in sends:
            cp.wait_send()
        local.wait()
        passed.wait()

    vmem = pl.BlockSpec(memory_space=pltpu.VMEM)
    return pl.pallas_call(
        body, name="all_gather_small", in_specs=[_ANY, vmem], out_specs=[_ANY, vmem, _ANY],
        out_shape=[jax.ShapeDtypeStruct((N_DEV,) + vec.shape, vec.dtype), jax.ShapeDtypeStruct((r, c_dim), F32),
                   jax.ShapeDtypeStruct((r, c_dim), F32)],
        scratch_shapes=[pltpu.SemaphoreType.DMA((N_DEV - 1,)), pltpu.SemaphoreType.DMA((N_DEV - 1,)),
                        pltpu.SemaphoreType.DMA] + _sibling_sems(1),
        compiler_params=_cparams(0),
    )(vec, parts)


def sum_slots(stacked, tm=256):
    s, r, c = stacked.shape
    tm = min(tm, r)

    def body(in_ref, out_ref):
        acc = in_ref[0].astype(F32)
        for t in range(1, s):
            acc = acc + in_ref[t].astype(F32)
        out_ref[...] = acc

    return pl.pallas_call(
        body, name=f"sum_slots_{s}_{r}_{c}", grid=(r // tm,),
        in_specs=[pl.BlockSpec((s, tm, c), lambda i: (0, i, 0))], out_specs=_row_spec(tm, c),
        out_shape=jax.ShapeDtypeStruct((r, c), F32), compiler_params=_cparams(1),
    )(stacked)


def adamw(w, m, v, g_a, g_b=None, tm=512):
    r, c = w.shape
    tm = min(tm, r)
    two = g_b is not None
    slots = g_a.shape[0] if g_a.ndim == 3 else 0

    def body(*refs):
        w_ref, m_ref, v_ref, ga_ref = refs[:4]
        g_ref, d_ref, nm_ref, nv_ref = refs[-4:]
        if slots:
            g = ga_ref[0]
            for t in range(1, slots):
                g = g + ga_ref[t]
        else:
            g = ga_ref[...] + refs[4][...] if two else ga_ref[...]
        nm = ADAM_B1 * m_ref[...] + (1.0 - ADAM_B1) * g
        nv = ADAM_B2 * v_ref[...] + (1.0 - ADAM_B2) * (g * g)
        m_hat = nm / (1.0 - ADAM_B1 ** ADAM_STEP)
        v_hat = nv / (1.0 - ADAM_B2 ** ADAM_STEP)
        g_ref[...] = g
        d_ref[...] = -ADAM_LR * (m_hat / (jnp.sqrt(v_hat) + ADAM_EPS) + ADAM_WD * w_ref[...])
        nm_ref[...] = nm
        nv_ref[...] = nv

    args = [w, m, v, g_a] + ([g_b] if two else [])
    in_specs = [_row_spec(tm, c)] * len(args)
    if slots:
        in_specs[3] = pl.BlockSpec((slots, tm, c), lambda i: (0, i, 0))
    return pl.pallas_call(
        body, name=f"adamw_{r}_{c}", grid=(r // tm,),
        in_specs=in_specs, out_specs=[_row_spec(tm, c)] * 4,
        out_shape=[jax.ShapeDtypeStruct((r, c), F32)] * 4, compiler_params=_cparams(1),
    )(*args)


def adamw_shards(items, small, steps=8):
    n = len(items)
    slots = small[3].shape[0]

    def update(w_ref, m_ref, v_ref, g, g_ref, d_ref, nm_ref, nv_ref):
        nm = ADAM_B1 * m_ref[...] + (1.0 - ADAM_B1) * g
        nv = ADAM_B2 * v_ref[...] + (1.0 - ADAM_B2) * (g * g)
        m_hat = nm / (1.0 - ADAM_B1 ** ADAM_STEP)
        v_hat = nv / (1.0 - ADAM_B2 ** ADAM_STEP)
        g_ref[...] = g
        d_ref[...] = -ADAM_LR * (m_hat / (jnp.sqrt(v_hat) + ADAM_EPS) + ADAM_WD * w_ref[...])
        nm_ref[...] = nm
        nv_ref[...] = nv

    def body(*refs):
        ins, outs = refs[:5 * n + 4], refs[5 * n + 4:]
        for k in range(n):
            w_ref, m_ref, v_ref, ga_ref, gb_ref = ins[5 * k:5 * k + 5]
            update(w_ref, m_ref, v_ref, ga_ref[...] + gb_ref[...], *outs[4 * k:4 * k + 4])

        @pl.when(pl.program_id(0) == 0)
        def _():
            w_ref, m_ref, v_ref, parts_ref = ins[5 * n:]
            g = parts_ref[0]
            for t in range(1, slots):
                g = g + parts_ref[t]
            update(w_ref, m_ref, v_ref, g, *outs[4 * n:])

    in_specs, out_specs, out_shape = [], [], []
    for item in items:
        r, c = item[0].shape
        in_specs += [_row_spec(r // steps, c)] * 5
        out_specs += [_row_spec(r // steps, c)] * 4
        out_shape += [jax.ShapeDtypeStruct((r, c), F32)] * 4
    in_specs += [_full_spec(a.shape) for a in small]
    out_specs += [_full_spec(small[0].shape)] * 4
    out_shape += [jax.ShapeDtypeStruct(small[0].shape, F32)] * 4
    out = pl.pallas_call(
        body, name="adamw_shards", grid=(steps,), in_specs=in_specs, out_specs=out_specs, out_shape=out_shape,
        compiler_params=_cparams(1),
    )(*[a for item in items for a in item], *small)
    return [out[4 * k:4 * k + 4] for k in range(n + 1)]


_SMALL = (("pre_norm_w", (2, D_MODEL)), ("post_norm_w", (2, D_MODEL)), ("attn_b_in", (1, ATTN_IN)),
          ("attn_sinks", (1, N_HEADS)), ("attn_b_out", (1, D_MODEL)), ("rec_lb_logits", (2, D_MODEL)),
          ("rec_gnorm_w", (1, REC_DIM)))
_SMALL_ROWS = 16


def _pack_small(parts, last_row=None):
    rows = []
    for (name, shape) in _SMALL:
        flat = parts[name].reshape(-1)
        pad = -flat.shape[0] % D_MODEL
        rows.append(jnp.pad(flat, (0, pad)).reshape(-1, D_MODEL))
    used = sum(r.shape[0] for r in rows)
    rows.append(jnp.zeros((_SMALL_ROWS - 1 - used, D_MODEL), F32))
    rows.append(jnp.zeros((1, D_MODEL), F32) if last_row is None else last_row)
    return jnp.concatenate(rows, axis=0)


def _unpack_small(packed):
    out, row = {}, 0
    for (name, shape) in _SMALL:
        size = shape[0] * shape[1]
        nrows = -(-size // D_MODEL)
        out[name] = packed[row:row + nrows].reshape(-1)[:size].reshape(shape)
        row += nrows
    return out


_CARRIED = ("rec_w_in", "rec_w_out", "attn_w_out")


_LATE = ("attn_w_out", "rec_w_in", "rec_w_out")


def local_step(x, positions, pre_norm_w, post_norm_w, attn_w_in, attn_b_in, attn_sinks, attn_w_out, attn_b_out,
               rec_w_in, rec_lb_logits, rec_gnorm_w, rec_w_out, loss_target, distributed=False):
    batch, seq, _ = x.shape
    n = batch * seq
    x0 = x.reshape(n, D_MODEL)
    angles = _rope_angles(positions)
    pre0, pre1 = pre_norm_w[0:1], pre_norm_w[1:2]
    post0, post1 = post_norm_w[0:1], post_norm_w[1:2]

    late = (attn_w_out, rec_w_in, rec_w_out) if distributed else ()
    (h0, q, k, v, z), late = attn_in_proj(x0, pre0, attn_w_in, attn_b_in, angles, to_bf16=late)
    sink_tab = _sink_table(attn_sinks)
    og0, gathered = attn_fwd(q, k, v, z, sink_tab, batch, seq, gather=late)
    if distributed:
        attn_w_out, rec_w_in, rec_w_out = (g if name == "rec_w_in" else _whole_from_shards(name, g)
                                           for name, g in zip(_LATE, gathered))
    y0, x1 = out_proj(og0, attn_w_out, attn_b_out, x0, post0)

    h1, proj1 = rec_in_proj(x1, pre1, rec_w_in)
    og1, states, safe = rec_fwd(proj1, rec_lb_logits, rec_gnorm_w, batch, seq)
    dx2, loss_vec, dog1, d_rec_w_out, d_post1, d_rec_w_out_bf16 = out_proj_loss_bwd(
        og1, rec_w_out, x1, post1, loss_target.reshape(n, D_MODEL))
    dproj1, d_lb, d_gnorm = rec_bwd(proj1, states, safe, rec_lb_logits, rec_gnorm_w, dog1, batch, seq)
    dx1, d_pre1, _, _ = in_proj_bwd_x(dproj1, rec_w_in, x1, pre1, dx2)
    d_rec_w_in, _ = in_proj_bwd_w(h1, dproj1, as_shards=distributed)

    dog0, d_attn_w_out, d_attn_b_out, d_post0, d_attn_w_out_bf16 = out_proj_bwd(dx1, y0, og0, attn_w_out, post0)
    ready = dict(rec_w_in=d_rec_w_in, rec_w_out=_shards_from_whole("rec_w_out", d_rec_w_out_bf16),
                 attn_w_out=_shards_from_whole("attn_w_out", d_attn_w_out_bf16))
    outgoing = [ready[name] for name in _CARRIED] if distributed else []
    dproj0, dk, dv, d_sink_tab, arrived = attn_bwd(q, k, v, z, sink_tab, dog0, angles, batch, seq, scatter=outgoing)
    d_sinks = jnp.transpose(jnp.sum(d_sink_tab, axis=-1), (0, 2, 1)).reshape(1, N_HEADS)
    d_attn_w_in, d_attn_b_in, dproj0, *summed = in_proj_bwd_w(h0, dproj0, kv=(dk, dv, angles),
                                                              slot_sums=list(arrived))
    last = [_shards_from_whole("attn_w_in", d_attn_w_in).astype(BF16)] if distributed else []
    dx0, d_pre0, arrived_last, theirs = in_proj_bwd_x(dproj0, attn_w_in, x0, pre0, dx1, scatter=last,
                                                      to_sibling=summed)

    grads = dict(
        pre_norm_w=jnp.concatenate([d_pre0, d_pre1], axis=0), post_norm_w=jnp.concatenate([d_post0, d_post1], axis=0),
        attn_w_in=d_attn_w_in, attn_b_in=d_attn_b_in, attn_sinks=d_sinks, attn_w_out=d_attn_w_out,
        attn_b_out=d_attn_b_out, rec_w_in=d_rec_w_in, rec_lb_logits=d_lb, rec_gnorm_w=d_gnorm,
        rec_w_out=d_rec_w_out)
    exchanged = dict(zip(_CARRIED, zip(summed, theirs)))
    exchanged.update(zip(("attn_w_in",), arrived_last))
    return loss_vec, dx0.reshape(batch, seq, D_MODEL), grads, exchanged


_BIG = ("attn_w_in", "attn_w_out", "rec_w_in", "rec_w_out")
_COLUMN_SHARDED = ("attn_w_in", "rec_w_in")
_ORDER = ("pre_norm_w", "post_norm_w", "attn_w_in", "attn_b_in", "attn_sinks", "attn_w_out", "attn_b_out",
          "rec_w_in", "rec_lb_logits", "rec_gnorm_w", "rec_w_out")


def _whole_from_shards(name, stacked):
    if name in _COLUMN_SHARDED:
        return jnp.transpose(stacked, (1, 0, 2)).reshape(stacked.shape[1], -1)
    return stacked.reshape(-1, stacked.shape[2])


def _shards_from_whole(name, whole):
    if name in _COLUMN_SHARDED:
        return jnp.transpose(whole.reshape(whole.shape[0], N_CHIPS, -1), (1, 0, 2))
    return whole.reshape(N_CHIPS, -1, whole.shape[1])


def kernel(x, positions, pre_norm_w, post_norm_w, attn_w_in, attn_b_in, attn_sinks, attn_w_out, attn_b_out, rec_w_in, rec_lb_logits, rec_gnorm_w, rec_w_out, loss_target, m_pre_norm_w, m_post_norm_w, m_attn_w_in, m_attn_b_in, m_attn_sinks, m_attn_w_out, m_attn_b_out, m_rec_w_in, m_rec_lb_logits, m_rec_gnorm_w, m_rec_w_out, v_pre_norm_w, v_post_norm_w, v_attn_w_in, v_attn_b_in, v_attn_sinks, v_attn_w_out, v_attn_b_out, v_rec_w_in, v_rec_lb_logits, v_rec_gnorm_w, v_rec_w_out):
    w = dict(pre_norm_w=pre_norm_w, post_norm_w=post_norm_w, attn_w_in=attn_w_in, attn_b_in=attn_b_in,
             attn_sinks=attn_sinks, attn_w_out=attn_w_out, attn_b_out=attn_b_out, rec_w_in=rec_w_in,
             rec_lb_logits=rec_lb_logits, rec_gnorm_w=rec_gnorm_w, rec_w_out=rec_w_out)
    m = dict(pre_norm_w=m_pre_norm_w, post_norm_w=m_post_norm_w, attn_w_in=m_attn_w_in, attn_b_in=m_attn_b_in,
             attn_sinks=m_attn_sinks, attn_w_out=m_attn_w_out, attn_b_out=m_attn_b_out, rec_w_in=m_rec_w_in,
             rec_lb_logits=m_rec_lb_logits, rec_gnorm_w=m_rec_gnorm_w, rec_w_out=m_rec_w_out)
    v = dict(pre_norm_w=v_pre_norm_w, post_norm_w=v_post_norm_w, attn_w_in=v_attn_w_in, attn_b_in=v_attn_b_in,
             attn_sinks=v_attn_sinks, attn_w_out=v_attn_w_out, attn_b_out=v_attn_b_out, rec_w_in=v_rec_w_in,
             rec_lb_logits=v_rec_lb_logits, rec_gnorm_w=v_rec_gnorm_w, rec_w_out=v_rec_w_out)

    shards = {name: w[name][0] for name in _BIG}
    attn_w_in_whole = _whole_from_shards("attn_w_in", chip_gather([shards["attn_w_in"].astype(BF16)])[0])

    loss_vec, grad_x, grads, exchanged = local_step(
        x, positions, pre_norm_w, post_norm_w, attn_w_in_whole, attn_b_in, attn_sinks, shards["attn_w_out"],
        attn_b_out, shards["rec_w_in"], rec_lb_logits, rec_gnorm_w, shards["rec_w_out"], loss_target,
        distributed=True)

    small_parts, mine, other = all_gather_small(_pack_small(grads, last_row=loss_vec), exchanged["attn_w_in"])
    exchanged["attn_w_in"] = (mine, other)
    out_g, out_d, out_m, out_v = {}, {}, {}, {}
    *updated, packed = adamw_shards([(shards[name], m[name][0], v[name][0], *exchanged[name]) for name in _BIG],
                                    (_pack_small(w), _pack_small(m), _pack_small(v), small_parts))
    for name, (g, d, nm, nv) in zip(_BIG, updated):
        out_g[name], out_d[name], out_m[name], out_v[name] = g[None], d[None], nm[None], nv[None]

    loss = jnp.sum(packed[0][_SMALL_ROWS - 1]) * (0.5 / D_MODEL)
    for dst, val in zip((out_g, out_d, out_m, out_v), packed):
        dst.update(_unpack_small(val))

    return (loss, grad_x, *[out_g[n] for n in _ORDER], *[out_d[n] for n in _ORDER],
            *[out_m[n] for n in _ORDER], *[out_v[n] for n in _ORDER])
```

```python
import functools

import jax
import jax.numpy as jnp
from jax import lax
from jax.experimental import pallas as pl
from jax.experimental.pallas import tpu as pltpu

F32 = jnp.float32
BF16 = jnp.bfloat16
MESH = pl.DeviceIdType.MESH

D_MODEL = 1024
HEAD_DIM = 64
N_HEADS = 16
N_KV_HEADS = 2
GROUP = N_HEADS // N_KV_HEADS
KV_WIDTH = N_KV_HEADS * HEAD_DIM
ATTN_IN = 2 * D_MODEL + 2 * KV_WIDTH
ATTN_BLOCK = 128
ROPE_THETA = 500000.0
ROPE_DIM = HEAD_DIM // 4
REC_HEADS = 8
REC_DIM = 128
REC_IN = 4 * D_MODEL
REC_BLOCK = 128
DIAG = 8
NORM_EPS = 1e-6
N_CHIPS = 4
N_DEV = 8
LANES = 128

ADAM_LR = 0.001
ADAM_B1 = 0.9
ADAM_B2 = 0.999
ADAM_EPS = 1e-08
ADAM_WD = 0.01
ADAM_STEP = 10

VMEM_LIMIT = 56 * 1024 * 1024


def _cparams(n_axes):
    return pltpu.CompilerParams(dimension_semantics=("arbitrary",) * n_axes, vmem_limit_bytes=VMEM_LIMIT)


def _dot(a, b, contract):
    return lax.dot_general(a.astype(BF16), b.astype(BF16), (contract, ((), ())), preferred_element_type=F32)


_NN = ((1,), (0,))
_NT = ((1,), (1,))
_TN = ((0,), (0,))


@jax.custom_vjp
def mm_nn(a, b):
    return _dot(a, b, _NN)


mm_nn.defvjp(lambda a, b: (_dot(a, b, _NN), (a, b)),
             lambda res, g: (_dot(g, res[1], _NT), _dot(res[0], g, _TN)))


@jax.custom_vjp
def mm_nt(a, b):
    return _dot(a, b, _NT)


mm_nt.defvjp(lambda a, b: (_dot(a, b, _NT), (a, b)),
             lambda res, g: (_dot(g, res[1], _NN), _dot(g, res[0], _TN)))


@jax.custom_vjp
def mm_tn(a, b):
    return _dot(a, b, _TN)


mm_tn.defvjp(lambda a, b: (_dot(a, b, _TN), (a, b)),
             lambda res, g: (_dot(res[1], g, _NT), _dot(res[0], g, _NN)))


def _tri_dot(x, lower):
    n = x.shape[0]
    r = lax.broadcasted_iota(jnp.int32, (n, n), 0)
    c = lax.broadcasted_iota(jnp.int32, (n, n), 1)
    tri = ((c <= r) if lower else (c >= r)).astype(BF16)
    hi = x.astype(BF16)
    rest = x - hi.astype(F32)
    mid = rest.astype(BF16)
    lo = (rest - mid.astype(F32)).astype(BF16)
    dot = lambda p: lax.dot_general(tri, p, (_NN, ((), ())), preferred_element_type=F32)
    return (dot(lo) + dot(mid)) + dot(hi)


@jax.custom_vjp
def cumsum_rows(x):
    return _tri_dot(x, True)


cumsum_rows.defvjp(lambda x: (cumsum_rows(x), None), lambda _, g: (_tri_dot(g, False),))


@functools.partial(jax.custom_vjp, nondiff_argnums=(1,))
def roll_sub(x, d):
    return pltpu.roll(x, d, 1) if d else x


roll_sub.defvjp(lambda x, d: (roll_sub(x, d), None),
                lambda d, _, g: (roll_sub(g, (DIAG - d) % DIAG),))


def sigmoid(x):
    return 1.0 / (1.0 + jnp.exp(-x))


@jax.custom_vjp
def silu(x):
    return x * sigmoid(x)


def _silu_fwd(x):
    s = sigmoid(x)
    return x * s, (x, s)


silu.defvjp(_silu_fwd, lambda res, g: (g * (res[1] * (1.0 + res[0] * (1.0 - res[1]))),))


F32_TINY = 1.17549435e-38


def sigmoid_pair(x):
    e = jnp.exp(-jnp.abs(x))
    r = 1.0 / (1.0 + e)
    er = e * r
    pos = x >= 0.0
    return jnp.where(pos, r, er), jnp.where(pos, er, r)


def _forget_fwd(x, a):
    lb, one_m_lb = sigmoid_pair(a)
    sp, sn = sigmoid_pair(x)
    f = lb + one_m_lb * sp
    k = one_m_lb * sn
    return (jnp.log(jnp.maximum(f, F32_TINY)), k), (sp, sn, f, k, lb, one_m_lb)


def _forget_bwd(res, g):
    sp, sn, f, k, lb, one_m_lb = res
    g_lf, g_k = g
    t = jnp.where(f >= F32_TINY, g_lf / jnp.maximum(f, F32_TINY), 0.0) - g_k
    return (k * sp) * t, jnp.sum(sn * t, axis=0, keepdims=True) * (lb * one_m_lb)


@jax.custom_vjp
def forget_gate(x, a):
    return _forget_fwd(x, a)[0]


forget_gate.defvjp(_forget_fwd, _forget_bwd)


@jax.custom_vjp
def decayed(x, e):
    return (x * jnp.exp(e)).astype(BF16).astype(F32)


def _decayed_fwd(x, e):
    y = decayed(x, e)
    return y, (y, e)


decayed.defvjp(_decayed_fwd, lambda res, g: (g * jnp.exp(res[1]), g * res[0]))


def _row(x, r):
    shape = x.shape

    @jax.custom_vjp
    def take(x):
        return x[r:r + 1, :]

    take.defvjp(lambda x: (x[r:r + 1, :], None),
                lambda _, g: (jnp.where(lax.broadcasted_iota(jnp.int32, shape, 0) == r, g, 0.0),))
    return take(x)


def _rms(x):
    return lax.rsqrt(jnp.mean(x * x, axis=-1, keepdims=True) + NORM_EPS)


def _attn_group(qs, k_a, v_a, k_b, v_b, zs, sink_a, sink_b, bias, at_sink=None):
    def half(kh, vh, sink):
        s = mm_nn(qs, kh) + bias
        if at_sink is None:
            m = jnp.maximum(jnp.max(s, axis=-1, keepdims=True), jnp.max(sink, axis=-1, keepdims=True))
            p = jnp.exp(s - lax.stop_gradient(m))
            own = jnp.sum(jnp.exp(sink - lax.stop_gradient(m)), axis=-1, keepdims=True) * (1.0 / LANES)
            return mm_nt(p * (1.0 / (jnp.sum(p, axis=-1, keepdims=True) + own)), vh)
        s = jnp.where(at_sink, jnp.concatenate([sink, sink], axis=1), s)
        p = jnp.exp(s - jnp.max(s, axis=-1, keepdims=True))
        return mm_nt(jnp.where(at_sink, 0.0, p), vh) * (1.0 / jnp.sum(p, axis=-1, keepdims=True))

    return (half(k_a, v_a, sink_a) + half(k_b, v_b, sink_b)) * silu(zs)


SAFE_RANGE = 80.0


def _rec_front(qr, fr, l0, l1):
    lf, k = forget_gate(fr, l1 - l0)
    return silu(qr), k, lf


def _rec_tail(o, z, gw):
    return o * _rms(o) * gw * silu(z)


def _rec_margin(b):
    R = b.shape[0]
    mid, last = _row(b, R // 2 - 1), _row(b, R - 1)
    return jnp.minimum(mid, last - mid)


def _heads(x):
    w = x.shape[1] // REC_HEADS
    return [x[:, h * w:(h + 1) * w] for h in range(REC_HEADS)]


def _hdot(a, b, contract):
    return jnp.concatenate([_dot(ah, bh, contract) for ah, bh in zip(_heads(a), _heads(b))], axis=1)


@jax.custom_vjp
def hmm_nn(a, b):
    return _hdot(a, b, _NN)


hmm_nn.defvjp(lambda a, b: (_hdot(a, b, _NN), (a, b)),
              lambda res, g: (_hdot(g, res[1], _NT), _hdot(res[0], g, _TN)))


@jax.custom_vjp
def hmm_nt(a, b):
    return _hdot(a, b, _NT)


hmm_nt.defvjp(lambda a, b: (_hdot(a, b, _NT), (a, b)),
              lambda res, g: (_hdot(g, res[1], _NN), _hdot(g, res[0], _TN)))


@jax.custom_vjp
def hmm_tn(a, b):
    return _hdot(a, b, _TN)


hmm_tn.defvjp(lambda a, b: (_hdot(a, b, _TN), (a, b)),
              lambda res, g: (_hdot(res[1], g, _NT), _hdot(res[0], g, _NN)))


def _head_sums(x):
    return jnp.concatenate([jnp.broadcast_to(jnp.sum(xh, axis=-1, keepdims=True), xh.shape) for xh in _heads(x)],
                           axis=1)


@jax.custom_vjp
def head_sum(x):
    return _head_sums(x)


head_sum.defvjp(lambda x: (_head_sums(x), None), lambda _, g: (_head_sums(g),))


def _rec_cores_fast(q, k, v, b, S):
    R = q.shape[0]
    ri = lax.broadcasted_iota(jnp.int32, (R, REC_HEADS * R), 0)
    ci = lax.broadcasted_iota(jnp.int32, (R, REC_HEADS * R), 1) % R
    d = b - _row(b, R // 2 - 1)
    sc = jnp.where(ci < ri, hmm_nt(decayed(q, d), decayed(k, -d)), 0.0)
    o = hmm_nt(q * jnp.exp(b), S) + hmm_nn(sc, v) + head_sum(q * k) * v
    b_last = _row(b, R - 1)
    return o, S * jnp.exp(b_last) + hmm_tn(v, k * jnp.exp(b_last - b))


def _rec_tails(o, z, gw):
    return o * lax.rsqrt(head_sum(o * o) * (1.0 / REC_DIM) + NORM_EPS) * gw * silu(z)


def _rec_block_fast(qr, fr, v, z, S, l0, l1, gw):
    lf, k = forget_gate(fr, l1 - l0)
    o, S_new = _rec_cores_fast(silu(qr), k, v, cumsum_rows(lf), S)
    return _rec_tails(o, z, gw), S_new


def _rec_core_slow(q, k, v, b, S):
    R = q.shape[0]
    rows = lax.broadcasted_iota(jnp.int32, (R, REC_DIM), 0)

    o = mm_nt(q * jnp.exp(jnp.minimum(b, 0.0)), S)

    ri = lax.broadcasted_iota(jnp.int32, (R, R), 0)
    ci = lax.broadcasted_iota(jnp.int32, (R, R), 1)
    sc = jnp.zeros((R, R), F32)
    w = R
    while w > DIAG:
        h = w // 2
        b3 = b.reshape(R // w, w, REC_DIM)
        rin = lax.broadcasted_iota(jnp.int32, (R // w, w, REC_DIM), 1)
        mid = jnp.sum(jnp.where(rin == h - 1, b3, 0.0), axis=1, keepdims=True)
        fac = jnp.exp(jnp.minimum(jnp.where(rin >= h, b3 - mid, mid - b3), 0.0)).reshape(R, REC_DIM)
        upper = (rows % w) >= h
        s_w = mm_nt(jnp.where(upper, q * fac, 0.0), jnp.where(upper, 0.0, k * fac))
        sc = sc + jnp.where((ri // w) == (ci // w), s_w, 0.0)
        w = h
    o = o + mm_nn(sc, v)

    g = R // DIAG
    q3, k3, v3, b3 = (t.reshape(g, DIAG, REC_DIM) for t in (q, k, v, b))
    rin = lax.broadcasted_iota(jnp.int32, (g, DIAG, 1), 1)
    od = jnp.zeros((g, DIAG, REC_DIM), F32)
    for d in range(DIAG):
        e = jnp.exp(jnp.minimum(b3 - roll_sub(b3, d), 0.0))
        sd = jnp.sum(q3 * roll_sub(k3, d) * e, axis=-1, keepdims=True)
        od = od + jnp.where(rin >= d, sd, 0.0) * roll_sub(v3, d)
    o = o + od.reshape(R, REC_DIM)

    b_last = _row(b, R - 1)
    return o, S * jnp.exp(jnp.minimum(b_last, 0.0)) + mm_tn(v, k * jnp.exp(jnp.minimum(b_last - b, 0.0)))


def _rec_head(core, qr, fr, v, z, S, l0, l1, gw):
    q, k, lf = _rec_front(qr, fr, l0, l1)
    o, S_new = core(q, k, v, cumsum_rows(lf), S)
    return _rec_tail(o, z, gw), S_new


ANGLE_COLS = 3 * ROPE_DIM


def _rope_angles(positions):
    half = ROPE_DIM // 2
    inv_freq = ROPE_THETA ** (-(jnp.arange(half, dtype=F32) * 2.0 / ROPE_DIM))
    ang = positions.astype(F32).reshape(-1, 1) * inv_freq
    cs = jnp.concatenate([jnp.cos(ang), jnp.sin(ang)], axis=-1)
    hi = cs.astype(BF16)
    rest = cs - hi.astype(F32)
    mid = rest.astype(BF16)
    return jnp.concatenate([hi, mid, (rest - mid.astype(F32)).astype(BF16)], axis=-1)


def _rope_tables(pieces):
    half = ROPE_DIM // 2
    r = lax.broadcasted_iota(jnp.int32, (ANGLE_COLS, 3 * LANES), 0) % ROPE_DIM
    c = lax.broadcasted_iota(jnp.int32, (ANGLE_COLS, 3 * LANES), 1)
    table, j = c // LANES, c % HEAD_DIM
    angle, low = j % half, j < half
    plus = ((table == 0) & (j < ROPE_DIM) & (r == angle)) | ((table == 1) & (j >= half) & (j < ROPE_DIM)
                                                                & (r == half + angle))
    minus = (table == 2) & low & (r == half + angle)
    pick = jnp.where(plus, 1.0, jnp.where(minus, -1.0, 0.0)).astype(BF16)
    out = jnp.dot(pieces, pick, preferred_element_type=F32)
    lane = lax.broadcasted_iota(jnp.int32, (1, LANES), 1) % HEAD_DIM
    return out[:, :LANES] + jnp.where(lane < ROPE_DIM, 0.0, 1.0), out[:, LANES:2 * LANES], out[:, 2 * LANES:]


def _rope(x, cos_t, sin_a, sin_b):
    half = ROPE_DIM // 2
    return x * cos_t + pltpu.roll(x, half, 1) * sin_a + pltpu.roll(x, LANES - half, 1) * sin_b


def _rope_transposed(g, cos_t, sin_a, sin_b):
    half = ROPE_DIM // 2
    return g * cos_t + pltpu.roll(g * sin_a, LANES - half, 1) + pltpu.roll(g * sin_b, half, 1)


def _row_spec(tm, width):
    return pl.BlockSpec((tm, width), lambda i: (i, 0))


def _weight_spec(shape):
    return pl.BlockSpec(shape, lambda *_: (0,) * len(shape), pipeline_mode=pl.Buffered(1))


def _full_spec(shape):
    return pl.BlockSpec(shape, lambda *_: (0,) * len(shape))


def attn_in_proj(x, w_pre, w_in, b_in, angles, tm=1024, to_bf16=()):
    n = x.shape[0]
    tm = min(tm, n)
    nc = len(to_bf16)

    def body(*refs):
        x_ref, wp_ref, w_ref, b_ref, cs_ref = refs[:5]
        h_ref, q_ref, k_ref, v_ref, z_ref = refs[5 + nc:10 + nc]

        @pl.when(pl.program_id(0) == 0)
        def _():
            for src, dst in zip(refs[5:5 + nc], refs[10 + nc:]):
                dst[...] = src[...].astype(BF16)

        xv = x_ref[...]
        h = (xv * _rms(xv) * wp_ref[...]).astype(BF16)
        h_ref[...] = h
        proj = jnp.dot(h, w_ref[...], preferred_element_type=F32) + b_ref[...]
        tabs = _rope_tables(cs_ref[...])
        for s in range(D_MODEL // LANES):
            sl = slice(s * LANES, (s + 1) * LANES)
            q_ref[:, sl] = _rope(proj[:, sl] * (HEAD_DIM ** -0.5), *tabs).astype(BF16)
        k_ref[...] = _rope(proj[:, D_MODEL:D_MODEL + KV_WIDTH], *tabs).astype(BF16)
        v_ref[...] = proj[:, D_MODEL + KV_WIDTH:D_MODEL + 2 * KV_WIDTH].astype(BF16)
        z_ref[...] = proj[:, D_MODEL + 2 * KV_WIDTH:]

    out = pl.pallas_call(
        body, name="attn_in_proj", grid=(n // tm,),
        in_specs=[_row_spec(tm, D_MODEL), _full_spec((1, D_MODEL)), _weight_spec((D_MODEL, ATTN_IN)),
                  _full_spec((1, ATTN_IN)), _row_spec(tm, ANGLE_COLS)] + [_weight_spec(a.shape) for a in to_bf16],
        out_specs=[_row_spec(tm, D_MODEL), _row_spec(tm, D_MODEL), _row_spec(tm, KV_WIDTH),
                   _row_spec(tm, KV_WIDTH), _row_spec(tm, D_MODEL)] + [_full_spec(a.shape) for a in to_bf16],
        out_shape=[jax.ShapeDtypeStruct((n, D_MODEL), BF16), jax.ShapeDtypeStruct((n, D_MODEL), BF16),
                   jax.ShapeDtypeStruct((n, KV_WIDTH), BF16), jax.ShapeDtypeStruct((n, KV_WIDTH), BF16),
                   jax.ShapeDtypeStruct((n, D_MODEL), F32)] + [jax.ShapeDtypeStruct(a.shape, BF16) for a in to_bf16],
        compiler_params=_cparams(1),
    )(x, w_pre, w_in, b_in, angles, *to_bf16)
    return out[:5], out[5:]


def _column_blocks(w):
    if len(w.shape) == 2:
        return [slice(0, w.shape[1])], lambda ref, s: ref[...]
    width = w.shape[2]
    return [slice(s * width, (s + 1) * width) for s in range(w.shape[0])], lambda ref, s: ref[s]


def rec_in_proj(x, w_pre, w_in, tm=1024):
    n = x.shape[0]
    tm = min(tm, n)
    columns, block = _column_blocks(w_in)

    def body(x_ref, wp_ref, w_ref, h_ref, p_ref):
        xv = x_ref[...]
        h = (xv * _rms(xv) * wp_ref[...]).astype(BF16)
        h_ref[...] = h
        for s, cols in enumerate(columns):
            p_ref[:, cols] = jnp.dot(h, block(w_ref, s), preferred_element_type=F32)

    return pl.pallas_call(
        body, name="rec_in_proj", grid=(n // tm,),
        in_specs=[_row_spec(tm, D_MODEL), _full_spec((1, D_MODEL)), _weight_spec(w_in.shape)],
        out_specs=[_row_spec(tm, D_MODEL), _row_spec(tm, REC_IN)],
        out_shape=[jax.ShapeDtypeStruct((n, D_MODEL), BF16), jax.ShapeDtypeStruct((n, REC_IN), F32)],
        compiler_params=_cparams(1),
    )(x, w_pre, w_in)


def out_proj(og, w_out, b_out, x_res, w_post, tm=1024):
    n = og.shape[0]
    tm = min(tm, n)

    def body(og_ref, w_ref, b_ref, x_ref, wp_ref, y_ref, xo_ref):
        y = jnp.dot(og_ref[...], w_ref[...], preferred_element_type=F32) + b_ref[...]
        y_ref[...] = y.astype(BF16)
        xo_ref[...] = x_ref[...] + y * _rms(y) * wp_ref[...]

    return pl.pallas_call(
        body, name="out_proj", grid=(n // tm,),
        in_specs=[_row_spec(tm, D_MODEL), _weight_spec((D_MODEL, D_MODEL)), _full_spec((1, D_MODEL)),
                  _row_spec(tm, D_MODEL), _full_spec((1, D_MODEL))],
        out_specs=[_row_spec(tm, D_MODEL), _row_spec(tm, D_MODEL)],
        out_shape=[jax.ShapeDtypeStruct((n, D_MODEL), BF16), jax.ShapeDtypeStruct((n, D_MODEL), F32)],
        compiler_params=_cparams(1),
    )(og, w_out, b_out, x_res, w_post)


def _post_norm_bwd(g, y, w_post):
    rstd = _rms(y)
    yn = y * rstd
    gw = g * w_post
    return rstd * (gw - yn * jnp.mean(gw * yn, axis=-1, keepdims=True)), jnp.sum(g * yn, axis=0, keepdims=True)


def out_proj_loss_bwd(og, w_out, x_res, w_post, target, tm=1024):
    n = og.shape[0]
    tm = min(tm, n)
    steps = n // tm

    def body(og_ref, w_ref, x_ref, wp_ref, t_ref, dx_ref, l_ref, dog_ref, dw_ref, dwp_ref, dwb_ref):
        @pl.when(pl.program_id(0) == 0)
        def _():
            l_ref[...] = jnp.zeros_like(l_ref)
            dw_ref[...] = jnp.zeros_like(dw_ref)
            dwp_ref[...] = jnp.zeros_like(dwp_ref)

        og_tile = og_ref[...]
        y = jnp.dot(og_tile, w_ref[...], preferred_element_type=F32)
        err = x_ref[...] + y * _rms(y) * wp_ref[...] - t_ref[...]
        g = err * (1.0 / D_MODEL)
        dx_ref[...] = g
        l_ref[...] += jnp.sum(err * err, axis=0, keepdims=True)
        dy, dwp = _post_norm_bwd(g, y, wp_ref[...])
        dwp_ref[...] += dwp
        dyb = dy.astype(BF16)
        dog_ref[...] = _dot(dyb, w_ref[...], _NT).astype(BF16)
        dw_ref[...] += _dot(og_tile, dyb, _TN)

        @pl.when(pl.program_id(0) == steps - 1)
        def _():
            dwb_ref[...] = dw_ref[...].astype(BF16)

    return pl.pallas_call(
        body, name="out_proj_loss_bwd", grid=(steps,),
        in_specs=[_row_spec(tm, D_MODEL), _weight_spec((D_MODEL, D_MODEL)), _row_spec(tm, D_MODEL),
                  _full_spec((1, D_MODEL)), _row_spec(tm, D_MODEL)],
        out_specs=[_row_spec(tm, D_MODEL), _full_spec((1, D_MODEL)), _row_spec(tm, D_MODEL),
                   _full_spec((D_MODEL, D_MODEL)), _full_spec((1, D_MODEL)), _full_spec((D_MODEL, D_MODEL))],
        out_shape=[jax.ShapeDtypeStruct((n, D_MODEL), F32), jax.ShapeDtypeStruct((1, D_MODEL), F32),
                   jax.ShapeDtypeStruct((n, D_MODEL), BF16), jax.ShapeDtypeStruct((D_MODEL, D_MODEL), F32),
                   jax.ShapeDtypeStruct((1, D_MODEL), F32), jax.ShapeDtypeStruct((D_MODEL, D_MODEL), BF16)],
        compiler_params=_cparams(1),
    )(og, w_out, x_res, w_post, target)


def out_proj_bwd(dxo, y, og, w_out, w_post, tm=1024):
    n = og.shape[0]
    tm = min(tm, n)
    steps = n // tm

    def body(g_ref, y_ref, og_ref, w_ref, wp_ref, dog_ref, dw_ref, db_ref, dwp_ref, dwb_ref):
        @pl.when(pl.program_id(0) == 0)
        def _():
            dw_ref[...] = jnp.zeros_like(dw_ref)
            db_ref[...] = jnp.zeros_like(db_ref)
            dwp_ref[...] = jnp.zeros_like(dwp_ref)

        dy, dwp = _post_norm_bwd(g_ref[...], y_ref[...].astype(F32), wp_ref[...])
        dwp_ref[...] += dwp
        db_ref[...] += jnp.sum(dy, axis=0, keepdims=True)
        dyb = dy.astype(BF16)
        dog_ref[...] = _dot(dyb, w_ref[...], _NT).astype(BF16)
        dw_ref[...] += _dot(og_ref[...], dyb, _TN)

        @pl.when(pl.program_id(0) == steps - 1)
        def _():
            dwb_ref[...] = dw_ref[...].astype(BF16)

    return pl.pallas_call(
        body, name="out_proj_bwd", grid=(steps,),
        in_specs=[_row_spec(tm, D_MODEL), _row_spec(tm, D_MODEL), _row_spec(tm, D_MODEL),
                  _weight_spec((D_MODEL, D_MODEL)), _full_spec((1, D_MODEL))],
        out_specs=[_row_spec(tm, D_MODEL), _full_spec((D_MODEL, D_MODEL)), _full_spec((1, D_MODEL)),
                   _full_spec((1, D_MODEL)), _full_spec((D_MODEL, D_MODEL))],
        out_shape=[jax.ShapeDtypeStruct((n, D_MODEL), BF16), jax.ShapeDtypeStruct((D_MODEL, D_MODEL), F32),
                   jax.ShapeDtypeStruct((1, D_MODEL), F32), jax.ShapeDtypeStruct((1, D_MODEL), F32),
                   jax.ShapeDtypeStruct((D_MODEL, D_MODEL), BF16)],
        compiler_params=_cparams(1),
    )(dxo, y, og, w_out, w_post)


def _slot_sum_specs(slot_sums, steps):
    return ([pl.BlockSpec((a.shape[0], a.shape[1] // steps, a.shape[2]), lambda i: (0, i, 0)) for a in slot_sums],
            [_row_spec(a.shape[1] // steps, a.shape[2]) for a in slot_sums],
            [jax.ShapeDtypeStruct(a.shape[1:], F32) for a in slot_sums])


def _sum_slots_into(slot_refs, sum_refs):
    for slots_ref, sum_ref in zip(slot_refs, sum_refs):
        sum_ref[...] = functools.reduce(jnp.add, [slots_ref[t].astype(F32) for t in range(slots_ref.shape[0])])


def in_proj_bwd_x(dproj, w_in, x, w_pre, dxo, tm=1024, scatter=(), to_sibling=()):
    n, p = dproj.shape
    tm = min(tm, n)
    steps = n // tm
    ns, nsib = len(scatter), len(to_sibling)
    columns, block = _column_blocks(w_in)

    def body(*refs):
        dp_ref, w_ref, x_ref, wp_ref, g_ref = refs[:5]
        outs = 5 + ns + nsib
        dx_ref, dwp_ref = refs[outs:outs + 2]
        sems = refs[outs + 2 + ns + nsib:]
        exchange = (refs[5:5 + ns], refs[outs + 2:outs + 2 + ns]) + tuple(sems[:3])
        sibling = (refs[5 + ns:outs], refs[outs + 2 + ns:outs + 2 + ns + nsib]) + tuple(sems[3:])

        @pl.when(pl.program_id(0) == 0)
        def _():
            dwp_ref[...] = jnp.zeros_like(dwp_ref)
            if ns:
                _scatter_start(*exchange)
            for cp in _sibling_copies(*sibling) if nsib else ():
                cp.start()

        dh = functools.reduce(jnp.add, [_dot(dp_ref[:, cols], block(w_ref, s), _NT)
                                        for s, cols in enumerate(columns)])
        xv = x_ref[...]
        rstd = _rms(xv)
        xn = xv * rstd
        gw = dh * wp_ref[...]
        dwp_ref[...] += jnp.sum(dh * xn, axis=0, keepdims=True)
        dx_ref[...] = rstd * (gw - xn * jnp.mean(gw * xn, axis=-1, keepdims=True)) + g_ref[...]

        if ns or nsib:
            @pl.when(pl.program_id(0) == steps - 1)
            def _():
                if ns:
                    _scatter_finish(*exchange)
                for cp in _sibling_copies(*sibling) if nsib else ():
                    cp.wait()

    out = pl.pallas_call(
        body, name=f"in_proj_bwd_x_{p}", grid=(steps,),
        in_specs=[_row_spec(tm, p), _weight_spec(w_in.shape), _row_spec(tm, D_MODEL), _full_spec((1, D_MODEL)),
                  _row_spec(tm, D_MODEL)] + [_ANY] * (ns + nsib),
        out_specs=[_row_spec(tm, D_MODEL), _full_spec((1, D_MODEL))] + [_ANY] * (ns + nsib),
        out_shape=[jax.ShapeDtypeStruct((n, D_MODEL), F32), jax.ShapeDtypeStruct((1, D_MODEL), F32)]
        + [jax.ShapeDtypeStruct(a.shape, a.dtype) for a in tuple(scatter) + tuple(to_sibling)],
        scratch_shapes=(_scatter_sems(ns) if ns else []) + (_sibling_sems(nsib) if nsib else []),
        compiler_params=_cparams(1),
    )(dproj, w_in, x, w_pre, dxo, *scatter, *to_sibling)
    return out[0], out[1], out[2:2 + ns], out[2 + ns:]


def in_proj_bwd_w(h, dproj, tm=1024, as_shards=False, kv=None, slot_sums=()):
    n, p = dproj.shape
    chunk = p // (4 if p % 4096 == 0 else 3)
    tm = min(tm, n)
    steps = n // tm
    shard = p // N_CHIPS
    n_kv = 0 if kv is None else 3
    n_sum = len(slot_sums)
    kv_from, kv_to = D_MODEL, D_MODEL + 2 * KV_WIDTH

    def body(*refs):
        h_ref, dp_ref = refs[:2]
        outs = 2 + n_kv + n_sum
        dw_ref, db_ref = refs[outs:outs + 2]
        sums_at = outs + 2 + (kv is not None)
        scratch = refs[sums_at + n_sum:]
        acc_scr, sem, staging = scratch[0], scratch[1], scratch[2:]
        i = pl.program_id(0)
        _sum_slots_into(refs[2 + n_kv:outs], refs[sums_at:sums_at + n_sum])

        @pl.when(i == 0)
        def _():
            acc_scr[...] = jnp.zeros_like(acc_scr)
            db_ref[...] = jnp.zeros_like(db_ref)

        if kv is not None:
            dk_ref, dv_ref, cs_ref, kv_ref = refs[2], refs[3], refs[4], refs[outs + 2]
            made = jnp.concatenate([_rope_transposed(dk_ref[...].T, *_rope_tables(cs_ref[...])), dv_ref[...].T],
                                   axis=1).astype(BF16)
            kv_ref[...] = made

        def columns(c0):
            if kv is None or c0 + chunk <= kv_from or c0 >= kv_to:
                return dp_ref[:, c0:c0 + chunk]
            return jnp.concatenate([dp_ref[:, c0:kv_from], made, dp_ref[:, kv_to:c0 + chunk]], axis=1)

        ht = h_ref[...].T
        for c0 in range(0, p, chunk):
            dp = columns(c0)
            acc_scr[:, c0:c0 + chunk] += jnp.dot(ht, dp, preferred_element_type=F32)
            db_ref[:, c0:c0 + chunk] += jnp.sum(dp.astype(F32), axis=0, keepdims=True)

        @pl.when(i == steps - 1)
        def _():
            if as_shards:
                for s in range(N_CHIPS):
                    staging[0][...] = acc_scr[:, s * shard:(s + 1) * shard].astype(BF16)
                    out = pltpu.make_async_copy(staging[0], dw_ref.at[s], sem)
                    out.start()
                    out.wait()
            else:
                out = pltpu.make_async_copy(acc_scr, dw_ref, sem)
                out.start()
                out.wait()

    dw_shape = jax.ShapeDtypeStruct((N_CHIPS, D_MODEL, shard), BF16) if as_shards else (
        jax.ShapeDtypeStruct((D_MODEL, p), F32))
    in_specs = [_row_spec(tm, D_MODEL), _row_spec(tm, p)]
    out_specs = [_ANY, _full_spec((1, p))]
    out_shape = [dw_shape, jax.ShapeDtypeStruct((1, p), F32)]
    if kv is not None:
        columns_t = pl.BlockSpec((KV_WIDTH, tm), lambda i: (0, i))
        in_specs += [columns_t, columns_t, _row_spec(tm, ANGLE_COLS)]
        out_specs.append(pl.BlockSpec((tm, kv_to - kv_from), lambda i: (i, kv_from // (kv_to - kv_from))))
        out_shape.append(jax.ShapeDtypeStruct(dproj.shape, dproj.dtype))
    sum_in, sum_out, sum_shapes = _slot_sum_specs(slot_sums, steps)
    in_specs, out_specs, out_shape = in_specs + sum_in, out_specs + sum_out, out_shape + sum_shapes
    return pl.pallas_call(
        body, name=f"in_proj_bwd_w_{p}", grid=(steps,),
        in_specs=in_specs, out_specs=out_specs, out_shape=out_shape,
        scratch_shapes=[pltpu.VMEM((D_MODEL, p), F32), pltpu.SemaphoreType.DMA]
        + ([pltpu.VMEM((D_MODEL, shard), BF16)] if as_shards else []),
        input_output_aliases={1: 2} if kv is not None else {},
        compiler_params=_cparams(1),
    )(h, dproj, *(kv or ()), *slot_sums)


PAIRS = GROUP // 2
GROUP_ROWS = PAIRS * ATTN_BLOCK
MASKED = -1e30


def _kv_windows(k_ref, v_ref, i):
    ps = pl.multiple_of(jnp.maximum(i - 1, 0) * ATTN_BLOCK, ATTN_BLOCK)
    cs = pl.multiple_of(i * ATTN_BLOCK, ATTN_BLOCK)
    kw = jnp.concatenate([k_ref[pl.ds(ps, ATTN_BLOCK), :], k_ref[pl.ds(cs, ATTN_BLOCK), :]], axis=0)
    vw = jnp.concatenate([v_ref[pl.ds(ps, ATTN_BLOCK), :], v_ref[pl.ds(cs, ATTN_BLOCK), :]], axis=0)
    return kw.astype(F32).T, vw.astype(F32).T, ps, cs


def _low_rows(shape):
    return lax.broadcasted_iota(jnp.int32, shape, 0) < HEAD_DIM


def _spread(w, kvh):
    low = _low_rows(w.shape)
    swapped = pltpu.roll(w, HEAD_DIM, 0)
    if kvh == 0:
        return jnp.where(low, w, 0.0), jnp.where(low, 0.0, swapped)
    return jnp.where(low, swapped, 0.0), jnp.where(low, 0.0, w)


def _unspread(d_a, d_b, kvh):
    low = _low_rows(d_a.shape)
    if kvh == 0:
        return jnp.where(low, d_a + pltpu.roll(d_b, HEAD_DIM, 0), 0.0)
    return jnp.where(low, 0.0, pltpu.roll(d_a, HEAD_DIM, 0) + d_b)


def _stack_pairs(ref, kvh):
    return jnp.concatenate([ref[:, (kvh * PAIRS + j) * LANES:(kvh * PAIRS + j + 1) * LANES] for j in range(PAIRS)],
                           axis=0)


def _fill_bias(bias_scr):
    shape = (GROUP_ROWS, 2 * ATTN_BLOCK)
    r = lax.broadcasted_iota(jnp.int32, shape, 0) % ATTN_BLOCK
    c = lax.broadcasted_iota(jnp.int32, shape, 1)
    in_cur = (c >= ATTN_BLOCK) & ((c - ATTN_BLOCK) <= r)
    in_prev = (c < ATTN_BLOCK) & (c > r)
    bias_scr[0] = jnp.where(in_cur, 0.0, MASKED)
    bias_scr[1] = jnp.where(in_cur | in_prev, 0.0, MASKED)
    bias_scr[2] = jnp.where(c == r, 1.0, 0.0)


N_BIAS_TABLES = 3


def _sink_table(sinks):
    t = jnp.transpose(sinks.reshape(N_KV_HEADS, PAIRS, 2), (0, 2, 1))
    return jnp.broadcast_to(t[:, :, :, None, None], (N_KV_HEADS, 2, PAIRS, ATTN_BLOCK, LANES)).reshape(
        N_KV_HEADS, 2, GROUP_ROWS, LANES)


def attn_fwd(q, k, v, z, sink_tab, batch, seq, gather=()):
    nb = seq // ATTN_BLOCK
    ng = len(gather)

    def body(*refs):
        q_ref, k_ref, v_ref, z_ref, s_ref = refs[:5]
        og_ref, bias_scr = refs[5 + ng], refs[6 + 2 * ng]
        exchange = (refs[5:5 + ng], refs[6 + ng:6 + 2 * ng]) + tuple(refs[7 + 2 * ng:])
        b, i = pl.program_id(0), pl.program_id(1)

        @pl.when((b == 0) & (i == 0))
        def _():
            _fill_bias(bias_scr)
            if ng:
                _gather_start(*exchange)

        kw, vw, _, _ = _kv_windows(k_ref, v_ref, i)
        bias, at_sink = bias_scr[jnp.minimum(i, 1)], bias_scr[2] > 0.5
        for kvh in range(N_KV_HEADS):
            k_a, k_b = _spread(kw, kvh)
            v_a, v_b = _spread(vw, kvh)
            og = _attn_group(_stack_pairs(q_ref, kvh), k_a, v_a, k_b, v_b, _stack_pairs(z_ref, kvh),
                             s_ref[kvh, 0], s_ref[kvh, 1], bias, at_sink)
            for j in range(PAIRS):
                og_ref[:, (kvh * PAIRS + j) * LANES:(kvh * PAIRS + j + 1) * LANES] = (
                    og[j * ATTN_BLOCK:(j + 1) * ATTN_BLOCK].astype(BF16))

        if ng:
            @pl.when((b == batch - 1) & (i == nb - 1))
            def _():
                _gather_finish(*exchange)

    blk = lambda w: pl.BlockSpec((ATTN_BLOCK, w), lambda b, i: (b * nb + i, 0))
    seq_spec = pl.BlockSpec((seq, KV_WIDTH), lambda b, i: (b, 0))
    out = pl.pallas_call(
        body, name="attn_fwd", grid=(batch, nb),
        in_specs=[blk(D_MODEL), seq_spec, seq_spec, blk(D_MODEL), _full_spec(sink_tab.shape)] + [_ANY] * ng,
        out_specs=[blk(D_MODEL)] + [_ANY] * ng,
        out_shape=[jax.ShapeDtypeStruct((batch * seq, D_MODEL), BF16)]
        + [jax.ShapeDtypeStruct((N_CHIPS,) + a.shape, a.dtype) for a in gather],
        scratch_shapes=[pltpu.VMEM((N_BIAS_TABLES, GROUP_ROWS, 2 * ATTN_BLOCK), F32)] + (_gather_sems(ng) if ng else []),
        compiler_params=_cparams(2),
    )(q, k, v, z, sink_tab, *gather)
    return out[0], out[1:]


def attn_bwd(q, k, v, z, sink_tab, dog, angles, batch, seq, scatter=()):
    nb = seq // ATTN_BLOCK
    ns = len(scatter)

    def body(*refs):
        q_ref, k_ref, v_ref, z_ref, s_ref, g_ref, cs_ref = refs[:7]
        dp_ref, dk_ref, dv_ref, ds_ref = refs[7 + ns:11 + ns]
        bias_scr = refs[11 + 2 * ns]
        exchange = (refs[7:7 + ns], refs[11 + ns:11 + 2 * ns]) + tuple(refs[12 + 2 * ns:])
        b, i = pl.program_id(0), pl.program_id(1)

        @pl.when((b == 0) & (i == 0))
        def _():
            _fill_bias(bias_scr)
            ds_ref[...] = jnp.zeros_like(ds_ref)
            if ns:
                _scatter_start(*exchange)

        @pl.when(i == 0)
        def _():
            dk_ref[...] = jnp.zeros_like(dk_ref)
            dv_ref[...] = jnp.zeros_like(dv_ref)

        kw, vw, ps, cs = _kv_windows(k_ref, v_ref, i)
        bias = bias_scr[jnp.minimum(i, 1)]
        tabs = _rope_tables(cs_ref[...])
        dkw = jnp.zeros_like(kw)
        dvw = jnp.zeros_like(vw)
        for kvh in range(N_KV_HEADS):
            k_a, k_b = _spread(kw, kvh)
            v_a, v_b = _spread(vw, kvh)
            _, vjp = jax.vjp(functools.partial(_attn_group, bias=bias), _stack_pairs(q_ref, kvh).astype(F32),
                             k_a, v_a, k_b, v_b, _stack_pairs(z_ref, kvh), s_ref[kvh, 0], s_ref[kvh, 1])
            dqs, dk_a, dv_a, dk_b, dv_b, dzs, ds_a, ds_b = vjp(_stack_pairs(g_ref, kvh).astype(F32))
            dkw = dkw + _unspread(dk_a, dk_b, kvh)
            dvw = dvw + _unspread(dv_a, dv_b, kvh)
            ds_ref[kvh, 0] += jnp.sum(ds_a.reshape(PAIRS, ATTN_BLOCK, LANES), axis=1)
            ds_ref[kvh, 1] += jnp.sum(ds_b.reshape(PAIRS, ATTN_BLOCK, LANES), axis=1)
            for j in range(PAIRS):
                rows = slice(j * ATTN_BLOCK, (j + 1) * ATTN_BLOCK)
                col = (kvh * PAIRS + j) * LANES
                dp_ref[:, col:col + LANES] = _rope_transposed(dqs[rows] * (HEAD_DIM ** -0.5), *tabs).astype(BF16)
                zc = D_MODEL + 2 * KV_WIDTH + col
                dp_ref[:, zc:zc + LANES] = dzs[rows].astype(BF16)
        dp_ref[:, D_MODEL:D_MODEL + 2 * KV_WIDTH] = jnp.zeros((ATTN_BLOCK, 2 * KV_WIDTH), BF16)
        dk_ref[:, pl.ds(ps, ATTN_BLOCK)] += dkw[:, :ATTN_BLOCK]
        dk_ref[:, pl.ds(cs, ATTN_BLOCK)] += dkw[:, ATTN_BLOCK:]
        dv_ref[:, pl.ds(ps, ATTN_BLOCK)] += dvw[:, :ATTN_BLOCK]
        dv_ref[:, pl.ds(cs, ATTN_BLOCK)] += dvw[:, ATTN_BLOCK:]

        if ns:
            @pl.when((b == batch - 1) & (i == nb - 1))
            def _():
                _scatter_finish(*exchange)

    blk = lambda w: pl.BlockSpec((ATTN_BLOCK, w), lambda b, i: (b * nb + i, 0))
    seq_spec = pl.BlockSpec((seq, KV_WIDTH), lambda b, i: (b, 0))
    seq_spec_t = pl.BlockSpec((KV_WIDTH, seq), lambda b, i: (0, b))
    n = batch * seq
    ds_shape = (N_KV_HEADS, 2, PAIRS, LANES)
    out = pl.pallas_call(
        body, name="attn_bwd", grid=(batch, nb),
        in_specs=[blk(D_MODEL), seq_spec, seq_spec, blk(D_MODEL), _full_spec(sink_tab.shape), blk(D_MODEL)]
        + [blk(ANGLE_COLS)] + [_ANY] * ns,
        out_specs=[blk(ATTN_IN), seq_spec_t, seq_spec_t, _full_spec(ds_shape)] + [_ANY] * ns,
        out_shape=[jax.ShapeDtypeStruct((n, ATTN_IN), BF16), jax.ShapeDtypeStruct((KV_WIDTH, n), F32),
                   jax.ShapeDtypeStruct((KV_WIDTH, n), F32), jax.ShapeDtypeStruct(ds_shape, F32)]
        + [jax.ShapeDtypeStruct(a.shape, a.dtype) for a in scatter],
        scratch_shapes=[pltpu.VMEM((N_BIAS_TABLES, GROUP_ROWS, 2 * ATTN_BLOCK), F32)] + (_scatter_sems(ns) if ns else []),
        compiler_params=_cparams(2),
    )(q, k, v, z, sink_tab, dog, angles, *scatter)
    return out[0], out[1], out[2], out[3], out[4:]


def rec_fwd(proj, lb_logits, gnorm_w, batch, seq):
    nblk = seq // REC_BLOCK

    def body(p_ref, lb_ref, gw_ref, og_ref, st_ref, safe_ref, s_scr):
        @pl.when(pl.program_id(1) == 0)
        def _():
            s_scr[...] = jnp.zeros_like(s_scr)

        S = s_scr[...]
        st_ref[0] = S
        qr, fr, v, z = (p_ref[:, part * D_MODEL:(part + 1) * D_MODEL] for part in range(4))
        lf, k = forget_gate(fr, lb_ref[1:2, :] - lb_ref[0:1, :])
        q, b = silu(qr), cumsum_rows(lf)
        safe = jnp.min(_rec_margin(b)) >= -SAFE_RANGE

        gate = gw_ref[...] * silu(z)
        safe_ref[0] = jnp.full((REC_HEADS, LANES), safe.astype(F32))

        def store(o, S_new):
            og_ref[...] = (o * lax.rsqrt(head_sum(o * o) * (1.0 / REC_DIM) + NORM_EPS) * gate).astype(BF16)
            s_scr[...] = S_new

        @pl.when(safe)
        def _():
            store(*_rec_cores_fast(q, k, v, b, S))

        @pl.when(jnp.logical_not(safe))
        def _():
            outs = [_rec_core_slow(*args) for args in zip(*(_heads(t) for t in (q, k, v, b, S)))]
            store(*(jnp.concatenate(parts, axis=1) for parts in zip(*outs)))

    blk = lambda w: pl.BlockSpec((REC_BLOCK, w), lambda b, j: (b * nblk + j, 0))
    st_spec = pl.BlockSpec((1, REC_DIM, D_MODEL), lambda b, j: (b * nblk + j, 0, 0))
    safe_spec = pl.BlockSpec((1, REC_HEADS, LANES), lambda b, j: (b * nblk + j, 0, 0))
    return pl.pallas_call(
        body, name="rec_fwd", grid=(batch, nblk),
        in_specs=[blk(REC_IN), _full_spec((2, D_MODEL)), _full_spec((1, D_MODEL))],
        out_specs=[blk(D_MODEL), st_spec, safe_spec],
        out_shape=[jax.ShapeDtypeStruct((batch * seq, D_MODEL), BF16),
                   jax.ShapeDtypeStruct((batch * nblk, REC_DIM, D_MODEL), F32),
                   jax.ShapeDtypeStruct((batch * nblk, REC_HEADS, LANES), F32)],
        scratch_shapes=[pltpu.VMEM((REC_DIM, D_MODEL), F32)],
        compiler_params=_cparams(2),
    )(proj, lb_logits, jnp.tile(gnorm_w, (1, REC_HEADS)))


def rec_bwd(proj, states, safe, lb_logits, gnorm_w, dog, batch, seq):
    nblk = seq // REC_BLOCK

    def body(p_ref, st_ref, safe_ref, lb_ref, gw_ref, g_ref, dp_ref, dlb_ref, dgw_ref, ds_scr):
        @pl.when((pl.program_id(0) == 0) & (pl.program_id(1) == 0))
        def _():
            dlb_ref[...] = jnp.zeros_like(dlb_ref)
            dgw_ref[...] = jnp.zeros_like(dgw_ref)

        @pl.when(pl.program_id(1) == 0)
        def _():
            ds_scr[...] = jnp.zeros_like(ds_scr)

        def load():
            primals = tuple(p_ref[:, part * D_MODEL:(part + 1) * D_MODEL] for part in range(4)) + (
                st_ref[0], lb_ref[0:1, :], lb_ref[1:2, :], gw_ref[...])
            return primals, (g_ref[...].astype(F32), ds_scr[...])

        def store(dqr, dfr, dv, dz, dS, dl0, dl1, dgw):
            for part, val in enumerate((dqr, dfr, dv, dz)):
                dp_ref[:, part * D_MODEL:(part + 1) * D_MODEL] = val.astype(BF16)
            ds_scr[...] = dS
            dlb_ref[0:1, :] += dl0
            dlb_ref[1:2, :] += dl1
            dgw_ref[...] += functools.reduce(jnp.add, _heads(dgw))

        fast = jnp.max(safe_ref[0]) > 0.5

        @pl.when(fast)
        def _():
            primals, cotangents = load()
            store(*jax.vjp(_rec_block_fast, *primals)[1](cotangents))

        @pl.when(jnp.logical_not(fast))
        def _():
            primals, cotangents = load()
            outs = [jax.vjp(functools.partial(_rec_head, _rec_core_slow), *args)[1](cts)
                    for args, cts in zip(zip(*(_heads(t) for t in primals)), zip(*(_heads(t) for t in cotangents)))]
            store(*(jnp.concatenate(parts, axis=1) for parts in zip(*outs)))

    blk = lambda w: pl.BlockSpec((REC_BLOCK, w), lambda b, j: (b * nblk + nblk - 1 - j, 0))
    st_spec = pl.BlockSpec((1, REC_DIM, D_MODEL), lambda b, j: (b * nblk + nblk - 1 - j, 0, 0))
    safe_spec = pl.BlockSpec((1, REC_HEADS, LANES), lambda b, j: (b * nblk + nblk - 1 - j, 0, 0))
    return pl.pallas_call(
        body, name="rec_bwd", grid=(batch, nblk),
        in_specs=[blk(REC_IN), st_spec, safe_spec, _full_spec((2, D_MODEL)), _full_spec((1, D_MODEL)),
                  blk(D_MODEL)],
        out_specs=[blk(REC_IN), _full_spec((2, D_MODEL)), _full_spec((1, REC_DIM))],
        out_shape=[jax.ShapeDtypeStruct((batch * seq, REC_IN), BF16), jax.ShapeDtypeStruct((2, D_MODEL), F32),
                   jax.ShapeDtypeStruct((1, REC_DIM), F32)],
        scratch_shapes=[pltpu.VMEM((REC_DIM, D_MODEL), F32)],
        compiler_params=_cparams(2),
    )(proj, states, safe, lb_logits, jnp.tile(gnorm_w, (1, REC_HEADS)), dog)


_ANY = pl.BlockSpec(memory_space=pl.ANY)


def _chip_peers():
    x, y, c = lax.axis_index("x"), lax.axis_index("y"), lax.axis_index("c")
    peers = []
    for fx, fy in ((1, 0), (0, 1), (1, 1)):
        px, py = (1 - x if fx else x), (1 - y if fy else y)
        peers.append(((px, py, c), 2 * px + py))
    return 2 * x + y, peers


def _remote(src, dst, send_sem, recv_sem, device):
    return pltpu.make_async_remote_copy(src_ref=src, dst_ref=dst, send_sem=send_sem, recv_sem=recv_sem,
                                        device_id=device, device_id_type=MESH)


N_FLIPS = N_CHIPS - 1


def _scatter_sems(n):
    return [pltpu.SemaphoreType.DMA((n * N_FLIPS,)), pltpu.SemaphoreType.DMA((n * N_FLIPS,)),
            pltpu.SemaphoreType.DMA((n,))]


def _scatter_copies(ins, outs, send_sems, recv_sems, local_sems, starting):
    me, peers = _chip_peers()
    local = [pltpu.make_async_copy(ins[k].at[me], outs[k].at[me], local_sems.at[k]) for k in range(len(ins))]
    sends, arrivals = [], []
    for k in range(len(ins)):
        for j, (device, idx) in enumerate(peers):
            sems = (send_sems.at[k * N_FLIPS + j], recv_sems.at[k * N_FLIPS + j], device)
            sends.append(_remote(ins[k].at[idx], outs[k].at[me], *sems))
            if not starting:
                arrivals.append(_remote(ins[k].at[me], outs[k].at[idx], *sems))
    return local, sends, arrivals


def _scatter_start(*refs):
    local, sends, _ = _scatter_copies(*refs, starting=True)
    for cp in local + sends:
        cp.start()


def _scatter_finish(*refs):
    local, sends, arrivals = _scatter_copies(*refs, starting=False)
    for cp in arrivals:
        cp.wait_recv()
    for cp in sends:
        cp.wait_send()
    for cp in local:
        cp.wait()


def _gather_sems(n):
    return [pltpu.SemaphoreType.DMA((n * N_FLIPS,)) for _ in range(4)] + [pltpu.SemaphoreType.DMA((n,))]


def _gather_copies(ins, outs, send_sems, recv_sems, pass_send_sems, pass_recv_sems, local_sems, starting):
    me, peers = _chip_peers()
    c = lax.axis_index("c")
    sibling = (lax.axis_index("x"), lax.axis_index("y"), 1 - c)
    local = [pltpu.make_async_copy(ins[k], outs[k].at[me], local_sems.at[k]) for k in range(len(ins))]
    sends, arrivals, passes, pass_arrivals = [], [], [], []
    for k in range(len(ins)):
        half = ins[k].shape[0] // 2
        mine, other = pl.ds(c * half, half), pl.ds((1 - c) * half, half)
        for j, (device, idx) in enumerate(peers):
            s = k * N_FLIPS + j
            sends.append(_remote(ins[k].at[mine], outs[k].at[me].at[mine], send_sems.at[s], recv_sems.at[s], device))
            if starting:
                continue
            arrived = outs[k].at[idx].at[mine]
            arrivals.append(_remote(ins[k].at[mine], arrived, send_sems.at[s], recv_sems.at[s], device))
            passes.append(_remote(arrived, arrived, pass_send_sems.at[s], pass_recv_sems.at[s], sibling))
            passed = outs[k].at[idx].at[other]
            pass_arrivals.append(_remote(passed, passed, pass_send_sems.at[s], pass_recv_sems.at[s], sibling))
    return local, sends, arrivals, passes, pass_arrivals


def _gather_start(*refs):
    local, sends, _, _, _ = _gather_copies(*refs, starting=True)
    for cp in local + sends:
        cp.start()


def _gather_finish(*refs):
    local, sends, arrivals, passes, pass_arrivals = _gather_copies(*refs, starting=False)
    for arrival, onward in zip(arrivals, passes):
        arrival.wait_recv()
        onward.start()
    for cp in pass_arrivals:
        cp.wait_recv()
    for cp in sends + passes:
        cp.wait_send()
    for cp in local:
        cp.wait()


def chip_gather(arrays):
    n = len(arrays)

    def body(*refs):
        _gather_start(refs[:n], refs[n:2 * n], *refs[2 * n:])
        _gather_finish(refs[:n], refs[n:2 * n], *refs[2 * n:])

    return pl.pallas_call(
        body, name="chip_gather", in_specs=[_ANY] * n, out_specs=[_ANY] * n,
        out_shape=[jax.ShapeDtypeStruct((N_CHIPS,) + a.shape, a.dtype) for a in arrays],
        scratch_shapes=_gather_sems(n),
    )(*arrays)


def _sibling_sems(n):
    return [pltpu.SemaphoreType.DMA((n,)), pltpu.SemaphoreType.DMA((n,))]


def _sibling_copies(ins, outs, send_sems, recv_sems):
    sibling = (lax.axis_index("x"), lax.axis_index("y"), 1 - lax.axis_index("c"))
    return [_remote(ins[k], outs[k], send_sems.at[k], recv_sems.at[k], sibling) for k in range(len(ins))]


def all_gather_small(vec, parts):
    s, r, c_dim = parts.shape
    rows = min(128, r)

    def body(v_ref, parts_ref, out_ref, mine_ref, other_ref, send_sems, recv_sems, local_sem, sib_send, sib_recv):
        x, y, c = lax.axis_index("x"), lax.axis_index("y"), lax.axis_index("c")
        me = 4 * x + 2 * y + c
        local = pltpu.make_async_copy(v_ref, out_ref.at[me], local_sem)
        local.start()
        sends, recvs = [], []
        for j in range(1, N_DEV):
            px = jnp.where(j & 4, 1 - x, x)
            py = jnp.where(j & 2, 1 - y, y)
            pc = jnp.where(j & 1, 1 - c, c)
            common = dict(send_sem=send_sems.at[j - 1], recv_sem=recv_sems.at[j - 1], device_id=(px, py, pc),
                          device_id_type=MESH)
            sends.append(pltpu.make_async_remote_copy(src_ref=v_ref, dst_ref=out_ref.at[me], **common))
            recvs.append(pltpu.make_async_remote_copy(src_ref=v_ref, dst_ref=out_ref.at[4 * px + 2 * py + pc],
                                                      **common))
        for cp in sends:
            cp.start()

        for i in range(r // rows):
            at = pl.ds(i * rows, rows)
            acc = parts_ref[0, at, :].astype(F32)
            for t in range(1, s):
                acc = acc + parts_ref[t, at, :].astype(F32)
            mine_ref[at, :] = acc
        passed = _sibling_copies([mine_ref], [other_ref], sib_send, sib_recv)[0]
        passed.start()

        for cp in recvs:
            cp.wait_recv()
        for cp in sends:
            cp.wait_send()
        local.wait()
        passed.wait()

    vmem = pl.BlockSpec(memory_space=pltpu.VMEM)
    return pl.pallas_call(
        body, name="all_gather_small", in_specs=[_ANY, vmem], out_specs=[_ANY, vmem, _ANY],
        out_shape=[jax.ShapeDtypeStruct((N_DEV,) + vec.shape, vec.dtype), jax.ShapeDtypeStruct((r, c_dim), F32),
                   jax.ShapeDtypeStruct((r, c_dim), F32)],
        scratch_shapes=[pltpu.SemaphoreType.DMA((N_DEV - 1,)), pltpu.SemaphoreType.DMA((N_DEV - 1,)),
                        pltpu.SemaphoreType.DMA] + _sibling_sems(1),
        compiler_params=_cparams(0),
    )(vec, parts)


def sum_slots(stacked, tm=256):
    s, r, c = stacked.shape
    tm = min(tm, r)

    def body(in_ref, out_ref):
        acc = in_ref[0].astype(F32)
        for t in range(1, s):
            acc = acc + in_ref[t].astype(F32)
        out_ref[...] = acc

    return pl.pallas_call(
        body, name=f"sum_slots_{s}_{r}_{c}", grid=(r // tm,),
        in_specs=[pl.BlockSpec((s, tm, c), lambda i: (0, i, 0))], out_specs=_row_spec(tm, c),
        out_shape=jax.ShapeDtypeStruct((r, c), F32), compiler_params=_cparams(1),
    )(stacked)


def adamw(w, m, v, g_a, g_b=None, tm=512):
    r, c = w.shape
    tm = min(tm, r)
    two = g_b is not None
    slots = g_a.shape[0] if g_a.ndim == 3 else 0

    def body(*refs):
        w_ref, m_ref, v_ref, ga_ref = refs[:4]
        g_ref, d_ref, nm_ref, nv_ref = refs[-4:]
        if slots:
            g = ga_ref[0]
            for t in range(1, slots):
                g = g + ga_ref[t]
        else:
            g = ga_ref[...] + refs[4][...] if two else ga_ref[...]
        nm = ADAM_B1 * m_ref[...] + (1.0 - ADAM_B1) * g
        nv = ADAM_B2 * v_ref[...] + (1.0 - ADAM_B2) * (g * g)
        m_hat = nm / (1.0 - ADAM_B1 ** ADAM_STEP)
        v_hat = nv / (1.0 - ADAM_B2 ** ADAM_STEP)
        g_ref[...] = g
        d_ref[...] = -ADAM_LR * (m_hat / (jnp.sqrt(v_hat) + ADAM_EPS) + ADAM_WD * w_ref[...])
        nm_ref[...] = nm
        nv_ref[...] = nv

    args = [w, m, v, g_a] + ([g_b] if two else [])
    in_specs = [_row_spec(tm, c)] * len(args)
    if slots:
        in_specs[3] = pl.BlockSpec((slots, tm, c), lambda i: (0, i, 0))
    return pl.pallas_call(
        body, name=f"adamw_{r}_{c}", grid=(r // tm,),
        in_specs=in_specs, out_specs=[_row_spec(tm, c)] * 4,
        out_shape=[jax.ShapeDtypeStruct((r, c), F32)] * 4, compiler_params=_cparams(1),
    )(*args)


def adamw_shards(items, steps=8):
    n = len(items)

    def body(*refs):
        for k in range(n):
            w_ref, m_ref, v_ref, ga_ref, gb_ref = refs[5 * k:5 * k + 5]
            g_ref, d_ref, nm_ref, nv_ref = refs[5 * n + 4 * k:5 * n + 4 * k + 4]
            g = ga_ref[...] + gb_ref[...]
            nm = ADAM_B1 * m_ref[...] + (1.0 - ADAM_B1) * g
            nv = ADAM_B2 * v_ref[...] + (1.0 - ADAM_B2) * (g * g)
            m_hat = nm / (1.0 - ADAM_B1 ** ADAM_STEP)
            v_hat = nv / (1.0 - ADAM_B2 ** ADAM_STEP)
            g_ref[...] = g
            d_ref[...] = -ADAM_LR * (m_hat / (jnp.sqrt(v_hat) + ADAM_EPS) + ADAM_WD * w_ref[...])
            nm_ref[...] = nm
            nv_ref[...] = nv

    in_specs, out_specs, out_shape = [], [], []
    for item in items:
        r, c = item[0].shape
        in_specs += [pl.BlockSpec((r // steps, c), lambda i: (i, 0), pipeline_mode=pl.Buffered(3))] * 5
        out_specs += [_row_spec(r // steps, c)] * 4
        out_shape += [jax.ShapeDtypeStruct((r, c), F32)] * 4

    def outer(*hbm_refs):
        pltpu.emit_pipeline(body, grid=(steps,), in_specs=in_specs, out_specs=out_specs)(*hbm_refs)

    out = pl.pallas_call(
        outer, name="adamw_shards", in_specs=[_ANY] * (5 * n), out_specs=[_ANY] * (4 * n), out_shape=out_shape,
        compiler_params=_cparams(0),
    )(*[a for item in items for a in item])
    return [out[4 * k:4 * k + 4] for k in range(n)]


_SMALL = (("pre_norm_w", (2, D_MODEL)), ("post_norm_w", (2, D_MODEL)), ("attn_b_in", (1, ATTN_IN)),
          ("attn_sinks", (1, N_HEADS)), ("attn_b_out", (1, D_MODEL)), ("rec_lb_logits", (2, D_MODEL)),
          ("rec_gnorm_w", (1, REC_DIM)))
_SMALL_ROWS = 16


def _pack_small(parts, last_row=None):
    rows = []
    for (name, shape) in _SMALL:
        flat = parts[name].reshape(-1)
        pad = -flat.shape[0] % D_MODEL
        rows.append(jnp.pad(flat, (0, pad)).reshape(-1, D_MODEL))
    used = sum(r.shape[0] for r in rows)
    rows.append(jnp.zeros((_SMALL_ROWS - 1 - used, D_MODEL), F32))
    rows.append(jnp.zeros((1, D_MODEL), F32) if last_row is None else last_row)
    return jnp.concatenate(rows, axis=0)


def _unpack_small(packed):
    out, row = {}, 0
    for (name, shape) in _SMALL:
        size = shape[0] * shape[1]
        nrows = -(-size // D_MODEL)
        out[name] = packed[row:row + nrows].reshape(-1)[:size].reshape(shape)
        row += nrows
    return out


_CARRIED = ("rec_w_in", "rec_w_out", "attn_w_out")


_LATE = ("attn_w_out", "rec_w_in", "rec_w_out")


def local_step(x, positions, pre_norm_w, post_norm_w, attn_w_in, attn_b_in, attn_sinks, attn_w_out, attn_b_out,
               rec_w_in, rec_lb_logits, rec_gnorm_w, rec_w_out, loss_target, distributed=False):
    batch, seq, _ = x.shape
    n = batch * seq
    x0 = x.reshape(n, D_MODEL)
    angles = _rope_angles(positions)
    pre0, pre1 = pre_norm_w[0:1], pre_norm_w[1:2]
    post0, post1 = post_norm_w[0:1], post_norm_w[1:2]

    late = (attn_w_out, rec_w_in, rec_w_out) if distributed else ()
    (h0, q, k, v, z), late = attn_in_proj(x0, pre0, attn_w_in, attn_b_in, angles, to_bf16=late)
    sink_tab = _sink_table(attn_sinks)
    og0, gathered = attn_fwd(q, k, v, z, sink_tab, batch, seq, gather=late)
    if distributed:
        attn_w_out, rec_w_in, rec_w_out = (g if name == "rec_w_in" else _whole_from_shards(name, g)
                                           for name, g in zip(_LATE, gathered))
    y0, x1 = out_proj(og0, attn_w_out, attn_b_out, x0, post0)

    h1, proj1 = rec_in_proj(x1, pre1, rec_w_in)
    og1, states, safe = rec_fwd(proj1, rec_lb_logits, rec_gnorm_w, batch, seq)
    dx2, loss_vec, dog1, d_rec_w_out, d_post1, d_rec_w_out_bf16 = out_proj_loss_bwd(
        og1, rec_w_out, x1, post1, loss_target.reshape(n, D_MODEL))
    dproj1, d_lb, d_gnorm = rec_bwd(proj1, states, safe, rec_lb_logits, rec_gnorm_w, dog1, batch, seq)
    dx1, d_pre1, _, _ = in_proj_bwd_x(dproj1, rec_w_in, x1, pre1, dx2)
    d_rec_w_in, _ = in_proj_bwd_w(h1, dproj1, as_shards=distributed)

    dog0, d_attn_w_out, d_attn_b_out, d_post0, d_attn_w_out_bf16 = out_proj_bwd(dx1, y0, og0, attn_w_out, post0)
    ready = dict(rec_w_in=d_rec_w_in, rec_w_out=_shards_from_whole("rec_w_out", d_rec_w_out_bf16),
                 attn_w_out=_shards_from_whole("attn_w_out", d_attn_w_out_bf16))
    outgoing = [ready[name] for name in _CARRIED] if distributed else []
    dproj0, dk, dv, d_sink_tab, arrived = attn_bwd(q, k, v, z, sink_tab, dog0, angles, batch, seq, scatter=outgoing)
    d_sinks = jnp.transpose(jnp.sum(d_sink_tab, axis=-1), (0, 2, 1)).reshape(1, N_HEADS)
    d_attn_w_in, d_attn_b_in, dproj0, *summed = in_proj_bwd_w(h0, dproj0, kv=(dk, dv, angles),
                                                              slot_sums=list(arrived))
    last = [_shards_from_whole("attn_w_in", d_attn_w_in).astype(BF16)] if distributed else []
    dx0, d_pre0, arrived_last, theirs = in_proj_bwd_x(dproj0, attn_w_in, x0, pre0, dx1, scatter=last,
                                                      to_sibling=summed)

    grads = dict(
        pre_norm_w=jnp.concatenate([d_pre0, d_pre1], axis=0), post_norm_w=jnp.concatenate([d_post0, d_post1], axis=0),
        attn_w_in=d_attn_w_in, attn_b_in=d_attn_b_in, attn_sinks=d_sinks, attn_w_out=d_attn_w_out,
        attn_b_out=d_attn_b_out, rec_w_in=d_rec_w_in, rec_lb_logits=d_lb, rec_gnorm_w=d_gnorm,
        rec_w_out=d_rec_w_out)
    exchanged = dict(zip(_CARRIED, zip(summed, theirs)))
    exchanged.update(zip(("attn_w_in",), arrived_last))
    return loss_vec, dx0.reshape(batch, seq, D_MODEL), grads, exchanged


_BIG = ("attn_w_in", "attn_w_out", "rec_w_in", "rec_w_out")
_COLUMN_SHARDED = ("attn_w_in", "rec_w_in")
_ORDER = ("pre_norm_w", "post_norm_w", "attn_w_in", "attn_b_in", "attn_sinks", "attn_w_out", "attn_b_out",
          "rec_w_in", "rec_lb_logits", "rec_gnorm_w", "rec_w_out")


def _whole_from_shards(name, stacked):
    if name in _COLUMN_SHARDED:
        return jnp.transpose(stacked, (1, 0, 2)).reshape(stacked.shape[1], -1)
    return stacked.reshape(-1, stacked.shape[2])


def _shards_from_whole(name, whole):
    if name in _COLUMN_SHARDED:
        return jnp.transpose(whole.reshape(whole.shape[0], N_CHIPS, -1), (1, 0, 2))
    return whole.reshape(N_CHIPS, -1, whole.shape[1])


def kernel(x, positions, pre_norm_w, post_norm_w, attn_w_in, attn_b_in, attn_sinks, attn_w_out, attn_b_out, rec_w_in, rec_lb_logits, rec_gnorm_w, rec_w_out, loss_target, m_pre_norm_w, m_post_norm_w, m_attn_w_in, m_attn_b_in, m_attn_sinks, m_attn_w_out, m_attn_b_out, m_rec_w_in, m_rec_lb_logits, m_rec_gnorm_w, m_rec_w_out, v_pre_norm_w, v_post_norm_w, v_attn_w_in, v_attn_b_in, v_attn_sinks, v_attn_w_out, v_attn_b_out, v_rec_w_in, v_rec_lb_logits, v_rec_gnorm_w, v_rec_w_out):
    w = dict(pre_norm_w=pre_norm_w, post_norm_w=post_norm_w, attn_w_in=attn_w_in, attn_b_in=attn_b_in,
             attn_sinks=attn_sinks, attn_w_out=attn_w_out, attn_b_out=attn_b_out, rec_w_in=rec_w_in,
             rec_lb_logits=rec_lb_logits, rec_gnorm_w=rec_gnorm_w, rec_w_out=rec_w_out)
    m = dict(pre_norm_w=m_pre_norm_w, post_norm_w=m_post_norm_w, attn_w_in=m_attn_w_in, attn_b_in=m_attn_b_in,
             attn_sinks=m_attn_sinks, attn_w_out=m_attn_w_out, attn_b_out=m_attn_b_out, rec_w_in=m_rec_w_in,
             rec_lb_logits=m_rec_lb_logits, rec_gnorm_w=m_rec_gnorm_w, rec_w_out=m_rec_w_out)
    v = dict(pre_norm_w=v_pre_norm_w, post_norm_w=v_post_norm_w, attn_w_in=v_attn_w_in, attn_b_in=v_attn_b_in,
             attn_sinks=v_attn_sinks, attn_w_out=v_attn_w_out, attn_b_out=v_attn_b_out, rec_w_in=v_rec_w_in,
             rec_lb_logits=v_rec_lb_logits, rec_gnorm_w=v_rec_gnorm_w, rec_w_out=v_rec_w_out)

    shards = {name: w[name][0] for name in _BIG}
    attn_w_in_whole = _whole_from_shards("attn_w_in", chip_gather([shards["attn_w_in"].astype(BF16)])[0])

    loss_vec, grad_x, grads, exchanged = local_step(
        x, positions, pre_norm_w, post_norm_w, attn_w_in_whole, attn_b_in, attn_sinks, shards["attn_w_out"],
        attn_b_out, shards["rec_w_in"], rec_lb_logits, rec_gnorm_w, shards["rec_w_out"], loss_target,
        distributed=True)

    small_parts, mine, other = all_gather_small(_pack_small(grads, last_row=loss_vec), exchanged["attn_w_in"])
    exchanged["attn_w_in"] = (mine, other)
    out_g, out_d, out_m, out_v = {}, {}, {}, {}
    updated = adamw_shards([(shards[name], m[name][0], v[name][0], *exchanged[name]) for name in _BIG])
    for name, (g, d, nm, nv) in zip(_BIG, updated):
        out_g[name], out_d[name], out_m[name], out_v[name] = g[None], d[None], nm[None], nv[None]

    packed = adamw(_pack_small(w), _pack_small(m), _pack_small(v), small_parts)
    loss = jnp.sum(packed[0][_SMALL_ROWS - 1]) * (0.5 / D_MODEL)
    for dst, val in zip((out_g, out_d, out_m, out_v), packed):
        dst.update(_unpack_small(val))

    return (loss, grad_x, *[out_g[n] for n in _ORDER], *[out_d[n] for n in _ORDER],
            *[out_m[n] for n in _ORDER], *[out_v[n] for n in _ORDER])
```
